```python
import math
import jax
import jax.numpy as jnp
from jax import lax
import numpy as np

D_MODEL = 2048
BATCH = 8
SEQ = 2048
DEPTH = 1

MIX_WIDTH = D_MODEL
ATTN_WIDTH = MIX_WIDTH // 2
ATTN_HEAD_DIM = 128
N_ATTN_HEADS = ATTN_WIDTH // ATTN_HEAD_DIM
RET_WIDTH = MIX_WIDTH - ATTN_WIDTH
RET_HEAD_DIM = 256
N_RET_HEADS = RET_WIDTH // RET_HEAD_DIM
RET_CHUNK = 128
DILATED_PATTERNS = ((128, 1), (512, 4), (2048, 16))
FFN_HIDDEN = ((8 * D_MODEL // 3 + 255) // 256) * 256
IN_PROJ_WIDTH = 3 * ATTN_WIDTH + 4 * RET_WIDTH
NORM_EPS = 1e-6

kernel_name = "hybrid_dilated_attn_retention_block"


def _rmsnorm(x, w):
    x32 = x.astype(jnp.float32)
    y = x32 * lax.rsqrt(jnp.mean(x32 * x32, axis=-1, keepdims=True) + NORM_EPS)
    return (y * w.astype(jnp.float32)).astype(x.dtype)


def _alibi_slopes(n_heads):
    return jnp.exp2(-8.0 * jnp.arange(1, n_heads + 1, dtype=jnp.float32) / n_heads)


def _dilated_branch(q, k, v, slopes, window, dilation):
    B, H, S, hd = q.shape
    blk = window // dilation
    span = dilation * blk
    sp = -(-S // span) * span
    L = sp // dilation
    nb = L // blk

    def to_sub(t):
        t = jnp.pad(t, ((0, 0), (0, 0), (0, sp - S), (0, 0)))
        t = t.reshape(B, H, L, dilation, hd).transpose(0, 1, 3, 2, 4)
        return t.reshape(B, H, dilation, nb, blk, hd)

    def two_block(t):
        prev = jnp.concatenate([jnp.zeros_like(t[:, :, :, :1]), t[:, :, :, :-1]], axis=3)
        return jnp.concatenate([prev, t], axis=4)

    qb = to_sub(q)
    kk = two_block(to_sub(k))
    vv = two_block(to_sub(v))
    s = jnp.einsum('bhrnqd,bhrnkd->bhrnqk', qb, kk).astype(jnp.float32) * (1.0 / math.sqrt(hd))
    qi = jnp.arange(blk)[:, None]
    kj = jnp.arange(2 * blk)[None, :]
    diff = qi - kj + blk
    key_idx = jnp.arange(nb)[:, None, None] * blk + kj[None] - blk
    valid = (diff >= 0) & (diff <= blk) & (key_idx >= 0)
    bias = -slopes[:, None, None] * (diff * dilation).astype(jnp.float32)
    s = s + bias[None, :, None, None]
    s = jnp.where(valid, s, -jnp.inf)
    lse = jax.nn.logsumexp(s, axis=-1)
    p = jnp.exp(s - lse[..., None])
    o = jnp.einsum('bhrnqk,bhrnkd->bhrnqd', p.astype(v.dtype), vv)

    def from_sub(t):
        tail = t.shape[5:]
        t = t.reshape((B, H, dilation, L) + tail)
        t = jnp.moveaxis(t, 2, 3)
        return t.reshape((B, H, sp) + tail)[:, :, :S]

    return from_sub(o), from_sub(lse)


def _dilated_attention(q, k, v):
    slopes = _alibi_slopes(q.shape[1])
    outs, lses = [], []
    for window, dilation in DILATED_PATTERNS:
        o, l = _dilated_branch(q, k, v, slopes, window, dilation)
        outs.append(o.astype(jnp.float32))
        lses.append(l)
    wts = jax.nn.softmax(jnp.stack(lses), axis=0)
    return jnp.sum(wts[..., None] * jnp.stack(outs), axis=0)


def _retention_chunkwise(q, k, v):
    B, H, S, dh = q.shape
    C = RET_CHUNK
    nc = S // C
    log_gamma = jnp.log(1.0 - jnp.exp2(-5.0 - jnp.arange(H, dtype=jnp.float32)))
    k = k * (1.0 / math.sqrt(dh))
    qc = q.reshape(B, H, nc, C, dh)
    kc = k.reshape(B, H, nc, C, dh)
    vc = v.reshape(B, H, nc, C, dh)
    idx = jnp.arange(C, dtype=jnp.float32)
    dif = idx[:, None] - idx[None, :]
    decay = jnp.where(dif >= 0, jnp.exp(log_gamma[:, None, None] * jnp.maximum(dif, 0.0)), 0.0)
    scores = jnp.einsum('bhnid,bhnjd->bhnij', qc, kc) * decay[None, :, None]
    inner = jnp.einsum('bhnij,bhnjd->bhnid', scores, vc)
    zeta = jnp.exp(log_gamma[:, None] * (C - 1.0 - idx))
    kv = jnp.einsum('bhnjd,bhnje->bhnde', kc * zeta[None, :, None, :, None], vc)
    gamma_chunk = jnp.exp(log_gamma * C)[None, :, None, None]

    def step(state, kv_n):
        return state * gamma_chunk + kv_n, state

    _, r_prev = lax.scan(step, jnp.zeros((B, H, dh, dh), jnp.float32), jnp.moveaxis(kv, 2, 0))
    r_prev = jnp.moveaxis(r_prev, 0, 2)
    xi = jnp.exp(log_gamma[:, None] * (idx + 1.0))
    cross = jnp.einsum('bhnid,bhnde->bhnie', qc, r_prev) * xi[None, :, None, :, None]
    return (inner + cross).reshape(B, H, S, dh)


def _heads(t, n_heads, head_dim):
    B, S, _ = t.shape
    return t.reshape(B, S, n_heads, head_dim).transpose(0, 2, 1, 3)


def _merge(t):
    B, H, S, hd = t.shape
    return t.transpose(0, 2, 1, 3).reshape(B, S, H * hd)


def _hybrid_mixer(h, w_in, w_out):
    proj = jnp.einsum('bsd,de->bse', h, w_in)
    cuts = np.cumsum([ATTN_WIDTH] * 3 + [RET_WIDTH] * 3)
    qa, ka, va, qr, kr, vr, gr = jnp.split(proj, cuts, axis=-1)
    attn = _dilated_attention(_heads(qa, N_ATTN_HEADS, ATTN_HEAD_DIM),
                              _heads(ka, N_ATTN_HEADS, ATTN_HEAD_DIM),
                              _heads(va, N_ATTN_HEADS, ATTN_HEAD_DIM))
    f32 = jnp.float32
    ret = _retention_chunkwise(_heads(qr, N_RET_HEADS, RET_HEAD_DIM).astype(f32),
                               _heads(kr, N_RET_HEADS, RET_HEAD_DIM).astype(f32),
                               _heads(vr, N_RET_HEADS, RET_HEAD_DIM).astype(f32))
    ret = ret * lax.rsqrt(jnp.mean(ret * ret, axis=-1, keepdims=True) + NORM_EPS)
    ret = jax.nn.silu(gr.astype(f32)) * _merge(ret)
    mixed = jnp.concatenate([_merge(attn), ret], axis=-1).astype(h.dtype)
    return jnp.einsum('bse,ed->bsd', mixed, w_out)


def _swiglu(h, w_gate, w_up, w_down):
    g = jnp.einsum('bsd,df->bsf', h, w_gate)
    u = jnp.einsum('bsd,df->bsf', h, w_up)
    return jnp.einsum('bsf,fd->bsd', jax.nn.silu(g) * u, w_down)


def _fwd_setup_inputs(seed: int = 0) -> dict:
    key = jax.random.key(seed)
    ks = jax.random.split(key, 10)
    f32 = jnp.float32

    def normal(k, shape, fan_in):
        return jax.random.normal(k, shape, f32) * (fan_in ** -0.5)

    return {
        "x": jax.random.normal(ks[0], (BATCH, SEQ, D_MODEL), f32),
        "norm_mix_w": 1.0 + 0.02 * jax.random.normal(ks[1], (DEPTH, D_MODEL), f32),
        "w_in": normal(ks[2], (DEPTH, D_MODEL, IN_PROJ_WIDTH), D_MODEL),
        "w_out": normal(ks[3], (DEPTH, MIX_WIDTH, D_MODEL), MIX_WIDTH),
        "norm_ffn_w": 1.0 + 0.02 * jax.random.normal(ks[4], (DEPTH, D_MODEL), f32),
        "w_gate": normal(ks[5], (DEPTH, D_MODEL, FFN_HIDDEN), D_MODEL),
        "w_up": normal(ks[6], (DEPTH, D_MODEL, FFN_HIDDEN), D_MODEL),
        "w_down": normal(ks[7], (DEPTH, FFN_HIDDEN, D_MODEL), FFN_HIDDEN),
        "norm_final_w": 1.0 + 0.02 * jax.random.normal(ks[8], (D_MODEL,), f32),
    }


def _fwd_reference(x, norm_mix_w, w_in, w_out, norm_ffn_w, w_gate, w_up, w_down, norm_final_w):
    for layer in range(DEPTH):
        h = _rmsnorm(x, norm_mix_w[layer])
        x = x + _hybrid_mixer(h, w_in[layer], w_out[layer])
        h = _rmsnorm(x, norm_ffn_w[layer])
        x = x + _swiglu(h, w_gate[layer], w_up[layer], w_down[layer])
    return _rmsnorm(x, norm_final_w)


import jax as _jax
import jax.numpy as _jnp

TWIN_FORMAT = 'train_step'
FWD_PARAMS = ['x', 'norm_mix_w', 'w_in', 'w_out', 'norm_ffn_w', 'w_gate', 'w_up', 'w_down', 'norm_final_w']
TWIN_WEIGHTS = ['norm_mix_w', 'w_in', 'w_out', 'norm_ffn_w', 'w_gate', 'w_up', 'w_down', 'norm_final_w']
TWIN_DIFF_INPUT = 'x'
TWIN_INPUTS = ['x', 'norm_mix_w', 'w_in', 'w_out', 'norm_ffn_w', 'w_gate', 'w_up', 'w_down', 'norm_final_w', 'loss_target', 'm_norm_mix_w', 'm_w_in', 'm_w_out', 'm_norm_ffn_w', 'm_w_gate', 'm_w_up', 'm_w_down', 'm_norm_final_w', 'v_norm_mix_w', 'v_w_in', 'v_w_out', 'v_norm_ffn_w', 'v_w_gate', 'v_w_up', 'v_w_down', 'v_norm_final_w']
TWIN_OUTPUTS = ['loss', 'grad_x', 'grad_norm_mix_w', 'grad_w_in', 'grad_w_out', 'grad_norm_ffn_w', 'grad_w_gate', 'grad_w_up', 'grad_w_down', 'grad_norm_final_w', 'delta_norm_mix_w', 'delta_w_in', 'delta_w_out', 'delta_norm_ffn_w', 'delta_w_gate', 'delta_w_up', 'delta_w_down', 'delta_norm_final_w', 'new_m_norm_mix_w', 'new_m_w_in', 'new_m_w_out', 'new_m_norm_ffn_w', 'new_m_w_gate', 'new_m_w_up', 'new_m_w_down', 'new_m_norm_final_w', 'new_v_norm_mix_w', 'new_v_w_in', 'new_v_w_out', 'new_v_norm_ffn_w', 'new_v_w_gate', 'new_v_w_up', 'new_v_w_down', 'new_v_norm_final_w']
TWIN_LEAF_KINDS = {'loss': 'loss', 'grad_x': 'grad_x', 'grad_norm_mix_w': 'grad_w', 'grad_w_in': 'grad_w', 'grad_w_out': 'grad_w', 'grad_norm_ffn_w': 'grad_w', 'grad_w_gate': 'grad_w', 'grad_w_up': 'grad_w', 'grad_w_down': 'grad_w', 'grad_norm_final_w': 'grad_w', 'delta_norm_mix_w': 'delta_w', 'delta_w_in': 'delta_w', 'delta_w_out': 'delta_w', 'delta_norm_ffn_w': 'delta_w', 'delta_w_gate': 'delta_w', 'delta_w_up': 'delta_w', 'delta_w_down': 'delta_w', 'delta_norm_final_w': 'delta_w', 'new_m_norm_mix_w': 'new_m', 'new_m_w_in': 'new_m', 'new_m_w_out': 'new_m', 'new_m_norm_ffn_w': 'new_m', 'new_m_w_gate': 'new_m', 'new_m_w_up': 'new_m', 'new_m_w_down': 'new_m', 'new_m_norm_final_w': 'new_m', 'new_v_norm_mix_w': 'new_v', 'new_v_w_in': 'new_v', 'new_v_w_out': 'new_v', 'new_v_norm_ffn_w': 'new_v', 'new_v_w_gate': 'new_v', 'new_v_w_up': 'new_v', 'new_v_w_down': 'new_v', 'new_v_norm_final_w': 'new_v'}


def _forward(args):
    return _fwd_reference(*[args[k] for k in FWD_PARAMS])


def _output_shape():
    out = _jax.eval_shape(lambda: _forward(_fwd_setup_inputs(0)))
    return out.shape, out.dtype

N_MICROBATCH = 1
ADAM_LR = 0.001
ADAM_B1 = 0.9
ADAM_B2 = 0.999
ADAM_EPS = 1e-08
ADAM_WD = 0.01
ADAM_STEP = 10
PER_EXAMPLE_BATCH_AXIS = {'x': 0, 'loss_target': 0}
SHARED_INPUTS = []
_WEIGHT_DTYPES = {'norm_mix_w': _jnp.float32, 'w_in': _jnp.float32, 'w_out': _jnp.float32, 'norm_ffn_w': _jnp.float32, 'w_gate': _jnp.float32, 'w_up': _jnp.float32, 'w_down': _jnp.float32, 'norm_final_w': _jnp.float32}
MOMENT_SCALE = {'norm_mix_w': 5.923557e-02, 'w_in': 3.115505e-02, 'w_out': 3.111981e-02, 'norm_ffn_w': 4.365106e-02, 'w_gate': 1.839305e-02, 'w_up': 1.785073e-02, 'w_down': 2.955715e-02, 'norm_final_w': 8.002085e+00}


def _to_microbatches(a, axis):
    t = _jnp.moveaxis(a, axis, 0)
    t = t.reshape((N_MICROBATCH, t.shape[0] // N_MICROBATCH) + t.shape[1:])
    return _jnp.moveaxis(t, 1, axis + 1)


def setup_inputs(seed: int = 0) -> dict:
    inp = _fwd_setup_inputs(seed)
    key = _jax.random.fold_in(_jax.random.key(seed), 7919)
    shape, _ = _output_shape()
    out = dict(inp)
    out["loss_target"] = _jax.random.normal(_jax.random.fold_in(key, 0), shape, _jnp.float32)
    for i, name in enumerate(TWIN_WEIGHTS):
        w = inp[name].astype(_jnp.float32)
        if MOMENT_SCALE is None:
            s = _jnp.sqrt(_jnp.mean(_jnp.square(w)) + 1e-30)
        else:
            s = MOMENT_SCALE[name]
        km, kv = _jax.random.split(_jax.random.fold_in(key, i + 1))
        out[name] = w
        out["m_" + name] = s * _jax.random.normal(km, w.shape, _jnp.float32)
        out["v_" + name] = (s * s) * _jax.random.uniform(kv, w.shape, _jnp.float32, 0.5, 1.5)
    if N_MICROBATCH > 1:
        for name, axis in PER_EXAMPLE_BATCH_AXIS.items():
            out[name] = _to_microbatches(out[name], axis)
    return {'x': out['x'], 'norm_mix_w': out['norm_mix_w'], 'w_in': out['w_in'], 'w_out': out['w_out'], 'norm_ffn_w': out['norm_ffn_w'], 'w_gate': out['w_gate'], 'w_up': out['w_up'], 'w_down': out['w_down'], 'norm_final_w': out['norm_final_w'], 'loss_target': out['loss_target'], 'm_norm_mix_w': out['m_norm_mix_w'], 'm_w_in': out['m_w_in'], 'm_w_out': out['m_w_out'], 'm_norm_ffn_w': out['m_norm_ffn_w'], 'm_w_gate': out['m_w_gate'], 'm_w_up': out['m_w_up'], 'm_w_down': out['m_w_down'], 'm_norm_final_w': out['m_norm_final_w'], 'v_norm_mix_w': out['v_norm_mix_w'], 'v_w_in': out['v_w_in'], 'v_w_out': out['v_w_out'], 'v_norm_ffn_w': out['v_norm_ffn_w'], 'v_w_gate': out['v_w_gate'], 'v_w_up': out['v_w_up'], 'v_w_down': out['v_w_down'], 'v_norm_final_w': out['v_norm_final_w']}


def _loss(weights, diff, rest, loss_target):
    with _jax.named_scope("forward"):
        args = {**rest, TWIN_DIFF_INPUT: diff, **{k: w.astype(_WEIGHT_DTYPES[k]) for k, w in weights.items()}}
        y = _forward(args)
    with _jax.named_scope("loss_head"):
        err = _jnp.square(y.astype(_jnp.float32) - loss_target)
        return 0.5 * _jnp.sum(_jnp.mean(err, axis=-1)) if err.ndim else 0.5 * err


def _adamw(w, g, m, v):
    m = ADAM_B1 * m + (1.0 - ADAM_B1) * g
    v = ADAM_B2 * v + (1.0 - ADAM_B2) * _jnp.square(g)
    m_hat = m / (1.0 - ADAM_B1 ** ADAM_STEP)
    v_hat = v / (1.0 - ADAM_B2 ** ADAM_STEP)
    delta = -ADAM_LR * (m_hat / (_jnp.sqrt(v_hat) + ADAM_EPS) + ADAM_WD * w)
    return delta, m, v


def reference(x, norm_mix_w, w_in, w_out, norm_ffn_w, w_gate, w_up, w_down, norm_final_w, loss_target, m_norm_mix_w, m_w_in, m_w_out, m_norm_ffn_w, m_w_gate, m_w_up, m_w_down, m_norm_final_w, v_norm_mix_w, v_w_in, v_w_out, v_norm_ffn_w, v_w_gate, v_w_up, v_w_down, v_norm_final_w):
    given = dict(x=x, norm_mix_w=norm_mix_w, w_in=w_in, w_out=w_out, norm_ffn_w=norm_ffn_w, w_gate=w_gate, w_up=w_up, w_down=w_down, norm_final_w=norm_final_w, loss_target=loss_target, m_norm_mix_w=m_norm_mix_w, m_w_in=m_w_in, m_w_out=m_w_out, m_norm_ffn_w=m_norm_ffn_w, m_w_gate=m_w_gate, m_w_up=m_w_up, m_w_down=m_w_down, m_norm_final_w=m_norm_final_w, v_norm_mix_w=v_norm_mix_w, v_w_in=v_w_in, v_w_out=v_w_out, v_norm_ffn_w=v_norm_ffn_w, v_w_gate=v_w_gate, v_w_up=v_w_up, v_w_down=v_w_down, v_norm_final_w=v_norm_final_w)
    weights = {n: given[n] for n in TWIN_WEIGHTS}
    shared = {n: given[n] for n in SHARED_INPUTS}
    per_example = {n: given[n] for n in ['x']}
    grad_fn = _jax.value_and_grad(_loss, argnums=(0, 1))

    def one_microbatch(ex, loss_target):
        ex = dict(ex)
        diff = ex.pop(TWIN_DIFF_INPUT)
        return grad_fn(weights, diff, {**shared, **ex}, loss_target)

    if N_MICROBATCH == 1:
        loss, (grad_w, grad_x) = one_microbatch(per_example, given["loss_target"])
    else:
        def body(carry, xs):
            loss_sum, grad_sum = carry
            l_k, (gw_k, gx_k) = one_microbatch(xs[0], xs[1])
            with _jax.named_scope("update"):
                return (loss_sum + l_k, _jax.tree.map(_jnp.add, grad_sum, gw_k)), gx_k

        init = (_jnp.zeros((), _jnp.float32), _jax.tree.map(_jnp.zeros_like, weights))
        (loss, grad_w), grad_x = _jax.lax.scan(body, init, (per_example, given["loss_target"]))
    with _jax.named_scope("update"):
        delta_w, new_m, new_v = {}, {}, {}
        for n in TWIN_WEIGHTS:
            delta_w[n], new_m[n], new_v[n] = _adamw(weights[n], grad_w[n], given["m_" + n], given["v_" + n])
    return (loss, grad_x, *[grad_w[n] for n in TWIN_WEIGHTS], *[delta_w[n] for n in TWIN_WEIGHTS],
            *[new_m[n] for n in TWIN_WEIGHTS], *[new_v[n] for n in TWIN_WEIGHTS])
```

```python
import functools
import math

import numpy as np
import jax
import jax.numpy as jnp
from jax import lax
from jax.experimental import pallas as pl
from jax.experimental.pallas import tpu as pltpu

F32 = jnp.float32
BF16 = jnp.bfloat16

S = 2048
D = 2048
NDEV = 8
N_IN = 7168 // NDEV
N_FF = 5632 // NDEV
N_OUT = 2048 // NDEV
AH, AHD = 8, 128
RH, RHD = 4, 256
CH = 128
NB = S // CH
EPS = 1e-6
PATTERNS = ((1, 16), (4, 4), (16, 1))
NEG = -1e30
VMEM_LIMIT = 56 * 1024 * 1024

ADAM_LR, ADAM_B1, ADAM_B2, ADAM_EPS, ADAM_WD, ADAM_STEP = 0.001, 0.9, 0.999, 1e-08, 0.01, 10
MESH = pl.DeviceIdType.MESH


def _cp(sem=None):
    return pltpu.CompilerParams(dimension_semantics=sem, vmem_limit_bytes=VMEM_LIMIT)


def _dot(a, b):
    return jnp.dot(a, b, preferred_element_type=F32)


def _dot_nt(a, b):
    return lax.dot_general(a, b, (((1,), (1,)), ((), ())), preferred_element_type=F32)


def _dot_tn(a, b):
    return lax.dot_general(a, b, (((0,), (0,)), ((), ())), preferred_element_type=F32)


def _sigmoid(x):
    return 1.0 / (1.0 + jnp.exp(-x))


def _cast_bf16(w, name):
    r, c = w.shape
    tm = r if r <= 1024 else 512

    def body(w_ref, o_ref):
        o_ref[...] = w_ref[...].astype(BF16)

    return pl.pallas_call(
        body, name=name, grid=(r // tm,),
        in_specs=[pl.BlockSpec((tm, c), lambda i: (i, 0))],
        out_specs=pl.BlockSpec((tm, c), lambda i: (i, 0)),
        out_shape=jax.ShapeDtypeStruct((r, c), BF16),
        compiler_params=_cp(("parallel",)),
    )(w)


def _rms_fwd(x, nw):
    tm = 256

    def body(x_ref, w_ref, h_ref, r_ref):
        xs = x_ref[...]
        r = lax.rsqrt(jnp.mean(xs * xs, axis=-1, keepdims=True) + EPS)
        h_ref[...] = ((xs * r) * w_ref[...]).astype(BF16)
        r_ref[...] = r

    return pl.pallas_call(
        body, name="rms_fwd", grid=(S // tm,),
        in_specs=[pl.BlockSpec((tm, D), lambda i: (i, 0)), pl.BlockSpec((1, D), lambda i: (0, 0))],
        out_specs=[pl.BlockSpec((tm, D), lambda i: (i, 0)), pl.BlockSpec((tm, 1), lambda i: (i, 0))],
        out_shape=[jax.ShapeDtypeStruct((S, D), BF16), jax.ShapeDtypeStruct((S, 1), F32)],
        compiler_params=_cp(("parallel",)),
    )(x, nw)


def _rms_bwd_tile(dh, xs, r, nw):
    dnw = jnp.sum(dh * (xs * r), axis=0, keepdims=True)
    gy = dh * nw
    dx = r * gy - xs * ((r * r * r) * jnp.mean(gy * xs, axis=-1, keepdims=True))
    return dx, dnw


def _proj(h1, win):
    tm = 512

    def body(a_ref, w_ref, o_ref):
        o_ref[...] = _dot(a_ref[...], w_ref[...])

    return pl.pallas_call(
        body, name="proj", grid=(NDEV, S // tm),
        in_specs=[pl.BlockSpec((tm, D), lambda p, m: (m, 0)),
                  pl.BlockSpec((None, D, N_IN), lambda p, m: (p, 0, 0))],
        out_specs=pl.BlockSpec((tm, N_IN), lambda p, m: (m, p)),
        out_shape=jax.ShapeDtypeStruct((S, NDEV * N_IN), F32),
        compiler_params=_cp(("parallel", "parallel")),
    )(h1, win)


def _out_proj_rms(x, ma, mr, wout, nw):
    tm = 256
    half = D // 2

    def body(x_ref, ma_ref, mr_ref, w_ref, nw_ref, x2_ref, h_ref, r_ref):
        acc = _dot(ma_ref[...], w_ref[0:half, :]) + _dot(mr_ref[...], w_ref[half:D, :])
        x2 = x_ref[...] + acc
        r = lax.rsqrt(jnp.mean(x2 * x2, axis=-1, keepdims=True) + EPS)
        x2_ref[...] = x2
        h_ref[...] = ((x2 * r) * nw_ref[...]).astype(BF16)
        r_ref[...] = r

    return pl.pallas_call(
        body, name="out_proj_rms", grid=(S // tm,),
        in_specs=[pl.BlockSpec((tm, D), lambda i: (i, 0)),
                  pl.BlockSpec((tm, half), lambda i: (i, 0)),
                  pl.BlockSpec((tm, half), lambda i: (i, 0)),
                  pl.BlockSpec((D, D), lambda i: (0, 0)),
                  pl.BlockSpec((1, D), lambda i: (0, 0))],
        out_specs=[pl.BlockSpec((tm, D), lambda i: (i, 0)), pl.BlockSpec((tm, D), lambda i: (i, 0)),
                   pl.BlockSpec((tm, 1), lambda i: (i, 0))],
        out_shape=[jax.ShapeDtypeStruct((S, D), F32), jax.ShapeDtypeStruct((S, D), BF16),
                   jax.ShapeDtypeStruct((S, 1), F32)],
        compiler_params=_cp(("parallel",)),
    )(x, ma, mr, wout, nw)


def _ffn_up(h2, wg, wu):
    tm = 512

    def body(h_ref, wg_ref, wu_ref, g_ref, u_ref, a_ref):
        h = h_ref[...]
        g = _dot(h, wg_ref[...])
        u = _dot(h, wu_ref[...])
        g_ref[...] = g
        u_ref[...] = u
        a_ref[...] = ((g * _sigmoid(g)) * u).astype(BF16)

    blk = pl.BlockSpec((None, tm, N_FF), lambda p, m: (p, m, 0))
    wblk = pl.BlockSpec((None, D, N_FF), lambda p, m: (p, 0, 0))
    return pl.pallas_call(
        body, name="ffn_up", grid=(NDEV, S // tm),
        in_specs=[pl.BlockSpec((tm, D), lambda p, m: (m, 0)), wblk, wblk],
        out_specs=[blk, blk, blk],
        out_shape=[jax.ShapeDtypeStruct((NDEV, S, N_FF), F32), jax.ShapeDtypeStruct((NDEV, S, N_FF), F32),
                   jax.ShapeDtypeStruct((NDEV, S, N_FF), BF16)],
        compiler_params=_cp(("parallel", "parallel")),
    )(h2, wg, wu)


def _ffn_down_loss(x2, a, wd, nw, tgt):
    tm = 512

    def body(x2_ref, a_ref, w_ref, nw_ref, t_ref, dx_ref, dxb_ref, st_ref, acc_ref):
        m, p = pl.program_id(0), pl.program_id(1)

        @pl.when(p == 0)
        def _():
            acc_ref[...] = jnp.zeros_like(acc_ref)

        @pl.when((p == 0) & (m == 0))
        def _():
            st_ref[...] = jnp.zeros_like(st_ref)

        acc_ref[...] += _dot(a_ref[...], w_ref[...])

        @pl.when(p == NDEV - 1)
        def _():
            x3 = x2_ref[...] + acc_ref[...]
            nwv = nw_ref[...]
            r = lax.rsqrt(jnp.mean(x3 * x3, axis=-1, keepdims=True) + EPS)
            y = (x3 * r) * nwv
            err = y - t_ref[...]
            loss = 0.5 * jnp.sum(jnp.mean(err * err, axis=-1, keepdims=True), axis=0, keepdims=True)
            dy = err * (1.0 / D)
            dx, dnw = _rms_bwd_tile(dy, x3, r, nwv)
            dx_ref[...] = dx
            dxb_ref[...] = dx.astype(BF16)
            st_ref[0:1, :] += dnw
            st_ref[1:2, :] += jnp.broadcast_to(loss, (1, D))

    return pl.pallas_call(
        body, name="ffn_down_loss", grid=(S // tm, NDEV),
        in_specs=[pl.BlockSpec((tm, D), lambda m, p: (m, 0)),
                  pl.BlockSpec((None, tm, N_FF), lambda m, p: (p, m, 0)),
                  pl.BlockSpec((None, N_FF, D), lambda m, p: (p, 0, 0)),
                  pl.BlockSpec((1, D), lambda m, p: (0, 0)),
                  pl.BlockSpec((tm, D), lambda m, p: (m, 0))],
        out_specs=[pl.BlockSpec((tm, D), lambda m, p: (m, 0)), pl.BlockSpec((tm, D), lambda m, p: (m, 0)),
                   pl.BlockSpec((8, D), lambda m, p: (0, 0))],
        out_shape=[jax.ShapeDtypeStruct((S, D), F32), jax.ShapeDtypeStruct((S, D), BF16),
                   jax.ShapeDtypeStruct((8, D), F32)],
        scratch_shapes=[pltpu.VMEM((tm, D), F32)],
        compiler_params=_cp(("arbitrary", "arbitrary")),
    )(x2, a, wd, nw, tgt)


def _ffn_down_bwd(dx3b, wd, g, u):
    tm = 512

    def body(dx_ref, w_ref, g_ref, u_ref, dg_ref, du_ref):
        da = _dot_nt(dx_ref[...], w_ref[...])
        gv = g_ref[...]
        sg = _sigmoid(gv)
        silu = gv * sg
        dg_ref[...] = ((da * u_ref[...]) * (sg * (1.0 + gv * (1.0 - sg)))).astype(BF16)
        du_ref[...] = (da * silu).astype(BF16)

    blk = pl.BlockSpec((None, tm, N_FF), lambda p, m: (p, m, 0))
    return pl.pallas_call(
        body, name="ffn_down_bwd", grid=(NDEV, S // tm),
        in_specs=[pl.BlockSpec((tm, D), lambda p, m: (m, 0)),
                  pl.BlockSpec((None, N_FF, D), lambda p, m: (p, 0, 0)), blk, blk],
        out_specs=[blk, blk],
        out_shape=[jax.ShapeDtypeStruct((NDEV, S, N_FF), BF16), jax.ShapeDtypeStruct((NDEV, S, N_FF), BF16)],
        compiler_params=_cp(("parallel", "parallel")),
    )(dx3b, wd, g, u)


def _ffn_up_bwd(dg, du, wg, wu, dres, xs, r, nw):
    tm = 512

    def body(dg_ref, du_ref, wg_ref, wu_ref, dres_ref, x_ref, r_ref, nw_ref, dx_ref, dxb_ref, st_ref, acc_ref):
        m, p = pl.program_id(0), pl.program_id(1)

        @pl.when(p == 0)
        def _():
            acc_ref[...] = jnp.zeros_like(acc_ref)

        @pl.when((p == 0) & (m == 0))
        def _():
            st_ref[...] = jnp.zeros_like(st_ref)

        acc_ref[...] += _dot_nt(dg_ref[...], wg_ref[...]) + _dot_nt(du_ref[...], wu_ref[...])

        @pl.when(p == NDEV - 1)
        def _():
            dx, dnw = _rms_bwd_tile(acc_ref[...], x_ref[...], r_ref[...], nw_ref[...])
            dx = dres_ref[...] + dx
            dx_ref[...] = dx
            dxb_ref[...] = dx.astype(BF16)
            st_ref[0:1, :] += dnw

    blk = pl.BlockSpec((None, tm, N_FF), lambda m, p: (p, m, 0))
    wblk = pl.BlockSpec((None, D, N_FF), lambda m, p: (p, 0, 0))
    row = pl.BlockSpec((tm, D), lambda m, p: (m, 0))
    return pl.pallas_call(
        body, name="ffn_up_bwd", grid=(S // tm, NDEV),
        in_specs=[blk, blk, wblk, wblk, row, row, pl.BlockSpec((tm, 1), lambda m, p: (m, 0)),
                  pl.BlockSpec((1, D), lambda m, p: (0, 0))],
        out_specs=[row, row, pl.BlockSpec((8, D), lambda m, p: (0, 0))],
        out_shape=[jax.ShapeDtypeStruct((S, D), F32), jax.ShapeDtypeStruct((S, D), BF16),
                   jax.ShapeDtypeStruct((8, D), F32)],
        scratch_shapes=[pltpu.VMEM((tm, D), F32)],
        compiler_params=_cp(("arbitrary", "arbitrary")),
    )(dg, du, wg, wu, dres, xs, r, nw)


def _out_proj_bwd(dx2b, wout):
    tm = 256

    def body(dx_ref, w_ref, o_ref):
        o_ref[...] = _dot_nt(dx_ref[...], w_ref[...])

    return pl.pallas_call(
        body, name="out_proj_bwd", grid=(S // tm,),
        in_specs=[pl.BlockSpec((tm, D), lambda i: (i, 0)), pl.BlockSpec((D, D), lambda i: (0, 0))],
        out_specs=pl.BlockSpec((tm, D), lambda i: (i, 0)),
        out_shape=jax.ShapeDtypeStruct((S, D), F32),
        compiler_params=_cp(("parallel",)),
    )(dx2b, wout)


def _in_proj_bwd(dproj, win, dres, xs, r, nw):
    tm = 512

    def body(dp_ref, w_ref, dres_ref, x_ref, r_ref, nw_ref, dx_ref, st_ref, acc_ref):
        m, p = pl.program_id(0), pl.program_id(1)

        @pl.when(p == 0)
        def _():
            acc_ref[...] = jnp.zeros_like(acc_ref)

        @pl.when((p == 0) & (m == 0))
        def _():
            st_ref[...] = jnp.zeros_like(st_ref)

        acc_ref[...] += _dot_nt(dp_ref[...], w_ref[...])

        @pl.when(p == NDEV - 1)
        def _():
            dx, dnw = _rms_bwd_tile(acc_ref[...], x_ref[...], r_ref[...], nw_ref[...])
            dx_ref[...] = dres_ref[...] + dx
            st_ref[0:1, :] += dnw

    row = pl.BlockSpec((tm, D), lambda m, p: (m, 0))
    return pl.pallas_call(
        body, name="in_proj_bwd", grid=(S // tm, NDEV),
        in_specs=[pl.BlockSpec((tm, N_IN), lambda m, p: (m, p)),
                  pl.BlockSpec((None, D, N_IN), lambda m, p: (p, 0, 0)),
                  row, row, pl.BlockSpec((tm, 1), lambda m, p: (m, 0)),
                  pl.BlockSpec((1, D), lambda m, p: (0, 0))],
        out_specs=[row, pl.BlockSpec((8, D), lambda m, p: (0, 0))],
        out_shape=[jax.ShapeDtypeStruct((S, D), F32), jax.ShapeDtypeStruct((8, D), F32)],
        scratch_shapes=[pltpu.VMEM((tm, D), F32)],
        compiler_params=_cp(("arbitrary", "arbitrary")),
    )(dproj, win, dres, xs, r, nw)


def _wgrad_cols(act, dy3, name):
    n = dy3.shape[-1]

    def body(a_ref, d_ref, o_ref):
        o_ref[...] = _dot_tn(a_ref[...], d_ref[...]).astype(BF16)

    return pl.pallas_call(
        body, name=name, grid=(NDEV,),
        in_specs=[pl.BlockSpec((S, D), lambda p: (0, 0)), pl.BlockSpec((None, S, n), lambda p: (p, 0, 0))],
        out_specs=pl.BlockSpec((None, D, n), lambda p: (p, 0, 0)),
        out_shape=jax.ShapeDtypeStruct((NDEV, D, n), BF16),
        compiler_params=_cp(("parallel",)),
    )(act, dy3)


def _wgrad_in(h1, dproj):
    def body(a_ref, d_ref, o_ref):
        o_ref[...] = _dot_tn(a_ref[...], d_ref[...]).astype(BF16)

    return pl.pallas_call(
        body, name="wgrad_in", grid=(NDEV,),
        in_specs=[pl.BlockSpec((S, D), lambda p: (0, 0)), pl.BlockSpec((S, N_IN), lambda p: (0, p))],
        out_specs=pl.BlockSpec((None, D, N_IN), lambda p: (p, 0, 0)),
        out_shape=jax.ShapeDtypeStruct((NDEV, D, N_IN), BF16),
        compiler_params=_cp(("parallel",)),
    )(h1, dproj)


def _wgrad_rows(a3, dy):
    def body(a_ref, d_ref, o_ref):
        o_ref[...] = _dot_tn(a_ref[...], d_ref[...]).astype(BF16)

    return pl.pallas_call(
        body, name="wgrad_down", grid=(NDEV,),
        in_specs=[pl.BlockSpec((None, S, N_FF), lambda p: (p, 0, 0)), pl.BlockSpec((S, D), lambda p: (0, 0))],
        out_specs=pl.BlockSpec((None, N_FF, D), lambda p: (p, 0, 0)),
        out_shape=jax.ShapeDtypeStruct((NDEV, N_FF, D), BF16),
        compiler_params=_cp(("parallel",)),
    )(a3, dy)


def _wgrad_out(ma, mr, dx2b):
    half = D // 2
    per = half // N_OUT

    def body(ma_ref, mr_ref, d_ref, o_ref):
        p = pl.program_id(0)

        @pl.when(p < per)
        def _():
            o_ref[...] = _dot_tn(ma_ref[...], d_ref[...]).astype(BF16)

        @pl.when(p >= per)
        def _():
            o_ref[...] = _dot_tn(mr_ref[...], d_ref[...]).astype(BF16)

    return pl.pallas_call(
        body, name="wgrad_out", grid=(NDEV,),
        in_specs=[pl.BlockSpec((S, N_OUT), lambda p: (0, jnp.minimum(p, per - 1))),
                  pl.BlockSpec((S, N_OUT), lambda p: (0, jnp.maximum(p - per, 0))),
                  pl.BlockSpec((S, D), lambda p: (0, 0))],
        out_specs=pl.BlockSpec((None, N_OUT, D), lambda p: (p, 0, 0)),
        out_shape=jax.ShapeDtypeStruct((NDEV, N_OUT, D), BF16),
        compiler_params=_cp(("parallel",)),
    )(ma, mr, dx2b)


def _attn_consts():
    c = np.zeros((AH, 8, AHD), np.float32)
    for h in range(AH):
        c[h, :, :] = 2.0 ** (-(h + 1))
    return jnp.asarray(c)


def _permute_in(dst, src, d, cast=None):
    ln = S // d
    for rr in range(d):
        v = src[pl.ds(rr, ln, stride=d), :] if d > 1 else src[...]
        dst[rr * ln:(rr + 1) * ln, :] = v if cast is None else v.astype(cast)


def _attn_masks():
    qi = lax.broadcasted_iota(jnp.int32, (CH, CH), 0)
    kj = lax.broadcasted_iota(jnp.int32, (CH, CH), 1)
    dist_c = (qi - kj).astype(F32)
    dist_p = (qi - kj + CH).astype(F32)
    return qi >= kj, kj >= qi, dist_c, dist_p


def _attn_fwd(proj):
    scale = 1.0 / math.sqrt(AHD)

    def body(c_ref, q_ref, k_ref, v_ref, o_ref, ob_ref, lse_ref, qd, kd, vd, od, ld, *nat):
        onat, lnat = nat[0:3], nat[3:6]
        slope = c_ref[0:1, :]
        mask_c, mask_p, dist_c, dist_p = _attn_masks()
        for pi, (d, nb) in enumerate(PATTERNS):
            _permute_in(qd, q_ref, d, BF16)
            _permute_in(kd, k_ref, d, BF16)
            _permute_in(vd, v_ref, d, BF16)
            bias_c = -(slope * float(d)) * dist_c
            bias_p = -(slope * float(d)) * dist_p

            def blk(b, carry, nb=nb, bias_c=bias_c, bias_p=bias_p):
                st = pl.multiple_of(b * CH, CH)
                qb = qd[pl.ds(st, CH), :]
                kc = kd[pl.ds(st, CH), :]
                vc = vd[pl.ds(st, CH), :]
                s_c = jnp.where(mask_c, _dot_nt(qb, kc) * scale + bias_c, NEG)
                mx = jnp.max(s_c, axis=-1, keepdims=True)
                if nb > 1:
                    pst = pl.multiple_of(jnp.maximum(b - 1, 0) * CH, CH)
                    kp = kd[pl.ds(pst, CH), :]
                    vp = vd[pl.ds(pst, CH), :]
                    has_prev = (b % nb) != 0
                    s_p = jnp.where(jnp.logical_and(mask_p, has_prev), _dot_nt(qb, kp) * scale + bias_p, NEG)
                    mx = jnp.maximum(mx, jnp.max(s_p, axis=-1, keepdims=True))
                    l = (jnp.sum(jnp.exp(s_c - mx), axis=-1, keepdims=True)
                         + jnp.sum(jnp.exp(s_p - mx), axis=-1, keepdims=True))
                    lse = mx + jnp.log(l)
                    o = _dot(jnp.exp(s_c - lse).astype(BF16), vc) + _dot(jnp.exp(s_p - lse).astype(BF16), vp)
                else:
                    l = jnp.sum(jnp.exp(s_c - mx), axis=-1, keepdims=True)
                    lse = mx + jnp.log(l)
                    o = _dot(jnp.exp(s_c - lse).astype(BF16), vc)
                od[pl.ds(st, CH), :] = o
                ld[pl.ds(st, CH), :] = jnp.broadcast_to(lse, (CH, AHD))
                return carry

            lax.fori_loop(0, NB, blk, 0)
            ln = S // d
            for rr in range(d):
                if d > 1:
                    onat[pi][pl.ds(rr, ln, stride=d), :] = od[rr * ln:(rr + 1) * ln, :]
                    lnat[pi][pl.ds(rr, ln, stride=d), :] = ld[rr * ln:(rr + 1) * ln, :]
                else:
                    onat[pi][...] = od[...]
                    lnat[pi][...] = ld[...]
        l0, l1, l2 = lnat[0][...], lnat[1][...], lnat[2][...]
        mx = jnp.maximum(jnp.maximum(l0, l1), l2)
        e0, e1, e2 = jnp.exp(l0 - mx), jnp.exp(l1 - mx), jnp.exp(l2 - mx)
        den = e0 + e1 + e2
        out = (e0 / den) * onat[0][...] + (e1 / den) * onat[1][...] + (e2 / den) * onat[2][...]
        o_ref[...] = out
        ob_ref[...] = out.astype(BF16)
        lse_ref[...] = mx + jnp.log(den)

    def col(off):
        return pl.BlockSpec((S, AHD), lambda h: (0, off + h))

    return pl.pallas_call(
        body, name="attn_fwd", grid=(AH,),
        in_specs=[pl.BlockSpec((None, 8, AHD), lambda h: (h, 0, 0)), col(0), col(AH), col(2 * AH)],
        out_specs=[col(0), col(0), col(0)],
        out_shape=[jax.ShapeDtypeStruct((S, AH * AHD), F32), jax.ShapeDtypeStruct((S, AH * AHD), BF16),
                   jax.ShapeDtypeStruct((S, AH * AHD), F32)],
        scratch_shapes=[pltpu.VMEM((S, AHD), BF16), pltpu.VMEM((S, AHD), BF16), pltpu.VMEM((S, AHD), BF16),
                        pltpu.VMEM((S, AHD), F32), pltpu.VMEM((S, AHD), F32)]
        + [pltpu.VMEM((S, AHD), F32) for _ in range(6)],
        compiler_params=_cp(("parallel",)),
    )(_attn_consts(), proj, proj, proj)


def _attn_bwd(proj, dmixed, o, lse):
    scale = 1.0 / math.sqrt(AHD)

    def body(c_ref, q_ref, k_ref, v_ref, do_ref, o_ref, lse_ref, dq_ref, dk_ref, dv_ref,
             qd, kd, vd, dod, lsd, dld, dqd, dkd, dvd, delta, aq, ak, av):
        slope = c_ref[0:1, :]
        mask_c, mask_p, dist_c, dist_p = _attn_masks()
        delta[...] = jnp.broadcast_to(jnp.sum(do_ref[...] * o_ref[...], axis=-1, keepdims=True), (S, AHD))
        for pi, (d, nb) in enumerate(PATTERNS):
            _permute_in(qd, q_ref, d, BF16)
            _permute_in(kd, k_ref, d, BF16)
            _permute_in(vd, v_ref, d, BF16)
            _permute_in(dod, do_ref, d, BF16)
            _permute_in(lsd, lse_ref, d)
            _permute_in(dld, delta, d)
            dkd[...] = jnp.zeros_like(dkd)
            dvd[...] = jnp.zeros_like(dvd)
            bias_c = -(slope * float(d)) * dist_c
            bias_p = -(slope * float(d)) * dist_p

            def blk(b, carry, nb=nb, bias_c=bias_c, bias_p=bias_p):
                st = pl.multiple_of(b * CH, CH)
                cur = pl.ds(st, CH)
                qb, kc, vc, dob = qd[cur, :], kd[cur, :], vd[cur, :], dod[cur, :]
                ls, dl = lsd[cur, :], dld[cur, :]
                p_c = jnp.exp(jnp.where(mask_c, _dot_nt(qb, kc) * scale + bias_c, NEG) - ls)
                ds_c = ((p_c * (_dot_nt(dob, vc) - dl)) * scale).astype(BF16)
                dq = _dot(ds_c, kc)
                dkd[cur, :] += _dot_tn(ds_c, qb)
                dvd[cur, :] += _dot_tn(p_c.astype(BF16), dob)
                if nb > 1:
                    prev = pl.ds(pl.multiple_of(jnp.maximum(b - 1, 0) * CH, CH), CH)
                    kp, vp = kd[prev, :], vd[prev, :]
                    has_prev = (b % nb) != 0
                    p_p = jnp.exp(jnp.where(jnp.logical_and(mask_p, has_prev),
                                            _dot_nt(qb, kp) * scale + bias_p, NEG) - ls)
                    ds_p = ((p_p * (_dot_nt(dob, vp) - dl)) * scale).astype(BF16)
                    dq = dq + _dot(ds_p, kp)
                    dkd[prev, :] += _dot_tn(ds_p, qb)
                    dvd[prev, :] += _dot_tn(p_p.astype(BF16), dob)
                dqd[cur, :] = dq
                return carry

            lax.fori_loop(0, NB, blk, 0)
            ln = S // d
            for acc, src in ((aq, dqd), (ak, dkd), (av, dvd)):
                if pi == 0:
                    acc[...] = src[...]
                else:
                    for rr in range(d):
                        acc[pl.ds(rr, ln, stride=d), :] += src[rr * ln:(rr + 1) * ln, :]
        dq_ref[...] = aq[...].astype(BF16)
        dk_ref[...] = ak[...].astype(BF16)
        dv_ref[...] = av[...].astype(BF16)

    def col(off):
        return pl.BlockSpec((S, AHD), lambda h: (0, off + h))

    bf = lambda: pltpu.VMEM((S, AHD), BF16)
    f3 = lambda: pltpu.VMEM((S, AHD), F32)
    return pl.pallas_call(
        body, name="attn_bwd", grid=(AH,),
        in_specs=[pl.BlockSpec((None, 8, AHD), lambda h: (h, 0, 0)), col(0), col(AH), col(2 * AH),
                  col(0), col(0), col(0)],
        out_specs=[col(0), col(0), col(0)],
        out_shape=[jax.ShapeDtypeStruct((S, AH * AHD), BF16)] * 3,
        scratch_shapes=[bf(), bf(), bf(), bf(), f3(), f3(), f3(), f3(), f3(), f3(), f3(), f3(), f3()],
        compiler_params=_cp(("parallel",)),
    )(_attn_consts(), proj, proj, proj, dmixed, o, lse)


def _ret_consts():
    c = np.zeros((RH, 8, RHD), np.float32)
    for h in range(RH):
        c[h, :, :] = np.log(np.float32(1.0) - np.float32(2.0 ** (-5.0 - h)))
    return jnp.asarray(c)


def _ret_factors(lg):
    i = lax.broadcasted_iota(jnp.int32, (CH, CH), 0)
    j = lax.broadcasted_iota(jnp.int32, (CH, CH), 1)
    dif = (i - j).astype(F32)
    decay = jnp.where(dif >= 0, jnp.exp(lg[:, 0:CH] * jnp.maximum(dif, 0.0)), 0.0)
    row = lax.broadcasted_iota(jnp.int32, (CH, RHD), 0).astype(F32)
    zeta = jnp.exp(lg * (CH - 1.0 - row))
    xi = jnp.exp(lg * (row + 1.0))
    return decay, zeta, xi, jnp.exp(lg * float(CH))


def _ret_specs(rev):
    off = 3 * AH * AHD // RHD

    def ch(n):
        return (NB - 1 - n) if rev else n

    def col(k):
        return pl.BlockSpec((CH, RHD), lambda h, n: (ch(n), off + k * RH + h))

    own = pl.BlockSpec((CH, RHD), lambda h, n: (ch(n), h))
    state = pl.BlockSpec((None, None, RHD, RHD), lambda h, n: (h, ch(n), 0, 0))
    const = pl.BlockSpec((None, 8, RHD), lambda h, n: (h, 0, 0))
    return col, own, state, const


def _ret_fwd(proj):
    def body(c_ref, q_ref, k_ref, v_ref, g_ref, ret_ref, mr_ref, st_ref, r_acc):
        n = pl.program_id(1)

        @pl.when(n == 0)
        def _():
            r_acc[...] = jnp.zeros_like(r_acc)

        decay, zeta, xi, gch = _ret_factors(c_ref[0:1, :])
        qb = q_ref[...].astype(BF16)
        kc = k_ref[...] * (1.0 / math.sqrt(RHD))
        kb = kc.astype(BF16)
        vb = v_ref[...].astype(BF16)
        rb = r_acc[...].astype(BF16)
        st_ref[...] = rb
        scores = _dot_nt(qb, kb) * decay
        ret = _dot(scores.astype(BF16), vb) + _dot(qb, rb) * xi
        r_acc[...] = r_acc[...] * gch + _dot_tn((kc * zeta).astype(BF16), vb)
        ret_ref[...] = ret
        rr = lax.rsqrt(jnp.mean(ret * ret, axis=-1, keepdims=True) + EPS)
        gv = g_ref[...]
        mr_ref[...] = ((gv * _sigmoid(gv)) * (ret * rr)).astype(BF16)

    col, own, state, const = _ret_specs(False)
    return pl.pallas_call(
        body, name="ret_fwd", grid=(RH, NB),
        in_specs=[const, col(0), col(1), col(2), col(3)],
        out_specs=[own, own, state],
        out_shape=[jax.ShapeDtypeStruct((S, RH * RHD), F32), jax.ShapeDtypeStruct((S, RH * RHD), BF16),
                   jax.ShapeDtypeStruct((RH, NB, RHD, RHD), BF16)],
        scratch_shapes=[pltpu.VMEM((RHD, RHD), F32)],
        compiler_params=_cp(("parallel", "arbitrary")),
    )(_ret_consts(), proj, proj, proj, proj)


def _ret_bwd(proj, ret, states, dmixed):
    def body(c_ref, q_ref, k_ref, v_ref, g_ref, ret_ref, st_ref, dm_ref, dq_ref, dk_ref, dv_ref, dg_ref, g_acc):
        n = pl.program_id(1)

        @pl.when(n == 0)
        def _():
            g_acc[...] = jnp.zeros_like(g_acc)

        decay, zeta, xi, gch = _ret_factors(c_ref[0:1, :])
        ret_v = ret_ref[...]
        rr = lax.rsqrt(jnp.mean(ret_v * ret_v, axis=-1, keepdims=True) + EPS)
        gv = g_ref[...]
        sg = _sigmoid(gv)
        dmix = dm_ref[...]
        dg_ref[...] = ((dmix * (ret_v * rr)) * (sg * (1.0 + gv * (1.0 - sg)))).astype(BF16)
        dretn = dmix * (gv * sg)
        dret = rr * dretn - ret_v * ((rr * rr * rr) * jnp.mean(dretn * ret_v, axis=-1, keepdims=True))

        qb = q_ref[...].astype(BF16)
        kc = k_ref[...] * (1.0 / math.sqrt(RHD))
        kb = kc.astype(BF16)
        vb = v_ref[...].astype(BF16)
        rb = st_ref[...]
        db = dret.astype(BF16)
        sc = (_dot_nt(qb, kb) * decay).astype(BF16)
        da = (_dot_nt(db, vb) * decay).astype(BF16)
        dxi = (dret * xi).astype(BF16)
        gb = g_acc[...].astype(BF16)
        kz = (kc * zeta).astype(BF16)
        dq = _dot(da, kb) + _dot_nt(dxi, rb)
        dkc = _dot_tn(da, qb) + _dot_nt(vb, gb) * zeta
        dv = _dot_tn(sc, db) + _dot(kz, gb)
        g_acc[...] = _dot_tn(qb, dxi) + gch * g_acc[...]
        dq_ref[...] = dq.astype(BF16)
        dk_ref[...] = (dkc * (1.0 / math.sqrt(RHD))).astype(BF16)
        dv_ref[...] = dv.astype(BF16)

    col, own, state, const = _ret_specs(True)
    dm = pl.BlockSpec((CH, RHD), lambda h, n: (NB - 1 - n, AH * AHD // RHD + h))
    return pl.pallas_call(
        body, name="ret_bwd", grid=(RH, NB),
        in_specs=[const, col(0), col(1), col(2), col(3), own, state, dm],
        out_specs=[own, own, own, own],
        out_shape=[jax.ShapeDtypeStruct((S, RH * RHD), BF16)] * 4,
        scratch_shapes=[pltpu.VMEM((RHD, RHD), F32)],
        compiler_params=_cp(("parallel", "arbitrary")),
    )(_ret_consts(), proj, proj, proj, proj, ret, states, dmixed)


def _local_step(x, tgt, nw1, nw2, nw3, win, wout, wg, wu, wd):
    h1, r1 = _rms_fwd(x, nw1)
    proj = _proj(h1, win)
    o, ma, lse = _attn_fwd(proj)
    ret, mr, states = _ret_fwd(proj)
    x2, h2, r2 = _out_proj_rms(x, ma, mr, wout, nw2)
    g, u, a = _ffn_up(h2, wg, wu)
    dx3, dx3b, st3 = _ffn_down_loss(x2, a, wd, nw3, tgt)

    dwd = _wgrad_rows(a, dx3b)
    dg, du = _ffn_down_bwd(dx3b, wd, g, u)
    dwg = _wgrad_cols(h2, dg, "wgrad_gate")
    dwu = _wgrad_cols(h2, du, "wgrad_up")
    dx2, dx2b, st2 = _ffn_up_bwd(dg, du, wg, wu, dx3, x2, r2, nw2)
    dwo = _wgrad_out(ma, mr, dx2b)
    dmixed = _out_proj_bwd(dx2b, wout)
    dqa, dka, dva = _attn_bwd(proj, dmixed, o, lse)
    dqr, dkr, dvr, dgr = _ret_bwd(proj, ret, states, dmixed)
    dproj = jnp.concatenate([dqa, dka, dva, dqr, dkr, dvr, dgr], axis=1)
    dwi = _wgrad_in(h1, dproj)
    gx, st1 = _in_proj_bwd(dproj, win, dx2, x, r1, nw1)
    stats = jnp.concatenate([st1[0:1], st2[0:1], st3[0:2], jnp.zeros((4, D), F32)], axis=0)
    return stats, gx, dwi, dwo, dwg, dwu, dwd


def _place():
    x, y, c = lax.axis_index("x"), lax.axis_index("y"), lax.axis_index("c")
    return x, y, c, [(1 - x, y), (x, 1 - y), (1 - x, 1 - y)]


def _all_gather(shards):
    na = len(shards)

    def body(*refs):
        ins, outs = refs[:na], refs[na:2 * na]
        send_sems, recv_sems, local_sems = refs[2 * na:]
        x, y, c, chips = _place()
        sib = (x, y, 1 - c)

        def copy(a, k, block, to, src=None):
            idx = 4 * block[0] + 2 * block[1] + block[2]
            return pltpu.make_async_remote_copy(
                src_ref=outs[a].at[idx] if src is None else src, dst_ref=outs[a].at[idx],
                send_sem=send_sems.at[a, k], recv_sem=recv_sems.at[a, k], device_id=to, device_id_type=MESH)

        me = (x, y, c)
        mine = [pltpu.make_async_copy(ins[a], outs[a].at[4 * x + 2 * y + c], local_sems.at[a]) for a in range(na)]
        for cp in mine:
            cp.start()
        first = []
        for a in range(na):
            first += [copy(a, 1 + j, me, (*chip, c), src=ins[a]) for j, chip in enumerate(chips)]
        for a in range(na):
            first.append(copy(a, 0, me, sib, src=ins[a]))
        for cp in first:
            cp.start()
        passed = []
        for a in range(na):
            for j, chip in enumerate(chips):
                copy(a, 1 + j, (*chip, c), me).wait_recv()
                fw = copy(a, 4 + j, (*chip, c), sib)
                fw.start()
                passed.append(fw)
        for a in range(na):
            copy(a, 0, (x, y, 1 - c), me).wait_recv()
            for j, chip in enumerate(chips):
                copy(a, 4 + j, (*chip, 1 - c), me).wait_recv()
        for cp in first + passed:
            cp.wait_send()
        for cp in mine:
            cp.wait()

    any_spec = pl.BlockSpec(memory_space=pl.ANY)
    return pl.pallas_call(
        body, name="all_gather_weights",
        in_specs=[any_spec] * na, out_specs=[any_spec] * na,
        out_shape=[jax.ShapeDtypeStruct((NDEV,) + s.shape, s.dtype) for s in shards],
        scratch_shapes=[pltpu.SemaphoreType.DMA((na, 7)), pltpu.SemaphoreType.DMA((na, 7)),
                        pltpu.SemaphoreType.DMA((na,))],
    )(*shards)


def _exchange_sibling(grads):
    na = len(grads)

    def body(*refs):
        ins, outs = refs[:na], refs[na:2 * na]
        send_sems, recv_sems = refs[2 * na:]
        x, y, c, _ = _place()
        cps = []
        for a in range(na):
            for k in range(4):
                cps.append(pltpu.make_async_remote_copy(
                    src_ref=ins[a].at[2 * k + (1 - c)], dst_ref=outs[a].at[k],
                    send_sem=send_sems.at[a, k], recv_sem=recv_sems.at[a, k],
                    device_id=(x, y, 1 - c), device_id_type=MESH))
        for cp in cps:
            cp.start()
        for cp in cps:
            cp.wait()

    any_spec = pl.BlockSpec(memory_space=pl.ANY)
    return pl.pallas_call(
        body, name="reduce_sibling_exchange",
        in_specs=[any_spec] * na, out_specs=[any_spec] * na,
        out_shape=[jax.ShapeDtypeStruct((4,) + g.shape[1:], g.dtype) for g in grads],
        scratch_shapes=[pltpu.SemaphoreType.DMA((na, 4)), pltpu.SemaphoreType.DMA((na, 4))],
    )(*grads)


def _row_tile(rows, cols):
    for t in (512, 256, 176, 128, 64, 32, 16):
        if rows % t == 0 and t * cols * 4 <= (1 << 20):
            return t
    raise ValueError((rows, cols))


def _chip_sum(place, g, got, name):
    _, r, c = g.shape
    tm = _row_tile(r, c)

    def body(pos_ref, g_ref, got_ref, o_ref):
        o_ref[...] = (g_ref[...].astype(F32) + got_ref[...].astype(F32)).astype(BF16)

    return pl.pallas_call(
        body, name=name,
        grid_spec=pltpu.PrefetchScalarGridSpec(
            num_scalar_prefetch=1, grid=(4, r // tm),
            in_specs=[pl.BlockSpec((None, tm, c), lambda k, i, pos: (2 * k + pos[2], i, 0)),
                      pl.BlockSpec((None, tm, c), lambda k, i, pos: (k, i, 0))],
            out_specs=pl.BlockSpec((None, tm, c), lambda k, i, pos: (k, i, 0))),
        out_shape=jax.ShapeDtypeStruct((4, r, c), BF16),
        compiler_params=_cp(("parallel", "parallel")),
    )(place, g, got)


def _exchange_chips(sums, stats):
    na = len(sums)

    def body(*refs):
        ins, st_in = refs[:na], refs[na]
        outs, st_out = refs[na + 1:2 * na + 1], refs[2 * na + 1]
        send_sems, recv_sems, st_send, st_recv, local_sem = refs[2 * na + 2:]
        x, y, c, chips = _place()
        me_idx = 4 * x + 2 * y + c
        mine = pltpu.make_async_copy(st_in, st_out.at[me_idx], local_sem)
        mine.start()
        cps = []
        for a in range(na):
            for j, chip in enumerate(chips):
                cps.append(pltpu.make_async_remote_copy(
                    src_ref=ins[a].at[2 * chip[0] + chip[1]], dst_ref=outs[a].at[j],
                    send_sem=send_sems.at[a, j], recv_sem=recv_sems.at[a, j],
                    device_id=(*chip, c), device_id_type=MESH))
        for k in range(1, 8):
            fx, fy, fc = (k >> 2) & 1, (k >> 1) & 1, k & 1
            peer = (x ^ fx, y ^ fy, c ^ fc)
            cps.append(pltpu.make_async_remote_copy(
                src_ref=st_in, dst_ref=st_out.at[me_idx],
                send_sem=st_send.at[k - 1], recv_sem=st_recv.at[k - 1], device_id=peer, device_id_type=MESH))
        for cp in cps:
            cp.start()
        for cp in cps:
            cp.wait()
        mine.wait()

    any_spec = pl.BlockSpec(memory_space=pl.ANY)
    return pl.pallas_call(
        body, name="reduce_chip_exchange",
        in_specs=[any_spec] * (na + 1), out_specs=[any_spec] * (na + 1),
        out_shape=[jax.ShapeDtypeStruct((3,) + s.shape[1:], s.dtype) for s in sums]
        + [jax.ShapeDtypeStruct((NDEV,) + stats.shape, stats.dtype)],
        scratch_shapes=[pltpu.SemaphoreType.DMA((na, 3)), pltpu.SemaphoreType.DMA((na, 3)),
                        pltpu.SemaphoreType.DMA((7,)), pltpu.SemaphoreType.DMA((7,)), pltpu.SemaphoreType.DMA],
    )(*sums, stats)


def _adamw(w, g, m, v):
    m = ADAM_B1 * m + (1.0 - ADAM_B1) * g
    v = ADAM_B2 * v + (1.0 - ADAM_B2) * (g * g)
    m_hat = m / (1.0 - ADAM_B1 ** ADAM_STEP)
    v_hat = v / (1.0 - ADAM_B2 ** ADAM_STEP)
    delta = -ADAM_LR * (m_hat / (jnp.sqrt(v_hat) + ADAM_EPS) + ADAM_WD * w)
    return delta, m, v


def _shard_update(place, w, m, v, g, got_sib, got_chips, name):
    r, c = w.shape
    tm = _row_tile(r, c)

    def body(pos_ref, w_ref, m_ref, v_ref, g_ref, s_ref, c_ref, go_ref, d_ref, mo_ref, vo_ref):
        grad = g_ref[...].astype(F32) + s_ref[...].astype(F32)
        for j in range(3):
            grad = grad + c_ref[j].astype(F32)
        delta, mn, vn = _adamw(w_ref[...], grad, m_ref[...], v_ref[...])
        go_ref[...] = grad
        d_ref[...] = delta
        mo_ref[...] = mn
        vo_ref[...] = vn

    row = pl.BlockSpec((tm, c), lambda i, pos: (i, 0))
    return pl.pallas_call(
        body, name=name,
        grid_spec=pltpu.PrefetchScalarGridSpec(
            num_scalar_prefetch=1, grid=(r // tm,),
            in_specs=[row, row, row,
                      pl.BlockSpec((None, tm, c), lambda i, pos: (4 * pos[0] + 2 * pos[1] + pos[2], i, 0)),
                      pl.BlockSpec((None, tm, c), lambda i, pos: (2 * pos[0] + pos[1], i, 0)),
                      pl.BlockSpec((3, tm, c), lambda i, pos: (0, i, 0))],
            out_specs=[row, row, row, row]),
        out_shape=[jax.ShapeDtypeStruct((r, c), F32)] * 4,
        compiler_params=_cp(("parallel",)),
    )(place, w, m, v, g, got_sib, got_chips)


def _small_update(stats_all, ws, ms, vs):
    def body(st_ref, w_ref, m_ref, v_ref, go_ref, d_ref, mo_ref, vo_ref):
        grad = st_ref[0]
        for k in range(1, NDEV):
            grad = grad + st_ref[k]
        delta, mn, vn = _adamw(w_ref[...], grad, m_ref[...], v_ref[...])
        go_ref[...] = grad
        d_ref[...] = delta
        mo_ref[...] = mn
        vo_ref[...] = vn

    return pl.pallas_call(
        body, name="small_update",
        out_shape=[jax.ShapeDtypeStruct((8, D), F32)] * 4,
        compiler_params=_cp(),
    )(stats_all, ws, ms, vs)


def kernel(x, norm_mix_w, w_in, w_out, norm_ffn_w, w_gate, w_up, w_down, norm_final_w, loss_target, m_norm_mix_w, m_w_in, m_w_out, m_norm_ffn_w, m_w_gate, m_w_up, m_w_down, m_norm_final_w, v_norm_mix_w, v_w_in, v_w_out, v_norm_ffn_w, v_w_gate, v_w_up, v_w_down, v_norm_final_w):
    big_w = [w_in[0], w_out[0], w_gate[0], w_up[0], w_down[0]]
    big_m = [m_w_in[0], m_w_out[0], m_w_gate[0], m_w_up[0], m_w_down[0]]
    big_v = [v_w_in[0], v_w_out[0], v_w_gate[0], v_w_up[0], v_w_down[0]]
    names = ["w_in", "w_out", "w_gate", "w_up", "w_down"]

    shards = [_cast_bf16(w, "cast_" + n) for w, n in zip(big_w, names)]
    win, wout, wg, wu, wd = _all_gather(shards)
    nw3 = norm_final_w.reshape(1, D)
    stats, gx, dwi, dwo, dwg, dwu, dwd = _local_step(
        x[0], loss_target[0], norm_mix_w, norm_ffn_w, nw3, win, wout.reshape(D, D), wg, wu, wd)

    place = jnp.stack([lax.axis_index("x"), lax.axis_index("y"), lax.axis_index("c")]).astype(jnp.int32)
    grads = [dwi, dwo, dwg, dwu, dwd]
    got_sib = _exchange_sibling(grads)
    sums = [_chip_sum(place, g, s, "chip_sum_" + n) for g, s, n in zip(grads, got_sib, names)]
    *got_chips, stats_all = _exchange_chips(sums, stats)
    upd = [_shard_update(place, w, m, v, g, s, c, "update_" + n)
           for w, m, v, g, s, c, n in zip(big_w, big_m, big_v, grads, got_sib, got_chips, names)]

    def rows(a, b, c):
        return jnp.concatenate([a.reshape(1, D), b.reshape(1, D), c.reshape(1, D), jnp.zeros((5, D), F32)], axis=0)

    sg, sd, sm, sv = _small_update(stats_all, rows(norm_mix_w, norm_ffn_w, norm_final_w),
                                   rows(m_norm_mix_w, m_norm_ffn_w, m_norm_final_w),
                                   rows(v_norm_mix_w, v_norm_ffn_w, v_norm_final_w) + jnp.concatenate(
                                       [jnp.zeros((3, D), F32), jnp.ones((5, D), F32)], axis=0))
    loss = sg[3, 0]

    def outs(k, small):
        big = [u[k][None] for u in upd]
        return [small[0:1], big[0], big[1], small[1:2], big[2], big[3], big[4], small[2]]

    return (loss, gx[None], *outs(0, sg), *outs(1, sd), *outs(2, sm), *outs(3, sv))
```

```python
import functools
import math

import numpy as np
import jax
import jax.numpy as jnp
from jax import lax
from jax.experimental import pallas as pl
from jax.experimental.pallas import tpu as pltpu
from jax.experimental.pallas import tpu_sc as plsc

F32 = jnp.float32
BF16 = jnp.bfloat16

S = 2048
D = 2048
NDEV = 8
N_IN = 7168 // NDEV
N_FF = 5632 // NDEV
N_OUT = 2048 // NDEV
AH, AHD = 8, 128
RH, RHD = 4, 256
CH = 128
NB = S // CH
EPS = 1e-6
PATTERNS = ((1, 16), (4, 4), (16, 1))
NEG = -1e30
VMEM_LIMIT = 56 * 1024 * 1024

ADAM_LR, ADAM_B1, ADAM_B2, ADAM_EPS, ADAM_WD, ADAM_STEP = 0.001, 0.9, 0.999, 1e-08, 0.01, 10
MESH = pl.DeviceIdType.MESH


def _cp(sem=None):
    return pltpu.CompilerParams(dimension_semantics=sem, vmem_limit_bytes=VMEM_LIMIT)


def _dot(a, b):
    return jnp.dot(a, b, preferred_element_type=F32)


def _dot_nt(a, b):
    return lax.dot_general(a, b, (((1,), (1,)), ((), ())), preferred_element_type=F32)


def _dot_tn(a, b):
    return lax.dot_general(a, b, (((0,), (0,)), ((), ())), preferred_element_type=F32)


def _sigmoid(x):
    return 1.0 / (1.0 + jnp.exp(-x))


def _cast_bf16(w, name):
    r, c = w.shape
    tm = r if r <= 1024 else 512

    def body(w_ref, o_ref):
        o_ref[...] = w_ref[...].astype(BF16)

    return pl.pallas_call(
        body, name=name, grid=(r // tm,),
        in_specs=[pl.BlockSpec((tm, c), lambda i: (i, 0))],
        out_specs=pl.BlockSpec((tm, c), lambda i: (i, 0)),
        out_shape=jax.ShapeDtypeStruct((r, c), BF16),
        compiler_params=_cp(("parallel",)),
    )(w)


def _rms_fwd(x, nw):
    tm = 256

    def body(x_ref, w_ref, h_ref, r_ref):
        xs = x_ref[...]
        r = lax.rsqrt(jnp.mean(xs * xs, axis=-1, keepdims=True) + EPS)
        h_ref[...] = ((xs * r) * w_ref[...]).astype(BF16)
        r_ref[...] = r

    return pl.pallas_call(
        body, name="rms_fwd", grid=(S // tm,),
        in_specs=[pl.BlockSpec((tm, D), lambda i: (i, 0)), pl.BlockSpec((1, D), lambda i: (0, 0))],
        out_specs=[pl.BlockSpec((tm, D), lambda i: (i, 0)), pl.BlockSpec((tm, 1), lambda i: (i, 0))],
        out_shape=[jax.ShapeDtypeStruct((S, D), BF16), jax.ShapeDtypeStruct((S, 1), F32)],
        compiler_params=_cp(("parallel",)),
    )(x, nw)


def _rms_bwd_tile(dh, xs, r, nw):
    dnw = jnp.sum(dh * (xs * r), axis=0, keepdims=True)
    gy = dh * nw
    dx = r * gy - xs * ((r * r * r) * jnp.mean(gy * xs, axis=-1, keepdims=True))
    return dx, dnw


def _proj(h1, win):
    tm = 512

    def body(a_ref, w_ref, o_ref):
        o_ref[...] = _dot(a_ref[...], w_ref[...])

    return pl.pallas_call(
        body, name="proj", grid=(NDEV, S // tm),
        in_specs=[pl.BlockSpec((tm, D), lambda p, m: (m, 0)),
                  pl.BlockSpec((None, D, N_IN), lambda p, m: (p, 0, 0))],
        out_specs=pl.BlockSpec((tm, N_IN), lambda p, m: (m, p)),
        out_shape=jax.ShapeDtypeStruct((S, NDEV * N_IN), F32),
        compiler_params=_cp(("parallel", "parallel")),
    )(h1, win)


def _out_proj_rms(x, ma, mr, wout, nw):
    tm = 256
    half = D // 2

    def body(x_ref, ma_ref, mr_ref, w_ref, nw_ref, x2_ref, h_ref, r_ref):
        acc = _dot(ma_ref[...], w_ref[0:half, :]) + _dot(mr_ref[...], w_ref[half:D, :])
        x2 = x_ref[...] + acc
        r = lax.rsqrt(jnp.mean(x2 * x2, axis=-1, keepdims=True) + EPS)
        x2_ref[...] = x2
        h_ref[...] = ((x2 * r) * nw_ref[...]).astype(BF16)
        r_ref[...] = r

    return pl.pallas_call(
        body, name="out_proj_rms", grid=(S // tm,),
        in_specs=[pl.BlockSpec((tm, D), lambda i: (i, 0)),
                  pl.BlockSpec((tm, half), lambda i: (i, 0)),
                  pl.BlockSpec((tm, half), lambda i: (i, 0)),
                  pl.BlockSpec((D, D), lambda i: (0, 0)),
                  pl.BlockSpec((1, D), lambda i: (0, 0))],
        out_specs=[pl.BlockSpec((tm, D), lambda i: (i, 0)), pl.BlockSpec((tm, D), lambda i: (i, 0)),
                   pl.BlockSpec((tm, 1), lambda i: (i, 0))],
        out_shape=[jax.ShapeDtypeStruct((S, D), F32), jax.ShapeDtypeStruct((S, D), BF16),
                   jax.ShapeDtypeStruct((S, 1), F32)],
        compiler_params=_cp(("parallel",)),
    )(x, ma, mr, wout, nw)


def _ffn_up(h2, wg, wu):
    tm = 512

    def body(h_ref, wg_ref, wu_ref, g_ref, u_ref, a_ref):
        h = h_ref[...]
        g = _dot(h, wg_ref[...])
        u = _dot(h, wu_ref[...])
        g_ref[...] = g
        u_ref[...] = u
        a_ref[...] = ((g * _sigmoid(g)) * u).astype(BF16)

    blk = pl.BlockSpec((None, tm, N_FF), lambda p, m: (p, m, 0))
    wblk = pl.BlockSpec((None, D, N_FF), lambda p, m: (p, 0, 0))
    return pl.pallas_call(
        body, name="ffn_up", grid=(NDEV, S // tm),
        in_specs=[pl.BlockSpec((tm, D), lambda p, m: (m, 0)), wblk, wblk],
        out_specs=[blk, blk, blk],
        out_shape=[jax.ShapeDtypeStruct((NDEV, S, N_FF), F32), jax.ShapeDtypeStruct((NDEV, S, N_FF), F32),
                   jax.ShapeDtypeStruct((NDEV, S, N_FF), BF16)],
        compiler_params=_cp(("parallel", "parallel")),
    )(h2, wg, wu)


def _ffn_down_loss(x2, a, wd, nw, tgt):
    tm = 512

    def body(x2_ref, a_ref, w_ref, nw_ref, t_ref, dx_ref, dxb_ref, st_ref, acc_ref):
        m, p = pl.program_id(0), pl.program_id(1)

        @pl.when(p == 0)
        def _():
            acc_ref[...] = jnp.zeros_like(acc_ref)

        @pl.when((p == 0) & (m == 0))
        def _():
            st_ref[...] = jnp.zeros_like(st_ref)

        acc_ref[...] += _dot(a_ref[...], w_ref[...])

        @pl.when(p == NDEV - 1)
        def _():
            x3 = x2_ref[...] + acc_ref[...]
            nwv = nw_ref[...]
            r = lax.rsqrt(jnp.mean(x3 * x3, axis=-1, keepdims=True) + EPS)
            y = (x3 * r) * nwv
            err = y - t_ref[...]
            loss = 0.5 * jnp.sum(jnp.mean(err * err, axis=-1, keepdims=True), axis=0, keepdims=True)
            dy = err * (1.0 / D)
            dx, dnw = _rms_bwd_tile(dy, x3, r, nwv)
            dx_ref[...] = dx
            dxb_ref[...] = dx.astype(BF16)
            st_ref[0:1, :] += dnw
            st_ref[1:2, :] += jnp.broadcast_to(loss, (1, D))

    return pl.pallas_call(
        body, name="ffn_down_loss", grid=(S // tm, NDEV),
        in_specs=[pl.BlockSpec((tm, D), lambda m, p: (m, 0)),
                  pl.BlockSpec((None, tm, N_FF), lambda m, p: (p, m, 0)),
                  pl.BlockSpec((None, N_FF, D), lambda m, p: (p, 0, 0)),
                  pl.BlockSpec((1, D), lambda m, p: (0, 0)),
                  pl.BlockSpec((tm, D), lambda m, p: (m, 0))],
        out_specs=[pl.BlockSpec((tm, D), lambda m, p: (m, 0)), pl.BlockSpec((tm, D), lambda m, p: (m, 0)),
                   pl.BlockSpec((8, D), lambda m, p: (0, 0))],
        out_shape=[jax.ShapeDtypeStruct((S, D), F32), jax.ShapeDtypeStruct((S, D), BF16),
                   jax.ShapeDtypeStruct((8, D), F32)],
        scratch_shapes=[pltpu.VMEM((tm, D), F32)],
        compiler_params=_cp(("arbitrary", "arbitrary")),
    )(x2, a, wd, nw, tgt)


def _ffn_down_bwd(dx3b, wd, g, u):
    tm = 512

    def body(dx_ref, w_ref, g_ref, u_ref, dg_ref, du_ref):
        da = _dot_nt(dx_ref[...], w_ref[...])
        gv = g_ref[...]
        sg = _sigmoid(gv)
        silu = gv * sg
        dg_ref[...] = ((da * u_ref[...]) * (sg * (1.0 + gv * (1.0 - sg)))).astype(BF16)
        du_ref[...] = (da * silu).astype(BF16)

    blk = pl.BlockSpec((None, tm, N_FF), lambda p, m: (p, m, 0))
    return pl.pallas_call(
        body, name="ffn_down_bwd", grid=(NDEV, S // tm),
        in_specs=[pl.BlockSpec((tm, D), lambda p, m: (m, 0)),
                  pl.BlockSpec((None, N_FF, D), lambda p, m: (p, 0, 0)), blk, blk],
        out_specs=[blk, blk],
        out_shape=[jax.ShapeDtypeStruct((NDEV, S, N_FF), BF16), jax.ShapeDtypeStruct((NDEV, S, N_FF), BF16)],
        compiler_params=_cp(("parallel", "parallel")),
    )(dx3b, wd, g, u)


def _ffn_up_bwd(dg, du, wg, wu, dres, xs, r, nw):
    tm = 512

    def body(dg_ref, du_ref, wg_ref, wu_ref, dres_ref, x_ref, r_ref, nw_ref, dx_ref, dxb_ref, st_ref, acc_ref):
        m, p = pl.program_id(0), pl.program_id(1)

        @pl.when(p == 0)
        def _():
            acc_ref[...] = jnp.zeros_like(acc_ref)

        @pl.when((p == 0) & (m == 0))
        def _():
            st_ref[...] = jnp.zeros_like(st_ref)

        acc_ref[...] += _dot_nt(dg_ref[...], wg_ref[...]) + _dot_nt(du_ref[...], wu_ref[...])

        @pl.when(p == NDEV - 1)
        def _():
            dx, dnw = _rms_bwd_tile(acc_ref[...], x_ref[...], r_ref[...], nw_ref[...])
            dx = dres_ref[...] + dx
            dx_ref[...] = dx
            dxb_ref[...] = dx.astype(BF16)
            st_ref[0:1, :] += dnw

    blk = pl.BlockSpec((None, tm, N_FF), lambda m, p: (p, m, 0))
    wblk = pl.BlockSpec((None, D, N_FF), lambda m, p: (p, 0, 0))
    row = pl.BlockSpec((tm, D), lambda m, p: (m, 0))
    return pl.pallas_call(
        body, name="ffn_up_bwd", grid=(S // tm, NDEV),
        in_specs=[blk, blk, wblk, wblk, row, row, pl.BlockSpec((tm, 1), lambda m, p: (m, 0)),
                  pl.BlockSpec((1, D), lambda m, p: (0, 0))],
        out_specs=[row, row, pl.BlockSpec((8, D), lambda m, p: (0, 0))],
        out_shape=[jax.ShapeDtypeStruct((S, D), F32), jax.ShapeDtypeStruct((S, D), BF16),
                   jax.ShapeDtypeStruct((8, D), F32)],
        scratch_shapes=[pltpu.VMEM((tm, D), F32)],
        compiler_params=_cp(("arbitrary", "arbitrary")),
    )(dg, du, wg, wu, dres, xs, r, nw)


def _out_proj_bwd(dx2b, wout):
    tm = 256

    def body(dx_ref, w_ref, o_ref):
        o_ref[...] = _dot_nt(dx_ref[...], w_ref[...])

    return pl.pallas_call(
        body, name="out_proj_bwd", grid=(S // tm,),
        in_specs=[pl.BlockSpec((tm, D), lambda i: (i, 0)), pl.BlockSpec((D, D), lambda i: (0, 0))],
        out_specs=pl.BlockSpec((tm, D), lambda i: (i, 0)),
        out_shape=jax.ShapeDtypeStruct((S, D), F32),
        compiler_params=_cp(("parallel",)),
    )(dx2b, wout)


def _in_proj_bwd(dproj, win, dres, xs, r, nw):
    tm = 512

    def body(dp_ref, w_ref, dres_ref, x_ref, r_ref, nw_ref, dx_ref, st_ref, acc_ref):
        m, p = pl.program_id(0), pl.program_id(1)

        @pl.when(p == 0)
        def _():
            acc_ref[...] = jnp.zeros_like(acc_ref)

        @pl.when((p == 0) & (m == 0))
        def _():
            st_ref[...] = jnp.zeros_like(st_ref)

        acc_ref[...] += _dot_nt(dp_ref[...], w_ref[...])

        @pl.when(p == NDEV - 1)
        def _():
            dx, dnw = _rms_bwd_tile(acc_ref[...], x_ref[...], r_ref[...], nw_ref[...])
            dx_ref[...] = dres_ref[...] + dx
            st_ref[0:1, :] += dnw

    row = pl.BlockSpec((tm, D), lambda m, p: (m, 0))
    return pl.pallas_call(
        body, name="in_proj_bwd", grid=(S // tm, NDEV),
        in_specs=[pl.BlockSpec((tm, N_IN), lambda m, p: (m, p)),
                  pl.BlockSpec((None, D, N_IN), lambda m, p: (p, 0, 0)),
                  row, row, pl.BlockSpec((tm, 1), lambda m, p: (m, 0)),
                  pl.BlockSpec((1, D), lambda m, p: (0, 0))],
        out_specs=[row, pl.BlockSpec((8, D), lambda m, p: (0, 0))],
        out_shape=[jax.ShapeDtypeStruct((S, D), F32), jax.ShapeDtypeStruct((8, D), F32)],
        scratch_shapes=[pltpu.VMEM((tm, D), F32)],
        compiler_params=_cp(("arbitrary", "arbitrary")),
    )(dproj, win, dres, xs, r, nw)


def _wgrad_cols(act, dy3, name):
    n = dy3.shape[-1]

    def body(a_ref, d_ref, o_ref):
        o_ref[...] = _dot_tn(a_ref[...], d_ref[...]).astype(BF16)

    return pl.pallas_call(
        body, name=name, grid=(NDEV,),
        in_specs=[pl.BlockSpec((S, D), lambda p: (0, 0)), pl.BlockSpec((None, S, n), lambda p: (p, 0, 0))],
        out_specs=pl.BlockSpec((None, D, n), lambda p: (p, 0, 0)),
        out_shape=jax.ShapeDtypeStruct((NDEV, D, n), BF16),
        compiler_params=_cp(("parallel",)),
    )(act, dy3)


def _wgrad_in(h1, dproj):
    def body(a_ref, d_ref, o_ref):
        o_ref[...] = _dot_tn(a_ref[...], d_ref[...]).astype(BF16)

    return pl.pallas_call(
        body, name="wgrad_in", grid=(NDEV,),
        in_specs=[pl.BlockSpec((S, D), lambda p: (0, 0)), pl.BlockSpec((S, N_IN), lambda p: (0, p))],
        out_specs=pl.BlockSpec((None, D, N_IN), lambda p: (p, 0, 0)),
        out_shape=jax.ShapeDtypeStruct((NDEV, D, N_IN), BF16),
        compiler_params=_cp(("parallel",)),
    )(h1, dproj)


def _wgrad_rows(a3, dy):
    def body(a_ref, d_ref, o_ref):
        o_ref[...] = _dot_tn(a_ref[...], d_ref[...]).astype(BF16)

    return pl.pallas_call(
        body, name="wgrad_down", grid=(NDEV,),
        in_specs=[pl.BlockSpec((None, S, N_FF), lambda p: (p, 0, 0)), pl.BlockSpec((S, D), lambda p: (0, 0))],
        out_specs=pl.BlockSpec((None, N_FF, D), lambda p: (p, 0, 0)),
        out_shape=jax.ShapeDtypeStruct((NDEV, N_FF, D), BF16),
        compiler_params=_cp(("parallel",)),
    )(a3, dy)


def _wgrad_out(ma, mr, dx2b):
    half = D // 2
    per = half // N_OUT

    def body(ma_ref, mr_ref, d_ref, o_ref):
        p = pl.program_id(0)

        @pl.when(p < per)
        def _():
            o_ref[...] = _dot_tn(ma_ref[...], d_ref[...]).astype(BF16)

        @pl.when(p >= per)
        def _():
            o_ref[...] = _dot_tn(mr_ref[...], d_ref[...]).astype(BF16)

    return pl.pallas_call(
        body, name="wgrad_out", grid=(NDEV,),
        in_specs=[pl.BlockSpec((S, N_OUT), lambda p: (0, jnp.minimum(p, per - 1))),
                  pl.BlockSpec((S, N_OUT), lambda p: (0, jnp.maximum(p - per, 0))),
                  pl.BlockSpec((S, D), lambda p: (0, 0))],
        out_specs=pl.BlockSpec((None, N_OUT, D), lambda p: (p, 0, 0)),
        out_shape=jax.ShapeDtypeStruct((NDEV, N_OUT, D), BF16),
        compiler_params=_cp(("parallel",)),
    )(ma, mr, dx2b)


def _attn_consts():
    c = np.zeros((AH, 8, AHD), np.float32)
    for h in range(AH):
        c[h, :, :] = 2.0 ** (-(h + 1))
    return jnp.asarray(c)


def _permute_in(dst, src, d, cast=None):
    ln = S // d
    for rr in range(d):
        v = src[pl.ds(rr, ln, stride=d), :] if d > 1 else src[...]
        dst[rr * ln:(rr + 1) * ln, :] = v if cast is None else v.astype(cast)


def _attn_masks():
    qi = lax.broadcasted_iota(jnp.int32, (CH, CH), 0)
    kj = lax.broadcasted_iota(jnp.int32, (CH, CH), 1)
    dist_c = (qi - kj).astype(F32)
    dist_p = (qi - kj + CH).astype(F32)
    return qi >= kj, kj >= qi, dist_c, dist_p


def _attn_fwd(proj):
    scale = 1.0 / math.sqrt(AHD)

    def body(c_ref, q_ref, k_ref, v_ref, o_ref, ob_ref, lse_ref, qd, kd, vd, od, ld, *nat):
        onat, lnat = nat[0:3], nat[3:6]
        slope = c_ref[0:1, :]
        mask_c, mask_p, dist_c, dist_p = _attn_masks()
        for pi, (d, nb) in enumerate(PATTERNS):
            _permute_in(qd, q_ref, d, BF16)
            _permute_in(kd, k_ref, d, BF16)
            _permute_in(vd, v_ref, d, BF16)
            bias_c = -(slope * float(d)) * dist_c
            bias_p = -(slope * float(d)) * dist_p

            def blk(b, carry, nb=nb, bias_c=bias_c, bias_p=bias_p):
                st = pl.multiple_of(b * CH, CH)
                qb = qd[pl.ds(st, CH), :]
                kc = kd[pl.ds(st, CH), :]
                vc = vd[pl.ds(st, CH), :]
                s_c = jnp.where(mask_c, _dot_nt(qb, kc) * scale + bias_c, NEG)
                mx = jnp.max(s_c, axis=-1, keepdims=True)
                if nb > 1:
                    pst = pl.multiple_of(jnp.maximum(b - 1, 0) * CH, CH)
                    kp = kd[pl.ds(pst, CH), :]
                    vp = vd[pl.ds(pst, CH), :]
                    has_prev = (b % nb) != 0
                    s_p = jnp.where(jnp.logical_and(mask_p, has_prev), _dot_nt(qb, kp) * scale + bias_p, NEG)
                    mx = jnp.maximum(mx, jnp.max(s_p, axis=-1, keepdims=True))
                    l = (jnp.sum(jnp.exp(s_c - mx), axis=-1, keepdims=True)
                         + jnp.sum(jnp.exp(s_p - mx), axis=-1, keepdims=True))
                    lse = mx + jnp.log(l)
                    o = _dot(jnp.exp(s_c - lse).astype(BF16), vc) + _dot(jnp.exp(s_p - lse).astype(BF16), vp)
                else:
                    l = jnp.sum(jnp.exp(s_c - mx), axis=-1, keepdims=True)
                    lse = mx + jnp.log(l)
                    o = _dot(jnp.exp(s_c - lse).astype(BF16), vc)
                od[pl.ds(st, CH), :] = o
                ld[pl.ds(st, CH), :] = jnp.broadcast_to(lse, (CH, AHD))
                return carry

            lax.fori_loop(0, NB, blk, 0)
            ln = S // d
            for rr in range(d):
                if d > 1:
                    onat[pi][pl.ds(rr, ln, stride=d), :] = od[rr * ln:(rr + 1) * ln, :]
                    lnat[pi][pl.ds(rr, ln, stride=d), :] = ld[rr * ln:(rr + 1) * ln, :]
                else:
                    onat[pi][...] = od[...]
                    lnat[pi][...] = ld[...]
        l0, l1, l2 = lnat[0][...], lnat[1][...], lnat[2][...]
        mx = jnp.maximum(jnp.maximum(l0, l1), l2)
        e0, e1, e2 = jnp.exp(l0 - mx), jnp.exp(l1 - mx), jnp.exp(l2 - mx)
        den = e0 + e1 + e2
        out = (e0 / den) * onat[0][...] + (e1 / den) * onat[1][...] + (e2 / den) * onat[2][...]
        o_ref[...] = out
        ob_ref[...] = out.astype(BF16)
        lse_ref[...] = mx + jnp.log(den)

    def col(off):
        return pl.BlockSpec((S, AHD), lambda h: (0, off + h))

    return pl.pallas_call(
        body, name="attn_fwd", grid=(AH,),
        in_specs=[pl.BlockSpec((None, 8, AHD), lambda h: (h, 0, 0)), col(0), col(AH), col(2 * AH)],
        out_specs=[col(0), col(0), col(0)],
        out_shape=[jax.ShapeDtypeStruct((S, AH * AHD), F32), jax.ShapeDtypeStruct((S, AH * AHD), BF16),
                   jax.ShapeDtypeStruct((S, AH * AHD), F32)],
        scratch_shapes=[pltpu.VMEM((S, AHD), BF16), pltpu.VMEM((S, AHD), BF16), pltpu.VMEM((S, AHD), BF16),
                        pltpu.VMEM((S, AHD), F32), pltpu.VMEM((S, AHD), F32)]
        + [pltpu.VMEM((S, AHD), F32) for _ in range(6)],
        compiler_params=_cp(("parallel",)),
    )(_attn_consts(), proj, proj, proj)


def _attn_bwd(proj, dmixed, o, lse):
    scale = 1.0 / math.sqrt(AHD)

    def body(c_ref, q_ref, k_ref, v_ref, do_ref, o_ref, lse_ref, dq_ref, dk_ref, dv_ref,
             qd, kd, vd, dod, lsd, dld, dqd, dkd, dvd, delta, aq, ak, av):
        slope = c_ref[0:1, :]
        mask_c, mask_p, dist_c, dist_p = _attn_masks()
        delta[...] = jnp.broadcast_to(jnp.sum(do_ref[...] * o_ref[...], axis=-1, keepdims=True), (S, AHD))
        for pi, (d, nb) in enumerate(PATTERNS):
            _permute_in(qd, q_ref, d, BF16)
            _permute_in(kd, k_ref, d, BF16)
            _permute_in(vd, v_ref, d, BF16)
            _permute_in(dod, do_ref, d, BF16)
            _permute_in(lsd, lse_ref, d)
            _permute_in(dld, delta, d)
            dkd[...] = jnp.zeros_like(dkd)
            dvd[...] = jnp.zeros_like(dvd)
            bias_c = -(slope * float(d)) * dist_c
            bias_p = -(slope * float(d)) * dist_p

            def blk(b, carry, nb=nb, bias_c=bias_c, bias_p=bias_p):
                st = pl.multiple_of(b * CH, CH)
                cur = pl.ds(st, CH)
                qb, kc, vc, dob = qd[cur, :], kd[cur, :], vd[cur, :], dod[cur, :]
                ls, dl = lsd[cur, :], dld[cur, :]
                p_c = jnp.exp(jnp.where(mask_c, _dot_nt(qb, kc) * scale + bias_c, NEG) - ls)
                ds_c = ((p_c * (_dot_nt(dob, vc) - dl)) * scale).astype(BF16)
                dq = _dot(ds_c, kc)
                dkd[cur, :] += _dot_tn(ds_c, qb)
                dvd[cur, :] += _dot_tn(p_c.astype(BF16), dob)
                if nb > 1:
                    prev = pl.ds(pl.multiple_of(jnp.maximum(b - 1, 0) * CH, CH), CH)
                    kp, vp = kd[prev, :], vd[prev, :]
                    has_prev = (b % nb) != 0
                    p_p = jnp.exp(jnp.where(jnp.logical_and(mask_p, has_prev),
                                            _dot_nt(qb, kp) * scale + bias_p, NEG) - ls)
                    ds_p = ((p_p * (_dot_nt(dob, vp) - dl)) * scale).astype(BF16)
                    dq = dq + _dot(ds_p, kp)
                    dkd[prev, :] += _dot_tn(ds_p, qb)
                    dvd[prev, :] += _dot_tn(p_p.astype(BF16), dob)
                dqd[cur, :] = dq
                return carry

            lax.fori_loop(0, NB, blk, 0)
            ln = S // d
            for acc, src in ((aq, dqd), (ak, dkd), (av, dvd)):
                if pi == 0:
                    acc[...] = src[...]
                else:
                    for rr in range(d):
                        acc[pl.ds(rr, ln, stride=d), :] += src[rr * ln:(rr + 1) * ln, :]
        dq_ref[...] = aq[...].astype(BF16)
        dk_ref[...] = ak[...].astype(BF16)
        dv_ref[...] = av[...].astype(BF16)

    def col(off):
        return pl.BlockSpec((S, AHD), lambda h: (0, off + h))

    bf = lambda: pltpu.VMEM((S, AHD), BF16)
    f3 = lambda: pltpu.VMEM((S, AHD), F32)
    return pl.pallas_call(
        body, name="attn_bwd", grid=(AH,),
        in_specs=[pl.BlockSpec((None, 8, AHD), lambda h: (h, 0, 0)), col(0), col(AH), col(2 * AH),
                  col(0), col(0), col(0)],
        out_specs=[col(0), col(0), col(0)],
        out_shape=[jax.ShapeDtypeStruct((S, AH * AHD), BF16)] * 3,
        scratch_shapes=[bf(), bf(), bf(), bf(), f3(), f3(), f3(), f3(), f3(), f3(), f3(), f3(), f3()],
        compiler_params=_cp(("parallel",)),
    )(_attn_consts(), proj, proj, proj, dmixed, o, lse)


def _ret_consts():
    c = np.zeros((RH, 8, RHD), np.float32)
    for h in range(RH):
        c[h, :, :] = np.log(np.float32(1.0) - np.float32(2.0 ** (-5.0 - h)))
    return jnp.asarray(c)


def _ret_factors(lg):
    i = lax.broadcasted_iota(jnp.int32, (CH, CH), 0)
    j = lax.broadcasted_iota(jnp.int32, (CH, CH), 1)
    dif = (i - j).astype(F32)
    decay = jnp.where(dif >= 0, jnp.exp(lg[:, 0:CH] * jnp.maximum(dif, 0.0)), 0.0)
    row = lax.broadcasted_iota(jnp.int32, (CH, RHD), 0).astype(F32)
    zeta = jnp.exp(lg * (CH - 1.0 - row))
    xi = jnp.exp(lg * (row + 1.0))
    return decay, zeta, xi, jnp.exp(lg * float(CH))


def _ret_specs(rev):
    off = 3 * AH * AHD // RHD

    def ch(n):
        return (NB - 1 - n) if rev else n

    def col(k):
        return pl.BlockSpec((CH, RHD), lambda h, n: (ch(n), off + k * RH + h))

    own = pl.BlockSpec((CH, RHD), lambda h, n: (ch(n), h))
    state = pl.BlockSpec((None, None, RHD, RHD), lambda h, n: (h, ch(n), 0, 0))
    const = pl.BlockSpec((None, 8, RHD), lambda h, n: (h, 0, 0))
    return col, own, state, const


def _ret_fwd(proj):
    def body(c_ref, q_ref, k_ref, v_ref, g_ref, ret_ref, mr_ref, st_ref, r_acc):
        n = pl.program_id(1)

        @pl.when(n == 0)
        def _():
            r_acc[...] = jnp.zeros_like(r_acc)

        decay, zeta, xi, gch = _ret_factors(c_ref[0:1, :])
        qb = q_ref[...].astype(BF16)
        kc = k_ref[...] * (1.0 / math.sqrt(RHD))
        kb = kc.astype(BF16)
        vb = v_ref[...].astype(BF16)
        rb = r_acc[...].astype(BF16)
        st_ref[...] = rb
        scores = _dot_nt(qb, kb) * decay
        ret = _dot(scores.astype(BF16), vb) + _dot(qb, rb) * xi
        r_acc[...] = r_acc[...] * gch + _dot_tn((kc * zeta).astype(BF16), vb)
        ret_ref[...] = ret
        rr = lax.rsqrt(jnp.mean(ret * ret, axis=-1, keepdims=True) + EPS)
        gv = g_ref[...]
        mr_ref[...] = ((gv * _sigmoid(gv)) * (ret * rr)).astype(BF16)

    col, own, state, const = _ret_specs(False)
    return pl.pallas_call(
        body, name="ret_fwd", grid=(RH, NB),
        in_specs=[const, col(0), col(1), col(2), col(3)],
        out_specs=[own, own, state],
        out_shape=[jax.ShapeDtypeStruct((S, RH * RHD), F32), jax.ShapeDtypeStruct((S, RH * RHD), BF16),
                   jax.ShapeDtypeStruct((RH, NB, RHD, RHD), BF16)],
        scratch_shapes=[pltpu.VMEM((RHD, RHD), F32)],
        compiler_params=_cp(("parallel", "arbitrary")),
    )(_ret_consts(), proj, proj, proj, proj)


def _ret_bwd(proj, ret, states, dmixed):
    def body(c_ref, q_ref, k_ref, v_ref, g_ref, ret_ref, st_ref, dm_ref, dq_ref, dk_ref, dv_ref, dg_ref, g_acc):
        n = pl.program_id(1)

        @pl.when(n == 0)
        def _():
            g_acc[...] = jnp.zeros_like(g_acc)

        decay, zeta, xi, gch = _ret_factors(c_ref[0:1, :])
        ret_v = ret_ref[...]
        rr = lax.rsqrt(jnp.mean(ret_v * ret_v, axis=-1, keepdims=True) + EPS)
        gv = g_ref[...]
        sg = _sigmoid(gv)
        dmix = dm_ref[...]
        dg_ref[...] = ((dmix * (ret_v * rr)) * (sg * (1.0 + gv * (1.0 - sg)))).astype(BF16)
        dretn = dmix * (gv * sg)
        dret = rr * dretn - ret_v * ((rr * rr * rr) * jnp.mean(dretn * ret_v, axis=-1, keepdims=True))

        qb = q_ref[...].astype(BF16)
        kc = k_ref[...] * (1.0 / math.sqrt(RHD))
        kb = kc.astype(BF16)
        vb = v_ref[...].astype(BF16)
        rb = st_ref[...]
        db = dret.astype(BF16)
        sc = (_dot_nt(qb, kb) * decay).astype(BF16)
        da = (_dot_nt(db, vb) * decay).astype(BF16)
        dxi = (dret * xi).astype(BF16)
        gb = g_acc[...].astype(BF16)
        kz = (kc * zeta).astype(BF16)
        dq = _dot(da, kb) + _dot_nt(dxi, rb)
        dkc = _dot_tn(da, qb) + _dot_nt(vb, gb) * zeta
        dv = _dot_tn(sc, db) + _dot(kz, gb)
        g_acc[...] = _dot_tn(qb, dxi) + gch * g_acc[...]
        dq_ref[...] = dq.astype(BF16)
        dk_ref[...] = (dkc * (1.0 / math.sqrt(RHD))).astype(BF16)
        dv_ref[...] = dv.astype(BF16)

    col, own, state, const = _ret_specs(True)
    dm = pl.BlockSpec((CH, RHD), lambda h, n: (NB - 1 - n, AH * AHD // RHD + h))
    return pl.pallas_call(
        body, name="ret_bwd", grid=(RH, NB),
        in_specs=[const, col(0), col(1), col(2), col(3), own, state, dm],
        out_specs=[own, own, own, own],
        out_shape=[jax.ShapeDtypeStruct((S, RH * RHD), BF16)] * 4,
        scratch_shapes=[pltpu.VMEM((RHD, RHD), F32)],
        compiler_params=_cp(("parallel", "arbitrary")),
    )(_ret_consts(), proj, proj, proj, proj, ret, states, dmixed)


def _local_step(x, tgt, nw1, nw2, nw3, win, wout, wg, wu, wd):
    h1, r1 = _rms_fwd(x, nw1)
    proj = _proj(h1, win)
    o, ma, lse = _attn_fwd(proj)
    ret, mr, states = _ret_fwd(proj)
    x2, h2, r2 = _out_proj_rms(x, ma, mr, wout, nw2)
    g, u, a = _ffn_up(h2, wg, wu)
    dx3, dx3b, st3 = _ffn_down_loss(x2, a, wd, nw3, tgt)

    dwd = _wgrad_rows(a, dx3b)
    dg, du = _ffn_down_bwd(dx3b, wd, g, u)
    dwg = _wgrad_cols(h2, dg, "wgrad_gate")
    dwu = _wgrad_cols(h2, du, "wgrad_up")
    dx2, dx2b, st2 = _ffn_up_bwd(dg, du, wg, wu, dx3, x2, r2, nw2)
    dwo = _wgrad_out(ma, mr, dx2b)
    dmixed = _out_proj_bwd(dx2b, wout)
    dqa, dka, dva = _attn_bwd(proj, dmixed, o, lse)
    dqr, dkr, dvr, dgr = _ret_bwd(proj, ret, states, dmixed)
    dproj = jnp.concatenate([dqa, dka, dva, dqr, dkr, dvr, dgr], axis=1)
    dwi = _wgrad_in(h1, dproj)
    gx, st1 = _in_proj_bwd(dproj, win, dx2, x, r1, nw1)
    stats = jnp.concatenate([st1[0:1], st2[0:1], st3[0:2], jnp.zeros((4, D), F32)], axis=0)
    return stats, gx, dwi, dwo, dwg, dwu, dwd


def _place():
    x, y, c = lax.axis_index("x"), lax.axis_index("y"), lax.axis_index("c")
    return x, y, c, [(1 - x, y), (x, 1 - y), (1 - x, 1 - y)]


def _handshake(peers):
    barrier = pltpu.get_barrier_semaphore()
    for peer in peers:
        pl.semaphore_signal(barrier, inc=1, device_id=peer, device_id_type=MESH)
    pl.semaphore_wait(barrier, len(peers))


def _all_gather(shards, name, collective_id):
    na = len(shards)

    def body(*refs):
        ins, outs = refs[:na], refs[na:2 * na]
        send_sems, recv_sems, local_sems = refs[2 * na:]
        x, y, c, chips = _place()
        sib = (x, y, 1 - c)
        _handshake([sib] + [(*chip, c) for chip in chips])

        def copy(a, k, block, to, src=None):
            idx = 4 * block[0] + 2 * block[1] + block[2]
            return pltpu.make_async_remote_copy(
                src_ref=outs[a].at[idx] if src is None else src, dst_ref=outs[a].at[idx],
                send_sem=send_sems.at[a, k], recv_sem=recv_sems.at[a, k], device_id=to, device_id_type=MESH)

        me = (x, y, c)
        mine = [pltpu.make_async_copy(ins[a], outs[a].at[4 * x + 2 * y + c], local_sems.at[a]) for a in range(na)]
        for cp in mine:
            cp.start()
        first = []
        for a in range(na):
            first += [copy(a, 1 + j, me, (*chip, c), src=ins[a]) for j, chip in enumerate(chips)]
        for a in range(na):
            first.append(copy(a, 0, me, sib, src=ins[a]))
        for cp in first:
            cp.start()
        passed = []
        for a in range(na):
            for j, chip in enumerate(chips):
                copy(a, 1 + j, (*chip, c), me).wait_recv()
                fw = copy(a, 4 + j, (*chip, c), sib)
                fw.start()
                passed.append(fw)
        for a in range(na):
            copy(a, 0, (x, y, 1 - c), me).wait_recv()
            for j, chip in enumerate(chips):
                copy(a, 4 + j, (*chip, 1 - c), me).wait_recv()
        for cp in first + passed:
            cp.wait_send()
        for cp in mine:
            cp.wait()

    return pl.kernel(
        body, name=name,
        out_type=[jax.ShapeDtypeStruct((NDEV,) + s.shape, s.dtype) for s in shards],
        mesh=plsc.ScalarSubcoreMesh(axis_name="sequencer", num_cores=1),
        scratch_types=[pltpu.SemaphoreType.DMA((na, 7)), pltpu.SemaphoreType.DMA((na, 7)),
                       pltpu.SemaphoreType.DMA((na,))],
        compiler_params=pltpu.CompilerParams(collective_id=collective_id),
    )(*shards)


def _exchange_sibling(grads):
    na = len(grads)

    def body(*refs):
        ins, outs = refs[:na], refs[na:2 * na]
        send_sems, recv_sems = refs[2 * na:]
        x, y, c, _ = _place()
        cps = []
        for a in range(na):
            for k in range(4):
                cps.append(pltpu.make_async_remote_copy(
                    src_ref=ins[a].at[2 * k + (1 - c)], dst_ref=outs[a].at[k],
                    send_sem=send_sems.at[a, k], recv_sem=recv_sems.at[a, k],
                    device_id=(x, y, 1 - c), device_id_type=MESH))
        for cp in cps:
            cp.start()
        for cp in cps:
            cp.wait()

    any_spec = pl.BlockSpec(memory_space=pl.ANY)
    return pl.pallas_call(
        body, name="reduce_sibling_exchange",
        in_specs=[any_spec] * na, out_specs=[any_spec] * na,
        out_shape=[jax.ShapeDtypeStruct((4,) + g.shape[1:], g.dtype) for g in grads],
        scratch_shapes=[pltpu.SemaphoreType.DMA((na, 4)), pltpu.SemaphoreType.DMA((na, 4))],
    )(*grads)


def _row_tile(rows, cols):
    for t in (512, 256, 176, 128, 64, 32, 16):
        if rows % t == 0 and t * cols * 4 <= (1 << 20):
            return t
    raise ValueError((rows, cols))


def _chip_sum(place, g, got, name):
    _, r, c = g.shape
    tm = _row_tile(r, c)

    def body(pos_ref, g_ref, got_ref, o_ref):
        o_ref[...] = (g_ref[...].astype(F32) + got_ref[...].astype(F32)).astype(BF16)

    return pl.pallas_call(
        body, name=name,
        grid_spec=pltpu.PrefetchScalarGridSpec(
            num_scalar_prefetch=1, grid=(4, r // tm),
            in_specs=[pl.BlockSpec((None, tm, c), lambda k, i, pos: (2 * k + pos[2], i, 0)),
                      pl.BlockSpec((None, tm, c), lambda k, i, pos: (k, i, 0))],
            out_specs=pl.BlockSpec((None, tm, c), lambda k, i, pos: (k, i, 0))),
        out_shape=jax.ShapeDtypeStruct((4, r, c), BF16),
        compiler_params=_cp(("parallel", "parallel")),
    )(place, g, got)


def _exchange_chips(sums, stats):
    na = len(sums)

    def body(*refs):
        ins, st_in = refs[:na], refs[na]
        outs, st_out = refs[na + 1:2 * na + 1], refs[2 * na + 1]
        send_sems, recv_sems, st_send, st_recv, local_sem = refs[2 * na + 2:]
        x, y, c, chips = _place()
        me_idx = 4 * x + 2 * y + c
        mine = pltpu.make_async_copy(st_in, st_out.at[me_idx], local_sem)
        mine.start()
        cps = []
        for a in range(na):
            for j, chip in enumerate(chips):
                cps.append(pltpu.make_async_remote_copy(
                    src_ref=ins[a].at[2 * chip[0] + chip[1]], dst_ref=outs[a].at[j],
                    send_sem=send_sems.at[a, j], recv_sem=recv_sems.at[a, j],
                    device_id=(*chip, c), device_id_type=MESH))
        for k in range(1, 8):
            fx, fy, fc = (k >> 2) & 1, (k >> 1) & 1, k & 1
            peer = (x ^ fx, y ^ fy, c ^ fc)
            cps.append(pltpu.make_async_remote_copy(
                src_ref=st_in, dst_ref=st_out.at[me_idx],
                send_sem=st_send.at[k - 1], recv_sem=st_recv.at[k - 1], device_id=peer, device_id_type=MESH))
        for cp in cps:
            cp.start()
        for cp in cps:
            cp.wait()
        mine.wait()

    any_spec = pl.BlockSpec(memory_space=pl.ANY)
    return pl.pallas_call(
        body, name="reduce_chip_exchange",
        in_specs=[any_spec] * (na + 1), out_specs=[any_spec] * (na + 1),
        out_shape=[jax.ShapeDtypeStruct((3,) + s.shape[1:], s.dtype) for s in sums]
        + [jax.ShapeDtypeStruct((NDEV,) + stats.shape, stats.dtype)],
        scratch_shapes=[pltpu.SemaphoreType.DMA((na, 3)), pltpu.SemaphoreType.DMA((na, 3)),
                        pltpu.SemaphoreType.DMA((7,)), pltpu.SemaphoreType.DMA((7,)), pltpu.SemaphoreType.DMA],
    )(*sums, stats)


def _adamw(w, g, m, v):
    m = ADAM_B1 * m + (1.0 - ADAM_B1) * g
    v = ADAM_B2 * v + (1.0 - ADAM_B2) * (g * g)
    m_hat = m / (1.0 - ADAM_B1 ** ADAM_STEP)
    v_hat = v / (1.0 - ADAM_B2 ** ADAM_STEP)
    delta = -ADAM_LR * (m_hat / (jnp.sqrt(v_hat) + ADAM_EPS) + ADAM_WD * w)
    return delta, m, v


def _shard_update(place, w, m, v, g, got_sib, got_chips, name):
    r, c = w.shape
    tm = _row_tile(r, c)

    def body(pos_ref, w_ref, m_ref, v_ref, g_ref, s_ref, c_ref, go_ref, d_ref, mo_ref, vo_ref):
        grad = g_ref[...].astype(F32) + s_ref[...].astype(F32)
        for j in range(3):
            grad = grad + c_ref[j].astype(F32)
        delta, mn, vn = _adamw(w_ref[...], grad, m_ref[...], v_ref[...])
        go_ref[...] = grad
        d_ref[...] = delta
        mo_ref[...] = mn
        vo_ref[...] = vn

    row = pl.BlockSpec((tm, c), lambda i, pos: (i, 0))
    return pl.pallas_call(
        body, name=name,
        grid_spec=pltpu.PrefetchScalarGridSpec(
            num_scalar_prefetch=1, grid=(r // tm,),
            in_specs=[row, row, row,
                      pl.BlockSpec((None, tm, c), lambda i, pos: (4 * pos[0] + 2 * pos[1] + pos[2], i, 0)),
                      pl.BlockSpec((None, tm, c), lambda i, pos: (2 * pos[0] + pos[1], i, 0)),
                      pl.BlockSpec((3, tm, c), lambda i, pos: (0, i, 0))],
            out_specs=[row, row, row, row]),
        out_shape=[jax.ShapeDtypeStruct((r, c), F32)] * 4,
        compiler_params=_cp(("parallel",)),
    )(place, w, m, v, g, got_sib, got_chips)


def _small_update(stats_all, ws, ms, vs):
    def body(st_ref, w_ref, m_ref, v_ref, go_ref, d_ref, mo_ref, vo_ref):
        grad = st_ref[0]
        for k in range(1, NDEV):
            grad = grad + st_ref[k]
        delta, mn, vn = _adamw(w_ref[...], grad, m_ref[...], v_ref[...])
        go_ref[...] = grad
        d_ref[...] = delta
        mo_ref[...] = mn
        vo_ref[...] = vn

    return pl.pallas_call(
        body, name="small_update",
        out_shape=[jax.ShapeDtypeStruct((8, D), F32)] * 4,
        compiler_params=_cp(),
    )(stats_all, ws, ms, vs)


def kernel(x, norm_mix_w, w_in, w_out, norm_ffn_w, w_gate, w_up, w_down, norm_final_w, loss_target, m_norm_mix_w, m_w_in, m_w_out, m_norm_ffn_w, m_w_gate, m_w_up, m_w_down, m_norm_final_w, v_norm_mix_w, v_w_in, v_w_out, v_norm_ffn_w, v_w_gate, v_w_up, v_w_down, v_norm_final_w):
    big_w = [w_in[0], w_out[0], w_gate[0], w_up[0], w_down[0]]
    big_m = [m_w_in[0], m_w_out[0], m_w_gate[0], m_w_up[0], m_w_down[0]]
    big_v = [v_w_in[0], v_w_out[0], v_w_gate[0], v_w_up[0], v_w_down[0]]
    names = ["w_in", "w_out", "w_gate", "w_up", "w_down"]

    shards = [_cast_bf16(w, "cast_" + n) for w, n in zip(big_w, names)]
    (win,) = _all_gather(shards[0:1], "all_gather_w_in", 1)
    wout, wg, wu, wd = _all_gather(shards[1:], "all_gather_rest", 2)
    nw3 = norm_final_w.reshape(1, D)
    stats, gx, dwi, dwo, dwg, dwu, dwd = _local_step(
        x[0], loss_target[0], norm_mix_w, norm_ffn_w, nw3, win, wout.reshape(D, D), wg, wu, wd)

    place = jnp.stack([lax.axis_index("x"), lax.axis_index("y"), lax.axis_index("c")]).astype(jnp.int32)
    grads = [dwi, dwo, dwg, dwu, dwd]
    got_sib = _exchange_sibling(grads)
    sums = [_chip_sum(place, g, s, "chip_sum_" + n) for g, s, n in zip(grads, got_sib, names)]
    *got_chips, stats_all = _exchange_chips(sums, stats)
    upd = [_shard_update(place, w, m, v, g, s, c, "update_" + n)
           for w, m, v, g, s, c, n in zip(big_w, big_m, big_v, grads, got_sib, got_chips, names)]

    def rows(a, b, c):
        return jnp.concatenate([a.reshape(1, D), b.reshape(1, D), c.reshape(1, D), jnp.zeros((5, D), F32)], axis=0)

    sg, sd, sm, sv = _small_update(stats_all, rows(norm_mix_w, norm_ffn_w, norm_final_w),
                                   rows(m_norm_mix_w, m_norm_ffn_w, m_norm_final_w),
                                   rows(v_norm_mix_w, v_norm_ffn_w, v_norm_final_w) + jnp.concatenate(
                                       [jnp.zeros((3, D), F32), jnp.ones((5, D), F32)], axis=0))
    loss = sg[3, 0]

    def outs(k, small):
        big = [u[k][None] for u in upd]
        return [small[0:1], big[0], big[1], small[1:2], big[2], big[3], big[4], small[2]]

    return (loss, gx[None], *outs(0, sg), *outs(1, sd), *outs(2, sm), *outs(3, sv))
```

```python
import functools
import math

import numpy as np
import jax
import jax.numpy as jnp
from jax import lax
from jax.experimental import pallas as pl
from jax.experimental.pallas import tpu as pltpu
from jax.experimental.pallas import tpu_sc as plsc

F32 = jnp.float32
BF16 = jnp.bfloat16

S = 2048
D = 2048
NDEV = 8
N_IN = 7168 // NDEV
N_FF = 5632 // NDEV
N_OUT = 2048 // NDEV
AH, AHD = 8, 128
RH, RHD = 4, 256
CH = 128
NB = S // CH
EPS = 1e-6
PATTERNS = ((1, 16), (4, 4), (16, 1))
NEG = -1e30
VMEM_LIMIT = 56 * 1024 * 1024

ADAM_LR, ADAM_B1, ADAM_B2, ADAM_EPS, ADAM_WD, ADAM_STEP = 0.001, 0.9, 0.999, 1e-08, 0.01, 10
MESH = pl.DeviceIdType.MESH


def _cp(sem=None):
    return pltpu.CompilerParams(dimension_semantics=sem, vmem_limit_bytes=VMEM_LIMIT)


def _dot(a, b):
    return jnp.dot(a, b, preferred_element_type=F32)


def _dot_nt(a, b):
    return lax.dot_general(a, b, (((1,), (1,)), ((), ())), preferred_element_type=F32)


def _dot_tn(a, b):
    return lax.dot_general(a, b, (((0,), (0,)), ((), ())), preferred_element_type=F32)


def _sigmoid(x):
    return 1.0 / (1.0 + jnp.exp(-x))


def _cast_bf16(w, name):
    r, c = w.shape
    tm = r if r <= 1024 else 512

    def body(w_ref, o_ref):
        o_ref[...] = w_ref[...].astype(BF16)

    return pl.pallas_call(
        body, name=name, grid=(r // tm,),
        in_specs=[pl.BlockSpec((tm, c), lambda i: (i, 0))],
        out_specs=pl.BlockSpec((tm, c), lambda i: (i, 0)),
        out_shape=jax.ShapeDtypeStruct((r, c), BF16),
        compiler_params=_cp(("parallel",)),
    )(w)


def _rms_fwd(x, nw):
    tm = 256

    def body(x_ref, w_ref, h_ref, r_ref):
        xs = x_ref[...]
        r = lax.rsqrt(jnp.mean(xs * xs, axis=-1, keepdims=True) + EPS)
        h_ref[...] = ((xs * r) * w_ref[...]).astype(BF16)
        r_ref[...] = r

    return pl.pallas_call(
        body, name="rms_fwd", grid=(S // tm,),
        in_specs=[pl.BlockSpec((tm, D), lambda i: (i, 0)), pl.BlockSpec((1, D), lambda i: (0, 0))],
        out_specs=[pl.BlockSpec((tm, D), lambda i: (i, 0)), pl.BlockSpec((tm, 1), lambda i: (i, 0))],
        out_shape=[jax.ShapeDtypeStruct((S, D), BF16), jax.ShapeDtypeStruct((S, 1), F32)],
        compiler_params=_cp(("parallel",)),
    )(x, nw)


def _rms_bwd_tile(dh, xs, r, nw):
    dnw = jnp.sum(dh * (xs * r), axis=0, keepdims=True)
    gy = dh * nw
    dx = r * gy - xs * ((r * r * r) * jnp.mean(gy * xs, axis=-1, keepdims=True))
    return dx, dnw


def _proj(h1, win):
    tm = 512

    def body(a_ref, w_ref, o_ref):
        o_ref[...] = _dot(a_ref[...], w_ref[...])

    return pl.pallas_call(
        body, name="proj", grid=(NDEV, S // tm),
        in_specs=[pl.BlockSpec((tm, D), lambda p, m: (m, 0)),
                  pl.BlockSpec((None, D, N_IN), lambda p, m: (p, 0, 0))],
        out_specs=pl.BlockSpec((tm, N_IN), lambda p, m: (m, p)),
        out_shape=jax.ShapeDtypeStruct((S, NDEV * N_IN), F32),
        compiler_params=_cp(("parallel", "parallel")),
    )(h1, win)


def _out_proj_rms(x, ma, mr, wout, nw):
    tm = 256
    half = D // 2

    def body(x_ref, ma_ref, mr_ref, w_ref, nw_ref, x2_ref, h_ref, r_ref):
        acc = _dot(ma_ref[...], w_ref[0:half, :]) + _dot(mr_ref[...], w_ref[half:D, :])
        x2 = x_ref[...] + acc
        r = lax.rsqrt(jnp.mean(x2 * x2, axis=-1, keepdims=True) + EPS)
        x2_ref[...] = x2
        h_ref[...] = ((x2 * r) * nw_ref[...]).astype(BF16)
        r_ref[...] = r

    return pl.pallas_call(
        body, name="out_proj_rms", grid=(S // tm,),
        in_specs=[pl.BlockSpec((tm, D), lambda i: (i, 0)),
                  pl.BlockSpec((tm, half), lambda i: (i, 0)),
                  pl.BlockSpec((tm, half), lambda i: (i, 0)),
                  pl.BlockSpec((D, D), lambda i: (0, 0)),
                  pl.BlockSpec((1, D), lambda i: (0, 0))],
        out_specs=[pl.BlockSpec((tm, D), lambda i: (i, 0)), pl.BlockSpec((tm, D), lambda i: (i, 0)),
                   pl.BlockSpec((tm, 1), lambda i: (i, 0))],
        out_shape=[jax.ShapeDtypeStruct((S, D), F32), jax.ShapeDtypeStruct((S, D), BF16),
                   jax.ShapeDtypeStruct((S, 1), F32)],
        compiler_params=_cp(("parallel",)),
    )(x, ma, mr, wout, nw)


def _ffn_up(h2, wg, wu):
    tm = 512

    def body(h_ref, wg_ref, wu_ref, g_ref, u_ref, a_ref):
        h = h_ref[...]
        g = _dot(h, wg_ref[...])
        u = _dot(h, wu_ref[...])
        g_ref[...] = g
        u_ref[...] = u
        a_ref[...] = ((g * _sigmoid(g)) * u).astype(BF16)

    blk = pl.BlockSpec((None, tm, N_FF), lambda p, m: (p, m, 0))
    wblk = pl.BlockSpec((None, D, N_FF), lambda p, m: (p, 0, 0))
    return pl.pallas_call(
        body, name="ffn_up", grid=(NDEV, S // tm),
        in_specs=[pl.BlockSpec((tm, D), lambda p, m: (m, 0)), wblk, wblk],
        out_specs=[blk, blk, blk],
        out_shape=[jax.ShapeDtypeStruct((NDEV, S, N_FF), F32), jax.ShapeDtypeStruct((NDEV, S, N_FF), F32),
                   jax.ShapeDtypeStruct((NDEV, S, N_FF), BF16)],
        compiler_params=_cp(("parallel", "parallel")),
    )(h2, wg, wu)


def _ffn_down_loss(x2, a, wd, nw, tgt):
    tm = 512

    def body(x2_ref, a_ref, w_ref, nw_ref, t_ref, dx_ref, dxb_ref, st_ref, acc_ref):
        m, p = pl.program_id(0), pl.program_id(1)

        @pl.when(p == 0)
        def _():
            acc_ref[...] = jnp.zeros_like(acc_ref)

        @pl.when((p == 0) & (m == 0))
        def _():
            st_ref[...] = jnp.zeros_like(st_ref)

        acc_ref[...] += _dot(a_ref[...], w_ref[...])

        @pl.when(p == NDEV - 1)
        def _():
            x3 = x2_ref[...] + acc_ref[...]
            nwv = nw_ref[...]
            r = lax.rsqrt(jnp.mean(x3 * x3, axis=-1, keepdims=True) + EPS)
            y = (x3 * r) * nwv
            err = y - t_ref[...]
            loss = 0.5 * jnp.sum(jnp.mean(err * err, axis=-1, keepdims=True), axis=0, keepdims=True)
            dy = err * (1.0 / D)
            dx, dnw = _rms_bwd_tile(dy, x3, r, nwv)
            dx_ref[...] = dx
            dxb_ref[...] = dx.astype(BF16)
            st_ref[0:1, :] += dnw
            st_ref[1:2, :] += jnp.broadcast_to(loss, (1, D))

    return pl.pallas_call(
        body, name="ffn_down_loss", grid=(S // tm, NDEV),
        in_specs=[pl.BlockSpec((tm, D), lambda m, p: (m, 0)),
                  pl.BlockSpec((None, tm, N_FF), lambda m, p: (p, m, 0)),
                  pl.BlockSpec((None, N_FF, D), lambda m, p: (p, 0, 0)),
                  pl.BlockSpec((1, D), lambda m, p: (0, 0)),
                  pl.BlockSpec((tm, D), lambda m, p: (m, 0))],
        out_specs=[pl.BlockSpec((tm, D), lambda m, p: (m, 0)), pl.BlockSpec((tm, D), lambda m, p: (m, 0)),
                   pl.BlockSpec((8, D), lambda m, p: (0, 0))],
        out_shape=[jax.ShapeDtypeStruct((S, D), F32), jax.ShapeDtypeStruct((S, D), BF16),
                   jax.ShapeDtypeStruct((8, D), F32)],
        scratch_shapes=[pltpu.VMEM((tm, D), F32)],
        compiler_params=_cp(("arbitrary", "arbitrary")),
    )(x2, a, wd, nw, tgt)


def _ffn_down_bwd(dx3b, wd, g, u):
    tm = 512

    def body(dx_ref, w_ref, g_ref, u_ref, dg_ref, du_ref):
        da = _dot_nt(dx_ref[...], w_ref[...])
        gv = g_ref[...]
        sg = _sigmoid(gv)
        silu = gv * sg
        dg_ref[...] = ((da * u_ref[...]) * (sg * (1.0 + gv * (1.0 - sg)))).astype(BF16)
        du_ref[...] = (da * silu).astype(BF16)

    blk = pl.BlockSpec((None, tm, N_FF), lambda p, m: (p, m, 0))
    return pl.pallas_call(
        body, name="ffn_down_bwd", grid=(NDEV, S // tm),
        in_specs=[pl.BlockSpec((tm, D), lambda p, m: (m, 0)),
                  pl.BlockSpec((None, N_FF, D), lambda p, m: (p, 0, 0)), blk, blk],
        out_specs=[blk, blk],
        out_shape=[jax.ShapeDtypeStruct((NDEV, S, N_FF), BF16), jax.ShapeDtypeStruct((NDEV, S, N_FF), BF16)],
        compiler_params=_cp(("parallel", "parallel")),
    )(dx3b, wd, g, u)


def _ffn_up_bwd(dg, du, wg, wu, dres, xs, r, nw):
    tm = 512

    def body(dg_ref, du_ref, wg_ref, wu_ref, dres_ref, x_ref, r_ref, nw_ref, dx_ref, dxb_ref, st_ref, acc_ref):
        m, p = pl.program_id(0), pl.program_id(1)

        @pl.when(p == 0)
        def _():
            acc_ref[...] = jnp.zeros_like(acc_ref)

        @pl.when((p == 0) & (m == 0))
        def _():
            st_ref[...] = jnp.zeros_like(st_ref)

        acc_ref[...] += _dot_nt(dg_ref[...], wg_ref[...]) + _dot_nt(du_ref[...], wu_ref[...])

        @pl.when(p == NDEV - 1)
        def _():
            dx, dnw = _rms_bwd_tile(acc_ref[...], x_ref[...], r_ref[...], nw_ref[...])
            dx = dres_ref[...] + dx
            dx_ref[...] = dx
            dxb_ref[...] = dx.astype(BF16)
            st_ref[0:1, :] += dnw

    blk = pl.BlockSpec((None, tm, N_FF), lambda m, p: (p, m, 0))
    wblk = pl.BlockSpec((None, D, N_FF), lambda m, p: (p, 0, 0))
    row = pl.BlockSpec((tm, D), lambda m, p: (m, 0))
    return pl.pallas_call(
        body, name="ffn_up_bwd", grid=(S // tm, NDEV),
        in_specs=[blk, blk, wblk, wblk, row, row, pl.BlockSpec((tm, 1), lambda m, p: (m, 0)),
                  pl.BlockSpec((1, D), lambda m, p: (0, 0))],
        out_specs=[row, row, pl.BlockSpec((8, D), lambda m, p: (0, 0))],
        out_shape=[jax.ShapeDtypeStruct((S, D), F32), jax.ShapeDtypeStruct((S, D), BF16),
                   jax.ShapeDtypeStruct((8, D), F32)],
        scratch_shapes=[pltpu.VMEM((tm, D), F32)],
        compiler_params=_cp(("arbitrary", "arbitrary")),
    )(dg, du, wg, wu, dres, xs, r, nw)


def _out_proj_bwd(dx2b, wout):
    tm = 256

    def body(dx_ref, w_ref, o_ref):
        o_ref[...] = _dot_nt(dx_ref[...], w_ref[...])

    return pl.pallas_call(
        body, name="out_proj_bwd", grid=(S // tm,),
        in_specs=[pl.BlockSpec((tm, D), lambda i: (i, 0)), pl.BlockSpec((D, D), lambda i: (0, 0))],
        out_specs=pl.BlockSpec((tm, D), lambda i: (i, 0)),
        out_shape=jax.ShapeDtypeStruct((S, D), F32),
        compiler_params=_cp(("parallel",)),
    )(dx2b, wout)


def _in_proj_bwd(dproj, win, dres, xs, r, nw):
    tm = 512

    def body(dp_ref, w_ref, dres_ref, x_ref, r_ref, nw_ref, dx_ref, st_ref, acc_ref):
        m, p = pl.program_id(0), pl.program_id(1)

        @pl.when(p == 0)
        def _():
            acc_ref[...] = jnp.zeros_like(acc_ref)

        @pl.when((p == 0) & (m == 0))
        def _():
            st_ref[...] = jnp.zeros_like(st_ref)

        acc_ref[...] += _dot_nt(dp_ref[...], w_ref[...])

        @pl.when(p == NDEV - 1)
        def _():
            dx, dnw = _rms_bwd_tile(acc_ref[...], x_ref[...], r_ref[...], nw_ref[...])
            dx_ref[...] = dres_ref[...] + dx
            st_ref[0:1, :] += dnw

    row = pl.BlockSpec((tm, D), lambda m, p: (m, 0))
    return pl.pallas_call(
        body, name="in_proj_bwd", grid=(S // tm, NDEV),
        in_specs=[pl.BlockSpec((tm, N_IN), lambda m, p: (m, p)),
                  pl.BlockSpec((None, D, N_IN), lambda m, p: (p, 0, 0)),
                  row, row, pl.BlockSpec((tm, 1), lambda m, p: (m, 0)),
                  pl.BlockSpec((1, D), lambda m, p: (0, 0))],
        out_specs=[row, pl.BlockSpec((8, D), lambda m, p: (0, 0))],
        out_shape=[jax.ShapeDtypeStruct((S, D), F32), jax.ShapeDtypeStruct((8, D), F32)],
        scratch_shapes=[pltpu.VMEM((tm, D), F32)],
        compiler_params=_cp(("arbitrary", "arbitrary")),
    )(dproj, win, dres, xs, r, nw)


def _wgrad_cols(act, dy3, name):
    n = dy3.shape[-1]

    def body(a_ref, d_ref, o_ref):
        o_ref[...] = _dot_tn(a_ref[...], d_ref[...]).astype(BF16)

    return pl.pallas_call(
        body, name=name, grid=(NDEV,),
        in_specs=[pl.BlockSpec((S, D), lambda p: (0, 0)), pl.BlockSpec((None, S, n), lambda p: (p, 0, 0))],
        out_specs=pl.BlockSpec((None, D, n), lambda p: (p, 0, 0)),
        out_shape=jax.ShapeDtypeStruct((NDEV, D, n), BF16),
        compiler_params=_cp(("parallel",)),
    )(act, dy3)


def _wgrad_in(h1, dproj):
    def body(a_ref, d_ref, o_ref):
        o_ref[...] = _dot_tn(a_ref[...], d_ref[...]).astype(BF16)

    return pl.pallas_call(
        body, name="wgrad_in", grid=(NDEV,),
        in_specs=[pl.BlockSpec((S, D), lambda p: (0, 0)), pl.BlockSpec((S, N_IN), lambda p: (0, p))],
        out_specs=pl.BlockSpec((None, D, N_IN), lambda p: (p, 0, 0)),
        out_shape=jax.ShapeDtypeStruct((NDEV, D, N_IN), BF16),
        compiler_params=_cp(("parallel",)),
    )(h1, dproj)


def _wgrad_rows(a3, dy):
    def body(a_ref, d_ref, o_ref):
        o_ref[...] = _dot_tn(a_ref[...], d_ref[...]).astype(BF16)

    return pl.pallas_call(
        body, name="wgrad_down", grid=(NDEV,),
        in_specs=[pl.BlockSpec((None, S, N_FF), lambda p: (p, 0, 0)), pl.BlockSpec((S, D), lambda p: (0, 0))],
        out_specs=pl.BlockSpec((None, N_FF, D), lambda p: (p, 0, 0)),
        out_shape=jax.ShapeDtypeStruct((NDEV, N_FF, D), BF16),
        compiler_params=_cp(("parallel",)),
    )(a3, dy)


def _wgrad_out(ma, mr, dx2b):
    half = D // 2
    per = half // N_OUT

    def body(ma_ref, mr_ref, d_ref, o_ref):
        p = pl.program_id(0)

        @pl.when(p < per)
        def _():
            o_ref[...] = _dot_tn(ma_ref[...], d_ref[...]).astype(BF16)

        @pl.when(p >= per)
        def _():
            o_ref[...] = _dot_tn(mr_ref[...], d_ref[...]).astype(BF16)

    return pl.pallas_call(
        body, name="wgrad_out", grid=(NDEV,),
        in_specs=[pl.BlockSpec((S, N_OUT), lambda p: (0, jnp.minimum(p, per - 1))),
                  pl.BlockSpec((S, N_OUT), lambda p: (0, jnp.maximum(p - per, 0))),
                  pl.BlockSpec((S, D), lambda p: (0, 0))],
        out_specs=pl.BlockSpec((None, N_OUT, D), lambda p: (p, 0, 0)),
        out_shape=jax.ShapeDtypeStruct((NDEV, N_OUT, D), BF16),
        compiler_params=_cp(("parallel",)),
    )(ma, mr, dx2b)


def _attn_consts():
    c = np.zeros((AH, 8, AHD), np.float32)
    for h in range(AH):
        c[h, :, :] = 2.0 ** (-(h + 1))
    return jnp.asarray(c)


def _permute_in(dst, src, d, cast=None):
    ln = S // d
    for rr in range(d):
        v = src[pl.ds(rr, ln, stride=d), :] if d > 1 else src[...]
        dst[rr * ln:(rr + 1) * ln, :] = v if cast is None else v.astype(cast)


def _attn_masks():
    qi = lax.broadcasted_iota(jnp.int32, (CH, CH), 0)
    kj = lax.broadcasted_iota(jnp.int32, (CH, CH), 1)
    dist_c = (qi - kj).astype(F32)
    dist_p = (qi - kj + CH).astype(F32)
    return qi >= kj, kj >= qi, dist_c, dist_p


def _attn_fwd(proj):
    scale = 1.0 / math.sqrt(AHD)

    def body(c_ref, q_ref, k_ref, v_ref, o_ref, ob_ref, lse_ref, qd, kd, vd, od, ld, *nat):
        onat, lnat = nat[0:3], nat[3:6]
        slope = c_ref[0:1, :]
        mask_c, mask_p, dist_c, dist_p = _attn_masks()
        for pi, (d, nb) in enumerate(PATTERNS):
            _permute_in(qd, q_ref, d, BF16)
            _permute_in(kd, k_ref, d, BF16)
            _permute_in(vd, v_ref, d, BF16)
            bias_c = -(slope * float(d)) * dist_c
            bias_p = -(slope * float(d)) * dist_p

            def blk(b, carry, nb=nb, bias_c=bias_c, bias_p=bias_p):
                st = pl.multiple_of(b * CH, CH)
                qb = qd[pl.ds(st, CH), :]
                kc = kd[pl.ds(st, CH), :]
                vc = vd[pl.ds(st, CH), :]
                s_c = jnp.where(mask_c, _dot_nt(qb, kc) * scale + bias_c, NEG)
                mx = jnp.max(s_c, axis=-1, keepdims=True)
                if nb > 1:
                    pst = pl.multiple_of(jnp.maximum(b - 1, 0) * CH, CH)
                    kp = kd[pl.ds(pst, CH), :]
                    vp = vd[pl.ds(pst, CH), :]
                    has_prev = (b % nb) != 0
                    s_p = jnp.where(jnp.logical_and(mask_p, has_prev), _dot_nt(qb, kp) * scale + bias_p, NEG)
                    mx = jnp.maximum(mx, jnp.max(s_p, axis=-1, keepdims=True))
                    l = (jnp.sum(jnp.exp(s_c - mx), axis=-1, keepdims=True)
                         + jnp.sum(jnp.exp(s_p - mx), axis=-1, keepdims=True))
                    lse = mx + jnp.log(l)
                    o = _dot(jnp.exp(s_c - lse).astype(BF16), vc) + _dot(jnp.exp(s_p - lse).astype(BF16), vp)
                else:
                    l = jnp.sum(jnp.exp(s_c - mx), axis=-1, keepdims=True)
                    lse = mx + jnp.log(l)
                    o = _dot(jnp.exp(s_c - lse).astype(BF16), vc)
                od[pl.ds(st, CH), :] = o
                ld[pl.ds(st, CH), :] = jnp.broadcast_to(lse, (CH, AHD))
                return carry

            lax.fori_loop(0, NB, blk, 0)
            ln = S // d
            for rr in range(d):
                if d > 1:
                    onat[pi][pl.ds(rr, ln, stride=d), :] = od[rr * ln:(rr + 1) * ln, :]
                    lnat[pi][pl.ds(rr, ln, stride=d), :] = ld[rr * ln:(rr + 1) * ln, :]
                else:
                    onat[pi][...] = od[...]
                    lnat[pi][...] = ld[...]
        l0, l1, l2 = lnat[0][...], lnat[1][...], lnat[2][...]
        mx = jnp.maximum(jnp.maximum(l0, l1), l2)
        e0, e1, e2 = jnp.exp(l0 - mx), jnp.exp(l1 - mx), jnp.exp(l2 - mx)
        den = e0 + e1 + e2
        out = (e0 / den) * onat[0][...] + (e1 / den) * onat[1][...] + (e2 / den) * onat[2][...]
        o_ref[...] = out
        ob_ref[...] = out.astype(BF16)
        lse_ref[...] = mx + jnp.log(den)

    def col(off):
        return pl.BlockSpec((S, AHD), lambda h: (0, off + h))

    return pl.pallas_call(
        body, name="attn_fwd", grid=(AH,),
        in_specs=[pl.BlockSpec((None, 8, AHD), lambda h: (h, 0, 0)), col(0), col(AH), col(2 * AH)],
        out_specs=[col(0), col(0), col(0)],
        out_shape=[jax.ShapeDtypeStruct((S, AH * AHD), F32), jax.ShapeDtypeStruct((S, AH * AHD), BF16),
                   jax.ShapeDtypeStruct((S, AH * AHD), F32)],
        scratch_shapes=[pltpu.VMEM((S, AHD), BF16), pltpu.VMEM((S, AHD), BF16), pltpu.VMEM((S, AHD), BF16),
                        pltpu.VMEM((S, AHD), F32), pltpu.VMEM((S, AHD), F32)]
        + [pltpu.VMEM((S, AHD), F32) for _ in range(6)],
        compiler_params=_cp(("parallel",)),
    )(_attn_consts(), proj, proj, proj)


def _attn_bwd(proj, dmixed, o, lse):
    scale = 1.0 / math.sqrt(AHD)

    def body(c_ref, q_ref, k_ref, v_ref, do_ref, o_ref, lse_ref, dq_ref, dk_ref, dv_ref,
             qd, kd, vd, dod, lsd, dld, dqd, dkd, dvd, delta, aq, ak, av):
        slope = c_ref[0:1, :]
        mask_c, mask_p, dist_c, dist_p = _attn_masks()
        delta[...] = jnp.broadcast_to(jnp.sum(do_ref[...] * o_ref[...], axis=-1, keepdims=True), (S, AHD))
        for pi, (d, nb) in enumerate(PATTERNS):
            _permute_in(qd, q_ref, d, BF16)
            _permute_in(kd, k_ref, d, BF16)
            _permute_in(vd, v_ref, d, BF16)
            _permute_in(dod, do_ref, d, BF16)
            _permute_in(lsd, lse_ref, d)
            _permute_in(dld, delta, d)
            dkd[...] = jnp.zeros_like(dkd)
            dvd[...] = jnp.zeros_like(dvd)
            bias_c = -(slope * float(d)) * dist_c
            bias_p = -(slope * float(d)) * dist_p

            def blk(b, carry, nb=nb, bias_c=bias_c, bias_p=bias_p):
                st = pl.multiple_of(b * CH, CH)
                cur = pl.ds(st, CH)
                qb, kc, vc, dob = qd[cur, :], kd[cur, :], vd[cur, :], dod[cur, :]
                ls, dl = lsd[cur, :], dld[cur, :]
                p_c = jnp.exp(jnp.where(mask_c, _dot_nt(qb, kc) * scale + bias_c, NEG) - ls)
                ds_c = ((p_c * (_dot_nt(dob, vc) - dl)) * scale).astype(BF16)
                dq = _dot(ds_c, kc)
                dkd[cur, :] += _dot_tn(ds_c, qb)
                dvd[cur, :] += _dot_tn(p_c.astype(BF16), dob)
                if nb > 1:
                    prev = pl.ds(pl.multiple_of(jnp.maximum(b - 1, 0) * CH, CH), CH)
                    kp, vp = kd[prev, :], vd[prev, :]
                    has_prev = (b % nb) != 0
                    p_p = jnp.exp(jnp.where(jnp.logical_and(mask_p, has_prev),
                                            _dot_nt(qb, kp) * scale + bias_p, NEG) - ls)
                    ds_p = ((p_p * (_dot_nt(dob, vp) - dl)) * scale).astype(BF16)
                    dq = dq + _dot(ds_p, kp)
                    dkd[prev, :] += _dot_tn(ds_p, qb)
                    dvd[prev, :] += _dot_tn(p_p.astype(BF16), dob)
                dqd[cur, :] = dq
                return carry

            lax.fori_loop(0, NB, blk, 0)
            ln = S // d
            for acc, src in ((aq, dqd), (ak, dkd), (av, dvd)):
                if pi == 0:
                    acc[...] = src[...]
                else:
                    for rr in range(d):
                        acc[pl.ds(rr, ln, stride=d), :] += src[rr * ln:(rr + 1) * ln, :]
        dq_ref[...] = aq[...].astype(BF16)
        dk_ref[...] = ak[...].astype(BF16)
        dv_ref[...] = av[...].astype(BF16)

    def col(off):
        return pl.BlockSpec((S, AHD), lambda h: (0, off + h))

    bf = lambda: pltpu.VMEM((S, AHD), BF16)
    f3 = lambda: pltpu.VMEM((S, AHD), F32)
    return pl.pallas_call(
        body, name="attn_bwd", grid=(AH,),
        in_specs=[pl.BlockSpec((None, 8, AHD), lambda h: (h, 0, 0)), col(0), col(AH), col(2 * AH),
                  col(0), col(0), col(0)],
        out_specs=[col(0), col(0), col(0)],
        out_shape=[jax.ShapeDtypeStruct((S, AH * AHD), BF16)] * 3,
        scratch_shapes=[bf(), bf(), bf(), bf(), f3(), f3(), f3(), f3(), f3(), f3(), f3(), f3(), f3()],
        compiler_params=_cp(("parallel",)),
    )(_attn_consts(), proj, proj, proj, dmixed, o, lse)


def _ret_consts():
    c = np.zeros((RH, 8, RHD), np.float32)
    for h in range(RH):
        c[h, :, :] = np.log(np.float32(1.0) - np.float32(2.0 ** (-5.0 - h)))
    return jnp.asarray(c)


def _ret_factors(lg):
    i = lax.broadcasted_iota(jnp.int32, (CH, CH), 0)
    j = lax.broadcasted_iota(jnp.int32, (CH, CH), 1)
    dif = (i - j).astype(F32)
    decay = jnp.where(dif >= 0, jnp.exp(lg[:, 0:CH] * jnp.maximum(dif, 0.0)), 0.0)
    row = lax.broadcasted_iota(jnp.int32, (CH, RHD), 0).astype(F32)
    zeta = jnp.exp(lg * (CH - 1.0 - row))
    xi = jnp.exp(lg * (row + 1.0))
    return decay, zeta, xi, jnp.exp(lg * float(CH))


def _ret_specs(rev):
    off = 3 * AH * AHD // RHD

    def ch(n):
        return (NB - 1 - n) if rev else n

    def col(k):
        return pl.BlockSpec((CH, RHD), lambda h, n: (ch(n), off + k * RH + h))

    own = pl.BlockSpec((CH, RHD), lambda h, n: (ch(n), h))
    state = pl.BlockSpec((None, None, RHD, RHD), lambda h, n: (h, ch(n), 0, 0))
    const = pl.BlockSpec((None, 8, RHD), lambda h, n: (h, 0, 0))
    return col, own, state, const


def _ret_fwd(proj):
    def body(c_ref, q_ref, k_ref, v_ref, g_ref, ret_ref, mr_ref, st_ref, r_acc):
        n = pl.program_id(1)

        @pl.when(n == 0)
        def _():
            r_acc[...] = jnp.zeros_like(r_acc)

        decay, zeta, xi, gch = _ret_factors(c_ref[0:1, :])
        qb = q_ref[...].astype(BF16)
        kc = k_ref[...] * (1.0 / math.sqrt(RHD))
        kb = kc.astype(BF16)
        vb = v_ref[...].astype(BF16)
        rb = r_acc[...].astype(BF16)
        st_ref[...] = rb
        scores = _dot_nt(qb, kb) * decay
        ret = _dot(scores.astype(BF16), vb) + _dot(qb, rb) * xi
        r_acc[...] = r_acc[...] * gch + _dot_tn((kc * zeta).astype(BF16), vb)
        ret_ref[...] = ret
        rr = lax.rsqrt(jnp.mean(ret * ret, axis=-1, keepdims=True) + EPS)
        gv = g_ref[...]
        mr_ref[...] = ((gv * _sigmoid(gv)) * (ret * rr)).astype(BF16)

    col, own, state, const = _ret_specs(False)
    return pl.pallas_call(
        body, name="ret_fwd", grid=(RH, NB),
        in_specs=[const, col(0), col(1), col(2), col(3)],
        out_specs=[own, own, state],
        out_shape=[jax.ShapeDtypeStruct((S, RH * RHD), F32), jax.ShapeDtypeStruct((S, RH * RHD), BF16),
                   jax.ShapeDtypeStruct((RH, NB, RHD, RHD), BF16)],
        scratch_shapes=[pltpu.VMEM((RHD, RHD), F32)],
        compiler_params=_cp(("parallel", "arbitrary")),
    )(_ret_consts(), proj, proj, proj, proj)


def _ret_bwd(proj, ret, states, dmixed):
    def body(c_ref, q_ref, k_ref, v_ref, g_ref, ret_ref, st_ref, dm_ref, dq_ref, dk_ref, dv_ref, dg_ref, g_acc):
        n = pl.program_id(1)

        @pl.when(n == 0)
        def _():
            g_acc[...] = jnp.zeros_like(g_acc)

        decay, zeta, xi, gch = _ret_factors(c_ref[0:1, :])
        ret_v = ret_ref[...]
        rr = lax.rsqrt(jnp.mean(ret_v * ret_v, axis=-1, keepdims=True) + EPS)
        gv = g_ref[...]
        sg = _sigmoid(gv)
        dmix = dm_ref[...]
        dg_ref[...] = ((dmix * (ret_v * rr)) * (sg * (1.0 + gv * (1.0 - sg)))).astype(BF16)
        dretn = dmix * (gv * sg)
        dret = rr * dretn - ret_v * ((rr * rr * rr) * jnp.mean(dretn * ret_v, axis=-1, keepdims=True))

        qb = q_ref[...].astype(BF16)
        kc = k_ref[...] * (1.0 / math.sqrt(RHD))
        kb = kc.astype(BF16)
        vb = v_ref[...].astype(BF16)
        rb = st_ref[...]
        db = dret.astype(BF16)
        sc = (_dot_nt(qb, kb) * decay).astype(BF16)
        da = (_dot_nt(db, vb) * decay).astype(BF16)
        dxi = (dret * xi).astype(BF16)
        gb = g_acc[...].astype(BF16)
        kz = (kc * zeta).astype(BF16)
        dq = _dot(da, kb) + _dot_nt(dxi, rb)
        dkc = _dot_tn(da, qb) + _dot_nt(vb, gb) * zeta
        dv = _dot_tn(sc, db) + _dot(kz, gb)
        g_acc[...] = _dot_tn(qb, dxi) + gch * g_acc[...]
        dq_ref[...] = dq.astype(BF16)
        dk_ref[...] = (dkc * (1.0 / math.sqrt(RHD))).astype(BF16)
        dv_ref[...] = dv.astype(BF16)

    col, own, state, const = _ret_specs(True)
    dm = pl.BlockSpec((CH, RHD), lambda h, n: (NB - 1 - n, AH * AHD // RHD + h))
    return pl.pallas_call(
        body, name="ret_bwd", grid=(RH, NB),
        in_specs=[const, col(0), col(1), col(2), col(3), own, state, dm],
        out_specs=[own, own, own, own],
        out_shape=[jax.ShapeDtypeStruct((S, RH * RHD), BF16)] * 4,
        scratch_shapes=[pltpu.VMEM((RHD, RHD), F32)],
        compiler_params=_cp(("parallel", "arbitrary")),
    )(_ret_consts(), proj, proj, proj, proj, ret, states, dmixed)


def _local_step(x, tgt, nw1, nw2, nw3, win, wout, wg, wu, wd, on_grads=None, advance=None):
    on_grads = on_grads or (lambda names, grads: None)
    advance = advance or (lambda: None)
    h1, r1 = _rms_fwd(x, nw1)
    proj = _proj(h1, win)
    o, ma, lse = _attn_fwd(proj)
    ret, mr, states = _ret_fwd(proj)
    x2, h2, r2 = _out_proj_rms(x, ma, mr, wout, nw2)
    g, u, a = _ffn_up(h2, wg, wu)
    dx3, dx3b, st3 = _ffn_down_loss(x2, a, wd, nw3, tgt)

    dwd = _wgrad_rows(a, dx3b)
    on_grads(["w_down"], [dwd])
    dg, du = _ffn_down_bwd(dx3b, wd, g, u)
    dwg = _wgrad_cols(h2, dg, "wgrad_gate")
    dwu = _wgrad_cols(h2, du, "wgrad_up")
    on_grads(["w_gate", "w_up"], [dwg, dwu])
    advance()
    dx2, dx2b, st2 = _ffn_up_bwd(dg, du, wg, wu, dx3, x2, r2, nw2)
    dwo = _wgrad_out(ma, mr, dx2b)
    on_grads(["w_out"], [dwo])
    advance()
    dmixed = _out_proj_bwd(dx2b, wout)
    dqa, dka, dva = _attn_bwd(proj, dmixed, o, lse)
    advance()
    dqr, dkr, dvr, dgr = _ret_bwd(proj, ret, states, dmixed)
    dproj = jnp.concatenate([dqa, dka, dva, dqr, dkr, dvr, dgr], axis=1)
    dwi = _wgrad_in(h1, dproj)
    on_grads(["w_in"], [dwi])
    gx, st1 = _in_proj_bwd(dproj, win, dx2, x, r1, nw1)
    advance()
    stats = jnp.concatenate([st1[0:1], st2[0:1], st3[0:2], jnp.zeros((4, D), F32)], axis=0)
    return stats, gx, dwi, dwo, dwg, dwu, dwd


def _place():
    x, y, c = lax.axis_index("x"), lax.axis_index("y"), lax.axis_index("c")
    return x, y, c, [(1 - x, y), (x, 1 - y), (1 - x, 1 - y)]


def _handshake(peers):
    barrier = pltpu.get_barrier_semaphore()
    for peer in peers:
        pl.semaphore_signal(barrier, inc=1, device_id=peer, device_id_type=MESH)
    pl.semaphore_wait(barrier, len(peers))


def _all_gather(shards, name, collective_id):
    na = len(shards)

    def body(*refs):
        ins, outs = refs[:na], refs[na:2 * na]
        send_sems, recv_sems, local_sems = refs[2 * na:]
        x, y, c, chips = _place()
        sib = (x, y, 1 - c)
        _handshake([sib] + [(*chip, c) for chip in chips])

        def copy(a, k, block, to, src=None):
            idx = 4 * block[0] + 2 * block[1] + block[2]
            return pltpu.make_async_remote_copy(
                src_ref=outs[a].at[idx] if src is None else src, dst_ref=outs[a].at[idx],
                send_sem=send_sems.at[a, k], recv_sem=recv_sems.at[a, k], device_id=to, device_id_type=MESH)

        me = (x, y, c)
        mine = [pltpu.make_async_copy(ins[a], outs[a].at[4 * x + 2 * y + c], local_sems.at[a]) for a in range(na)]
        for cp in mine:
            cp.start()
        first = []
        for a in range(na):
            first += [copy(a, 1 + j, me, (*chip, c), src=ins[a]) for j, chip in enumerate(chips)]
        for a in range(na):
            first.append(copy(a, 0, me, sib, src=ins[a]))
        for cp in first:
            cp.start()
        passed = []
        for a in range(na):
            for j, chip in enumerate(chips):
                copy(a, 1 + j, (*chip, c), me).wait_recv()
                fw = copy(a, 4 + j, (*chip, c), sib)
                fw.start()
                passed.append(fw)
        for a in range(na):
            copy(a, 0, (x, y, 1 - c), me).wait_recv()
            for j, chip in enumerate(chips):
                copy(a, 4 + j, (*chip, 1 - c), me).wait_recv()
        for cp in first + passed:
            cp.wait_send()
        for cp in mine:
            cp.wait()

    return pl.kernel(
        body, name=name,
        out_type=[jax.ShapeDtypeStruct((NDEV,) + s.shape, s.dtype) for s in shards],
        mesh=plsc.ScalarSubcoreMesh(axis_name="sequencer", num_cores=1),
        scratch_types=[pltpu.SemaphoreType.DMA((na, 7)), pltpu.SemaphoreType.DMA((na, 7)),
                       pltpu.SemaphoreType.DMA((na,))],
        compiler_params=pltpu.CompilerParams(collective_id=collective_id),
    )(*shards)


def _sequencer_call(body, name, collective_id, out_type, scratch_types):
    return pl.kernel(
        body, name=name, out_type=out_type,
        mesh=plsc.ScalarSubcoreMesh(axis_name="sequencer", num_cores=1),
        scratch_types=scratch_types,
        compiler_params=pltpu.CompilerParams(collective_id=collective_id))


def _exchange_sibling(grads, name, collective_id):
    na = len(grads)

    def body(*refs):
        ins, outs = refs[:na], refs[na:2 * na]
        send_sems, recv_sems = refs[2 * na:]
        x, y, c, _ = _place()
        _handshake([(x, y, 1 - c)])
        cps = []
        for a in range(na):
            for k in range(4):
                cps.append(pltpu.make_async_remote_copy(
                    src_ref=ins[a].at[2 * k + (1 - c)], dst_ref=outs[a].at[k],
                    send_sem=send_sems.at[a, k], recv_sem=recv_sems.at[a, k],
                    device_id=(x, y, 1 - c), device_id_type=MESH))
        for cp in cps:
            cp.start()
        for cp in cps:
            cp.wait()

    return _sequencer_call(
        body, name, collective_id,
        [jax.ShapeDtypeStruct((4,) + g.shape[1:], g.dtype) for g in grads],
        [pltpu.SemaphoreType.DMA((na, 4)), pltpu.SemaphoreType.DMA((na, 4))])(*grads)


def _row_tile(rows, cols):
    for t in (512, 256, 176, 128, 64, 32, 16):
        if rows % t == 0 and t * cols * 4 <= (1 << 20):
            return t
    raise ValueError((rows, cols))


def _chip_sum(place, g, got, name):
    _, r, c = g.shape
    tm = _row_tile(r, c)

    def body(pos_ref, g_ref, got_ref, o_ref):
        o_ref[...] = (g_ref[...].astype(F32) + got_ref[...].astype(F32)).astype(BF16)

    return pl.pallas_call(
        body, name=name,
        grid_spec=pltpu.PrefetchScalarGridSpec(
            num_scalar_prefetch=1, grid=(4, r // tm),
            in_specs=[pl.BlockSpec((None, tm, c), lambda k, i, pos: (2 * k + pos[2], i, 0)),
                      pl.BlockSpec((None, tm, c), lambda k, i, pos: (k, i, 0))],
            out_specs=pl.BlockSpec((None, tm, c), lambda k, i, pos: (k, i, 0))),
        out_shape=jax.ShapeDtypeStruct((4, r, c), BF16),
        compiler_params=_cp(("parallel", "parallel")),
    )(place, g, got)


def _exchange_chips(sums, name, collective_id):
    na = len(sums)

    def body(*refs):
        ins, outs = refs[:na], refs[na:2 * na]
        send_sems, recv_sems = refs[2 * na:]
        x, y, c, chips = _place()
        _handshake([(*chip, c) for chip in chips])
        cps = []
        for a in range(na):
            for j, chip in enumerate(chips):
                cps.append(pltpu.make_async_remote_copy(
                    src_ref=ins[a].at[2 * chip[0] + chip[1]], dst_ref=outs[a].at[j],
                    send_sem=send_sems.at[a, j], recv_sem=recv_sems.at[a, j],
                    device_id=(*chip, c), device_id_type=MESH))
        for cp in cps:
            cp.start()
        for cp in cps:
            cp.wait()

    return _sequencer_call(
        body, name, collective_id,
        [jax.ShapeDtypeStruct((3,) + s.shape[1:], s.dtype) for s in sums],
        [pltpu.SemaphoreType.DMA((na, 3)), pltpu.SemaphoreType.DMA((na, 3))])(*sums)


def _exchange_stats(stats, collective_id):
    def body(st_in, st_out, st_send, st_recv, local_sem):
        x, y, c, _ = _place()
        me_idx = 4 * x + 2 * y + c
        peers = [(x ^ ((k >> 2) & 1), y ^ ((k >> 1) & 1), c ^ (k & 1)) for k in range(1, 8)]
        _handshake(peers)
        mine = pltpu.make_async_copy(st_in, st_out.at[me_idx], local_sem)
        mine.start()
        cps = [pltpu.make_async_remote_copy(
            src_ref=st_in, dst_ref=st_out.at[me_idx], send_sem=st_send.at[k], recv_sem=st_recv.at[k],
            device_id=peer, device_id_type=MESH) for k, peer in enumerate(peers)]
        for cp in cps:
            cp.start()
        for cp in cps:
            cp.wait()
        mine.wait()

    return _sequencer_call(
        body, "exchange_stats", collective_id,
        jax.ShapeDtypeStruct((NDEV,) + stats.shape, stats.dtype),
        [pltpu.SemaphoreType.DMA((7,)), pltpu.SemaphoreType.DMA((7,)), pltpu.SemaphoreType.DMA])(stats)


def _adamw(w, g, m, v):
    m = ADAM_B1 * m + (1.0 - ADAM_B1) * g
    v = ADAM_B2 * v + (1.0 - ADAM_B2) * (g * g)
    m_hat = m / (1.0 - ADAM_B1 ** ADAM_STEP)
    v_hat = v / (1.0 - ADAM_B2 ** ADAM_STEP)
    delta = -ADAM_LR * (m_hat / (jnp.sqrt(v_hat) + ADAM_EPS) + ADAM_WD * w)
    return delta, m, v


def _shard_update(place, w, m, v, g, got_sib, got_chips, name):
    r, c = w.shape
    tm = _row_tile(r, c)

    def body(pos_ref, w_ref, m_ref, v_ref, g_ref, s_ref, c_ref, go_ref, d_ref, mo_ref, vo_ref):
        grad = g_ref[...].astype(F32) + s_ref[...].astype(F32)
        for j in range(3):
            grad = grad + c_ref[j].astype(F32)
        delta, mn, vn = _adamw(w_ref[...], grad, m_ref[...], v_ref[...])
        go_ref[...] = grad
        d_ref[...] = delta
        mo_ref[...] = mn
        vo_ref[...] = vn

    row = pl.BlockSpec((tm, c), lambda i, pos: (i, 0))
    return pl.pallas_call(
        body, name=name,
        grid_spec=pltpu.PrefetchScalarGridSpec(
            num_scalar_prefetch=1, grid=(r // tm,),
            in_specs=[row, row, row,
                      pl.BlockSpec((None, tm, c), lambda i, pos: (4 * pos[0] + 2 * pos[1] + pos[2], i, 0)),
                      pl.BlockSpec((None, tm, c), lambda i, pos: (2 * pos[0] + pos[1], i, 0)),
                      pl.BlockSpec((3, tm, c), lambda i, pos: (0, i, 0))],
            out_specs=[row, row, row, row]),
        out_shape=[jax.ShapeDtypeStruct((r, c), F32)] * 4,
        compiler_params=_cp(("parallel",)),
    )(place, w, m, v, g, got_sib, got_chips)


def _small_update(stats_all, ws, ms, vs):
    def body(st_ref, w_ref, m_ref, v_ref, go_ref, d_ref, mo_ref, vo_ref):
        grad = st_ref[0]
        for k in range(1, NDEV):
            grad = grad + st_ref[k]
        delta, mn, vn = _adamw(w_ref[...], grad, m_ref[...], v_ref[...])
        go_ref[...] = grad
        d_ref[...] = delta
        mo_ref[...] = mn
        vo_ref[...] = vn

    return pl.pallas_call(
        body, name="small_update",
        out_shape=[jax.ShapeDtypeStruct((8, D), F32)] * 4,
        compiler_params=_cp(),
    )(stats_all, ws, ms, vs)


def kernel(x, norm_mix_w, w_in, w_out, norm_ffn_w, w_gate, w_up, w_down, norm_final_w, loss_target, m_norm_mix_w, m_w_in, m_w_out, m_norm_ffn_w, m_w_gate, m_w_up, m_w_down, m_norm_final_w, v_norm_mix_w, v_w_in, v_w_out, v_norm_ffn_w, v_w_gate, v_w_up, v_w_down, v_norm_final_w):
    big_w = [w_in[0], w_out[0], w_gate[0], w_up[0], w_down[0]]
    big_m = [m_w_in[0], m_w_out[0], m_w_gate[0], m_w_up[0], m_w_down[0]]
    big_v = [v_w_in[0], v_w_out[0], v_w_gate[0], v_w_up[0], v_w_down[0]]
    names = ["w_in", "w_out", "w_gate", "w_up", "w_down"]

    shards = [_cast_bf16(w, "cast_" + n) for w, n in zip(big_w, names)]
    (win,) = _all_gather(shards[0:1], "all_gather_w_in", 1)
    wout, wg, wu, wd = _all_gather(shards[1:], "all_gather_rest", 2)
    nw3 = norm_final_w.reshape(1, D)
    place = jnp.stack([lax.axis_index("x"), lax.axis_index("y"), lax.axis_index("c")]).astype(jnp.int32)
    ids = iter(range(3, 32))
    waiting, reduced = [], {}

    def on_grads(group, grads):
        waiting.append((group, grads, _exchange_sibling(grads, "sibling_exchange_" + group[0], next(ids))))

    def advance():
        while waiting:
            group, grads, got = waiting.pop(0)
            sums = [_chip_sum(place, g, s, "chip_sum_" + n) for g, s, n in zip(grads, got, group)]
            chips = _exchange_chips(sums, "chip_exchange_" + group[0], next(ids))
            for n, g, s, c in zip(group, grads, got, chips):
                reduced[n] = (g, s, c)

    stats, gx, *_ = _local_step(
        x[0], loss_target[0], norm_mix_w, norm_ffn_w, nw3, win, wout.reshape(D, D), wg, wu, wd, on_grads, advance)
    stats_all = _exchange_stats(stats, next(ids))
    upd = [_shard_update(place, w, m, v, *reduced[n], "update_" + n)
           for w, m, v, n in zip(big_w, big_m, big_v, names)]

    def rows(a, b, c):
        return jnp.concatenate([a.reshape(1, D), b.reshape(1, D), c.reshape(1, D), jnp.zeros((5, D), F32)], axis=0)

    sg, sd, sm, sv = _small_update(stats_all, rows(norm_mix_w, norm_ffn_w, norm_final_w),
                                   rows(m_norm_mix_w, m_norm_ffn_w, m_norm_final_w),
                                   rows(v_norm_mix_w, v_norm_ffn_w, v_norm_final_w) + jnp.concatenate(
                                       [jnp.zeros((3, D), F32), jnp.ones((5, D), F32)], axis=0))
    loss = sg[3, 0]

    def outs(k, small):
        big = [u[k][None] for u in upd]
        return [small[0:1], big[0], big[1], small[1:2], big[2], big[3], big[4], small[2]]

    return (loss, gx[None], *outs(0, sg), *outs(1, sd), *outs(2, sm), *outs(3, sv))
```

```python
import functools
import math

import numpy as np
import jax
import jax.numpy as jnp
from jax import lax
from jax.experimental import pallas as pl
from jax.experimental.pallas import tpu as pltpu
from jax.experimental.pallas import tpu_sc as plsc

F32 = jnp.float32
BF16 = jnp.bfloat16

S = 2048
D = 2048
NDEV = 8
N_IN = 7168 // NDEV
N_FF = 5632 // NDEV
N_OUT = 2048 // NDEV
AH, AHD = 8, 128
RH, RHD = 4, 256
CH = 128
NB = S // CH
EPS = 1e-6
PATTERNS = ((1, 16), (4, 4), (16, 1))
NEG = -1e30
VMEM_LIMIT = 56 * 1024 * 1024

ADAM_LR, ADAM_B1, ADAM_B2, ADAM_EPS, ADAM_WD, ADAM_STEP = 0.001, 0.9, 0.999, 1e-08, 0.01, 10
MESH = pl.DeviceIdType.MESH


def _cp(sem=None):
    return pltpu.CompilerParams(dimension_semantics=sem, vmem_limit_bytes=VMEM_LIMIT)


def _dot(a, b):
    return jnp.dot(a, b, preferred_element_type=F32)


def _dot_nt(a, b):
    return lax.dot_general(a, b, (((1,), (1,)), ((), ())), preferred_element_type=F32)


def _dot_tn(a, b):
    return lax.dot_general(a, b, (((0,), (0,)), ((), ())), preferred_element_type=F32)


def _sigmoid(x):
    return 1.0 / (1.0 + jnp.exp(-x))


def _cast_bf16(w, name):
    r, c = w.shape
    tm = r if r <= 1024 else 512

    def body(w_ref, o_ref):
        o_ref[...] = w_ref[...].astype(BF16)

    return pl.pallas_call(
        body, name=name, grid=(r // tm,),
        in_specs=[pl.BlockSpec((tm, c), lambda i: (i, 0))],
        out_specs=pl.BlockSpec((tm, c), lambda i: (i, 0)),
        out_shape=jax.ShapeDtypeStruct((r, c), BF16),
        compiler_params=_cp(("parallel",)),
    )(w)


def _rms_fwd(x, nw):
    tm = 256

    def body(x_ref, w_ref, h_ref, r_ref):
        xs = x_ref[...]
        r = lax.rsqrt(jnp.mean(xs * xs, axis=-1, keepdims=True) + EPS)
        h_ref[...] = ((xs * r) * w_ref[...]).astype(BF16)
        r_ref[...] = r

    return pl.pallas_call(
        body, name="rms_fwd", grid=(S // tm,),
        in_specs=[pl.BlockSpec((tm, D), lambda i: (i, 0)), pl.BlockSpec((1, D), lambda i: (0, 0))],
        out_specs=[pl.BlockSpec((tm, D), lambda i: (i, 0)), pl.BlockSpec((tm, 1), lambda i: (i, 0))],
        out_shape=[jax.ShapeDtypeStruct((S, D), BF16), jax.ShapeDtypeStruct((S, 1), F32)],
        compiler_params=_cp(("parallel",)),
    )(x, nw)


def _rms_bwd_tile(dh, xs, r, nw):
    dnw = jnp.sum(dh * (xs * r), axis=0, keepdims=True)
    gy = dh * nw
    dx = r * gy - xs * ((r * r * r) * jnp.mean(gy * xs, axis=-1, keepdims=True))
    return dx, dnw


def _proj(h1, win):
    tm = 512

    def body(a_ref, w_ref, o_ref):
        o_ref[...] = _dot(a_ref[...], w_ref[...])

    return pl.pallas_call(
        body, name="proj", grid=(NDEV, S // tm),
        in_specs=[pl.BlockSpec((tm, D), lambda p, m: (m, 0)),
                  pl.BlockSpec((None, D, N_IN), lambda p, m: (p, 0, 0))],
        out_specs=pl.BlockSpec((tm, N_IN), lambda p, m: (m, p)),
        out_shape=jax.ShapeDtypeStruct((S, NDEV * N_IN), F32),
        compiler_params=_cp(("parallel", "parallel")),
    )(h1, win)


def _out_proj_rms(x, ma, mr, wout, nw):
    tm = 256
    half = D // 2

    def body(x_ref, ma_ref, mr_ref, w_ref, nw_ref, x2_ref, h_ref, r_ref):
        acc = _dot(ma_ref[...], w_ref[0:half, :]) + _dot(mr_ref[...], w_ref[half:D, :])
        x2 = x_ref[...] + acc
        r = lax.rsqrt(jnp.mean(x2 * x2, axis=-1, keepdims=True) + EPS)
        x2_ref[...] = x2
        h_ref[...] = ((x2 * r) * nw_ref[...]).astype(BF16)
        r_ref[...] = r

    return pl.pallas_call(
        body, name="out_proj_rms", grid=(S // tm,),
        in_specs=[pl.BlockSpec((tm, D), lambda i: (i, 0)),
                  pl.BlockSpec((tm, half), lambda i: (i, 0)),
                  pl.BlockSpec((tm, half), lambda i: (i, 0)),
                  pl.BlockSpec((D, D), lambda i: (0, 0)),
                  pl.BlockSpec((1, D), lambda i: (0, 0))],
        out_specs=[pl.BlockSpec((tm, D), lambda i: (i, 0)), pl.BlockSpec((tm, D), lambda i: (i, 0)),
                   pl.BlockSpec((tm, 1), lambda i: (i, 0))],
        out_shape=[jax.ShapeDtypeStruct((S, D), F32), jax.ShapeDtypeStruct((S, D), BF16),
                   jax.ShapeDtypeStruct((S, 1), F32)],
        compiler_params=_cp(("parallel",)),
    )(x, ma, mr, wout, nw)


def _ffn_up(h2, wg, wu):
    tm = 512

    def body(h_ref, wg_ref, wu_ref, g_ref, u_ref, a_ref):
        h = h_ref[...]
        g = _dot_nt(h, wg_ref[...])
        u = _dot_nt(h, wu_ref[...])
        g_ref[...] = g
        u_ref[...] = u
        a_ref[...] = ((g * _sigmoid(g)) * u).astype(BF16)

    blk = pl.BlockSpec((None, tm, N_FF), lambda p, m: (p, m, 0))
    wblk = pl.BlockSpec((None, N_FF, D), lambda p, m: (p, 0, 0))
    return pl.pallas_call(
        body, name="ffn_up", grid=(NDEV, S // tm),
        in_specs=[pl.BlockSpec((tm, D), lambda p, m: (m, 0)), wblk, wblk],
        out_specs=[blk, blk, blk],
        out_shape=[jax.ShapeDtypeStruct((NDEV, S, N_FF), F32), jax.ShapeDtypeStruct((NDEV, S, N_FF), F32),
                   jax.ShapeDtypeStruct((NDEV, S, N_FF), BF16)],
        compiler_params=_cp(("parallel", "parallel")),
    )(h2, wg, wu)


def _ffn_down_loss(x2, a, wd, nw, tgt):
    tm = 512

    def body(x2_ref, a_ref, w_ref, nw_ref, t_ref, dx_ref, dxb_ref, st_ref, acc_ref):
        m, p = pl.program_id(0), pl.program_id(1)

        @pl.when(p == 0)
        def _():
            acc_ref[...] = jnp.zeros_like(acc_ref)

        @pl.when((p == 0) & (m == 0))
        def _():
            st_ref[...] = jnp.zeros_like(st_ref)

        acc_ref[...] += _dot(a_ref[...], w_ref[...])

        @pl.when(p == NDEV - 1)
        def _():
            x3 = x2_ref[...] + acc_ref[...]
            nwv = nw_ref[...]
            r = lax.rsqrt(jnp.mean(x3 * x3, axis=-1, keepdims=True) + EPS)
            y = (x3 * r) * nwv
            err = y - t_ref[...]
            loss = 0.5 * jnp.sum(jnp.mean(err * err, axis=-1, keepdims=True), axis=0, keepdims=True)
            dy = err * (1.0 / D)
            dx, dnw = _rms_bwd_tile(dy, x3, r, nwv)
            dx_ref[...] = dx
            dxb_ref[...] = dx.astype(BF16)
            st_ref[0:1, :] += dnw
            st_ref[1:2, :] += jnp.broadcast_to(loss, (1, D))

    return pl.pallas_call(
        body, name="ffn_down_loss", grid=(S // tm, NDEV),
        in_specs=[pl.BlockSpec((tm, D), lambda m, p: (m, 0)),
                  pl.BlockSpec((None, tm, N_FF), lambda m, p: (p, m, 0)),
                  pl.BlockSpec((None, N_FF, D), lambda m, p: (p, 0, 0)),
                  pl.BlockSpec((1, D), lambda m, p: (0, 0)),
                  pl.BlockSpec((tm, D), lambda m, p: (m, 0))],
        out_specs=[pl.BlockSpec((tm, D), lambda m, p: (m, 0)), pl.BlockSpec((tm, D), lambda m, p: (m, 0)),
                   pl.BlockSpec((8, D), lambda m, p: (0, 0))],
        out_shape=[jax.ShapeDtypeStruct((S, D), F32), jax.ShapeDtypeStruct((S, D), BF16),
                   jax.ShapeDtypeStruct((8, D), F32)],
        scratch_shapes=[pltpu.VMEM((tm, D), F32)],
        compiler_params=_cp(("arbitrary", "arbitrary")),
    )(x2, a, wd, nw, tgt)


def _ffn_down_bwd(dx3b, wd, g, u):
    tm = 512

    def body(dx_ref, w_ref, g_ref, u_ref, dg_ref, du_ref):
        da = _dot_nt(dx_ref[...], w_ref[...])
        gv = g_ref[...]
        sg = _sigmoid(gv)
        silu = gv * sg
        dg_ref[...] = ((da * u_ref[...]) * (sg * (1.0 + gv * (1.0 - sg)))).astype(BF16)
        du_ref[...] = (da * silu).astype(BF16)

    blk = pl.BlockSpec((None, tm, N_FF), lambda p, m: (p, m, 0))
    return pl.pallas_call(
        body, name="ffn_down_bwd", grid=(NDEV, S // tm),
        in_specs=[pl.BlockSpec((tm, D), lambda p, m: (m, 0)),
                  pl.BlockSpec((None, N_FF, D), lambda p, m: (p, 0, 0)), blk, blk],
        out_specs=[blk, blk],
        out_shape=[jax.ShapeDtypeStruct((NDEV, S, N_FF), BF16), jax.ShapeDtypeStruct((NDEV, S, N_FF), BF16)],
        compiler_params=_cp(("parallel", "parallel")),
    )(dx3b, wd, g, u)


def _ffn_up_bwd(dg, du, wg, wu, dres, xs, r, nw):
    tm = 512

    def body(dg_ref, du_ref, wg_ref, wu_ref, dres_ref, x_ref, r_ref, nw_ref, dx_ref, dxb_ref, st_ref, acc_ref):
        m, p = pl.program_id(0), pl.program_id(1)

        @pl.when(p == 0)
        def _():
            acc_ref[...] = jnp.zeros_like(acc_ref)

        @pl.when((p == 0) & (m == 0))
        def _():
            st_ref[...] = jnp.zeros_like(st_ref)

        acc_ref[...] += _dot(dg_ref[...], wg_ref[...]) + _dot(du_ref[...], wu_ref[...])

        @pl.when(p == NDEV - 1)
        def _():
            dx, dnw = _rms_bwd_tile(acc_ref[...], x_ref[...], r_ref[...], nw_ref[...])
            dx = dres_ref[...] + dx
            dx_ref[...] = dx
            dxb_ref[...] = dx.astype(BF16)
            st_ref[0:1, :] += dnw

    blk = pl.BlockSpec((None, tm, N_FF), lambda m, p: (p, m, 0))
    wblk = pl.BlockSpec((None, N_FF, D), lambda m, p: (p, 0, 0))
    row = pl.BlockSpec((tm, D), lambda m, p: (m, 0))
    return pl.pallas_call(
        body, name="ffn_up_bwd", grid=(S // tm, NDEV),
        in_specs=[blk, blk, wblk, wblk, row, row, pl.BlockSpec((tm, 1), lambda m, p: (m, 0)),
                  pl.BlockSpec((1, D), lambda m, p: (0, 0))],
        out_specs=[row, row, pl.BlockSpec((8, D), lambda m, p: (0, 0))],
        out_shape=[jax.ShapeDtypeStruct((S, D), F32), jax.ShapeDtypeStruct((S, D), BF16),
                   jax.ShapeDtypeStruct((8, D), F32)],
        scratch_shapes=[pltpu.VMEM((tm, D), F32)],
        compiler_params=_cp(("arbitrary", "arbitrary")),
    )(dg, du, wg, wu, dres, xs, r, nw)


def _out_proj_bwd(dx2b, wout):
    tm = 256

    def body(dx_ref, w_ref, o_ref):
        o_ref[...] = _dot_nt(dx_ref[...], w_ref[...])

    return pl.pallas_call(
        body, name="out_proj_bwd", grid=(S // tm,),
        in_specs=[pl.BlockSpec((tm, D), lambda i: (i, 0)), pl.BlockSpec((D, D), lambda i: (0, 0))],
        out_specs=pl.BlockSpec((tm, D), lambda i: (i, 0)),
        out_shape=jax.ShapeDtypeStruct((S, D), F32),
        compiler_params=_cp(("parallel",)),
    )(dx2b, wout)


def _in_proj_bwd(dproj, win, dres, xs, r, nw):
    tm = 512

    def body(dp_ref, w_ref, dres_ref, x_ref, r_ref, nw_ref, dx_ref, st_ref, acc_ref):
        m, p = pl.program_id(0), pl.program_id(1)

        @pl.when(p == 0)
        def _():
            acc_ref[...] = jnp.zeros_like(acc_ref)

        @pl.when((p == 0) & (m == 0))
        def _():
            st_ref[...] = jnp.zeros_like(st_ref)

        acc_ref[...] += _dot_nt(dp_ref[...], w_ref[...])

        @pl.when(p == NDEV - 1)
        def _():
            dx, dnw = _rms_bwd_tile(acc_ref[...], x_ref[...], r_ref[...], nw_ref[...])
            dx_ref[...] = dres_ref[...] + dx
            st_ref[0:1, :] += dnw

    row = pl.BlockSpec((tm, D), lambda m, p: (m, 0))
    return pl.pallas_call(
        body, name="in_proj_bwd", grid=(S // tm, NDEV),
        in_specs=[pl.BlockSpec((tm, N_IN), lambda m, p: (m, p)),
                  pl.BlockSpec((None, D, N_IN), lambda m, p: (p, 0, 0)),
                  row, row, pl.BlockSpec((tm, 1), lambda m, p: (m, 0)),
                  pl.BlockSpec((1, D), lambda m, p: (0, 0))],
        out_specs=[row, pl.BlockSpec((8, D), lambda m, p: (0, 0))],
        out_shape=[jax.ShapeDtypeStruct((S, D), F32), jax.ShapeDtypeStruct((8, D), F32)],
        scratch_shapes=[pltpu.VMEM((tm, D), F32)],
        compiler_params=_cp(("arbitrary", "arbitrary")),
    )(dproj, win, dres, xs, r, nw)


def _wgrad_in(h1, dproj):
    def body(a_ref, d_ref, o_ref):
        o_ref[...] = _dot_tn(a_ref[...], d_ref[...]).astype(BF16)

    return pl.pallas_call(
        body, name="wgrad_in", grid=(NDEV,),
        in_specs=[pl.BlockSpec((S, D), lambda p: (0, 0)), pl.BlockSpec((S, N_IN), lambda p: (0, p))],
        out_specs=pl.BlockSpec((None, D, N_IN), lambda p: (p, 0, 0)),
        out_shape=jax.ShapeDtypeStruct((NDEV, D, N_IN), BF16),
        compiler_params=_cp(("parallel",)),
    )(h1, dproj)


def _wgrad_rows(a3, dy, name):
    def body(a_ref, d_ref, o_ref):
        o_ref[...] = _dot_tn(a_ref[...], d_ref[...]).astype(BF16)

    return pl.pallas_call(
        body, name=name, grid=(NDEV,),
        in_specs=[pl.BlockSpec((None, S, N_FF), lambda p: (p, 0, 0)), pl.BlockSpec((S, D), lambda p: (0, 0))],
        out_specs=pl.BlockSpec((None, N_FF, D), lambda p: (p, 0, 0)),
        out_shape=jax.ShapeDtypeStruct((NDEV, N_FF, D), BF16),
        compiler_params=_cp(("parallel",)),
    )(a3, dy)


def _wgrad_out(ma, mr, dx2b):
    half = D // 2
    per = half // N_OUT

    def body(ma_ref, mr_ref, d_ref, o_ref):
        p = pl.program_id(0)

        @pl.when(p < per)
        def _():
            o_ref[...] = _dot_tn(ma_ref[...], d_ref[...]).astype(BF16)

        @pl.when(p >= per)
        def _():
            o_ref[...] = _dot_tn(mr_ref[...], d_ref[...]).astype(BF16)

    return pl.pallas_call(
        body, name="wgrad_out", grid=(NDEV,),
        in_specs=[pl.BlockSpec((S, N_OUT), lambda p: (0, jnp.minimum(p, per - 1))),
                  pl.BlockSpec((S, N_OUT), lambda p: (0, jnp.maximum(p - per, 0))),
                  pl.BlockSpec((S, D), lambda p: (0, 0))],
        out_specs=pl.BlockSpec((None, N_OUT, D), lambda p: (p, 0, 0)),
        out_shape=jax.ShapeDtypeStruct((NDEV, N_OUT, D), BF16),
        compiler_params=_cp(("parallel",)),
    )(ma, mr, dx2b)


def _attn_consts():
    c = np.zeros((AH, 8, AHD), np.float32)
    for h in range(AH):
        c[h, :, :] = 2.0 ** (-(h + 1))
    return jnp.asarray(c)


def _permute_in(dst, src, d, cast=None):
    ln = S // d
    for rr in range(d):
        v = src[pl.ds(rr, ln, stride=d), :] if d > 1 else src[...]
        dst[rr * ln:(rr + 1) * ln, :] = v if cast is None else v.astype(cast)


def _attn_masks():
    qi = lax.broadcasted_iota(jnp.int32, (CH, CH), 0)
    kj = lax.broadcasted_iota(jnp.int32, (CH, CH), 1)
    dist_c = (qi - kj).astype(F32)
    dist_p = (qi - kj + CH).astype(F32)
    return qi >= kj, kj >= qi, dist_c, dist_p


def _attn_fwd(proj):
    scale = 1.0 / math.sqrt(AHD)

    def body(c_ref, q_ref, k_ref, v_ref, o_ref, ob_ref, lse_ref, qd, kd, vd, od, ld, *nat):
        onat, lnat = nat[0:3], nat[3:6]
        slope = c_ref[0:1, :]
        mask_c, mask_p, dist_c, dist_p = _attn_masks()
        for pi, (d, nb) in enumerate(PATTERNS):
            _permute_in(qd, q_ref, d, BF16)
            _permute_in(kd, k_ref, d, BF16)
            _permute_in(vd, v_ref, d, BF16)
            bias_c = -(slope * float(d)) * dist_c
            bias_p = -(slope * float(d)) * dist_p

            def blk(b, carry, nb=nb, bias_c=bias_c, bias_p=bias_p):
                st = pl.multiple_of(b * CH, CH)
                qb = qd[pl.ds(st, CH), :]
                kc = kd[pl.ds(st, CH), :]
                vc = vd[pl.ds(st, CH), :]
                s_c = jnp.where(mask_c, _dot_nt(qb, kc) * scale + bias_c, NEG)
                mx = jnp.max(s_c, axis=-1, keepdims=True)
                if nb > 1:
                    pst = pl.multiple_of(jnp.maximum(b - 1, 0) * CH, CH)
                    kp = kd[pl.ds(pst, CH), :]
                    vp = vd[pl.ds(pst, CH), :]
                    has_prev = (b % nb) != 0
                    s_p = jnp.where(jnp.logical_and(mask_p, has_prev), _dot_nt(qb, kp) * scale + bias_p, NEG)
                    mx = jnp.maximum(mx, jnp.max(s_p, axis=-1, keepdims=True))
                    l = (jnp.sum(jnp.exp(s_c - mx), axis=-1, keepdims=True)
                         + jnp.sum(jnp.exp(s_p - mx), axis=-1, keepdims=True))
                    lse = mx + jnp.log(l)
                    o = _dot(jnp.exp(s_c - lse).astype(BF16), vc) + _dot(jnp.exp(s_p - lse).astype(BF16), vp)
                else:
                    l = jnp.sum(jnp.exp(s_c - mx), axis=-1, keepdims=True)
                    lse = mx + jnp.log(l)
                    o = _dot(jnp.exp(s_c - lse).astype(BF16), vc)
                od[pl.ds(st, CH), :] = o
                ld[pl.ds(st, CH), :] = jnp.broadcast_to(lse, (CH, AHD))
                return carry

            lax.fori_loop(0, NB, blk, 0)
            ln = S // d
            for rr in range(d):
                if d > 1:
                    onat[pi][pl.ds(rr, ln, stride=d), :] = od[rr * ln:(rr + 1) * ln, :]
                    lnat[pi][pl.ds(rr, ln, stride=d), :] = ld[rr * ln:(rr + 1) * ln, :]
                else:
                    onat[pi][...] = od[...]
                    lnat[pi][...] = ld[...]
        l0, l1, l2 = lnat[0][...], lnat[1][...], lnat[2][...]
        mx = jnp.maximum(jnp.maximum(l0, l1), l2)
        e0, e1, e2 = jnp.exp(l0 - mx), jnp.exp(l1 - mx), jnp.exp(l2 - mx)
        den = e0 + e1 + e2
        out = (e0 / den) * onat[0][...] + (e1 / den) * onat[1][...] + (e2 / den) * onat[2][...]
        o_ref[...] = out
        ob_ref[...] = out.astype(BF16)
        lse_ref[...] = mx + jnp.log(den)

    def col(off):
        return pl.BlockSpec((S, AHD), lambda h: (0, off + h))

    return pl.pallas_call(
        body, name="attn_fwd", grid=(AH,),
        in_specs=[pl.BlockSpec((None, 8, AHD), lambda h: (h, 0, 0)), col(0), col(AH), col(2 * AH)],
        out_specs=[col(0), col(0), col(0)],
        out_shape=[jax.ShapeDtypeStruct((S, AH * AHD), F32), jax.ShapeDtypeStruct((S, AH * AHD), BF16),
                   jax.ShapeDtypeStruct((S, AH * AHD), F32)],
        scratch_shapes=[pltpu.VMEM((S, AHD), BF16), pltpu.VMEM((S, AHD), BF16), pltpu.VMEM((S, AHD), BF16),
                        pltpu.VMEM((S, AHD), F32), pltpu.VMEM((S, AHD), F32)]
        + [pltpu.VMEM((S, AHD), F32) for _ in range(6)],
        compiler_params=_cp(("parallel",)),
    )(_attn_consts(), proj, proj, proj)


def _attn_bwd(proj, dmixed, o, lse):
    scale = 1.0 / math.sqrt(AHD)

    def body(c_ref, q_ref, k_ref, v_ref, do_ref, o_ref, lse_ref, dq_ref, dk_ref, dv_ref,
             qd, kd, vd, dod, lsd, dld, dqd, dkd, dvd, delta, aq, ak, av):
        slope = c_ref[0:1, :]
        mask_c, mask_p, dist_c, dist_p = _attn_masks()
        delta[...] = jnp.broadcast_to(jnp.sum(do_ref[...] * o_ref[...], axis=-1, keepdims=True), (S, AHD))
        for pi, (d, nb) in enumerate(PATTERNS):
            _permute_in(qd, q_ref, d, BF16)
            _permute_in(kd, k_ref, d, BF16)
            _permute_in(vd, v_ref, d, BF16)
            _permute_in(dod, do_ref, d, BF16)
            _permute_in(lsd, lse_ref, d)
            _permute_in(dld, delta, d)
            dkd[...] = jnp.zeros_like(dkd)
            dvd[...] = jnp.zeros_like(dvd)
            bias_c = -(slope * float(d)) * dist_c
            bias_p = -(slope * float(d)) * dist_p

            def blk(b, carry, nb=nb, bias_c=bias_c, bias_p=bias_p):
                st = pl.multiple_of(b * CH, CH)
                cur = pl.ds(st, CH)
                qb, kc, vc, dob = qd[cur, :], kd[cur, :], vd[cur, :], dod[cur, :]
                ls, dl = lsd[cur, :], dld[cur, :]
                p_c = jnp.exp(jnp.where(mask_c, _dot_nt(qb, kc) * scale + bias_c, NEG) - ls)
                ds_c = ((p_c * (_dot_nt(dob, vc) - dl)) * scale).astype(BF16)
                dq = _dot(ds_c, kc)
                dkd[cur, :] += _dot_tn(ds_c, qb)
                dvd[cur, :] += _dot_tn(p_c.astype(BF16), dob)
                if nb > 1:
                    prev = pl.ds(pl.multiple_of(jnp.maximum(b - 1, 0) * CH, CH), CH)
                    kp, vp = kd[prev, :], vd[prev, :]
                    has_prev = (b % nb) != 0
                    p_p = jnp.exp(jnp.where(jnp.logical_and(mask_p, has_prev),
                                            _dot_nt(qb, kp) * scale + bias_p, NEG) - ls)
                    ds_p = ((p_p * (_dot_nt(dob, vp) - dl)) * scale).astype(BF16)
                    dq = dq + _dot(ds_p, kp)
                    dkd[prev, :] += _dot_tn(ds_p, qb)
                    dvd[prev, :] += _dot_tn(p_p.astype(BF16), dob)
                dqd[cur, :] = dq
                return carry

            lax.fori_loop(0, NB, blk, 0)
            ln = S // d
            for acc, src in ((aq, dqd), (ak, dkd), (av, dvd)):
                if pi == 0:
                    acc[...] = src[...]
                else:
                    for rr in range(d):
                        acc[pl.ds(rr, ln, stride=d), :] += src[rr * ln:(rr + 1) * ln, :]
        dq_ref[...] = aq[...].astype(BF16)
        dk_ref[...] = ak[...].astype(BF16)
        dv_ref[...] = av[...].astype(BF16)

    def col(off):
        return pl.BlockSpec((S, AHD), lambda h: (0, off + h))

    bf = lambda: pltpu.VMEM((S, AHD), BF16)
    f3 = lambda: pltpu.VMEM((S, AHD), F32)
    return pl.pallas_call(
        body, name="attn_bwd", grid=(AH,),
        in_specs=[pl.BlockSpec((None, 8, AHD), lambda h: (h, 0, 0)), col(0), col(AH), col(2 * AH),
                  col(0), col(0), col(0)],
        out_specs=[col(0), col(0), col(0)],
        out_shape=[jax.ShapeDtypeStruct((S, AH * AHD), BF16)] * 3,
        scratch_shapes=[bf(), bf(), bf(), bf(), f3(), f3(), f3(), f3(), f3(), f3(), f3(), f3(), f3()],
        compiler_params=_cp(("parallel",)),
    )(_attn_consts(), proj, proj, proj, dmixed, o, lse)


def _ret_consts():
    c = np.zeros((RH, 8, RHD), np.float32)
    for h in range(RH):
        c[h, :, :] = np.log(np.float32(1.0) - np.float32(2.0 ** (-5.0 - h)))
    return jnp.asarray(c)


def _ret_factors(lg):
    i = lax.broadcasted_iota(jnp.int32, (CH, CH), 0)
    j = lax.broadcasted_iota(jnp.int32, (CH, CH), 1)
    dif = (i - j).astype(F32)
    decay = jnp.where(dif >= 0, jnp.exp(lg[:, 0:CH] * jnp.maximum(dif, 0.0)), 0.0)
    row = lax.broadcasted_iota(jnp.int32, (CH, RHD), 0).astype(F32)
    zeta = jnp.exp(lg * (CH - 1.0 - row))
    xi = jnp.exp(lg * (row + 1.0))
    return decay, zeta, xi, jnp.exp(lg * float(CH))


def _ret_specs(rev):
    off = 3 * AH * AHD // RHD

    def ch(n):
        return (NB - 1 - n) if rev else n

    def col(k):
        return pl.BlockSpec((CH, RHD), lambda h, n: (ch(n), off + k * RH + h))

    own = pl.BlockSpec((CH, RHD), lambda h, n: (ch(n), h))
    state = pl.BlockSpec((None, None, RHD, RHD), lambda h, n: (h, ch(n), 0, 0))
    const = pl.BlockSpec((None, 8, RHD), lambda h, n: (h, 0, 0))
    return col, own, state, const


def _ret_fwd(proj):
    def body(c_ref, q_ref, k_ref, v_ref, g_ref, ret_ref, mr_ref, st_ref, r_acc):
        n = pl.program_id(1)

        @pl.when(n == 0)
        def _():
            r_acc[...] = jnp.zeros_like(r_acc)

        decay, zeta, xi, gch = _ret_factors(c_ref[0:1, :])
        qb = q_ref[...].astype(BF16)
        kc = k_ref[...] * (1.0 / math.sqrt(RHD))
        kb = kc.astype(BF16)
        vb = v_ref[...].astype(BF16)
        rb = r_acc[...].astype(BF16)
        st_ref[...] = rb
        scores = _dot_nt(qb, kb) * decay
        ret = _dot(scores.astype(BF16), vb) + _dot(qb, rb) * xi
        r_acc[...] = r_acc[...] * gch + _dot_tn((kc * zeta).astype(BF16), vb)
        ret_ref[...] = ret
        rr = lax.rsqrt(jnp.mean(ret * ret, axis=-1, keepdims=True) + EPS)
        gv = g_ref[...]
        mr_ref[...] = ((gv * _sigmoid(gv)) * (ret * rr)).astype(BF16)

    col, own, state, const = _ret_specs(False)
    return pl.pallas_call(
        body, name="ret_fwd", grid=(RH, NB),
        in_specs=[const, col(0), col(1), col(2), col(3)],
        out_specs=[own, own, state],
        out_shape=[jax.ShapeDtypeStruct((S, RH * RHD), F32), jax.ShapeDtypeStruct((S, RH * RHD), BF16),
                   jax.ShapeDtypeStruct((RH, NB, RHD, RHD), BF16)],
        scratch_shapes=[pltpu.VMEM((RHD, RHD), F32)],
        compiler_params=_cp(("parallel", "arbitrary")),
    )(_ret_consts(), proj, proj, proj, proj)


def _ret_bwd(proj, ret, states, dmixed):
    def body(c_ref, q_ref, k_ref, v_ref, g_ref, ret_ref, st_ref, dm_ref, dq_ref, dk_ref, dv_ref, dg_ref, g_acc):
        n = pl.program_id(1)

        @pl.when(n == 0)
        def _():
            g_acc[...] = jnp.zeros_like(g_acc)

        decay, zeta, xi, gch = _ret_factors(c_ref[0:1, :])
        ret_v = ret_ref[...]
        rr = lax.rsqrt(jnp.mean(ret_v * ret_v, axis=-1, keepdims=True) + EPS)
        gv = g_ref[...]
        sg = _sigmoid(gv)
        dmix = dm_ref[...]
        dg_ref[...] = ((dmix * (ret_v * rr)) * (sg * (1.0 + gv * (1.0 - sg)))).astype(BF16)
        dretn = dmix * (gv * sg)
        dret = rr * dretn - ret_v * ((rr * rr * rr) * jnp.mean(dretn * ret_v, axis=-1, keepdims=True))

        qb = q_ref[...].astype(BF16)
        kc = k_ref[...] * (1.0 / math.sqrt(RHD))
        kb = kc.astype(BF16)
        vb = v_ref[...].astype(BF16)
        rb = st_ref[...]
        db = dret.astype(BF16)
        sc = (_dot_nt(qb, kb) * decay).astype(BF16)
        da = (_dot_nt(db, vb) * decay).astype(BF16)
        dxi = (dret * xi).astype(BF16)
        gb = g_acc[...].astype(BF16)
        kz = (kc * zeta).astype(BF16)
        dq = _dot(da, kb) + _dot_nt(dxi, rb)
        dkc = _dot_tn(da, qb) + _dot_nt(vb, gb) * zeta
        dv = _dot_tn(sc, db) + _dot(kz, gb)
        g_acc[...] = _dot_tn(qb, dxi) + gch * g_acc[...]
        dq_ref[...] = dq.astype(BF16)
        dk_ref[...] = (dkc * (1.0 / math.sqrt(RHD))).astype(BF16)
        dv_ref[...] = dv.astype(BF16)

    col, own, state, const = _ret_specs(True)
    dm = pl.BlockSpec((CH, RHD), lambda h, n: (NB - 1 - n, AH * AHD // RHD + h))
    return pl.pallas_call(
        body, name="ret_bwd", grid=(RH, NB),
        in_specs=[const, col(0), col(1), col(2), col(3), own, state, dm],
        out_specs=[own, own, own, own],
        out_shape=[jax.ShapeDtypeStruct((S, RH * RHD), BF16)] * 4,
        scratch_shapes=[pltpu.VMEM((RHD, RHD), F32)],
        compiler_params=_cp(("parallel", "arbitrary")),
    )(_ret_consts(), proj, proj, proj, proj, ret, states, dmixed)


def _local_step(x, tgt, nw1, nw2, nw3, win, wout, wg, wu, wd, on_grads=None, advance=None):
    on_grads = on_grads or (lambda names, grads: None)
    advance = advance or (lambda: [])

    def after(values, first):
        return lax.optimization_barrier((tuple(values), tuple(first)))[0]

    h1, r1 = _rms_fwd(x, nw1)
    proj = _proj(h1, win)
    o, ma, lse = _attn_fwd(proj)
    ret, mr, states = _ret_fwd(proj)
    x2, h2, r2 = _out_proj_rms(x, ma, mr, wout, nw2)
    g, u, a = _ffn_up(h2, wg, wu)
    dx3, dx3b, st3 = _ffn_down_loss(x2, a, wd, nw3, tgt)

    dwd = _wgrad_rows(a, dx3b, "wgrad_down")
    on_grads(["w_down"], [dwd])
    (dx3b,) = after([dx3b], [dwd])
    dg, du = _ffn_down_bwd(dx3b, wd, g, u)
    dwg = _wgrad_rows(dg, h2, "wgrad_gate")
    dwu = _wgrad_rows(du, h2, "wgrad_up")
    on_grads(["w_gate", "w_up"], [dwg, dwu])
    dg, du = after([dg, du], [dwg, dwu] + advance())
    dx2, dx2b, st2 = _ffn_up_bwd(dg, du, wg, wu, dx3, x2, r2, nw2)
    dwo = _wgrad_out(ma, mr, dx2b)
    on_grads(["w_out"], [dwo])
    (dx2b,) = after([dx2b], [dwo] + advance())
    dmixed = _out_proj_bwd(dx2b, wout)
    dqa, dka, dva = _attn_bwd(proj, dmixed, o, lse)
    (dmixed,) = after([dmixed], [dqa] + advance())
    dqr, dkr, dvr, dgr = _ret_bwd(proj, ret, states, dmixed)
    dproj = jnp.concatenate([dqa, dka, dva, dqr, dkr, dvr, dgr], axis=1)
    dwi = _wgrad_in(h1, dproj)
    on_grads(["w_in"], [dwi])
    (dproj,) = after([dproj], [dwi])
    gx, st1 = _in_proj_bwd(dproj, win, dx2, x, r1, nw1)
    advance()
    stats = jnp.concatenate([st1[0:1], st2[0:1], st3[0:2], jnp.zeros((4, D), F32)], axis=0)
    return stats, gx, dwi, dwo, dwg, dwu, dwd


def _place():
    x, y, c = lax.axis_index("x"), lax.axis_index("y"), lax.axis_index("c")
    return x, y, c, [(1 - x, y), (x, 1 - y), (1 - x, 1 - y)]


def _handshake(peers):
    barrier = pltpu.get_barrier_semaphore()
    for peer in peers:
        pl.semaphore_signal(barrier, inc=1, device_id=peer, device_id_type=MESH)
    pl.semaphore_wait(barrier, len(peers))


def _all_gather(shards, name, collective_id):
    na = len(shards)

    def body(*refs):
        ins, outs = refs[:na], refs[na:2 * na]
        send_sems, recv_sems, local_sems = refs[2 * na:]
        x, y, c, chips = _place()
        sib = (x, y, 1 - c)
        _handshake([sib] + [(*chip, c) for chip in chips])

        def copy(a, k, block, to, src=None):
            idx = 4 * block[0] + 2 * block[1] + block[2]
            return pltpu.make_async_remote_copy(
                src_ref=outs[a].at[idx] if src is None else src, dst_ref=outs[a].at[idx],
                send_sem=send_sems.at[a, k], recv_sem=recv_sems.at[a, k], device_id=to, device_id_type=MESH)

        me = (x, y, c)
        mine = [pltpu.make_async_copy(ins[a], outs[a].at[4 * x + 2 * y + c], local_sems.at[a]) for a in range(na)]
        for cp in mine:
            cp.start()
        first = []
        for a in range(na):
            first += [copy(a, 1 + j, me, (*chip, c), src=ins[a]) for j, chip in enumerate(chips)]
        for a in range(na):
            first.append(copy(a, 0, me, sib, src=ins[a]))
        for cp in first:
            cp.start()
        passed = []
        for a in range(na):
            for j, chip in enumerate(chips):
                copy(a, 1 + j, (*chip, c), me).wait_recv()
                fw = copy(a, 4 + j, (*chip, c), sib)
                fw.start()
                passed.append(fw)
        for a in range(na):
            copy(a, 0, (x, y, 1 - c), me).wait_recv()
            for j, chip in enumerate(chips):
                copy(a, 4 + j, (*chip, 1 - c), me).wait_recv()
        for cp in first + passed:
            cp.wait_send()
        for cp in mine:
            cp.wait()

    return pl.kernel(
        body, name=name,
        out_type=[jax.ShapeDtypeStruct((NDEV,) + s.shape, s.dtype) for s in shards],
        mesh=plsc.ScalarSubcoreMesh(axis_name="sequencer", num_cores=1),
        scratch_types=[pltpu.SemaphoreType.DMA((na, 7)), pltpu.SemaphoreType.DMA((na, 7)),
                       pltpu.SemaphoreType.DMA((na,))],
        compiler_params=pltpu.CompilerParams(collective_id=collective_id),
    )(*shards)


def _sequencer_call(body, name, collective_id, out_type, scratch_types):
    return pl.kernel(
        body, name=name, out_type=out_type,
        mesh=plsc.ScalarSubcoreMesh(axis_name="sequencer", num_cores=1),
        scratch_types=scratch_types,
        compiler_params=pltpu.CompilerParams(collective_id=collective_id))


def _exchange_sibling(grads, name, collective_id):
    na = len(grads)

    def body(*refs):
        ins, outs = refs[:na], refs[na:2 * na]
        send_sems, recv_sems = refs[2 * na:]
        x, y, c, _ = _place()
        _handshake([(x, y, 1 - c)])
        cps = []
        for a in range(na):
            for k in range(4):
                cps.append(pltpu.make_async_remote_copy(
                    src_ref=ins[a].at[2 * k + (1 - c)], dst_ref=outs[a].at[k],
                    send_sem=send_sems.at[a, k], recv_sem=recv_sems.at[a, k],
                    device_id=(x, y, 1 - c), device_id_type=MESH))
        for cp in cps:
            cp.start()
        for cp in cps:
            cp.wait()

    return _sequencer_call(
        body, name, collective_id,
        [jax.ShapeDtypeStruct((4,) + g.shape[1:], g.dtype) for g in grads],
        [pltpu.SemaphoreType.DMA((na, 4)), pltpu.SemaphoreType.DMA((na, 4))])(*grads)


def _row_tile(rows, cols):
    for t in (512, 256, 176, 128, 64, 32, 16):
        if rows % t == 0 and t * cols * 4 <= (1 << 20):
            return t
    raise ValueError((rows, cols))


def _chip_sum(place, g, got, name):
    _, r, c = g.shape
    tm = _row_tile(r, c)

    def body(pos_ref, g_ref, got_ref, o_ref):
        o_ref[...] = (g_ref[...].astype(F32) + got_ref[...].astype(F32)).astype(BF16)

    return pl.pallas_call(
        body, name=name,
        grid_spec=pltpu.PrefetchScalarGridSpec(
            num_scalar_prefetch=1, grid=(4, r // tm),
            in_specs=[pl.BlockSpec((None, tm, c), lambda k, i, pos: (2 * k + pos[2], i, 0)),
                      pl.BlockSpec((None, tm, c), lambda k, i, pos: (k, i, 0))],
            out_specs=pl.BlockSpec((None, tm, c), lambda k, i, pos: (k, i, 0))),
        out_shape=jax.ShapeDtypeStruct((4, r, c), BF16),
        compiler_params=_cp(("parallel", "parallel")),
    )(place, g, got)


def _exchange_chips(sums, name, collective_id):
    na = len(sums)

    def body(*refs):
        ins, outs = refs[:na], refs[na:2 * na]
        send_sems, recv_sems = refs[2 * na:]
        x, y, c, chips = _place()
        _handshake([(*chip, c) for chip in chips])
        cps = []
        for a in range(na):
            for j, chip in enumerate(chips):
                cps.append(pltpu.make_async_remote_copy(
                    src_ref=ins[a].at[2 * chip[0] + chip[1]], dst_ref=outs[a].at[j],
                    send_sem=send_sems.at[a, j], recv_sem=recv_sems.at[a, j],
                    device_id=(*chip, c), device_id_type=MESH))
        for cp in cps:
            cp.start()
        for cp in cps:
            cp.wait()

    return _sequencer_call(
        body, name, collective_id,
        [jax.ShapeDtypeStruct((3,) + s.shape[1:], s.dtype) for s in sums],
        [pltpu.SemaphoreType.DMA((na, 3)), pltpu.SemaphoreType.DMA((na, 3))])(*sums)


def _exchange_stats(stats, collective_id):
    def body(st_in, st_out, st_send, st_recv, local_sem):
        x, y, c, _ = _place()
        me_idx = 4 * x + 2 * y + c
        peers = [(x ^ ((k >> 2) & 1), y ^ ((k >> 1) & 1), c ^ (k & 1)) for k in range(1, 8)]
        _handshake(peers)
        mine = pltpu.make_async_copy(st_in, st_out.at[me_idx], local_sem)
        mine.start()
        cps = [pltpu.make_async_remote_copy(
            src_ref=st_in, dst_ref=st_out.at[me_idx], send_sem=st_send.at[k], recv_sem=st_recv.at[k],
            device_id=peer, device_id_type=MESH) for k, peer in enumerate(peers)]
        for cp in cps:
            cp.start()
        for cp in cps:
            cp.wait()
        mine.wait()

    return _sequencer_call(
        body, "exchange_stats", collective_id,
        jax.ShapeDtypeStruct((NDEV,) + stats.shape, stats.dtype),
        [pltpu.SemaphoreType.DMA((7,)), pltpu.SemaphoreType.DMA((7,)), pltpu.SemaphoreType.DMA])(stats)


def _adamw(w, g, m, v):
    m = ADAM_B1 * m + (1.0 - ADAM_B1) * g
    v = ADAM_B2 * v + (1.0 - ADAM_B2) * (g * g)
    m_hat = m / (1.0 - ADAM_B1 ** ADAM_STEP)
    v_hat = v / (1.0 - ADAM_B2 ** ADAM_STEP)
    delta = -ADAM_LR * (m_hat / (jnp.sqrt(v_hat) + ADAM_EPS) + ADAM_WD * w)
    return delta, m, v


def _shard_update(place, w, m, v, g, got_sib, got_chips, name):
    r, c = w.shape
    tm = _row_tile(r, c)

    def body(pos_ref, w_ref, m_ref, v_ref, g_ref, s_ref, c_ref, go_ref, d_ref, mo_ref, vo_ref):
        grad = g_ref[...].astype(F32) + s_ref[...].astype(F32)
        for j in range(3):
            grad = grad + c_ref[j].astype(F32)
        delta, mn, vn = _adamw(w_ref[...], grad, m_ref[...], v_ref[...])
        go_ref[...] = grad
        d_ref[...] = delta
        mo_ref[...] = mn
        vo_ref[...] = vn

    row = pl.BlockSpec((tm, c), lambda i, pos: (i, 0))
    return pl.pallas_call(
        body, name=name,
        grid_spec=pltpu.PrefetchScalarGridSpec(
            num_scalar_prefetch=1, grid=(r // tm,),
            in_specs=[row, row, row,
                      pl.BlockSpec((None, tm, c), lambda i, pos: (4 * pos[0] + 2 * pos[1] + pos[2], i, 0)),
                      pl.BlockSpec((None, tm, c), lambda i, pos: (2 * pos[0] + pos[1], i, 0)),
                      pl.BlockSpec((3, tm, c), lambda i, pos: (0, i, 0))],
            out_specs=[row, row, row, row]),
        out_shape=[jax.ShapeDtypeStruct((r, c), F32)] * 4,
        compiler_params=_cp(("parallel",)),
    )(place, w, m, v, g, got_sib, got_chips)


def _small_update(stats_all, ws, ms, vs):
    def body(st_ref, w_ref, m_ref, v_ref, go_ref, d_ref, mo_ref, vo_ref):
        grad = st_ref[0]
        for k in range(1, NDEV):
            grad = grad + st_ref[k]
        delta, mn, vn = _adamw(w_ref[...], grad, m_ref[...], v_ref[...])
        go_ref[...] = grad
        d_ref[...] = delta
        mo_ref[...] = mn
        vo_ref[...] = vn

    return pl.pallas_call(
        body, name="small_update",
        out_shape=[jax.ShapeDtypeStruct((8, D), F32)] * 4,
        compiler_params=_cp(),
    )(stats_all, ws, ms, vs)


def kernel(x, norm_mix_w, w_in, w_out, norm_ffn_w, w_gate, w_up, w_down, norm_final_w, loss_target, m_norm_mix_w, m_w_in, m_w_out, m_norm_ffn_w, m_w_gate, m_w_up, m_w_down, m_norm_final_w, v_norm_mix_w, v_w_in, v_w_out, v_norm_ffn_w, v_w_gate, v_w_up, v_w_down, v_norm_final_w):
    tr = {"w_gate", "w_up"}
    names = ["w_in", "w_out", "w_gate", "w_up", "w_down"]

    def view(a, n):
        return a[0].T if n in tr else a[0]

    big_w = [view(a, n) for a, n in zip([w_in, w_out, w_gate, w_up, w_down], names)]
    big_m = [view(a, n) for a, n in zip([m_w_in, m_w_out, m_w_gate, m_w_up, m_w_down], names)]
    big_v = [view(a, n) for a, n in zip([v_w_in, v_w_out, v_w_gate, v_w_up, v_w_down], names)]

    shards = [_cast_bf16(w, "cast_" + n) for w, n in zip(big_w, names)]
    (win,) = _all_gather(shards[0:1], "all_gather_w_in", 1)
    wout, wg, wu, wd = _all_gather(shards[1:], "all_gather_rest", 2)
    nw3 = norm_final_w.reshape(1, D)
    place = jnp.stack([lax.axis_index("x"), lax.axis_index("y"), lax.axis_index("c")]).astype(jnp.int32)
    ids = iter(range(3, 32))
    waiting, reduced = [], {}

    def on_grads(group, grads):
        waiting.append((group, grads, _exchange_sibling(grads, "sibling_exchange_" + group[0], next(ids))))

    def advance():
        local = []
        while waiting:
            group, grads, got = waiting.pop(0)
            sums = [_chip_sum(place, g, s, "chip_sum_" + n) for g, s, n in zip(grads, got, group)]
            chips = _exchange_chips(sums, "chip_exchange_" + group[0], next(ids))
            for n, g, s, c in zip(group, grads, got, chips):
                reduced[n] = (g, s, c)
            local += sums
        return local

    stats, gx, *_ = _local_step(
        x[0], loss_target[0], norm_mix_w, norm_ffn_w, nw3, win, wout.reshape(D, D), wg, wu, wd, on_grads, advance)
    stats_all = _exchange_stats(stats, next(ids))
    upd = [_shard_update(place, w, m, v, *reduced[n], "update_" + n)
           for w, m, v, n in zip(big_w, big_m, big_v, names)]

    def rows(a, b, c):
        return jnp.concatenate([a.reshape(1, D), b.reshape(1, D), c.reshape(1, D), jnp.zeros((5, D), F32)], axis=0)

    sg, sd, sm, sv = _small_update(stats_all, rows(norm_mix_w, norm_ffn_w, norm_final_w),
                                   rows(m_norm_mix_w, m_norm_ffn_w, m_norm_final_w),
                                   rows(v_norm_mix_w, v_norm_ffn_w, v_norm_final_w))
    loss = sg[3, 0]

    def outs(k, small):
        big = [(u[k].T if n in tr else u[k])[None] for u, n in zip(upd, names)]
        return [small[0:1], big[0], big[1], small[1:2], big[2], big[3], big[4], small[2]]

    return (loss, gx[None], *outs(0, sg), *outs(1, sd), *outs(2, sm), *outs(3, sv))
```

```python
import functools
import math

import numpy as np
import jax
import jax.numpy as jnp
from jax import lax
from jax.experimental import pallas as pl
from jax.experimental.pallas import tpu as pltpu
from jax.experimental.pallas import tpu_sc as plsc

F32 = jnp.float32
BF16 = jnp.bfloat16

S = 2048
D = 2048
NDEV = 8
N_IN = 7168 // NDEV
N_FF = 5632 // NDEV
N_OUT = 2048 // NDEV
AH, AHD = 8, 128
RH, RHD = 4, 256
CH = 128
NB = S // CH
EPS = 1e-6
PATTERNS = ((1, 16), (4, 4), (16, 1))
NEG = -1e30
VMEM_LIMIT = 56 * 1024 * 1024

ADAM_LR, ADAM_B1, ADAM_B2, ADAM_EPS, ADAM_WD, ADAM_STEP = 0.001, 0.9, 0.999, 1e-08, 0.01, 10
MESH = pl.DeviceIdType.MESH


def _cp(sem=None):
    return pltpu.CompilerParams(dimension_semantics=sem, vmem_limit_bytes=VMEM_LIMIT)


def _dot(a, b):
    return jnp.dot(a, b, preferred_element_type=F32)


def _dot_nt(a, b):
    return lax.dot_general(a, b, (((1,), (1,)), ((), ())), preferred_element_type=F32)


def _dot_tn(a, b):
    return lax.dot_general(a, b, (((0,), (0,)), ((), ())), preferred_element_type=F32)


def _sigmoid(x):
    return 1.0 / (1.0 + jnp.exp(-x))


def _cast_bf16(w, name):
    r, c = w.shape
    tm = r if r <= 1024 else 512

    def body(w_ref, o_ref):
        o_ref[...] = w_ref[...].astype(BF16)

    return pl.pallas_call(
        body, name=name, grid=(r // tm,),
        in_specs=[pl.BlockSpec((tm, c), lambda i: (i, 0))],
        out_specs=pl.BlockSpec((tm, c), lambda i: (i, 0)),
        out_shape=jax.ShapeDtypeStruct((r, c), BF16),
        compiler_params=_cp(("parallel",)),
    )(w)


def _rms_fwd(x, nw):
    tm = 256

    def body(x_ref, w_ref, h_ref, r_ref):
        xs = x_ref[...]
        r = lax.rsqrt(jnp.mean(xs * xs, axis=-1, keepdims=True) + EPS)
        h_ref[...] = ((xs * r) * w_ref[...]).astype(BF16)
        r_ref[...] = r

    return pl.pallas_call(
        body, name="rms_fwd", grid=(S // tm,),
        in_specs=[pl.BlockSpec((tm, D), lambda i: (i, 0)), pl.BlockSpec((1, D), lambda i: (0, 0))],
        out_specs=[pl.BlockSpec((tm, D), lambda i: (i, 0)), pl.BlockSpec((tm, 1), lambda i: (i, 0))],
        out_shape=[jax.ShapeDtypeStruct((S, D), BF16), jax.ShapeDtypeStruct((S, 1), F32)],
        compiler_params=_cp(("parallel",)),
    )(x, nw)


def _rms_bwd_tile(dh, xs, r, nw):
    dnw = jnp.sum(dh * (xs * r), axis=0, keepdims=True)
    gy = dh * nw
    dx = r * gy - xs * ((r * r * r) * jnp.mean(gy * xs, axis=-1, keepdims=True))
    return dx, dnw


def _proj(h1, win):
    tm = 512

    def body(a_ref, w_ref, o_ref):
        o_ref[...] = _dot(a_ref[...], w_ref[...])

    return pl.pallas_call(
        body, name="proj", grid=(NDEV, S // tm),
        in_specs=[pl.BlockSpec((tm, D), lambda p, m: (m, 0)),
                  pl.BlockSpec((None, D, N_IN), lambda p, m: (p, 0, 0))],
        out_specs=pl.BlockSpec((tm, N_IN), lambda p, m: (m, p)),
        out_shape=jax.ShapeDtypeStruct((S, NDEV * N_IN), F32),
        compiler_params=_cp(("parallel", "parallel")),
    )(h1, win)


def _out_proj_rms(x, ma, mr, wout, nw):
    tm = 256
    half = D // 2

    def body(x_ref, ma_ref, mr_ref, w_ref, nw_ref, x2_ref, h_ref, r_ref):
        acc = _dot(ma_ref[...], w_ref[0:half, :]) + _dot(mr_ref[...], w_ref[half:D, :])
        x2 = x_ref[...] + acc
        r = lax.rsqrt(jnp.mean(x2 * x2, axis=-1, keepdims=True) + EPS)
        x2_ref[...] = x2
        h_ref[...] = ((x2 * r) * nw_ref[...]).astype(BF16)
        r_ref[...] = r

    return pl.pallas_call(
        body, name="out_proj_rms", grid=(S // tm,),
        in_specs=[pl.BlockSpec((tm, D), lambda i: (i, 0)),
                  pl.BlockSpec((tm, half), lambda i: (i, 0)),
                  pl.BlockSpec((tm, half), lambda i: (i, 0)),
                  pl.BlockSpec((D, D), lambda i: (0, 0)),
                  pl.BlockSpec((1, D), lambda i: (0, 0))],
        out_specs=[pl.BlockSpec((tm, D), lambda i: (i, 0)), pl.BlockSpec((tm, D), lambda i: (i, 0)),
                   pl.BlockSpec((tm, 1), lambda i: (i, 0))],
        out_shape=[jax.ShapeDtypeStruct((S, D), F32), jax.ShapeDtypeStruct((S, D), BF16),
                   jax.ShapeDtypeStruct((S, 1), F32)],
        compiler_params=_cp(("parallel",)),
    )(x, ma, mr, wout, nw)


def _ffn_up(h2, wg, wu):
    tm = 512

    def body(h_ref, wg_ref, wu_ref, g_ref, u_ref, a_ref):
        h = h_ref[...]
        g = _dot_nt(h, wg_ref[...])
        u = _dot_nt(h, wu_ref[...])
        g_ref[...] = g
        u_ref[...] = u
        a_ref[...] = ((g * _sigmoid(g)) * u).astype(BF16)

    blk = pl.BlockSpec((None, tm, N_FF), lambda p, m: (p, m, 0))
    wblk = pl.BlockSpec((None, N_FF, D), lambda p, m: (p, 0, 0))
    return pl.pallas_call(
        body, name="ffn_up", grid=(NDEV, S // tm),
        in_specs=[pl.BlockSpec((tm, D), lambda p, m: (m, 0)), wblk, wblk],
        out_specs=[blk, blk, blk],
        out_shape=[jax.ShapeDtypeStruct((NDEV, S, N_FF), F32), jax.ShapeDtypeStruct((NDEV, S, N_FF), F32),
                   jax.ShapeDtypeStruct((NDEV, S, N_FF), BF16)],
        compiler_params=_cp(("parallel", "parallel")),
    )(h2, wg, wu)


def _ffn_down_loss(x2, a, wd, nw, tgt):
    tm = 512

    def body(x2_ref, a_ref, w_ref, nw_ref, t_ref, dx_ref, dxb_ref, st_ref, acc_ref):
        m, p = pl.program_id(0), pl.program_id(1)

        @pl.when(p == 0)
        def _():
            acc_ref[...] = jnp.zeros_like(acc_ref)

        @pl.when((p == 0) & (m == 0))
        def _():
            st_ref[...] = jnp.zeros_like(st_ref)

        acc_ref[...] += _dot(a_ref[...], w_ref[...])

        @pl.when(p == NDEV - 1)
        def _():
            x3 = x2_ref[...] + acc_ref[...]
            nwv = nw_ref[...]
            r = lax.rsqrt(jnp.mean(x3 * x3, axis=-1, keepdims=True) + EPS)
            y = (x3 * r) * nwv
            err = y - t_ref[...]
            loss = 0.5 * jnp.sum(jnp.mean(err * err, axis=-1, keepdims=True), axis=0, keepdims=True)
            dy = err * (1.0 / D)
            dx, dnw = _rms_bwd_tile(dy, x3, r, nwv)
            dx_ref[...] = dx
            dxb_ref[...] = dx.astype(BF16)
            st_ref[0:1, :] += dnw
            st_ref[1:2, :] += jnp.broadcast_to(loss, (1, D))

    return pl.pallas_call(
        body, name="ffn_down_loss", grid=(S // tm, NDEV),
        in_specs=[pl.BlockSpec((tm, D), lambda m, p: (m, 0)),
                  pl.BlockSpec((None, tm, N_FF), lambda m, p: (p, m, 0)),
                  pl.BlockSpec((None, N_FF, D), lambda m, p: (p, 0, 0)),
                  pl.BlockSpec((1, D), lambda m, p: (0, 0)),
                  pl.BlockSpec((tm, D), lambda m, p: (m, 0))],
        out_specs=[pl.BlockSpec((tm, D), lambda m, p: (m, 0)), pl.BlockSpec((tm, D), lambda m, p: (m, 0)),
                   pl.BlockSpec((8, D), lambda m, p: (0, 0))],
        out_shape=[jax.ShapeDtypeStruct((S, D), F32), jax.ShapeDtypeStruct((S, D), BF16),
                   jax.ShapeDtypeStruct((8, D), F32)],
        scratch_shapes=[pltpu.VMEM((tm, D), F32)],
        compiler_params=_cp(("arbitrary", "arbitrary")),
    )(x2, a, wd, nw, tgt)


def _ffn_down_bwd(dx3b, wd, g, u):
    tm = 512

    def body(dx_ref, w_ref, g_ref, u_ref, dg_ref, du_ref):
        da = _dot_nt(dx_ref[...], w_ref[...])
        gv = g_ref[...]
        sg = _sigmoid(gv)
        silu = gv * sg
        dg_ref[...] = ((da * u_ref[...]) * (sg * (1.0 + gv * (1.0 - sg)))).astype(BF16)
        du_ref[...] = (da * silu).astype(BF16)

    blk = pl.BlockSpec((None, tm, N_FF), lambda p, m: (p, m, 0))
    return pl.pallas_call(
        body, name="ffn_down_bwd", grid=(NDEV, S // tm),
        in_specs=[pl.BlockSpec((tm, D), lambda p, m: (m, 0)),
                  pl.BlockSpec((None, N_FF, D), lambda p, m: (p, 0, 0)), blk, blk],
        out_specs=[blk, blk],
        out_shape=[jax.ShapeDtypeStruct((NDEV, S, N_FF), BF16), jax.ShapeDtypeStruct((NDEV, S, N_FF), BF16)],
        compiler_params=_cp(("parallel", "parallel")),
    )(dx3b, wd, g, u)


def _ffn_up_bwd(dg, du, wg, wu, dres, xs, r, nw):
    tm = 512

    def body(dg_ref, du_ref, wg_ref, wu_ref, dres_ref, x_ref, r_ref, nw_ref, dx_ref, dxb_ref, st_ref, acc_ref):
        m, p = pl.program_id(0), pl.program_id(1)

        @pl.when(p == 0)
        def _():
            acc_ref[...] = jnp.zeros_like(acc_ref)

        @pl.when((p == 0) & (m == 0))
        def _():
            st_ref[...] = jnp.zeros_like(st_ref)

        acc_ref[...] += _dot(dg_ref[...], wg_ref[...]) + _dot(du_ref[...], wu_ref[...])

        @pl.when(p == NDEV - 1)
        def _():
            dx, dnw = _rms_bwd_tile(acc_ref[...], x_ref[...], r_ref[...], nw_ref[...])
            dx = dres_ref[...] + dx
            dx_ref[...] = dx
            dxb_ref[...] = dx.astype(BF16)
            st_ref[0:1, :] += dnw

    blk = pl.BlockSpec((None, tm, N_FF), lambda m, p: (p, m, 0))
    wblk = pl.BlockSpec((None, N_FF, D), lambda m, p: (p, 0, 0))
    row = pl.BlockSpec((tm, D), lambda m, p: (m, 0))
    return pl.pallas_call(
        body, name="ffn_up_bwd", grid=(S // tm, NDEV),
        in_specs=[blk, blk, wblk, wblk, row, row, pl.BlockSpec((tm, 1), lambda m, p: (m, 0)),
                  pl.BlockSpec((1, D), lambda m, p: (0, 0))],
        out_specs=[row, row, pl.BlockSpec((8, D), lambda m, p: (0, 0))],
        out_shape=[jax.ShapeDtypeStruct((S, D), F32), jax.ShapeDtypeStruct((S, D), BF16),
                   jax.ShapeDtypeStruct((8, D), F32)],
        scratch_shapes=[pltpu.VMEM((tm, D), F32)],
        compiler_params=_cp(("arbitrary", "arbitrary")),
    )(dg, du, wg, wu, dres, xs, r, nw)


def _out_proj_bwd(dx2b, wout):
    tm = 256

    def body(dx_ref, w_ref, o_ref):
        o_ref[...] = _dot_nt(dx_ref[...], w_ref[...])

    return pl.pallas_call(
        body, name="out_proj_bwd", grid=(S // tm,),
        in_specs=[pl.BlockSpec((tm, D), lambda i: (i, 0)), pl.BlockSpec((D, D), lambda i: (0, 0))],
        out_specs=pl.BlockSpec((tm, D), lambda i: (i, 0)),
        out_shape=jax.ShapeDtypeStruct((S, D), F32),
        compiler_params=_cp(("parallel",)),
    )(dx2b, wout)


def _in_proj_bwd(dproj, win, dres, xs, r, nw):
    tm = 512

    def body(dp_ref, w_ref, dres_ref, x_ref, r_ref, nw_ref, dx_ref, st_ref, acc_ref):
        m, p = pl.program_id(0), pl.program_id(1)

        @pl.when(p == 0)
        def _():
            acc_ref[...] = jnp.zeros_like(acc_ref)

        @pl.when((p == 0) & (m == 0))
        def _():
            st_ref[...] = jnp.zeros_like(st_ref)

        acc_ref[...] += _dot_nt(dp_ref[...], w_ref[...])

        @pl.when(p == NDEV - 1)
        def _():
            dx, dnw = _rms_bwd_tile(acc_ref[...], x_ref[...], r_ref[...], nw_ref[...])
            dx_ref[...] = dres_ref[...] + dx
            st_ref[0:1, :] += dnw

    row = pl.BlockSpec((tm, D), lambda m, p: (m, 0))
    return pl.pallas_call(
        body, name="in_proj_bwd", grid=(S // tm, NDEV),
        in_specs=[pl.BlockSpec((tm, N_IN), lambda m, p: (m, p)),
                  pl.BlockSpec((None, D, N_IN), lambda m, p: (p, 0, 0)),
                  row, row, pl.BlockSpec((tm, 1), lambda m, p: (m, 0)),
                  pl.BlockSpec((1, D), lambda m, p: (0, 0))],
        out_specs=[row, pl.BlockSpec((8, D), lambda m, p: (0, 0))],
        out_shape=[jax.ShapeDtypeStruct((S, D), F32), jax.ShapeDtypeStruct((8, D), F32)],
        scratch_shapes=[pltpu.VMEM((tm, D), F32)],
        compiler_params=_cp(("arbitrary", "arbitrary")),
    )(dproj, win, dres, xs, r, nw)


def _wgrad_in(h1, dproj):
    def body(a_ref, d_ref, o_ref):
        o_ref[...] = _dot_tn(a_ref[...], d_ref[...]).astype(BF16)

    return pl.pallas_call(
        body, name="wgrad_in", grid=(NDEV,),
        in_specs=[pl.BlockSpec((S, D), lambda p: (0, 0)), pl.BlockSpec((S, N_IN), lambda p: (0, p))],
        out_specs=pl.BlockSpec((None, D, N_IN), lambda p: (p, 0, 0)),
        out_shape=jax.ShapeDtypeStruct((NDEV, D, N_IN), BF16),
        compiler_params=_cp(("parallel",)),
    )(h1, dproj)


def _wgrad_rows(a3, dy, name):
    def body(a_ref, d_ref, o_ref):
        o_ref[...] = _dot_tn(a_ref[...], d_ref[...]).astype(BF16)

    return pl.pallas_call(
        body, name=name, grid=(NDEV,),
        in_specs=[pl.BlockSpec((None, S, N_FF), lambda p: (p, 0, 0)), pl.BlockSpec((S, D), lambda p: (0, 0))],
        out_specs=pl.BlockSpec((None, N_FF, D), lambda p: (p, 0, 0)),
        out_shape=jax.ShapeDtypeStruct((NDEV, N_FF, D), BF16),
        compiler_params=_cp(("parallel",)),
    )(a3, dy)


def _wgrad_out(ma, mr, dx2b):
    half = D // 2
    per = half // N_OUT

    def body(ma_ref, mr_ref, d_ref, o_ref):
        p = pl.program_id(0)

        @pl.when(p < per)
        def _():
            o_ref[...] = _dot_tn(ma_ref[...], d_ref[...]).astype(BF16)

        @pl.when(p >= per)
        def _():
            o_ref[...] = _dot_tn(mr_ref[...], d_ref[...]).astype(BF16)

    return pl.pallas_call(
        body, name="wgrad_out", grid=(NDEV,),
        in_specs=[pl.BlockSpec((S, N_OUT), lambda p: (0, jnp.minimum(p, per - 1))),
                  pl.BlockSpec((S, N_OUT), lambda p: (0, jnp.maximum(p - per, 0))),
                  pl.BlockSpec((S, D), lambda p: (0, 0))],
        out_specs=pl.BlockSpec((None, N_OUT, D), lambda p: (p, 0, 0)),
        out_shape=jax.ShapeDtypeStruct((NDEV, N_OUT, D), BF16),
        compiler_params=_cp(("parallel",)),
    )(ma, mr, dx2b)


def _attn_consts():
    c = np.zeros((AH, 8, AHD), np.float32)
    for h in range(AH):
        c[h, :, :] = 2.0 ** (-(h + 1))
    return jnp.asarray(c)


def _permute_in(dst, src, d, cast=None):
    ln = S // d
    for rr in range(d):
        v = src[pl.ds(rr, ln, stride=d), :] if d > 1 else src[...]
        dst[rr * ln:(rr + 1) * ln, :] = v if cast is None else v.astype(cast)


def _attn_masks():
    qi = lax.broadcasted_iota(jnp.int32, (CH, CH), 0)
    kj = lax.broadcasted_iota(jnp.int32, (CH, CH), 1)
    dist_c = (qi - kj).astype(F32)
    dist_p = (qi - kj + CH).astype(F32)
    return qi >= kj, kj >= qi, dist_c, dist_p


def _attn_fwd(proj):
    scale = 1.0 / math.sqrt(AHD)

    def body(c_ref, q_ref, k_ref, v_ref, o_ref, ob_ref, lse_ref, qd, kd, vd, od, ld, *nat):
        onat, lnat = nat[0:3], nat[3:6]
        slope = c_ref[0:1, :]
        mask_c, mask_p, dist_c, dist_p = _attn_masks()
        for pi, (d, nb) in enumerate(PATTERNS):
            _permute_in(qd, q_ref, d, BF16)
            _permute_in(kd, k_ref, d, BF16)
            _permute_in(vd, v_ref, d, BF16)
            bias_c = -(slope * float(d)) * dist_c
            bias_p = -(slope * float(d)) * dist_p

            def blk(b, carry, nb=nb, bias_c=bias_c, bias_p=bias_p):
                st = pl.multiple_of(b * CH, CH)
                qb = qd[pl.ds(st, CH), :]
                kc = kd[pl.ds(st, CH), :]
                vc = vd[pl.ds(st, CH), :]
                s_c = jnp.where(mask_c, _dot_nt(qb, kc) * scale + bias_c, NEG)
                mx = jnp.max(s_c, axis=-1, keepdims=True)
                if nb > 1:
                    pst = pl.multiple_of(jnp.maximum(b - 1, 0) * CH, CH)
                    kp = kd[pl.ds(pst, CH), :]
                    vp = vd[pl.ds(pst, CH), :]
                    has_prev = (b % nb) != 0
                    s_p = jnp.where(jnp.logical_and(mask_p, has_prev), _dot_nt(qb, kp) * scale + bias_p, NEG)
                    mx = jnp.maximum(mx, jnp.max(s_p, axis=-1, keepdims=True))
                    l = (jnp.sum(jnp.exp(s_c - mx), axis=-1, keepdims=True)
                         + jnp.sum(jnp.exp(s_p - mx), axis=-1, keepdims=True))
                    lse = mx + jnp.log(l)
                    o = _dot(jnp.exp(s_c - lse).astype(BF16), vc) + _dot(jnp.exp(s_p - lse).astype(BF16), vp)
                else:
                    l = jnp.sum(jnp.exp(s_c - mx), axis=-1, keepdims=True)
                    lse = mx + jnp.log(l)
                    o = _dot(jnp.exp(s_c - lse).astype(BF16), vc)
                od[pl.ds(st, CH), :] = o
                ld[pl.ds(st, CH), :] = jnp.broadcast_to(lse, (CH, AHD))
                return carry

            lax.fori_loop(0, NB, blk, 0)
            ln = S // d
            for rr in range(d):
                if d > 1:
                    onat[pi][pl.ds(rr, ln, stride=d), :] = od[rr * ln:(rr + 1) * ln, :]
                    lnat[pi][pl.ds(rr, ln, stride=d), :] = ld[rr * ln:(rr + 1) * ln, :]
                else:
                    onat[pi][...] = od[...]
                    lnat[pi][...] = ld[...]
        l0, l1, l2 = lnat[0][...], lnat[1][...], lnat[2][...]
        mx = jnp.maximum(jnp.maximum(l0, l1), l2)
        e0, e1, e2 = jnp.exp(l0 - mx), jnp.exp(l1 - mx), jnp.exp(l2 - mx)
        den = e0 + e1 + e2
        out = (e0 / den) * onat[0][...] + (e1 / den) * onat[1][...] + (e2 / den) * onat[2][...]
        o_ref[...] = out
        ob_ref[...] = out.astype(BF16)
        lse_ref[...] = mx + jnp.log(den)

    def col(off):
        return pl.BlockSpec((S, AHD), lambda h: (0, off + h))

    return pl.pallas_call(
        body, name="attn_fwd", grid=(AH,),
        in_specs=[pl.BlockSpec((None, 8, AHD), lambda h: (h, 0, 0)), col(0), col(AH), col(2 * AH)],
        out_specs=[col(0), col(0), col(0)],
        out_shape=[jax.ShapeDtypeStruct((S, AH * AHD), F32), jax.ShapeDtypeStruct((S, AH * AHD), BF16),
                   jax.ShapeDtypeStruct((S, AH * AHD), F32)],
        scratch_shapes=[pltpu.VMEM((S, AHD), BF16), pltpu.VMEM((S, AHD), BF16), pltpu.VMEM((S, AHD), BF16),
                        pltpu.VMEM((S, AHD), F32), pltpu.VMEM((S, AHD), F32)]
        + [pltpu.VMEM((S, AHD), F32) for _ in range(6)],
        compiler_params=_cp(("parallel",)),
    )(_attn_consts(), proj, proj, proj)


def _attn_bwd(proj, dmixed, o, lse):
    scale = 1.0 / math.sqrt(AHD)

    def body(c_ref, q_ref, k_ref, v_ref, do_ref, o_ref, lse_ref, dq_ref, dk_ref, dv_ref,
             qd, kd, vd, dod, lsd, dld, dqd, dkd, dvd, delta, aq, ak, av):
        slope = c_ref[0:1, :]
        mask_c, mask_p, dist_c, dist_p = _attn_masks()
        delta[...] = jnp.broadcast_to(jnp.sum(do_ref[...] * o_ref[...], axis=-1, keepdims=True), (S, AHD))
        for pi, (d, nb) in enumerate(PATTERNS):
            _permute_in(qd, q_ref, d, BF16)
            _permute_in(kd, k_ref, d, BF16)
            _permute_in(vd, v_ref, d, BF16)
            _permute_in(dod, do_ref, d, BF16)
            _permute_in(lsd, lse_ref, d)
            _permute_in(dld, delta, d)
            dkd[...] = jnp.zeros_like(dkd)
            dvd[...] = jnp.zeros_like(dvd)
            bias_c = -(slope * float(d)) * dist_c
            bias_p = -(slope * float(d)) * dist_p

            def blk(b, carry, nb=nb, bias_c=bias_c, bias_p=bias_p):
                st = pl.multiple_of(b * CH, CH)
                cur = pl.ds(st, CH)
                qb, kc, vc, dob = qd[cur, :], kd[cur, :], vd[cur, :], dod[cur, :]
                ls, dl = lsd[cur, :], dld[cur, :]
                p_c = jnp.exp(jnp.where(mask_c, _dot_nt(qb, kc) * scale + bias_c, NEG) - ls)
                ds_c = ((p_c * (_dot_nt(dob, vc) - dl)) * scale).astype(BF16)
                dq = _dot(ds_c, kc)
                dkd[cur, :] += _dot_tn(ds_c, qb)
                dvd[cur, :] += _dot_tn(p_c.astype(BF16), dob)
                if nb > 1:
                    prev = pl.ds(pl.multiple_of(jnp.maximum(b - 1, 0) * CH, CH), CH)
                    kp, vp = kd[prev, :], vd[prev, :]
                    has_prev = (b % nb) != 0
                    p_p = jnp.exp(jnp.where(jnp.logical_and(mask_p, has_prev),
                                            _dot_nt(qb, kp) * scale + bias_p, NEG) - ls)
                    ds_p = ((p_p * (_dot_nt(dob, vp) - dl)) * scale).astype(BF16)
                    dq = dq + _dot(ds_p, kp)
                    dkd[prev, :] += _dot_tn(ds_p, qb)
                    dvd[prev, :] += _dot_tn(p_p.astype(BF16), dob)
                dqd[cur, :] = dq
                return carry

            lax.fori_loop(0, NB, blk, 0)
            ln = S // d
            for acc, src in ((aq, dqd), (ak, dkd), (av, dvd)):
                if pi == 0:
                    acc[...] = src[...]
                else:
                    for rr in range(d):
                        acc[pl.ds(rr, ln, stride=d), :] += src[rr * ln:(rr + 1) * ln, :]
        dq_ref[...] = aq[...].astype(BF16)
        dk_ref[...] = ak[...].astype(BF16)
        dv_ref[...] = av[...].astype(BF16)

    def col(off):
        return pl.BlockSpec((S, AHD), lambda h: (0, off + h))

    bf = lambda: pltpu.VMEM((S, AHD), BF16)
    f3 = lambda: pltpu.VMEM((S, AHD), F32)
    return pl.pallas_call(
        body, name="attn_bwd", grid=(AH,),
        in_specs=[pl.BlockSpec((None, 8, AHD), lambda h: (h, 0, 0)), col(0), col(AH), col(2 * AH),
                  col(0), col(0), col(0)],
        out_specs=[col(0), col(0), col(0)],
        out_shape=[jax.ShapeDtypeStruct((S, AH * AHD), BF16)] * 3,
        scratch_shapes=[bf(), bf(), bf(), bf(), f3(), f3(), f3(), f3(), f3(), f3(), f3(), f3(), f3()],
        compiler_params=_cp(("parallel",)),
    )(_attn_consts(), proj, proj, proj, dmixed, o, lse)


def _ret_consts():
    c = np.zeros((RH, 8, RHD), np.float32)
    for h in range(RH):
        c[h, :, :] = np.log(np.float32(1.0) - np.float32(2.0 ** (-5.0 - h)))
    return jnp.asarray(c)


def _ret_factors(lg):
    i = lax.broadcasted_iota(jnp.int32, (CH, CH), 0)
    j = lax.broadcasted_iota(jnp.int32, (CH, CH), 1)
    dif = (i - j).astype(F32)
    decay = jnp.where(dif >= 0, jnp.exp(lg[:, 0:CH] * jnp.maximum(dif, 0.0)), 0.0)
    row = lax.broadcasted_iota(jnp.int32, (CH, RHD), 0).astype(F32)
    zeta = jnp.exp(lg * (CH - 1.0 - row))
    xi = jnp.exp(lg * (row + 1.0))
    return decay, zeta, xi, jnp.exp(lg * float(CH))


def _ret_specs(rev):
    off = 3 * AH * AHD // RHD

    def ch(n):
        return (NB - 1 - n) if rev else n

    def col(k):
        return pl.BlockSpec((CH, RHD), lambda h, n: (ch(n), off + k * RH + h))

    own = pl.BlockSpec((CH, RHD), lambda h, n: (ch(n), h))
    state = pl.BlockSpec((None, None, RHD, RHD), lambda h, n: (h, ch(n), 0, 0))
    const = pl.BlockSpec((None, 8, RHD), lambda h, n: (h, 0, 0))
    return col, own, state, const


def _ret_fwd(proj):
    def body(c_ref, q_ref, k_ref, v_ref, g_ref, ret_ref, mr_ref, st_ref, r_acc):
        n = pl.program_id(1)

        @pl.when(n == 0)
        def _():
            r_acc[...] = jnp.zeros_like(r_acc)

        decay, zeta, xi, gch = _ret_factors(c_ref[0:1, :])
        qb = q_ref[...].astype(BF16)
        kc = k_ref[...] * (1.0 / math.sqrt(RHD))
        kb = kc.astype(BF16)
        vb = v_ref[...].astype(BF16)
        rb = r_acc[...].astype(BF16)
        st_ref[...] = rb
        scores = _dot_nt(qb, kb) * decay
        ret = _dot(scores.astype(BF16), vb) + _dot(qb, rb) * xi
        r_acc[...] = r_acc[...] * gch + _dot_tn((kc * zeta).astype(BF16), vb)
        ret_ref[...] = ret
        rr = lax.rsqrt(jnp.mean(ret * ret, axis=-1, keepdims=True) + EPS)
        gv = g_ref[...]
        mr_ref[...] = ((gv * _sigmoid(gv)) * (ret * rr)).astype(BF16)

    col, own, state, const = _ret_specs(False)
    return pl.pallas_call(
        body, name="ret_fwd", grid=(RH, NB),
        in_specs=[const, col(0), col(1), col(2), col(3)],
        out_specs=[own, own, state],
        out_shape=[jax.ShapeDtypeStruct((S, RH * RHD), F32), jax.ShapeDtypeStruct((S, RH * RHD), BF16),
                   jax.ShapeDtypeStruct((RH, NB, RHD, RHD), BF16)],
        scratch_shapes=[pltpu.VMEM((RHD, RHD), F32)],
        compiler_params=_cp(("parallel", "arbitrary")),
    )(_ret_consts(), proj, proj, proj, proj)


def _ret_bwd(proj, ret, states, dmixed):
    def body(c_ref, q_ref, k_ref, v_ref, g_ref, ret_ref, st_ref, dm_ref, dq_ref, dk_ref, dv_ref, dg_ref, g_acc):
        n = pl.program_id(1)

        @pl.when(n == 0)
        def _():
            g_acc[...] = jnp.zeros_like(g_acc)

        decay, zeta, xi, gch = _ret_factors(c_ref[0:1, :])
        ret_v = ret_ref[...]
        rr = lax.rsqrt(jnp.mean(ret_v * ret_v, axis=-1, keepdims=True) + EPS)
        gv = g_ref[...]
        sg = _sigmoid(gv)
        dmix = dm_ref[...]
        dg_ref[...] = ((dmix * (ret_v * rr)) * (sg * (1.0 + gv * (1.0 - sg)))).astype(BF16)
        dretn = dmix * (gv * sg)
        dret = rr * dretn - ret_v * ((rr * rr * rr) * jnp.mean(dretn * ret_v, axis=-1, keepdims=True))

        qb = q_ref[...].astype(BF16)
        kc = k_ref[...] * (1.0 / math.sqrt(RHD))
        kb = kc.astype(BF16)
        vb = v_ref[...].astype(BF16)
        rb = st_ref[...]
        db = dret.astype(BF16)
        sc = (_dot_nt(qb, kb) * decay).astype(BF16)
        da = (_dot_nt(db, vb) * decay).astype(BF16)
        dxi = (dret * xi).astype(BF16)
        gb = g_acc[...].astype(BF16)
        kz = (kc * zeta).astype(BF16)
        dq = _dot(da, kb) + _dot_nt(dxi, rb)
        dkc = _dot_tn(da, qb) + _dot_nt(vb, gb) * zeta
        dv = _dot_tn(sc, db) + _dot(kz, gb)
        g_acc[...] = _dot_tn(qb, dxi) + gch * g_acc[...]
        dq_ref[...] = dq.astype(BF16)
        dk_ref[...] = (dkc * (1.0 / math.sqrt(RHD))).astype(BF16)
        dv_ref[...] = dv.astype(BF16)

    col, own, state, const = _ret_specs(True)
    dm = pl.BlockSpec((CH, RHD), lambda h, n: (NB - 1 - n, AH * AHD // RHD + h))
    return pl.pallas_call(
        body, name="ret_bwd", grid=(RH, NB),
        in_specs=[const, col(0), col(1), col(2), col(3), own, state, dm],
        out_specs=[own, own, own, own],
        out_shape=[jax.ShapeDtypeStruct((S, RH * RHD), BF16)] * 4,
        scratch_shapes=[pltpu.VMEM((RHD, RHD), F32)],
        compiler_params=_cp(("parallel", "arbitrary")),
    )(_ret_consts(), proj, proj, proj, proj, ret, states, dmixed)


class _NoReduction:
    def start(self, group, grads):
        pass

    def local(self, name):
        return []

    def landed(self, name):
        return []


def _local_step(x, tgt, nw1, nw2, nw3, win, wout, wg, wu, wd, red=None):
    red = red or _NoReduction()

    def after(values, first):
        return lax.optimization_barrier((tuple(values), tuple(first)))[0]

    h1, r1 = _rms_fwd(x, nw1)
    proj = _proj(h1, win)
    o, ma, lse = _attn_fwd(proj)
    ret, mr, states = _ret_fwd(proj)
    x2, h2, r2 = _out_proj_rms(x, ma, mr, wout, nw2)
    g, u, a = _ffn_up(h2, wg, wu)
    dx3, dx3b, st3 = _ffn_down_loss(x2, a, wd, nw3, tgt)

    dwd = _wgrad_rows(a, dx3b, "wgrad_down")
    red.start(["w_down"], [dwd])
    (dx3b,) = after([dx3b], [dwd])
    dg, du = _ffn_down_bwd(dx3b, wd, g, u)
    dg, du = after([dg, du], red.local("w_down"))
    dwg = _wgrad_rows(dg, h2, "wgrad_gate")
    dwu = _wgrad_rows(du, h2, "wgrad_up")
    red.start(["w_gate", "w_up"], [dwg, dwu])
    dg, du = after([dg, du], [dwg, dwu])
    dx2, dx2b, st2 = _ffn_up_bwd(dg, du, wg, wu, dx3, x2, r2, nw2)
    (dx2b,) = after([dx2b], red.landed("w_down"))
    dwo = _wgrad_out(ma, mr, dx2b)
    red.start(["w_out"], [dwo])
    (dx2b,) = after([dx2b], [dwo] + red.local("w_gate"))
    dmixed = _out_proj_bwd(dx2b, wout)
    dqa, dka, dva = _attn_bwd(proj, dmixed, o, lse)
    (dmixed,) = after([dmixed], [dqa] + red.local("w_out"))
    dqr, dkr, dvr, dgr = _ret_bwd(proj, ret, states, dmixed)
    dproj = jnp.concatenate([dqa, dka, dva, dqr, dkr, dvr, dgr], axis=1)
    (dproj,) = after([dproj], red.landed("w_gate") + red.landed("w_out"))
    dwi = _wgrad_in(h1, dproj)
    red.start(["w_in"], [dwi])
    (dproj,) = after([dproj], [dwi])
    gx, st1 = _in_proj_bwd(dproj, win, dx2, x, r1, nw1)
    red.local("w_in")
    stats = jnp.concatenate([st1[0:1], st2[0:1], st3[0:2], jnp.zeros((4, D), F32)], axis=0)
    return stats, gx, dwi, dwo, dwg, dwu, dwd


def _place():
    x, y, c = lax.axis_index("x"), lax.axis_index("y"), lax.axis_index("c")
    return x, y, c, [(1 - x, y), (x, 1 - y), (1 - x, 1 - y)]


def _handshake(peers):
    barrier = pltpu.get_barrier_semaphore()
    for peer in peers:
        pl.semaphore_signal(barrier, inc=1, device_id=peer, device_id_type=MESH)
    pl.semaphore_wait(barrier, len(peers))


def _all_gather(shards, name, collective_id):
    na = len(shards)

    def body(*refs):
        ins, outs = refs[:na], refs[na:2 * na]
        send_sems, recv_sems, local_sems = refs[2 * na:]
        x, y, c, chips = _place()
        sib = (x, y, 1 - c)
        _handshake([sib] + [(*chip, c) for chip in chips])

        def copy(a, k, block, to, src=None):
            idx = 4 * block[0] + 2 * block[1] + block[2]
            return pltpu.make_async_remote_copy(
                src_ref=outs[a].at[idx] if src is None else src, dst_ref=outs[a].at[idx],
                send_sem=send_sems.at[a, k], recv_sem=recv_sems.at[a, k], device_id=to, device_id_type=MESH)

        me = (x, y, c)
        mine = [pltpu.make_async_copy(ins[a], outs[a].at[4 * x + 2 * y + c], local_sems.at[a]) for a in range(na)]
        for cp in mine:
            cp.start()
        first = []
        for a in range(na):
            first += [copy(a, 1 + j, me, (*chip, c), src=ins[a]) for j, chip in enumerate(chips)]
        for a in range(na):
            first.append(copy(a, 0, me, sib, src=ins[a]))
        for cp in first:
            cp.start()
        passed = []
        for a in range(na):
            for j, chip in enumerate(chips):
                copy(a, 1 + j, (*chip, c), me).wait_recv()
                fw = copy(a, 4 + j, (*chip, c), sib)
                fw.start()
                passed.append(fw)
        for a in range(na):
            copy(a, 0, (x, y, 1 - c), me).wait_recv()
            for j, chip in enumerate(chips):
                copy(a, 4 + j, (*chip, 1 - c), me).wait_recv()
        for cp in first + passed:
            cp.wait_send()
        for cp in mine:
            cp.wait()

    return pl.kernel(
        body, name=name,
        out_type=[jax.ShapeDtypeStruct((NDEV,) + s.shape, s.dtype) for s in shards],
        mesh=plsc.ScalarSubcoreMesh(axis_name="sequencer", num_cores=1),
        scratch_types=[pltpu.SemaphoreType.DMA((na, 7)), pltpu.SemaphoreType.DMA((na, 7)),
                       pltpu.SemaphoreType.DMA((na,))],
        compiler_params=pltpu.CompilerParams(collective_id=collective_id),
    )(*shards)


def _sequencer_call(body, name, collective_id, out_type, scratch_types):
    return pl.kernel(
        body, name=name, out_type=out_type,
        mesh=plsc.ScalarSubcoreMesh(axis_name="sequencer", num_cores=1),
        scratch_types=scratch_types,
        compiler_params=pltpu.CompilerParams(collective_id=collective_id))


def _exchange_sibling(grads, name, collective_id):
    na = len(grads)

    def body(*refs):
        ins, outs = refs[:na], refs[na:2 * na]
        send_sems, recv_sems = refs[2 * na:]
        x, y, c, _ = _place()
        _handshake([(x, y, 1 - c)])
        cps = []
        for a in range(na):
            for k in range(4):
                cps.append(pltpu.make_async_remote_copy(
                    src_ref=ins[a].at[2 * k + (1 - c)], dst_ref=outs[a].at[k],
                    send_sem=send_sems.at[a, k], recv_sem=recv_sems.at[a, k],
                    device_id=(x, y, 1 - c), device_id_type=MESH))
        for cp in cps:
            cp.start()
        for cp in cps:
            cp.wait()

    return _sequencer_call(
        body, name, collective_id,
        [jax.ShapeDtypeStruct((4,) + g.shape[1:], g.dtype) for g in grads],
        [pltpu.SemaphoreType.DMA((na, 4)), pltpu.SemaphoreType.DMA((na, 4))])(*grads)


def _row_tile(rows, cols):
    for t in (512, 256, 176, 128, 64, 32, 16):
        if rows % t == 0 and t * cols * 4 <= (1 << 20):
            return t
    raise ValueError((rows, cols))


def _chip_sum(place, g, got, name):
    _, r, c = g.shape
    tm = _row_tile(r, c)

    def body(pos_ref, g_ref, got_ref, o_ref):
        o_ref[...] = (g_ref[...].astype(F32) + got_ref[...].astype(F32)).astype(BF16)

    return pl.pallas_call(
        body, name=name,
        grid_spec=pltpu.PrefetchScalarGridSpec(
            num_scalar_prefetch=1, grid=(4, r // tm),
            in_specs=[pl.BlockSpec((None, tm, c), lambda k, i, pos: (2 * k + pos[2], i, 0)),
                      pl.BlockSpec((None, tm, c), lambda k, i, pos: (k, i, 0))],
            out_specs=pl.BlockSpec((None, tm, c), lambda k, i, pos: (k, i, 0))),
        out_shape=jax.ShapeDtypeStruct((4, r, c), BF16),
        compiler_params=_cp(("parallel", "parallel")),
    )(place, g, got)


def _exchange_chips(sums, name, collective_id):
    na = len(sums)

    def body(*refs):
        ins, outs = refs[:na], refs[na:2 * na]
        send_sems, recv_sems = refs[2 * na:]
        x, y, c, chips = _place()
        _handshake([(*chip, c) for chip in chips])
        cps = []
        for a in range(na):
            for j, chip in enumerate(chips):
                cps.append(pltpu.make_async_remote_copy(
                    src_ref=ins[a].at[2 * chip[0] + chip[1]], dst_ref=outs[a].at[j],
                    send_sem=send_sems.at[a, j], recv_sem=recv_sems.at[a, j],
                    device_id=(*chip, c), device_id_type=MESH))
        for cp in cps:
            cp.start()
        for cp in cps:
            cp.wait()

    return _sequencer_call(
        body, name, collective_id,
        [jax.ShapeDtypeStruct((3,) + s.shape[1:], s.dtype) for s in sums],
        [pltpu.SemaphoreType.DMA((na, 3)), pltpu.SemaphoreType.DMA((na, 3))])(*sums)


def _exchange_stats(stats, collective_id):
    def body(st_in, st_out, st_send, st_recv, local_sem):
        x, y, c, _ = _place()
        me_idx = 4 * x + 2 * y + c
        peers = [(x ^ ((k >> 2) & 1), y ^ ((k >> 1) & 1), c ^ (k & 1)) for k in range(1, 8)]
        _handshake(peers)
        mine = pltpu.make_async_copy(st_in, st_out.at[me_idx], local_sem)
        mine.start()
        cps = [pltpu.make_async_remote_copy(
            src_ref=st_in, dst_ref=st_out.at[me_idx], send_sem=st_send.at[k], recv_sem=st_recv.at[k],
            device_id=peer, device_id_type=MESH) for k, peer in enumerate(peers)]
        for cp in cps:
            cp.start()
        for cp in cps:
            cp.wait()
        mine.wait()

    return _sequencer_call(
        body, "exchange_stats", collective_id,
        jax.ShapeDtypeStruct((NDEV,) + stats.shape, stats.dtype),
        [pltpu.SemaphoreType.DMA((7,)), pltpu.SemaphoreType.DMA((7,)), pltpu.SemaphoreType.DMA])(stats)


class _Reduction:
    def __init__(self, place, first_collective_id):
        self.place = place
        self.ids = iter(range(first_collective_id, 32))
        self.groups = {}

    def next_id(self):
        return next(self.ids)

    def start(self, group, grads):
        got = _exchange_sibling(grads, "sibling_exchange_" + group[0], self.next_id())
        self.groups[group[0]] = dict(names=group, grads=grads, got=got)

    def local(self, name):
        grp = self.groups[name]
        grp["sums"] = [_chip_sum(self.place, g, s, "chip_sum_" + n)
                       for g, s, n in zip(grp["grads"], grp["got"], grp["names"])]
        grp["chips"] = _exchange_chips(grp["sums"], "chip_exchange_" + name, self.next_id())
        return grp["sums"]

    def landed(self, name):
        return list(self.groups[name]["chips"])

    def parts(self, name):
        for grp in self.groups.values():
            if name in grp["names"]:
                k = grp["names"].index(name)
                return grp["grads"][k], grp["got"][k], grp["chips"][k]
        raise KeyError(name)


def _adamw(w, g, m, v):
    m = ADAM_B1 * m + (1.0 - ADAM_B1) * g
    v = ADAM_B2 * v + (1.0 - ADAM_B2) * (g * g)
    m_hat = m / (1.0 - ADAM_B1 ** ADAM_STEP)
    v_hat = v / (1.0 - ADAM_B2 ** ADAM_STEP)
    delta = -ADAM_LR * (m_hat / (jnp.sqrt(v_hat) + ADAM_EPS) + ADAM_WD * w)
    return delta, m, v


def _shard_update(place, w, m, v, g, got_sib, got_chips, name):
    r, c = w.shape
    tm = _row_tile(r, c)

    def body(pos_ref, w_ref, m_ref, v_ref, g_ref, s_ref, c_ref, go_ref, d_ref, mo_ref, vo_ref):
        grad = g_ref[...].astype(F32) + s_ref[...].astype(F32)
        for j in range(3):
            grad = grad + c_ref[j].astype(F32)
        delta, mn, vn = _adamw(w_ref[...], grad, m_ref[...], v_ref[...])
        go_ref[...] = grad
        d_ref[...] = delta
        mo_ref[...] = mn
        vo_ref[...] = vn

    row = pl.BlockSpec((tm, c), lambda i, pos: (i, 0))
    return pl.pallas_call(
        body, name=name,
        grid_spec=pltpu.PrefetchScalarGridSpec(
            num_scalar_prefetch=1, grid=(r // tm,),
            in_specs=[row, row, row,
                      pl.BlockSpec((None, tm, c), lambda i, pos: (4 * pos[0] + 2 * pos[1] + pos[2], i, 0)),
                      pl.BlockSpec((None, tm, c), lambda i, pos: (2 * pos[0] + pos[1], i, 0)),
                      pl.BlockSpec((3, tm, c), lambda i, pos: (0, i, 0))],
            out_specs=[row, row, row, row]),
        out_shape=[jax.ShapeDtypeStruct((r, c), F32)] * 4,
        compiler_params=_cp(("parallel",)),
    )(place, w, m, v, g, got_sib, got_chips)


def _small_update(stats_all, ws, ms, vs):
    def body(st_ref, w_ref, m_ref, v_ref, go_ref, d_ref, mo_ref, vo_ref):
        grad = st_ref[0]
        for k in range(1, NDEV):
            grad = grad + st_ref[k]
        delta, mn, vn = _adamw(w_ref[...], grad, m_ref[...], v_ref[...])
        go_ref[...] = grad
        d_ref[...] = delta
        mo_ref[...] = mn
        vo_ref[...] = vn

    return pl.pallas_call(
        body, name="small_update",
        out_shape=[jax.ShapeDtypeStruct((8, D), F32)] * 4,
        compiler_params=_cp(),
    )(stats_all, ws, ms, vs)


def kernel(x, norm_mix_w, w_in, w_out, norm_ffn_w, w_gate, w_up, w_down, norm_final_w, loss_target, m_norm_mix_w, m_w_in, m_w_out, m_norm_ffn_w, m_w_gate, m_w_up, m_w_down, m_norm_final_w, v_norm_mix_w, v_w_in, v_w_out, v_norm_ffn_w, v_w_gate, v_w_up, v_w_down, v_norm_final_w):
    tr = {"w_gate", "w_up"}
    names = ["w_in", "w_out", "w_gate", "w_up", "w_down"]

    def view(a, n):
        return a[0].T if n in tr else a[0]

    big_w = [view(a, n) for a, n in zip([w_in, w_out, w_gate, w_up, w_down], names)]
    big_m = [view(a, n) for a, n in zip([m_w_in, m_w_out, m_w_gate, m_w_up, m_w_down], names)]
    big_v = [view(a, n) for a, n in zip([v_w_in, v_w_out, v_w_gate, v_w_up, v_w_down], names)]

    shards = [_cast_bf16(w, "cast_" + n) for w, n in zip(big_w, names)]
    (win,) = _all_gather(shards[0:1], "all_gather_w_in", 1)
    wout, wg, wu, wd = _all_gather(shards[1:], "all_gather_rest", 2)
    nw3 = norm_final_w.reshape(1, D)
    place = jnp.stack([lax.axis_index("x"), lax.axis_index("y"), lax.axis_index("c")]).astype(jnp.int32)
    red = _Reduction(place, first_collective_id=3)
    stats, gx, *_ = _local_step(
        x[0], loss_target[0], norm_mix_w, norm_ffn_w, nw3, win, wout.reshape(D, D), wg, wu, wd, red)
    stats_all = _exchange_stats(stats, red.next_id())
    upd = [_shard_update(place, w, m, v, *red.parts(n), "update_" + n)
           for w, m, v, n in zip(big_w, big_m, big_v, names)]

    def rows(a, b, c):
        return jnp.concatenate([a.reshape(1, D), b.reshape(1, D), c.reshape(1, D), jnp.zeros((5, D), F32)], axis=0)

    sg, sd, sm, sv = _small_update(stats_all, rows(norm_mix_w, norm_ffn_w, norm_final_w),
                                   rows(m_norm_mix_w, m_norm_ffn_w, m_norm_final_w),
                                   rows(v_norm_mix_w, v_norm_ffn_w, v_norm_final_w))
    loss = sg[3, 0]

    def outs(k, small):
        big = [(u[k].T if n in tr else u[k])[None] for u, n in zip(upd, names)]
        return [small[0:1], big[0], big[1], small[1:2], big[2], big[3], big[4], small[2]]

    return (loss, gx[None], *outs(0, sg), *outs(1, sd), *outs(2, sm), *outs(3, sv))
```

```python
import functools
import math

import numpy as np
import jax
import jax.numpy as jnp
from jax import lax
from jax.experimental import pallas as pl
from jax.experimental.pallas import tpu as pltpu
from jax.experimental.pallas import tpu_sc as plsc

F32 = jnp.float32
BF16 = jnp.bfloat16

S = 2048
D = 2048
NDEV = 8
N_IN = 7168 // NDEV
N_FF = 5632 // NDEV
N_OUT = 2048 // NDEV
AH, AHD = 8, 128
RH, RHD = 4, 256
CH = 128
NB = S // CH
EPS = 1e-6
PATTERNS = ((1, 16), (4, 4), (16, 1))
NEG = -1e30
VMEM_LIMIT = 56 * 1024 * 1024

ADAM_LR, ADAM_B1, ADAM_B2, ADAM_EPS, ADAM_WD, ADAM_STEP = 0.001, 0.9, 0.999, 1e-08, 0.01, 10
MESH = pl.DeviceIdType.MESH


def _cp(sem=None):
    return pltpu.CompilerParams(dimension_semantics=sem, vmem_limit_bytes=VMEM_LIMIT)


def _dot(a, b):
    return jnp.dot(a, b, preferred_element_type=F32)


def _dot_nt(a, b):
    return lax.dot_general(a, b, (((1,), (1,)), ((), ())), preferred_element_type=F32)


def _dot_tn(a, b):
    return lax.dot_general(a, b, (((0,), (0,)), ((), ())), preferred_element_type=F32)


def _sigmoid(x):
    return 1.0 / (1.0 + jnp.exp(-x))


def _cast_bf16(w, name):
    r, c = w.shape
    tm = r if r <= 1024 else 512

    def body(w_ref, o_ref):
        o_ref[...] = w_ref[...].astype(BF16)

    return pl.pallas_call(
        body, name=name, grid=(r // tm,),
        in_specs=[pl.BlockSpec((tm, c), lambda i: (i, 0))],
        out_specs=pl.BlockSpec((tm, c), lambda i: (i, 0)),
        out_shape=jax.ShapeDtypeStruct((r, c), BF16),
        compiler_params=_cp(("parallel",)),
    )(w)


def _rms_fwd(x, nw):
    tm = 256

    def body(x_ref, w_ref, h_ref, r_ref):
        xs = x_ref[...]
        r = lax.rsqrt(jnp.mean(xs * xs, axis=-1, keepdims=True) + EPS)
        h_ref[...] = ((xs * r) * w_ref[...]).astype(BF16)
        r_ref[...] = r

    return pl.pallas_call(
        body, name="rms_fwd", grid=(S // tm,),
        in_specs=[pl.BlockSpec((tm, D), lambda i: (i, 0)), pl.BlockSpec((1, D), lambda i: (0, 0))],
        out_specs=[pl.BlockSpec((tm, D), lambda i: (i, 0)), pl.BlockSpec((tm, 1), lambda i: (i, 0))],
        out_shape=[jax.ShapeDtypeStruct((S, D), BF16), jax.ShapeDtypeStruct((S, 1), F32)],
        compiler_params=_cp(("parallel",)),
    )(x, nw)


def _rms_bwd_tile(dh, xs, r, nw):
    dnw = jnp.sum(dh * (xs * r), axis=0, keepdims=True)
    gy = dh * nw
    dx = r * gy - xs * ((r * r * r) * jnp.mean(gy * xs, axis=-1, keepdims=True))
    return dx, dnw


def _proj(h1, win):
    tm = 512

    def body(a_ref, w_ref, o_ref):
        o_ref[...] = _dot(a_ref[...], w_ref[...])

    return pl.pallas_call(
        body, name="proj", grid=(NDEV, S // tm),
        in_specs=[pl.BlockSpec((tm, D), lambda p, m: (m, 0)),
                  pl.BlockSpec((None, D, N_IN), lambda p, m: (p, 0, 0))],
        out_specs=pl.BlockSpec((tm, N_IN), lambda p, m: (m, p)),
        out_shape=jax.ShapeDtypeStruct((S, NDEV * N_IN), F32),
        compiler_params=_cp(("parallel", "parallel")),
    )(h1, win)


def _out_proj_rms(x, ma, mr, wout, nw):
    tm = 256
    half = D // 2

    def body(x_ref, ma_ref, mr_ref, w_ref, nw_ref, x2_ref, h_ref, r_ref):
        acc = _dot(ma_ref[...], w_ref[0:half, :]) + _dot(mr_ref[...], w_ref[half:D, :])
        x2 = x_ref[...] + acc
        r = lax.rsqrt(jnp.mean(x2 * x2, axis=-1, keepdims=True) + EPS)
        x2_ref[...] = x2
        h_ref[...] = ((x2 * r) * nw_ref[...]).astype(BF16)
        r_ref[...] = r

    return pl.pallas_call(
        body, name="out_proj_rms", grid=(S // tm,),
        in_specs=[pl.BlockSpec((tm, D), lambda i: (i, 0)),
                  pl.BlockSpec((tm, half), lambda i: (i, 0)),
                  pl.BlockSpec((tm, half), lambda i: (i, 0)),
                  pl.BlockSpec((D, D), lambda i: (0, 0)),
                  pl.BlockSpec((1, D), lambda i: (0, 0))],
        out_specs=[pl.BlockSpec((tm, D), lambda i: (i, 0)), pl.BlockSpec((tm, D), lambda i: (i, 0)),
                   pl.BlockSpec((tm, 1), lambda i: (i, 0))],
        out_shape=[jax.ShapeDtypeStruct((S, D), F32), jax.ShapeDtypeStruct((S, D), BF16),
                   jax.ShapeDtypeStruct((S, 1), F32)],
        compiler_params=_cp(("parallel",)),
    )(x, ma, mr, wout, nw)


def _ffn_up(h2, wg, wu):
    tm = 512

    def body(h_ref, wg_ref, wu_ref, g_ref, u_ref, a_ref):
        h = h_ref[...]
        g = _dot_nt(h, wg_ref[...])
        u = _dot_nt(h, wu_ref[...])
        g_ref[...] = g
        u_ref[...] = u
        a_ref[...] = ((g * _sigmoid(g)) * u).astype(BF16)

    blk = pl.BlockSpec((None, tm, N_FF), lambda p, m: (p, m, 0))
    wblk = pl.BlockSpec((None, N_FF, D), lambda p, m: (p, 0, 0))
    return pl.pallas_call(
        body, name="ffn_up", grid=(NDEV, S // tm),
        in_specs=[pl.BlockSpec((tm, D), lambda p, m: (m, 0)), wblk, wblk],
        out_specs=[blk, blk, blk],
        out_shape=[jax.ShapeDtypeStruct((NDEV, S, N_FF), F32), jax.ShapeDtypeStruct((NDEV, S, N_FF), F32),
                   jax.ShapeDtypeStruct((NDEV, S, N_FF), BF16)],
        compiler_params=_cp(("parallel", "parallel")),
    )(h2, wg, wu)


def _ffn_down_loss(x2, a, wd, nw, tgt):
    tm = 512

    def body(x2_ref, a_ref, w_ref, nw_ref, t_ref, dx_ref, dxb_ref, st_ref, acc_ref):
        m, p = pl.program_id(0), pl.program_id(1)

        @pl.when(p == 0)
        def _():
            acc_ref[...] = jnp.zeros_like(acc_ref)

        @pl.when((p == 0) & (m == 0))
        def _():
            st_ref[...] = jnp.zeros_like(st_ref)

        acc_ref[...] += _dot(a_ref[...], w_ref[...])

        @pl.when(p == NDEV - 1)
        def _():
            x3 = x2_ref[...] + acc_ref[...]
            nwv = nw_ref[...]
            r = lax.rsqrt(jnp.mean(x3 * x3, axis=-1, keepdims=True) + EPS)
            y = (x3 * r) * nwv
            err = y - t_ref[...]
            loss = 0.5 * jnp.sum(jnp.mean(err * err, axis=-1, keepdims=True), axis=0, keepdims=True)
            dy = err * (1.0 / D)
            dx, dnw = _rms_bwd_tile(dy, x3, r, nwv)
            dx_ref[...] = dx
            dxb_ref[...] = dx.astype(BF16)
            st_ref[0:1, :] += dnw
            st_ref[1:2, :] += jnp.broadcast_to(loss, (1, D))

    return pl.pallas_call(
        body, name="ffn_down_loss", grid=(S // tm, NDEV),
        in_specs=[pl.BlockSpec((tm, D), lambda m, p: (m, 0)),
                  pl.BlockSpec((None, tm, N_FF), lambda m, p: (p, m, 0)),
                  pl.BlockSpec((None, N_FF, D), lambda m, p: (p, 0, 0)),
                  pl.BlockSpec((1, D), lambda m, p: (0, 0)),
                  pl.BlockSpec((tm, D), lambda m, p: (m, 0))],
        out_specs=[pl.BlockSpec((tm, D), lambda m, p: (m, 0)), pl.BlockSpec((tm, D), lambda m, p: (m, 0)),
                   pl.BlockSpec((8, D), lambda m, p: (0, 0))],
        out_shape=[jax.ShapeDtypeStruct((S, D), F32), jax.ShapeDtypeStruct((S, D), BF16),
                   jax.ShapeDtypeStruct((8, D), F32)],
        scratch_shapes=[pltpu.VMEM((tm, D), F32)],
        compiler_params=_cp(("arbitrary", "arbitrary")),
    )(x2, a, wd, nw, tgt)


def _ffn_down_bwd(dx3b, wd, g, u):
    tm = 512

    def body(dx_ref, w_ref, g_ref, u_ref, dg_ref, du_ref):
        da = _dot_nt(dx_ref[...], w_ref[...])
        gv = g_ref[...]
        sg = _sigmoid(gv)
        silu = gv * sg
        dg_ref[...] = ((da * u_ref[...]) * (sg * (1.0 + gv * (1.0 - sg)))).astype(BF16)
        du_ref[...] = (da * silu).astype(BF16)

    blk = pl.BlockSpec((None, tm, N_FF), lambda p, m: (p, m, 0))
    return pl.pallas_call(
        body, name="ffn_down_bwd", grid=(NDEV, S // tm),
        in_specs=[pl.BlockSpec((tm, D), lambda p, m: (m, 0)),
                  pl.BlockSpec((None, N_FF, D), lambda p, m: (p, 0, 0)), blk, blk],
        out_specs=[blk, blk],
        out_shape=[jax.ShapeDtypeStruct((NDEV, S, N_FF), BF16), jax.ShapeDtypeStruct((NDEV, S, N_FF), BF16)],
        compiler_params=_cp(("parallel", "parallel")),
    )(dx3b, wd, g, u)


def _ffn_up_bwd(dg, du, wg, wu, dres, xs, r, nw):
    tm = 512

    def body(dg_ref, du_ref, wg_ref, wu_ref, dres_ref, x_ref, r_ref, nw_ref, dx_ref, dxb_ref, st_ref, acc_ref):
        m, p = pl.program_id(0), pl.program_id(1)

        @pl.when(p == 0)
        def _():
            acc_ref[...] = jnp.zeros_like(acc_ref)

        @pl.when((p == 0) & (m == 0))
        def _():
            st_ref[...] = jnp.zeros_like(st_ref)

        acc_ref[...] += _dot(dg_ref[...], wg_ref[...]) + _dot(du_ref[...], wu_ref[...])

        @pl.when(p == NDEV - 1)
        def _():
            dx, dnw = _rms_bwd_tile(acc_ref[...], x_ref[...], r_ref[...], nw_ref[...])
            dx = dres_ref[...] + dx
            dx_ref[...] = dx
            dxb_ref[...] = dx.astype(BF16)
            st_ref[0:1, :] += dnw

    blk = pl.BlockSpec((None, tm, N_FF), lambda m, p: (p, m, 0))
    wblk = pl.BlockSpec((None, N_FF, D), lambda m, p: (p, 0, 0))
    row = pl.BlockSpec((tm, D), lambda m, p: (m, 0))
    return pl.pallas_call(
        body, name="ffn_up_bwd", grid=(S // tm, NDEV),
        in_specs=[blk, blk, wblk, wblk, row, row, pl.BlockSpec((tm, 1), lambda m, p: (m, 0)),
                  pl.BlockSpec((1, D), lambda m, p: (0, 0))],
        out_specs=[row, row, pl.BlockSpec((8, D), lambda m, p: (0, 0))],
        out_shape=[jax.ShapeDtypeStruct((S, D), F32), jax.ShapeDtypeStruct((S, D), BF16),
                   jax.ShapeDtypeStruct((8, D), F32)],
        scratch_shapes=[pltpu.VMEM((tm, D), F32)],
        compiler_params=_cp(("arbitrary", "arbitrary")),
    )(dg, du, wg, wu, dres, xs, r, nw)


def _out_proj_bwd(dx2b, wout):
    tm = 256

    def body(dx_ref, w_ref, o_ref):
        o_ref[...] = _dot_nt(dx_ref[...], w_ref[...])

    return pl.pallas_call(
        body, name="out_proj_bwd", grid=(S // tm,),
        in_specs=[pl.BlockSpec((tm, D), lambda i: (i, 0)), pl.BlockSpec((D, D), lambda i: (0, 0))],
        out_specs=pl.BlockSpec((tm, D), lambda i: (i, 0)),
        out_shape=jax.ShapeDtypeStruct((S, D), F32),
        compiler_params=_cp(("parallel",)),
    )(dx2b, wout)


def _in_proj_bwd(dproj, win, dres, xs, r, nw):
    tm = 512

    def body(dp_ref, w_ref, dres_ref, x_ref, r_ref, nw_ref, dx_ref, st_ref, acc_ref):
        m, p = pl.program_id(0), pl.program_id(1)

        @pl.when(p == 0)
        def _():
            acc_ref[...] = jnp.zeros_like(acc_ref)

        @pl.when((p == 0) & (m == 0))
        def _():
            st_ref[...] = jnp.zeros_like(st_ref)

        acc_ref[...] += _dot_nt(dp_ref[...], w_ref[...])

        @pl.when(p == NDEV - 1)
        def _():
            dx, dnw = _rms_bwd_tile(acc_ref[...], x_ref[...], r_ref[...], nw_ref[...])
            dx_ref[...] = dres_ref[...] + dx
            st_ref[0:1, :] += dnw

    row = pl.BlockSpec((tm, D), lambda m, p: (m, 0))
    return pl.pallas_call(
        body, name="in_proj_bwd", grid=(S // tm, NDEV),
        in_specs=[pl.BlockSpec((tm, N_IN), lambda m, p: (m, p)),
                  pl.BlockSpec((None, D, N_IN), lambda m, p: (p, 0, 0)),
                  row, row, pl.BlockSpec((tm, 1), lambda m, p: (m, 0)),
                  pl.BlockSpec((1, D), lambda m, p: (0, 0))],
        out_specs=[row, pl.BlockSpec((8, D), lambda m, p: (0, 0))],
        out_shape=[jax.ShapeDtypeStruct((S, D), F32), jax.ShapeDtypeStruct((8, D), F32)],
        scratch_shapes=[pltpu.VMEM((tm, D), F32)],
        compiler_params=_cp(("arbitrary", "arbitrary")),
    )(dproj, win, dres, xs, r, nw)


def _wgrad_in(h1, dproj):
    def body(a_ref, d_ref, o_ref):
        o_ref[...] = _dot_tn(a_ref[...], d_ref[...]).astype(BF16)

    return pl.pallas_call(
        body, name="wgrad_in", grid=(NDEV,),
        in_specs=[pl.BlockSpec((S, D), lambda p: (0, 0)), pl.BlockSpec((S, N_IN), lambda p: (0, p))],
        out_specs=pl.BlockSpec((None, D, N_IN), lambda p: (p, 0, 0)),
        out_shape=jax.ShapeDtypeStruct((NDEV, D, N_IN), BF16),
        compiler_params=_cp(("parallel",)),
    )(h1, dproj)


def _wgrad_rows(a3, dy, name):
    def body(a_ref, d_ref, o_ref):
        o_ref[...] = _dot_tn(a_ref[...], d_ref[...]).astype(BF16)

    return pl.pallas_call(
        body, name=name, grid=(NDEV,),
        in_specs=[pl.BlockSpec((None, S, N_FF), lambda p: (p, 0, 0)), pl.BlockSpec((S, D), lambda p: (0, 0))],
        out_specs=pl.BlockSpec((None, N_FF, D), lambda p: (p, 0, 0)),
        out_shape=jax.ShapeDtypeStruct((NDEV, N_FF, D), BF16),
        compiler_params=_cp(("parallel",)),
    )(a3, dy)


def _wgrad_out(ma, mr, dx2b):
    half = D // 2
    per = half // N_OUT

    def body(ma_ref, mr_ref, d_ref, o_ref):
        p = pl.program_id(0)

        @pl.when(p < per)
        def _():
            o_ref[...] = _dot_tn(ma_ref[...], d_ref[...]).astype(BF16)

        @pl.when(p >= per)
        def _():
            o_ref[...] = _dot_tn(mr_ref[...], d_ref[...]).astype(BF16)

    return pl.pallas_call(
        body, name="wgrad_out", grid=(NDEV,),
        in_specs=[pl.BlockSpec((S, N_OUT), lambda p: (0, jnp.minimum(p, per - 1))),
                  pl.BlockSpec((S, N_OUT), lambda p: (0, jnp.maximum(p - per, 0))),
                  pl.BlockSpec((S, D), lambda p: (0, 0))],
        out_specs=pl.BlockSpec((None, N_OUT, D), lambda p: (p, 0, 0)),
        out_shape=jax.ShapeDtypeStruct((NDEV, N_OUT, D), BF16),
        compiler_params=_cp(("parallel",)),
    )(ma, mr, dx2b)


def _attn_consts():
    c = np.zeros((AH, 8, AHD), np.float32)
    for h in range(AH):
        c[h, :, :] = 2.0 ** (-(h + 1))
    return jnp.asarray(c)


def _permute_in(dst, src, d, cast=None):
    ln = S // d
    for rr in range(d):
        v = src[pl.ds(rr, ln, stride=d), :] if d > 1 else src[...]
        dst[rr * ln:(rr + 1) * ln, :] = v if cast is None else v.astype(cast)


def _attn_masks():
    qi = lax.broadcasted_iota(jnp.int32, (CH, CH), 0)
    kj = lax.broadcasted_iota(jnp.int32, (CH, CH), 1)
    dist_c = (qi - kj).astype(F32)
    dist_p = (qi - kj + CH).astype(F32)
    return (qi >= kj)[None], (kj >= qi)[None], dist_c[None], dist_p[None]


GB = 8


def _bdot_nt(a, b):
    return lax.dot_general(a, b, (((2,), (2,)), ((0,), (0,))), preferred_element_type=F32)


def _bdot(a, b):
    return lax.dot_general(a, b, (((2,), (1,)), ((0,), (0,))), preferred_element_type=F32)


def _bdot_tn(a, b):
    return lax.dot_general(a, b, (((1,), (1,)), ((0,), (0,))), preferred_element_type=F32)


def _shift_block(dst, src):
    dst[0:CH, :] = jnp.zeros((CH, AHD), dst.dtype)
    dst[CH:S, :] = src[0:S - CH, :]


def _has_prev(g, nb):
    blk = lax.broadcasted_iota(jnp.int32, (GB, 1, 1), 0) + g * GB
    return (blk & (nb - 1)) != 0


def _blocks(ref, g):
    return ref[g * GB * CH:(g + 1) * GB * CH, :].reshape(GB, CH, AHD)


def _attn_fwd(proj):
    scale = 1.0 / math.sqrt(AHD)

    def body(c_ref, q_ref, k_ref, v_ref, o_ref, ob_ref, lse_ref, qd, kd, vd, kps, vps, od, ld, *nat):
        onat, lnat = nat[0:3], nat[3:6]
        slope = c_ref[0:1, :]
        mask_c, mask_p, dist_c, dist_p = _attn_masks()
        for pi, (d, nb) in enumerate(PATTERNS):
            _permute_in(qd, q_ref, d, BF16)
            _permute_in(kd, k_ref, d, BF16)
            _permute_in(vd, v_ref, d, BF16)
            if nb > 1:
                _shift_block(kps, kd)
                _shift_block(vps, vd)
            bias_c = -(slope * float(d)) * dist_c
            bias_p = -(slope * float(d)) * dist_p
            for g in range(NB // GB):
                q3, k3, v3 = _blocks(qd, g), _blocks(kd, g), _blocks(vd, g)
                s_c = jnp.where(mask_c, _bdot_nt(q3, k3) * scale + bias_c, NEG)
                mx = jnp.max(s_c, axis=-1, keepdims=True)
                if nb > 1:
                    kp3, vp3 = _blocks(kps, g), _blocks(vps, g)
                    s_p = jnp.where(jnp.logical_and(mask_p, _has_prev(g, nb)),
                                    _bdot_nt(q3, kp3) * scale + bias_p, NEG)
                    mx = jnp.maximum(mx, jnp.max(s_p, axis=-1, keepdims=True))
                    l = (jnp.sum(jnp.exp(s_c - mx), axis=-1, keepdims=True)
                         + jnp.sum(jnp.exp(s_p - mx), axis=-1, keepdims=True))
                    lse = mx + jnp.log(l)
                    o3 = _bdot(jnp.exp(s_c - lse).astype(BF16), v3) + _bdot(jnp.exp(s_p - lse).astype(BF16), vp3)
                else:
                    l = jnp.sum(jnp.exp(s_c - mx), axis=-1, keepdims=True)
                    lse = mx + jnp.log(l)
                    o3 = _bdot(jnp.exp(s_c - lse).astype(BF16), v3)
                rows = slice(g * GB * CH, (g + 1) * GB * CH)
                od[rows, :] = o3.reshape(GB * CH, AHD)
                ld[rows, :] = jnp.broadcast_to(lse, (GB, CH, AHD)).reshape(GB * CH, AHD)
            ln = S // d
            for rr in range(d):
                if d > 1:
                    onat[pi][pl.ds(rr, ln, stride=d), :] = od[rr * ln:(rr + 1) * ln, :]
                    lnat[pi][pl.ds(rr, ln, stride=d), :] = ld[rr * ln:(rr + 1) * ln, :]
                else:
                    onat[pi][...] = od[...]
                    lnat[pi][...] = ld[...]
        l0, l1, l2 = lnat[0][...], lnat[1][...], lnat[2][...]
        mx = jnp.maximum(jnp.maximum(l0, l1), l2)
        e0, e1, e2 = jnp.exp(l0 - mx), jnp.exp(l1 - mx), jnp.exp(l2 - mx)
        den = e0 + e1 + e2
        out = (e0 / den) * onat[0][...] + (e1 / den) * onat[1][...] + (e2 / den) * onat[2][...]
        o_ref[...] = out
        ob_ref[...] = out.astype(BF16)
        lse_ref[...] = mx + jnp.log(den)

    def col(off):
        return pl.BlockSpec((S, AHD), lambda h: (0, off + h))

    return pl.pallas_call(
        body, name="attn_fwd", grid=(AH,),
        in_specs=[pl.BlockSpec((None, 8, AHD), lambda h: (h, 0, 0)), col(0), col(AH), col(2 * AH)],
        out_specs=[col(0), col(0), col(0)],
        out_shape=[jax.ShapeDtypeStruct((S, AH * AHD), F32), jax.ShapeDtypeStruct((S, AH * AHD), BF16),
                   jax.ShapeDtypeStruct((S, AH * AHD), F32)],
        scratch_shapes=[pltpu.VMEM((S, AHD), BF16) for _ in range(5)]
        + [pltpu.VMEM((S, AHD), F32) for _ in range(8)],
        compiler_params=_cp(("parallel",)),
    )(_attn_consts(), proj, proj, proj)


def _attn_bwd(proj, dmixed, o, lse):
    scale = 1.0 / math.sqrt(AHD)

    def body(c_ref, q_ref, k_ref, v_ref, do_ref, o_ref, lse_ref, dq_ref, dk_ref, dv_ref,
             qd, kd, vd, dod, kps, vps, lsd, dld, dqd, dkd, dvd, delta, aq, ak, av):
        slope = c_ref[0:1, :]
        mask_c, mask_p, dist_c, dist_p = _attn_masks()
        delta[...] = jnp.broadcast_to(jnp.sum(do_ref[...] * o_ref[...], axis=-1, keepdims=True), (S, AHD))
        for pi, (d, nb) in enumerate(PATTERNS):
            _permute_in(qd, q_ref, d, BF16)
            _permute_in(kd, k_ref, d, BF16)
            _permute_in(vd, v_ref, d, BF16)
            _permute_in(dod, do_ref, d, BF16)
            _permute_in(lsd, lse_ref, d)
            _permute_in(dld, delta, d)
            if nb > 1:
                _shift_block(kps, kd)
                _shift_block(vps, vd)
            bias_c = -(slope * float(d)) * dist_c
            bias_p = -(slope * float(d)) * dist_p
            for g in range(NB // GB):
                q3, k3, v3, do3 = _blocks(qd, g), _blocks(kd, g), _blocks(vd, g), _blocks(dod, g)
                ls, dl = _blocks(lsd, g), _blocks(dld, g)
                lo, hi = g * GB * CH, (g + 1) * GB * CH
                p_c = jnp.exp(jnp.where(mask_c, _bdot_nt(q3, k3) * scale + bias_c, NEG) - ls)
                ds_c = ((p_c * (_bdot_nt(do3, v3) - dl)) * scale).astype(BF16)
                dq3 = _bdot(ds_c, k3)
                dkd[lo:hi, :] = _bdot_tn(ds_c, q3).reshape(GB * CH, AHD)
                dvd[lo:hi, :] = _bdot_tn(p_c.astype(BF16), do3).reshape(GB * CH, AHD)
                if nb > 1:
                    kp3, vp3 = _blocks(kps, g), _blocks(vps, g)
                    p_p = jnp.exp(jnp.where(jnp.logical_and(mask_p, _has_prev(g, nb)),
                                            _bdot_nt(q3, kp3) * scale + bias_p, NEG) - ls)
                    ds_p = ((p_p * (_bdot_nt(do3, vp3) - dl)) * scale).astype(BF16)
                    dq3 = dq3 + _bdot(ds_p, kp3)
                    dkp = _bdot_tn(ds_p, q3).reshape(GB * CH, AHD)
                    dvp = _bdot_tn(p_p.astype(BF16), do3).reshape(GB * CH, AHD)
                    if g == 0:
                        dkd[0:hi - CH, :] += dkp[CH:, :]
                        dvd[0:hi - CH, :] += dvp[CH:, :]
                    else:
                        dkd[lo - CH:hi - CH, :] += dkp
                        dvd[lo - CH:hi - CH, :] += dvp
                dqd[lo:hi, :] = dq3.reshape(GB * CH, AHD)
            ln = S // d
            for acc, src in ((aq, dqd), (ak, dkd), (av, dvd)):
                if pi == 0:
                    acc[...] = src[...]
                else:
                    for rr in range(d):
                        acc[pl.ds(rr, ln, stride=d), :] += src[rr * ln:(rr + 1) * ln, :]
        dq_ref[...] = aq[...].astype(BF16)
        dk_ref[...] = ak[...].astype(BF16)
        dv_ref[...] = av[...].astype(BF16)

    def col(off):
        return pl.BlockSpec((S, AHD), lambda h: (0, off + h))

    return pl.pallas_call(
        body, name="attn_bwd", grid=(AH,),
        in_specs=[pl.BlockSpec((None, 8, AHD), lambda h: (h, 0, 0)), col(0), col(AH), col(2 * AH),
                  col(0), col(0), col(0)],
        out_specs=[col(0), col(0), col(0)],
        out_shape=[jax.ShapeDtypeStruct((S, AH * AHD), BF16)] * 3,
        scratch_shapes=[pltpu.VMEM((S, AHD), BF16) for _ in range(6)]
        + [pltpu.VMEM((S, AHD), F32) for _ in range(9)],
        compiler_params=_cp(("parallel",)),
    )(_attn_consts(), proj, proj, proj, dmixed, o, lse)


def _ret_consts():
    c = np.zeros((RH, 8, RHD), np.float32)
    for h in range(RH):
        c[h, :, :] = np.log(np.float32(1.0) - np.float32(2.0 ** (-5.0 - h)))
    return jnp.asarray(c)


def _ret_factors(lg):
    i = lax.broadcasted_iota(jnp.int32, (CH, CH), 0)
    j = lax.broadcasted_iota(jnp.int32, (CH, CH), 1)
    dif = (i - j).astype(F32)
    decay = jnp.where(dif >= 0, jnp.exp(lg[:, 0:CH] * jnp.maximum(dif, 0.0)), 0.0)
    row = lax.broadcasted_iota(jnp.int32, (CH, RHD), 0).astype(F32)
    zeta = jnp.exp(lg * (CH - 1.0 - row))
    xi = jnp.exp(lg * (row + 1.0))
    return decay, zeta, xi, jnp.exp(lg * float(CH))


def _ret_specs(rev):
    off = 3 * AH * AHD // RHD

    def ch(n):
        return (NB - 1 - n) if rev else n

    def col(k):
        return pl.BlockSpec((CH, RHD), lambda h, n: (ch(n), off + k * RH + h))

    own = pl.BlockSpec((CH, RHD), lambda h, n: (ch(n), h))
    state = pl.BlockSpec((None, None, RHD, RHD), lambda h, n: (h, ch(n), 0, 0))
    const = pl.BlockSpec((None, 8, RHD), lambda h, n: (h, 0, 0))
    return col, own, state, const


def _ret_fwd(proj):
    def body(c_ref, q_ref, k_ref, v_ref, g_ref, ret_ref, mr_ref, st_ref, r_acc):
        n = pl.program_id(1)

        @pl.when(n == 0)
        def _():
            r_acc[...] = jnp.zeros_like(r_acc)

        decay, zeta, xi, gch = _ret_factors(c_ref[0:1, :])
        qb = q_ref[...].astype(BF16)
        kc = k_ref[...] * (1.0 / math.sqrt(RHD))
        kb = kc.astype(BF16)
        vb = v_ref[...].astype(BF16)
        rb = r_acc[...].astype(BF16)
        st_ref[...] = rb
        scores = _dot_nt(qb, kb) * decay
        ret = _dot(scores.astype(BF16), vb) + _dot(qb, rb) * xi
        r_acc[...] = r_acc[...] * gch + _dot_tn((kc * zeta).astype(BF16), vb)
        ret_ref[...] = ret
        rr = lax.rsqrt(jnp.mean(ret * ret, axis=-1, keepdims=True) + EPS)
        gv = g_ref[...]
        mr_ref[...] = ((gv * _sigmoid(gv)) * (ret * rr)).astype(BF16)

    col, own, state, const = _ret_specs(False)
    return pl.pallas_call(
        body, name="ret_fwd", grid=(RH, NB),
        in_specs=[const, col(0), col(1), col(2), col(3)],
        out_specs=[own, own, state],
        out_shape=[jax.ShapeDtypeStruct((S, RH * RHD), F32), jax.ShapeDtypeStruct((S, RH * RHD), BF16),
                   jax.ShapeDtypeStruct((RH, NB, RHD, RHD), BF16)],
        scratch_shapes=[pltpu.VMEM((RHD, RHD), F32)],
        compiler_params=_cp(("parallel", "arbitrary")),
    )(_ret_consts(), proj, proj, proj, proj)


def _ret_bwd(proj, ret, states, dmixed):
    def body(c_ref, q_ref, k_ref, v_ref, g_ref, ret_ref, st_ref, dm_ref, dq_ref, dk_ref, dv_ref, dg_ref, g_acc):
        n = pl.program_id(1)

        @pl.when(n == 0)
        def _():
            g_acc[...] = jnp.zeros_like(g_acc)

        decay, zeta, xi, gch = _ret_factors(c_ref[0:1, :])
        ret_v = ret_ref[...]
        rr = lax.rsqrt(jnp.mean(ret_v * ret_v, axis=-1, keepdims=True) + EPS)
        gv = g_ref[...]
        sg = _sigmoid(gv)
        dmix = dm_ref[...]
        dg_ref[...] = ((dmix * (ret_v * rr)) * (sg * (1.0 + gv * (1.0 - sg)))).astype(BF16)
        dretn = dmix * (gv * sg)
        dret = rr * dretn - ret_v * ((rr * rr * rr) * jnp.mean(dretn * ret_v, axis=-1, keepdims=True))

        qb = q_ref[...].astype(BF16)
        kc = k_ref[...] * (1.0 / math.sqrt(RHD))
        kb = kc.astype(BF16)
        vb = v_ref[...].astype(BF16)
        rb = st_ref[...]
        db = dret.astype(BF16)
        sc = (_dot_nt(qb, kb) * decay).astype(BF16)
        da = (_dot_nt(db, vb) * decay).astype(BF16)
        dxi = (dret * xi).astype(BF16)
        gb = g_acc[...].astype(BF16)
        kz = (kc * zeta).astype(BF16)
        dq = _dot(da, kb) + _dot_nt(dxi, rb)
        dkc = _dot_tn(da, qb) + _dot_nt(vb, gb) * zeta
        dv = _dot_tn(sc, db) + _dot(kz, gb)
        g_acc[...] = _dot_tn(qb, dxi) + gch * g_acc[...]
        dq_ref[...] = dq.astype(BF16)
        dk_ref[...] = (dkc * (1.0 / math.sqrt(RHD))).astype(BF16)
        dv_ref[...] = dv.astype(BF16)

    col, own, state, const = _ret_specs(True)
    dm = pl.BlockSpec((CH, RHD), lambda h, n: (NB - 1 - n, AH * AHD // RHD + h))
    return pl.pallas_call(
        body, name="ret_bwd", grid=(RH, NB),
        in_specs=[const, col(0), col(1), col(2), col(3), own, state, dm],
        out_specs=[own, own, own, own],
        out_shape=[jax.ShapeDtypeStruct((S, RH * RHD), BF16)] * 4,
        scratch_shapes=[pltpu.VMEM((RHD, RHD), F32)],
        compiler_params=_cp(("parallel", "arbitrary")),
    )(_ret_consts(), proj, proj, proj, proj, ret, states, dmixed)


class _NoReduction:
    def start(self, group, grads):
        pass

    def local(self, name):
        return []

    def landed(self, name):
        return []


def _local_step(x, tgt, nw1, nw2, nw3, win, wout, wg, wu, wd, red=None):
    red = red or _NoReduction()

    def after(values, first):
        return lax.optimization_barrier((tuple(values), tuple(first)))[0]

    h1, r1 = _rms_fwd(x, nw1)
    proj = _proj(h1, win)
    o, ma, lse = _attn_fwd(proj)
    ret, mr, states = _ret_fwd(proj)
    x2, h2, r2 = _out_proj_rms(x, ma, mr, wout, nw2)
    g, u, a = _ffn_up(h2, wg, wu)
    dx3, dx3b, st3 = _ffn_down_loss(x2, a, wd, nw3, tgt)

    dwd = _wgrad_rows(a, dx3b, "wgrad_down")
    red.start(["w_down"], [dwd])
    (dx3b,) = after([dx3b], [dwd])
    dg, du = _ffn_down_bwd(dx3b, wd, g, u)
    dg, du = after([dg, du], red.local("w_down"))
    dwg = _wgrad_rows(dg, h2, "wgrad_gate")
    dwu = _wgrad_rows(du, h2, "wgrad_up")
    red.start(["w_gate", "w_up"], [dwg, dwu])
    dg, du = after([dg, du], [dwg, dwu])
    dx2, dx2b, st2 = _ffn_up_bwd(dg, du, wg, wu, dx3, x2, r2, nw2)
    (dx2b,) = after([dx2b], red.landed("w_down"))
    dwo = _wgrad_out(ma, mr, dx2b)
    red.start(["w_out"], [dwo])
    (dx2b,) = after([dx2b], [dwo] + red.local("w_gate"))
    dmixed = _out_proj_bwd(dx2b, wout)
    dqa, dka, dva = _attn_bwd(proj, dmixed, o, lse)
    (dmixed,) = after([dmixed], [dqa] + red.local("w_out"))
    dqr, dkr, dvr, dgr = _ret_bwd(proj, ret, states, dmixed)
    dproj = jnp.concatenate([dqa, dka, dva, dqr, dkr, dvr, dgr], axis=1)
    (dproj,) = after([dproj], red.landed("w_gate") + red.landed("w_out"))
    dwi = _wgrad_in(h1, dproj)
    red.start(["w_in"], [dwi])
    (dproj,) = after([dproj], [dwi])
    gx, st1 = _in_proj_bwd(dproj, win, dx2, x, r1, nw1)
    red.local("w_in")
    stats = jnp.concatenate([st1[0:1], st2[0:1], st3[0:2], jnp.zeros((4, D), F32)], axis=0)
    return stats, gx, dwi, dwo, dwg, dwu, dwd


def _place():
    x, y, c = lax.axis_index("x"), lax.axis_index("y"), lax.axis_index("c")
    return x, y, c, [(1 - x, y), (x, 1 - y), (1 - x, 1 - y)]


def _handshake(peers):
    barrier = pltpu.get_barrier_semaphore()
    for peer in peers:
        pl.semaphore_signal(barrier, inc=1, device_id=peer, device_id_type=MESH)
    pl.semaphore_wait(barrier, len(peers))


def _all_gather(shards, name, collective_id):
    na = len(shards)

    def body(*refs):
        ins, outs = refs[:na], refs[na:2 * na]
        send_sems, recv_sems, local_sems = refs[2 * na:]
        x, y, c, chips = _place()
        sib = (x, y, 1 - c)
        _handshake([sib] + [(*chip, c) for chip in chips])

        def copy(a, k, block, to, src=None):
            idx = 4 * block[0] + 2 * block[1] + block[2]
            return pltpu.make_async_remote_copy(
                src_ref=outs[a].at[idx] if src is None else src, dst_ref=outs[a].at[idx],
                send_sem=send_sems.at[a, k], recv_sem=recv_sems.at[a, k], device_id=to, device_id_type=MESH)

        me = (x, y, c)
        mine = [pltpu.make_async_copy(ins[a], outs[a].at[4 * x + 2 * y + c], local_sems.at[a]) for a in range(na)]
        for cp in mine:
            cp.start()
        first = []
        for a in range(na):
            first += [copy(a, 1 + j, me, (*chip, c), src=ins[a]) for j, chip in enumerate(chips)]
        for a in range(na):
            first.append(copy(a, 0, me, sib, src=ins[a]))
        for cp in first:
            cp.start()
        passed = []
        for a in range(na):
            for j, chip in enumerate(chips):
                copy(a, 1 + j, (*chip, c), me).wait_recv()
                fw = copy(a, 4 + j, (*chip, c), sib)
                fw.start()
                passed.append(fw)
        for a in range(na):
            copy(a, 0, (x, y, 1 - c), me).wait_recv()
            for j, chip in enumerate(chips):
                copy(a, 4 + j, (*chip, 1 - c), me).wait_recv()
        for cp in first + passed:
            cp.wait_send()
        for cp in mine:
            cp.wait()

    return pl.kernel(
        body, name=name,
        out_type=[jax.ShapeDtypeStruct((NDEV,) + s.shape, s.dtype) for s in shards],
        mesh=plsc.ScalarSubcoreMesh(axis_name="sequencer", num_cores=1),
        scratch_types=[pltpu.SemaphoreType.DMA((na, 7)), pltpu.SemaphoreType.DMA((na, 7)),
                       pltpu.SemaphoreType.DMA((na,))],
        compiler_params=pltpu.CompilerParams(collective_id=collective_id),
    )(*shards)


def _sequencer_call(body, name, collective_id, out_type, scratch_types):
    return pl.kernel(
        body, name=name, out_type=out_type,
        mesh=plsc.ScalarSubcoreMesh(axis_name="sequencer", num_cores=1),
        scratch_types=scratch_types,
        compiler_params=pltpu.CompilerParams(collective_id=collective_id))


def _exchange_sibling(grads, name, collective_id):
    na = len(grads)

    def body(*refs):
        ins, outs = refs[:na], refs[na:2 * na]
        send_sems, recv_sems = refs[2 * na:]
        x, y, c, _ = _place()
        _handshake([(x, y, 1 - c)])
        cps = []
        for a in range(na):
            for k in range(4):
                cps.append(pltpu.make_async_remote_copy(
                    src_ref=ins[a].at[2 * k + (1 - c)], dst_ref=outs[a].at[k],
                    send_sem=send_sems.at[a, k], recv_sem=recv_sems.at[a, k],
                    device_id=(x, y, 1 - c), device_id_type=MESH))
        for cp in cps:
            cp.start()
        for cp in cps:
            cp.wait()

    return _sequencer_call(
        body, name, collective_id,
        [jax.ShapeDtypeStruct((4,) + g.shape[1:], g.dtype) for g in grads],
        [pltpu.SemaphoreType.DMA((na, 4)), pltpu.SemaphoreType.DMA((na, 4))])(*grads)


def _row_tile(rows, cols):
    for t in (512, 256, 176, 128, 64, 32, 16):
        if rows % t == 0 and t * cols * 4 <= (1 << 20):
            return t
    raise ValueError((rows, cols))


def _chip_sum(place, g, got, name):
    _, r, c = g.shape
    tm = _row_tile(r, c)

    def body(pos_ref, g_ref, got_ref, o_ref):
        o_ref[...] = (g_ref[...].astype(F32) + got_ref[...].astype(F32)).astype(BF16)

    return pl.pallas_call(
        body, name=name,
        grid_spec=pltpu.PrefetchScalarGridSpec(
            num_scalar_prefetch=1, grid=(4, r // tm),
            in_specs=[pl.BlockSpec((None, tm, c), lambda k, i, pos: (2 * k + pos[2], i, 0)),
                      pl.BlockSpec((None, tm, c), lambda k, i, pos: (k, i, 0))],
            out_specs=pl.BlockSpec((None, tm, c), lambda k, i, pos: (k, i, 0))),
        out_shape=jax.ShapeDtypeStruct((4, r, c), BF16),
        compiler_params=_cp(("parallel", "parallel")),
    )(place, g, got)


def _exchange_chips(sums, name, collective_id):
    na = len(sums)

    def body(*refs):
        ins, outs = refs[:na], refs[na:2 * na]
        send_sems, recv_sems = refs[2 * na:]
        x, y, c, chips = _place()
        _handshake([(*chip, c) for chip in chips])
        cps = []
        for a in range(na):
            for j, chip in enumerate(chips):
                cps.append(pltpu.make_async_remote_copy(
                    src_ref=ins[a].at[2 * chip[0] + chip[1]], dst_ref=outs[a].at[j],
                    send_sem=send_sems.at[a, j], recv_sem=recv_sems.at[a, j],
                    device_id=(*chip, c), device_id_type=MESH))
        for cp in cps:
            cp.start()
        for cp in cps:
            cp.wait()

    return _sequencer_call(
        body, name, collective_id,
        [jax.ShapeDtypeStruct((3,) + s.shape[1:], s.dtype) for s in sums],
        [pltpu.SemaphoreType.DMA((na, 3)), pltpu.SemaphoreType.DMA((na, 3))])(*sums)


def _exchange_stats(stats, collective_id):
    def body(st_in, st_out, st_send, st_recv, local_sem):
        x, y, c, _ = _place()
        me_idx = 4 * x + 2 * y + c
        peers = [(x ^ ((k >> 2) & 1), y ^ ((k >> 1) & 1), c ^ (k & 1)) for k in range(1, 8)]
        _handshake(peers)
        mine = pltpu.make_async_copy(st_in, st_out.at[me_idx], local_sem)
        mine.start()
        cps = [pltpu.make_async_remote_copy(
            src_ref=st_in, dst_ref=st_out.at[me_idx], send_sem=st_send.at[k], recv_sem=st_recv.at[k],
            device_id=peer, device_id_type=MESH) for k, peer in enumerate(peers)]
        for cp in cps:
            cp.start()
        for cp in cps:
            cp.wait()
        mine.wait()

    return _sequencer_call(
        body, "exchange_stats", collective_id,
        jax.ShapeDtypeStruct((NDEV,) + stats.shape, stats.dtype),
        [pltpu.SemaphoreType.DMA((7,)), pltpu.SemaphoreType.DMA((7,)), pltpu.SemaphoreType.DMA])(stats)


class _Reduction:
    def __init__(self, place, first_collective_id):
        self.place = place
        self.ids = iter(range(first_collective_id, 32))
        self.groups = {}

    def next_id(self):
        return next(self.ids)

    def start(self, group, grads):
        got = _exchange_sibling(grads, "sibling_exchange_" + group[0], self.next_id())
        self.groups[group[0]] = dict(names=group, grads=grads, got=got)

    def local(self, name):
        grp = self.groups[name]
        grp["sums"] = [_chip_sum(self.place, g, s, "chip_sum_" + n)
                       for g, s, n in zip(grp["grads"], grp["got"], grp["names"])]
        grp["chips"] = _exchange_chips(grp["sums"], "chip_exchange_" + name, self.next_id())
        return grp["sums"]

    def landed(self, name):
        return list(self.groups[name]["chips"])

    def parts(self, name):
        for grp in self.groups.values():
            if name in grp["names"]:
                k = grp["names"].index(name)
                return grp["grads"][k], grp["got"][k], grp["chips"][k]
        raise KeyError(name)


def _adamw(w, g, m, v):
    m = ADAM_B1 * m + (1.0 - ADAM_B1) * g
    v = ADAM_B2 * v + (1.0 - ADAM_B2) * (g * g)
    m_hat = m / (1.0 - ADAM_B1 ** ADAM_STEP)
    v_hat = v / (1.0 - ADAM_B2 ** ADAM_STEP)
    delta = -ADAM_LR * (m_hat / (jnp.sqrt(v_hat) + ADAM_EPS) + ADAM_WD * w)
    return delta, m, v


def _shard_update(place, w, m, v, g, got_sib, got_chips, name):
    r, c = w.shape
    tm = _row_tile(r, c)

    def body(pos_ref, w_ref, m_ref, v_ref, g_ref, s_ref, c_ref, go_ref, d_ref, mo_ref, vo_ref):
        grad = g_ref[...].astype(F32) + s_ref[...].astype(F32)
        for j in range(3):
            grad = grad + c_ref[j].astype(F32)
        delta, mn, vn = _adamw(w_ref[...], grad, m_ref[...], v_ref[...])
        go_ref[...] = grad
        d_ref[...] = delta
        mo_ref[...] = mn
        vo_ref[...] = vn

    row = pl.BlockSpec((tm, c), lambda i, pos: (i, 0))
    return pl.pallas_call(
        body, name=name,
        grid_spec=pltpu.PrefetchScalarGridSpec(
            num_scalar_prefetch=1, grid=(r // tm,),
            in_specs=[row, row, row,
                      pl.BlockSpec((None, tm, c), lambda i, pos: (4 * pos[0] + 2 * pos[1] + pos[2], i, 0)),
                      pl.BlockSpec((None, tm, c), lambda i, pos: (2 * pos[0] + pos[1], i, 0)),
                      pl.BlockSpec((3, tm, c), lambda i, pos: (0, i, 0))],
            out_specs=[row, row, row, row]),
        out_shape=[jax.ShapeDtypeStruct((r, c), F32)] * 4,
        compiler_params=_cp(("parallel",)),
    )(place, w, m, v, g, got_sib, got_chips)


def _small_update(stats_all, ws, ms, vs):
    def body(st_ref, w_ref, m_ref, v_ref, go_ref, d_ref, mo_ref, vo_ref):
        grad = st_ref[0]
        for k in range(1, NDEV):
            grad = grad + st_ref[k]
        delta, mn, vn = _adamw(w_ref[...], grad, m_ref[...], v_ref[...])
        go_ref[...] = grad
        d_ref[...] = delta
        mo_ref[...] = mn
        vo_ref[...] = vn

    return pl.pallas_call(
        body, name="small_update",
        out_shape=[jax.ShapeDtypeStruct((8, D), F32)] * 4,
        compiler_params=_cp(),
    )(stats_all, ws, ms, vs)


def kernel(x, norm_mix_w, w_in, w_out, norm_ffn_w, w_gate, w_up, w_down, norm_final_w, loss_target, m_norm_mix_w, m_w_in, m_w_out, m_norm_ffn_w, m_w_gate, m_w_up, m_w_down, m_norm_final_w, v_norm_mix_w, v_w_in, v_w_out, v_norm_ffn_w, v_w_gate, v_w_up, v_w_down, v_norm_final_w):
    tr = {"w_gate", "w_up"}
    names = ["w_in", "w_out", "w_gate", "w_up", "w_down"]

    def view(a, n):
        return a[0].T if n in tr else a[0]

    big_w = [view(a, n) for a, n in zip([w_in, w_out, w_gate, w_up, w_down], names)]
    big_m = [view(a, n) for a, n in zip([m_w_in, m_w_out, m_w_gate, m_w_up, m_w_down], names)]
    big_v = [view(a, n) for a, n in zip([v_w_in, v_w_out, v_w_gate, v_w_up, v_w_down], names)]

    shards = [_cast_bf16(w, "cast_" + n) for w, n in zip(big_w, names)]
    (win,) = _all_gather(shards[0:1], "all_gather_w_in", 1)
    wout, wg, wu, wd = _all_gather(shards[1:], "all_gather_rest", 2)
    nw3 = norm_final_w.reshape(1, D)
    place = jnp.stack([lax.axis_index("x"), lax.axis_index("y"), lax.axis_index("c")]).astype(jnp.int32)
    red = _Reduction(place, first_collective_id=3)
    stats, gx, *_ = _local_step(
        x[0], loss_target[0], norm_mix_w, norm_ffn_w, nw3, win, wout.reshape(D, D), wg, wu, wd, red)
    stats_all = _exchange_stats(stats, red.next_id())
    upd = [_shard_update(place, w, m, v, *red.parts(n), "update_" + n)
           for w, m, v, n in zip(big_w, big_m, big_v, names)]

    def rows(a, b, c):
        return jnp.concatenate([a.reshape(1, D), b.reshape(1, D), c.reshape(1, D), jnp.zeros((5, D), F32)], axis=0)

    sg, sd, sm, sv = _small_update(stats_all, rows(norm_mix_w, norm_ffn_w, norm_final_w),
                                   rows(m_norm_mix_w, m_norm_ffn_w, m_norm_final_w),
                                   rows(v_norm_mix_w, v_norm_ffn_w, v_norm_final_w))
    loss = sg[3, 0]

    def outs(k, small):
        big = [(u[k].T if n in tr else u[k])[None] for u, n in zip(upd, names)]
        return [small[0:1], big[0], big[1], small[1:2], big[2], big[3], big[4], small[2]]

    return (loss, gx[None], *outs(0, sg), *outs(1, sd), *outs(2, sm), *outs(3, sv))
```

```python
import functools
import math

import numpy as np
import jax
import jax.numpy as jnp
from jax import lax
from jax.experimental import pallas as pl
from jax.experimental.pallas import tpu as pltpu
from jax.experimental.pallas import tpu_sc as plsc

F32 = jnp.float32
BF16 = jnp.bfloat16

S = 2048
D = 2048
NDEV = 8
N_IN = 7168 // NDEV
N_FF = 5632 // NDEV
N_OUT = 2048 // NDEV
AH, AHD = 8, 128
RH, RHD = 4, 256
CH = 128
NB = S // CH
EPS = 1e-6
PATTERNS = ((1, 16), (4, 4), (16, 1))
NEG = -1e30
VMEM_LIMIT = 56 * 1024 * 1024

ADAM_LR, ADAM_B1, ADAM_B2, ADAM_EPS, ADAM_WD, ADAM_STEP = 0.001, 0.9, 0.999, 1e-08, 0.01, 10
MESH = pl.DeviceIdType.MESH


def _cp(sem=None):
    return pltpu.CompilerParams(dimension_semantics=sem, vmem_limit_bytes=VMEM_LIMIT)


def _dot(a, b):
    return jnp.dot(a, b, preferred_element_type=F32)


def _dot_nt(a, b):
    return lax.dot_general(a, b, (((1,), (1,)), ((), ())), preferred_element_type=F32)


def _dot_tn(a, b):
    return lax.dot_general(a, b, (((0,), (0,)), ((), ())), preferred_element_type=F32)


def _sigmoid(x):
    return 1.0 / (1.0 + jnp.exp(-x))


def _cast_bf16(w, name):
    r, c = w.shape
    tm = r if r <= 1024 else 512

    def body(w_ref, o_ref):
        o_ref[...] = w_ref[...].astype(BF16)

    return pl.pallas_call(
        body, name=name, grid=(r // tm,),
        in_specs=[pl.BlockSpec((tm, c), lambda i: (i, 0))],
        out_specs=pl.BlockSpec((tm, c), lambda i: (i, 0)),
        out_shape=jax.ShapeDtypeStruct((r, c), BF16),
        compiler_params=_cp(("parallel",)),
    )(w)


def _rms_fwd(x, nw):
    tm = 256

    def body(x_ref, w_ref, h_ref, r_ref):
        xs = x_ref[...]
        r = lax.rsqrt(jnp.mean(xs * xs, axis=-1, keepdims=True) + EPS)
        h_ref[...] = ((xs * r) * w_ref[...]).astype(BF16)
        r_ref[...] = r

    return pl.pallas_call(
        body, name="rms_fwd", grid=(S // tm,),
        in_specs=[pl.BlockSpec((tm, D), lambda i: (i, 0)), pl.BlockSpec((1, D), lambda i: (0, 0))],
        out_specs=[pl.BlockSpec((tm, D), lambda i: (i, 0)), pl.BlockSpec((tm, 1), lambda i: (i, 0))],
        out_shape=[jax.ShapeDtypeStruct((S, D), BF16), jax.ShapeDtypeStruct((S, 1), F32)],
        compiler_params=_cp(("parallel",)),
    )(x, nw)


def _rms_bwd_tile(dh, xs, r, nw):
    dnw = jnp.sum(dh * (xs * r), axis=0, keepdims=True)
    gy = dh * nw
    dx = r * gy - xs * ((r * r * r) * jnp.mean(gy * xs, axis=-1, keepdims=True))
    return dx, dnw


def _proj(h1, win):
    tm = 512

    def body(a_ref, w_ref, o_ref):
        o_ref[...] = _dot(a_ref[...], w_ref[...])

    return pl.pallas_call(
        body, name="proj", grid=(NDEV, S // tm),
        in_specs=[pl.BlockSpec((tm, D), lambda p, m: (m, 0)),
                  pl.BlockSpec((None, D, N_IN), lambda p, m: (p, 0, 0))],
        out_specs=pl.BlockSpec((tm, N_IN), lambda p, m: (m, p)),
        out_shape=jax.ShapeDtypeStruct((S, NDEV * N_IN), F32),
        compiler_params=_cp(("parallel", "parallel")),
    )(h1, win)


def _out_proj_rms(x, ma, mr, wout, nw):
    tm = 256
    half = D // 2

    def body(x_ref, ma_ref, mr_ref, w_ref, nw_ref, x2_ref, h_ref, r_ref):
        acc = _dot(ma_ref[...], w_ref[0:half, :]) + _dot(mr_ref[...], w_ref[half:D, :])
        x2 = x_ref[...] + acc
        r = lax.rsqrt(jnp.mean(x2 * x2, axis=-1, keepdims=True) + EPS)
        x2_ref[...] = x2
        h_ref[...] = ((x2 * r) * nw_ref[...]).astype(BF16)
        r_ref[...] = r

    return pl.pallas_call(
        body, name="out_proj_rms", grid=(S // tm,),
        in_specs=[pl.BlockSpec((tm, D), lambda i: (i, 0)),
                  pl.BlockSpec((tm, half), lambda i: (i, 0)),
                  pl.BlockSpec((tm, half), lambda i: (i, 0)),
                  pl.BlockSpec((D, D), lambda i: (0, 0)),
                  pl.BlockSpec((1, D), lambda i: (0, 0))],
        out_specs=[pl.BlockSpec((tm, D), lambda i: (i, 0)), pl.BlockSpec((tm, D), lambda i: (i, 0)),
                   pl.BlockSpec((tm, 1), lambda i: (i, 0))],
        out_shape=[jax.ShapeDtypeStruct((S, D), F32), jax.ShapeDtypeStruct((S, D), BF16),
                   jax.ShapeDtypeStruct((S, 1), F32)],
        compiler_params=_cp(("parallel",)),
    )(x, ma, mr, wout, nw)


def _ffn_up(h2, wg, wu):
    tm = 512

    def body(h_ref, wg_ref, wu_ref, g_ref, u_ref, a_ref):
        h = h_ref[...]
        g = _dot_nt(h, wg_ref[...])
        u = _dot_nt(h, wu_ref[...])
        g_ref[...] = g
        u_ref[...] = u
        a_ref[...] = ((g * _sigmoid(g)) * u).astype(BF16)

    blk = pl.BlockSpec((None, tm, N_FF), lambda p, m: (p, m, 0))
    wblk = pl.BlockSpec((None, N_FF, D), lambda p, m: (p, 0, 0))
    return pl.pallas_call(
        body, name="ffn_up", grid=(NDEV, S // tm),
        in_specs=[pl.BlockSpec((tm, D), lambda p, m: (m, 0)), wblk, wblk],
        out_specs=[blk, blk, blk],
        out_shape=[jax.ShapeDtypeStruct((NDEV, S, N_FF), F32), jax.ShapeDtypeStruct((NDEV, S, N_FF), F32),
                   jax.ShapeDtypeStruct((NDEV, S, N_FF), BF16)],
        compiler_params=_cp(("parallel", "parallel")),
    )(h2, wg, wu)


def _ffn_down_loss(x2, a, wd, nw, tgt):
    tm = 512

    def body(x2_ref, a_ref, w_ref, nw_ref, t_ref, dx_ref, dxb_ref, st_ref, acc_ref):
        m, p = pl.program_id(0), pl.program_id(1)

        @pl.when(p == 0)
        def _():
            acc_ref[...] = jnp.zeros_like(acc_ref)

        @pl.when((p == 0) & (m == 0))
        def _():
            st_ref[...] = jnp.zeros_like(st_ref)

        acc_ref[...] += _dot(a_ref[...], w_ref[...])

        @pl.when(p == NDEV - 1)
        def _():
            x3 = x2_ref[...] + acc_ref[...]
            nwv = nw_ref[...]
            r = lax.rsqrt(jnp.mean(x3 * x3, axis=-1, keepdims=True) + EPS)
            y = (x3 * r) * nwv
            err = y - t_ref[...]
            loss = 0.5 * jnp.sum(jnp.mean(err * err, axis=-1, keepdims=True), axis=0, keepdims=True)
            dy = err * (1.0 / D)
            dx, dnw = _rms_bwd_tile(dy, x3, r, nwv)
            dx_ref[...] = dx
            dxb_ref[...] = dx.astype(BF16)
            st_ref[0:1, :] += dnw
            st_ref[1:2, :] += jnp.broadcast_to(loss, (1, D))

    return pl.pallas_call(
        body, name="ffn_down_loss", grid=(S // tm, NDEV),
        in_specs=[pl.BlockSpec((tm, D), lambda m, p: (m, 0)),
                  pl.BlockSpec((None, tm, N_FF), lambda m, p: (p, m, 0)),
                  pl.BlockSpec((None, N_FF, D), lambda m, p: (p, 0, 0)),
                  pl.BlockSpec((1, D), lambda m, p: (0, 0)),
                  pl.BlockSpec((tm, D), lambda m, p: (m, 0))],
        out_specs=[pl.BlockSpec((tm, D), lambda m, p: (m, 0)), pl.BlockSpec((tm, D), lambda m, p: (m, 0)),
                   pl.BlockSpec((8, D), lambda m, p: (0, 0))],
        out_shape=[jax.ShapeDtypeStruct((S, D), F32), jax.ShapeDtypeStruct((S, D), BF16),
                   jax.ShapeDtypeStruct((8, D), F32)],
        scratch_shapes=[pltpu.VMEM((tm, D), F32)],
        compiler_params=_cp(("arbitrary", "arbitrary")),
    )(x2, a, wd, nw, tgt)


def _ffn_down_bwd(dx3b, wd, g, u):
    tm = 512

    def body(dx_ref, w_ref, g_ref, u_ref, dg_ref, du_ref):
        da = _dot_nt(dx_ref[...], w_ref[...])
        gv = g_ref[...]
        sg = _sigmoid(gv)
        silu = gv * sg
        dg_ref[...] = ((da * u_ref[...]) * (sg * (1.0 + gv * (1.0 - sg)))).astype(BF16)
        du_ref[...] = (da * silu).astype(BF16)

    blk = pl.BlockSpec((None, tm, N_FF), lambda p, m: (p, m, 0))
    return pl.pallas_call(
        body, name="ffn_down_bwd", grid=(NDEV, S // tm),
        in_specs=[pl.BlockSpec((tm, D), lambda p, m: (m, 0)),
                  pl.BlockSpec((None, N_FF, D), lambda p, m: (p, 0, 0)), blk, blk],
        out_specs=[blk, blk],
        out_shape=[jax.ShapeDtypeStruct((NDEV, S, N_FF), BF16), jax.ShapeDtypeStruct((NDEV, S, N_FF), BF16)],
        compiler_params=_cp(("parallel", "parallel")),
    )(dx3b, wd, g, u)


def _ffn_up_bwd(dg, du, wg, wu, dres, xs, r, nw):
    tm = 512

    def body(dg_ref, du_ref, wg_ref, wu_ref, dres_ref, x_ref, r_ref, nw_ref, dx_ref, dxb_ref, st_ref, acc_ref):
        m, p = pl.program_id(0), pl.program_id(1)

        @pl.when(p == 0)
        def _():
            acc_ref[...] = jnp.zeros_like(acc_ref)

        @pl.when((p == 0) & (m == 0))
        def _():
            st_ref[...] = jnp.zeros_like(st_ref)

        acc_ref[...] += _dot(dg_ref[...], wg_ref[...]) + _dot(du_ref[...], wu_ref[...])

        @pl.when(p == NDEV - 1)
        def _():
            dx, dnw = _rms_bwd_tile(acc_ref[...], x_ref[...], r_ref[...], nw_ref[...])
            dx = dres_ref[...] + dx
            dx_ref[...] = dx
            dxb_ref[...] = dx.astype(BF16)
            st_ref[0:1, :] += dnw

    blk = pl.BlockSpec((None, tm, N_FF), lambda m, p: (p, m, 0))
    wblk = pl.BlockSpec((None, N_FF, D), lambda m, p: (p, 0, 0))
    row = pl.BlockSpec((tm, D), lambda m, p: (m, 0))
    return pl.pallas_call(
        body, name="ffn_up_bwd", grid=(S // tm, NDEV),
        in_specs=[blk, blk, wblk, wblk, row, row, pl.BlockSpec((tm, 1), lambda m, p: (m, 0)),
                  pl.BlockSpec((1, D), lambda m, p: (0, 0))],
        out_specs=[row, row, pl.BlockSpec((8, D), lambda m, p: (0, 0))],
        out_shape=[jax.ShapeDtypeStruct((S, D), F32), jax.ShapeDtypeStruct((S, D), BF16),
                   jax.ShapeDtypeStruct((8, D), F32)],
        scratch_shapes=[pltpu.VMEM((tm, D), F32)],
        compiler_params=_cp(("arbitrary", "arbitrary")),
    )(dg, du, wg, wu, dres, xs, r, nw)


def _out_proj_bwd(dx2b, wout):
    tm = 256

    def body(dx_ref, w_ref, o_ref):
        o_ref[...] = _dot_nt(dx_ref[...], w_ref[...])

    return pl.pallas_call(
        body, name="out_proj_bwd", grid=(S // tm,),
        in_specs=[pl.BlockSpec((tm, D), lambda i: (i, 0)), pl.BlockSpec((D, D), lambda i: (0, 0))],
        out_specs=pl.BlockSpec((tm, D), lambda i: (i, 0)),
        out_shape=jax.ShapeDtypeStruct((S, D), F32),
        compiler_params=_cp(("parallel",)),
    )(dx2b, wout)


def _in_proj_bwd(dproj, win, dres, xs, r, nw):
    tm = 512

    def body(dp_ref, w_ref, dres_ref, x_ref, r_ref, nw_ref, dx_ref, st_ref, acc_ref):
        m, p = pl.program_id(0), pl.program_id(1)

        @pl.when(p == 0)
        def _():
            acc_ref[...] = jnp.zeros_like(acc_ref)

        @pl.when((p == 0) & (m == 0))
        def _():
            st_ref[...] = jnp.zeros_like(st_ref)

        acc_ref[...] += _dot_nt(dp_ref[...], w_ref[...])

        @pl.when(p == NDEV - 1)
        def _():
            dx, dnw = _rms_bwd_tile(acc_ref[...], x_ref[...], r_ref[...], nw_ref[...])
            dx_ref[...] = dres_ref[...] + dx
            st_ref[0:1, :] += dnw

    row = pl.BlockSpec((tm, D), lambda m, p: (m, 0))
    return pl.pallas_call(
        body, name="in_proj_bwd", grid=(S // tm, NDEV),
        in_specs=[pl.BlockSpec((tm, N_IN), lambda m, p: (m, p)),
                  pl.BlockSpec((None, D, N_IN), lambda m, p: (p, 0, 0)),
                  row, row, pl.BlockSpec((tm, 1), lambda m, p: (m, 0)),
                  pl.BlockSpec((1, D), lambda m, p: (0, 0))],
        out_specs=[row, pl.BlockSpec((8, D), lambda m, p: (0, 0))],
        out_shape=[jax.ShapeDtypeStruct((S, D), F32), jax.ShapeDtypeStruct((8, D), F32)],
        scratch_shapes=[pltpu.VMEM((tm, D), F32)],
        compiler_params=_cp(("arbitrary", "arbitrary")),
    )(dproj, win, dres, xs, r, nw)


def _wgrad_in(h1, dproj):
    def body(a_ref, d_ref, o_ref):
        o_ref[...] = _dot_tn(a_ref[...], d_ref[...]).astype(BF16)

    return pl.pallas_call(
        body, name="wgrad_in", grid=(NDEV,),
        in_specs=[pl.BlockSpec((S, D), lambda p: (0, 0)), pl.BlockSpec((S, N_IN), lambda p: (0, p))],
        out_specs=pl.BlockSpec((None, D, N_IN), lambda p: (p, 0, 0)),
        out_shape=jax.ShapeDtypeStruct((NDEV, D, N_IN), BF16),
        compiler_params=_cp(("parallel",)),
    )(h1, dproj)


def _wgrad_rows(a3, dy, name):
    def body(a_ref, d_ref, o_ref):
        o_ref[...] = _dot_tn(a_ref[...], d_ref[...]).astype(BF16)

    return pl.pallas_call(
        body, name=name, grid=(NDEV,),
        in_specs=[pl.BlockSpec((None, S, N_FF), lambda p: (p, 0, 0)), pl.BlockSpec((S, D), lambda p: (0, 0))],
        out_specs=pl.BlockSpec((None, N_FF, D), lambda p: (p, 0, 0)),
        out_shape=jax.ShapeDtypeStruct((NDEV, N_FF, D), BF16),
        compiler_params=_cp(("parallel",)),
    )(a3, dy)


def _wgrad_out(ma, mr, dx2b):
    half = D // 2
    per = half // N_OUT

    def body(ma_ref, mr_ref, d_ref, o_ref):
        p = pl.program_id(0)

        @pl.when(p < per)
        def _():
            o_ref[...] = _dot_tn(ma_ref[...], d_ref[...]).astype(BF16)

        @pl.when(p >= per)
        def _():
            o_ref[...] = _dot_tn(mr_ref[...], d_ref[...]).astype(BF16)

    return pl.pallas_call(
        body, name="wgrad_out", grid=(NDEV,),
        in_specs=[pl.BlockSpec((S, N_OUT), lambda p: (0, jnp.minimum(p, per - 1))),
                  pl.BlockSpec((S, N_OUT), lambda p: (0, jnp.maximum(p - per, 0))),
                  pl.BlockSpec((S, D), lambda p: (0, 0))],
        out_specs=pl.BlockSpec((None, N_OUT, D), lambda p: (p, 0, 0)),
        out_shape=jax.ShapeDtypeStruct((NDEV, N_OUT, D), BF16),
        compiler_params=_cp(("parallel",)),
    )(ma, mr, dx2b)


def _attn_consts():
    c = np.zeros((AH, 8, AHD), np.float32)
    for h in range(AH):
        c[h, :, :] = 2.0 ** (-(h + 1))
    return jnp.asarray(c)


def _permute_in(dst, src, d, cast=None):
    ln = S // d
    for rr in range(d):
        v = src[pl.ds(rr, ln, stride=d), :] if d > 1 else src[...]
        dst[rr * ln:(rr + 1) * ln, :] = v if cast is None else v.astype(cast)


def _attn_masks():
    qi = lax.broadcasted_iota(jnp.int32, (CH, CH), 0)
    kj = lax.broadcasted_iota(jnp.int32, (CH, CH), 1)
    dist_c = (qi - kj).astype(F32)
    dist_p = (qi - kj + CH).astype(F32)
    return (qi >= kj)[None], (kj >= qi)[None], dist_c[None], dist_p[None]


GB = 8


def _bdot_nt(a, b):
    return lax.dot_general(a, b, (((2,), (2,)), ((0,), (0,))), preferred_element_type=F32)


def _bdot(a, b):
    return lax.dot_general(a, b, (((2,), (1,)), ((0,), (0,))), preferred_element_type=F32)


def _bdot_tn(a, b):
    return lax.dot_general(a, b, (((1,), (1,)), ((0,), (0,))), preferred_element_type=F32)


def _shift_block(dst, src):
    dst[0:CH, :] = jnp.zeros((CH, AHD), dst.dtype)
    dst[CH:S, :] = src[0:S - CH, :]


def _has_prev(g, nb):
    blk = lax.broadcasted_iota(jnp.int32, (GB, 1, 1), 0) + g * GB
    return (blk & (nb - 1)) != 0


def _blocks(ref, g):
    return ref[g * GB * CH:(g + 1) * GB * CH, :].reshape(GB, CH, AHD)


def _attn_fwd(proj):
    scale = 1.0 / math.sqrt(AHD)

    def body(c_ref, q_ref, k_ref, v_ref, o_ref, ob_ref, lse_ref, qd, kd, vd, kps, vps, od, ld, *nat):
        onat, lnat = nat[0:3], nat[3:6]
        slope = c_ref[0:1, :]
        mask_c, mask_p, dist_c, dist_p = _attn_masks()
        for pi, (d, nb) in enumerate(PATTERNS):
            _permute_in(qd, q_ref, d, BF16)
            _permute_in(kd, k_ref, d, BF16)
            _permute_in(vd, v_ref, d, BF16)
            if nb > 1:
                _shift_block(kps, kd)
                _shift_block(vps, vd)
            bias_c = -(slope * float(d)) * dist_c
            bias_p = -(slope * float(d)) * dist_p
            for g in range(NB // GB):
                q3, k3, v3 = _blocks(qd, g), _blocks(kd, g), _blocks(vd, g)
                s_c = jnp.where(mask_c, _bdot_nt(q3, k3) * scale + bias_c, NEG)
                mx = jnp.max(s_c, axis=-1, keepdims=True)
                if nb > 1:
                    kp3, vp3 = _blocks(kps, g), _blocks(vps, g)
                    s_p = jnp.where(jnp.logical_and(mask_p, _has_prev(g, nb)),
                                    _bdot_nt(q3, kp3) * scale + bias_p, NEG)
                    mx = jnp.maximum(mx, jnp.max(s_p, axis=-1, keepdims=True))
                    l = (jnp.sum(jnp.exp(s_c - mx), axis=-1, keepdims=True)
                         + jnp.sum(jnp.exp(s_p - mx), axis=-1, keepdims=True))
                    lse = mx + jnp.log(l)
                    o3 = _bdot(jnp.exp(s_c - lse).astype(BF16), v3) + _bdot(jnp.exp(s_p - lse).astype(BF16), vp3)
                else:
                    l = jnp.sum(jnp.exp(s_c - mx), axis=-1, keepdims=True)
                    lse = mx + jnp.log(l)
                    o3 = _bdot(jnp.exp(s_c - lse).astype(BF16), v3)
                rows = slice(g * GB * CH, (g + 1) * GB * CH)
                od[rows, :] = o3.reshape(GB * CH, AHD)
                ld[rows, :] = jnp.broadcast_to(lse, (GB, CH, AHD)).reshape(GB * CH, AHD)
            ln = S // d
            for rr in range(d):
                if d > 1:
                    onat[pi][pl.ds(rr, ln, stride=d), :] = od[rr * ln:(rr + 1) * ln, :]
                    lnat[pi][pl.ds(rr, ln, stride=d), :] = ld[rr * ln:(rr + 1) * ln, :]
                else:
                    onat[pi][...] = od[...]
                    lnat[pi][...] = ld[...]
        l0, l1, l2 = lnat[0][...], lnat[1][...], lnat[2][...]
        mx = jnp.maximum(jnp.maximum(l0, l1), l2)
        e0, e1, e2 = jnp.exp(l0 - mx), jnp.exp(l1 - mx), jnp.exp(l2 - mx)
        den = e0 + e1 + e2
        out = (e0 / den) * onat[0][...] + (e1 / den) * onat[1][...] + (e2 / den) * onat[2][...]
        o_ref[...] = out
        ob_ref[...] = out.astype(BF16)
        lse_ref[...] = mx + jnp.log(den)

    def col(off):
        return pl.BlockSpec((S, AHD), lambda h: (0, off + h))

    return pl.pallas_call(
        body, name="attn_fwd", grid=(AH,),
        in_specs=[pl.BlockSpec((None, 8, AHD), lambda h: (h, 0, 0)), col(0), col(AH), col(2 * AH)],
        out_specs=[col(0), col(0), col(0)],
        out_shape=[jax.ShapeDtypeStruct((S, AH * AHD), F32), jax.ShapeDtypeStruct((S, AH * AHD), BF16),
                   jax.ShapeDtypeStruct((S, AH * AHD), F32)],
        scratch_shapes=[pltpu.VMEM((S, AHD), BF16) for _ in range(5)]
        + [pltpu.VMEM((S, AHD), F32) for _ in range(8)],
        compiler_params=_cp(("parallel",)),
    )(_attn_consts(), proj, proj, proj)


def _attn_bwd(proj, dmixed, o, lse):
    scale = 1.0 / math.sqrt(AHD)

    def body(c_ref, q_ref, k_ref, v_ref, do_ref, o_ref, lse_ref, dq_ref, dk_ref, dv_ref,
             qd, kd, vd, dod, kps, vps, lsd, dld, dqd, dkd, dvd, delta, aq, ak, av):
        slope = c_ref[0:1, :]
        mask_c, mask_p, dist_c, dist_p = _attn_masks()
        delta[...] = jnp.broadcast_to(jnp.sum(do_ref[...] * o_ref[...], axis=-1, keepdims=True), (S, AHD))
        for pi, (d, nb) in enumerate(PATTERNS):
            _permute_in(qd, q_ref, d, BF16)
            _permute_in(kd, k_ref, d, BF16)
            _permute_in(vd, v_ref, d, BF16)
            _permute_in(dod, do_ref, d, BF16)
            _permute_in(lsd, lse_ref, d)
            _permute_in(dld, delta, d)
            if nb > 1:
                _shift_block(kps, kd)
                _shift_block(vps, vd)
            bias_c = -(slope * float(d)) * dist_c
            bias_p = -(slope * float(d)) * dist_p
            for g in range(NB // GB):
                q3, k3, v3, do3 = _blocks(qd, g), _blocks(kd, g), _blocks(vd, g), _blocks(dod, g)
                ls, dl = _blocks(lsd, g), _blocks(dld, g)
                lo, hi = g * GB * CH, (g + 1) * GB * CH
                p_c = jnp.exp(jnp.where(mask_c, _bdot_nt(q3, k3) * scale + bias_c, NEG) - ls)
                ds_c = ((p_c * (_bdot_nt(do3, v3) - dl)) * scale).astype(BF16)
                dq3 = _bdot(ds_c, k3)
                dkd[lo:hi, :] = _bdot_tn(ds_c, q3).reshape(GB * CH, AHD)
                dvd[lo:hi, :] = _bdot_tn(p_c.astype(BF16), do3).reshape(GB * CH, AHD)
                if nb > 1:
                    kp3, vp3 = _blocks(kps, g), _blocks(vps, g)
                    p_p = jnp.exp(jnp.where(jnp.logical_and(mask_p, _has_prev(g, nb)),
                                            _bdot_nt(q3, kp3) * scale + bias_p, NEG) - ls)
                    ds_p = ((p_p * (_bdot_nt(do3, vp3) - dl)) * scale).astype(BF16)
                    dq3 = dq3 + _bdot(ds_p, kp3)
                    dkp = _bdot_tn(ds_p, q3).reshape(GB * CH, AHD)
                    dvp = _bdot_tn(p_p.astype(BF16), do3).reshape(GB * CH, AHD)
                    if g == 0:
                        dkd[0:hi - CH, :] += dkp[CH:, :]
                        dvd[0:hi - CH, :] += dvp[CH:, :]
                    else:
                        dkd[lo - CH:hi - CH, :] += dkp
                        dvd[lo - CH:hi - CH, :] += dvp
                dqd[lo:hi, :] = dq3.reshape(GB * CH, AHD)
            ln = S // d
            for acc, src in ((aq, dqd), (ak, dkd), (av, dvd)):
                if pi == 0:
                    acc[...] = src[...]
                else:
                    for rr in range(d):
                        acc[pl.ds(rr, ln, stride=d), :] += src[rr * ln:(rr + 1) * ln, :]
        dq_ref[...] = aq[...].astype(BF16)
        dk_ref[...] = ak[...].astype(BF16)
        dv_ref[...] = av[...].astype(BF16)

    def col(off):
        return pl.BlockSpec((S, AHD), lambda h: (0, off + h))

    return pl.pallas_call(
        body, name="attn_bwd", grid=(AH,),
        in_specs=[pl.BlockSpec((None, 8, AHD), lambda h: (h, 0, 0)), col(0), col(AH), col(2 * AH),
                  col(0), col(0), col(0)],
        out_specs=[col(0), col(0), col(0)],
        out_shape=[jax.ShapeDtypeStruct((S, AH * AHD), BF16)] * 3,
        scratch_shapes=[pltpu.VMEM((S, AHD), BF16) for _ in range(6)]
        + [pltpu.VMEM((S, AHD), F32) for _ in range(9)],
        compiler_params=_cp(("parallel",)),
    )(_attn_consts(), proj, proj, proj, dmixed, o, lse)


def _ret_consts():
    c = np.zeros((RH, 8, RHD), np.float32)
    for h in range(RH):
        c[h, :, :] = np.log(np.float32(1.0) - np.float32(2.0 ** (-5.0 - h)))
    return jnp.asarray(c)


def _ret_factors(lg):
    i = lax.broadcasted_iota(jnp.int32, (CH, CH), 0)
    j = lax.broadcasted_iota(jnp.int32, (CH, CH), 1)
    dif = (i - j).astype(F32)
    decay = jnp.where(dif >= 0, jnp.exp(lg[:, 0:CH] * jnp.maximum(dif, 0.0)), 0.0)
    row = lax.broadcasted_iota(jnp.int32, (CH, RHD), 0).astype(F32)
    zeta = jnp.exp(lg * (CH - 1.0 - row))
    xi = jnp.exp(lg * (row + 1.0))
    return decay, zeta, xi, jnp.exp(lg * float(CH))


def _ret_specs(rev):
    off = 3 * AH * AHD // RHD

    def ch(n):
        return (NB - 1 - n) if rev else n

    def col(k):
        return pl.BlockSpec((CH, RHD), lambda h, n: (ch(n), off + k * RH + h))

    own = pl.BlockSpec((CH, RHD), lambda h, n: (ch(n), h))
    state = pl.BlockSpec((None, None, RHD, RHD), lambda h, n: (h, ch(n), 0, 0))
    const = pl.BlockSpec((None, 8, RHD), lambda h, n: (h, 0, 0))
    return col, own, state, const


def _ret_fwd(proj):
    def body(c_ref, q_ref, k_ref, v_ref, g_ref, ret_ref, mr_ref, st_ref, r_acc):
        n = pl.program_id(1)

        @pl.when(n == 0)
        def _():
            r_acc[...] = jnp.zeros_like(r_acc)

        decay, zeta, xi, gch = _ret_factors(c_ref[0:1, :])
        qb = q_ref[...].astype(BF16)
        kc = k_ref[...] * (1.0 / math.sqrt(RHD))
        kb = kc.astype(BF16)
        vb = v_ref[...].astype(BF16)
        rb = r_acc[...].astype(BF16)
        st_ref[...] = rb
        scores = _dot_nt(qb, kb) * decay
        ret = _dot(scores.astype(BF16), vb) + _dot(qb, rb) * xi
        r_acc[...] = r_acc[...] * gch + _dot_tn((kc * zeta).astype(BF16), vb)
        ret_ref[...] = ret
        rr = lax.rsqrt(jnp.mean(ret * ret, axis=-1, keepdims=True) + EPS)
        gv = g_ref[...]
        mr_ref[...] = ((gv * _sigmoid(gv)) * (ret * rr)).astype(BF16)

    col, own, state, const = _ret_specs(False)
    return pl.pallas_call(
        body, name="ret_fwd", grid=(RH, NB),
        in_specs=[const, col(0), col(1), col(2), col(3)],
        out_specs=[own, own, state],
        out_shape=[jax.ShapeDtypeStruct((S, RH * RHD), F32), jax.ShapeDtypeStruct((S, RH * RHD), BF16),
                   jax.ShapeDtypeStruct((RH, NB, RHD, RHD), BF16)],
        scratch_shapes=[pltpu.VMEM((RHD, RHD), F32)],
        compiler_params=_cp(("parallel", "arbitrary")),
    )(_ret_consts(), proj, proj, proj, proj)


def _ret_bwd(proj, ret, states, dmixed):
    def body(c_ref, q_ref, k_ref, v_ref, g_ref, ret_ref, st_ref, dm_ref, dq_ref, dk_ref, dv_ref, dg_ref, g_acc):
        n = pl.program_id(1)

        @pl.when(n == 0)
        def _():
            g_acc[...] = jnp.zeros_like(g_acc)

        decay, zeta, xi, gch = _ret_factors(c_ref[0:1, :])
        ret_v = ret_ref[...]
        rr = lax.rsqrt(jnp.mean(ret_v * ret_v, axis=-1, keepdims=True) + EPS)
        gv = g_ref[...]
        sg = _sigmoid(gv)
        dmix = dm_ref[...]
        dg_ref[...] = ((dmix * (ret_v * rr)) * (sg * (1.0 + gv * (1.0 - sg)))).astype(BF16)
        dretn = dmix * (gv * sg)
        dret = rr * dretn - ret_v * ((rr * rr * rr) * jnp.mean(dretn * ret_v, axis=-1, keepdims=True))

        qb = q_ref[...].astype(BF16)
        kc = k_ref[...] * (1.0 / math.sqrt(RHD))
        kb = kc.astype(BF16)
        vb = v_ref[...].astype(BF16)
        rb = st_ref[...]
        db = dret.astype(BF16)
        sc = (_dot_nt(qb, kb) * decay).astype(BF16)
        da = (_dot_nt(db, vb) * decay).astype(BF16)
        dxi = (dret * xi).astype(BF16)
        gb = g_acc[...].astype(BF16)
        kz = (kc * zeta).astype(BF16)
        dq = _dot(da, kb) + _dot_nt(dxi, rb)
        dkc = _dot_tn(da, qb) + _dot_nt(vb, gb) * zeta
        dv = _dot_tn(sc, db) + _dot(kz, gb)
        g_acc[...] = _dot_tn(qb, dxi) + gch * g_acc[...]
        dq_ref[...] = dq.astype(BF16)
        dk_ref[...] = (dkc * (1.0 / math.sqrt(RHD))).astype(BF16)
        dv_ref[...] = dv.astype(BF16)

    col, own, state, const = _ret_specs(True)
    dm = pl.BlockSpec((CH, RHD), lambda h, n: (NB - 1 - n, AH * AHD // RHD + h))
    return pl.pallas_call(
        body, name="ret_bwd", grid=(RH, NB),
        in_specs=[const, col(0), col(1), col(2), col(3), own, state, dm],
        out_specs=[own, own, own, own],
        out_shape=[jax.ShapeDtypeStruct((S, RH * RHD), BF16)] * 4,
        scratch_shapes=[pltpu.VMEM((RHD, RHD), F32)],
        compiler_params=_cp(("parallel", "arbitrary")),
    )(_ret_consts(), proj, proj, proj, proj, ret, states, dmixed)


class _NoReduction:
    def start(self, group, grads):
        pass

    def local(self, name):
        return []

    def landed(self, name):
        return []


def _local_step(x, tgt, nw1, nw2, nw3, win, wout, wg, wu, wd, red=None):
    red = red or _NoReduction()

    def after(values, first):
        return lax.optimization_barrier((tuple(values), tuple(first)))[0]

    h1, r1 = _rms_fwd(x, nw1)
    proj = _proj(h1, win)
    o, ma, lse = _attn_fwd(proj)
    ret, mr, states = _ret_fwd(proj)
    x2, h2, r2 = _out_proj_rms(x, ma, mr, wout, nw2)
    g, u, a = _ffn_up(h2, wg, wu)
    dx3, dx3b, st3 = _ffn_down_loss(x2, a, wd, nw3, tgt)

    dwd = _wgrad_rows(a, dx3b, "wgrad_down")
    red.start(["w_down"], [dwd])
    (dx3b,) = after([dx3b], [dwd])
    dg, du = _ffn_down_bwd(dx3b, wd, g, u)
    dg, du = after([dg, du], red.local("w_down"))
    dwg = _wgrad_rows(dg, h2, "wgrad_gate")
    dwu = _wgrad_rows(du, h2, "wgrad_up")
    red.start(["w_gate", "w_up"], [dwg, dwu])
    dg, du = after([dg, du], [dwg, dwu])
    dx2, dx2b, st2 = _ffn_up_bwd(dg, du, wg, wu, dx3, x2, r2, nw2)
    (dx2b,) = after([dx2b], red.landed("w_down"))
    dwo = _wgrad_out(ma, mr, dx2b)
    red.start(["w_out"], [dwo])
    (dx2b,) = after([dx2b], [dwo] + red.local("w_gate"))
    dmixed = _out_proj_bwd(dx2b, wout)
    dqa, dka, dva = _attn_bwd(proj, dmixed, o, lse)
    (dmixed,) = after([dmixed], [dqa] + red.local("w_out"))
    dqr, dkr, dvr, dgr = _ret_bwd(proj, ret, states, dmixed)
    dproj = jnp.concatenate([dqa, dka, dva, dqr, dkr, dvr, dgr], axis=1)
    (dproj,) = after([dproj], red.landed("w_gate") + red.landed("w_out"))
    dwi = _wgrad_in(h1, dproj)
    red.start(["w_in"], [dwi])
    (dproj,) = after([dproj], [dwi])
    gx, st1 = _in_proj_bwd(dproj, win, dx2, x, r1, nw1)
    red.local("w_in")
    stats = jnp.concatenate([st1[0:1], st2[0:1], st3[0:2], jnp.zeros((4, D), F32)], axis=0)
    return stats, gx, dwi, dwo, dwg, dwu, dwd


def _place():
    x, y, c = lax.axis_index("x"), lax.axis_index("y"), lax.axis_index("c")
    return x, y, c, [(1 - x, y), (x, 1 - y), (1 - x, 1 - y)]


def _handshake(peers):
    barrier = pltpu.get_barrier_semaphore()
    for peer in peers:
        pl.semaphore_signal(barrier, inc=1, device_id=peer, device_id_type=MESH)
    pl.semaphore_wait(barrier, len(peers))


def _all_gather(shards, name, collective_id):
    na = len(shards)
    SIB, XN0, XN1, YN1, YN0, VIA_X, VIA_Y = 0, 1, 2, 3, 4, 5, 6
    D2D = {XN0: 7, XN1: 8, YN1: 9, YN0: 10, VIA_X: 11, VIA_Y: 12}

    def body(*refs):
        ins, outs = refs[:na], refs[na:2 * na]
        send_sems, recv_sems, local_sems = refs[2 * na:]
        x, y, c, _ = _place()
        me, sib = (x, y, c), (x, y, 1 - c)
        xn, yn, dg = (1 - x, y, c), (x, 1 - y, c), (1 - x, 1 - y, c)
        _handshake([sib, xn, yn])

        def part(ref, h):
            rows = ref.shape[0] // 2
            return ref if h is None else ref.at[pl.ds(h * rows, rows)]

        def block(a, owner, h):
            return part(outs[a].at[4 * owner[0] + 2 * owner[1] + owner[2]], h)

        def copy(a, k, owner, h, to, own_src=False):
            return pltpu.make_async_remote_copy(
                src_ref=part(ins[a], h) if own_src else block(a, owner, h), dst_ref=block(a, owner, h),
                send_sem=send_sems.at[a, k], recv_sem=recv_sems.at[a, k], device_id=to, device_id_type=MESH)

        def other(p):
            return (p[0], p[1], 1 - c)

        mine = [pltpu.make_async_copy(ins[a], block(a, me, None), local_sems.at[a]) for a in range(na)]
        for cp in mine:
            cp.start()
        sent = []
        for a in range(na):
            sent += [copy(a, XN0, me, 0, xn, True), copy(a, YN1, me, 1, yn, True),
                     copy(a, XN1, me, 1, xn, True), copy(a, YN0, me, 0, yn, True)]
        sent += [copy(a, SIB, me, None, sib, True) for a in range(na)]
        for cp in sent:
            cp.start()

        def landed(a, k, owner, h, then):
            copy(a, k, owner, h, me).wait_recv()
            for k2, to in then + [(D2D[k], sib)]:
                cp = copy(a, k2, owner, h, to)
                cp.start()
                sent.append(cp)

        for a in range(na):
            landed(a, XN0, xn, 0, [(VIA_Y, yn)])
            landed(a, YN1, yn, 1, [(VIA_X, xn)])
            landed(a, XN1, xn, 1, [])
            landed(a, YN0, yn, 0, [])
        for a in range(na):
            landed(a, VIA_Y, dg, 0, [])
            landed(a, VIA_X, dg, 1, [])
        for a in range(na):
            copy(a, SIB, sib, None, me).wait_recv()
            for k, owner, h in ((XN0, xn, 0), (XN1, xn, 1), (YN1, yn, 1), (YN0, yn, 0), (VIA_Y, dg, 0), (VIA_X, dg, 1)):
                copy(a, D2D[k], other(owner), h, me).wait_recv()
        for cp in sent:
            cp.wait_send()
        for cp in mine:
            cp.wait()

    return _sequencer_call(
        body, name, collective_id,
        [jax.ShapeDtypeStruct((NDEV,) + s.shape, s.dtype) for s in shards],
        [pltpu.SemaphoreType.DMA((na, 13)), pltpu.SemaphoreType.DMA((na, 13)), pltpu.SemaphoreType.DMA((na,))])(*shards)


def _sequencer_call(body, name, collective_id, out_type, scratch_types):
    return pl.kernel(
        body, name=name, out_type=out_type,
        mesh=plsc.ScalarSubcoreMesh(axis_name="sequencer", num_cores=1),
        scratch_types=scratch_types,
        compiler_params=pltpu.CompilerParams(collective_id=collective_id))


def _exchange_sibling(grads, name, collective_id):
    na = len(grads)

    def body(*refs):
        ins, outs = refs[:na], refs[na:2 * na]
        send_sems, recv_sems = refs[2 * na:]
        x, y, c, _ = _place()
        _handshake([(x, y, 1 - c)])
        cps = []
        for a in range(na):
            for k in range(4):
                cps.append(pltpu.make_async_remote_copy(
                    src_ref=ins[a].at[2 * k + (1 - c)], dst_ref=outs[a].at[k],
                    send_sem=send_sems.at[a, k], recv_sem=recv_sems.at[a, k],
                    device_id=(x, y, 1 - c), device_id_type=MESH))
        for cp in cps:
            cp.start()
        for cp in cps:
            cp.wait()

    return _sequencer_call(
        body, name, collective_id,
        [jax.ShapeDtypeStruct((4,) + g.shape[1:], g.dtype) for g in grads],
        [pltpu.SemaphoreType.DMA((na, 4)), pltpu.SemaphoreType.DMA((na, 4))])(*grads)


def _row_tile(rows, cols):
    for t in (512, 256, 176, 128, 64, 32, 16):
        if rows % t == 0 and t * cols * 4 <= (1 << 20):
            return t
    raise ValueError((rows, cols))


def _chip_sum(place, g, got, name):
    _, r, c = g.shape
    tm = _row_tile(r, c)

    def body(pos_ref, g_ref, got_ref, o_ref):
        o_ref[...] = (g_ref[...].astype(F32) + got_ref[...].astype(F32)).astype(BF16)

    return pl.pallas_call(
        body, name=name,
        grid_spec=pltpu.PrefetchScalarGridSpec(
            num_scalar_prefetch=1, grid=(4, r // tm),
            in_specs=[pl.BlockSpec((None, tm, c), lambda k, i, pos: (2 * k + pos[2], i, 0)),
                      pl.BlockSpec((None, tm, c), lambda k, i, pos: (k, i, 0))],
            out_specs=pl.BlockSpec((None, tm, c), lambda k, i, pos: (k, i, 0))),
        out_shape=jax.ShapeDtypeStruct((4, r, c), BF16),
        compiler_params=_cp(("parallel", "parallel")),
    )(place, g, got)


def _exchange_chips(sums, name, collective_id):
    na = len(sums)

    def body(*refs):
        ins, outs = refs[:na], refs[na:2 * na]
        send_sems, recv_sems = refs[2 * na:]
        x, y, c, chips = _place()
        _handshake([(*chip, c) for chip in chips])
        cps = []
        for a in range(na):
            for j, chip in enumerate(chips):
                cps.append(pltpu.make_async_remote_copy(
                    src_ref=ins[a].at[2 * chip[0] + chip[1]], dst_ref=outs[a].at[j],
                    send_sem=send_sems.at[a, j], recv_sem=recv_sems.at[a, j],
                    device_id=(*chip, c), device_id_type=MESH))
        for cp in cps:
            cp.start()
        for cp in cps:
            cp.wait()

    return _sequencer_call(
        body, name, collective_id,
        [jax.ShapeDtypeStruct((3,) + s.shape[1:], s.dtype) for s in sums],
        [pltpu.SemaphoreType.DMA((na, 3)), pltpu.SemaphoreType.DMA((na, 3))])(*sums)


def _exchange_stats(stats, collective_id):
    def body(st_in, st_out, st_send, st_recv, local_sem):
        x, y, c, _ = _place()
        me_idx = 4 * x + 2 * y + c
        peers = [(x ^ ((k >> 2) & 1), y ^ ((k >> 1) & 1), c ^ (k & 1)) for k in range(1, 8)]
        _handshake(peers)
        mine = pltpu.make_async_copy(st_in, st_out.at[me_idx], local_sem)
        mine.start()
        cps = [pltpu.make_async_remote_copy(
            src_ref=st_in, dst_ref=st_out.at[me_idx], send_sem=st_send.at[k], recv_sem=st_recv.at[k],
            device_id=peer, device_id_type=MESH) for k, peer in enumerate(peers)]
        for cp in cps:
            cp.start()
        for cp in cps:
            cp.wait()
        mine.wait()

    return _sequencer_call(
        body, "exchange_stats", collective_id,
        jax.ShapeDtypeStruct((NDEV,) + stats.shape, stats.dtype),
        [pltpu.SemaphoreType.DMA((7,)), pltpu.SemaphoreType.DMA((7,)), pltpu.SemaphoreType.DMA])(stats)


class _Reduction:
    def __init__(self, place, first_collective_id):
        self.place = place
        self.ids = iter(range(first_collective_id, 32))
        self.groups = {}

    def next_id(self):
        return next(self.ids)

    def start(self, group, grads):
        got = _exchange_sibling(grads, "sibling_exchange_" + group[0], self.next_id())
        self.groups[group[0]] = dict(names=group, grads=grads, got=got)

    def local(self, name):
        grp = self.groups[name]
        grp["sums"] = [_chip_sum(self.place, g, s, "chip_sum_" + n)
                       for g, s, n in zip(grp["grads"], grp["got"], grp["names"])]
        grp["chips"] = _exchange_chips(grp["sums"], "chip_exchange_" + name, self.next_id())
        return grp["sums"]

    def landed(self, name):
        return list(self.groups[name]["chips"])

    def parts(self, name):
        for grp in self.groups.values():
            if name in grp["names"]:
                k = grp["names"].index(name)
                return grp["grads"][k], grp["got"][k], grp["chips"][k]
        raise KeyError(name)


def _adamw(w, g, m, v):
    m = ADAM_B1 * m + (1.0 - ADAM_B1) * g
    v = ADAM_B2 * v + (1.0 - ADAM_B2) * (g * g)
    m_hat = m / (1.0 - ADAM_B1 ** ADAM_STEP)
    v_hat = v / (1.0 - ADAM_B2 ** ADAM_STEP)
    delta = -ADAM_LR * (m_hat / (jnp.sqrt(v_hat) + ADAM_EPS) + ADAM_WD * w)
    return delta, m, v


def _shard_update(place, w, m, v, g, got_sib, got_chips, name):
    r, c = w.shape
    tm = _row_tile(r, c)

    def body(pos_ref, w_ref, m_ref, v_ref, g_ref, s_ref, c_ref, go_ref, d_ref, mo_ref, vo_ref):
        grad = g_ref[...].astype(F32) + s_ref[...].astype(F32)
        for j in range(3):
            grad = grad + c_ref[j].astype(F32)
        delta, mn, vn = _adamw(w_ref[...], grad, m_ref[...], v_ref[...])
        go_ref[...] = grad
        d_ref[...] = delta
        mo_ref[...] = mn
        vo_ref[...] = vn

    row = pl.BlockSpec((tm, c), lambda i, pos: (i, 0))
    return pl.pallas_call(
        body, name=name,
        grid_spec=pltpu.PrefetchScalarGridSpec(
            num_scalar_prefetch=1, grid=(r // tm,),
            in_specs=[row, row, row,
                      pl.BlockSpec((None, tm, c), lambda i, pos: (4 * pos[0] + 2 * pos[1] + pos[2], i, 0)),
                      pl.BlockSpec((None, tm, c), lambda i, pos: (2 * pos[0] + pos[1], i, 0)),
                      pl.BlockSpec((3, tm, c), lambda i, pos: (0, i, 0))],
            out_specs=[row, row, row, row]),
        out_shape=[jax.ShapeDtypeStruct((r, c), F32)] * 4,
        compiler_params=_cp(("parallel",)),
    )(place, w, m, v, g, got_sib, got_chips)


def _small_update(stats_all, ws, ms, vs):
    def body(st_ref, w_ref, m_ref, v_ref, go_ref, d_ref, mo_ref, vo_ref):
        grad = st_ref[0]
        for k in range(1, NDEV):
            grad = grad + st_ref[k]
        delta, mn, vn = _adamw(w_ref[...], grad, m_ref[...], v_ref[...])
        go_ref[...] = grad
        d_ref[...] = delta
        mo_ref[...] = mn
        vo_ref[...] = vn

    return pl.pallas_call(
        body, name="small_update",
        out_shape=[jax.ShapeDtypeStruct((8, D), F32)] * 4,
        compiler_params=_cp(),
    )(stats_all, ws, ms, vs)


def kernel(x, norm_mix_w, w_in, w_out, norm_ffn_w, w_gate, w_up, w_down, norm_final_w, loss_target, m_norm_mix_w, m_w_in, m_w_out, m_norm_ffn_w, m_w_gate, m_w_up, m_w_down, m_norm_final_w, v_norm_mix_w, v_w_in, v_w_out, v_norm_ffn_w, v_w_gate, v_w_up, v_w_down, v_norm_final_w):
    tr = {"w_gate", "w_up"}
    names = ["w_in", "w_out", "w_gate", "w_up", "w_down"]

    def view(a, n):
        return a[0].T if n in tr else a[0]

    big_w = [view(a, n) for a, n in zip([w_in, w_out, w_gate, w_up, w_down], names)]
    big_m = [view(a, n) for a, n in zip([m_w_in, m_w_out, m_w_gate, m_w_up, m_w_down], names)]
    big_v = [view(a, n) for a, n in zip([v_w_in, v_w_out, v_w_gate, v_w_up, v_w_down], names)]

    shards = [_cast_bf16(w, "cast_" + n) for w, n in zip(big_w, names)]
    (win,) = _all_gather(shards[0:1], "all_gather_w_in", 1)
    wout, wg, wu = _all_gather(shards[1:4], "all_gather_out_gate_up", 2)
    (wd,) = _all_gather(shards[4:5], "all_gather_w_down", 3)
    nw3 = norm_final_w.reshape(1, D)
    place = jnp.stack([lax.axis_index("x"), lax.axis_index("y"), lax.axis_index("c")]).astype(jnp.int32)
    red = _Reduction(place, first_collective_id=4)
    stats, gx, *_ = _local_step(
        x[0], loss_target[0], norm_mix_w, norm_ffn_w, nw3, win, wout.reshape(D, D), wg, wu, wd, red)
    stats_all = _exchange_stats(stats, red.next_id())
    upd = [_shard_update(place, w, m, v, *red.parts(n), "update_" + n)
           for w, m, v, n in zip(big_w, big_m, big_v, names)]

    def rows(a, b, c):
        return jnp.concatenate([a.reshape(1, D), b.reshape(1, D), c.reshape(1, D), jnp.zeros((5, D), F32)], axis=0)

    sg, sd, sm, sv = _small_update(stats_all, rows(norm_mix_w, norm_ffn_w, norm_final_w),
                                   rows(m_norm_mix_w, m_norm_ffn_w, m_norm_final_w),
                                   rows(v_norm_mix_w, v_norm_ffn_w, v_norm_final_w))
    loss = sg[3, 0]

    def outs(k, small):
        big = [(u[k].T if n in tr else u[k])[None] for u, n in zip(upd, names)]
        return [small[0:1], big[0], big[1], small[1:2], big[2], big[3], big[4], small[2]]

    return (loss, gx[None], *outs(0, sg), *outs(1, sd), *outs(2, sm), *outs(3, sv))
```

```python
import functools
import math

import numpy as np
import jax
import jax.numpy as jnp
from jax import lax
from jax.experimental import pallas as pl
from jax.experimental.pallas import tpu as pltpu
from jax.experimental.pallas import tpu_sc as plsc

F32 = jnp.float32
BF16 = jnp.bfloat16

S = 2048
D = 2048
NDEV = 8
N_IN = 7168 // NDEV
N_FF = 5632 // NDEV
N_OUT = 2048 // NDEV
AH, AHD = 8, 128
RH, RHD = 4, 256
CH = 128
NB = S // CH
EPS = 1e-6
PATTERNS = ((1, 16), (4, 4), (16, 1))
NEG = -1e30
VMEM_LIMIT = 56 * 1024 * 1024

ADAM_LR, ADAM_B1, ADAM_B2, ADAM_EPS, ADAM_WD, ADAM_STEP = 0.001, 0.9, 0.999, 1e-08, 0.01, 10
MESH = pl.DeviceIdType.MESH


def _cp(sem=None):
    return pltpu.CompilerParams(dimension_semantics=sem, vmem_limit_bytes=VMEM_LIMIT)


def _dot(a, b):
    return jnp.dot(a, b, preferred_element_type=F32)


def _dot_nt(a, b):
    return lax.dot_general(a, b, (((1,), (1,)), ((), ())), preferred_element_type=F32)


def _dot_tn(a, b):
    return lax.dot_general(a, b, (((0,), (0,)), ((), ())), preferred_element_type=F32)


def _sigmoid(x):
    return 1.0 / (1.0 + jnp.exp(-x))


def _cast_bf16(w, name):
    r, c = w.shape
    tm = r if r <= 1024 else 512

    def body(w_ref, o_ref):
        o_ref[...] = w_ref[...].astype(BF16)

    return pl.pallas_call(
        body, name=name, grid=(r // tm,),
        in_specs=[pl.BlockSpec((tm, c), lambda i: (i, 0))],
        out_specs=pl.BlockSpec((tm, c), lambda i: (i, 0)),
        out_shape=jax.ShapeDtypeStruct((r, c), BF16),
        compiler_params=_cp(("parallel",)),
    )(w)


def _rms_fwd(x, nw):
    tm = 256

    def body(x_ref, w_ref, h_ref, r_ref):
        xs = x_ref[...]
        r = lax.rsqrt(jnp.mean(xs * xs, axis=-1, keepdims=True) + EPS)
        h_ref[...] = ((xs * r) * w_ref[...]).astype(BF16)
        r_ref[...] = r

    return pl.pallas_call(
        body, name="rms_fwd", grid=(S // tm,),
        in_specs=[pl.BlockSpec((tm, D), lambda i: (i, 0)), pl.BlockSpec((1, D), lambda i: (0, 0))],
        out_specs=[pl.BlockSpec((tm, D), lambda i: (i, 0)), pl.BlockSpec((tm, 1), lambda i: (i, 0))],
        out_shape=[jax.ShapeDtypeStruct((S, D), BF16), jax.ShapeDtypeStruct((S, 1), F32)],
        compiler_params=_cp(("parallel",)),
    )(x, nw)


def _rms_bwd_tile(dh, xs, r, nw):
    dnw = jnp.sum(dh * (xs * r), axis=0, keepdims=True)
    gy = dh * nw
    dx = r * gy - xs * ((r * r * r) * jnp.mean(gy * xs, axis=-1, keepdims=True))
    return dx, dnw


def _proj(h1, win):
    tm = 512

    def body(a_ref, w_ref, o_ref):
        o_ref[...] = _dot(a_ref[...], w_ref[...])

    return pl.pallas_call(
        body, name="proj", grid=(NDEV, S // tm),
        in_specs=[pl.BlockSpec((tm, D), lambda p, m: (m, 0)),
                  pl.BlockSpec((None, D, N_IN), lambda p, m: (p, 0, 0))],
        out_specs=pl.BlockSpec((tm, N_IN), lambda p, m: (m, p)),
        out_shape=jax.ShapeDtypeStruct((S, NDEV * N_IN), F32),
        compiler_params=_cp(("parallel", "parallel")),
    )(h1, win)


def _out_proj_rms(x, ma, mr, wout, nw):
    tm = 256
    half = D // 2

    def body(x_ref, ma_ref, mr_ref, w_ref, nw_ref, x2_ref, h_ref, r_ref):
        acc = _dot(ma_ref[...], w_ref[0:half, :]) + _dot(mr_ref[...], w_ref[half:D, :])
        x2 = x_ref[...] + acc
        r = lax.rsqrt(jnp.mean(x2 * x2, axis=-1, keepdims=True) + EPS)
        x2_ref[...] = x2
        h_ref[...] = ((x2 * r) * nw_ref[...]).astype(BF16)
        r_ref[...] = r

    return pl.pallas_call(
        body, name="out_proj_rms", grid=(S // tm,),
        in_specs=[pl.BlockSpec((tm, D), lambda i: (i, 0)),
                  pl.BlockSpec((tm, half), lambda i: (i, 0)),
                  pl.BlockSpec((tm, half), lambda i: (i, 0)),
                  pl.BlockSpec((D, D), lambda i: (0, 0)),
                  pl.BlockSpec((1, D), lambda i: (0, 0))],
        out_specs=[pl.BlockSpec((tm, D), lambda i: (i, 0)), pl.BlockSpec((tm, D), lambda i: (i, 0)),
                   pl.BlockSpec((tm, 1), lambda i: (i, 0))],
        out_shape=[jax.ShapeDtypeStruct((S, D), F32), jax.ShapeDtypeStruct((S, D), BF16),
                   jax.ShapeDtypeStruct((S, 1), F32)],
        compiler_params=_cp(("parallel",)),
    )(x, ma, mr, wout, nw)


def _ffn_up(h2, wg, wu):
    tm = 512

    def body(h_ref, wg_ref, wu_ref, g_ref, u_ref, a_ref):
        h = h_ref[...]
        g = _dot_nt(h, wg_ref[...])
        u = _dot_nt(h, wu_ref[...])
        g_ref[...] = g
        u_ref[...] = u
        a_ref[...] = ((g * _sigmoid(g)) * u).astype(BF16)

    blk = pl.BlockSpec((None, tm, N_FF), lambda p, m: (p, m, 0))
    wblk = pl.BlockSpec((None, N_FF, D), lambda p, m: (p, 0, 0))
    return pl.pallas_call(
        body, name="ffn_up", grid=(NDEV, S // tm),
        in_specs=[pl.BlockSpec((tm, D), lambda p, m: (m, 0)), wblk, wblk],
        out_specs=[blk, blk, blk],
        out_shape=[jax.ShapeDtypeStruct((NDEV, S, N_FF), F32), jax.ShapeDtypeStruct((NDEV, S, N_FF), F32),
                   jax.ShapeDtypeStruct((NDEV, S, N_FF), BF16)],
        compiler_params=_cp(("parallel", "parallel")),
    )(h2, wg, wu)


def _ffn_down_loss(x2, a, wd, nw, tgt):
    tm = 512

    def body(x2_ref, a_ref, w_ref, nw_ref, t_ref, dx_ref, dxb_ref, st_ref, acc_ref):
        m, p = pl.program_id(0), pl.program_id(1)

        @pl.when(p == 0)
        def _():
            acc_ref[...] = jnp.zeros_like(acc_ref)

        @pl.when((p == 0) & (m == 0))
        def _():
            st_ref[...] = jnp.zeros_like(st_ref)

        acc_ref[...] += _dot(a_ref[...], w_ref[...])

        @pl.when(p == NDEV - 1)
        def _():
            x3 = x2_ref[...] + acc_ref[...]
            nwv = nw_ref[...]
            r = lax.rsqrt(jnp.mean(x3 * x3, axis=-1, keepdims=True) + EPS)
            y = (x3 * r) * nwv
            err = y - t_ref[...]
            loss = 0.5 * jnp.sum(jnp.mean(err * err, axis=-1, keepdims=True), axis=0, keepdims=True)
            dy = err * (1.0 / D)
            dx, dnw = _rms_bwd_tile(dy, x3, r, nwv)
            dx_ref[...] = dx
            dxb_ref[...] = dx.astype(BF16)
            st_ref[0:1, :] += dnw
            st_ref[1:2, :] += jnp.broadcast_to(loss, (1, D))

    return pl.pallas_call(
        body, name="ffn_down_loss", grid=(S // tm, NDEV),
        in_specs=[pl.BlockSpec((tm, D), lambda m, p: (m, 0)),
                  pl.BlockSpec((None, tm, N_FF), lambda m, p: (p, m, 0)),
                  pl.BlockSpec((None, N_FF, D), lambda m, p: (p, 0, 0)),
                  pl.BlockSpec((1, D), lambda m, p: (0, 0)),
                  pl.BlockSpec((tm, D), lambda m, p: (m, 0))],
        out_specs=[pl.BlockSpec((tm, D), lambda m, p: (m, 0)), pl.BlockSpec((tm, D), lambda m, p: (m, 0)),
                   pl.BlockSpec((8, D), lambda m, p: (0, 0))],
        out_shape=[jax.ShapeDtypeStruct((S, D), F32), jax.ShapeDtypeStruct((S, D), BF16),
                   jax.ShapeDtypeStruct((8, D), F32)],
        scratch_shapes=[pltpu.VMEM((tm, D), F32)],
        compiler_params=_cp(("arbitrary", "arbitrary")),
    )(x2, a, wd, nw, tgt)


def _ffn_down_bwd(dx3b, wd, g, u):
    tm = 512

    def body(dx_ref, w_ref, g_ref, u_ref, dg_ref, du_ref):
        da = _dot_nt(dx_ref[...], w_ref[...])
        gv = g_ref[...]
        sg = _sigmoid(gv)
        silu = gv * sg
        dg_ref[...] = ((da * u_ref[...]) * (sg * (1.0 + gv * (1.0 - sg)))).astype(BF16)
        du_ref[...] = (da * silu).astype(BF16)

    blk = pl.BlockSpec((None, tm, N_FF), lambda p, m: (p, m, 0))
    return pl.pallas_call(
        body, name="ffn_down_bwd", grid=(NDEV, S // tm),
        in_specs=[pl.BlockSpec((tm, D), lambda p, m: (m, 0)),
                  pl.BlockSpec((None, N_FF, D), lambda p, m: (p, 0, 0)), blk, blk],
        out_specs=[blk, blk],
        out_shape=[jax.ShapeDtypeStruct((NDEV, S, N_FF), BF16), jax.ShapeDtypeStruct((NDEV, S, N_FF), BF16)],
        compiler_params=_cp(("parallel", "parallel")),
    )(dx3b, wd, g, u)


def _ffn_up_bwd(dg, du, wg, wu, dres, xs, r, nw):
    tm = 512

    def body(dg_ref, du_ref, wg_ref, wu_ref, dres_ref, x_ref, r_ref, nw_ref, dx_ref, dxb_ref, st_ref, acc_ref):
        m, p = pl.program_id(0), pl.program_id(1)

        @pl.when(p == 0)
        def _():
            acc_ref[...] = jnp.zeros_like(acc_ref)

        @pl.when((p == 0) & (m == 0))
        def _():
            st_ref[...] = jnp.zeros_like(st_ref)

        acc_ref[...] += _dot(dg_ref[...], wg_ref[...]) + _dot(du_ref[...], wu_ref[...])

        @pl.when(p == NDEV - 1)
        def _():
            dx, dnw = _rms_bwd_tile(acc_ref[...], x_ref[...], r_ref[...], nw_ref[...])
            dx = dres_ref[...] + dx
            dx_ref[...] = dx
            dxb_ref[...] = dx.astype(BF16)
            st_ref[0:1, :] += dnw

    blk = pl.BlockSpec((None, tm, N_FF), lambda m, p: (p, m, 0))
    wblk = pl.BlockSpec((None, N_FF, D), lambda m, p: (p, 0, 0))
    row = pl.BlockSpec((tm, D), lambda m, p: (m, 0))
    return pl.pallas_call(
        body, name="ffn_up_bwd", grid=(S // tm, NDEV),
        in_specs=[blk, blk, wblk, wblk, row, row, pl.BlockSpec((tm, 1), lambda m, p: (m, 0)),
                  pl.BlockSpec((1, D), lambda m, p: (0, 0))],
        out_specs=[row, row, pl.BlockSpec((8, D), lambda m, p: (0, 0))],
        out_shape=[jax.ShapeDtypeStruct((S, D), F32), jax.ShapeDtypeStruct((S, D), BF16),
                   jax.ShapeDtypeStruct((8, D), F32)],
        scratch_shapes=[pltpu.VMEM((tm, D), F32)],
        compiler_params=_cp(("arbitrary", "arbitrary")),
    )(dg, du, wg, wu, dres, xs, r, nw)


def _out_proj_bwd(dx2b, wout):
    tm = 256

    def body(dx_ref, w_ref, o_ref):
        o_ref[...] = _dot_nt(dx_ref[...], w_ref[...])

    return pl.pallas_call(
        body, name="out_proj_bwd", grid=(S // tm,),
        in_specs=[pl.BlockSpec((tm, D), lambda i: (i, 0)), pl.BlockSpec((D, D), lambda i: (0, 0))],
        out_specs=pl.BlockSpec((tm, D), lambda i: (i, 0)),
        out_shape=jax.ShapeDtypeStruct((S, D), F32),
        compiler_params=_cp(("parallel",)),
    )(dx2b, wout)


def _in_proj_bwd(dproj, win, dres, xs, r, nw):
    tm = 512

    def body(dp_ref, w_ref, dres_ref, x_ref, r_ref, nw_ref, dx_ref, st_ref, acc_ref):
        m, p = pl.program_id(0), pl.program_id(1)

        @pl.when(p == 0)
        def _():
            acc_ref[...] = jnp.zeros_like(acc_ref)

        @pl.when((p == 0) & (m == 0))
        def _():
            st_ref[...] = jnp.zeros_like(st_ref)

        acc_ref[...] += _dot_nt(dp_ref[...], w_ref[...])

        @pl.when(p == NDEV - 1)
        def _():
            dx, dnw = _rms_bwd_tile(acc_ref[...], x_ref[...], r_ref[...], nw_ref[...])
            dx_ref[...] = dres_ref[...] + dx
            st_ref[0:1, :] += dnw

    row = pl.BlockSpec((tm, D), lambda m, p: (m, 0))
    return pl.pallas_call(
        body, name="in_proj_bwd", grid=(S // tm, NDEV),
        in_specs=[pl.BlockSpec((tm, N_IN), lambda m, p: (m, p)),
                  pl.BlockSpec((None, D, N_IN), lambda m, p: (p, 0, 0)),
                  row, row, pl.BlockSpec((tm, 1), lambda m, p: (m, 0)),
                  pl.BlockSpec((1, D), lambda m, p: (0, 0))],
        out_specs=[row, pl.BlockSpec((8, D), lambda m, p: (0, 0))],
        out_shape=[jax.ShapeDtypeStruct((S, D), F32), jax.ShapeDtypeStruct((8, D), F32)],
        scratch_shapes=[pltpu.VMEM((tm, D), F32)],
        compiler_params=_cp(("arbitrary", "arbitrary")),
    )(dproj, win, dres, xs, r, nw)


def _wgrad_in(h1, dproj):
    def body(a_ref, d_ref, o_ref):
        o_ref[...] = _dot_tn(a_ref[...], d_ref[...]).astype(BF16)

    return pl.pallas_call(
        body, name="wgrad_in", grid=(NDEV,),
        in_specs=[pl.BlockSpec((S, D), lambda p: (0, 0)), pl.BlockSpec((S, N_IN), lambda p: (0, p))],
        out_specs=pl.BlockSpec((None, D, N_IN), lambda p: (p, 0, 0)),
        out_shape=jax.ShapeDtypeStruct((NDEV, D, N_IN), BF16),
        compiler_params=_cp(("parallel",)),
    )(h1, dproj)


def _wgrad_rows(a3, dy, name):
    def body(a_ref, d_ref, o_ref):
        o_ref[...] = _dot_tn(a_ref[...], d_ref[...]).astype(BF16)

    return pl.pallas_call(
        body, name=name, grid=(NDEV,),
        in_specs=[pl.BlockSpec((None, S, N_FF), lambda p: (p, 0, 0)), pl.BlockSpec((S, D), lambda p: (0, 0))],
        out_specs=pl.BlockSpec((None, N_FF, D), lambda p: (p, 0, 0)),
        out_shape=jax.ShapeDtypeStruct((NDEV, N_FF, D), BF16),
        compiler_params=_cp(("parallel",)),
    )(a3, dy)


def _wgrad_out(ma, mr, dx2b):
    half = D // 2
    per = half // N_OUT

    def body(ma_ref, mr_ref, d_ref, o_ref):
        p = pl.program_id(0)

        @pl.when(p < per)
        def _():
            o_ref[...] = _dot_tn(ma_ref[...], d_ref[...]).astype(BF16)

        @pl.when(p >= per)
        def _():
            o_ref[...] = _dot_tn(mr_ref[...], d_ref[...]).astype(BF16)

    return pl.pallas_call(
        body, name="wgrad_out", grid=(NDEV,),
        in_specs=[pl.BlockSpec((S, N_OUT), lambda p: (0, jnp.minimum(p, per - 1))),
                  pl.BlockSpec((S, N_OUT), lambda p: (0, jnp.maximum(p - per, 0))),
                  pl.BlockSpec((S, D), lambda p: (0, 0))],
        out_specs=pl.BlockSpec((None, N_OUT, D), lambda p: (p, 0, 0)),
        out_shape=jax.ShapeDtypeStruct((NDEV, N_OUT, D), BF16),
        compiler_params=_cp(("parallel",)),
    )(ma, mr, dx2b)


def _attn_consts():
    c = np.zeros((AH, 8, AHD), np.float32)
    for h in range(AH):
        c[h, :, :] = 2.0 ** (-(h + 1))
    return jnp.asarray(c)


def _permute_in(dst, src, d, cast=None):
    ln = S // d
    for rr in range(d):
        v = src[pl.ds(rr, ln, stride=d), :] if d > 1 else src[...]
        dst[rr * ln:(rr + 1) * ln, :] = v if cast is None else v.astype(cast)


def _attn_masks():
    qi = lax.broadcasted_iota(jnp.int32, (CH, CH), 0)
    kj = lax.broadcasted_iota(jnp.int32, (CH, CH), 1)
    dist_c = (qi - kj).astype(F32)
    dist_p = (qi - kj + CH).astype(F32)
    return (qi >= kj)[None], (kj >= qi)[None], dist_c[None], dist_p[None]


GB = 8


def _bdot_nt(a, b):
    return lax.dot_general(a, b, (((2,), (2,)), ((0,), (0,))), preferred_element_type=F32)


def _bdot(a, b):
    return lax.dot_general(a, b, (((2,), (1,)), ((0,), (0,))), preferred_element_type=F32)


def _bdot_tn(a, b):
    return lax.dot_general(a, b, (((1,), (1,)), ((0,), (0,))), preferred_element_type=F32)


def _shift_block(dst, src):
    dst[0:CH, :] = jnp.zeros((CH, AHD), dst.dtype)
    dst[CH:S, :] = src[0:S - CH, :]


def _has_prev(g, nb):
    blk = lax.broadcasted_iota(jnp.int32, (GB, 1, 1), 0) + g * GB
    return (blk & (nb - 1)) != 0


def _blocks(ref, g):
    return ref[g * GB * CH:(g + 1) * GB * CH, :].reshape(GB, CH, AHD)


def _attn_fwd(proj):
    scale = 1.0 / math.sqrt(AHD)

    def body(c_ref, q_ref, k_ref, v_ref, o_ref, ob_ref, lse_ref, qd, kd, vd, kps, vps, od, ld, *nat):
        onat, lnat = nat[0:3], nat[3:6]
        slope = c_ref[0:1, :]
        mask_c, mask_p, dist_c, dist_p = _attn_masks()
        for pi, (d, nb) in enumerate(PATTERNS):
            _permute_in(qd, q_ref, d, BF16)
            _permute_in(kd, k_ref, d, BF16)
            _permute_in(vd, v_ref, d, BF16)
            if nb > 1:
                _shift_block(kps, kd)
                _shift_block(vps, vd)
            bias_c = -(slope * float(d)) * dist_c
            bias_p = -(slope * float(d)) * dist_p
            for g in range(NB // GB):
                q3, k3, v3 = _blocks(qd, g), _blocks(kd, g), _blocks(vd, g)
                s_c = jnp.where(mask_c, _bdot_nt(q3, k3) * scale + bias_c, NEG)
                mx = jnp.max(s_c, axis=-1, keepdims=True)
                if nb > 1:
                    kp3, vp3 = _blocks(kps, g), _blocks(vps, g)
                    s_p = jnp.where(jnp.logical_and(mask_p, _has_prev(g, nb)),
                                    _bdot_nt(q3, kp3) * scale + bias_p, NEG)
                    mx = jnp.maximum(mx, jnp.max(s_p, axis=-1, keepdims=True))
                    l = (jnp.sum(jnp.exp(s_c - mx), axis=-1, keepdims=True)
                         + jnp.sum(jnp.exp(s_p - mx), axis=-1, keepdims=True))
                    lse = mx + jnp.log(l)
                    o3 = _bdot(jnp.exp(s_c - lse).astype(BF16), v3) + _bdot(jnp.exp(s_p - lse).astype(BF16), vp3)
                else:
                    l = jnp.sum(jnp.exp(s_c - mx), axis=-1, keepdims=True)
                    lse = mx + jnp.log(l)
                    o3 = _bdot(jnp.exp(s_c - lse).astype(BF16), v3)
                rows = slice(g * GB * CH, (g + 1) * GB * CH)
                od[rows, :] = o3.reshape(GB * CH, AHD)
                ld[rows, :] = jnp.broadcast_to(lse, (GB, CH, AHD)).reshape(GB * CH, AHD)
            ln = S // d
            for rr in range(d):
                if d > 1:
                    onat[pi][pl.ds(rr, ln, stride=d), :] = od[rr * ln:(rr + 1) * ln, :]
                    lnat[pi][pl.ds(rr, ln, stride=d), :] = ld[rr * ln:(rr + 1) * ln, :]
                else:
                    onat[pi][...] = od[...]
                    lnat[pi][...] = ld[...]
        l0, l1, l2 = lnat[0][...], lnat[1][...], lnat[2][...]
        mx = jnp.maximum(jnp.maximum(l0, l1), l2)
        e0, e1, e2 = jnp.exp(l0 - mx), jnp.exp(l1 - mx), jnp.exp(l2 - mx)
        den = e0 + e1 + e2
        out = (e0 / den) * onat[0][...] + (e1 / den) * onat[1][...] + (e2 / den) * onat[2][...]
        o_ref[...] = out
        ob_ref[...] = out.astype(BF16)
        lse_ref[...] = mx + jnp.log(den)

    def col(off):
        return pl.BlockSpec((S, AHD), lambda h: (0, off + h))

    return pl.pallas_call(
        body, name="attn_fwd", grid=(AH,),
        in_specs=[pl.BlockSpec((None, 8, AHD), lambda h: (h, 0, 0)), col(0), col(AH), col(2 * AH)],
        out_specs=[col(0), col(0), col(0)],
        out_shape=[jax.ShapeDtypeStruct((S, AH * AHD), F32), jax.ShapeDtypeStruct((S, AH * AHD), BF16),
                   jax.ShapeDtypeStruct((S, AH * AHD), F32)],
        scratch_shapes=[pltpu.VMEM((S, AHD), BF16) for _ in range(5)]
        + [pltpu.VMEM((S, AHD), F32) for _ in range(8)],
        compiler_params=_cp(("parallel",)),
    )(_attn_consts(), proj, proj, proj)


def _attn_bwd(proj, dmixed, o, lse):
    scale = 1.0 / math.sqrt(AHD)

    def body(c_ref, q_ref, k_ref, v_ref, do_ref, o_ref, lse_ref, dq_ref, dk_ref, dv_ref,
             qd, kd, vd, dod, kps, vps, lsd, dld, dqd, dkd, dvd, delta, aq, ak, av):
        slope = c_ref[0:1, :]
        mask_c, mask_p, dist_c, dist_p = _attn_masks()
        delta[...] = jnp.broadcast_to(jnp.sum(do_ref[...] * o_ref[...], axis=-1, keepdims=True), (S, AHD))
        for pi, (d, nb) in enumerate(PATTERNS):
            _permute_in(qd, q_ref, d, BF16)
            _permute_in(kd, k_ref, d, BF16)
            _permute_in(vd, v_ref, d, BF16)
            _permute_in(dod, do_ref, d, BF16)
            _permute_in(lsd, lse_ref, d)
            _permute_in(dld, delta, d)
            if nb > 1:
                _shift_block(kps, kd)
                _shift_block(vps, vd)
            bias_c = -(slope * float(d)) * dist_c
            bias_p = -(slope * float(d)) * dist_p
            for g in range(NB // GB):
                q3, k3, v3, do3 = _blocks(qd, g), _blocks(kd, g), _blocks(vd, g), _blocks(dod, g)
                ls, dl = _blocks(lsd, g), _blocks(dld, g)
                lo, hi = g * GB * CH, (g + 1) * GB * CH
                p_c = jnp.exp(jnp.where(mask_c, _bdot_nt(q3, k3) * scale + bias_c, NEG) - ls)
                ds_c = ((p_c * (_bdot_nt(do3, v3) - dl)) * scale).astype(BF16)
                dq3 = _bdot(ds_c, k3)
                dkd[lo:hi, :] = _bdot_tn(ds_c, q3).reshape(GB * CH, AHD)
                dvd[lo:hi, :] = _bdot_tn(p_c.astype(BF16), do3).reshape(GB * CH, AHD)
                if nb > 1:
                    kp3, vp3 = _blocks(kps, g), _blocks(vps, g)
                    p_p = jnp.exp(jnp.where(jnp.logical_and(mask_p, _has_prev(g, nb)),
                                            _bdot_nt(q3, kp3) * scale + bias_p, NEG) - ls)
                    ds_p = ((p_p * (_bdot_nt(do3, vp3) - dl)) * scale).astype(BF16)
                    dq3 = dq3 + _bdot(ds_p, kp3)
                    dkp = _bdot_tn(ds_p, q3).reshape(GB * CH, AHD)
                    dvp = _bdot_tn(p_p.astype(BF16), do3).reshape(GB * CH, AHD)
                    if g == 0:
                        dkd[0:hi - CH, :] += dkp[CH:, :]
                        dvd[0:hi - CH, :] += dvp[CH:, :]
                    else:
                        dkd[lo - CH:hi - CH, :] += dkp
                        dvd[lo - CH:hi - CH, :] += dvp
                dqd[lo:hi, :] = dq3.reshape(GB * CH, AHD)
            ln = S // d
            for acc, src in ((aq, dqd), (ak, dkd), (av, dvd)):
                if pi == 0:
                    acc[...] = src[...]
                else:
                    for rr in range(d):
                        acc[pl.ds(rr, ln, stride=d), :] += src[rr * ln:(rr + 1) * ln, :]
        dq_ref[...] = aq[...].astype(BF16)
        dk_ref[...] = ak[...].astype(BF16)
        dv_ref[...] = av[...].astype(BF16)

    def col(off):
        return pl.BlockSpec((S, AHD), lambda h: (0, off + h))

    return pl.pallas_call(
        body, name="attn_bwd", grid=(AH,),
        in_specs=[pl.BlockSpec((None, 8, AHD), lambda h: (h, 0, 0)), col(0), col(AH), col(2 * AH),
                  col(0), col(0), col(0)],
        out_specs=[col(0), col(0), col(0)],
        out_shape=[jax.ShapeDtypeStruct((S, AH * AHD), BF16)] * 3,
        scratch_shapes=[pltpu.VMEM((S, AHD), BF16) for _ in range(6)]
        + [pltpu.VMEM((S, AHD), F32) for _ in range(9)],
        compiler_params=_cp(("parallel",)),
    )(_attn_consts(), proj, proj, proj, dmixed, o, lse)


def _ret_consts():
    c = np.zeros((RH, 8, RHD), np.float32)
    for h in range(RH):
        c[h, :, :] = np.log(np.float32(1.0) - np.float32(2.0 ** (-5.0 - h)))
    return jnp.asarray(c)


def _ret_factors(lg):
    i = lax.broadcasted_iota(jnp.int32, (CH, CH), 0)
    j = lax.broadcasted_iota(jnp.int32, (CH, CH), 1)
    dif = (i - j).astype(F32)
    decay = jnp.where(dif >= 0, jnp.exp(lg[:, 0:CH] * jnp.maximum(dif, 0.0)), 0.0)
    row = lax.broadcasted_iota(jnp.int32, (CH, RHD), 0).astype(F32)
    zeta = jnp.exp(lg * (CH - 1.0 - row))
    xi = jnp.exp(lg * (row + 1.0))
    return decay, zeta, xi, jnp.exp(lg * float(CH))


def _ret_specs(rev):
    off = 3 * AH * AHD // RHD

    def ch(n):
        return (NB - 1 - n) if rev else n

    def col(k):
        return pl.BlockSpec((CH, RHD), lambda h, n: (ch(n), off + k * RH + h))

    own = pl.BlockSpec((CH, RHD), lambda h, n: (ch(n), h))
    state = pl.BlockSpec((None, None, RHD, RHD), lambda h, n: (h, ch(n), 0, 0))
    const = pl.BlockSpec((None, 8, RHD), lambda h, n: (h, 0, 0))
    return col, own, state, const


def _ret_fwd(proj):
    def body(c_ref, q_ref, k_ref, v_ref, g_ref, ret_ref, mr_ref, st_ref, r_acc):
        n = pl.program_id(1)

        @pl.when(n == 0)
        def _():
            r_acc[...] = jnp.zeros_like(r_acc)

        decay, zeta, xi, gch = _ret_factors(c_ref[0:1, :])
        qb = q_ref[...].astype(BF16)
        kc = k_ref[...] * (1.0 / math.sqrt(RHD))
        kb = kc.astype(BF16)
        vb = v_ref[...].astype(BF16)
        rb = r_acc[...].astype(BF16)
        st_ref[...] = rb
        scores = _dot_nt(qb, kb) * decay
        ret = _dot(scores.astype(BF16), vb) + _dot(qb, rb) * xi
        r_acc[...] = r_acc[...] * gch + _dot_tn((kc * zeta).astype(BF16), vb)
        ret_ref[...] = ret
        rr = lax.rsqrt(jnp.mean(ret * ret, axis=-1, keepdims=True) + EPS)
        gv = g_ref[...]
        mr_ref[...] = ((gv * _sigmoid(gv)) * (ret * rr)).astype(BF16)

    col, own, state, const = _ret_specs(False)
    return pl.pallas_call(
        body, name="ret_fwd", grid=(RH, NB),
        in_specs=[const, col(0), col(1), col(2), col(3)],
        out_specs=[own, own, state],
        out_shape=[jax.ShapeDtypeStruct((S, RH * RHD), F32), jax.ShapeDtypeStruct((S, RH * RHD), BF16),
                   jax.ShapeDtypeStruct((RH, NB, RHD, RHD), BF16)],
        scratch_shapes=[pltpu.VMEM((RHD, RHD), F32)],
        compiler_params=_cp(("parallel", "arbitrary")),
    )(_ret_consts(), proj, proj, proj, proj)


def _ret_bwd(proj, ret, states, dmixed):
    def body(c_ref, q_ref, k_ref, v_ref, g_ref, ret_ref, st_ref, dm_ref, dq_ref, dk_ref, dv_ref, dg_ref, g_acc):
        n = pl.program_id(1)

        @pl.when(n == 0)
        def _():
            g_acc[...] = jnp.zeros_like(g_acc)

        decay, zeta, xi, gch = _ret_factors(c_ref[0:1, :])
        ret_v = ret_ref[...]
        rr = lax.rsqrt(jnp.mean(ret_v * ret_v, axis=-1, keepdims=True) + EPS)
        gv = g_ref[...]
        sg = _sigmoid(gv)
        dmix = dm_ref[...]
        dg_ref[...] = ((dmix * (ret_v * rr)) * (sg * (1.0 + gv * (1.0 - sg)))).astype(BF16)
        dretn = dmix * (gv * sg)
        dret = rr * dretn - ret_v * ((rr * rr * rr) * jnp.mean(dretn * ret_v, axis=-1, keepdims=True))

        qb = q_ref[...].astype(BF16)
        kc = k_ref[...] * (1.0 / math.sqrt(RHD))
        kb = kc.astype(BF16)
        vb = v_ref[...].astype(BF16)
        rb = st_ref[...]
        db = dret.astype(BF16)
        sc = (_dot_nt(qb, kb) * decay).astype(BF16)
        da = (_dot_nt(db, vb) * decay).astype(BF16)
        dxi = (dret * xi).astype(BF16)
        gb = g_acc[...].astype(BF16)
        kz = (kc * zeta).astype(BF16)
        dq = _dot(da, kb) + _dot_nt(dxi, rb)
        dkc = _dot_tn(da, qb) + _dot_nt(vb, gb) * zeta
        dv = _dot_tn(sc, db) + _dot(kz, gb)
        g_acc[...] = _dot_tn(qb, dxi) + gch * g_acc[...]
        dq_ref[...] = dq.astype(BF16)
        dk_ref[...] = (dkc * (1.0 / math.sqrt(RHD))).astype(BF16)
        dv_ref[...] = dv.astype(BF16)

    col, own, state, const = _ret_specs(True)
    dm = pl.BlockSpec((CH, RHD), lambda h, n: (NB - 1 - n, AH * AHD // RHD + h))
    return pl.pallas_call(
        body, name="ret_bwd", grid=(RH, NB),
        in_specs=[const, col(0), col(1), col(2), col(3), own, state, dm],
        out_specs=[own, own, own, own],
        out_shape=[jax.ShapeDtypeStruct((S, RH * RHD), BF16)] * 4,
        scratch_shapes=[pltpu.VMEM((RHD, RHD), F32)],
        compiler_params=_cp(("parallel", "arbitrary")),
    )(_ret_consts(), proj, proj, proj, proj, ret, states, dmixed)


class _NoReduction:
    def start(self, group, grads):
        pass

    def local(self, name, first=()):
        return []

    def landed(self, name):
        return []

    def update(self, name):
        return []


def _local_step(x, tgt, nw1, nw2, nw3, win, wout, wg, wu, wd, red=None):
    red = red or _NoReduction()

    def after(values, first):
        return lax.optimization_barrier((tuple(values), tuple(first)))[0]

    h1, r1 = _rms_fwd(x, nw1)
    proj = _proj(h1, win)
    o, ma, lse = _attn_fwd(proj)
    ret, mr, states = _ret_fwd(proj)
    x2, h2, r2 = _out_proj_rms(x, ma, mr, wout, nw2)
    g, u, a = _ffn_up(h2, wg, wu)
    dx3, dx3b, st3 = _ffn_down_loss(x2, a, wd, nw3, tgt)

    dwd = _wgrad_rows(a, dx3b, "wgrad_down")
    red.start(["w_down"], [dwd])
    (dx3b,) = after([dx3b], [dwd])
    dg, du = _ffn_down_bwd(dx3b, wd, g, u)
    dg, du = after([dg, du], red.local("w_down", first=[dg]))
    dwg = _wgrad_rows(dg, h2, "wgrad_gate")
    dwu = _wgrad_rows(du, h2, "wgrad_up")
    red.start(["w_gate", "w_up"], [dwg, dwu])
    dg, du = after([dg, du], [dwg, dwu])
    dx2, dx2b, st2 = _ffn_up_bwd(dg, du, wg, wu, dx3, x2, r2, nw2)
    (dx2b,) = after([dx2b], red.landed("w_down"))
    dwo = _wgrad_out(ma, mr, dx2b)
    red.start(["w_out"], [dwo])
    (dx2b,) = after([dx2b], [dwo] + red.local("w_gate"))
    dmixed = _out_proj_bwd(dx2b, wout)
    dqa, dka, dva = _attn_bwd(proj, dmixed, o, lse)
    (dmixed,) = after([dmixed], [dqa] + red.local("w_out"))
    dqr, dkr, dvr, dgr = _ret_bwd(proj, ret, states, dmixed)
    dproj = jnp.concatenate([dqa, dka, dva, dqr, dkr, dvr, dgr], axis=1)
    dwi = _wgrad_in(h1, dproj)
    red.start(["w_in"], after([dwi], red.landed("w_gate") + red.landed("w_out")))
    early = red.update("w_down") + red.update("w_gate")
    (dproj,) = after([dproj], red.local("w_in", first=early))
    gx, st1 = _in_proj_bwd(dproj, win, dx2, x, r1, nw1)
    stats = jnp.concatenate([st1[0:1], st2[0:1], st3[0:2], jnp.zeros((4, D), F32)], axis=0)
    return stats, gx, dwi, dwo, dwg, dwu, dwd


def _place():
    x, y, c = lax.axis_index("x"), lax.axis_index("y"), lax.axis_index("c")
    return x, y, c, [(1 - x, y), (x, 1 - y), (1 - x, 1 - y)]


def _handshake(peers):
    barrier = pltpu.get_barrier_semaphore()
    for peer in peers:
        pl.semaphore_signal(barrier, inc=1, device_id=peer, device_id_type=MESH)
    pl.semaphore_wait(barrier, len(peers))


def _all_gather(shards, name, collective_id):
    na = len(shards)
    SIB, XN0, XN1, YN1, YN0, VIA_X, VIA_Y = 0, 1, 2, 3, 4, 5, 6
    D2D = {XN0: 7, XN1: 8, YN1: 9, YN0: 10, VIA_X: 11, VIA_Y: 12}

    def body(*refs):
        ins, outs = refs[:na], refs[na:2 * na]
        send_sems, recv_sems, local_sems = refs[2 * na:]
        x, y, c, _ = _place()
        me, sib = (x, y, c), (x, y, 1 - c)
        xn, yn, dg = (1 - x, y, c), (x, 1 - y, c), (1 - x, 1 - y, c)
        _handshake([sib, xn, yn])

        def part(ref, h):
            rows = ref.shape[0] // 2
            return ref if h is None else ref.at[pl.ds(h * rows, rows)]

        def block(a, owner, h):
            return part(outs[a].at[4 * owner[0] + 2 * owner[1] + owner[2]], h)

        def copy(a, k, owner, h, to, own_src=False):
            return pltpu.make_async_remote_copy(
                src_ref=part(ins[a], h) if own_src else block(a, owner, h), dst_ref=block(a, owner, h),
                send_sem=send_sems.at[a, k], recv_sem=recv_sems.at[a, k], device_id=to, device_id_type=MESH)

        def other(p):
            return (p[0], p[1], 1 - c)

        mine = [pltpu.make_async_copy(ins[a], block(a, me, None), local_sems.at[a]) for a in range(na)]
        for cp in mine:
            cp.start()
        sent = []
        for a in range(na):
            sent += [copy(a, XN0, me, 0, xn, True), copy(a, YN1, me, 1, yn, True),
                     copy(a, XN1, me, 1, xn, True), copy(a, YN0, me, 0, yn, True)]
        sent += [copy(a, SIB, me, None, sib, True) for a in range(na)]
        for cp in sent:
            cp.start()

        def landed(a, k, owner, h, then):
            copy(a, k, owner, h, me).wait_recv()
            for k2, to in then + [(D2D[k], sib)]:
                cp = copy(a, k2, owner, h, to)
                cp.start()
                sent.append(cp)

        for a in range(na):
            landed(a, XN0, xn, 0, [(VIA_Y, yn)])
            landed(a, YN1, yn, 1, [(VIA_X, xn)])
            landed(a, XN1, xn, 1, [])
            landed(a, YN0, yn, 0, [])
        for a in range(na):
            landed(a, VIA_Y, dg, 0, [])
            landed(a, VIA_X, dg, 1, [])
        for a in range(na):
            copy(a, SIB, sib, None, me).wait_recv()
            for k, owner, h in ((XN0, xn, 0), (XN1, xn, 1), (YN1, yn, 1), (YN0, yn, 0), (VIA_Y, dg, 0), (VIA_X, dg, 1)):
                copy(a, D2D[k], other(owner), h, me).wait_recv()
        for cp in sent:
            cp.wait_send()
        for cp in mine:
            cp.wait()

    return _sequencer_call(
        body, name, collective_id,
        [jax.ShapeDtypeStruct((NDEV,) + s.shape, s.dtype) for s in shards],
        [pltpu.SemaphoreType.DMA((na, 13)), pltpu.SemaphoreType.DMA((na, 13)), pltpu.SemaphoreType.DMA((na,))])(*shards)


def _sequencer_call(body, name, collective_id, out_type, scratch_types):
    return pl.kernel(
        body, name=name, out_type=out_type,
        mesh=plsc.ScalarSubcoreMesh(axis_name="sequencer", num_cores=1),
        scratch_types=scratch_types,
        compiler_params=pltpu.CompilerParams(collective_id=collective_id))


def _exchange_sibling(grads, name, collective_id):
    na = len(grads)

    def body(*refs):
        ins, outs = refs[:na], refs[na:2 * na]
        send_sems, recv_sems = refs[2 * na:]
        x, y, c, _ = _place()
        _handshake([(x, y, 1 - c)])
        cps = []
        for a in range(na):
            for k in range(4):
                cps.append(pltpu.make_async_remote_copy(
                    src_ref=ins[a].at[2 * k + (1 - c)], dst_ref=outs[a].at[k],
                    send_sem=send_sems.at[a, k], recv_sem=recv_sems.at[a, k],
                    device_id=(x, y, 1 - c), device_id_type=MESH))
        for cp in cps:
            cp.start()
        for cp in cps:
            cp.wait()

    return _sequencer_call(
        body, name, collective_id,
        [jax.ShapeDtypeStruct((4,) + g.shape[1:], g.dtype) for g in grads],
        [pltpu.SemaphoreType.DMA((na, 4)), pltpu.SemaphoreType.DMA((na, 4))])(*grads)


def _row_tile(rows, cols):
    for t in (512, 256, 176, 128, 64, 32, 16):
        if rows % t == 0 and t * cols * 4 <= (1 << 20):
            return t
    raise ValueError((rows, cols))


def _chip_sum(place, g, got, name):
    _, r, c = g.shape
    tm = _row_tile(r, c)

    def body(pos_ref, g_ref, got_ref, o_ref):
        o_ref[...] = (g_ref[...].astype(F32) + got_ref[...].astype(F32)).astype(BF16)

    def chip(j, pos):
        return 2 * (pos[0] ^ jnp.where(j == 1, 0, 1)) + (pos[1] ^ jnp.where(j == 0, 0, 1))

    return pl.pallas_call(
        body, name=name,
        grid_spec=pltpu.PrefetchScalarGridSpec(
            num_scalar_prefetch=1, grid=(3, r // tm),
            in_specs=[pl.BlockSpec((None, tm, c), lambda j, i, pos: (2 * chip(j, pos) + pos[2], i, 0)),
                      pl.BlockSpec((None, tm, c), lambda j, i, pos: (chip(j, pos), i, 0))],
            out_specs=pl.BlockSpec((None, tm, c), lambda j, i, pos: (j, i, 0))),
        out_shape=jax.ShapeDtypeStruct((3, r, c), BF16),
        compiler_params=_cp(("parallel", "parallel")),
    )(place, g, got)


def _exchange_chips(sums, name, collective_id):
    na = len(sums)

    def body(*refs):
        ins, outs = refs[:na], refs[na:2 * na]
        send_sems, recv_sems = refs[2 * na:]
        x, y, c, chips = _place()
        _handshake([(*chip, c) for chip in chips])
        cps = []
        for a in range(na):
            for j, chip in enumerate(chips):
                cps.append(pltpu.make_async_remote_copy(
                    src_ref=ins[a].at[j], dst_ref=outs[a].at[j],
                    send_sem=send_sems.at[a, j], recv_sem=recv_sems.at[a, j],
                    device_id=(*chip, c), device_id_type=MESH))
        for cp in cps:
            cp.start()
        for cp in cps:
            cp.wait()

    return _sequencer_call(
        body, name, collective_id,
        [jax.ShapeDtypeStruct((3,) + s.shape[1:], s.dtype) for s in sums],
        [pltpu.SemaphoreType.DMA((na, 3)), pltpu.SemaphoreType.DMA((na, 3))])(*sums)


def _exchange_stats(stats, collective_id):
    def body(st_in, st_out, st_send, st_recv, local_sem):
        x, y, c, _ = _place()
        me_idx = 4 * x + 2 * y + c
        peers = [(x ^ ((k >> 2) & 1), y ^ ((k >> 1) & 1), c ^ (k & 1)) for k in range(1, 8)]
        _handshake(peers)
        mine = pltpu.make_async_copy(st_in, st_out.at[me_idx], local_sem)
        mine.start()
        cps = [pltpu.make_async_remote_copy(
            src_ref=st_in, dst_ref=st_out.at[me_idx], send_sem=st_send.at[k], recv_sem=st_recv.at[k],
            device_id=peer, device_id_type=MESH) for k, peer in enumerate(peers)]
        for cp in cps:
            cp.start()
        for cp in cps:
            cp.wait()
        mine.wait()

    return _sequencer_call(
        body, "exchange_stats", collective_id,
        jax.ShapeDtypeStruct((NDEV,) + stats.shape, stats.dtype),
        [pltpu.SemaphoreType.DMA((7,)), pltpu.SemaphoreType.DMA((7,)), pltpu.SemaphoreType.DMA])(stats)


class _Reduction:
    def __init__(self, place, first_collective_id, state):
        self.place = place
        self.ids = iter(range(first_collective_id, 32))
        self.state = state
        self.groups = {}
        self.updates = {}

    def next_id(self):
        return next(self.ids)

    def start(self, group, grads):
        got = _exchange_sibling(grads, "sibling_exchange_" + group[0], self.next_id())
        self.groups[group[0]] = dict(names=group, grads=grads, got=got)

    def local(self, name, first=()):
        grp = self.groups[name]
        grads, got = lax.optimization_barrier((tuple(grp["grads"]), tuple(grp["got"]), tuple(first)))[:2]
        grp["sums"] = [_chip_sum(self.place, g, s, "chip_sum_" + n) for g, s, n in zip(grads, got, grp["names"])]
        grp["chips"] = _exchange_chips(grp["sums"], "chip_exchange_" + name, self.next_id())
        return grp["sums"]

    def landed(self, name):
        return list(self.groups[name]["chips"])

    def update(self, name):
        if name not in self.updates:
            grp = next(g for g in self.groups.values() if name in g["names"])
            k = grp["names"].index(name)
            self.updates[name] = _shard_update(self.place, *self.state[name], grp["grads"][k], grp["got"][k],
                                               grp["chips"][k], "update_" + name)
        return list(self.updates[name])


def _adamw(w, g, m, v):
    m = ADAM_B1 * m + (1.0 - ADAM_B1) * g
    v = ADAM_B2 * v + (1.0 - ADAM_B2) * (g * g)
    m_hat = m / (1.0 - ADAM_B1 ** ADAM_STEP)
    v_hat = v / (1.0 - ADAM_B2 ** ADAM_STEP)
    delta = -ADAM_LR * (m_hat / (jnp.sqrt(v_hat) + ADAM_EPS) + ADAM_WD * w)
    return delta, m, v


def _shard_update(place, w, m, v, g, got_sib, got_chips, name):
    r, c = w.shape
    tm = _row_tile(r, c)

    def body(pos_ref, w_ref, m_ref, v_ref, g_ref, s_ref, c_ref, go_ref, d_ref, mo_ref, vo_ref):
        grad = g_ref[...].astype(F32) + s_ref[...].astype(F32)
        for j in range(3):
            grad = grad + c_ref[j].astype(F32)
        delta, mn, vn = _adamw(w_ref[...], grad, m_ref[...], v_ref[...])
        go_ref[...] = grad
        d_ref[...] = delta
        mo_ref[...] = mn
        vo_ref[...] = vn

    row = pl.BlockSpec((tm, c), lambda i, pos: (i, 0))
    return pl.pallas_call(
        body, name=name,
        grid_spec=pltpu.PrefetchScalarGridSpec(
            num_scalar_prefetch=1, grid=(r // tm,),
            in_specs=[row, row, row,
                      pl.BlockSpec((None, tm, c), lambda i, pos: (4 * pos[0] + 2 * pos[1] + pos[2], i, 0)),
                      pl.BlockSpec((None, tm, c), lambda i, pos: (2 * pos[0] + pos[1], i, 0)),
                      pl.BlockSpec((3, tm, c), lambda i, pos: (0, i, 0))],
            out_specs=[row, row, row, row]),
        out_shape=[jax.ShapeDtypeStruct((r, c), F32)] * 4,
        compiler_params=_cp(("parallel",)),
    )(place, w, m, v, g, got_sib, got_chips)


def _small_update(stats_all, ws, ms, vs):
    def body(st_ref, w_ref, m_ref, v_ref, go_ref, d_ref, mo_ref, vo_ref):
        grad = st_ref[0]
        for k in range(1, NDEV):
            grad = grad + st_ref[k]
        delta, mn, vn = _adamw(w_ref[...], grad, m_ref[...], v_ref[...])
        go_ref[...] = grad
        d_ref[...] = delta
        mo_ref[...] = mn
        vo_ref[...] = vn

    return pl.pallas_call(
        body, name="small_update",
        out_shape=[jax.ShapeDtypeStruct((8, D), F32)] * 4,
        compiler_params=_cp(),
    )(stats_all, ws, ms, vs)


def kernel(x, norm_mix_w, w_in, w_out, norm_ffn_w, w_gate, w_up, w_down, norm_final_w, loss_target, m_norm_mix_w, m_w_in, m_w_out, m_norm_ffn_w, m_w_gate, m_w_up, m_w_down, m_norm_final_w, v_norm_mix_w, v_w_in, v_w_out, v_norm_ffn_w, v_w_gate, v_w_up, v_w_down, v_norm_final_w):
    tr = {"w_gate", "w_up"}
    names = ["w_in", "w_out", "w_gate", "w_up", "w_down"]

    def view(a, n):
        return a[0].T if n in tr else a[0]

    big_w = [view(a, n) for a, n in zip([w_in, w_out, w_gate, w_up, w_down], names)]
    big_m = [view(a, n) for a, n in zip([m_w_in, m_w_out, m_w_gate, m_w_up, m_w_down], names)]
    big_v = [view(a, n) for a, n in zip([v_w_in, v_w_out, v_w_gate, v_w_up, v_w_down], names)]

    shards = [_cast_bf16(w, "cast_" + n) for w, n in zip(big_w, names)]
    (win,) = _all_gather(shards[0:1], "all_gather_w_in", 1)
    wout, wg, wu = _all_gather(shards[1:4], "all_gather_out_gate_up", 2)
    (wd,) = _all_gather(shards[4:5], "all_gather_w_down", 3)
    nw3 = norm_final_w.reshape(1, D)
    place = jnp.stack([lax.axis_index("x"), lax.axis_index("y"), lax.axis_index("c")]).astype(jnp.int32)
    red = _Reduction(place, 4, {n: (w, m, v) for n, w, m, v in zip(names, big_w, big_m, big_v)})
    stats, gx, *_ = _local_step(
        x[0], loss_target[0], norm_mix_w, norm_ffn_w, nw3, win, wout.reshape(D, D), wg, wu, wd, red)
    stats_all = _exchange_stats(stats, red.next_id())
    upd = [red.update(n) for n in names]

    def rows(a, b, c):
        return jnp.concatenate([a.reshape(1, D), b.reshape(1, D), c.reshape(1, D), jnp.zeros((5, D), F32)], axis=0)

    sg, sd, sm, sv = _small_update(stats_all, rows(norm_mix_w, norm_ffn_w, norm_final_w),
                                   rows(m_norm_mix_w, m_norm_ffn_w, m_norm_final_w),
                                   rows(v_norm_mix_w, v_norm_ffn_w, v_norm_final_w))
    loss = sg[3, 0]

    def outs(k, small):
        big = [(u[k].T if n in tr else u[k])[None] for u, n in zip(upd, names)]
        return [small[0:1], big[0], big[1], small[1:2], big[2], big[3], big[4], small[2]]

    return (loss, gx[None], *outs(0, sg), *outs(1, sd), *outs(2, sm), *outs(3, sv))
```

```python
import functools
import math

import numpy as np
import jax
import jax.numpy as jnp
from jax import lax
from jax.experimental import pallas as pl
from jax.experimental.pallas import tpu as pltpu
from jax.experimental.pallas import tpu_sc as plsc

F32 = jnp.float32
BF16 = jnp.bfloat16

S = 2048
D = 2048
NDEV = 8
N_IN = 7168 // NDEV
N_FF = 5632 // NDEV
N_OUT = 2048 // NDEV
AH, AHD = 8, 128
RH, RHD = 4, 256
CH = 128
NB = S // CH
EPS = 1e-6
PATTERNS = ((1, 16), (4, 4), (16, 1))
NEG = -1e30
VMEM_LIMIT = 56 * 1024 * 1024

ADAM_LR, ADAM_B1, ADAM_B2, ADAM_EPS, ADAM_WD, ADAM_STEP = 0.001, 0.9, 0.999, 1e-08, 0.01, 10
MESH = pl.DeviceIdType.MESH


def _cp(sem=None):
    return pltpu.CompilerParams(dimension_semantics=sem, vmem_limit_bytes=VMEM_LIMIT)


def _dot(a, b):
    return jnp.dot(a, b, preferred_element_type=F32)


def _dot_nt(a, b):
    return lax.dot_general(a, b, (((1,), (1,)), ((), ())), preferred_element_type=F32)


def _dot_tn(a, b):
    return lax.dot_general(a, b, (((0,), (0,)), ((), ())), preferred_element_type=F32)


def _sigmoid(x):
    return 1.0 / (1.0 + jnp.exp(-x))


def _cast_bf16(w, name):
    r, c = w.shape
    tm = r if r <= 1024 else 512

    def body(w_ref, o_ref):
        o_ref[...] = w_ref[...].astype(BF16)

    return pl.pallas_call(
        body, name=name, grid=(r // tm,),
        in_specs=[pl.BlockSpec((tm, c), lambda i: (i, 0))],
        out_specs=pl.BlockSpec((tm, c), lambda i: (i, 0)),
        out_shape=jax.ShapeDtypeStruct((r, c), BF16),
        compiler_params=_cp(("parallel",)),
    )(w)


def _rms_fwd(x, nw):
    tm = 256

    def body(x_ref, w_ref, h_ref, r_ref):
        xs = x_ref[...]
        r = lax.rsqrt(jnp.mean(xs * xs, axis=-1, keepdims=True) + EPS)
        h_ref[...] = ((xs * r) * w_ref[...]).astype(BF16)
        r_ref[...] = r

    return pl.pallas_call(
        body, name="rms_fwd", grid=(S // tm,),
        in_specs=[pl.BlockSpec((tm, D), lambda i: (i, 0)), pl.BlockSpec((1, D), lambda i: (0, 0))],
        out_specs=[pl.BlockSpec((tm, D), lambda i: (i, 0)), pl.BlockSpec((tm, 1), lambda i: (i, 0))],
        out_shape=[jax.ShapeDtypeStruct((S, D), BF16), jax.ShapeDtypeStruct((S, 1), F32)],
        compiler_params=_cp(("parallel",)),
    )(x, nw)


def _rms_bwd_tile(dh, xs, r, nw):
    dnw = jnp.sum(dh * (xs * r), axis=0, keepdims=True)
    gy = dh * nw
    dx = r * gy - xs * ((r * r * r) * jnp.mean(gy * xs, axis=-1, keepdims=True))
    return dx, dnw


def _proj(h1, win):
    tm = 512

    def body(a_ref, w_ref, o_ref):
        o_ref[...] = _dot(a_ref[...], w_ref[...])

    return pl.pallas_call(
        body, name="proj", grid=(NDEV, S // tm),
        in_specs=[pl.BlockSpec((tm, D), lambda p, m: (m, 0)),
                  pl.BlockSpec((None, D, N_IN), lambda p, m: (p, 0, 0))],
        out_specs=pl.BlockSpec((tm, N_IN), lambda p, m: (m, p)),
        out_shape=jax.ShapeDtypeStruct((S, NDEV * N_IN), F32),
        compiler_params=_cp(("parallel", "parallel")),
    )(h1, win)


def _out_proj_rms(x, ma, mr, wout, nw):
    tm = 256
    half = D // 2

    def body(x_ref, ma_ref, mr_ref, w_ref, nw_ref, x2_ref, h_ref, r_ref):
        acc = _dot(ma_ref[...], w_ref[0:half, :]) + _dot(mr_ref[...], w_ref[half:D, :])
        x2 = x_ref[...] + acc
        r = lax.rsqrt(jnp.mean(x2 * x2, axis=-1, keepdims=True) + EPS)
        x2_ref[...] = x2
        h_ref[...] = ((x2 * r) * nw_ref[...]).astype(BF16)
        r_ref[...] = r

    return pl.pallas_call(
        body, name="out_proj_rms", grid=(S // tm,),
        in_specs=[pl.BlockSpec((tm, D), lambda i: (i, 0)),
                  pl.BlockSpec((tm, half), lambda i: (i, 0)),
                  pl.BlockSpec((tm, half), lambda i: (i, 0)),
                  pl.BlockSpec((D, D), lambda i: (0, 0)),
                  pl.BlockSpec((1, D), lambda i: (0, 0))],
        out_specs=[pl.BlockSpec((tm, D), lambda i: (i, 0)), pl.BlockSpec((tm, D), lambda i: (i, 0)),
                   pl.BlockSpec((tm, 1), lambda i: (i, 0))],
        out_shape=[jax.ShapeDtypeStruct((S, D), F32), jax.ShapeDtypeStruct((S, D), BF16),
                   jax.ShapeDtypeStruct((S, 1), F32)],
        compiler_params=_cp(("parallel",)),
    )(x, ma, mr, wout, nw)


def _ffn_up(h2, wg, wu):
    tm = 512

    def body(h_ref, wg_ref, wu_ref, g_ref, u_ref, a_ref):
        h = h_ref[...]
        g = _dot_nt(h, wg_ref[...])
        u = _dot_nt(h, wu_ref[...])
        g_ref[...] = g
        u_ref[...] = u
        a_ref[...] = ((g * _sigmoid(g)) * u).astype(BF16)

    blk = pl.BlockSpec((None, tm, N_FF), lambda p, m: (p, m, 0))
    wblk = pl.BlockSpec((None, N_FF, D), lambda p, m: (p, 0, 0))
    return pl.pallas_call(
        body, name="ffn_up", grid=(NDEV, S // tm),
        in_specs=[pl.BlockSpec((tm, D), lambda p, m: (m, 0)), wblk, wblk],
        out_specs=[blk, blk, blk],
        out_shape=[jax.ShapeDtypeStruct((NDEV, S, N_FF), F32), jax.ShapeDtypeStruct((NDEV, S, N_FF), F32),
                   jax.ShapeDtypeStruct((NDEV, S, N_FF), BF16)],
        compiler_params=_cp(("parallel", "parallel")),
    )(h2, wg, wu)


def _ffn_down_loss(x2, a, wd, nw, tgt):
    tm = 512

    def body(x2_ref, a_ref, w_ref, nw_ref, t_ref, dx_ref, dxb_ref, st_ref, acc_ref):
        m, p = pl.program_id(0), pl.program_id(1)

        @pl.when(p == 0)
        def _():
            acc_ref[...] = jnp.zeros_like(acc_ref)

        @pl.when((p == 0) & (m == 0))
        def _():
            st_ref[...] = jnp.zeros_like(st_ref)

        acc_ref[...] += _dot(a_ref[...], w_ref[...])

        @pl.when(p == NDEV - 1)
        def _():
            x3 = x2_ref[...] + acc_ref[...]
            nwv = nw_ref[...]
            r = lax.rsqrt(jnp.mean(x3 * x3, axis=-1, keepdims=True) + EPS)
            y = (x3 * r) * nwv
            err = y - t_ref[...]
            loss = 0.5 * jnp.sum(jnp.mean(err * err, axis=-1, keepdims=True), axis=0, keepdims=True)
            dy = err * (1.0 / D)
            dx, dnw = _rms_bwd_tile(dy, x3, r, nwv)
            dx_ref[...] = dx
            dxb_ref[...] = dx.astype(BF16)
            st_ref[0:1, :] += dnw
            st_ref[1:2, :] += jnp.broadcast_to(loss, (1, D))

    return pl.pallas_call(
        body, name="ffn_down_loss", grid=(S // tm, NDEV),
        in_specs=[pl.BlockSpec((tm, D), lambda m, p: (m, 0)),
                  pl.BlockSpec((None, tm, N_FF), lambda m, p: (p, m, 0)),
                  pl.BlockSpec((None, N_FF, D), lambda m, p: (p, 0, 0)),
                  pl.BlockSpec((1, D), lambda m, p: (0, 0)),
                  pl.BlockSpec((tm, D), lambda m, p: (m, 0))],
        out_specs=[pl.BlockSpec((tm, D), lambda m, p: (m, 0)), pl.BlockSpec((tm, D), lambda m, p: (m, 0)),
                   pl.BlockSpec((8, D), lambda m, p: (0, 0))],
        out_shape=[jax.ShapeDtypeStruct((S, D), F32), jax.ShapeDtypeStruct((S, D), BF16),
                   jax.ShapeDtypeStruct((8, D), F32)],
        scratch_shapes=[pltpu.VMEM((tm, D), F32)],
        compiler_params=_cp(("arbitrary", "arbitrary")),
    )(x2, a, wd, nw, tgt)


def _ffn_down_bwd(dx3b, wd, g, u):
    tm = 512

    def body(dx_ref, w_ref, g_ref, u_ref, dg_ref, du_ref):
        da = _dot_nt(dx_ref[...], w_ref[...])
        gv = g_ref[...]
        sg = _sigmoid(gv)
        silu = gv * sg
        dg_ref[...] = ((da * u_ref[...]) * (sg * (1.0 + gv * (1.0 - sg)))).astype(BF16)
        du_ref[...] = (da * silu).astype(BF16)

    blk = pl.BlockSpec((None, tm, N_FF), lambda p, m: (p, m, 0))
    return pl.pallas_call(
        body, name="ffn_down_bwd", grid=(NDEV, S // tm),
        in_specs=[pl.BlockSpec((tm, D), lambda p, m: (m, 0)),
                  pl.BlockSpec((None, N_FF, D), lambda p, m: (p, 0, 0)), blk, blk],
        out_specs=[blk, blk],
        out_shape=[jax.ShapeDtypeStruct((NDEV, S, N_FF), BF16), jax.ShapeDtypeStruct((NDEV, S, N_FF), BF16)],
        compiler_params=_cp(("parallel", "parallel")),
    )(dx3b, wd, g, u)


def _ffn_up_bwd(dg, du, wg, wu, dres, xs, r, nw):
    tm = 512

    def body(dg_ref, du_ref, wg_ref, wu_ref, dres_ref, x_ref, r_ref, nw_ref, dx_ref, dxb_ref, st_ref, acc_ref):
        m, p = pl.program_id(0), pl.program_id(1)

        @pl.when(p == 0)
        def _():
            acc_ref[...] = jnp.zeros_like(acc_ref)

        @pl.when((p == 0) & (m == 0))
        def _():
            st_ref[...] = jnp.zeros_like(st_ref)

        acc_ref[...] += _dot(dg_ref[...], wg_ref[...]) + _dot(du_ref[...], wu_ref[...])

        @pl.when(p == NDEV - 1)
        def _():
            dx, dnw = _rms_bwd_tile(acc_ref[...], x_ref[...], r_ref[...], nw_ref[...])
            dx = dres_ref[...] + dx
            dx_ref[...] = dx
            dxb_ref[...] = dx.astype(BF16)
            st_ref[0:1, :] += dnw

    blk = pl.BlockSpec((None, tm, N_FF), lambda m, p: (p, m, 0))
    wblk = pl.BlockSpec((None, N_FF, D), lambda m, p: (p, 0, 0))
    row = pl.BlockSpec((tm, D), lambda m, p: (m, 0))
    return pl.pallas_call(
        body, name="ffn_up_bwd", grid=(S // tm, NDEV),
        in_specs=[blk, blk, wblk, wblk, row, row, pl.BlockSpec((tm, 1), lambda m, p: (m, 0)),
                  pl.BlockSpec((1, D), lambda m, p: (0, 0))],
        out_specs=[row, row, pl.BlockSpec((8, D), lambda m, p: (0, 0))],
        out_shape=[jax.ShapeDtypeStruct((S, D), F32), jax.ShapeDtypeStruct((S, D), BF16),
                   jax.ShapeDtypeStruct((8, D), F32)],
        scratch_shapes=[pltpu.VMEM((tm, D), F32)],
        compiler_params=_cp(("arbitrary", "arbitrary")),
    )(dg, du, wg, wu, dres, xs, r, nw)


def _out_proj_bwd(dx2b, wout):
    tm = 256

    def body(dx_ref, w_ref, o_ref):
        o_ref[...] = _dot_nt(dx_ref[...], w_ref[...])

    return pl.pallas_call(
        body, name="out_proj_bwd", grid=(S // tm,),
        in_specs=[pl.BlockSpec((tm, D), lambda i: (i, 0)), pl.BlockSpec((D, D), lambda i: (0, 0))],
        out_specs=pl.BlockSpec((tm, D), lambda i: (i, 0)),
        out_shape=jax.ShapeDtypeStruct((S, D), F32),
        compiler_params=_cp(("parallel",)),
    )(dx2b, wout)


def _in_proj_bwd(dproj, win, dres, xs, r, nw):
    tm = 512

    def body(dp_ref, w_ref, dres_ref, x_ref, r_ref, nw_ref, dx_ref, st_ref, acc_ref):
        m, p = pl.program_id(0), pl.program_id(1)

        @pl.when(p == 0)
        def _():
            acc_ref[...] = jnp.zeros_like(acc_ref)

        @pl.when((p == 0) & (m == 0))
        def _():
            st_ref[...] = jnp.zeros_like(st_ref)

        acc_ref[...] += _dot_nt(dp_ref[...], w_ref[...])

        @pl.when(p == NDEV - 1)
        def _():
            dx, dnw = _rms_bwd_tile(acc_ref[...], x_ref[...], r_ref[...], nw_ref[...])
            dx_ref[...] = dres_ref[...] + dx
            st_ref[0:1, :] += dnw

    row = pl.BlockSpec((tm, D), lambda m, p: (m, 0))
    return pl.pallas_call(
        body, name="in_proj_bwd", grid=(S // tm, NDEV),
        in_specs=[pl.BlockSpec((tm, N_IN), lambda m, p: (m, p)),
                  pl.BlockSpec((None, D, N_IN), lambda m, p: (p, 0, 0)),
                  row, row, pl.BlockSpec((tm, 1), lambda m, p: (m, 0)),
                  pl.BlockSpec((1, D), lambda m, p: (0, 0))],
        out_specs=[row, pl.BlockSpec((8, D), lambda m, p: (0, 0))],
        out_shape=[jax.ShapeDtypeStruct((S, D), F32), jax.ShapeDtypeStruct((8, D), F32)],
        scratch_shapes=[pltpu.VMEM((tm, D), F32)],
        compiler_params=_cp(("arbitrary", "arbitrary")),
    )(dproj, win, dres, xs, r, nw)


def _wgrad_in(h1, dproj):
    def body(a_ref, d_ref, o_ref):
        o_ref[...] = _dot_tn(a_ref[...], d_ref[...]).astype(BF16)

    return pl.pallas_call(
        body, name="wgrad_in", grid=(NDEV,),
        in_specs=[pl.BlockSpec((S, D), lambda p: (0, 0)), pl.BlockSpec((S, N_IN), lambda p: (0, p))],
        out_specs=pl.BlockSpec((None, D, N_IN), lambda p: (p, 0, 0)),
        out_shape=jax.ShapeDtypeStruct((NDEV, D, N_IN), BF16),
        compiler_params=_cp(("parallel",)),
    )(h1, dproj)


def _wgrad_rows(a3, dy, name):
    def body(a_ref, d_ref, o_ref):
        o_ref[...] = _dot_tn(a_ref[...], d_ref[...]).astype(BF16)

    return pl.pallas_call(
        body, name=name, grid=(NDEV,),
        in_specs=[pl.BlockSpec((None, S, N_FF), lambda p: (p, 0, 0)), pl.BlockSpec((S, D), lambda p: (0, 0))],
        out_specs=pl.BlockSpec((None, N_FF, D), lambda p: (p, 0, 0)),
        out_shape=jax.ShapeDtypeStruct((NDEV, N_FF, D), BF16),
        compiler_params=_cp(("parallel",)),
    )(a3, dy)


def _wgrad_out(ma, mr, dx2b):
    half = D // 2
    per = half // N_OUT

    def body(ma_ref, mr_ref, d_ref, o_ref):
        p = pl.program_id(0)

        @pl.when(p < per)
        def _():
            o_ref[...] = _dot_tn(ma_ref[...], d_ref[...]).astype(BF16)

        @pl.when(p >= per)
        def _():
            o_ref[...] = _dot_tn(mr_ref[...], d_ref[...]).astype(BF16)

    return pl.pallas_call(
        body, name="wgrad_out", grid=(NDEV,),
        in_specs=[pl.BlockSpec((S, N_OUT), lambda p: (0, jnp.minimum(p, per - 1))),
                  pl.BlockSpec((S, N_OUT), lambda p: (0, jnp.maximum(p - per, 0))),
                  pl.BlockSpec((S, D), lambda p: (0, 0))],
        out_specs=pl.BlockSpec((None, N_OUT, D), lambda p: (p, 0, 0)),
        out_shape=jax.ShapeDtypeStruct((NDEV, N_OUT, D), BF16),
        compiler_params=_cp(("parallel",)),
    )(ma, mr, dx2b)


def _attn_consts():
    c = np.zeros((AH, 8, AHD), np.float32)
    for h in range(AH):
        c[h, :, :] = 2.0 ** (-(h + 1))
    return jnp.asarray(c)


def _permute_in(dst, src, d, cast=None):
    ln = S // d
    for rr in range(d):
        v = src[pl.ds(rr, ln, stride=d), :] if d > 1 else src[...]
        dst[rr * ln:(rr + 1) * ln, :] = v if cast is None else v.astype(cast)


def _attn_masks():
    qi = lax.broadcasted_iota(jnp.int32, (CH, CH), 0)
    kj = lax.broadcasted_iota(jnp.int32, (CH, CH), 1)
    dist_c = (qi - kj).astype(F32)
    dist_p = (qi - kj + CH).astype(F32)
    return (qi >= kj)[None], (kj >= qi)[None], dist_c[None], dist_p[None]


GB = 8


def _bdot_nt(a, b):
    return lax.dot_general(a, b, (((2,), (2,)), ((0,), (0,))), preferred_element_type=F32)


def _bdot(a, b):
    return lax.dot_general(a, b, (((2,), (1,)), ((0,), (0,))), preferred_element_type=F32)


def _bdot_tn(a, b):
    return lax.dot_general(a, b, (((1,), (1,)), ((0,), (0,))), preferred_element_type=F32)


def _shift_block(dst, src):
    dst[0:CH, :] = jnp.zeros((CH, AHD), dst.dtype)
    dst[CH:S, :] = src[0:S - CH, :]


def _has_prev(g, nb):
    blk = lax.broadcasted_iota(jnp.int32, (GB, 1, 1), 0) + g * GB
    return (blk & (nb - 1)) != 0


def _blocks(ref, g):
    return ref[g * GB * CH:(g + 1) * GB * CH, :].reshape(GB, CH, AHD)


def _attn_fwd(proj):
    scale = 1.0 / math.sqrt(AHD)

    def body(c_ref, q_ref, k_ref, v_ref, o_ref, ob_ref, lse_ref, qd, kd, vd, kps, vps, od, ld, *nat):
        onat, lnat = nat[0:3], nat[3:6]
        slope = c_ref[0:1, :]
        mask_c, mask_p, dist_c, dist_p = _attn_masks()
        for pi, (d, nb) in enumerate(PATTERNS):
            _permute_in(qd, q_ref, d, BF16)
            _permute_in(kd, k_ref, d, BF16)
            _permute_in(vd, v_ref, d, BF16)
            if nb > 1:
                _shift_block(kps, kd)
                _shift_block(vps, vd)
            bias_c = -(slope * float(d)) * dist_c
            bias_p = -(slope * float(d)) * dist_p
            for g in range(NB // GB):
                q3, k3, v3 = _blocks(qd, g), _blocks(kd, g), _blocks(vd, g)
                s_c = jnp.where(mask_c, _bdot_nt(q3, k3) * scale + bias_c, NEG)
                mx = jnp.max(s_c, axis=-1, keepdims=True)
                if nb > 1:
                    kp3, vp3 = _blocks(kps, g), _blocks(vps, g)
                    s_p = jnp.where(jnp.logical_and(mask_p, _has_prev(g, nb)),
                                    _bdot_nt(q3, kp3) * scale + bias_p, NEG)
                    mx = jnp.maximum(mx, jnp.max(s_p, axis=-1, keepdims=True))
                    l = (jnp.sum(jnp.exp(s_c - mx), axis=-1, keepdims=True)
                         + jnp.sum(jnp.exp(s_p - mx), axis=-1, keepdims=True))
                    lse = mx + jnp.log(l)
                    o3 = _bdot(jnp.exp(s_c - lse).astype(BF16), v3) + _bdot(jnp.exp(s_p - lse).astype(BF16), vp3)
                else:
                    l = jnp.sum(jnp.exp(s_c - mx), axis=-1, keepdims=True)
                    lse = mx + jnp.log(l)
                    o3 = _bdot(jnp.exp(s_c - lse).astype(BF16), v3)
                rows = slice(g * GB * CH, (g + 1) * GB * CH)
                od[rows, :] = o3.reshape(GB * CH, AHD)
                ld[rows, :] = jnp.broadcast_to(lse, (GB, CH, AHD)).reshape(GB * CH, AHD)
            ln = S // d
            for rr in range(d):
                if d > 1:
                    onat[pi][pl.ds(rr, ln, stride=d), :] = od[rr * ln:(rr + 1) * ln, :]
                    lnat[pi][pl.ds(rr, ln, stride=d), :] = ld[rr * ln:(rr + 1) * ln, :]
                else:
                    onat[pi][...] = od[...]
                    lnat[pi][...] = ld[...]
        l0, l1, l2 = lnat[0][...], lnat[1][...], lnat[2][...]
        mx = jnp.maximum(jnp.maximum(l0, l1), l2)
        e0, e1, e2 = jnp.exp(l0 - mx), jnp.exp(l1 - mx), jnp.exp(l2 - mx)
        den = e0 + e1 + e2
        out = (e0 / den) * onat[0][...] + (e1 / den) * onat[1][...] + (e2 / den) * onat[2][...]
        o_ref[...] = out
        ob_ref[...] = out.astype(BF16)
        lse_ref[...] = mx + jnp.log(den)

    def col(off):
        return pl.BlockSpec((S, AHD), lambda h: (0, off + h))

    return pl.pallas_call(
        body, name="attn_fwd", grid=(AH,),
        in_specs=[pl.BlockSpec((None, 8, AHD), lambda h: (h, 0, 0)), col(0), col(AH), col(2 * AH)],
        out_specs=[col(0), col(0), col(0)],
        out_shape=[jax.ShapeDtypeStruct((S, AH * AHD), F32), jax.ShapeDtypeStruct((S, AH * AHD), BF16),
                   jax.ShapeDtypeStruct((S, AH * AHD), F32)],
        scratch_shapes=[pltpu.VMEM((S, AHD), BF16) for _ in range(5)]
        + [pltpu.VMEM((S, AHD), F32) for _ in range(8)],
        compiler_params=_cp(("parallel",)),
    )(_attn_consts(), proj, proj, proj)


def _attn_bwd(proj, dmixed, o, lse):
    scale = 1.0 / math.sqrt(AHD)

    def body(c_ref, q_ref, k_ref, v_ref, do_ref, o_ref, lse_ref, dq_ref, dk_ref, dv_ref,
             qd, kd, vd, dod, kps, vps, lsd, dld, dqd, dkd, dvd, delta, aq, ak, av):
        slope = c_ref[0:1, :]
        mask_c, mask_p, dist_c, dist_p = _attn_masks()
        delta[...] = jnp.broadcast_to(jnp.sum(do_ref[...] * o_ref[...], axis=-1, keepdims=True), (S, AHD))
        for pi, (d, nb) in enumerate(PATTERNS):
            _permute_in(qd, q_ref, d, BF16)
            _permute_in(kd, k_ref, d, BF16)
            _permute_in(vd, v_ref, d, BF16)
            _permute_in(dod, do_ref, d, BF16)
            _permute_in(lsd, lse_ref, d)
            _permute_in(dld, delta, d)
            if nb > 1:
                _shift_block(kps, kd)
                _shift_block(vps, vd)
            bias_c = -(slope * float(d)) * dist_c
            bias_p = -(slope * float(d)) * dist_p
            for g in range(NB // GB):
                q3, k3, v3, do3 = _blocks(qd, g), _blocks(kd, g), _blocks(vd, g), _blocks(dod, g)
                ls, dl = _blocks(lsd, g), _blocks(dld, g)
                lo, hi = g * GB * CH, (g + 1) * GB * CH
                p_c = jnp.exp(jnp.where(mask_c, _bdot_nt(q3, k3) * scale + bias_c, NEG) - ls)
                ds_c = ((p_c * (_bdot_nt(do3, v3) - dl)) * scale).astype(BF16)
                dq3 = _bdot(ds_c, k3)
                dkd[lo:hi, :] = _bdot_tn(ds_c, q3).reshape(GB * CH, AHD)
                dvd[lo:hi, :] = _bdot_tn(p_c.astype(BF16), do3).reshape(GB * CH, AHD)
                if nb > 1:
                    kp3, vp3 = _blocks(kps, g), _blocks(vps, g)
                    p_p = jnp.exp(jnp.where(jnp.logical_and(mask_p, _has_prev(g, nb)),
                                            _bdot_nt(q3, kp3) * scale + bias_p, NEG) - ls)
                    ds_p = ((p_p * (_bdot_nt(do3, vp3) - dl)) * scale).astype(BF16)
                    dq3 = dq3 + _bdot(ds_p, kp3)
                    dkp = _bdot_tn(ds_p, q3).reshape(GB * CH, AHD)
                    dvp = _bdot_tn(p_p.astype(BF16), do3).reshape(GB * CH, AHD)
                    if g == 0:
                        dkd[0:hi - CH, :] += dkp[CH:, :]
                        dvd[0:hi - CH, :] += dvp[CH:, :]
                    else:
                        dkd[lo - CH:hi - CH, :] += dkp
                        dvd[lo - CH:hi - CH, :] += dvp
                dqd[lo:hi, :] = dq3.reshape(GB * CH, AHD)
            ln = S // d
            for acc, src in ((aq, dqd), (ak, dkd), (av, dvd)):
                if pi == 0:
                    acc[...] = src[...]
                else:
                    for rr in range(d):
                        acc[pl.ds(rr, ln, stride=d), :] += src[rr * ln:(rr + 1) * ln, :]
        dq_ref[...] = aq[...].astype(BF16)
        dk_ref[...] = ak[...].astype(BF16)
        dv_ref[...] = av[...].astype(BF16)

    def col(off):
        return pl.BlockSpec((S, AHD), lambda h: (0, off + h))

    return pl.pallas_call(
        body, name="attn_bwd", grid=(AH,),
        in_specs=[pl.BlockSpec((None, 8, AHD), lambda h: (h, 0, 0)), col(0), col(AH), col(2 * AH),
                  col(0), col(0), col(0)],
        out_specs=[col(0), col(0), col(0)],
        out_shape=[jax.ShapeDtypeStruct((S, AH * AHD), BF16)] * 3,
        scratch_shapes=[pltpu.VMEM((S, AHD), BF16) for _ in range(6)]
        + [pltpu.VMEM((S, AHD), F32) for _ in range(9)],
        compiler_params=_cp(("parallel",)),
    )(_attn_consts(), proj, proj, proj, dmixed, o, lse)


def _ret_consts():
    c = np.zeros((RH, 8, RHD), np.float32)
    for h in range(RH):
        c[h, :, :] = np.log(np.float32(1.0) - np.float32(2.0 ** (-5.0 - h)))
    return jnp.asarray(c)


def _ret_factors(lg):
    i = lax.broadcasted_iota(jnp.int32, (CH, CH), 0)
    j = lax.broadcasted_iota(jnp.int32, (CH, CH), 1)
    dif = (i - j).astype(F32)
    decay = jnp.where(dif >= 0, jnp.exp(lg[:, 0:CH] * jnp.maximum(dif, 0.0)), 0.0)
    row = lax.broadcasted_iota(jnp.int32, (CH, RHD), 0).astype(F32)
    zeta = jnp.exp(lg * (CH - 1.0 - row))
    xi = jnp.exp(lg * (row + 1.0))
    return decay, zeta, xi, jnp.exp(lg * float(CH))


def _ret_specs(rev):
    off = 3 * AH * AHD // RHD

    def ch(n):
        return (NB - 1 - n) if rev else n

    def col(k):
        return pl.BlockSpec((CH, RHD), lambda h, n: (ch(n), off + k * RH + h))

    own = pl.BlockSpec((CH, RHD), lambda h, n: (ch(n), h))
    state = pl.BlockSpec((None, None, RHD, RHD), lambda h, n: (h, ch(n), 0, 0))
    const = pl.BlockSpec((None, 8, RHD), lambda h, n: (h, 0, 0))
    return col, own, state, const


def _ret_fwd(proj):
    def body(c_ref, q_ref, k_ref, v_ref, g_ref, ret_ref, mr_ref, st_ref, r_acc):
        n = pl.program_id(1)

        @pl.when(n == 0)
        def _():
            r_acc[...] = jnp.zeros_like(r_acc)

        decay, zeta, xi, gch = _ret_factors(c_ref[0:1, :])
        qb = q_ref[...].astype(BF16)
        kc = k_ref[...] * (1.0 / math.sqrt(RHD))
        kb = kc.astype(BF16)
        vb = v_ref[...].astype(BF16)
        rb = r_acc[...].astype(BF16)
        st_ref[...] = rb
        scores = _dot_nt(qb, kb) * decay
        ret = _dot(scores.astype(BF16), vb) + _dot(qb, rb) * xi
        r_acc[...] = r_acc[...] * gch + _dot_tn((kc * zeta).astype(BF16), vb)
        ret_ref[...] = ret
        rr = lax.rsqrt(jnp.mean(ret * ret, axis=-1, keepdims=True) + EPS)
        gv = g_ref[...]
        mr_ref[...] = ((gv * _sigmoid(gv)) * (ret * rr)).astype(BF16)

    col, own, state, const = _ret_specs(False)
    return pl.pallas_call(
        body, name="ret_fwd", grid=(RH, NB),
        in_specs=[const, col(0), col(1), col(2), col(3)],
        out_specs=[own, own, state],
        out_shape=[jax.ShapeDtypeStruct((S, RH * RHD), F32), jax.ShapeDtypeStruct((S, RH * RHD), BF16),
                   jax.ShapeDtypeStruct((RH, NB, RHD, RHD), BF16)],
        scratch_shapes=[pltpu.VMEM((RHD, RHD), F32)],
        compiler_params=_cp(("parallel", "arbitrary")),
    )(_ret_consts(), proj, proj, proj, proj)


def _ret_bwd(proj, ret, states, dmixed):
    def body(c_ref, q_ref, k_ref, v_ref, g_ref, ret_ref, st_ref, dm_ref, dq_ref, dk_ref, dv_ref, dg_ref, g_acc):
        n = pl.program_id(1)

        @pl.when(n == 0)
        def _():
            g_acc[...] = jnp.zeros_like(g_acc)

        decay, zeta, xi, gch = _ret_factors(c_ref[0:1, :])
        ret_v = ret_ref[...]
        rr = lax.rsqrt(jnp.mean(ret_v * ret_v, axis=-1, keepdims=True) + EPS)
        gv = g_ref[...]
        sg = _sigmoid(gv)
        dmix = dm_ref[...]
        dg_ref[...] = ((dmix * (ret_v * rr)) * (sg * (1.0 + gv * (1.0 - sg)))).astype(BF16)
        dretn = dmix * (gv * sg)
        dret = rr * dretn - ret_v * ((rr * rr * rr) * jnp.mean(dretn * ret_v, axis=-1, keepdims=True))

        qb = q_ref[...].astype(BF16)
        kc = k_ref[...] * (1.0 / math.sqrt(RHD))
        kb = kc.astype(BF16)
        vb = v_ref[...].astype(BF16)
        rb = st_ref[...]
        db = dret.astype(BF16)
        sc = (_dot_nt(qb, kb) * decay).astype(BF16)
        da = (_dot_nt(db, vb) * decay).astype(BF16)
        dxi = (dret * xi).astype(BF16)
        gb = g_acc[...].astype(BF16)
        kz = (kc * zeta).astype(BF16)
        dq = _dot(da, kb) + _dot_nt(dxi, rb)
        dkc = _dot_tn(da, qb) + _dot_nt(vb, gb) * zeta
        dv = _dot_tn(sc, db) + _dot(kz, gb)
        g_acc[...] = _dot_tn(qb, dxi) + gch * g_acc[...]
        dq_ref[...] = dq.astype(BF16)
        dk_ref[...] = (dkc * (1.0 / math.sqrt(RHD))).astype(BF16)
        dv_ref[...] = dv.astype(BF16)

    col, own, state, const = _ret_specs(True)
    dm = pl.BlockSpec((CH, RHD), lambda h, n: (NB - 1 - n, AH * AHD // RHD + h))
    return pl.pallas_call(
        body, name="ret_bwd", grid=(RH, NB),
        in_specs=[const, col(0), col(1), col(2), col(3), own, state, dm],
        out_specs=[own, own, own, own],
        out_shape=[jax.ShapeDtypeStruct((S, RH * RHD), BF16)] * 4,
        scratch_shapes=[pltpu.VMEM((RHD, RHD), F32)],
        compiler_params=_cp(("parallel", "arbitrary")),
    )(_ret_consts(), proj, proj, proj, proj, ret, states, dmixed)


class _NoReduction:
    def start(self, group, grads):
        pass

    def local(self, name, first=()):
        return []

    def landed(self, name):
        return []

    def update(self, name):
        return []


def _local_step(x, tgt, nw1, nw2, nw3, win, wout, wg, wu, wd, red=None):
    red = red or _NoReduction()

    def after(values, first):
        return lax.optimization_barrier((tuple(values), tuple(first)))[0]

    h1, r1 = _rms_fwd(x, nw1)
    proj = _proj(h1, win)
    o, ma, lse = _attn_fwd(proj)
    ret, mr, states = _ret_fwd(proj)
    x2, h2, r2 = _out_proj_rms(x, ma, mr, wout, nw2)
    g, u, a = _ffn_up(h2, wg, wu)
    dx3, dx3b, st3 = _ffn_down_loss(x2, a, wd, nw3, tgt)

    dwd = _wgrad_rows(a, dx3b, "wgrad_down")
    red.start(["w_down"], [dwd])
    (dx3b,) = after([dx3b], [dwd])
    dg, du = _ffn_down_bwd(dx3b, wd, g, u)
    dg, du = after([dg, du], red.local("w_down", first=[dg]))
    dwg = _wgrad_rows(dg, h2, "wgrad_gate")
    dwu = _wgrad_rows(du, h2, "wgrad_up")
    red.start(["w_gate", "w_up"], [dwg, dwu])
    dg, du = after([dg, du], [dwg, dwu])
    dx2, dx2b, st2 = _ffn_up_bwd(dg, du, wg, wu, dx3, x2, r2, nw2)
    (dx2b,) = after([dx2b], red.landed("w_down"))
    dwo = _wgrad_out(ma, mr, dx2b)
    red.start(["w_out"], [dwo])
    (dx2b,) = after([dx2b], [dwo] + red.local("w_gate"))
    dmixed = _out_proj_bwd(dx2b, wout)
    dqa, dka, dva = _attn_bwd(proj, dmixed, o, lse)
    (dmixed,) = after([dmixed], [dqa] + red.local("w_out"))
    dqr, dkr, dvr, dgr = _ret_bwd(proj, ret, states, dmixed)
    dproj = jnp.concatenate([dqa, dka, dva, dqr, dkr, dvr, dgr], axis=1)
    dwi = _wgrad_in(h1, dproj)
    red.start(["w_in"], after([dwi], red.landed("w_gate") + red.landed("w_out")))
    early = red.update("w_down") + red.update("w_gate")
    (dproj,) = after([dproj], red.local("w_in", first=early))
    gx, st1 = _in_proj_bwd(dproj, win, dx2, x, r1, nw1)
    stats = jnp.concatenate([st1[0:1], st2[0:1], st3[0:2], jnp.zeros((4, D), F32)], axis=0)
    return stats, gx, dwi, dwo, dwg, dwu, dwd


def _place():
    x, y, c = lax.axis_index("x"), lax.axis_index("y"), lax.axis_index("c")
    return x, y, c, [(1 - x, y), (x, 1 - y), (1 - x, 1 - y)]


def _handshake(peers):
    barrier = pltpu.get_barrier_semaphore()
    for peer in peers:
        pl.semaphore_signal(barrier, inc=1, device_id=peer, device_id_type=MESH)
    pl.semaphore_wait(barrier, len(peers))


def _all_gather(shards, name, collective_id):
    na = len(shards)
    SIB, XN0, XN1, YN1, YN0, VIA_X, VIA_Y = 0, 1, 2, 3, 4, 5, 6
    D2D = {XN0: 7, XN1: 8, YN1: 9, YN0: 10, VIA_X: 11, VIA_Y: 12}

    def body(*refs):
        ins, outs = refs[:na], refs[na:2 * na]
        send_sems, recv_sems, local_sems = refs[2 * na:]
        x, y, c, _ = _place()
        me, sib = (x, y, c), (x, y, 1 - c)
        xn, yn, dg = (1 - x, y, c), (x, 1 - y, c), (1 - x, 1 - y, c)
        _handshake([sib, xn, yn])

        def part(ref, h):
            rows = ref.shape[0] // 2
            return ref if h is None else ref.at[pl.ds(h * rows, rows)]

        def block(a, owner, h):
            return part(outs[a].at[4 * owner[0] + 2 * owner[1] + owner[2]], h)

        def copy(a, k, owner, h, to, own_src=False):
            return pltpu.make_async_remote_copy(
                src_ref=part(ins[a], h) if own_src else block(a, owner, h), dst_ref=block(a, owner, h),
                send_sem=send_sems.at[a, k], recv_sem=recv_sems.at[a, k], device_id=to, device_id_type=MESH)

        def other(p):
            return (p[0], p[1], 1 - c)

        mine = [pltpu.make_async_copy(ins[a], block(a, me, None), local_sems.at[a]) for a in range(na)]
        for cp in mine:
            cp.start()
        sent = []
        for a in range(na):
            sent += [copy(a, XN0, me, 0, xn, True), copy(a, YN1, me, 1, yn, True),
                     copy(a, XN1, me, 1, xn, True), copy(a, YN0, me, 0, yn, True)]
        sent += [copy(a, SIB, me, None, sib, True) for a in range(na)]
        for cp in sent:
            cp.start()

        def landed(a, k, owner, h, then):
            copy(a, k, owner, h, me).wait_recv()
            for k2, to in then + [(D2D[k], sib)]:
                cp = copy(a, k2, owner, h, to)
                cp.start()
                sent.append(cp)

        for a in range(na):
            landed(a, XN0, xn, 0, [(VIA_Y, yn)])
            landed(a, YN1, yn, 1, [(VIA_X, xn)])
            landed(a, XN1, xn, 1, [])
            landed(a, YN0, yn, 0, [])
        for a in range(na):
            landed(a, VIA_Y, dg, 0, [])
            landed(a, VIA_X, dg, 1, [])
        for a in range(na):
            copy(a, SIB, sib, None, me).wait_recv()
            for k, owner, h in ((XN0, xn, 0), (XN1, xn, 1), (YN1, yn, 1), (YN0, yn, 0), (VIA_Y, dg, 0), (VIA_X, dg, 1)):
                copy(a, D2D[k], other(owner), h, me).wait_recv()
        for cp in sent:
            cp.wait_send()
        for cp in mine:
            cp.wait()

    return _sequencer_call(
        body, name, collective_id,
        [jax.ShapeDtypeStruct((NDEV,) + s.shape, s.dtype) for s in shards],
        [pltpu.SemaphoreType.DMA((na, 13)), pltpu.SemaphoreType.DMA((na, 13)), pltpu.SemaphoreType.DMA((na,))])(*shards)


def _sequencer_call(body, name, collective_id, out_type, scratch_types):
    return pl.kernel(
        body, name=name, out_type=out_type,
        mesh=plsc.ScalarSubcoreMesh(axis_name="sequencer", num_cores=1),
        scratch_types=scratch_types,
        compiler_params=pltpu.CompilerParams(collective_id=collective_id))


def _exchange_sibling(grads, name, collective_id):
    na = len(grads)

    def body(*refs):
        ins, outs = refs[:na], refs[na:2 * na]
        send_sems, recv_sems = refs[2 * na:]
        x, y, c, _ = _place()
        _handshake([(x, y, 1 - c)])
        cps = []
        for a in range(na):
            for k in range(4):
                cps.append(pltpu.make_async_remote_copy(
                    src_ref=ins[a].at[2 * k + (1 - c)], dst_ref=outs[a].at[k],
                    send_sem=send_sems.at[a, k], recv_sem=recv_sems.at[a, k],
                    device_id=(x, y, 1 - c), device_id_type=MESH))
        for cp in cps:
            cp.start()
        for cp in cps:
            cp.wait()

    return _sequencer_call(
        body, name, collective_id,
        [jax.ShapeDtypeStruct((4,) + g.shape[1:], g.dtype) for g in grads],
        [pltpu.SemaphoreType.DMA((na, 4)), pltpu.SemaphoreType.DMA((na, 4))])(*grads)


def _row_tile(rows, cols):
    for t in (512, 256, 176, 128, 64, 32, 16):
        if rows % t == 0 and t * cols * 4 <= (1 << 20):
            return t
    raise ValueError((rows, cols))


def _chip_sum(place, g, got, name):
    _, r, c = g.shape
    tm = _row_tile(r, c)

    def body(pos_ref, g_ref, got_ref, o_ref):
        o_ref[...] = (g_ref[...].astype(F32) + got_ref[...].astype(F32)).astype(BF16)

    def chip(j, pos):
        return 2 * (pos[0] ^ jnp.where(j == 1, 0, 1)) + (pos[1] ^ jnp.where(j == 0, 0, 1))

    return pl.pallas_call(
        body, name=name,
        grid_spec=pltpu.PrefetchScalarGridSpec(
            num_scalar_prefetch=1, grid=(3, r // tm),
            in_specs=[pl.BlockSpec((None, tm, c), lambda j, i, pos: (2 * chip(j, pos) + pos[2], i, 0)),
                      pl.BlockSpec((None, tm, c), lambda j, i, pos: (chip(j, pos), i, 0))],
            out_specs=pl.BlockSpec((None, tm, c), lambda j, i, pos: (j, i, 0))),
        out_shape=jax.ShapeDtypeStruct((3, r, c), BF16),
        compiler_params=_cp(("parallel", "parallel")),
    )(place, g, got)


def _exchange_chips(sums, name, collective_id):
    na = len(sums)

    def body(*refs):
        ins, outs = refs[:na], refs[na:2 * na]
        send_sems, recv_sems = refs[2 * na:]
        x, y, c, chips = _place()
        _handshake([(*chip, c) for chip in chips])
        cps = []
        for a in range(na):
            for j, chip in enumerate(chips):
                cps.append(pltpu.make_async_remote_copy(
                    src_ref=ins[a].at[j], dst_ref=outs[a].at[j],
                    send_sem=send_sems.at[a, j], recv_sem=recv_sems.at[a, j],
                    device_id=(*chip, c), device_id_type=MESH))
        for cp in cps:
            cp.start()
        for cp in cps:
            cp.wait()

    return _sequencer_call(
        body, name, collective_id,
        [jax.ShapeDtypeStruct((3,) + s.shape[1:], s.dtype) for s in sums],
        [pltpu.SemaphoreType.DMA((na, 3)), pltpu.SemaphoreType.DMA((na, 3))])(*sums)


def _exchange_stats(stats, collective_id):
    def body(st_in, st_out, st_send, st_recv, local_sem):
        x, y, c, _ = _place()
        me_idx = 4 * x + 2 * y + c
        peers = [(x ^ ((k >> 2) & 1), y ^ ((k >> 1) & 1), c ^ (k & 1)) for k in range(1, 8)]
        _handshake(peers)
        mine = pltpu.make_async_copy(st_in, st_out.at[me_idx], local_sem)
        mine.start()
        cps = [pltpu.make_async_remote_copy(
            src_ref=st_in, dst_ref=st_out.at[me_idx], send_sem=st_send.at[k], recv_sem=st_recv.at[k],
            device_id=peer, device_id_type=MESH) for k, peer in enumerate(peers)]
        for cp in cps:
            cp.start()
        for cp in cps:
            cp.wait()
        mine.wait()

    return _sequencer_call(
        body, "exchange_stats", collective_id,
        jax.ShapeDtypeStruct((NDEV,) + stats.shape, stats.dtype),
        [pltpu.SemaphoreType.DMA((7,)), pltpu.SemaphoreType.DMA((7,)), pltpu.SemaphoreType.DMA])(stats)


class _Reduction:
    def __init__(self, place, first_collective_id, state):
        self.place = place
        self.ids = iter(range(first_collective_id, 32))
        self.state = state
        self.groups = {}
        self.updates = {}

    def next_id(self):
        return next(self.ids)

    def start(self, group, grads):
        got = _exchange_sibling(grads, "sibling_exchange_" + group[0], self.next_id())
        self.groups[group[0]] = dict(names=group, grads=grads, got=got)

    def local(self, name, first=()):
        grp = self.groups[name]
        grads, got = lax.optimization_barrier((tuple(grp["grads"]), tuple(grp["got"]), tuple(first)))[:2]
        grp["sums"] = [_chip_sum(self.place, g, s, "chip_sum_" + n) for g, s, n in zip(grads, got, grp["names"])]
        grp["chips"] = _exchange_chips(grp["sums"], "chip_exchange_" + name, self.next_id())
        return grp["sums"]

    def landed(self, name):
        return list(self.groups[name]["chips"])

    def update(self, name):
        if name not in self.updates:
            grp = next(g for g in self.groups.values() if name in g["names"])
            k = grp["names"].index(name)
            self.updates[name] = _shard_update(self.place, *self.state[name], grp["grads"][k], grp["got"][k],
                                               grp["chips"][k], "update_" + name)
        return list(self.updates[name])


def _adamw(w, g, m, v):
    m = ADAM_B1 * m + (1.0 - ADAM_B1) * g
    v = ADAM_B2 * v + (1.0 - ADAM_B2) * (g * g)
    m_hat = m / (1.0 - ADAM_B1 ** ADAM_STEP)
    v_hat = v / (1.0 - ADAM_B2 ** ADAM_STEP)
    delta = -ADAM_LR * (m_hat / (jnp.sqrt(v_hat) + ADAM_EPS) + ADAM_WD * w)
    return delta, m, v


def _shard_update(place, w, m, v, g, got_sib, got_chips, name):
    r, c = w.shape
    tm = _row_tile(r, c)

    def body(pos_ref, w_ref, m_ref, v_ref, g_ref, s_ref, c_ref, go_ref, d_ref, mo_ref, vo_ref):
        grad = g_ref[...].astype(F32) + s_ref[...].astype(F32)
        for j in range(3):
            grad = grad + c_ref[j].astype(F32)
        delta, mn, vn = _adamw(w_ref[...], grad, m_ref[...], v_ref[...])
        go_ref[...] = grad
        d_ref[...] = delta
        mo_ref[...] = mn
        vo_ref[...] = vn

    row = pl.BlockSpec((tm, c), lambda i, pos: (i, 0))
    return pl.pallas_call(
        body, name=name,
        grid_spec=pltpu.PrefetchScalarGridSpec(
            num_scalar_prefetch=1, grid=(r // tm,),
            in_specs=[row, row, row,
                      pl.BlockSpec((None, tm, c), lambda i, pos: (4 * pos[0] + 2 * pos[1] + pos[2], i, 0)),
                      pl.BlockSpec((None, tm, c), lambda i, pos: (2 * pos[0] + pos[1], i, 0)),
                      pl.BlockSpec((3, tm, c), lambda i, pos: (0, i, 0))],
            out_specs=[row, row, row, row]),
        out_shape=[jax.ShapeDtypeStruct((r, c), F32)] * 4,
        compiler_params=_cp(("parallel",)),
    )(place, w, m, v, g, got_sib, got_chips)


def _small_update(stats_all, ws, ms, vs):
    def body(st_ref, w_ref, m_ref, v_ref, go_ref, d_ref, mo_ref, vo_ref):
        grad = st_ref[0]
        for k in range(1, NDEV):
            grad = grad + st_ref[k]
        delta, mn, vn = _adamw(w_ref[...], grad, m_ref[...], v_ref[...])
        go_ref[...] = grad
        d_ref[...] = delta
        mo_ref[...] = mn
        vo_ref[...] = vn

    return pl.pallas_call(
        body, name="small_update",
        out_shape=[jax.ShapeDtypeStruct((8, D), F32)] * 4,
        compiler_params=_cp(),
    )(stats_all, ws, ms, vs)


def kernel(x, norm_mix_w, w_in, w_out, norm_ffn_w, w_gate, w_up, w_down, norm_final_w, loss_target, m_norm_mix_w, m_w_in, m_w_out, m_norm_ffn_w, m_w_gate, m_w_up, m_w_down, m_norm_final_w, v_norm_mix_w, v_w_in, v_w_out, v_norm_ffn_w, v_w_gate, v_w_up, v_w_down, v_norm_final_w):
    tr = {"w_gate", "w_up"}
    names = ["w_in", "w_out", "w_gate", "w_up", "w_down"]

    def view(a, n):
        return a[0].T if n in tr else a[0]

    big_w = [view(a, n) for a, n in zip([w_in, w_out, w_gate, w_up, w_down], names)]
    big_m = [view(a, n) for a, n in zip([m_w_in, m_w_out, m_w_gate, m_w_up, m_w_down], names)]
    big_v = [view(a, n) for a, n in zip([v_w_in, v_w_out, v_w_gate, v_w_up, v_w_down], names)]

    shards = [_cast_bf16(w, "cast_" + n) for w, n in zip(big_w, names)]
    (win,) = _all_gather(shards[0:1], "all_gather_w_in", 1)
    wout, wg, wu = _all_gather(shards[1:4], "all_gather_out_gate_up", 2)
    (wd,) = _all_gather(shards[4:5], "all_gather_w_down", 3)
    nw3 = norm_final_w.reshape(1, D)
    place = jnp.stack([lax.axis_index("x"), lax.axis_index("y"), lax.axis_index("c")]).astype(jnp.int32)
    red = _Reduction(place, 4, {n: (w, m, v) for n, w, m, v in zip(names, big_w, big_m, big_v)})
    stats, gx, *_ = _local_step(
        x[0], loss_target[0], norm_mix_w, norm_ffn_w, nw3, win, wout.reshape(D, D), wg, wu, wd, red)
    stats_all = _exchange_stats(stats, red.next_id())
    upd = [red.update(n) for n in names]
    stats_all = lax.optimization_barrier((stats_all, tuple(upd[0])))[0]

    def rows(a, b, c):
        return jnp.concatenate([a.reshape(1, D), b.reshape(1, D), c.reshape(1, D), jnp.zeros((5, D), F32)], axis=0)

    sg, sd, sm, sv = _small_update(stats_all, rows(norm_mix_w, norm_ffn_w, norm_final_w),
                                   rows(m_norm_mix_w, m_norm_ffn_w, m_norm_final_w),
                                   rows(v_norm_mix_w, v_norm_ffn_w, v_norm_final_w))
    loss = sg[3, 0]

    def outs(k, small):
        big = [(u[k].T if n in tr else u[k])[None] for u, n in zip(upd, names)]
        return [small[0:1], big[0], big[1], small[1:2], big[2], big[3], big[4], small[2]]

    return (loss, gx[None], *outs(0, sg), *outs(1, sd), *outs(2, sm), *outs(3, sv))
```

```python
import functools
import math

import numpy as np
import jax
import jax.numpy as jnp
from jax import lax
from jax.experimental import pallas as pl
from jax.experimental.pallas import tpu as pltpu
from jax.experimental.pallas import tpu_sc as plsc

F32 = jnp.float32
BF16 = jnp.bfloat16

S = 2048
D = 2048
NDEV = 8
N_IN = 7168 // NDEV
N_FF = 5632 // NDEV
N_OUT = 2048 // NDEV
AH, AHD = 8, 128
RH, RHD = 4, 256
CH = 128
NB = S // CH
EPS = 1e-6
PATTERNS = ((1, 16), (4, 4), (16, 1))
NEG = -1e30
VMEM_LIMIT = 56 * 1024 * 1024

ADAM_LR, ADAM_B1, ADAM_B2, ADAM_EPS, ADAM_WD, ADAM_STEP = 0.001, 0.9, 0.999, 1e-08, 0.01, 10
MESH = pl.DeviceIdType.MESH


def _cp(sem=None):
    return pltpu.CompilerParams(dimension_semantics=sem, vmem_limit_bytes=VMEM_LIMIT)


def _dot(a, b):
    return jnp.dot(a, b, preferred_element_type=F32)


def _dot_nt(a, b):
    return lax.dot_general(a, b, (((1,), (1,)), ((), ())), preferred_element_type=F32)


def _dot_tn(a, b):
    return lax.dot_general(a, b, (((0,), (0,)), ((), ())), preferred_element_type=F32)


def _sigmoid(x):
    return 0.5 * jnp.tanh(0.5 * x) + 0.5


def _cast_bf16(w, name):
    r, c = w.shape
    tm = r if r <= 1024 else 512

    def body(w_ref, o_ref):
        o_ref[...] = w_ref[...].astype(BF16)

    return pl.pallas_call(
        body, name=name, grid=(r // tm,),
        in_specs=[pl.BlockSpec((tm, c), lambda i: (i, 0))],
        out_specs=pl.BlockSpec((tm, c), lambda i: (i, 0)),
        out_shape=jax.ShapeDtypeStruct((r, c), BF16),
        compiler_params=_cp(("parallel",)),
    )(w)


def _rms_fwd(x, nw):
    tm = 256

    def body(x_ref, w_ref, h_ref, r_ref):
        xs = x_ref[...]
        r = lax.rsqrt(jnp.mean(xs * xs, axis=-1, keepdims=True) + EPS)
        h_ref[...] = ((xs * r) * w_ref[...]).astype(BF16)
        r_ref[...] = r

    return pl.pallas_call(
        body, name="rms_fwd", grid=(S // tm,),
        in_specs=[pl.BlockSpec((tm, D), lambda i: (i, 0)), pl.BlockSpec((1, D), lambda i: (0, 0))],
        out_specs=[pl.BlockSpec((tm, D), lambda i: (i, 0)), pl.BlockSpec((tm, 1), lambda i: (i, 0))],
        out_shape=[jax.ShapeDtypeStruct((S, D), BF16), jax.ShapeDtypeStruct((S, 1), F32)],
        compiler_params=_cp(("parallel",)),
    )(x, nw)


def _rms_bwd_tile(dh, xs, r, nw):
    dnw = jnp.sum(dh * (xs * r), axis=0, keepdims=True)
    gy = dh * nw
    dx = r * gy - xs * ((r * r * r) * jnp.mean(gy * xs, axis=-1, keepdims=True))
    return dx, dnw


def _proj(h1, win):
    tm = 1024

    def body(a_ref, w_ref, o_ref):
        o_ref[...] = _dot(a_ref[...], w_ref[...])

    return pl.pallas_call(
        body, name="proj", grid=(NDEV, S // tm),
        in_specs=[pl.BlockSpec((tm, D), lambda p, m: (m, 0)),
                  pl.BlockSpec((None, D, N_IN), lambda p, m: (p, 0, 0))],
        out_specs=pl.BlockSpec((tm, N_IN), lambda p, m: (m, p)),
        out_shape=jax.ShapeDtypeStruct((S, NDEV * N_IN), F32),
        compiler_params=_cp(("parallel", "parallel")),
    )(h1, win)


def _out_proj_rms(x, ma, mr, wout, nw):
    tm = 256
    half = D // 2

    def body(x_ref, ma_ref, mr_ref, w_ref, nw_ref, x2_ref, h_ref, r_ref):
        acc = _dot(ma_ref[...], w_ref[0:half, :]) + _dot(mr_ref[...], w_ref[half:D, :])
        x2 = x_ref[...] + acc
        r = lax.rsqrt(jnp.mean(x2 * x2, axis=-1, keepdims=True) + EPS)
        x2_ref[...] = x2
        h_ref[...] = ((x2 * r) * nw_ref[...]).astype(BF16)
        r_ref[...] = r

    return pl.pallas_call(
        body, name="out_proj_rms", grid=(S // tm,),
        in_specs=[pl.BlockSpec((tm, D), lambda i: (i, 0)),
                  pl.BlockSpec((tm, half), lambda i: (i, 0)),
                  pl.BlockSpec((tm, half), lambda i: (i, 0)),
                  pl.BlockSpec((D, D), lambda i: (0, 0)),
                  pl.BlockSpec((1, D), lambda i: (0, 0))],
        out_specs=[pl.BlockSpec((tm, D), lambda i: (i, 0)), pl.BlockSpec((tm, D), lambda i: (i, 0)),
                   pl.BlockSpec((tm, 1), lambda i: (i, 0))],
        out_shape=[jax.ShapeDtypeStruct((S, D), F32), jax.ShapeDtypeStruct((S, D), BF16),
                   jax.ShapeDtypeStruct((S, 1), F32)],
        compiler_params=_cp(("parallel",)),
    )(x, ma, mr, wout, nw)


def _ffn_up(h2, wg, wu):
    tm = 1024

    def body(h_ref, wg_ref, wu_ref, g_ref, u_ref, a_ref):
        h = h_ref[...]
        g = _dot_nt(h, wg_ref[...])
        u = _dot_nt(h, wu_ref[...])
        g_ref[...] = g
        u_ref[...] = u
        a_ref[...] = ((g * _sigmoid(g)) * u).astype(BF16)

    blk = pl.BlockSpec((None, tm, N_FF), lambda p, m: (p, m, 0))
    wblk = pl.BlockSpec((None, N_FF, D), lambda p, m: (p, 0, 0))
    return pl.pallas_call(
        body, name="ffn_up", grid=(NDEV, S // tm),
        in_specs=[pl.BlockSpec((tm, D), lambda p, m: (m, 0)), wblk, wblk],
        out_specs=[blk, blk, blk],
        out_shape=[jax.ShapeDtypeStruct((NDEV, S, N_FF), F32), jax.ShapeDtypeStruct((NDEV, S, N_FF), F32),
                   jax.ShapeDtypeStruct((NDEV, S, N_FF), BF16)],
        compiler_params=_cp(("parallel", "parallel")),
    )(h2, wg, wu)


def _ffn_down_loss(x2, a, wd, nw, tgt):
    tm = 512

    def body(x2_ref, a_ref, w_ref, nw_ref, t_ref, dx_ref, dxb_ref, st_ref, acc_ref):
        m, p = pl.program_id(0), pl.program_id(1)

        @pl.when(p == 0)
        def _():
            acc_ref[...] = jnp.zeros_like(acc_ref)

        @pl.when((p == 0) & (m == 0))
        def _():
            st_ref[...] = jnp.zeros_like(st_ref)

        acc_ref[...] += _dot(a_ref[...], w_ref[...])

        @pl.when(p == NDEV - 1)
        def _():
            x3 = x2_ref[...] + acc_ref[...]
            nwv = nw_ref[...]
            r = lax.rsqrt(jnp.mean(x3 * x3, axis=-1, keepdims=True) + EPS)
            y = (x3 * r) * nwv
            err = y - t_ref[...]
            loss = 0.5 * jnp.sum(jnp.mean(err * err, axis=-1, keepdims=True), axis=0, keepdims=True)
            dy = err * (1.0 / D)
            dx, dnw = _rms_bwd_tile(dy, x3, r, nwv)
            dx_ref[...] = dx
            dxb_ref[...] = dx.astype(BF16)
            st_ref[0:1, :] += dnw
            st_ref[1:2, :] += jnp.broadcast_to(loss, (1, D))

    return pl.pallas_call(
        body, name="ffn_down_loss", grid=(S // tm, NDEV),
        in_specs=[pl.BlockSpec((tm, D), lambda m, p: (m, 0)),
                  pl.BlockSpec((None, tm, N_FF), lambda m, p: (p, m, 0)),
                  pl.BlockSpec((None, N_FF, D), lambda m, p: (p, 0, 0)),
                  pl.BlockSpec((1, D), lambda m, p: (0, 0)),
                  pl.BlockSpec((tm, D), lambda m, p: (m, 0))],
        out_specs=[pl.BlockSpec((tm, D), lambda m, p: (m, 0)), pl.BlockSpec((tm, D), lambda m, p: (m, 0)),
                   pl.BlockSpec((8, D), lambda m, p: (0, 0))],
        out_shape=[jax.ShapeDtypeStruct((S, D), F32), jax.ShapeDtypeStruct((S, D), BF16),
                   jax.ShapeDtypeStruct((8, D), F32)],
        scratch_shapes=[pltpu.VMEM((tm, D), F32)],
        compiler_params=_cp(("arbitrary", "arbitrary")),
    )(x2, a, wd, nw, tgt)


def _ffn_down_bwd(dx3b, wd, g, u):
    tm = 1024

    def body(dx_ref, w_ref, g_ref, u_ref, dg_ref, du_ref):
        da = _dot_nt(dx_ref[...], w_ref[...])
        gv = g_ref[...]
        sg = _sigmoid(gv)
        silu = gv * sg
        dg_ref[...] = ((da * u_ref[...]) * (sg * (1.0 + gv * (1.0 - sg)))).astype(BF16)
        du_ref[...] = (da * silu).astype(BF16)

    blk = pl.BlockSpec((None, tm, N_FF), lambda p, m: (p, m, 0))
    return pl.pallas_call(
        body, name="ffn_down_bwd", grid=(NDEV, S // tm),
        in_specs=[pl.BlockSpec((tm, D), lambda p, m: (m, 0)),
                  pl.BlockSpec((None, N_FF, D), lambda p, m: (p, 0, 0)), blk, blk],
        out_specs=[blk, blk],
        out_shape=[jax.ShapeDtypeStruct((NDEV, S, N_FF), BF16), jax.ShapeDtypeStruct((NDEV, S, N_FF), BF16)],
        compiler_params=_cp(("parallel", "parallel")),
    )(dx3b, wd, g, u)


def _ffn_up_bwd(dg, du, wg, wu, dres, xs, r, nw):
    tm = 512

    def body(dg_ref, du_ref, wg_ref, wu_ref, dres_ref, x_ref, r_ref, nw_ref, dx_ref, dxb_ref, st_ref, acc_ref):
        m, p = pl.program_id(0), pl.program_id(1)

        @pl.when(p == 0)
        def _():
            acc_ref[...] = jnp.zeros_like(acc_ref)

        @pl.when((p == 0) & (m == 0))
        def _():
            st_ref[...] = jnp.zeros_like(st_ref)

        acc_ref[...] += _dot(dg_ref[...], wg_ref[...]) + _dot(du_ref[...], wu_ref[...])

        @pl.when(p == NDEV - 1)
        def _():
            dx, dnw = _rms_bwd_tile(acc_ref[...], x_ref[...], r_ref[...], nw_ref[...])
            dx = dres_ref[...] + dx
            dx_ref[...] = dx
            dxb_ref[...] = dx.astype(BF16)
            st_ref[0:1, :] += dnw

    blk = pl.BlockSpec((None, tm, N_FF), lambda m, p: (p, m, 0))
    wblk = pl.BlockSpec((None, N_FF, D), lambda m, p: (p, 0, 0))
    row = pl.BlockSpec((tm, D), lambda m, p: (m, 0))
    return pl.pallas_call(
        body, name="ffn_up_bwd", grid=(S // tm, NDEV),
        in_specs=[blk, blk, wblk, wblk, row, row, pl.BlockSpec((tm, 1), lambda m, p: (m, 0)),
                  pl.BlockSpec((1, D), lambda m, p: (0, 0))],
        out_specs=[row, row, pl.BlockSpec((8, D), lambda m, p: (0, 0))],
        out_shape=[jax.ShapeDtypeStruct((S, D), F32), jax.ShapeDtypeStruct((S, D), BF16),
                   jax.ShapeDtypeStruct((8, D), F32)],
        scratch_shapes=[pltpu.VMEM((tm, D), F32)],
        compiler_params=_cp(("arbitrary", "arbitrary")),
    )(dg, du, wg, wu, dres, xs, r, nw)


def _out_proj_bwd(dx2b, wout):
    tm = 256

    def body(dx_ref, w_ref, o_ref):
        o_ref[...] = _dot_nt(dx_ref[...], w_ref[...])

    return pl.pallas_call(
        body, name="out_proj_bwd", grid=(S // tm,),
        in_specs=[pl.BlockSpec((tm, D), lambda i: (i, 0)), pl.BlockSpec((D, D), lambda i: (0, 0))],
        out_specs=pl.BlockSpec((tm, D), lambda i: (i, 0)),
        out_shape=jax.ShapeDtypeStruct((S, D), F32),
        compiler_params=_cp(("parallel",)),
    )(dx2b, wout)


def _in_proj_bwd(dproj, win, dres, xs, r, nw):
    tm = 512

    def body(dp_ref, w_ref, dres_ref, x_ref, r_ref, nw_ref, dx_ref, st_ref, acc_ref):
        m, p = pl.program_id(0), pl.program_id(1)

        @pl.when(p == 0)
        def _():
            acc_ref[...] = jnp.zeros_like(acc_ref)

        @pl.when((p == 0) & (m == 0))
        def _():
            st_ref[...] = jnp.zeros_like(st_ref)

        acc_ref[...] += _dot_nt(dp_ref[...], w_ref[...])

        @pl.when(p == NDEV - 1)
        def _():
            dx, dnw = _rms_bwd_tile(acc_ref[...], x_ref[...], r_ref[...], nw_ref[...])
            dx_ref[...] = dres_ref[...] + dx
            st_ref[0:1, :] += dnw

    row = pl.BlockSpec((tm, D), lambda m, p: (m, 0))
    return pl.pallas_call(
        body, name="in_proj_bwd", grid=(S // tm, NDEV),
        in_specs=[pl.BlockSpec((tm, N_IN), lambda m, p: (m, p)),
                  pl.BlockSpec((None, D, N_IN), lambda m, p: (p, 0, 0)),
                  row, row, pl.BlockSpec((tm, 1), lambda m, p: (m, 0)),
                  pl.BlockSpec((1, D), lambda m, p: (0, 0))],
        out_specs=[row, pl.BlockSpec((8, D), lambda m, p: (0, 0))],
        out_shape=[jax.ShapeDtypeStruct((S, D), F32), jax.ShapeDtypeStruct((8, D), F32)],
        scratch_shapes=[pltpu.VMEM((tm, D), F32)],
        compiler_params=_cp(("arbitrary", "arbitrary")),
    )(dproj, win, dres, xs, r, nw)


def _wgrad_in(h1, dproj):
    def body(a_ref, d_ref, o_ref):
        o_ref[...] = _dot_tn(a_ref[...], d_ref[...]).astype(BF16)

    return pl.pallas_call(
        body, name="wgrad_in", grid=(NDEV,),
        in_specs=[pl.BlockSpec((S, D), lambda p: (0, 0)), pl.BlockSpec((S, N_IN), lambda p: (0, p))],
        out_specs=pl.BlockSpec((None, D, N_IN), lambda p: (p, 0, 0)),
        out_shape=jax.ShapeDtypeStruct((NDEV, D, N_IN), BF16),
        compiler_params=_cp(("parallel",)),
    )(h1, dproj)


def _wgrad_rows(a3, dy, name):
    def body(a_ref, d_ref, o_ref):
        o_ref[...] = _dot_tn(a_ref[...], d_ref[...]).astype(BF16)

    return pl.pallas_call(
        body, name=name, grid=(NDEV,),
        in_specs=[pl.BlockSpec((None, S, N_FF), lambda p: (p, 0, 0)), pl.BlockSpec((S, D), lambda p: (0, 0))],
        out_specs=pl.BlockSpec((None, N_FF, D), lambda p: (p, 0, 0)),
        out_shape=jax.ShapeDtypeStruct((NDEV, N_FF, D), BF16),
        compiler_params=_cp(("parallel",)),
    )(a3, dy)


def _wgrad_out(ma, mr, dx2b):
    half = D // 2
    per = half // N_OUT

    def body(ma_ref, mr_ref, d_ref, o_ref):
        p = pl.program_id(0)

        @pl.when(p < per)
        def _():
            o_ref[...] = _dot_tn(ma_ref[...], d_ref[...]).astype(BF16)

        @pl.when(p >= per)
        def _():
            o_ref[...] = _dot_tn(mr_ref[...], d_ref[...]).astype(BF16)

    return pl.pallas_call(
        body, name="wgrad_out", grid=(NDEV,),
        in_specs=[pl.BlockSpec((S, N_OUT), lambda p: (0, jnp.minimum(p, per - 1))),
                  pl.BlockSpec((S, N_OUT), lambda p: (0, jnp.maximum(p - per, 0))),
                  pl.BlockSpec((S, D), lambda p: (0, 0))],
        out_specs=pl.BlockSpec((None, N_OUT, D), lambda p: (p, 0, 0)),
        out_shape=jax.ShapeDtypeStruct((NDEV, N_OUT, D), BF16),
        compiler_params=_cp(("parallel",)),
    )(ma, mr, dx2b)


def _attn_consts():
    c = np.zeros((AH, 8, AHD), np.float32)
    for h in range(AH):
        c[h, :, :] = 2.0 ** (-(h + 1))
    return jnp.asarray(c)


def _permute_in(dst, src, d, cast=None):
    ln = S // d
    for rr in range(d):
        v = src[pl.ds(rr, ln, stride=d), :] if d > 1 else src[...]
        dst[rr * ln:(rr + 1) * ln, :] = v if cast is None else v.astype(cast)


def _attn_masks():
    qi = lax.broadcasted_iota(jnp.int32, (CH, CH), 0)
    kj = lax.broadcasted_iota(jnp.int32, (CH, CH), 1)
    dist_c = (qi - kj).astype(F32)
    dist_p = (qi - kj + CH).astype(F32)
    return (qi >= kj)[None], (kj >= qi)[None], dist_c[None], dist_p[None]


GB = 8


def _bdot_nt(a, b):
    return lax.dot_general(a, b, (((2,), (2,)), ((0,), (0,))), preferred_element_type=F32)


def _bdot(a, b):
    return lax.dot_general(a, b, (((2,), (1,)), ((0,), (0,))), preferred_element_type=F32)


def _bdot_tn(a, b):
    return lax.dot_general(a, b, (((1,), (1,)), ((0,), (0,))), preferred_element_type=F32)


def _shift_block(dst, src):
    dst[0:CH, :] = jnp.zeros((CH, AHD), dst.dtype)
    dst[CH:S, :] = src[0:S - CH, :]


def _has_prev(g, nb):
    blk = lax.broadcasted_iota(jnp.int32, (GB, 1, 1), 0) + g * GB
    return (blk & (nb - 1)) != 0


def _blocks(ref, g):
    return ref[g * GB * CH:(g + 1) * GB * CH, :].reshape(GB, CH, AHD)


def _attn_fwd(proj):
    scale = 1.0 / math.sqrt(AHD)

    def body(c_ref, q_ref, k_ref, v_ref, o_ref, ob_ref, lse_ref, qd, kd, vd, kps, vps, od, ld, *nat):
        onat, lnat = nat[0:3], nat[3:6]
        slope = c_ref[0:1, :]
        mask_c, mask_p, dist_c, dist_p = _attn_masks()
        for pi, (d, nb) in enumerate(PATTERNS):
            _permute_in(qd, q_ref, d, BF16)
            _permute_in(kd, k_ref, d, BF16)
            _permute_in(vd, v_ref, d, BF16)
            if nb > 1:
                _shift_block(kps, kd)
                _shift_block(vps, vd)
            bias_c = -(slope * float(d)) * dist_c
            bias_p = -(slope * float(d)) * dist_p
            for g in range(NB // GB):
                q3, k3, v3 = _blocks(qd, g), _blocks(kd, g), _blocks(vd, g)
                s_c = jnp.where(mask_c, _bdot_nt(q3, k3) * scale + bias_c, NEG)
                mx = jnp.max(s_c, axis=-1, keepdims=True)
                if nb > 1:
                    kp3, vp3 = _blocks(kps, g), _blocks(vps, g)
                    s_p = jnp.where(jnp.logical_and(mask_p, _has_prev(g, nb)),
                                    _bdot_nt(q3, kp3) * scale + bias_p, NEG)
                    mx = jnp.maximum(mx, jnp.max(s_p, axis=-1, keepdims=True))
                    l = (jnp.sum(jnp.exp(s_c - mx), axis=-1, keepdims=True)
                         + jnp.sum(jnp.exp(s_p - mx), axis=-1, keepdims=True))
                    lse = mx + jnp.log(l)
                    o3 = _bdot(jnp.exp(s_c - lse).astype(BF16), v3) + _bdot(jnp.exp(s_p - lse).astype(BF16), vp3)
                else:
                    l = jnp.sum(jnp.exp(s_c - mx), axis=-1, keepdims=True)
                    lse = mx + jnp.log(l)
                    o3 = _bdot(jnp.exp(s_c - lse).astype(BF16), v3)
                rows = slice(g * GB * CH, (g + 1) * GB * CH)
                od[rows, :] = o3.reshape(GB * CH, AHD)
                ld[rows, :] = jnp.broadcast_to(lse, (GB, CH, AHD)).reshape(GB * CH, AHD)
            ln = S // d
            for rr in range(d):
                if d > 1:
                    onat[pi][pl.ds(rr, ln, stride=d), :] = od[rr * ln:(rr + 1) * ln, :]
                    lnat[pi][pl.ds(rr, ln, stride=d), :] = ld[rr * ln:(rr + 1) * ln, :]
                else:
                    onat[pi][...] = od[...]
                    lnat[pi][...] = ld[...]
        l0, l1, l2 = lnat[0][...], lnat[1][...], lnat[2][...]
        mx = jnp.maximum(jnp.maximum(l0, l1), l2)
        e0, e1, e2 = jnp.exp(l0 - mx), jnp.exp(l1 - mx), jnp.exp(l2 - mx)
        den = e0 + e1 + e2
        out = (e0 / den) * onat[0][...] + (e1 / den) * onat[1][...] + (e2 / den) * onat[2][...]
        o_ref[...] = out
        ob_ref[...] = out.astype(BF16)
        lse_ref[...] = mx + jnp.log(den)

    def col(off):
        return pl.BlockSpec((S, AHD), lambda h: (0, off + h))

    return pl.pallas_call(
        body, name="attn_fwd", grid=(AH,),
        in_specs=[pl.BlockSpec((None, 8, AHD), lambda h: (h, 0, 0)), col(0), col(AH), col(2 * AH)],
        out_specs=[col(0), col(0), col(0)],
        out_shape=[jax.ShapeDtypeStruct((S, AH * AHD), F32), jax.ShapeDtypeStruct((S, AH * AHD), BF16),
                   jax.ShapeDtypeStruct((S, AH * AHD), F32)],
        scratch_shapes=[pltpu.VMEM((S, AHD), BF16) for _ in range(5)]
        + [pltpu.VMEM((S, AHD), F32) for _ in range(8)],
        compiler_params=_cp(("parallel",)),
    )(_attn_consts(), proj, proj, proj)


def _attn_bwd(proj, dmixed, o, lse):
    scale = 1.0 / math.sqrt(AHD)

    def body(c_ref, q_ref, k_ref, v_ref, do_ref, o_ref, lse_ref, dq_ref, dk_ref, dv_ref,
             qd, kd, vd, dod, kps, vps, lsd, dld, dqd, dkd, dvd, delta, aq, ak, av):
        slope = c_ref[0:1, :]
        mask_c, mask_p, dist_c, dist_p = _attn_masks()
        delta[...] = jnp.broadcast_to(jnp.sum(do_ref[...] * o_ref[...], axis=-1, keepdims=True), (S, AHD))
        for pi, (d, nb) in enumerate(PATTERNS):
            _permute_in(qd, q_ref, d, BF16)
            _permute_in(kd, k_ref, d, BF16)
            _permute_in(vd, v_ref, d, BF16)
            _permute_in(dod, do_ref, d, BF16)
            _permute_in(lsd, lse_ref, d)
            _permute_in(dld, delta, d)
            if nb > 1:
                _shift_block(kps, kd)
                _shift_block(vps, vd)
            bias_c = -(slope * float(d)) * dist_c
            bias_p = -(slope * float(d)) * dist_p
            for g in range(NB // GB):
                q3, k3, v3, do3 = _blocks(qd, g), _blocks(kd, g), _blocks(vd, g), _blocks(dod, g)
                ls, dl = _blocks(lsd, g), _blocks(dld, g)
                lo, hi = g * GB * CH, (g + 1) * GB * CH
                p_c = jnp.exp(jnp.where(mask_c, _bdot_nt(q3, k3) * scale + bias_c, NEG) - ls)
                ds_c = ((p_c * (_bdot_nt(do3, v3) - dl)) * scale).astype(BF16)
                dq3 = _bdot(ds_c, k3)
                dkd[lo:hi, :] = _bdot_tn(ds_c, q3).reshape(GB * CH, AHD)
                dvd[lo:hi, :] = _bdot_tn(p_c.astype(BF16), do3).reshape(GB * CH, AHD)
                if nb > 1:
                    kp3, vp3 = _blocks(kps, g), _blocks(vps, g)
                    p_p = jnp.exp(jnp.where(jnp.logical_and(mask_p, _has_prev(g, nb)),
                                            _bdot_nt(q3, kp3) * scale + bias_p, NEG) - ls)
                    ds_p = ((p_p * (_bdot_nt(do3, vp3) - dl)) * scale).astype(BF16)
                    dq3 = dq3 + _bdot(ds_p, kp3)
                    dkp = _bdot_tn(ds_p, q3).reshape(GB * CH, AHD)
                    dvp = _bdot_tn(p_p.astype(BF16), do3).reshape(GB * CH, AHD)
                    if g == 0:
                        dkd[0:hi - CH, :] += dkp[CH:, :]
                        dvd[0:hi - CH, :] += dvp[CH:, :]
                    else:
                        dkd[lo - CH:hi - CH, :] += dkp
                        dvd[lo - CH:hi - CH, :] += dvp
                dqd[lo:hi, :] = dq3.reshape(GB * CH, AHD)
            ln = S // d
            for acc, src in ((aq, dqd), (ak, dkd), (av, dvd)):
                if pi == 0:
                    acc[...] = src[...]
                else:
                    for rr in range(d):
                        acc[pl.ds(rr, ln, stride=d), :] += src[rr * ln:(rr + 1) * ln, :]
        dq_ref[...] = aq[...].astype(BF16)
        dk_ref[...] = ak[...].astype(BF16)
        dv_ref[...] = av[...].astype(BF16)

    def col(off):
        return pl.BlockSpec((S, AHD), lambda h: (0, off + h))

    return pl.pallas_call(
        body, name="attn_bwd", grid=(AH,),
        in_specs=[pl.BlockSpec((None, 8, AHD), lambda h: (h, 0, 0)), col(0), col(AH), col(2 * AH),
                  col(0), col(0), col(0)],
        out_specs=[col(0), col(0), col(0)],
        out_shape=[jax.ShapeDtypeStruct((S, AH * AHD), BF16)] * 3,
        scratch_shapes=[pltpu.VMEM((S, AHD), BF16) for _ in range(6)]
        + [pltpu.VMEM((S, AHD), F32) for _ in range(9)],
        compiler_params=_cp(("parallel",)),
    )(_attn_consts(), proj, proj, proj, dmixed, o, lse)


def _ret_consts():
    c = np.zeros((RH, 8, RHD), np.float32)
    for h in range(RH):
        c[h, :, :] = np.log(np.float32(1.0) - np.float32(2.0 ** (-5.0 - h)))
    return jnp.asarray(c)


def _ret_factors(lg):
    i = lax.broadcasted_iota(jnp.int32, (CH, CH), 0)
    j = lax.broadcasted_iota(jnp.int32, (CH, CH), 1)
    dif = (i - j).astype(F32)
    decay = jnp.where(dif >= 0, jnp.exp(lg[:, 0:CH] * jnp.maximum(dif, 0.0)), 0.0)
    row = lax.broadcasted_iota(jnp.int32, (CH, RHD), 0).astype(F32)
    zeta = jnp.exp(lg * (CH - 1.0 - row))
    xi = jnp.exp(lg * (row + 1.0))
    return decay, zeta, xi, jnp.exp(lg * float(CH))


def _ret_specs(rev):
    off = 3 * AH * AHD // RHD

    def ch(n):
        return (NB - 1 - n) if rev else n

    def col(k):
        return pl.BlockSpec((CH, RHD), lambda h, n: (ch(n), off + k * RH + h))

    own = pl.BlockSpec((CH, RHD), lambda h, n: (ch(n), h))
    state = pl.BlockSpec((None, None, RHD, RHD), lambda h, n: (h, ch(n), 0, 0))
    const = pl.BlockSpec((None, 8, RHD), lambda h, n: (h, 0, 0))
    return col, own, state, const


def _ret_fwd(proj):
    def body(c_ref, q_ref, k_ref, v_ref, g_ref, ret_ref, mr_ref, st_ref, r_acc):
        n = pl.program_id(1)

        @pl.when(n == 0)
        def _():
            r_acc[...] = jnp.zeros_like(r_acc)

        decay, zeta, xi, gch = _ret_factors(c_ref[0:1, :])
        qb = q_ref[...].astype(BF16)
        kc = k_ref[...] * (1.0 / math.sqrt(RHD))
        kb = kc.astype(BF16)
        vb = v_ref[...].astype(BF16)
        rb = r_acc[...].astype(BF16)
        st_ref[...] = rb
        scores = _dot_nt(qb, kb) * decay
        ret = _dot(scores.astype(BF16), vb) + _dot(qb, rb) * xi
        r_acc[...] = r_acc[...] * gch + _dot_tn((kc * zeta).astype(BF16), vb)
        ret_ref[...] = ret
        rr = lax.rsqrt(jnp.mean(ret * ret, axis=-1, keepdims=True) + EPS)
        gv = g_ref[...]
        mr_ref[...] = ((gv * _sigmoid(gv)) * (ret * rr)).astype(BF16)

    col, own, state, const = _ret_specs(False)
    return pl.pallas_call(
        body, name="ret_fwd", grid=(RH, NB),
        in_specs=[const, col(0), col(1), col(2), col(3)],
        out_specs=[own, own, state],
        out_shape=[jax.ShapeDtypeStruct((S, RH * RHD), F32), jax.ShapeDtypeStruct((S, RH * RHD), BF16),
                   jax.ShapeDtypeStruct((RH, NB, RHD, RHD), BF16)],
        scratch_shapes=[pltpu.VMEM((RHD, RHD), F32)],
        compiler_params=_cp(("parallel", "arbitrary")),
    )(_ret_consts(), proj, proj, proj, proj)


def _ret_bwd(proj, ret, states, dmixed):
    def body(c_ref, q_ref, k_ref, v_ref, g_ref, ret_ref, st_ref, dm_ref, dq_ref, dk_ref, dv_ref, dg_ref, g_acc):
        n = pl.program_id(1)

        @pl.when(n == 0)
        def _():
            g_acc[...] = jnp.zeros_like(g_acc)

        decay, zeta, xi, gch = _ret_factors(c_ref[0:1, :])
        ret_v = ret_ref[...]
        rr = lax.rsqrt(jnp.mean(ret_v * ret_v, axis=-1, keepdims=True) + EPS)
        gv = g_ref[...]
        sg = _sigmoid(gv)
        dmix = dm_ref[...]
        dg_ref[...] = ((dmix * (ret_v * rr)) * (sg * (1.0 + gv * (1.0 - sg)))).astype(BF16)
        dretn = dmix * (gv * sg)
        dret = rr * dretn - ret_v * ((rr * rr * rr) * jnp.mean(dretn * ret_v, axis=-1, keepdims=True))

        qb = q_ref[...].astype(BF16)
        kc = k_ref[...] * (1.0 / math.sqrt(RHD))
        kb = kc.astype(BF16)
        vb = v_ref[...].astype(BF16)
        rb = st_ref[...]
        db = dret.astype(BF16)
        sc = (_dot_nt(qb, kb) * decay).astype(BF16)
        da = (_dot_nt(db, vb) * decay).astype(BF16)
        dxi = (dret * xi).astype(BF16)
        gb = g_acc[...].astype(BF16)
        kz = (kc * zeta).astype(BF16)
        dq = _dot(da, kb) + _dot_nt(dxi, rb)
        dkc = _dot_tn(da, qb) + _dot_nt(vb, gb) * zeta
        dv = _dot_tn(sc, db) + _dot(kz, gb)
        g_acc[...] = _dot_tn(qb, dxi) + gch * g_acc[...]
        dq_ref[...] = dq.astype(BF16)
        dk_ref[...] = (dkc * (1.0 / math.sqrt(RHD))).astype(BF16)
        dv_ref[...] = dv.astype(BF16)

    col, own, state, const = _ret_specs(True)
    dm = pl.BlockSpec((CH, RHD), lambda h, n: (NB - 1 - n, AH * AHD // RHD + h))
    return pl.pallas_call(
        body, name="ret_bwd", grid=(RH, NB),
        in_specs=[const, col(0), col(1), col(2), col(3), own, state, dm],
        out_specs=[own, own, own, own],
        out_shape=[jax.ShapeDtypeStruct((S, RH * RHD), BF16)] * 4,
        scratch_shapes=[pltpu.VMEM((RHD, RHD), F32)],
        compiler_params=_cp(("parallel", "arbitrary")),
    )(_ret_consts(), proj, proj, proj, proj, ret, states, dmixed)


class _NoReduction:
    def start(self, group, grads):
        pass

    def local(self, name, first=()):
        return []

    def landed(self, name):
        return []

    def update(self, name):
        return []


def _local_step(x, tgt, nw1, nw2, nw3, win, wout, wg, wu, wd, red=None):
    red = red or _NoReduction()

    def after(values, first):
        return lax.optimization_barrier((tuple(values), tuple(first)))[0]

    h1, r1 = _rms_fwd(x, nw1)
    proj = _proj(h1, win)
    o, ma, lse = _attn_fwd(proj)
    ret, mr, states = _ret_fwd(proj)
    x2, h2, r2 = _out_proj_rms(x, ma, mr, wout, nw2)
    g, u, a = _ffn_up(h2, wg, wu)
    dx3, dx3b, st3 = _ffn_down_loss(x2, a, wd, nw3, tgt)

    dwd = _wgrad_rows(a, dx3b, "wgrad_down")
    red.start(["w_down"], [dwd])
    (dx3b,) = after([dx3b], [dwd])
    dg, du = _ffn_down_bwd(dx3b, wd, g, u)
    dg, du = after([dg, du], red.local("w_down", first=[dg]))
    dwg = _wgrad_rows(dg, h2, "wgrad_gate")
    dwu = _wgrad_rows(du, h2, "wgrad_up")
    red.start(["w_gate", "w_up"], [dwg, dwu])
    dg, du = after([dg, du], [dwg, dwu])
    dx2, dx2b, st2 = _ffn_up_bwd(dg, du, wg, wu, dx3, x2, r2, nw2)
    (dx2b,) = after([dx2b], red.landed("w_down"))
    dwo = _wgrad_out(ma, mr, dx2b)
    red.start(["w_out"], [dwo])
    (dx2b,) = after([dx2b], [dwo] + red.local("w_gate"))
    dmixed = _out_proj_bwd(dx2b, wout)
    dqa, dka, dva = _attn_bwd(proj, dmixed, o, lse)
    (dmixed,) = after([dmixed], [dqa] + red.local("w_out"))
    dqr, dkr, dvr, dgr = _ret_bwd(proj, ret, states, dmixed)
    dproj = jnp.concatenate([dqa, dka, dva, dqr, dkr, dvr, dgr], axis=1)
    dwi = _wgrad_in(h1, dproj)
    red.start(["w_in"], after([dwi], red.landed("w_gate") + red.landed("w_out")))
    early = red.update("w_down") + red.update("w_gate")
    (dproj,) = after([dproj], red.local("w_in", first=early))
    gx, st1 = _in_proj_bwd(dproj, win, dx2, x, r1, nw1)
    stats = jnp.concatenate([st1[0:1], st2[0:1], st3[0:2], jnp.zeros((4, D), F32)], axis=0)
    return stats, gx, dwi, dwo, dwg, dwu, dwd


def _place():
    x, y, c = lax.axis_index("x"), lax.axis_index("y"), lax.axis_index("c")
    return x, y, c, [(1 - x, y), (x, 1 - y), (1 - x, 1 - y)]


def _handshake(peers):
    barrier = pltpu.get_barrier_semaphore()
    for peer in peers:
        pl.semaphore_signal(barrier, inc=1, device_id=peer, device_id_type=MESH)
    pl.semaphore_wait(barrier, len(peers))


def _all_gather(shards, name, collective_id):
    na = len(shards)
    SIB, XN0, XN1, YN1, YN0, VIA_X, VIA_Y = 0, 1, 2, 3, 4, 5, 6
    D2D = {XN0: 7, XN1: 8, YN1: 9, YN0: 10, VIA_X: 11, VIA_Y: 12}

    def body(*refs):
        ins, outs = refs[:na], refs[na:2 * na]
        send_sems, recv_sems, local_sems = refs[2 * na:]
        x, y, c, _ = _place()
        me, sib = (x, y, c), (x, y, 1 - c)
        xn, yn, dg = (1 - x, y, c), (x, 1 - y, c), (1 - x, 1 - y, c)
        _handshake([sib, xn, yn])

        def part(ref, h):
            rows = ref.shape[0] // 2
            return ref if h is None else ref.at[pl.ds(h * rows, rows)]

        def block(a, owner, h):
            return part(outs[a].at[4 * owner[0] + 2 * owner[1] + owner[2]], h)

        def copy(a, k, owner, h, to, own_src=False):
            return pltpu.make_async_remote_copy(
                src_ref=part(ins[a], h) if own_src else block(a, owner, h), dst_ref=block(a, owner, h),
                send_sem=send_sems.at[a, k], recv_sem=recv_sems.at[a, k], device_id=to, device_id_type=MESH)

        def other(p):
            return (p[0], p[1], 1 - c)

        mine = [pltpu.make_async_copy(ins[a], block(a, me, None), local_sems.at[a]) for a in range(na)]
        for cp in mine:
            cp.start()
        sent = []
        for a in range(na):
            sent += [copy(a, XN0, me, 0, xn, True), copy(a, YN1, me, 1, yn, True),
                     copy(a, XN1, me, 1, xn, True), copy(a, YN0, me, 0, yn, True)]
        sent += [copy(a, SIB, me, None, sib, True) for a in range(na)]
        for cp in sent:
            cp.start()

        def landed(a, k, owner, h, then):
            copy(a, k, owner, h, me).wait_recv()
            for k2, to in then + [(D2D[k], sib)]:
                cp = copy(a, k2, owner, h, to)
                cp.start()
                sent.append(cp)

        for a in range(na):
            landed(a, XN0, xn, 0, [(VIA_Y, yn)])
            landed(a, YN1, yn, 1, [(VIA_X, xn)])
            landed(a, XN1, xn, 1, [])
            landed(a, YN0, yn, 0, [])
        for a in range(na):
            landed(a, VIA_Y, dg, 0, [])
            landed(a, VIA_X, dg, 1, [])
        for a in range(na):
            copy(a, SIB, sib, None, me).wait_recv()
            for k, owner, h in ((XN0, xn, 0), (XN1, xn, 1), (YN1, yn, 1), (YN0, yn, 0), (VIA_Y, dg, 0), (VIA_X, dg, 1)):
                copy(a, D2D[k], other(owner), h, me).wait_recv()
        for cp in sent:
            cp.wait_send()
        for cp in mine:
            cp.wait()

    return _sequencer_call(
        body, name, collective_id,
        [jax.ShapeDtypeStruct((NDEV,) + s.shape, s.dtype) for s in shards],
        [pltpu.SemaphoreType.DMA((na, 13)), pltpu.SemaphoreType.DMA((na, 13)), pltpu.SemaphoreType.DMA((na,))])(*shards)


def _sequencer_call(body, name, collective_id, out_type, scratch_types):
    return pl.kernel(
        body, name=name, out_type=out_type,
        mesh=plsc.ScalarSubcoreMesh(axis_name="sequencer", num_cores=1),
        scratch_types=scratch_types,
        compiler_params=pltpu.CompilerParams(collective_id=collective_id))


def _exchange_sibling(grads, name, collective_id):
    na = len(grads)

    def body(*refs):
        ins, outs = refs[:na], refs[na:2 * na]
        send_sems, recv_sems = refs[2 * na:]
        x, y, c, _ = _place()
        _handshake([(x, y, 1 - c)])
        cps = []
        for a in range(na):
            for k in range(4):
                cps.append(pltpu.make_async_remote_copy(
                    src_ref=ins[a].at[2 * k + (1 - c)], dst_ref=outs[a].at[k],
                    send_sem=send_sems.at[a, k], recv_sem=recv_sems.at[a, k],
                    device_id=(x, y, 1 - c), device_id_type=MESH))
        for cp in cps:
            cp.start()
        for cp in cps:
            cp.wait()

    return _sequencer_call(
        body, name, collective_id,
        [jax.ShapeDtypeStruct((4,) + g.shape[1:], g.dtype) for g in grads],
        [pltpu.SemaphoreType.DMA((na, 4)), pltpu.SemaphoreType.DMA((na, 4))])(*grads)


def _row_tile(rows, cols):
    for t in (512, 256, 176, 128, 64, 32, 16):
        if rows % t == 0 and t * cols * 4 <= (1 << 20):
            return t
    raise ValueError((rows, cols))


def _chip_sum(place, g, got, name):
    _, r, c = g.shape
    tm = _row_tile(r, c)

    def body(pos_ref, g_ref, got_ref, o_ref):
        o_ref[...] = (g_ref[...].astype(F32) + got_ref[...].astype(F32)).astype(BF16)

    def chip(j, pos):
        return 2 * (pos[0] ^ jnp.where(j == 1, 0, 1)) + (pos[1] ^ jnp.where(j == 0, 0, 1))

    return pl.pallas_call(
        body, name=name,
        grid_spec=pltpu.PrefetchScalarGridSpec(
            num_scalar_prefetch=1, grid=(3, r // tm),
            in_specs=[pl.BlockSpec((None, tm, c), lambda j, i, pos: (2 * chip(j, pos) + pos[2], i, 0)),
                      pl.BlockSpec((None, tm, c), lambda j, i, pos: (chip(j, pos), i, 0))],
            out_specs=pl.BlockSpec((None, tm, c), lambda j, i, pos: (j, i, 0))),
        out_shape=jax.ShapeDtypeStruct((3, r, c), BF16),
        compiler_params=_cp(("parallel", "parallel")),
    )(place, g, got)


def _exchange_chips(sums, name, collective_id):
    na = len(sums)

    def body(*refs):
        ins, outs = refs[:na], refs[na:2 * na]
        send_sems, recv_sems = refs[2 * na:]
        x, y, c, chips = _place()
        _handshake([(*chip, c) for chip in chips])
        cps = []
        for a in range(na):
            for j, chip in enumerate(chips):
                cps.append(pltpu.make_async_remote_copy(
                    src_ref=ins[a].at[j], dst_ref=outs[a].at[j],
                    send_sem=send_sems.at[a, j], recv_sem=recv_sems.at[a, j],
                    device_id=(*chip, c), device_id_type=MESH))
        for cp in cps:
            cp.start()
        for cp in cps:
            cp.wait()

    return _sequencer_call(
        body, name, collective_id,
        [jax.ShapeDtypeStruct((3,) + s.shape[1:], s.dtype) for s in sums],
        [pltpu.SemaphoreType.DMA((na, 3)), pltpu.SemaphoreType.DMA((na, 3))])(*sums)


def _exchange_stats(stats, collective_id):
    def body(st_in, st_out, st_send, st_recv, local_sem):
        x, y, c, _ = _place()
        me_idx = 4 * x + 2 * y + c
        peers = [(x ^ ((k >> 2) & 1), y ^ ((k >> 1) & 1), c ^ (k & 1)) for k in range(1, 8)]
        _handshake(peers)
        mine = pltpu.make_async_copy(st_in, st_out.at[me_idx], local_sem)
        mine.start()
        cps = [pltpu.make_async_remote_copy(
            src_ref=st_in, dst_ref=st_out.at[me_idx], send_sem=st_send.at[k], recv_sem=st_recv.at[k],
            device_id=peer, device_id_type=MESH) for k, peer in enumerate(peers)]
        for cp in cps:
            cp.start()
        for cp in cps:
            cp.wait()
        mine.wait()

    return _sequencer_call(
        body, "exchange_stats", collective_id,
        jax.ShapeDtypeStruct((NDEV,) + stats.shape, stats.dtype),
        [pltpu.SemaphoreType.DMA((7,)), pltpu.SemaphoreType.DMA((7,)), pltpu.SemaphoreType.DMA])(stats)


class _Reduction:
    def __init__(self, place, first_collective_id, state):
        self.place = place
        self.ids = iter(range(first_collective_id, 32))
        self.state = state
        self.groups = {}
        self.updates = {}

    def next_id(self):
        return next(self.ids)

    def start(self, group, grads):
        got = _exchange_sibling(grads, "sibling_exchange_" + group[0], self.next_id())
        self.groups[group[0]] = dict(names=group, grads=grads, got=got)

    def local(self, name, first=()):
        grp = self.groups[name]
        grads, got = lax.optimization_barrier((tuple(grp["grads"]), tuple(grp["got"]), tuple(first)))[:2]
        grp["sums"] = [_chip_sum(self.place, g, s, "chip_sum_" + n) for g, s, n in zip(grads, got, grp["names"])]
        grp["chips"] = _exchange_chips(grp["sums"], "chip_exchange_" + name, self.next_id())
        return grp["sums"]

    def landed(self, name):
        return list(self.groups[name]["chips"])

    def update(self, name):
        if name not in self.updates:
            grp = next(g for g in self.groups.values() if name in g["names"])
            k = grp["names"].index(name)
            self.updates[name] = _shard_update(self.place, *self.state[name], grp["grads"][k], grp["got"][k],
                                               grp["chips"][k], "update_" + name)
        return list(self.updates[name])


def _adamw(w, g, m, v):
    m = ADAM_B1 * m + (1.0 - ADAM_B1) * g
    v = ADAM_B2 * v + (1.0 - ADAM_B2) * (g * g)
    m_hat = m / (1.0 - ADAM_B1 ** ADAM_STEP)
    v_hat = v / (1.0 - ADAM_B2 ** ADAM_STEP)
    delta = -ADAM_LR * (m_hat / (jnp.sqrt(v_hat) + ADAM_EPS) + ADAM_WD * w)
    return delta, m, v


def _shard_update(place, w, m, v, g, got_sib, got_chips, name):
    r, c = w.shape
    tm = _row_tile(r, c)

    def body(pos_ref, w_ref, m_ref, v_ref, g_ref, s_ref, c_ref, go_ref, d_ref, mo_ref, vo_ref):
        grad = g_ref[...].astype(F32) + s_ref[...].astype(F32)
        for j in range(3):
            grad = grad + c_ref[j].astype(F32)
        delta, mn, vn = _adamw(w_ref[...], grad, m_ref[...], v_ref[...])
        go_ref[...] = grad
        d_ref[...] = delta
        mo_ref[...] = mn
        vo_ref[...] = vn

    row = pl.BlockSpec((tm, c), lambda i, pos: (i, 0))
    return pl.pallas_call(
        body, name=name,
        grid_spec=pltpu.PrefetchScalarGridSpec(
            num_scalar_prefetch=1, grid=(r // tm,),
            in_specs=[row, row, row,
                      pl.BlockSpec((None, tm, c), lambda i, pos: (4 * pos[0] + 2 * pos[1] + pos[2], i, 0)),
                      pl.BlockSpec((None, tm, c), lambda i, pos: (2 * pos[0] + pos[1], i, 0)),
                      pl.BlockSpec((3, tm, c), lambda i, pos: (0, i, 0))],
            out_specs=[row, row, row, row]),
        out_shape=[jax.ShapeDtypeStruct((r, c), F32)] * 4,
        compiler_params=_cp(("parallel",)),
    )(place, w, m, v, g, got_sib, got_chips)


def _small_update(stats_all, ws, ms, vs):
    def body(st_ref, w_ref, m_ref, v_ref, go_ref, d_ref, mo_ref, vo_ref):
        grad = st_ref[0]
        for k in range(1, NDEV):
            grad = grad + st_ref[k]
        delta, mn, vn = _adamw(w_ref[...], grad, m_ref[...], v_ref[...])
        go_ref[...] = grad
        d_ref[...] = delta
        mo_ref[...] = mn
        vo_ref[...] = vn

    return pl.pallas_call(
        body, name="small_update",
        out_shape=[jax.ShapeDtypeStruct((8, D), F32)] * 4,
        compiler_params=_cp(),
    )(stats_all, ws, ms, vs)


def kernel(x, norm_mix_w, w_in, w_out, norm_ffn_w, w_gate, w_up, w_down, norm_final_w, loss_target, m_norm_mix_w, m_w_in, m_w_out, m_norm_ffn_w, m_w_gate, m_w_up, m_w_down, m_norm_final_w, v_norm_mix_w, v_w_in, v_w_out, v_norm_ffn_w, v_w_gate, v_w_up, v_w_down, v_norm_final_w):
    tr = {"w_gate", "w_up"}
    names = ["w_in", "w_out", "w_gate", "w_up", "w_down"]

    def view(a, n):
        return a[0].T if n in tr else a[0]

    big_w = [view(a, n) for a, n in zip([w_in, w_out, w_gate, w_up, w_down], names)]
    big_m = [view(a, n) for a, n in zip([m_w_in, m_w_out, m_w_gate, m_w_up, m_w_down], names)]
    big_v = [view(a, n) for a, n in zip([v_w_in, v_w_out, v_w_gate, v_w_up, v_w_down], names)]

    shards = [_cast_bf16(w, "cast_" + n) for w, n in zip(big_w, names)]
    (win,) = _all_gather(shards[0:1], "all_gather_w_in", 1)
    wout, wg, wu = _all_gather(shards[1:4], "all_gather_out_gate_up", 2)
    (wd,) = _all_gather(shards[4:5], "all_gather_w_down", 3)
    nw3 = norm_final_w.reshape(1, D)
    place = jnp.stack([lax.axis_index("x"), lax.axis_index("y"), lax.axis_index("c")]).astype(jnp.int32)
    red = _Reduction(place, 4, {n: (w, m, v) for n, w, m, v in zip(names, big_w, big_m, big_v)})
    stats, gx, *_ = _local_step(
        x[0], loss_target[0], norm_mix_w, norm_ffn_w, nw3, win, wout.reshape(D, D), wg, wu, wd, red)
    stats_all = _exchange_stats(stats, red.next_id())
    upd = [red.update(n) for n in names]
    stats_all = lax.optimization_barrier((stats_all, tuple(upd[0])))[0]

    def rows(a, b, c):
        return jnp.concatenate([a.reshape(1, D), b.reshape(1, D), c.reshape(1, D), jnp.zeros((5, D), F32)], axis=0)

    sg, sd, sm, sv = _small_update(stats_all, rows(norm_mix_w, norm_ffn_w, norm_final_w),
                                   rows(m_norm_mix_w, m_norm_ffn_w, m_norm_final_w),
                                   rows(v_norm_mix_w, v_norm_ffn_w, v_norm_final_w))
    loss = sg[3, 0]

    def outs(k, small):
        big = [(u[k].T if n in tr else u[k])[None] for u, n in zip(upd, names)]
        return [small[0:1], big[0], big[1], small[1:2], big[2], big[3], big[4], small[2]]

    return (loss, gx[None], *outs(0, sg), *outs(1, sd), *outs(2, sm), *outs(3, sv))
```

```python
import functools
import math

import numpy as np
import jax
import jax.numpy as jnp
from jax import lax
from jax.experimental import pallas as pl
from jax.experimental.pallas import tpu as pltpu
from jax.experimental.pallas import tpu_sc as plsc

F32 = jnp.float32
BF16 = jnp.bfloat16

S = 2048
D = 2048
NDEV = 8
N_IN = 7168 // NDEV
N_FF = 5632 // NDEV
N_OUT = 2048 // NDEV
AH, AHD = 8, 128
RH, RHD = 4, 256
CH = 128
NB = S // CH
EPS = 1e-6
PATTERNS = ((1, 16), (4, 4), (16, 1))
NEG = -1e30
VMEM_LIMIT = 56 * 1024 * 1024

ADAM_LR, ADAM_B1, ADAM_B2, ADAM_EPS, ADAM_WD, ADAM_STEP = 0.001, 0.9, 0.999, 1e-08, 0.01, 10
MESH = pl.DeviceIdType.MESH


def _cp(sem=None):
    return pltpu.CompilerParams(dimension_semantics=sem, vmem_limit_bytes=VMEM_LIMIT)


def _dot(a, b):
    return jnp.dot(a, b, preferred_element_type=F32)


def _dot_nt(a, b):
    return lax.dot_general(a, b, (((1,), (1,)), ((), ())), preferred_element_type=F32)


def _dot_tn(a, b):
    return lax.dot_general(a, b, (((0,), (0,)), ((), ())), preferred_element_type=F32)


def _sigmoid(x):
    return 0.5 * jnp.tanh(0.5 * x) + 0.5


def _cast_bf16(w, name):
    r, c = w.shape
    tm = r if r <= 1024 else 512

    def body(w_ref, o_ref):
        o_ref[...] = w_ref[...].astype(BF16)

    return pl.pallas_call(
        body, name=name, grid=(r // tm,),
        in_specs=[pl.BlockSpec((tm, c), lambda i: (i, 0))],
        out_specs=pl.BlockSpec((tm, c), lambda i: (i, 0)),
        out_shape=jax.ShapeDtypeStruct((r, c), BF16),
        compiler_params=_cp(("parallel",)),
    )(w)


def _rms_fwd(x, nw):
    tm = 256

    def body(x_ref, w_ref, h_ref, r_ref):
        xs = x_ref[...]
        r = lax.rsqrt(jnp.mean(xs * xs, axis=-1, keepdims=True) + EPS)
        h_ref[...] = ((xs * r) * w_ref[...]).astype(BF16)
        r_ref[...] = r

    return pl.pallas_call(
        body, name="rms_fwd", grid=(S // tm,),
        in_specs=[pl.BlockSpec((tm, D), lambda i: (i, 0)), pl.BlockSpec((1, D), lambda i: (0, 0))],
        out_specs=[pl.BlockSpec((tm, D), lambda i: (i, 0)), pl.BlockSpec((tm, 1), lambda i: (i, 0))],
        out_shape=[jax.ShapeDtypeStruct((S, D), BF16), jax.ShapeDtypeStruct((S, 1), F32)],
        compiler_params=_cp(("parallel",)),
    )(x, nw)


def _rms_bwd_tile(dh, xs, r, nw):
    dnw = jnp.sum(dh * (xs * r), axis=0, keepdims=True)
    gy = dh * nw
    dx = r * gy - xs * ((r * r * r) * jnp.mean(gy * xs, axis=-1, keepdims=True))
    return dx, dnw


def _proj(h1, win):
    tm = 1024

    def body(a_ref, w_ref, o_ref):
        o_ref[...] = _dot(a_ref[...], w_ref[...])

    return pl.pallas_call(
        body, name="proj", grid=(NDEV, S // tm),
        in_specs=[pl.BlockSpec((tm, D), lambda p, m: (m, 0)),
                  pl.BlockSpec((None, D, N_IN), lambda p, m: (p, 0, 0))],
        out_specs=pl.BlockSpec((tm, N_IN), lambda p, m: (m, p)),
        out_shape=jax.ShapeDtypeStruct((S, NDEV * N_IN), F32),
        compiler_params=_cp(("parallel", "parallel")),
    )(h1, win)


def _out_proj_rms(x, ma, mr, wout, nw):
    tm = 256
    half = D // 2

    def body(x_ref, ma_ref, mr_ref, w_ref, nw_ref, x2_ref, h_ref, r_ref):
        acc = _dot(ma_ref[...], w_ref[0:half, :]) + _dot(mr_ref[...], w_ref[half:D, :])
        x2 = x_ref[...] + acc
        r = lax.rsqrt(jnp.mean(x2 * x2, axis=-1, keepdims=True) + EPS)
        x2_ref[...] = x2
        h_ref[...] = ((x2 * r) * nw_ref[...]).astype(BF16)
        r_ref[...] = r

    return pl.pallas_call(
        body, name="out_proj_rms", grid=(S // tm,),
        in_specs=[pl.BlockSpec((tm, D), lambda i: (i, 0)),
                  pl.BlockSpec((tm, half), lambda i: (i, 0)),
                  pl.BlockSpec((tm, half), lambda i: (i, 0)),
                  pl.BlockSpec((D, D), lambda i: (0, 0)),
                  pl.BlockSpec((1, D), lambda i: (0, 0))],
        out_specs=[pl.BlockSpec((tm, D), lambda i: (i, 0)), pl.BlockSpec((tm, D), lambda i: (i, 0)),
                   pl.BlockSpec((tm, 1), lambda i: (i, 0))],
        out_shape=[jax.ShapeDtypeStruct((S, D), F32), jax.ShapeDtypeStruct((S, D), BF16),
                   jax.ShapeDtypeStruct((S, 1), F32)],
        compiler_params=_cp(("parallel",)),
    )(x, ma, mr, wout, nw)


def _ffn_up(h2, wg, wu):
    tm = 1024

    def body(h_ref, wg_ref, wu_ref, g_ref, u_ref, a_ref):
        h = h_ref[...]
        g = _dot_nt(h, wg_ref[...])
        u = _dot_nt(h, wu_ref[...])
        g_ref[...] = g
        u_ref[...] = u
        a_ref[...] = ((g * _sigmoid(g)) * u).astype(BF16)

    blk = pl.BlockSpec((None, tm, N_FF), lambda p, m: (p, m, 0))
    wblk = pl.BlockSpec((None, N_FF, D), lambda p, m: (p, 0, 0))
    return pl.pallas_call(
        body, name="ffn_up", grid=(NDEV, S // tm),
        in_specs=[pl.BlockSpec((tm, D), lambda p, m: (m, 0)), wblk, wblk],
        out_specs=[blk, blk, blk],
        out_shape=[jax.ShapeDtypeStruct((NDEV, S, N_FF), F32), jax.ShapeDtypeStruct((NDEV, S, N_FF), F32),
                   jax.ShapeDtypeStruct((NDEV, S, N_FF), BF16)],
        compiler_params=_cp(("parallel", "parallel")),
    )(h2, wg, wu)


def _ffn_down_loss(x2, a, wd, nw, tgt):
    tm = 512

    def body(x2_ref, a_ref, w_ref, nw_ref, t_ref, dx_ref, dxb_ref, st_ref, acc_ref):
        m, p = pl.program_id(0), pl.program_id(1)

        @pl.when(p == 0)
        def _():
            acc_ref[...] = jnp.zeros_like(acc_ref)

        @pl.when((p == 0) & (m == 0))
        def _():
            st_ref[...] = jnp.zeros_like(st_ref)

        acc_ref[...] += _dot(a_ref[...], w_ref[...])

        @pl.when(p == NDEV - 1)
        def _():
            x3 = x2_ref[...] + acc_ref[...]
            nwv = nw_ref[...]
            r = lax.rsqrt(jnp.mean(x3 * x3, axis=-1, keepdims=True) + EPS)
            y = (x3 * r) * nwv
            err = y - t_ref[...]
            loss = 0.5 * jnp.sum(jnp.mean(err * err, axis=-1, keepdims=True), axis=0, keepdims=True)
            dy = err * (1.0 / D)
            dx, dnw = _rms_bwd_tile(dy, x3, r, nwv)
            dx_ref[...] = dx
            dxb_ref[...] = dx.astype(BF16)
            st_ref[0:1, :] += dnw
            st_ref[1:2, :] += jnp.broadcast_to(loss, (1, D))

    return pl.pallas_call(
        body, name="ffn_down_loss", grid=(S // tm, NDEV),
        in_specs=[pl.BlockSpec((tm, D), lambda m, p: (m, 0)),
                  pl.BlockSpec((None, tm, N_FF), lambda m, p: (p, m, 0)),
                  pl.BlockSpec((None, N_FF, D), lambda m, p: (p, 0, 0)),
                  pl.BlockSpec((1, D), lambda m, p: (0, 0)),
                  pl.BlockSpec((tm, D), lambda m, p: (m, 0))],
        out_specs=[pl.BlockSpec((tm, D), lambda m, p: (m, 0)), pl.BlockSpec((tm, D), lambda m, p: (m, 0)),
                   pl.BlockSpec((8, D), lambda m, p: (0, 0))],
        out_shape=[jax.ShapeDtypeStruct((S, D), F32), jax.ShapeDtypeStruct((S, D), BF16),
                   jax.ShapeDtypeStruct((8, D), F32)],
        scratch_shapes=[pltpu.VMEM((tm, D), F32)],
        compiler_params=_cp(("arbitrary", "arbitrary")),
    )(x2, a, wd, nw, tgt)


def _ffn_down_bwd(dx3b, wd, g, u):
    tm = 1024

    def body(dx_ref, w_ref, g_ref, u_ref, dg_ref, du_ref):
        da = _dot_nt(dx_ref[...], w_ref[...])
        gv = g_ref[...]
        sg = _sigmoid(gv)
        silu = gv * sg
        dg_ref[...] = ((da * u_ref[...]) * (sg * (1.0 + gv * (1.0 - sg)))).astype(BF16)
        du_ref[...] = (da * silu).astype(BF16)

    blk = pl.BlockSpec((None, tm, N_FF), lambda p, m: (p, m, 0))
    return pl.pallas_call(
        body, name="ffn_down_bwd", grid=(NDEV, S // tm),
        in_specs=[pl.BlockSpec((tm, D), lambda p, m: (m, 0)),
                  pl.BlockSpec((None, N_FF, D), lambda p, m: (p, 0, 0)), blk, blk],
        out_specs=[blk, blk],
        out_shape=[jax.ShapeDtypeStruct((NDEV, S, N_FF), BF16), jax.ShapeDtypeStruct((NDEV, S, N_FF), BF16)],
        compiler_params=_cp(("parallel", "parallel")),
    )(dx3b, wd, g, u)


def _ffn_up_bwd(dg, du, wg, wu, dres, xs, r, nw):
    tm = 512

    def body(dg_ref, du_ref, wg_ref, wu_ref, dres_ref, x_ref, r_ref, nw_ref, dx_ref, dxb_ref, st_ref, acc_ref):
        m, p = pl.program_id(0), pl.program_id(1)

        @pl.when(p == 0)
        def _():
            acc_ref[...] = jnp.zeros_like(acc_ref)

        @pl.when((p == 0) & (m == 0))
        def _():
            st_ref[...] = jnp.zeros_like(st_ref)

        acc_ref[...] += _dot(dg_ref[...], wg_ref[...]) + _dot(du_ref[...], wu_ref[...])

        @pl.when(p == NDEV - 1)
        def _():
            dx, dnw = _rms_bwd_tile(acc_ref[...], x_ref[...], r_ref[...], nw_ref[...])
            dx = dres_ref[...] + dx
            dx_ref[...] = dx
            dxb_ref[...] = dx.astype(BF16)
            st_ref[0:1, :] += dnw

    blk = pl.BlockSpec((None, tm, N_FF), lambda m, p: (p, m, 0))
    wblk = pl.BlockSpec((None, N_FF, D), lambda m, p: (p, 0, 0))
    row = pl.BlockSpec((tm, D), lambda m, p: (m, 0))
    return pl.pallas_call(
        body, name="ffn_up_bwd", grid=(S // tm, NDEV),
        in_specs=[blk, blk, wblk, wblk, row, row, pl.BlockSpec((tm, 1), lambda m, p: (m, 0)),
                  pl.BlockSpec((1, D), lambda m, p: (0, 0))],
        out_specs=[row, row, pl.BlockSpec((8, D), lambda m, p: (0, 0))],
        out_shape=[jax.ShapeDtypeStruct((S, D), F32), jax.ShapeDtypeStruct((S, D), BF16),
                   jax.ShapeDtypeStruct((8, D), F32)],
        scratch_shapes=[pltpu.VMEM((tm, D), F32)],
        compiler_params=_cp(("arbitrary", "arbitrary")),
    )(dg, du, wg, wu, dres, xs, r, nw)


def _out_proj_bwd(dx2b, wout):
    tm = 256

    def body(dx_ref, w_ref, o_ref):
        o_ref[...] = _dot_nt(dx_ref[...], w_ref[...])

    return pl.pallas_call(
        body, name="out_proj_bwd", grid=(S // tm,),
        in_specs=[pl.BlockSpec((tm, D), lambda i: (i, 0)), pl.BlockSpec((D, D), lambda i: (0, 0))],
        out_specs=pl.BlockSpec((tm, D), lambda i: (i, 0)),
        out_shape=jax.ShapeDtypeStruct((S, D), F32),
        compiler_params=_cp(("parallel",)),
    )(dx2b, wout)


def _in_proj_bwd(dproj, win, dres, xs, r, nw):
    tm = 512

    def body(dp_ref, w_ref, dres_ref, x_ref, r_ref, nw_ref, dx_ref, st_ref, acc_ref):
        m, p = pl.program_id(0), pl.program_id(1)

        @pl.when(p == 0)
        def _():
            acc_ref[...] = jnp.zeros_like(acc_ref)

        @pl.when((p == 0) & (m == 0))
        def _():
            st_ref[...] = jnp.zeros_like(st_ref)

        acc_ref[...] += _dot_nt(dp_ref[...], w_ref[...])

        @pl.when(p == NDEV - 1)
        def _():
            dx, dnw = _rms_bwd_tile(acc_ref[...], x_ref[...], r_ref[...], nw_ref[...])
            dx_ref[...] = dres_ref[...] + dx
            st_ref[0:1, :] += dnw

    row = pl.BlockSpec((tm, D), lambda m, p: (m, 0))
    return pl.pallas_call(
        body, name="in_proj_bwd", grid=(S // tm, NDEV),
        in_specs=[pl.BlockSpec((tm, N_IN), lambda m, p: (m, p)),
                  pl.BlockSpec((None, D, N_IN), lambda m, p: (p, 0, 0)),
                  row, row, pl.BlockSpec((tm, 1), lambda m, p: (m, 0)),
                  pl.BlockSpec((1, D), lambda m, p: (0, 0))],
        out_specs=[row, pl.BlockSpec((8, D), lambda m, p: (0, 0))],
        out_shape=[jax.ShapeDtypeStruct((S, D), F32), jax.ShapeDtypeStruct((8, D), F32)],
        scratch_shapes=[pltpu.VMEM((tm, D), F32)],
        compiler_params=_cp(("arbitrary", "arbitrary")),
    )(dproj, win, dres, xs, r, nw)


def _wgrad_in(h1, dproj):
    def body(a_ref, d_ref, o_ref):
        o_ref[...] = _dot_tn(a_ref[...], d_ref[...]).astype(BF16)

    return pl.pallas_call(
        body, name="wgrad_in", grid=(NDEV,),
        in_specs=[pl.BlockSpec((S, D), lambda p: (0, 0)), pl.BlockSpec((S, N_IN), lambda p: (0, p))],
        out_specs=pl.BlockSpec((None, D, N_IN), lambda p: (p, 0, 0)),
        out_shape=jax.ShapeDtypeStruct((NDEV, D, N_IN), BF16),
        compiler_params=_cp(("parallel",)),
    )(h1, dproj)


def _wgrad_rows(a3, dy, name):
    def body(a_ref, d_ref, o_ref):
        o_ref[...] = _dot_tn(a_ref[...], d_ref[...]).astype(BF16)

    return pl.pallas_call(
        body, name=name, grid=(NDEV,),
        in_specs=[pl.BlockSpec((None, S, N_FF), lambda p: (p, 0, 0)), pl.BlockSpec((S, D), lambda p: (0, 0))],
        out_specs=pl.BlockSpec((None, N_FF, D), lambda p: (p, 0, 0)),
        out_shape=jax.ShapeDtypeStruct((NDEV, N_FF, D), BF16),
        compiler_params=_cp(("parallel",)),
    )(a3, dy)


def _wgrad_out(ma, mr, dx2b):
    half = D // 2
    per = half // N_OUT

    def body(ma_ref, mr_ref, d_ref, o_ref):
        p = pl.program_id(0)

        @pl.when(p < per)
        def _():
            o_ref[...] = _dot_tn(ma_ref[...], d_ref[...]).astype(BF16)

        @pl.when(p >= per)
        def _():
            o_ref[...] = _dot_tn(mr_ref[...], d_ref[...]).astype(BF16)

    return pl.pallas_call(
        body, name="wgrad_out", grid=(NDEV,),
        in_specs=[pl.BlockSpec((S, N_OUT), lambda p: (0, jnp.minimum(p, per - 1))),
                  pl.BlockSpec((S, N_OUT), lambda p: (0, jnp.maximum(p - per, 0))),
                  pl.BlockSpec((S, D), lambda p: (0, 0))],
        out_specs=pl.BlockSpec((None, N_OUT, D), lambda p: (p, 0, 0)),
        out_shape=jax.ShapeDtypeStruct((NDEV, N_OUT, D), BF16),
        compiler_params=_cp(("parallel",)),
    )(ma, mr, dx2b)


def _attn_consts():
    c = np.zeros((AH, 8, AHD), np.float32)
    for h in range(AH):
        c[h, :, :] = 2.0 ** (-(h + 1))
    return jnp.asarray(c)


def _permute_in(dst, src, d, cast=None):
    ln = S // d
    for rr in range(d):
        v = src[pl.ds(rr, ln, stride=d), :] if d > 1 else src[...]
        dst[rr * ln:(rr + 1) * ln, :] = v if cast is None else v.astype(cast)


def _attn_masks():
    qi = lax.broadcasted_iota(jnp.int32, (CH, CH), 0)
    kj = lax.broadcasted_iota(jnp.int32, (CH, CH), 1)
    dist_c = (qi - kj).astype(F32)
    dist_p = (qi - kj + CH).astype(F32)
    return (qi >= kj)[None], (kj >= qi)[None], dist_c[None], dist_p[None]


GB = 8


def _bdot_nt(a, b):
    return lax.dot_general(a, b, (((2,), (2,)), ((0,), (0,))), preferred_element_type=F32)


def _bdot(a, b):
    return lax.dot_general(a, b, (((2,), (1,)), ((0,), (0,))), preferred_element_type=F32)


def _bdot_tn(a, b):
    return lax.dot_general(a, b, (((1,), (1,)), ((0,), (0,))), preferred_element_type=F32)


def _shift_block(dst, src):
    dst[0:CH, :] = jnp.zeros((CH, AHD), dst.dtype)
    dst[CH:S, :] = src[0:S - CH, :]


def _has_prev(g, nb):
    blk = lax.broadcasted_iota(jnp.int32, (GB, 1, 1), 0) + g * GB
    return (blk & (nb - 1)) != 0


def _blocks(ref, g):
    return ref[g * GB * CH:(g + 1) * GB * CH, :].reshape(GB, CH, AHD)


def _attn_fwd(proj):
    scale = 1.0 / math.sqrt(AHD)

    def body(c_ref, q_ref, k_ref, v_ref, o_ref, ob_ref, lse_ref, qd, kd, vd, kps, vps, od, ld, *nat):
        onat, lnat = nat[0:3], nat[3:6]
        slope = c_ref[0:1, :]
        mask_c, mask_p, dist_c, dist_p = _attn_masks()
        for pi, (d, nb) in enumerate(PATTERNS):
            _permute_in(qd, q_ref, d, BF16)
            _permute_in(kd, k_ref, d, BF16)
            _permute_in(vd, v_ref, d, BF16)
            if nb > 1:
                _shift_block(kps, kd)
                _shift_block(vps, vd)
            bias_c = -(slope * float(d)) * dist_c
            bias_p = -(slope * float(d)) * dist_p
            for g in range(NB // GB):
                q3, k3, v3 = _blocks(qd, g), _blocks(kd, g), _blocks(vd, g)
                s_c = jnp.where(mask_c, _bdot_nt(q3, k3) * scale + bias_c, NEG)
                mx = jnp.max(s_c, axis=-1, keepdims=True)
                if nb > 1:
                    kp3, vp3 = _blocks(kps, g), _blocks(vps, g)
                    s_p = jnp.where(jnp.logical_and(mask_p, _has_prev(g, nb)),
                                    _bdot_nt(q3, kp3) * scale + bias_p, NEG)
                    mx = jnp.maximum(mx, jnp.max(s_p, axis=-1, keepdims=True))
                    l = (jnp.sum(jnp.exp(s_c - mx), axis=-1, keepdims=True)
                         + jnp.sum(jnp.exp(s_p - mx), axis=-1, keepdims=True))
                    lse = mx + jnp.log(l)
                    o3 = _bdot(jnp.exp(s_c - lse).astype(BF16), v3) + _bdot(jnp.exp(s_p - lse).astype(BF16), vp3)
                else:
                    l = jnp.sum(jnp.exp(s_c - mx), axis=-1, keepdims=True)
                    lse = mx + jnp.log(l)
                    o3 = _bdot(jnp.exp(s_c - lse).astype(BF16), v3)
                rows = slice(g * GB * CH, (g + 1) * GB * CH)
                od[rows, :] = o3.reshape(GB * CH, AHD)
                ld[rows, :] = jnp.broadcast_to(lse, (GB, CH, AHD)).reshape(GB * CH, AHD)
            ln = S // d
            for rr in range(d):
                if d > 1:
                    onat[pi][pl.ds(rr, ln, stride=d), :] = od[rr * ln:(rr + 1) * ln, :]
                    lnat[pi][pl.ds(rr, ln, stride=d), :] = ld[rr * ln:(rr + 1) * ln, :]
                else:
                    onat[pi][...] = od[...]
                    lnat[pi][...] = ld[...]
        l0, l1, l2 = lnat[0][...], lnat[1][...], lnat[2][...]
        mx = jnp.maximum(jnp.maximum(l0, l1), l2)
        e0, e1, e2 = jnp.exp(l0 - mx), jnp.exp(l1 - mx), jnp.exp(l2 - mx)
        den = e0 + e1 + e2
        out = (e0 / den) * onat[0][...] + (e1 / den) * onat[1][...] + (e2 / den) * onat[2][...]
        o_ref[...] = out
        ob_ref[...] = out.astype(BF16)
        lse_ref[...] = mx + jnp.log(den)

    def col(off):
        return pl.BlockSpec((S, AHD), lambda h: (0, off + h))

    return pl.pallas_call(
        body, name="attn_fwd", grid=(AH,),
        in_specs=[pl.BlockSpec((None, 8, AHD), lambda h: (h, 0, 0)), col(0), col(AH), col(2 * AH)],
        out_specs=[col(0), col(0), col(0)],
        out_shape=[jax.ShapeDtypeStruct((S, AH * AHD), F32), jax.ShapeDtypeStruct((S, AH * AHD), BF16),
                   jax.ShapeDtypeStruct((S, AH * AHD), F32)],
        scratch_shapes=[pltpu.VMEM((S, AHD), BF16) for _ in range(5)]
        + [pltpu.VMEM((S, AHD), F32) for _ in range(8)],
        compiler_params=_cp(("parallel",)),
    )(_attn_consts(), proj, proj, proj)


def _attn_bwd(proj, dmixed, o, lse):
    scale = 1.0 / math.sqrt(AHD)

    def body(c_ref, q_ref, k_ref, v_ref, do_ref, o_ref, lse_ref, dq_ref, dk_ref, dv_ref,
             qd, kd, vd, dod, kps, vps, lsd, dld, dqd, dkd, dvd, delta, aq, ak, av):
        slope = c_ref[0:1, :]
        mask_c, mask_p, dist_c, dist_p = _attn_masks()
        delta[...] = jnp.broadcast_to(jnp.sum(do_ref[...] * o_ref[...], axis=-1, keepdims=True), (S, AHD))
        for pi, (d, nb) in enumerate(PATTERNS):
            _permute_in(qd, q_ref, d, BF16)
            _permute_in(kd, k_ref, d, BF16)
            _permute_in(vd, v_ref, d, BF16)
            _permute_in(dod, do_ref, d, BF16)
            _permute_in(lsd, lse_ref, d)
            _permute_in(dld, delta, d)
            if nb > 1:
                _shift_block(kps, kd)
                _shift_block(vps, vd)
            bias_c = -(slope * float(d)) * dist_c
            bias_p = -(slope * float(d)) * dist_p
            for g in range(NB // GB):
                q3, k3, v3, do3 = _blocks(qd, g), _blocks(kd, g), _blocks(vd, g), _blocks(dod, g)
                ls, dl = _blocks(lsd, g), _blocks(dld, g)
                lo, hi = g * GB * CH, (g + 1) * GB * CH
                p_c = jnp.exp(jnp.where(mask_c, _bdot_nt(q3, k3) * scale + bias_c, NEG) - ls)
                ds_c = ((p_c * (_bdot_nt(do3, v3) - dl)) * scale).astype(BF16)
                dq3 = _bdot(ds_c, k3)
                dkd[lo:hi, :] = _bdot_tn(ds_c, q3).reshape(GB * CH, AHD)
                dvd[lo:hi, :] = _bdot_tn(p_c.astype(BF16), do3).reshape(GB * CH, AHD)
                if nb > 1:
                    kp3, vp3 = _blocks(kps, g), _blocks(vps, g)
                    p_p = jnp.exp(jnp.where(jnp.logical_and(mask_p, _has_prev(g, nb)),
                                            _bdot_nt(q3, kp3) * scale + bias_p, NEG) - ls)
                    ds_p = ((p_p * (_bdot_nt(do3, vp3) - dl)) * scale).astype(BF16)
                    dq3 = dq3 + _bdot(ds_p, kp3)
                    dkp = _bdot_tn(ds_p, q3).reshape(GB * CH, AHD)
                    dvp = _bdot_tn(p_p.astype(BF16), do3).reshape(GB * CH, AHD)
                    if g == 0:
                        dkd[0:hi - CH, :] += dkp[CH:, :]
                        dvd[0:hi - CH, :] += dvp[CH:, :]
                    else:
                        dkd[lo - CH:hi - CH, :] += dkp
                        dvd[lo - CH:hi - CH, :] += dvp
                dqd[lo:hi, :] = dq3.reshape(GB * CH, AHD)
            ln = S // d
            for acc, src in ((aq, dqd), (ak, dkd), (av, dvd)):
                if pi == 0:
                    acc[...] = src[...]
                else:
                    for rr in range(d):
                        acc[pl.ds(rr, ln, stride=d), :] += src[rr * ln:(rr + 1) * ln, :]
        dq_ref[...] = aq[...].astype(BF16)
        dk_ref[...] = ak[...].astype(BF16)
        dv_ref[...] = av[...].astype(BF16)

    def col(off):
        return pl.BlockSpec((S, AHD), lambda h: (0, off + h))

    return pl.pallas_call(
        body, name="attn_bwd", grid=(AH,),
        in_specs=[pl.BlockSpec((None, 8, AHD), lambda h: (h, 0, 0)), col(0), col(AH), col(2 * AH),
                  col(0), col(0), col(0)],
        out_specs=[col(0), col(0), col(0)],
        out_shape=[jax.ShapeDtypeStruct((S, AH * AHD), BF16)] * 3,
        scratch_shapes=[pltpu.VMEM((S, AHD), BF16) for _ in range(6)]
        + [pltpu.VMEM((S, AHD), F32) for _ in range(9)],
        compiler_params=_cp(("parallel",)),
    )(_attn_consts(), proj, proj, proj, dmixed, o, lse)


def _ret_consts():
    c = np.zeros((RH, 8, RHD), np.float32)
    for h in range(RH):
        c[h, :, :] = np.log(np.float32(1.0) - np.float32(2.0 ** (-5.0 - h)))
    return jnp.asarray(c)


def _ret_factors(lg):
    i = lax.broadcasted_iota(jnp.int32, (CH, CH), 0)
    j = lax.broadcasted_iota(jnp.int32, (CH, CH), 1)
    dif = (i - j).astype(F32)
    decay = jnp.where(dif >= 0, jnp.exp(lg[:, 0:CH] * jnp.maximum(dif, 0.0)), 0.0)
    row = lax.broadcasted_iota(jnp.int32, (CH, RHD), 0).astype(F32)
    zeta = jnp.exp(lg * (CH - 1.0 - row))
    xi = jnp.exp(lg * (row + 1.0))
    return decay, zeta, xi, jnp.exp(lg * float(CH))


CBK = 8
RSTEPS = NB // CBK


def _ret_specs(rev):
    off = 3 * AH * AHD // RHD
    rows = CBK * CH

    def ch(n):
        return (RSTEPS - 1 - n) if rev else n

    def col(k):
        return pl.BlockSpec((rows, RHD), lambda h, n: (ch(n), off + k * RH + h))

    own = pl.BlockSpec((rows, RHD), lambda h, n: (ch(n), h))
    state = pl.BlockSpec((None, CBK, RHD, RHD), lambda h, n: (h, ch(n), 0, 0))
    const = pl.BlockSpec((None, 8, RHD), lambda h, n: (h, 0, 0))
    dm = pl.BlockSpec((rows, RHD), lambda h, n: (ch(n), AH * AHD // RHD + h))
    return col, own, state, const, dm


def _chunks(x):
    return x.reshape(CBK, CH, RHD)


def _ret_fwd(proj):
    def body(c_ref, q_ref, k_ref, v_ref, g_ref, ret_ref, mr_ref, st_ref, r_acc):
        n = pl.program_id(1)

        @pl.when(n == 0)
        def _():
            r_acc[...] = jnp.zeros_like(r_acc)

        decay, zeta, xi, gch = _ret_factors(c_ref[0:1, :])
        q3 = _chunks(q_ref[...].astype(BF16))
        kc = _chunks(k_ref[...] * (1.0 / math.sqrt(RHD)))
        k3 = kc.astype(BF16)
        v3 = _chunks(v_ref[...].astype(BF16))
        kv3 = _bdot_tn((kc * zeta[None]).astype(BF16), v3)
        r = r_acc[...]
        for i in range(CBK):
            st_ref[i] = r.astype(BF16)
            r = r * gch + kv3[i]
        r_acc[...] = r
        scores = _bdot_nt(q3, k3) * decay[None]
        ret = (_bdot(scores.astype(BF16), v3) + _bdot(q3, st_ref[...]) * xi[None]).reshape(CBK * CH, RHD)
        ret_ref[...] = ret
        rr = lax.rsqrt(jnp.mean(ret * ret, axis=-1, keepdims=True) + EPS)
        gv = g_ref[...]
        mr_ref[...] = ((gv * _sigmoid(gv)) * (ret * rr)).astype(BF16)

    col, own, state, const, _ = _ret_specs(False)
    return pl.pallas_call(
        body, name="ret_fwd", grid=(RH, RSTEPS),
        in_specs=[const, col(0), col(1), col(2), col(3)],
        out_specs=[own, own, state],
        out_shape=[jax.ShapeDtypeStruct((S, RH * RHD), F32), jax.ShapeDtypeStruct((S, RH * RHD), BF16),
                   jax.ShapeDtypeStruct((RH, NB, RHD, RHD), BF16)],
        scratch_shapes=[pltpu.VMEM((RHD, RHD), F32)],
        compiler_params=_cp(("parallel", "arbitrary")),
    )(_ret_consts(), proj, proj, proj, proj)


def _ret_bwd(proj, ret, states, dmixed):
    def body(c_ref, q_ref, k_ref, v_ref, g_ref, ret_ref, st_ref, dm_ref, dq_ref, dk_ref, dv_ref, dg_ref, g_acc, gs):
        n = pl.program_id(1)

        @pl.when(n == 0)
        def _():
            g_acc[...] = jnp.zeros_like(g_acc)

        decay, zeta, xi, gch = _ret_factors(c_ref[0:1, :])
        ret_v = ret_ref[...]
        rr = lax.rsqrt(jnp.mean(ret_v * ret_v, axis=-1, keepdims=True) + EPS)
        gv = g_ref[...]
        sg = _sigmoid(gv)
        dmix = dm_ref[...]
        dg_ref[...] = ((dmix * (ret_v * rr)) * (sg * (1.0 + gv * (1.0 - sg)))).astype(BF16)
        dretn = dmix * (gv * sg)
        dret = _chunks(rr * dretn - ret_v * ((rr * rr * rr) * jnp.mean(dretn * ret_v, axis=-1, keepdims=True)))

        q3 = _chunks(q_ref[...].astype(BF16))
        kc = _chunks(k_ref[...] * (1.0 / math.sqrt(RHD)))
        k3 = kc.astype(BF16)
        v3 = _chunks(v_ref[...].astype(BF16))
        d3 = dret.astype(BF16)
        dxi = (dret * xi[None]).astype(BF16)
        kz = (kc * zeta[None]).astype(BF16)
        dr3 = _bdot_tn(q3, dxi)
        acc = g_acc[...]
        for i in reversed(range(CBK)):
            gs[i] = acc.astype(BF16)
            acc = dr3[i] + gch * acc
        g_acc[...] = acc
        g3 = gs[...]
        sc = (_bdot_nt(q3, k3) * decay[None]).astype(BF16)
        da = (_bdot_nt(d3, v3) * decay[None]).astype(BF16)
        dq = _bdot(da, k3) + _bdot_nt(dxi, st_ref[...])
        dkc = _bdot_tn(da, q3) + _bdot_nt(v3, g3) * zeta[None]
        dv = _bdot_tn(sc, d3) + _bdot(kz, g3)
        dq_ref[...] = dq.reshape(CBK * CH, RHD).astype(BF16)
        dk_ref[...] = (dkc * (1.0 / math.sqrt(RHD))).reshape(CBK * CH, RHD).astype(BF16)
        dv_ref[...] = dv.reshape(CBK * CH, RHD).astype(BF16)

    col, own, state, const, dm = _ret_specs(True)
    return pl.pallas_call(
        body, name="ret_bwd", grid=(RH, RSTEPS),
        in_specs=[const, col(0), col(1), col(2), col(3), own, state, dm],
        out_specs=[own, own, own, own],
        out_shape=[jax.ShapeDtypeStruct((S, RH * RHD), BF16)] * 4,
        scratch_shapes=[pltpu.VMEM((RHD, RHD), F32), pltpu.VMEM((CBK, RHD, RHD), BF16)],
        compiler_params=_cp(("parallel", "arbitrary")),
    )(_ret_consts(), proj, proj, proj, proj, ret, states, dmixed)


class _NoReduction:
    def start(self, group, grads):
        pass

    def local(self, name, first=()):
        return []

    def landed(self, name):
        return []

    def update(self, name):
        return []


def _local_step(x, tgt, nw1, nw2, nw3, win, wout, wg, wu, wd, red=None):
    red = red or _NoReduction()

    def after(values, first):
        return lax.optimization_barrier((tuple(values), tuple(first)))[0]

    h1, r1 = _rms_fwd(x, nw1)
    proj = _proj(h1, win)
    o, ma, lse = _attn_fwd(proj)
    ret, mr, states = _ret_fwd(proj)
    x2, h2, r2 = _out_proj_rms(x, ma, mr, wout, nw2)
    g, u, a = _ffn_up(h2, wg, wu)
    dx3, dx3b, st3 = _ffn_down_loss(x2, a, wd, nw3, tgt)

    dwd = _wgrad_rows(a, dx3b, "wgrad_down")
    red.start(["w_down"], [dwd])
    (dx3b,) = after([dx3b], [dwd])
    dg, du = _ffn_down_bwd(dx3b, wd, g, u)
    dg, du = after([dg, du], red.local("w_down", first=[dg]))
    dwg = _wgrad_rows(dg, h2, "wgrad_gate")
    dwu = _wgrad_rows(du, h2, "wgrad_up")
    red.start(["w_gate", "w_up"], [dwg, dwu])
    dg, du = after([dg, du], [dwg, dwu])
    dx2, dx2b, st2 = _ffn_up_bwd(dg, du, wg, wu, dx3, x2, r2, nw2)
    (dx2b,) = after([dx2b], red.landed("w_down"))
    dwo = _wgrad_out(ma, mr, dx2b)
    red.start(["w_out"], [dwo])
    (dx2b,) = after([dx2b], [dwo] + red.local("w_gate"))
    dmixed = _out_proj_bwd(dx2b, wout)
    dqa, dka, dva = _attn_bwd(proj, dmixed, o, lse)
    (dmixed,) = after([dmixed], [dqa] + red.local("w_out"))
    dqr, dkr, dvr, dgr = _ret_bwd(proj, ret, states, dmixed)
    dproj = jnp.concatenate([dqa, dka, dva, dqr, dkr, dvr, dgr], axis=1)
    dwi = _wgrad_in(h1, dproj)
    red.start(["w_in"], after([dwi], red.landed("w_gate") + red.landed("w_out")))
    early = red.update("w_down") + red.update("w_gate")
    (dproj,) = after([dproj], red.local("w_in", first=early))
    gx, st1 = _in_proj_bwd(dproj, win, dx2, x, r1, nw1)
    stats = jnp.concatenate([st1[0:1], st2[0:1], st3[0:2], jnp.zeros((4, D), F32)], axis=0)
    return stats, gx, dwi, dwo, dwg, dwu, dwd


def _place():
    x, y, c = lax.axis_index("x"), lax.axis_index("y"), lax.axis_index("c")
    return x, y, c, [(1 - x, y), (x, 1 - y), (1 - x, 1 - y)]


def _handshake(peers):
    barrier = pltpu.get_barrier_semaphore()
    for peer in peers:
        pl.semaphore_signal(barrier, inc=1, device_id=peer, device_id_type=MESH)
    pl.semaphore_wait(barrier, len(peers))


def _all_gather(shards, name, collective_id):
    na = len(shards)
    SIB, XN0, XN1, YN1, YN0, VIA_X, VIA_Y = 0, 1, 2, 3, 4, 5, 6
    D2D = {XN0: 7, XN1: 8, YN1: 9, YN0: 10, VIA_X: 11, VIA_Y: 12}

    def body(*refs):
        ins, outs = refs[:na], refs[na:2 * na]
        send_sems, recv_sems, local_sems = refs[2 * na:]
        x, y, c, _ = _place()
        me, sib = (x, y, c), (x, y, 1 - c)
        xn, yn, dg = (1 - x, y, c), (x, 1 - y, c), (1 - x, 1 - y, c)
        _handshake([sib, xn, yn])

        def part(ref, h):
            rows = ref.shape[0] // 2
            return ref if h is None else ref.at[pl.ds(h * rows, rows)]

        def block(a, owner, h):
            return part(outs[a].at[4 * owner[0] + 2 * owner[1] + owner[2]], h)

        def copy(a, k, owner, h, to, own_src=False):
            return pltpu.make_async_remote_copy(
                src_ref=part(ins[a], h) if own_src else block(a, owner, h), dst_ref=block(a, owner, h),
                send_sem=send_sems.at[a, k], recv_sem=recv_sems.at[a, k], device_id=to, device_id_type=MESH)

        def other(p):
            return (p[0], p[1], 1 - c)

        mine = [pltpu.make_async_copy(ins[a], block(a, me, None), local_sems.at[a]) for a in range(na)]
        for cp in mine:
            cp.start()
        sent = []
        for a in range(na):
            sent += [copy(a, XN0, me, 0, xn, True), copy(a, YN1, me, 1, yn, True),
                     copy(a, XN1, me, 1, xn, True), copy(a, YN0, me, 0, yn, True)]
        sent += [copy(a, SIB, me, None, sib, True) for a in range(na)]
        for cp in sent:
            cp.start()

        def landed(a, k, owner, h, then):
            copy(a, k, owner, h, me).wait_recv()
            for k2, to in then + [(D2D[k], sib)]:
                cp = copy(a, k2, owner, h, to)
                cp.start()
                sent.append(cp)

        for a in range(na):
            landed(a, XN0, xn, 0, [(VIA_Y, yn)])
            landed(a, YN1, yn, 1, [(VIA_X, xn)])
            landed(a, XN1, xn, 1, [])
            landed(a, YN0, yn, 0, [])
        for a in range(na):
            landed(a, VIA_Y, dg, 0, [])
            landed(a, VIA_X, dg, 1, [])
        for a in range(na):
            copy(a, SIB, sib, None, me).wait_recv()
            for k, owner, h in ((XN0, xn, 0), (XN1, xn, 1), (YN1, yn, 1), (YN0, yn, 0), (VIA_Y, dg, 0), (VIA_X, dg, 1)):
                copy(a, D2D[k], other(owner), h, me).wait_recv()
        for cp in sent:
            cp.wait_send()
        for cp in mine:
            cp.wait()

    return _sequencer_call(
        body, name, collective_id,
        [jax.ShapeDtypeStruct((NDEV,) + s.shape, s.dtype) for s in shards],
        [pltpu.SemaphoreType.DMA((na, 13)), pltpu.SemaphoreType.DMA((na, 13)), pltpu.SemaphoreType.DMA((na,))])(*shards)


def _sequencer_call(body, name, collective_id, out_type, scratch_types):
    return pl.kernel(
        body, name=name, out_type=out_type,
        mesh=plsc.ScalarSubcoreMesh(axis_name="sequencer", num_cores=1),
        scratch_types=scratch_types,
        compiler_params=pltpu.CompilerParams(collective_id=collective_id))


def _exchange_sibling(grads, name, collective_id):
    na = len(grads)

    def body(*refs):
        ins, outs = refs[:na], refs[na:2 * na]
        send_sems, recv_sems = refs[2 * na:]
        x, y, c, _ = _place()
        _handshake([(x, y, 1 - c)])
        cps = []
        for a in range(na):
            for k in range(4):
                cps.append(pltpu.make_async_remote_copy(
                    src_ref=ins[a].at[2 * k + (1 - c)], dst_ref=outs[a].at[k],
                    send_sem=send_sems.at[a, k], recv_sem=recv_sems.at[a, k],
                    device_id=(x, y, 1 - c), device_id_type=MESH))
        for cp in cps:
            cp.start()
        for cp in cps:
            cp.wait()

    return _sequencer_call(
        body, name, collective_id,
        [jax.ShapeDtypeStruct((4,) + g.shape[1:], g.dtype) for g in grads],
        [pltpu.SemaphoreType.DMA((na, 4)), pltpu.SemaphoreType.DMA((na, 4))])(*grads)


def _row_tile(rows, cols):
    for t in (512, 256, 176, 128, 64, 32, 16):
        if rows % t == 0 and t * cols * 4 <= (1 << 20):
            return t
    raise ValueError((rows, cols))


def _chip_sum(place, g, got, name):
    _, r, c = g.shape
    tm = r

    def body(pos_ref, g_ref, got_ref, o_ref):
        o_ref[...] = (g_ref[...].astype(F32) + got_ref[...].astype(F32)).astype(BF16)

    def chip(j, pos):
        return 2 * (pos[0] ^ jnp.where(j == 1, 0, 1)) + (pos[1] ^ jnp.where(j == 0, 0, 1))

    return pl.pallas_call(
        body, name=name,
        grid_spec=pltpu.PrefetchScalarGridSpec(
            num_scalar_prefetch=1, grid=(3, r // tm),
            in_specs=[pl.BlockSpec((None, tm, c), lambda j, i, pos: (2 * chip(j, pos) + pos[2], i, 0)),
                      pl.BlockSpec((None, tm, c), lambda j, i, pos: (chip(j, pos), i, 0))],
            out_specs=pl.BlockSpec((None, tm, c), lambda j, i, pos: (j, i, 0))),
        out_shape=jax.ShapeDtypeStruct((3, r, c), BF16),
        compiler_params=_cp(("parallel", "parallel")),
    )(place, g, got)


def _exchange_chips(sums, name, collective_id):
    na = len(sums)

    def body(*refs):
        ins, outs = refs[:na], refs[na:2 * na]
        send_sems, recv_sems = refs[2 * na:]
        x, y, c, chips = _place()
        _handshake([(*chip, c) for chip in chips])
        cps = []
        for a in range(na):
            for j, chip in enumerate(chips):
                cps.append(pltpu.make_async_remote_copy(
                    src_ref=ins[a].at[j], dst_ref=outs[a].at[j],
                    send_sem=send_sems.at[a, j], recv_sem=recv_sems.at[a, j],
                    device_id=(*chip, c), device_id_type=MESH))
        for cp in cps:
            cp.start()
        for cp in cps:
            cp.wait()

    return _sequencer_call(
        body, name, collective_id,
        [jax.ShapeDtypeStruct((3,) + s.shape[1:], s.dtype) for s in sums],
        [pltpu.SemaphoreType.DMA((na, 3)), pltpu.SemaphoreType.DMA((na, 3))])(*sums)


def _exchange_stats(stats, collective_id):
    def body(st_in, st_out, st_send, st_recv, local_sem):
        x, y, c, _ = _place()
        me_idx = 4 * x + 2 * y + c
        peers = [(x ^ ((k >> 2) & 1), y ^ ((k >> 1) & 1), c ^ (k & 1)) for k in range(1, 8)]
        _handshake(peers)
        mine = pltpu.make_async_copy(st_in, st_out.at[me_idx], local_sem)
        mine.start()
        cps = [pltpu.make_async_remote_copy(
            src_ref=st_in, dst_ref=st_out.at[me_idx], send_sem=st_send.at[k], recv_sem=st_recv.at[k],
            device_id=peer, device_id_type=MESH) for k, peer in enumerate(peers)]
        for cp in cps:
            cp.start()
        for cp in cps:
            cp.wait()
        mine.wait()

    return _sequencer_call(
        body, "exchange_stats", collective_id,
        jax.ShapeDtypeStruct((NDEV,) + stats.shape, stats.dtype),
        [pltpu.SemaphoreType.DMA((7,)), pltpu.SemaphoreType.DMA((7,)), pltpu.SemaphoreType.DMA])(stats)


class _Reduction:
    def __init__(self, place, first_collective_id, state):
        self.place = place
        self.ids = iter(range(first_collective_id, 32))
        self.state = state
        self.groups = {}
        self.updates = {}

    def next_id(self):
        return next(self.ids)

    def start(self, group, grads):
        got = _exchange_sibling(grads, "sibling_exchange_" + group[0], self.next_id())
        self.groups[group[0]] = dict(names=group, grads=grads, got=got)

    def local(self, name, first=()):
        grp = self.groups[name]
        grads, got = lax.optimization_barrier((tuple(grp["grads"]), tuple(grp["got"]), tuple(first)))[:2]
        grp["sums"] = [_chip_sum(self.place, g, s, "chip_sum_" + n) for g, s, n in zip(grads, got, grp["names"])]
        grp["chips"] = _exchange_chips(grp["sums"], "chip_exchange_" + name, self.next_id())
        return grp["sums"]

    def landed(self, name):
        return list(self.groups[name]["chips"])

    def update(self, name):
        if name not in self.updates:
            grp = next(g for g in self.groups.values() if name in g["names"])
            k = grp["names"].index(name)
            self.updates[name] = _shard_update(self.place, *self.state[name], grp["grads"][k], grp["got"][k],
                                               grp["chips"][k], "update_" + name)
        return list(self.updates[name])


def _adamw(w, g, m, v):
    m = ADAM_B1 * m + (1.0 - ADAM_B1) * g
    v = ADAM_B2 * v + (1.0 - ADAM_B2) * (g * g)
    m_hat = m / (1.0 - ADAM_B1 ** ADAM_STEP)
    v_hat = v / (1.0 - ADAM_B2 ** ADAM_STEP)
    delta = -ADAM_LR * (m_hat / (jnp.sqrt(v_hat) + ADAM_EPS) + ADAM_WD * w)
    return delta, m, v


def _shard_update(place, w, m, v, g, got_sib, got_chips, name):
    r, c = w.shape
    tm = _row_tile(r, c)

    def body(pos_ref, w_ref, m_ref, v_ref, g_ref, s_ref, c_ref, go_ref, d_ref, mo_ref, vo_ref):
        grad = g_ref[...].astype(F32) + s_ref[...].astype(F32)
        for j in range(3):
            grad = grad + c_ref[j].astype(F32)
        delta, mn, vn = _adamw(w_ref[...], grad, m_ref[...], v_ref[...])
        go_ref[...] = grad
        d_ref[...] = delta
        mo_ref[...] = mn
        vo_ref[...] = vn

    row = pl.BlockSpec((tm, c), lambda i, pos: (i, 0))
    return pl.pallas_call(
        body, name=name,
        grid_spec=pltpu.PrefetchScalarGridSpec(
            num_scalar_prefetch=1, grid=(r // tm,),
            in_specs=[row, row, row,
                      pl.BlockSpec((None, tm, c), lambda i, pos: (4 * pos[0] + 2 * pos[1] + pos[2], i, 0)),
                      pl.BlockSpec((None, tm, c), lambda i, pos: (2 * pos[0] + pos[1], i, 0)),
                      pl.BlockSpec((3, tm, c), lambda i, pos: (0, i, 0))],
            out_specs=[row, row, row, row]),
        out_shape=[jax.ShapeDtypeStruct((r, c), F32)] * 4,
        compiler_params=_cp(("parallel",)),
    )(place, w, m, v, g, got_sib, got_chips)


def _small_update(stats_all, ws, ms, vs):
    def body(st_ref, w_ref, m_ref, v_ref, go_ref, d_ref, mo_ref, vo_ref):
        grad = st_ref[0]
        for k in range(1, NDEV):
            grad = grad + st_ref[k]
        delta, mn, vn = _adamw(w_ref[...], grad, m_ref[...], v_ref[...])
        go_ref[...] = grad
        d_ref[...] = delta
        mo_ref[...] = mn
        vo_ref[...] = vn

    return pl.pallas_call(
        body, name="small_update",
        out_shape=[jax.ShapeDtypeStruct((8, D), F32)] * 4,
        compiler_params=_cp(),
    )(stats_all, ws, ms, vs)


def kernel(x, norm_mix_w, w_in, w_out, norm_ffn_w, w_gate, w_up, w_down, norm_final_w, loss_target, m_norm_mix_w, m_w_in, m_w_out, m_norm_ffn_w, m_w_gate, m_w_up, m_w_down, m_norm_final_w, v_norm_mix_w, v_w_in, v_w_out, v_norm_ffn_w, v_w_gate, v_w_up, v_w_down, v_norm_final_w):
    tr = {"w_gate", "w_up"}
    names = ["w_in", "w_out", "w_gate", "w_up", "w_down"]

    def view(a, n):
        return a[0].T if n in tr else a[0]

    big_w = [view(a, n) for a, n in zip([w_in, w_out, w_gate, w_up, w_down], names)]
    big_m = [view(a, n) for a, n in zip([m_w_in, m_w_out, m_w_gate, m_w_up, m_w_down], names)]
    big_v = [view(a, n) for a, n in zip([v_w_in, v_w_out, v_w_gate, v_w_up, v_w_down], names)]

    shards = [_cast_bf16(w, "cast_" + n) for w, n in zip(big_w, names)]
    (win,) = _all_gather(shards[0:1], "all_gather_w_in", 1)
    wout, wg, wu = _all_gather(shards[1:4], "all_gather_out_gate_up", 2)
    (wd,) = _all_gather(shards[4:5], "all_gather_w_down", 3)
    nw3 = norm_final_w.reshape(1, D)
    place = jnp.stack([lax.axis_index("x"), lax.axis_index("y"), lax.axis_index("c")]).astype(jnp.int32)
    red = _Reduction(place, 4, {n: (w, m, v) for n, w, m, v in zip(names, big_w, big_m, big_v)})
    stats, gx, *_ = _local_step(
        x[0], loss_target[0], norm_mix_w, norm_ffn_w, nw3, win, wout.reshape(D, D), wg, wu, wd, red)
    stats_all = _exchange_stats(stats, red.next_id())
    upd = [red.update(n) for n in names]
    stats_all = lax.optimization_barrier((stats_all, tuple(upd[0])))[0]

    def rows(a, b, c):
        return jnp.concatenate([a.reshape(1, D), b.reshape(1, D), c.reshape(1, D), jnp.zeros((5, D), F32)], axis=0)

    sg, sd, sm, sv = _small_update(stats_all, rows(norm_mix_w, norm_ffn_w, norm_final_w),
                                   rows(m_norm_mix_w, m_norm_ffn_w, m_norm_final_w),
                                   rows(v_norm_mix_w, v_norm_ffn_w, v_norm_final_w))
    loss = sg[3, 0]

    def outs(k, small):
        big = [(u[k].T if n in tr else u[k])[None] for u, n in zip(upd, names)]
        return [small[0:1], big[0], big[1], small[1:2], big[2], big[3], big[4], small[2]]

    return (loss, gx[None], *outs(0, sg), *outs(1, sd), *outs(2, sm), *outs(3, sv))
```

```python
import functools
import math

import numpy as np
import jax
import jax.numpy as jnp
from jax import lax
from jax.experimental import pallas as pl
from jax.experimental.pallas import tpu as pltpu
from jax.experimental.pallas import tpu_sc as plsc

F32 = jnp.float32
BF16 = jnp.bfloat16

S = 2048
D = 2048
NDEV = 8
N_IN = 7168 // NDEV
N_FF = 5632 // NDEV
N_OUT = 2048 // NDEV
AH, AHD = 8, 128
RH, RHD = 4, 256
CH = 128
NB = S // CH
EPS = 1e-6
PATTERNS = ((1, 16), (4, 4), (16, 1))
NEG = -1e30
VMEM_LIMIT = 56 * 1024 * 1024

ADAM_LR, ADAM_B1, ADAM_B2, ADAM_EPS, ADAM_WD, ADAM_STEP = 0.001, 0.9, 0.999, 1e-08, 0.01, 10
MESH = pl.DeviceIdType.MESH


def _cp(sem=None):
    return pltpu.CompilerParams(dimension_semantics=sem, vmem_limit_bytes=VMEM_LIMIT)


def _dot(a, b):
    return jnp.dot(a, b, preferred_element_type=F32)


def _dot_nt(a, b):
    return lax.dot_general(a, b, (((1,), (1,)), ((), ())), preferred_element_type=F32)


def _dot_tn(a, b):
    return lax.dot_general(a, b, (((0,), (0,)), ((), ())), preferred_element_type=F32)


def _sigmoid(x):
    return 0.5 * jnp.tanh(0.5 * x) + 0.5


def _cast_bf16(w, name):
    r, c = w.shape
    tm = r if r <= 1024 else 512

    def body(w_ref, o_ref):
        o_ref[...] = w_ref[...].astype(BF16)

    return pl.pallas_call(
        body, name=name, grid=(r // tm,),
        in_specs=[pl.BlockSpec((tm, c), lambda i: (i, 0))],
        out_specs=pl.BlockSpec((tm, c), lambda i: (i, 0)),
        out_shape=jax.ShapeDtypeStruct((r, c), BF16),
        compiler_params=_cp(("parallel",)),
    )(w)


def _rms_fwd(x, nw):
    tm = 256

    def body(x_ref, w_ref, h_ref, r_ref):
        xs = x_ref[...]
        r = lax.rsqrt(jnp.mean(xs * xs, axis=-1, keepdims=True) + EPS)
        h_ref[...] = ((xs * r) * w_ref[...]).astype(BF16)
        r_ref[...] = r

    return pl.pallas_call(
        body, name="rms_fwd", grid=(S // tm,),
        in_specs=[pl.BlockSpec((tm, D), lambda i: (i, 0)), pl.BlockSpec((1, D), lambda i: (0, 0))],
        out_specs=[pl.BlockSpec((tm, D), lambda i: (i, 0)), pl.BlockSpec((tm, 1), lambda i: (i, 0))],
        out_shape=[jax.ShapeDtypeStruct((S, D), BF16), jax.ShapeDtypeStruct((S, 1), F32)],
        compiler_params=_cp(("parallel",)),
    )(x, nw)


def _rms_bwd_tile(dh, xs, r, nw):
    dnw = jnp.sum(dh * (xs * r), axis=0, keepdims=True)
    gy = dh * nw
    dx = r * gy - xs * ((r * r * r) * jnp.mean(gy * xs, axis=-1, keepdims=True))
    return dx, dnw


def _proj(h1, win):
    tm = 1024

    def body(a_ref, w_ref, o_ref):
        o_ref[...] = _dot(a_ref[...], w_ref[...])

    return pl.pallas_call(
        body, name="proj", grid=(NDEV, S // tm),
        in_specs=[pl.BlockSpec((tm, D), lambda p, m: (m, 0)),
                  pl.BlockSpec((None, D, N_IN), lambda p, m: (p, 0, 0))],
        out_specs=pl.BlockSpec((tm, N_IN), lambda p, m: (m, p)),
        out_shape=jax.ShapeDtypeStruct((S, NDEV * N_IN), F32),
        compiler_params=_cp(("parallel", "parallel")),
    )(h1, win)


def _out_proj_rms(x, ma, mr, wout, nw):
    tm = 256
    half = D // 2

    def body(x_ref, ma_ref, mr_ref, w_ref, nw_ref, x2_ref, h_ref, r_ref):
        acc = _dot(ma_ref[...], w_ref[0:half, :]) + _dot(mr_ref[...], w_ref[half:D, :])
        x2 = x_ref[...] + acc
        r = lax.rsqrt(jnp.mean(x2 * x2, axis=-1, keepdims=True) + EPS)
        x2_ref[...] = x2
        h_ref[...] = ((x2 * r) * nw_ref[...]).astype(BF16)
        r_ref[...] = r

    return pl.pallas_call(
        body, name="out_proj_rms", grid=(S // tm,),
        in_specs=[pl.BlockSpec((tm, D), lambda i: (i, 0)),
                  pl.BlockSpec((tm, half), lambda i: (i, 0)),
                  pl.BlockSpec((tm, half), lambda i: (i, 0)),
                  pl.BlockSpec((D, D), lambda i: (0, 0)),
                  pl.BlockSpec((1, D), lambda i: (0, 0))],
        out_specs=[pl.BlockSpec((tm, D), lambda i: (i, 0)), pl.BlockSpec((tm, D), lambda i: (i, 0)),
                   pl.BlockSpec((tm, 1), lambda i: (i, 0))],
        out_shape=[jax.ShapeDtypeStruct((S, D), F32), jax.ShapeDtypeStruct((S, D), BF16),
                   jax.ShapeDtypeStruct((S, 1), F32)],
        compiler_params=_cp(("parallel",)),
    )(x, ma, mr, wout, nw)


def _ffn_up(h2, wg, wu):
    tm = 1024

    def body(h_ref, wg_ref, wu_ref, g_ref, u_ref, a_ref):
        h = h_ref[...]
        g = _dot_nt(h, wg_ref[...])
        u = _dot_nt(h, wu_ref[...])
        g_ref[...] = g
        u_ref[...] = u
        a_ref[...] = ((g * _sigmoid(g)) * u).astype(BF16)

    blk = pl.BlockSpec((None, tm, N_FF), lambda p, m: (p, m, 0))
    wblk = pl.BlockSpec((None, N_FF, D), lambda p, m: (p, 0, 0))
    return pl.pallas_call(
        body, name="ffn_up", grid=(NDEV, S // tm),
        in_specs=[pl.BlockSpec((tm, D), lambda p, m: (m, 0)), wblk, wblk],
        out_specs=[blk, blk, blk],
        out_shape=[jax.ShapeDtypeStruct((NDEV, S, N_FF), F32), jax.ShapeDtypeStruct((NDEV, S, N_FF), F32),
                   jax.ShapeDtypeStruct((NDEV, S, N_FF), BF16)],
        compiler_params=_cp(("parallel", "parallel")),
    )(h2, wg, wu)


def _ffn_down_loss(x2, a, wd, nw, tgt):
    tm = 512

    def body(x2_ref, a_ref, w_ref, nw_ref, t_ref, dx_ref, dxb_ref, st_ref, acc_ref):
        m, p = pl.program_id(0), pl.program_id(1)

        @pl.when(p == 0)
        def _():
            acc_ref[...] = jnp.zeros_like(acc_ref)

        @pl.when((p == 0) & (m == 0))
        def _():
            st_ref[...] = jnp.zeros_like(st_ref)

        acc_ref[...] += _dot(a_ref[...], w_ref[...])

        @pl.when(p == NDEV - 1)
        def _():
            x3 = x2_ref[...] + acc_ref[...]
            nwv = nw_ref[...]
            r = lax.rsqrt(jnp.mean(x3 * x3, axis=-1, keepdims=True) + EPS)
            y = (x3 * r) * nwv
            err = y - t_ref[...]
            loss = 0.5 * jnp.sum(jnp.mean(err * err, axis=-1, keepdims=True), axis=0, keepdims=True)
            dy = err * (1.0 / D)
            dx, dnw = _rms_bwd_tile(dy, x3, r, nwv)
            dx_ref[...] = dx
            dxb_ref[...] = dx.astype(BF16)
            st_ref[0:1, :] += dnw
            st_ref[1:2, :] += jnp.broadcast_to(loss, (1, D))

    return pl.pallas_call(
        body, name="ffn_down_loss", grid=(S // tm, NDEV),
        in_specs=[pl.BlockSpec((tm, D), lambda m, p: (m, 0)),
                  pl.BlockSpec((None, tm, N_FF), lambda m, p: (p, m, 0)),
                  pl.BlockSpec((None, N_FF, D), lambda m, p: (p, 0, 0)),
                  pl.BlockSpec((1, D), lambda m, p: (0, 0)),
                  pl.BlockSpec((tm, D), lambda m, p: (m, 0))],
        out_specs=[pl.BlockSpec((tm, D), lambda m, p: (m, 0)), pl.BlockSpec((tm, D), lambda m, p: (m, 0)),
                   pl.BlockSpec((8, D), lambda m, p: (0, 0))],
        out_shape=[jax.ShapeDtypeStruct((S, D), F32), jax.ShapeDtypeStruct((S, D), BF16),
                   jax.ShapeDtypeStruct((8, D), F32)],
        scratch_shapes=[pltpu.VMEM((tm, D), F32)],
        compiler_params=_cp(("arbitrary", "arbitrary")),
    )(x2, a, wd, nw, tgt)


def _ffn_down_bwd(dx3b, wd, g, u):
    tm = 1024

    def body(dx_ref, w_ref, g_ref, u_ref, dg_ref, du_ref):
        da = _dot_nt(dx_ref[...], w_ref[...])
        gv = g_ref[...]
        sg = _sigmoid(gv)
        silu = gv * sg
        dg_ref[...] = ((da * u_ref[...]) * (sg * (1.0 + gv * (1.0 - sg)))).astype(BF16)
        du_ref[...] = (da * silu).astype(BF16)

    blk = pl.BlockSpec((None, tm, N_FF), lambda p, m: (p, m, 0))
    return pl.pallas_call(
        body, name="ffn_down_bwd", grid=(NDEV, S // tm),
        in_specs=[pl.BlockSpec((tm, D), lambda p, m: (m, 0)),
                  pl.BlockSpec((None, N_FF, D), lambda p, m: (p, 0, 0)), blk, blk],
        out_specs=[blk, blk],
        out_shape=[jax.ShapeDtypeStruct((NDEV, S, N_FF), BF16), jax.ShapeDtypeStruct((NDEV, S, N_FF), BF16)],
        compiler_params=_cp(("parallel", "parallel")),
    )(dx3b, wd, g, u)


def _ffn_up_bwd(dg, du, wg, wu, dres, xs, r, nw):
    tm = 512

    def body(dg_ref, du_ref, wg_ref, wu_ref, dres_ref, x_ref, r_ref, nw_ref, dx_ref, dxb_ref, st_ref, acc_ref):
        m, p = pl.program_id(0), pl.program_id(1)

        @pl.when(p == 0)
        def _():
            acc_ref[...] = jnp.zeros_like(acc_ref)

        @pl.when((p == 0) & (m == 0))
        def _():
            st_ref[...] = jnp.zeros_like(st_ref)

        acc_ref[...] += _dot(dg_ref[...], wg_ref[...]) + _dot(du_ref[...], wu_ref[...])

        @pl.when(p == NDEV - 1)
        def _():
            dx, dnw = _rms_bwd_tile(acc_ref[...], x_ref[...], r_ref[...], nw_ref[...])
            dx = dres_ref[...] + dx
            dx_ref[...] = dx
            dxb_ref[...] = dx.astype(BF16)
            st_ref[0:1, :] += dnw

    blk = pl.BlockSpec((None, tm, N_FF), lambda m, p: (p, m, 0))
    wblk = pl.BlockSpec((None, N_FF, D), lambda m, p: (p, 0, 0))
    row = pl.BlockSpec((tm, D), lambda m, p: (m, 0))
    return pl.pallas_call(
        body, name="ffn_up_bwd", grid=(S // tm, NDEV),
        in_specs=[blk, blk, wblk, wblk, row, row, pl.BlockSpec((tm, 1), lambda m, p: (m, 0)),
                  pl.BlockSpec((1, D), lambda m, p: (0, 0))],
        out_specs=[row, row, pl.BlockSpec((8, D), lambda m, p: (0, 0))],
        out_shape=[jax.ShapeDtypeStruct((S, D), F32), jax.ShapeDtypeStruct((S, D), BF16),
                   jax.ShapeDtypeStruct((8, D), F32)],
        scratch_shapes=[pltpu.VMEM((tm, D), F32)],
        compiler_params=_cp(("arbitrary", "arbitrary")),
    )(dg, du, wg, wu, dres, xs, r, nw)


def _out_proj_bwd(dx2b, wout):
    tm = 256

    def body(dx_ref, w_ref, o_ref):
        o_ref[...] = _dot_nt(dx_ref[...], w_ref[...])

    return pl.pallas_call(
        body, name="out_proj_bwd", grid=(S // tm,),
        in_specs=[pl.BlockSpec((tm, D), lambda i: (i, 0)), pl.BlockSpec((D, D), lambda i: (0, 0))],
        out_specs=pl.BlockSpec((tm, D), lambda i: (i, 0)),
        out_shape=jax.ShapeDtypeStruct((S, D), F32),
        compiler_params=_cp(("parallel",)),
    )(dx2b, wout)


def _in_proj_bwd(dproj, win, dres, xs, r, nw):
    tm = 512

    def body(dp_ref, w_ref, dres_ref, x_ref, r_ref, nw_ref, dx_ref, st_ref, acc_ref):
        m, p = pl.program_id(0), pl.program_id(1)

        @pl.when(p == 0)
        def _():
            acc_ref[...] = jnp.zeros_like(acc_ref)

        @pl.when((p == 0) & (m == 0))
        def _():
            st_ref[...] = jnp.zeros_like(st_ref)

        acc_ref[...] += _dot_nt(dp_ref[...], w_ref[...])

        @pl.when(p == NDEV - 1)
        def _():
            dx, dnw = _rms_bwd_tile(acc_ref[...], x_ref[...], r_ref[...], nw_ref[...])
            dx_ref[...] = dres_ref[...] + dx
            st_ref[0:1, :] += dnw

    row = pl.BlockSpec((tm, D), lambda m, p: (m, 0))
    return pl.pallas_call(
        body, name="in_proj_bwd", grid=(S // tm, NDEV),
        in_specs=[pl.BlockSpec((tm, N_IN), lambda m, p: (m, p)),
                  pl.BlockSpec((None, D, N_IN), lambda m, p: (p, 0, 0)),
                  row, row, pl.BlockSpec((tm, 1), lambda m, p: (m, 0)),
                  pl.BlockSpec((1, D), lambda m, p: (0, 0))],
        out_specs=[row, pl.BlockSpec((8, D), lambda m, p: (0, 0))],
        out_shape=[jax.ShapeDtypeStruct((S, D), F32), jax.ShapeDtypeStruct((8, D), F32)],
        scratch_shapes=[pltpu.VMEM((tm, D), F32)],
        compiler_params=_cp(("arbitrary", "arbitrary")),
    )(dproj, win, dres, xs, r, nw)


def _wgrad_in(h1, dproj):
    def body(a_ref, d_ref, o_ref):
        o_ref[...] = _dot_tn(a_ref[...], d_ref[...]).astype(BF16)

    return pl.pallas_call(
        body, name="wgrad_in", grid=(NDEV,),
        in_specs=[pl.BlockSpec((S, D), lambda p: (0, 0)), pl.BlockSpec((S, N_IN), lambda p: (0, p))],
        out_specs=pl.BlockSpec((None, D, N_IN), lambda p: (p, 0, 0)),
        out_shape=jax.ShapeDtypeStruct((NDEV, D, N_IN), BF16),
        compiler_params=_cp(("parallel",)),
    )(h1, dproj)


def _wgrad_rows(a3, dy, name):
    def body(a_ref, d_ref, o_ref):
        o_ref[...] = _dot_tn(a_ref[...], d_ref[...]).astype(BF16)

    return pl.pallas_call(
        body, name=name, grid=(NDEV,),
        in_specs=[pl.BlockSpec((None, S, N_FF), lambda p: (p, 0, 0)), pl.BlockSpec((S, D), lambda p: (0, 0))],
        out_specs=pl.BlockSpec((None, N_FF, D), lambda p: (p, 0, 0)),
        out_shape=jax.ShapeDtypeStruct((NDEV, N_FF, D), BF16),
        compiler_params=_cp(("parallel",)),
    )(a3, dy)


def _wgrad_out(ma, mr, dx2b):
    half = D // 2
    per = half // N_OUT

    def body(ma_ref, mr_ref, d_ref, o_ref):
        p = pl.program_id(0)

        @pl.when(p < per)
        def _():
            o_ref[...] = _dot_tn(ma_ref[...], d_ref[...]).astype(BF16)

        @pl.when(p >= per)
        def _():
            o_ref[...] = _dot_tn(mr_ref[...], d_ref[...]).astype(BF16)

    return pl.pallas_call(
        body, name="wgrad_out", grid=(NDEV,),
        in_specs=[pl.BlockSpec((S, N_OUT), lambda p: (0, jnp.minimum(p, per - 1))),
                  pl.BlockSpec((S, N_OUT), lambda p: (0, jnp.maximum(p - per, 0))),
                  pl.BlockSpec((S, D), lambda p: (0, 0))],
        out_specs=pl.BlockSpec((None, N_OUT, D), lambda p: (p, 0, 0)),
        out_shape=jax.ShapeDtypeStruct((NDEV, N_OUT, D), BF16),
        compiler_params=_cp(("parallel",)),
    )(ma, mr, dx2b)


def _attn_consts():
    c = np.zeros((AH, 8, AHD), np.float32)
    for h in range(AH):
        c[h, :, :] = 2.0 ** (-(h + 1))
    return jnp.asarray(c)


def _permute_in(dst, src, d, cast=None):
    ln = S // d
    for rr in range(d):
        v = src[pl.ds(rr, ln, stride=d), :] if d > 1 else src[...]
        dst[rr * ln:(rr + 1) * ln, :] = v if cast is None else v.astype(cast)


def _attn_masks():
    qi = lax.broadcasted_iota(jnp.int32, (CH, CH), 0)
    kj = lax.broadcasted_iota(jnp.int32, (CH, CH), 1)
    dist_c = (qi - kj).astype(F32)
    dist_p = (qi - kj + CH).astype(F32)
    return (qi >= kj)[None], (kj >= qi)[None], dist_c[None], dist_p[None]


GB = 8


def _bdot_nt(a, b):
    return lax.dot_general(a, b, (((2,), (2,)), ((0,), (0,))), preferred_element_type=F32)


def _bdot(a, b):
    return lax.dot_general(a, b, (((2,), (1,)), ((0,), (0,))), preferred_element_type=F32)


def _bdot_tn(a, b):
    return lax.dot_general(a, b, (((1,), (1,)), ((0,), (0,))), preferred_element_type=F32)


def _shift_block(dst, src):
    dst[0:CH, :] = jnp.zeros((CH, AHD), dst.dtype)
    dst[CH:S, :] = src[0:S - CH, :]


def _has_prev(g, nb):
    blk = lax.broadcasted_iota(jnp.int32, (GB, 1, 1), 0) + g * GB
    return (blk & (nb - 1)) != 0


def _blocks(ref, g):
    return ref[g * GB * CH:(g + 1) * GB * CH, :].reshape(GB, CH, AHD)


def _attn_fwd(proj):
    scale = 1.0 / math.sqrt(AHD)

    def body(c_ref, q_ref, k_ref, v_ref, o_ref, ob_ref, lse_ref, qd, kd, vd, kps, vps, od, ld, *nat):
        onat, lnat = nat[0:3], nat[3:6]
        slope = c_ref[0:1, :]
        mask_c, mask_p, dist_c, dist_p = _attn_masks()
        for pi, (d, nb) in enumerate(PATTERNS):
            _permute_in(qd, q_ref, d, BF16)
            _permute_in(kd, k_ref, d, BF16)
            _permute_in(vd, v_ref, d, BF16)
            if nb > 1:
                _shift_block(kps, kd)
                _shift_block(vps, vd)
            bias_c = -(slope * float(d)) * dist_c
            bias_p = -(slope * float(d)) * dist_p
            for g in range(NB // GB):
                q3, k3, v3 = _blocks(qd, g), _blocks(kd, g), _blocks(vd, g)
                s_c = jnp.where(mask_c, _bdot_nt(q3, k3) * scale + bias_c, NEG)
                mx = jnp.max(s_c, axis=-1, keepdims=True)
                if nb > 1:
                    kp3, vp3 = _blocks(kps, g), _blocks(vps, g)
                    s_p = jnp.where(jnp.logical_and(mask_p, _has_prev(g, nb)),
                                    _bdot_nt(q3, kp3) * scale + bias_p, NEG)
                    mx = jnp.maximum(mx, jnp.max(s_p, axis=-1, keepdims=True))
                    l = (jnp.sum(jnp.exp(s_c - mx), axis=-1, keepdims=True)
                         + jnp.sum(jnp.exp(s_p - mx), axis=-1, keepdims=True))
                    lse = mx + jnp.log(l)
                    o3 = _bdot(jnp.exp(s_c - lse).astype(BF16), v3) + _bdot(jnp.exp(s_p - lse).astype(BF16), vp3)
                else:
                    l = jnp.sum(jnp.exp(s_c - mx), axis=-1, keepdims=True)
                    lse = mx + jnp.log(l)
                    o3 = _bdot(jnp.exp(s_c - lse).astype(BF16), v3)
                rows = slice(g * GB * CH, (g + 1) * GB * CH)
                od[rows, :] = o3.reshape(GB * CH, AHD)
                ld[rows, :] = jnp.broadcast_to(lse, (GB, CH, AHD)).reshape(GB * CH, AHD)
            ln = S // d
            for rr in range(d):
                if d > 1:
                    onat[pi][pl.ds(rr, ln, stride=d), :] = od[rr * ln:(rr + 1) * ln, :]
                    lnat[pi][pl.ds(rr, ln, stride=d), :] = ld[rr * ln:(rr + 1) * ln, :]
                else:
                    onat[pi][...] = od[...]
                    lnat[pi][...] = ld[...]
        l0, l1, l2 = lnat[0][...], lnat[1][...], lnat[2][...]
        mx = jnp.maximum(jnp.maximum(l0, l1), l2)
        e0, e1, e2 = jnp.exp(l0 - mx), jnp.exp(l1 - mx), jnp.exp(l2 - mx)
        den = e0 + e1 + e2
        out = (e0 / den) * onat[0][...] + (e1 / den) * onat[1][...] + (e2 / den) * onat[2][...]
        o_ref[...] = out
        ob_ref[...] = out.astype(BF16)
        lse_ref[...] = mx + jnp.log(den)

    def col(off):
        return pl.BlockSpec((S, AHD), lambda h: (0, off + h))

    return pl.pallas_call(
        body, name="attn_fwd", grid=(AH,),
        in_specs=[pl.BlockSpec((None, 8, AHD), lambda h: (h, 0, 0)), col(0), col(AH), col(2 * AH)],
        out_specs=[col(0), col(0), col(0)],
        out_shape=[jax.ShapeDtypeStruct((S, AH * AHD), F32), jax.ShapeDtypeStruct((S, AH * AHD), BF16),
                   jax.ShapeDtypeStruct((S, AH * AHD), F32)],
        scratch_shapes=[pltpu.VMEM((S, AHD), BF16) for _ in range(5)]
        + [pltpu.VMEM((S, AHD), F32) for _ in range(8)],
        compiler_params=_cp(("parallel",)),
    )(_attn_consts(), proj, proj, proj)


def _attn_bwd(proj, dmixed, o, lse):
    scale = 1.0 / math.sqrt(AHD)

    def body(c_ref, q_ref, k_ref, v_ref, do_ref, o_ref, lse_ref, dq_ref, dk_ref, dv_ref,
             qd, kd, vd, dod, kps, vps, lsd, dld, dqd, dkd, dvd, delta, aq, ak, av):
        slope = c_ref[0:1, :]
        mask_c, mask_p, dist_c, dist_p = _attn_masks()
        delta[...] = jnp.broadcast_to(jnp.sum(do_ref[...] * o_ref[...], axis=-1, keepdims=True), (S, AHD))
        for pi, (d, nb) in enumerate(PATTERNS):
            _permute_in(qd, q_ref, d, BF16)
            _permute_in(kd, k_ref, d, BF16)
            _permute_in(vd, v_ref, d, BF16)
            _permute_in(dod, do_ref, d, BF16)
            _permute_in(lsd, lse_ref, d)
            _permute_in(dld, delta, d)
            if nb > 1:
                _shift_block(kps, kd)
                _shift_block(vps, vd)
            bias_c = -(slope * float(d)) * dist_c
            bias_p = -(slope * float(d)) * dist_p
            for g in range(NB // GB):
                q3, k3, v3, do3 = _blocks(qd, g), _blocks(kd, g), _blocks(vd, g), _blocks(dod, g)
                ls, dl = _blocks(lsd, g), _blocks(dld, g)
                lo, hi = g * GB * CH, (g + 1) * GB * CH
                p_c = jnp.exp(jnp.where(mask_c, _bdot_nt(q3, k3) * scale + bias_c, NEG) - ls)
                ds_c = ((p_c * (_bdot_nt(do3, v3) - dl)) * scale).astype(BF16)
                dq3 = _bdot(ds_c, k3)
                dkd[lo:hi, :] = _bdot_tn(ds_c, q3).reshape(GB * CH, AHD)
                dvd[lo:hi, :] = _bdot_tn(p_c.astype(BF16), do3).reshape(GB * CH, AHD)
                if nb > 1:
                    kp3, vp3 = _blocks(kps, g), _blocks(vps, g)
                    p_p = jnp.exp(jnp.where(jnp.logical_and(mask_p, _has_prev(g, nb)),
                                            _bdot_nt(q3, kp3) * scale + bias_p, NEG) - ls)
                    ds_p = ((p_p * (_bdot_nt(do3, vp3) - dl)) * scale).astype(BF16)
                    dq3 = dq3 + _bdot(ds_p, kp3)
                    dkp = _bdot_tn(ds_p, q3).reshape(GB * CH, AHD)
                    dvp = _bdot_tn(p_p.astype(BF16), do3).reshape(GB * CH, AHD)
                    if g == 0:
                        dkd[0:hi - CH, :] += dkp[CH:, :]
                        dvd[0:hi - CH, :] += dvp[CH:, :]
                    else:
                        dkd[lo - CH:hi - CH, :] += dkp
                        dvd[lo - CH:hi - CH, :] += dvp
                dqd[lo:hi, :] = dq3.reshape(GB * CH, AHD)
            ln = S // d
            for acc, src in ((aq, dqd), (ak, dkd), (av, dvd)):
                if pi == 0:
                    acc[...] = src[...]
                else:
                    for rr in range(d):
                        acc[pl.ds(rr, ln, stride=d), :] += src[rr * ln:(rr + 1) * ln, :]
        dq_ref[...] = aq[...].astype(BF16)
        dk_ref[...] = ak[...].astype(BF16)
        dv_ref[...] = av[...].astype(BF16)

    def col(off):
        return pl.BlockSpec((S, AHD), lambda h: (0, off + h))

    return pl.pallas_call(
        body, name="attn_bwd", grid=(AH,),
        in_specs=[pl.BlockSpec((None, 8, AHD), lambda h: (h, 0, 0)), col(0), col(AH), col(2 * AH),
                  col(0), col(0), col(0)],
        out_specs=[col(0), col(0), col(0)],
        out_shape=[jax.ShapeDtypeStruct((S, AH * AHD), BF16)] * 3,
        scratch_shapes=[pltpu.VMEM((S, AHD), BF16) for _ in range(6)]
        + [pltpu.VMEM((S, AHD), F32) for _ in range(9)],
        compiler_params=_cp(("parallel",)),
    )(_attn_consts(), proj, proj, proj, dmixed, o, lse)


def _ret_consts():
    c = np.zeros((RH, 8, RHD), np.float32)
    for h in range(RH):
        c[h, :, :] = np.log(np.float32(1.0) - np.float32(2.0 ** (-5.0 - h)))
    return jnp.asarray(c)


def _ret_factors(lg):
    i = lax.broadcasted_iota(jnp.int32, (CH, CH), 0)
    j = lax.broadcasted_iota(jnp.int32, (CH, CH), 1)
    dif = (i - j).astype(F32)
    decay = jnp.where(dif >= 0, jnp.exp(lg[:, 0:CH] * jnp.maximum(dif, 0.0)), 0.0)
    row = lax.broadcasted_iota(jnp.int32, (CH, RHD), 0).astype(F32)
    zeta = jnp.exp(lg * (CH - 1.0 - row))
    xi = jnp.exp(lg * (row + 1.0))
    return decay, zeta, xi, jnp.exp(lg * float(CH))


CBK = 8
RSTEPS = NB // CBK


def _ret_specs(rev):
    off = 3 * AH * AHD // RHD
    rows = CBK * CH

    def ch(n):
        return (RSTEPS - 1 - n) if rev else n

    def col(k):
        return pl.BlockSpec((rows, RHD), lambda h, n: (ch(n), off + k * RH + h))

    own = pl.BlockSpec((rows, RHD), lambda h, n: (ch(n), h))
    state = pl.BlockSpec((None, CBK, RHD, RHD), lambda h, n: (h, ch(n), 0, 0))
    const = pl.BlockSpec((None, 8, RHD), lambda h, n: (h, 0, 0))
    dm = pl.BlockSpec((rows, RHD), lambda h, n: (ch(n), AH * AHD // RHD + h))
    return col, own, state, const, dm


def _chunks(x):
    return x.reshape(CBK, CH, RHD)


def _ret_fwd(proj):
    def body(c_ref, q_ref, k_ref, v_ref, g_ref, ret_ref, mr_ref, st_ref, r_acc):
        n = pl.program_id(1)

        @pl.when(n == 0)
        def _():
            r_acc[...] = jnp.zeros_like(r_acc)

        decay, zeta, xi, gch = _ret_factors(c_ref[0:1, :])
        q3 = _chunks(q_ref[...].astype(BF16))
        kc = _chunks(k_ref[...] * (1.0 / math.sqrt(RHD)))
        k3 = kc.astype(BF16)
        v3 = _chunks(v_ref[...].astype(BF16))
        kv3 = _bdot_tn((kc * zeta[None]).astype(BF16), v3)
        r = r_acc[...]
        for i in range(CBK):
            st_ref[i] = r.astype(BF16)
            r = r * gch + kv3[i]
        r_acc[...] = r
        scores = _bdot_nt(q3, k3) * decay[None]
        ret = (_bdot(scores.astype(BF16), v3) + _bdot(q3, st_ref[...]) * xi[None]).reshape(CBK * CH, RHD)
        ret_ref[...] = ret
        rr = lax.rsqrt(jnp.mean(ret * ret, axis=-1, keepdims=True) + EPS)
        gv = g_ref[...]
        mr_ref[...] = ((gv * _sigmoid(gv)) * (ret * rr)).astype(BF16)

    col, own, state, const, _ = _ret_specs(False)
    return pl.pallas_call(
        body, name="ret_fwd", grid=(RH, RSTEPS),
        in_specs=[const, col(0), col(1), col(2), col(3)],
        out_specs=[own, own, state],
        out_shape=[jax.ShapeDtypeStruct((S, RH * RHD), F32), jax.ShapeDtypeStruct((S, RH * RHD), BF16),
                   jax.ShapeDtypeStruct((RH, NB, RHD, RHD), BF16)],
        scratch_shapes=[pltpu.VMEM((RHD, RHD), F32)],
        compiler_params=_cp(("parallel", "arbitrary")),
    )(_ret_consts(), proj, proj, proj, proj)


def _ret_bwd(proj, ret, states, dmixed):
    def body(c_ref, q_ref, k_ref, v_ref, g_ref, ret_ref, st_ref, dm_ref, dq_ref, dk_ref, dv_ref, dg_ref, g_acc, gs):
        n = pl.program_id(1)

        @pl.when(n == 0)
        def _():
            g_acc[...] = jnp.zeros_like(g_acc)

        decay, zeta, xi, gch = _ret_factors(c_ref[0:1, :])
        ret_v = ret_ref[...]
        rr = lax.rsqrt(jnp.mean(ret_v * ret_v, axis=-1, keepdims=True) + EPS)
        gv = g_ref[...]
        sg = _sigmoid(gv)
        dmix = dm_ref[...]
        dg_ref[...] = ((dmix * (ret_v * rr)) * (sg * (1.0 + gv * (1.0 - sg)))).astype(BF16)
        dretn = dmix * (gv * sg)
        dret = _chunks(rr * dretn - ret_v * ((rr * rr * rr) * jnp.mean(dretn * ret_v, axis=-1, keepdims=True)))

        q3 = _chunks(q_ref[...].astype(BF16))
        kc = _chunks(k_ref[...] * (1.0 / math.sqrt(RHD)))
        k3 = kc.astype(BF16)
        v3 = _chunks(v_ref[...].astype(BF16))
        d3 = dret.astype(BF16)
        dxi = (dret * xi[None]).astype(BF16)
        kz = (kc * zeta[None]).astype(BF16)
        dr3 = _bdot_tn(q3, dxi)
        acc = g_acc[...]
        for i in reversed(range(CBK)):
            gs[i] = acc.astype(BF16)
            acc = dr3[i] + gch * acc
        g_acc[...] = acc
        g3 = gs[...]
        sc = (_bdot_nt(q3, k3) * decay[None]).astype(BF16)
        da = (_bdot_nt(d3, v3) * decay[None]).astype(BF16)
        dq = _bdot(da, k3) + _bdot_nt(dxi, st_ref[...])
        dkc = _bdot_tn(da, q3) + _bdot_nt(v3, g3) * zeta[None]
        dv = _bdot_tn(sc, d3) + _bdot(kz, g3)
        dq_ref[...] = dq.reshape(CBK * CH, RHD).astype(BF16)
        dk_ref[...] = (dkc * (1.0 / math.sqrt(RHD))).reshape(CBK * CH, RHD).astype(BF16)
        dv_ref[...] = dv.reshape(CBK * CH, RHD).astype(BF16)

    col, own, state, const, dm = _ret_specs(True)
    return pl.pallas_call(
        body, name="ret_bwd", grid=(RH, RSTEPS),
        in_specs=[const, col(0), col(1), col(2), col(3), own, state, dm],
        out_specs=[own, own, own, own],
        out_shape=[jax.ShapeDtypeStruct((S, RH * RHD), BF16)] * 4,
        scratch_shapes=[pltpu.VMEM((RHD, RHD), F32), pltpu.VMEM((CBK, RHD, RHD), BF16)],
        compiler_params=_cp(("parallel", "arbitrary")),
    )(_ret_consts(), proj, proj, proj, proj, ret, states, dmixed)


class _NoReduction:
    def start(self, group, grads):
        pass

    def local(self, name, first=()):
        return []

    def landed(self, name):
        return []

    def update(self, name):
        return []


def _local_step(x, tgt, nw1, nw2, nw3, win, wout, wg, wu, wd, red=None):
    red = red or _NoReduction()

    def after(values, first):
        return lax.optimization_barrier((tuple(values), tuple(first)))[0]

    h1, r1 = _rms_fwd(x, nw1)
    proj = _proj(h1, win)
    o, ma, lse = _attn_fwd(proj)
    ret, mr, states = _ret_fwd(proj)
    x2, h2, r2 = _out_proj_rms(x, ma, mr, wout, nw2)
    g, u, a = _ffn_up(h2, wg, wu)
    dx3, dx3b, st3 = _ffn_down_loss(x2, a, wd, nw3, tgt)

    dwd = _wgrad_rows(a, dx3b, "wgrad_down")
    red.start(["w_down"], [dwd])
    (dx3b,) = after([dx3b], [dwd])
    dg, du = _ffn_down_bwd(dx3b, wd, g, u)
    dg, du = after([dg, du], red.local("w_down", first=[dg]))
    dwg = _wgrad_rows(dg, h2, "wgrad_gate")
    red.start(["w_gate"], [dwg])
    (du,) = after([du], [dwg])
    dwu = _wgrad_rows(du, h2, "wgrad_up")
    red.start(["w_up"], [dwu])
    dg, du = after([dg, du], [dwu] + red.local("w_gate"))
    dx2, dx2b, st2 = _ffn_up_bwd(dg, du, wg, wu, dx3, x2, r2, nw2)
    (dx2b,) = after([dx2b], red.local("w_up", first=[dx2b] + red.landed("w_down")))
    dwo = _wgrad_out(ma, mr, dx2b)
    red.start(["w_out"], [dwo])
    (dx2b,) = after([dx2b], [dwo])
    dmixed = _out_proj_bwd(dx2b, wout)
    dqa, dka, dva = _attn_bwd(proj, dmixed, o, lse)
    (dmixed,) = after([dmixed], [dqa] + red.landed("w_gate"))
    dqr, dkr, dvr, dgr = _ret_bwd(proj, ret, states, dmixed)
    (dqr,) = after([dqr], red.local("w_out", first=[dqr] + red.landed("w_up")))
    dproj = jnp.concatenate([dqa, dka, dva, dqr, dkr, dvr, dgr], axis=1)
    dwi = _wgrad_in(h1, dproj)
    red.start(["w_in"], after([dwi], red.landed("w_out")))
    early = red.update("w_down") + red.update("w_gate")
    (dproj,) = after([dproj], red.local("w_in", first=early))
    gx, st1 = _in_proj_bwd(dproj, win, dx2, x, r1, nw1)
    stats = jnp.concatenate([st1[0:1], st2[0:1], st3[0:2], jnp.zeros((4, D), F32)], axis=0)
    return stats, gx, dwi, dwo, dwg, dwu, dwd


def _place():
    x, y, c = lax.axis_index("x"), lax.axis_index("y"), lax.axis_index("c")
    return x, y, c, [(1 - x, y), (x, 1 - y), (1 - x, 1 - y)]


def _handshake(peers):
    barrier = pltpu.get_barrier_semaphore()
    for peer in peers:
        pl.semaphore_signal(barrier, inc=1, device_id=peer, device_id_type=MESH)
    pl.semaphore_wait(barrier, len(peers))


def _all_gather(shards, name, collective_id):
    na = len(shards)
    SIB, XN0, XN1, YN1, YN0, VIA_X, VIA_Y = 0, 1, 2, 3, 4, 5, 6
    D2D = {XN0: 7, XN1: 8, YN1: 9, YN0: 10, VIA_X: 11, VIA_Y: 12}

    def body(*refs):
        ins, outs = refs[:na], refs[na:2 * na]
        send_sems, recv_sems, local_sems = refs[2 * na:]
        x, y, c, _ = _place()
        me, sib = (x, y, c), (x, y, 1 - c)
        xn, yn, dg = (1 - x, y, c), (x, 1 - y, c), (1 - x, 1 - y, c)
        _handshake([sib, xn, yn])

        def part(ref, h):
            rows = ref.shape[0] // 2
            return ref if h is None else ref.at[pl.ds(h * rows, rows)]

        def block(a, owner, h):
            return part(outs[a].at[4 * owner[0] + 2 * owner[1] + owner[2]], h)

        def copy(a, k, owner, h, to, own_src=False):
            return pltpu.make_async_remote_copy(
                src_ref=part(ins[a], h) if own_src else block(a, owner, h), dst_ref=block(a, owner, h),
                send_sem=send_sems.at[a, k], recv_sem=recv_sems.at[a, k], device_id=to, device_id_type=MESH)

        def other(p):
            return (p[0], p[1], 1 - c)

        mine = [pltpu.make_async_copy(ins[a], block(a, me, None), local_sems.at[a]) for a in range(na)]
        for cp in mine:
            cp.start()
        sent = []
        for a in range(na):
            sent += [copy(a, XN0, me, 0, xn, True), copy(a, YN1, me, 1, yn, True),
                     copy(a, XN1, me, 1, xn, True), copy(a, YN0, me, 0, yn, True)]
        sent += [copy(a, SIB, me, None, sib, True) for a in range(na)]
        for cp in sent:
            cp.start()

        def landed(a, k, owner, h, then):
            copy(a, k, owner, h, me).wait_recv()
            for k2, to in then + [(D2D[k], sib)]:
                cp = copy(a, k2, owner, h, to)
                cp.start()
                sent.append(cp)

        for a in range(na):
            landed(a, XN0, xn, 0, [(VIA_Y, yn)])
            landed(a, YN1, yn, 1, [(VIA_X, xn)])
            landed(a, XN1, xn, 1, [])
            landed(a, YN0, yn, 0, [])
        for a in range(na):
            landed(a, VIA_Y, dg, 0, [])
            landed(a, VIA_X, dg, 1, [])
        for a in range(na):
            copy(a, SIB, sib, None, me).wait_recv()
            for k, owner, h in ((XN0, xn, 0), (XN1, xn, 1), (YN1, yn, 1), (YN0, yn, 0), (VIA_Y, dg, 0), (VIA_X, dg, 1)):
                copy(a, D2D[k], other(owner), h, me).wait_recv()
        for cp in sent:
            cp.wait_send()
        for cp in mine:
            cp.wait()

    return _sequencer_call(
        body, name, collective_id,
        [jax.ShapeDtypeStruct((NDEV,) + s.shape, s.dtype) for s in shards],
        [pltpu.SemaphoreType.DMA((na, 13)), pltpu.SemaphoreType.DMA((na, 13)), pltpu.SemaphoreType.DMA((na,))])(*shards)


def _sequencer_call(body, name, collective_id, out_type, scratch_types):
    return pl.kernel(
        body, name=name, out_type=out_type,
        mesh=plsc.ScalarSubcoreMesh(axis_name="sequencer", num_cores=1),
        scratch_types=scratch_types,
        compiler_params=pltpu.CompilerParams(collective_id=collective_id))


def _exchange_sibling(grads, name, collective_id):
    na = len(grads)

    def body(*refs):
        ins, outs = refs[:na], refs[na:2 * na]
        send_sems, recv_sems = refs[2 * na:]
        x, y, c, _ = _place()
        _handshake([(x, y, 1 - c)])
        cps = []
        for a in range(na):
            for k in range(4):
                cps.append(pltpu.make_async_remote_copy(
                    src_ref=ins[a].at[2 * k + (1 - c)], dst_ref=outs[a].at[k],
                    send_sem=send_sems.at[a, k], recv_sem=recv_sems.at[a, k],
                    device_id=(x, y, 1 - c), device_id_type=MESH))
        for cp in cps:
            cp.start()
        for cp in cps:
            cp.wait()

    return _sequencer_call(
        body, name, collective_id,
        [jax.ShapeDtypeStruct((4,) + g.shape[1:], g.dtype) for g in grads],
        [pltpu.SemaphoreType.DMA((na, 4)), pltpu.SemaphoreType.DMA((na, 4))])(*grads)


def _row_tile(rows, cols):
    for t in (512, 256, 176, 128, 64, 32, 16):
        if rows % t == 0 and t * cols * 4 <= (1 << 20):
            return t
    raise ValueError((rows, cols))


def _chip_sum(place, g, got, name):
    _, r, c = g.shape
    tm = r

    def body(pos_ref, g_ref, got_ref, o_ref):
        o_ref[...] = (g_ref[...].astype(F32) + got_ref[...].astype(F32)).astype(BF16)

    def chip(j, pos):
        return 2 * (pos[0] ^ jnp.where(j == 1, 0, 1)) + (pos[1] ^ jnp.where(j == 0, 0, 1))

    return pl.pallas_call(
        body, name=name,
        grid_spec=pltpu.PrefetchScalarGridSpec(
            num_scalar_prefetch=1, grid=(3, r // tm),
            in_specs=[pl.BlockSpec((None, tm, c), lambda j, i, pos: (2 * chip(j, pos) + pos[2], i, 0)),
                      pl.BlockSpec((None, tm, c), lambda j, i, pos: (chip(j, pos), i, 0))],
            out_specs=pl.BlockSpec((None, tm, c), lambda j, i, pos: (j, i, 0))),
        out_shape=jax.ShapeDtypeStruct((3, r, c), BF16),
        compiler_params=_cp(("parallel", "parallel")),
    )(place, g, got)


def _exchange_chips(sums, name, collective_id):
    na = len(sums)

    def body(*refs):
        ins, outs = refs[:na], refs[na:2 * na]
        send_sems, recv_sems = refs[2 * na:]
        x, y, c, chips = _place()
        _handshake([(*chip, c) for chip in chips])
        cps = []
        for a in range(na):
            for j, chip in enumerate(chips):
                cps.append(pltpu.make_async_remote_copy(
                    src_ref=ins[a].at[j], dst_ref=outs[a].at[j],
                    send_sem=send_sems.at[a, j], recv_sem=recv_sems.at[a, j],
                    device_id=(*chip, c), device_id_type=MESH))
        for cp in cps:
            cp.start()
        for cp in cps:
            cp.wait()

    return _sequencer_call(
        body, name, collective_id,
        [jax.ShapeDtypeStruct((3,) + s.shape[1:], s.dtype) for s in sums],
        [pltpu.SemaphoreType.DMA((na, 3)), pltpu.SemaphoreType.DMA((na, 3))])(*sums)


def _exchange_stats(stats, collective_id):
    def body(st_in, st_out, st_send, st_recv, local_sem):
        x, y, c, _ = _place()
        me_idx = 4 * x + 2 * y + c
        peers = [(x ^ ((k >> 2) & 1), y ^ ((k >> 1) & 1), c ^ (k & 1)) for k in range(1, 8)]
        _handshake(peers)
        mine = pltpu.make_async_copy(st_in, st_out.at[me_idx], local_sem)
        mine.start()
        cps = [pltpu.make_async_remote_copy(
            src_ref=st_in, dst_ref=st_out.at[me_idx], send_sem=st_send.at[k], recv_sem=st_recv.at[k],
            device_id=peer, device_id_type=MESH) for k, peer in enumerate(peers)]
        for cp in cps:
            cp.start()
        for cp in cps:
            cp.wait()
        mine.wait()

    return _sequencer_call(
        body, "exchange_stats", collective_id,
        jax.ShapeDtypeStruct((NDEV,) + stats.shape, stats.dtype),
        [pltpu.SemaphoreType.DMA((7,)), pltpu.SemaphoreType.DMA((7,)), pltpu.SemaphoreType.DMA])(stats)


class _Reduction:
    def __init__(self, place, first_collective_id, state):
        self.place = place
        self.ids = iter(range(first_collective_id, 32))
        self.state = state
        self.groups = {}
        self.updates = {}

    def next_id(self):
        return next(self.ids)

    def start(self, group, grads):
        got = _exchange_sibling(grads, "sibling_exchange_" + group[0], self.next_id())
        self.groups[group[0]] = dict(names=group, grads=grads, got=got)

    def local(self, name, first=()):
        grp = self.groups[name]
        grads = lax.optimization_barrier((tuple(grp["grads"]), tuple(first)))[0]
        grp["sums"] = [_chip_sum(self.place, g, s, "chip_sum_" + n)
                       for g, s, n in zip(grads, grp["got"], grp["names"])]
        grp["chips"] = _exchange_chips(grp["sums"], "chip_exchange_" + name, self.next_id())
        return grp["sums"]

    def landed(self, name):
        return list(self.groups[name]["chips"])

    def update(self, name):
        if name not in self.updates:
            grp = next(g for g in self.groups.values() if name in g["names"])
            k = grp["names"].index(name)
            self.updates[name] = _shard_update(self.place, *self.state[name], grp["grads"][k], grp["got"][k],
                                               grp["chips"][k], "update_" + name)
        return list(self.updates[name])


def _adamw(w, g, m, v):
    m = ADAM_B1 * m + (1.0 - ADAM_B1) * g
    v = ADAM_B2 * v + (1.0 - ADAM_B2) * (g * g)
    m_hat = m / (1.0 - ADAM_B1 ** ADAM_STEP)
    v_hat = v / (1.0 - ADAM_B2 ** ADAM_STEP)
    delta = -ADAM_LR * (m_hat / (jnp.sqrt(v_hat) + ADAM_EPS) + ADAM_WD * w)
    return delta, m, v


def _shard_update(place, w, m, v, g, got_sib, got_chips, name):
    r, c = w.shape
    tm = _row_tile(r, c)

    def body(pos_ref, w_ref, m_ref, v_ref, g_ref, s_ref, c_ref, go_ref, d_ref, mo_ref, vo_ref):
        grad = g_ref[...].astype(F32) + s_ref[...].astype(F32)
        for j in range(3):
            grad = grad + c_ref[j].astype(F32)
        delta, mn, vn = _adamw(w_ref[...], grad, m_ref[...], v_ref[...])
        go_ref[...] = grad
        d_ref[...] = delta
        mo_ref[...] = mn
        vo_ref[...] = vn

    row = pl.BlockSpec((tm, c), lambda i, pos: (i, 0))
    return pl.pallas_call(
        body, name=name,
        grid_spec=pltpu.PrefetchScalarGridSpec(
            num_scalar_prefetch=1, grid=(r // tm,),
            in_specs=[row, row, row,
                      pl.BlockSpec((None, tm, c), lambda i, pos: (4 * pos[0] + 2 * pos[1] + pos[2], i, 0)),
                      pl.BlockSpec((None, tm, c), lambda i, pos: (2 * pos[0] + pos[1], i, 0)),
                      pl.BlockSpec((3, tm, c), lambda i, pos: (0, i, 0))],
            out_specs=[row, row, row, row]),
        out_shape=[jax.ShapeDtypeStruct((r, c), F32)] * 4,
        compiler_params=_cp(("parallel",)),
    )(place, w, m, v, g, got_sib, got_chips)


def _small_update(stats_all, ws, ms, vs):
    def body(st_ref, w_ref, m_ref, v_ref, go_ref, d_ref, mo_ref, vo_ref):
        grad = st_ref[0]
        for k in range(1, NDEV):
            grad = grad + st_ref[k]
        delta, mn, vn = _adamw(w_ref[...], grad, m_ref[...], v_ref[...])
        go_ref[...] = grad
        d_ref[...] = delta
        mo_ref[...] = mn
        vo_ref[...] = vn

    return pl.pallas_call(
        body, name="small_update",
        out_shape=[jax.ShapeDtypeStruct((8, D), F32)] * 4,
        compiler_params=_cp(),
    )(stats_all, ws, ms, vs)


def kernel(x, norm_mix_w, w_in, w_out, norm_ffn_w, w_gate, w_up, w_down, norm_final_w, loss_target, m_norm_mix_w, m_w_in, m_w_out, m_norm_ffn_w, m_w_gate, m_w_up, m_w_down, m_norm_final_w, v_norm_mix_w, v_w_in, v_w_out, v_norm_ffn_w, v_w_gate, v_w_up, v_w_down, v_norm_final_w):
    tr = {"w_gate", "w_up"}
    names = ["w_in", "w_out", "w_gate", "w_up", "w_down"]

    def view(a, n):
        return a[0].T if n in tr else a[0]

    big_w = [view(a, n) for a, n in zip([w_in, w_out, w_gate, w_up, w_down], names)]
    big_m = [view(a, n) for a, n in zip([m_w_in, m_w_out, m_w_gate, m_w_up, m_w_down], names)]
    big_v = [view(a, n) for a, n in zip([v_w_in, v_w_out, v_w_gate, v_w_up, v_w_down], names)]

    shards = [_cast_bf16(w, "cast_" + n) for w, n in zip(big_w, names)]
    (win,) = _all_gather(shards[0:1], "all_gather_w_in", 1)
    wout, wg, wu = _all_gather(shards[1:4], "all_gather_out_gate_up", 2)
    (wd,) = _all_gather(shards[4:5], "all_gather_w_down", 3)
    nw3 = norm_final_w.reshape(1, D)
    place = jnp.stack([lax.axis_index("x"), lax.axis_index("y"), lax.axis_index("c")]).astype(jnp.int32)
    red = _Reduction(place, 4, {n: (w, m, v) for n, w, m, v in zip(names, big_w, big_m, big_v)})
    stats, gx, *_ = _local_step(
        x[0], loss_target[0], norm_mix_w, norm_ffn_w, nw3, win, wout.reshape(D, D), wg, wu, wd, red)
    stats_all = _exchange_stats(stats, red.next_id())
    upd = [red.update(n) for n in names]
    stats_all = lax.optimization_barrier((stats_all, tuple(upd[0])))[0]

    def rows(a, b, c):
        return jnp.concatenate([a.reshape(1, D), b.reshape(1, D), c.reshape(1, D), jnp.zeros((5, D), F32)], axis=0)

    sg, sd, sm, sv = _small_update(stats_all, rows(norm_mix_w, norm_ffn_w, norm_final_w),
                                   rows(m_norm_mix_w, m_norm_ffn_w, m_norm_final_w),
                                   rows(v_norm_mix_w, v_norm_ffn_w, v_norm_final_w))
    loss = sg[3, 0]

    def outs(k, small):
        big = [(u[k].T if n in tr else u[k])[None] for u, n in zip(upd, names)]
        return [small[0:1], big[0], big[1], small[1:2], big[2], big[3], big[4], small[2]]

    return (loss, gx[None], *outs(0, sg), *outs(1, sd), *outs(2, sm), *outs(3, sv))
```

```python
import functools
import math

import numpy as np
import jax
import jax.numpy as jnp
from jax import lax
from jax.experimental import pallas as pl
from jax.experimental.pallas import tpu as pltpu
from jax.experimental.pallas import tpu_sc as plsc

F32 = jnp.float32
BF16 = jnp.bfloat16

S = 2048
D = 2048
NDEV = 8
N_IN = 7168 // NDEV
N_FF = 5632 // NDEV
N_OUT = 2048 // NDEV
AH, AHD = 8, 128
RH, RHD = 4, 256
CH = 128
NB = S // CH
EPS = 1e-6
PATTERNS = ((1, 16), (4, 4), (16, 1))
NEG = -1e30
VMEM_LIMIT = 56 * 1024 * 1024

ADAM_LR, ADAM_B1, ADAM_B2, ADAM_EPS, ADAM_WD, ADAM_STEP = 0.001, 0.9, 0.999, 1e-08, 0.01, 10
MESH = pl.DeviceIdType.MESH


def _cp(sem=None):
    return pltpu.CompilerParams(dimension_semantics=sem, vmem_limit_bytes=VMEM_LIMIT)


def _dot(a, b):
    return jnp.dot(a, b, preferred_element_type=F32)


def _dot_nt(a, b):
    return lax.dot_general(a, b, (((1,), (1,)), ((), ())), preferred_element_type=F32)


def _dot_tn(a, b):
    return lax.dot_general(a, b, (((0,), (0,)), ((), ())), preferred_element_type=F32)


def _sigmoid(x):
    return 0.5 * jnp.tanh(0.5 * x) + 0.5


def _cast_bf16(w, name):
    r, c = w.shape
    tm = r if r <= 1024 else 512

    def body(w_ref, o_ref):
        o_ref[...] = w_ref[...].astype(BF16)

    return pl.pallas_call(
        body, name=name, grid=(r // tm,),
        in_specs=[pl.BlockSpec((tm, c), lambda i: (i, 0))],
        out_specs=pl.BlockSpec((tm, c), lambda i: (i, 0)),
        out_shape=jax.ShapeDtypeStruct((r, c), BF16),
        compiler_params=_cp(("parallel",)),
    )(w)


def _rms_fwd(x, nw):
    tm = 256

    def body(x_ref, w_ref, h_ref, r_ref):
        xs = x_ref[...]
        r = lax.rsqrt(jnp.mean(xs * xs, axis=-1, keepdims=True) + EPS)
        h_ref[...] = ((xs * r) * w_ref[...]).astype(BF16)
        r_ref[...] = r

    return pl.pallas_call(
        body, name="rms_fwd", grid=(S // tm,),
        in_specs=[pl.BlockSpec((tm, D), lambda i: (i, 0)), pl.BlockSpec((1, D), lambda i: (0, 0))],
        out_specs=[pl.BlockSpec((tm, D), lambda i: (i, 0)), pl.BlockSpec((tm, 1), lambda i: (i, 0))],
        out_shape=[jax.ShapeDtypeStruct((S, D), BF16), jax.ShapeDtypeStruct((S, 1), F32)],
        compiler_params=_cp(("parallel",)),
    )(x, nw)


def _rms_bwd_tile(dh, xs, r, nw):
    dnw = jnp.sum(dh * (xs * r), axis=0, keepdims=True)
    gy = dh * nw
    dx = r * gy - xs * ((r * r * r) * jnp.mean(gy * xs, axis=-1, keepdims=True))
    return dx, dnw


def _proj(h1, win):
    tm = 1024

    def body(a_ref, w_ref, o_ref):
        o_ref[...] = _dot(a_ref[...], w_ref[...])

    return pl.pallas_call(
        body, name="proj", grid=(NDEV, S // tm),
        in_specs=[pl.BlockSpec((tm, D), lambda p, m: (m, 0)),
                  pl.BlockSpec((None, D, N_IN), lambda p, m: (p, 0, 0))],
        out_specs=pl.BlockSpec((tm, N_IN), lambda p, m: (m, p)),
        out_shape=jax.ShapeDtypeStruct((S, NDEV * N_IN), F32),
        compiler_params=_cp(("parallel", "parallel")),
    )(h1, win)


def _out_proj_rms(x, ma, mr, wout, nw):
    tm = 256
    half = D // 2

    def body(x_ref, ma_ref, mr_ref, w_ref, nw_ref, x2_ref, h_ref, r_ref):
        acc = _dot(ma_ref[...], w_ref[0:half, :]) + _dot(mr_ref[...], w_ref[half:D, :])
        x2 = x_ref[...] + acc
        r = lax.rsqrt(jnp.mean(x2 * x2, axis=-1, keepdims=True) + EPS)
        x2_ref[...] = x2
        h_ref[...] = ((x2 * r) * nw_ref[...]).astype(BF16)
        r_ref[...] = r

    return pl.pallas_call(
        body, name="out_proj_rms", grid=(S // tm,),
        in_specs=[pl.BlockSpec((tm, D), lambda i: (i, 0)),
                  pl.BlockSpec((tm, half), lambda i: (i, 0)),
                  pl.BlockSpec((tm, half), lambda i: (i, 0)),
                  pl.BlockSpec((D, D), lambda i: (0, 0)),
                  pl.BlockSpec((1, D), lambda i: (0, 0))],
        out_specs=[pl.BlockSpec((tm, D), lambda i: (i, 0)), pl.BlockSpec((tm, D), lambda i: (i, 0)),
                   pl.BlockSpec((tm, 1), lambda i: (i, 0))],
        out_shape=[jax.ShapeDtypeStruct((S, D), F32), jax.ShapeDtypeStruct((S, D), BF16),
                   jax.ShapeDtypeStruct((S, 1), F32)],
        compiler_params=_cp(("parallel",)),
    )(x, ma, mr, wout, nw)


def _ffn_up(h2, wg, wu):
    tm = 1024

    def body(h_ref, wg_ref, wu_ref, g_ref, u_ref, a_ref):
        h = h_ref[...]
        g = _dot_nt(h, wg_ref[...])
        u = _dot_nt(h, wu_ref[...])
        g_ref[...] = g
        u_ref[...] = u
        a_ref[...] = ((g * _sigmoid(g)) * u).astype(BF16)

    blk = pl.BlockSpec((None, tm, N_FF), lambda p, m: (p, m, 0))
    wblk = pl.BlockSpec((None, N_FF, D), lambda p, m: (p, 0, 0))
    return pl.pallas_call(
        body, name="ffn_up", grid=(NDEV, S // tm),
        in_specs=[pl.BlockSpec((tm, D), lambda p, m: (m, 0)), wblk, wblk],
        out_specs=[blk, blk, blk],
        out_shape=[jax.ShapeDtypeStruct((NDEV, S, N_FF), F32), jax.ShapeDtypeStruct((NDEV, S, N_FF), F32),
                   jax.ShapeDtypeStruct((NDEV, S, N_FF), BF16)],
        compiler_params=_cp(("parallel", "parallel")),
    )(h2, wg, wu)


def _ffn_down_loss(x2, a, wd, nw, tgt):
    tm = 512

    def body(x2_ref, a_ref, w_ref, nw_ref, t_ref, dx_ref, dxb_ref, st_ref, acc_ref):
        m, p = pl.program_id(0), pl.program_id(1)

        @pl.when(p == 0)
        def _():
            acc_ref[...] = jnp.zeros_like(acc_ref)

        @pl.when((p == 0) & (m == 0))
        def _():
            st_ref[...] = jnp.zeros_like(st_ref)

        acc_ref[...] += _dot(a_ref[...], w_ref[...])

        @pl.when(p == NDEV - 1)
        def _():
            x3 = x2_ref[...] + acc_ref[...]
            nwv = nw_ref[...]
            r = lax.rsqrt(jnp.mean(x3 * x3, axis=-1, keepdims=True) + EPS)
            y = (x3 * r) * nwv
            err = y - t_ref[...]
            loss = 0.5 * jnp.sum(jnp.mean(err * err, axis=-1, keepdims=True), axis=0, keepdims=True)
            dy = err * (1.0 / D)
            dx, dnw = _rms_bwd_tile(dy, x3, r, nwv)
            dx_ref[...] = dx
            dxb_ref[...] = dx.astype(BF16)
            st_ref[0:1, :] += dnw
            st_ref[1:2, :] += jnp.broadcast_to(loss, (1, D))

    return pl.pallas_call(
        body, name="ffn_down_loss", grid=(S // tm, NDEV),
        in_specs=[pl.BlockSpec((tm, D), lambda m, p: (m, 0)),
                  pl.BlockSpec((None, tm, N_FF), lambda m, p: (p, m, 0)),
                  pl.BlockSpec((None, N_FF, D), lambda m, p: (p, 0, 0)),
                  pl.BlockSpec((1, D), lambda m, p: (0, 0)),
                  pl.BlockSpec((tm, D), lambda m, p: (m, 0))],
        out_specs=[pl.BlockSpec((tm, D), lambda m, p: (m, 0)), pl.BlockSpec((tm, D), lambda m, p: (m, 0)),
                   pl.BlockSpec((8, D), lambda m, p: (0, 0))],
        out_shape=[jax.ShapeDtypeStruct((S, D), F32), jax.ShapeDtypeStruct((S, D), BF16),
                   jax.ShapeDtypeStruct((8, D), F32)],
        scratch_shapes=[pltpu.VMEM((tm, D), F32)],
        compiler_params=_cp(("arbitrary", "arbitrary")),
    )(x2, a, wd, nw, tgt)


def _ffn_down_bwd(dx3b, wd, g, u):
    tm = 1024

    def body(dx_ref, w_ref, g_ref, u_ref, dg_ref, du_ref):
        da = _dot_nt(dx_ref[...], w_ref[...])
        gv = g_ref[...]
        sg = _sigmoid(gv)
        silu = gv * sg
        dg_ref[...] = ((da * u_ref[...]) * (sg * (1.0 + gv * (1.0 - sg)))).astype(BF16)
        du_ref[...] = (da * silu).astype(BF16)

    blk = pl.BlockSpec((None, tm, N_FF), lambda p, m: (p, m, 0))
    return pl.pallas_call(
        body, name="ffn_down_bwd", grid=(NDEV, S // tm),
        in_specs=[pl.BlockSpec((tm, D), lambda p, m: (m, 0)),
                  pl.BlockSpec((None, N_FF, D), lambda p, m: (p, 0, 0)), blk, blk],
        out_specs=[blk, blk],
        out_shape=[jax.ShapeDtypeStruct((NDEV, S, N_FF), BF16), jax.ShapeDtypeStruct((NDEV, S, N_FF), BF16)],
        compiler_params=_cp(("parallel", "parallel")),
    )(dx3b, wd, g, u)


def _ffn_up_bwd(dg, du, wg, wu, dres, xs, r, nw):
    tm = 512

    def body(dg_ref, du_ref, wg_ref, wu_ref, dres_ref, x_ref, r_ref, nw_ref, dx_ref, dxb_ref, st_ref, acc_ref):
        m, p = pl.program_id(0), pl.program_id(1)

        @pl.when(p == 0)
        def _():
            acc_ref[...] = jnp.zeros_like(acc_ref)

        @pl.when((p == 0) & (m == 0))
        def _():
            st_ref[...] = jnp.zeros_like(st_ref)

        acc_ref[...] += _dot(dg_ref[...], wg_ref[...]) + _dot(du_ref[...], wu_ref[...])

        @pl.when(p == NDEV - 1)
        def _():
            dx, dnw = _rms_bwd_tile(acc_ref[...], x_ref[...], r_ref[...], nw_ref[...])
            dx = dres_ref[...] + dx
            dx_ref[...] = dx
            dxb_ref[...] = dx.astype(BF16)
            st_ref[0:1, :] += dnw

    blk = pl.BlockSpec((None, tm, N_FF), lambda m, p: (p, m, 0))
    wblk = pl.BlockSpec((None, N_FF, D), lambda m, p: (p, 0, 0))
    row = pl.BlockSpec((tm, D), lambda m, p: (m, 0))
    return pl.pallas_call(
        body, name="ffn_up_bwd", grid=(S // tm, NDEV),
        in_specs=[blk, blk, wblk, wblk, row, row, pl.BlockSpec((tm, 1), lambda m, p: (m, 0)),
                  pl.BlockSpec((1, D), lambda m, p: (0, 0))],
        out_specs=[row, row, pl.BlockSpec((8, D), lambda m, p: (0, 0))],
        out_shape=[jax.ShapeDtypeStruct((S, D), F32), jax.ShapeDtypeStruct((S, D), BF16),
                   jax.ShapeDtypeStruct((8, D), F32)],
        scratch_shapes=[pltpu.VMEM((tm, D), F32)],
        compiler_params=_cp(("arbitrary", "arbitrary")),
    )(dg, du, wg, wu, dres, xs, r, nw)


def _out_proj_bwd(dx2b, wout):
    tm = 256

    def body(dx_ref, w_ref, o_ref):
        o_ref[...] = _dot_nt(dx_ref[...], w_ref[...])

    return pl.pallas_call(
        body, name="out_proj_bwd", grid=(S // tm,),
        in_specs=[pl.BlockSpec((tm, D), lambda i: (i, 0)), pl.BlockSpec((D, D), lambda i: (0, 0))],
        out_specs=pl.BlockSpec((tm, D), lambda i: (i, 0)),
        out_shape=jax.ShapeDtypeStruct((S, D), F32),
        compiler_params=_cp(("parallel",)),
    )(dx2b, wout)


def _in_proj_bwd(dproj, win, dres, xs, r, nw):
    tm = 512

    def body(dp_ref, w_ref, dres_ref, x_ref, r_ref, nw_ref, dx_ref, st_ref, acc_ref):
        m, p = pl.program_id(0), pl.program_id(1)

        @pl.when(p == 0)
        def _():
            acc_ref[...] = jnp.zeros_like(acc_ref)

        @pl.when((p == 0) & (m == 0))
        def _():
            st_ref[...] = jnp.zeros_like(st_ref)

        acc_ref[...] += _dot_nt(dp_ref[...], w_ref[...])

        @pl.when(p == NDEV - 1)
        def _():
            dx, dnw = _rms_bwd_tile(acc_ref[...], x_ref[...], r_ref[...], nw_ref[...])
            dx_ref[...] = dres_ref[...] + dx
            st_ref[0:1, :] += dnw

    row = pl.BlockSpec((tm, D), lambda m, p: (m, 0))
    return pl.pallas_call(
        body, name="in_proj_bwd", grid=(S // tm, NDEV),
        in_specs=[pl.BlockSpec((tm, N_IN), lambda m, p: (m, p)),
                  pl.BlockSpec((None, D, N_IN), lambda m, p: (p, 0, 0)),
                  row, row, pl.BlockSpec((tm, 1), lambda m, p: (m, 0)),
                  pl.BlockSpec((1, D), lambda m, p: (0, 0))],
        out_specs=[row, pl.BlockSpec((8, D), lambda m, p: (0, 0))],
        out_shape=[jax.ShapeDtypeStruct((S, D), F32), jax.ShapeDtypeStruct((8, D), F32)],
        scratch_shapes=[pltpu.VMEM((tm, D), F32)],
        compiler_params=_cp(("arbitrary", "arbitrary")),
    )(dproj, win, dres, xs, r, nw)


def _wgrad_in(h1, dproj):
    def body(a_ref, d_ref, o_ref):
        o_ref[...] = _dot_tn(a_ref[...], d_ref[...]).astype(BF16)

    return pl.pallas_call(
        body, name="wgrad_in", grid=(NDEV,),
        in_specs=[pl.BlockSpec((S, D), lambda p: (0, 0)), pl.BlockSpec((S, N_IN), lambda p: (0, p))],
        out_specs=pl.BlockSpec((None, D, N_IN), lambda p: (p, 0, 0)),
        out_shape=jax.ShapeDtypeStruct((NDEV, D, N_IN), BF16),
        compiler_params=_cp(("parallel",)),
    )(h1, dproj)


def _wgrad_rows(a3, dy, name):
    def body(a_ref, d_ref, o_ref):
        o_ref[...] = _dot_tn(a_ref[...], d_ref[...]).astype(BF16)

    return pl.pallas_call(
        body, name=name, grid=(NDEV,),
        in_specs=[pl.BlockSpec((None, S, N_FF), lambda p: (p, 0, 0)), pl.BlockSpec((S, D), lambda p: (0, 0))],
        out_specs=pl.BlockSpec((None, N_FF, D), lambda p: (p, 0, 0)),
        out_shape=jax.ShapeDtypeStruct((NDEV, N_FF, D), BF16),
        compiler_params=_cp(("parallel",)),
    )(a3, dy)


def _wgrad_out(ma, mr, dx2b):
    half = D // 2
    per = half // N_OUT

    def body(ma_ref, mr_ref, d_ref, o_ref):
        p = pl.program_id(0)

        @pl.when(p < per)
        def _():
            o_ref[...] = _dot_tn(ma_ref[...], d_ref[...]).astype(BF16)

        @pl.when(p >= per)
        def _():
            o_ref[...] = _dot_tn(mr_ref[...], d_ref[...]).astype(BF16)

    return pl.pallas_call(
        body, name="wgrad_out", grid=(NDEV,),
        in_specs=[pl.BlockSpec((S, N_OUT), lambda p: (0, jnp.minimum(p, per - 1))),
                  pl.BlockSpec((S, N_OUT), lambda p: (0, jnp.maximum(p - per, 0))),
                  pl.BlockSpec((S, D), lambda p: (0, 0))],
        out_specs=pl.BlockSpec((None, N_OUT, D), lambda p: (p, 0, 0)),
        out_shape=jax.ShapeDtypeStruct((NDEV, N_OUT, D), BF16),
        compiler_params=_cp(("parallel",)),
    )(ma, mr, dx2b)


def _attn_consts():
    c = np.zeros((AH, 8, AHD), np.float32)
    for h in range(AH):
        c[h, :, :] = 2.0 ** (-(h + 1))
    return jnp.asarray(c)


def _permute_in(dst, src, d, cast=None):
    ln = S // d
    for rr in range(d):
        v = src[pl.ds(rr, ln, stride=d), :] if d > 1 else src[...]
        dst[rr * ln:(rr + 1) * ln, :] = v if cast is None else v.astype(cast)


def _attn_masks():
    qi = lax.broadcasted_iota(jnp.int32, (CH, CH), 0)
    kj = lax.broadcasted_iota(jnp.int32, (CH, CH), 1)
    dist_c = (qi - kj).astype(F32)
    dist_p = (qi - kj + CH).astype(F32)
    return (qi >= kj)[None], (kj >= qi)[None], dist_c[None], dist_p[None]


GB = 8


def _bdot_nt(a, b):
    return lax.dot_general(a, b, (((2,), (2,)), ((0,), (0,))), preferred_element_type=F32)


def _bdot(a, b):
    return lax.dot_general(a, b, (((2,), (1,)), ((0,), (0,))), preferred_element_type=F32)


def _bdot_tn(a, b):
    return lax.dot_general(a, b, (((1,), (1,)), ((0,), (0,))), preferred_element_type=F32)


def _shift_block(dst, src):
    dst[0:CH, :] = jnp.zeros((CH, AHD), dst.dtype)
    dst[CH:S, :] = src[0:S - CH, :]


def _has_prev(g, nb):
    blk = lax.broadcasted_iota(jnp.int32, (GB, 1, 1), 0) + g * GB
    return (blk & (nb - 1)) != 0


def _blocks(ref, g):
    return ref[g * GB * CH:(g + 1) * GB * CH, :].reshape(GB, CH, AHD)


def _attn_fwd(proj):
    scale = 1.0 / math.sqrt(AHD)

    def body(c_ref, q_ref, k_ref, v_ref, o_ref, ob_ref, lse_ref, qd, kd, vd, kps, vps, od, ld, *nat):
        onat, lnat = nat[0:3], nat[3:6]
        slope = c_ref[0:1, :]
        mask_c, mask_p, dist_c, dist_p = _attn_masks()
        for pi, (d, nb) in enumerate(PATTERNS):
            _permute_in(qd, q_ref, d, BF16)
            _permute_in(kd, k_ref, d, BF16)
            _permute_in(vd, v_ref, d, BF16)
            if nb > 1:
                _shift_block(kps, kd)
                _shift_block(vps, vd)
            bias_c = -(slope * float(d)) * dist_c
            bias_p = -(slope * float(d)) * dist_p
            for g in range(NB // GB):
                q3, k3, v3 = _blocks(qd, g), _blocks(kd, g), _blocks(vd, g)
                s_c = jnp.where(mask_c, _bdot_nt(q3, k3) * scale + bias_c, NEG)
                mx = jnp.max(s_c, axis=-1, keepdims=True)
                if nb > 1:
                    kp3, vp3 = _blocks(kps, g), _blocks(vps, g)
                    s_p = jnp.where(jnp.logical_and(mask_p, _has_prev(g, nb)),
                                    _bdot_nt(q3, kp3) * scale + bias_p, NEG)
                    mx = jnp.maximum(mx, jnp.max(s_p, axis=-1, keepdims=True))
                    l = (jnp.sum(jnp.exp(s_c - mx), axis=-1, keepdims=True)
                         + jnp.sum(jnp.exp(s_p - mx), axis=-1, keepdims=True))
                    lse = mx + jnp.log(l)
                    o3 = _bdot(jnp.exp(s_c - lse).astype(BF16), v3) + _bdot(jnp.exp(s_p - lse).astype(BF16), vp3)
                else:
                    l = jnp.sum(jnp.exp(s_c - mx), axis=-1, keepdims=True)
                    lse = mx + jnp.log(l)
                    o3 = _bdot(jnp.exp(s_c - lse).astype(BF16), v3)
                rows = slice(g * GB * CH, (g + 1) * GB * CH)
                od[rows, :] = o3.reshape(GB * CH, AHD)
                ld[rows, :] = jnp.broadcast_to(lse, (GB, CH, AHD)).reshape(GB * CH, AHD)
            ln = S // d
            for rr in range(d):
                if d > 1:
                    onat[pi][pl.ds(rr, ln, stride=d), :] = od[rr * ln:(rr + 1) * ln, :]
                    lnat[pi][pl.ds(rr, ln, stride=d), :] = ld[rr * ln:(rr + 1) * ln, :]
                else:
                    onat[pi][...] = od[...]
                    lnat[pi][...] = ld[...]
        l0, l1, l2 = lnat[0][...], lnat[1][...], lnat[2][...]
        mx = jnp.maximum(jnp.maximum(l0, l1), l2)
        e0, e1, e2 = jnp.exp(l0 - mx), jnp.exp(l1 - mx), jnp.exp(l2 - mx)
        den = e0 + e1 + e2
        out = (e0 / den) * onat[0][...] + (e1 / den) * onat[1][...] + (e2 / den) * onat[2][...]
        o_ref[...] = out
        ob_ref[...] = out.astype(BF16)
        lse_ref[...] = mx + jnp.log(den)

    def col(off):
        return pl.BlockSpec((S, AHD), lambda h: (0, off + h))

    return pl.pallas_call(
        body, name="attn_fwd", grid=(AH,),
        in_specs=[pl.BlockSpec((None, 8, AHD), lambda h: (h, 0, 0)), col(0), col(AH), col(2 * AH)],
        out_specs=[col(0), col(0), col(0)],
        out_shape=[jax.ShapeDtypeStruct((S, AH * AHD), F32), jax.ShapeDtypeStruct((S, AH * AHD), BF16),
                   jax.ShapeDtypeStruct((S, AH * AHD), F32)],
        scratch_shapes=[pltpu.VMEM((S, AHD), BF16) for _ in range(5)]
        + [pltpu.VMEM((S, AHD), F32) for _ in range(8)],
        compiler_params=_cp(("parallel",)),
    )(_attn_consts(), proj, proj, proj)


def _attn_bwd(proj, dmixed, o, lse):
    scale = 1.0 / math.sqrt(AHD)

    def body(c_ref, q_ref, k_ref, v_ref, do_ref, o_ref, lse_ref, dq_ref, dk_ref, dv_ref,
             qd, kd, vd, dod, kps, vps, lsd, dld, dqd, dkd, dvd, delta, aq, ak, av):
        slope = c_ref[0:1, :]
        mask_c, mask_p, dist_c, dist_p = _attn_masks()
        delta[...] = jnp.broadcast_to(jnp.sum(do_ref[...] * o_ref[...], axis=-1, keepdims=True), (S, AHD))
        for pi, (d, nb) in enumerate(PATTERNS):
            _permute_in(qd, q_ref, d, BF16)
            _permute_in(kd, k_ref, d, BF16)
            _permute_in(vd, v_ref, d, BF16)
            _permute_in(dod, do_ref, d, BF16)
            _permute_in(lsd, lse_ref, d)
            _permute_in(dld, delta, d)
            if nb > 1:
                _shift_block(kps, kd)
                _shift_block(vps, vd)
            bias_c = -(slope * float(d)) * dist_c
            bias_p = -(slope * float(d)) * dist_p
            for g in range(NB // GB):
                q3, k3, v3, do3 = _blocks(qd, g), _blocks(kd, g), _blocks(vd, g), _blocks(dod, g)
                ls, dl = _blocks(lsd, g), _blocks(dld, g)
                lo, hi = g * GB * CH, (g + 1) * GB * CH
                p_c = jnp.exp(jnp.where(mask_c, _bdot_nt(q3, k3) * scale + bias_c, NEG) - ls)
                ds_c = ((p_c * (_bdot_nt(do3, v3) - dl)) * scale).astype(BF16)
                dq3 = _bdot(ds_c, k3)
                dkd[lo:hi, :] = _bdot_tn(ds_c, q3).reshape(GB * CH, AHD)
                dvd[lo:hi, :] = _bdot_tn(p_c.astype(BF16), do3).reshape(GB * CH, AHD)
                if nb > 1:
                    kp3, vp3 = _blocks(kps, g), _blocks(vps, g)
                    p_p = jnp.exp(jnp.where(jnp.logical_and(mask_p, _has_prev(g, nb)),
                                            _bdot_nt(q3, kp3) * scale + bias_p, NEG) - ls)
                    ds_p = ((p_p * (_bdot_nt(do3, vp3) - dl)) * scale).astype(BF16)
                    dq3 = dq3 + _bdot(ds_p, kp3)
                    dkp = _bdot_tn(ds_p, q3).reshape(GB * CH, AHD)
                    dvp = _bdot_tn(p_p.astype(BF16), do3).reshape(GB * CH, AHD)
                    if g == 0:
                        dkd[0:hi - CH, :] += dkp[CH:, :]
                        dvd[0:hi - CH, :] += dvp[CH:, :]
                    else:
                        dkd[lo - CH:hi - CH, :] += dkp
                        dvd[lo - CH:hi - CH, :] += dvp
                dqd[lo:hi, :] = dq3.reshape(GB * CH, AHD)
            ln = S // d
            for acc, src in ((aq, dqd), (ak, dkd), (av, dvd)):
                if pi == 0:
                    acc[...] = src[...]
                else:
                    for rr in range(d):
                        acc[pl.ds(rr, ln, stride=d), :] += src[rr * ln:(rr + 1) * ln, :]
        dq_ref[...] = aq[...].astype(BF16)
        dk_ref[...] = ak[...].astype(BF16)
        dv_ref[...] = av[...].astype(BF16)

    def col(off):
        return pl.BlockSpec((S, AHD), lambda h: (0, off + h))

    return pl.pallas_call(
        body, name="attn_bwd", grid=(AH,),
        in_specs=[pl.BlockSpec((None, 8, AHD), lambda h: (h, 0, 0)), col(0), col(AH), col(2 * AH),
                  col(0), col(0), col(0)],
        out_specs=[col(0), col(0), col(0)],
        out_shape=[jax.ShapeDtypeStruct((S, AH * AHD), BF16)] * 3,
        scratch_shapes=[pltpu.VMEM((S, AHD), BF16) for _ in range(6)]
        + [pltpu.VMEM((S, AHD), F32) for _ in range(9)],
        compiler_params=_cp(("parallel",)),
    )(_attn_consts(), proj, proj, proj, dmixed, o, lse)


def _ret_consts():
    c = np.zeros((RH, 8, RHD), np.float32)
    for h in range(RH):
        c[h, :, :] = np.log(np.float32(1.0) - np.float32(2.0 ** (-5.0 - h)))
    return jnp.asarray(c)


def _ret_factors(lg):
    i = lax.broadcasted_iota(jnp.int32, (CH, CH), 0)
    j = lax.broadcasted_iota(jnp.int32, (CH, CH), 1)
    dif = (i - j).astype(F32)
    decay = jnp.where(dif >= 0, jnp.exp(lg[:, 0:CH] * jnp.maximum(dif, 0.0)), 0.0)
    row = lax.broadcasted_iota(jnp.int32, (CH, RHD), 0).astype(F32)
    zeta = jnp.exp(lg * (CH - 1.0 - row))
    xi = jnp.exp(lg * (row + 1.0))
    return decay, zeta, xi, jnp.exp(lg * float(CH))


CBK = 8
RSTEPS = NB // CBK


def _ret_specs(rev):
    off = 3 * AH * AHD // RHD
    rows = CBK * CH

    def ch(n):
        return (RSTEPS - 1 - n) if rev else n

    def col(k):
        return pl.BlockSpec((rows, RHD), lambda h, n: (ch(n), off + k * RH + h))

    own = pl.BlockSpec((rows, RHD), lambda h, n: (ch(n), h))
    state = pl.BlockSpec((None, CBK, RHD, RHD), lambda h, n: (h, ch(n), 0, 0))
    const = pl.BlockSpec((None, 8, RHD), lambda h, n: (h, 0, 0))
    dm = pl.BlockSpec((rows, RHD), lambda h, n: (ch(n), AH * AHD // RHD + h))
    return col, own, state, const, dm


def _chunks(x):
    return x.reshape(CBK, CH, RHD)


def _ret_fwd(proj):
    def body(c_ref, q_ref, k_ref, v_ref, g_ref, ret_ref, mr_ref, st_ref, r_acc):
        n = pl.program_id(1)

        @pl.when(n == 0)
        def _():
            r_acc[...] = jnp.zeros_like(r_acc)

        decay, zeta, xi, gch = _ret_factors(c_ref[0:1, :])
        q3 = _chunks(q_ref[...].astype(BF16))
        kc = _chunks(k_ref[...] * (1.0 / math.sqrt(RHD)))
        k3 = kc.astype(BF16)
        v3 = _chunks(v_ref[...].astype(BF16))
        kv3 = _bdot_tn((kc * zeta[None]).astype(BF16), v3)
        r = r_acc[...]
        for i in range(CBK):
            st_ref[i] = r.astype(BF16)
            r = r * gch + kv3[i]
        r_acc[...] = r
        scores = _bdot_nt(q3, k3) * decay[None]
        ret = (_bdot(scores.astype(BF16), v3) + _bdot(q3, st_ref[...]) * xi[None]).reshape(CBK * CH, RHD)
        ret_ref[...] = ret
        rr = lax.rsqrt(jnp.mean(ret * ret, axis=-1, keepdims=True) + EPS)
        gv = g_ref[...]
        mr_ref[...] = ((gv * _sigmoid(gv)) * (ret * rr)).astype(BF16)

    col, own, state, const, _ = _ret_specs(False)
    return pl.pallas_call(
        body, name="ret_fwd", grid=(RH, RSTEPS),
        in_specs=[const, col(0), col(1), col(2), col(3)],
        out_specs=[own, own, state],
        out_shape=[jax.ShapeDtypeStruct((S, RH * RHD), F32), jax.ShapeDtypeStruct((S, RH * RHD), BF16),
                   jax.ShapeDtypeStruct((RH, NB, RHD, RHD), BF16)],
        scratch_shapes=[pltpu.VMEM((RHD, RHD), F32)],
        compiler_params=_cp(("parallel", "arbitrary")),
    )(_ret_consts(), proj, proj, proj, proj)


def _ret_bwd(proj, ret, states, dmixed):
    def body(c_ref, q_ref, k_ref, v_ref, g_ref, ret_ref, st_ref, dm_ref, dq_ref, dk_ref, dv_ref, dg_ref, g_acc, gs):
        n = pl.program_id(1)

        @pl.when(n == 0)
        def _():
            g_acc[...] = jnp.zeros_like(g_acc)

        decay, zeta, xi, gch = _ret_factors(c_ref[0:1, :])
        ret_v = ret_ref[...]
        rr = lax.rsqrt(jnp.mean(ret_v * ret_v, axis=-1, keepdims=True) + EPS)
        gv = g_ref[...]
        sg = _sigmoid(gv)
        dmix = dm_ref[...]
        dg_ref[...] = ((dmix * (ret_v * rr)) * (sg * (1.0 + gv * (1.0 - sg)))).astype(BF16)
        dretn = dmix * (gv * sg)
        dret = _chunks(rr * dretn - ret_v * ((rr * rr * rr) * jnp.mean(dretn * ret_v, axis=-1, keepdims=True)))

        q3 = _chunks(q_ref[...].astype(BF16))
        kc = _chunks(k_ref[...] * (1.0 / math.sqrt(RHD)))
        k3 = kc.astype(BF16)
        v3 = _chunks(v_ref[...].astype(BF16))
        d3 = dret.astype(BF16)
        dxi = (dret * xi[None]).astype(BF16)
        kz = (kc * zeta[None]).astype(BF16)
        dr3 = _bdot_tn(q3, dxi)
        acc = g_acc[...]
        for i in reversed(range(CBK)):
            gs[i] = acc.astype(BF16)
            acc = dr3[i] + gch * acc
        g_acc[...] = acc
        g3 = gs[...]
        sc = (_bdot_nt(q3, k3) * decay[None]).astype(BF16)
        da = (_bdot_nt(d3, v3) * decay[None]).astype(BF16)
        dq = _bdot(da, k3) + _bdot_nt(dxi, st_ref[...])
        dkc = _bdot_tn(da, q3) + _bdot_nt(v3, g3) * zeta[None]
        dv = _bdot_tn(sc, d3) + _bdot(kz, g3)
        dq_ref[...] = dq.reshape(CBK * CH, RHD).astype(BF16)
        dk_ref[...] = (dkc * (1.0 / math.sqrt(RHD))).reshape(CBK * CH, RHD).astype(BF16)
        dv_ref[...] = dv.reshape(CBK * CH, RHD).astype(BF16)

    col, own, state, const, dm = _ret_specs(True)
    return pl.pallas_call(
        body, name="ret_bwd", grid=(RH, RSTEPS),
        in_specs=[const, col(0), col(1), col(2), col(3), own, state, dm],
        out_specs=[own, own, own, own],
        out_shape=[jax.ShapeDtypeStruct((S, RH * RHD), BF16)] * 4,
        scratch_shapes=[pltpu.VMEM((RHD, RHD), F32), pltpu.VMEM((CBK, RHD, RHD), BF16)],
        compiler_params=_cp(("parallel", "arbitrary")),
    )(_ret_consts(), proj, proj, proj, proj, ret, states, dmixed)


class _NoReduction:
    def start(self, group, grads):
        pass

    def local(self, name, first=()):
        return []

    def landed(self, name):
        return []

    def update(self, name):
        return []


def _local_step(x, tgt, nw1, nw2, nw3, win, wout, wg, wu, wd, red=None):
    red = red or _NoReduction()

    def after(values, first):
        return lax.optimization_barrier((tuple(values), tuple(first)))[0]

    h1, r1 = _rms_fwd(x, nw1)
    proj = _proj(h1, win)
    o, ma, lse = _attn_fwd(proj)
    ret, mr, states = _ret_fwd(proj)
    x2, h2, r2 = _out_proj_rms(x, ma, mr, wout, nw2)
    g, u, a = _ffn_up(h2, wg, wu)
    dx3, dx3b, st3 = _ffn_down_loss(x2, a, wd, nw3, tgt)

    dwd = _wgrad_rows(a, dx3b, "wgrad_down")
    red.start(["w_down"], [dwd])
    (dx3b,) = after([dx3b], [dwd])
    dg, du = _ffn_down_bwd(dx3b, wd, g, u)
    dg, du = after([dg, du], red.local("w_down", first=[dg]))
    dwg = _wgrad_rows(dg, h2, "wgrad_gate")
    red.start(["w_gate"], [dwg])
    (du,) = after([du], [dwg])
    dwu = _wgrad_rows(du, h2, "wgrad_up")
    red.start(["w_up"], [dwu])
    dg, du = after([dg, du], [dwu] + red.local("w_gate"))
    dx2, dx2b, st2 = _ffn_up_bwd(dg, du, wg, wu, dx3, x2, r2, nw2)
    (dx2b,) = after([dx2b], red.local("w_up", first=[dx2b] + red.landed("w_down")))
    dwo = _wgrad_out(ma, mr, dx2b)
    red.start(["w_out"], [dwo])
    (dx2b,) = after([dx2b], [dwo])
    dmixed = _out_proj_bwd(dx2b, wout)
    dqa, dka, dva = _attn_bwd(proj, dmixed, o, lse)
    (dmixed,) = after([dmixed], [dqa] + red.landed("w_gate"))
    dqr, dkr, dvr, dgr = _ret_bwd(proj, ret, states, dmixed)
    dproj = jnp.concatenate([dqa, dka, dva, dqr, dkr, dvr, dgr], axis=1)
    dwi = after([_wgrad_in(h1, dproj)], red.landed("w_up"))
    red.start(["w_in"], dwi)
    early = red.local("w_out", first=dwi) + red.update("w_down") + red.update("w_gate")
    (dproj,) = after([dproj], red.local("w_in", first=early))
    gx, st1 = _in_proj_bwd(dproj, win, dx2, x, r1, nw1)
    stats = jnp.concatenate([st1[0:1], st2[0:1], st3[0:2], jnp.zeros((4, D), F32)], axis=0)
    return stats, gx, dwi, dwo, dwg, dwu, dwd


def _place():
    x, y, c = lax.axis_index("x"), lax.axis_index("y"), lax.axis_index("c")
    return x, y, c, [(1 - x, y), (x, 1 - y), (1 - x, 1 - y)]


def _handshake(peers):
    barrier = pltpu.get_barrier_semaphore()
    for peer in peers:
        pl.semaphore_signal(barrier, inc=1, device_id=peer, device_id_type=MESH)
    pl.semaphore_wait(barrier, len(peers))


def _all_gather(shards, name, collective_id):
    na = len(shards)
    SIB, XN0, XN1, YN1, YN0, VIA_X, VIA_Y = 0, 1, 2, 3, 4, 5, 6
    D2D = {XN0: 7, XN1: 8, YN1: 9, YN0: 10, VIA_X: 11, VIA_Y: 12}

    def body(*refs):
        ins, outs = refs[:na], refs[na:2 * na]
        send_sems, recv_sems, local_sems = refs[2 * na:]
        x, y, c, _ = _place()
        me, sib = (x, y, c), (x, y, 1 - c)
        xn, yn, dg = (1 - x, y, c), (x, 1 - y, c), (1 - x, 1 - y, c)
        _handshake([sib, xn, yn])

        def part(ref, h):
            rows = ref.shape[0] // 2
            return ref if h is None else ref.at[pl.ds(h * rows, rows)]

        def block(a, owner, h):
            return part(outs[a].at[4 * owner[0] + 2 * owner[1] + owner[2]], h)

        def copy(a, k, owner, h, to, own_src=False):
            return pltpu.make_async_remote_copy(
                src_ref=part(ins[a], h) if own_src else block(a, owner, h), dst_ref=block(a, owner, h),
                send_sem=send_sems.at[a, k], recv_sem=recv_sems.at[a, k], device_id=to, device_id_type=MESH)

        def other(p):
            return (p[0], p[1], 1 - c)

        mine = [pltpu.make_async_copy(ins[a], block(a, me, None), local_sems.at[a]) for a in range(na)]
        for cp in mine:
            cp.start()
        sent = []
        for a in range(na):
            sent += [copy(a, XN0, me, 0, xn, True), copy(a, YN1, me, 1, yn, True),
                     copy(a, XN1, me, 1, xn, True), copy(a, YN0, me, 0, yn, True)]
        sent += [copy(a, SIB, me, None, sib, True) for a in range(na)]
        for cp in sent:
            cp.start()

        def landed(a, k, owner, h, then):
            copy(a, k, owner, h, me).wait_recv()
            for k2, to in then + [(D2D[k], sib)]:
                cp = copy(a, k2, owner, h, to)
                cp.start()
                sent.append(cp)

        for a in range(na):
            landed(a, XN0, xn, 0, [(VIA_Y, yn)])
            landed(a, YN1, yn, 1, [(VIA_X, xn)])
            landed(a, XN1, xn, 1, [])
            landed(a, YN0, yn, 0, [])
        for a in range(na):
            landed(a, VIA_Y, dg, 0, [])
            landed(a, VIA_X, dg, 1, [])
        for a in range(na):
            copy(a, SIB, sib, None, me).wait_recv()
            for k, owner, h in ((XN0, xn, 0), (XN1, xn, 1), (YN1, yn, 1), (YN0, yn, 0), (VIA_Y, dg, 0), (VIA_X, dg, 1)):
                copy(a, D2D[k], other(owner), h, me).wait_recv()
        for cp in sent:
            cp.wait_send()
        for cp in mine:
            cp.wait()

    return _sequencer_call(
        body, name, collective_id,
        [jax.ShapeDtypeStruct((NDEV,) + s.shape, s.dtype) for s in shards],
        [pltpu.SemaphoreType.DMA((na, 13)), pltpu.SemaphoreType.DMA((na, 13)), pltpu.SemaphoreType.DMA((na,))])(*shards)


def _sequencer_call(body, name, collective_id, out_type, scratch_types):
    return pl.kernel(
        body, name=name, out_type=out_type,
        mesh=plsc.ScalarSubcoreMesh(axis_name="sequencer", num_cores=1),
        scratch_types=scratch_types,
        compiler_params=pltpu.CompilerParams(collective_id=collective_id))


def _exchange_sibling(grads, name, collective_id):
    na = len(grads)

    def body(*refs):
        ins, outs = refs[:na], refs[na:2 * na]
        send_sems, recv_sems = refs[2 * na:]
        x, y, c, _ = _place()
        _handshake([(x, y, 1 - c)])
        cps = []
        for a in range(na):
            for k in range(4):
                cps.append(pltpu.make_async_remote_copy(
                    src_ref=ins[a].at[2 * k + (1 - c)], dst_ref=outs[a].at[k],
                    send_sem=send_sems.at[a, k], recv_sem=recv_sems.at[a, k],
                    device_id=(x, y, 1 - c), device_id_type=MESH))
        for cp in cps:
            cp.start()
        for cp in cps:
            cp.wait()

    return _sequencer_call(
        body, name, collective_id,
        [jax.ShapeDtypeStruct((4,) + g.shape[1:], g.dtype) for g in grads],
        [pltpu.SemaphoreType.DMA((na, 4)), pltpu.SemaphoreType.DMA((na, 4))])(*grads)


def _row_tile(rows, cols):
    for t in (512, 256, 176, 128, 64, 32, 16):
        if rows % t == 0 and t * cols * 4 <= (2 << 20):
            return t
    raise ValueError((rows, cols))


def _chip_sum(place, g, got, name):
    _, r, c = g.shape
    tm = r

    def body(pos_ref, g_ref, got_ref, o_ref):
        o_ref[...] = (g_ref[...].astype(F32) + got_ref[...].astype(F32)).astype(BF16)

    def chip(j, pos):
        return 2 * (pos[0] ^ jnp.where(j == 1, 0, 1)) + (pos[1] ^ jnp.where(j == 0, 0, 1))

    return pl.pallas_call(
        body, name=name,
        grid_spec=pltpu.PrefetchScalarGridSpec(
            num_scalar_prefetch=1, grid=(3, r // tm),
            in_specs=[pl.BlockSpec((None, tm, c), lambda j, i, pos: (2 * chip(j, pos) + pos[2], i, 0)),
                      pl.BlockSpec((None, tm, c), lambda j, i, pos: (chip(j, pos), i, 0))],
            out_specs=pl.BlockSpec((None, tm, c), lambda j, i, pos: (j, i, 0))),
        out_shape=jax.ShapeDtypeStruct((3, r, c), BF16),
        compiler_params=_cp(("parallel", "parallel")),
    )(place, g, got)


def _exchange_chips(sums, name, collective_id):
    na = len(sums)

    def body(*refs):
        ins, outs = refs[:na], refs[na:2 * na]
        send_sems, recv_sems = refs[2 * na:]
        x, y, c, chips = _place()
        _handshake([(*chip, c) for chip in chips])
        cps = []
        for a in range(na):
            for j, chip in enumerate(chips):
                cps.append(pltpu.make_async_remote_copy(
                    src_ref=ins[a].at[j], dst_ref=outs[a].at[j],
                    send_sem=send_sems.at[a, j], recv_sem=recv_sems.at[a, j],
                    device_id=(*chip, c), device_id_type=MESH))
        for cp in cps:
            cp.start()
        for cp in cps:
            cp.wait()

    return _sequencer_call(
        body, name, collective_id,
        [jax.ShapeDtypeStruct((3,) + s.shape[1:], s.dtype) for s in sums],
        [pltpu.SemaphoreType.DMA((na, 3)), pltpu.SemaphoreType.DMA((na, 3))])(*sums)


def _exchange_stats(stats, collective_id):
    def body(st_in, st_out, st_send, st_recv, local_sem):
        x, y, c, _ = _place()
        me_idx = 4 * x + 2 * y + c
        peers = [(x ^ ((k >> 2) & 1), y ^ ((k >> 1) & 1), c ^ (k & 1)) for k in range(1, 8)]
        _handshake(peers)
        mine = pltpu.make_async_copy(st_in, st_out.at[me_idx], local_sem)
        mine.start()
        cps = [pltpu.make_async_remote_copy(
            src_ref=st_in, dst_ref=st_out.at[me_idx], send_sem=st_send.at[k], recv_sem=st_recv.at[k],
            device_id=peer, device_id_type=MESH) for k, peer in enumerate(peers)]
        for cp in cps:
            cp.start()
        for cp in cps:
            cp.wait()
        mine.wait()

    return _sequencer_call(
        body, "exchange_stats", collective_id,
        jax.ShapeDtypeStruct((NDEV,) + stats.shape, stats.dtype),
        [pltpu.SemaphoreType.DMA((7,)), pltpu.SemaphoreType.DMA((7,)), pltpu.SemaphoreType.DMA])(stats)


class _Reduction:
    def __init__(self, place, first_collective_id, state):
        self.place = place
        self.ids = iter(range(first_collective_id, 32))
        self.state = state
        self.groups = {}
        self.updates = {}

    def next_id(self):
        return next(self.ids)

    def start(self, group, grads):
        got = _exchange_sibling(grads, "sibling_exchange_" + group[0], self.next_id())
        self.groups[group[0]] = dict(names=group, grads=grads, got=got)

    def local(self, name, first=()):
        grp = self.groups[name]
        grads = lax.optimization_barrier((tuple(grp["grads"]), tuple(first)))[0]
        grp["sums"] = [_chip_sum(self.place, g, s, "chip_sum_" + n)
                       for g, s, n in zip(grads, grp["got"], grp["names"])]
        grp["chips"] = _exchange_chips(grp["sums"], "chip_exchange_" + name, self.next_id())
        return grp["sums"]

    def landed(self, name):
        return list(self.groups[name]["chips"])

    def update(self, name):
        if name not in self.updates:
            grp = next(g for g in self.groups.values() if name in g["names"])
            k = grp["names"].index(name)
            self.updates[name] = _shard_update(self.place, *self.state[name], grp["grads"][k], grp["got"][k],
                                               grp["chips"][k], "update_" + name)
        return list(self.updates[name])


def _adamw(w, g, m, v):
    m = ADAM_B1 * m + (1.0 - ADAM_B1) * g
    v = ADAM_B2 * v + (1.0 - ADAM_B2) * (g * g)
    m_hat = m / (1.0 - ADAM_B1 ** ADAM_STEP)
    v_hat = v / (1.0 - ADAM_B2 ** ADAM_STEP)
    delta = -ADAM_LR * (m_hat / (jnp.sqrt(v_hat) + ADAM_EPS) + ADAM_WD * w)
    return delta, m, v


def _shard_update(place, w, m, v, g, got_sib, got_chips, name):
    r, c = w.shape
    tm = _row_tile(r, c)

    def body(pos_ref, w_ref, m_ref, v_ref, g_ref, s_ref, c_ref, go_ref, d_ref, mo_ref, vo_ref):
        grad = g_ref[...].astype(F32) + s_ref[...].astype(F32)
        for j in range(3):
            grad = grad + c_ref[j].astype(F32)
        delta, mn, vn = _adamw(w_ref[...], grad, m_ref[...], v_ref[...])
        go_ref[...] = grad
        d_ref[...] = delta
        mo_ref[...] = mn
        vo_ref[...] = vn

    row = pl.BlockSpec((tm, c), lambda i, pos: (i, 0))
    return pl.pallas_call(
        body, name=name,
        grid_spec=pltpu.PrefetchScalarGridSpec(
            num_scalar_prefetch=1, grid=(r // tm,),
            in_specs=[row, row, row,
                      pl.BlockSpec((None, tm, c), lambda i, pos: (4 * pos[0] + 2 * pos[1] + pos[2], i, 0)),
                      pl.BlockSpec((None, tm, c), lambda i, pos: (2 * pos[0] + pos[1], i, 0)),
                      pl.BlockSpec((3, tm, c), lambda i, pos: (0, i, 0))],
            out_specs=[row, row, row, row]),
        out_shape=[jax.ShapeDtypeStruct((r, c), F32)] * 4,
        compiler_params=_cp(("parallel",)),
    )(place, w, m, v, g, got_sib, got_chips)


def _small_update(stats_all, ws, ms, vs):
    def body(st_ref, w_ref, m_ref, v_ref, go_ref, d_ref, mo_ref, vo_ref):
        grad = st_ref[0]
        for k in range(1, NDEV):
            grad = grad + st_ref[k]
        delta, mn, vn = _adamw(w_ref[...], grad, m_ref[...], v_ref[...])
        go_ref[...] = grad
        d_ref[...] = delta
        mo_ref[...] = mn
        vo_ref[...] = vn

    return pl.pallas_call(
        body, name="small_update",
        out_shape=[jax.ShapeDtypeStruct((8, D), F32)] * 4,
        compiler_params=_cp(),
    )(stats_all, ws, ms, vs)


def kernel(x, norm_mix_w, w_in, w_out, norm_ffn_w, w_gate, w_up, w_down, norm_final_w, loss_target, m_norm_mix_w, m_w_in, m_w_out, m_norm_ffn_w, m_w_gate, m_w_up, m_w_down, m_norm_final_w, v_norm_mix_w, v_w_in, v_w_out, v_norm_ffn_w, v_w_gate, v_w_up, v_w_down, v_norm_final_w):
    tr = {"w_gate", "w_up"}
    names = ["w_in", "w_out", "w_gate", "w_up", "w_down"]

    def view(a, n):
        return a[0].T if n in tr else a[0]

    big_w = [view(a, n) for a, n in zip([w_in, w_out, w_gate, w_up, w_down], names)]
    big_m = [view(a, n) for a, n in zip([m_w_in, m_w_out, m_w_gate, m_w_up, m_w_down], names)]
    big_v = [view(a, n) for a, n in zip([v_w_in, v_w_out, v_w_gate, v_w_up, v_w_down], names)]

    shards = [_cast_bf16(w, "cast_" + n) for w, n in zip(big_w, names)]
    (win,) = _all_gather(shards[0:1], "all_gather_w_in", 1)
    wout, wg, wu = _all_gather(shards[1:4], "all_gather_out_gate_up", 2)
    (wd,) = _all_gather(shards[4:5], "all_gather_w_down", 3)
    nw3 = norm_final_w.reshape(1, D)
    place = jnp.stack([lax.axis_index("x"), lax.axis_index("y"), lax.axis_index("c")]).astype(jnp.int32)
    red = _Reduction(place, 4, {n: (w, m, v) for n, w, m, v in zip(names, big_w, big_m, big_v)})
    stats, gx, *_ = _local_step(
        x[0], loss_target[0], norm_mix_w, norm_ffn_w, nw3, win, wout.reshape(D, D), wg, wu, wd, red)
    stats_all = _exchange_stats(stats, red.next_id())
    upd = [red.update(n) for n in names]
    stats_all = lax.optimization_barrier((stats_all, tuple(upd[0])))[0]

    def rows(a, b, c):
        return jnp.concatenate([a.reshape(1, D), b.reshape(1, D), c.reshape(1, D), jnp.zeros((5, D), F32)], axis=0)

    sg, sd, sm, sv = _small_update(stats_all, rows(norm_mix_w, norm_ffn_w, norm_final_w),
                                   rows(m_norm_mix_w, m_norm_ffn_w, m_norm_final_w),
                                   rows(v_norm_mix_w, v_norm_ffn_w, v_norm_final_w))
    loss = sg[3, 0]

    def outs(k, small):
        big = [(u[k].T if n in tr else u[k])[None] for u, n in zip(upd, names)]
        return [small[0:1], big[0], big[1], small[1:2], big[2], big[3], big[4], small[2]]

    return (loss, gx[None], *outs(0, sg), *outs(1, sd), *outs(2, sm), *outs(3, sv))
```

```python
import functools
import math

import numpy as np
import jax
import jax.numpy as jnp
from jax import lax
from jax.experimental import pallas as pl
from jax.experimental.pallas import tpu as pltpu
from jax.experimental.pallas import tpu_sc as plsc

F32 = jnp.float32
BF16 = jnp.bfloat16

S = 2048
D = 2048
NDEV = 8
N_IN = 7168 // NDEV
N_FF = 5632 // NDEV
NFG, N_FG = NDEV // 2, 2 * N_FF
N_OUT = 2048 // NDEV
AH, AHD = 8, 128
RH, RHD = 4, 256
CH = 128
NB = S // CH
EPS = 1e-6
PATTERNS = ((1, 16), (4, 4), (16, 1))
NEG = -1e30
VMEM_LIMIT = 56 * 1024 * 1024

ADAM_LR, ADAM_B1, ADAM_B2, ADAM_EPS, ADAM_WD, ADAM_STEP = 0.001, 0.9, 0.999, 1e-08, 0.01, 10
MESH = pl.DeviceIdType.MESH


def _cp(sem=None):
    return pltpu.CompilerParams(dimension_semantics=sem, vmem_limit_bytes=VMEM_LIMIT)


def _dot(a, b):
    return jnp.dot(a, b, preferred_element_type=F32)


def _dot_nt(a, b):
    return lax.dot_general(a, b, (((1,), (1,)), ((), ())), preferred_element_type=F32)


def _dot_tn(a, b):
    return lax.dot_general(a, b, (((0,), (0,)), ((), ())), preferred_element_type=F32)


def _sigmoid(x):
    return 0.5 * jnp.tanh(0.5 * x) + 0.5


def _cast_bf16(w, name):
    r, c = w.shape
    tm = r if r <= 1024 else 512

    def body(w_ref, o_ref):
        o_ref[...] = w_ref[...].astype(BF16)

    return pl.pallas_call(
        body, name=name, grid=(r // tm,),
        in_specs=[pl.BlockSpec((tm, c), lambda i: (i, 0))],
        out_specs=pl.BlockSpec((tm, c), lambda i: (i, 0)),
        out_shape=jax.ShapeDtypeStruct((r, c), BF16),
        compiler_params=_cp(("parallel",)),
    )(w)


def _rms_fwd(x, nw):
    tm = 256

    def body(x_ref, w_ref, h_ref, r_ref):
        xs = x_ref[...]
        r = lax.rsqrt(jnp.mean(xs * xs, axis=-1, keepdims=True) + EPS)
        h_ref[...] = ((xs * r) * w_ref[...]).astype(BF16)
        r_ref[...] = r

    return pl.pallas_call(
        body, name="rms_fwd", grid=(S // tm,),
        in_specs=[pl.BlockSpec((tm, D), lambda i: (i, 0)), pl.BlockSpec((1, D), lambda i: (0, 0))],
        out_specs=[pl.BlockSpec((tm, D), lambda i: (i, 0)), pl.BlockSpec((tm, 1), lambda i: (i, 0))],
        out_shape=[jax.ShapeDtypeStruct((S, D), BF16), jax.ShapeDtypeStruct((S, 1), F32)],
        compiler_params=_cp(("parallel",)),
    )(x, nw)


def _rms_bwd_tile(dh, xs, r, nw):
    dnw = jnp.sum(dh * (xs * r), axis=0, keepdims=True)
    gy = dh * nw
    dx = r * gy - xs * ((r * r * r) * jnp.mean(gy * xs, axis=-1, keepdims=True))
    return dx, dnw


def _proj(h1, win):
    tm = 1024

    def body(a_ref, w_ref, o_ref):
        o_ref[...] = _dot(a_ref[...], w_ref[...])

    return pl.pallas_call(
        body, name="proj", grid=(NDEV, S // tm),
        in_specs=[pl.BlockSpec((tm, D), lambda p, m: (m, 0)),
                  pl.BlockSpec((None, D, N_IN), lambda p, m: (p, 0, 0))],
        out_specs=pl.BlockSpec((tm, N_IN), lambda p, m: (m, p)),
        out_shape=jax.ShapeDtypeStruct((S, NDEV * N_IN), F32),
        compiler_params=_cp(("parallel", "parallel")),
    )(h1, win)


def _out_proj_rms(x, ma, mr, wout, nw):
    tm = 256
    half = D // 2

    def body(x_ref, ma_ref, mr_ref, w_ref, nw_ref, x2_ref, h_ref, r_ref):
        acc = _dot(ma_ref[...], w_ref[0:half, :]) + _dot(mr_ref[...], w_ref[half:D, :])
        x2 = x_ref[...] + acc
        r = lax.rsqrt(jnp.mean(x2 * x2, axis=-1, keepdims=True) + EPS)
        x2_ref[...] = x2
        h_ref[...] = ((x2 * r) * nw_ref[...]).astype(BF16)
        r_ref[...] = r

    return pl.pallas_call(
        body, name="out_proj_rms", grid=(S // tm,),
        in_specs=[pl.BlockSpec((tm, D), lambda i: (i, 0)),
                  pl.BlockSpec((tm, half), lambda i: (i, 0)),
                  pl.BlockSpec((tm, half), lambda i: (i, 0)),
                  pl.BlockSpec((D, D), lambda i: (0, 0)),
                  pl.BlockSpec((1, D), lambda i: (0, 0))],
        out_specs=[pl.BlockSpec((tm, D), lambda i: (i, 0)), pl.BlockSpec((tm, D), lambda i: (i, 0)),
                   pl.BlockSpec((tm, 1), lambda i: (i, 0))],
        out_shape=[jax.ShapeDtypeStruct((S, D), F32), jax.ShapeDtypeStruct((S, D), BF16),
                   jax.ShapeDtypeStruct((S, 1), F32)],
        compiler_params=_cp(("parallel",)),
    )(x, ma, mr, wout, nw)


def _ffn_up(h2, wg, wu):
    tm = 512

    def body(h_ref, wg_ref, wu_ref, g_ref, u_ref, a_ref):
        h = h_ref[...]
        g = _dot_nt(h, wg_ref[...])
        u = _dot_nt(h, wu_ref[...])
        g_ref[...] = g
        u_ref[...] = u
        a_ref[...] = ((g * _sigmoid(g)) * u).astype(BF16)

    blk = pl.BlockSpec((None, tm, N_FG), lambda p, m: (p, m, 0))
    wblk = pl.BlockSpec((None, N_FG, D), lambda p, m: (p, 0, 0))
    return pl.pallas_call(
        body, name="ffn_up", grid=(NFG, S // tm),
        in_specs=[pl.BlockSpec((tm, D), lambda p, m: (m, 0)), wblk, wblk],
        out_specs=[blk, blk, blk],
        out_shape=[jax.ShapeDtypeStruct((NFG, S, N_FG),F32), jax.ShapeDtypeStruct((NFG, S, N_FG),F32),
                   jax.ShapeDtypeStruct((NFG, S, N_FG),BF16)],
        compiler_params=_cp(("parallel", "parallel")),
    )(h2, wg, wu)


def _ffn_down_loss(x2, a, wd, nw, tgt):
    tm = 256

    def body(x2_ref, a_ref, w_ref, nw_ref, t_ref, dx_ref, dxb_ref, st_ref, acc_ref):
        m, p = pl.program_id(0), pl.program_id(1)

        @pl.when(p == 0)
        def _():
            acc_ref[...] = jnp.zeros_like(acc_ref)

        @pl.when((p == 0) & (m == 0))
        def _():
            st_ref[...] = jnp.zeros_like(st_ref)

        acc_ref[...] += _dot(a_ref[...], w_ref[...])

        @pl.when(p == NFG - 1)
        def _():
            x3 = x2_ref[...] + acc_ref[...]
            nwv = nw_ref[...]
            r = lax.rsqrt(jnp.mean(x3 * x3, axis=-1, keepdims=True) + EPS)
            y = (x3 * r) * nwv
            err = y - t_ref[...]
            loss = 0.5 * jnp.sum(jnp.mean(err * err, axis=-1, keepdims=True), axis=0, keepdims=True)
            dy = err * (1.0 / D)
            dx, dnw = _rms_bwd_tile(dy, x3, r, nwv)
            dx_ref[...] = dx
            dxb_ref[...] = dx.astype(BF16)
            st_ref[0:1, :] += dnw
            st_ref[1:2, :] += jnp.broadcast_to(loss, (1, D))

    return pl.pallas_call(
        body, name="ffn_down_loss", grid=(S // tm, NFG),
        in_specs=[pl.BlockSpec((tm, D), lambda m, p: (m, 0)),
                  pl.BlockSpec((None, tm, N_FG), lambda m, p: (p, m, 0)),
                  pl.BlockSpec((None, N_FG, D), lambda m, p: (p, 0, 0)),
                  pl.BlockSpec((1, D), lambda m, p: (0, 0)),
                  pl.BlockSpec((tm, D), lambda m, p: (m, 0))],
        out_specs=[pl.BlockSpec((tm, D), lambda m, p: (m, 0)), pl.BlockSpec((tm, D), lambda m, p: (m, 0)),
                   pl.BlockSpec((8, D), lambda m, p: (0, 0))],
        out_shape=[jax.ShapeDtypeStruct((S, D), F32), jax.ShapeDtypeStruct((S, D), BF16),
                   jax.ShapeDtypeStruct((8, D), F32)],
        scratch_shapes=[pltpu.VMEM((tm, D), F32)],
        compiler_params=_cp(("arbitrary", "arbitrary")),
    )(x2, a, wd, nw, tgt)


def _ffn_down_bwd(dx3b, wd, g, u):
    tm = 512

    def body(dx_ref, w_ref, g_ref, u_ref, dg_ref, du_ref):
        da = _dot_nt(dx_ref[...], w_ref[...])
        gv = g_ref[...]
        sg = _sigmoid(gv)
        silu = gv * sg
        dg_ref[...] = ((da * u_ref[...]) * (sg * (1.0 + gv * (1.0 - sg)))).astype(BF16)
        du_ref[...] = (da * silu).astype(BF16)

    blk = pl.BlockSpec((None, tm, N_FG), lambda p, m: (p, m, 0))
    return pl.pallas_call(
        body, name="ffn_down_bwd", grid=(NFG, S // tm),
        in_specs=[pl.BlockSpec((tm, D), lambda p, m: (m, 0)),
                  pl.BlockSpec((None, N_FG, D), lambda p, m: (p, 0, 0)), blk, blk],
        out_specs=[blk, blk],
        out_shape=[jax.ShapeDtypeStruct((NFG, S, N_FG),BF16), jax.ShapeDtypeStruct((NFG, S, N_FG),BF16)],
        compiler_params=_cp(("parallel", "parallel")),
    )(dx3b, wd, g, u)


def _ffn_up_bwd(dg, du, wg, wu, dres, xs, r, nw):
    tm = 256

    def body(dg_ref, du_ref, wg_ref, wu_ref, dres_ref, x_ref, r_ref, nw_ref, dx_ref, dxb_ref, st_ref, acc_ref):
        m, p = pl.program_id(0), pl.program_id(1)

        @pl.when(p == 0)
        def _():
            acc_ref[...] = jnp.zeros_like(acc_ref)

        @pl.when((p == 0) & (m == 0))
        def _():
            st_ref[...] = jnp.zeros_like(st_ref)

        acc_ref[...] += _dot(dg_ref[...], wg_ref[...]) + _dot(du_ref[...], wu_ref[...])

        @pl.when(p == NFG - 1)
        def _():
            dx, dnw = _rms_bwd_tile(acc_ref[...], x_ref[...], r_ref[...], nw_ref[...])
            dx = dres_ref[...] + dx
            dx_ref[...] = dx
            dxb_ref[...] = dx.astype(BF16)
            st_ref[0:1, :] += dnw

    blk = pl.BlockSpec((None, tm, N_FG), lambda m, p: (p, m, 0))
    wblk = pl.BlockSpec((None, N_FG, D), lambda m, p: (p, 0, 0))
    row = pl.BlockSpec((tm, D), lambda m, p: (m, 0))
    return pl.pallas_call(
        body, name="ffn_up_bwd", grid=(S // tm, NFG),
        in_specs=[blk, blk, wblk, wblk, row, row, pl.BlockSpec((tm, 1), lambda m, p: (m, 0)),
                  pl.BlockSpec((1, D), lambda m, p: (0, 0))],
        out_specs=[row, row, pl.BlockSpec((8, D), lambda m, p: (0, 0))],
        out_shape=[jax.ShapeDtypeStruct((S, D), F32), jax.ShapeDtypeStruct((S, D), BF16),
                   jax.ShapeDtypeStruct((8, D), F32)],
        scratch_shapes=[pltpu.VMEM((tm, D), F32)],
        compiler_params=_cp(("arbitrary", "arbitrary")),
    )(dg, du, wg, wu, dres, xs, r, nw)


def _out_proj_bwd(dx2b, wout):
    tm = 256

    def body(dx_ref, w_ref, o_ref):
        o_ref[...] = _dot_nt(dx_ref[...], w_ref[...])

    return pl.pallas_call(
        body, name="out_proj_bwd", grid=(S // tm,),
        in_specs=[pl.BlockSpec((tm, D), lambda i: (i, 0)), pl.BlockSpec((D, D), lambda i: (0, 0))],
        out_specs=pl.BlockSpec((tm, D), lambda i: (i, 0)),
        out_shape=jax.ShapeDtypeStruct((S, D), F32),
        compiler_params=_cp(("parallel",)),
    )(dx2b, wout)


def _in_proj_bwd(dproj, win, dres, xs, r, nw):
    tm = 512

    def body(dp_ref, w_ref, dres_ref, x_ref, r_ref, nw_ref, dx_ref, st_ref, acc_ref):
        m, p = pl.program_id(0), pl.program_id(1)

        @pl.when(p == 0)
        def _():
            acc_ref[...] = jnp.zeros_like(acc_ref)

        @pl.when((p == 0) & (m == 0))
        def _():
            st_ref[...] = jnp.zeros_like(st_ref)

        acc_ref[...] += _dot_nt(dp_ref[...], w_ref[...])

        @pl.when(p == NDEV - 1)
        def _():
            dx, dnw = _rms_bwd_tile(acc_ref[...], x_ref[...], r_ref[...], nw_ref[...])
            dx_ref[...] = dres_ref[...] + dx
            st_ref[0:1, :] += dnw

    row = pl.BlockSpec((tm, D), lambda m, p: (m, 0))
    return pl.pallas_call(
        body, name="in_proj_bwd", grid=(S // tm, NDEV),
        in_specs=[pl.BlockSpec((tm, N_IN), lambda m, p: (m, p)),
                  pl.BlockSpec((None, D, N_IN), lambda m, p: (p, 0, 0)),
                  row, row, pl.BlockSpec((tm, 1), lambda m, p: (m, 0)),
                  pl.BlockSpec((1, D), lambda m, p: (0, 0))],
        out_specs=[row, pl.BlockSpec((8, D), lambda m, p: (0, 0))],
        out_shape=[jax.ShapeDtypeStruct((S, D), F32), jax.ShapeDtypeStruct((8, D), F32)],
        scratch_shapes=[pltpu.VMEM((tm, D), F32)],
        compiler_params=_cp(("arbitrary", "arbitrary")),
    )(dproj, win, dres, xs, r, nw)


def _wgrad_in(h1, dproj):
    def body(a_ref, d_ref, o_ref):
        o_ref[...] = _dot_tn(a_ref[...], d_ref[...]).astype(BF16)

    return pl.pallas_call(
        body, name="wgrad_in", grid=(NDEV,),
        in_specs=[pl.BlockSpec((S, D), lambda p: (0, 0)), pl.BlockSpec((S, N_IN), lambda p: (0, p))],
        out_specs=pl.BlockSpec((None, D, N_IN), lambda p: (p, 0, 0)),
        out_shape=jax.ShapeDtypeStruct((NDEV, D, N_IN), BF16),
        compiler_params=_cp(("parallel",)),
    )(h1, dproj)


def _wgrad_rows(a3, dy, name):
    def body(a_ref, d_ref, o_ref):
        o_ref[...] = _dot_tn(a_ref[...], d_ref[...]).astype(BF16)

    return pl.pallas_call(
        body, name=name, grid=(NFG,),
        in_specs=[pl.BlockSpec((None, S, N_FG), lambda p: (p, 0, 0)), pl.BlockSpec((S, D), lambda p: (0, 0))],
        out_specs=pl.BlockSpec((None, N_FG, D), lambda p: (p, 0, 0)),
        out_shape=jax.ShapeDtypeStruct((NFG, N_FG, D), BF16),
        compiler_params=_cp(("parallel",)),
    )(a3, dy).reshape(NDEV, N_FF, D)


def _wgrad_out(ma, mr, dx2b):
    half = D // 2
    per = half // N_OUT

    def body(ma_ref, mr_ref, d_ref, o_ref):
        p = pl.program_id(0)

        @pl.when(p < per)
        def _():
            o_ref[...] = _dot_tn(ma_ref[...], d_ref[...]).astype(BF16)

        @pl.when(p >= per)
        def _():
            o_ref[...] = _dot_tn(mr_ref[...], d_ref[...]).astype(BF16)

    return pl.pallas_call(
        body, name="wgrad_out", grid=(NDEV,),
        in_specs=[pl.BlockSpec((S, N_OUT), lambda p: (0, jnp.minimum(p, per - 1))),
                  pl.BlockSpec((S, N_OUT), lambda p: (0, jnp.maximum(p - per, 0))),
                  pl.BlockSpec((S, D), lambda p: (0, 0))],
        out_specs=pl.BlockSpec((None, N_OUT, D), lambda p: (p, 0, 0)),
        out_shape=jax.ShapeDtypeStruct((NDEV, N_OUT, D), BF16),
        compiler_params=_cp(("parallel",)),
    )(ma, mr, dx2b)


def _attn_consts():
    c = np.zeros((AH, 8, AHD), np.float32)
    for h in range(AH):
        c[h, :, :] = 2.0 ** (-(h + 1))
    return jnp.asarray(c)


def _permute_in(dst, src, d, cast=None):
    ln = S // d
    for rr in range(d):
        v = src[pl.ds(rr, ln, stride=d), :] if d > 1 else src[...]
        dst[rr * ln:(rr + 1) * ln, :] = v if cast is None else v.astype(cast)


def _attn_masks():
    qi = lax.broadcasted_iota(jnp.int32, (CH, CH), 0)
    kj = lax.broadcasted_iota(jnp.int32, (CH, CH), 1)
    dist_c = (qi - kj).astype(F32)
    dist_p = (qi - kj + CH).astype(F32)
    return (qi >= kj)[None], (kj >= qi)[None], dist_c[None], dist_p[None]


GB = 8


def _bdot_nt(a, b):
    return lax.dot_general(a, b, (((2,), (2,)), ((0,), (0,))), preferred_element_type=F32)


def _bdot(a, b):
    return lax.dot_general(a, b, (((2,), (1,)), ((0,), (0,))), preferred_element_type=F32)


def _bdot_tn(a, b):
    return lax.dot_general(a, b, (((1,), (1,)), ((0,), (0,))), preferred_element_type=F32)


def _shift_block(dst, src):
    dst[0:CH, :] = jnp.zeros((CH, AHD), dst.dtype)
    dst[CH:S, :] = src[0:S - CH, :]


def _has_prev(g, nb):
    blk = lax.broadcasted_iota(jnp.int32, (GB, 1, 1), 0) + g * GB
    return (blk & (nb - 1)) != 0


def _blocks(ref, g):
    return ref[g * GB * CH:(g + 1) * GB * CH, :].reshape(GB, CH, AHD)


def _attn_fwd(proj):
    scale = 1.0 / math.sqrt(AHD)

    def body(c_ref, q_ref, k_ref, v_ref, o_ref, ob_ref, lse_ref, qd, kd, vd, kps, vps, od, ld, *nat):
        onat, lnat = nat[0:3], nat[3:6]
        slope = c_ref[0:1, :]
        mask_c, mask_p, dist_c, dist_p = _attn_masks()
        for pi, (d, nb) in enumerate(PATTERNS):
            _permute_in(qd, q_ref, d, BF16)
            _permute_in(kd, k_ref, d, BF16)
            _permute_in(vd, v_ref, d, BF16)
            if nb > 1:
                _shift_block(kps, kd)
                _shift_block(vps, vd)
            bias_c = -(slope * float(d)) * dist_c
            bias_p = -(slope * float(d)) * dist_p
            for g in range(NB // GB):
                q3, k3, v3 = _blocks(qd, g), _blocks(kd, g), _blocks(vd, g)
                s_c = jnp.where(mask_c, _bdot_nt(q3, k3) * scale + bias_c, NEG)
                mx = jnp.max(s_c, axis=-1, keepdims=True)
                if nb > 1:
                    kp3, vp3 = _blocks(kps, g), _blocks(vps, g)
                    s_p = jnp.where(jnp.logical_and(mask_p, _has_prev(g, nb)),
                                    _bdot_nt(q3, kp3) * scale + bias_p, NEG)
                    mx = jnp.maximum(mx, jnp.max(s_p, axis=-1, keepdims=True))
                    l = (jnp.sum(jnp.exp(s_c - mx), axis=-1, keepdims=True)
                         + jnp.sum(jnp.exp(s_p - mx), axis=-1, keepdims=True))
                    lse = mx + jnp.log(l)
                    o3 = _bdot(jnp.exp(s_c - lse).astype(BF16), v3) + _bdot(jnp.exp(s_p - lse).astype(BF16), vp3)
                else:
                    l = jnp.sum(jnp.exp(s_c - mx), axis=-1, keepdims=True)
                    lse = mx + jnp.log(l)
                    o3 = _bdot(jnp.exp(s_c - lse).astype(BF16), v3)
                rows = slice(g * GB * CH, (g + 1) * GB * CH)
                od[rows, :] = o3.reshape(GB * CH, AHD)
                ld[rows, :] = jnp.broadcast_to(lse, (GB, CH, AHD)).reshape(GB * CH, AHD)
            ln = S // d
            for rr in range(d):
                if d > 1:
                    onat[pi][pl.ds(rr, ln, stride=d), :] = od[rr * ln:(rr + 1) * ln, :]
                    lnat[pi][pl.ds(rr, ln, stride=d), :] = ld[rr * ln:(rr + 1) * ln, :]
                else:
                    onat[pi][...] = od[...]
                    lnat[pi][...] = ld[...]
        l0, l1, l2 = lnat[0][...], lnat[1][...], lnat[2][...]
        mx = jnp.maximum(jnp.maximum(l0, l1), l2)
        e0, e1, e2 = jnp.exp(l0 - mx), jnp.exp(l1 - mx), jnp.exp(l2 - mx)
        den = e0 + e1 + e2
        out = (e0 / den) * onat[0][...] + (e1 / den) * onat[1][...] + (e2 / den) * onat[2][...]
        o_ref[...] = out
        ob_ref[...] = out.astype(BF16)
        lse_ref[...] = mx + jnp.log(den)

    def col(off):
        return pl.BlockSpec((S, AHD), lambda h: (0, off + h))

    return pl.pallas_call(
        body, name="attn_fwd", grid=(AH,),
        in_specs=[pl.BlockSpec((None, 8, AHD), lambda h: (h, 0, 0)), col(0), col(AH), col(2 * AH)],
        out_specs=[col(0), col(0), col(0)],
        out_shape=[jax.ShapeDtypeStruct((S, AH * AHD), F32), jax.ShapeDtypeStruct((S, AH * AHD), BF16),
                   jax.ShapeDtypeStruct((S, AH * AHD), F32)],
        scratch_shapes=[pltpu.VMEM((S, AHD), BF16) for _ in range(5)]
        + [pltpu.VMEM((S, AHD), F32) for _ in range(8)],
        compiler_params=_cp(("parallel",)),
    )(_attn_consts(), proj, proj, proj)


def _attn_bwd(proj, dmixed, o, lse):
    scale = 1.0 / math.sqrt(AHD)

    def body(c_ref, q_ref, k_ref, v_ref, do_ref, o_ref, lse_ref, dq_ref, dk_ref, dv_ref,
             qd, kd, vd, dod, kps, vps, lsd, dld, dqd, dkd, dvd, delta, aq, ak, av):
        slope = c_ref[0:1, :]
        mask_c, mask_p, dist_c, dist_p = _attn_masks()
        delta[...] = jnp.broadcast_to(jnp.sum(do_ref[...] * o_ref[...], axis=-1, keepdims=True), (S, AHD))
        for pi, (d, nb) in enumerate(PATTERNS):
            _permute_in(qd, q_ref, d, BF16)
            _permute_in(kd, k_ref, d, BF16)
            _permute_in(vd, v_ref, d, BF16)
            _permute_in(dod, do_ref, d, BF16)
            _permute_in(lsd, lse_ref, d)
            _permute_in(dld, delta, d)
            if nb > 1:
                _shift_block(kps, kd)
                _shift_block(vps, vd)
            bias_c = -(slope * float(d)) * dist_c
            bias_p = -(slope * float(d)) * dist_p
            for g in range(NB // GB):
                q3, k3, v3, do3 = _blocks(qd, g), _blocks(kd, g), _blocks(vd, g), _blocks(dod, g)
                ls, dl = _blocks(lsd, g), _blocks(dld, g)
                lo, hi = g * GB * CH, (g + 1) * GB * CH
                p_c = jnp.exp(jnp.where(mask_c, _bdot_nt(q3, k3) * scale + bias_c, NEG) - ls)
                ds_c = ((p_c * (_bdot_nt(do3, v3) - dl)) * scale).astype(BF16)
                dq3 = _bdot(ds_c, k3)
                dkd[lo:hi, :] = _bdot_tn(ds_c, q3).reshape(GB * CH, AHD)
                dvd[lo:hi, :] = _bdot_tn(p_c.astype(BF16), do3).reshape(GB * CH, AHD)
                if nb > 1:
                    kp3, vp3 = _blocks(kps, g), _blocks(vps, g)
                    p_p = jnp.exp(jnp.where(jnp.logical_and(mask_p, _has_prev(g, nb)),
                                            _bdot_nt(q3, kp3) * scale + bias_p, NEG) - ls)
                    ds_p = ((p_p * (_bdot_nt(do3, vp3) - dl)) * scale).astype(BF16)
                    dq3 = dq3 + _bdot(ds_p, kp3)
                    dkp = _bdot_tn(ds_p, q3).reshape(GB * CH, AHD)
                    dvp = _bdot_tn(p_p.astype(BF16), do3).reshape(GB * CH, AHD)
                    if g == 0:
                        dkd[0:hi - CH, :] += dkp[CH:, :]
                        dvd[0:hi - CH, :] += dvp[CH:, :]
                    else:
                        dkd[lo - CH:hi - CH, :] += dkp
                        dvd[lo - CH:hi - CH, :] += dvp
                dqd[lo:hi, :] = dq3.reshape(GB * CH, AHD)
            ln = S // d
            for acc, src in ((aq, dqd), (ak, dkd), (av, dvd)):
                if pi == 0:
                    acc[...] = src[...]
                else:
                    for rr in range(d):
                        acc[pl.ds(rr, ln, stride=d), :] += src[rr * ln:(rr + 1) * ln, :]
        dq_ref[...] = aq[...].astype(BF16)
        dk_ref[...] = ak[...].astype(BF16)
        dv_ref[...] = av[...].astype(BF16)

    def col(off):
        return pl.BlockSpec((S, AHD), lambda h: (0, off + h))

    return pl.pallas_call(
        body, name="attn_bwd", grid=(AH,),
        in_specs=[pl.BlockSpec((None, 8, AHD), lambda h: (h, 0, 0)), col(0), col(AH), col(2 * AH),
                  col(0), col(0), col(0)],
        out_specs=[col(0), col(0), col(0)],
        out_shape=[jax.ShapeDtypeStruct((S, AH * AHD), BF16)] * 3,
        scratch_shapes=[pltpu.VMEM((S, AHD), BF16) for _ in range(6)]
        + [pltpu.VMEM((S, AHD), F32) for _ in range(9)],
        compiler_params=_cp(("parallel",)),
    )(_attn_consts(), proj, proj, proj, dmixed, o, lse)


def _ret_consts():
    c = np.zeros((RH, 8, RHD), np.float32)
    for h in range(RH):
        c[h, :, :] = np.log(np.float32(1.0) - np.float32(2.0 ** (-5.0 - h)))
    return jnp.asarray(c)


def _ret_factors(lg):
    i = lax.broadcasted_iota(jnp.int32, (CH, CH), 0)
    j = lax.broadcasted_iota(jnp.int32, (CH, CH), 1)
    dif = (i - j).astype(F32)
    decay = jnp.where(dif >= 0, jnp.exp(lg[:, 0:CH] * jnp.maximum(dif, 0.0)), 0.0)
    row = lax.broadcasted_iota(jnp.int32, (CH, RHD), 0).astype(F32)
    zeta = jnp.exp(lg * (CH - 1.0 - row))
    xi = jnp.exp(lg * (row + 1.0))
    return decay, zeta, xi, jnp.exp(lg * float(CH))


CBK = 8
RSTEPS = NB // CBK


def _ret_specs(rev):
    off = 3 * AH * AHD // RHD
    rows = CBK * CH

    def ch(n):
        return (RSTEPS - 1 - n) if rev else n

    def col(k):
        return pl.BlockSpec((rows, RHD), lambda h, n: (ch(n), off + k * RH + h))

    own = pl.BlockSpec((rows, RHD), lambda h, n: (ch(n), h))
    state = pl.BlockSpec((None, CBK, RHD, RHD), lambda h, n: (h, ch(n), 0, 0))
    const = pl.BlockSpec((None, 8, RHD), lambda h, n: (h, 0, 0))
    dm = pl.BlockSpec((rows, RHD), lambda h, n: (ch(n), AH * AHD // RHD + h))
    return col, own, state, const, dm


def _chunks(x):
    return x.reshape(CBK, CH, RHD)


def _ret_fwd(proj):
    def body(c_ref, q_ref, k_ref, v_ref, g_ref, ret_ref, mr_ref, st_ref, r_acc):
        n = pl.program_id(1)

        @pl.when(n == 0)
        def _():
            r_acc[...] = jnp.zeros_like(r_acc)

        decay, zeta, xi, gch = _ret_factors(c_ref[0:1, :])
        q3 = _chunks(q_ref[...].astype(BF16))
        kc = _chunks(k_ref[...] * (1.0 / math.sqrt(RHD)))
        k3 = kc.astype(BF16)
        v3 = _chunks(v_ref[...].astype(BF16))
        kv3 = _bdot_tn((kc * zeta[None]).astype(BF16), v3)
        r = r_acc[...]
        for i in range(CBK):
            st_ref[i] = r.astype(BF16)
            r = r * gch + kv3[i]
        r_acc[...] = r
        scores = _bdot_nt(q3, k3) * decay[None]
        ret = (_bdot(scores.astype(BF16), v3) + _bdot(q3, st_ref[...]) * xi[None]).reshape(CBK * CH, RHD)
        ret_ref[...] = ret
        rr = lax.rsqrt(jnp.mean(ret * ret, axis=-1, keepdims=True) + EPS)
        gv = g_ref[...]
        mr_ref[...] = ((gv * _sigmoid(gv)) * (ret * rr)).astype(BF16)

    col, own, state, const, _ = _ret_specs(False)
    return pl.pallas_call(
        body, name="ret_fwd", grid=(RH, RSTEPS),
        in_specs=[const, col(0), col(1), col(2), col(3)],
        out_specs=[own, own, state],
        out_shape=[jax.ShapeDtypeStruct((S, RH * RHD), F32), jax.ShapeDtypeStruct((S, RH * RHD), BF16),
                   jax.ShapeDtypeStruct((RH, NB, RHD, RHD), BF16)],
        scratch_shapes=[pltpu.VMEM((RHD, RHD), F32)],
        compiler_params=_cp(("parallel", "arbitrary")),
    )(_ret_consts(), proj, proj, proj, proj)


def _ret_bwd(proj, ret, states, dmixed):
    def body(c_ref, q_ref, k_ref, v_ref, g_ref, ret_ref, st_ref, dm_ref, dq_ref, dk_ref, dv_ref, dg_ref, g_acc, gs):
        n = pl.program_id(1)

        @pl.when(n == 0)
        def _():
            g_acc[...] = jnp.zeros_like(g_acc)

        decay, zeta, xi, gch = _ret_factors(c_ref[0:1, :])
        ret_v = ret_ref[...]
        rr = lax.rsqrt(jnp.mean(ret_v * ret_v, axis=-1, keepdims=True) + EPS)
        gv = g_ref[...]
        sg = _sigmoid(gv)
        dmix = dm_ref[...]
        dg_ref[...] = ((dmix * (ret_v * rr)) * (sg * (1.0 + gv * (1.0 - sg)))).astype(BF16)
        dretn = dmix * (gv * sg)
        dret = _chunks(rr * dretn - ret_v * ((rr * rr * rr) * jnp.mean(dretn * ret_v, axis=-1, keepdims=True)))

        q3 = _chunks(q_ref[...].astype(BF16))
        kc = _chunks(k_ref[...] * (1.0 / math.sqrt(RHD)))
        k3 = kc.astype(BF16)
        v3 = _chunks(v_ref[...].astype(BF16))
        d3 = dret.astype(BF16)
        dxi = (dret * xi[None]).astype(BF16)
        kz = (kc * zeta[None]).astype(BF16)
        dr3 = _bdot_tn(q3, dxi)
        acc = g_acc[...]
        for i in reversed(range(CBK)):
            gs[i] = acc.astype(BF16)
            acc = dr3[i] + gch * acc
        g_acc[...] = acc
        g3 = gs[...]
        sc = (_bdot_nt(q3, k3) * decay[None]).astype(BF16)
        da = (_bdot_nt(d3, v3) * decay[None]).astype(BF16)
        dq = _bdot(da, k3) + _bdot_nt(dxi, st_ref[...])
        dkc = _bdot_tn(da, q3) + _bdot_nt(v3, g3) * zeta[None]
        dv = _bdot_tn(sc, d3) + _bdot(kz, g3)
        dq_ref[...] = dq.reshape(CBK * CH, RHD).astype(BF16)
        dk_ref[...] = (dkc * (1.0 / math.sqrt(RHD))).reshape(CBK * CH, RHD).astype(BF16)
        dv_ref[...] = dv.reshape(CBK * CH, RHD).astype(BF16)

    col, own, state, const, dm = _ret_specs(True)
    return pl.pallas_call(
        body, name="ret_bwd", grid=(RH, RSTEPS),
        in_specs=[const, col(0), col(1), col(2), col(3), own, state, dm],
        out_specs=[own, own, own, own],
        out_shape=[jax.ShapeDtypeStruct((S, RH * RHD), BF16)] * 4,
        scratch_shapes=[pltpu.VMEM((RHD, RHD), F32), pltpu.VMEM((CBK, RHD, RHD), BF16)],
        compiler_params=_cp(("parallel", "arbitrary")),
    )(_ret_consts(), proj, proj, proj, proj, ret, states, dmixed)


class _NoReduction:
    def start(self, group, grads):
        pass

    def local(self, name, first=()):
        return []

    def landed(self, name):
        return []

    def update(self, name):
        return []


def _local_step(x, tgt, nw1, nw2, nw3, win, wout, wg, wu, wd, red=None):
    red = red or _NoReduction()

    def after(values, first):
        return lax.optimization_barrier((tuple(values), tuple(first)))[0]

    wg, wu, wd = (w.reshape(NFG, N_FG, D) for w in (wg, wu, wd))
    h1, r1 = _rms_fwd(x, nw1)
    proj = _proj(h1, win)
    o, ma, lse = _attn_fwd(proj)
    ret, mr, states = _ret_fwd(proj)
    x2, h2, r2 = _out_proj_rms(x, ma, mr, wout, nw2)
    g, u, a = _ffn_up(h2, wg, wu)
    dx3, dx3b, st3 = _ffn_down_loss(x2, a, wd, nw3, tgt)

    dwd = _wgrad_rows(a, dx3b, "wgrad_down")
    red.start(["w_down"], [dwd])
    (dx3b,) = after([dx3b], [dwd])
    dg, du = _ffn_down_bwd(dx3b, wd, g, u)
    dg, du = after([dg, du], red.local("w_down", first=[dg]))
    dwg = _wgrad_rows(dg, h2, "wgrad_gate")
    red.start(["w_gate"], [dwg])
    (du,) = after([du], [dwg])
    dwu = _wgrad_rows(du, h2, "wgrad_up")
    red.start(["w_up"], [dwu])
    dg, du = after([dg, du], [dwu] + red.local("w_gate"))
    dx2, dx2b, st2 = _ffn_up_bwd(dg, du, wg, wu, dx3, x2, r2, nw2)
    (dx2b,) = after([dx2b], red.local("w_up", first=[dx2b] + red.landed("w_down")))
    dwo = _wgrad_out(ma, mr, dx2b)
    red.start(["w_out"], [dwo])
    (dx2b,) = after([dx2b], [dwo])
    dmixed = _out_proj_bwd(dx2b, wout)
    dqa, dka, dva = _attn_bwd(proj, dmixed, o, lse)
    (dmixed,) = after([dmixed], [dqa] + red.landed("w_gate"))
    dqr, dkr, dvr, dgr = _ret_bwd(proj, ret, states, dmixed)
    dproj = jnp.concatenate([dqa, dka, dva, dqr, dkr, dvr, dgr], axis=1)
    (dwi,) = after([_wgrad_in(h1, dproj)], red.landed("w_up"))
    red.start(["w_in"], [dwi])
    early = red.local("w_out", first=[dwi]) + red.update("w_down") + red.update("w_gate")
    (dproj,) = after([dproj], red.local("w_in", first=early))
    gx, st1 = _in_proj_bwd(dproj, win, dx2, x, r1, nw1)
    stats = jnp.concatenate([st1[0:1], st2[0:1], st3[0:2], jnp.zeros((4, D), F32)], axis=0)
    return stats, gx, dwi, dwo, dwg, dwu, dwd


def _place():
    x, y, c = lax.axis_index("x"), lax.axis_index("y"), lax.axis_index("c")
    return x, y, c, [(1 - x, y), (x, 1 - y), (1 - x, 1 - y)]


def _handshake(peers):
    barrier = pltpu.get_barrier_semaphore()
    for peer in peers:
        pl.semaphore_signal(barrier, inc=1, device_id=peer, device_id_type=MESH)
    pl.semaphore_wait(barrier, len(peers))


def _all_gather(shards, name, collective_id):
    na = len(shards)
    SIB, XN0, XN1, YN1, YN0, VIA_X, VIA_Y = 0, 1, 2, 3, 4, 5, 6
    D2D = {XN0: 7, XN1: 8, YN1: 9, YN0: 10, VIA_X: 11, VIA_Y: 12}

    def body(*refs):
        ins, outs = refs[:na], refs[na:2 * na]
        send_sems, recv_sems, local_sems = refs[2 * na:]
        x, y, c, _ = _place()
        me, sib = (x, y, c), (x, y, 1 - c)
        xn, yn, dg = (1 - x, y, c), (x, 1 - y, c), (1 - x, 1 - y, c)
        _handshake([sib, xn, yn])

        def part(ref, h):
            rows = ref.shape[0] // 2
            return ref if h is None else ref.at[pl.ds(h * rows, rows)]

        def block(a, owner, h):
            return part(outs[a].at[4 * owner[0] + 2 * owner[1] + owner[2]], h)

        def copy(a, k, owner, h, to, own_src=False):
            return pltpu.make_async_remote_copy(
                src_ref=part(ins[a], h) if own_src else block(a, owner, h), dst_ref=block(a, owner, h),
                send_sem=send_sems.at[a, k], recv_sem=recv_sems.at[a, k], device_id=to, device_id_type=MESH)

        def other(p):
            return (p[0], p[1], 1 - c)

        mine = [pltpu.make_async_copy(ins[a], block(a, me, None), local_sems.at[a]) for a in range(na)]
        for cp in mine:
            cp.start()
        sent = []
        for a in range(na):
            sent += [copy(a, XN0, me, 0, xn, True), copy(a, YN1, me, 1, yn, True),
                     copy(a, XN1, me, 1, xn, True), copy(a, YN0, me, 0, yn, True)]
        sent += [copy(a, SIB, me, None, sib, True) for a in range(na)]
        for cp in sent:
            cp.start()

        def landed(a, k, owner, h, then):
            copy(a, k, owner, h, me).wait_recv()
            for k2, to in then + [(D2D[k], sib)]:
                cp = copy(a, k2, owner, h, to)
                cp.start()
                sent.append(cp)

        for a in range(na):
            landed(a, XN0, xn, 0, [(VIA_Y, yn)])
            landed(a, YN1, yn, 1, [(VIA_X, xn)])
            landed(a, XN1, xn, 1, [])
            landed(a, YN0, yn, 0, [])
        for a in range(na):
            landed(a, VIA_Y, dg, 0, [])
            landed(a, VIA_X, dg, 1, [])
        for a in range(na):
            copy(a, SIB, sib, None, me).wait_recv()
            for k, owner, h in ((XN0, xn, 0), (XN1, xn, 1), (YN1, yn, 1), (YN0, yn, 0), (VIA_Y, dg, 0), (VIA_X, dg, 1)):
                copy(a, D2D[k], other(owner), h, me).wait_recv()
        for cp in sent:
            cp.wait_send()
        for cp in mine:
            cp.wait()

    return _sequencer_call(
        body, name, collective_id,
        [jax.ShapeDtypeStruct((NDEV,) + s.shape, s.dtype) for s in shards],
        [pltpu.SemaphoreType.DMA((na, 13)), pltpu.SemaphoreType.DMA((na, 13)), pltpu.SemaphoreType.DMA((na,))])(*shards)


def _sequencer_call(body, name, collective_id, out_type, scratch_types):
    return pl.kernel(
        body, name=name, out_type=out_type,
        mesh=plsc.ScalarSubcoreMesh(axis_name="sequencer", num_cores=1),
        scratch_types=scratch_types,
        compiler_params=pltpu.CompilerParams(collective_id=collective_id))


def _exchange_sibling(grads, name, collective_id):
    na = len(grads)

    def body(*refs):
        ins, outs = refs[:na], refs[na:2 * na]
        send_sems, recv_sems = refs[2 * na:]
        x, y, c, _ = _place()
        _handshake([(x, y, 1 - c)])
        cps = []
        for a in range(na):
            for k in range(4):
                cps.append(pltpu.make_async_remote_copy(
                    src_ref=ins[a].at[2 * k + (1 - c)], dst_ref=outs[a].at[k],
                    send_sem=send_sems.at[a, k], recv_sem=recv_sems.at[a, k],
                    device_id=(x, y, 1 - c), device_id_type=MESH))
        for cp in cps:
            cp.start()
        for cp in cps:
            cp.wait()

    return _sequencer_call(
        body, name, collective_id,
        [jax.ShapeDtypeStruct((4,) + g.shape[1:], g.dtype) for g in grads],
        [pltpu.SemaphoreType.DMA((na, 4)), pltpu.SemaphoreType.DMA((na, 4))])(*grads)


def _row_tile(rows, cols):
    for t in (512, 256, 176, 128, 64, 32, 16):
        if rows % t == 0 and t * cols * 4 <= (2 << 20):
            return t
    raise ValueError((rows, cols))


def _chip_sum(place, g, got, name):
    _, r, c = g.shape
    tm = r

    def body(pos_ref, g_ref, got_ref, o_ref):
        o_ref[...] = (g_ref[...].astype(F32) + got_ref[...].astype(F32)).astype(BF16)

    def chip(j, pos):
        return 2 * (pos[0] ^ jnp.where(j == 1, 0, 1)) + (pos[1] ^ jnp.where(j == 0, 0, 1))

    return pl.pallas_call(
        body, name=name,
        grid_spec=pltpu.PrefetchScalarGridSpec(
            num_scalar_prefetch=1, grid=(3, r // tm),
            in_specs=[pl.BlockSpec((None, tm, c), lambda j, i, pos: (2 * chip(j, pos) + pos[2], i, 0)),
                      pl.BlockSpec((None, tm, c), lambda j, i, pos: (chip(j, pos), i, 0))],
            out_specs=pl.BlockSpec((None, tm, c), lambda j, i, pos: (j, i, 0))),
        out_shape=jax.ShapeDtypeStruct((3, r, c), BF16),
        compiler_params=_cp(("parallel", "parallel")),
    )(place, g, got)


def _exchange_chips(sums, name, collective_id):
    na = len(sums)

    def body(*refs):
        ins, outs = refs[:na], refs[na:2 * na]
        send_sems, recv_sems = refs[2 * na:]
        x, y, c, chips = _place()
        _handshake([(*chip, c) for chip in chips])
        cps = []
        for a in range(na):
            for j, chip in enumerate(chips):
                cps.append(pltpu.make_async_remote_copy(
                    src_ref=ins[a].at[j], dst_ref=outs[a].at[j],
                    send_sem=send_sems.at[a, j], recv_sem=recv_sems.at[a, j],
                    device_id=(*chip, c), device_id_type=MESH))
        for cp in cps:
            cp.start()
        for cp in cps:
            cp.wait()

    return _sequencer_call(
        body, name, collective_id,
        [jax.ShapeDtypeStruct((3,) + s.shape[1:], s.dtype) for s in sums],
        [pltpu.SemaphoreType.DMA((na, 3)), pltpu.SemaphoreType.DMA((na, 3))])(*sums)


def _exchange_stats(stats, collective_id):
    def body(st_in, st_out, st_send, st_recv, local_sem):
        x, y, c, _ = _place()
        me_idx = 4 * x + 2 * y + c
        peers = [(x ^ ((k >> 2) & 1), y ^ ((k >> 1) & 1), c ^ (k & 1)) for k in range(1, 8)]
        _handshake(peers)
        mine = pltpu.make_async_copy(st_in, st_out.at[me_idx], local_sem)
        mine.start()
        cps = [pltpu.make_async_remote_copy(
            src_ref=st_in, dst_ref=st_out.at[me_idx], send_sem=st_send.at[k], recv_sem=st_recv.at[k],
            device_id=peer, device_id_type=MESH) for k, peer in enumerate(peers)]
        for cp in cps:
            cp.start()
        for cp in cps:
            cp.wait()
        mine.wait()

    return _sequencer_call(
        body, "exchange_stats", collective_id,
        jax.ShapeDtypeStruct((NDEV,) + stats.shape, stats.dtype),
        [pltpu.SemaphoreType.DMA((7,)), pltpu.SemaphoreType.DMA((7,)), pltpu.SemaphoreType.DMA])(stats)


class _Reduction:
    def __init__(self, place, first_collective_id, state):
        self.place = place
        self.ids = iter(range(first_collective_id, 32))
        self.state = state
        self.groups = {}
        self.updates = {}

    def next_id(self):
        return next(self.ids)

    def start(self, group, grads):
        got = _exchange_sibling(grads, "sibling_exchange_" + group[0], self.next_id())
        self.groups[group[0]] = dict(names=group, grads=grads, got=got)

    def local(self, name, first=()):
        grp = self.groups[name]
        grads = lax.optimization_barrier((tuple(grp["grads"]), tuple(first)))[0]
        grp["sums"] = [_chip_sum(self.place, g, s, "chip_sum_" + n)
                       for g, s, n in zip(grads, grp["got"], grp["names"])]
        grp["chips"] = _exchange_chips(grp["sums"], "chip_exchange_" + name, self.next_id())
        return grp["sums"]

    def landed(self, name):
        return list(self.groups[name]["chips"])

    def update(self, name):
        if name not in self.updates:
            grp = next(g for g in self.groups.values() if name in g["names"])
            k = grp["names"].index(name)
            self.updates[name] = _shard_update(self.place, *self.state[name], grp["grads"][k], grp["got"][k],
                                               grp["chips"][k], "update_" + name)
        return list(self.updates[name])


def _adamw(w, g, m, v):
    m = ADAM_B1 * m + (1.0 - ADAM_B1) * g
    v = ADAM_B2 * v + (1.0 - ADAM_B2) * (g * g)
    m_hat = m / (1.0 - ADAM_B1 ** ADAM_STEP)
    v_hat = v / (1.0 - ADAM_B2 ** ADAM_STEP)
    delta = -ADAM_LR * (m_hat / (jnp.sqrt(v_hat) + ADAM_EPS) + ADAM_WD * w)
    return delta, m, v


def _shard_update(place, w, m, v, g, got_sib, got_chips, name):
    r, c = w.shape
    tm = _row_tile(r, c)

    def body(pos_ref, w_ref, m_ref, v_ref, g_ref, s_ref, c_ref, go_ref, d_ref, mo_ref, vo_ref):
        grad = g_ref[...].astype(F32) + s_ref[...].astype(F32)
        for j in range(3):
            grad = grad + c_ref[j].astype(F32)
        delta, mn, vn = _adamw(w_ref[...], grad, m_ref[...], v_ref[...])
        go_ref[...] = grad
        d_ref[...] = delta
        mo_ref[...] = mn
        vo_ref[...] = vn

    row = pl.BlockSpec((tm, c), lambda i, pos: (i, 0))
    return pl.pallas_call(
        body, name=name,
        grid_spec=pltpu.PrefetchScalarGridSpec(
            num_scalar_prefetch=1, grid=(r // tm,),
            in_specs=[row, row, row,
                      pl.BlockSpec((None, tm, c), lambda i, pos: (4 * pos[0] + 2 * pos[1] + pos[2], i, 0)),
                      pl.BlockSpec((None, tm, c), lambda i, pos: (2 * pos[0] + pos[1], i, 0)),
                      pl.BlockSpec((3, tm, c), lambda i, pos: (0, i, 0))],
            out_specs=[row, row, row, row]),
        out_shape=[jax.ShapeDtypeStruct((r, c), F32)] * 4,
        compiler_params=_cp(("parallel",)),
    )(place, w, m, v, g, got_sib, got_chips)


def _small_update(stats_all, ws, ms, vs):
    def body(st_ref, w_ref, m_ref, v_ref, go_ref, d_ref, mo_ref, vo_ref):
        grad = st_ref[0]
        for k in range(1, NDEV):
            grad = grad + st_ref[k]
        delta, mn, vn = _adamw(w_ref[...], grad, m_ref[...], v_ref[...])
        go_ref[...] = grad
        d_ref[...] = delta
        mo_ref[...] = mn
        vo_ref[...] = vn

    return pl.pallas_call(
        body, name="small_update",
        out_shape=[jax.ShapeDtypeStruct((8, D), F32)] * 4,
        compiler_params=_cp(),
    )(stats_all, ws, ms, vs)


def kernel(x, norm_mix_w, w_in, w_out, norm_ffn_w, w_gate, w_up, w_down, norm_final_w, loss_target, m_norm_mix_w, m_w_in, m_w_out, m_norm_ffn_w, m_w_gate, m_w_up, m_w_down, m_norm_final_w, v_norm_mix_w, v_w_in, v_w_out, v_norm_ffn_w, v_w_gate, v_w_up, v_w_down, v_norm_final_w):
    tr = {"w_gate", "w_up"}
    names = ["w_in", "w_out", "w_gate", "w_up", "w_down"]

    def view(a, n):
        return a[0].T if n in tr else a[0]

    big_w = [view(a, n) for a, n in zip([w_in, w_out, w_gate, w_up, w_down], names)]
    big_m = [view(a, n) for a, n in zip([m_w_in, m_w_out, m_w_gate, m_w_up, m_w_down], names)]
    big_v = [view(a, n) for a, n in zip([v_w_in, v_w_out, v_w_gate, v_w_up, v_w_down], names)]

    shards = [_cast_bf16(w, "cast_" + n) for w, n in zip(big_w, names)]
    (win,) = _all_gather(shards[0:1], "all_gather_w_in", 1)
    wout, wg, wu = _all_gather(shards[1:4], "all_gather_out_gate_up", 2)
    (wd,) = _all_gather(shards[4:5], "all_gather_w_down", 3)
    nw3 = norm_final_w.reshape(1, D)
    place = jnp.stack([lax.axis_index("x"), lax.axis_index("y"), lax.axis_index("c")]).astype(jnp.int32)
    red = _Reduction(place, 4, {n: (w, m, v) for n, w, m, v in zip(names, big_w, big_m, big_v)})
    stats, gx, *_ = _local_step(
        x[0], loss_target[0], norm_mix_w, norm_ffn_w, nw3, win, wout.reshape(D, D), wg, wu, wd, red)
    stats_all = _exchange_stats(stats, red.next_id())
    upd = [red.update(n) for n in names]
    stats_all = lax.optimization_barrier((stats_all, tuple(upd[0])))[0]

    def rows(a, b, c):
        return jnp.concatenate([a.reshape(1, D), b.reshape(1, D), c.reshape(1, D), jnp.zeros((5, D), F32)], axis=0)

    sg, sd, sm, sv = _small_update(stats_all, rows(norm_mix_w, norm_ffn_w, norm_final_w),
                                   rows(m_norm_mix_w, m_norm_ffn_w, m_norm_final_w),
                                   rows(v_norm_mix_w, v_norm_ffn_w, v_norm_final_w))
    loss = sg[3, 0]

    def outs(k, small):
        big = [(u[k].T if n in tr else u[k])[None] for u, n in zip(upd, names)]
        return [small[0:1], big[0], big[1], small[1:2], big[2], big[3], big[4], small[2]]

    return (loss, gx[None], *outs(0, sg), *outs(1, sd), *outs(2, sm), *outs(3, sv))
```

```python
import functools
import math

import numpy as np
import jax
import jax.numpy as jnp
from jax import lax
from jax.experimental import pallas as pl
from jax.experimental.pallas import tpu as pltpu
from jax.experimental.pallas import tpu_sc as plsc

F32 = jnp.float32
BF16 = jnp.bfloat16

S = 2048
D = 2048
NDEV = 8
N_IN = 7168 // NDEV
N_FF = 5632 // NDEV
NFG, N_FG = NDEV // 2, 2 * N_FF
N_OUT = 2048 // NDEV
AH, AHD = 8, 128
RH, RHD = 4, 256
CH = 128
NB = S // CH
EPS = 1e-6
PATTERNS = ((1, 16), (4, 4), (16, 1))
NEG = -1e30
VMEM_LIMIT = 56 * 1024 * 1024

ADAM_LR, ADAM_B1, ADAM_B2, ADAM_EPS, ADAM_WD, ADAM_STEP = 0.001, 0.9, 0.999, 1e-08, 0.01, 10
MESH = pl.DeviceIdType.MESH


def _cp(sem=None):
    return pltpu.CompilerParams(dimension_semantics=sem, vmem_limit_bytes=VMEM_LIMIT)


def _dot(a, b):
    return jnp.dot(a, b, preferred_element_type=F32)


def _dot_nt(a, b):
    return lax.dot_general(a, b, (((1,), (1,)), ((), ())), preferred_element_type=F32)


def _dot_tn(a, b):
    return lax.dot_general(a, b, (((0,), (0,)), ((), ())), preferred_element_type=F32)


def _sigmoid(x):
    return 0.5 * jnp.tanh(0.5 * x) + 0.5


def _cast_bf16(w, name):
    r, c = w.shape
    tm = r if r <= 1024 else 512

    def body(w_ref, o_ref):
        o_ref[...] = w_ref[...].astype(BF16)

    return pl.pallas_call(
        body, name=name, grid=(r // tm,),
        in_specs=[pl.BlockSpec((tm, c), lambda i: (i, 0))],
        out_specs=pl.BlockSpec((tm, c), lambda i: (i, 0)),
        out_shape=jax.ShapeDtypeStruct((r, c), BF16),
        compiler_params=_cp(("parallel",)),
    )(w)


def _rms_fwd(x, nw):
    tm = 256

    def body(x_ref, w_ref, h_ref, r_ref):
        xs = x_ref[...]
        r = lax.rsqrt(jnp.mean(xs * xs, axis=-1, keepdims=True) + EPS)
        h_ref[...] = ((xs * r) * w_ref[...]).astype(BF16)
        r_ref[...] = r

    return pl.pallas_call(
        body, name="rms_fwd", grid=(S // tm,),
        in_specs=[pl.BlockSpec((tm, D), lambda i: (i, 0)), pl.BlockSpec((1, D), lambda i: (0, 0))],
        out_specs=[pl.BlockSpec((tm, D), lambda i: (i, 0)), pl.BlockSpec((tm, 1), lambda i: (i, 0))],
        out_shape=[jax.ShapeDtypeStruct((S, D), BF16), jax.ShapeDtypeStruct((S, 1), F32)],
        compiler_params=_cp(("parallel",)),
    )(x, nw)


def _row_copies(hbm_refs, bufs, sems, m, tm):
    rows = pl.ds(pl.multiple_of(m * tm, tm), tm)
    return [pltpu.make_async_copy(h.at[rows], b, sems.at[i]) for i, (h, b) in enumerate(zip(hbm_refs, bufs))]


def _rms_bwd_tile(dh, xs, r, nw):
    dnw = jnp.sum(dh * (xs * r), axis=0, keepdims=True)
    gy = dh * nw
    dx = r * gy - xs * ((r * r * r) * jnp.mean(gy * xs, axis=-1, keepdims=True))
    return dx, dnw


def _proj(h1, win):
    tm = 1024

    def body(a_ref, w_ref, o_ref):
        o_ref[...] = _dot(a_ref[...], w_ref[...])

    return pl.pallas_call(
        body, name="proj", grid=(NDEV, S // tm),
        in_specs=[pl.BlockSpec((tm, D), lambda p, m: (m, 0)),
                  pl.BlockSpec((None, D, N_IN), lambda p, m: (p, 0, 0))],
        out_specs=pl.BlockSpec((tm, N_IN), lambda p, m: (m, p)),
        out_shape=jax.ShapeDtypeStruct((S, NDEV * N_IN), F32),
        compiler_params=_cp(("parallel", "parallel")),
    )(h1, win)


def _out_proj_rms(x, ma, mr, wout, nw):
    tm = 256
    half = D // 2

    def body(x_ref, ma_ref, mr_ref, w_ref, nw_ref, x2_ref, h_ref, r_ref):
        acc = _dot(ma_ref[...], w_ref[0:half, :]) + _dot(mr_ref[...], w_ref[half:D, :])
        x2 = x_ref[...] + acc
        r = lax.rsqrt(jnp.mean(x2 * x2, axis=-1, keepdims=True) + EPS)
        x2_ref[...] = x2
        h_ref[...] = ((x2 * r) * nw_ref[...]).astype(BF16)
        r_ref[...] = r

    return pl.pallas_call(
        body, name="out_proj_rms", grid=(S // tm,),
        in_specs=[pl.BlockSpec((tm, D), lambda i: (i, 0)),
                  pl.BlockSpec((tm, half), lambda i: (i, 0)),
                  pl.BlockSpec((tm, half), lambda i: (i, 0)),
                  pl.BlockSpec((D, D), lambda i: (0, 0)),
                  pl.BlockSpec((1, D), lambda i: (0, 0))],
        out_specs=[pl.BlockSpec((tm, D), lambda i: (i, 0)), pl.BlockSpec((tm, D), lambda i: (i, 0)),
                   pl.BlockSpec((tm, 1), lambda i: (i, 0))],
        out_shape=[jax.ShapeDtypeStruct((S, D), F32), jax.ShapeDtypeStruct((S, D), BF16),
                   jax.ShapeDtypeStruct((S, 1), F32)],
        compiler_params=_cp(("parallel",)),
    )(x, ma, mr, wout, nw)


def _ffn_up(h2, wg, wu):
    tm = 512

    def body(h_ref, wg_ref, wu_ref, g_ref, u_ref, a_ref):
        h = h_ref[...]
        g = _dot_nt(h, wg_ref[...])
        u = _dot_nt(h, wu_ref[...])
        g_ref[...] = g
        u_ref[...] = u
        a_ref[...] = ((g * _sigmoid(g)) * u).astype(BF16)

    blk = pl.BlockSpec((None, tm, N_FG), lambda p, m: (p, m, 0))
    wblk = pl.BlockSpec((None, N_FG, D), lambda p, m: (p, 0, 0))
    return pl.pallas_call(
        body, name="ffn_up", grid=(NFG, S // tm),
        in_specs=[pl.BlockSpec((tm, D), lambda p, m: (m, 0)), wblk, wblk],
        out_specs=[blk, blk, blk],
        out_shape=[jax.ShapeDtypeStruct((NFG, S, N_FG),F32), jax.ShapeDtypeStruct((NFG, S, N_FG),F32),
                   jax.ShapeDtypeStruct((NFG, S, N_FG),BF16)],
        compiler_params=_cp(("parallel", "parallel")),
    )(h2, wg, wu)


def _ffn_down_loss(x2, a, wd, nw, tgt):
    tm = 512

    def body(x2_hbm, a_ref, w_ref, nw_ref, t_hbm, dx_ref, dxb_ref, st_ref, acc_ref, x2_buf, t_buf, sems):
        m, p = pl.program_id(0), pl.program_id(1)
        tail_in = _row_copies((x2_hbm, t_hbm), (x2_buf, t_buf), sems, m, tm)

        @pl.when(p == 0)
        def _():
            acc_ref[...] = jnp.zeros_like(acc_ref)
            for cp in tail_in:
                cp.start()

        @pl.when((p == 0) & (m == 0))
        def _():
            st_ref[...] = jnp.zeros_like(st_ref)

        acc_ref[...] += _dot(a_ref[...], w_ref[...])

        @pl.when(p == NFG - 1)
        def _():
            for cp in tail_in:
                cp.wait()
            x3 = x2_buf[...] + acc_ref[...]
            nwv = nw_ref[...]
            r = lax.rsqrt(jnp.mean(x3 * x3, axis=-1, keepdims=True) + EPS)
            y = (x3 * r) * nwv
            err = y - t_buf[...]
            loss = 0.5 * jnp.sum(jnp.mean(err * err, axis=-1, keepdims=True), axis=0, keepdims=True)
            dy = err * (1.0 / D)
            dx, dnw = _rms_bwd_tile(dy, x3, r, nwv)
            dx_ref[...] = dx
            dxb_ref[...] = dx.astype(BF16)
            st_ref[0:1, :] += dnw
            st_ref[1:2, :] += jnp.broadcast_to(loss, (1, D))

    return pl.pallas_call(
        body, name="ffn_down_loss", grid=(S // tm, NFG),
        in_specs=[pl.BlockSpec(memory_space=pl.ANY),
                  pl.BlockSpec((None, tm, N_FG), lambda m, p: (p, m, 0)),
                  pl.BlockSpec((None, N_FG, D), lambda m, p: (p, 0, 0)),
                  pl.BlockSpec((1, D), lambda m, p: (0, 0)),
                  pl.BlockSpec(memory_space=pl.ANY)],
        out_specs=[pl.BlockSpec((tm, D), lambda m, p: (m, 0)), pl.BlockSpec((tm, D), lambda m, p: (m, 0)),
                   pl.BlockSpec((8, D), lambda m, p: (0, 0))],
        out_shape=[jax.ShapeDtypeStruct((S, D), F32), jax.ShapeDtypeStruct((S, D), BF16),
                   jax.ShapeDtypeStruct((8, D), F32)],
        scratch_shapes=[pltpu.VMEM((tm, D), F32), pltpu.VMEM((tm, D), F32), pltpu.VMEM((tm, D), F32),
                        pltpu.SemaphoreType.DMA((2,))],
        compiler_params=_cp(("arbitrary", "arbitrary")),
    )(x2, a, wd, nw, tgt)


def _ffn_down_bwd(dx3b, wd, g, u):
    tm = 512

    def body(dx_ref, w_ref, g_ref, u_ref, dg_ref, du_ref):
        da = _dot_nt(dx_ref[...], w_ref[...])
        gv = g_ref[...]
        sg = _sigmoid(gv)
        silu = gv * sg
        dg_ref[...] = ((da * u_ref[...]) * (sg * (1.0 + gv * (1.0 - sg)))).astype(BF16)
        du_ref[...] = (da * silu).astype(BF16)

    blk = pl.BlockSpec((None, tm, N_FG), lambda p, m: (p, m, 0))
    return pl.pallas_call(
        body, name="ffn_down_bwd", grid=(NFG, S // tm),
        in_specs=[pl.BlockSpec((tm, D), lambda p, m: (m, 0)),
                  pl.BlockSpec((None, N_FG, D), lambda p, m: (p, 0, 0)), blk, blk],
        out_specs=[blk, blk],
        out_shape=[jax.ShapeDtypeStruct((NFG, S, N_FG),BF16), jax.ShapeDtypeStruct((NFG, S, N_FG),BF16)],
        compiler_params=_cp(("parallel", "parallel")),
    )(dx3b, wd, g, u)


def _ffn_up_bwd(dg, du, wg, wu, dres, xs, r, nw):
    tm = 512

    def body(dg_ref, du_ref, wg_ref, wu_ref, dres_hbm, x_hbm, r_ref, nw_ref, dx_ref, dxb_ref, st_ref,
             dres_buf, x_buf, sems):
        m, p = pl.program_id(0), pl.program_id(1)
        tail_in = _row_copies((dres_hbm, x_hbm), (dres_buf, x_buf), sems, m, tm)

        @pl.when(p == 0)
        def _():
            dx_ref[...] = jnp.zeros_like(dx_ref)
            for cp in tail_in:
                cp.start()

        @pl.when((p == 0) & (m == 0))
        def _():
            st_ref[...] = jnp.zeros_like(st_ref)

        dx_ref[...] += _dot(dg_ref[...], wg_ref[...])
        dx_ref[...] += _dot(du_ref[...], wu_ref[...])

        @pl.when(p == NFG - 1)
        def _():
            for cp in tail_in:
                cp.wait()
            dx, dnw = _rms_bwd_tile(dx_ref[...], x_buf[...], r_ref[...], nw_ref[...])
            dx = dres_buf[...] + dx
            dx_ref[...] = dx
            dxb_ref[...] = dx.astype(BF16)
            st_ref[0:1, :] += dnw

    blk = pl.BlockSpec((None, tm, N_FG), lambda m, p: (p, m, 0))
    wblk = pl.BlockSpec((None, N_FG, D), lambda m, p: (p, 0, 0))
    row = pl.BlockSpec((tm, D), lambda m, p: (m, 0))
    hbm = pl.BlockSpec(memory_space=pl.ANY)
    return pl.pallas_call(
        body, name="ffn_up_bwd", grid=(S // tm, NFG),
        in_specs=[blk, blk, wblk, wblk, hbm, hbm, pl.BlockSpec((tm, 1), lambda m, p: (m, 0)),
                  pl.BlockSpec((1, D), lambda m, p: (0, 0))],
        out_specs=[row, row, pl.BlockSpec((8, D), lambda m, p: (0, 0))],
        out_shape=[jax.ShapeDtypeStruct((S, D), F32), jax.ShapeDtypeStruct((S, D), BF16),
                   jax.ShapeDtypeStruct((8, D), F32)],
        scratch_shapes=[pltpu.VMEM((tm, D), F32), pltpu.VMEM((tm, D), F32), pltpu.SemaphoreType.DMA((2,))],
        compiler_params=_cp(("arbitrary", "arbitrary")),
    )(dg, du, wg, wu, dres, xs, r, nw)


def _out_proj_bwd(dx2b, wout):
    tm = 256

    def body(dx_ref, w_ref, o_ref):
        o_ref[...] = _dot_nt(dx_ref[...], w_ref[...])

    return pl.pallas_call(
        body, name="out_proj_bwd", grid=(S // tm,),
        in_specs=[pl.BlockSpec((tm, D), lambda i: (i, 0)), pl.BlockSpec((D, D), lambda i: (0, 0))],
        out_specs=pl.BlockSpec((tm, D), lambda i: (i, 0)),
        out_shape=jax.ShapeDtypeStruct((S, D), F32),
        compiler_params=_cp(("parallel",)),
    )(dx2b, wout)


def _in_proj_bwd(dproj, win, dres, xs, r, nw):
    tm = 512

    def body(dp_ref, w_ref, dres_ref, x_ref, r_ref, nw_ref, dx_ref, st_ref, acc_ref):
        m, p = pl.program_id(0), pl.program_id(1)

        @pl.when(p == 0)
        def _():
            acc_ref[...] = jnp.zeros_like(acc_ref)

        @pl.when((p == 0) & (m == 0))
        def _():
            st_ref[...] = jnp.zeros_like(st_ref)

        acc_ref[...] += _dot_nt(dp_ref[...], w_ref[...])

        @pl.when(p == NDEV - 1)
        def _():
            dx, dnw = _rms_bwd_tile(acc_ref[...], x_ref[...], r_ref[...], nw_ref[...])
            dx_ref[...] = dres_ref[...] + dx
            st_ref[0:1, :] += dnw

    row = pl.BlockSpec((tm, D), lambda m, p: (m, 0))
    return pl.pallas_call(
        body, name="in_proj_bwd", grid=(S // tm, NDEV),
        in_specs=[pl.BlockSpec((tm, N_IN), lambda m, p: (m, p)),
                  pl.BlockSpec((None, D, N_IN), lambda m, p: (p, 0, 0)),
                  row, row, pl.BlockSpec((tm, 1), lambda m, p: (m, 0)),
                  pl.BlockSpec((1, D), lambda m, p: (0, 0))],
        out_specs=[row, pl.BlockSpec((8, D), lambda m, p: (0, 0))],
        out_shape=[jax.ShapeDtypeStruct((S, D), F32), jax.ShapeDtypeStruct((8, D), F32)],
        scratch_shapes=[pltpu.VMEM((tm, D), F32)],
        compiler_params=_cp(("arbitrary", "arbitrary")),
    )(dproj, win, dres, xs, r, nw)


def _wgrad_in(h1, dproj):
    def body(a_ref, d_ref, o_ref):
        o_ref[...] = _dot_tn(a_ref[...], d_ref[...]).astype(BF16)

    return pl.pallas_call(
        body, name="wgrad_in", grid=(NDEV,),
        in_specs=[pl.BlockSpec((S, D), lambda p: (0, 0)), pl.BlockSpec((S, N_IN), lambda p: (0, p))],
        out_specs=pl.BlockSpec((None, D, N_IN), lambda p: (p, 0, 0)),
        out_shape=jax.ShapeDtypeStruct((NDEV, D, N_IN), BF16),
        compiler_params=_cp(("parallel",)),
    )(h1, dproj)


def _wgrad_rows(a3, dy, name):
    def body(a_ref, d_ref, o_ref):
        o_ref[...] = _dot_tn(a_ref[...], d_ref[...]).astype(BF16)

    return pl.pallas_call(
        body, name=name, grid=(NFG,),
        in_specs=[pl.BlockSpec((None, S, N_FG), lambda p: (p, 0, 0)), pl.BlockSpec((S, D), lambda p: (0, 0))],
        out_specs=pl.BlockSpec((None, N_FG, D), lambda p: (p, 0, 0)),
        out_shape=jax.ShapeDtypeStruct((NFG, N_FG, D), BF16),
        compiler_params=_cp(("parallel",)),
    )(a3, dy).reshape(NDEV, N_FF, D)


def _wgrad_out(ma, mr, dx2b):
    half = D // 2
    per = half // N_OUT

    def body(ma_ref, mr_ref, d_ref, o_ref):
        p = pl.program_id(0)

        @pl.when(p < per)
        def _():
            o_ref[...] = _dot_tn(ma_ref[...], d_ref[...]).astype(BF16)

        @pl.when(p >= per)
        def _():
            o_ref[...] = _dot_tn(mr_ref[...], d_ref[...]).astype(BF16)

    return pl.pallas_call(
        body, name="wgrad_out", grid=(NDEV,),
        in_specs=[pl.BlockSpec((S, N_OUT), lambda p: (0, jnp.minimum(p, per - 1))),
                  pl.BlockSpec((S, N_OUT), lambda p: (0, jnp.maximum(p - per, 0))),
                  pl.BlockSpec((S, D), lambda p: (0, 0))],
        out_specs=pl.BlockSpec((None, N_OUT, D), lambda p: (p, 0, 0)),
        out_shape=jax.ShapeDtypeStruct((NDEV, N_OUT, D), BF16),
        compiler_params=_cp(("parallel",)),
    )(ma, mr, dx2b)


def _attn_consts():
    c = np.zeros((AH, 8, AHD), np.float32)
    for h in range(AH):
        c[h, :, :] = 2.0 ** (-(h + 1))
    return jnp.asarray(c)


def _permute_in(dst, src, d, cast=None):
    ln = S // d
    for rr in range(d):
        v = src[pl.ds(rr, ln, stride=d), :] if d > 1 else src[...]
        dst[rr * ln:(rr + 1) * ln, :] = v if cast is None else v.astype(cast)


def _attn_masks():
    qi = lax.broadcasted_iota(jnp.int32, (CH, CH), 0)
    kj = lax.broadcasted_iota(jnp.int32, (CH, CH), 1)
    dist_c = (qi - kj).astype(F32)
    dist_p = (qi - kj + CH).astype(F32)
    return (qi >= kj)[None], (kj >= qi)[None], dist_c[None], dist_p[None]


GB = 8


def _bdot_nt(a, b):
    return lax.dot_general(a, b, (((2,), (2,)), ((0,), (0,))), preferred_element_type=F32)


def _bdot(a, b):
    return lax.dot_general(a, b, (((2,), (1,)), ((0,), (0,))), preferred_element_type=F32)


def _bdot_tn(a, b):
    return lax.dot_general(a, b, (((1,), (1,)), ((0,), (0,))), preferred_element_type=F32)


def _shift_block(dst, src):
    dst[0:CH, :] = jnp.zeros((CH, AHD), dst.dtype)
    dst[CH:S, :] = src[0:S - CH, :]


def _has_prev(g, nb):
    blk = lax.broadcasted_iota(jnp.int32, (GB, 1, 1), 0) + g * GB
    return (blk & (nb - 1)) != 0


def _blocks(ref, g):
    return ref[g * GB * CH:(g + 1) * GB * CH, :].reshape(GB, CH, AHD)


def _attn_fwd(proj):
    scale = 1.0 / math.sqrt(AHD)

    def body(c_ref, q_ref, k_ref, v_ref, o_ref, ob_ref, lse_ref, qd, kd, vd, kps, vps, od, ld, *nat):
        onat, lnat = nat[0:3], nat[3:6]
        slope = c_ref[0:1, :]
        mask_c, mask_p, dist_c, dist_p = _attn_masks()
        for pi, (d, nb) in enumerate(PATTERNS):
            _permute_in(qd, q_ref, d, BF16)
            _permute_in(kd, k_ref, d, BF16)
            _permute_in(vd, v_ref, d, BF16)
            if nb > 1:
                _shift_block(kps, kd)
                _shift_block(vps, vd)
            bias_c = -(slope * float(d)) * dist_c
            bias_p = -(slope * float(d)) * dist_p
            for g in range(NB // GB):
                q3, k3, v3 = _blocks(qd, g), _blocks(kd, g), _blocks(vd, g)
                s_c = jnp.where(mask_c, _bdot_nt(q3, k3) * scale + bias_c, NEG)
                mx = jnp.max(s_c, axis=-1, keepdims=True)
                if nb > 1:
                    kp3, vp3 = _blocks(kps, g), _blocks(vps, g)
                    s_p = jnp.where(jnp.logical_and(mask_p, _has_prev(g, nb)),
                                    _bdot_nt(q3, kp3) * scale + bias_p, NEG)
                    mx = jnp.maximum(mx, jnp.max(s_p, axis=-1, keepdims=True))
                    l = (jnp.sum(jnp.exp(s_c - mx), axis=-1, keepdims=True)
                         + jnp.sum(jnp.exp(s_p - mx), axis=-1, keepdims=True))
                    lse = mx + jnp.log(l)
                    o3 = _bdot(jnp.exp(s_c - lse).astype(BF16), v3) + _bdot(jnp.exp(s_p - lse).astype(BF16), vp3)
                else:
                    l = jnp.sum(jnp.exp(s_c - mx), axis=-1, keepdims=True)
                    lse = mx + jnp.log(l)
                    o3 = _bdot(jnp.exp(s_c - lse).astype(BF16), v3)
                rows = slice(g * GB * CH, (g + 1) * GB * CH)
                od[rows, :] = o3.reshape(GB * CH, AHD)
                ld[rows, :] = jnp.broadcast_to(lse, (GB, CH, AHD)).reshape(GB * CH, AHD)
            ln = S // d
            for rr in range(d):
                if d > 1:
                    onat[pi][pl.ds(rr, ln, stride=d), :] = od[rr * ln:(rr + 1) * ln, :]
                    lnat[pi][pl.ds(rr, ln, stride=d), :] = ld[rr * ln:(rr + 1) * ln, :]
                else:
                    onat[pi][...] = od[...]
                    lnat[pi][...] = ld[...]
        l0, l1, l2 = lnat[0][...], lnat[1][...], lnat[2][...]
        mx = jnp.maximum(jnp.maximum(l0, l1), l2)
        e0, e1, e2 = jnp.exp(l0 - mx), jnp.exp(l1 - mx), jnp.exp(l2 - mx)
        den = e0 + e1 + e2
        out = (e0 / den) * onat[0][...] + (e1 / den) * onat[1][...] + (e2 / den) * onat[2][...]
        o_ref[...] = out
        ob_ref[...] = out.astype(BF16)
        lse_ref[...] = mx + jnp.log(den)

    def col(off):
        return pl.BlockSpec((S, AHD), lambda h: (0, off + h))

    return pl.pallas_call(
        body, name="attn_fwd", grid=(AH,),
        in_specs=[pl.BlockSpec((None, 8, AHD), lambda h: (h, 0, 0)), col(0), col(AH), col(2 * AH)],
        out_specs=[col(0), col(0), col(0)],
        out_shape=[jax.ShapeDtypeStruct((S, AH * AHD), F32), jax.ShapeDtypeStruct((S, AH * AHD), BF16),
                   jax.ShapeDtypeStruct((S, AH * AHD), F32)],
        scratch_shapes=[pltpu.VMEM((S, AHD), BF16) for _ in range(5)]
        + [pltpu.VMEM((S, AHD), F32) for _ in range(8)],
        compiler_params=_cp(("parallel",)),
    )(_attn_consts(), proj, proj, proj)


def _attn_bwd(proj, dmixed, o, lse):
    scale = 1.0 / math.sqrt(AHD)

    def body(c_ref, q_ref, k_ref, v_ref, do_ref, o_ref, lse_ref, dq_ref, dk_ref, dv_ref,
             qd, kd, vd, dod, kps, vps, lsd, dld, dqd, dkd, dvd, delta, aq, ak, av):
        slope = c_ref[0:1, :]
        mask_c, mask_p, dist_c, dist_p = _attn_masks()
        delta[...] = jnp.broadcast_to(jnp.sum(do_ref[...] * o_ref[...], axis=-1, keepdims=True), (S, AHD))
        for pi, (d, nb) in enumerate(PATTERNS):
            _permute_in(qd, q_ref, d, BF16)
            _permute_in(kd, k_ref, d, BF16)
            _permute_in(vd, v_ref, d, BF16)
            _permute_in(dod, do_ref, d, BF16)
            _permute_in(lsd, lse_ref, d)
            _permute_in(dld, delta, d)
            if nb > 1:
                _shift_block(kps, kd)
                _shift_block(vps, vd)
            bias_c = -(slope * float(d)) * dist_c
            bias_p = -(slope * float(d)) * dist_p
            for g in range(NB // GB):
                q3, k3, v3, do3 = _blocks(qd, g), _blocks(kd, g), _blocks(vd, g), _blocks(dod, g)
                ls, dl = _blocks(lsd, g), _blocks(dld, g)
                lo, hi = g * GB * CH, (g + 1) * GB * CH
                p_c = jnp.exp(jnp.where(mask_c, _bdot_nt(q3, k3) * scale + bias_c, NEG) - ls)
                ds_c = ((p_c * (_bdot_nt(do3, v3) - dl)) * scale).astype(BF16)
                dq3 = _bdot(ds_c, k3)
                dkd[lo:hi, :] = _bdot_tn(ds_c, q3).reshape(GB * CH, AHD)
                dvd[lo:hi, :] = _bdot_tn(p_c.astype(BF16), do3).reshape(GB * CH, AHD)
                if nb > 1:
                    kp3, vp3 = _blocks(kps, g), _blocks(vps, g)
                    p_p = jnp.exp(jnp.where(jnp.logical_and(mask_p, _has_prev(g, nb)),
                                            _bdot_nt(q3, kp3) * scale + bias_p, NEG) - ls)
                    ds_p = ((p_p * (_bdot_nt(do3, vp3) - dl)) * scale).astype(BF16)
                    dq3 = dq3 + _bdot(ds_p, kp3)
                    dkp = _bdot_tn(ds_p, q3).reshape(GB * CH, AHD)
                    dvp = _bdot_tn(p_p.astype(BF16), do3).reshape(GB * CH, AHD)
                    if g == 0:
                        dkd[0:hi - CH, :] += dkp[CH:, :]
                        dvd[0:hi - CH, :] += dvp[CH:, :]
                    else:
                        dkd[lo - CH:hi - CH, :] += dkp
                        dvd[lo - CH:hi - CH, :] += dvp
                dqd[lo:hi, :] = dq3.reshape(GB * CH, AHD)
            ln = S // d
            for acc, src in ((aq, dqd), (ak, dkd), (av, dvd)):
                if pi == 0:
                    acc[...] = src[...]
                else:
                    for rr in range(d):
                        acc[pl.ds(rr, ln, stride=d), :] += src[rr * ln:(rr + 1) * ln, :]
        dq_ref[...] = aq[...].astype(BF16)
        dk_ref[...] = ak[...].astype(BF16)
        dv_ref[...] = av[...].astype(BF16)

    def col(off):
        return pl.BlockSpec((S, AHD), lambda h: (0, off + h))

    return pl.pallas_call(
        body, name="attn_bwd", grid=(AH,),
        in_specs=[pl.BlockSpec((None, 8, AHD), lambda h: (h, 0, 0)), col(0), col(AH), col(2 * AH),
                  col(0), col(0), col(0)],
        out_specs=[col(0), col(0), col(0)],
        out_shape=[jax.ShapeDtypeStruct((S, AH * AHD), BF16)] * 3,
        scratch_shapes=[pltpu.VMEM((S, AHD), BF16) for _ in range(6)]
        + [pltpu.VMEM((S, AHD), F32) for _ in range(9)],
        compiler_params=_cp(("parallel",)),
    )(_attn_consts(), proj, proj, proj, dmixed, o, lse)


def _ret_consts():
    c = np.zeros((RH, 8, RHD), np.float32)
    for h in range(RH):
        c[h, :, :] = np.log(np.float32(1.0) - np.float32(2.0 ** (-5.0 - h)))
    return jnp.asarray(c)


def _ret_factors(lg):
    i = lax.broadcasted_iota(jnp.int32, (CH, CH), 0)
    j = lax.broadcasted_iota(jnp.int32, (CH, CH), 1)
    dif = (i - j).astype(F32)
    decay = jnp.where(dif >= 0, jnp.exp(lg[:, 0:CH] * jnp.maximum(dif, 0.0)), 0.0)
    row = lax.broadcasted_iota(jnp.int32, (CH, RHD), 0).astype(F32)
    zeta = jnp.exp(lg * (CH - 1.0 - row))
    xi = jnp.exp(lg * (row + 1.0))
    return decay, zeta, xi, jnp.exp(lg * float(CH))


CBK = 8
RSTEPS = NB // CBK


def _ret_specs(rev):
    off = 3 * AH * AHD // RHD
    rows = CBK * CH

    def ch(n):
        return (RSTEPS - 1 - n) if rev else n

    def col(k):
        return pl.BlockSpec((rows, RHD), lambda h, n: (ch(n), off + k * RH + h))

    own = pl.BlockSpec((rows, RHD), lambda h, n: (ch(n), h))
    state = pl.BlockSpec((None, CBK, RHD, RHD), lambda h, n: (h, ch(n), 0, 0))
    const = pl.BlockSpec((None, 8, RHD), lambda h, n: (h, 0, 0))
    dm = pl.BlockSpec((rows, RHD), lambda h, n: (ch(n), AH * AHD // RHD + h))
    return col, own, state, const, dm


def _chunks(x):
    return x.reshape(CBK, CH, RHD)


def _ret_fwd(proj):
    def body(c_ref, q_ref, k_ref, v_ref, g_ref, ret_ref, mr_ref, st_ref, r_acc):
        n = pl.program_id(1)

        @pl.when(n == 0)
        def _():
            r_acc[...] = jnp.zeros_like(r_acc)

        decay, zeta, xi, gch = _ret_factors(c_ref[0:1, :])
        q3 = _chunks(q_ref[...].astype(BF16))
        kc = _chunks(k_ref[...] * (1.0 / math.sqrt(RHD)))
        k3 = kc.astype(BF16)
        v3 = _chunks(v_ref[...].astype(BF16))
        kv3 = _bdot_tn((kc * zeta[None]).astype(BF16), v3)
        r = r_acc[...]
        for i in range(CBK):
            st_ref[i] = r.astype(BF16)
            r = r * gch + kv3[i]
        r_acc[...] = r
        scores = _bdot_nt(q3, k3) * decay[None]
        ret = (_bdot(scores.astype(BF16), v3) + _bdot(q3, st_ref[...]) * xi[None]).reshape(CBK * CH, RHD)
        ret_ref[...] = ret
        rr = lax.rsqrt(jnp.mean(ret * ret, axis=-1, keepdims=True) + EPS)
        gv = g_ref[...]
        mr_ref[...] = ((gv * _sigmoid(gv)) * (ret * rr)).astype(BF16)

    col, own, state, const, _ = _ret_specs(False)
    return pl.pallas_call(
        body, name="ret_fwd", grid=(RH, RSTEPS),
        in_specs=[const, col(0), col(1), col(2), col(3)],
        out_specs=[own, own, state],
        out_shape=[jax.ShapeDtypeStruct((S, RH * RHD), F32), jax.ShapeDtypeStruct((S, RH * RHD), BF16),
                   jax.ShapeDtypeStruct((RH, NB, RHD, RHD), BF16)],
        scratch_shapes=[pltpu.VMEM((RHD, RHD), F32)],
        compiler_params=_cp(("parallel", "arbitrary")),
    )(_ret_consts(), proj, proj, proj, proj)


def _ret_bwd(proj, ret, states, dmixed):
    def body(c_ref, q_ref, k_ref, v_ref, g_ref, ret_ref, st_ref, dm_ref, dq_ref, dk_ref, dv_ref, dg_ref, g_acc, gs):
        n = pl.program_id(1)

        @pl.when(n == 0)
        def _():
            g_acc[...] = jnp.zeros_like(g_acc)

        decay, zeta, xi, gch = _ret_factors(c_ref[0:1, :])
        ret_v = ret_ref[...]
        rr = lax.rsqrt(jnp.mean(ret_v * ret_v, axis=-1, keepdims=True) + EPS)
        gv = g_ref[...]
        sg = _sigmoid(gv)
        dmix = dm_ref[...]
        dg_ref[...] = ((dmix * (ret_v * rr)) * (sg * (1.0 + gv * (1.0 - sg)))).astype(BF16)
        dretn = dmix * (gv * sg)
        dret = _chunks(rr * dretn - ret_v * ((rr * rr * rr) * jnp.mean(dretn * ret_v, axis=-1, keepdims=True)))

        q3 = _chunks(q_ref[...].astype(BF16))
        kc = _chunks(k_ref[...] * (1.0 / math.sqrt(RHD)))
        k3 = kc.astype(BF16)
        v3 = _chunks(v_ref[...].astype(BF16))
        d3 = dret.astype(BF16)
        dxi = (dret * xi[None]).astype(BF16)
        kz = (kc * zeta[None]).astype(BF16)
        dr3 = _bdot_tn(q3, dxi)
        acc = g_acc[...]
        for i in reversed(range(CBK)):
            gs[i] = acc.astype(BF16)
            acc = dr3[i] + gch * acc
        g_acc[...] = acc
        g3 = gs[...]
        sc = (_bdot_nt(q3, k3) * decay[None]).astype(BF16)
        da = (_bdot_nt(d3, v3) * decay[None]).astype(BF16)
        dq = _bdot(da, k3) + _bdot_nt(dxi, st_ref[...])
        dkc = _bdot_tn(da, q3) + _bdot_nt(v3, g3) * zeta[None]
        dv = _bdot_tn(sc, d3) + _bdot(kz, g3)
        dq_ref[...] = dq.reshape(CBK * CH, RHD).astype(BF16)
        dk_ref[...] = (dkc * (1.0 / math.sqrt(RHD))).reshape(CBK * CH, RHD).astype(BF16)
        dv_ref[...] = dv.reshape(CBK * CH, RHD).astype(BF16)

    col, own, state, const, dm = _ret_specs(True)
    return pl.pallas_call(
        body, name="ret_bwd", grid=(RH, RSTEPS),
        in_specs=[const, col(0), col(1), col(2), col(3), own, state, dm],
        out_specs=[own, own, own, own],
        out_shape=[jax.ShapeDtypeStruct((S, RH * RHD), BF16)] * 4,
        scratch_shapes=[pltpu.VMEM((RHD, RHD), F32), pltpu.VMEM((CBK, RHD, RHD), BF16)],
        compiler_params=_cp(("parallel", "arbitrary")),
    )(_ret_consts(), proj, proj, proj, proj, ret, states, dmixed)


class _NoReduction:
    def start(self, group, grads):
        pass

    def local(self, name, first=()):
        return []

    def landed(self, name):
        return []

    def update(self, name):
        return []


def _local_step(x, tgt, nw1, nw2, nw3, win, wout, wg, wu, wd, red=None):
    red = red or _NoReduction()

    def after(values, first):
        return lax.optimization_barrier((tuple(values), tuple(first)))[0]

    wg, wu, wd = (w.reshape(NFG, N_FG, D) for w in (wg, wu, wd))
    h1, r1 = _rms_fwd(x, nw1)
    proj = _proj(h1, win)
    o, ma, lse = _attn_fwd(proj)
    ret, mr, states = _ret_fwd(proj)
    x2, h2, r2 = _out_proj_rms(x, ma, mr, wout, nw2)
    g, u, a = _ffn_up(h2, wg, wu)
    dx3, dx3b, st3 = _ffn_down_loss(x2, a, wd, nw3, tgt)

    dwd = _wgrad_rows(a, dx3b, "wgrad_down")
    red.start(["w_down"], [dwd])
    (dx3b,) = after([dx3b], [dwd])
    dg, du = _ffn_down_bwd(dx3b, wd, g, u)
    dg, du = after([dg, du], red.local("w_down", first=[dg]))
    dwg = _wgrad_rows(dg, h2, "wgrad_gate")
    red.start(["w_gate"], [dwg])
    (du,) = after([du], [dwg])
    dwu = _wgrad_rows(du, h2, "wgrad_up")
    red.start(["w_up"], [dwu])
    dg, du = after([dg, du], [dwu] + red.local("w_gate"))
    dx2, dx2b, st2 = _ffn_up_bwd(dg, du, wg, wu, dx3, x2, r2, nw2)
    (dx2b,) = after([dx2b], red.local("w_up", first=[dx2b] + red.landed("w_down")))
    dwo = _wgrad_out(ma, mr, dx2b)
    red.start(["w_out"], [dwo])
    (dx2b,) = after([dx2b], [dwo])
    dmixed = _out_proj_bwd(dx2b, wout)
    dqa, dka, dva = _attn_bwd(proj, dmixed, o, lse)
    (dmixed,) = after([dmixed], [dqa] + red.landed("w_gate"))
    dqr, dkr, dvr, dgr = _ret_bwd(proj, ret, states, dmixed)
    dproj = jnp.concatenate([dqa, dka, dva, dqr, dkr, dvr, dgr], axis=1)
    (dwi,) = after([_wgrad_in(h1, dproj)], red.landed("w_up"))
    red.start(["w_in"], [dwi])
    early = red.local("w_out", first=[dwi]) + red.update("w_down") + red.update("w_gate")
    (dproj,) = after([dproj], red.local("w_in", first=early))
    gx, st1 = _in_proj_bwd(dproj, win, dx2, x, r1, nw1)
    stats = jnp.concatenate([st1[0:1], st2[0:1], st3[0:2], jnp.zeros((4, D), F32)], axis=0)
    return stats, gx, dwi, dwo, dwg, dwu, dwd


def _place():
    x, y, c = lax.axis_index("x"), lax.axis_index("y"), lax.axis_index("c")
    return x, y, c, [(1 - x, y), (x, 1 - y), (1 - x, 1 - y)]


def _handshake(peers):
    barrier = pltpu.get_barrier_semaphore()
    for peer in peers:
        pl.semaphore_signal(barrier, inc=1, device_id=peer, device_id_type=MESH)
    pl.semaphore_wait(barrier, len(peers))


def _all_gather(shards, name, collective_id):
    na = len(shards)
    SIB, XN0, XN1, YN1, YN0, VIA_X, VIA_Y = 0, 1, 2, 3, 4, 5, 6
    D2D = {XN0: 7, XN1: 8, YN1: 9, YN0: 10, VIA_X: 11, VIA_Y: 12}

    def body(*refs):
        ins, outs = refs[:na], refs[na:2 * na]
        send_sems, recv_sems, local_sems = refs[2 * na:]
        x, y, c, _ = _place()
        me, sib = (x, y, c), (x, y, 1 - c)
        xn, yn, dg = (1 - x, y, c), (x, 1 - y, c), (1 - x, 1 - y, c)
        _handshake([sib, xn, yn])

        def part(ref, h):
            rows = ref.shape[0] // 2
            return ref if h is None else ref.at[pl.ds(h * rows, rows)]

        def block(a, owner, h):
            return part(outs[a].at[4 * owner[0] + 2 * owner[1] + owner[2]], h)

        def copy(a, k, owner, h, to, own_src=False):
            return pltpu.make_async_remote_copy(
                src_ref=part(ins[a], h) if own_src else block(a, owner, h), dst_ref=block(a, owner, h),
                send_sem=send_sems.at[a, k], recv_sem=recv_sems.at[a, k], device_id=to, device_id_type=MESH)

        def other(p):
            return (p[0], p[1], 1 - c)

        mine = [pltpu.make_async_copy(ins[a], block(a, me, None), local_sems.at[a]) for a in range(na)]
        for cp in mine:
            cp.start()
        sent = []
        for a in range(na):
            sent += [copy(a, XN0, me, 0, xn, True), copy(a, YN1, me, 1, yn, True),
                     copy(a, XN1, me, 1, xn, True), copy(a, YN0, me, 0, yn, True)]
        sent += [copy(a, SIB, me, None, sib, True) for a in range(na)]
        for cp in sent:
            cp.start()

        def landed(a, k, owner, h, then):
            copy(a, k, owner, h, me).wait_recv()
            for k2, to in then + [(D2D[k], sib)]:
                cp = copy(a, k2, owner, h, to)
                cp.start()
                sent.append(cp)

        for a in range(na):
            landed(a, XN0, xn, 0, [(VIA_Y, yn)])
            landed(a, YN1, yn, 1, [(VIA_X, xn)])
            landed(a, XN1, xn, 1, [])
            landed(a, YN0, yn, 0, [])
        for a in range(na):
            landed(a, VIA_Y, dg, 0, [])
            landed(a, VIA_X, dg, 1, [])
        for a in range(na):
            copy(a, SIB, sib, None, me).wait_recv()
            for k, owner, h in ((XN0, xn, 0), (XN1, xn, 1), (YN1, yn, 1), (YN0, yn, 0), (VIA_Y, dg, 0), (VIA_X, dg, 1)):
                copy(a, D2D[k], other(owner), h, me).wait_recv()
        for cp in sent:
            cp.wait_send()
        for cp in mine:
            cp.wait()

    return _sequencer_call(
        body, name, collective_id,
        [jax.ShapeDtypeStruct((NDEV,) + s.shape, s.dtype) for s in shards],
        [pltpu.SemaphoreType.DMA((na, 13)), pltpu.SemaphoreType.DMA((na, 13)), pltpu.SemaphoreType.DMA((na,))])(*shards)


def _sequencer_call(body, name, collective_id, out_type, scratch_types):
    return pl.kernel(
        body, name=name, out_type=out_type,
        mesh=plsc.ScalarSubcoreMesh(axis_name="sequencer", num_cores=1),
        scratch_types=scratch_types,
        compiler_params=pltpu.CompilerParams(collective_id=collective_id))


def _exchange_sibling(grads, name, collective_id):
    na = len(grads)

    def body(*refs):
        ins, outs = refs[:na], refs[na:2 * na]
        send_sems, recv_sems = refs[2 * na:]
        x, y, c, _ = _place()
        _handshake([(x, y, 1 - c)])
        cps = []
        for a in range(na):
            for k in range(4):
                cps.append(pltpu.make_async_remote_copy(
                    src_ref=ins[a].at[2 * k + (1 - c)], dst_ref=outs[a].at[k],
                    send_sem=send_sems.at[a, k], recv_sem=recv_sems.at[a, k],
                    device_id=(x, y, 1 - c), device_id_type=MESH))
        for cp in cps:
            cp.start()
        for cp in cps:
            cp.wait()

    return _sequencer_call(
        body, name, collective_id,
        [jax.ShapeDtypeStruct((4,) + g.shape[1:], g.dtype) for g in grads],
        [pltpu.SemaphoreType.DMA((na, 4)), pltpu.SemaphoreType.DMA((na, 4))])(*grads)


def _row_tile(rows, cols):
    for t in (512, 256, 176, 128, 64, 32, 16):
        if rows % t == 0 and t * cols * 4 <= (2 << 20):
            return t
    raise ValueError((rows, cols))


def _chip_sum(place, g, got, name):
    _, r, c = g.shape
    tm = r

    def body(pos_ref, g_ref, got_ref, o_ref):
        o_ref[...] = (g_ref[...].astype(F32) + got_ref[...].astype(F32)).astype(BF16)

    def chip(j, pos):
        return 2 * (pos[0] ^ jnp.where(j == 1, 0, 1)) + (pos[1] ^ jnp.where(j == 0, 0, 1))

    return pl.pallas_call(
        body, name=name,
        grid_spec=pltpu.PrefetchScalarGridSpec(
            num_scalar_prefetch=1, grid=(3, r // tm),
            in_specs=[pl.BlockSpec((None, tm, c), lambda j, i, pos: (2 * chip(j, pos) + pos[2], i, 0)),
                      pl.BlockSpec((None, tm, c), lambda j, i, pos: (chip(j, pos), i, 0))],
            out_specs=pl.BlockSpec((None, tm, c), lambda j, i, pos: (j, i, 0))),
        out_shape=jax.ShapeDtypeStruct((3, r, c), BF16),
        compiler_params=_cp(("parallel", "parallel")),
    )(place, g, got)


def _exchange_chips(sums, name, collective_id):
    na = len(sums)

    def body(*refs):
        ins, outs = refs[:na], refs[na:2 * na]
        send_sems, recv_sems = refs[2 * na:]
        x, y, c, chips = _place()
        _handshake([(*chip, c) for chip in chips])
        cps = []
        for a in range(na):
            for j, chip in enumerate(chips):
                cps.append(pltpu.make_async_remote_copy(
                    src_ref=ins[a].at[j], dst_ref=outs[a].at[j],
                    send_sem=send_sems.at[a, j], recv_sem=recv_sems.at[a, j],
                    device_id=(*chip, c), device_id_type=MESH))
        for cp in cps:
            cp.start()
        for cp in cps:
            cp.wait()

    return _sequencer_call(
        body, name, collective_id,
        [jax.ShapeDtypeStruct((3,) + s.shape[1:], s.dtype) for s in sums],
        [pltpu.SemaphoreType.DMA((na, 3)), pltpu.SemaphoreType.DMA((na, 3))])(*sums)


def _exchange_stats(stats, collective_id):
    def body(st_in, st_out, st_send, st_recv, local_sem):
        x, y, c, _ = _place()
        me_idx = 4 * x + 2 * y + c
        peers = [(x ^ ((k >> 2) & 1), y ^ ((k >> 1) & 1), c ^ (k & 1)) for k in range(1, 8)]
        _handshake(peers)
        mine = pltpu.make_async_copy(st_in, st_out.at[me_idx], local_sem)
        mine.start()
        cps = [pltpu.make_async_remote_copy(
            src_ref=st_in, dst_ref=st_out.at[me_idx], send_sem=st_send.at[k], recv_sem=st_recv.at[k],
            device_id=peer, device_id_type=MESH) for k, peer in enumerate(peers)]
        for cp in cps:
            cp.start()
        for cp in cps:
            cp.wait()
        mine.wait()

    return _sequencer_call(
        body, "exchange_stats", collective_id,
        jax.ShapeDtypeStruct((NDEV,) + stats.shape, stats.dtype),
        [pltpu.SemaphoreType.DMA((7,)), pltpu.SemaphoreType.DMA((7,)), pltpu.SemaphoreType.DMA])(stats)


class _Reduction:
    def __init__(self, place, first_collective_id, state):
        self.place = place
        self.ids = iter(range(first_collective_id, 32))
        self.state = state
        self.groups = {}
        self.updates = {}

    def next_id(self):
        return next(self.ids)

    def start(self, group, grads):
        got = _exchange_sibling(grads, "sibling_exchange_" + group[0], self.next_id())
        self.groups[group[0]] = dict(names=group, grads=grads, got=got)

    def local(self, name, first=()):
        grp = self.groups[name]
        grads = lax.optimization_barrier((tuple(grp["grads"]), tuple(first)))[0]
        grp["sums"] = [_chip_sum(self.place, g, s, "chip_sum_" + n)
                       for g, s, n in zip(grads, grp["got"], grp["names"])]
        grp["chips"] = _exchange_chips(grp["sums"], "chip_exchange_" + name, self.next_id())
        return grp["sums"]

    def landed(self, name):
        return list(self.groups[name]["chips"])

    def update(self, name):
        if name not in self.updates:
            grp = next(g for g in self.groups.values() if name in g["names"])
            k = grp["names"].index(name)
            self.updates[name] = _shard_update(self.place, *self.state[name], grp["grads"][k], grp["got"][k],
                                               grp["chips"][k], "update_" + name)
        return list(self.updates[name])


def _adamw(w, g, m, v):
    m = ADAM_B1 * m + (1.0 - ADAM_B1) * g
    v = ADAM_B2 * v + (1.0 - ADAM_B2) * (g * g)
    m_hat = m / (1.0 - ADAM_B1 ** ADAM_STEP)
    v_hat = v / (1.0 - ADAM_B2 ** ADAM_STEP)
    delta = -ADAM_LR * (m_hat / (jnp.sqrt(v_hat) + ADAM_EPS) + ADAM_WD * w)
    return delta, m, v


def _shard_update(place, w, m, v, g, got_sib, got_chips, name):
    r, c = w.shape
    tm = _row_tile(r, c)

    def body(pos_ref, w_ref, m_ref, v_ref, g_ref, s_ref, c_ref, go_ref, d_ref, mo_ref, vo_ref):
        grad = g_ref[...].astype(F32) + s_ref[...].astype(F32)
        for j in range(3):
            grad = grad + c_ref[j].astype(F32)
        delta, mn, vn = _adamw(w_ref[...], grad, m_ref[...], v_ref[...])
        go_ref[...] = grad
        d_ref[...] = delta
        mo_ref[...] = mn
        vo_ref[...] = vn

    row = pl.BlockSpec((tm, c), lambda i, pos: (i, 0))
    return pl.pallas_call(
        body, name=name,
        grid_spec=pltpu.PrefetchScalarGridSpec(
            num_scalar_prefetch=1, grid=(r // tm,),
            in_specs=[row, row, row,
                      pl.BlockSpec((None, tm, c), lambda i, pos: (4 * pos[0] + 2 * pos[1] + pos[2], i, 0)),
                      pl.BlockSpec((None, tm, c), lambda i, pos: (2 * pos[0] + pos[1], i, 0)),
                      pl.BlockSpec((3, tm, c), lambda i, pos: (0, i, 0))],
            out_specs=[row, row, row, row]),
        out_shape=[jax.ShapeDtypeStruct((r, c), F32)] * 4,
        compiler_params=_cp(("parallel",)),
    )(place, w, m, v, g, got_sib, got_chips)


def _small_update(stats_all, ws, ms, vs):
    def body(st_ref, w_ref, m_ref, v_ref, go_ref, d_ref, mo_ref, vo_ref):
        grad = st_ref[0]
        for k in range(1, NDEV):
            grad = grad + st_ref[k]
        delta, mn, vn = _adamw(w_ref[...], grad, m_ref[...], v_ref[...])
        go_ref[...] = grad
        d_ref[...] = delta
        mo_ref[...] = mn
        vo_ref[...] = vn

    return pl.pallas_call(
        body, name="small_update",
        out_shape=[jax.ShapeDtypeStruct((8, D), F32)] * 4,
        compiler_params=_cp(),
    )(stats_all, ws, ms, vs)


def kernel(x, norm_mix_w, w_in, w_out, norm_ffn_w, w_gate, w_up, w_down, norm_final_w, loss_target, m_norm_mix_w, m_w_in, m_w_out, m_norm_ffn_w, m_w_gate, m_w_up, m_w_down, m_norm_final_w, v_norm_mix_w, v_w_in, v_w_out, v_norm_ffn_w, v_w_gate, v_w_up, v_w_down, v_norm_final_w):
    tr = {"w_gate", "w_up"}
    names = ["w_in", "w_out", "w_gate", "w_up", "w_down"]

    def view(a, n):
        return a[0].T if n in tr else a[0]

    big_w = [view(a, n) for a, n in zip([w_in, w_out, w_gate, w_up, w_down], names)]
    big_m = [view(a, n) for a, n in zip([m_w_in, m_w_out, m_w_gate, m_w_up, m_w_down], names)]
    big_v = [view(a, n) for a, n in zip([v_w_in, v_w_out, v_w_gate, v_w_up, v_w_down], names)]

    shards = [_cast_bf16(w, "cast_" + n) for w, n in zip(big_w, names)]
    (win,) = _all_gather(shards[0:1], "all_gather_w_in", 1)
    wout, wg, wu = _all_gather(shards[1:4], "all_gather_out_gate_up", 2)
    (wd,) = _all_gather(shards[4:5], "all_gather_w_down", 3)
    nw3 = norm_final_w.reshape(1, D)
    place = jnp.stack([lax.axis_index("x"), lax.axis_index("y"), lax.axis_index("c")]).astype(jnp.int32)
    red = _Reduction(place, 4, {n: (w, m, v) for n, w, m, v in zip(names, big_w, big_m, big_v)})
    stats, gx, *_ = _local_step(
        x[0], loss_target[0], norm_mix_w, norm_ffn_w, nw3, win, wout.reshape(D, D), wg, wu, wd, red)
    stats_all = _exchange_stats(stats, red.next_id())
    upd = [red.update(n) for n in names]
    stats_all = lax.optimization_barrier((stats_all, tuple(upd[0])))[0]

    def rows(a, b, c):
        return jnp.concatenate([a.reshape(1, D), b.reshape(1, D), c.reshape(1, D), jnp.zeros((5, D), F32)], axis=0)

    sg, sd, sm, sv = _small_update(stats_all, rows(norm_mix_w, norm_ffn_w, norm_final_w),
                                   rows(m_norm_mix_w, m_norm_ffn_w, m_norm_final_w),
                                   rows(v_norm_mix_w, v_norm_ffn_w, v_norm_final_w))
    loss = sg[3, 0]

    def outs(k, small):
        big = [(u[k].T if n in tr else u[k])[None] for u, n in zip(upd, names)]
        return [small[0:1], big[0], big[1], small[1:2], big[2], big[3], big[4], small[2]]

    return (loss, gx[None], *outs(0, sg), *outs(1, sd), *outs(2, sm), *outs(3, sv))
```

```python
import functools
import math

import numpy as np
import jax
import jax.numpy as jnp
from jax import lax
from jax.experimental import pallas as pl
from jax.experimental.pallas import tpu as pltpu
from jax.experimental.pallas import tpu_sc as plsc

F32 = jnp.float32
BF16 = jnp.bfloat16

S = 2048
D = 2048
NDEV = 8
N_IN = 7168 // NDEV
N_FF = 5632 // NDEV
NFG, N_FG = NDEV // 2, 2 * N_FF
N_OUT = 2048 // NDEV
AH, AHD = 8, 128
RH, RHD = 4, 256
CH = 128
NB = S // CH
EPS = 1e-6
PATTERNS = ((1, 16), (4, 4), (16, 1))
NEG = -1e30
VMEM_LIMIT = 56 * 1024 * 1024

ADAM_LR, ADAM_B1, ADAM_B2, ADAM_EPS, ADAM_WD, ADAM_STEP = 0.001, 0.9, 0.999, 1e-08, 0.01, 10
MESH = pl.DeviceIdType.MESH


def _cp(sem=None):
    return pltpu.CompilerParams(dimension_semantics=sem, vmem_limit_bytes=VMEM_LIMIT)


def _dot(a, b):
    return jnp.dot(a, b, preferred_element_type=F32)


def _dot_nt(a, b):
    return lax.dot_general(a, b, (((1,), (1,)), ((), ())), preferred_element_type=F32)


def _dot_tn(a, b):
    return lax.dot_general(a, b, (((0,), (0,)), ((), ())), preferred_element_type=F32)


def _sigmoid(x):
    return 0.5 * jnp.tanh(0.5 * x) + 0.5


def _cast_bf16(w, name):
    r, c = w.shape
    tm = r if r <= 1024 else 512

    def body(w_ref, o_ref):
        o_ref[...] = w_ref[...].astype(BF16)

    return pl.pallas_call(
        body, name=name, grid=(r // tm,),
        in_specs=[pl.BlockSpec((tm, c), lambda i: (i, 0))],
        out_specs=pl.BlockSpec((tm, c), lambda i: (i, 0)),
        out_shape=jax.ShapeDtypeStruct((r, c), BF16),
        compiler_params=_cp(("parallel",)),
    )(w)


def _rms_fwd(x, nw):
    tm = 256

    def body(x_ref, w_ref, h_ref, r_ref):
        xs = x_ref[...]
        r = lax.rsqrt(jnp.mean(xs * xs, axis=-1, keepdims=True) + EPS)
        h_ref[...] = ((xs * r) * w_ref[...]).astype(BF16)
        r_ref[...] = r

    return pl.pallas_call(
        body, name="rms_fwd", grid=(S // tm,),
        in_specs=[pl.BlockSpec((tm, D), lambda i: (i, 0)), pl.BlockSpec((1, D), lambda i: (0, 0))],
        out_specs=[pl.BlockSpec((tm, D), lambda i: (i, 0)), pl.BlockSpec((tm, 1), lambda i: (i, 0))],
        out_shape=[jax.ShapeDtypeStruct((S, D), BF16), jax.ShapeDtypeStruct((S, 1), F32)],
        compiler_params=_cp(("parallel",)),
    )(x, nw)


def _row_copies(hbm_refs, bufs, sems, m, tm):
    rows = pl.ds(pl.multiple_of(m * tm, tm), tm)
    return [pltpu.make_async_copy(h.at[rows], b, sems.at[i]) for i, (h, b) in enumerate(zip(hbm_refs, bufs))]


def _rms_bwd_tile(dh, xs, r, nw):
    dnw = jnp.sum(dh * (xs * r), axis=0, keepdims=True)
    gy = dh * nw
    dx = r * gy - xs * ((r * r * r) * jnp.mean(gy * xs, axis=-1, keepdims=True))
    return dx, dnw


def _proj(h1, win):
    tm = 1024

    def body(a_ref, w_ref, o_ref):
        o_ref[...] = _dot(a_ref[...], w_ref[...])

    return pl.pallas_call(
        body, name="proj", grid=(NDEV, S // tm),
        in_specs=[pl.BlockSpec((tm, D), lambda p, m: (m, 0)),
                  pl.BlockSpec((None, D, N_IN), lambda p, m: (p, 0, 0))],
        out_specs=pl.BlockSpec((tm, N_IN), lambda p, m: (m, p)),
        out_shape=jax.ShapeDtypeStruct((S, NDEV * N_IN), F32),
        compiler_params=_cp(("parallel", "parallel")),
    )(h1, win)


def _out_proj_rms(x, ma, mr, wout, nw):
    tm = 256
    half = D // 2

    def body(x_ref, ma_ref, mr_ref, w_ref, nw_ref, x2_ref, h_ref, r_ref):
        acc = _dot(ma_ref[...], w_ref[0:half, :]) + _dot(mr_ref[...], w_ref[half:D, :])
        x2 = x_ref[...] + acc
        r = lax.rsqrt(jnp.mean(x2 * x2, axis=-1, keepdims=True) + EPS)
        x2_ref[...] = x2
        h_ref[...] = ((x2 * r) * nw_ref[...]).astype(BF16)
        r_ref[...] = r

    return pl.pallas_call(
        body, name="out_proj_rms", grid=(S // tm,),
        in_specs=[pl.BlockSpec((tm, D), lambda i: (i, 0)),
                  pl.BlockSpec((tm, half), lambda i: (i, 0)),
                  pl.BlockSpec((tm, half), lambda i: (i, 0)),
                  pl.BlockSpec((D, D), lambda i: (0, 0)),
                  pl.BlockSpec((1, D), lambda i: (0, 0))],
        out_specs=[pl.BlockSpec((tm, D), lambda i: (i, 0)), pl.BlockSpec((tm, D), lambda i: (i, 0)),
                   pl.BlockSpec((tm, 1), lambda i: (i, 0))],
        out_shape=[jax.ShapeDtypeStruct((S, D), F32), jax.ShapeDtypeStruct((S, D), BF16),
                   jax.ShapeDtypeStruct((S, 1), F32)],
        compiler_params=_cp(("parallel",)),
    )(x, ma, mr, wout, nw)


def _ffn_up(h2, wg, wu):
    tm = 512

    def body(h_ref, wg_ref, wu_ref, g_ref, u_ref, a_ref):
        h = h_ref[...]
        g = _dot_nt(h, wg_ref[...])
        u = _dot_nt(h, wu_ref[...])
        g_ref[...] = g
        u_ref[...] = u
        a_ref[...] = ((g * _sigmoid(g)) * u).astype(BF16)

    blk = pl.BlockSpec((None, tm, N_FG), lambda p, m: (p, m, 0))
    wblk = pl.BlockSpec((None, N_FG, D), lambda p, m: (p, 0, 0))
    return pl.pallas_call(
        body, name="ffn_up", grid=(NFG, S // tm),
        in_specs=[pl.BlockSpec((tm, D), lambda p, m: (m, 0)), wblk, wblk],
        out_specs=[blk, blk, blk],
        out_shape=[jax.ShapeDtypeStruct((NFG, S, N_FG),F32), jax.ShapeDtypeStruct((NFG, S, N_FG),F32),
                   jax.ShapeDtypeStruct((NFG, S, N_FG),BF16)],
        compiler_params=_cp(("parallel", "parallel")),
    )(h2, wg, wu)


def _ffn_down_loss(x2, a, wd, nw, tgt):
    tm = 512

    def body(x2_hbm, a_ref, w_ref, nw_ref, t_hbm, dx_ref, dxb_ref, st_ref, acc_ref, x2_buf, t_buf, sems):
        m, p = pl.program_id(0), pl.program_id(1)
        tail_in = _row_copies((x2_hbm, t_hbm), (x2_buf, t_buf), sems, m, tm)

        @pl.when(p == 0)
        def _():
            acc_ref[...] = jnp.zeros_like(acc_ref)
            for cp in tail_in:
                cp.start()

        @pl.when((p == 0) & (m == 0))
        def _():
            st_ref[...] = jnp.zeros_like(st_ref)

        acc_ref[...] += _dot(a_ref[...], w_ref[...])

        @pl.when(p == NFG - 1)
        def _():
            for cp in tail_in:
                cp.wait()
            x3 = x2_buf[...] + acc_ref[...]
            nwv = nw_ref[...]
            r = lax.rsqrt(jnp.mean(x3 * x3, axis=-1, keepdims=True) + EPS)
            y = (x3 * r) * nwv
            err = y - t_buf[...]
            loss = 0.5 * jnp.sum(jnp.mean(err * err, axis=-1, keepdims=True), axis=0, keepdims=True)
            dy = err * (1.0 / D)
            dx, dnw = _rms_bwd_tile(dy, x3, r, nwv)
            dx_ref[...] = dx
            dxb_ref[...] = dx.astype(BF16)
            st_ref[0:1, :] += dnw
            st_ref[1:2, :] += jnp.broadcast_to(loss, (1, D))

    return pl.pallas_call(
        body, name="ffn_down_loss", grid=(S // tm, NFG),
        in_specs=[pl.BlockSpec(memory_space=pl.ANY),
                  pl.BlockSpec((None, tm, N_FG), lambda m, p: (p, m, 0)),
                  pl.BlockSpec((None, N_FG, D), lambda m, p: (p, 0, 0)),
                  pl.BlockSpec((1, D), lambda m, p: (0, 0)),
                  pl.BlockSpec(memory_space=pl.ANY)],
        out_specs=[pl.BlockSpec((tm, D), lambda m, p: (m, 0)), pl.BlockSpec((tm, D), lambda m, p: (m, 0)),
                   pl.BlockSpec((8, D), lambda m, p: (0, 0))],
        out_shape=[jax.ShapeDtypeStruct((S, D), F32), jax.ShapeDtypeStruct((S, D), BF16),
                   jax.ShapeDtypeStruct((8, D), F32)],
        scratch_shapes=[pltpu.VMEM((tm, D), F32), pltpu.VMEM((tm, D), F32), pltpu.VMEM((tm, D), F32),
                        pltpu.SemaphoreType.DMA((2,))],
        compiler_params=_cp(("arbitrary", "arbitrary")),
    )(x2, a, wd, nw, tgt)


def _ffn_down_bwd(dx3b, wd, g, u):
    tm = 512

    def body(dx_ref, w_ref, g_ref, u_ref, dg_ref, du_ref):
        da = _dot_nt(dx_ref[...], w_ref[...])
        gv = g_ref[...]
        sg = _sigmoid(gv)
        silu = gv * sg
        dg_ref[...] = ((da * u_ref[...]) * (sg * (1.0 + gv * (1.0 - sg)))).astype(BF16)
        du_ref[...] = (da * silu).astype(BF16)

    blk = pl.BlockSpec((None, tm, N_FG), lambda p, m: (p, m, 0))
    return pl.pallas_call(
        body, name="ffn_down_bwd", grid=(NFG, S // tm),
        in_specs=[pl.BlockSpec((tm, D), lambda p, m: (m, 0)),
                  pl.BlockSpec((None, N_FG, D), lambda p, m: (p, 0, 0)), blk, blk],
        out_specs=[blk, blk],
        out_shape=[jax.ShapeDtypeStruct((NFG, S, N_FG),BF16), jax.ShapeDtypeStruct((NFG, S, N_FG),BF16)],
        compiler_params=_cp(("parallel", "parallel")),
    )(dx3b, wd, g, u)


def _ffn_up_bwd(dg, du, wg, wu, dres, xs, r, nw):
    tm = 512

    def body(dg_ref, du_ref, wg_ref, wu_ref, dres_hbm, x_hbm, r_ref, nw_ref, dx_ref, dxb_ref, st_ref,
             dres_buf, x_buf, sems):
        m, p = pl.program_id(0), pl.program_id(1)
        tail_in = _row_copies((dres_hbm, x_hbm), (dres_buf, x_buf), sems, m, tm)

        @pl.when(p == 0)
        def _():
            dx_ref[...] = jnp.zeros_like(dx_ref)
            for cp in tail_in:
                cp.start()

        @pl.when((p == 0) & (m == 0))
        def _():
            st_ref[...] = jnp.zeros_like(st_ref)

        dx_ref[...] += _dot(dg_ref[...], wg_ref[...])
        dx_ref[...] += _dot(du_ref[...], wu_ref[...])

        @pl.when(p == NFG - 1)
        def _():
            for cp in tail_in:
                cp.wait()
            dx, dnw = _rms_bwd_tile(dx_ref[...], x_buf[...], r_ref[...], nw_ref[...])
            dx = dres_buf[...] + dx
            dx_ref[...] = dx
            dxb_ref[...] = dx.astype(BF16)
            st_ref[0:1, :] += dnw

    blk = pl.BlockSpec((None, tm, N_FG), lambda m, p: (p, m, 0))
    wblk = pl.BlockSpec((None, N_FG, D), lambda m, p: (p, 0, 0))
    row = pl.BlockSpec((tm, D), lambda m, p: (m, 0))
    hbm = pl.BlockSpec(memory_space=pl.ANY)
    return pl.pallas_call(
        body, name="ffn_up_bwd", grid=(S // tm, NFG),
        in_specs=[blk, blk, wblk, wblk, hbm, hbm, pl.BlockSpec((tm, 1), lambda m, p: (m, 0)),
                  pl.BlockSpec((1, D), lambda m, p: (0, 0))],
        out_specs=[row, row, pl.BlockSpec((8, D), lambda m, p: (0, 0))],
        out_shape=[jax.ShapeDtypeStruct((S, D), F32), jax.ShapeDtypeStruct((S, D), BF16),
                   jax.ShapeDtypeStruct((8, D), F32)],
        scratch_shapes=[pltpu.VMEM((tm, D), F32), pltpu.VMEM((tm, D), F32), pltpu.SemaphoreType.DMA((2,))],
        compiler_params=_cp(("arbitrary", "arbitrary")),
    )(dg, du, wg, wu, dres, xs, r, nw)


def _out_proj_bwd(dx2b, wout):
    tm = 256

    def body(dx_ref, w_ref, o_ref):
        o_ref[...] = _dot_nt(dx_ref[...], w_ref[...])

    return pl.pallas_call(
        body, name="out_proj_bwd", grid=(S // tm,),
        in_specs=[pl.BlockSpec((tm, D), lambda i: (i, 0)), pl.BlockSpec((D, D), lambda i: (0, 0))],
        out_specs=pl.BlockSpec((tm, D), lambda i: (i, 0)),
        out_shape=jax.ShapeDtypeStruct((S, D), F32),
        compiler_params=_cp(("parallel",)),
    )(dx2b, wout)


def _in_proj_bwd(dproj, win, dres, xs, r, nw):
    tm = 1024

    def body(dp_ref, w_ref, dres_hbm, x_hbm, r_ref, nw_ref, dx_ref, st_ref, dres_buf, x_buf, sems):
        m, p = pl.program_id(0), pl.program_id(1)
        tail_in = _row_copies((dres_hbm, x_hbm), (dres_buf, x_buf), sems, m, tm)

        @pl.when(p == 0)
        def _():
            dx_ref[...] = jnp.zeros_like(dx_ref)
            for cp in tail_in:
                cp.start()

        @pl.when((p == 0) & (m == 0))
        def _():
            st_ref[...] = jnp.zeros_like(st_ref)

        dx_ref[...] += _dot_nt(dp_ref[...], w_ref[...])

        @pl.when(p == NDEV - 1)
        def _():
            for cp in tail_in:
                cp.wait()
            dx, dnw = _rms_bwd_tile(dx_ref[...], x_buf[...], r_ref[...], nw_ref[...])
            dx_ref[...] = dres_buf[...] + dx
            st_ref[0:1, :] += dnw

    row = pl.BlockSpec((tm, D), lambda m, p: (m, 0))
    hbm = pl.BlockSpec(memory_space=pl.ANY)
    return pl.pallas_call(
        body, name="in_proj_bwd", grid=(S // tm, NDEV),
        in_specs=[pl.BlockSpec((tm, N_IN), lambda m, p: (m, p)),
                  pl.BlockSpec((None, D, N_IN), lambda m, p: (p, 0, 0)),
                  hbm, hbm, pl.BlockSpec((tm, 1), lambda m, p: (m, 0)),
                  pl.BlockSpec((1, D), lambda m, p: (0, 0))],
        out_specs=[row, pl.BlockSpec((8, D), lambda m, p: (0, 0))],
        out_shape=[jax.ShapeDtypeStruct((S, D), F32), jax.ShapeDtypeStruct((8, D), F32)],
        scratch_shapes=[pltpu.VMEM((tm, D), F32), pltpu.VMEM((tm, D), F32), pltpu.SemaphoreType.DMA((2,))],
        compiler_params=_cp(("arbitrary", "arbitrary")),
    )(dproj, win, dres, xs, r, nw)


def _wgrad_in(h1, dproj):
    def body(a_ref, d_ref, o_ref):
        o_ref[...] = _dot_tn(a_ref[...], d_ref[...]).astype(BF16)

    return pl.pallas_call(
        body, name="wgrad_in", grid=(NDEV,),
        in_specs=[pl.BlockSpec((S, D), lambda p: (0, 0)), pl.BlockSpec((S, N_IN), lambda p: (0, p))],
        out_specs=pl.BlockSpec((None, D, N_IN), lambda p: (p, 0, 0)),
        out_shape=jax.ShapeDtypeStruct((NDEV, D, N_IN), BF16),
        compiler_params=_cp(("parallel",)),
    )(h1, dproj)


def _wgrad_rows(a3, dy, name):
    def body(a_ref, d_ref, o_ref):
        o_ref[...] = _dot_tn(a_ref[...], d_ref[...]).astype(BF16)

    return pl.pallas_call(
        body, name=name, grid=(NFG,),
        in_specs=[pl.BlockSpec((None, S, N_FG), lambda p: (p, 0, 0)), pl.BlockSpec((S, D), lambda p: (0, 0))],
        out_specs=pl.BlockSpec((None, N_FG, D), lambda p: (p, 0, 0)),
        out_shape=jax.ShapeDtypeStruct((NFG, N_FG, D), BF16),
        compiler_params=_cp(("parallel",)),
    )(a3, dy).reshape(NDEV, N_FF, D)


def _wgrad_out(ma, mr, dx2b):
    half = D // 2
    per = half // N_OUT

    def body(ma_ref, mr_ref, d_ref, o_ref):
        p = pl.program_id(0)

        @pl.when(p < per)
        def _():
            o_ref[...] = _dot_tn(ma_ref[...], d_ref[...]).astype(BF16)

        @pl.when(p >= per)
        def _():
            o_ref[...] = _dot_tn(mr_ref[...], d_ref[...]).astype(BF16)

    return pl.pallas_call(
        body, name="wgrad_out", grid=(NDEV,),
        in_specs=[pl.BlockSpec((S, N_OUT), lambda p: (0, jnp.minimum(p, per - 1))),
                  pl.BlockSpec((S, N_OUT), lambda p: (0, jnp.maximum(p - per, 0))),
                  pl.BlockSpec((S, D), lambda p: (0, 0))],
        out_specs=pl.BlockSpec((None, N_OUT, D), lambda p: (p, 0, 0)),
        out_shape=jax.ShapeDtypeStruct((NDEV, N_OUT, D), BF16),
        compiler_params=_cp(("parallel",)),
    )(ma, mr, dx2b)


def _attn_consts():
    c = np.zeros((AH, 8, AHD), np.float32)
    for h in range(AH):
        c[h, :, :] = 2.0 ** (-(h + 1))
    return jnp.asarray(c)


def _permute_in(dst, src, d, cast=None):
    ln = S // d
    for rr in range(d):
        v = src[pl.ds(rr, ln, stride=d), :] if d > 1 else src[...]
        dst[rr * ln:(rr + 1) * ln, :] = v if cast is None else v.astype(cast)


def _attn_masks():
    qi = lax.broadcasted_iota(jnp.int32, (CH, CH), 0)
    kj = lax.broadcasted_iota(jnp.int32, (CH, CH), 1)
    dist_c = (qi - kj).astype(F32)
    dist_p = (qi - kj + CH).astype(F32)
    return (qi >= kj)[None], (kj >= qi)[None], dist_c[None], dist_p[None]


GB = 8


def _bdot_nt(a, b):
    return lax.dot_general(a, b, (((2,), (2,)), ((0,), (0,))), preferred_element_type=F32)


def _bdot(a, b):
    return lax.dot_general(a, b, (((2,), (1,)), ((0,), (0,))), preferred_element_type=F32)


def _bdot_tn(a, b):
    return lax.dot_general(a, b, (((1,), (1,)), ((0,), (0,))), preferred_element_type=F32)


def _shift_block(dst, src):
    dst[0:CH, :] = jnp.zeros((CH, AHD), dst.dtype)
    dst[CH:S, :] = src[0:S - CH, :]


def _has_prev(g, nb):
    blk = lax.broadcasted_iota(jnp.int32, (GB, 1, 1), 0) + g * GB
    return (blk & (nb - 1)) != 0


def _blocks(ref, g):
    return ref[g * GB * CH:(g + 1) * GB * CH, :].reshape(GB, CH, AHD)


def _attn_fwd(proj):
    scale = 1.0 / math.sqrt(AHD)

    def body(c_ref, q_ref, k_ref, v_ref, o_ref, ob_ref, lse_ref, qd, kd, vd, kps, vps, od, ld, *nat):
        onat, lnat = nat[0:3], nat[3:6]
        slope = c_ref[0:1, :]
        mask_c, mask_p, dist_c, dist_p = _attn_masks()
        for pi, (d, nb) in enumerate(PATTERNS):
            _permute_in(qd, q_ref, d, BF16)
            _permute_in(kd, k_ref, d, BF16)
            _permute_in(vd, v_ref, d, BF16)
            if nb > 1:
                _shift_block(kps, kd)
                _shift_block(vps, vd)
            bias_c = -(slope * float(d)) * dist_c
            bias_p = -(slope * float(d)) * dist_p
            for g in range(NB // GB):
                q3, k3, v3 = _blocks(qd, g), _blocks(kd, g), _blocks(vd, g)
                s_c = jnp.where(mask_c, _bdot_nt(q3, k3) * scale + bias_c, NEG)
                mx = jnp.max(s_c, axis=-1, keepdims=True)
                if nb > 1:
                    kp3, vp3 = _blocks(kps, g), _blocks(vps, g)
                    s_p = jnp.where(jnp.logical_and(mask_p, _has_prev(g, nb)),
                                    _bdot_nt(q3, kp3) * scale + bias_p, NEG)
                    mx = jnp.maximum(mx, jnp.max(s_p, axis=-1, keepdims=True))
                    l = (jnp.sum(jnp.exp(s_c - mx), axis=-1, keepdims=True)
                         + jnp.sum(jnp.exp(s_p - mx), axis=-1, keepdims=True))
                    lse = mx + jnp.log(l)
                    o3 = _bdot(jnp.exp(s_c - lse).astype(BF16), v3) + _bdot(jnp.exp(s_p - lse).astype(BF16), vp3)
                else:
                    l = jnp.sum(jnp.exp(s_c - mx), axis=-1, keepdims=True)
                    lse = mx + jnp.log(l)
                    o3 = _bdot(jnp.exp(s_c - lse).astype(BF16), v3)
                rows = slice(g * GB * CH, (g + 1) * GB * CH)
                od[rows, :] = o3.reshape(GB * CH, AHD)
                ld[rows, :] = jnp.broadcast_to(lse, (GB, CH, AHD)).reshape(GB * CH, AHD)
            ln = S // d
            for rr in range(d):
                if d > 1:
                    onat[pi][pl.ds(rr, ln, stride=d), :] = od[rr * ln:(rr + 1) * ln, :]
                    lnat[pi][pl.ds(rr, ln, stride=d), :] = ld[rr * ln:(rr + 1) * ln, :]
                else:
                    onat[pi][...] = od[...]
                    lnat[pi][...] = ld[...]
        l0, l1, l2 = lnat[0][...], lnat[1][...], lnat[2][...]
        mx = jnp.maximum(jnp.maximum(l0, l1), l2)
        e0, e1, e2 = jnp.exp(l0 - mx), jnp.exp(l1 - mx), jnp.exp(l2 - mx)
        den = e0 + e1 + e2
        out = (e0 / den) * onat[0][...] + (e1 / den) * onat[1][...] + (e2 / den) * onat[2][...]
        o_ref[...] = out
        ob_ref[...] = out.astype(BF16)
        lse_ref[...] = mx + jnp.log(den)

    def col(off):
        return pl.BlockSpec((S, AHD), lambda h: (0, off + h))

    return pl.pallas_call(
        body, name="attn_fwd", grid=(AH,),
        in_specs=[pl.BlockSpec((None, 8, AHD), lambda h: (h, 0, 0)), col(0), col(AH), col(2 * AH)],
        out_specs=[col(0), col(0), col(0)],
        out_shape=[jax.ShapeDtypeStruct((S, AH * AHD), F32), jax.ShapeDtypeStruct((S, AH * AHD), BF16),
                   jax.ShapeDtypeStruct((S, AH * AHD), F32)],
        scratch_shapes=[pltpu.VMEM((S, AHD), BF16) for _ in range(5)]
        + [pltpu.VMEM((S, AHD), F32) for _ in range(8)],
        compiler_params=_cp(("parallel",)),
    )(_attn_consts(), proj, proj, proj)


def _attn_bwd(proj, dmixed, o, lse):
    scale = 1.0 / math.sqrt(AHD)

    def body(c_ref, q_ref, k_ref, v_ref, do_ref, o_ref, lse_ref, dq_ref, dk_ref, dv_ref,
             qd, kd, vd, dod, kps, vps, lsd, dld, dqd, dkd, dvd, delta, aq, ak, av):
        slope = c_ref[0:1, :]
        mask_c, mask_p, dist_c, dist_p = _attn_masks()
        delta[...] = jnp.broadcast_to(jnp.sum(do_ref[...] * o_ref[...], axis=-1, keepdims=True), (S, AHD))
        for pi, (d, nb) in enumerate(PATTERNS):
            _permute_in(qd, q_ref, d, BF16)
            _permute_in(kd, k_ref, d, BF16)
            _permute_in(vd, v_ref, d, BF16)
            _permute_in(dod, do_ref, d, BF16)
            _permute_in(lsd, lse_ref, d)
            _permute_in(dld, delta, d)
            if nb > 1:
                _shift_block(kps, kd)
                _shift_block(vps, vd)
            bias_c = -(slope * float(d)) * dist_c
            bias_p = -(slope * float(d)) * dist_p
            for g in range(NB // GB):
                q3, k3, v3, do3 = _blocks(qd, g), _blocks(kd, g), _blocks(vd, g), _blocks(dod, g)
                ls, dl = _blocks(lsd, g), _blocks(dld, g)
                lo, hi = g * GB * CH, (g + 1) * GB * CH
                p_c = jnp.exp(jnp.where(mask_c, _bdot_nt(q3, k3) * scale + bias_c, NEG) - ls)
                ds_c = ((p_c * (_bdot_nt(do3, v3) - dl)) * scale).astype(BF16)
                dq3 = _bdot(ds_c, k3)
                dkd[lo:hi, :] = _bdot_tn(ds_c, q3).reshape(GB * CH, AHD)
                dvd[lo:hi, :] = _bdot_tn(p_c.astype(BF16), do3).reshape(GB * CH, AHD)
                if nb > 1:
                    kp3, vp3 = _blocks(kps, g), _blocks(vps, g)
                    p_p = jnp.exp(jnp.where(jnp.logical_and(mask_p, _has_prev(g, nb)),
                                            _bdot_nt(q3, kp3) * scale + bias_p, NEG) - ls)
                    ds_p = ((p_p * (_bdot_nt(do3, vp3) - dl)) * scale).astype(BF16)
                    dq3 = dq3 + _bdot(ds_p, kp3)
                    dkp = _bdot_tn(ds_p, q3).reshape(GB * CH, AHD)
                    dvp = _bdot_tn(p_p.astype(BF16), do3).reshape(GB * CH, AHD)
                    if g == 0:
                        dkd[0:hi - CH, :] += dkp[CH:, :]
                        dvd[0:hi - CH, :] += dvp[CH:, :]
                    else:
                        dkd[lo - CH:hi - CH, :] += dkp
                        dvd[lo - CH:hi - CH, :] += dvp
                dqd[lo:hi, :] = dq3.reshape(GB * CH, AHD)
            ln = S // d
            for acc, src in ((aq, dqd), (ak, dkd), (av, dvd)):
                if pi == 0:
                    acc[...] = src[...]
                else:
                    for rr in range(d):
                        acc[pl.ds(rr, ln, stride=d), :] += src[rr * ln:(rr + 1) * ln, :]
        dq_ref[...] = aq[...].astype(BF16)
        dk_ref[...] = ak[...].astype(BF16)
        dv_ref[...] = av[...].astype(BF16)

    def col(off):
        return pl.BlockSpec((S, AHD), lambda h: (0, off + h))

    return pl.pallas_call(
        body, name="attn_bwd", grid=(AH,),
        in_specs=[pl.BlockSpec((None, 8, AHD), lambda h: (h, 0, 0)), col(0), col(AH), col(2 * AH),
                  col(0), col(0), col(0)],
        out_specs=[col(0), col(0), col(0)],
        out_shape=[jax.ShapeDtypeStruct((S, AH * AHD), BF16)] * 3,
        scratch_shapes=[pltpu.VMEM((S, AHD), BF16) for _ in range(6)]
        + [pltpu.VMEM((S, AHD), F32) for _ in range(9)],
        compiler_params=_cp(("parallel",)),
    )(_attn_consts(), proj, proj, proj, dmixed, o, lse)


def _ret_consts():
    c = np.zeros((RH, 8, RHD), np.float32)
    for h in range(RH):
        c[h, :, :] = np.log(np.float32(1.0) - np.float32(2.0 ** (-5.0 - h)))
    return jnp.asarray(c)


def _ret_factors(lg):
    i = lax.broadcasted_iota(jnp.int32, (CH, CH), 0)
    j = lax.broadcasted_iota(jnp.int32, (CH, CH), 1)
    dif = (i - j).astype(F32)
    decay = jnp.where(dif >= 0, jnp.exp(lg[:, 0:CH] * jnp.maximum(dif, 0.0)), 0.0)
    row = lax.broadcasted_iota(jnp.int32, (CH, RHD), 0).astype(F32)
    zeta = jnp.exp(lg * (CH - 1.0 - row))
    xi = jnp.exp(lg * (row + 1.0))
    return decay, zeta, xi, jnp.exp(lg * float(CH))


CBK = 8
RSTEPS = NB // CBK


def _ret_specs(rev):
    off = 3 * AH * AHD // RHD
    rows = CBK * CH

    def ch(n):
        return (RSTEPS - 1 - n) if rev else n

    def col(k):
        return pl.BlockSpec((rows, RHD), lambda h, n: (ch(n), off + k * RH + h))

    own = pl.BlockSpec((rows, RHD), lambda h, n: (ch(n), h))
    state = pl.BlockSpec((None, CBK, RHD, RHD), lambda h, n: (h, ch(n), 0, 0))
    const = pl.BlockSpec((None, 8, RHD), lambda h, n: (h, 0, 0))
    dm = pl.BlockSpec((rows, RHD), lambda h, n: (ch(n), AH * AHD // RHD + h))
    return col, own, state, const, dm


def _chunks(x):
    return x.reshape(CBK, CH, RHD)


def _ret_fwd(proj):
    def body(c_ref, q_ref, k_ref, v_ref, g_ref, ret_ref, mr_ref, st_ref, r_acc):
        n = pl.program_id(1)

        @pl.when(n == 0)
        def _():
            r_acc[...] = jnp.zeros_like(r_acc)

        decay, zeta, xi, gch = _ret_factors(c_ref[0:1, :])
        q3 = _chunks(q_ref[...].astype(BF16))
        kc = _chunks(k_ref[...] * (1.0 / math.sqrt(RHD)))
        k3 = kc.astype(BF16)
        v3 = _chunks(v_ref[...].astype(BF16))
        kv3 = _bdot_tn((kc * zeta[None]).astype(BF16), v3)
        r = r_acc[...]
        for i in range(CBK):
            st_ref[i] = r.astype(BF16)
            r = r * gch + kv3[i]
        r_acc[...] = r
        scores = _bdot_nt(q3, k3) * decay[None]
        ret = (_bdot(scores.astype(BF16), v3) + _bdot(q3, st_ref[...]) * xi[None]).reshape(CBK * CH, RHD)
        ret_ref[...] = ret
        rr = lax.rsqrt(jnp.mean(ret * ret, axis=-1, keepdims=True) + EPS)
        gv = g_ref[...]
        mr_ref[...] = ((gv * _sigmoid(gv)) * (ret * rr)).astype(BF16)

    col, own, state, const, _ = _ret_specs(False)
    return pl.pallas_call(
        body, name="ret_fwd", grid=(RH, RSTEPS),
        in_specs=[const, col(0), col(1), col(2), col(3)],
        out_specs=[own, own, state],
        out_shape=[jax.ShapeDtypeStruct((S, RH * RHD), F32), jax.ShapeDtypeStruct((S, RH * RHD), BF16),
                   jax.ShapeDtypeStruct((RH, NB, RHD, RHD), BF16)],
        scratch_shapes=[pltpu.VMEM((RHD, RHD), F32)],
        compiler_params=_cp(("parallel", "arbitrary")),
    )(_ret_consts(), proj, proj, proj, proj)


def _ret_bwd(proj, ret, states, dmixed):
    def body(c_ref, q_ref, k_ref, v_ref, g_ref, ret_ref, st_ref, dm_ref, dq_ref, dk_ref, dv_ref, dg_ref, g_acc, gs):
        n = pl.program_id(1)

        @pl.when(n == 0)
        def _():
            g_acc[...] = jnp.zeros_like(g_acc)

        decay, zeta, xi, gch = _ret_factors(c_ref[0:1, :])
        ret_v = ret_ref[...]
        rr = lax.rsqrt(jnp.mean(ret_v * ret_v, axis=-1, keepdims=True) + EPS)
        gv = g_ref[...]
        sg = _sigmoid(gv)
        dmix = dm_ref[...]
        dg_ref[...] = ((dmix * (ret_v * rr)) * (sg * (1.0 + gv * (1.0 - sg)))).astype(BF16)
        dretn = dmix * (gv * sg)
        dret = _chunks(rr * dretn - ret_v * ((rr * rr * rr) * jnp.mean(dretn * ret_v, axis=-1, keepdims=True)))

        q3 = _chunks(q_ref[...].astype(BF16))
        kc = _chunks(k_ref[...] * (1.0 / math.sqrt(RHD)))
        k3 = kc.astype(BF16)
        v3 = _chunks(v_ref[...].astype(BF16))
        d3 = dret.astype(BF16)
        dxi = (dret * xi[None]).astype(BF16)
        kz = (kc * zeta[None]).astype(BF16)
        dr3 = _bdot_tn(q3, dxi)
        acc = g_acc[...]
        for i in reversed(range(CBK)):
            gs[i] = acc.astype(BF16)
            acc = dr3[i] + gch * acc
        g_acc[...] = acc
        g3 = gs[...]
        sc = (_bdot_nt(q3, k3) * decay[None]).astype(BF16)
        da = (_bdot_nt(d3, v3) * decay[None]).astype(BF16)
        dq = _bdot(da, k3) + _bdot_nt(dxi, st_ref[...])
        dkc = _bdot_tn(da, q3) + _bdot_nt(v3, g3) * zeta[None]
        dv = _bdot_tn(sc, d3) + _bdot(kz, g3)
        dq_ref[...] = dq.reshape(CBK * CH, RHD).astype(BF16)
        dk_ref[...] = (dkc * (1.0 / math.sqrt(RHD))).reshape(CBK * CH, RHD).astype(BF16)
        dv_ref[...] = dv.reshape(CBK * CH, RHD).astype(BF16)

    col, own, state, const, dm = _ret_specs(True)
    return pl.pallas_call(
        body, name="ret_bwd", grid=(RH, RSTEPS),
        in_specs=[const, col(0), col(1), col(2), col(3), own, state, dm],
        out_specs=[own, own, own, own],
        out_shape=[jax.ShapeDtypeStruct((S, RH * RHD), BF16)] * 4,
        scratch_shapes=[pltpu.VMEM((RHD, RHD), F32), pltpu.VMEM((CBK, RHD, RHD), BF16)],
        compiler_params=_cp(("parallel", "arbitrary")),
    )(_ret_consts(), proj, proj, proj, proj, ret, states, dmixed)


class _NoReduction:
    def start(self, group, grads):
        pass

    def local(self, name, first=()):
        return []

    def landed(self, name):
        return []

    def update(self, name):
        return []


def _local_step(x, tgt, nw1, nw2, nw3, win, wout, wg, wu, wd, red=None):
    red = red or _NoReduction()

    def after(values, first):
        return lax.optimization_barrier((tuple(values), tuple(first)))[0]

    wg, wu, wd = (w.reshape(NFG, N_FG, D) for w in (wg, wu, wd))
    h1, r1 = _rms_fwd(x, nw1)
    proj = _proj(h1, win)
    o, ma, lse = _attn_fwd(proj)
    ret, mr, states = _ret_fwd(proj)
    x2, h2, r2 = _out_proj_rms(x, ma, mr, wout, nw2)
    g, u, a = _ffn_up(h2, wg, wu)
    dx3, dx3b, st3 = _ffn_down_loss(x2, a, wd, nw3, tgt)

    dwd = _wgrad_rows(a, dx3b, "wgrad_down")
    red.start(["w_down"], [dwd])
    (dx3b,) = after([dx3b], [dwd])
    dg, du = _ffn_down_bwd(dx3b, wd, g, u)
    dg, du = after([dg, du], red.local("w_down", first=[dg]))
    dwg = _wgrad_rows(dg, h2, "wgrad_gate")
    red.start(["w_gate"], [dwg])
    (du,) = after([du], [dwg])
    dwu = _wgrad_rows(du, h2, "wgrad_up")
    red.start(["w_up"], [dwu])
    dg, du = after([dg, du], [dwu] + red.local("w_gate"))
    dx2, dx2b, st2 = _ffn_up_bwd(dg, du, wg, wu, dx3, x2, r2, nw2)
    (dx2b,) = after([dx2b], red.local("w_up", first=[dx2b] + red.landed("w_down")))
    dwo = _wgrad_out(ma, mr, dx2b)
    red.start(["w_out"], [dwo])
    (dx2b,) = after([dx2b], [dwo])
    dmixed = _out_proj_bwd(dx2b, wout)
    dqa, dka, dva = _attn_bwd(proj, dmixed, o, lse)
    (dmixed,) = after([dmixed], [dqa] + red.landed("w_gate"))
    dqr, dkr, dvr, dgr = _ret_bwd(proj, ret, states, dmixed)
    dproj = jnp.concatenate([dqa, dka, dva, dqr, dkr, dvr, dgr], axis=1)
    (dwi,) = after([_wgrad_in(h1, dproj)], red.landed("w_up"))
    red.start(["w_in"], [dwi])
    early = red.local("w_out", first=[dwi]) + red.update("w_down")
    (dproj,) = after([dproj], red.local("w_in", first=early))
    gx, st1 = _in_proj_bwd(dproj, win, dx2, x, r1, nw1)
    stats = jnp.concatenate([st1[0:1], st2[0:1], st3[0:2], jnp.zeros((4, D), F32)], axis=0)
    return stats, gx, dwi, dwo, dwg, dwu, dwd


def _place():
    x, y, c = lax.axis_index("x"), lax.axis_index("y"), lax.axis_index("c")
    return x, y, c, [(1 - x, y), (x, 1 - y), (1 - x, 1 - y)]


def _handshake(peers):
    barrier = pltpu.get_barrier_semaphore()
    for peer in peers:
        pl.semaphore_signal(barrier, inc=1, device_id=peer, device_id_type=MESH)
    pl.semaphore_wait(barrier, len(peers))


def _all_gather(shards, name, collective_id):
    na = len(shards)
    SIB, XN0, XN1, YN1, YN0, VIA_X, VIA_Y = 0, 1, 2, 3, 4, 5, 6
    D2D = {XN0: 7, XN1: 8, YN1: 9, YN0: 10, VIA_X: 11, VIA_Y: 12}

    def body(*refs):
        ins, outs = refs[:na], refs[na:2 * na]
        send_sems, recv_sems, local_sems = refs[2 * na:]
        x, y, c, _ = _place()
        me, sib = (x, y, c), (x, y, 1 - c)
        xn, yn, dg = (1 - x, y, c), (x, 1 - y, c), (1 - x, 1 - y, c)
        _handshake([sib, xn, yn])

        def part(ref, h):
            rows = ref.shape[0] // 2
            return ref if h is None else ref.at[pl.ds(h * rows, rows)]

        def block(a, owner, h):
            return part(outs[a].at[4 * owner[0] + 2 * owner[1] + owner[2]], h)

        def copy(a, k, owner, h, to, own_src=False):
            return pltpu.make_async_remote_copy(
                src_ref=part(ins[a], h) if own_src else block(a, owner, h), dst_ref=block(a, owner, h),
                send_sem=send_sems.at[a, k], recv_sem=recv_sems.at[a, k], device_id=to, device_id_type=MESH)

        def other(p):
            return (p[0], p[1], 1 - c)

        mine = [pltpu.make_async_copy(ins[a], block(a, me, None), local_sems.at[a]) for a in range(na)]
        for cp in mine:
            cp.start()
        sent = []
        for a in range(na):
            sent += [copy(a, XN0, me, 0, xn, True), copy(a, YN1, me, 1, yn, True),
                     copy(a, XN1, me, 1, xn, True), copy(a, YN0, me, 0, yn, True)]
        sent += [copy(a, SIB, me, None, sib, True) for a in range(na)]
        for cp in sent:
            cp.start()

        def landed(a, k, owner, h, then):
            copy(a, k, owner, h, me).wait_recv()
            for k2, to in then + [(D2D[k], sib)]:
                cp = copy(a, k2, owner, h, to)
                cp.start()
                sent.append(cp)

        for a in range(na):
            landed(a, XN0, xn, 0, [(VIA_Y, yn)])
            landed(a, YN1, yn, 1, [(VIA_X, xn)])
            landed(a, XN1, xn, 1, [])
            landed(a, YN0, yn, 0, [])
        for a in range(na):
            landed(a, VIA_Y, dg, 0, [])
            landed(a, VIA_X, dg, 1, [])
        for a in range(na):
            copy(a, SIB, sib, None, me).wait_recv()
            for k, owner, h in ((XN0, xn, 0), (XN1, xn, 1), (YN1, yn, 1), (YN0, yn, 0), (VIA_Y, dg, 0), (VIA_X, dg, 1)):
                copy(a, D2D[k], other(owner), h, me).wait_recv()
        for cp in sent:
            cp.wait_send()
        for cp in mine:
            cp.wait()

    return _sequencer_call(
        body, name, collective_id,
        [jax.ShapeDtypeStruct((NDEV,) + s.shape, s.dtype) for s in shards],
        [pltpu.SemaphoreType.DMA((na, 13)), pltpu.SemaphoreType.DMA((na, 13)), pltpu.SemaphoreType.DMA((na,))])(*shards)


def _sequencer_call(body, name, collective_id, out_type, scratch_types):
    return pl.kernel(
        body, name=name, out_type=out_type,
        mesh=plsc.ScalarSubcoreMesh(axis_name="sequencer", num_cores=1),
        scratch_types=scratch_types,
        compiler_params=pltpu.CompilerParams(collective_id=collective_id))


def _exchange_sibling(grads, name, collective_id):
    na = len(grads)

    def body(*refs):
        ins, outs = refs[:na], refs[na:2 * na]
        send_sems, recv_sems = refs[2 * na:]
        x, y, c, _ = _place()
        _handshake([(x, y, 1 - c)])
        cps = []
        for a in range(na):
            for k in range(4):
                cps.append(pltpu.make_async_remote_copy(
                    src_ref=ins[a].at[2 * k + (1 - c)], dst_ref=outs[a].at[k],
                    send_sem=send_sems.at[a, k], recv_sem=recv_sems.at[a, k],
                    device_id=(x, y, 1 - c), device_id_type=MESH))
        for cp in cps:
            cp.start()
        for cp in cps:
            cp.wait()

    return _sequencer_call(
        body, name, collective_id,
        [jax.ShapeDtypeStruct((4,) + g.shape[1:], g.dtype) for g in grads],
        [pltpu.SemaphoreType.DMA((na, 4)), pltpu.SemaphoreType.DMA((na, 4))])(*grads)


def _row_tile(rows, cols):
    for t in (512, 256, 176, 128, 64, 32, 16):
        if rows % t == 0 and t * cols * 4 <= (2 << 20):
            return t
    raise ValueError((rows, cols))


def _chip_sum(place, g, got, name):
    _, r, c = g.shape
    tm = r

    def body(pos_ref, g_ref, got_ref, o_ref):
        o_ref[...] = (g_ref[...].astype(F32) + got_ref[...].astype(F32)).astype(BF16)

    def chip(j, pos):
        return 2 * (pos[0] ^ jnp.where(j == 1, 0, 1)) + (pos[1] ^ jnp.where(j == 0, 0, 1))

    return pl.pallas_call(
        body, name=name,
        grid_spec=pltpu.PrefetchScalarGridSpec(
            num_scalar_prefetch=1, grid=(3, r // tm),
            in_specs=[pl.BlockSpec((None, tm, c), lambda j, i, pos: (2 * chip(j, pos) + pos[2], i, 0)),
                      pl.BlockSpec((None, tm, c), lambda j, i, pos: (chip(j, pos), i, 0))],
            out_specs=pl.BlockSpec((None, tm, c), lambda j, i, pos: (j, i, 0))),
        out_shape=jax.ShapeDtypeStruct((3, r, c), BF16),
        compiler_params=_cp(("parallel", "parallel")),
    )(place, g, got)


def _exchange_chips(sums, name, collective_id):
    na = len(sums)

    def body(*refs):
        ins, outs = refs[:na], refs[na:2 * na]
        send_sems, recv_sems = refs[2 * na:]
        x, y, c, chips = _place()
        _handshake([(*chip, c) for chip in chips])
        cps = []
        for a in range(na):
            for j, chip in enumerate(chips):
                cps.append(pltpu.make_async_remote_copy(
                    src_ref=ins[a].at[j], dst_ref=outs[a].at[j],
                    send_sem=send_sems.at[a, j], recv_sem=recv_sems.at[a, j],
                    device_id=(*chip, c), device_id_type=MESH))
        for cp in cps:
            cp.start()
        for cp in cps:
            cp.wait()

    return _sequencer_call(
        body, name, collective_id,
        [jax.ShapeDtypeStruct((3,) + s.shape[1:], s.dtype) for s in sums],
        [pltpu.SemaphoreType.DMA((na, 3)), pltpu.SemaphoreType.DMA((na, 3))])(*sums)


def _exchange_stats(stats, collective_id):
    def body(st_in, st_out, st_send, st_recv, local_sem):
        x, y, c, _ = _place()
        me_idx = 4 * x + 2 * y + c
        peers = [(x ^ ((k >> 2) & 1), y ^ ((k >> 1) & 1), c ^ (k & 1)) for k in range(1, 8)]
        _handshake(peers)
        mine = pltpu.make_async_copy(st_in, st_out.at[me_idx], local_sem)
        mine.start()
        cps = [pltpu.make_async_remote_copy(
            src_ref=st_in, dst_ref=st_out.at[me_idx], send_sem=st_send.at[k], recv_sem=st_recv.at[k],
            device_id=peer, device_id_type=MESH) for k, peer in enumerate(peers)]
        for cp in cps:
            cp.start()
        for cp in cps:
            cp.wait()
        mine.wait()

    return _sequencer_call(
        body, "exchange_stats", collective_id,
        jax.ShapeDtypeStruct((NDEV,) + stats.shape, stats.dtype),
        [pltpu.SemaphoreType.DMA((7,)), pltpu.SemaphoreType.DMA((7,)), pltpu.SemaphoreType.DMA])(stats)


class _Reduction:
    def __init__(self, place, first_collective_id, state):
        self.place = place
        self.ids = iter(range(first_collective_id, 32))
        self.state = state
        self.groups = {}
        self.updates = {}

    def next_id(self):
        return next(self.ids)

    def start(self, group, grads):
        got = _exchange_sibling(grads, "sibling_exchange_" + group[0], self.next_id())
        self.groups[group[0]] = dict(names=group, grads=grads, got=got)

    def local(self, name, first=()):
        grp = self.groups[name]
        grads = lax.optimization_barrier((tuple(grp["grads"]), tuple(first)))[0]
        grp["sums"] = [_chip_sum(self.place, g, s, "chip_sum_" + n)
                       for g, s, n in zip(grads, grp["got"], grp["names"])]
        grp["chips"] = _exchange_chips(grp["sums"], "chip_exchange_" + name, self.next_id())
        return grp["sums"]

    def landed(self, name):
        return list(self.groups[name]["chips"])

    def update(self, name):
        if name not in self.updates:
            grp = next(g for g in self.groups.values() if name in g["names"])
            k = grp["names"].index(name)
            self.updates[name] = _shard_update(self.place, *self.state[name], grp["grads"][k], grp["got"][k],
                                               grp["chips"][k], "update_" + name)
        return list(self.updates[name])


def _adamw(w, g, m, v):
    m = ADAM_B1 * m + (1.0 - ADAM_B1) * g
    v = ADAM_B2 * v + (1.0 - ADAM_B2) * (g * g)
    m_hat = m / (1.0 - ADAM_B1 ** ADAM_STEP)
    v_hat = v / (1.0 - ADAM_B2 ** ADAM_STEP)
    delta = -ADAM_LR * (m_hat / (jnp.sqrt(v_hat) + ADAM_EPS) + ADAM_WD * w)
    return delta, m, v


def _shard_update(place, w, m, v, g, got_sib, got_chips, name):
    r, c = w.shape
    tm = _row_tile(r, c)

    def body(pos_ref, w_ref, m_ref, v_ref, g_ref, s_ref, c_ref, go_ref, d_ref, mo_ref, vo_ref):
        grad = g_ref[...].astype(F32) + s_ref[...].astype(F32)
        for j in range(3):
            grad = grad + c_ref[j].astype(F32)
        delta, mn, vn = _adamw(w_ref[...], grad, m_ref[...], v_ref[...])
        go_ref[...] = grad
        d_ref[...] = delta
        mo_ref[...] = mn
        vo_ref[...] = vn

    row = pl.BlockSpec((tm, c), lambda i, pos: (i, 0))
    return pl.pallas_call(
        body, name=name,
        grid_spec=pltpu.PrefetchScalarGridSpec(
            num_scalar_prefetch=1, grid=(r // tm,),
            in_specs=[row, row, row,
                      pl.BlockSpec((None, tm, c), lambda i, pos: (4 * pos[0] + 2 * pos[1] + pos[2], i, 0)),
                      pl.BlockSpec((None, tm, c), lambda i, pos: (2 * pos[0] + pos[1], i, 0)),
                      pl.BlockSpec((3, tm, c), lambda i, pos: (0, i, 0))],
            out_specs=[row, row, row, row]),
        out_shape=[jax.ShapeDtypeStruct((r, c), F32)] * 4,
        compiler_params=_cp(("parallel",)),
    )(place, w, m, v, g, got_sib, got_chips)


def _small_update(stats_all, ws, ms, vs):
    def body(st_ref, w_ref, m_ref, v_ref, go_ref, d_ref, mo_ref, vo_ref):
        grad = st_ref[0]
        for k in range(1, NDEV):
            grad = grad + st_ref[k]
        delta, mn, vn = _adamw(w_ref[...], grad, m_ref[...], v_ref[...])
        go_ref[...] = grad
        d_ref[...] = delta
        mo_ref[...] = mn
        vo_ref[...] = vn

    return pl.pallas_call(
        body, name="small_update",
        out_shape=[jax.ShapeDtypeStruct((8, D), F32)] * 4,
        compiler_params=_cp(),
    )(stats_all, ws, ms, vs)


def kernel(x, norm_mix_w, w_in, w_out, norm_ffn_w, w_gate, w_up, w_down, norm_final_w, loss_target, m_norm_mix_w, m_w_in, m_w_out, m_norm_ffn_w, m_w_gate, m_w_up, m_w_down, m_norm_final_w, v_norm_mix_w, v_w_in, v_w_out, v_norm_ffn_w, v_w_gate, v_w_up, v_w_down, v_norm_final_w):
    tr = {"w_gate", "w_up"}
    names = ["w_in", "w_out", "w_gate", "w_up", "w_down"]

    def view(a, n):
        return a[0].T if n in tr else a[0]

    big_w = [view(a, n) for a, n in zip([w_in, w_out, w_gate, w_up, w_down], names)]
    big_m = [view(a, n) for a, n in zip([m_w_in, m_w_out, m_w_gate, m_w_up, m_w_down], names)]
    big_v = [view(a, n) for a, n in zip([v_w_in, v_w_out, v_w_gate, v_w_up, v_w_down], names)]

    shards = [_cast_bf16(w, "cast_" + n) for w, n in zip(big_w, names)]
    (win,) = _all_gather(shards[0:1], "all_gather_w_in", 1)
    wout, wg, wu = _all_gather(shards[1:4], "all_gather_out_gate_up", 2)
    (wd,) = _all_gather(shards[4:5], "all_gather_w_down", 3)
    nw3 = norm_final_w.reshape(1, D)
    place = jnp.stack([lax.axis_index("x"), lax.axis_index("y"), lax.axis_index("c")]).astype(jnp.int32)
    red = _Reduction(place, 4, {n: (w, m, v) for n, w, m, v in zip(names, big_w, big_m, big_v)})
    stats, gx, *_ = _local_step(
        x[0], loss_target[0], norm_mix_w, norm_ffn_w, nw3, win, wout.reshape(D, D), wg, wu, wd, red)
    stats_all = _exchange_stats(stats, red.next_id())
    upd = [red.update(n) for n in names]
    stats_all = lax.optimization_barrier((stats_all, tuple(upd[0])))[0]

    def rows(a, b, c):
        return jnp.concatenate([a.reshape(1, D), b.reshape(1, D), c.reshape(1, D), jnp.zeros((5, D), F32)], axis=0)

    sg, sd, sm, sv = _small_update(stats_all, rows(norm_mix_w, norm_ffn_w, norm_final_w),
                                   rows(m_norm_mix_w, m_norm_ffn_w, m_norm_final_w),
                                   rows(v_norm_mix_w, v_norm_ffn_w, v_norm_final_w))
    loss = sg[3, 0]

    def outs(k, small):
        big = [(u[k].T if n in tr else u[k])[None] for u, n in zip(upd, names)]
        return [small[0:1], big[0], big[1], small[1:2], big[2], big[3], big[4], small[2]]

    return (loss, gx[None], *outs(0, sg), *outs(1, sd), *outs(2, sm), *outs(3, sv))
```

```python
import functools
import math

import numpy as np
import jax
import jax.numpy as jnp
from jax import lax
from jax.experimental import pallas as pl
from jax.experimental.pallas import tpu as pltpu
from jax.experimental.pallas import tpu_sc as plsc

F32 = jnp.float32
BF16 = jnp.bfloat16

S = 2048
D = 2048
NDEV = 8
N_IN = 7168 // NDEV
N_FF = 5632 // NDEV
NFG, N_FG = NDEV // 2, 2 * N_FF
N_OUT = 2048 // NDEV
AH, AHD = 8, 128
RH, RHD = 4, 256
CH = 128
NB = S // CH
EPS = 1e-6
PATTERNS = ((1, 16), (4, 4), (16, 1))
NEG = -1e30
VMEM_LIMIT = 56 * 1024 * 1024

ADAM_LR, ADAM_B1, ADAM_B2, ADAM_EPS, ADAM_WD, ADAM_STEP = 0.001, 0.9, 0.999, 1e-08, 0.01, 10
MESH = pl.DeviceIdType.MESH


def _cp(sem=None):
    return pltpu.CompilerParams(dimension_semantics=sem, vmem_limit_bytes=VMEM_LIMIT)


def _dot(a, b):
    return jnp.dot(a, b, preferred_element_type=F32)


def _dot_nt(a, b):
    return lax.dot_general(a, b, (((1,), (1,)), ((), ())), preferred_element_type=F32)


def _dot_tn(a, b):
    return lax.dot_general(a, b, (((0,), (0,)), ((), ())), preferred_element_type=F32)


def _sigmoid(x):
    return 0.5 * jnp.tanh(0.5 * x) + 0.5


def _cast_bf16(w, name):
    r, c = w.shape
    tm = r if r <= 1024 else 512

    def body(w_ref, o_ref):
        o_ref[...] = w_ref[...].astype(BF16)

    return pl.pallas_call(
        body, name=name, grid=(r // tm,),
        in_specs=[pl.BlockSpec((tm, c), lambda i: (i, 0))],
        out_specs=pl.BlockSpec((tm, c), lambda i: (i, 0)),
        out_shape=jax.ShapeDtypeStruct((r, c), BF16),
        compiler_params=_cp(("parallel",)),
    )(w)


def _rms_fwd(x, nw):
    tm = 256

    def body(x_ref, w_ref, h_ref, r_ref):
        xs = x_ref[...]
        r = lax.rsqrt(jnp.mean(xs * xs, axis=-1, keepdims=True) + EPS)
        h_ref[...] = ((xs * r) * w_ref[...]).astype(BF16)
        r_ref[...] = r

    return pl.pallas_call(
        body, name="rms_fwd", grid=(S // tm,),
        in_specs=[pl.BlockSpec((tm, D), lambda i: (i, 0)), pl.BlockSpec((1, D), lambda i: (0, 0))],
        out_specs=[pl.BlockSpec((tm, D), lambda i: (i, 0)), pl.BlockSpec((tm, 1), lambda i: (i, 0))],
        out_shape=[jax.ShapeDtypeStruct((S, D), BF16), jax.ShapeDtypeStruct((S, 1), F32)],
        compiler_params=_cp(("parallel",)),
    )(x, nw)


def _row_copies(hbm_refs, bufs, sems, m, tm):
    rows = pl.ds(pl.multiple_of(m * tm, tm), tm)
    return [pltpu.make_async_copy(h.at[rows], b, sems.at[i]) for i, (h, b) in enumerate(zip(hbm_refs, bufs))]


def _rms_bwd_tile(dh, xs, r, nw):
    dnw = jnp.sum(dh * (xs * r), axis=0, keepdims=True)
    gy = dh * nw
    dx = r * gy - xs * ((r * r * r) * jnp.mean(gy * xs, axis=-1, keepdims=True))
    return dx, dnw


def _proj(h1, win):
    tm = 1024

    def body(a_ref, w_ref, o_ref):
        o_ref[...] = _dot(a_ref[...], w_ref[...])

    return pl.pallas_call(
        body, name="proj", grid=(NDEV, S // tm),
        in_specs=[pl.BlockSpec((tm, D), lambda p, m: (m, 0)),
                  pl.BlockSpec((None, D, N_IN), lambda p, m: (p, 0, 0))],
        out_specs=pl.BlockSpec((tm, N_IN), lambda p, m: (m, p)),
        out_shape=jax.ShapeDtypeStruct((S, NDEV * N_IN), F32),
        compiler_params=_cp(("parallel", "parallel")),
    )(h1, win)


def _out_proj_rms(x, ma, mr, wout, nw):
    tm = 256
    half = D // 2

    def body(x_ref, ma_ref, mr_ref, w_ref, nw_ref, x2_ref, h_ref, r_ref):
        acc = _dot(ma_ref[...], w_ref[0:half, :]) + _dot(mr_ref[...], w_ref[half:D, :])
        x2 = x_ref[...] + acc
        r = lax.rsqrt(jnp.mean(x2 * x2, axis=-1, keepdims=True) + EPS)
        x2_ref[...] = x2
        h_ref[...] = ((x2 * r) * nw_ref[...]).astype(BF16)
        r_ref[...] = r

    return pl.pallas_call(
        body, name="out_proj_rms", grid=(S // tm,),
        in_specs=[pl.BlockSpec((tm, D), lambda i: (i, 0)),
                  pl.BlockSpec((tm, half), lambda i: (i, 0)),
                  pl.BlockSpec((tm, half), lambda i: (i, 0)),
                  pl.BlockSpec((D, D), lambda i: (0, 0)),
                  pl.BlockSpec((1, D), lambda i: (0, 0))],
        out_specs=[pl.BlockSpec((tm, D), lambda i: (i, 0)), pl.BlockSpec((tm, D), lambda i: (i, 0)),
                   pl.BlockSpec((tm, 1), lambda i: (i, 0))],
        out_shape=[jax.ShapeDtypeStruct((S, D), F32), jax.ShapeDtypeStruct((S, D), BF16),
                   jax.ShapeDtypeStruct((S, 1), F32)],
        compiler_params=_cp(("parallel",)),
    )(x, ma, mr, wout, nw)


def _ffn_up(h2, wg, wu):
    tm = 512

    def body(h_ref, wg_ref, wu_ref, g_ref, u_ref, a_ref):
        h = h_ref[...]
        g = _dot_nt(h, wg_ref[...])
        u = _dot_nt(h, wu_ref[...])
        g_ref[...] = g
        u_ref[...] = u
        a_ref[...] = ((g * _sigmoid(g)) * u).astype(BF16)

    blk = pl.BlockSpec((None, tm, N_FG), lambda p, m: (p, m, 0))
    wblk = pl.BlockSpec((None, N_FG, D), lambda p, m: (p, 0, 0))
    return pl.pallas_call(
        body, name="ffn_up", grid=(NFG, S // tm),
        in_specs=[pl.BlockSpec((tm, D), lambda p, m: (m, 0)), wblk, wblk],
        out_specs=[blk, blk, blk],
        out_shape=[jax.ShapeDtypeStruct((NFG, S, N_FG),F32), jax.ShapeDtypeStruct((NFG, S, N_FG),F32),
                   jax.ShapeDtypeStruct((NFG, S, N_FG),BF16)],
        compiler_params=_cp(("parallel", "parallel")),
    )(h2, wg, wu)


def _ffn_down_loss(x2, a, wd, nw, tgt):
    tm = 512

    def body(x2_hbm, a_ref, w_ref, nw_ref, t_hbm, dx_ref, dxb_ref, st_ref, acc_ref, x2_buf, t_buf, sems):
        m, p = pl.program_id(0), pl.program_id(1)
        tail_in = _row_copies((x2_hbm, t_hbm), (x2_buf, t_buf), sems, m, tm)

        @pl.when(p == 0)
        def _():
            acc_ref[...] = jnp.zeros_like(acc_ref)
            for cp in tail_in:
                cp.start()

        @pl.when((p == 0) & (m == 0))
        def _():
            st_ref[...] = jnp.zeros_like(st_ref)

        acc_ref[...] += _dot(a_ref[...], w_ref[...])

        @pl.when(p == NFG - 1)
        def _():
            for cp in tail_in:
                cp.wait()
            x3 = x2_buf[...] + acc_ref[...]
            nwv = nw_ref[...]
            r = lax.rsqrt(jnp.mean(x3 * x3, axis=-1, keepdims=True) + EPS)
            y = (x3 * r) * nwv
            err = y - t_buf[...]
            loss = 0.5 * jnp.sum(jnp.mean(err * err, axis=-1, keepdims=True), axis=0, keepdims=True)
            dy = err * (1.0 / D)
            dx, dnw = _rms_bwd_tile(dy, x3, r, nwv)
            dx_ref[...] = dx
            dxb_ref[...] = dx.astype(BF16)
            st_ref[0:1, :] += dnw
            st_ref[1:2, :] += jnp.broadcast_to(loss, (1, D))

    return pl.pallas_call(
        body, name="ffn_down_loss", grid=(S // tm, NFG),
        in_specs=[pl.BlockSpec(memory_space=pl.ANY),
                  pl.BlockSpec((None, tm, N_FG), lambda m, p: (p, m, 0)),
                  pl.BlockSpec((None, N_FG, D), lambda m, p: (p, 0, 0)),
                  pl.BlockSpec((1, D), lambda m, p: (0, 0)),
                  pl.BlockSpec(memory_space=pl.ANY)],
        out_specs=[pl.BlockSpec((tm, D), lambda m, p: (m, 0)), pl.BlockSpec((tm, D), lambda m, p: (m, 0)),
                   pl.BlockSpec((8, D), lambda m, p: (0, 0))],
        out_shape=[jax.ShapeDtypeStruct((S, D), F32), jax.ShapeDtypeStruct((S, D), BF16),
                   jax.ShapeDtypeStruct((8, D), F32)],
        scratch_shapes=[pltpu.VMEM((tm, D), F32), pltpu.VMEM((tm, D), F32), pltpu.VMEM((tm, D), F32),
                        pltpu.SemaphoreType.DMA((2,))],
        compiler_params=_cp(("arbitrary", "arbitrary")),
    )(x2, a, wd, nw, tgt)


def _ffn_down_bwd(dx3b, wd, g, u):
    tm = 512

    def body(dx_ref, w_ref, g_ref, u_ref, dg_ref, du_ref):
        da = _dot_nt(dx_ref[...], w_ref[...])
        gv = g_ref[...]
        sg = _sigmoid(gv)
        silu = gv * sg
        dg_ref[...] = ((da * u_ref[...]) * (sg * (1.0 + gv * (1.0 - sg)))).astype(BF16)
        du_ref[...] = (da * silu).astype(BF16)

    blk = pl.BlockSpec((None, tm, N_FG), lambda p, m: (p, m, 0))
    return pl.pallas_call(
        body, name="ffn_down_bwd", grid=(NFG, S // tm),
        in_specs=[pl.BlockSpec((tm, D), lambda p, m: (m, 0)),
                  pl.BlockSpec((None, N_FG, D), lambda p, m: (p, 0, 0)), blk, blk],
        out_specs=[blk, blk],
        out_shape=[jax.ShapeDtypeStruct((NFG, S, N_FG),BF16), jax.ShapeDtypeStruct((NFG, S, N_FG),BF16)],
        compiler_params=_cp(("parallel", "parallel")),
    )(dx3b, wd, g, u)


def _ffn_up_bwd(dg, du, wg, wu, dres, xs, r, nw):
    tm = 512

    def body(dg_ref, du_ref, wg_ref, wu_ref, dres_hbm, x_hbm, r_ref, nw_ref, dx_ref, dxb_ref, st_ref,
             dres_buf, x_buf, sems):
        m, p = pl.program_id(0), pl.program_id(1)
        tail_in = _row_copies((dres_hbm, x_hbm), (dres_buf, x_buf), sems, m, tm)

        @pl.when(p == 0)
        def _():
            dx_ref[...] = jnp.zeros_like(dx_ref)
            for cp in tail_in:
                cp.start()

        @pl.when((p == 0) & (m == 0))
        def _():
            st_ref[...] = jnp.zeros_like(st_ref)

        dx_ref[...] += _dot(dg_ref[...], wg_ref[...])
        dx_ref[...] += _dot(du_ref[...], wu_ref[...])

        @pl.when(p == NFG - 1)
        def _():
            for cp in tail_in:
                cp.wait()
            dx, dnw = _rms_bwd_tile(dx_ref[...], x_buf[...], r_ref[...], nw_ref[...])
            dx = dres_buf[...] + dx
            dx_ref[...] = dx
            dxb_ref[...] = dx.astype(BF16)
            st_ref[0:1, :] += dnw

    blk = pl.BlockSpec((None, tm, N_FG), lambda m, p: (p, m, 0))
    wblk = pl.BlockSpec((None, N_FG, D), lambda m, p: (p, 0, 0))
    row = pl.BlockSpec((tm, D), lambda m, p: (m, 0))
    hbm = pl.BlockSpec(memory_space=pl.ANY)
    return pl.pallas_call(
        body, name="ffn_up_bwd", grid=(S // tm, NFG),
        in_specs=[blk, blk, wblk, wblk, hbm, hbm, pl.BlockSpec((tm, 1), lambda m, p: (m, 0)),
                  pl.BlockSpec((1, D), lambda m, p: (0, 0))],
        out_specs=[row, row, pl.BlockSpec((8, D), lambda m, p: (0, 0))],
        out_shape=[jax.ShapeDtypeStruct((S, D), F32), jax.ShapeDtypeStruct((S, D), BF16),
                   jax.ShapeDtypeStruct((8, D), F32)],
        scratch_shapes=[pltpu.VMEM((tm, D), F32), pltpu.VMEM((tm, D), F32), pltpu.SemaphoreType.DMA((2,))],
        compiler_params=_cp(("arbitrary", "arbitrary")),
    )(dg, du, wg, wu, dres, xs, r, nw)


def _out_proj_bwd(dx2b, wout):
    tm = 256

    def body(dx_ref, w_ref, o_ref):
        o_ref[...] = _dot_nt(dx_ref[...], w_ref[...])

    return pl.pallas_call(
        body, name="out_proj_bwd", grid=(S // tm,),
        in_specs=[pl.BlockSpec((tm, D), lambda i: (i, 0)), pl.BlockSpec((D, D), lambda i: (0, 0))],
        out_specs=pl.BlockSpec((tm, D), lambda i: (i, 0)),
        out_shape=jax.ShapeDtypeStruct((S, D), F32),
        compiler_params=_cp(("parallel",)),
    )(dx2b, wout)


def _in_proj_bwd(dproj, win, dres, xs, r, nw):
    tm = 1024

    def body(dp_ref, w_ref, dres_hbm, x_hbm, r_ref, nw_ref, dx_ref, st_ref, dres_buf, x_buf, sems):
        m, p = pl.program_id(0), pl.program_id(1)
        tail_in = _row_copies((dres_hbm, x_hbm), (dres_buf, x_buf), sems, m, tm)

        @pl.when(p == 0)
        def _():
            dx_ref[...] = jnp.zeros_like(dx_ref)
            for cp in tail_in:
                cp.start()

        @pl.when((p == 0) & (m == 0))
        def _():
            st_ref[...] = jnp.zeros_like(st_ref)

        dx_ref[...] += _dot_nt(dp_ref[...], w_ref[...])

        @pl.when(p == NDEV - 1)
        def _():
            for cp in tail_in:
                cp.wait()
            dx, dnw = _rms_bwd_tile(dx_ref[...], x_buf[...], r_ref[...], nw_ref[...])
            dx_ref[...] = dres_buf[...] + dx
            st_ref[0:1, :] += dnw

    row = pl.BlockSpec((tm, D), lambda m, p: (m, 0))
    hbm = pl.BlockSpec(memory_space=pl.ANY)
    return pl.pallas_call(
        body, name="in_proj_bwd", grid=(S // tm, NDEV),
        in_specs=[pl.BlockSpec((tm, N_IN), lambda m, p: (m, p)),
                  pl.BlockSpec((None, D, N_IN), lambda m, p: (p, 0, 0)),
                  hbm, hbm, pl.BlockSpec((tm, 1), lambda m, p: (m, 0)),
                  pl.BlockSpec((1, D), lambda m, p: (0, 0))],
        out_specs=[row, pl.BlockSpec((8, D), lambda m, p: (0, 0))],
        out_shape=[jax.ShapeDtypeStruct((S, D), F32), jax.ShapeDtypeStruct((8, D), F32)],
        scratch_shapes=[pltpu.VMEM((tm, D), F32), pltpu.VMEM((tm, D), F32), pltpu.SemaphoreType.DMA((2,))],
        compiler_params=_cp(("arbitrary", "arbitrary")),
    )(dproj, win, dres, xs, r, nw)


def _wgrad_in(h1, dproj):
    def body(a_ref, d_ref, o_ref):
        o_ref[...] = _dot_tn(a_ref[...], d_ref[...]).astype(BF16)

    return pl.pallas_call(
        body, name="wgrad_in", grid=(NDEV,),
        in_specs=[pl.BlockSpec((S, D), lambda p: (0, 0)), pl.BlockSpec((S, N_IN), lambda p: (0, p))],
        out_specs=pl.BlockSpec((None, D, N_IN), lambda p: (p, 0, 0)),
        out_shape=jax.ShapeDtypeStruct((NDEV, D, N_IN), BF16),
        compiler_params=_cp(("parallel",)),
    )(h1, dproj)


def _wgrad_rows(a3, dy, name):
    def body(a_ref, d_ref, o_ref):
        o_ref[...] = _dot_tn(a_ref[...], d_ref[...]).astype(BF16)

    return pl.pallas_call(
        body, name=name, grid=(NFG,),
        in_specs=[pl.BlockSpec((None, S, N_FG), lambda p: (p, 0, 0)), pl.BlockSpec((S, D), lambda p: (0, 0))],
        out_specs=pl.BlockSpec((None, N_FG, D), lambda p: (p, 0, 0)),
        out_shape=jax.ShapeDtypeStruct((NFG, N_FG, D), BF16),
        compiler_params=_cp(("parallel",)),
    )(a3, dy).reshape(NDEV, N_FF, D)


def _wgrad_out(ma, mr, dx2b):
    half = D // 2
    per = half // N_OUT

    def body(ma_ref, mr_ref, d_ref, o_ref):
        p = pl.program_id(0)

        @pl.when(p < per)
        def _():
            o_ref[...] = _dot_tn(ma_ref[...], d_ref[...]).astype(BF16)

        @pl.when(p >= per)
        def _():
            o_ref[...] = _dot_tn(mr_ref[...], d_ref[...]).astype(BF16)

    return pl.pallas_call(
        body, name="wgrad_out", grid=(NDEV,),
        in_specs=[pl.BlockSpec((S, N_OUT), lambda p: (0, jnp.minimum(p, per - 1))),
                  pl.BlockSpec((S, N_OUT), lambda p: (0, jnp.maximum(p - per, 0))),
                  pl.BlockSpec((S, D), lambda p: (0, 0))],
        out_specs=pl.BlockSpec((None, N_OUT, D), lambda p: (p, 0, 0)),
        out_shape=jax.ShapeDtypeStruct((NDEV, N_OUT, D), BF16),
        compiler_params=_cp(("parallel",)),
    )(ma, mr, dx2b)


def _attn_consts():
    c = np.zeros((AH, 8, AHD), np.float32)
    for h in range(AH):
        c[h, :, :] = 2.0 ** (-(h + 1))
    return jnp.asarray(c)


def _permute_in(dst, src, d, cast=None):
    ln = S // d
    for rr in range(d):
        v = src[pl.ds(rr, ln, stride=d), :] if d > 1 else src[...]
        dst[rr * ln:(rr + 1) * ln, :] = v if cast is None else v.astype(cast)


def _attn_masks():
    qi = lax.broadcasted_iota(jnp.int32, (CH, CH), 0)
    kj = lax.broadcasted_iota(jnp.int32, (CH, CH), 1)
    dist_c = (qi - kj).astype(F32)
    dist_p = (qi - kj + CH).astype(F32)
    return (qi >= kj)[None], (kj >= qi)[None], dist_c[None], dist_p[None]


GB = 8


def _bdot_nt(a, b):
    return lax.dot_general(a, b, (((2,), (2,)), ((0,), (0,))), preferred_element_type=F32)


def _bdot(a, b):
    return lax.dot_general(a, b, (((2,), (1,)), ((0,), (0,))), preferred_element_type=F32)


def _bdot_tn(a, b):
    return lax.dot_general(a, b, (((1,), (1,)), ((0,), (0,))), preferred_element_type=F32)


def _shift_block(dst, src):
    dst[0:CH, :] = jnp.zeros((CH, AHD), dst.dtype)
    dst[CH:S, :] = src[0:S - CH, :]


def _has_prev(g, nb):
    blk = lax.broadcasted_iota(jnp.int32, (GB, 1, 1), 0) + g * GB
    return (blk & (nb - 1)) != 0


def _blocks(ref, g):
    return ref[g * GB * CH:(g + 1) * GB * CH, :].reshape(GB, CH, AHD)


def _attn_fwd(proj):
    scale = 1.0 / math.sqrt(AHD)

    def body(c_ref, q_ref, k_ref, v_ref, o_ref, ob_ref, lse_ref, qd, kd, vd, kps, vps, od, ld, *nat):
        onat, lnat = nat[0:3], nat[3:6]
        slope = c_ref[0:1, :]
        mask_c, mask_p, dist_c, dist_p = _attn_masks()
        for pi, (d, nb) in enumerate(PATTERNS):
            _permute_in(qd, q_ref, d, BF16)
            _permute_in(kd, k_ref, d, BF16)
            _permute_in(vd, v_ref, d, BF16)
            if nb > 1:
                _shift_block(kps, kd)
                _shift_block(vps, vd)
            bias_c = -(slope * float(d)) * dist_c
            bias_p = -(slope * float(d)) * dist_p
            for g in range(NB // GB):
                q3, k3, v3 = _blocks(qd, g), _blocks(kd, g), _blocks(vd, g)
                s_c = jnp.where(mask_c, _bdot_nt(q3, k3) * scale + bias_c, NEG)
                mx = jnp.max(s_c, axis=-1, keepdims=True)
                if nb > 1:
                    kp3, vp3 = _blocks(kps, g), _blocks(vps, g)
                    s_p = jnp.where(jnp.logical_and(mask_p, _has_prev(g, nb)),
                                    _bdot_nt(q3, kp3) * scale + bias_p, NEG)
                    mx = jnp.maximum(mx, jnp.max(s_p, axis=-1, keepdims=True))
                    l = (jnp.sum(jnp.exp(s_c - mx), axis=-1, keepdims=True)
                         + jnp.sum(jnp.exp(s_p - mx), axis=-1, keepdims=True))
                    lse = mx + jnp.log(l)
                    o3 = _bdot(jnp.exp(s_c - lse).astype(BF16), v3) + _bdot(jnp.exp(s_p - lse).astype(BF16), vp3)
                else:
                    l = jnp.sum(jnp.exp(s_c - mx), axis=-1, keepdims=True)
                    lse = mx + jnp.log(l)
                    o3 = _bdot(jnp.exp(s_c - lse).astype(BF16), v3)
                rows = slice(g * GB * CH, (g + 1) * GB * CH)
                od[rows, :] = o3.reshape(GB * CH, AHD)
                ld[rows, :] = jnp.broadcast_to(lse, (GB, CH, AHD)).reshape(GB * CH, AHD)
            ln = S // d
            for rr in range(d):
                if d > 1:
                    onat[pi][pl.ds(rr, ln, stride=d), :] = od[rr * ln:(rr + 1) * ln, :]
                    lnat[pi][pl.ds(rr, ln, stride=d), :] = ld[rr * ln:(rr + 1) * ln, :]
                else:
                    onat[pi][...] = od[...]
                    lnat[pi][...] = ld[...]
        l0, l1, l2 = lnat[0][...], lnat[1][...], lnat[2][...]
        mx = jnp.maximum(jnp.maximum(l0, l1), l2)
        e0, e1, e2 = jnp.exp(l0 - mx), jnp.exp(l1 - mx), jnp.exp(l2 - mx)
        den = e0 + e1 + e2
        out = (e0 / den) * onat[0][...] + (e1 / den) * onat[1][...] + (e2 / den) * onat[2][...]
        o_ref[...] = out
        ob_ref[...] = out.astype(BF16)
        lse_ref[...] = mx + jnp.log(den)

    def col(off):
        return pl.BlockSpec((S, AHD), lambda h: (0, off + h))

    return pl.pallas_call(
        body, name="attn_fwd", grid=(AH,),
        in_specs=[pl.BlockSpec((None, 8, AHD), lambda h: (h, 0, 0)), col(0), col(AH), col(2 * AH)],
        out_specs=[col(0), col(0), col(0)],
        out_shape=[jax.ShapeDtypeStruct((S, AH * AHD), F32), jax.ShapeDtypeStruct((S, AH * AHD), BF16),
                   jax.ShapeDtypeStruct((S, AH * AHD), F32)],
        scratch_shapes=[pltpu.VMEM((S, AHD), BF16) for _ in range(5)]
        + [pltpu.VMEM((S, AHD), F32) for _ in range(8)],
        compiler_params=_cp(("parallel",)),
    )(_attn_consts(), proj, proj, proj)


def _attn_bwd(proj, dmixed, o, lse):
    scale = 1.0 / math.sqrt(AHD)

    def body(c_ref, q_ref, k_ref, v_ref, do_ref, o_ref, lse_ref, dq_ref, dk_ref, dv_ref,
             qd, kd, vd, dod, kps, vps, lsd, dld, dqd, dkd, dvd, delta, aq, ak, av):
        slope = c_ref[0:1, :]
        mask_c, mask_p, dist_c, dist_p = _attn_masks()
        delta[...] = jnp.broadcast_to(jnp.sum(do_ref[...] * o_ref[...], axis=-1, keepdims=True), (S, AHD))
        for pi, (d, nb) in enumerate(PATTERNS):
            _permute_in(qd, q_ref, d, BF16)
            _permute_in(kd, k_ref, d, BF16)
            _permute_in(vd, v_ref, d, BF16)
            _permute_in(dod, do_ref, d, BF16)
            _permute_in(lsd, lse_ref, d)
            _permute_in(dld, delta, d)
            if nb > 1:
                _shift_block(kps, kd)
                _shift_block(vps, vd)
            bias_c = -(slope * float(d)) * dist_c
            bias_p = -(slope * float(d)) * dist_p
            for g in range(NB // GB):
                q3, k3, v3, do3 = _blocks(qd, g), _blocks(kd, g), _blocks(vd, g), _blocks(dod, g)
                ls, dl = _blocks(lsd, g), _blocks(dld, g)
                lo, hi = g * GB * CH, (g + 1) * GB * CH
                p_c = jnp.exp(jnp.where(mask_c, _bdot_nt(q3, k3) * scale + bias_c, NEG) - ls)
                ds_c = ((p_c * (_bdot_nt(do3, v3) - dl)) * scale).astype(BF16)
                dq3 = _bdot(ds_c, k3)
                dkd[lo:hi, :] = _bdot_tn(ds_c, q3).reshape(GB * CH, AHD)
                dvd[lo:hi, :] = _bdot_tn(p_c.astype(BF16), do3).reshape(GB * CH, AHD)
                if nb > 1:
                    kp3, vp3 = _blocks(kps, g), _blocks(vps, g)
                    p_p = jnp.exp(jnp.where(jnp.logical_and(mask_p, _has_prev(g, nb)),
                                            _bdot_nt(q3, kp3) * scale + bias_p, NEG) - ls)
                    ds_p = ((p_p * (_bdot_nt(do3, vp3) - dl)) * scale).astype(BF16)
                    dq3 = dq3 + _bdot(ds_p, kp3)
                    dkp = _bdot_tn(ds_p, q3).reshape(GB * CH, AHD)
                    dvp = _bdot_tn(p_p.astype(BF16), do3).reshape(GB * CH, AHD)
                    if g == 0:
                        dkd[0:hi - CH, :] += dkp[CH:, :]
                        dvd[0:hi - CH, :] += dvp[CH:, :]
                    else:
                        dkd[lo - CH:hi - CH, :] += dkp
                        dvd[lo - CH:hi - CH, :] += dvp
                dqd[lo:hi, :] = dq3.reshape(GB * CH, AHD)
            ln = S // d
            for acc, src in ((aq, dqd), (ak, dkd), (av, dvd)):
                if pi == 0:
                    acc[...] = src[...]
                else:
                    for rr in range(d):
                        acc[pl.ds(rr, ln, stride=d), :] += src[rr * ln:(rr + 1) * ln, :]
        dq_ref[...] = aq[...].astype(BF16)
        dk_ref[...] = ak[...].astype(BF16)
        dv_ref[...] = av[...].astype(BF16)

    def col(off):
        return pl.BlockSpec((S, AHD), lambda h: (0, off + h))

    return pl.pallas_call(
        body, name="attn_bwd", grid=(AH,),
        in_specs=[pl.BlockSpec((None, 8, AHD), lambda h: (h, 0, 0)), col(0), col(AH), col(2 * AH),
                  col(0), col(0), col(0)],
        out_specs=[col(0), col(0), col(0)],
        out_shape=[jax.ShapeDtypeStruct((S, AH * AHD), BF16)] * 3,
        scratch_shapes=[pltpu.VMEM((S, AHD), BF16) for _ in range(6)]
        + [pltpu.VMEM((S, AHD), F32) for _ in range(9)],
        compiler_params=_cp(("parallel",)),
    )(_attn_consts(), proj, proj, proj, dmixed, o, lse)


def _ret_consts():
    c = np.zeros((RH, 8, RHD), np.float32)
    for h in range(RH):
        c[h, :, :] = np.log(np.float32(1.0) - np.float32(2.0 ** (-5.0 - h)))
    return jnp.asarray(c)


def _ret_factors(lg):
    i = lax.broadcasted_iota(jnp.int32, (CH, CH), 0)
    j = lax.broadcasted_iota(jnp.int32, (CH, CH), 1)
    dif = (i - j).astype(F32)
    decay = jnp.where(dif >= 0, jnp.exp(lg[:, 0:CH] * jnp.maximum(dif, 0.0)), 0.0)
    row = lax.broadcasted_iota(jnp.int32, (CH, RHD), 0).astype(F32)
    zeta = jnp.exp(lg * (CH - 1.0 - row))
    xi = jnp.exp(lg * (row + 1.0))
    return decay, zeta, xi, jnp.exp(lg * float(CH))


CBK = 8
RSTEPS = NB // CBK


def _ret_specs(rev):
    off = 3 * AH * AHD // RHD
    rows = CBK * CH

    def ch(n):
        return (RSTEPS - 1 - n) if rev else n

    def col(k):
        return pl.BlockSpec((rows, RHD), lambda h, n: (ch(n), off + k * RH + h))

    own = pl.BlockSpec((rows, RHD), lambda h, n: (ch(n), h))
    state = pl.BlockSpec((None, CBK, RHD, RHD), lambda h, n: (h, ch(n), 0, 0))
    const = pl.BlockSpec((None, 8, RHD), lambda h, n: (h, 0, 0))
    dm = pl.BlockSpec((rows, RHD), lambda h, n: (ch(n), AH * AHD // RHD + h))
    return col, own, state, const, dm


def _chunks(x):
    return x.reshape(CBK, CH, RHD)


def _ret_fwd(proj):
    def body(c_ref, q_ref, k_ref, v_ref, g_ref, ret_ref, mr_ref, st_ref, r_acc):
        n = pl.program_id(1)

        @pl.when(n == 0)
        def _():
            r_acc[...] = jnp.zeros_like(r_acc)

        decay, zeta, xi, gch = _ret_factors(c_ref[0:1, :])
        q3 = _chunks(q_ref[...].astype(BF16))
        kc = _chunks(k_ref[...] * (1.0 / math.sqrt(RHD)))
        k3 = kc.astype(BF16)
        v3 = _chunks(v_ref[...].astype(BF16))
        kv3 = _bdot_tn((kc * zeta[None]).astype(BF16), v3)
        r = r_acc[...]
        for i in range(CBK):
            st_ref[i] = r.astype(BF16)
            r = r * gch + kv3[i]
        r_acc[...] = r
        scores = _bdot_nt(q3, k3) * decay[None]
        ret = (_bdot(scores.astype(BF16), v3) + _bdot(q3, st_ref[...]) * xi[None]).reshape(CBK * CH, RHD)
        ret_ref[...] = ret
        rr = lax.rsqrt(jnp.mean(ret * ret, axis=-1, keepdims=True) + EPS)
        gv = g_ref[...]
        mr_ref[...] = ((gv * _sigmoid(gv)) * (ret * rr)).astype(BF16)

    col, own, state, const, _ = _ret_specs(False)
    return pl.pallas_call(
        body, name="ret_fwd", grid=(RH, RSTEPS),
        in_specs=[const, col(0), col(1), col(2), col(3)],
        out_specs=[own, own, state],
        out_shape=[jax.ShapeDtypeStruct((S, RH * RHD), F32), jax.ShapeDtypeStruct((S, RH * RHD), BF16),
                   jax.ShapeDtypeStruct((RH, NB, RHD, RHD), BF16)],
        scratch_shapes=[pltpu.VMEM((RHD, RHD), F32)],
        compiler_params=_cp(("parallel", "arbitrary")),
    )(_ret_consts(), proj, proj, proj, proj)


def _ret_bwd(proj, ret, states, dmixed):
    def body(c_ref, q_ref, k_ref, v_ref, g_ref, ret_ref, st_ref, dm_ref, dq_ref, dk_ref, dv_ref, dg_ref, g_acc, gs):
        n = pl.program_id(1)

        @pl.when(n == 0)
        def _():
            g_acc[...] = jnp.zeros_like(g_acc)

        decay, zeta, xi, gch = _ret_factors(c_ref[0:1, :])
        ret_v = ret_ref[...]
        rr = lax.rsqrt(jnp.mean(ret_v * ret_v, axis=-1, keepdims=True) + EPS)
        gv = g_ref[...]
        sg = _sigmoid(gv)
        dmix = dm_ref[...]
        dg_ref[...] = ((dmix * (ret_v * rr)) * (sg * (1.0 + gv * (1.0 - sg)))).astype(BF16)
        dretn = dmix * (gv * sg)
        dret = _chunks(rr * dretn - ret_v * ((rr * rr * rr) * jnp.mean(dretn * ret_v, axis=-1, keepdims=True)))

        q3 = _chunks(q_ref[...].astype(BF16))
        kc = _chunks(k_ref[...] * (1.0 / math.sqrt(RHD)))
        k3 = kc.astype(BF16)
        v3 = _chunks(v_ref[...].astype(BF16))
        d3 = dret.astype(BF16)
        dxi = (dret * xi[None]).astype(BF16)
        kz = (kc * zeta[None]).astype(BF16)
        dr3 = _bdot_tn(q3, dxi)
        acc = g_acc[...]
        for i in reversed(range(CBK)):
            gs[i] = acc.astype(BF16)
            acc = dr3[i] + gch * acc
        g_acc[...] = acc
        g3 = gs[...]
        sc = (_bdot_nt(q3, k3) * decay[None]).astype(BF16)
        da = (_bdot_nt(d3, v3) * decay[None]).astype(BF16)
        dq = _bdot(da, k3) + _bdot_nt(dxi, st_ref[...])
        dkc = _bdot_tn(da, q3) + _bdot_nt(v3, g3) * zeta[None]
        dv = _bdot_tn(sc, d3) + _bdot(kz, g3)
        dq_ref[...] = dq.reshape(CBK * CH, RHD).astype(BF16)
        dk_ref[...] = (dkc * (1.0 / math.sqrt(RHD))).reshape(CBK * CH, RHD).astype(BF16)
        dv_ref[...] = dv.reshape(CBK * CH, RHD).astype(BF16)

    col, own, state, const, dm = _ret_specs(True)
    return pl.pallas_call(
        body, name="ret_bwd", grid=(RH, RSTEPS),
        in_specs=[const, col(0), col(1), col(2), col(3), own, state, dm],
        out_specs=[own, own, own, own],
        out_shape=[jax.ShapeDtypeStruct((S, RH * RHD), BF16)] * 4,
        scratch_shapes=[pltpu.VMEM((RHD, RHD), F32), pltpu.VMEM((CBK, RHD, RHD), BF16)],
        compiler_params=_cp(("parallel", "arbitrary")),
    )(_ret_consts(), proj, proj, proj, proj, ret, states, dmixed)


class _NoReduction:
    def start(self, group, grads):
        pass

    def local(self, name, first=()):
        return []

    def landed(self, name):
        return []

    def update(self, name):
        return []


def _local_step(x, tgt, nw1, nw2, nw3, win, wout, wg, wu, wd, red=None):
    red = red or _NoReduction()

    def after(values, first):
        return lax.optimization_barrier((tuple(values), tuple(first)))[0]

    wg, wu, wd = (w.reshape(NFG, N_FG, D) for w in (wg, wu, wd))
    h1, r1 = _rms_fwd(x, nw1)
    proj = _proj(h1, win)
    o, ma, lse = _attn_fwd(proj)
    ret, mr, states = _ret_fwd(proj)
    x2, h2, r2 = _out_proj_rms(x, ma, mr, wout, nw2)
    g, u, a = _ffn_up(h2, wg, wu)
    dx3, dx3b, st3 = _ffn_down_loss(x2, a, wd, nw3, tgt)

    dwd = _wgrad_rows(a, dx3b, "wgrad_down")
    red.start(["w_down"], [dwd])
    (dx3b,) = after([dx3b], [dwd])
    dg, du = _ffn_down_bwd(dx3b, wd, g, u)
    dg, du = after([dg, du], red.local("w_down", first=[dg]))
    dwg = _wgrad_rows(dg, h2, "wgrad_gate")
    red.start(["w_gate"], [dwg])
    (du,) = after([du], [dwg])
    dwu = _wgrad_rows(du, h2, "wgrad_up")
    red.start(["w_up"], [dwu])
    dg, du = after([dg, du], [dwu] + red.local("w_gate"))
    dx2, dx2b, st2 = _ffn_up_bwd(dg, du, wg, wu, dx3, x2, r2, nw2)
    (dx2b,) = after([dx2b], red.local("w_up", first=[dx2b] + red.landed("w_down")))
    dwo = _wgrad_out(ma, mr, dx2b)
    red.start(["w_out"], [dwo])
    (dx2b,) = after([dx2b], [dwo])
    dmixed = _out_proj_bwd(dx2b, wout)
    dqa, dka, dva = _attn_bwd(proj, dmixed, o, lse)
    (dmixed,) = after([dmixed], red.local("w_out", first=[dqa] + red.landed("w_gate")))
    dqr, dkr, dvr, dgr = _ret_bwd(proj, ret, states, dmixed)
    dproj = jnp.concatenate([dqa, dka, dva, dqr, dkr, dvr, dgr], axis=1)
    (dwi,) = after([_wgrad_in(h1, dproj)], red.landed("w_up"))
    red.start(["w_in"], [dwi])
    early = red.update("w_down") + red.landed("w_out")
    (dproj,) = after([dproj], red.local("w_in", first=early))
    gx, st1 = _in_proj_bwd(dproj, win, dx2, x, r1, nw1)
    stats = jnp.concatenate([st1[0:1], st2[0:1], st3[0:2], jnp.zeros((4, D), F32)], axis=0)
    return stats, gx, dwi, dwo, dwg, dwu, dwd


def _place():
    x, y, c = lax.axis_index("x"), lax.axis_index("y"), lax.axis_index("c")
    return x, y, c, [(1 - x, y), (x, 1 - y), (1 - x, 1 - y)]


def _handshake(peers):
    barrier = pltpu.get_barrier_semaphore()
    for peer in peers:
        pl.semaphore_signal(barrier, inc=1, device_id=peer, device_id_type=MESH)
    pl.semaphore_wait(barrier, len(peers))


def _all_gather(shards, name, collective_id):
    na = len(shards)
    SIB, XN0, XN1, YN1, YN0, VIA_X, VIA_Y = 0, 1, 2, 3, 4, 5, 6
    D2D = {XN0: 7, XN1: 8, YN1: 9, YN0: 10, VIA_X: 11, VIA_Y: 12}

    def body(*refs):
        ins, outs = refs[:na], refs[na:2 * na]
        send_sems, recv_sems, local_sems = refs[2 * na:]
        x, y, c, _ = _place()
        me, sib = (x, y, c), (x, y, 1 - c)
        xn, yn, dg = (1 - x, y, c), (x, 1 - y, c), (1 - x, 1 - y, c)
        _handshake([sib, xn, yn])

        def part(ref, h):
            rows = ref.shape[0] // 2
            return ref if h is None else ref.at[pl.ds(h * rows, rows)]

        def block(a, owner, h):
            return part(outs[a].at[4 * owner[0] + 2 * owner[1] + owner[2]], h)

        def copy(a, k, owner, h, to, own_src=False):
            return pltpu.make_async_remote_copy(
                src_ref=part(ins[a], h) if own_src else block(a, owner, h), dst_ref=block(a, owner, h),
                send_sem=send_sems.at[a, k], recv_sem=recv_sems.at[a, k], device_id=to, device_id_type=MESH)

        def other(p):
            return (p[0], p[1], 1 - c)

        mine = [pltpu.make_async_copy(ins[a], block(a, me, None), local_sems.at[a]) for a in range(na)]
        for cp in mine:
            cp.start()
        sent = []
        for a in range(na):
            sent += [copy(a, XN0, me, 0, xn, True), copy(a, YN1, me, 1, yn, True),
                     copy(a, XN1, me, 1, xn, True), copy(a, YN0, me, 0, yn, True)]
        sent += [copy(a, SIB, me, None, sib, True) for a in range(na)]
        for cp in sent:
            cp.start()

        def landed(a, k, owner, h, then):
            copy(a, k, owner, h, me).wait_recv()
            for k2, to in then + [(D2D[k], sib)]:
                cp = copy(a, k2, owner, h, to)
                cp.start()
                sent.append(cp)

        for a in range(na):
            landed(a, XN0, xn, 0, [(VIA_Y, yn)])
            landed(a, YN1, yn, 1, [(VIA_X, xn)])
            landed(a, XN1, xn, 1, [])
            landed(a, YN0, yn, 0, [])
        for a in range(na):
            landed(a, VIA_Y, dg, 0, [])
            landed(a, VIA_X, dg, 1, [])
        for a in range(na):
            copy(a, SIB, sib, None, me).wait_recv()
            for k, owner, h in ((XN0, xn, 0), (XN1, xn, 1), (YN1, yn, 1), (YN0, yn, 0), (VIA_Y, dg, 0), (VIA_X, dg, 1)):
                copy(a, D2D[k], other(owner), h, me).wait_recv()
        for cp in sent:
            cp.wait_send()
        for cp in mine:
            cp.wait()

    return _sequencer_call(
        body, name, collective_id,
        [jax.ShapeDtypeStruct((NDEV,) + s.shape, s.dtype) for s in shards],
        [pltpu.SemaphoreType.DMA((na, 13)), pltpu.SemaphoreType.DMA((na, 13)), pltpu.SemaphoreType.DMA((na,))])(*shards)


def _sequencer_call(body, name, collective_id, out_type, scratch_types):
    return pl.kernel(
        body, name=name, out_type=out_type,
        mesh=plsc.ScalarSubcoreMesh(axis_name="sequencer", num_cores=1),
        scratch_types=scratch_types,
        compiler_params=pltpu.CompilerParams(collective_id=collective_id))


def _exchange_sibling(grads, name, collective_id):
    na = len(grads)

    def body(*refs):
        ins, outs = refs[:na], refs[na:2 * na]
        send_sems, recv_sems = refs[2 * na:]
        x, y, c, _ = _place()
        _handshake([(x, y, 1 - c)])
        cps = []
        for a in range(na):
            for k in range(4):
                cps.append(pltpu.make_async_remote_copy(
                    src_ref=ins[a].at[2 * k + (1 - c)], dst_ref=outs[a].at[k],
                    send_sem=send_sems.at[a, k], recv_sem=recv_sems.at[a, k],
                    device_id=(x, y, 1 - c), device_id_type=MESH))
        for cp in cps:
            cp.start()
        for cp in cps:
            cp.wait()

    return _sequencer_call(
        body, name, collective_id,
        [jax.ShapeDtypeStruct((4,) + g.shape[1:], g.dtype) for g in grads],
        [pltpu.SemaphoreType.DMA((na, 4)), pltpu.SemaphoreType.DMA((na, 4))])(*grads)


def _row_tile(rows, cols):
    for t in (512, 256, 176, 128, 64, 32, 16):
        if rows % t == 0 and t * cols * 4 <= (2 << 20):
            return t
    raise ValueError((rows, cols))


def _chip_sum(place, g, got, name):
    _, r, c = g.shape
    tm = r

    def body(pos_ref, g_ref, got_ref, o_ref):
        o_ref[...] = (g_ref[...].astype(F32) + got_ref[...].astype(F32)).astype(BF16)

    def chip(j, pos):
        return 2 * (pos[0] ^ jnp.where(j == 1, 0, 1)) + (pos[1] ^ jnp.where(j == 0, 0, 1))

    return pl.pallas_call(
        body, name=name,
        grid_spec=pltpu.PrefetchScalarGridSpec(
            num_scalar_prefetch=1, grid=(3, r // tm),
            in_specs=[pl.BlockSpec((None, tm, c), lambda j, i, pos: (2 * chip(j, pos) + pos[2], i, 0)),
                      pl.BlockSpec((None, tm, c), lambda j, i, pos: (chip(j, pos), i, 0))],
            out_specs=pl.BlockSpec((None, tm, c), lambda j, i, pos: (j, i, 0))),
        out_shape=jax.ShapeDtypeStruct((3, r, c), BF16),
        compiler_params=_cp(("parallel", "parallel")),
    )(place, g, got)


def _exchange_chips(sums, name, collective_id):
    na = len(sums)

    def body(*refs):
        ins, outs = refs[:na], refs[na:2 * na]
        send_sems, recv_sems = refs[2 * na:]
        x, y, c, chips = _place()
        _handshake([(*chip, c) for chip in chips])
        cps = []
        for a in range(na):
            for j, chip in enumerate(chips):
                cps.append(pltpu.make_async_remote_copy(
                    src_ref=ins[a].at[j], dst_ref=outs[a].at[j],
                    send_sem=send_sems.at[a, j], recv_sem=recv_sems.at[a, j],
                    device_id=(*chip, c), device_id_type=MESH))
        for cp in cps:
            cp.start()
        for cp in cps:
            cp.wait()

    return _sequencer_call(
        body, name, collective_id,
        [jax.ShapeDtypeStruct((3,) + s.shape[1:], s.dtype) for s in sums],
        [pltpu.SemaphoreType.DMA((na, 3)), pltpu.SemaphoreType.DMA((na, 3))])(*sums)


def _exchange_stats(stats, collective_id):
    def body(st_in, st_out, st_send, st_recv, local_sem):
        x, y, c, _ = _place()
        me_idx = 4 * x + 2 * y + c
        peers = [(x ^ ((k >> 2) & 1), y ^ ((k >> 1) & 1), c ^ (k & 1)) for k in range(1, 8)]
        _handshake(peers)
        mine = pltpu.make_async_copy(st_in, st_out.at[me_idx], local_sem)
        mine.start()
        cps = [pltpu.make_async_remote_copy(
            src_ref=st_in, dst_ref=st_out.at[me_idx], send_sem=st_send.at[k], recv_sem=st_recv.at[k],
            device_id=peer, device_id_type=MESH) for k, peer in enumerate(peers)]
        for cp in cps:
            cp.start()
        for cp in cps:
            cp.wait()
        mine.wait()

    return _sequencer_call(
        body, "exchange_stats", collective_id,
        jax.ShapeDtypeStruct((NDEV,) + stats.shape, stats.dtype),
        [pltpu.SemaphoreType.DMA((7,)), pltpu.SemaphoreType.DMA((7,)), pltpu.SemaphoreType.DMA])(stats)


class _Reduction:
    def __init__(self, place, first_collective_id, state):
        self.place = place
        self.ids = iter(range(first_collective_id, 32))
        self.state = state
        self.groups = {}
        self.updates = {}

    def next_id(self):
        return next(self.ids)

    def start(self, group, grads):
        got = _exchange_sibling(grads, "sibling_exchange_" + group[0], self.next_id())
        self.groups[group[0]] = dict(names=group, grads=grads, got=got)

    def local(self, name, first=()):
        grp = self.groups[name]
        grads = lax.optimization_barrier((tuple(grp["grads"]), tuple(first)))[0]
        grp["sums"] = [_chip_sum(self.place, g, s, "chip_sum_" + n)
                       for g, s, n in zip(grads, grp["got"], grp["names"])]
        grp["chips"] = _exchange_chips(grp["sums"], "chip_exchange_" + name, self.next_id())
        return grp["sums"]

    def landed(self, name):
        return list(self.groups[name]["chips"])

    def update(self, name):
        if name not in self.updates:
            grp = next(g for g in self.groups.values() if name in g["names"])
            k = grp["names"].index(name)
            self.updates[name] = _shard_update(self.place, *self.state[name], grp["grads"][k], grp["got"][k],
                                               grp["chips"][k], "update_" + name)
        return list(self.updates[name])


def _adamw(w, g, m, v):
    m = ADAM_B1 * m + (1.0 - ADAM_B1) * g
    v = ADAM_B2 * v + (1.0 - ADAM_B2) * (g * g)
    m_hat = m / (1.0 - ADAM_B1 ** ADAM_STEP)
    v_hat = v / (1.0 - ADAM_B2 ** ADAM_STEP)
    delta = -ADAM_LR * (m_hat / (jnp.sqrt(v_hat) + ADAM_EPS) + ADAM_WD * w)
    return delta, m, v


def _shard_update(place, w, m, v, g, got_sib, got_chips, name):
    r, c = w.shape
    tm = _row_tile(r, c)

    def body(pos_ref, w_ref, m_ref, v_ref, g_ref, s_ref, c_ref, go_ref, d_ref, mo_ref, vo_ref):
        grad = g_ref[...].astype(F32) + s_ref[...].astype(F32)
        for j in range(3):
            grad = grad + c_ref[j].astype(F32)
        delta, mn, vn = _adamw(w_ref[...], grad, m_ref[...], v_ref[...])
        go_ref[...] = grad
        d_ref[...] = delta
        mo_ref[...] = mn
        vo_ref[...] = vn

    row = pl.BlockSpec((tm, c), lambda i, pos: (i, 0))
    return pl.pallas_call(
        body, name=name,
        grid_spec=pltpu.PrefetchScalarGridSpec(
            num_scalar_prefetch=1, grid=(r // tm,),
            in_specs=[row, row, row,
                      pl.BlockSpec((None, tm, c), lambda i, pos: (4 * pos[0] + 2 * pos[1] + pos[2], i, 0)),
                      pl.BlockSpec((None, tm, c), lambda i, pos: (2 * pos[0] + pos[1], i, 0)),
                      pl.BlockSpec((3, tm, c), lambda i, pos: (0, i, 0))],
            out_specs=[row, row, row, row]),
        out_shape=[jax.ShapeDtypeStruct((r, c), F32)] * 4,
        compiler_params=_cp(("parallel",)),
    )(place, w, m, v, g, got_sib, got_chips)


def _small_update(stats_all, ws, ms, vs):
    def body(st_ref, w_ref, m_ref, v_ref, go_ref, d_ref, mo_ref, vo_ref):
        grad = st_ref[0]
        for k in range(1, NDEV):
            grad = grad + st_ref[k]
        delta, mn, vn = _adamw(w_ref[...], grad, m_ref[...], v_ref[...])
        go_ref[...] = grad
        d_ref[...] = delta
        mo_ref[...] = mn
        vo_ref[...] = vn

    return pl.pallas_call(
        body, name="small_update",
        out_shape=[jax.ShapeDtypeStruct((8, D), F32)] * 4,
        compiler_params=_cp(),
    )(stats_all, ws, ms, vs)


def kernel(x, norm_mix_w, w_in, w_out, norm_ffn_w, w_gate, w_up, w_down, norm_final_w, loss_target, m_norm_mix_w, m_w_in, m_w_out, m_norm_ffn_w, m_w_gate, m_w_up, m_w_down, m_norm_final_w, v_norm_mix_w, v_w_in, v_w_out, v_norm_ffn_w, v_w_gate, v_w_up, v_w_down, v_norm_final_w):
    tr = {"w_gate", "w_up"}
    names = ["w_in", "w_out", "w_gate", "w_up", "w_down"]

    def view(a, n):
        return a[0].T if n in tr else a[0]

    big_w = [view(a, n) for a, n in zip([w_in, w_out, w_gate, w_up, w_down], names)]
    big_m = [view(a, n) for a, n in zip([m_w_in, m_w_out, m_w_gate, m_w_up, m_w_down], names)]
    big_v = [view(a, n) for a, n in zip([v_w_in, v_w_out, v_w_gate, v_w_up, v_w_down], names)]

    shards = [_cast_bf16(w, "cast_" + n) for w, n in zip(big_w, names)]
    (win,) = _all_gather(shards[0:1], "all_gather_w_in", 1)
    wout, wg, wu = _all_gather(shards[1:4], "all_gather_out_gate_up", 2)
    (wd,) = _all_gather(shards[4:5], "all_gather_w_down", 3)
    nw3 = norm_final_w.reshape(1, D)
    place = jnp.stack([lax.axis_index("x"), lax.axis_index("y"), lax.axis_index("c")]).astype(jnp.int32)
    red = _Reduction(place, 4, {n: (w, m, v) for n, w, m, v in zip(names, big_w, big_m, big_v)})
    stats, gx, *_ = _local_step(
        x[0], loss_target[0], norm_mix_w, norm_ffn_w, nw3, win, wout.reshape(D, D), wg, wu, wd, red)
    stats_all = _exchange_stats(stats, red.next_id())
    upd = [red.update(n) for n in names]
    stats_all = lax.optimization_barrier((stats_all, tuple(upd[0])))[0]

    def rows(a, b, c):
        return jnp.concatenate([a.reshape(1, D), b.reshape(1, D), c.reshape(1, D), jnp.zeros((5, D), F32)], axis=0)

    sg, sd, sm, sv = _small_update(stats_all, rows(norm_mix_w, norm_ffn_w, norm_final_w),
                                   rows(m_norm_mix_w, m_norm_ffn_w, m_norm_final_w),
                                   rows(v_norm_mix_w, v_norm_ffn_w, v_norm_final_w))
    loss = sg[3, 0]

    def outs(k, small):
        big = [(u[k].T if n in tr else u[k])[None] for u, n in zip(upd, names)]
        return [small[0:1], big[0], big[1], small[1:2], big[2], big[3], big[4], small[2]]

    return (loss, gx[None], *outs(0, sg), *outs(1, sd), *outs(2, sm), *outs(3, sv))
```

```python
import functools
import math

import numpy as np
import jax
import jax.numpy as jnp
from jax import lax
from jax.experimental import pallas as pl
from jax.experimental.pallas import tpu as pltpu
from jax.experimental.pallas import tpu_sc as plsc

F32 = jnp.float32
BF16 = jnp.bfloat16

S = 2048
D = 2048
NDEV = 8
N_IN = 7168 // NDEV
N_FF = 5632 // NDEV
NFG, N_FG = NDEV // 2, 2 * N_FF
N_OUT = 2048 // NDEV
AH, AHD = 8, 128
RH, RHD = 4, 256
CH = 128
NB = S // CH
EPS = 1e-6
PATTERNS = ((1, 16), (4, 4), (16, 1))
NEG = -1e30
VMEM_LIMIT = 56 * 1024 * 1024

ADAM_LR, ADAM_B1, ADAM_B2, ADAM_EPS, ADAM_WD, ADAM_STEP = 0.001, 0.9, 0.999, 1e-08, 0.01, 10
MESH = pl.DeviceIdType.MESH


def _cp(sem=None):
    return pltpu.CompilerParams(dimension_semantics=sem, vmem_limit_bytes=VMEM_LIMIT)


def _dot(a, b):
    return jnp.dot(a, b, preferred_element_type=F32)


def _dot_nt(a, b):
    return lax.dot_general(a, b, (((1,), (1,)), ((), ())), preferred_element_type=F32)


def _dot_tn(a, b):
    return lax.dot_general(a, b, (((0,), (0,)), ((), ())), preferred_element_type=F32)


def _sigmoid(x):
    return 0.5 * jnp.tanh(0.5 * x) + 0.5


def _cast_bf16(w, name):
    r, c = w.shape
    tm = r if r <= 1024 else 512

    def body(w_ref, o_ref):
        o_ref[...] = w_ref[...].astype(BF16)

    return pl.pallas_call(
        body, name=name, grid=(r // tm,),
        in_specs=[pl.BlockSpec((tm, c), lambda i: (i, 0))],
        out_specs=pl.BlockSpec((tm, c), lambda i: (i, 0)),
        out_shape=jax.ShapeDtypeStruct((r, c), BF16),
        compiler_params=_cp(("parallel",)),
    )(w)


def _rms_fwd(x, nw):
    tm = 256

    def body(x_ref, w_ref, h_ref, r_ref):
        xs = x_ref[...]
        r = lax.rsqrt(jnp.mean(xs * xs, axis=-1, keepdims=True) + EPS)
        h_ref[...] = ((xs * r) * w_ref[...]).astype(BF16)
        r_ref[...] = r

    return pl.pallas_call(
        body, name="rms_fwd", grid=(S // tm,),
        in_specs=[pl.BlockSpec((tm, D), lambda i: (i, 0)), pl.BlockSpec((1, D), lambda i: (0, 0))],
        out_specs=[pl.BlockSpec((tm, D), lambda i: (i, 0)), pl.BlockSpec((tm, 1), lambda i: (i, 0))],
        out_shape=[jax.ShapeDtypeStruct((S, D), BF16), jax.ShapeDtypeStruct((S, 1), F32)],
        compiler_params=_cp(("parallel",)),
    )(x, nw)


def _row_copies(hbm_refs, bufs, sems, m, tm):
    rows = pl.ds(pl.multiple_of(m * tm, tm), tm)
    return [pltpu.make_async_copy(h.at[rows], b, sems.at[i]) for i, (h, b) in enumerate(zip(hbm_refs, bufs))]


def _rms_bwd_tile(dh, xs, r, nw):
    dnw = jnp.sum(dh * (xs * r), axis=0, keepdims=True)
    gy = dh * nw
    dx = r * gy - xs * ((r * r * r) * jnp.mean(gy * xs, axis=-1, keepdims=True))
    return dx, dnw


def _proj(h1, win):
    tm = 1024

    def body(a_ref, w_ref, o_ref):
        o_ref[...] = _dot(a_ref[...], w_ref[...])

    return pl.pallas_call(
        body, name="proj", grid=(NDEV, S // tm),
        in_specs=[pl.BlockSpec((tm, D), lambda p, m: (m, 0)),
                  pl.BlockSpec((None, D, N_IN), lambda p, m: (p, 0, 0))],
        out_specs=pl.BlockSpec((tm, N_IN), lambda p, m: (m, p)),
        out_shape=jax.ShapeDtypeStruct((S, NDEV * N_IN), F32),
        compiler_params=_cp(("parallel", "parallel")),
    )(h1, win)


def _out_proj_rms(x, ma, mr, wout, nw):
    tm = 256
    half = D // 2

    def body(x_ref, ma_ref, mr_ref, w_ref, nw_ref, x2_ref, h_ref, r_ref):
        acc = _dot(ma_ref[...], w_ref[0:half, :]) + _dot(mr_ref[...], w_ref[half:D, :])
        x2 = x_ref[...] + acc
        r = lax.rsqrt(jnp.mean(x2 * x2, axis=-1, keepdims=True) + EPS)
        x2_ref[...] = x2
        h_ref[...] = ((x2 * r) * nw_ref[...]).astype(BF16)
        r_ref[...] = r

    return pl.pallas_call(
        body, name="out_proj_rms", grid=(S // tm,),
        in_specs=[pl.BlockSpec((tm, D), lambda i: (i, 0)),
                  pl.BlockSpec((tm, half), lambda i: (i, 0)),
                  pl.BlockSpec((tm, half), lambda i: (i, 0)),
                  pl.BlockSpec((D, D), lambda i: (0, 0)),
                  pl.BlockSpec((1, D), lambda i: (0, 0))],
        out_specs=[pl.BlockSpec((tm, D), lambda i: (i, 0)), pl.BlockSpec((tm, D), lambda i: (i, 0)),
                   pl.BlockSpec((tm, 1), lambda i: (i, 0))],
        out_shape=[jax.ShapeDtypeStruct((S, D), F32), jax.ShapeDtypeStruct((S, D), BF16),
                   jax.ShapeDtypeStruct((S, 1), F32)],
        compiler_params=_cp(("parallel",)),
    )(x, ma, mr, wout, nw)


def _ffn_up(h2, wg, wu):
    tm = 512

    def body(h_ref, wg_ref, wu_ref, g_ref, u_ref, a_ref):
        h = h_ref[...]
        g = _dot_nt(h, wg_ref[...])
        u = _dot_nt(h, wu_ref[...])
        g_ref[...] = g
        u_ref[...] = u
        a_ref[...] = ((g * _sigmoid(g)) * u).astype(BF16)

    blk = pl.BlockSpec((None, tm, N_FG), lambda p, m: (p, m, 0))
    wblk = pl.BlockSpec((None, N_FG, D), lambda p, m: (p, 0, 0))
    return pl.pallas_call(
        body, name="ffn_up", grid=(NFG, S // tm),
        in_specs=[pl.BlockSpec((tm, D), lambda p, m: (m, 0)), wblk, wblk],
        out_specs=[blk, blk, blk],
        out_shape=[jax.ShapeDtypeStruct((NFG, S, N_FG),F32), jax.ShapeDtypeStruct((NFG, S, N_FG),F32),
                   jax.ShapeDtypeStruct((NFG, S, N_FG),BF16)],
        compiler_params=_cp(("parallel", "parallel")),
    )(h2, wg, wu)


def _ffn_down_loss(x2, a, wd, nw, tgt):
    tm = 512

    def body(x2_hbm, a_ref, w_ref, nw_ref, t_hbm, dx_ref, dxb_ref, st_ref, acc_ref, x2_buf, t_buf, sems):
        m, p = pl.program_id(0), pl.program_id(1)
        tail_in = _row_copies((x2_hbm, t_hbm), (x2_buf, t_buf), sems, m, tm)

        @pl.when(p == 0)
        def _():
            acc_ref[...] = jnp.zeros_like(acc_ref)
            for cp in tail_in:
                cp.start()

        @pl.when((p == 0) & (m == 0))
        def _():
            st_ref[...] = jnp.zeros_like(st_ref)

        acc_ref[...] += _dot(a_ref[...], w_ref[...])

        @pl.when(p == NFG - 1)
        def _():
            for cp in tail_in:
                cp.wait()
            x3 = x2_buf[...] + acc_ref[...]
            nwv = nw_ref[...]
            r = lax.rsqrt(jnp.mean(x3 * x3, axis=-1, keepdims=True) + EPS)
            y = (x3 * r) * nwv
            err = y - t_buf[...]
            loss = 0.5 * jnp.sum(jnp.mean(err * err, axis=-1, keepdims=True), axis=0, keepdims=True)
            dy = err * (1.0 / D)
            dx, dnw = _rms_bwd_tile(dy, x3, r, nwv)
            dx_ref[...] = dx
            dxb_ref[...] = dx.astype(BF16)
            st_ref[0:1, :] += dnw
            st_ref[1:2, :] += jnp.broadcast_to(loss, (1, D))

    return pl.pallas_call(
        body, name="ffn_down_loss", grid=(S // tm, NFG),
        in_specs=[pl.BlockSpec(memory_space=pl.ANY),
                  pl.BlockSpec((None, tm, N_FG), lambda m, p: (p, m, 0)),
                  pl.BlockSpec((None, N_FG, D), lambda m, p: (p, 0, 0)),
                  pl.BlockSpec((1, D), lambda m, p: (0, 0)),
                  pl.BlockSpec(memory_space=pl.ANY)],
        out_specs=[pl.BlockSpec((tm, D), lambda m, p: (m, 0)), pl.BlockSpec((tm, D), lambda m, p: (m, 0)),
                   pl.BlockSpec((8, D), lambda m, p: (0, 0))],
        out_shape=[jax.ShapeDtypeStruct((S, D), F32), jax.ShapeDtypeStruct((S, D), BF16),
                   jax.ShapeDtypeStruct((8, D), F32)],
        scratch_shapes=[pltpu.VMEM((tm, D), F32), pltpu.VMEM((tm, D), F32), pltpu.VMEM((tm, D), F32),
                        pltpu.SemaphoreType.DMA((2,))],
        compiler_params=_cp(("arbitrary", "arbitrary")),
    )(x2, a, wd, nw, tgt)


def _ffn_down_bwd(dx3b, wd, g, u):
    tm = 512

    def body(dx_ref, w_ref, g_ref, u_ref, dg_ref, du_ref):
        da = _dot_nt(dx_ref[...], w_ref[...])
        gv = g_ref[...]
        sg = _sigmoid(gv)
        silu = gv * sg
        dg_ref[...] = ((da * u_ref[...]) * (sg * (1.0 + gv * (1.0 - sg)))).astype(BF16)
        du_ref[...] = (da * silu).astype(BF16)

    blk = pl.BlockSpec((None, tm, N_FG), lambda p, m: (p, m, 0))
    return pl.pallas_call(
        body, name="ffn_down_bwd", grid=(NFG, S // tm),
        in_specs=[pl.BlockSpec((tm, D), lambda p, m: (m, 0)),
                  pl.BlockSpec((None, N_FG, D), lambda p, m: (p, 0, 0)), blk, blk],
        out_specs=[blk, blk],
        out_shape=[jax.ShapeDtypeStruct((NFG, S, N_FG),BF16), jax.ShapeDtypeStruct((NFG, S, N_FG),BF16)],
        compiler_params=_cp(("parallel", "parallel")),
    )(dx3b, wd, g, u)


def _ffn_up_bwd(dg, du, wg, wu, dres, xs, r, nw):
    tm = 512

    def body(dg_ref, du_ref, wg_ref, wu_ref, dres_hbm, x_hbm, r_ref, nw_ref, dx_ref, dxb_ref, st_ref,
             dres_buf, x_buf, sems):
        m, p = pl.program_id(0), pl.program_id(1)
        tail_in = _row_copies((dres_hbm, x_hbm), (dres_buf, x_buf), sems, m, tm)

        @pl.when(p == 0)
        def _():
            dx_ref[...] = jnp.zeros_like(dx_ref)
            for cp in tail_in:
                cp.start()

        @pl.when((p == 0) & (m == 0))
        def _():
            st_ref[...] = jnp.zeros_like(st_ref)

        dx_ref[...] += _dot(dg_ref[...], wg_ref[...])
        dx_ref[...] += _dot(du_ref[...], wu_ref[...])

        @pl.when(p == NFG - 1)
        def _():
            for cp in tail_in:
                cp.wait()
            dx, dnw = _rms_bwd_tile(dx_ref[...], x_buf[...], r_ref[...], nw_ref[...])
            dx = dres_buf[...] + dx
            dx_ref[...] = dx
            dxb_ref[...] = dx.astype(BF16)
            st_ref[0:1, :] += dnw

    blk = pl.BlockSpec((None, tm, N_FG), lambda m, p: (p, m, 0))
    wblk = pl.BlockSpec((None, N_FG, D), lambda m, p: (p, 0, 0))
    row = pl.BlockSpec((tm, D), lambda m, p: (m, 0))
    hbm = pl.BlockSpec(memory_space=pl.ANY)
    return pl.pallas_call(
        body, name="ffn_up_bwd", grid=(S // tm, NFG),
        in_specs=[blk, blk, wblk, wblk, hbm, hbm, pl.BlockSpec((tm, 1), lambda m, p: (m, 0)),
                  pl.BlockSpec((1, D), lambda m, p: (0, 0))],
        out_specs=[row, row, pl.BlockSpec((8, D), lambda m, p: (0, 0))],
        out_shape=[jax.ShapeDtypeStruct((S, D), F32), jax.ShapeDtypeStruct((S, D), BF16),
                   jax.ShapeDtypeStruct((8, D), F32)],
        scratch_shapes=[pltpu.VMEM((tm, D), F32), pltpu.VMEM((tm, D), F32), pltpu.SemaphoreType.DMA((2,))],
        compiler_params=_cp(("arbitrary", "arbitrary")),
    )(dg, du, wg, wu, dres, xs, r, nw)


def _out_proj_bwd(dx2b, wout):
    tm = 256

    def body(dx_ref, w_ref, o_ref):
        o_ref[...] = _dot_nt(dx_ref[...], w_ref[...])

    return pl.pallas_call(
        body, name="out_proj_bwd", grid=(S // tm,),
        in_specs=[pl.BlockSpec((tm, D), lambda i: (i, 0)), pl.BlockSpec((D, D), lambda i: (0, 0))],
        out_specs=pl.BlockSpec((tm, D), lambda i: (i, 0)),
        out_shape=jax.ShapeDtypeStruct((S, D), F32),
        compiler_params=_cp(("parallel",)),
    )(dx2b, wout)


def _in_proj_bwd(dproj, win, dres, xs, r, nw):
    tm = 1024

    def body(dp_ref, w_ref, dres_hbm, x_hbm, r_ref, nw_ref, dx_ref, st_ref, dres_buf, x_buf, sems):
        m, p = pl.program_id(0), pl.program_id(1)
        tail_in = _row_copies((dres_hbm, x_hbm), (dres_buf, x_buf), sems, m, tm)

        @pl.when(p == 0)
        def _():
            dx_ref[...] = jnp.zeros_like(dx_ref)
            for cp in tail_in:
                cp.start()

        @pl.when((p == 0) & (m == 0))
        def _():
            st_ref[...] = jnp.zeros_like(st_ref)

        dx_ref[...] += _dot_nt(dp_ref[...], w_ref[...])

        @pl.when(p == NDEV - 1)
        def _():
            for cp in tail_in:
                cp.wait()
            dx, dnw = _rms_bwd_tile(dx_ref[...], x_buf[...], r_ref[...], nw_ref[...])
            dx_ref[...] = dres_buf[...] + dx
            st_ref[0:1, :] += dnw

    row = pl.BlockSpec((tm, D), lambda m, p: (m, 0))
    hbm = pl.BlockSpec(memory_space=pl.ANY)
    return pl.pallas_call(
        body, name="in_proj_bwd", grid=(S // tm, NDEV),
        in_specs=[pl.BlockSpec((tm, N_IN), lambda m, p: (m, p)),
                  pl.BlockSpec((None, D, N_IN), lambda m, p: (p, 0, 0)),
                  hbm, hbm, pl.BlockSpec((tm, 1), lambda m, p: (m, 0)),
                  pl.BlockSpec((1, D), lambda m, p: (0, 0))],
        out_specs=[row, pl.BlockSpec((8, D), lambda m, p: (0, 0))],
        out_shape=[jax.ShapeDtypeStruct((S, D), F32), jax.ShapeDtypeStruct((8, D), F32)],
        scratch_shapes=[pltpu.VMEM((tm, D), F32), pltpu.VMEM((tm, D), F32), pltpu.SemaphoreType.DMA((2,))],
        compiler_params=_cp(("arbitrary", "arbitrary")),
    )(dproj, win, dres, xs, r, nw)


W_IN_PARTS = 2


def _wgrad_in(h1, dproj, part):
    rows = D // W_IN_PARTS

    def body(a_ref, d_ref, o_ref):
        o_ref[...] = _dot_tn(a_ref[...], d_ref[...]).astype(BF16)

    return pl.pallas_call(
        body, name=f"wgrad_in_{part}", grid=(NDEV,),
        in_specs=[pl.BlockSpec((S, rows), lambda p: (0, part)), pl.BlockSpec((S, N_IN), lambda p: (0, p))],
        out_specs=pl.BlockSpec((None, rows, N_IN), lambda p: (p, 0, 0)),
        out_shape=jax.ShapeDtypeStruct((NDEV, rows, N_IN), BF16),
        compiler_params=_cp(("parallel",)),
    )(h1, dproj)


def _wgrad_rows(a3, dy, name):
    def body(a_ref, d_ref, o_ref):
        o_ref[...] = _dot_tn(a_ref[...], d_ref[...]).astype(BF16)

    return pl.pallas_call(
        body, name=name, grid=(NFG,),
        in_specs=[pl.BlockSpec((None, S, N_FG), lambda p: (p, 0, 0)), pl.BlockSpec((S, D), lambda p: (0, 0))],
        out_specs=pl.BlockSpec((None, N_FG, D), lambda p: (p, 0, 0)),
        out_shape=jax.ShapeDtypeStruct((NFG, N_FG, D), BF16),
        compiler_params=_cp(("parallel",)),
    )(a3, dy).reshape(NDEV, N_FF, D)


def _wgrad_out(ma, mr, dx2b):
    half = D // 2
    per = half // N_OUT

    def body(ma_ref, mr_ref, d_ref, o_ref):
        p = pl.program_id(0)

        @pl.when(p < per)
        def _():
            o_ref[...] = _dot_tn(ma_ref[...], d_ref[...]).astype(BF16)

        @pl.when(p >= per)
        def _():
            o_ref[...] = _dot_tn(mr_ref[...], d_ref[...]).astype(BF16)

    return pl.pallas_call(
        body, name="wgrad_out", grid=(NDEV,),
        in_specs=[pl.BlockSpec((S, N_OUT), lambda p: (0, jnp.minimum(p, per - 1))),
                  pl.BlockSpec((S, N_OUT), lambda p: (0, jnp.maximum(p - per, 0))),
                  pl.BlockSpec((S, D), lambda p: (0, 0))],
        out_specs=pl.BlockSpec((None, N_OUT, D), lambda p: (p, 0, 0)),
        out_shape=jax.ShapeDtypeStruct((NDEV, N_OUT, D), BF16),
        compiler_params=_cp(("parallel",)),
    )(ma, mr, dx2b)


def _attn_consts():
    c = np.zeros((AH, 8, AHD), np.float32)
    for h in range(AH):
        c[h, :, :] = 2.0 ** (-(h + 1))
    return jnp.asarray(c)


def _permute_in(dst, src, d, cast=None):
    ln = S // d
    for rr in range(d):
        v = src[pl.ds(rr, ln, stride=d), :] if d > 1 else src[...]
        dst[rr * ln:(rr + 1) * ln, :] = v if cast is None else v.astype(cast)


def _attn_masks():
    qi = lax.broadcasted_iota(jnp.int32, (CH, CH), 0)
    kj = lax.broadcasted_iota(jnp.int32, (CH, CH), 1)
    dist_c = (qi - kj).astype(F32)
    dist_p = (qi - kj + CH).astype(F32)
    return (qi >= kj)[None], (kj >= qi)[None], dist_c[None], dist_p[None]


GB = 8


def _bdot_nt(a, b):
    return lax.dot_general(a, b, (((2,), (2,)), ((0,), (0,))), preferred_element_type=F32)


def _bdot(a, b):
    return lax.dot_general(a, b, (((2,), (1,)), ((0,), (0,))), preferred_element_type=F32)


def _bdot_tn(a, b):
    return lax.dot_general(a, b, (((1,), (1,)), ((0,), (0,))), preferred_element_type=F32)


def _shift_block(dst, src):
    dst[0:CH, :] = jnp.zeros((CH, AHD), dst.dtype)
    dst[CH:S, :] = src[0:S - CH, :]


def _has_prev(g, nb):
    blk = lax.broadcasted_iota(jnp.int32, (GB, 1, 1), 0) + g * GB
    return (blk & (nb - 1)) != 0


def _blocks(ref, g):
    return ref[g * GB * CH:(g + 1) * GB * CH, :].reshape(GB, CH, AHD)


def _attn_fwd(proj):
    scale = 1.0 / math.sqrt(AHD)

    def body(c_ref, q_ref, k_ref, v_ref, o_ref, ob_ref, lse_ref, qd, kd, vd, kps, vps, od, ld, *nat):
        onat, lnat = nat[0:3], nat[3:6]
        slope = c_ref[0:1, :]
        mask_c, mask_p, dist_c, dist_p = _attn_masks()
        for pi, (d, nb) in enumerate(PATTERNS):
            _permute_in(qd, q_ref, d, BF16)
            _permute_in(kd, k_ref, d, BF16)
            _permute_in(vd, v_ref, d, BF16)
            if nb > 1:
                _shift_block(kps, kd)
                _shift_block(vps, vd)
            bias_c = -(slope * float(d)) * dist_c
            bias_p = -(slope * float(d)) * dist_p
            for g in range(NB // GB):
                q3, k3, v3 = _blocks(qd, g), _blocks(kd, g), _blocks(vd, g)
                s_c = jnp.where(mask_c, _bdot_nt(q3, k3) * scale + bias_c, NEG)
                mx = jnp.max(s_c, axis=-1, keepdims=True)
                if nb > 1:
                    kp3, vp3 = _blocks(kps, g), _blocks(vps, g)
                    s_p = jnp.where(jnp.logical_and(mask_p, _has_prev(g, nb)),
                                    _bdot_nt(q3, kp3) * scale + bias_p, NEG)
                    mx = jnp.maximum(mx, jnp.max(s_p, axis=-1, keepdims=True))
                    l = (jnp.sum(jnp.exp(s_c - mx), axis=-1, keepdims=True)
                         + jnp.sum(jnp.exp(s_p - mx), axis=-1, keepdims=True))
                    lse = mx + jnp.log(l)
                    o3 = _bdot(jnp.exp(s_c - lse).astype(BF16), v3) + _bdot(jnp.exp(s_p - lse).astype(BF16), vp3)
                else:
                    l = jnp.sum(jnp.exp(s_c - mx), axis=-1, keepdims=True)
                    lse = mx + jnp.log(l)
                    o3 = _bdot(jnp.exp(s_c - lse).astype(BF16), v3)
                rows = slice(g * GB * CH, (g + 1) * GB * CH)
                od[rows, :] = o3.reshape(GB * CH, AHD)
                ld[rows, :] = jnp.broadcast_to(lse, (GB, CH, AHD)).reshape(GB * CH, AHD)
            ln = S // d
            for rr in range(d):
                if d > 1:
                    onat[pi][pl.ds(rr, ln, stride=d), :] = od[rr * ln:(rr + 1) * ln, :]
                    lnat[pi][pl.ds(rr, ln, stride=d), :] = ld[rr * ln:(rr + 1) * ln, :]
                else:
                    onat[pi][...] = od[...]
                    lnat[pi][...] = ld[...]
        l0, l1, l2 = lnat[0][...], lnat[1][...], lnat[2][...]
        mx = jnp.maximum(jnp.maximum(l0, l1), l2)
        e0, e1, e2 = jnp.exp(l0 - mx), jnp.exp(l1 - mx), jnp.exp(l2 - mx)
        den = e0 + e1 + e2
        out = (e0 / den) * onat[0][...] + (e1 / den) * onat[1][...] + (e2 / den) * onat[2][...]
        o_ref[...] = out
        ob_ref[...] = out.astype(BF16)
        lse_ref[...] = mx + jnp.log(den)

    def col(off):
        return pl.BlockSpec((S, AHD), lambda h: (0, off + h))

    return pl.pallas_call(
        body, name="attn_fwd", grid=(AH,),
        in_specs=[pl.BlockSpec((None, 8, AHD), lambda h: (h, 0, 0)), col(0), col(AH), col(2 * AH)],
        out_specs=[col(0), col(0), col(0)],
        out_shape=[jax.ShapeDtypeStruct((S, AH * AHD), F32), jax.ShapeDtypeStruct((S, AH * AHD), BF16),
                   jax.ShapeDtypeStruct((S, AH * AHD), F32)],
        scratch_shapes=[pltpu.VMEM((S, AHD), BF16) for _ in range(5)]
        + [pltpu.VMEM((S, AHD), F32) for _ in range(8)],
        compiler_params=_cp(("parallel",)),
    )(_attn_consts(), proj, proj, proj)


def _attn_bwd(proj, dmixed, o, lse):
    scale = 1.0 / math.sqrt(AHD)

    def body(c_ref, q_ref, k_ref, v_ref, do_ref, o_ref, lse_ref, dq_ref, dk_ref, dv_ref,
             qd, kd, vd, dod, kps, vps, lsd, dld, dqd, dkd, dvd, delta, aq, ak, av):
        slope = c_ref[0:1, :]
        mask_c, mask_p, dist_c, dist_p = _attn_masks()
        delta[...] = jnp.broadcast_to(jnp.sum(do_ref[...] * o_ref[...], axis=-1, keepdims=True), (S, AHD))
        for pi, (d, nb) in enumerate(PATTERNS):
            _permute_in(qd, q_ref, d, BF16)
            _permute_in(kd, k_ref, d, BF16)
            _permute_in(vd, v_ref, d, BF16)
            _permute_in(dod, do_ref, d, BF16)
            _permute_in(lsd, lse_ref, d)
            _permute_in(dld, delta, d)
            if nb > 1:
                _shift_block(kps, kd)
                _shift_block(vps, vd)
            bias_c = -(slope * float(d)) * dist_c
            bias_p = -(slope * float(d)) * dist_p
            for g in range(NB // GB):
                q3, k3, v3, do3 = _blocks(qd, g), _blocks(kd, g), _blocks(vd, g), _blocks(dod, g)
                ls, dl = _blocks(lsd, g), _blocks(dld, g)
                lo, hi = g * GB * CH, (g + 1) * GB * CH
                p_c = jnp.exp(jnp.where(mask_c, _bdot_nt(q3, k3) * scale + bias_c, NEG) - ls)
                ds_c = ((p_c * (_bdot_nt(do3, v3) - dl)) * scale).astype(BF16)
                dq3 = _bdot(ds_c, k3)
                dkd[lo:hi, :] = _bdot_tn(ds_c, q3).reshape(GB * CH, AHD)
                dvd[lo:hi, :] = _bdot_tn(p_c.astype(BF16), do3).reshape(GB * CH, AHD)
                if nb > 1:
                    kp3, vp3 = _blocks(kps, g), _blocks(vps, g)
                    p_p = jnp.exp(jnp.where(jnp.logical_and(mask_p, _has_prev(g, nb)),
                                            _bdot_nt(q3, kp3) * scale + bias_p, NEG) - ls)
                    ds_p = ((p_p * (_bdot_nt(do3, vp3) - dl)) * scale).astype(BF16)
                    dq3 = dq3 + _bdot(ds_p, kp3)
                    dkp = _bdot_tn(ds_p, q3).reshape(GB * CH, AHD)
                    dvp = _bdot_tn(p_p.astype(BF16), do3).reshape(GB * CH, AHD)
                    if g == 0:
                        dkd[0:hi - CH, :] += dkp[CH:, :]
                        dvd[0:hi - CH, :] += dvp[CH:, :]
                    else:
                        dkd[lo - CH:hi - CH, :] += dkp
                        dvd[lo - CH:hi - CH, :] += dvp
                dqd[lo:hi, :] = dq3.reshape(GB * CH, AHD)
            ln = S // d
            for acc, src in ((aq, dqd), (ak, dkd), (av, dvd)):
                if pi == 0:
                    acc[...] = src[...]
                else:
                    for rr in range(d):
                        acc[pl.ds(rr, ln, stride=d), :] += src[rr * ln:(rr + 1) * ln, :]
        dq_ref[...] = aq[...].astype(BF16)
        dk_ref[...] = ak[...].astype(BF16)
        dv_ref[...] = av[...].astype(BF16)

    def col(off):
        return pl.BlockSpec((S, AHD), lambda h: (0, off + h))

    return pl.pallas_call(
        body, name="attn_bwd", grid=(AH,),
        in_specs=[pl.BlockSpec((None, 8, AHD), lambda h: (h, 0, 0)), col(0), col(AH), col(2 * AH),
                  col(0), col(0), col(0)],
        out_specs=[col(0), col(0), col(0)],
        out_shape=[jax.ShapeDtypeStruct((S, AH * AHD), BF16)] * 3,
        scratch_shapes=[pltpu.VMEM((S, AHD), BF16) for _ in range(6)]
        + [pltpu.VMEM((S, AHD), F32) for _ in range(9)],
        compiler_params=_cp(("parallel",)),
    )(_attn_consts(), proj, proj, proj, dmixed, o, lse)


def _ret_consts():
    c = np.zeros((RH, 8, RHD), np.float32)
    for h in range(RH):
        c[h, :, :] = np.log(np.float32(1.0) - np.float32(2.0 ** (-5.0 - h)))
    return jnp.asarray(c)


def _ret_factors(lg):
    i = lax.broadcasted_iota(jnp.int32, (CH, CH), 0)
    j = lax.broadcasted_iota(jnp.int32, (CH, CH), 1)
    dif = (i - j).astype(F32)
    decay = jnp.where(dif >= 0, jnp.exp(lg[:, 0:CH] * jnp.maximum(dif, 0.0)), 0.0)
    row = lax.broadcasted_iota(jnp.int32, (CH, RHD), 0).astype(F32)
    zeta = jnp.exp(lg * (CH - 1.0 - row))
    xi = jnp.exp(lg * (row + 1.0))
    return decay, zeta, xi, jnp.exp(lg * float(CH))


CBK = 8
RSTEPS = NB // CBK


def _ret_specs(rev):
    off = 3 * AH * AHD // RHD
    rows = CBK * CH

    def ch(n):
        return (RSTEPS - 1 - n) if rev else n

    def col(k):
        return pl.BlockSpec((rows, RHD), lambda h, n: (ch(n), off + k * RH + h))

    own = pl.BlockSpec((rows, RHD), lambda h, n: (ch(n), h))
    state = pl.BlockSpec((None, CBK, RHD, RHD), lambda h, n: (h, ch(n), 0, 0))
    const = pl.BlockSpec((None, 8, RHD), lambda h, n: (h, 0, 0))
    dm = pl.BlockSpec((rows, RHD), lambda h, n: (ch(n), AH * AHD // RHD + h))
    return col, own, state, const, dm


def _chunks(x):
    return x.reshape(CBK, CH, RHD)


def _ret_fwd(proj):
    def body(c_ref, q_ref, k_ref, v_ref, g_ref, ret_ref, mr_ref, st_ref, r_acc):
        n = pl.program_id(1)

        @pl.when(n == 0)
        def _():
            r_acc[...] = jnp.zeros_like(r_acc)

        decay, zeta, xi, gch = _ret_factors(c_ref[0:1, :])
        q3 = _chunks(q_ref[...].astype(BF16))
        kc = _chunks(k_ref[...] * (1.0 / math.sqrt(RHD)))
        k3 = kc.astype(BF16)
        v3 = _chunks(v_ref[...].astype(BF16))
        kv3 = _bdot_tn((kc * zeta[None]).astype(BF16), v3)
        r = r_acc[...]
        for i in range(CBK):
            st_ref[i] = r.astype(BF16)
            r = r * gch + kv3[i]
        r_acc[...] = r
        scores = _bdot_nt(q3, k3) * decay[None]
        ret = (_bdot(scores.astype(BF16), v3) + _bdot(q3, st_ref[...]) * xi[None]).reshape(CBK * CH, RHD)
        ret_ref[...] = ret
        rr = lax.rsqrt(jnp.mean(ret * ret, axis=-1, keepdims=True) + EPS)
        gv = g_ref[...]
        mr_ref[...] = ((gv * _sigmoid(gv)) * (ret * rr)).astype(BF16)

    col, own, state, const, _ = _ret_specs(False)
    return pl.pallas_call(
        body, name="ret_fwd", grid=(RH, RSTEPS),
        in_specs=[const, col(0), col(1), col(2), col(3)],
        out_specs=[own, own, state],
        out_shape=[jax.ShapeDtypeStruct((S, RH * RHD), F32), jax.ShapeDtypeStruct((S, RH * RHD), BF16),
                   jax.ShapeDtypeStruct((RH, NB, RHD, RHD), BF16)],
        scratch_shapes=[pltpu.VMEM((RHD, RHD), F32)],
        compiler_params=_cp(("parallel", "arbitrary")),
    )(_ret_consts(), proj, proj, proj, proj)


def _ret_bwd(proj, ret, states, dmixed):
    def body(c_ref, q_ref, k_ref, v_ref, g_ref, ret_ref, st_ref, dm_ref, dq_ref, dk_ref, dv_ref, dg_ref, g_acc, gs):
        n = pl.program_id(1)

        @pl.when(n == 0)
        def _():
            g_acc[...] = jnp.zeros_like(g_acc)

        decay, zeta, xi, gch = _ret_factors(c_ref[0:1, :])
        ret_v = ret_ref[...]
        rr = lax.rsqrt(jnp.mean(ret_v * ret_v, axis=-1, keepdims=True) + EPS)
        gv = g_ref[...]
        sg = _sigmoid(gv)
        dmix = dm_ref[...]
        dg_ref[...] = ((dmix * (ret_v * rr)) * (sg * (1.0 + gv * (1.0 - sg)))).astype(BF16)
        dretn = dmix * (gv * sg)
        dret = _chunks(rr * dretn - ret_v * ((rr * rr * rr) * jnp.mean(dretn * ret_v, axis=-1, keepdims=True)))

        q3 = _chunks(q_ref[...].astype(BF16))
        kc = _chunks(k_ref[...] * (1.0 / math.sqrt(RHD)))
        k3 = kc.astype(BF16)
        v3 = _chunks(v_ref[...].astype(BF16))
        d3 = dret.astype(BF16)
        dxi = (dret * xi[None]).astype(BF16)
        kz = (kc * zeta[None]).astype(BF16)
        dr3 = _bdot_tn(q3, dxi)
        acc = g_acc[...]
        for i in reversed(range(CBK)):
            gs[i] = acc.astype(BF16)
            acc = dr3[i] + gch * acc
        g_acc[...] = acc
        g3 = gs[...]
        sc = (_bdot_nt(q3, k3) * decay[None]).astype(BF16)
        da = (_bdot_nt(d3, v3) * decay[None]).astype(BF16)
        dq = _bdot(da, k3) + _bdot_nt(dxi, st_ref[...])
        dkc = _bdot_tn(da, q3) + _bdot_nt(v3, g3) * zeta[None]
        dv = _bdot_tn(sc, d3) + _bdot(kz, g3)
        dq_ref[...] = dq.reshape(CBK * CH, RHD).astype(BF16)
        dk_ref[...] = (dkc * (1.0 / math.sqrt(RHD))).reshape(CBK * CH, RHD).astype(BF16)
        dv_ref[...] = dv.reshape(CBK * CH, RHD).astype(BF16)

    col, own, state, const, dm = _ret_specs(True)
    return pl.pallas_call(
        body, name="ret_bwd", grid=(RH, RSTEPS),
        in_specs=[const, col(0), col(1), col(2), col(3), own, state, dm],
        out_specs=[own, own, own, own],
        out_shape=[jax.ShapeDtypeStruct((S, RH * RHD), BF16)] * 4,
        scratch_shapes=[pltpu.VMEM((RHD, RHD), F32), pltpu.VMEM((CBK, RHD, RHD), BF16)],
        compiler_params=_cp(("parallel", "arbitrary")),
    )(_ret_consts(), proj, proj, proj, proj, ret, states, dmixed)


class _NoReduction:
    def start(self, group, grads):
        pass

    def local(self, name, first=()):
        return []

    def landed(self, name):
        return []

    def update(self, name):
        return []


def _local_step(x, tgt, nw1, nw2, nw3, win, wout, wg, wu, wd, red=None):
    red = red or _NoReduction()

    def after(values, first):
        return lax.optimization_barrier((tuple(values), tuple(first)))[0]

    wg, wu, wd = (w.reshape(NFG, N_FG, D) for w in (wg, wu, wd))
    h1, r1 = _rms_fwd(x, nw1)
    proj = _proj(h1, win)
    o, ma, lse = _attn_fwd(proj)
    ret, mr, states = _ret_fwd(proj)
    x2, h2, r2 = _out_proj_rms(x, ma, mr, wout, nw2)
    g, u, a = _ffn_up(h2, wg, wu)
    dx3, dx3b, st3 = _ffn_down_loss(x2, a, wd, nw3, tgt)

    dwd = _wgrad_rows(a, dx3b, "wgrad_down")
    red.start(["w_down"], [dwd])
    (dx3b,) = after([dx3b], [dwd])
    dg, du = _ffn_down_bwd(dx3b, wd, g, u)
    dg, du = after([dg, du], red.local("w_down", first=[dg]))
    dwg = _wgrad_rows(dg, h2, "wgrad_gate")
    red.start(["w_gate"], [dwg])
    (du,) = after([du], [dwg])
    dwu = _wgrad_rows(du, h2, "wgrad_up")
    red.start(["w_up"], [dwu])
    dg, du = after([dg, du], [dwu] + red.local("w_gate"))
    dx2, dx2b, st2 = _ffn_up_bwd(dg, du, wg, wu, dx3, x2, r2, nw2)
    (dx2b,) = after([dx2b], red.local("w_up", first=[dx2b] + red.landed("w_down")))
    dwo = _wgrad_out(ma, mr, dx2b)
    red.start(["w_out"], [dwo])
    (dx2b,) = after([dx2b], [dwo])
    dmixed = _out_proj_bwd(dx2b, wout)
    dqa, dka, dva = _attn_bwd(proj, dmixed, o, lse)
    (dmixed,) = after([dmixed], red.local("w_out", first=[dqa] + red.landed("w_gate")))
    dqr, dkr, dvr, dgr = _ret_bwd(proj, ret, states, dmixed)
    dproj = jnp.concatenate([dqa, dka, dva, dqr, dkr, dvr, dgr], axis=1)
    (dwi0,) = after([_wgrad_in(h1, dproj, 0)], red.landed("w_up"))
    red.start(["w_in_0"], [dwi0])
    (dproj,) = after([dproj], [dwi0])
    dwi1 = _wgrad_in(h1, dproj, 1)
    red.start(["w_in_1"], [dwi1])
    sums = red.local("w_in_0", first=[dwi1] + red.landed("w_out"))
    sums = red.local("w_in_1", first=sums + red.update("w_down"))
    (dproj,) = after([dproj], sums)
    gx, st1 = _in_proj_bwd(dproj, win, dx2, x, r1, nw1)
    dwi = jnp.concatenate([dwi0, dwi1], axis=1)
    stats = jnp.concatenate([st1[0:1], st2[0:1], st3[0:2], jnp.zeros((4, D), F32)], axis=0)
    return stats, gx, dwi, dwo, dwg, dwu, dwd


def _place():
    x, y, c = lax.axis_index("x"), lax.axis_index("y"), lax.axis_index("c")
    return x, y, c, [(1 - x, y), (x, 1 - y), (1 - x, 1 - y)]


def _handshake(peers):
    barrier = pltpu.get_barrier_semaphore()
    for peer in peers:
        pl.semaphore_signal(barrier, inc=1, device_id=peer, device_id_type=MESH)
    pl.semaphore_wait(barrier, len(peers))


def _all_gather(shards, name, collective_id):
    na = len(shards)
    SIB, XN0, XN1, YN1, YN0, VIA_X, VIA_Y = 0, 1, 2, 3, 4, 5, 6
    D2D = {XN0: 7, XN1: 8, YN1: 9, YN0: 10, VIA_X: 11, VIA_Y: 12}

    def body(*refs):
        ins, outs = refs[:na], refs[na:2 * na]
        send_sems, recv_sems, local_sems = refs[2 * na:]
        x, y, c, _ = _place()
        me, sib = (x, y, c), (x, y, 1 - c)
        xn, yn, dg = (1 - x, y, c), (x, 1 - y, c), (1 - x, 1 - y, c)
        _handshake([sib, xn, yn])

        def part(ref, h):
            rows = ref.shape[0] // 2
            return ref if h is None else ref.at[pl.ds(h * rows, rows)]

        def block(a, owner, h):
            return part(outs[a].at[4 * owner[0] + 2 * owner[1] + owner[2]], h)

        def copy(a, k, owner, h, to, own_src=False):
            return pltpu.make_async_remote_copy(
                src_ref=part(ins[a], h) if own_src else block(a, owner, h), dst_ref=block(a, owner, h),
                send_sem=send_sems.at[a, k], recv_sem=recv_sems.at[a, k], device_id=to, device_id_type=MESH)

        def other(p):
            return (p[0], p[1], 1 - c)

        mine = [pltpu.make_async_copy(ins[a], block(a, me, None), local_sems.at[a]) for a in range(na)]
        for cp in mine:
            cp.start()
        sent = []
        for a in range(na):
            sent += [copy(a, XN0, me, 0, xn, True), copy(a, YN1, me, 1, yn, True),
                     copy(a, XN1, me, 1, xn, True), copy(a, YN0, me, 0, yn, True)]
        sent += [copy(a, SIB, me, None, sib, True) for a in range(na)]
        for cp in sent:
            cp.start()

        def landed(a, k, owner, h, then):
            copy(a, k, owner, h, me).wait_recv()
            for k2, to in then + [(D2D[k], sib)]:
                cp = copy(a, k2, owner, h, to)
                cp.start()
                sent.append(cp)

        for a in range(na):
            landed(a, XN0, xn, 0, [(VIA_Y, yn)])
            landed(a, YN1, yn, 1, [(VIA_X, xn)])
            landed(a, XN1, xn, 1, [])
            landed(a, YN0, yn, 0, [])
        for a in range(na):
            landed(a, VIA_Y, dg, 0, [])
            landed(a, VIA_X, dg, 1, [])
        for a in range(na):
            copy(a, SIB, sib, None, me).wait_recv()
            for k, owner, h in ((XN0, xn, 0), (XN1, xn, 1), (YN1, yn, 1), (YN0, yn, 0), (VIA_Y, dg, 0), (VIA_X, dg, 1)):
                copy(a, D2D[k], other(owner), h, me).wait_recv()
        for cp in sent:
            cp.wait_send()
        for cp in mine:
            cp.wait()

    return _sequencer_call(
        body, name, collective_id,
        [jax.ShapeDtypeStruct((NDEV,) + s.shape, s.dtype) for s in shards],
        [pltpu.SemaphoreType.DMA((na, 13)), pltpu.SemaphoreType.DMA((na, 13)), pltpu.SemaphoreType.DMA((na,))])(*shards)


def _sequencer_call(body, name, collective_id, out_type, scratch_types):
    return pl.kernel(
        body, name=name, out_type=out_type,
        mesh=plsc.ScalarSubcoreMesh(axis_name="sequencer", num_cores=1),
        scratch_types=scratch_types,
        compiler_params=pltpu.CompilerParams(collective_id=collective_id))


def _exchange_sibling(grads, name, collective_id):
    na = len(grads)

    def body(*refs):
        ins, outs = refs[:na], refs[na:2 * na]
        send_sems, recv_sems = refs[2 * na:]
        x, y, c, _ = _place()
        _handshake([(x, y, 1 - c)])
        cps = []
        for a in range(na):
            for k in range(4):
                cps.append(pltpu.make_async_remote_copy(
                    src_ref=ins[a].at[2 * k + (1 - c)], dst_ref=outs[a].at[k],
                    send_sem=send_sems.at[a, k], recv_sem=recv_sems.at[a, k],
                    device_id=(x, y, 1 - c), device_id_type=MESH))
        for cp in cps:
            cp.start()
        for cp in cps:
            cp.wait()

    return _sequencer_call(
        body, name, collective_id,
        [jax.ShapeDtypeStruct((4,) + g.shape[1:], g.dtype) for g in grads],
        [pltpu.SemaphoreType.DMA((na, 4)), pltpu.SemaphoreType.DMA((na, 4))])(*grads)


def _row_tile(rows, cols):
    for t in (512, 256, 176, 128, 64, 32, 16):
        if rows % t == 0 and t * cols * 4 <= (2 << 20):
            return t
    raise ValueError((rows, cols))


def _chip_sum(place, g, got, name):
    _, r, c = g.shape
    tm = r

    def body(pos_ref, g_ref, got_ref, o_ref):
        o_ref[...] = (g_ref[...].astype(F32) + got_ref[...].astype(F32)).astype(BF16)

    def chip(j, pos):
        return 2 * (pos[0] ^ jnp.where(j == 1, 0, 1)) + (pos[1] ^ jnp.where(j == 0, 0, 1))

    return pl.pallas_call(
        body, name=name,
        grid_spec=pltpu.PrefetchScalarGridSpec(
            num_scalar_prefetch=1, grid=(3, r // tm),
            in_specs=[pl.BlockSpec((None, tm, c), lambda j, i, pos: (2 * chip(j, pos) + pos[2], i, 0)),
                      pl.BlockSpec((None, tm, c), lambda j, i, pos: (chip(j, pos), i, 0))],
            out_specs=pl.BlockSpec((None, tm, c), lambda j, i, pos: (j, i, 0))),
        out_shape=jax.ShapeDtypeStruct((3, r, c), BF16),
        compiler_params=_cp(("parallel", "parallel")),
    )(place, g, got)


def _exchange_chips(sums, name, collective_id):
    na = len(sums)

    def body(*refs):
        ins, outs = refs[:na], refs[na:2 * na]
        send_sems, recv_sems = refs[2 * na:]
        x, y, c, chips = _place()
        _handshake([(*chip, c) for chip in chips])
        cps = []
        for a in range(na):
            for j, chip in enumerate(chips):
                cps.append(pltpu.make_async_remote_copy(
                    src_ref=ins[a].at[j], dst_ref=outs[a].at[j],
                    send_sem=send_sems.at[a, j], recv_sem=recv_sems.at[a, j],
                    device_id=(*chip, c), device_id_type=MESH))
        for cp in cps:
            cp.start()
        for cp in cps:
            cp.wait()

    return _sequencer_call(
        body, name, collective_id,
        [jax.ShapeDtypeStruct((3,) + s.shape[1:], s.dtype) for s in sums],
        [pltpu.SemaphoreType.DMA((na, 3)), pltpu.SemaphoreType.DMA((na, 3))])(*sums)


def _exchange_stats(stats, collective_id):
    def body(st_in, st_out, st_send, st_recv, local_sem):
        x, y, c, _ = _place()
        me_idx = 4 * x + 2 * y + c
        peers = [(x ^ ((k >> 2) & 1), y ^ ((k >> 1) & 1), c ^ (k & 1)) for k in range(1, 8)]
        _handshake(peers)
        mine = pltpu.make_async_copy(st_in, st_out.at[me_idx], local_sem)
        mine.start()
        cps = [pltpu.make_async_remote_copy(
            src_ref=st_in, dst_ref=st_out.at[me_idx], send_sem=st_send.at[k], recv_sem=st_recv.at[k],
            device_id=peer, device_id_type=MESH) for k, peer in enumerate(peers)]
        for cp in cps:
            cp.start()
        for cp in cps:
            cp.wait()
        mine.wait()

    return _sequencer_call(
        body, "exchange_stats", collective_id,
        jax.ShapeDtypeStruct((NDEV,) + stats.shape, stats.dtype),
        [pltpu.SemaphoreType.DMA((7,)), pltpu.SemaphoreType.DMA((7,)), pltpu.SemaphoreType.DMA])(stats)


class _Reduction:
    def __init__(self, place, first_collective_id, state):
        self.place = place
        self.ids = iter(range(first_collective_id, 32))
        self.state = state
        self.groups = {}
        self.updates = {}

    def next_id(self):
        return next(self.ids)

    def start(self, group, grads):
        got = _exchange_sibling(grads, "sibling_exchange_" + group[0], self.next_id())
        self.groups[group[0]] = dict(names=group, grads=grads, got=got)

    def local(self, name, first=()):
        grp = self.groups[name]
        grads = lax.optimization_barrier((tuple(grp["grads"]), tuple(first)))[0]
        grp["sums"] = [_chip_sum(self.place, g, s, "chip_sum_" + n)
                       for g, s, n in zip(grads, grp["got"], grp["names"])]
        grp["chips"] = _exchange_chips(grp["sums"], "chip_exchange_" + name, self.next_id())
        return grp["sums"]

    def landed(self, name):
        return list(self.groups[name]["chips"])

    def update(self, name):
        if name not in self.updates:
            grp = next(g for g in self.groups.values() if name in g["names"])
            k = grp["names"].index(name)
            w, m, v, part, parts = self.state[name]
            before = self.update(f"{name[:-1]}{part - 1}") if part else None
            self.updates[name] = _shard_update(self.place, w, m, v, grp["grads"][k], grp["got"][k],
                                               grp["chips"][k], "update_" + name, part, parts, before)
        return list(self.updates[name])


def _adamw(w, g, m, v):
    m = ADAM_B1 * m + (1.0 - ADAM_B1) * g
    v = ADAM_B2 * v + (1.0 - ADAM_B2) * (g * g)
    m_hat = m / (1.0 - ADAM_B1 ** ADAM_STEP)
    v_hat = v / (1.0 - ADAM_B2 ** ADAM_STEP)
    delta = -ADAM_LR * (m_hat / (jnp.sqrt(v_hat) + ADAM_EPS) + ADAM_WD * w)
    return delta, m, v


def _shard_update(place, w, m, v, g, got_sib, got_chips, name, part=0, parts=1, before=None):
    r, c = w.shape
    rp = r // parts
    tm = _row_tile(rp, c)
    off = part * (rp // tm)

    def body(pos_ref, w_ref, m_ref, v_ref, g_ref, s_ref, c_ref, *rest):
        go_ref, d_ref, mo_ref, vo_ref = rest[-4:]
        grad = g_ref[...].astype(F32) + s_ref[...].astype(F32)
        for j in range(3):
            grad = grad + c_ref[j].astype(F32)
        delta, mn, vn = _adamw(w_ref[...], grad, m_ref[...], v_ref[...])
        go_ref[...] = grad
        d_ref[...] = delta
        mo_ref[...] = mn
        vo_ref[...] = vn

    row = pl.BlockSpec((tm, c), lambda i, pos: (i + off, 0))
    before = list(before or [])
    return pl.pallas_call(
        body, name=name,
        grid_spec=pltpu.PrefetchScalarGridSpec(
            num_scalar_prefetch=1, grid=(rp // tm,),
            in_specs=[row, row, row,
                      pl.BlockSpec((None, tm, c), lambda i, pos: (4 * pos[0] + 2 * pos[1] + pos[2], i, 0)),
                      pl.BlockSpec((None, tm, c), lambda i, pos: (2 * pos[0] + pos[1], i, 0)),
                      pl.BlockSpec((3, tm, c), lambda i, pos: (0, i, 0))]
            + [pl.BlockSpec(memory_space=pl.ANY)] * len(before),
            out_specs=[row, row, row, row]),
        out_shape=[jax.ShapeDtypeStruct((r, c), F32)] * 4,
        input_output_aliases={7 + k: k for k in range(len(before))},
        compiler_params=_cp(("parallel",)),
    )(place, w, m, v, g, got_sib, got_chips, *before)


def _small_update(stats_all, ws, ms, vs):
    def body(st_ref, w_ref, m_ref, v_ref, go_ref, d_ref, mo_ref, vo_ref):
        grad = st_ref[0]
        for k in range(1, NDEV):
            grad = grad + st_ref[k]
        delta, mn, vn = _adamw(w_ref[...], grad, m_ref[...], v_ref[...])
        go_ref[...] = grad
        d_ref[...] = delta
        mo_ref[...] = mn
        vo_ref[...] = vn

    return pl.pallas_call(
        body, name="small_update",
        out_shape=[jax.ShapeDtypeStruct((8, D), F32)] * 4,
        compiler_params=_cp(),
    )(stats_all, ws, ms, vs)


def kernel(x, norm_mix_w, w_in, w_out, norm_ffn_w, w_gate, w_up, w_down, norm_final_w, loss_target, m_norm_mix_w, m_w_in, m_w_out, m_norm_ffn_w, m_w_gate, m_w_up, m_w_down, m_norm_final_w, v_norm_mix_w, v_w_in, v_w_out, v_norm_ffn_w, v_w_gate, v_w_up, v_w_down, v_norm_final_w):
    tr = {"w_gate", "w_up"}
    names = ["w_in", "w_out", "w_gate", "w_up", "w_down"]

    def view(a, n):
        return a[0].T if n in tr else a[0]

    big_w = [view(a, n) for a, n in zip([w_in, w_out, w_gate, w_up, w_down], names)]
    big_m = [view(a, n) for a, n in zip([m_w_in, m_w_out, m_w_gate, m_w_up, m_w_down], names)]
    big_v = [view(a, n) for a, n in zip([v_w_in, v_w_out, v_w_gate, v_w_up, v_w_down], names)]

    shards = [_cast_bf16(w, "cast_" + n) for w, n in zip(big_w, names)]
    (win,) = _all_gather(shards[0:1], "all_gather_w_in", 1)
    wout, wg, wu = _all_gather(shards[1:4], "all_gather_out_gate_up", 2)
    (wd,) = _all_gather(shards[4:5], "all_gather_w_down", 3)
    nw3 = norm_final_w.reshape(1, D)
    place = jnp.stack([lax.axis_index("x"), lax.axis_index("y"), lax.axis_index("c")]).astype(jnp.int32)
    state = {n: (w, m, v, 0, 1) for n, w, m, v in zip(names, big_w, big_m, big_v)}
    for part in range(W_IN_PARTS):
        state[f"w_in_{part}"] = state["w_in"][:3] + (part, W_IN_PARTS)
    red = _Reduction(place, 4, state)
    stats, gx, *_ = _local_step(
        x[0], loss_target[0], norm_mix_w, norm_ffn_w, nw3, win, wout.reshape(D, D), wg, wu, wd, red)
    stats_all = _exchange_stats(stats, red.next_id())
    upd = [red.update(f"w_in_{W_IN_PARTS - 1}" if n == "w_in" else n) for n in names]
    stats_all = lax.optimization_barrier((stats_all, tuple(upd[0])))[0]

    def rows(a, b, c):
        return jnp.concatenate([a.reshape(1, D), b.reshape(1, D), c.reshape(1, D), jnp.zeros((5, D), F32)], axis=0)

    sg, sd, sm, sv = _small_update(stats_all, rows(norm_mix_w, norm_ffn_w, norm_final_w),
                                   rows(m_norm_mix_w, m_norm_ffn_w, m_norm_final_w),
                                   rows(v_norm_mix_w, v_norm_ffn_w, v_norm_final_w))
    loss = sg[3, 0]

    def outs(k, small):
        big = [(u[k].T if n in tr else u[k])[None] for u, n in zip(upd, names)]
        return [small[0:1], big[0], big[1], small[1:2], big[2], big[3], big[4], small[2]]

    return (loss, gx[None], *outs(0, sg), *outs(1, sd), *outs(2, sm), *outs(3, sv))
```

```python
import functools
import math

import numpy as np
import jax
import jax.numpy as jnp
from jax import lax
from jax.experimental import pallas as pl
from jax.experimental.pallas import tpu as pltpu
from jax.experimental.pallas import tpu_sc as plsc

F32 = jnp.float32
BF16 = jnp.bfloat16

S = 2048
D = 2048
NDEV = 8
N_IN = 7168 // NDEV
N_FF = 5632 // NDEV
NFG, N_FG = NDEV // 2, 2 * N_FF
N_OUT = 2048 // NDEV
AH, AHD = 8, 128
RH, RHD = 4, 256
CH = 128
NB = S // CH
EPS = 1e-6
PATTERNS = ((1, 16), (4, 4), (16, 1))
NEG = -1e30
VMEM_LIMIT = 56 * 1024 * 1024

ADAM_LR, ADAM_B1, ADAM_B2, ADAM_EPS, ADAM_WD, ADAM_STEP = 0.001, 0.9, 0.999, 1e-08, 0.01, 10
MESH = pl.DeviceIdType.MESH


def _cp(sem=None):
    return pltpu.CompilerParams(dimension_semantics=sem, vmem_limit_bytes=VMEM_LIMIT)


def _dot(a, b):
    return jnp.dot(a, b, preferred_element_type=F32)


def _dot_nt(a, b):
    return lax.dot_general(a, b, (((1,), (1,)), ((), ())), preferred_element_type=F32)


def _dot_tn(a, b):
    return lax.dot_general(a, b, (((0,), (0,)), ((), ())), preferred_element_type=F32)


def _sigmoid(x):
    return 0.5 * jnp.tanh(0.5 * x) + 0.5


def _cast_bf16(w, name):
    r, c = w.shape
    tm = r if r <= 1024 else 512

    def body(w_ref, o_ref):
        o_ref[...] = w_ref[...].astype(BF16)

    return pl.pallas_call(
        body, name=name, grid=(r // tm,),
        in_specs=[pl.BlockSpec((tm, c), lambda i: (i, 0))],
        out_specs=pl.BlockSpec((tm, c), lambda i: (i, 0)),
        out_shape=jax.ShapeDtypeStruct((r, c), BF16),
        compiler_params=_cp(("parallel",)),
    )(w)


def _rms_fwd(x, nw):
    tm = 256

    def body(x_ref, w_ref, h_ref, r_ref):
        xs = x_ref[...]
        r = lax.rsqrt(jnp.mean(xs * xs, axis=-1, keepdims=True) + EPS)
        h_ref[...] = ((xs * r) * w_ref[...]).astype(BF16)
        r_ref[...] = r

    return pl.pallas_call(
        body, name="rms_fwd", grid=(S // tm,),
        in_specs=[pl.BlockSpec((tm, D), lambda i: (i, 0)), pl.BlockSpec((1, D), lambda i: (0, 0))],
        out_specs=[pl.BlockSpec((tm, D), lambda i: (i, 0)), pl.BlockSpec((tm, 1), lambda i: (i, 0))],
        out_shape=[jax.ShapeDtypeStruct((S, D), BF16), jax.ShapeDtypeStruct((S, 1), F32)],
        compiler_params=_cp(("parallel",)),
    )(x, nw)


def _row_copies(hbm_refs, bufs, sems, m, tm):
    rows = pl.ds(pl.multiple_of(m * tm, tm), tm)
    return [pltpu.make_async_copy(h.at[rows], b, sems.at[i]) for i, (h, b) in enumerate(zip(hbm_refs, bufs))]


def _rms_bwd_tile(dh, xs, r, nw):
    dnw = jnp.sum(dh * (xs * r), axis=0, keepdims=True)
    gy = dh * nw
    dx = r * gy - xs * ((r * r * r) * jnp.mean(gy * xs, axis=-1, keepdims=True))
    return dx, dnw


def _proj(h1, win):
    tm = 1024

    def body(a_ref, w_ref, o_ref):
        o_ref[...] = _dot(a_ref[...], w_ref[...])

    return pl.pallas_call(
        body, name="proj", grid=(NDEV, S // tm),
        in_specs=[pl.BlockSpec((tm, D), lambda p, m: (m, 0)),
                  pl.BlockSpec((None, D, N_IN), lambda p, m: (p, 0, 0))],
        out_specs=pl.BlockSpec((tm, N_IN), lambda p, m: (m, p)),
        out_shape=jax.ShapeDtypeStruct((S, NDEV * N_IN), F32),
        compiler_params=_cp(("parallel", "parallel")),
    )(h1, win)


def _out_proj_rms(x, ma, mr, wout, nw):
    tm = 256
    half = D // 2

    def body(x_ref, ma_ref, mr_ref, w_ref, nw_ref, x2_ref, h_ref, r_ref):
        acc = _dot(ma_ref[...], w_ref[0:half, :]) + _dot(mr_ref[...], w_ref[half:D, :])
        x2 = x_ref[...] + acc
        r = lax.rsqrt(jnp.mean(x2 * x2, axis=-1, keepdims=True) + EPS)
        x2_ref[...] = x2
        h_ref[...] = ((x2 * r) * nw_ref[...]).astype(BF16)
        r_ref[...] = r

    return pl.pallas_call(
        body, name="out_proj_rms", grid=(S // tm,),
        in_specs=[pl.BlockSpec((tm, D), lambda i: (i, 0)),
                  pl.BlockSpec((tm, half), lambda i: (i, 0)),
                  pl.BlockSpec((tm, half), lambda i: (i, 0)),
                  pl.BlockSpec((D, D), lambda i: (0, 0)),
                  pl.BlockSpec((1, D), lambda i: (0, 0))],
        out_specs=[pl.BlockSpec((tm, D), lambda i: (i, 0)), pl.BlockSpec((tm, D), lambda i: (i, 0)),
                   pl.BlockSpec((tm, 1), lambda i: (i, 0))],
        out_shape=[jax.ShapeDtypeStruct((S, D), F32), jax.ShapeDtypeStruct((S, D), BF16),
                   jax.ShapeDtypeStruct((S, 1), F32)],
        compiler_params=_cp(("parallel",)),
    )(x, ma, mr, wout, nw)


def _ffn_up(h2, wg, wu):
    tm = 512

    def body(h_ref, wg_ref, wu_ref, a_ref, dadg_ref, dadu_ref):
        h = h_ref[...]
        g = _dot_nt(h, wg_ref[...])
        u = _dot_nt(h, wu_ref[...])
        sg = _sigmoid(g)
        silu = g * sg
        a_ref[...] = (silu * u).astype(BF16)
        dadg_ref[...] = (u * (sg * (1.0 + g * (1.0 - sg)))).astype(BF16)
        dadu_ref[...] = silu.astype(BF16)

    blk = pl.BlockSpec((None, tm, N_FG), lambda p, m: (p, m, 0))
    wblk = pl.BlockSpec((None, N_FG, D), lambda p, m: (p, 0, 0))
    return pl.pallas_call(
        body, name="ffn_up", grid=(NFG, S // tm),
        in_specs=[pl.BlockSpec((tm, D), lambda p, m: (m, 0)), wblk, wblk],
        out_specs=[blk, blk, blk],
        out_shape=[jax.ShapeDtypeStruct((NFG, S, N_FG), BF16)] * 3,
        compiler_params=_cp(("parallel", "parallel")),
    )(h2, wg, wu)


def _ffn_down_loss(x2, a, wd, nw, tgt):
    tm = 512

    def body(x2_hbm, a_ref, w_ref, nw_ref, t_hbm, dx_ref, dxb_ref, st_ref, acc_ref, x2_buf, t_buf, sems):
        m, p = pl.program_id(0), pl.program_id(1)
        tail_in = _row_copies((x2_hbm, t_hbm), (x2_buf, t_buf), sems, m, tm)

        @pl.when(p == 0)
        def _():
            acc_ref[...] = jnp.zeros_like(acc_ref)
            for cp in tail_in:
                cp.start()

        @pl.when((p == 0) & (m == 0))
        def _():
            st_ref[...] = jnp.zeros_like(st_ref)

        acc_ref[...] += _dot(a_ref[...], w_ref[...])

        @pl.when(p == NFG - 1)
        def _():
            for cp in tail_in:
                cp.wait()
            x3 = x2_buf[...] + acc_ref[...]
            nwv = nw_ref[...]
            r = lax.rsqrt(jnp.mean(x3 * x3, axis=-1, keepdims=True) + EPS)
            y = (x3 * r) * nwv
            err = y - t_buf[...]
            loss = 0.5 * jnp.sum(jnp.mean(err * err, axis=-1, keepdims=True), axis=0, keepdims=True)
            dy = err * (1.0 / D)
            dx, dnw = _rms_bwd_tile(dy, x3, r, nwv)
            dx_ref[...] = dx
            dxb_ref[...] = dx.astype(BF16)
            st_ref[0:1, :] += dnw
            st_ref[1:2, :] += jnp.broadcast_to(loss, (1, D))

    return pl.pallas_call(
        body, name="ffn_down_loss", grid=(S // tm, NFG),
        in_specs=[pl.BlockSpec(memory_space=pl.ANY),
                  pl.BlockSpec((None, tm, N_FG), lambda m, p: (p, m, 0)),
                  pl.BlockSpec((None, N_FG, D), lambda m, p: (p, 0, 0)),
                  pl.BlockSpec((1, D), lambda m, p: (0, 0)),
                  pl.BlockSpec(memory_space=pl.ANY)],
        out_specs=[pl.BlockSpec((tm, D), lambda m, p: (m, 0)), pl.BlockSpec((tm, D), lambda m, p: (m, 0)),
                   pl.BlockSpec((8, D), lambda m, p: (0, 0))],
        out_shape=[jax.ShapeDtypeStruct((S, D), F32), jax.ShapeDtypeStruct((S, D), BF16),
                   jax.ShapeDtypeStruct((8, D), F32)],
        scratch_shapes=[pltpu.VMEM((tm, D), F32), pltpu.VMEM((tm, D), F32), pltpu.VMEM((tm, D), F32),
                        pltpu.SemaphoreType.DMA((2,))],
        compiler_params=_cp(("arbitrary", "arbitrary")),
    )(x2, a, wd, nw, tgt)


def _ffn_down_bwd(dx3b, wd, dadg, dadu):
    tm = 1024

    def body(dx_ref, w_ref, dadg_ref, dadu_ref, dg_ref, du_ref):
        da = _dot_nt(dx_ref[...], w_ref[...])
        dg_ref[...] = (da * dadg_ref[...].astype(F32)).astype(BF16)
        du_ref[...] = (da * dadu_ref[...].astype(F32)).astype(BF16)

    blk = pl.BlockSpec((None, tm, N_FG), lambda p, m: (p, m, 0))
    return pl.pallas_call(
        body, name="ffn_down_bwd", grid=(NFG, S // tm),
        in_specs=[pl.BlockSpec((tm, D), lambda p, m: (m, 0)),
                  pl.BlockSpec((None, N_FG, D), lambda p, m: (p, 0, 0)), blk, blk],
        out_specs=[blk, blk],
        out_shape=[jax.ShapeDtypeStruct((NFG, S, N_FG), BF16)] * 2,
        compiler_params=_cp(("parallel", "parallel")),
    )(dx3b, wd, dadg, dadu)


def _ffn_up_bwd(dg, du, wg, wu, dres, xs, r, nw):
    tm = 512

    def body(dg_ref, du_ref, wg_ref, wu_ref, dres_hbm, x_hbm, r_ref, nw_ref, dx_ref, dxb_ref, st_ref,
             dres_buf, x_buf, sems):
        m, p = pl.program_id(0), pl.program_id(1)
        tail_in = _row_copies((dres_hbm, x_hbm), (dres_buf, x_buf), sems, m, tm)

        @pl.when(p == 0)
        def _():
            dx_ref[...] = jnp.zeros_like(dx_ref)
            for cp in tail_in:
                cp.start()

        @pl.when((p == 0) & (m == 0))
        def _():
            st_ref[...] = jnp.zeros_like(st_ref)

        dx_ref[...] += _dot(dg_ref[...], wg_ref[...])
        dx_ref[...] += _dot(du_ref[...], wu_ref[...])

        @pl.when(p == NFG - 1)
        def _():
            for cp in tail_in:
                cp.wait()
            dx, dnw = _rms_bwd_tile(dx_ref[...], x_buf[...], r_ref[...], nw_ref[...])
            dx = dres_buf[...] + dx
            dx_ref[...] = dx
            dxb_ref[...] = dx.astype(BF16)
            st_ref[0:1, :] += dnw

    blk = pl.BlockSpec((None, tm, N_FG), lambda m, p: (p, m, 0))
    wblk = pl.BlockSpec((None, N_FG, D), lambda m, p: (p, 0, 0))
    row = pl.BlockSpec((tm, D), lambda m, p: (m, 0))
    hbm = pl.BlockSpec(memory_space=pl.ANY)
    return pl.pallas_call(
        body, name="ffn_up_bwd", grid=(S // tm, NFG),
        in_specs=[blk, blk, wblk, wblk, hbm, hbm, pl.BlockSpec((tm, 1), lambda m, p: (m, 0)),
                  pl.BlockSpec((1, D), lambda m, p: (0, 0))],
        out_specs=[row, row, pl.BlockSpec((8, D), lambda m, p: (0, 0))],
        out_shape=[jax.ShapeDtypeStruct((S, D), F32), jax.ShapeDtypeStruct((S, D), BF16),
                   jax.ShapeDtypeStruct((8, D), F32)],
        scratch_shapes=[pltpu.VMEM((tm, D), F32), pltpu.VMEM((tm, D), F32), pltpu.SemaphoreType.DMA((2,))],
        compiler_params=_cp(("arbitrary", "arbitrary")),
    )(dg, du, wg, wu, dres, xs, r, nw)


def _out_proj_bwd(dx2b, wout):
    tm = 256

    def body(dx_ref, w_ref, o_ref):
        o_ref[...] = _dot_nt(dx_ref[...], w_ref[...])

    return pl.pallas_call(
        body, name="out_proj_bwd", grid=(S // tm,),
        in_specs=[pl.BlockSpec((tm, D), lambda i: (i, 0)), pl.BlockSpec((D, D), lambda i: (0, 0))],
        out_specs=pl.BlockSpec((tm, D), lambda i: (i, 0)),
        out_shape=jax.ShapeDtypeStruct((S, D), F32),
        compiler_params=_cp(("parallel",)),
    )(dx2b, wout)


def _in_proj_bwd(dproj, win, dres, xs, r, nw):
    tm = 1024

    def body(dp_ref, w_ref, dres_hbm, x_hbm, r_ref, nw_ref, dx_ref, st_ref, dres_buf, x_buf, sems):
        m, p = pl.program_id(0), pl.program_id(1)
        tail_in = _row_copies((dres_hbm, x_hbm), (dres_buf, x_buf), sems, m, tm)

        @pl.when(p == 0)
        def _():
            dx_ref[...] = jnp.zeros_like(dx_ref)
            for cp in tail_in:
                cp.start()

        @pl.when((p == 0) & (m == 0))
        def _():
            st_ref[...] = jnp.zeros_like(st_ref)

        dx_ref[...] += _dot_nt(dp_ref[...], w_ref[...])

        @pl.when(p == NDEV - 1)
        def _():
            for cp in tail_in:
                cp.wait()
            dx, dnw = _rms_bwd_tile(dx_ref[...], x_buf[...], r_ref[...], nw_ref[...])
            dx_ref[...] = dres_buf[...] + dx
            st_ref[0:1, :] += dnw

    row = pl.BlockSpec((tm, D), lambda m, p: (m, 0))
    hbm = pl.BlockSpec(memory_space=pl.ANY)
    return pl.pallas_call(
        body, name="in_proj_bwd", grid=(S // tm, NDEV),
        in_specs=[pl.BlockSpec((tm, N_IN), lambda m, p: (m, p)),
                  pl.BlockSpec((None, D, N_IN), lambda m, p: (p, 0, 0)),
                  hbm, hbm, pl.BlockSpec((tm, 1), lambda m, p: (m, 0)),
                  pl.BlockSpec((1, D), lambda m, p: (0, 0))],
        out_specs=[row, pl.BlockSpec((8, D), lambda m, p: (0, 0))],
        out_shape=[jax.ShapeDtypeStruct((S, D), F32), jax.ShapeDtypeStruct((8, D), F32)],
        scratch_shapes=[pltpu.VMEM((tm, D), F32), pltpu.VMEM((tm, D), F32), pltpu.SemaphoreType.DMA((2,))],
        compiler_params=_cp(("arbitrary", "arbitrary")),
    )(dproj, win, dres, xs, r, nw)


W_IN_PARTS = 2


def _wgrad_in(h1, dproj, part):
    rows = D // W_IN_PARTS

    def body(a_ref, d_ref, o_ref):
        o_ref[...] = _dot_tn(a_ref[...], d_ref[...]).astype(BF16)

    return pl.pallas_call(
        body, name=f"wgrad_in_{part}", grid=(NDEV,),
        in_specs=[pl.BlockSpec((S, rows), lambda p: (0, part)), pl.BlockSpec((S, N_IN), lambda p: (0, p))],
        out_specs=pl.BlockSpec((None, rows, N_IN), lambda p: (p, 0, 0)),
        out_shape=jax.ShapeDtypeStruct((NDEV, rows, N_IN), BF16),
        compiler_params=_cp(("parallel",)),
    )(h1, dproj)


def _wgrad_rows(a3, dy, name):
    def body(a_ref, d_ref, o_ref):
        o_ref[...] = _dot_tn(a_ref[...], d_ref[...]).astype(BF16)

    return pl.pallas_call(
        body, name=name, grid=(NFG,),
        in_specs=[pl.BlockSpec((None, S, N_FG), lambda p: (p, 0, 0)), pl.BlockSpec((S, D), lambda p: (0, 0))],
        out_specs=pl.BlockSpec((None, N_FG, D), lambda p: (p, 0, 0)),
        out_shape=jax.ShapeDtypeStruct((NFG, N_FG, D), BF16),
        compiler_params=_cp(("parallel",)),
    )(a3, dy).reshape(NDEV, N_FF, D)


def _wgrad_out(ma, mr, dx2b):
    half = D // 2
    per = half // N_OUT

    def body(ma_ref, mr_ref, d_ref, o_ref):
        p = pl.program_id(0)

        @pl.when(p < per)
        def _():
            o_ref[...] = _dot_tn(ma_ref[...], d_ref[...]).astype(BF16)

        @pl.when(p >= per)
        def _():
            o_ref[...] = _dot_tn(mr_ref[...], d_ref[...]).astype(BF16)

    return pl.pallas_call(
        body, name="wgrad_out", grid=(NDEV,),
        in_specs=[pl.BlockSpec((S, N_OUT), lambda p: (0, jnp.minimum(p, per - 1))),
                  pl.BlockSpec((S, N_OUT), lambda p: (0, jnp.maximum(p - per, 0))),
                  pl.BlockSpec((S, D), lambda p: (0, 0))],
        out_specs=pl.BlockSpec((None, N_OUT, D), lambda p: (p, 0, 0)),
        out_shape=jax.ShapeDtypeStruct((NDEV, N_OUT, D), BF16),
        compiler_params=_cp(("parallel",)),
    )(ma, mr, dx2b)


def _attn_consts():
    c = np.zeros((AH, 8, AHD), np.float32)
    for h in range(AH):
        c[h, :, :] = 2.0 ** (-(h + 1))
    return jnp.asarray(c)


def _permute_in(dst, src, d, cast=None):
    ln = S // d
    for rr in range(d):
        v = src[pl.ds(rr, ln, stride=d), :] if d > 1 else src[...]
        dst[rr * ln:(rr + 1) * ln, :] = v if cast is None else v.astype(cast)


def _attn_masks():
    qi = lax.broadcasted_iota(jnp.int32, (CH, CH), 0)
    kj = lax.broadcasted_iota(jnp.int32, (CH, CH), 1)
    dist_c = (qi - kj).astype(F32)
    dist_p = (qi - kj + CH).astype(F32)
    return (qi >= kj)[None], (kj >= qi)[None], dist_c[None], dist_p[None]


GB = 8


def _bdot_nt(a, b):
    return lax.dot_general(a, b, (((2,), (2,)), ((0,), (0,))), preferred_element_type=F32)


def _bdot(a, b):
    return lax.dot_general(a, b, (((2,), (1,)), ((0,), (0,))), preferred_element_type=F32)


def _bdot_tn(a, b):
    return lax.dot_general(a, b, (((1,), (1,)), ((0,), (0,))), preferred_element_type=F32)


def _shift_block(dst, src):
    dst[0:CH, :] = jnp.zeros((CH, AHD), dst.dtype)
    dst[CH:S, :] = src[0:S - CH, :]


def _has_prev(g, nb):
    blk = lax.broadcasted_iota(jnp.int32, (GB, 1, 1), 0) + g * GB
    return (blk & (nb - 1)) != 0


def _blocks(ref, g):
    return ref[g * GB * CH:(g + 1) * GB * CH, :].reshape(GB, CH, AHD)


def _attn_fwd(proj):
    scale = 1.0 / math.sqrt(AHD)

    def body(c_ref, q_ref, k_ref, v_ref, o_ref, ob_ref, lse_ref, qd, kd, vd, kps, vps, od, ld, *nat):
        onat, lnat = nat[0:3], nat[3:6]
        slope = c_ref[0:1, :]
        mask_c, mask_p, dist_c, dist_p = _attn_masks()
        for pi, (d, nb) in enumerate(PATTERNS):
            _permute_in(qd, q_ref, d, BF16)
            _permute_in(kd, k_ref, d, BF16)
            _permute_in(vd, v_ref, d, BF16)
            if nb > 1:
                _shift_block(kps, kd)
                _shift_block(vps, vd)
            bias_c = -(slope * float(d)) * dist_c
            bias_p = -(slope * float(d)) * dist_p
            for g in range(NB // GB):
                q3, k3, v3 = _blocks(qd, g), _blocks(kd, g), _blocks(vd, g)
                s_c = jnp.where(mask_c, _bdot_nt(q3, k3) * scale + bias_c, NEG)
                mx = jnp.max(s_c, axis=-1, keepdims=True)
                if nb > 1:
                    kp3, vp3 = _blocks(kps, g), _blocks(vps, g)
                    s_p = jnp.where(jnp.logical_and(mask_p, _has_prev(g, nb)),
                                    _bdot_nt(q3, kp3) * scale + bias_p, NEG)
                    mx = jnp.maximum(mx, jnp.max(s_p, axis=-1, keepdims=True))
                    l = (jnp.sum(jnp.exp(s_c - mx), axis=-1, keepdims=True)
                         + jnp.sum(jnp.exp(s_p - mx), axis=-1, keepdims=True))
                    lse = mx + jnp.log(l)
                    o3 = _bdot(jnp.exp(s_c - lse).astype(BF16), v3) + _bdot(jnp.exp(s_p - lse).astype(BF16), vp3)
                else:
                    l = jnp.sum(jnp.exp(s_c - mx), axis=-1, keepdims=True)
                    lse = mx + jnp.log(l)
                    o3 = _bdot(jnp.exp(s_c - lse).astype(BF16), v3)
                rows = slice(g * GB * CH, (g + 1) * GB * CH)
                od[rows, :] = o3.reshape(GB * CH, AHD)
                ld[rows, :] = jnp.broadcast_to(lse, (GB, CH, AHD)).reshape(GB * CH, AHD)
            ln = S // d
            for rr in range(d):
                if d > 1:
                    onat[pi][pl.ds(rr, ln, stride=d), :] = od[rr * ln:(rr + 1) * ln, :]
                    lnat[pi][pl.ds(rr, ln, stride=d), :] = ld[rr * ln:(rr + 1) * ln, :]
                else:
                    onat[pi][...] = od[...]
                    lnat[pi][...] = ld[...]
        l0, l1, l2 = lnat[0][...], lnat[1][...], lnat[2][...]
        mx = jnp.maximum(jnp.maximum(l0, l1), l2)
        e0, e1, e2 = jnp.exp(l0 - mx), jnp.exp(l1 - mx), jnp.exp(l2 - mx)
        den = e0 + e1 + e2
        out = (e0 / den) * onat[0][...] + (e1 / den) * onat[1][...] + (e2 / den) * onat[2][...]
        o_ref[...] = out
        ob_ref[...] = out.astype(BF16)
        lse_ref[...] = mx + jnp.log(den)

    def col(off):
        return pl.BlockSpec((S, AHD), lambda h: (0, off + h))

    return pl.pallas_call(
        body, name="attn_fwd", grid=(AH,),
        in_specs=[pl.BlockSpec((None, 8, AHD), lambda h: (h, 0, 0)), col(0), col(AH), col(2 * AH)],
        out_specs=[col(0), col(0), col(0)],
        out_shape=[jax.ShapeDtypeStruct((S, AH * AHD), F32), jax.ShapeDtypeStruct((S, AH * AHD), BF16),
                   jax.ShapeDtypeStruct((S, AH * AHD), F32)],
        scratch_shapes=[pltpu.VMEM((S, AHD), BF16) for _ in range(5)]
        + [pltpu.VMEM((S, AHD), F32) for _ in range(8)],
        compiler_params=_cp(("parallel",)),
    )(_attn_consts(), proj, proj, proj)


def _attn_bwd(proj, dmixed, o, lse):
    scale = 1.0 / math.sqrt(AHD)

    def body(c_ref, q_ref, k_ref, v_ref, do_ref, o_ref, lse_ref, dq_ref, dk_ref, dv_ref,
             qd, kd, vd, dod, kps, vps, lsd, dld, dqd, dkd, dvd, delta, aq, ak, av):
        slope = c_ref[0:1, :]
        mask_c, mask_p, dist_c, dist_p = _attn_masks()
        delta[...] = jnp.broadcast_to(jnp.sum(do_ref[...] * o_ref[...], axis=-1, keepdims=True), (S, AHD))
        for pi, (d, nb) in enumerate(PATTERNS):
            _permute_in(qd, q_ref, d, BF16)
            _permute_in(kd, k_ref, d, BF16)
            _permute_in(vd, v_ref, d, BF16)
            _permute_in(dod, do_ref, d, BF16)
            _permute_in(lsd, lse_ref, d)
            _permute_in(dld, delta, d)
            if nb > 1:
                _shift_block(kps, kd)
                _shift_block(vps, vd)
            bias_c = -(slope * float(d)) * dist_c
            bias_p = -(slope * float(d)) * dist_p
            for g in range(NB // GB):
                q3, k3, v3, do3 = _blocks(qd, g), _blocks(kd, g), _blocks(vd, g), _blocks(dod, g)
                ls, dl = _blocks(lsd, g), _blocks(dld, g)
                lo, hi = g * GB * CH, (g + 1) * GB * CH
                p_c = jnp.exp(jnp.where(mask_c, _bdot_nt(q3, k3) * scale + bias_c, NEG) - ls)
                ds_c = ((p_c * (_bdot_nt(do3, v3) - dl)) * scale).astype(BF16)
                dq3 = _bdot(ds_c, k3)
                dkd[lo:hi, :] = _bdot_tn(ds_c, q3).reshape(GB * CH, AHD)
                dvd[lo:hi, :] = _bdot_tn(p_c.astype(BF16), do3).reshape(GB * CH, AHD)
                if nb > 1:
                    kp3, vp3 = _blocks(kps, g), _blocks(vps, g)
                    p_p = jnp.exp(jnp.where(jnp.logical_and(mask_p, _has_prev(g, nb)),
                                            _bdot_nt(q3, kp3) * scale + bias_p, NEG) - ls)
                    ds_p = ((p_p * (_bdot_nt(do3, vp3) - dl)) * scale).astype(BF16)
                    dq3 = dq3 + _bdot(ds_p, kp3)
                    dkp = _bdot_tn(ds_p, q3).reshape(GB * CH, AHD)
                    dvp = _bdot_tn(p_p.astype(BF16), do3).reshape(GB * CH, AHD)
                    if g == 0:
                        dkd[0:hi - CH, :] += dkp[CH:, :]
                        dvd[0:hi - CH, :] += dvp[CH:, :]
                    else:
                        dkd[lo - CH:hi - CH, :] += dkp
                        dvd[lo - CH:hi - CH, :] += dvp
                dqd[lo:hi, :] = dq3.reshape(GB * CH, AHD)
            ln = S // d
            for acc, src in ((aq, dqd), (ak, dkd), (av, dvd)):
                if pi == 0:
                    acc[...] = src[...]
                else:
                    for rr in range(d):
                        acc[pl.ds(rr, ln, stride=d), :] += src[rr * ln:(rr + 1) * ln, :]
        dq_ref[...] = aq[...].astype(BF16)
        dk_ref[...] = ak[...].astype(BF16)
        dv_ref[...] = av[...].astype(BF16)

    def col(off):
        return pl.BlockSpec((S, AHD), lambda h: (0, off + h))

    return pl.pallas_call(
        body, name="attn_bwd", grid=(AH,),
        in_specs=[pl.BlockSpec((None, 8, AHD), lambda h: (h, 0, 0)), col(0), col(AH), col(2 * AH),
                  col(0), col(0), col(0)],
        out_specs=[col(0), col(0), col(0)],
        out_shape=[jax.ShapeDtypeStruct((S, AH * AHD), BF16)] * 3,
        scratch_shapes=[pltpu.VMEM((S, AHD), BF16) for _ in range(6)]
        + [pltpu.VMEM((S, AHD), F32) for _ in range(9)],
        compiler_params=_cp(("parallel",)),
    )(_attn_consts(), proj, proj, proj, dmixed, o, lse)


def _ret_consts():
    c = np.zeros((RH, 8, RHD), np.float32)
    for h in range(RH):
        c[h, :, :] = np.log(np.float32(1.0) - np.float32(2.0 ** (-5.0 - h)))
    return jnp.asarray(c)


def _ret_factors(lg):
    i = lax.broadcasted_iota(jnp.int32, (CH, CH), 0)
    j = lax.broadcasted_iota(jnp.int32, (CH, CH), 1)
    dif = (i - j).astype(F32)
    decay = jnp.where(dif >= 0, jnp.exp(lg[:, 0:CH] * jnp.maximum(dif, 0.0)), 0.0)
    row = lax.broadcasted_iota(jnp.int32, (CH, RHD), 0).astype(F32)
    zeta = jnp.exp(lg * (CH - 1.0 - row))
    xi = jnp.exp(lg * (row + 1.0))
    return decay, zeta, xi, jnp.exp(lg * float(CH))


CBK = 8
RSTEPS = NB // CBK


def _ret_specs(rev):
    off = 3 * AH * AHD // RHD
    rows = CBK * CH

    def ch(n):
        return (RSTEPS - 1 - n) if rev else n

    def col(k):
        return pl.BlockSpec((rows, RHD), lambda h, n: (ch(n), off + k * RH + h))

    own = pl.BlockSpec((rows, RHD), lambda h, n: (ch(n), h))
    state = pl.BlockSpec((None, CBK, RHD, RHD), lambda h, n: (h, ch(n), 0, 0))
    const = pl.BlockSpec((None, 8, RHD), lambda h, n: (h, 0, 0))
    dm = pl.BlockSpec((rows, RHD), lambda h, n: (ch(n), AH * AHD // RHD + h))
    return col, own, state, const, dm


def _chunks(x):
    return x.reshape(CBK, CH, RHD)


def _ret_fwd(proj):
    def body(c_ref, q_ref, k_ref, v_ref, g_ref, ret_ref, mr_ref, st_ref, r_acc):
        n = pl.program_id(1)

        @pl.when(n == 0)
        def _():
            r_acc[...] = jnp.zeros_like(r_acc)

        decay, zeta, xi, gch = _ret_factors(c_ref[0:1, :])
        q3 = _chunks(q_ref[...].astype(BF16))
        kc = _chunks(k_ref[...] * (1.0 / math.sqrt(RHD)))
        k3 = kc.astype(BF16)
        v3 = _chunks(v_ref[...].astype(BF16))
        kv3 = _bdot_tn((kc * zeta[None]).astype(BF16), v3)
        r = r_acc[...]
        for i in range(CBK):
            st_ref[i] = r.astype(BF16)
            r = r * gch + kv3[i]
        r_acc[...] = r
        scores = _bdot_nt(q3, k3) * decay[None]
        ret = (_bdot(scores.astype(BF16), v3) + _bdot(q3, st_ref[...]) * xi[None]).reshape(CBK * CH, RHD)
        ret_ref[...] = ret
        rr = lax.rsqrt(jnp.mean(ret * ret, axis=-1, keepdims=True) + EPS)
        gv = g_ref[...]
        mr_ref[...] = ((gv * _sigmoid(gv)) * (ret * rr)).astype(BF16)

    col, own, state, const, _ = _ret_specs(False)
    return pl.pallas_call(
        body, name="ret_fwd", grid=(RH, RSTEPS),
        in_specs=[const, col(0), col(1), col(2), col(3)],
        out_specs=[own, own, state],
        out_shape=[jax.ShapeDtypeStruct((S, RH * RHD), F32), jax.ShapeDtypeStruct((S, RH * RHD), BF16),
                   jax.ShapeDtypeStruct((RH, NB, RHD, RHD), BF16)],
        scratch_shapes=[pltpu.VMEM((RHD, RHD), F32)],
        compiler_params=_cp(("parallel", "arbitrary")),
    )(_ret_consts(), proj, proj, proj, proj)


def _ret_bwd(proj, ret, states, dmixed):
    def body(c_ref, q_ref, k_ref, v_ref, g_ref, ret_ref, st_ref, dm_ref, dq_ref, dk_ref, dv_ref, dg_ref, g_acc, gs):
        n = pl.program_id(1)

        @pl.when(n == 0)
        def _():
            g_acc[...] = jnp.zeros_like(g_acc)

        decay, zeta, xi, gch = _ret_factors(c_ref[0:1, :])
        ret_v = ret_ref[...]
        rr = lax.rsqrt(jnp.mean(ret_v * ret_v, axis=-1, keepdims=True) + EPS)
        gv = g_ref[...]
        sg = _sigmoid(gv)
        dmix = dm_ref[...]
        dg_ref[...] = ((dmix * (ret_v * rr)) * (sg * (1.0 + gv * (1.0 - sg)))).astype(BF16)
        dretn = dmix * (gv * sg)
        dret = _chunks(rr * dretn - ret_v * ((rr * rr * rr) * jnp.mean(dretn * ret_v, axis=-1, keepdims=True)))

        q3 = _chunks(q_ref[...].astype(BF16))
        kc = _chunks(k_ref[...] * (1.0 / math.sqrt(RHD)))
        k3 = kc.astype(BF16)
        v3 = _chunks(v_ref[...].astype(BF16))
        d3 = dret.astype(BF16)
        dxi = (dret * xi[None]).astype(BF16)
        kz = (kc * zeta[None]).astype(BF16)
        dr3 = _bdot_tn(q3, dxi)
        acc = g_acc[...]
        for i in reversed(range(CBK)):
            gs[i] = acc.astype(BF16)
            acc = dr3[i] + gch * acc
        g_acc[...] = acc
        g3 = gs[...]
        sc = (_bdot_nt(q3, k3) * decay[None]).astype(BF16)
        da = (_bdot_nt(d3, v3) * decay[None]).astype(BF16)
        dq = _bdot(da, k3) + _bdot_nt(dxi, st_ref[...])
        dkc = _bdot_tn(da, q3) + _bdot_nt(v3, g3) * zeta[None]
        dv = _bdot_tn(sc, d3) + _bdot(kz, g3)
        dq_ref[...] = dq.reshape(CBK * CH, RHD).astype(BF16)
        dk_ref[...] = (dkc * (1.0 / math.sqrt(RHD))).reshape(CBK * CH, RHD).astype(BF16)
        dv_ref[...] = dv.reshape(CBK * CH, RHD).astype(BF16)

    col, own, state, const, dm = _ret_specs(True)
    return pl.pallas_call(
        body, name="ret_bwd", grid=(RH, RSTEPS),
        in_specs=[const, col(0), col(1), col(2), col(3), own, state, dm],
        out_specs=[own, own, own, own],
        out_shape=[jax.ShapeDtypeStruct((S, RH * RHD), BF16)] * 4,
        scratch_shapes=[pltpu.VMEM((RHD, RHD), F32), pltpu.VMEM((CBK, RHD, RHD), BF16)],
        compiler_params=_cp(("parallel", "arbitrary")),
    )(_ret_consts(), proj, proj, proj, proj, ret, states, dmixed)


class _NoReduction:
    def start(self, group, grads):
        pass

    def local(self, name, first=()):
        return []

    def landed(self, name):
        return []

    def update(self, name):
        return []


def _local_step(x, tgt, nw1, nw2, nw3, win, wout, wg, wu, wd, red=None):
    red = red or _NoReduction()

    def after(values, first):
        return lax.optimization_barrier((tuple(values), tuple(first)))[0]

    wg, wu, wd = (w.reshape(NFG, N_FG, D) for w in (wg, wu, wd))
    h1, r1 = _rms_fwd(x, nw1)
    proj = _proj(h1, win)
    o, ma, lse = _attn_fwd(proj)
    ret, mr, states = _ret_fwd(proj)
    x2, h2, r2 = _out_proj_rms(x, ma, mr, wout, nw2)
    a, dadg, dadu = _ffn_up(h2, wg, wu)
    dx3, dx3b, st3 = _ffn_down_loss(x2, a, wd, nw3, tgt)

    dwd = _wgrad_rows(a, dx3b, "wgrad_down")
    red.start(["w_down"], [dwd])
    (dx3b,) = after([dx3b], [dwd])
    dg, du = _ffn_down_bwd(dx3b, wd, dadg, dadu)
    dg, du = after([dg, du], red.local("w_down", first=[dg]))
    dwg = _wgrad_rows(dg, h2, "wgrad_gate")
    red.start(["w_gate"], [dwg])
    (du,) = after([du], [dwg])
    dwu = _wgrad_rows(du, h2, "wgrad_up")
    red.start(["w_up"], [dwu])
    dg, du = after([dg, du], [dwu] + red.local("w_gate"))
    dx2, dx2b, st2 = _ffn_up_bwd(dg, du, wg, wu, dx3, x2, r2, nw2)
    (dx2b,) = after([dx2b], red.local("w_up", first=[dx2b] + red.landed("w_down")))
    dwo = _wgrad_out(ma, mr, dx2b)
    red.start(["w_out"], [dwo])
    (dx2b,) = after([dx2b], [dwo])
    dmixed = _out_proj_bwd(dx2b, wout)
    dqa, dka, dva = _attn_bwd(proj, dmixed, o, lse)
    (dmixed,) = after([dmixed], red.local("w_out", first=[dqa] + red.landed("w_gate")))
    dqr, dkr, dvr, dgr = _ret_bwd(proj, ret, states, dmixed)
    dproj = jnp.concatenate([dqa, dka, dva, dqr, dkr, dvr, dgr], axis=1)
    (dwi0,) = after([_wgrad_in(h1, dproj, 0)], red.landed("w_up"))
    red.start(["w_in_0"], [dwi0])
    (dproj,) = after([dproj], [dwi0])
    dwi1 = _wgrad_in(h1, dproj, 1)
    red.start(["w_in_1"], [dwi1])
    sums = red.local("w_in_0", first=[dwi1] + red.landed("w_out"))
    sums = red.local("w_in_1", first=sums + red.update("w_down"))
    (dproj,) = after([dproj], sums)
    gx, st1 = _in_proj_bwd(dproj, win, dx2, x, r1, nw1)
    dwi = jnp.concatenate([dwi0, dwi1], axis=1)
    stats = jnp.concatenate([st1[0:1], st2[0:1], st3[0:2], jnp.zeros((4, D), F32)], axis=0)
    return stats, gx, dwi, dwo, dwg, dwu, dwd


def _place():
    x, y, c = lax.axis_index("x"), lax.axis_index("y"), lax.axis_index("c")
    return x, y, c, [(1 - x, y), (x, 1 - y), (1 - x, 1 - y)]


def _handshake(peers):
    barrier = pltpu.get_barrier_semaphore()
    for peer in peers:
        pl.semaphore_signal(barrier, inc=1, device_id=peer, device_id_type=MESH)
    pl.semaphore_wait(barrier, len(peers))


def _all_gather(shards, name, collective_id):
    na = len(shards)
    SIB, XN0, XN1, YN1, YN0, VIA_X, VIA_Y = 0, 1, 2, 3, 4, 5, 6
    D2D = {XN0: 7, XN1: 8, YN1: 9, YN0: 10, VIA_X: 11, VIA_Y: 12}

    def body(*refs):
        ins, outs = refs[:na], refs[na:2 * na]
        send_sems, recv_sems, local_sems = refs[2 * na:]
        x, y, c, _ = _place()
        me, sib = (x, y, c), (x, y, 1 - c)
        xn, yn, dg = (1 - x, y, c), (x, 1 - y, c), (1 - x, 1 - y, c)
        _handshake([sib, xn, yn])

        def part(ref, h):
            rows = ref.shape[0] // 2
            return ref if h is None else ref.at[pl.ds(h * rows, rows)]

        def block(a, owner, h):
            return part(outs[a].at[4 * owner[0] + 2 * owner[1] + owner[2]], h)

        def copy(a, k, owner, h, to, own_src=False):
            return pltpu.make_async_remote_copy(
                src_ref=part(ins[a], h) if own_src else block(a, owner, h), dst_ref=block(a, owner, h),
                send_sem=send_sems.at[a, k], recv_sem=recv_sems.at[a, k], device_id=to, device_id_type=MESH)

        def other(p):
            return (p[0], p[1], 1 - c)

        mine = [pltpu.make_async_copy(ins[a], block(a, me, None), local_sems.at[a]) for a in range(na)]
        for cp in mine:
            cp.start()
        sent = []
        for a in range(na):
            sent += [copy(a, XN0, me, 0, xn, True), copy(a, YN1, me, 1, yn, True),
                     copy(a, XN1, me, 1, xn, True), copy(a, YN0, me, 0, yn, True)]
        sent += [copy(a, SIB, me, None, sib, True) for a in range(na)]
        for cp in sent:
            cp.start()

        def landed(a, k, owner, h, then):
            copy(a, k, owner, h, me).wait_recv()
            for k2, to in then + [(D2D[k], sib)]:
                cp = copy(a, k2, owner, h, to)
                cp.start()
                sent.append(cp)

        for a in range(na):
            landed(a, XN0, xn, 0, [(VIA_Y, yn)])
            landed(a, YN1, yn, 1, [(VIA_X, xn)])
            landed(a, XN1, xn, 1, [])
            landed(a, YN0, yn, 0, [])
        for a in range(na):
            landed(a, VIA_Y, dg, 0, [])
            landed(a, VIA_X, dg, 1, [])
        for a in range(na):
            copy(a, SIB, sib, None, me).wait_recv()
            for k, owner, h in ((XN0, xn, 0), (XN1, xn, 1), (YN1, yn, 1), (YN0, yn, 0), (VIA_Y, dg, 0), (VIA_X, dg, 1)):
                copy(a, D2D[k], other(owner), h, me).wait_recv()
        for cp in sent:
            cp.wait_send()
        for cp in mine:
            cp.wait()

    return _sequencer_call(
        body, name, collective_id,
        [jax.ShapeDtypeStruct((NDEV,) + s.shape, s.dtype) for s in shards],
        [pltpu.SemaphoreType.DMA((na, 13)), pltpu.SemaphoreType.DMA((na, 13)), pltpu.SemaphoreType.DMA((na,))])(*shards)


def _sequencer_call(body, name, collective_id, out_type, scratch_types):
    return pl.kernel(
        body, name=name, out_type=out_type,
        mesh=plsc.ScalarSubcoreMesh(axis_name="sequencer", num_cores=1),
        scratch_types=scratch_types,
        compiler_params=pltpu.CompilerParams(collective_id=collective_id))


def _exchange_sibling(grads, name, collective_id):
    na = len(grads)

    def body(*refs):
        ins, outs = refs[:na], refs[na:2 * na]
        send_sems, recv_sems = refs[2 * na:]
        x, y, c, _ = _place()
        _handshake([(x, y, 1 - c)])
        cps = []
        for a in range(na):
            for k in range(4):
                cps.append(pltpu.make_async_remote_copy(
                    src_ref=ins[a].at[2 * k + (1 - c)], dst_ref=outs[a].at[k],
                    send_sem=send_sems.at[a, k], recv_sem=recv_sems.at[a, k],
                    device_id=(x, y, 1 - c), device_id_type=MESH))
        for cp in cps:
            cp.start()
        for cp in cps:
            cp.wait()

    return _sequencer_call(
        body, name, collective_id,
        [jax.ShapeDtypeStruct((4,) + g.shape[1:], g.dtype) for g in grads],
        [pltpu.SemaphoreType.DMA((na, 4)), pltpu.SemaphoreType.DMA((na, 4))])(*grads)


def _row_tile(rows, cols):
    for t in (512, 256, 176, 128, 64, 32, 16):
        if rows % t == 0 and t * cols * 4 <= (2 << 20):
            return t
    raise ValueError((rows, cols))


def _chip_sum(place, g, got, name):
    _, r, c = g.shape
    tm = r

    def body(pos_ref, g_ref, got_ref, o_ref):
        o_ref[...] = (g_ref[...].astype(F32) + got_ref[...].astype(F32)).astype(BF16)

    def chip(j, pos):
        return 2 * (pos[0] ^ jnp.where(j == 1, 0, 1)) + (pos[1] ^ jnp.where(j == 0, 0, 1))

    return pl.pallas_call(
        body, name=name,
        grid_spec=pltpu.PrefetchScalarGridSpec(
            num_scalar_prefetch=1, grid=(3, r // tm),
            in_specs=[pl.BlockSpec((None, tm, c), lambda j, i, pos: (2 * chip(j, pos) + pos[2], i, 0)),
                      pl.BlockSpec((None, tm, c), lambda j, i, pos: (chip(j, pos), i, 0))],
            out_specs=pl.BlockSpec((None, tm, c), lambda j, i, pos: (j, i, 0))),
        out_shape=jax.ShapeDtypeStruct((3, r, c), BF16),
        compiler_params=_cp(("parallel", "parallel")),
    )(place, g, got)


def _exchange_chips(sums, name, collective_id):
    na = len(sums)

    def body(*refs):
        ins, outs = refs[:na], refs[na:2 * na]
        send_sems, recv_sems = refs[2 * na:]
        x, y, c, chips = _place()
        _handshake([(*chip, c) for chip in chips])
        cps = []
        for a in range(na):
            for j, chip in enumerate(chips):
                cps.append(pltpu.make_async_remote_copy(
                    src_ref=ins[a].at[j], dst_ref=outs[a].at[j],
                    send_sem=send_sems.at[a, j], recv_sem=recv_sems.at[a, j],
                    device_id=(*chip, c), device_id_type=MESH))
        for cp in cps:
            cp.start()
        for cp in cps:
            cp.wait()

    return _sequencer_call(
        body, name, collective_id,
        [jax.ShapeDtypeStruct((3,) + s.shape[1:], s.dtype) for s in sums],
        [pltpu.SemaphoreType.DMA((na, 3)), pltpu.SemaphoreType.DMA((na, 3))])(*sums)


def _exchange_stats(stats, collective_id):
    def body(st_in, st_out, st_send, st_recv, local_sem):
        x, y, c, _ = _place()
        me_idx = 4 * x + 2 * y + c
        peers = [(x ^ ((k >> 2) & 1), y ^ ((k >> 1) & 1), c ^ (k & 1)) for k in range(1, 8)]
        _handshake(peers)
        mine = pltpu.make_async_copy(st_in, st_out.at[me_idx], local_sem)
        mine.start()
        cps = [pltpu.make_async_remote_copy(
            src_ref=st_in, dst_ref=st_out.at[me_idx], send_sem=st_send.at[k], recv_sem=st_recv.at[k],
            device_id=peer, device_id_type=MESH) for k, peer in enumerate(peers)]
        for cp in cps:
            cp.start()
        for cp in cps:
            cp.wait()
        mine.wait()

    return _sequencer_call(
        body, "exchange_stats", collective_id,
        jax.ShapeDtypeStruct((NDEV,) + stats.shape, stats.dtype),
        [pltpu.SemaphoreType.DMA((7,)), pltpu.SemaphoreType.DMA((7,)), pltpu.SemaphoreType.DMA])(stats)


class _Reduction:
    def __init__(self, place, first_collective_id, state):
        self.place = place
        self.ids = iter(range(first_collective_id, 32))
        self.state = state
        self.groups = {}
        self.updates = {}

    def next_id(self):
        return next(self.ids)

    def start(self, group, grads):
        got = _exchange_sibling(grads, "sibling_exchange_" + group[0], self.next_id())
        self.groups[group[0]] = dict(names=group, grads=grads, got=got)

    def local(self, name, first=()):
        grp = self.groups[name]
        grads = lax.optimization_barrier((tuple(grp["grads"]), tuple(first)))[0]
        grp["sums"] = [_chip_sum(self.place, g, s, "chip_sum_" + n)
                       for g, s, n in zip(grads, grp["got"], grp["names"])]
        grp["chips"] = _exchange_chips(grp["sums"], "chip_exchange_" + name, self.next_id())
        return grp["sums"]

    def landed(self, name):
        return list(self.groups[name]["chips"])

    def update(self, name):
        if name not in self.updates:
            grp = next(g for g in self.groups.values() if name in g["names"])
            k = grp["names"].index(name)
            w, m, v, part, parts = self.state[name]
            before = self.update(f"{name[:-1]}{part - 1}") if part else None
            self.updates[name] = _shard_update(self.place, w, m, v, grp["grads"][k], grp["got"][k],
                                               grp["chips"][k], "update_" + name, part, parts, before)
        return list(self.updates[name])


def _adamw(w, g, m, v):
    m = ADAM_B1 * m + (1.0 - ADAM_B1) * g
    v = ADAM_B2 * v + (1.0 - ADAM_B2) * (g * g)
    m_hat = m / (1.0 - ADAM_B1 ** ADAM_STEP)
    v_hat = v / (1.0 - ADAM_B2 ** ADAM_STEP)
    delta = -ADAM_LR * (m_hat / (jnp.sqrt(v_hat) + ADAM_EPS) + ADAM_WD * w)
    return delta, m, v


def _shard_update(place, w, m, v, g, got_sib, got_chips, name, part=0, parts=1, before=None):
    r, c = w.shape
    rp = r // parts
    tm = _row_tile(rp, c)
    off = part * (rp // tm)

    def body(pos_ref, w_ref, m_ref, v_ref, g_ref, s_ref, c_ref, *rest):
        go_ref, d_ref, mo_ref, vo_ref = rest[-4:]
        grad = g_ref[...].astype(F32) + s_ref[...].astype(F32)
        for j in range(3):
            grad = grad + c_ref[j].astype(F32)
        delta, mn, vn = _adamw(w_ref[...], grad, m_ref[...], v_ref[...])
        go_ref[...] = grad
        d_ref[...] = delta
        mo_ref[...] = mn
        vo_ref[...] = vn

    row = pl.BlockSpec((tm, c), lambda i, pos: (i + off, 0))
    before = list(before or [])
    return pl.pallas_call(
        body, name=name,
        grid_spec=pltpu.PrefetchScalarGridSpec(
            num_scalar_prefetch=1, grid=(rp // tm,),
            in_specs=[row, row, row,
                      pl.BlockSpec((None, tm, c), lambda i, pos: (4 * pos[0] + 2 * pos[1] + pos[2], i, 0)),
                      pl.BlockSpec((None, tm, c), lambda i, pos: (2 * pos[0] + pos[1], i, 0)),
                      pl.BlockSpec((3, tm, c), lambda i, pos: (0, i, 0))]
            + [pl.BlockSpec(memory_space=pl.ANY)] * len(before),
            out_specs=[row, row, row, row]),
        out_shape=[jax.ShapeDtypeStruct((r, c), F32)] * 4,
        input_output_aliases={7 + k: k for k in range(len(before))},
        compiler_params=_cp(("parallel",)),
    )(place, w, m, v, g, got_sib, got_chips, *before)


def _small_update(stats_all, ws, ms, vs):
    def body(st_ref, w_ref, m_ref, v_ref, go_ref, d_ref, mo_ref, vo_ref):
        grad = st_ref[0]
        for k in range(1, NDEV):
            grad = grad + st_ref[k]
        delta, mn, vn = _adamw(w_ref[...], grad, m_ref[...], v_ref[...])
        go_ref[...] = grad
        d_ref[...] = delta
        mo_ref[...] = mn
        vo_ref[...] = vn

    return pl.pallas_call(
        body, name="small_update",
        out_shape=[jax.ShapeDtypeStruct((8, D), F32)] * 4,
        compiler_params=_cp(),
    )(stats_all, ws, ms, vs)


def kernel(x, norm_mix_w, w_in, w_out, norm_ffn_w, w_gate, w_up, w_down, norm_final_w, loss_target, m_norm_mix_w, m_w_in, m_w_out, m_norm_ffn_w, m_w_gate, m_w_up, m_w_down, m_norm_final_w, v_norm_mix_w, v_w_in, v_w_out, v_norm_ffn_w, v_w_gate, v_w_up, v_w_down, v_norm_final_w):
    tr = {"w_gate", "w_up"}
    names = ["w_in", "w_out", "w_gate", "w_up", "w_down"]

    def view(a, n):
        return a[0].T if n in tr else a[0]

    big_w = [view(a, n) for a, n in zip([w_in, w_out, w_gate, w_up, w_down], names)]
    big_m = [view(a, n) for a, n in zip([m_w_in, m_w_out, m_w_gate, m_w_up, m_w_down], names)]
    big_v = [view(a, n) for a, n in zip([v_w_in, v_w_out, v_w_gate, v_w_up, v_w_down], names)]

    shards = [_cast_bf16(w, "cast_" + n) for w, n in zip(big_w, names)]
    (win,) = _all_gather(shards[0:1], "all_gather_w_in", 1)
    wout, wg, wu = _all_gather(shards[1:4], "all_gather_out_gate_up", 2)
    (wd,) = _all_gather(shards[4:5], "all_gather_w_down", 3)
    nw3 = norm_final_w.reshape(1, D)
    place = jnp.stack([lax.axis_index("x"), lax.axis_index("y"), lax.axis_index("c")]).astype(jnp.int32)
    state = {n: (w, m, v, 0, 1) for n, w, m, v in zip(names, big_w, big_m, big_v)}
    for part in range(W_IN_PARTS):
        state[f"w_in_{part}"] = state["w_in"][:3] + (part, W_IN_PARTS)
    red = _Reduction(place, 4, state)
    stats, gx, *_ = _local_step(
        x[0], loss_target[0], norm_mix_w, norm_ffn_w, nw3, win, wout.reshape(D, D), wg, wu, wd, red)
    stats_all = _exchange_stats(stats, red.next_id())
    upd = [red.update(f"w_in_{W_IN_PARTS - 1}" if n == "w_in" else n) for n in names]
    stats_all = lax.optimization_barrier((stats_all, tuple(upd[0])))[0]

    def rows(a, b, c):
        return jnp.concatenate([a.reshape(1, D), b.reshape(1, D), c.reshape(1, D), jnp.zeros((5, D), F32)], axis=0)

    sg, sd, sm, sv = _small_update(stats_all, rows(norm_mix_w, norm_ffn_w, norm_final_w),
                                   rows(m_norm_mix_w, m_norm_ffn_w, m_norm_final_w),
                                   rows(v_norm_mix_w, v_norm_ffn_w, v_norm_final_w))
    loss = sg[3, 0]

    def outs(k, small):
        big = [(u[k].T if n in tr else u[k])[None] for u, n in zip(upd, names)]
        return [small[0:1], big[0], big[1], small[1:2], big[2], big[3], big[4], small[2]]

    return (loss, gx[None], *outs(0, sg), *outs(1, sd), *outs(2, sm), *outs(3, sv))
```

```python
import functools
import math

import numpy as np
import jax
import jax.numpy as jnp
from jax import lax
from jax.experimental import pallas as pl
from jax.experimental.pallas import tpu as pltpu
from jax.experimental.pallas import tpu_sc as plsc

F32 = jnp.float32
BF16 = jnp.bfloat16

S = 2048
D = 2048
NDEV = 8
N_IN = 7168 // NDEV
N_FF = 5632 // NDEV
NFG, N_FG = NDEV // 2, 2 * N_FF
N_OUT = 2048 // NDEV
AH, AHD = 8, 128
RH, RHD = 4, 256
CH = 128
NB = S // CH
EPS = 1e-6
PATTERNS = ((1, 16), (4, 4), (16, 1))
NEG = -1e30
VMEM_LIMIT = 56 * 1024 * 1024

ADAM_LR, ADAM_B1, ADAM_B2, ADAM_EPS, ADAM_WD, ADAM_STEP = 0.001, 0.9, 0.999, 1e-08, 0.01, 10
MESH = pl.DeviceIdType.MESH


def _cp(sem=None):
    return pltpu.CompilerParams(dimension_semantics=sem, vmem_limit_bytes=VMEM_LIMIT)


def _dot(a, b):
    return jnp.dot(a, b, preferred_element_type=F32)


def _dot_nt(a, b):
    return lax.dot_general(a, b, (((1,), (1,)), ((), ())), preferred_element_type=F32)


def _dot_tn(a, b):
    return lax.dot_general(a, b, (((0,), (0,)), ((), ())), preferred_element_type=F32)


def _sigmoid(x):
    return 0.5 * jnp.tanh(0.5 * x) + 0.5


def _cast_bf16(w, name):
    r, c = w.shape
    tm = r if r <= 1024 else 512

    def body(w_ref, o_ref):
        o_ref[...] = w_ref[...].astype(BF16)

    return pl.pallas_call(
        body, name=name, grid=(r // tm,),
        in_specs=[pl.BlockSpec((tm, c), lambda i: (i, 0))],
        out_specs=pl.BlockSpec((tm, c), lambda i: (i, 0)),
        out_shape=jax.ShapeDtypeStruct((r, c), BF16),
        compiler_params=_cp(("parallel",)),
    )(w)


def _rms_fwd(x, nw):
    tm = 256

    def body(x_ref, w_ref, h_ref, r_ref):
        xs = x_ref[...]
        r = lax.rsqrt(jnp.mean(xs * xs, axis=-1, keepdims=True) + EPS)
        h_ref[...] = ((xs * r) * w_ref[...]).astype(BF16)
        r_ref[...] = r

    return pl.pallas_call(
        body, name="rms_fwd", grid=(S // tm,),
        in_specs=[pl.BlockSpec((tm, D), lambda i: (i, 0)), pl.BlockSpec((1, D), lambda i: (0, 0))],
        out_specs=[pl.BlockSpec((tm, D), lambda i: (i, 0)), pl.BlockSpec((tm, 1), lambda i: (i, 0))],
        out_shape=[jax.ShapeDtypeStruct((S, D), BF16), jax.ShapeDtypeStruct((S, 1), F32)],
        compiler_params=_cp(("parallel",)),
    )(x, nw)


def _row_copies(hbm_refs, bufs, sems, m, tm):
    rows = pl.ds(pl.multiple_of(m * tm, tm), tm)
    return [pltpu.make_async_copy(h.at[rows], b, sems.at[i]) for i, (h, b) in enumerate(zip(hbm_refs, bufs))]


def _rms_bwd_tile(dh, xs, r, nw):
    dnw = jnp.sum(dh * (xs * r), axis=0, keepdims=True)
    gy = dh * nw
    dx = r * gy - xs * ((r * r * r) * jnp.mean(gy * xs, axis=-1, keepdims=True))
    return dx, dnw


def _proj(h1, win):
    tm = 1024

    def body(a_ref, w_ref, o_ref):
        o_ref[...] = _dot(a_ref[...], w_ref[...])

    return pl.pallas_call(
        body, name="proj", grid=(NDEV, S // tm),
        in_specs=[pl.BlockSpec((tm, D), lambda p, m: (m, 0)),
                  pl.BlockSpec((None, D, N_IN), lambda p, m: (p, 0, 0))],
        out_specs=pl.BlockSpec((tm, N_IN), lambda p, m: (m, p)),
        out_shape=jax.ShapeDtypeStruct((S, NDEV * N_IN), F32),
        compiler_params=_cp(("parallel", "parallel")),
    )(h1, win)


def _out_proj_rms(x, ma, mr, wout, nw):
    tm = 256
    half = D // 2

    def body(x_ref, ma_ref, mr_ref, w_ref, nw_ref, x2_ref, h_ref, r_ref):
        acc = _dot(ma_ref[...], w_ref[0:half, :]) + _dot(mr_ref[...], w_ref[half:D, :])
        x2 = x_ref[...] + acc
        r = lax.rsqrt(jnp.mean(x2 * x2, axis=-1, keepdims=True) + EPS)
        x2_ref[...] = x2
        h_ref[...] = ((x2 * r) * nw_ref[...]).astype(BF16)
        r_ref[...] = r

    return pl.pallas_call(
        body, name="out_proj_rms", grid=(S // tm,),
        in_specs=[pl.BlockSpec((tm, D), lambda i: (i, 0)),
                  pl.BlockSpec((tm, half), lambda i: (i, 0)),
                  pl.BlockSpec((tm, half), lambda i: (i, 0)),
                  pl.BlockSpec((D, D), lambda i: (0, 0)),
                  pl.BlockSpec((1, D), lambda i: (0, 0))],
        out_specs=[pl.BlockSpec((tm, D), lambda i: (i, 0)), pl.BlockSpec((tm, D), lambda i: (i, 0)),
                   pl.BlockSpec((tm, 1), lambda i: (i, 0))],
        out_shape=[jax.ShapeDtypeStruct((S, D), F32), jax.ShapeDtypeStruct((S, D), BF16),
                   jax.ShapeDtypeStruct((S, 1), F32)],
        compiler_params=_cp(("parallel",)),
    )(x, ma, mr, wout, nw)


def _ffn_up(h2, wg, wu):
    tm = 512

    def body(h_ref, wg_ref, wu_ref, a_ref, dadg_ref, dadu_ref):
        h = h_ref[...]
        g = _dot_nt(h, wg_ref[...])
        u = _dot_nt(h, wu_ref[...])
        sg = _sigmoid(g)
        silu = g * sg
        a_ref[...] = (silu * u).astype(BF16)
        dadg_ref[...] = (u * (sg * (1.0 + g * (1.0 - sg)))).astype(BF16)
        dadu_ref[...] = silu.astype(BF16)

    blk = pl.BlockSpec((None, tm, N_FG), lambda p, m: (p, m, 0))
    wblk = pl.BlockSpec((None, N_FG, D), lambda p, m: (p, 0, 0))
    return pl.pallas_call(
        body, name="ffn_up", grid=(NFG, S // tm),
        in_specs=[pl.BlockSpec((tm, D), lambda p, m: (m, 0)), wblk, wblk],
        out_specs=[blk, blk, blk],
        out_shape=[jax.ShapeDtypeStruct((NFG, S, N_FG), BF16)] * 3,
        compiler_params=_cp(("parallel", "parallel")),
    )(h2, wg, wu)


def _ffn_down_loss(x2, a, wd, nw, tgt):
    tm = 512

    def body(x2_hbm, a_ref, w_ref, nw_ref, t_hbm, dx_ref, dxb_ref, st_ref, acc_ref, x2_buf, t_buf, sems):
        m, p = pl.program_id(0), pl.program_id(1)
        tail_in = _row_copies((x2_hbm, t_hbm), (x2_buf, t_buf), sems, m, tm)

        @pl.when(p == 0)
        def _():
            acc_ref[...] = jnp.zeros_like(acc_ref)
            for cp in tail_in:
                cp.start()

        @pl.when((p == 0) & (m == 0))
        def _():
            st_ref[...] = jnp.zeros_like(st_ref)

        acc_ref[...] += _dot(a_ref[...], w_ref[...])

        @pl.when(p == NFG - 1)
        def _():
            for cp in tail_in:
                cp.wait()
            x3 = x2_buf[...] + acc_ref[...]
            nwv = nw_ref[...]
            r = lax.rsqrt(jnp.mean(x3 * x3, axis=-1, keepdims=True) + EPS)
            y = (x3 * r) * nwv
            err = y - t_buf[...]
            loss = 0.5 * jnp.sum(jnp.mean(err * err, axis=-1, keepdims=True), axis=0, keepdims=True)
            dy = err * (1.0 / D)
            dx, dnw = _rms_bwd_tile(dy, x3, r, nwv)
            dx_ref[...] = dx
            dxb_ref[...] = dx.astype(BF16)
            st_ref[0:1, :] += dnw
            st_ref[1:2, :] += jnp.broadcast_to(loss, (1, D))

    return pl.pallas_call(
        body, name="ffn_down_loss", grid=(S // tm, NFG),
        in_specs=[pl.BlockSpec(memory_space=pl.ANY),
                  pl.BlockSpec((None, tm, N_FG), lambda m, p: (p, m, 0)),
                  pl.BlockSpec((None, N_FG, D), lambda m, p: (p, 0, 0)),
                  pl.BlockSpec((1, D), lambda m, p: (0, 0)),
                  pl.BlockSpec(memory_space=pl.ANY)],
        out_specs=[pl.BlockSpec((tm, D), lambda m, p: (m, 0)), pl.BlockSpec((tm, D), lambda m, p: (m, 0)),
                   pl.BlockSpec((8, D), lambda m, p: (0, 0))],
        out_shape=[jax.ShapeDtypeStruct((S, D), F32), jax.ShapeDtypeStruct((S, D), BF16),
                   jax.ShapeDtypeStruct((8, D), F32)],
        scratch_shapes=[pltpu.VMEM((tm, D), F32), pltpu.VMEM((tm, D), F32), pltpu.VMEM((tm, D), F32),
                        pltpu.SemaphoreType.DMA((2,))],
        compiler_params=_cp(("arbitrary", "arbitrary")),
    )(x2, a, wd, nw, tgt)


def _ffn_down_bwd(dx3b, wd, dadg, dadu):
    tm = 1024

    def body(dx_ref, w_ref, dadg_ref, dadu_ref, dg_ref, du_ref):
        da = _dot_nt(dx_ref[...], w_ref[...])
        dg_ref[...] = (da * dadg_ref[...].astype(F32)).astype(BF16)
        du_ref[...] = (da * dadu_ref[...].astype(F32)).astype(BF16)

    blk = pl.BlockSpec((None, tm, N_FG), lambda p, m: (p, m, 0))
    return pl.pallas_call(
        body, name="ffn_down_bwd", grid=(NFG, S // tm),
        in_specs=[pl.BlockSpec((tm, D), lambda p, m: (m, 0)),
                  pl.BlockSpec((None, N_FG, D), lambda p, m: (p, 0, 0)), blk, blk],
        out_specs=[blk, blk],
        out_shape=[jax.ShapeDtypeStruct((NFG, S, N_FG), BF16)] * 2,
        compiler_params=_cp(("parallel", "parallel")),
    )(dx3b, wd, dadg, dadu)


def _ffn_up_bwd(dg, du, wg, wu, dres, xs, r, nw):
    tm = 512

    def body(dg_ref, du_ref, wg_ref, wu_ref, dres_hbm, x_hbm, r_ref, nw_ref, dx_ref, dxb_ref, st_ref,
             dres_buf, x_buf, sems):
        m, p = pl.program_id(0), pl.program_id(1)
        tail_in = _row_copies((dres_hbm, x_hbm), (dres_buf, x_buf), sems, m, tm)

        @pl.when(p == 0)
        def _():
            dx_ref[...] = jnp.zeros_like(dx_ref)
            for cp in tail_in:
                cp.start()

        @pl.when((p == 0) & (m == 0))
        def _():
            st_ref[...] = jnp.zeros_like(st_ref)

        dx_ref[...] += _dot(dg_ref[...], wg_ref[...])
        dx_ref[...] += _dot(du_ref[...], wu_ref[...])

        @pl.when(p == NFG - 1)
        def _():
            for cp in tail_in:
                cp.wait()
            dx, dnw = _rms_bwd_tile(dx_ref[...], x_buf[...], r_ref[...], nw_ref[...])
            dx = dres_buf[...] + dx
            dx_ref[...] = dx
            dxb_ref[...] = dx.astype(BF16)
            st_ref[0:1, :] += dnw

    blk = pl.BlockSpec((None, tm, N_FG), lambda m, p: (p, m, 0))
    wblk = pl.BlockSpec((None, N_FG, D), lambda m, p: (p, 0, 0))
    row = pl.BlockSpec((tm, D), lambda m, p: (m, 0))
    hbm = pl.BlockSpec(memory_space=pl.ANY)
    return pl.pallas_call(
        body, name="ffn_up_bwd", grid=(S // tm, NFG),
        in_specs=[blk, blk, wblk, wblk, hbm, hbm, pl.BlockSpec((tm, 1), lambda m, p: (m, 0)),
                  pl.BlockSpec((1, D), lambda m, p: (0, 0))],
        out_specs=[row, row, pl.BlockSpec((8, D), lambda m, p: (0, 0))],
        out_shape=[jax.ShapeDtypeStruct((S, D), F32), jax.ShapeDtypeStruct((S, D), BF16),
                   jax.ShapeDtypeStruct((8, D), F32)],
        scratch_shapes=[pltpu.VMEM((tm, D), F32), pltpu.VMEM((tm, D), F32), pltpu.SemaphoreType.DMA((2,))],
        compiler_params=_cp(("arbitrary", "arbitrary")),
    )(dg, du, wg, wu, dres, xs, r, nw)


def _out_proj_bwd(dx2b, wout):
    tm = 256

    def body(dx_ref, w_ref, o_ref):
        o_ref[...] = _dot_nt(dx_ref[...], w_ref[...])

    return pl.pallas_call(
        body, name="out_proj_bwd", grid=(S // tm,),
        in_specs=[pl.BlockSpec((tm, D), lambda i: (i, 0)), pl.BlockSpec((D, D), lambda i: (0, 0))],
        out_specs=pl.BlockSpec((tm, D), lambda i: (i, 0)),
        out_shape=jax.ShapeDtypeStruct((S, D), F32),
        compiler_params=_cp(("parallel",)),
    )(dx2b, wout)


def _in_proj_bwd(dproj, win, dres, xs, r, nw):
    tm = 1024

    def body(dp_ref, w_ref, dres_hbm, x_hbm, r_ref, nw_ref, dx_ref, st_ref, dres_buf, x_buf, sems):
        m, p = pl.program_id(0), pl.program_id(1)
        tail_in = _row_copies((dres_hbm, x_hbm), (dres_buf, x_buf), sems, m, tm)

        @pl.when(p == 0)
        def _():
            dx_ref[...] = jnp.zeros_like(dx_ref)
            for cp in tail_in:
                cp.start()

        @pl.when((p == 0) & (m == 0))
        def _():
            st_ref[...] = jnp.zeros_like(st_ref)

        dx_ref[...] += _dot_nt(dp_ref[...], w_ref[...])

        @pl.when(p == NDEV - 1)
        def _():
            for cp in tail_in:
                cp.wait()
            dx, dnw = _rms_bwd_tile(dx_ref[...], x_buf[...], r_ref[...], nw_ref[...])
            dx_ref[...] = dres_buf[...] + dx
            st_ref[0:1, :] += dnw

    row = pl.BlockSpec((tm, D), lambda m, p: (m, 0))
    hbm = pl.BlockSpec(memory_space=pl.ANY)
    return pl.pallas_call(
        body, name="in_proj_bwd", grid=(S // tm, NDEV),
        in_specs=[pl.BlockSpec((tm, N_IN), lambda m, p: (m, p)),
                  pl.BlockSpec((None, D, N_IN), lambda m, p: (p, 0, 0)),
                  hbm, hbm, pl.BlockSpec((tm, 1), lambda m, p: (m, 0)),
                  pl.BlockSpec((1, D), lambda m, p: (0, 0))],
        out_specs=[row, pl.BlockSpec((8, D), lambda m, p: (0, 0))],
        out_shape=[jax.ShapeDtypeStruct((S, D), F32), jax.ShapeDtypeStruct((8, D), F32)],
        scratch_shapes=[pltpu.VMEM((tm, D), F32), pltpu.VMEM((tm, D), F32), pltpu.SemaphoreType.DMA((2,))],
        compiler_params=_cp(("arbitrary", "arbitrary")),
    )(dproj, win, dres, xs, r, nw)


W_IN_PARTS = 2


def _wgrad_in(h1, dproj, part):
    rows = D // W_IN_PARTS

    def body(a_ref, d_ref, o_ref):
        o_ref[...] = _dot_tn(a_ref[...], d_ref[...]).astype(BF16)

    return pl.pallas_call(
        body, name=f"wgrad_in_{part}", grid=(NDEV,),
        in_specs=[pl.BlockSpec((S, rows), lambda p: (0, part)), pl.BlockSpec((S, N_IN), lambda p: (0, p))],
        out_specs=pl.BlockSpec((None, rows, N_IN), lambda p: (p, 0, 0)),
        out_shape=jax.ShapeDtypeStruct((NDEV, rows, N_IN), BF16),
        compiler_params=_cp(("parallel",)),
    )(h1, dproj)


def _wgrad_rows(a3, dy, name):
    def body(a_ref, d_ref, o_ref):
        o_ref[...] = _dot_tn(a_ref[...], d_ref[...]).astype(BF16)

    return pl.pallas_call(
        body, name=name, grid=(NFG,),
        in_specs=[pl.BlockSpec((None, S, N_FG), lambda p: (p, 0, 0)), pl.BlockSpec((S, D), lambda p: (0, 0))],
        out_specs=pl.BlockSpec((None, N_FG, D), lambda p: (p, 0, 0)),
        out_shape=jax.ShapeDtypeStruct((NFG, N_FG, D), BF16),
        compiler_params=_cp(("parallel",)),
    )(a3, dy).reshape(NDEV, N_FF, D)


def _wgrad_out(ma, mr, dx2b):
    half = D // 2
    per = half // N_OUT

    def body(ma_ref, mr_ref, d_ref, o_ref):
        p = pl.program_id(0)

        @pl.when(p < per)
        def _():
            o_ref[...] = _dot_tn(ma_ref[...], d_ref[...]).astype(BF16)

        @pl.when(p >= per)
        def _():
            o_ref[...] = _dot_tn(mr_ref[...], d_ref[...]).astype(BF16)

    return pl.pallas_call(
        body, name="wgrad_out", grid=(NDEV,),
        in_specs=[pl.BlockSpec((S, N_OUT), lambda p: (0, jnp.minimum(p, per - 1))),
                  pl.BlockSpec((S, N_OUT), lambda p: (0, jnp.maximum(p - per, 0))),
                  pl.BlockSpec((S, D), lambda p: (0, 0))],
        out_specs=pl.BlockSpec((None, N_OUT, D), lambda p: (p, 0, 0)),
        out_shape=jax.ShapeDtypeStruct((NDEV, N_OUT, D), BF16),
        compiler_params=_cp(("parallel",)),
    )(ma, mr, dx2b)


def _attn_consts():
    c = np.zeros((AH, 8, AHD), np.float32)
    for h in range(AH):
        c[h, :, :] = 2.0 ** (-(h + 1))
    return jnp.asarray(c)


def _permute_in(dst, src, d, cast=None):
    v = src[...]
    if d > 1:
        v = pltpu.einshape("jrc->rjc", v.reshape(S // d, d, AHD)).reshape(S, AHD)
    dst[...] = v if cast is None else v.astype(cast)


def _natural_order(v, d):
    if d == 1:
        return v
    return pltpu.einshape("rjc->jrc", v.reshape(d, S // d, AHD)).reshape(S, AHD)


def _attn_masks():
    qi = lax.broadcasted_iota(jnp.int32, (CH, CH), 0)
    kj = lax.broadcasted_iota(jnp.int32, (CH, CH), 1)
    dist_c = (qi - kj).astype(F32)
    dist_p = (qi - kj + CH).astype(F32)
    return (qi >= kj)[None], (kj >= qi)[None], dist_c[None], dist_p[None]


GB = 16


def _bdot_nt(a, b):
    return lax.dot_general(a, b, (((2,), (2,)), ((0,), (0,))), preferred_element_type=F32)


def _bdot(a, b):
    return lax.dot_general(a, b, (((2,), (1,)), ((0,), (0,))), preferred_element_type=F32)


def _bdot_tn(a, b):
    return lax.dot_general(a, b, (((1,), (1,)), ((0,), (0,))), preferred_element_type=F32)


def _shift_block(dst, src):
    dst[0:CH, :] = jnp.zeros((CH, AHD), dst.dtype)
    dst[CH:S, :] = src[0:S - CH, :]


def _has_prev(g, nb):
    blk = lax.broadcasted_iota(jnp.int32, (GB, 1, 1), 0) + g * GB
    return (blk & (nb - 1)) != 0


def _blocks(ref, g):
    return ref[g * GB * CH:(g + 1) * GB * CH, :].reshape(GB, CH, AHD)


def _attn_fwd(proj):
    scale = 1.0 / math.sqrt(AHD)

    def body(c_ref, q_ref, k_ref, v_ref, o_ref, ob_ref, lse_ref, qd, kd, vd, kps, vps, od, ld, *nat):
        onat, lnat = nat[0:3], nat[3:6]
        slope = c_ref[0:1, :]
        mask_c, mask_p, dist_c, dist_p = _attn_masks()
        for pi, (d, nb) in enumerate(PATTERNS):
            _permute_in(qd, q_ref, d, BF16)
            _permute_in(kd, k_ref, d, BF16)
            _permute_in(vd, v_ref, d, BF16)
            if nb > 1:
                _shift_block(kps, kd)
                _shift_block(vps, vd)
            bias_c = -(slope * float(d)) * dist_c
            bias_p = -(slope * float(d)) * dist_p
            for g in range(NB // GB):
                q3, k3, v3 = _blocks(qd, g), _blocks(kd, g), _blocks(vd, g)
                s_c = jnp.where(mask_c, _bdot_nt(q3, k3) * scale + bias_c, NEG)
                mx = jnp.max(s_c, axis=-1, keepdims=True)
                if nb > 1:
                    kp3, vp3 = _blocks(kps, g), _blocks(vps, g)
                    s_p = jnp.where(jnp.logical_and(mask_p, _has_prev(g, nb)),
                                    _bdot_nt(q3, kp3) * scale + bias_p, NEG)
                    mx = jnp.maximum(mx, jnp.max(s_p, axis=-1, keepdims=True))
                    l = (jnp.sum(jnp.exp(s_c - mx), axis=-1, keepdims=True)
                         + jnp.sum(jnp.exp(s_p - mx), axis=-1, keepdims=True))
                    lse = mx + jnp.log(l)
                    o3 = _bdot(jnp.exp(s_c - lse).astype(BF16), v3) + _bdot(jnp.exp(s_p - lse).astype(BF16), vp3)
                else:
                    l = jnp.sum(jnp.exp(s_c - mx), axis=-1, keepdims=True)
                    lse = mx + jnp.log(l)
                    o3 = _bdot(jnp.exp(s_c - lse).astype(BF16), v3)
                rows = slice(g * GB * CH, (g + 1) * GB * CH)
                od[rows, :] = o3.reshape(GB * CH, AHD)
                ld[rows, :] = jnp.broadcast_to(lse, (GB, CH, AHD)).reshape(GB * CH, AHD)
            onat[pi][...] = _natural_order(od[...], d)
            lnat[pi][...] = _natural_order(ld[...], d)
        l0, l1, l2 = lnat[0][...], lnat[1][...], lnat[2][...]
        mx = jnp.maximum(jnp.maximum(l0, l1), l2)
        e0, e1, e2 = jnp.exp(l0 - mx), jnp.exp(l1 - mx), jnp.exp(l2 - mx)
        den = e0 + e1 + e2
        out = (e0 / den) * onat[0][...] + (e1 / den) * onat[1][...] + (e2 / den) * onat[2][...]
        o_ref[...] = out
        ob_ref[...] = out.astype(BF16)
        lse_ref[...] = mx + jnp.log(den)

    def col(off):
        return pl.BlockSpec((S, AHD), lambda h: (0, off + h))

    return pl.pallas_call(
        body, name="attn_fwd", grid=(AH,),
        in_specs=[pl.BlockSpec((None, 8, AHD), lambda h: (h, 0, 0)), col(0), col(AH), col(2 * AH)],
        out_specs=[col(0), col(0), col(0)],
        out_shape=[jax.ShapeDtypeStruct((S, AH * AHD), F32), jax.ShapeDtypeStruct((S, AH * AHD), BF16),
                   jax.ShapeDtypeStruct((S, AH * AHD), F32)],
        scratch_shapes=[pltpu.VMEM((S, AHD), BF16) for _ in range(5)]
        + [pltpu.VMEM((S, AHD), F32) for _ in range(8)],
        compiler_params=_cp(("parallel",)),
    )(_attn_consts(), proj, proj, proj)


def _attn_bwd(proj, dmixed, o, lse):
    scale = 1.0 / math.sqrt(AHD)

    def body(c_ref, q_ref, k_ref, v_ref, do_ref, o_ref, lse_ref, dq_ref, dk_ref, dv_ref,
             qd, kd, vd, dod, kps, vps, lsd, dld, dqd, dkd, dvd, delta, aq, ak, av):
        slope = c_ref[0:1, :]
        mask_c, mask_p, dist_c, dist_p = _attn_masks()
        delta[...] = jnp.broadcast_to(jnp.sum(do_ref[...] * o_ref[...], axis=-1, keepdims=True), (S, AHD))
        for pi, (d, nb) in enumerate(PATTERNS):
            _permute_in(qd, q_ref, d, BF16)
            _permute_in(kd, k_ref, d, BF16)
            _permute_in(vd, v_ref, d, BF16)
            _permute_in(dod, do_ref, d, BF16)
            _permute_in(lsd, lse_ref, d)
            _permute_in(dld, delta, d)
            if nb > 1:
                _shift_block(kps, kd)
                _shift_block(vps, vd)
            bias_c = -(slope * float(d)) * dist_c
            bias_p = -(slope * float(d)) * dist_p
            for g in range(NB // GB):
                q3, k3, v3, do3 = _blocks(qd, g), _blocks(kd, g), _blocks(vd, g), _blocks(dod, g)
                ls, dl = _blocks(lsd, g), _blocks(dld, g)
                lo, hi = g * GB * CH, (g + 1) * GB * CH
                p_c = jnp.exp(jnp.where(mask_c, _bdot_nt(q3, k3) * scale + bias_c, NEG) - ls)
                ds_c = ((p_c * (_bdot_nt(do3, v3) - dl)) * scale).astype(BF16)
                dq3 = _bdot(ds_c, k3)
                dkd[lo:hi, :] = _bdot_tn(ds_c, q3).reshape(GB * CH, AHD)
                dvd[lo:hi, :] = _bdot_tn(p_c.astype(BF16), do3).reshape(GB * CH, AHD)
                if nb > 1:
                    kp3, vp3 = _blocks(kps, g), _blocks(vps, g)
                    p_p = jnp.exp(jnp.where(jnp.logical_and(mask_p, _has_prev(g, nb)),
                                            _bdot_nt(q3, kp3) * scale + bias_p, NEG) - ls)
                    ds_p = ((p_p * (_bdot_nt(do3, vp3) - dl)) * scale).astype(BF16)
                    dq3 = dq3 + _bdot(ds_p, kp3)
                    dkp = _bdot_tn(ds_p, q3).reshape(GB * CH, AHD)
                    dvp = _bdot_tn(p_p.astype(BF16), do3).reshape(GB * CH, AHD)
                    if g == 0:
                        dkd[0:hi - CH, :] += dkp[CH:, :]
                        dvd[0:hi - CH, :] += dvp[CH:, :]
                    else:
                        dkd[lo - CH:hi - CH, :] += dkp
                        dvd[lo - CH:hi - CH, :] += dvp
                dqd[lo:hi, :] = dq3.reshape(GB * CH, AHD)
            ln = S // d
            for acc, src in ((aq, dqd), (ak, dkd), (av, dvd)):
                if pi == 0:
                    acc[...] = src[...]
                else:
                    acc[...] += _natural_order(src[...], d)
        dq_ref[...] = aq[...].astype(BF16)
        dk_ref[...] = ak[...].astype(BF16)
        dv_ref[...] = av[...].astype(BF16)

    def col(off):
        return pl.BlockSpec((S, AHD), lambda h: (0, off + h))

    return pl.pallas_call(
        body, name="attn_bwd", grid=(AH,),
        in_specs=[pl.BlockSpec((None, 8, AHD), lambda h: (h, 0, 0)), col(0), col(AH), col(2 * AH),
                  col(0), col(0), col(0)],
        out_specs=[col(0), col(0), col(0)],
        out_shape=[jax.ShapeDtypeStruct((S, AH * AHD), BF16)] * 3,
        scratch_shapes=[pltpu.VMEM((S, AHD), BF16) for _ in range(6)]
        + [pltpu.VMEM((S, AHD), F32) for _ in range(9)],
        compiler_params=_cp(("parallel",)),
    )(_attn_consts(), proj, proj, proj, dmixed, o, lse)


def _ret_consts():
    c = np.zeros((RH, 8, RHD), np.float32)
    for h in range(RH):
        c[h, :, :] = np.log(np.float32(1.0) - np.float32(2.0 ** (-5.0 - h)))
    return jnp.asarray(c)


def _ret_factors(lg):
    i = lax.broadcasted_iota(jnp.int32, (CH, CH), 0)
    j = lax.broadcasted_iota(jnp.int32, (CH, CH), 1)
    dif = (i - j).astype(F32)
    decay = jnp.where(dif >= 0, jnp.exp(lg[:, 0:CH] * jnp.maximum(dif, 0.0)), 0.0)
    row = lax.broadcasted_iota(jnp.int32, (CH, RHD), 0).astype(F32)
    zeta = jnp.exp(lg * (CH - 1.0 - row))
    xi = jnp.exp(lg * (row + 1.0))
    return decay, zeta, xi, jnp.exp(lg * float(CH))


CBK = 8
RSTEPS = NB // CBK


def _ret_specs(rev):
    off = 3 * AH * AHD // RHD
    rows = CBK * CH

    def ch(n):
        return (RSTEPS - 1 - n) if rev else n

    def col(k):
        return pl.BlockSpec((rows, RHD), lambda h, n: (ch(n), off + k * RH + h))

    own = pl.BlockSpec((rows, RHD), lambda h, n: (ch(n), h))
    state = pl.BlockSpec((None, CBK, RHD, RHD), lambda h, n: (h, ch(n), 0, 0))
    const = pl.BlockSpec((None, 8, RHD), lambda h, n: (h, 0, 0))
    dm = pl.BlockSpec((rows, RHD), lambda h, n: (ch(n), AH * AHD // RHD + h))
    return col, own, state, const, dm


def _chunks(x):
    return x.reshape(CBK, CH, RHD)


def _ret_fwd(proj):
    def body(c_ref, q_ref, k_ref, v_ref, g_ref, ret_ref, mr_ref, st_ref, r_acc):
        n = pl.program_id(1)

        @pl.when(n == 0)
        def _():
            r_acc[...] = jnp.zeros_like(r_acc)

        decay, zeta, xi, gch = _ret_factors(c_ref[0:1, :])
        q3 = _chunks(q_ref[...].astype(BF16))
        kc = _chunks(k_ref[...] * (1.0 / math.sqrt(RHD)))
        k3 = kc.astype(BF16)
        v3 = _chunks(v_ref[...].astype(BF16))
        kv3 = _bdot_tn((kc * zeta[None]).astype(BF16), v3)
        r = r_acc[...]
        for i in range(CBK):
            st_ref[i] = r.astype(BF16)
            r = r * gch + kv3[i]
        r_acc[...] = r
        scores = _bdot_nt(q3, k3) * decay[None]
        ret = (_bdot(scores.astype(BF16), v3) + _bdot(q3, st_ref[...]) * xi[None]).reshape(CBK * CH, RHD)
        ret_ref[...] = ret
        rr = lax.rsqrt(jnp.mean(ret * ret, axis=-1, keepdims=True) + EPS)
        gv = g_ref[...]
        mr_ref[...] = ((gv * _sigmoid(gv)) * (ret * rr)).astype(BF16)

    col, own, state, const, _ = _ret_specs(False)
    return pl.pallas_call(
        body, name="ret_fwd", grid=(RH, RSTEPS),
        in_specs=[const, col(0), col(1), col(2), col(3)],
        out_specs=[own, own, state],
        out_shape=[jax.ShapeDtypeStruct((S, RH * RHD), F32), jax.ShapeDtypeStruct((S, RH * RHD), BF16),
                   jax.ShapeDtypeStruct((RH, NB, RHD, RHD), BF16)],
        scratch_shapes=[pltpu.VMEM((RHD, RHD), F32)],
        compiler_params=_cp(("parallel", "arbitrary")),
    )(_ret_consts(), proj, proj, proj, proj)


def _ret_bwd(proj, ret, states, dmixed):
    def body(c_ref, q_ref, k_ref, v_ref, g_ref, ret_ref, st_ref, dm_ref, dq_ref, dk_ref, dv_ref, dg_ref, g_acc, gs):
        n = pl.program_id(1)

        @pl.when(n == 0)
        def _():
            g_acc[...] = jnp.zeros_like(g_acc)

        decay, zeta, xi, gch = _ret_factors(c_ref[0:1, :])
        ret_v = ret_ref[...]
        rr = lax.rsqrt(jnp.mean(ret_v * ret_v, axis=-1, keepdims=True) + EPS)
        gv = g_ref[...]
        sg = _sigmoid(gv)
        dmix = dm_ref[...]
        dg_ref[...] = ((dmix * (ret_v * rr)) * (sg * (1.0 + gv * (1.0 - sg)))).astype(BF16)
        dretn = dmix * (gv * sg)
        dret = _chunks(rr * dretn - ret_v * ((rr * rr * rr) * jnp.mean(dretn * ret_v, axis=-1, keepdims=True)))

        q3 = _chunks(q_ref[...].astype(BF16))
        kc = _chunks(k_ref[...] * (1.0 / math.sqrt(RHD)))
        k3 = kc.astype(BF16)
        v3 = _chunks(v_ref[...].astype(BF16))
        d3 = dret.astype(BF16)
        dxi = (dret * xi[None]).astype(BF16)
        kz = (kc * zeta[None]).astype(BF16)
        dr3 = _bdot_tn(q3, dxi)
        acc = g_acc[...]
        for i in reversed(range(CBK)):
            gs[i] = acc.astype(BF16)
            acc = dr3[i] + gch * acc
        g_acc[...] = acc
        g3 = gs[...]
        sc = (_bdot_nt(q3, k3) * decay[None]).astype(BF16)
        da = (_bdot_nt(d3, v3) * decay[None]).astype(BF16)
        dq = _bdot(da, k3) + _bdot_nt(dxi, st_ref[...])
        dkc = _bdot_tn(da, q3) + _bdot_nt(v3, g3) * zeta[None]
        dv = _bdot_tn(sc, d3) + _bdot(kz, g3)
        dq_ref[...] = dq.reshape(CBK * CH, RHD).astype(BF16)
        dk_ref[...] = (dkc * (1.0 / math.sqrt(RHD))).reshape(CBK * CH, RHD).astype(BF16)
        dv_ref[...] = dv.reshape(CBK * CH, RHD).astype(BF16)

    col, own, state, const, dm = _ret_specs(True)
    return pl.pallas_call(
        body, name="ret_bwd", grid=(RH, RSTEPS),
        in_specs=[const, col(0), col(1), col(2), col(3), own, state, dm],
        out_specs=[own, own, own, own],
        out_shape=[jax.ShapeDtypeStruct((S, RH * RHD), BF16)] * 4,
        scratch_shapes=[pltpu.VMEM((RHD, RHD), F32), pltpu.VMEM((CBK, RHD, RHD), BF16)],
        compiler_params=_cp(("parallel", "arbitrary")),
    )(_ret_consts(), proj, proj, proj, proj, ret, states, dmixed)


class _NoReduction:
    def start(self, group, grads):
        pass

    def local(self, name, first=()):
        return []

    def landed(self, name):
        return []

    def update(self, name):
        return []


def _local_step(x, tgt, nw1, nw2, nw3, win, wout, wg, wu, wd, red=None):
    red = red or _NoReduction()

    def after(values, first):
        return lax.optimization_barrier((tuple(values), tuple(first)))[0]

    wg, wu, wd = (w.reshape(NFG, N_FG, D) for w in (wg, wu, wd))
    h1, r1 = _rms_fwd(x, nw1)
    proj = _proj(h1, win)
    o, ma, lse = _attn_fwd(proj)
    ret, mr, states = _ret_fwd(proj)
    x2, h2, r2 = _out_proj_rms(x, ma, mr, wout, nw2)
    a, dadg, dadu = _ffn_up(h2, wg, wu)
    dx3, dx3b, st3 = _ffn_down_loss(x2, a, wd, nw3, tgt)

    dwd = _wgrad_rows(a, dx3b, "wgrad_down")
    red.start(["w_down"], [dwd])
    (dx3b,) = after([dx3b], [dwd])
    dg, du = _ffn_down_bwd(dx3b, wd, dadg, dadu)
    dg, du = after([dg, du], red.local("w_down", first=[dg]))
    dwg = _wgrad_rows(dg, h2, "wgrad_gate")
    red.start(["w_gate"], [dwg])
    (du,) = after([du], [dwg])
    dwu = _wgrad_rows(du, h2, "wgrad_up")
    red.start(["w_up"], [dwu])
    dg, du = after([dg, du], [dwu] + red.local("w_gate"))
    dx2, dx2b, st2 = _ffn_up_bwd(dg, du, wg, wu, dx3, x2, r2, nw2)
    (dx2b,) = after([dx2b], red.local("w_up", first=[dx2b] + red.landed("w_down")))
    dwo = _wgrad_out(ma, mr, dx2b)
    red.start(["w_out"], [dwo])
    (dx2b,) = after([dx2b], [dwo])
    dmixed = _out_proj_bwd(dx2b, wout)
    dqa, dka, dva = _attn_bwd(proj, dmixed, o, lse)
    (dmixed,) = after([dmixed], red.local("w_out", first=[dqa] + red.landed("w_gate")))
    dqr, dkr, dvr, dgr = _ret_bwd(proj, ret, states, dmixed)
    dproj = jnp.concatenate([dqa, dka, dva, dqr, dkr, dvr, dgr], axis=1)
    (dwi0,) = after([_wgrad_in(h1, dproj, 0)], red.landed("w_up"))
    red.start(["w_in_0"], [dwi0])
    (dproj,) = after([dproj], [dwi0])
    dwi1 = _wgrad_in(h1, dproj, 1)
    red.start(["w_in_1"], [dwi1])
    sums = red.local("w_in_0", first=[dwi1] + red.landed("w_out"))
    sums = red.local("w_in_1", first=sums + red.update("w_down"))
    (dproj,) = after([dproj], sums)
    gx, st1 = _in_proj_bwd(dproj, win, dx2, x, r1, nw1)
    dwi = jnp.concatenate([dwi0, dwi1], axis=1)
    stats = jnp.concatenate([st1[0:1], st2[0:1], st3[0:2], jnp.zeros((4, D), F32)], axis=0)
    return stats, gx, dwi, dwo, dwg, dwu, dwd


def _place():
    x, y, c = lax.axis_index("x"), lax.axis_index("y"), lax.axis_index("c")
    return x, y, c, [(1 - x, y), (x, 1 - y), (1 - x, 1 - y)]


def _handshake(peers):
    barrier = pltpu.get_barrier_semaphore()
    for peer in peers:
        pl.semaphore_signal(barrier, inc=1, device_id=peer, device_id_type=MESH)
    pl.semaphore_wait(barrier, len(peers))


def _all_gather(shards, name, collective_id):
    na = len(shards)
    SIB, XN0, XN1, YN1, YN0, VIA_X, VIA_Y = 0, 1, 2, 3, 4, 5, 6
    D2D = {XN0: 7, XN1: 8, YN1: 9, YN0: 10, VIA_X: 11, VIA_Y: 12}

    def body(*refs):
        ins, outs = refs[:na], refs[na:2 * na]
        send_sems, recv_sems, local_sems = refs[2 * na:]
        x, y, c, _ = _place()
        me, sib = (x, y, c), (x, y, 1 - c)
        xn, yn, dg = (1 - x, y, c), (x, 1 - y, c), (1 - x, 1 - y, c)
        _handshake([sib, xn, yn])

        def part(ref, h):
            rows = ref.shape[0] // 2
            return ref if h is None else ref.at[pl.ds(h * rows, rows)]

        def block(a, owner, h):
            return part(outs[a].at[4 * owner[0] + 2 * owner[1] + owner[2]], h)

        def copy(a, k, owner, h, to, own_src=False):
            return pltpu.make_async_remote_copy(
                src_ref=part(ins[a], h) if own_src else block(a, owner, h), dst_ref=block(a, owner, h),
                send_sem=send_sems.at[a, k], recv_sem=recv_sems.at[a, k], device_id=to, device_id_type=MESH)

        def other(p):
            return (p[0], p[1], 1 - c)

        mine = [pltpu.make_async_copy(ins[a], block(a, me, None), local_sems.at[a]) for a in range(na)]
        for cp in mine:
            cp.start()
        sent = []
        for a in range(na):
            sent += [copy(a, XN0, me, 0, xn, True), copy(a, YN1, me, 1, yn, True),
                     copy(a, XN1, me, 1, xn, True), copy(a, YN0, me, 0, yn, True)]
        sent += [copy(a, SIB, me, None, sib, True) for a in range(na)]
        for cp in sent:
            cp.start()

        def landed(a, k, owner, h, then):
            copy(a, k, owner, h, me).wait_recv()
            for k2, to in then + [(D2D[k], sib)]:
                cp = copy(a, k2, owner, h, to)
                cp.start()
                sent.append(cp)

        for a in range(na):
            landed(a, XN0, xn, 0, [(VIA_Y, yn)])
            landed(a, YN1, yn, 1, [(VIA_X, xn)])
            landed(a, XN1, xn, 1, [])
            landed(a, YN0, yn, 0, [])
        for a in range(na):
            landed(a, VIA_Y, dg, 0, [])
            landed(a, VIA_X, dg, 1, [])
        for a in range(na):
            copy(a, SIB, sib, None, me).wait_recv()
            for k, owner, h in ((XN0, xn, 0), (XN1, xn, 1), (YN1, yn, 1), (YN0, yn, 0), (VIA_Y, dg, 0), (VIA_X, dg, 1)):
                copy(a, D2D[k], other(owner), h, me).wait_recv()
        for cp in sent:
            cp.wait_send()
        for cp in mine:
            cp.wait()

    return _sequencer_call(
        body, name, collective_id,
        [jax.ShapeDtypeStruct((NDEV,) + s.shape, s.dtype) for s in shards],
        [pltpu.SemaphoreType.DMA((na, 13)), pltpu.SemaphoreType.DMA((na, 13)), pltpu.SemaphoreType.DMA((na,))])(*shards)


def _sequencer_call(body, name, collective_id, out_type, scratch_types):
    return pl.kernel(
        body, name=name, out_type=out_type,
        mesh=plsc.ScalarSubcoreMesh(axis_name="sequencer", num_cores=1),
        scratch_types=scratch_types,
        compiler_params=pltpu.CompilerParams(collective_id=collective_id))


def _exchange_sibling(grads, name, collective_id):
    na = len(grads)

    def body(*refs):
        ins, outs = refs[:na], refs[na:2 * na]
        send_sems, recv_sems = refs[2 * na:]
        x, y, c, _ = _place()
        _handshake([(x, y, 1 - c)])
        cps = []
        for a in range(na):
            for k in range(4):
                cps.append(pltpu.make_async_remote_copy(
                    src_ref=ins[a].at[2 * k + (1 - c)], dst_ref=outs[a].at[k],
                    send_sem=send_sems.at[a, k], recv_sem=recv_sems.at[a, k],
                    device_id=(x, y, 1 - c), device_id_type=MESH))
        for cp in cps:
            cp.start()
        for cp in cps:
            cp.wait()

    return _sequencer_call(
        body, name, collective_id,
        [jax.ShapeDtypeStruct((4,) + g.shape[1:], g.dtype) for g in grads],
        [pltpu.SemaphoreType.DMA((na, 4)), pltpu.SemaphoreType.DMA((na, 4))])(*grads)


def _row_tile(rows, cols):
    for t in (512, 256, 176, 128, 64, 32, 16):
        if rows % t == 0 and t * cols * 4 <= (2 << 20):
            return t
    raise ValueError((rows, cols))


def _chip_sum(place, g, got, name):
    _, r, c = g.shape
    tm = r

    def body(pos_ref, g_ref, got_ref, o_ref):
        o_ref[...] = (g_ref[...].astype(F32) + got_ref[...].astype(F32)).astype(BF16)

    def chip(j, pos):
        return 2 * (pos[0] ^ jnp.where(j == 1, 0, 1)) + (pos[1] ^ jnp.where(j == 0, 0, 1))

    return pl.pallas_call(
        body, name=name,
        grid_spec=pltpu.PrefetchScalarGridSpec(
            num_scalar_prefetch=1, grid=(3, r // tm),
            in_specs=[pl.BlockSpec((None, tm, c), lambda j, i, pos: (2 * chip(j, pos) + pos[2], i, 0)),
                      pl.BlockSpec((None, tm, c), lambda j, i, pos: (chip(j, pos), i, 0))],
            out_specs=pl.BlockSpec((None, tm, c), lambda j, i, pos: (j, i, 0))),
        out_shape=jax.ShapeDtypeStruct((3, r, c), BF16),
        compiler_params=_cp(("parallel", "parallel")),
    )(place, g, got)


def _exchange_chips(sums, name, collective_id):
    na = len(sums)

    def body(*refs):
        ins, outs = refs[:na], refs[na:2 * na]
        send_sems, recv_sems = refs[2 * na:]
        x, y, c, chips = _place()
        _handshake([(*chip, c) for chip in chips])
        cps = []
        for a in range(na):
            for j, chip in enumerate(chips):
                cps.append(pltpu.make_async_remote_copy(
                    src_ref=ins[a].at[j], dst_ref=outs[a].at[j],
                    send_sem=send_sems.at[a, j], recv_sem=recv_sems.at[a, j],
                    device_id=(*chip, c), device_id_type=MESH))
        for cp in cps:
            cp.start()
        for cp in cps:
            cp.wait()

    return _sequencer_call(
        body, name, collective_id,
        [jax.ShapeDtypeStruct((3,) + s.shape[1:], s.dtype) for s in sums],
        [pltpu.SemaphoreType.DMA((na, 3)), pltpu.SemaphoreType.DMA((na, 3))])(*sums)


def _exchange_stats(stats, collective_id):
    def body(st_in, st_out, st_send, st_recv, local_sem):
        x, y, c, _ = _place()
        me_idx = 4 * x + 2 * y + c
        peers = [(x ^ ((k >> 2) & 1), y ^ ((k >> 1) & 1), c ^ (k & 1)) for k in range(1, 8)]
        _handshake(peers)
        mine = pltpu.make_async_copy(st_in, st_out.at[me_idx], local_sem)
        mine.start()
        cps = [pltpu.make_async_remote_copy(
            src_ref=st_in, dst_ref=st_out.at[me_idx], send_sem=st_send.at[k], recv_sem=st_recv.at[k],
            device_id=peer, device_id_type=MESH) for k, peer in enumerate(peers)]
        for cp in cps:
            cp.start()
        for cp in cps:
            cp.wait()
        mine.wait()

    return _sequencer_call(
        body, "exchange_stats", collective_id,
        jax.ShapeDtypeStruct((NDEV,) + stats.shape, stats.dtype),
        [pltpu.SemaphoreType.DMA((7,)), pltpu.SemaphoreType.DMA((7,)), pltpu.SemaphoreType.DMA])(stats)


class _Reduction:
    def __init__(self, place, first_collective_id, state):
        self.place = place
        self.ids = iter(range(first_collective_id, 32))
        self.state = state
        self.groups = {}
        self.updates = {}

    def next_id(self):
        return next(self.ids)

    def start(self, group, grads):
        got = _exchange_sibling(grads, "sibling_exchange_" + group[0], self.next_id())
        self.groups[group[0]] = dict(names=group, grads=grads, got=got)

    def local(self, name, first=()):
        grp = self.groups[name]
        grads = lax.optimization_barrier((tuple(grp["grads"]), tuple(first)))[0]
        grp["sums"] = [_chip_sum(self.place, g, s, "chip_sum_" + n)
                       for g, s, n in zip(grads, grp["got"], grp["names"])]
        grp["chips"] = _exchange_chips(grp["sums"], "chip_exchange_" + name, self.next_id())
        return grp["sums"]

    def landed(self, name):
        return list(self.groups[name]["chips"])

    def update(self, name):
        if name not in self.updates:
            grp = next(g for g in self.groups.values() if name in g["names"])
            k = grp["names"].index(name)
            w, m, v, part, parts = self.state[name]
            before = self.update(f"{name[:-1]}{part - 1}") if part else None
            self.updates[name] = _shard_update(self.place, w, m, v, grp["grads"][k], grp["got"][k],
                                               grp["chips"][k], "update_" + name, part, parts, before)
        return list(self.updates[name])


def _adamw(w, g, m, v):
    m = ADAM_B1 * m + (1.0 - ADAM_B1) * g
    v = ADAM_B2 * v + (1.0 - ADAM_B2) * (g * g)
    m_hat = m / (1.0 - ADAM_B1 ** ADAM_STEP)
    v_hat = v / (1.0 - ADAM_B2 ** ADAM_STEP)
    delta = -ADAM_LR * (m_hat / (jnp.sqrt(v_hat) + ADAM_EPS) + ADAM_WD * w)
    return delta, m, v


def _shard_update(place, w, m, v, g, got_sib, got_chips, name, part=0, parts=1, before=None):
    r, c = w.shape
    rp = r // parts
    tm = _row_tile(rp, c)
    off = part * (rp // tm)

    def body(pos_ref, w_ref, m_ref, v_ref, g_ref, s_ref, c_ref, *rest):
        go_ref, d_ref, mo_ref, vo_ref = rest[-4:]
        grad = g_ref[...].astype(F32) + s_ref[...].astype(F32)
        for j in range(3):
            grad = grad + c_ref[j].astype(F32)
        delta, mn, vn = _adamw(w_ref[...], grad, m_ref[...], v_ref[...])
        go_ref[...] = grad
        d_ref[...] = delta
        mo_ref[...] = mn
        vo_ref[...] = vn

    row = pl.BlockSpec((tm, c), lambda i, pos: (i + off, 0))
    before = list(before or [])
    return pl.pallas_call(
        body, name=name,
        grid_spec=pltpu.PrefetchScalarGridSpec(
            num_scalar_prefetch=1, grid=(rp // tm,),
            in_specs=[row, row, row,
                      pl.BlockSpec((None, tm, c), lambda i, pos: (4 * pos[0] + 2 * pos[1] + pos[2], i, 0)),
                      pl.BlockSpec((None, tm, c), lambda i, pos: (2 * pos[0] + pos[1], i, 0)),
                      pl.BlockSpec((3, tm, c), lambda i, pos: (0, i, 0))]
            + [pl.BlockSpec(memory_space=pl.ANY)] * len(before),
            out_specs=[row, row, row, row]),
        out_shape=[jax.ShapeDtypeStruct((r, c), F32)] * 4,
        input_output_aliases={7 + k: k for k in range(len(before))},
        compiler_params=_cp(("parallel",)),
    )(place, w, m, v, g, got_sib, got_chips, *before)


def _small_update(stats_all, ws, ms, vs):
    def body(st_ref, w_ref, m_ref, v_ref, go_ref, d_ref, mo_ref, vo_ref):
        grad = st_ref[0]
        for k in range(1, NDEV):
            grad = grad + st_ref[k]
        delta, mn, vn = _adamw(w_ref[...], grad, m_ref[...], v_ref[...])
        go_ref[...] = grad
        d_ref[...] = delta
        mo_ref[...] = mn
        vo_ref[...] = vn

    return pl.pallas_call(
        body, name="small_update",
        out_shape=[jax.ShapeDtypeStruct((8, D), F32)] * 4,
        compiler_params=_cp(),
    )(stats_all, ws, ms, vs)


def kernel(x, norm_mix_w, w_in, w_out, norm_ffn_w, w_gate, w_up, w_down, norm_final_w, loss_target, m_norm_mix_w, m_w_in, m_w_out, m_norm_ffn_w, m_w_gate, m_w_up, m_w_down, m_norm_final_w, v_norm_mix_w, v_w_in, v_w_out, v_norm_ffn_w, v_w_gate, v_w_up, v_w_down, v_norm_final_w):
    tr = {"w_gate", "w_up"}
    names = ["w_in", "w_out", "w_gate", "w_up", "w_down"]

    def view(a, n):
        return a[0].T if n in tr else a[0]

    big_w = [view(a, n) for a, n in zip([w_in, w_out, w_gate, w_up, w_down], names)]
    big_m = [view(a, n) for a, n in zip([m_w_in, m_w_out, m_w_gate, m_w_up, m_w_down], names)]
    big_v = [view(a, n) for a, n in zip([v_w_in, v_w_out, v_w_gate, v_w_up, v_w_down], names)]

    shards = [_cast_bf16(w, "cast_" + n) for w, n in zip(big_w, names)]
    (win,) = _all_gather(shards[0:1], "all_gather_w_in", 1)
    wout, wg, wu = _all_gather(shards[1:4], "all_gather_out_gate_up", 2)
    (wd,) = _all_gather(shards[4:5], "all_gather_w_down", 3)
    nw3 = norm_final_w.reshape(1, D)
    place = jnp.stack([lax.axis_index("x"), lax.axis_index("y"), lax.axis_index("c")]).astype(jnp.int32)
    state = {n: (w, m, v, 0, 1) for n, w, m, v in zip(names, big_w, big_m, big_v)}
    for part in range(W_IN_PARTS):
        state[f"w_in_{part}"] = state["w_in"][:3] + (part, W_IN_PARTS)
    red = _Reduction(place, 4, state)
    stats, gx, *_ = _local_step(
        x[0], loss_target[0], norm_mix_w, norm_ffn_w, nw3, win, wout.reshape(D, D), wg, wu, wd, red)
    stats_all = _exchange_stats(stats, red.next_id())
    upd = [red.update(f"w_in_{W_IN_PARTS - 1}" if n == "w_in" else n) for n in names]
    stats_all = lax.optimization_barrier((stats_all, tuple(upd[0])))[0]

    def rows(a, b, c):
        return jnp.concatenate([a.reshape(1, D), b.reshape(1, D), c.reshape(1, D), jnp.zeros((5, D), F32)], axis=0)

    sg, sd, sm, sv = _small_update(stats_all, rows(norm_mix_w, norm_ffn_w, norm_final_w),
                                   rows(m_norm_mix_w, m_norm_ffn_w, m_norm_final_w),
                                   rows(v_norm_mix_w, v_norm_ffn_w, v_norm_final_w))
    loss = sg[3, 0]

    def outs(k, small):
        big = [(u[k].T if n in tr else u[k])[None] for u, n in zip(upd, names)]
        return [small[0:1], big[0], big[1], small[1:2], big[2], big[3], big[4], small[2]]

    return (loss, gx[None], *outs(0, sg), *outs(1, sd), *outs(2, sm), *outs(3, sv))
```

```python
import functools
import math

import numpy as np
import jax
import jax.numpy as jnp
from jax import lax
from jax.experimental import pallas as pl
from jax.experimental.pallas import tpu as pltpu
from jax.experimental.pallas import tpu_sc as plsc

F32 = jnp.float32
BF16 = jnp.bfloat16

S = 2048
D = 2048
NDEV = 8
N_IN = 7168 // NDEV
N_FF = 5632 // NDEV
NFG, N_FG = NDEV // 2, 2 * N_FF
N_OUT = 2048 // NDEV
AH, AHD = 8, 128
RH, RHD = 4, 256
CH = 128
NB = S // CH
EPS = 1e-6
PATTERNS = ((1, 16), (4, 4), (16, 1))
NEG = -1e30
VMEM_LIMIT = 56 * 1024 * 1024

ADAM_LR, ADAM_B1, ADAM_B2, ADAM_EPS, ADAM_WD, ADAM_STEP = 0.001, 0.9, 0.999, 1e-08, 0.01, 10
MESH = pl.DeviceIdType.MESH


def _cp(sem=None):
    return pltpu.CompilerParams(dimension_semantics=sem, vmem_limit_bytes=VMEM_LIMIT)


def _dot(a, b):
    return jnp.dot(a, b, preferred_element_type=F32)


def _dot_nt(a, b):
    return lax.dot_general(a, b, (((1,), (1,)), ((), ())), preferred_element_type=F32)


def _dot_tn(a, b):
    return lax.dot_general(a, b, (((0,), (0,)), ((), ())), preferred_element_type=F32)


def _sigmoid(x):
    return 0.5 * jnp.tanh(0.5 * x) + 0.5


def _cast_bf16(w, name):
    r, c = w.shape
    tm = r if r <= 1024 else 512

    def body(w_ref, o_ref):
        o_ref[...] = w_ref[...].astype(BF16)

    return pl.pallas_call(
        body, name=name, grid=(r // tm,),
        in_specs=[pl.BlockSpec((tm, c), lambda i: (i, 0))],
        out_specs=pl.BlockSpec((tm, c), lambda i: (i, 0)),
        out_shape=jax.ShapeDtypeStruct((r, c), BF16),
        compiler_params=_cp(("parallel",)),
    )(w)


def _rms_fwd(x, nw):
    tm = 256

    def body(x_ref, w_ref, h_ref, r_ref):
        xs = x_ref[...]
        r = lax.rsqrt(jnp.mean(xs * xs, axis=-1, keepdims=True) + EPS)
        h_ref[...] = ((xs * r) * w_ref[...]).astype(BF16)
        r_ref[...] = r

    return pl.pallas_call(
        body, name="rms_fwd", grid=(S // tm,),
        in_specs=[pl.BlockSpec((tm, D), lambda i: (i, 0)), pl.BlockSpec((1, D), lambda i: (0, 0))],
        out_specs=[pl.BlockSpec((tm, D), lambda i: (i, 0)), pl.BlockSpec((tm, 1), lambda i: (i, 0))],
        out_shape=[jax.ShapeDtypeStruct((S, D), BF16), jax.ShapeDtypeStruct((S, 1), F32)],
        compiler_params=_cp(("parallel",)),
    )(x, nw)


def _row_copies(hbm_refs, bufs, sems, m, tm):
    rows = pl.ds(pl.multiple_of(m * tm, tm), tm)
    return [pltpu.make_async_copy(h.at[rows], b, sems.at[i]) for i, (h, b) in enumerate(zip(hbm_refs, bufs))]


def _rms_bwd_tile(dh, xs, r, nw):
    dnw = jnp.sum(dh * (xs * r), axis=0, keepdims=True)
    gy = dh * nw
    dx = r * gy - xs * ((r * r * r) * jnp.mean(gy * xs, axis=-1, keepdims=True))
    return dx, dnw


def _proj(h1, win):
    tm = 1024

    def body(a_ref, w_ref, o_ref):
        o_ref[...] = _dot(a_ref[...], w_ref[...])

    return pl.pallas_call(
        body, name="proj", grid=(NDEV, S // tm),
        in_specs=[pl.BlockSpec((tm, D), lambda p, m: (m, 0)),
                  pl.BlockSpec((None, D, N_IN), lambda p, m: (p, 0, 0))],
        out_specs=pl.BlockSpec((tm, N_IN), lambda p, m: (m, p)),
        out_shape=jax.ShapeDtypeStruct((S, NDEV * N_IN), F32),
        compiler_params=_cp(("parallel", "parallel")),
    )(h1, win)


def _out_proj_rms(x, ma, mr, wout, nw):
    tm = 256
    half = D // 2

    def body(x_ref, ma_ref, mr_ref, w_ref, nw_ref, x2_ref, h_ref, r_ref):
        acc = _dot(ma_ref[...], w_ref[0:half, :]) + _dot(mr_ref[...], w_ref[half:D, :])
        x2 = x_ref[...] + acc
        r = lax.rsqrt(jnp.mean(x2 * x2, axis=-1, keepdims=True) + EPS)
        x2_ref[...] = x2
        h_ref[...] = ((x2 * r) * nw_ref[...]).astype(BF16)
        r_ref[...] = r

    return pl.pallas_call(
        body, name="out_proj_rms", grid=(S // tm,),
        in_specs=[pl.BlockSpec((tm, D), lambda i: (i, 0)),
                  pl.BlockSpec((tm, half), lambda i: (i, 0)),
                  pl.BlockSpec((tm, half), lambda i: (i, 0)),
                  pl.BlockSpec((D, D), lambda i: (0, 0)),
                  pl.BlockSpec((1, D), lambda i: (0, 0))],
        out_specs=[pl.BlockSpec((tm, D), lambda i: (i, 0)), pl.BlockSpec((tm, D), lambda i: (i, 0)),
                   pl.BlockSpec((tm, 1), lambda i: (i, 0))],
        out_shape=[jax.ShapeDtypeStruct((S, D), F32), jax.ShapeDtypeStruct((S, D), BF16),
                   jax.ShapeDtypeStruct((S, 1), F32)],
        compiler_params=_cp(("parallel",)),
    )(x, ma, mr, wout, nw)


def _ffn_up(h2, wg, wu):
    tm = 512

    def body(h_ref, wg_ref, wu_ref, a_ref, dadg_ref, dadu_ref):
        h = h_ref[...]
        g = _dot_nt(h, wg_ref[...])
        u = _dot_nt(h, wu_ref[...])
        sg = _sigmoid(g)
        silu = g * sg
        a_ref[...] = (silu * u).astype(BF16)
        dadg_ref[...] = (u * (sg * (1.0 + g * (1.0 - sg)))).astype(BF16)
        dadu_ref[...] = silu.astype(BF16)

    blk = pl.BlockSpec((None, tm, N_FG), lambda p, m: (p, m, 0))
    wblk = pl.BlockSpec((None, N_FG, D), lambda p, m: (p, 0, 0))
    return pl.pallas_call(
        body, name="ffn_up", grid=(NFG, S // tm),
        in_specs=[pl.BlockSpec((tm, D), lambda p, m: (m, 0)), wblk, wblk],
        out_specs=[blk, blk, blk],
        out_shape=[jax.ShapeDtypeStruct((NFG, S, N_FG), BF16)] * 3,
        compiler_params=_cp(("parallel", "parallel")),
    )(h2, wg, wu)


def _ffn_down_loss(x2, a, wd, nw, tgt):
    tm = 512

    def body(x2_hbm, a_ref, w_ref, nw_ref, t_hbm, dx_ref, dxb_ref, st_ref, acc_ref, x2_buf, t_buf, sems):
        m, p = pl.program_id(0), pl.program_id(1)
        tail_in = _row_copies((x2_hbm, t_hbm), (x2_buf, t_buf), sems, m, tm)

        @pl.when(p == 0)
        def _():
            acc_ref[...] = jnp.zeros_like(acc_ref)
            for cp in tail_in:
                cp.start()

        @pl.when((p == 0) & (m == 0))
        def _():
            st_ref[...] = jnp.zeros_like(st_ref)

        acc_ref[...] += _dot(a_ref[...], w_ref[...])

        @pl.when(p == NFG - 1)
        def _():
            for cp in tail_in:
                cp.wait()
            x3 = x2_buf[...] + acc_ref[...]
            nwv = nw_ref[...]
            r = lax.rsqrt(jnp.mean(x3 * x3, axis=-1, keepdims=True) + EPS)
            y = (x3 * r) * nwv
            err = y - t_buf[...]
            loss = 0.5 * jnp.sum(jnp.mean(err * err, axis=-1, keepdims=True), axis=0, keepdims=True)
            dy = err * (1.0 / D)
            dx, dnw = _rms_bwd_tile(dy, x3, r, nwv)
            dx_ref[...] = dx
            dxb_ref[...] = dx.astype(BF16)
            st_ref[0:1, :] += dnw
            st_ref[1:2, :] += jnp.broadcast_to(loss, (1, D))

    return pl.pallas_call(
        body, name="ffn_down_loss", grid=(S // tm, NFG),
        in_specs=[pl.BlockSpec(memory_space=pl.ANY),
                  pl.BlockSpec((None, tm, N_FG), lambda m, p: (p, m, 0)),
                  pl.BlockSpec((None, N_FG, D), lambda m, p: (p, 0, 0)),
                  pl.BlockSpec((1, D), lambda m, p: (0, 0)),
                  pl.BlockSpec(memory_space=pl.ANY)],
        out_specs=[pl.BlockSpec((tm, D), lambda m, p: (m, 0)), pl.BlockSpec((tm, D), lambda m, p: (m, 0)),
                   pl.BlockSpec((8, D), lambda m, p: (0, 0))],
        out_shape=[jax.ShapeDtypeStruct((S, D), F32), jax.ShapeDtypeStruct((S, D), BF16),
                   jax.ShapeDtypeStruct((8, D), F32)],
        scratch_shapes=[pltpu.VMEM((tm, D), F32), pltpu.VMEM((tm, D), F32), pltpu.VMEM((tm, D), F32),
                        pltpu.SemaphoreType.DMA((2,))],
        compiler_params=_cp(("arbitrary", "arbitrary")),
    )(x2, a, wd, nw, tgt)


def _ffn_down_bwd(dx3b, wd, dadg, dadu):
    tm = 1024

    def body(dx_ref, w_ref, dadg_ref, dadu_ref, dg_ref, du_ref):
        da = _dot_nt(dx_ref[...], w_ref[...])
        dg_ref[...] = (da * dadg_ref[...].astype(F32)).astype(BF16)
        du_ref[...] = (da * dadu_ref[...].astype(F32)).astype(BF16)

    blk = pl.BlockSpec((None, tm, N_FG), lambda p, m: (p, m, 0))
    return pl.pallas_call(
        body, name="ffn_down_bwd", grid=(NFG, S // tm),
        in_specs=[pl.BlockSpec((tm, D), lambda p, m: (m, 0)),
                  pl.BlockSpec((None, N_FG, D), lambda p, m: (p, 0, 0)), blk, blk],
        out_specs=[blk, blk],
        out_shape=[jax.ShapeDtypeStruct((NFG, S, N_FG), BF16)] * 2,
        compiler_params=_cp(("parallel", "parallel")),
    )(dx3b, wd, dadg, dadu)


def _ffn_up_bwd(dg, du, wg, wu, dres, xs, r, nw):
    tm = 512

    def body(dg_ref, du_ref, wg_ref, wu_ref, dres_hbm, x_hbm, r_ref, nw_ref, dx_ref, dxb_ref, st_ref,
             dres_buf, x_buf, sems):
        m, p = pl.program_id(0), pl.program_id(1)
        tail_in = _row_copies((dres_hbm, x_hbm), (dres_buf, x_buf), sems, m, tm)

        @pl.when(p == 0)
        def _():
            dx_ref[...] = jnp.zeros_like(dx_ref)
            for cp in tail_in:
                cp.start()

        @pl.when((p == 0) & (m == 0))
        def _():
            st_ref[...] = jnp.zeros_like(st_ref)

        dx_ref[...] += _dot(dg_ref[...], wg_ref[...])
        dx_ref[...] += _dot(du_ref[...], wu_ref[...])

        @pl.when(p == NFG - 1)
        def _():
            for cp in tail_in:
                cp.wait()
            dx, dnw = _rms_bwd_tile(dx_ref[...], x_buf[...], r_ref[...], nw_ref[...])
            dx = dres_buf[...] + dx
            dx_ref[...] = dx
            dxb_ref[...] = dx.astype(BF16)
            st_ref[0:1, :] += dnw

    blk = pl.BlockSpec((None, tm, N_FG), lambda m, p: (p, m, 0))
    wblk = pl.BlockSpec((None, N_FG, D), lambda m, p: (p, 0, 0))
    row = pl.BlockSpec((tm, D), lambda m, p: (m, 0))
    hbm = pl.BlockSpec(memory_space=pl.ANY)
    return pl.pallas_call(
        body, name="ffn_up_bwd", grid=(S // tm, NFG),
        in_specs=[blk, blk, wblk, wblk, hbm, hbm, pl.BlockSpec((tm, 1), lambda m, p: (m, 0)),
                  pl.BlockSpec((1, D), lambda m, p: (0, 0))],
        out_specs=[row, row, pl.BlockSpec((8, D), lambda m, p: (0, 0))],
        out_shape=[jax.ShapeDtypeStruct((S, D), F32), jax.ShapeDtypeStruct((S, D), BF16),
                   jax.ShapeDtypeStruct((8, D), F32)],
        scratch_shapes=[pltpu.VMEM((tm, D), F32), pltpu.VMEM((tm, D), F32), pltpu.SemaphoreType.DMA((2,))],
        compiler_params=_cp(("arbitrary", "arbitrary")),
    )(dg, du, wg, wu, dres, xs, r, nw)


def _out_proj_bwd(dx2b, wout):
    tm = 256

    def body(dx_ref, w_ref, o_ref):
        o_ref[...] = _dot_nt(dx_ref[...], w_ref[...])

    return pl.pallas_call(
        body, name="out_proj_bwd", grid=(S // tm,),
        in_specs=[pl.BlockSpec((tm, D), lambda i: (i, 0)), pl.BlockSpec((D, D), lambda i: (0, 0))],
        out_specs=pl.BlockSpec((tm, D), lambda i: (i, 0)),
        out_shape=jax.ShapeDtypeStruct((S, D), F32),
        compiler_params=_cp(("parallel",)),
    )(dx2b, wout)


def _in_proj_bwd(dproj, win, dres, xs, r, nw):
    tm = 1024

    def body(dp_ref, w_ref, dres_hbm, x_hbm, r_ref, nw_ref, dx_ref, st_ref, dres_buf, x_buf, sems):
        m, p = pl.program_id(0), pl.program_id(1)
        tail_in = _row_copies((dres_hbm, x_hbm), (dres_buf, x_buf), sems, m, tm)

        @pl.when(p == 0)
        def _():
            dx_ref[...] = jnp.zeros_like(dx_ref)
            for cp in tail_in:
                cp.start()

        @pl.when((p == 0) & (m == 0))
        def _():
            st_ref[...] = jnp.zeros_like(st_ref)

        dx_ref[...] += _dot_nt(dp_ref[...], w_ref[...])

        @pl.when(p == NDEV - 1)
        def _():
            for cp in tail_in:
                cp.wait()
            dx, dnw = _rms_bwd_tile(dx_ref[...], x_buf[...], r_ref[...], nw_ref[...])
            dx_ref[...] = dres_buf[...] + dx
            st_ref[0:1, :] += dnw

    row = pl.BlockSpec((tm, D), lambda m, p: (m, 0))
    hbm = pl.BlockSpec(memory_space=pl.ANY)
    return pl.pallas_call(
        body, name="in_proj_bwd", grid=(S // tm, NDEV),
        in_specs=[pl.BlockSpec((tm, N_IN), lambda m, p: (m, p)),
                  pl.BlockSpec((None, D, N_IN), lambda m, p: (p, 0, 0)),
                  hbm, hbm, pl.BlockSpec((tm, 1), lambda m, p: (m, 0)),
                  pl.BlockSpec((1, D), lambda m, p: (0, 0))],
        out_specs=[row, pl.BlockSpec((8, D), lambda m, p: (0, 0))],
        out_shape=[jax.ShapeDtypeStruct((S, D), F32), jax.ShapeDtypeStruct((8, D), F32)],
        scratch_shapes=[pltpu.VMEM((tm, D), F32), pltpu.VMEM((tm, D), F32), pltpu.SemaphoreType.DMA((2,))],
        compiler_params=_cp(("arbitrary", "arbitrary")),
    )(dproj, win, dres, xs, r, nw)


W_IN_PARTS = 2


def _wgrad_in(h1, dproj, part):
    rows = D // W_IN_PARTS

    def body(a_ref, d_ref, o_ref):
        o_ref[...] = _dot_tn(a_ref[...], d_ref[...]).astype(BF16)

    return pl.pallas_call(
        body, name=f"wgrad_in_{part}", grid=(NDEV,),
        in_specs=[pl.BlockSpec((S, rows), lambda p: (0, part)), pl.BlockSpec((S, N_IN), lambda p: (0, p))],
        out_specs=pl.BlockSpec((None, rows, N_IN), lambda p: (p, 0, 0)),
        out_shape=jax.ShapeDtypeStruct((NDEV, rows, N_IN), BF16),
        compiler_params=_cp(("parallel",)),
    )(h1, dproj)


def _wgrad_rows(a3, dy, name):
    def body(a_ref, d_ref, o_ref):
        o_ref[...] = _dot_tn(a_ref[...], d_ref[...]).astype(BF16)

    return pl.pallas_call(
        body, name=name, grid=(NFG,),
        in_specs=[pl.BlockSpec((None, S, N_FG), lambda p: (p, 0, 0)), pl.BlockSpec((S, D), lambda p: (0, 0))],
        out_specs=pl.BlockSpec((None, N_FG, D), lambda p: (p, 0, 0)),
        out_shape=jax.ShapeDtypeStruct((NFG, N_FG, D), BF16),
        compiler_params=_cp(("parallel",)),
    )(a3, dy).reshape(NDEV, N_FF, D)


def _wgrad_out(ma, mr, dx2b):
    half = D // 2
    per = half // N_OUT

    def body(ma_ref, mr_ref, d_ref, o_ref):
        p = pl.program_id(0)

        @pl.when(p < per)
        def _():
            o_ref[...] = _dot_tn(ma_ref[...], d_ref[...]).astype(BF16)

        @pl.when(p >= per)
        def _():
            o_ref[...] = _dot_tn(mr_ref[...], d_ref[...]).astype(BF16)

    return pl.pallas_call(
        body, name="wgrad_out", grid=(NDEV,),
        in_specs=[pl.BlockSpec((S, N_OUT), lambda p: (0, jnp.minimum(p, per - 1))),
                  pl.BlockSpec((S, N_OUT), lambda p: (0, jnp.maximum(p - per, 0))),
                  pl.BlockSpec((S, D), lambda p: (0, 0))],
        out_specs=pl.BlockSpec((None, N_OUT, D), lambda p: (p, 0, 0)),
        out_shape=jax.ShapeDtypeStruct((NDEV, N_OUT, D), BF16),
        compiler_params=_cp(("parallel",)),
    )(ma, mr, dx2b)


def _attn_consts():
    c = np.zeros((AH, 8, AHD), np.float32)
    for h in range(AH):
        c[h, :, :] = 2.0 ** (-(h + 1))
    return jnp.asarray(c)


def _permute_in(dst, src, d, cast=None):
    v = src[...]
    if d > 1:
        v = pltpu.einshape("jrc->rjc", v.reshape(S // d, d, AHD)).reshape(S, AHD)
    dst[...] = v if cast is None else v.astype(cast)


def _natural_order(v, d):
    if d == 1:
        return v
    return pltpu.einshape("rjc->jrc", v.reshape(d, S // d, AHD)).reshape(S, AHD)


def _attn_masks():
    qi = lax.broadcasted_iota(jnp.int32, (CH, CH), 0)
    kj = lax.broadcasted_iota(jnp.int32, (CH, CH), 1)
    dist_c = (qi - kj).astype(F32)
    dist_p = (qi - kj + CH).astype(F32)
    return (qi >= kj)[None], (kj >= qi)[None], dist_c[None], dist_p[None]


GB = 16


def _bdot_nt(a, b):
    return lax.dot_general(a, b, (((2,), (2,)), ((0,), (0,))), preferred_element_type=F32)


def _bdot(a, b):
    return lax.dot_general(a, b, (((2,), (1,)), ((0,), (0,))), preferred_element_type=F32)


def _bdot_tn(a, b):
    return lax.dot_general(a, b, (((1,), (1,)), ((0,), (0,))), preferred_element_type=F32)


def _shift_block(dst, src):
    dst[0:CH, :] = jnp.zeros((CH, AHD), dst.dtype)
    dst[CH:S, :] = src[0:S - CH, :]


def _has_prev(g, nb):
    blk = lax.broadcasted_iota(jnp.int32, (GB, 1, 1), 0) + g * GB
    return (blk & (nb - 1)) != 0


def _blocks(ref, g):
    return ref[g * GB * CH:(g + 1) * GB * CH, :].reshape(GB, CH, AHD)


def _attn_fwd(proj):
    scale = 1.0 / math.sqrt(AHD)

    def body(c_ref, q_ref, k_ref, v_ref, o_ref, ob_ref, lse_ref, qd, kd, vd, kps, vps, od, ld, *nat):
        onat, lnat = nat[0:3], nat[3:6]
        slope = c_ref[0:1, :]
        mask_c, mask_p, dist_c, dist_p = _attn_masks()
        for pi, (d, nb) in enumerate(PATTERNS):
            _permute_in(qd, q_ref, d, BF16)
            _permute_in(kd, k_ref, d, BF16)
            _permute_in(vd, v_ref, d, BF16)
            if nb > 1:
                _shift_block(kps, kd)
                _shift_block(vps, vd)
            bias_c = -(slope * float(d)) * dist_c
            bias_p = -(slope * float(d)) * dist_p
            for g in range(NB // GB):
                q3, k3, v3 = _blocks(qd, g), _blocks(kd, g), _blocks(vd, g)
                s_c = jnp.where(mask_c, _bdot_nt(q3, k3) * scale + bias_c, NEG)
                mx = jnp.max(s_c, axis=-1, keepdims=True)
                if nb > 1:
                    kp3, vp3 = _blocks(kps, g), _blocks(vps, g)
                    s_p = jnp.where(jnp.logical_and(mask_p, _has_prev(g, nb)),
                                    _bdot_nt(q3, kp3) * scale + bias_p, NEG)
                    mx = jnp.maximum(mx, jnp.max(s_p, axis=-1, keepdims=True))
                    l = (jnp.sum(jnp.exp(s_c - mx), axis=-1, keepdims=True)
                         + jnp.sum(jnp.exp(s_p - mx), axis=-1, keepdims=True))
                    lse = mx + jnp.log(l)
                    o3 = _bdot(jnp.exp(s_c - lse).astype(BF16), v3) + _bdot(jnp.exp(s_p - lse).astype(BF16), vp3)
                else:
                    l = jnp.sum(jnp.exp(s_c - mx), axis=-1, keepdims=True)
                    lse = mx + jnp.log(l)
                    o3 = _bdot(jnp.exp(s_c - lse).astype(BF16), v3)
                rows = slice(g * GB * CH, (g + 1) * GB * CH)
                od[rows, :] = o3.reshape(GB * CH, AHD)
                ld[rows, :] = jnp.broadcast_to(lse, (GB, CH, AHD)).reshape(GB * CH, AHD)
            onat[pi][...] = _natural_order(od[...], d)
            lnat[pi][...] = _natural_order(ld[...], d)
        l0, l1, l2 = lnat[0][...], lnat[1][...], lnat[2][...]
        mx = jnp.maximum(jnp.maximum(l0, l1), l2)
        e0, e1, e2 = jnp.exp(l0 - mx), jnp.exp(l1 - mx), jnp.exp(l2 - mx)
        den = e0 + e1 + e2
        out = (e0 / den) * onat[0][...] + (e1 / den) * onat[1][...] + (e2 / den) * onat[2][...]
        o_ref[...] = out
        ob_ref[...] = out.astype(BF16)
        lse_ref[...] = mx + jnp.log(den)

    def col(off):
        return pl.BlockSpec((S, AHD), lambda h: (0, off + h))

    return pl.pallas_call(
        body, name="attn_fwd", grid=(AH,),
        in_specs=[pl.BlockSpec((None, 8, AHD), lambda h: (h, 0, 0)), col(0), col(AH), col(2 * AH)],
        out_specs=[col(0), col(0), col(0)],
        out_shape=[jax.ShapeDtypeStruct((S, AH * AHD), F32), jax.ShapeDtypeStruct((S, AH * AHD), BF16),
                   jax.ShapeDtypeStruct((S, AH * AHD), F32)],
        scratch_shapes=[pltpu.VMEM((S, AHD), BF16) for _ in range(5)]
        + [pltpu.VMEM((S, AHD), F32) for _ in range(8)],
        compiler_params=_cp(("parallel",)),
    )(_attn_consts(), proj, proj, proj)


def _attn_bwd(proj, dmixed, o, lse):
    scale = 1.0 / math.sqrt(AHD)

    def body(c_ref, q_ref, k_ref, v_ref, do_ref, o_ref, lse_ref, dq_ref, dk_ref, dv_ref,
             qd, kd, vd, dod, kps, vps, lsd, dld, dqd, dkd, dvd, delta, aq, ak, av):
        slope = c_ref[0:1, :]
        mask_c, mask_p, dist_c, dist_p = _attn_masks()
        delta[...] = jnp.broadcast_to(jnp.sum(do_ref[...] * o_ref[...], axis=-1, keepdims=True), (S, AHD))
        for pi, (d, nb) in enumerate(PATTERNS):
            _permute_in(qd, q_ref, d, BF16)
            _permute_in(kd, k_ref, d, BF16)
            _permute_in(vd, v_ref, d, BF16)
            _permute_in(dod, do_ref, d, BF16)
            _permute_in(lsd, lse_ref, d)
            _permute_in(dld, delta, d)
            if nb > 1:
                _shift_block(kps, kd)
                _shift_block(vps, vd)
            bias_c = -(slope * float(d)) * dist_c
            bias_p = -(slope * float(d)) * dist_p
            for g in range(NB // GB):
                q3, k3, v3, do3 = _blocks(qd, g), _blocks(kd, g), _blocks(vd, g), _blocks(dod, g)
                ls, dl = _blocks(lsd, g), _blocks(dld, g)
                lo, hi = g * GB * CH, (g + 1) * GB * CH
                p_c = jnp.exp(jnp.where(mask_c, _bdot_nt(q3, k3) * scale + bias_c, NEG) - ls)
                ds_c = ((p_c * (_bdot_nt(do3, v3) - dl)) * scale).astype(BF16)
                dq3 = _bdot(ds_c, k3)
                dkd[lo:hi, :] = _bdot_tn(ds_c, q3).reshape(GB * CH, AHD)
                dvd[lo:hi, :] = _bdot_tn(p_c.astype(BF16), do3).reshape(GB * CH, AHD)
                if nb > 1:
                    kp3, vp3 = _blocks(kps, g), _blocks(vps, g)
                    p_p = jnp.exp(jnp.where(jnp.logical_and(mask_p, _has_prev(g, nb)),
                                            _bdot_nt(q3, kp3) * scale + bias_p, NEG) - ls)
                    ds_p = ((p_p * (_bdot_nt(do3, vp3) - dl)) * scale).astype(BF16)
                    dq3 = dq3 + _bdot(ds_p, kp3)
                    dkp = _bdot_tn(ds_p, q3).reshape(GB * CH, AHD)
                    dvp = _bdot_tn(p_p.astype(BF16), do3).reshape(GB * CH, AHD)
                    if g == 0:
                        dkd[0:hi - CH, :] += dkp[CH:, :]
                        dvd[0:hi - CH, :] += dvp[CH:, :]
                    else:
                        dkd[lo - CH:hi - CH, :] += dkp
                        dvd[lo - CH:hi - CH, :] += dvp
                dqd[lo:hi, :] = dq3.reshape(GB * CH, AHD)
            ln = S // d
            for acc, src in ((aq, dqd), (ak, dkd), (av, dvd)):
                if pi == 0:
                    acc[...] = src[...]
                else:
                    acc[...] += _natural_order(src[...], d)
        dq_ref[...] = aq[...].astype(BF16)
        dk_ref[...] = ak[...].astype(BF16)
        dv_ref[...] = av[...].astype(BF16)

    def col(off):
        return pl.BlockSpec((S, AHD), lambda h: (0, off + h))

    return pl.pallas_call(
        body, name="attn_bwd", grid=(AH,),
        in_specs=[pl.BlockSpec((None, 8, AHD), lambda h: (h, 0, 0)), col(0), col(AH), col(2 * AH),
                  col(0), col(0), col(0)],
        out_specs=[col(0), col(0), col(0)],
        out_shape=[jax.ShapeDtypeStruct((S, AH * AHD), BF16)] * 3,
        scratch_shapes=[pltpu.VMEM((S, AHD), BF16) for _ in range(6)]
        + [pltpu.VMEM((S, AHD), F32) for _ in range(9)],
        compiler_params=_cp(("parallel",)),
    )(_attn_consts(), proj, proj, proj, dmixed, o, lse)


def _ret_consts():
    c = np.zeros((RH, 8, RHD), np.float32)
    for h in range(RH):
        c[h, :, :] = np.log(np.float32(1.0) - np.float32(2.0 ** (-5.0 - h)))
    return jnp.asarray(c)


def _ret_factors(lg):
    i = lax.broadcasted_iota(jnp.int32, (CH, CH), 0)
    j = lax.broadcasted_iota(jnp.int32, (CH, CH), 1)
    dif = (i - j).astype(F32)
    decay = jnp.where(dif >= 0, jnp.exp(lg[:, 0:CH] * jnp.maximum(dif, 0.0)), 0.0)
    row = lax.broadcasted_iota(jnp.int32, (CH, RHD), 0).astype(F32)
    zeta = jnp.exp(lg * (CH - 1.0 - row))
    xi = jnp.exp(lg * (row + 1.0))
    return decay, zeta, xi, jnp.exp(lg * float(CH))


CBK = 8
RSTEPS = NB // CBK


def _ret_specs(rev):
    off = 3 * AH * AHD // RHD
    rows = CBK * CH

    def ch(n):
        return (RSTEPS - 1 - n) if rev else n

    def col(k):
        return pl.BlockSpec((rows, RHD), lambda h, n: (ch(n), off + k * RH + h))

    own = pl.BlockSpec((rows, RHD), lambda h, n: (ch(n), h))
    state = pl.BlockSpec((None, CBK, RHD, RHD), lambda h, n: (h, ch(n), 0, 0))
    const = pl.BlockSpec((None, 8, RHD), lambda h, n: (h, 0, 0))
    dm = pl.BlockSpec((rows, RHD), lambda h, n: (ch(n), AH * AHD // RHD + h))
    return col, own, state, const, dm


def _chunks(x):
    return x.reshape(CBK, CH, RHD)


def _ret_fwd(proj):
    def body(c_ref, q_ref, k_ref, v_ref, g_ref, ret_ref, mr_ref, st_ref, r_acc):
        n = pl.program_id(1)

        @pl.when(n == 0)
        def _():
            r_acc[...] = jnp.zeros_like(r_acc)

        decay, zeta, xi, gch = _ret_factors(c_ref[0:1, :])
        q3 = _chunks(q_ref[...].astype(BF16))
        kc = _chunks(k_ref[...] * (1.0 / math.sqrt(RHD)))
        k3 = kc.astype(BF16)
        v3 = _chunks(v_ref[...].astype(BF16))
        kv3 = _bdot_tn((kc * zeta[None]).astype(BF16), v3)
        r = r_acc[...]
        for i in range(CBK):
            st_ref[i] = r.astype(BF16)
            r = r * gch + kv3[i]
        r_acc[...] = r
        scores = _bdot_nt(q3, k3) * decay[None]
        ret = (_bdot(scores.astype(BF16), v3) + _bdot(q3, st_ref[...]) * xi[None]).reshape(CBK * CH, RHD)
        ret_ref[...] = ret
        rr = lax.rsqrt(jnp.mean(ret * ret, axis=-1, keepdims=True) + EPS)
        gv = g_ref[...]
        mr_ref[...] = ((gv * _sigmoid(gv)) * (ret * rr)).astype(BF16)

    col, own, state, const, _ = _ret_specs(False)
    return pl.pallas_call(
        body, name="ret_fwd", grid=(RH, RSTEPS),
        in_specs=[const, col(0), col(1), col(2), col(3)],
        out_specs=[own, own, state],
        out_shape=[jax.ShapeDtypeStruct((S, RH * RHD), F32), jax.ShapeDtypeStruct((S, RH * RHD), BF16),
                   jax.ShapeDtypeStruct((RH, NB, RHD, RHD), BF16)],
        scratch_shapes=[pltpu.VMEM((RHD, RHD), F32)],
        compiler_params=_cp(("parallel", "arbitrary")),
    )(_ret_consts(), proj, proj, proj, proj)


def _ret_bwd(proj, ret, states, dmixed):
    def body(c_ref, q_ref, k_ref, v_ref, g_ref, ret_ref, st_ref, dm_ref, dq_ref, dk_ref, dv_ref, dg_ref, g_acc, gs):
        n = pl.program_id(1)

        @pl.when(n == 0)
        def _():
            g_acc[...] = jnp.zeros_like(g_acc)

        decay, zeta, xi, gch = _ret_factors(c_ref[0:1, :])
        ret_v = ret_ref[...]
        rr = lax.rsqrt(jnp.mean(ret_v * ret_v, axis=-1, keepdims=True) + EPS)
        gv = g_ref[...]
        sg = _sigmoid(gv)
        dmix = dm_ref[...]
        dg_ref[...] = ((dmix * (ret_v * rr)) * (sg * (1.0 + gv * (1.0 - sg)))).astype(BF16)
        dretn = dmix * (gv * sg)
        dret = _chunks(rr * dretn - ret_v * ((rr * rr * rr) * jnp.mean(dretn * ret_v, axis=-1, keepdims=True)))

        q3 = _chunks(q_ref[...].astype(BF16))
        kc = _chunks(k_ref[...] * (1.0 / math.sqrt(RHD)))
        k3 = kc.astype(BF16)
        v3 = _chunks(v_ref[...].astype(BF16))
        d3 = dret.astype(BF16)
        dxi = (dret * xi[None]).astype(BF16)
        kz = (kc * zeta[None]).astype(BF16)
        dr3 = _bdot_tn(q3, dxi)
        acc = g_acc[...]
        for i in reversed(range(CBK)):
            gs[i] = acc.astype(BF16)
            acc = dr3[i] + gch * acc
        g_acc[...] = acc
        g3 = gs[...]
        sc = (_bdot_nt(q3, k3) * decay[None]).astype(BF16)
        da = (_bdot_nt(d3, v3) * decay[None]).astype(BF16)
        dq = _bdot(da, k3) + _bdot_nt(dxi, st_ref[...])
        dkc = _bdot_tn(da, q3) + _bdot_nt(v3, g3) * zeta[None]
        dv = _bdot_tn(sc, d3) + _bdot(kz, g3)
        dq_ref[...] = dq.reshape(CBK * CH, RHD).astype(BF16)
        dk_ref[...] = (dkc * (1.0 / math.sqrt(RHD))).reshape(CBK * CH, RHD).astype(BF16)
        dv_ref[...] = dv.reshape(CBK * CH, RHD).astype(BF16)

    col, own, state, const, dm = _ret_specs(True)
    return pl.pallas_call(
        body, name="ret_bwd", grid=(RH, RSTEPS),
        in_specs=[const, col(0), col(1), col(2), col(3), own, state, dm],
        out_specs=[own, own, own, own],
        out_shape=[jax.ShapeDtypeStruct((S, RH * RHD), BF16)] * 4,
        scratch_shapes=[pltpu.VMEM((RHD, RHD), F32), pltpu.VMEM((CBK, RHD, RHD), BF16)],
        compiler_params=_cp(("parallel", "arbitrary")),
    )(_ret_consts(), proj, proj, proj, proj, ret, states, dmixed)


class _NoReduction:
    def start(self, group, grads):
        pass

    def local(self, name, first=()):
        return []

    def landed(self, name):
        return []

    def update(self, name):
        return []


def _local_step(x, tgt, nw1, nw2, nw3, win, wout, wg, wu, wd, red=None):
    red = red or _NoReduction()

    def after(values, first):
        return lax.optimization_barrier((tuple(values), tuple(first)))[0]

    wg, wu, wd = (w.reshape(NFG, N_FG, D) for w in (wg, wu, wd))
    h1, r1 = _rms_fwd(x, nw1)
    proj = _proj(h1, win)
    o, ma, lse = _attn_fwd(proj)
    ret, mr, states = _ret_fwd(proj)
    x2, h2, r2 = _out_proj_rms(x, ma, mr, wout, nw2)
    a, dadg, dadu = _ffn_up(h2, wg, wu)
    dx3, dx3b, st3 = _ffn_down_loss(x2, a, wd, nw3, tgt)

    dwd = _wgrad_rows(a, dx3b, "wgrad_down")
    red.start(["w_down"], [dwd])
    (dx3b,) = after([dx3b], [dwd])
    dg, du = _ffn_down_bwd(dx3b, wd, dadg, dadu)
    dg, du = after([dg, du], red.local("w_down", first=[dg]))
    dwg = _wgrad_rows(dg, h2, "wgrad_gate")
    red.start(["w_gate"], [dwg])
    (du,) = after([du], [dwg])
    dwu = _wgrad_rows(du, h2, "wgrad_up")
    red.start(["w_up"], [dwu])
    dg, du = after([dg, du], [dwu] + red.local("w_gate"))
    dx2, dx2b, st2 = _ffn_up_bwd(dg, du, wg, wu, dx3, x2, r2, nw2)
    (dx2b,) = after([dx2b], red.local("w_up", first=[dx2b] + red.landed("w_down")))
    dwo = _wgrad_out(ma, mr, dx2b)
    red.start(["w_out"], [dwo])
    (dx2b,) = after([dx2b], [dwo])
    dmixed = _out_proj_bwd(dx2b, wout)
    dqa, dka, dva = _attn_bwd(proj, dmixed, o, lse)
    (dmixed,) = after([dmixed], red.local("w_out", first=[dqa] + red.landed("w_gate")))
    dqr, dkr, dvr, dgr = _ret_bwd(proj, ret, states, dmixed)
    dproj = jnp.concatenate([dqa, dka, dva, dqr, dkr, dvr, dgr], axis=1)
    (dwi0,) = after([_wgrad_in(h1, dproj, 0)], red.landed("w_up"))
    red.start(["w_in_0"], [dwi0])
    (dproj,) = after([dproj], [dwi0])
    dwi1 = _wgrad_in(h1, dproj, 1)
    red.start(["w_in_1"], [dwi1])
    sums = red.local("w_in_0", first=[dwi1] + red.landed("w_out"))
    sums = red.local("w_in_1", first=sums + red.update("w_down"))
    (dproj,) = after([dproj], sums)
    gx, st1 = _in_proj_bwd(dproj, win, dx2, x, r1, nw1)
    dwi = jnp.concatenate([dwi0, dwi1], axis=1)
    stats = jnp.concatenate([st1[0:1], st2[0:1], st3[0:2], jnp.zeros((4, D), F32)], axis=0)
    return stats, gx, dwi, dwo, dwg, dwu, dwd


def _place():
    x, y, c = lax.axis_index("x"), lax.axis_index("y"), lax.axis_index("c")
    return x, y, c, [(1 - x, y), (x, 1 - y), (1 - x, 1 - y)]


def _handshake(peers):
    barrier = pltpu.get_barrier_semaphore()
    for peer in peers:
        pl.semaphore_signal(barrier, inc=1, device_id=peer, device_id_type=MESH)
    pl.semaphore_wait(barrier, len(peers))


def _all_gather(shards, name, collective_id):
    na = len(shards)
    SIB, XN0, XN1, YN1, YN0, VIA_X, VIA_Y = 0, 1, 2, 3, 4, 5, 6
    D2D = {XN0: 7, XN1: 8, YN1: 9, YN0: 10, VIA_X: 11, VIA_Y: 12}

    def body(*refs):
        ins, outs = refs[:na], refs[na:2 * na]
        send_sems, recv_sems, local_sems = refs[2 * na:]
        x, y, c, _ = _place()
        me, sib = (x, y, c), (x, y, 1 - c)
        xn, yn, dg = (1 - x, y, c), (x, 1 - y, c), (1 - x, 1 - y, c)
        _handshake([sib, xn, yn])

        def part(ref, h):
            rows = ref.shape[0] // 2
            return ref if h is None else ref.at[pl.ds(h * rows, rows)]

        def block(a, owner, h):
            return part(outs[a].at[4 * owner[0] + 2 * owner[1] + owner[2]], h)

        def copy(a, k, owner, h, to, own_src=False):
            return pltpu.make_async_remote_copy(
                src_ref=part(ins[a], h) if own_src else block(a, owner, h), dst_ref=block(a, owner, h),
                send_sem=send_sems.at[a, k], recv_sem=recv_sems.at[a, k], device_id=to, device_id_type=MESH)

        def other(p):
            return (p[0], p[1], 1 - c)

        mine = [pltpu.make_async_copy(ins[a], block(a, me, None), local_sems.at[a]) for a in range(na)]
        for cp in mine:
            cp.start()
        sent = []
        for a in range(na):
            sent += [copy(a, XN0, me, 0, xn, True), copy(a, YN1, me, 1, yn, True),
                     copy(a, XN1, me, 1, xn, True), copy(a, YN0, me, 0, yn, True)]
        sent += [copy(a, SIB, me, None, sib, True) for a in range(na)]
        for cp in sent:
            cp.start()

        def landed(a, k, owner, h, then):
            copy(a, k, owner, h, me).wait_recv()
            for k2, to in then + [(D2D[k], sib)]:
                cp = copy(a, k2, owner, h, to)
                cp.start()
                sent.append(cp)

        for a in range(na):
            landed(a, XN0, xn, 0, [(VIA_Y, yn)])
            landed(a, YN1, yn, 1, [(VIA_X, xn)])
            landed(a, XN1, xn, 1, [])
            landed(a, YN0, yn, 0, [])
        for a in range(na):
            landed(a, VIA_Y, dg, 0, [])
            landed(a, VIA_X, dg, 1, [])
        for a in range(na):
            copy(a, SIB, sib, None, me).wait_recv()
            for k, owner, h in ((XN0, xn, 0), (XN1, xn, 1), (YN1, yn, 1), (YN0, yn, 0), (VIA_Y, dg, 0), (VIA_X, dg, 1)):
                copy(a, D2D[k], other(owner), h, me).wait_recv()
        for cp in sent:
            cp.wait_send()
        for cp in mine:
            cp.wait()

    return _sequencer_call(
        body, name, collective_id,
        [jax.ShapeDtypeStruct((NDEV,) + s.shape, s.dtype) for s in shards],
        [pltpu.SemaphoreType.DMA((na, 13)), pltpu.SemaphoreType.DMA((na, 13)), pltpu.SemaphoreType.DMA((na,))])(*shards)


def _sequencer_call(body, name, collective_id, out_type, scratch_types):
    return pl.kernel(
        body, name=name, out_type=out_type,
        mesh=plsc.ScalarSubcoreMesh(axis_name="sequencer", num_cores=1),
        scratch_types=scratch_types,
        compiler_params=pltpu.CompilerParams(collective_id=collective_id))


def _exchange_sibling(grads, name, collective_id):
    na = len(grads)

    def body(*refs):
        ins, outs = refs[:na], refs[na:2 * na]
        send_sems, recv_sems = refs[2 * na:]
        x, y, c, _ = _place()
        _handshake([(x, y, 1 - c)])
        cps = []
        for a in range(na):
            for k in range(4):
                cps.append(pltpu.make_async_remote_copy(
                    src_ref=ins[a].at[2 * k + (1 - c)], dst_ref=outs[a].at[k],
                    send_sem=send_sems.at[a, k], recv_sem=recv_sems.at[a, k],
                    device_id=(x, y, 1 - c), device_id_type=MESH))
        for cp in cps:
            cp.start()
        for cp in cps:
            cp.wait()

    return _sequencer_call(
        body, name, collective_id,
        [jax.ShapeDtypeStruct((4,) + g.shape[1:], g.dtype) for g in grads],
        [pltpu.SemaphoreType.DMA((na, 4)), pltpu.SemaphoreType.DMA((na, 4))])(*grads)


def _row_tile(rows, cols):
    for t in (512, 256, 176, 128, 64, 32, 16):
        if rows % t == 0 and t * cols * 4 <= (2 << 20):
            return t
    raise ValueError((rows, cols))


def _chip_sum(place, g, got, name):
    _, r, c = g.shape
    tm = r

    def body(pos_ref, g_ref, got_ref, o_ref):
        o_ref[...] = (g_ref[...].astype(F32) + got_ref[...].astype(F32)).astype(BF16)

    def chip(j, pos):
        return 2 * (pos[0] ^ jnp.where(j == 1, 0, 1)) + (pos[1] ^ jnp.where(j == 0, 0, 1))

    return pl.pallas_call(
        body, name=name,
        grid_spec=pltpu.PrefetchScalarGridSpec(
            num_scalar_prefetch=1, grid=(3, r // tm),
            in_specs=[pl.BlockSpec((None, tm, c), lambda j, i, pos: (2 * chip(j, pos) + pos[2], i, 0)),
                      pl.BlockSpec((None, tm, c), lambda j, i, pos: (chip(j, pos), i, 0))],
            out_specs=pl.BlockSpec((None, tm, c), lambda j, i, pos: (j, i, 0))),
        out_shape=jax.ShapeDtypeStruct((3, r, c), BF16),
        compiler_params=_cp(("parallel", "parallel")),
    )(place, g, got)


def _exchange_chips(sums, name, collective_id):
    na = len(sums)

    def body(*refs):
        ins, outs = refs[:na], refs[na:2 * na]
        send_sems, recv_sems = refs[2 * na:]
        x, y, c, chips = _place()
        _handshake([(*chip, c) for chip in chips])
        cps = []
        for a in range(na):
            for j, chip in enumerate(chips):
                cps.append(pltpu.make_async_remote_copy(
                    src_ref=ins[a].at[j], dst_ref=outs[a].at[j],
                    send_sem=send_sems.at[a, j], recv_sem=recv_sems.at[a, j],
                    device_id=(*chip, c), device_id_type=MESH))
        for cp in cps:
            cp.start()
        for cp in cps:
            cp.wait()

    return _sequencer_call(
        body, name, collective_id,
        [jax.ShapeDtypeStruct((3,) + s.shape[1:], s.dtype) for s in sums],
        [pltpu.SemaphoreType.DMA((na, 3)), pltpu.SemaphoreType.DMA((na, 3))])(*sums)


def _exchange_stats(stats, collective_id):
    def body(st_in, st_out, st_send, st_recv, local_sem):
        x, y, c, _ = _place()
        me_idx = 4 * x + 2 * y + c
        peers = [(x ^ ((k >> 2) & 1), y ^ ((k >> 1) & 1), c ^ (k & 1)) for k in range(1, 8)]
        _handshake(peers)
        mine = pltpu.make_async_copy(st_in, st_out.at[me_idx], local_sem)
        mine.start()
        cps = [pltpu.make_async_remote_copy(
            src_ref=st_in, dst_ref=st_out.at[me_idx], send_sem=st_send.at[k], recv_sem=st_recv.at[k],
            device_id=peer, device_id_type=MESH) for k, peer in enumerate(peers)]
        for cp in cps:
            cp.start()
        for cp in cps:
            cp.wait()
        mine.wait()

    return _sequencer_call(
        body, "exchange_stats", collective_id,
        jax.ShapeDtypeStruct((NDEV,) + stats.shape, stats.dtype),
        [pltpu.SemaphoreType.DMA((7,)), pltpu.SemaphoreType.DMA((7,)), pltpu.SemaphoreType.DMA])(stats)


class _Reduction:
    def __init__(self, place, first_collective_id, state):
        self.place = place
        self.ids = iter(range(first_collective_id, 32))
        self.state = state
        self.groups = {}
        self.updates = {}

    def next_id(self):
        return next(self.ids)

    def start(self, group, grads):
        got = _exchange_sibling(grads, "sibling_exchange_" + group[0], self.next_id())
        self.groups[group[0]] = dict(names=group, grads=grads, got=got)

    def local(self, name, first=()):
        grp = self.groups[name]
        grads = lax.optimization_barrier((tuple(grp["grads"]), tuple(first)))[0]
        grp["sums"] = [_chip_sum(self.place, g, s, "chip_sum_" + n)
                       for g, s, n in zip(grads, grp["got"], grp["names"])]
        grp["chips"] = _exchange_chips(grp["sums"], "chip_exchange_" + name, self.next_id())
        return grp["sums"]

    def landed(self, name):
        return list(self.groups[name]["chips"])

    def update(self, name):
        if name not in self.updates:
            grp = next(g for g in self.groups.values() if name in g["names"])
            k = grp["names"].index(name)
            w, m, v, part, parts = self.state[name]
            before = self.update(f"{name[:-1]}{part - 1}") if part else None
            self.updates[name] = _shard_update(self.place, w, m, v, grp["grads"][k], grp["got"][k],
                                               grp["chips"][k], "update_" + name, part, parts, before)
        return list(self.updates[name])


def _adamw(w, g, m, v):
    m = ADAM_B1 * m + (1.0 - ADAM_B1) * g
    v = ADAM_B2 * v + (1.0 - ADAM_B2) * (g * g)
    m_hat = m / (1.0 - ADAM_B1 ** ADAM_STEP)
    v_hat = v / (1.0 - ADAM_B2 ** ADAM_STEP)
    delta = -ADAM_LR * (m_hat / (jnp.sqrt(v_hat) + ADAM_EPS) + ADAM_WD * w)
    return delta, m, v


def _shard_update(place, w, m, v, g, got_sib, got_chips, name, part=0, parts=1, before=None):
    r, c = w.shape
    rp = r // parts
    tm = _row_tile(rp, c)
    off = part * (rp // tm)

    def body(pos_ref, w_ref, m_ref, v_ref, g_ref, s_ref, c_ref, *rest):
        go_ref, d_ref, mo_ref, vo_ref = rest[-4:]
        grad = g_ref[...].astype(F32) + s_ref[...].astype(F32)
        for j in range(3):
            grad = grad + c_ref[j].astype(F32)
        delta, mn, vn = _adamw(w_ref[...], grad, m_ref[...], v_ref[...])
        go_ref[...] = grad
        d_ref[...] = delta
        mo_ref[...] = mn
        vo_ref[...] = vn

    row = pl.BlockSpec((tm, c), lambda i, pos: (i + off, 0))
    before = list(before or [])
    return pl.pallas_call(
        body, name=name,
        grid_spec=pltpu.PrefetchScalarGridSpec(
            num_scalar_prefetch=1, grid=(rp // tm,),
            in_specs=[row, row, row,
                      pl.BlockSpec((None, tm, c), lambda i, pos: (4 * pos[0] + 2 * pos[1] + pos[2], i, 0)),
                      pl.BlockSpec((None, tm, c), lambda i, pos: (2 * pos[0] + pos[1], i, 0)),
                      pl.BlockSpec((3, tm, c), lambda i, pos: (0, i, 0))]
            + [pl.BlockSpec(memory_space=pl.ANY)] * len(before),
            out_specs=[row, row, row, row]),
        out_shape=[jax.ShapeDtypeStruct((r, c), F32)] * 4,
        input_output_aliases={7 + k: k for k in range(len(before))},
        compiler_params=_cp(("parallel",)),
    )(place, w, m, v, g, got_sib, got_chips, *before)


def _small_update(stats_all, ws, ms, vs):
    def body(st_ref, w_ref, m_ref, v_ref, go_ref, d_ref, mo_ref, vo_ref):
        grad = st_ref[0]
        for k in range(1, NDEV):
            grad = grad + st_ref[k]
        delta, mn, vn = _adamw(w_ref[...], grad, m_ref[...], v_ref[...])
        go_ref[...] = grad
        d_ref[...] = delta
        mo_ref[...] = mn
        vo_ref[...] = vn

    return pl.pallas_call(
        body, name="small_update",
        out_shape=[jax.ShapeDtypeStruct((8, D), F32)] * 4,
        compiler_params=_cp(),
    )(stats_all, ws, ms, vs)


def kernel(x, norm_mix_w, w_in, w_out, norm_ffn_w, w_gate, w_up, w_down, norm_final_w, loss_target, m_norm_mix_w, m_w_in, m_w_out, m_norm_ffn_w, m_w_gate, m_w_up, m_w_down, m_norm_final_w, v_norm_mix_w, v_w_in, v_w_out, v_norm_ffn_w, v_w_gate, v_w_up, v_w_down, v_norm_final_w):
    tr = {"w_gate", "w_up"}
    names = ["w_in", "w_out", "w_gate", "w_up", "w_down"]

    def view(a, n):
        return a[0].T if n in tr else a[0]

    big_w = [view(a, n) for a, n in zip([w_in, w_out, w_gate, w_up, w_down], names)]
    big_m = [view(a, n) for a, n in zip([m_w_in, m_w_out, m_w_gate, m_w_up, m_w_down], names)]
    big_v = [view(a, n) for a, n in zip([v_w_in, v_w_out, v_w_gate, v_w_up, v_w_down], names)]

    shards = [_cast_bf16(w, "cast_" + n) for w, n in zip(big_w, names)]
    (win,) = _all_gather(shards[0:1], "all_gather_w_in", 1)
    (wout,) = _all_gather(shards[1:2], "all_gather_w_out", 2)
    wg, wu = _all_gather(shards[2:4], "all_gather_gate_up", 3)
    (wd,) = _all_gather(shards[4:5], "all_gather_w_down", 4)
    nw3 = norm_final_w.reshape(1, D)
    place = jnp.stack([lax.axis_index("x"), lax.axis_index("y"), lax.axis_index("c")]).astype(jnp.int32)
    state = {n: (w, m, v, 0, 1) for n, w, m, v in zip(names, big_w, big_m, big_v)}
    for part in range(W_IN_PARTS):
        state[f"w_in_{part}"] = state["w_in"][:3] + (part, W_IN_PARTS)
    red = _Reduction(place, 5, state)
    stats, gx, *_ = _local_step(
        x[0], loss_target[0], norm_mix_w, norm_ffn_w, nw3, win, wout.reshape(D, D), wg, wu, wd, red)
    stats_all = _exchange_stats(stats, red.next_id())
    upd = [red.update(f"w_in_{W_IN_PARTS - 1}" if n == "w_in" else n) for n in names]
    stats_all = lax.optimization_barrier((stats_all, tuple(upd[0])))[0]

    def rows(a, b, c):
        return jnp.concatenate([a.reshape(1, D), b.reshape(1, D), c.reshape(1, D), jnp.zeros((5, D), F32)], axis=0)

    sg, sd, sm, sv = _small_update(stats_all, rows(norm_mix_w, norm_ffn_w, norm_final_w),
                                   rows(m_norm_mix_w, m_norm_ffn_w, m_norm_final_w),
                                   rows(v_norm_mix_w, v_norm_ffn_w, v_norm_final_w))
    loss = sg[3, 0]

    def outs(k, small):
        big = [(u[k].T if n in tr else u[k])[None] for u, n in zip(upd, names)]
        return [small[0:1], big[0], big[1], small[1:2], big[2], big[3], big[4], small[2]]

    return (loss, gx[None], *outs(0, sg), *outs(1, sd), *outs(2, sm), *outs(3, sv))
```

```python
import functools
import math

import numpy as np
import jax
import jax.numpy as jnp
from jax import lax
from jax.experimental import pallas as pl
from jax.experimental.pallas import tpu as pltpu
from jax.experimental.pallas import tpu_sc as plsc

F32 = jnp.float32
BF16 = jnp.bfloat16

S = 2048
D = 2048
NDEV = 8
N_IN = 7168 // NDEV
N_FF = 5632 // NDEV
NFG, N_FG = NDEV // 2, 2 * N_FF
N_OUT = 2048 // NDEV
AH, AHD = 8, 128
RH, RHD = 4, 256
CH = 128
NB = S // CH
EPS = 1e-6
PATTERNS = ((1, 16), (4, 4), (16, 1))
NEG = -1e30
VMEM_LIMIT = 56 * 1024 * 1024

ADAM_LR, ADAM_B1, ADAM_B2, ADAM_EPS, ADAM_WD, ADAM_STEP = 0.001, 0.9, 0.999, 1e-08, 0.01, 10
MESH = pl.DeviceIdType.MESH


def _cp(sem=None):
    return pltpu.CompilerParams(dimension_semantics=sem, vmem_limit_bytes=VMEM_LIMIT)


def _dot(a, b):
    return jnp.dot(a, b, preferred_element_type=F32)


def _dot_nt(a, b):
    return lax.dot_general(a, b, (((1,), (1,)), ((), ())), preferred_element_type=F32)


def _dot_tn(a, b):
    return lax.dot_general(a, b, (((0,), (0,)), ((), ())), preferred_element_type=F32)


def _sigmoid(x):
    return 0.5 * jnp.tanh(0.5 * x) + 0.5


def _cast_bf16(w, name):
    r, c = w.shape
    tm = r if r <= 1024 else 512

    def body(w_ref, o_ref):
        o_ref[...] = w_ref[...].astype(BF16)

    return pl.pallas_call(
        body, name=name, grid=(r // tm,),
        in_specs=[pl.BlockSpec((tm, c), lambda i: (i, 0))],
        out_specs=pl.BlockSpec((tm, c), lambda i: (i, 0)),
        out_shape=jax.ShapeDtypeStruct((r, c), BF16),
        compiler_params=_cp(("parallel",)),
    )(w)


def _rms_fwd(x, nw):
    tm = 256

    def body(x_ref, w_ref, h_ref, r_ref):
        xs = x_ref[...]
        r = lax.rsqrt(jnp.mean(xs * xs, axis=-1, keepdims=True) + EPS)
        h_ref[...] = ((xs * r) * w_ref[...]).astype(BF16)
        r_ref[...] = r

    return pl.pallas_call(
        body, name="rms_fwd", grid=(S // tm,),
        in_specs=[pl.BlockSpec((tm, D), lambda i: (i, 0)), pl.BlockSpec((1, D), lambda i: (0, 0))],
        out_specs=[pl.BlockSpec((tm, D), lambda i: (i, 0)), pl.BlockSpec((tm, 1), lambda i: (i, 0))],
        out_shape=[jax.ShapeDtypeStruct((S, D), BF16), jax.ShapeDtypeStruct((S, 1), F32)],
        compiler_params=_cp(("parallel",)),
    )(x, nw)


def _row_copies(hbm_refs, bufs, sems, m, tm):
    rows = pl.ds(pl.multiple_of(m * tm, tm), tm)
    return [pltpu.make_async_copy(h.at[rows], b, sems.at[i]) for i, (h, b) in enumerate(zip(hbm_refs, bufs))]


def _rms_bwd_tile(dh, xs, r, nw):
    dnw = jnp.sum(dh * (xs * r), axis=0, keepdims=True)
    gy = dh * nw
    dx = r * gy - xs * ((r * r * r) * jnp.mean(gy * xs, axis=-1, keepdims=True))
    return dx, dnw


def _proj(h1, win):
    tm = 1024

    def body(a_ref, w_ref, o_ref):
        o_ref[...] = _dot(a_ref[...], w_ref[...])

    return pl.pallas_call(
        body, name="proj", grid=(NDEV, S // tm),
        in_specs=[pl.BlockSpec((tm, D), lambda p, m: (m, 0)),
                  pl.BlockSpec((None, D, N_IN), lambda p, m: (p, 0, 0))],
        out_specs=pl.BlockSpec((tm, N_IN), lambda p, m: (m, p)),
        out_shape=jax.ShapeDtypeStruct((S, NDEV * N_IN), F32),
        compiler_params=_cp(("parallel", "parallel")),
    )(h1, win)


def _out_proj_rms(x, ma, mr, wout, nw):
    tm = 256
    half = D // 2

    def body(x_ref, ma_ref, mr_ref, w_ref, nw_ref, x2_ref, h_ref, r_ref):
        acc = _dot(ma_ref[...], w_ref[0:half, :]) + _dot(mr_ref[...], w_ref[half:D, :])
        x2 = x_ref[...] + acc
        r = lax.rsqrt(jnp.mean(x2 * x2, axis=-1, keepdims=True) + EPS)
        x2_ref[...] = x2
        h_ref[...] = ((x2 * r) * nw_ref[...]).astype(BF16)
        r_ref[...] = r

    return pl.pallas_call(
        body, name="out_proj_rms", grid=(S // tm,),
        in_specs=[pl.BlockSpec((tm, D), lambda i: (i, 0)),
                  pl.BlockSpec((tm, half), lambda i: (i, 0)),
                  pl.BlockSpec((tm, half), lambda i: (i, 0)),
                  pl.BlockSpec((D, D), lambda i: (0, 0)),
                  pl.BlockSpec((1, D), lambda i: (0, 0))],
        out_specs=[pl.BlockSpec((tm, D), lambda i: (i, 0)), pl.BlockSpec((tm, D), lambda i: (i, 0)),
                   pl.BlockSpec((tm, 1), lambda i: (i, 0))],
        out_shape=[jax.ShapeDtypeStruct((S, D), F32), jax.ShapeDtypeStruct((S, D), BF16),
                   jax.ShapeDtypeStruct((S, 1), F32)],
        compiler_params=_cp(("parallel",)),
    )(x, ma, mr, wout, nw)


def _ffn_up(h2, wg, wu):
    tm = 512

    def body(h_ref, wg_ref, wu_ref, a_ref, dadg_ref, dadu_ref):
        h = h_ref[...]
        g = _dot_nt(h, wg_ref[...])
        u = _dot_nt(h, wu_ref[...])
        sg = _sigmoid(g)
        silu = g * sg
        a_ref[...] = (silu * u).astype(BF16)
        dadg_ref[...] = (u * (sg * (1.0 + g * (1.0 - sg)))).astype(BF16)
        dadu_ref[...] = silu.astype(BF16)

    blk = pl.BlockSpec((None, tm, N_FG), lambda p, m: (p, m, 0))
    wblk = pl.BlockSpec((None, N_FG, D), lambda p, m: (p, 0, 0))
    return pl.pallas_call(
        body, name="ffn_up", grid=(NFG, S // tm),
        in_specs=[pl.BlockSpec((tm, D), lambda p, m: (m, 0)), wblk, wblk],
        out_specs=[blk, blk, blk],
        out_shape=[jax.ShapeDtypeStruct((NFG, S, N_FG), BF16)] * 3,
        compiler_params=_cp(("parallel", "parallel")),
    )(h2, wg, wu)


def _ffn_down_loss(x2, a, wd, nw, tgt):
    tm = 512

    def body(x2_hbm, a_ref, w_ref, nw_ref, t_hbm, dx_ref, dxb_ref, st_ref, acc_ref, x2_buf, t_buf, sems):
        m, p = pl.program_id(0), pl.program_id(1)
        tail_in = _row_copies((x2_hbm, t_hbm), (x2_buf, t_buf), sems, m, tm)

        @pl.when(p == 0)
        def _():
            acc_ref[...] = jnp.zeros_like(acc_ref)
            for cp in tail_in:
                cp.start()

        @pl.when((p == 0) & (m == 0))
        def _():
            st_ref[...] = jnp.zeros_like(st_ref)

        acc_ref[...] += _dot(a_ref[...], w_ref[...])

        @pl.when(p == NFG - 1)
        def _():
            for cp in tail_in:
                cp.wait()
            x3 = x2_buf[...] + acc_ref[...]
            nwv = nw_ref[...]
            r = lax.rsqrt(jnp.mean(x3 * x3, axis=-1, keepdims=True) + EPS)
            y = (x3 * r) * nwv
            err = y - t_buf[...]
            loss = 0.5 * jnp.sum(jnp.mean(err * err, axis=-1, keepdims=True), axis=0, keepdims=True)
            dy = err * (1.0 / D)
            dx, dnw = _rms_bwd_tile(dy, x3, r, nwv)
            dx_ref[...] = dx
            dxb_ref[...] = dx.astype(BF16)
            st_ref[0:1, :] += dnw
            st_ref[1:2, :] += jnp.broadcast_to(loss, (1, D))

    return pl.pallas_call(
        body, name="ffn_down_loss", grid=(S // tm, NFG),
        in_specs=[pl.BlockSpec(memory_space=pl.ANY),
                  pl.BlockSpec((None, tm, N_FG), lambda m, p: (p, m, 0)),
                  pl.BlockSpec((None, N_FG, D), lambda m, p: (p, 0, 0)),
                  pl.BlockSpec((1, D), lambda m, p: (0, 0)),
                  pl.BlockSpec(memory_space=pl.ANY)],
        out_specs=[pl.BlockSpec((tm, D), lambda m, p: (m, 0)), pl.BlockSpec((tm, D), lambda m, p: (m, 0)),
                   pl.BlockSpec((8, D), lambda m, p: (0, 0))],
        out_shape=[jax.ShapeDtypeStruct((S, D), F32), jax.ShapeDtypeStruct((S, D), BF16),
                   jax.ShapeDtypeStruct((8, D), F32)],
        scratch_shapes=[pltpu.VMEM((tm, D), F32), pltpu.VMEM((tm, D), F32), pltpu.VMEM((tm, D), F32),
                        pltpu.SemaphoreType.DMA((2,))],
        compiler_params=_cp(("arbitrary", "arbitrary")),
    )(x2, a, wd, nw, tgt)


def _ffn_down_bwd(dx3b, wd, dadg, dadu):
    tm = 1024

    def body(dx_ref, w_ref, dadg_ref, dadu_ref, dg_ref, du_ref):
        da = _dot_nt(dx_ref[...], w_ref[...])
        dg_ref[...] = (da * dadg_ref[...].astype(F32)).astype(BF16)
        du_ref[...] = (da * dadu_ref[...].astype(F32)).astype(BF16)

    blk = pl.BlockSpec((None, tm, N_FG), lambda p, m: (p, m, 0))
    return pl.pallas_call(
        body, name="ffn_down_bwd", grid=(NFG, S // tm),
        in_specs=[pl.BlockSpec((tm, D), lambda p, m: (m, 0)),
                  pl.BlockSpec((None, N_FG, D), lambda p, m: (p, 0, 0)), blk, blk],
        out_specs=[blk, blk],
        out_shape=[jax.ShapeDtypeStruct((NFG, S, N_FG), BF16)] * 2,
        compiler_params=_cp(("parallel", "parallel")),
    )(dx3b, wd, dadg, dadu)


def _ffn_up_bwd(dg, du, wg, wu, dres, xs, r, nw):
    tm = 512

    def body(dg_ref, du_ref, wg_ref, wu_ref, dres_hbm, x_hbm, r_ref, nw_ref, dx_ref, dxb_ref, st_ref,
             dres_buf, x_buf, sems):
        m, p = pl.program_id(0), pl.program_id(1)
        tail_in = _row_copies((dres_hbm, x_hbm), (dres_buf, x_buf), sems, m, tm)

        @pl.when(p == 0)
        def _():
            dx_ref[...] = jnp.zeros_like(dx_ref)
            for cp in tail_in:
                cp.start()

        @pl.when((p == 0) & (m == 0))
        def _():
            st_ref[...] = jnp.zeros_like(st_ref)

        dx_ref[...] += _dot(dg_ref[...], wg_ref[...])
        dx_ref[...] += _dot(du_ref[...], wu_ref[...])

        @pl.when(p == NFG - 1)
        def _():
            for cp in tail_in:
                cp.wait()
            dx, dnw = _rms_bwd_tile(dx_ref[...], x_buf[...], r_ref[...], nw_ref[...])
            dx = dres_buf[...] + dx
            dx_ref[...] = dx
            dxb_ref[...] = dx.astype(BF16)
            st_ref[0:1, :] += dnw

    blk = pl.BlockSpec((None, tm, N_FG), lambda m, p: (p, m, 0))
    wblk = pl.BlockSpec((None, N_FG, D), lambda m, p: (p, 0, 0))
    row = pl.BlockSpec((tm, D), lambda m, p: (m, 0))
    hbm = pl.BlockSpec(memory_space=pl.ANY)
    return pl.pallas_call(
        body, name="ffn_up_bwd", grid=(S // tm, NFG),
        in_specs=[blk, blk, wblk, wblk, hbm, hbm, pl.BlockSpec((tm, 1), lambda m, p: (m, 0)),
                  pl.BlockSpec((1, D), lambda m, p: (0, 0))],
        out_specs=[row, row, pl.BlockSpec((8, D), lambda m, p: (0, 0))],
        out_shape=[jax.ShapeDtypeStruct((S, D), F32), jax.ShapeDtypeStruct((S, D), BF16),
                   jax.ShapeDtypeStruct((8, D), F32)],
        scratch_shapes=[pltpu.VMEM((tm, D), F32), pltpu.VMEM((tm, D), F32), pltpu.SemaphoreType.DMA((2,))],
        compiler_params=_cp(("arbitrary", "arbitrary")),
    )(dg, du, wg, wu, dres, xs, r, nw)


def _out_proj_bwd(dx2b, wout):
    tm = 256

    def body(dx_ref, w_ref, o_ref):
        o_ref[...] = _dot_nt(dx_ref[...], w_ref[...])

    return pl.pallas_call(
        body, name="out_proj_bwd", grid=(S // tm,),
        in_specs=[pl.BlockSpec((tm, D), lambda i: (i, 0)), pl.BlockSpec((D, D), lambda i: (0, 0))],
        out_specs=pl.BlockSpec((tm, D), lambda i: (i, 0)),
        out_shape=jax.ShapeDtypeStruct((S, D), F32),
        compiler_params=_cp(("parallel",)),
    )(dx2b, wout)


def _in_proj_bwd(dproj, win, dres, xs, r, nw):
    tm = 1024

    def body(dp_ref, w_ref, dres_hbm, x_hbm, r_ref, nw_ref, dx_ref, st_ref, dres_buf, x_buf, sems):
        m, p = pl.program_id(0), pl.program_id(1)
        tail_in = _row_copies((dres_hbm, x_hbm), (dres_buf, x_buf), sems, m, tm)

        @pl.when(p == 0)
        def _():
            dx_ref[...] = jnp.zeros_like(dx_ref)
            for cp in tail_in:
                cp.start()

        @pl.when((p == 0) & (m == 0))
        def _():
            st_ref[...] = jnp.zeros_like(st_ref)

        dx_ref[...] += _dot_nt(dp_ref[...], w_ref[...])

        @pl.when(p == NDEV - 1)
        def _():
            for cp in tail_in:
                cp.wait()
            dx, dnw = _rms_bwd_tile(dx_ref[...], x_buf[...], r_ref[...], nw_ref[...])
            dx_ref[...] = dres_buf[...] + dx
            st_ref[0:1, :] += dnw

    row = pl.BlockSpec((tm, D), lambda m, p: (m, 0))
    hbm = pl.BlockSpec(memory_space=pl.ANY)
    return pl.pallas_call(
        body, name="in_proj_bwd", grid=(S // tm, NDEV),
        in_specs=[pl.BlockSpec((tm, N_IN), lambda m, p: (m, p)),
                  pl.BlockSpec((None, D, N_IN), lambda m, p: (p, 0, 0)),
                  hbm, hbm, pl.BlockSpec((tm, 1), lambda m, p: (m, 0)),
                  pl.BlockSpec((1, D), lambda m, p: (0, 0))],
        out_specs=[row, pl.BlockSpec((8, D), lambda m, p: (0, 0))],
        out_shape=[jax.ShapeDtypeStruct((S, D), F32), jax.ShapeDtypeStruct((8, D), F32)],
        scratch_shapes=[pltpu.VMEM((tm, D), F32), pltpu.VMEM((tm, D), F32), pltpu.SemaphoreType.DMA((2,))],
        compiler_params=_cp(("arbitrary", "arbitrary")),
    )(dproj, win, dres, xs, r, nw)


W_IN_PARTS = 2


def _wgrad_in(h1, dproj, part):
    rows = D // W_IN_PARTS

    def body(a_ref, d_ref, o_ref):
        o_ref[...] = _dot_tn(a_ref[...], d_ref[...]).astype(BF16)

    return pl.pallas_call(
        body, name=f"wgrad_in_{part}", grid=(NDEV,),
        in_specs=[pl.BlockSpec((S, rows), lambda p: (0, part)), pl.BlockSpec((S, N_IN), lambda p: (0, p))],
        out_specs=pl.BlockSpec((None, rows, N_IN), lambda p: (p, 0, 0)),
        out_shape=jax.ShapeDtypeStruct((NDEV, rows, N_IN), BF16),
        compiler_params=_cp(("parallel",)),
    )(h1, dproj)


def _wgrad_rows(a3, dy, name):
    def body(a_ref, d_ref, o_ref):
        o_ref[...] = _dot_tn(a_ref[...], d_ref[...]).astype(BF16)

    return pl.pallas_call(
        body, name=name, grid=(NFG,),
        in_specs=[pl.BlockSpec((None, S, N_FG), lambda p: (p, 0, 0)), pl.BlockSpec((S, D), lambda p: (0, 0))],
        out_specs=pl.BlockSpec((None, N_FG, D), lambda p: (p, 0, 0)),
        out_shape=jax.ShapeDtypeStruct((NFG, N_FG, D), BF16),
        compiler_params=_cp(("parallel",)),
    )(a3, dy).reshape(NDEV, N_FF, D)


def _wgrad_out(ma, mr, dx2b):
    half = D // 2
    per = half // N_OUT

    def body(ma_ref, mr_ref, d_ref, o_ref):
        p = pl.program_id(0)

        @pl.when(p < per)
        def _():
            o_ref[...] = _dot_tn(ma_ref[...], d_ref[...]).astype(BF16)

        @pl.when(p >= per)
        def _():
            o_ref[...] = _dot_tn(mr_ref[...], d_ref[...]).astype(BF16)

    return pl.pallas_call(
        body, name="wgrad_out", grid=(NDEV,),
        in_specs=[pl.BlockSpec((S, N_OUT), lambda p: (0, jnp.minimum(p, per - 1))),
                  pl.BlockSpec((S, N_OUT), lambda p: (0, jnp.maximum(p - per, 0))),
                  pl.BlockSpec((S, D), lambda p: (0, 0))],
        out_specs=pl.BlockSpec((None, N_OUT, D), lambda p: (p, 0, 0)),
        out_shape=jax.ShapeDtypeStruct((NDEV, N_OUT, D), BF16),
        compiler_params=_cp(("parallel",)),
    )(ma, mr, dx2b)


def _attn_consts():
    c = np.zeros((AH, 8, AHD), np.float32)
    for h in range(AH):
        c[h, :, :] = 2.0 ** (-(h + 1))
    return jnp.asarray(c)


def _permute_in(dst, src, d, cast=None):
    v = src[...]
    if d > 1:
        v = pltpu.einshape("jrc->rjc", v.reshape(S // d, d, AHD)).reshape(S, AHD)
    dst[...] = v if cast is None else v.astype(cast)


def _natural_order(v, d):
    if d == 1:
        return v
    return pltpu.einshape("rjc->jrc", v.reshape(d, S // d, AHD)).reshape(S, AHD)


def _attn_masks():
    qi = lax.broadcasted_iota(jnp.int32, (CH, CH), 0)
    kj = lax.broadcasted_iota(jnp.int32, (CH, CH), 1)
    dist_c = (qi - kj).astype(F32)
    dist_p = (qi - kj + CH).astype(F32)
    return (qi >= kj)[None], (kj >= qi)[None], dist_c[None], dist_p[None]


GB = 16


def _bdot_nt(a, b):
    return lax.dot_general(a, b, (((2,), (2,)), ((0,), (0,))), preferred_element_type=F32)


def _bdot(a, b):
    return lax.dot_general(a, b, (((2,), (1,)), ((0,), (0,))), preferred_element_type=F32)


def _bdot_tn(a, b):
    return lax.dot_general(a, b, (((1,), (1,)), ((0,), (0,))), preferred_element_type=F32)


def _shift_block(dst, src):
    dst[0:CH, :] = jnp.zeros((CH, AHD), dst.dtype)
    dst[CH:S, :] = src[0:S - CH, :]


def _has_prev(g, nb):
    blk = lax.broadcasted_iota(jnp.int32, (GB, 1, 1), 0) + g * GB
    return (blk & (nb - 1)) != 0


def _blocks(ref, g):
    return ref[g * GB * CH:(g + 1) * GB * CH, :].reshape(GB, CH, AHD)


def _attn_fwd(proj):
    scale = 1.0 / math.sqrt(AHD)

    def body(c_ref, q_ref, k_ref, v_ref, o_ref, ob_ref, lse_ref, qd, kd, vd, kps, vps, od, ld, *nat):
        onat, lnat = nat[0:3], nat[3:6]
        slope = c_ref[0:1, :]
        mask_c, mask_p, dist_c, dist_p = _attn_masks()
        for pi, (d, nb) in enumerate(PATTERNS):
            _permute_in(qd, q_ref, d, BF16)
            _permute_in(kd, k_ref, d, BF16)
            _permute_in(vd, v_ref, d, BF16)
            if nb > 1:
                _shift_block(kps, kd)
                _shift_block(vps, vd)
            bias_c = -(slope * float(d)) * dist_c
            bias_p = -(slope * float(d)) * dist_p
            for g in range(NB // GB):
                q3, k3, v3 = _blocks(qd, g), _blocks(kd, g), _blocks(vd, g)
                s_c = jnp.where(mask_c, _bdot_nt(q3, k3) * scale + bias_c, NEG)
                mx = jnp.max(s_c, axis=-1, keepdims=True)
                if nb > 1:
                    kp3, vp3 = _blocks(kps, g), _blocks(vps, g)
                    s_p = jnp.where(jnp.logical_and(mask_p, _has_prev(g, nb)),
                                    _bdot_nt(q3, kp3) * scale + bias_p, NEG)
                    mx = jnp.maximum(mx, jnp.max(s_p, axis=-1, keepdims=True))
                    l = (jnp.sum(jnp.exp(s_c - mx), axis=-1, keepdims=True)
                         + jnp.sum(jnp.exp(s_p - mx), axis=-1, keepdims=True))
                    lse = mx + jnp.log(l)
                    o3 = _bdot(jnp.exp(s_c - lse).astype(BF16), v3) + _bdot(jnp.exp(s_p - lse).astype(BF16), vp3)
                else:
                    l = jnp.sum(jnp.exp(s_c - mx), axis=-1, keepdims=True)
                    lse = mx + jnp.log(l)
                    o3 = _bdot(jnp.exp(s_c - lse).astype(BF16), v3)
                rows = slice(g * GB * CH, (g + 1) * GB * CH)
                od[rows, :] = o3.reshape(GB * CH, AHD)
                ld[rows, :] = jnp.broadcast_to(lse, (GB, CH, AHD)).reshape(GB * CH, AHD)
            onat[pi][...] = _natural_order(od[...], d)
            lnat[pi][...] = _natural_order(ld[...], d)
        l0, l1, l2 = lnat[0][...], lnat[1][...], lnat[2][...]
        mx = jnp.maximum(jnp.maximum(l0, l1), l2)
        e0, e1, e2 = jnp.exp(l0 - mx), jnp.exp(l1 - mx), jnp.exp(l2 - mx)
        den = e0 + e1 + e2
        out = (e0 / den) * onat[0][...] + (e1 / den) * onat[1][...] + (e2 / den) * onat[2][...]
        o_ref[...] = out
        ob_ref[...] = out.astype(BF16)
        lse_ref[...] = mx + jnp.log(den)

    def col(off):
        return pl.BlockSpec((S, AHD), lambda h: (0, off + h))

    return pl.pallas_call(
        body, name="attn_fwd", grid=(AH,),
        in_specs=[pl.BlockSpec((None, 8, AHD), lambda h: (h, 0, 0)), col(0), col(AH), col(2 * AH)],
        out_specs=[col(0), col(0), col(0)],
        out_shape=[jax.ShapeDtypeStruct((S, AH * AHD), F32), jax.ShapeDtypeStruct((S, AH * AHD), BF16),
                   jax.ShapeDtypeStruct((S, AH * AHD), F32)],
        scratch_shapes=[pltpu.VMEM((S, AHD), BF16) for _ in range(5)]
        + [pltpu.VMEM((S, AHD), F32) for _ in range(8)],
        compiler_params=_cp(("parallel",)),
    )(_attn_consts(), proj, proj, proj)


def _attn_bwd(proj, dmixed, o, lse):
    scale = 1.0 / math.sqrt(AHD)

    def body(c_ref, q_ref, k_ref, v_ref, do_ref, o_ref, lse_ref, dproj_hbm,
             qd, kd, vd, dod, kps, vps, lsd, dld, dqd, dkd, dvd, delta, aq, ak, av, sq, sk, sv, sems):
        h = pl.program_id(0)

        def out_copies(head):
            return [pltpu.make_async_copy(
                st, dproj_hbm.at[:, pl.ds(pl.multiple_of((k * AH + head) * AHD, AHD), AHD)], sems.at[k])
                for k, st in enumerate((sq, sk, sv))]

        slope = c_ref[0:1, :]
        mask_c, mask_p, dist_c, dist_p = _attn_masks()
        delta[...] = jnp.broadcast_to(jnp.sum(do_ref[...] * o_ref[...], axis=-1, keepdims=True), (S, AHD))
        for pi, (d, nb) in enumerate(PATTERNS):
            _permute_in(qd, q_ref, d, BF16)
            _permute_in(kd, k_ref, d, BF16)
            _permute_in(vd, v_ref, d, BF16)
            _permute_in(dod, do_ref, d, BF16)
            _permute_in(lsd, lse_ref, d)
            _permute_in(dld, delta, d)
            if nb > 1:
                _shift_block(kps, kd)
                _shift_block(vps, vd)
            bias_c = -(slope * float(d)) * dist_c
            bias_p = -(slope * float(d)) * dist_p
            for g in range(NB // GB):
                q3, k3, v3, do3 = _blocks(qd, g), _blocks(kd, g), _blocks(vd, g), _blocks(dod, g)
                ls, dl = _blocks(lsd, g), _blocks(dld, g)
                lo, hi = g * GB * CH, (g + 1) * GB * CH
                p_c = jnp.exp(jnp.where(mask_c, _bdot_nt(q3, k3) * scale + bias_c, NEG) - ls)
                ds_c = ((p_c * (_bdot_nt(do3, v3) - dl)) * scale).astype(BF16)
                dq3 = _bdot(ds_c, k3)
                dkd[lo:hi, :] = _bdot_tn(ds_c, q3).reshape(GB * CH, AHD)
                dvd[lo:hi, :] = _bdot_tn(p_c.astype(BF16), do3).reshape(GB * CH, AHD)
                if nb > 1:
                    kp3, vp3 = _blocks(kps, g), _blocks(vps, g)
                    p_p = jnp.exp(jnp.where(jnp.logical_and(mask_p, _has_prev(g, nb)),
                                            _bdot_nt(q3, kp3) * scale + bias_p, NEG) - ls)
                    ds_p = ((p_p * (_bdot_nt(do3, vp3) - dl)) * scale).astype(BF16)
                    dq3 = dq3 + _bdot(ds_p, kp3)
                    dkp = _bdot_tn(ds_p, q3).reshape(GB * CH, AHD)
                    dvp = _bdot_tn(p_p.astype(BF16), do3).reshape(GB * CH, AHD)
                    if g == 0:
                        dkd[0:hi - CH, :] += dkp[CH:, :]
                        dvd[0:hi - CH, :] += dvp[CH:, :]
                    else:
                        dkd[lo - CH:hi - CH, :] += dkp
                        dvd[lo - CH:hi - CH, :] += dvp
                dqd[lo:hi, :] = dq3.reshape(GB * CH, AHD)
            ln = S // d
            for acc, src in ((aq, dqd), (ak, dkd), (av, dvd)):
                if pi == 0:
                    acc[...] = src[...]
                else:
                    acc[...] += _natural_order(src[...], d)

        @pl.when(h > 0)
        def _():
            for cp in out_copies(h - 1):
                cp.wait()

        sq[...] = aq[...].astype(BF16)
        sk[...] = ak[...].astype(BF16)
        sv[...] = av[...].astype(BF16)
        for cp in out_copies(h):
            cp.start()

        @pl.when(h == AH - 1)
        def _():
            for cp in out_copies(h):
                cp.wait()

    def col(off):
        return pl.BlockSpec((S, AHD), lambda h: (0, off + h))

    return pl.pallas_call(
        body, name="attn_bwd", grid=(AH,),
        in_specs=[pl.BlockSpec((None, 8, AHD), lambda h: (h, 0, 0)), col(0), col(AH), col(2 * AH),
                  col(0), col(0), col(0)],
        out_specs=pl.BlockSpec(memory_space=pl.ANY),
        out_shape=jax.ShapeDtypeStruct((S, NDEV * N_IN), BF16),
        scratch_shapes=[pltpu.VMEM((S, AHD), BF16) for _ in range(6)]
        + [pltpu.VMEM((S, AHD), F32) for _ in range(9)]
        + [pltpu.VMEM((S, AHD), BF16) for _ in range(3)] + [pltpu.SemaphoreType.DMA((3,))],
        compiler_params=_cp(("arbitrary",)),
    )(_attn_consts(), proj, proj, proj, dmixed, o, lse)


def _ret_consts():
    c = np.zeros((RH, 8, RHD), np.float32)
    for h in range(RH):
        c[h, :, :] = np.log(np.float32(1.0) - np.float32(2.0 ** (-5.0 - h)))
    return jnp.asarray(c)


def _ret_factors(lg):
    i = lax.broadcasted_iota(jnp.int32, (CH, CH), 0)
    j = lax.broadcasted_iota(jnp.int32, (CH, CH), 1)
    dif = (i - j).astype(F32)
    decay = jnp.where(dif >= 0, jnp.exp(lg[:, 0:CH] * jnp.maximum(dif, 0.0)), 0.0)
    row = lax.broadcasted_iota(jnp.int32, (CH, RHD), 0).astype(F32)
    zeta = jnp.exp(lg * (CH - 1.0 - row))
    xi = jnp.exp(lg * (row + 1.0))
    return decay, zeta, xi, jnp.exp(lg * float(CH))


CBK = 8
RSTEPS = NB // CBK


def _ret_specs(rev):
    off = 3 * AH * AHD // RHD
    rows = CBK * CH

    def ch(n):
        return (RSTEPS - 1 - n) if rev else n

    def col(k):
        return pl.BlockSpec((rows, RHD), lambda h, n: (ch(n), off + k * RH + h))

    own = pl.BlockSpec((rows, RHD), lambda h, n: (ch(n), h))
    state = pl.BlockSpec((None, CBK, RHD, RHD), lambda h, n: (h, ch(n), 0, 0))
    const = pl.BlockSpec((None, 8, RHD), lambda h, n: (h, 0, 0))
    dm = pl.BlockSpec((rows, RHD), lambda h, n: (ch(n), AH * AHD // RHD + h))
    return col, own, state, const, dm


def _chunks(x):
    return x.reshape(CBK, CH, RHD)


def _ret_fwd(proj):
    def body(c_ref, q_ref, k_ref, v_ref, g_ref, ret_ref, mr_ref, st_ref, r_acc):
        n = pl.program_id(1)

        @pl.when(n == 0)
        def _():
            r_acc[...] = jnp.zeros_like(r_acc)

        decay, zeta, xi, gch = _ret_factors(c_ref[0:1, :])
        q3 = _chunks(q_ref[...].astype(BF16))
        kc = _chunks(k_ref[...] * (1.0 / math.sqrt(RHD)))
        k3 = kc.astype(BF16)
        v3 = _chunks(v_ref[...].astype(BF16))
        kv3 = _bdot_tn((kc * zeta[None]).astype(BF16), v3)
        r = r_acc[...]
        for i in range(CBK):
            st_ref[i] = r.astype(BF16)
            r = r * gch + kv3[i]
        r_acc[...] = r
        scores = _bdot_nt(q3, k3) * decay[None]
        ret = (_bdot(scores.astype(BF16), v3) + _bdot(q3, st_ref[...]) * xi[None]).reshape(CBK * CH, RHD)
        ret_ref[...] = ret
        rr = lax.rsqrt(jnp.mean(ret * ret, axis=-1, keepdims=True) + EPS)
        gv = g_ref[...]
        mr_ref[...] = ((gv * _sigmoid(gv)) * (ret * rr)).astype(BF16)

    col, own, state, const, _ = _ret_specs(False)
    return pl.pallas_call(
        body, name="ret_fwd", grid=(RH, RSTEPS),
        in_specs=[const, col(0), col(1), col(2), col(3)],
        out_specs=[own, own, state],
        out_shape=[jax.ShapeDtypeStruct((S, RH * RHD), F32), jax.ShapeDtypeStruct((S, RH * RHD), BF16),
                   jax.ShapeDtypeStruct((RH, NB, RHD, RHD), BF16)],
        scratch_shapes=[pltpu.VMEM((RHD, RHD), F32)],
        compiler_params=_cp(("parallel", "arbitrary")),
    )(_ret_consts(), proj, proj, proj, proj)


def _ret_bwd(proj, ret, states, dmixed, dproj):
    rows = CBK * CH
    col0 = 3 * AH * AHD

    def body(c_ref, q_ref, k_ref, v_ref, g_ref, ret_ref, st_ref, dm_ref, dproj_in, dproj_hbm, g_acc, gs,
             sq, sk, sv, sg, sems):
        del dproj_in
        h, n = pl.program_id(0), pl.program_id(1)
        step = h * RSTEPS + n

        def out_copies(t):
            hh, nn = t // RSTEPS, t % RSTEPS
            r0 = pl.multiple_of((RSTEPS - 1 - nn) * rows, rows)
            return [pltpu.make_async_copy(
                st, dproj_hbm.at[pl.ds(r0, rows), pl.ds(pl.multiple_of(col0 + (k * RH + hh) * RHD, RHD), RHD)],
                sems.at[k]) for k, st in enumerate((sq, sk, sv, sg))]

        @pl.when(n == 0)
        def _():
            g_acc[...] = jnp.zeros_like(g_acc)

        decay, zeta, xi, gch = _ret_factors(c_ref[0:1, :])
        ret_v = ret_ref[...]
        rr = lax.rsqrt(jnp.mean(ret_v * ret_v, axis=-1, keepdims=True) + EPS)
        gv = g_ref[...]
        sgm = _sigmoid(gv)
        dmix = dm_ref[...]
        dgate = ((dmix * (ret_v * rr)) * (sgm * (1.0 + gv * (1.0 - sgm)))).astype(BF16)
        dretn = dmix * (gv * sgm)
        dret = _chunks(rr * dretn - ret_v * ((rr * rr * rr) * jnp.mean(dretn * ret_v, axis=-1, keepdims=True)))

        q3 = _chunks(q_ref[...].astype(BF16))
        kc = _chunks(k_ref[...] * (1.0 / math.sqrt(RHD)))
        k3 = kc.astype(BF16)
        v3 = _chunks(v_ref[...].astype(BF16))
        d3 = dret.astype(BF16)
        dxi = (dret * xi[None]).astype(BF16)
        kz = (kc * zeta[None]).astype(BF16)
        dr3 = _bdot_tn(q3, dxi)
        acc = g_acc[...]
        for i in reversed(range(CBK)):
            gs[i] = acc.astype(BF16)
            acc = dr3[i] + gch * acc
        g_acc[...] = acc
        g3 = gs[...]
        sc = (_bdot_nt(q3, k3) * decay[None]).astype(BF16)
        da = (_bdot_nt(d3, v3) * decay[None]).astype(BF16)
        dq = _bdot(da, k3) + _bdot_nt(dxi, st_ref[...])
        dkc = _bdot_tn(da, q3) + _bdot_nt(v3, g3) * zeta[None]
        dv = _bdot_tn(sc, d3) + _bdot(kz, g3)

        @pl.when(step > 0)
        def _():
            for cp in out_copies(step - 1):
                cp.wait()

        sq[...] = dq.reshape(rows, RHD).astype(BF16)
        sk[...] = (dkc * (1.0 / math.sqrt(RHD))).reshape(rows, RHD).astype(BF16)
        sv[...] = dv.reshape(rows, RHD).astype(BF16)
        sg[...] = dgate
        for cp in out_copies(step):
            cp.start()

        @pl.when(step == RH * RSTEPS - 1)
        def _():
            for cp in out_copies(step):
                cp.wait()

    col, own, state, const, dm = _ret_specs(True)
    hbm = pl.BlockSpec(memory_space=pl.ANY)
    return pl.pallas_call(
        body, name="ret_bwd", grid=(RH, RSTEPS),
        in_specs=[const, col(0), col(1), col(2), col(3), own, state, dm, hbm],
        out_specs=hbm,
        out_shape=jax.ShapeDtypeStruct(dproj.shape, dproj.dtype),
        input_output_aliases={8: 0},
        scratch_shapes=[pltpu.VMEM((RHD, RHD), F32), pltpu.VMEM((CBK, RHD, RHD), BF16)]
        + [pltpu.VMEM((rows, RHD), BF16) for _ in range(4)] + [pltpu.SemaphoreType.DMA((4,))],
        compiler_params=_cp(("arbitrary", "arbitrary")),
    )(_ret_consts(), proj, proj, proj, proj, ret, states, dmixed, dproj)


class _NoReduction:
    def start(self, group, grads):
        pass

    def local(self, name, first=()):
        return []

    def landed(self, name):
        return []

    def update(self, name):
        return []


def _local_step(x, tgt, nw1, nw2, nw3, win, wout, wg, wu, wd, red=None):
    red = red or _NoReduction()

    def after(values, first):
        return lax.optimization_barrier((tuple(values), tuple(first)))[0]

    wg, wu, wd = (w.reshape(NFG, N_FG, D) for w in (wg, wu, wd))
    h1, r1 = _rms_fwd(x, nw1)
    proj = _proj(h1, win)
    o, ma, lse = _attn_fwd(proj)
    ret, mr, states = _ret_fwd(proj)
    x2, h2, r2 = _out_proj_rms(x, ma, mr, wout, nw2)
    a, dadg, dadu = _ffn_up(h2, wg, wu)
    dx3, dx3b, st3 = _ffn_down_loss(x2, a, wd, nw3, tgt)

    dwd = _wgrad_rows(a, dx3b, "wgrad_down")
    red.start(["w_down"], [dwd])
    (dx3b,) = after([dx3b], [dwd])
    dg, du = _ffn_down_bwd(dx3b, wd, dadg, dadu)
    dg, du = after([dg, du], red.local("w_down", first=[dg]))
    dwg = _wgrad_rows(dg, h2, "wgrad_gate")
    red.start(["w_gate"], [dwg])
    (du,) = after([du], [dwg])
    dwu = _wgrad_rows(du, h2, "wgrad_up")
    red.start(["w_up"], [dwu])
    dg, du = after([dg, du], [dwu] + red.local("w_gate"))
    dx2, dx2b, st2 = _ffn_up_bwd(dg, du, wg, wu, dx3, x2, r2, nw2)
    (dx2b,) = after([dx2b], red.local("w_up", first=[dx2b] + red.landed("w_down")))
    dwo = _wgrad_out(ma, mr, dx2b)
    red.start(["w_out"], [dwo])
    (dx2b,) = after([dx2b], [dwo])
    dmixed = _out_proj_bwd(dx2b, wout)
    dproj = _attn_bwd(proj, dmixed, o, lse)
    (dmixed,) = after([dmixed], red.local("w_out", first=[dproj] + red.landed("w_gate")))
    dproj = _ret_bwd(proj, ret, states, dmixed, dproj)
    (dwi0,) = after([_wgrad_in(h1, dproj, 0)], red.landed("w_up"))
    red.start(["w_in_0"], [dwi0])
    (dproj,) = after([dproj], [dwi0])
    dwi1 = _wgrad_in(h1, dproj, 1)
    red.start(["w_in_1"], [dwi1])
    sums = red.local("w_in_0", first=[dwi1] + red.landed("w_out"))
    sums = red.local("w_in_1", first=sums + red.update("w_down"))
    (dproj,) = after([dproj], sums)
    gx, st1 = _in_proj_bwd(dproj, win, dx2, x, r1, nw1)
    dwi = jnp.concatenate([dwi0, dwi1], axis=1)
    stats = jnp.concatenate([st1[0:1], st2[0:1], st3[0:2], jnp.zeros((4, D), F32)], axis=0)
    return stats, gx, dwi, dwo, dwg, dwu, dwd


def _place():
    x, y, c = lax.axis_index("x"), lax.axis_index("y"), lax.axis_index("c")
    return x, y, c, [(1 - x, y), (x, 1 - y), (1 - x, 1 - y)]


def _handshake(peers):
    barrier = pltpu.get_barrier_semaphore()
    for peer in peers:
        pl.semaphore_signal(barrier, inc=1, device_id=peer, device_id_type=MESH)
    pl.semaphore_wait(barrier, len(peers))


def _all_gather(shards, name, collective_id):
    na = len(shards)
    SIB, XN0, XN1, YN1, YN0, VIA_X, VIA_Y = 0, 1, 2, 3, 4, 5, 6
    D2D = {XN0: 7, XN1: 8, YN1: 9, YN0: 10, VIA_X: 11, VIA_Y: 12}

    def body(*refs):
        ins, outs = refs[:na], refs[na:2 * na]
        send_sems, recv_sems, local_sems = refs[2 * na:]
        x, y, c, _ = _place()
        me, sib = (x, y, c), (x, y, 1 - c)
        xn, yn, dg = (1 - x, y, c), (x, 1 - y, c), (1 - x, 1 - y, c)
        _handshake([sib, xn, yn])

        def part(ref, h):
            rows = ref.shape[0] // 2
            return ref if h is None else ref.at[pl.ds(h * rows, rows)]

        def block(a, owner, h):
            return part(outs[a].at[4 * owner[0] + 2 * owner[1] + owner[2]], h)

        def copy(a, k, owner, h, to, own_src=False):
            return pltpu.make_async_remote_copy(
                src_ref=part(ins[a], h) if own_src else block(a, owner, h), dst_ref=block(a, owner, h),
                send_sem=send_sems.at[a, k], recv_sem=recv_sems.at[a, k], device_id=to, device_id_type=MESH)

        def other(p):
            return (p[0], p[1], 1 - c)

        mine = [pltpu.make_async_copy(ins[a], block(a, me, None), local_sems.at[a]) for a in range(na)]
        for cp in mine:
            cp.start()
        sent = []
        for a in range(na):
            sent += [copy(a, XN0, me, 0, xn, True), copy(a, YN1, me, 1, yn, True),
                     copy(a, XN1, me, 1, xn, True), copy(a, YN0, me, 0, yn, True)]
        sent += [copy(a, SIB, me, None, sib, True) for a in range(na)]
        for cp in sent:
            cp.start()

        def landed(a, k, owner, h, then):
            copy(a, k, owner, h, me).wait_recv()
            for k2, to in then + [(D2D[k], sib)]:
                cp = copy(a, k2, owner, h, to)
                cp.start()
                sent.append(cp)

        for a in range(na):
            landed(a, XN0, xn, 0, [(VIA_Y, yn)])
            landed(a, YN1, yn, 1, [(VIA_X, xn)])
            landed(a, XN1, xn, 1, [])
            landed(a, YN0, yn, 0, [])
        for a in range(na):
            landed(a, VIA_Y, dg, 0, [])
            landed(a, VIA_X, dg, 1, [])
        for a in range(na):
            copy(a, SIB, sib, None, me).wait_recv()
            for k, owner, h in ((XN0, xn, 0), (XN1, xn, 1), (YN1, yn, 1), (YN0, yn, 0), (VIA_Y, dg, 0), (VIA_X, dg, 1)):
                copy(a, D2D[k], other(owner), h, me).wait_recv()
        for cp in sent:
            cp.wait_send()
        for cp in mine:
            cp.wait()

    return _sequencer_call(
        body, name, collective_id,
        [jax.ShapeDtypeStruct((NDEV,) + s.shape, s.dtype) for s in shards],
        [pltpu.SemaphoreType.DMA((na, 13)), pltpu.SemaphoreType.DMA((na, 13)), pltpu.SemaphoreType.DMA((na,))])(*shards)


def _sequencer_call(body, name, collective_id, out_type, scratch_types):
    return pl.kernel(
        body, name=name, out_type=out_type,
        mesh=plsc.ScalarSubcoreMesh(axis_name="sequencer", num_cores=1),
        scratch_types=scratch_types,
        compiler_params=pltpu.CompilerParams(collective_id=collective_id))


def _exchange_sibling(grads, name, collective_id):
    na = len(grads)

    def body(*refs):
        ins, outs = refs[:na], refs[na:2 * na]
        send_sems, recv_sems = refs[2 * na:]
        x, y, c, _ = _place()
        _handshake([(x, y, 1 - c)])
        cps = []
        for a in range(na):
            for k in range(4):
                cps.append(pltpu.make_async_remote_copy(
                    src_ref=ins[a].at[2 * k + (1 - c)], dst_ref=outs[a].at[k],
                    send_sem=send_sems.at[a, k], recv_sem=recv_sems.at[a, k],
                    device_id=(x, y, 1 - c), device_id_type=MESH))
        for cp in cps:
            cp.start()
        for cp in cps:
            cp.wait()

    return _sequencer_call(
        body, name, collective_id,
        [jax.ShapeDtypeStruct((4,) + g.shape[1:], g.dtype) for g in grads],
        [pltpu.SemaphoreType.DMA((na, 4)), pltpu.SemaphoreType.DMA((na, 4))])(*grads)


def _row_tile(rows, cols):
    for t in (512, 256, 176, 128, 64, 32, 16):
        if rows % t == 0 and t * cols * 4 <= (2 << 20):
            return t
    raise ValueError((rows, cols))


def _chip_sum(place, g, got, name):
    _, r, c = g.shape
    tm = r

    def body(pos_ref, g_ref, got_ref, o_ref):
        o_ref[...] = (g_ref[...].astype(F32) + got_ref[...].astype(F32)).astype(BF16)

    def chip(j, pos):
        return 2 * (pos[0] ^ jnp.where(j == 1, 0, 1)) + (pos[1] ^ jnp.where(j == 0, 0, 1))

    return pl.pallas_call(
        body, name=name,
        grid_spec=pltpu.PrefetchScalarGridSpec(
            num_scalar_prefetch=1, grid=(3, r // tm),
            in_specs=[pl.BlockSpec((None, tm, c), lambda j, i, pos: (2 * chip(j, pos) + pos[2], i, 0)),
                      pl.BlockSpec((None, tm, c), lambda j, i, pos: (chip(j, pos), i, 0))],
            out_specs=pl.BlockSpec((None, tm, c), lambda j, i, pos: (j, i, 0))),
        out_shape=jax.ShapeDtypeStruct((3, r, c), BF16),
        compiler_params=_cp(("parallel", "parallel")),
    )(place, g, got)


def _exchange_chips(sums, name, collective_id):
    na = len(sums)

    def body(*refs):
        ins, outs = refs[:na], refs[na:2 * na]
        send_sems, recv_sems = refs[2 * na:]
        x, y, c, chips = _place()
        _handshake([(*chip, c) for chip in chips])
        cps = []
        for a in range(na):
            for j, chip in enumerate(chips):
                cps.append(pltpu.make_async_remote_copy(
                    src_ref=ins[a].at[j], dst_ref=outs[a].at[j],
                    send_sem=send_sems.at[a, j], recv_sem=recv_sems.at[a, j],
                    device_id=(*chip, c), device_id_type=MESH))
        for cp in cps:
            cp.start()
        for cp in cps:
            cp.wait()

    return _sequencer_call(
        body, name, collective_id,
        [jax.ShapeDtypeStruct((3,) + s.shape[1:], s.dtype) for s in sums],
        [pltpu.SemaphoreType.DMA((na, 3)), pltpu.SemaphoreType.DMA((na, 3))])(*sums)


def _exchange_stats(stats, collective_id):
    def body(st_in, st_out, st_send, st_recv, local_sem):
        x, y, c, _ = _place()
        me_idx = 4 * x + 2 * y + c
        peers = [(x ^ ((k >> 2) & 1), y ^ ((k >> 1) & 1), c ^ (k & 1)) for k in range(1, 8)]
        _handshake(peers)
        mine = pltpu.make_async_copy(st_in, st_out.at[me_idx], local_sem)
        mine.start()
        cps = [pltpu.make_async_remote_copy(
            src_ref=st_in, dst_ref=st_out.at[me_idx], send_sem=st_send.at[k], recv_sem=st_recv.at[k],
            device_id=peer, device_id_type=MESH) for k, peer in enumerate(peers)]
        for cp in cps:
            cp.start()
        for cp in cps:
            cp.wait()
        mine.wait()

    return _sequencer_call(
        body, "exchange_stats", collective_id,
        jax.ShapeDtypeStruct((NDEV,) + stats.shape, stats.dtype),
        [pltpu.SemaphoreType.DMA((7,)), pltpu.SemaphoreType.DMA((7,)), pltpu.SemaphoreType.DMA])(stats)


class _Reduction:
    def __init__(self, place, first_collective_id, state):
        self.place = place
        self.ids = iter(range(first_collective_id, 32))
        self.state = state
        self.groups = {}
        self.updates = {}

    def next_id(self):
        return next(self.ids)

    def start(self, group, grads):
        got = _exchange_sibling(grads, "sibling_exchange_" + group[0], self.next_id())
        self.groups[group[0]] = dict(names=group, grads=grads, got=got)

    def local(self, name, first=()):
        grp = self.groups[name]
        grads = lax.optimization_barrier((tuple(grp["grads"]), tuple(first)))[0]
        grp["sums"] = [_chip_sum(self.place, g, s, "chip_sum_" + n)
                       for g, s, n in zip(grads, grp["got"], grp["names"])]
        grp["chips"] = _exchange_chips(grp["sums"], "chip_exchange_" + name, self.next_id())
        return grp["sums"]

    def landed(self, name):
        return list(self.groups[name]["chips"])

    def update(self, name):
        if name not in self.updates:
            grp = next(g for g in self.groups.values() if name in g["names"])
            k = grp["names"].index(name)
            w, m, v, part, parts = self.state[name]
            before = self.update(f"{name[:-1]}{part - 1}") if part else None
            self.updates[name] = _shard_update(self.place, w, m, v, grp["grads"][k], grp["got"][k],
                                               grp["chips"][k], "update_" + name, part, parts, before)
        return list(self.updates[name])


def _adamw(w, g, m, v):
    m = ADAM_B1 * m + (1.0 - ADAM_B1) * g
    v = ADAM_B2 * v + (1.0 - ADAM_B2) * (g * g)
    m_hat = m / (1.0 - ADAM_B1 ** ADAM_STEP)
    v_hat = v / (1.0 - ADAM_B2 ** ADAM_STEP)
    delta = -ADAM_LR * (m_hat / (jnp.sqrt(v_hat) + ADAM_EPS) + ADAM_WD * w)
    return delta, m, v


def _shard_update(place, w, m, v, g, got_sib, got_chips, name, part=0, parts=1, before=None):
    r, c = w.shape
    rp = r // parts
    tm = _row_tile(rp, c)
    off = part * (rp // tm)

    def body(pos_ref, w_ref, m_ref, v_ref, g_ref, s_ref, c_ref, *rest):
        go_ref, d_ref, mo_ref, vo_ref = rest[-4:]
        grad = g_ref[...].astype(F32) + s_ref[...].astype(F32)
        for j in range(3):
            grad = grad + c_ref[j].astype(F32)
        delta, mn, vn = _adamw(w_ref[...], grad, m_ref[...], v_ref[...])
        go_ref[...] = grad
        d_ref[...] = delta
        mo_ref[...] = mn
        vo_ref[...] = vn

    row = pl.BlockSpec((tm, c), lambda i, pos: (i + off, 0))
    before = list(before or [])
    return pl.pallas_call(
        body, name=name,
        grid_spec=pltpu.PrefetchScalarGridSpec(
            num_scalar_prefetch=1, grid=(rp // tm,),
            in_specs=[row, row, row,
                      pl.BlockSpec((None, tm, c), lambda i, pos: (4 * pos[0] + 2 * pos[1] + pos[2], i, 0)),
                      pl.BlockSpec((None, tm, c), lambda i, pos: (2 * pos[0] + pos[1], i, 0)),
                      pl.BlockSpec((3, tm, c), lambda i, pos: (0, i, 0))]
            + [pl.BlockSpec(memory_space=pl.ANY)] * len(before),
            out_specs=[row, row, row, row]),
        out_shape=[jax.ShapeDtypeStruct((r, c), F32)] * 4,
        input_output_aliases={7 + k: k for k in range(len(before))},
        compiler_params=_cp(("parallel",)),
    )(place, w, m, v, g, got_sib, got_chips, *before)


def _small_update(stats_all, ws, ms, vs):
    def body(st_ref, w_ref, m_ref, v_ref, go_ref, d_ref, mo_ref, vo_ref):
        grad = st_ref[0]
        for k in range(1, NDEV):
            grad = grad + st_ref[k]
        delta, mn, vn = _adamw(w_ref[...], grad, m_ref[...], v_ref[...])
        go_ref[...] = grad
        d_ref[...] = delta
        mo_ref[...] = mn
        vo_ref[...] = vn

    return pl.pallas_call(
        body, name="small_update",
        out_shape=[jax.ShapeDtypeStruct((8, D), F32)] * 4,
        compiler_params=_cp(),
    )(stats_all, ws, ms, vs)


def kernel(x, norm_mix_w, w_in, w_out, norm_ffn_w, w_gate, w_up, w_down, norm_final_w, loss_target, m_norm_mix_w, m_w_in, m_w_out, m_norm_ffn_w, m_w_gate, m_w_up, m_w_down, m_norm_final_w, v_norm_mix_w, v_w_in, v_w_out, v_norm_ffn_w, v_w_gate, v_w_up, v_w_down, v_norm_final_w):
    tr = {"w_gate", "w_up"}
    names = ["w_in", "w_out", "w_gate", "w_up", "w_down"]

    def view(a, n):
        return a[0].T if n in tr else a[0]

    big_w = [view(a, n) for a, n in zip([w_in, w_out, w_gate, w_up, w_down], names)]
    big_m = [view(a, n) for a, n in zip([m_w_in, m_w_out, m_w_gate, m_w_up, m_w_down], names)]
    big_v = [view(a, n) for a, n in zip([v_w_in, v_w_out, v_w_gate, v_w_up, v_w_down], names)]

    shards = [_cast_bf16(w, "cast_" + n) for w, n in zip(big_w, names)]
    (win,) = _all_gather(shards[0:1], "all_gather_w_in", 1)
    (wout,) = _all_gather(shards[1:2], "all_gather_w_out", 2)
    wg, wu = _all_gather(shards[2:4], "all_gather_gate_up", 3)
    (wd,) = _all_gather(shards[4:5], "all_gather_w_down", 4)
    nw3 = norm_final_w.reshape(1, D)
    place = jnp.stack([lax.axis_index("x"), lax.axis_index("y"), lax.axis_index("c")]).astype(jnp.int32)
    state = {n: (w, m, v, 0, 1) for n, w, m, v in zip(names, big_w, big_m, big_v)}
    for part in range(W_IN_PARTS):
        state[f"w_in_{part}"] = state["w_in"][:3] + (part, W_IN_PARTS)
    red = _Reduction(place, 5, state)
    stats, gx, *_ = _local_step(
        x[0], loss_target[0], norm_mix_w, norm_ffn_w, nw3, win, wout.reshape(D, D), wg, wu, wd, red)
    stats_all = _exchange_stats(stats, red.next_id())
    upd = [red.update(f"w_in_{W_IN_PARTS - 1}" if n == "w_in" else n) for n in names]
    stats_all = lax.optimization_barrier((stats_all, tuple(upd[0])))[0]

    def rows(a, b, c):
        return jnp.concatenate([a.reshape(1, D), b.reshape(1, D), c.reshape(1, D), jnp.zeros((5, D), F32)], axis=0)

    sg, sd, sm, sv = _small_update(stats_all, rows(norm_mix_w, norm_ffn_w, norm_final_w),
                                   rows(m_norm_mix_w, m_norm_ffn_w, m_norm_final_w),
                                   rows(v_norm_mix_w, v_norm_ffn_w, v_norm_final_w))
    loss = sg[3, 0]

    def outs(k, small):
        big = [(u[k].T if n in tr else u[k])[None] for u, n in zip(upd, names)]
        return [small[0:1], big[0], big[1], small[1:2], big[2], big[3], big[4], small[2]]

    return (loss, gx[None], *outs(0, sg), *outs(1, sd), *outs(2, sm), *outs(3, sv))
```

```python
import functools
import math

import numpy as np
import jax
import jax.numpy as jnp
from jax import lax
from jax.experimental import pallas as pl
from jax.experimental.pallas import tpu as pltpu
from jax.experimental.pallas import tpu_sc as plsc

F32 = jnp.float32
BF16 = jnp.bfloat16

S = 2048
D = 2048
NDEV = 8
N_IN = 7168 // NDEV
N_FF = 5632 // NDEV
NFG, N_FG = NDEV // 2, 2 * N_FF
N_OUT = 2048 // NDEV
AH, AHD = 8, 128
RH, RHD = 4, 256
CH = 128
NB = S // CH
EPS = 1e-6
PATTERNS = ((1, 16), (4, 4), (16, 1))
NEG = -1e30
VMEM_LIMIT = 56 * 1024 * 1024

ADAM_LR, ADAM_B1, ADAM_B2, ADAM_EPS, ADAM_WD, ADAM_STEP = 0.001, 0.9, 0.999, 1e-08, 0.01, 10
MESH = pl.DeviceIdType.MESH


def _cp(sem=None):
    return pltpu.CompilerParams(dimension_semantics=sem, vmem_limit_bytes=VMEM_LIMIT)


def _dot(a, b):
    return jnp.dot(a, b, preferred_element_type=F32)


def _dot_nt(a, b):
    return lax.dot_general(a, b, (((1,), (1,)), ((), ())), preferred_element_type=F32)


def _dot_tn(a, b):
    return lax.dot_general(a, b, (((0,), (0,)), ((), ())), preferred_element_type=F32)


def _sigmoid(x):
    return 0.5 * jnp.tanh(0.5 * x) + 0.5


def _cast_bf16(w, name):
    r, c = w.shape
    tm = r if r <= 1024 else 512

    def body(w_ref, o_ref):
        o_ref[...] = w_ref[...].astype(BF16)

    return pl.pallas_call(
        body, name=name, grid=(r // tm,),
        in_specs=[pl.BlockSpec((tm, c), lambda i: (i, 0))],
        out_specs=pl.BlockSpec((tm, c), lambda i: (i, 0)),
        out_shape=jax.ShapeDtypeStruct((r, c), BF16),
        compiler_params=_cp(("parallel",)),
    )(w)


def _rms_fwd(x, nw):
    tm = 256

    def body(x_ref, w_ref, h_ref, r_ref):
        xs = x_ref[...]
        r = lax.rsqrt(jnp.mean(xs * xs, axis=-1, keepdims=True) + EPS)
        h_ref[...] = ((xs * r) * w_ref[...]).astype(BF16)
        r_ref[...] = r

    return pl.pallas_call(
        body, name="rms_fwd", grid=(S // tm,),
        in_specs=[pl.BlockSpec((tm, D), lambda i: (i, 0)), pl.BlockSpec((1, D), lambda i: (0, 0))],
        out_specs=[pl.BlockSpec((tm, D), lambda i: (i, 0)), pl.BlockSpec((tm, 1), lambda i: (i, 0))],
        out_shape=[jax.ShapeDtypeStruct((S, D), BF16), jax.ShapeDtypeStruct((S, 1), F32)],
        compiler_params=_cp(("parallel",)),
    )(x, nw)


def _row_copies(hbm_refs, bufs, sems, m, tm):
    rows = pl.ds(pl.multiple_of(m * tm, tm), tm)
    return [pltpu.make_async_copy(h.at[rows], b, sems.at[i]) for i, (h, b) in enumerate(zip(hbm_refs, bufs))]


def _rms_bwd_tile(dh, xs, r, nw):
    dnw = jnp.sum(dh * (xs * r), axis=0, keepdims=True)
    gy = dh * nw
    dx = r * gy - xs * ((r * r * r) * jnp.mean(gy * xs, axis=-1, keepdims=True))
    return dx, dnw


def _proj(h1, win):
    tm = 1024

    def body(a_ref, w_ref, o_ref):
        o_ref[...] = _dot(a_ref[...], w_ref[...])

    return pl.pallas_call(
        body, name="proj", grid=(NDEV, S // tm),
        in_specs=[pl.BlockSpec((tm, D), lambda p, m: (m, 0)),
                  pl.BlockSpec((None, D, N_IN), lambda p, m: (p, 0, 0))],
        out_specs=pl.BlockSpec((tm, N_IN), lambda p, m: (m, p)),
        out_shape=jax.ShapeDtypeStruct((S, NDEV * N_IN), F32),
        compiler_params=_cp(("parallel", "parallel")),
    )(h1, win)


def _out_proj_rms(x, ma, mr, wout, nw):
    tm = 256
    half = D // 2

    def body(x_ref, ma_ref, mr_ref, w_ref, nw_ref, x2_ref, h_ref, r_ref):
        acc = _dot(ma_ref[...], w_ref[0:half, :]) + _dot(mr_ref[...], w_ref[half:D, :])
        x2 = x_ref[...] + acc
        r = lax.rsqrt(jnp.mean(x2 * x2, axis=-1, keepdims=True) + EPS)
        x2_ref[...] = x2
        h_ref[...] = ((x2 * r) * nw_ref[...]).astype(BF16)
        r_ref[...] = r

    return pl.pallas_call(
        body, name="out_proj_rms", grid=(S // tm,),
        in_specs=[pl.BlockSpec((tm, D), lambda i: (i, 0)),
                  pl.BlockSpec((tm, half), lambda i: (i, 0)),
                  pl.BlockSpec((tm, half), lambda i: (i, 0)),
                  pl.BlockSpec((D, D), lambda i: (0, 0)),
                  pl.BlockSpec((1, D), lambda i: (0, 0))],
        out_specs=[pl.BlockSpec((tm, D), lambda i: (i, 0)), pl.BlockSpec((tm, D), lambda i: (i, 0)),
                   pl.BlockSpec((tm, 1), lambda i: (i, 0))],
        out_shape=[jax.ShapeDtypeStruct((S, D), F32), jax.ShapeDtypeStruct((S, D), BF16),
                   jax.ShapeDtypeStruct((S, 1), F32)],
        compiler_params=_cp(("parallel",)),
    )(x, ma, mr, wout, nw)


def _ffn_up(h2, wg, wu):
    tm = 512

    def body(h_ref, wg_ref, wu_ref, a_ref, dadg_ref, dadu_ref):
        h = h_ref[...]
        g = _dot_nt(h, wg_ref[...])
        u = _dot_nt(h, wu_ref[...])
        sg = _sigmoid(g)
        silu = g * sg
        a_ref[...] = (silu * u).astype(BF16)
        dadg_ref[...] = (u * (sg * (1.0 + g * (1.0 - sg)))).astype(BF16)
        dadu_ref[...] = silu.astype(BF16)

    blk = pl.BlockSpec((None, tm, N_FG), lambda p, m: (p, m, 0))
    wblk = pl.BlockSpec((None, N_FG, D), lambda p, m: (p, 0, 0))
    return pl.pallas_call(
        body, name="ffn_up", grid=(NFG, S // tm),
        in_specs=[pl.BlockSpec((tm, D), lambda p, m: (m, 0)), wblk, wblk],
        out_specs=[blk, blk, blk],
        out_shape=[jax.ShapeDtypeStruct((NFG, S, N_FG), BF16)] * 3,
        compiler_params=_cp(("parallel", "parallel")),
    )(h2, wg, wu)


def _ffn_down_loss(x2, a, wd, nw, tgt):
    tm = 512

    def body(x2_hbm, a_ref, w_ref, nw_ref, t_hbm, dx_ref, dxb_ref, st_ref, acc_ref, x2_buf, t_buf, sems):
        m, p = pl.program_id(0), pl.program_id(1)
        tail_in = _row_copies((x2_hbm, t_hbm), (x2_buf, t_buf), sems, m, tm)

        @pl.when(p == 0)
        def _():
            acc_ref[...] = jnp.zeros_like(acc_ref)
            for cp in tail_in:
                cp.start()

        @pl.when((p == 0) & (m == 0))
        def _():
            st_ref[...] = jnp.zeros_like(st_ref)

        acc_ref[...] += _dot(a_ref[...], w_ref[...])

        @pl.when(p == NFG - 1)
        def _():
            for cp in tail_in:
                cp.wait()
            x3 = x2_buf[...] + acc_ref[...]
            nwv = nw_ref[...]
            r = lax.rsqrt(jnp.mean(x3 * x3, axis=-1, keepdims=True) + EPS)
            y = (x3 * r) * nwv
            err = y - t_buf[...]
            loss = 0.5 * jnp.sum(jnp.mean(err * err, axis=-1, keepdims=True), axis=0, keepdims=True)
            dy = err * (1.0 / D)
            dx, dnw = _rms_bwd_tile(dy, x3, r, nwv)
            dx_ref[...] = dx
            dxb_ref[...] = dx.astype(BF16)
            st_ref[0:1, :] += dnw
            st_ref[1:2, :] += jnp.broadcast_to(loss, (1, D))

    return pl.pallas_call(
        body, name="ffn_down_loss", grid=(S // tm, NFG),
        in_specs=[pl.BlockSpec(memory_space=pl.ANY),
                  pl.BlockSpec((None, tm, N_FG), lambda m, p: (p, m, 0)),
                  pl.BlockSpec((None, N_FG, D), lambda m, p: (p, 0, 0)),
                  pl.BlockSpec((1, D), lambda m, p: (0, 0)),
                  pl.BlockSpec(memory_space=pl.ANY)],
        out_specs=[pl.BlockSpec((tm, D), lambda m, p: (m, 0)), pl.BlockSpec((tm, D), lambda m, p: (m, 0)),
                   pl.BlockSpec((8, D), lambda m, p: (0, 0))],
        out_shape=[jax.ShapeDtypeStruct((S, D), F32), jax.ShapeDtypeStruct((S, D), BF16),
                   jax.ShapeDtypeStruct((8, D), F32)],
        scratch_shapes=[pltpu.VMEM((tm, D), F32), pltpu.VMEM((tm, D), F32), pltpu.VMEM((tm, D), F32),
                        pltpu.SemaphoreType.DMA((2,))],
        compiler_params=_cp(("arbitrary", "arbitrary")),
    )(x2, a, wd, nw, tgt)


def _ffn_down_bwd(dx3b, wd, dadg, dadu, part, before=None):
    tm = 1024
    half = NFG // 2

    def body(dx_ref, w_ref, dadg_ref, dadu_ref, *rest):
        dg_ref, du_ref = rest[-2:]
        da = _dot_nt(dx_ref[...], w_ref[...])
        dg_ref[...] = (da * dadg_ref[...].astype(F32)).astype(BF16)
        du_ref[...] = (da * dadu_ref[...].astype(F32)).astype(BF16)

    blk = pl.BlockSpec((None, tm, N_FG), lambda p, m: (p + part * half, m, 0))
    before = list(before or [])
    return pl.pallas_call(
        body, name=f"ffn_down_bwd_{part}", grid=(half, S // tm),
        in_specs=[pl.BlockSpec((tm, D), lambda p, m: (m, 0)),
                  pl.BlockSpec((None, N_FG, D), lambda p, m: (p + part * half, 0, 0)), blk, blk]
        + [pl.BlockSpec(memory_space=pl.ANY)] * len(before),
        out_specs=[blk, blk],
        out_shape=[jax.ShapeDtypeStruct((NFG, S, N_FG), BF16)] * 2,
        input_output_aliases={4 + k: k for k in range(len(before))},
        compiler_params=_cp(("parallel", "parallel")),
    )(dx3b, wd, dadg, dadu, *before)


def _ffn_up_bwd(dg, du, wg, wu, dres, xs, r, nw):
    tm = 512

    def body(dg_ref, du_ref, wg_ref, wu_ref, dres_hbm, x_hbm, r_ref, nw_ref, dx_ref, dxb_ref, st_ref,
             dres_buf, x_buf, sems):
        m, p = pl.program_id(0), pl.program_id(1)
        tail_in = _row_copies((dres_hbm, x_hbm), (dres_buf, x_buf), sems, m, tm)

        @pl.when(p == 0)
        def _():
            dx_ref[...] = jnp.zeros_like(dx_ref)
            for cp in tail_in:
                cp.start()

        @pl.when((p == 0) & (m == 0))
        def _():
            st_ref[...] = jnp.zeros_like(st_ref)

        dx_ref[...] += _dot(dg_ref[...], wg_ref[...])
        dx_ref[...] += _dot(du_ref[...], wu_ref[...])

        @pl.when(p == NFG - 1)
        def _():
            for cp in tail_in:
                cp.wait()
            dx, dnw = _rms_bwd_tile(dx_ref[...], x_buf[...], r_ref[...], nw_ref[...])
            dx = dres_buf[...] + dx
            dx_ref[...] = dx
            dxb_ref[...] = dx.astype(BF16)
            st_ref[0:1, :] += dnw

    blk = pl.BlockSpec((None, tm, N_FG), lambda m, p: (p, m, 0))
    wblk = pl.BlockSpec((None, N_FG, D), lambda m, p: (p, 0, 0))
    row = pl.BlockSpec((tm, D), lambda m, p: (m, 0))
    hbm = pl.BlockSpec(memory_space=pl.ANY)
    return pl.pallas_call(
        body, name="ffn_up_bwd", grid=(S // tm, NFG),
        in_specs=[blk, blk, wblk, wblk, hbm, hbm, pl.BlockSpec((tm, 1), lambda m, p: (m, 0)),
                  pl.BlockSpec((1, D), lambda m, p: (0, 0))],
        out_specs=[row, row, pl.BlockSpec((8, D), lambda m, p: (0, 0))],
        out_shape=[jax.ShapeDtypeStruct((S, D), F32), jax.ShapeDtypeStruct((S, D), BF16),
                   jax.ShapeDtypeStruct((8, D), F32)],
        scratch_shapes=[pltpu.VMEM((tm, D), F32), pltpu.VMEM((tm, D), F32), pltpu.SemaphoreType.DMA((2,))],
        compiler_params=_cp(("arbitrary", "arbitrary")),
    )(dg, du, wg, wu, dres, xs, r, nw)


def _out_proj_bwd(dx2b, wout):
    tm = 256

    def body(dx_ref, w_ref, o_ref):
        o_ref[...] = _dot_nt(dx_ref[...], w_ref[...])

    return pl.pallas_call(
        body, name="out_proj_bwd", grid=(S // tm,),
        in_specs=[pl.BlockSpec((tm, D), lambda i: (i, 0)), pl.BlockSpec((D, D), lambda i: (0, 0))],
        out_specs=pl.BlockSpec((tm, D), lambda i: (i, 0)),
        out_shape=jax.ShapeDtypeStruct((S, D), F32),
        compiler_params=_cp(("parallel",)),
    )(dx2b, wout)


def _in_proj_bwd(dproj, win, dres, xs, r, nw):
    tm = 1024

    def body(dp_ref, w_ref, dres_hbm, x_hbm, r_ref, nw_ref, dx_ref, st_ref, dres_buf, x_buf, sems):
        m, p = pl.program_id(0), pl.program_id(1)
        tail_in = _row_copies((dres_hbm, x_hbm), (dres_buf, x_buf), sems, m, tm)

        @pl.when(p == 0)
        def _():
            dx_ref[...] = jnp.zeros_like(dx_ref)
            for cp in tail_in:
                cp.start()

        @pl.when((p == 0) & (m == 0))
        def _():
            st_ref[...] = jnp.zeros_like(st_ref)

        dx_ref[...] += _dot_nt(dp_ref[...], w_ref[...])

        @pl.when(p == NDEV - 1)
        def _():
            for cp in tail_in:
                cp.wait()
            dx, dnw = _rms_bwd_tile(dx_ref[...], x_buf[...], r_ref[...], nw_ref[...])
            dx_ref[...] = dres_buf[...] + dx
            st_ref[0:1, :] += dnw

    row = pl.BlockSpec((tm, D), lambda m, p: (m, 0))
    hbm = pl.BlockSpec(memory_space=pl.ANY)
    return pl.pallas_call(
        body, name="in_proj_bwd", grid=(S // tm, NDEV),
        in_specs=[pl.BlockSpec((tm, N_IN), lambda m, p: (m, p)),
                  pl.BlockSpec((None, D, N_IN), lambda m, p: (p, 0, 0)),
                  hbm, hbm, pl.BlockSpec((tm, 1), lambda m, p: (m, 0)),
                  pl.BlockSpec((1, D), lambda m, p: (0, 0))],
        out_specs=[row, pl.BlockSpec((8, D), lambda m, p: (0, 0))],
        out_shape=[jax.ShapeDtypeStruct((S, D), F32), jax.ShapeDtypeStruct((8, D), F32)],
        scratch_shapes=[pltpu.VMEM((tm, D), F32), pltpu.VMEM((tm, D), F32), pltpu.SemaphoreType.DMA((2,))],
        compiler_params=_cp(("arbitrary", "arbitrary")),
    )(dproj, win, dres, xs, r, nw)


W_IN_PARTS = 2


def _wgrad_in(h1, dproj, part):
    rows = D // W_IN_PARTS

    def body(a_ref, d_ref, o_ref):
        o_ref[...] = _dot_tn(a_ref[...], d_ref[...]).astype(BF16)

    return pl.pallas_call(
        body, name=f"wgrad_in_{part}", grid=(NDEV,),
        in_specs=[pl.BlockSpec((S, rows), lambda p: (0, part)), pl.BlockSpec((S, N_IN), lambda p: (0, p))],
        out_specs=pl.BlockSpec((None, rows, N_IN), lambda p: (p, 0, 0)),
        out_shape=jax.ShapeDtypeStruct((NDEV, rows, N_IN), BF16),
        compiler_params=_cp(("parallel",)),
    )(h1, dproj)


def _wgrad_rows(a3, dy, name):
    def body(a_ref, d_ref, o_ref):
        o_ref[...] = _dot_tn(a_ref[...], d_ref[...]).astype(BF16)

    return pl.pallas_call(
        body, name=name, grid=(NFG,),
        in_specs=[pl.BlockSpec((None, S, N_FG), lambda p: (p, 0, 0)), pl.BlockSpec((S, D), lambda p: (0, 0))],
        out_specs=pl.BlockSpec((None, N_FG, D), lambda p: (p, 0, 0)),
        out_shape=jax.ShapeDtypeStruct((NFG, N_FG, D), BF16),
        compiler_params=_cp(("parallel",)),
    )(a3, dy).reshape(NDEV, N_FF, D)


def _wgrad_out(ma, mr, dx2b):
    half = D // 2
    per = half // N_OUT

    def body(ma_ref, mr_ref, d_ref, o_ref):
        p = pl.program_id(0)

        @pl.when(p < per)
        def _():
            o_ref[...] = _dot_tn(ma_ref[...], d_ref[...]).astype(BF16)

        @pl.when(p >= per)
        def _():
            o_ref[...] = _dot_tn(mr_ref[...], d_ref[...]).astype(BF16)

    return pl.pallas_call(
        body, name="wgrad_out", grid=(NDEV,),
        in_specs=[pl.BlockSpec((S, N_OUT), lambda p: (0, jnp.minimum(p, per - 1))),
                  pl.BlockSpec((S, N_OUT), lambda p: (0, jnp.maximum(p - per, 0))),
                  pl.BlockSpec((S, D), lambda p: (0, 0))],
        out_specs=pl.BlockSpec((None, N_OUT, D), lambda p: (p, 0, 0)),
        out_shape=jax.ShapeDtypeStruct((NDEV, N_OUT, D), BF16),
        compiler_params=_cp(("parallel",)),
    )(ma, mr, dx2b)


def _attn_consts():
    c = np.zeros((AH, 8, AHD), np.float32)
    for h in range(AH):
        c[h, :, :] = 2.0 ** (-(h + 1))
    return jnp.asarray(c)


def _permute_in(dst, src, d, cast=None):
    v = src[...]
    if d > 1:
        v = pltpu.einshape("jrc->rjc", v.reshape(S // d, d, AHD)).reshape(S, AHD)
    dst[...] = v if cast is None else v.astype(cast)


def _natural_order(v, d):
    if d == 1:
        return v
    return pltpu.einshape("rjc->jrc", v.reshape(d, S // d, AHD)).reshape(S, AHD)


def _attn_masks():
    qi = lax.broadcasted_iota(jnp.int32, (CH, CH), 0)
    kj = lax.broadcasted_iota(jnp.int32, (CH, CH), 1)
    dist_c = (qi - kj).astype(F32)
    dist_p = (qi - kj + CH).astype(F32)
    return (qi >= kj)[None], (kj >= qi)[None], dist_c[None], dist_p[None]


GB = 16


def _bdot_nt(a, b):
    return lax.dot_general(a, b, (((2,), (2,)), ((0,), (0,))), preferred_element_type=F32)


def _bdot(a, b):
    return lax.dot_general(a, b, (((2,), (1,)), ((0,), (0,))), preferred_element_type=F32)


def _bdot_tn(a, b):
    return lax.dot_general(a, b, (((1,), (1,)), ((0,), (0,))), preferred_element_type=F32)


def _shift_block(dst, src):
    dst[0:CH, :] = jnp.zeros((CH, AHD), dst.dtype)
    dst[CH:S, :] = src[0:S - CH, :]


def _has_prev(g, nb):
    blk = lax.broadcasted_iota(jnp.int32, (GB, 1, 1), 0) + g * GB
    return (blk & (nb - 1)) != 0


def _blocks(ref, g):
    return ref[g * GB * CH:(g + 1) * GB * CH, :].reshape(GB, CH, AHD)


def _attn_fwd(proj):
    scale = 1.0 / math.sqrt(AHD)

    def body(c_ref, q_ref, k_ref, v_ref, o_ref, ob_ref, lse_ref, qd, kd, vd, kps, vps, od, ld, *nat):
        onat, lnat = nat[0:3], nat[3:6]
        slope = c_ref[0:1, :]
        mask_c, mask_p, dist_c, dist_p = _attn_masks()
        for pi, (d, nb) in enumerate(PATTERNS):
            _permute_in(qd, q_ref, d, BF16)
            _permute_in(kd, k_ref, d, BF16)
            _permute_in(vd, v_ref, d, BF16)
            if nb > 1:
                _shift_block(kps, kd)
                _shift_block(vps, vd)
            bias_c = -(slope * float(d)) * dist_c
            bias_p = -(slope * float(d)) * dist_p
            for g in range(NB // GB):
                q3, k3, v3 = _blocks(qd, g), _blocks(kd, g), _blocks(vd, g)
                s_c = jnp.where(mask_c, _bdot_nt(q3, k3) * scale + bias_c, NEG)
                mx = jnp.max(s_c, axis=-1, keepdims=True)
                if nb > 1:
                    kp3, vp3 = _blocks(kps, g), _blocks(vps, g)
                    s_p = jnp.where(jnp.logical_and(mask_p, _has_prev(g, nb)),
                                    _bdot_nt(q3, kp3) * scale + bias_p, NEG)
                    mx = jnp.maximum(mx, jnp.max(s_p, axis=-1, keepdims=True))
                    l = (jnp.sum(jnp.exp(s_c - mx), axis=-1, keepdims=True)
                         + jnp.sum(jnp.exp(s_p - mx), axis=-1, keepdims=True))
                    lse = mx + jnp.log(l)
                    o3 = _bdot(jnp.exp(s_c - lse).astype(BF16), v3) + _bdot(jnp.exp(s_p - lse).astype(BF16), vp3)
                else:
                    l = jnp.sum(jnp.exp(s_c - mx), axis=-1, keepdims=True)
                    lse = mx + jnp.log(l)
                    o3 = _bdot(jnp.exp(s_c - lse).astype(BF16), v3)
                rows = slice(g * GB * CH, (g + 1) * GB * CH)
                od[rows, :] = o3.reshape(GB * CH, AHD)
                ld[rows, :] = jnp.broadcast_to(lse, (GB, CH, AHD)).reshape(GB * CH, AHD)
            onat[pi][...] = _natural_order(od[...], d)
            lnat[pi][...] = _natural_order(ld[...], d)
        l0, l1, l2 = lnat[0][...], lnat[1][...], lnat[2][...]
        mx = jnp.maximum(jnp.maximum(l0, l1), l2)
        e0, e1, e2 = jnp.exp(l0 - mx), jnp.exp(l1 - mx), jnp.exp(l2 - mx)
        den = e0 + e1 + e2
        out = (e0 / den) * onat[0][...] + (e1 / den) * onat[1][...] + (e2 / den) * onat[2][...]
        o_ref[...] = out
        ob_ref[...] = out.astype(BF16)
        lse_ref[...] = mx + jnp.log(den)

    def col(off):
        return pl.BlockSpec((S, AHD), lambda h: (0, off + h))

    return pl.pallas_call(
        body, name="attn_fwd", grid=(AH,),
        in_specs=[pl.BlockSpec((None, 8, AHD), lambda h: (h, 0, 0)), col(0), col(AH), col(2 * AH)],
        out_specs=[col(0), col(0), col(0)],
        out_shape=[jax.ShapeDtypeStruct((S, AH * AHD), F32), jax.ShapeDtypeStruct((S, AH * AHD), BF16),
                   jax.ShapeDtypeStruct((S, AH * AHD), F32)],
        scratch_shapes=[pltpu.VMEM((S, AHD), BF16) for _ in range(5)]
        + [pltpu.VMEM((S, AHD), F32) for _ in range(8)],
        compiler_params=_cp(("parallel",)),
    )(_attn_consts(), proj, proj, proj)


def _attn_bwd(proj, dmixed, o, lse):
    scale = 1.0 / math.sqrt(AHD)

    def body(c_ref, q_ref, k_ref, v_ref, do_ref, o_ref, lse_ref, dproj_hbm,
             qd, kd, vd, dod, kps, vps, lsd, dld, dqd, dkd, dvd, delta, aq, ak, av, sq, sk, sv, sems):
        h = pl.program_id(0)

        def out_copies(head):
            return [pltpu.make_async_copy(
                st, dproj_hbm.at[:, pl.ds(pl.multiple_of((k * AH + head) * AHD, AHD), AHD)], sems.at[k])
                for k, st in enumerate((sq, sk, sv))]

        slope = c_ref[0:1, :]
        mask_c, mask_p, dist_c, dist_p = _attn_masks()
        delta[...] = jnp.broadcast_to(jnp.sum(do_ref[...] * o_ref[...], axis=-1, keepdims=True), (S, AHD))
        for pi, (d, nb) in enumerate(PATTERNS):
            _permute_in(qd, q_ref, d, BF16)
            _permute_in(kd, k_ref, d, BF16)
            _permute_in(vd, v_ref, d, BF16)
            _permute_in(dod, do_ref, d, BF16)
            _permute_in(lsd, lse_ref, d)
            _permute_in(dld, delta, d)
            if nb > 1:
                _shift_block(kps, kd)
                _shift_block(vps, vd)
            bias_c = -(slope * float(d)) * dist_c
            bias_p = -(slope * float(d)) * dist_p
            for g in range(NB // GB):
                q3, k3, v3, do3 = _blocks(qd, g), _blocks(kd, g), _blocks(vd, g), _blocks(dod, g)
                ls, dl = _blocks(lsd, g), _blocks(dld, g)
                lo, hi = g * GB * CH, (g + 1) * GB * CH
                p_c = jnp.exp(jnp.where(mask_c, _bdot_nt(q3, k3) * scale + bias_c, NEG) - ls)
                ds_c = ((p_c * (_bdot_nt(do3, v3) - dl)) * scale).astype(BF16)
                dq3 = _bdot(ds_c, k3)
                dkd[lo:hi, :] = _bdot_tn(ds_c, q3).reshape(GB * CH, AHD)
                dvd[lo:hi, :] = _bdot_tn(p_c.astype(BF16), do3).reshape(GB * CH, AHD)
                if nb > 1:
                    kp3, vp3 = _blocks(kps, g), _blocks(vps, g)
                    p_p = jnp.exp(jnp.where(jnp.logical_and(mask_p, _has_prev(g, nb)),
                                            _bdot_nt(q3, kp3) * scale + bias_p, NEG) - ls)
                    ds_p = ((p_p * (_bdot_nt(do3, vp3) - dl)) * scale).astype(BF16)
                    dq3 = dq3 + _bdot(ds_p, kp3)
                    dkp = _bdot_tn(ds_p, q3).reshape(GB * CH, AHD)
                    dvp = _bdot_tn(p_p.astype(BF16), do3).reshape(GB * CH, AHD)
                    if g == 0:
                        dkd[0:hi - CH, :] += dkp[CH:, :]
                        dvd[0:hi - CH, :] += dvp[CH:, :]
                    else:
                        dkd[lo - CH:hi - CH, :] += dkp
                        dvd[lo - CH:hi - CH, :] += dvp
                dqd[lo:hi, :] = dq3.reshape(GB * CH, AHD)
            ln = S // d
            for acc, src in ((aq, dqd), (ak, dkd), (av, dvd)):
                if pi == 0:
                    acc[...] = src[...]
                else:
                    acc[...] += _natural_order(src[...], d)

        @pl.when(h > 0)
        def _():
            for cp in out_copies(h - 1):
                cp.wait()

        sq[...] = aq[...].astype(BF16)
        sk[...] = ak[...].astype(BF16)
        sv[...] = av[...].astype(BF16)
        for cp in out_copies(h):
            cp.start()

        @pl.when(h == AH - 1)
        def _():
            for cp in out_copies(h):
                cp.wait()

    def col(off):
        return pl.BlockSpec((S, AHD), lambda h: (0, off + h))

    return pl.pallas_call(
        body, name="attn_bwd", grid=(AH,),
        in_specs=[pl.BlockSpec((None, 8, AHD), lambda h: (h, 0, 0)), col(0), col(AH), col(2 * AH),
                  col(0), col(0), col(0)],
        out_specs=pl.BlockSpec(memory_space=pl.ANY),
        out_shape=jax.ShapeDtypeStruct((S, NDEV * N_IN), BF16),
        scratch_shapes=[pltpu.VMEM((S, AHD), BF16) for _ in range(6)]
        + [pltpu.VMEM((S, AHD), F32) for _ in range(9)]
        + [pltpu.VMEM((S, AHD), BF16) for _ in range(3)] + [pltpu.SemaphoreType.DMA((3,))],
        compiler_params=_cp(("arbitrary",)),
    )(_attn_consts(), proj, proj, proj, dmixed, o, lse)


def _ret_consts():
    c = np.zeros((RH, 8, RHD), np.float32)
    for h in range(RH):
        c[h, :, :] = np.log(np.float32(1.0) - np.float32(2.0 ** (-5.0 - h)))
    return jnp.asarray(c)


def _ret_factors(lg):
    i = lax.broadcasted_iota(jnp.int32, (CH, CH), 0)
    j = lax.broadcasted_iota(jnp.int32, (CH, CH), 1)
    dif = (i - j).astype(F32)
    decay = jnp.where(dif >= 0, jnp.exp(lg[:, 0:CH] * jnp.maximum(dif, 0.0)), 0.0)
    row = lax.broadcasted_iota(jnp.int32, (CH, RHD), 0).astype(F32)
    zeta = jnp.exp(lg * (CH - 1.0 - row))
    xi = jnp.exp(lg * (row + 1.0))
    return decay, zeta, xi, jnp.exp(lg * float(CH))


CBK = 8
RSTEPS = NB // CBK


def _ret_specs(rev):
    off = 3 * AH * AHD // RHD
    rows = CBK * CH

    def ch(n):
        return (RSTEPS - 1 - n) if rev else n

    def col(k):
        return pl.BlockSpec((rows, RHD), lambda h, n: (ch(n), off + k * RH + h))

    own = pl.BlockSpec((rows, RHD), lambda h, n: (ch(n), h))
    state = pl.BlockSpec((None, CBK, RHD, RHD), lambda h, n: (h, ch(n), 0, 0))
    const = pl.BlockSpec((None, 8, RHD), lambda h, n: (h, 0, 0))
    dm = pl.BlockSpec((rows, RHD), lambda h, n: (ch(n), AH * AHD // RHD + h))
    return col, own, state, const, dm


def _chunks(x):
    return x.reshape(CBK, CH, RHD)


def _ret_fwd(proj):
    def body(c_ref, q_ref, k_ref, v_ref, g_ref, ret_ref, mr_ref, st_ref, r_acc):
        n = pl.program_id(1)

        @pl.when(n == 0)
        def _():
            r_acc[...] = jnp.zeros_like(r_acc)

        decay, zeta, xi, gch = _ret_factors(c_ref[0:1, :])
        q3 = _chunks(q_ref[...].astype(BF16))
        kc = _chunks(k_ref[...] * (1.0 / math.sqrt(RHD)))
        k3 = kc.astype(BF16)
        v3 = _chunks(v_ref[...].astype(BF16))
        kv3 = _bdot_tn((kc * zeta[None]).astype(BF16), v3)
        r = r_acc[...]
        for i in range(CBK):
            st_ref[i] = r.astype(BF16)
            r = r * gch + kv3[i]
        r_acc[...] = r
        scores = _bdot_nt(q3, k3) * decay[None]
        ret = (_bdot(scores.astype(BF16), v3) + _bdot(q3, st_ref[...]) * xi[None]).reshape(CBK * CH, RHD)
        ret_ref[...] = ret
        rr = lax.rsqrt(jnp.mean(ret * ret, axis=-1, keepdims=True) + EPS)
        gv = g_ref[...]
        mr_ref[...] = ((gv * _sigmoid(gv)) * (ret * rr)).astype(BF16)

    col, own, state, const, _ = _ret_specs(False)
    return pl.pallas_call(
        body, name="ret_fwd", grid=(RH, RSTEPS),
        in_specs=[const, col(0), col(1), col(2), col(3)],
        out_specs=[own, own, state],
        out_shape=[jax.ShapeDtypeStruct((S, RH * RHD), F32), jax.ShapeDtypeStruct((S, RH * RHD), BF16),
                   jax.ShapeDtypeStruct((RH, NB, RHD, RHD), BF16)],
        scratch_shapes=[pltpu.VMEM((RHD, RHD), F32)],
        compiler_params=_cp(("parallel", "arbitrary")),
    )(_ret_consts(), proj, proj, proj, proj)


def _ret_bwd(proj, ret, states, dmixed, dproj):
    rows = CBK * CH
    col0 = 3 * AH * AHD

    def body(c_ref, q_ref, k_ref, v_ref, g_ref, ret_ref, st_ref, dm_ref, dproj_in, dproj_hbm, g_acc, gs,
             sq, sk, sv, sg, sems):
        del dproj_in
        h, n = pl.program_id(0), pl.program_id(1)
        step = h * RSTEPS + n

        def out_copies(t):
            hh, nn = t // RSTEPS, t % RSTEPS
            r0 = pl.multiple_of((RSTEPS - 1 - nn) * rows, rows)
            return [pltpu.make_async_copy(
                st, dproj_hbm.at[pl.ds(r0, rows), pl.ds(pl.multiple_of(col0 + (k * RH + hh) * RHD, RHD), RHD)],
                sems.at[k]) for k, st in enumerate((sq, sk, sv, sg))]

        @pl.when(n == 0)
        def _():
            g_acc[...] = jnp.zeros_like(g_acc)

        decay, zeta, xi, gch = _ret_factors(c_ref[0:1, :])
        ret_v = ret_ref[...]
        rr = lax.rsqrt(jnp.mean(ret_v * ret_v, axis=-1, keepdims=True) + EPS)
        gv = g_ref[...]
        sgm = _sigmoid(gv)
        dmix = dm_ref[...]
        dgate = ((dmix * (ret_v * rr)) * (sgm * (1.0 + gv * (1.0 - sgm)))).astype(BF16)
        dretn = dmix * (gv * sgm)
        dret = _chunks(rr * dretn - ret_v * ((rr * rr * rr) * jnp.mean(dretn * ret_v, axis=-1, keepdims=True)))

        q3 = _chunks(q_ref[...].astype(BF16))
        kc = _chunks(k_ref[...] * (1.0 / math.sqrt(RHD)))
        k3 = kc.astype(BF16)
        v3 = _chunks(v_ref[...].astype(BF16))
        d3 = dret.astype(BF16)
        dxi = (dret * xi[None]).astype(BF16)
        kz = (kc * zeta[None]).astype(BF16)
        dr3 = _bdot_tn(q3, dxi)
        acc = g_acc[...]
        for i in reversed(range(CBK)):
            gs[i] = acc.astype(BF16)
            acc = dr3[i] + gch * acc
        g_acc[...] = acc
        g3 = gs[...]
        sc = (_bdot_nt(q3, k3) * decay[None]).astype(BF16)
        da = (_bdot_nt(d3, v3) * decay[None]).astype(BF16)
        dq = _bdot(da, k3) + _bdot_nt(dxi, st_ref[...])
        dkc = _bdot_tn(da, q3) + _bdot_nt(v3, g3) * zeta[None]
        dv = _bdot_tn(sc, d3) + _bdot(kz, g3)

        @pl.when(step > 0)
        def _():
            for cp in out_copies(step - 1):
                cp.wait()

        sq[...] = dq.reshape(rows, RHD).astype(BF16)
        sk[...] = (dkc * (1.0 / math.sqrt(RHD))).reshape(rows, RHD).astype(BF16)
        sv[...] = dv.reshape(rows, RHD).astype(BF16)
        sg[...] = dgate
        for cp in out_copies(step):
            cp.start()

        @pl.when(step == RH * RSTEPS - 1)
        def _():
            for cp in out_copies(step):
                cp.wait()

    col, own, state, const, dm = _ret_specs(True)
    hbm = pl.BlockSpec(memory_space=pl.ANY)
    return pl.pallas_call(
        body, name="ret_bwd", grid=(RH, RSTEPS),
        in_specs=[const, col(0), col(1), col(2), col(3), own, state, dm, hbm],
        out_specs=hbm,
        out_shape=jax.ShapeDtypeStruct(dproj.shape, dproj.dtype),
        input_output_aliases={8: 0},
        scratch_shapes=[pltpu.VMEM((RHD, RHD), F32), pltpu.VMEM((CBK, RHD, RHD), BF16)]
        + [pltpu.VMEM((rows, RHD), BF16) for _ in range(4)] + [pltpu.SemaphoreType.DMA((4,))],
        compiler_params=_cp(("arbitrary", "arbitrary")),
    )(_ret_consts(), proj, proj, proj, proj, ret, states, dmixed, dproj)


class _NoReduction:
    def start(self, group, grads):
        pass

    def local(self, name, first=()):
        return []

    def landed(self, name):
        return []

    def update(self, name):
        return []


def _local_step(x, tgt, nw1, nw2, nw3, win, wout, wg, wu, wd, red=None):
    red = red or _NoReduction()

    def after(values, first):
        return lax.optimization_barrier((tuple(values), tuple(first)))[0]

    wg, wu, wd = (w.reshape(NFG, N_FG, D) for w in (wg, wu, wd))
    h1, r1 = _rms_fwd(x, nw1)
    proj = _proj(h1, win)
    o, ma, lse = _attn_fwd(proj)
    ret, mr, states = _ret_fwd(proj)
    x2, h2, r2 = _out_proj_rms(x, ma, mr, wout, nw2)
    a, dadg, dadu = _ffn_up(h2, wg, wu)
    dx3, dx3b, st3 = _ffn_down_loss(x2, a, wd, nw3, tgt)

    dwd = _wgrad_rows(a, dx3b, "wgrad_down")
    red.start(["w_down"], [dwd])
    (dx3b,) = after([dx3b], [dwd])
    part = _ffn_down_bwd(dx3b, wd, dadg, dadu, 0)
    (dx3b,) = after([dx3b], red.local("w_down", first=part))
    dg, du = _ffn_down_bwd(dx3b, wd, dadg, dadu, 1, part)
    dwg = _wgrad_rows(dg, h2, "wgrad_gate")
    red.start(["w_gate"], [dwg])
    (du,) = after([du], [dwg])
    dwu = _wgrad_rows(du, h2, "wgrad_up")
    red.start(["w_up"], [dwu])
    dg, du = after([dg, du], red.local("w_gate", first=[dwu] + red.landed("w_down")))
    dx2, dx2b, st2 = _ffn_up_bwd(dg, du, wg, wu, dx3, x2, r2, nw2)
    (dx2b,) = after([dx2b], red.local("w_up", first=[dx2b]))
    dwo = _wgrad_out(ma, mr, dx2b)
    red.start(["w_out"], [dwo])
    (dx2b,) = after([dx2b], [dwo])
    dmixed = _out_proj_bwd(dx2b, wout)
    dproj = _attn_bwd(proj, dmixed, o, lse)
    (dmixed,) = after([dmixed], red.local("w_out", first=[dproj] + red.landed("w_gate")))
    dproj = _ret_bwd(proj, ret, states, dmixed, dproj)
    (dwi0,) = after([_wgrad_in(h1, dproj, 0)], red.landed("w_up"))
    red.start(["w_in_0"], [dwi0])
    (dproj,) = after([dproj], [dwi0])
    dwi1 = _wgrad_in(h1, dproj, 1)
    red.start(["w_in_1"], [dwi1])
    sums = red.local("w_in_0", first=[dwi1] + red.landed("w_out"))
    sums = red.local("w_in_1", first=sums + red.update("w_down"))
    (dproj,) = after([dproj], sums)
    gx, st1 = _in_proj_bwd(dproj, win, dx2, x, r1, nw1)
    dwi = jnp.concatenate([dwi0, dwi1], axis=1)
    stats = jnp.concatenate([st1[0:1], st2[0:1], st3[0:2], jnp.zeros((4, D), F32)], axis=0)
    return stats, gx, dwi, dwo, dwg, dwu, dwd


def _place():
    x, y, c = lax.axis_index("x"), lax.axis_index("y"), lax.axis_index("c")
    return x, y, c, [(1 - x, y), (x, 1 - y), (1 - x, 1 - y)]


def _handshake(peers):
    barrier = pltpu.get_barrier_semaphore()
    for peer in peers:
        pl.semaphore_signal(barrier, inc=1, device_id=peer, device_id_type=MESH)
    pl.semaphore_wait(barrier, len(peers))


def _all_gather(shards, name, collective_id):
    na = len(shards)
    SIB, XN0, XN1, YN1, YN0, VIA_X, VIA_Y = 0, 1, 2, 3, 4, 5, 6
    D2D = {XN0: 7, XN1: 8, YN1: 9, YN0: 10, VIA_X: 11, VIA_Y: 12}

    def body(*refs):
        ins, outs = refs[:na], refs[na:2 * na]
        send_sems, recv_sems, local_sems = refs[2 * na:]
        x, y, c, _ = _place()
        me, sib = (x, y, c), (x, y, 1 - c)
        xn, yn, dg = (1 - x, y, c), (x, 1 - y, c), (1 - x, 1 - y, c)
        _handshake([sib, xn, yn])

        def part(ref, h):
            rows = ref.shape[0] // 2
            return ref if h is None else ref.at[pl.ds(h * rows, rows)]

        def block(a, owner, h):
            return part(outs[a].at[4 * owner[0] + 2 * owner[1] + owner[2]], h)

        def copy(a, k, owner, h, to, own_src=False):
            return pltpu.make_async_remote_copy(
                src_ref=part(ins[a], h) if own_src else block(a, owner, h), dst_ref=block(a, owner, h),
                send_sem=send_sems.at[a, k], recv_sem=recv_sems.at[a, k], device_id=to, device_id_type=MESH)

        def other(p):
            return (p[0], p[1], 1 - c)

        mine = [pltpu.make_async_copy(ins[a], block(a, me, None), local_sems.at[a]) for a in range(na)]
        for cp in mine:
            cp.start()
        sent = []
        for a in range(na):
            sent += [copy(a, XN0, me, 0, xn, True), copy(a, YN1, me, 1, yn, True),
                     copy(a, XN1, me, 1, xn, True), copy(a, YN0, me, 0, yn, True)]
        sent += [copy(a, SIB, me, None, sib, True) for a in range(na)]
        for cp in sent:
            cp.start()

        def landed(a, k, owner, h, then):
            copy(a, k, owner, h, me).wait_recv()
            for k2, to in then + [(D2D[k], sib)]:
                cp = copy(a, k2, owner, h, to)
                cp.start()
                sent.append(cp)

        for a in range(na):
            landed(a, XN0, xn, 0, [(VIA_Y, yn)])
            landed(a, YN1, yn, 1, [(VIA_X, xn)])
            landed(a, XN1, xn, 1, [])
            landed(a, YN0, yn, 0, [])
        for a in range(na):
            landed(a, VIA_Y, dg, 0, [])
            landed(a, VIA_X, dg, 1, [])
        for a in range(na):
            copy(a, SIB, sib, None, me).wait_recv()
            for k, owner, h in ((XN0, xn, 0), (XN1, xn, 1), (YN1, yn, 1), (YN0, yn, 0), (VIA_Y, dg, 0), (VIA_X, dg, 1)):
                copy(a, D2D[k], other(owner), h, me).wait_recv()
        for cp in sent:
            cp.wait_send()
        for cp in mine:
            cp.wait()

    return _sequencer_call(
        body, name, collective_id,
        [jax.ShapeDtypeStruct((NDEV,) + s.shape, s.dtype) for s in shards],
        [pltpu.SemaphoreType.DMA((na, 13)), pltpu.SemaphoreType.DMA((na, 13)), pltpu.SemaphoreType.DMA((na,))])(*shards)


def _sequencer_call(body, name, collective_id, out_type, scratch_types):
    return pl.kernel(
        body, name=name, out_type=out_type,
        mesh=plsc.ScalarSubcoreMesh(axis_name="sequencer", num_cores=1),
        scratch_types=scratch_types,
        compiler_params=pltpu.CompilerParams(collective_id=collective_id))


def _exchange_sibling(grads, name, collective_id):
    na = len(grads)

    def body(*refs):
        ins, outs = refs[:na], refs[na:2 * na]
        send_sems, recv_sems = refs[2 * na:]
        x, y, c, _ = _place()
        _handshake([(x, y, 1 - c)])
        cps = []
        for a in range(na):
            for k in range(4):
                cps.append(pltpu.make_async_remote_copy(
                    src_ref=ins[a].at[2 * k + (1 - c)], dst_ref=outs[a].at[k],
                    send_sem=send_sems.at[a, k], recv_sem=recv_sems.at[a, k],
                    device_id=(x, y, 1 - c), device_id_type=MESH))
        for cp in cps:
            cp.start()
        for cp in cps:
            cp.wait()

    return _sequencer_call(
        body, name, collective_id,
        [jax.ShapeDtypeStruct((4,) + g.shape[1:], g.dtype) for g in grads],
        [pltpu.SemaphoreType.DMA((na, 4)), pltpu.SemaphoreType.DMA((na, 4))])(*grads)


def _row_tile(rows, cols):
    for t in (512, 256, 176, 128, 64, 32, 16):
        if rows % t == 0 and t * cols * 4 <= (2 << 20):
            return t
    raise ValueError((rows, cols))


def _chip_sum(place, g, got, name):
    _, r, c = g.shape
    tm = r

    def body(pos_ref, g_ref, got_ref, o_ref):
        o_ref[...] = (g_ref[...].astype(F32) + got_ref[...].astype(F32)).astype(BF16)

    def chip(j, pos):
        return 2 * (pos[0] ^ jnp.where(j == 1, 0, 1)) + (pos[1] ^ jnp.where(j == 0, 0, 1))

    return pl.pallas_call(
        body, name=name,
        grid_spec=pltpu.PrefetchScalarGridSpec(
            num_scalar_prefetch=1, grid=(3, r // tm),
            in_specs=[pl.BlockSpec((None, tm, c), lambda j, i, pos: (2 * chip(j, pos) + pos[2], i, 0)),
                      pl.BlockSpec((None, tm, c), lambda j, i, pos: (chip(j, pos), i, 0))],
            out_specs=pl.BlockSpec((None, tm, c), lambda j, i, pos: (j, i, 0))),
        out_shape=jax.ShapeDtypeStruct((3, r, c), BF16),
        compiler_params=_cp(("parallel", "parallel")),
    )(place, g, got)


def _exchange_chips(sums, name, collective_id):
    na = len(sums)

    def body(*refs):
        ins, outs = refs[:na], refs[na:2 * na]
        send_sems, recv_sems = refs[2 * na:]
        x, y, c, chips = _place()
        _handshake([(*chip, c) for chip in chips])
        cps = []
        for a in range(na):
            for j, chip in enumerate(chips):
                cps.append(pltpu.make_async_remote_copy(
                    src_ref=ins[a].at[j], dst_ref=outs[a].at[j],
                    send_sem=send_sems.at[a, j], recv_sem=recv_sems.at[a, j],
                    device_id=(*chip, c), device_id_type=MESH))
        for cp in cps:
            cp.start()
        for cp in cps:
            cp.wait()

    return _sequencer_call(
        body, name, collective_id,
        [jax.ShapeDtypeStruct((3,) + s.shape[1:], s.dtype) for s in sums],
        [pltpu.SemaphoreType.DMA((na, 3)), pltpu.SemaphoreType.DMA((na, 3))])(*sums)


def _exchange_stats(stats, collective_id):
    def body(st_in, st_out, st_send, st_recv, local_sem):
        x, y, c, _ = _place()
        me_idx = 4 * x + 2 * y + c
        peers = [(x ^ ((k >> 2) & 1), y ^ ((k >> 1) & 1), c ^ (k & 1)) for k in range(1, 8)]
        _handshake(peers)
        mine = pltpu.make_async_copy(st_in, st_out.at[me_idx], local_sem)
        mine.start()
        cps = [pltpu.make_async_remote_copy(
            src_ref=st_in, dst_ref=st_out.at[me_idx], send_sem=st_send.at[k], recv_sem=st_recv.at[k],
            device_id=peer, device_id_type=MESH) for k, peer in enumerate(peers)]
        for cp in cps:
            cp.start()
        for cp in cps:
            cp.wait()
        mine.wait()

    return _sequencer_call(
        body, "exchange_stats", collective_id,
        jax.ShapeDtypeStruct((NDEV,) + stats.shape, stats.dtype),
        [pltpu.SemaphoreType.DMA((7,)), pltpu.SemaphoreType.DMA((7,)), pltpu.SemaphoreType.DMA])(stats)


class _Reduction:
    def __init__(self, place, first_collective_id, state):
        self.place = place
        self.ids = iter(range(first_collective_id, 32))
        self.state = state
        self.groups = {}
        self.updates = {}

    def next_id(self):
        return next(self.ids)

    def start(self, group, grads):
        got = _exchange_sibling(grads, "sibling_exchange_" + group[0], self.next_id())
        self.groups[group[0]] = dict(names=group, grads=grads, got=got)

    def local(self, name, first=()):
        grp = self.groups[name]
        grads = lax.optimization_barrier((tuple(grp["grads"]), tuple(first)))[0]
        grp["sums"] = [_chip_sum(self.place, g, s, "chip_sum_" + n)
                       for g, s, n in zip(grads, grp["got"], grp["names"])]
        grp["chips"] = _exchange_chips(grp["sums"], "chip_exchange_" + name, self.next_id())
        return grp["sums"]

    def landed(self, name):
        return list(self.groups[name]["chips"])

    def update(self, name):
        if name not in self.updates:
            grp = next(g for g in self.groups.values() if name in g["names"])
            k = grp["names"].index(name)
            w, m, v, part, parts = self.state[name]
            before = self.update(f"{name[:-1]}{part - 1}") if part else None
            self.updates[name] = _shard_update(self.place, w, m, v, grp["grads"][k], grp["got"][k],
                                               grp["chips"][k], "update_" + name, part, parts, before)
        return list(self.updates[name])


def _adamw(w, g, m, v):
    m = ADAM_B1 * m + (1.0 - ADAM_B1) * g
    v = ADAM_B2 * v + (1.0 - ADAM_B2) * (g * g)
    m_hat = m / (1.0 - ADAM_B1 ** ADAM_STEP)
    v_hat = v / (1.0 - ADAM_B2 ** ADAM_STEP)
    delta = -ADAM_LR * (m_hat / (jnp.sqrt(v_hat) + ADAM_EPS) + ADAM_WD * w)
    return delta, m, v


def _shard_update(place, w, m, v, g, got_sib, got_chips, name, part=0, parts=1, before=None):
    r, c = w.shape
    rp = r // parts
    tm = _row_tile(rp, c)
    off = part * (rp // tm)

    def body(pos_ref, w_ref, m_ref, v_ref, g_ref, s_ref, c_ref, *rest):
        go_ref, d_ref, mo_ref, vo_ref = rest[-4:]
        grad = g_ref[...].astype(F32) + s_ref[...].astype(F32)
        for j in range(3):
            grad = grad + c_ref[j].astype(F32)
        delta, mn, vn = _adamw(w_ref[...], grad, m_ref[...], v_ref[...])
        go_ref[...] = grad
        d_ref[...] = delta
        mo_ref[...] = mn
        vo_ref[...] = vn

    row = pl.BlockSpec((tm, c), lambda i, pos: (i + off, 0))
    before = list(before or [])
    return pl.pallas_call(
        body, name=name,
        grid_spec=pltpu.PrefetchScalarGridSpec(
            num_scalar_prefetch=1, grid=(rp // tm,),
            in_specs=[row, row, row,
                      pl.BlockSpec((None, tm, c), lambda i, pos: (4 * pos[0] + 2 * pos[1] + pos[2], i, 0)),
                      pl.BlockSpec((None, tm, c), lambda i, pos: (2 * pos[0] + pos[1], i, 0)),
                      pl.BlockSpec((3, tm, c), lambda i, pos: (0, i, 0))]
            + [pl.BlockSpec(memory_space=pl.ANY)] * len(before),
            out_specs=[row, row, row, row]),
        out_shape=[jax.ShapeDtypeStruct((r, c), F32)] * 4,
        input_output_aliases={7 + k: k for k in range(len(before))},
        compiler_params=_cp(("parallel",)),
    )(place, w, m, v, g, got_sib, got_chips, *before)


def _small_update(stats_all, ws, ms, vs):
    def body(st_ref, w_ref, m_ref, v_ref, go_ref, d_ref, mo_ref, vo_ref):
        grad = st_ref[0]
        for k in range(1, NDEV):
            grad = grad + st_ref[k]
        delta, mn, vn = _adamw(w_ref[...], grad, m_ref[...], v_ref[...])
        go_ref[...] = grad
        d_ref[...] = delta
        mo_ref[...] = mn
        vo_ref[...] = vn

    return pl.pallas_call(
        body, name="small_update",
        out_shape=[jax.ShapeDtypeStruct((8, D), F32)] * 4,
        compiler_params=_cp(),
    )(stats_all, ws, ms, vs)


def kernel(x, norm_mix_w, w_in, w_out, norm_ffn_w, w_gate, w_up, w_down, norm_final_w, loss_target, m_norm_mix_w, m_w_in, m_w_out, m_norm_ffn_w, m_w_gate, m_w_up, m_w_down, m_norm_final_w, v_norm_mix_w, v_w_in, v_w_out, v_norm_ffn_w, v_w_gate, v_w_up, v_w_down, v_norm_final_w):
    tr = {"w_gate", "w_up"}
    names = ["w_in", "w_out", "w_gate", "w_up", "w_down"]

    def view(a, n):
        return a[0].T if n in tr else a[0]

    big_w = [view(a, n) for a, n in zip([w_in, w_out, w_gate, w_up, w_down], names)]
    big_m = [view(a, n) for a, n in zip([m_w_in, m_w_out, m_w_gate, m_w_up, m_w_down], names)]
    big_v = [view(a, n) for a, n in zip([v_w_in, v_w_out, v_w_gate, v_w_up, v_w_down], names)]

    shards = [_cast_bf16(w, "cast_" + n) for w, n in zip(big_w, names)]
    (win,) = _all_gather(shards[0:1], "all_gather_w_in", 1)
    (wout,) = _all_gather(shards[1:2], "all_gather_w_out", 2)
    wg, wu = _all_gather(shards[2:4], "all_gather_gate_up", 3)
    (wd,) = _all_gather(shards[4:5], "all_gather_w_down", 4)
    nw3 = norm_final_w.reshape(1, D)
    place = jnp.stack([lax.axis_index("x"), lax.axis_index("y"), lax.axis_index("c")]).astype(jnp.int32)
    state = {n: (w, m, v, 0, 1) for n, w, m, v in zip(names, big_w, big_m, big_v)}
    for part in range(W_IN_PARTS):
        state[f"w_in_{part}"] = state["w_in"][:3] + (part, W_IN_PARTS)
    red = _Reduction(place, 5, state)
    stats, gx, *_ = _local_step(
        x[0], loss_target[0], norm_mix_w, norm_ffn_w, nw3, win, wout.reshape(D, D), wg, wu, wd, red)
    stats_all = _exchange_stats(stats, red.next_id())
    upd = [red.update(f"w_in_{W_IN_PARTS - 1}" if n == "w_in" else n) for n in names]
    stats_all = lax.optimization_barrier((stats_all, tuple(upd[0])))[0]

    def rows(a, b, c):
        return jnp.concatenate([a.reshape(1, D), b.reshape(1, D), c.reshape(1, D), jnp.zeros((5, D), F32)], axis=0)

    sg, sd, sm, sv = _small_update(stats_all, rows(norm_mix_w, norm_ffn_w, norm_final_w),
                                   rows(m_norm_mix_w, m_norm_ffn_w, m_norm_final_w),
                                   rows(v_norm_mix_w, v_norm_ffn_w, v_norm_final_w))
    loss = sg[3, 0]

    def outs(k, small):
        big = [(u[k].T if n in tr else u[k])[None] for u, n in zip(upd, names)]
        return [small[0:1], big[0], big[1], small[1:2], big[2], big[3], big[4], small[2]]

    return (loss, gx[None], *outs(0, sg), *outs(1, sd), *outs(2, sm), *outs(3, sv))
```

```python
import functools
import math

import numpy as np
import jax
import jax.numpy as jnp
from jax import lax
from jax.experimental import pallas as pl
from jax.experimental.pallas import tpu as pltpu
from jax.experimental.pallas import tpu_sc as plsc

F32 = jnp.float32
BF16 = jnp.bfloat16

S = 2048
D = 2048
NDEV = 8
N_IN = 7168 // NDEV
N_FF = 5632 // NDEV
NFG, N_FG = NDEV // 2, 2 * N_FF
N_OUT = 2048 // NDEV
AH, AHD = 8, 128
RH, RHD = 4, 256
CH = 128
NB = S // CH
EPS = 1e-6
PATTERNS = ((1, 16), (4, 4), (16, 1))
NEG = -1e30
VMEM_LIMIT = 56 * 1024 * 1024

ADAM_LR, ADAM_B1, ADAM_B2, ADAM_EPS, ADAM_WD, ADAM_STEP = 0.001, 0.9, 0.999, 1e-08, 0.01, 10
MESH = pl.DeviceIdType.MESH


def _cp(sem=None):
    return pltpu.CompilerParams(dimension_semantics=sem, vmem_limit_bytes=VMEM_LIMIT)


def _dot(a, b):
    return jnp.dot(a, b, preferred_element_type=F32)


def _dot_nt(a, b):
    return lax.dot_general(a, b, (((1,), (1,)), ((), ())), preferred_element_type=F32)


def _dot_tn(a, b):
    return lax.dot_general(a, b, (((0,), (0,)), ((), ())), preferred_element_type=F32)


def _sigmoid(x):
    return 0.5 * jnp.tanh(0.5 * x) + 0.5


def _cast_bf16(w, name):
    r, c = w.shape
    tm = r if r <= 1024 else 512

    def body(w_ref, o_ref):
        o_ref[...] = w_ref[...].astype(BF16)

    return pl.pallas_call(
        body, name=name, grid=(r // tm,),
        in_specs=[pl.BlockSpec((tm, c), lambda i: (i, 0))],
        out_specs=pl.BlockSpec((tm, c), lambda i: (i, 0)),
        out_shape=jax.ShapeDtypeStruct((r, c), BF16),
        compiler_params=_cp(("parallel",)),
    )(w)


def _rms_fwd(x, nw):
    tm = 256

    def body(x_ref, w_ref, h_ref, r_ref):
        xs = x_ref[...]
        r = lax.rsqrt(jnp.mean(xs * xs, axis=-1, keepdims=True) + EPS)
        h_ref[...] = ((xs * r) * w_ref[...]).astype(BF16)
        r_ref[...] = r

    return pl.pallas_call(
        body, name="rms_fwd", grid=(S // tm,),
        in_specs=[pl.BlockSpec((tm, D), lambda i: (i, 0)), pl.BlockSpec((1, D), lambda i: (0, 0))],
        out_specs=[pl.BlockSpec((tm, D), lambda i: (i, 0)), pl.BlockSpec((tm, 1), lambda i: (i, 0))],
        out_shape=[jax.ShapeDtypeStruct((S, D), BF16), jax.ShapeDtypeStruct((S, 1), F32)],
        compiler_params=_cp(("parallel",)),
    )(x, nw)


def _row_copies(hbm_refs, bufs, sems, m, tm):
    rows = pl.ds(pl.multiple_of(m * tm, tm), tm)
    return [pltpu.make_async_copy(h.at[rows], b, sems.at[i]) for i, (h, b) in enumerate(zip(hbm_refs, bufs))]


def _rms_bwd_tile(dh, xs, r, nw):
    dnw = jnp.sum(dh * (xs * r), axis=0, keepdims=True)
    gy = dh * nw
    dx = r * gy - xs * ((r * r * r) * jnp.mean(gy * xs, axis=-1, keepdims=True))
    return dx, dnw


def _proj(h1, win):
    tm = 1024

    def body(a_ref, w_ref, o_ref):
        o_ref[...] = _dot(a_ref[...], w_ref[...])

    return pl.pallas_call(
        body, name="proj", grid=(NDEV, S // tm),
        in_specs=[pl.BlockSpec((tm, D), lambda p, m: (m, 0)),
                  pl.BlockSpec((None, D, N_IN), lambda p, m: (p, 0, 0))],
        out_specs=pl.BlockSpec((tm, N_IN), lambda p, m: (m, p)),
        out_shape=jax.ShapeDtypeStruct((S, NDEV * N_IN), F32),
        compiler_params=_cp(("parallel", "parallel")),
    )(h1, win)


def _out_proj_rms(x, ma, mr, wout, nw):
    tm = 256
    half = D // 2

    def body(x_ref, ma_ref, mr_ref, w_ref, nw_ref, x2_ref, h_ref, r_ref):
        acc = _dot(ma_ref[...], w_ref[0:half, :]) + _dot(mr_ref[...], w_ref[half:D, :])
        x2 = x_ref[...] + acc
        r = lax.rsqrt(jnp.mean(x2 * x2, axis=-1, keepdims=True) + EPS)
        x2_ref[...] = x2
        h_ref[...] = ((x2 * r) * nw_ref[...]).astype(BF16)
        r_ref[...] = r

    return pl.pallas_call(
        body, name="out_proj_rms", grid=(S // tm,),
        in_specs=[pl.BlockSpec((tm, D), lambda i: (i, 0)),
                  pl.BlockSpec((tm, half), lambda i: (i, 0)),
                  pl.BlockSpec((tm, half), lambda i: (i, 0)),
                  pl.BlockSpec((D, D), lambda i: (0, 0)),
                  pl.BlockSpec((1, D), lambda i: (0, 0))],
        out_specs=[pl.BlockSpec((tm, D), lambda i: (i, 0)), pl.BlockSpec((tm, D), lambda i: (i, 0)),
                   pl.BlockSpec((tm, 1), lambda i: (i, 0))],
        out_shape=[jax.ShapeDtypeStruct((S, D), F32), jax.ShapeDtypeStruct((S, D), BF16),
                   jax.ShapeDtypeStruct((S, 1), F32)],
        compiler_params=_cp(("parallel",)),
    )(x, ma, mr, wout, nw)


def _ffn_up(h2, wg, wu):
    tm = 512

    def body(h_ref, wg_ref, wu_ref, a_ref, dadg_ref, dadu_ref):
        h = h_ref[...]
        g = _dot_nt(h, wg_ref[...])
        u = _dot_nt(h, wu_ref[...])
        sg = _sigmoid(g)
        silu = g * sg
        a_ref[...] = (silu * u).astype(BF16)
        dadg_ref[...] = (u * (sg * (1.0 + g * (1.0 - sg)))).astype(BF16)
        dadu_ref[...] = silu.astype(BF16)

    blk = pl.BlockSpec((None, tm, N_FG), lambda p, m: (p, m, 0))
    wblk = pl.BlockSpec((None, N_FG, D), lambda p, m: (p, 0, 0))
    return pl.pallas_call(
        body, name="ffn_up", grid=(NFG, S // tm),
        in_specs=[pl.BlockSpec((tm, D), lambda p, m: (m, 0)), wblk, wblk],
        out_specs=[blk, blk, blk],
        out_shape=[jax.ShapeDtypeStruct((NFG, S, N_FG), BF16)] * 3,
        compiler_params=_cp(("parallel", "parallel")),
    )(h2, wg, wu)


def _ffn_down_loss(x2, a, wd, nw, tgt):
    tm = 512

    def body(x2_hbm, a_ref, w_ref, nw_ref, t_hbm, dx_ref, dxb_ref, st_ref, acc_ref, x2_buf, t_buf, sems):
        m, p = pl.program_id(0), pl.program_id(1)
        tail_in = _row_copies((x2_hbm, t_hbm), (x2_buf, t_buf), sems, m, tm)

        @pl.when(p == 0)
        def _():
            acc_ref[...] = jnp.zeros_like(acc_ref)
            for cp in tail_in:
                cp.start()

        @pl.when((p == 0) & (m == 0))
        def _():
            st_ref[...] = jnp.zeros_like(st_ref)

        acc_ref[...] += _dot(a_ref[...], w_ref[...])

        @pl.when(p == NFG - 1)
        def _():
            for cp in tail_in:
                cp.wait()
            x3 = x2_buf[...] + acc_ref[...]
            nwv = nw_ref[...]
            r = lax.rsqrt(jnp.mean(x3 * x3, axis=-1, keepdims=True) + EPS)
            y = (x3 * r) * nwv
            err = y - t_buf[...]
            loss = 0.5 * jnp.sum(jnp.mean(err * err, axis=-1, keepdims=True), axis=0, keepdims=True)
            dy = err * (1.0 / D)
            dx, dnw = _rms_bwd_tile(dy, x3, r, nwv)
            dx_ref[...] = dx
            dxb_ref[...] = dx.astype(BF16)
            st_ref[0:1, :] += dnw
            st_ref[1:2, :] += jnp.broadcast_to(loss, (1, D))

    return pl.pallas_call(
        body, name="ffn_down_loss", grid=(S // tm, NFG),
        in_specs=[pl.BlockSpec(memory_space=pl.ANY),
                  pl.BlockSpec((None, tm, N_FG), lambda m, p: (p, m, 0)),
                  pl.BlockSpec((None, N_FG, D), lambda m, p: (p, 0, 0)),
                  pl.BlockSpec((1, D), lambda m, p: (0, 0)),
                  pl.BlockSpec(memory_space=pl.ANY)],
        out_specs=[pl.BlockSpec((tm, D), lambda m, p: (m, 0)), pl.BlockSpec((tm, D), lambda m, p: (m, 0)),
                   pl.BlockSpec((8, D), lambda m, p: (0, 0))],
        out_shape=[jax.ShapeDtypeStruct((S, D), F32), jax.ShapeDtypeStruct((S, D), BF16),
                   jax.ShapeDtypeStruct((8, D), F32)],
        scratch_shapes=[pltpu.VMEM((tm, D), F32), pltpu.VMEM((tm, D), F32), pltpu.VMEM((tm, D), F32),
                        pltpu.SemaphoreType.DMA((2,))],
        compiler_params=_cp(("arbitrary", "arbitrary")),
    )(x2, a, wd, nw, tgt)


def _ffn_down_bwd(dx3b, wd, dadg, dadu, part, before=None):
    tm = 1024
    half = NFG // 2

    def body(dx_ref, w_ref, dadg_ref, dadu_ref, *rest):
        dg_ref, du_ref = rest[-2:]
        da = _dot_nt(dx_ref[...], w_ref[...])
        dg_ref[...] = (da * dadg_ref[...].astype(F32)).astype(BF16)
        du_ref[...] = (da * dadu_ref[...].astype(F32)).astype(BF16)

    blk = pl.BlockSpec((None, tm, N_FG), lambda p, m: (p + part * half, m, 0))
    before = list(before or [])
    return pl.pallas_call(
        body, name=f"ffn_down_bwd_{part}", grid=(half, S // tm),
        in_specs=[pl.BlockSpec((tm, D), lambda p, m: (m, 0)),
                  pl.BlockSpec((None, N_FG, D), lambda p, m: (p + part * half, 0, 0)), blk, blk]
        + [pl.BlockSpec(memory_space=pl.ANY)] * len(before),
        out_specs=[blk, blk],
        out_shape=[jax.ShapeDtypeStruct((NFG, S, N_FG), BF16)] * 2,
        input_output_aliases={4 + k: k for k in range(len(before))},
        compiler_params=_cp(("parallel", "parallel")),
    )(dx3b, wd, dadg, dadu, *before)


def _ffn_up_bwd(dg, du, wg, wu, dres, xs, r, nw):
    tm = 512

    def body(dg_ref, du_ref, wg_ref, wu_ref, dres_hbm, x_hbm, r_ref, nw_ref, dx_ref, dxb_ref, st_ref,
             dres_buf, x_buf, sems):
        m, p = pl.program_id(0), pl.program_id(1)
        tail_in = _row_copies((dres_hbm, x_hbm), (dres_buf, x_buf), sems, m, tm)

        @pl.when(p == 0)
        def _():
            dx_ref[...] = jnp.zeros_like(dx_ref)
            for cp in tail_in:
                cp.start()

        @pl.when((p == 0) & (m == 0))
        def _():
            st_ref[...] = jnp.zeros_like(st_ref)

        dx_ref[...] += _dot(dg_ref[...], wg_ref[...])
        dx_ref[...] += _dot(du_ref[...], wu_ref[...])

        @pl.when(p == NFG - 1)
        def _():
            for cp in tail_in:
                cp.wait()
            dx, dnw = _rms_bwd_tile(dx_ref[...], x_buf[...], r_ref[...], nw_ref[...])
            dx = dres_buf[...] + dx
            dx_ref[...] = dx
            dxb_ref[...] = dx.astype(BF16)
            st_ref[0:1, :] += dnw

    blk = pl.BlockSpec((None, tm, N_FG), lambda m, p: (p, m, 0))
    wblk = pl.BlockSpec((None, N_FG, D), lambda m, p: (p, 0, 0))
    row = pl.BlockSpec((tm, D), lambda m, p: (m, 0))
    hbm = pl.BlockSpec(memory_space=pl.ANY)
    return pl.pallas_call(
        body, name="ffn_up_bwd", grid=(S // tm, NFG),
        in_specs=[blk, blk, wblk, wblk, hbm, hbm, pl.BlockSpec((tm, 1), lambda m, p: (m, 0)),
                  pl.BlockSpec((1, D), lambda m, p: (0, 0))],
        out_specs=[row, row, pl.BlockSpec((8, D), lambda m, p: (0, 0))],
        out_shape=[jax.ShapeDtypeStruct((S, D), F32), jax.ShapeDtypeStruct((S, D), BF16),
                   jax.ShapeDtypeStruct((8, D), F32)],
        scratch_shapes=[pltpu.VMEM((tm, D), F32), pltpu.VMEM((tm, D), F32), pltpu.SemaphoreType.DMA((2,))],
        compiler_params=_cp(("arbitrary", "arbitrary")),
    )(dg, du, wg, wu, dres, xs, r, nw)


def _out_proj_bwd(dx2b, wout):
    tm = 256

    def body(dx_ref, w_ref, o_ref):
        o_ref[...] = _dot_nt(dx_ref[...], w_ref[...])

    return pl.pallas_call(
        body, name="out_proj_bwd", grid=(S // tm,),
        in_specs=[pl.BlockSpec((tm, D), lambda i: (i, 0)), pl.BlockSpec((D, D), lambda i: (0, 0))],
        out_specs=pl.BlockSpec((tm, D), lambda i: (i, 0)),
        out_shape=jax.ShapeDtypeStruct((S, D), F32),
        compiler_params=_cp(("parallel",)),
    )(dx2b, wout)


def _in_proj_bwd(dproj, win, dres, xs, r, nw):
    tm = 1024

    def body(dp_ref, w_ref, dres_hbm, x_hbm, r_ref, nw_ref, dx_ref, st_ref, dres_buf, x_buf, sems):
        m, p = pl.program_id(0), pl.program_id(1)
        tail_in = _row_copies((dres_hbm, x_hbm), (dres_buf, x_buf), sems, m, tm)

        @pl.when(p == 0)
        def _():
            dx_ref[...] = jnp.zeros_like(dx_ref)
            for cp in tail_in:
                cp.start()

        @pl.when((p == 0) & (m == 0))
        def _():
            st_ref[...] = jnp.zeros_like(st_ref)

        dx_ref[...] += _dot_nt(dp_ref[...], w_ref[...])

        @pl.when(p == NDEV - 1)
        def _():
            for cp in tail_in:
                cp.wait()
            dx, dnw = _rms_bwd_tile(dx_ref[...], x_buf[...], r_ref[...], nw_ref[...])
            dx_ref[...] = dres_buf[...] + dx
            st_ref[0:1, :] += dnw

    row = pl.BlockSpec((tm, D), lambda m, p: (m, 0))
    hbm = pl.BlockSpec(memory_space=pl.ANY)
    return pl.pallas_call(
        body, name="in_proj_bwd", grid=(S // tm, NDEV),
        in_specs=[pl.BlockSpec((tm, N_IN), lambda m, p: (m, p)),
                  pl.BlockSpec((None, D, N_IN), lambda m, p: (p, 0, 0)),
                  hbm, hbm, pl.BlockSpec((tm, 1), lambda m, p: (m, 0)),
                  pl.BlockSpec((1, D), lambda m, p: (0, 0))],
        out_specs=[row, pl.BlockSpec((8, D), lambda m, p: (0, 0))],
        out_shape=[jax.ShapeDtypeStruct((S, D), F32), jax.ShapeDtypeStruct((8, D), F32)],
        scratch_shapes=[pltpu.VMEM((tm, D), F32), pltpu.VMEM((tm, D), F32), pltpu.SemaphoreType.DMA((2,))],
        compiler_params=_cp(("arbitrary", "arbitrary")),
    )(dproj, win, dres, xs, r, nw)


W_IN_PARTS = 2


def _wgrad_in(h1, dproj, part):
    rows = D // W_IN_PARTS

    def body(a_ref, d_ref, o_ref):
        o_ref[...] = _dot_tn(a_ref[...], d_ref[...]).astype(BF16)

    return pl.pallas_call(
        body, name=f"wgrad_in_{part}", grid=(NDEV,),
        in_specs=[pl.BlockSpec((S, rows), lambda p: (0, part)), pl.BlockSpec((S, N_IN), lambda p: (0, p))],
        out_specs=pl.BlockSpec((None, rows, N_IN), lambda p: (p, 0, 0)),
        out_shape=jax.ShapeDtypeStruct((NDEV, rows, N_IN), BF16),
        compiler_params=_cp(("parallel",)),
    )(h1, dproj)


def _wgrad_rows(a3, dy, name):
    def body(a_ref, d_ref, o_ref):
        o_ref[...] = _dot_tn(a_ref[...], d_ref[...]).astype(BF16)

    return pl.pallas_call(
        body, name=name, grid=(NFG,),
        in_specs=[pl.BlockSpec((None, S, N_FG), lambda p: (p, 0, 0)), pl.BlockSpec((S, D), lambda p: (0, 0))],
        out_specs=pl.BlockSpec((None, N_FG, D), lambda p: (p, 0, 0)),
        out_shape=jax.ShapeDtypeStruct((NFG, N_FG, D), BF16),
        compiler_params=_cp(("parallel",)),
    )(a3, dy).reshape(NDEV, N_FF, D)


def _wgrad_out(ma, mr, dx2b):
    half = D // 2
    per = half // N_OUT

    def body(ma_ref, mr_ref, d_ref, o_ref):
        p = pl.program_id(0)

        @pl.when(p < per)
        def _():
            o_ref[...] = _dot_tn(ma_ref[...], d_ref[...]).astype(BF16)

        @pl.when(p >= per)
        def _():
            o_ref[...] = _dot_tn(mr_ref[...], d_ref[...]).astype(BF16)

    return pl.pallas_call(
        body, name="wgrad_out", grid=(NDEV,),
        in_specs=[pl.BlockSpec((S, N_OUT), lambda p: (0, jnp.minimum(p, per - 1))),
                  pl.BlockSpec((S, N_OUT), lambda p: (0, jnp.maximum(p - per, 0))),
                  pl.BlockSpec((S, D), lambda p: (0, 0))],
        out_specs=pl.BlockSpec((None, N_OUT, D), lambda p: (p, 0, 0)),
        out_shape=jax.ShapeDtypeStruct((NDEV, N_OUT, D), BF16),
        compiler_params=_cp(("parallel",)),
    )(ma, mr, dx2b)


def _attn_consts():
    c = np.zeros((AH, 8, AHD), np.float32)
    for h in range(AH):
        c[h, :, :] = 2.0 ** (-(h + 1))
    return jnp.asarray(c)


def _permute_in(dst, src, d, cast=None):
    v = src[...]
    if d > 1:
        v = pltpu.einshape("jrc->rjc", v.reshape(S // d, d, AHD)).reshape(S, AHD)
    dst[...] = v if cast is None else v.astype(cast)


def _natural_order(v, d):
    if d == 1:
        return v
    return pltpu.einshape("rjc->jrc", v.reshape(d, S // d, AHD)).reshape(S, AHD)


def _attn_masks():
    qi = lax.broadcasted_iota(jnp.int32, (CH, CH), 0)
    kj = lax.broadcasted_iota(jnp.int32, (CH, CH), 1)
    dist_c = (qi - kj).astype(F32)
    dist_p = (qi - kj + CH).astype(F32)
    return (qi >= kj)[None], (kj >= qi)[None], dist_c[None], dist_p[None]


GB = 16


def _bdot_nt(a, b):
    return lax.dot_general(a, b, (((2,), (2,)), ((0,), (0,))), preferred_element_type=F32)


def _bdot(a, b):
    return lax.dot_general(a, b, (((2,), (1,)), ((0,), (0,))), preferred_element_type=F32)


def _bdot_tn(a, b):
    return lax.dot_general(a, b, (((1,), (1,)), ((0,), (0,))), preferred_element_type=F32)


def _shift_block(dst, src):
    dst[0:CH, :] = jnp.zeros((CH, AHD), dst.dtype)
    dst[CH:S, :] = src[0:S - CH, :]


def _has_prev(g, nb):
    blk = lax.broadcasted_iota(jnp.int32, (GB, 1, 1), 0) + g * GB
    return (blk & (nb - 1)) != 0


def _blocks(ref, g):
    return ref[g * GB * CH:(g + 1) * GB * CH, :].reshape(GB, CH, AHD)


def _attn_fwd(proj):
    scale = 1.0 / math.sqrt(AHD)

    def body(c_ref, q_ref, k_ref, v_ref, o_ref, ob_ref, lse_ref, qkvp_ref, lsep_ref, qd, kd, vd, kps, vps, od, ld, *nat):
        onat, lnat = nat[0:3], nat[3:6]
        slope = c_ref[0:1, :]
        mask_c, mask_p, dist_c, dist_p = _attn_masks()
        for pi, (d, nb) in enumerate(PATTERNS):
            _permute_in(qd, q_ref, d, BF16)
            _permute_in(kd, k_ref, d, BF16)
            _permute_in(vd, v_ref, d, BF16)
            if d > 1:
                qkvp_ref[pi - 1, 0] = qd[...]
                qkvp_ref[pi - 1, 1] = kd[...]
                qkvp_ref[pi - 1, 2] = vd[...]
            if nb > 1:
                _shift_block(kps, kd)
                _shift_block(vps, vd)
            bias_c = -(slope * float(d)) * dist_c
            bias_p = -(slope * float(d)) * dist_p
            for g in range(NB // GB):
                q3, k3, v3 = _blocks(qd, g), _blocks(kd, g), _blocks(vd, g)
                s_c = jnp.where(mask_c, _bdot_nt(q3, k3) * scale + bias_c, NEG)
                mx = jnp.max(s_c, axis=-1, keepdims=True)
                if nb > 1:
                    kp3, vp3 = _blocks(kps, g), _blocks(vps, g)
                    s_p = jnp.where(jnp.logical_and(mask_p, _has_prev(g, nb)),
                                    _bdot_nt(q3, kp3) * scale + bias_p, NEG)
                    mx = jnp.maximum(mx, jnp.max(s_p, axis=-1, keepdims=True))
                    l = (jnp.sum(jnp.exp(s_c - mx), axis=-1, keepdims=True)
                         + jnp.sum(jnp.exp(s_p - mx), axis=-1, keepdims=True))
                    lse = mx + jnp.log(l)
                    o3 = _bdot(jnp.exp(s_c - lse).astype(BF16), v3) + _bdot(jnp.exp(s_p - lse).astype(BF16), vp3)
                else:
                    l = jnp.sum(jnp.exp(s_c - mx), axis=-1, keepdims=True)
                    lse = mx + jnp.log(l)
                    o3 = _bdot(jnp.exp(s_c - lse).astype(BF16), v3)
                rows = slice(g * GB * CH, (g + 1) * GB * CH)
                od[rows, :] = o3.reshape(GB * CH, AHD)
                ld[rows, :] = jnp.broadcast_to(lse, (GB, CH, AHD)).reshape(GB * CH, AHD)
            onat[pi][...] = _natural_order(od[...], d)
            lnat[pi][...] = _natural_order(ld[...], d)
        l0, l1, l2 = lnat[0][...], lnat[1][...], lnat[2][...]
        mx = jnp.maximum(jnp.maximum(l0, l1), l2)
        e0, e1, e2 = jnp.exp(l0 - mx), jnp.exp(l1 - mx), jnp.exp(l2 - mx)
        den = e0 + e1 + e2
        out = (e0 / den) * onat[0][...] + (e1 / den) * onat[1][...] + (e2 / den) * onat[2][...]
        o_ref[...] = out
        ob_ref[...] = out.astype(BF16)
        lse_ref[...] = mx + jnp.log(den)
        for pi, (d, _) in enumerate(PATTERNS[1:]):
            _permute_in(lsep_ref.at[pi], lse_ref, d)

    def col(off):
        return pl.BlockSpec((S, AHD), lambda h: (0, off + h))

    return pl.pallas_call(
        body, name="attn_fwd", grid=(AH,),
        in_specs=[pl.BlockSpec((None, 8, AHD), lambda h: (h, 0, 0)), col(0), col(AH), col(2 * AH)],
        out_specs=[col(0), col(0), col(0), pl.BlockSpec((2, 3, S, AHD), lambda h: (0, 0, 0, h)),
                   pl.BlockSpec((2, S, AHD), lambda h: (0, 0, h))],
        out_shape=[jax.ShapeDtypeStruct((S, AH * AHD), F32), jax.ShapeDtypeStruct((S, AH * AHD), BF16),
                   jax.ShapeDtypeStruct((S, AH * AHD), F32),
                   jax.ShapeDtypeStruct((2, 3, S, AH * AHD), BF16), jax.ShapeDtypeStruct((2, S, AH * AHD), F32)],
        scratch_shapes=[pltpu.VMEM((S, AHD), BF16) for _ in range(5)]
        + [pltpu.VMEM((S, AHD), F32) for _ in range(8)],
        compiler_params=_cp(("parallel",)),
    )(_attn_consts(), proj, proj, proj)


def _attn_bwd(proj, dmixed, o, lse, qkvp, lsep):
    scale = 1.0 / math.sqrt(AHD)

    def body(c_ref, q_ref, k_ref, v_ref, do_ref, o_ref, lse_ref, qkvp_ref, lsep_ref, dproj_hbm,
             qd, kd, vd, dod, kps, vps, dld, dqd, dkd, dvd, delta, aq, ak, av, sq, sk, sv, sems):
        h = pl.program_id(0)

        def out_copies(head):
            return [pltpu.make_async_copy(
                st, dproj_hbm.at[:, pl.ds(pl.multiple_of((k * AH + head) * AHD, AHD), AHD)], sems.at[k])
                for k, st in enumerate((sq, sk, sv))]

        slope = c_ref[0:1, :]
        mask_c, mask_p, dist_c, dist_p = _attn_masks()
        delta[...] = jnp.broadcast_to(jnp.sum(do_ref[...] * o_ref[...], axis=-1, keepdims=True), (S, AHD))
        for pi, (d, nb) in enumerate(PATTERNS):
            if d == 1:
                _permute_in(qd, q_ref, d, BF16)
                _permute_in(kd, k_ref, d, BF16)
                _permute_in(vd, v_ref, d, BF16)
                qs, ks, vs, lss = qd, kd, vd, lse_ref
            else:
                qs, ks, vs, lss = (qkvp_ref.at[pi - 1, 0], qkvp_ref.at[pi - 1, 1], qkvp_ref.at[pi - 1, 2],
                                   lsep_ref.at[pi - 1])
            _permute_in(dod, do_ref, d, BF16)
            _permute_in(dld, delta, d)
            if nb > 1:
                _shift_block(kps, ks)
                _shift_block(vps, vs)
            bias_c = -(slope * float(d)) * dist_c
            bias_p = -(slope * float(d)) * dist_p
            for g in range(NB // GB):
                q3, k3, v3, do3 = _blocks(qs, g), _blocks(ks, g), _blocks(vs, g), _blocks(dod, g)
                ls, dl = _blocks(lss, g), _blocks(dld, g)
                lo, hi = g * GB * CH, (g + 1) * GB * CH
                p_c = jnp.exp(jnp.where(mask_c, _bdot_nt(q3, k3) * scale + bias_c, NEG) - ls)
                ds_c = ((p_c * (_bdot_nt(do3, v3) - dl)) * scale).astype(BF16)
                dq3 = _bdot(ds_c, k3)
                dkd[lo:hi, :] = _bdot_tn(ds_c, q3).reshape(GB * CH, AHD)
                dvd[lo:hi, :] = _bdot_tn(p_c.astype(BF16), do3).reshape(GB * CH, AHD)
                if nb > 1:
                    kp3, vp3 = _blocks(kps, g), _blocks(vps, g)
                    p_p = jnp.exp(jnp.where(jnp.logical_and(mask_p, _has_prev(g, nb)),
                                            _bdot_nt(q3, kp3) * scale + bias_p, NEG) - ls)
                    ds_p = ((p_p * (_bdot_nt(do3, vp3) - dl)) * scale).astype(BF16)
                    dq3 = dq3 + _bdot(ds_p, kp3)
                    dkp = _bdot_tn(ds_p, q3).reshape(GB * CH, AHD)
                    dvp = _bdot_tn(p_p.astype(BF16), do3).reshape(GB * CH, AHD)
                    if g == 0:
                        dkd[0:hi - CH, :] += dkp[CH:, :]
                        dvd[0:hi - CH, :] += dvp[CH:, :]
                    else:
                        dkd[lo - CH:hi - CH, :] += dkp
                        dvd[lo - CH:hi - CH, :] += dvp
                dqd[lo:hi, :] = dq3.reshape(GB * CH, AHD)
            ln = S // d
            for acc, src in ((aq, dqd), (ak, dkd), (av, dvd)):
                if pi == 0:
                    acc[...] = src[...]
                else:
                    acc[...] += _natural_order(src[...], d)

        @pl.when(h > 0)
        def _():
            for cp in out_copies(h - 1):
                cp.wait()

        sq[...] = aq[...].astype(BF16)
        sk[...] = ak[...].astype(BF16)
        sv[...] = av[...].astype(BF16)
        for cp in out_copies(h):
            cp.start()

        @pl.when(h == AH - 1)
        def _():
            for cp in out_copies(h):
                cp.wait()

    def col(off):
        return pl.BlockSpec((S, AHD), lambda h: (0, off + h))

    return pl.pallas_call(
        body, name="attn_bwd", grid=(AH,),
        in_specs=[pl.BlockSpec((None, 8, AHD), lambda h: (h, 0, 0)), col(0), col(AH), col(2 * AH),
                  col(0), col(0), col(0), pl.BlockSpec((2, 3, S, AHD), lambda h: (0, 0, 0, h)),
                  pl.BlockSpec((2, S, AHD), lambda h: (0, 0, h))],
        out_specs=pl.BlockSpec(memory_space=pl.ANY),
        out_shape=jax.ShapeDtypeStruct((S, NDEV * N_IN), BF16),
        scratch_shapes=[pltpu.VMEM((S, AHD), BF16) for _ in range(6)]
        + [pltpu.VMEM((S, AHD), F32) for _ in range(8)]
        + [pltpu.VMEM((S, AHD), BF16) for _ in range(3)] + [pltpu.SemaphoreType.DMA((3,))],
        compiler_params=_cp(("arbitrary",)),
    )(_attn_consts(), proj, proj, proj, dmixed, o, lse, qkvp, lsep)


def _ret_consts():
    c = np.zeros((RH, 8, RHD), np.float32)
    for h in range(RH):
        c[h, :, :] = np.log(np.float32(1.0) - np.float32(2.0 ** (-5.0 - h)))
    return jnp.asarray(c)


def _ret_factors(lg):
    i = lax.broadcasted_iota(jnp.int32, (CH, CH), 0)
    j = lax.broadcasted_iota(jnp.int32, (CH, CH), 1)
    dif = (i - j).astype(F32)
    decay = jnp.where(dif >= 0, jnp.exp(lg[:, 0:CH] * jnp.maximum(dif, 0.0)), 0.0)
    row = lax.broadcasted_iota(jnp.int32, (CH, RHD), 0).astype(F32)
    zeta = jnp.exp(lg * (CH - 1.0 - row))
    xi = jnp.exp(lg * (row + 1.0))
    return decay, zeta, xi, jnp.exp(lg * float(CH))


CBK = 8
RSTEPS = NB // CBK


def _ret_specs(rev):
    off = 3 * AH * AHD // RHD
    rows = CBK * CH

    def ch(n):
        return (RSTEPS - 1 - n) if rev else n

    def col(k):
        return pl.BlockSpec((rows, RHD), lambda h, n: (ch(n), off + k * RH + h))

    own = pl.BlockSpec((rows, RHD), lambda h, n: (ch(n), h))
    state = pl.BlockSpec((None, CBK, RHD, RHD), lambda h, n: (h, ch(n), 0, 0))
    const = pl.BlockSpec((None, 8, RHD), lambda h, n: (h, 0, 0))
    dm = pl.BlockSpec((rows, RHD), lambda h, n: (ch(n), AH * AHD // RHD + h))
    return col, own, state, const, dm


def _chunks(x):
    return x.reshape(CBK, CH, RHD)


def _ret_fwd(proj):
    def body(c_ref, q_ref, k_ref, v_ref, g_ref, ret_ref, mr_ref, st_ref, r_acc):
        n = pl.program_id(1)

        @pl.when(n == 0)
        def _():
            r_acc[...] = jnp.zeros_like(r_acc)

        decay, zeta, xi, gch = _ret_factors(c_ref[0:1, :])
        q3 = _chunks(q_ref[...].astype(BF16))
        kc = _chunks(k_ref[...] * (1.0 / math.sqrt(RHD)))
        k3 = kc.astype(BF16)
        v3 = _chunks(v_ref[...].astype(BF16))
        kv3 = _bdot_tn((kc * zeta[None]).astype(BF16), v3)
        r = r_acc[...]
        for i in range(CBK):
            st_ref[i] = r.astype(BF16)
            r = r * gch + kv3[i]
        r_acc[...] = r
        scores = _bdot_nt(q3, k3) * decay[None]
        ret = (_bdot(scores.astype(BF16), v3) + _bdot(q3, st_ref[...]) * xi[None]).reshape(CBK * CH, RHD)
        ret_ref[...] = ret
        rr = lax.rsqrt(jnp.mean(ret * ret, axis=-1, keepdims=True) + EPS)
        gv = g_ref[...]
        mr_ref[...] = ((gv * _sigmoid(gv)) * (ret * rr)).astype(BF16)

    col, own, state, const, _ = _ret_specs(False)
    return pl.pallas_call(
        body, name="ret_fwd", grid=(RH, RSTEPS),
        in_specs=[const, col(0), col(1), col(2), col(3)],
        out_specs=[own, own, state],
        out_shape=[jax.ShapeDtypeStruct((S, RH * RHD), F32), jax.ShapeDtypeStruct((S, RH * RHD), BF16),
                   jax.ShapeDtypeStruct((RH, NB, RHD, RHD), BF16)],
        scratch_shapes=[pltpu.VMEM((RHD, RHD), F32)],
        compiler_params=_cp(("parallel", "arbitrary")),
    )(_ret_consts(), proj, proj, proj, proj)


def _ret_bwd(proj, ret, states, dmixed, dproj):
    rows = CBK * CH
    col0 = 3 * AH * AHD

    def body(c_ref, q_ref, k_ref, v_ref, g_ref, ret_ref, st_ref, dm_ref, dproj_in, dproj_hbm, g_acc, gs,
             sq, sk, sv, sg, sems):
        del dproj_in
        h, n = pl.program_id(0), pl.program_id(1)
        step = h * RSTEPS + n

        def out_copies(t):
            hh, nn = t // RSTEPS, t % RSTEPS
            r0 = pl.multiple_of((RSTEPS - 1 - nn) * rows, rows)
            return [pltpu.make_async_copy(
                st, dproj_hbm.at[pl.ds(r0, rows), pl.ds(pl.multiple_of(col0 + (k * RH + hh) * RHD, RHD), RHD)],
                sems.at[k]) for k, st in enumerate((sq, sk, sv, sg))]

        @pl.when(n == 0)
        def _():
            g_acc[...] = jnp.zeros_like(g_acc)

        decay, zeta, xi, gch = _ret_factors(c_ref[0:1, :])
        ret_v = ret_ref[...]
        rr = lax.rsqrt(jnp.mean(ret_v * ret_v, axis=-1, keepdims=True) + EPS)
        gv = g_ref[...]
        sgm = _sigmoid(gv)
        dmix = dm_ref[...]
        dgate = ((dmix * (ret_v * rr)) * (sgm * (1.0 + gv * (1.0 - sgm)))).astype(BF16)
        dretn = dmix * (gv * sgm)
        dret = _chunks(rr * dretn - ret_v * ((rr * rr * rr) * jnp.mean(dretn * ret_v, axis=-1, keepdims=True)))

        q3 = _chunks(q_ref[...].astype(BF16))
        kc = _chunks(k_ref[...] * (1.0 / math.sqrt(RHD)))
        k3 = kc.astype(BF16)
        v3 = _chunks(v_ref[...].astype(BF16))
        d3 = dret.astype(BF16)
        dxi = (dret * xi[None]).astype(BF16)
        kz = (kc * zeta[None]).astype(BF16)
        dr3 = _bdot_tn(q3, dxi)
        acc = g_acc[...]
        for i in reversed(range(CBK)):
            gs[i] = acc.astype(BF16)
            acc = dr3[i] + gch * acc
        g_acc[...] = acc
        g3 = gs[...]
        sc = (_bdot_nt(q3, k3) * decay[None]).astype(BF16)
        da = (_bdot_nt(d3, v3) * decay[None]).astype(BF16)
        dq = _bdot(da, k3) + _bdot_nt(dxi, st_ref[...])
        dkc = _bdot_tn(da, q3) + _bdot_nt(v3, g3) * zeta[None]
        dv = _bdot_tn(sc, d3) + _bdot(kz, g3)

        @pl.when(step > 0)
        def _():
            for cp in out_copies(step - 1):
                cp.wait()

        sq[...] = dq.reshape(rows, RHD).astype(BF16)
        sk[...] = (dkc * (1.0 / math.sqrt(RHD))).reshape(rows, RHD).astype(BF16)
        sv[...] = dv.reshape(rows, RHD).astype(BF16)
        sg[...] = dgate
        for cp in out_copies(step):
            cp.start()

        @pl.when(step == RH * RSTEPS - 1)
        def _():
            for cp in out_copies(step):
                cp.wait()

    col, own, state, const, dm = _ret_specs(True)
    hbm = pl.BlockSpec(memory_space=pl.ANY)
    return pl.pallas_call(
        body, name="ret_bwd", grid=(RH, RSTEPS),
        in_specs=[const, col(0), col(1), col(2), col(3), own, state, dm, hbm],
        out_specs=hbm,
        out_shape=jax.ShapeDtypeStruct(dproj.shape, dproj.dtype),
        input_output_aliases={8: 0},
        scratch_shapes=[pltpu.VMEM((RHD, RHD), F32), pltpu.VMEM((CBK, RHD, RHD), BF16)]
        + [pltpu.VMEM((rows, RHD), BF16) for _ in range(4)] + [pltpu.SemaphoreType.DMA((4,))],
        compiler_params=_cp(("arbitrary", "arbitrary")),
    )(_ret_consts(), proj, proj, proj, proj, ret, states, dmixed, dproj)


class _NoReduction:
    def start(self, group, grads):
        pass

    def local(self, name, first=()):
        return []

    def landed(self, name):
        return []

    def update(self, name):
        return []


def _local_step(x, tgt, nw1, nw2, nw3, win, wout, wg, wu, wd, red=None):
    red = red or _NoReduction()

    def after(values, first):
        return lax.optimization_barrier((tuple(values), tuple(first)))[0]

    wg, wu, wd = (w.reshape(NFG, N_FG, D) for w in (wg, wu, wd))
    h1, r1 = _rms_fwd(x, nw1)
    proj = _proj(h1, win)
    o, ma, lse, qkvp, lsep = _attn_fwd(proj)
    ret, mr, states = _ret_fwd(proj)
    x2, h2, r2 = _out_proj_rms(x, ma, mr, wout, nw2)
    a, dadg, dadu = _ffn_up(h2, wg, wu)
    dx3, dx3b, st3 = _ffn_down_loss(x2, a, wd, nw3, tgt)

    dwd = _wgrad_rows(a, dx3b, "wgrad_down")
    red.start(["w_down"], [dwd])
    (dx3b,) = after([dx3b], [dwd])
    part = _ffn_down_bwd(dx3b, wd, dadg, dadu, 0)
    (dx3b,) = after([dx3b], red.local("w_down", first=part))
    dg, du = _ffn_down_bwd(dx3b, wd, dadg, dadu, 1, part)
    dwg = _wgrad_rows(dg, h2, "wgrad_gate")
    red.start(["w_gate"], [dwg])
    (du,) = after([du], [dwg])
    dwu = _wgrad_rows(du, h2, "wgrad_up")
    red.start(["w_up"], [dwu])
    dg, du = after([dg, du], red.local("w_gate", first=[dwu] + red.landed("w_down")))
    dx2, dx2b, st2 = _ffn_up_bwd(dg, du, wg, wu, dx3, x2, r2, nw2)
    (dx2b,) = after([dx2b], red.local("w_up", first=[dx2b]))
    dwo = _wgrad_out(ma, mr, dx2b)
    red.start(["w_out"], [dwo])
    (dx2b,) = after([dx2b], [dwo])
    dmixed = _out_proj_bwd(dx2b, wout)
    dproj = _attn_bwd(proj, dmixed, o, lse, qkvp, lsep)
    (dmixed,) = after([dmixed], red.local("w_out", first=[dproj] + red.landed("w_gate")))
    dproj = _ret_bwd(proj, ret, states, dmixed, dproj)
    (dwi0,) = after([_wgrad_in(h1, dproj, 0)], red.landed("w_up"))
    red.start(["w_in_0"], [dwi0])
    (dproj,) = after([dproj], [dwi0])
    dwi1 = _wgrad_in(h1, dproj, 1)
    red.start(["w_in_1"], [dwi1])
    sums = red.local("w_in_0", first=[dwi1] + red.landed("w_out"))
    sums = red.local("w_in_1", first=sums + red.update("w_down"))
    (dproj,) = after([dproj], sums)
    gx, st1 = _in_proj_bwd(dproj, win, dx2, x, r1, nw1)
    dwi = jnp.concatenate([dwi0, dwi1], axis=1)
    stats = jnp.concatenate([st1[0:1], st2[0:1], st3[0:2], jnp.zeros((4, D), F32)], axis=0)
    return stats, gx, dwi, dwo, dwg, dwu, dwd


def _place():
    x, y, c = lax.axis_index("x"), lax.axis_index("y"), lax.axis_index("c")
    return x, y, c, [(1 - x, y), (x, 1 - y), (1 - x, 1 - y)]


def _handshake(peers):
    barrier = pltpu.get_barrier_semaphore()
    for peer in peers:
        pl.semaphore_signal(barrier, inc=1, device_id=peer, device_id_type=MESH)
    pl.semaphore_wait(barrier, len(peers))


def _all_gather(shards, name, collective_id):
    na = len(shards)
    SIB, XN0, XN1, YN1, YN0, VIA_X, VIA_Y = 0, 1, 2, 3, 4, 5, 6
    D2D = {XN0: 7, XN1: 8, YN1: 9, YN0: 10, VIA_X: 11, VIA_Y: 12}

    def body(*refs):
        ins, outs = refs[:na], refs[na:2 * na]
        send_sems, recv_sems, local_sems = refs[2 * na:]
        x, y, c, _ = _place()
        me, sib = (x, y, c), (x, y, 1 - c)
        xn, yn, dg = (1 - x, y, c), (x, 1 - y, c), (1 - x, 1 - y, c)
        _handshake([sib, xn, yn])

        def part(ref, h):
            rows = ref.shape[0] // 2
            return ref if h is None else ref.at[pl.ds(h * rows, rows)]

        def block(a, owner, h):
            return part(outs[a].at[4 * owner[0] + 2 * owner[1] + owner[2]], h)

        def copy(a, k, owner, h, to, own_src=False):
            return pltpu.make_async_remote_copy(
                src_ref=part(ins[a], h) if own_src else block(a, owner, h), dst_ref=block(a, owner, h),
                send_sem=send_sems.at[a, k], recv_sem=recv_sems.at[a, k], device_id=to, device_id_type=MESH)

        def other(p):
            return (p[0], p[1], 1 - c)

        mine = [pltpu.make_async_copy(ins[a], block(a, me, None), local_sems.at[a]) for a in range(na)]
        for cp in mine:
            cp.start()
        sent = []
        for a in range(na):
            sent += [copy(a, XN0, me, 0, xn, True), copy(a, YN1, me, 1, yn, True),
                     copy(a, XN1, me, 1, xn, True), copy(a, YN0, me, 0, yn, True)]
        sent += [copy(a, SIB, me, None, sib, True) for a in range(na)]
        for cp in sent:
            cp.start()

        def landed(a, k, owner, h, then):
            copy(a, k, owner, h, me).wait_recv()
            for k2, to in then + [(D2D[k], sib)]:
                cp = copy(a, k2, owner, h, to)
                cp.start()
                sent.append(cp)

        for a in range(na):
            landed(a, XN0, xn, 0, [(VIA_Y, yn)])
            landed(a, YN1, yn, 1, [(VIA_X, xn)])
            landed(a, XN1, xn, 1, [])
            landed(a, YN0, yn, 0, [])
        for a in range(na):
            landed(a, VIA_Y, dg, 0, [])
            landed(a, VIA_X, dg, 1, [])
        for a in range(na):
            copy(a, SIB, sib, None, me).wait_recv()
            for k, owner, h in ((XN0, xn, 0), (XN1, xn, 1), (YN1, yn, 1), (YN0, yn, 0), (VIA_Y, dg, 0), (VIA_X, dg, 1)):
                copy(a, D2D[k], other(owner), h, me).wait_recv()
        for cp in sent:
            cp.wait_send()
        for cp in mine:
            cp.wait()

    return _sequencer_call(
        body, name, collective_id,
        [jax.ShapeDtypeStruct((NDEV,) + s.shape, s.dtype) for s in shards],
        [pltpu.SemaphoreType.DMA((na, 13)), pltpu.SemaphoreType.DMA((na, 13)), pltpu.SemaphoreType.DMA((na,))])(*shards)


def _sequencer_call(body, name, collective_id, out_type, scratch_types):
    return pl.kernel(
        body, name=name, out_type=out_type,
        mesh=plsc.ScalarSubcoreMesh(axis_name="sequencer", num_cores=1),
        scratch_types=scratch_types,
        compiler_params=pltpu.CompilerParams(collective_id=collective_id))


def _exchange_sibling(grads, name, collective_id):
    na = len(grads)

    def body(*refs):
        ins, outs = refs[:na], refs[na:2 * na]
        send_sems, recv_sems = refs[2 * na:]
        x, y, c, _ = _place()
        _handshake([(x, y, 1 - c)])
        cps = []
        for a in range(na):
            for k in range(4):
                cps.append(pltpu.make_async_remote_copy(
                    src_ref=ins[a].at[2 * k + (1 - c)], dst_ref=outs[a].at[k],
                    send_sem=send_sems.at[a, k], recv_sem=recv_sems.at[a, k],
                    device_id=(x, y, 1 - c), device_id_type=MESH))
        for cp in cps:
            cp.start()
        for cp in cps:
            cp.wait()

    return _sequencer_call(
        body, name, collective_id,
        [jax.ShapeDtypeStruct((4,) + g.shape[1:], g.dtype) for g in grads],
        [pltpu.SemaphoreType.DMA((na, 4)), pltpu.SemaphoreType.DMA((na, 4))])(*grads)


def _row_tile(rows, cols):
    for t in (512, 256, 176, 128, 64, 32, 16):
        if rows % t == 0 and t * cols * 4 <= (2 << 20):
            return t
    raise ValueError((rows, cols))


def _chip_sum(place, g, got, name):
    _, r, c = g.shape
    tm = r

    def body(pos_ref, g_ref, got_ref, o_ref):
        o_ref[...] = (g_ref[...].astype(F32) + got_ref[...].astype(F32)).astype(BF16)

    def chip(j, pos):
        return 2 * (pos[0] ^ jnp.where(j == 1, 0, 1)) + (pos[1] ^ jnp.where(j == 0, 0, 1))

    return pl.pallas_call(
        body, name=name,
        grid_spec=pltpu.PrefetchScalarGridSpec(
            num_scalar_prefetch=1, grid=(3, r // tm),
            in_specs=[pl.BlockSpec((None, tm, c), lambda j, i, pos: (2 * chip(j, pos) + pos[2], i, 0)),
                      pl.BlockSpec((None, tm, c), lambda j, i, pos: (chip(j, pos), i, 0))],
            out_specs=pl.BlockSpec((None, tm, c), lambda j, i, pos: (j, i, 0))),
        out_shape=jax.ShapeDtypeStruct((3, r, c), BF16),
        compiler_params=_cp(("parallel", "parallel")),
    )(place, g, got)


def _exchange_chips(sums, name, collective_id):
    na = len(sums)

    def body(*refs):
        ins, outs = refs[:na], refs[na:2 * na]
        send_sems, recv_sems = refs[2 * na:]
        x, y, c, chips = _place()
        _handshake([(*chip, c) for chip in chips])
        cps = []
        for a in range(na):
            for j, chip in enumerate(chips):
                cps.append(pltpu.make_async_remote_copy(
                    src_ref=ins[a].at[j], dst_ref=outs[a].at[j],
                    send_sem=send_sems.at[a, j], recv_sem=recv_sems.at[a, j],
                    device_id=(*chip, c), device_id_type=MESH))
        for cp in cps:
            cp.start()
        for cp in cps:
            cp.wait()

    return _sequencer_call(
        body, name, collective_id,
        [jax.ShapeDtypeStruct((3,) + s.shape[1:], s.dtype) for s in sums],
        [pltpu.SemaphoreType.DMA((na, 3)), pltpu.SemaphoreType.DMA((na, 3))])(*sums)


def _exchange_stats(stats, collective_id):
    def body(st_in, st_out, st_send, st_recv, local_sem):
        x, y, c, _ = _place()
        me_idx = 4 * x + 2 * y + c
        peers = [(x ^ ((k >> 2) & 1), y ^ ((k >> 1) & 1), c ^ (k & 1)) for k in range(1, 8)]
        _handshake(peers)
        mine = pltpu.make_async_copy(st_in, st_out.at[me_idx], local_sem)
        mine.start()
        cps = [pltpu.make_async_remote_copy(
            src_ref=st_in, dst_ref=st_out.at[me_idx], send_sem=st_send.at[k], recv_sem=st_recv.at[k],
            device_id=peer, device_id_type=MESH) for k, peer in enumerate(peers)]
        for cp in cps:
            cp.start()
        for cp in cps:
            cp.wait()
        mine.wait()

    return _sequencer_call(
        body, "exchange_stats", collective_id,
        jax.ShapeDtypeStruct((NDEV,) + stats.shape, stats.dtype),
        [pltpu.SemaphoreType.DMA((7,)), pltpu.SemaphoreType.DMA((7,)), pltpu.SemaphoreType.DMA])(stats)


class _Reduction:
    def __init__(self, place, first_collective_id, state):
        self.place = place
        self.ids = iter(range(first_collective_id, 32))
        self.state = state
        self.groups = {}
        self.updates = {}

    def next_id(self):
        return next(self.ids)

    def start(self, group, grads):
        got = _exchange_sibling(grads, "sibling_exchange_" + group[0], self.next_id())
        self.groups[group[0]] = dict(names=group, grads=grads, got=got)

    def local(self, name, first=()):
        grp = self.groups[name]
        grads = lax.optimization_barrier((tuple(grp["grads"]), tuple(first)))[0]
        grp["sums"] = [_chip_sum(self.place, g, s, "chip_sum_" + n)
                       for g, s, n in zip(grads, grp["got"], grp["names"])]
        grp["chips"] = _exchange_chips(grp["sums"], "chip_exchange_" + name, self.next_id())
        return grp["sums"]

    def landed(self, name):
        return list(self.groups[name]["chips"])

    def update(self, name):
        if name not in self.updates:
            grp = next(g for g in self.groups.values() if name in g["names"])
            k = grp["names"].index(name)
            w, m, v, part, parts = self.state[name]
            before = self.update(f"{name[:-1]}{part - 1}") if part else None
            self.updates[name] = _shard_update(self.place, w, m, v, grp["grads"][k], grp["got"][k],
                                               grp["chips"][k], "update_" + name, part, parts, before)
        return list(self.updates[name])


def _adamw(w, g, m, v):
    m = ADAM_B1 * m + (1.0 - ADAM_B1) * g
    v = ADAM_B2 * v + (1.0 - ADAM_B2) * (g * g)
    m_hat = m / (1.0 - ADAM_B1 ** ADAM_STEP)
    v_hat = v / (1.0 - ADAM_B2 ** ADAM_STEP)
    delta = -ADAM_LR * (m_hat / (jnp.sqrt(v_hat) + ADAM_EPS) + ADAM_WD * w)
    return delta, m, v


def _shard_update(place, w, m, v, g, got_sib, got_chips, name, part=0, parts=1, before=None):
    r, c = w.shape
    rp = r // parts
    tm = _row_tile(rp, c)
    off = part * (rp // tm)

    def body(pos_ref, w_ref, m_ref, v_ref, g_ref, s_ref, c_ref, *rest):
        go_ref, d_ref, mo_ref, vo_ref = rest[-4:]
        grad = g_ref[...].astype(F32) + s_ref[...].astype(F32)
        for j in range(3):
            grad = grad + c_ref[j].astype(F32)
        delta, mn, vn = _adamw(w_ref[...], grad, m_ref[...], v_ref[...])
        go_ref[...] = grad
        d_ref[...] = delta
        mo_ref[...] = mn
        vo_ref[...] = vn

    row = pl.BlockSpec((tm, c), lambda i, pos: (i + off, 0))
    before = list(before or [])
    return pl.pallas_call(
        body, name=name,
        grid_spec=pltpu.PrefetchScalarGridSpec(
            num_scalar_prefetch=1, grid=(rp // tm,),
            in_specs=[row, row, row,
                      pl.BlockSpec((None, tm, c), lambda i, pos: (4 * pos[0] + 2 * pos[1] + pos[2], i, 0)),
                      pl.BlockSpec((None, tm, c), lambda i, pos: (2 * pos[0] + pos[1], i, 0)),
                      pl.BlockSpec((3, tm, c), lambda i, pos: (0, i, 0))]
            + [pl.BlockSpec(memory_space=pl.ANY)] * len(before),
            out_specs=[row, row, row, row]),
        out_shape=[jax.ShapeDtypeStruct((r, c), F32)] * 4,
        input_output_aliases={7 + k: k for k in range(len(before))},
        compiler_params=_cp(("parallel",)),
    )(place, w, m, v, g, got_sib, got_chips, *before)


def _small_update(stats_all, ws, ms, vs):
    def body(st_ref, w_ref, m_ref, v_ref, go_ref, d_ref, mo_ref, vo_ref):
        grad = st_ref[0]
        for k in range(1, NDEV):
            grad = grad + st_ref[k]
        delta, mn, vn = _adamw(w_ref[...], grad, m_ref[...], v_ref[...])
        go_ref[...] = grad
        d_ref[...] = delta
        mo_ref[...] = mn
        vo_ref[...] = vn

    return pl.pallas_call(
        body, name="small_update",
        out_shape=[jax.ShapeDtypeStruct((8, D), F32)] * 4,
        compiler_params=_cp(),
    )(stats_all, ws, ms, vs)


def kernel(x, norm_mix_w, w_in, w_out, norm_ffn_w, w_gate, w_up, w_down, norm_final_w, loss_target, m_norm_mix_w, m_w_in, m_w_out, m_norm_ffn_w, m_w_gate, m_w_up, m_w_down, m_norm_final_w, v_norm_mix_w, v_w_in, v_w_out, v_norm_ffn_w, v_w_gate, v_w_up, v_w_down, v_norm_final_w):
    tr = {"w_gate", "w_up"}
    names = ["w_in", "w_out", "w_gate", "w_up", "w_down"]

    def view(a, n):
        return a[0].T if n in tr else a[0]

    big_w = [view(a, n) for a, n in zip([w_in, w_out, w_gate, w_up, w_down], names)]
    big_m = [view(a, n) for a, n in zip([m_w_in, m_w_out, m_w_gate, m_w_up, m_w_down], names)]
    big_v = [view(a, n) for a, n in zip([v_w_in, v_w_out, v_w_gate, v_w_up, v_w_down], names)]

    shards = [_cast_bf16(w, "cast_" + n) for w, n in zip(big_w, names)]
    (win,) = _all_gather(shards[0:1], "all_gather_w_in", 1)
    (wout,) = _all_gather(shards[1:2], "all_gather_w_out", 2)
    wg, wu = _all_gather(shards[2:4], "all_gather_gate_up", 3)
    (wd,) = _all_gather(shards[4:5], "all_gather_w_down", 4)
    nw3 = norm_final_w.reshape(1, D)
    place = jnp.stack([lax.axis_index("x"), lax.axis_index("y"), lax.axis_index("c")]).astype(jnp.int32)
    state = {n: (w, m, v, 0, 1) for n, w, m, v in zip(names, big_w, big_m, big_v)}
    for part in range(W_IN_PARTS):
        state[f"w_in_{part}"] = state["w_in"][:3] + (part, W_IN_PARTS)
    red = _Reduction(place, 5, state)
    stats, gx, *_ = _local_step(
        x[0], loss_target[0], norm_mix_w, norm_ffn_w, nw3, win, wout.reshape(D, D), wg, wu, wd, red)
    stats_all = _exchange_stats(stats, red.next_id())
    upd = [red.update(f"w_in_{W_IN_PARTS - 1}" if n == "w_in" else n) for n in names]
    stats_all = lax.optimization_barrier((stats_all, tuple(upd[0])))[0]

    def rows(a, b, c):
        return jnp.concatenate([a.reshape(1, D), b.reshape(1, D), c.reshape(1, D), jnp.zeros((5, D), F32)], axis=0)

    sg, sd, sm, sv = _small_update(stats_all, rows(norm_mix_w, norm_ffn_w, norm_final_w),
                                   rows(m_norm_mix_w, m_norm_ffn_w, m_norm_final_w),
                                   rows(v_norm_mix_w, v_norm_ffn_w, v_norm_final_w))
    loss = sg[3, 0]

    def outs(k, small):
        big = [(u[k].T if n in tr else u[k])[None] for u, n in zip(upd, names)]
        return [small[0:1], big[0], big[1], small[1:2], big[2], big[3], big[4], small[2]]

    return (loss, gx[None], *outs(0, sg), *outs(1, sd), *outs(2, sm), *outs(3, sv))
```

```python
import functools
import math

import numpy as np
import jax
import jax.numpy as jnp
from jax import lax
from jax.experimental import pallas as pl
from jax.experimental.pallas import tpu as pltpu
from jax.experimental.pallas import tpu_sc as plsc

F32 = jnp.float32
BF16 = jnp.bfloat16

S = 2048
D = 2048
NDEV = 8
N_IN = 7168 // NDEV
N_FF = 5632 // NDEV
NFG, N_FG = NDEV // 2, 2 * N_FF
N_OUT = 2048 // NDEV
AH, AHD = 8, 128
RH, RHD = 4, 256
CH = 128
NB = S // CH
EPS = 1e-6
PATTERNS = ((1, 16), (4, 4), (16, 1))
NEG = -1e30
VMEM_LIMIT = 56 * 1024 * 1024

ADAM_LR, ADAM_B1, ADAM_B2, ADAM_EPS, ADAM_WD, ADAM_STEP = 0.001, 0.9, 0.999, 1e-08, 0.01, 10
MESH = pl.DeviceIdType.MESH


def _cp(sem=None):
    return pltpu.CompilerParams(dimension_semantics=sem, vmem_limit_bytes=VMEM_LIMIT)


def _dot(a, b):
    return jnp.dot(a, b, preferred_element_type=F32)


def _dot_nt(a, b):
    return lax.dot_general(a, b, (((1,), (1,)), ((), ())), preferred_element_type=F32)


def _dot_tn(a, b):
    return lax.dot_general(a, b, (((0,), (0,)), ((), ())), preferred_element_type=F32)


def _sigmoid(x):
    return 0.5 * jnp.tanh(0.5 * x) + 0.5


def _cast_bf16(w, name):
    r, c = w.shape
    tm = r if r <= 1024 else 512

    def body(w_ref, o_ref):
        o_ref[...] = w_ref[...].astype(BF16)

    return pl.pallas_call(
        body, name=name, grid=(r // tm,),
        in_specs=[pl.BlockSpec((tm, c), lambda i: (i, 0))],
        out_specs=pl.BlockSpec((tm, c), lambda i: (i, 0)),
        out_shape=jax.ShapeDtypeStruct((r, c), BF16),
        compiler_params=_cp(("parallel",)),
    )(w)


def _rms_fwd(x, nw):
    tm = 256

    def body(x_ref, w_ref, h_ref, r_ref):
        xs = x_ref[...]
        r = lax.rsqrt(jnp.mean(xs * xs, axis=-1, keepdims=True) + EPS)
        h_ref[...] = ((xs * r) * w_ref[...]).astype(BF16)
        r_ref[...] = r

    return pl.pallas_call(
        body, name="rms_fwd", grid=(S // tm,),
        in_specs=[pl.BlockSpec((tm, D), lambda i: (i, 0)), pl.BlockSpec((1, D), lambda i: (0, 0))],
        out_specs=[pl.BlockSpec((tm, D), lambda i: (i, 0)), pl.BlockSpec((tm, 1), lambda i: (i, 0))],
        out_shape=[jax.ShapeDtypeStruct((S, D), BF16), jax.ShapeDtypeStruct((S, 1), F32)],
        compiler_params=_cp(("parallel",)),
    )(x, nw)


def _row_copies(hbm_refs, bufs, sems, m, tm):
    rows = pl.ds(pl.multiple_of(m * tm, tm), tm)
    return [pltpu.make_async_copy(h.at[rows], b, sems.at[i]) for i, (h, b) in enumerate(zip(hbm_refs, bufs))]


def _rms_bwd_tile(dh, xs, r, nw):
    dnw = jnp.sum(dh * (xs * r), axis=0, keepdims=True)
    gy = dh * nw
    dx = r * gy - xs * ((r * r * r) * jnp.mean(gy * xs, axis=-1, keepdims=True))
    return dx, dnw


def _proj(h1, win):
    tm = 1024

    def body(a_ref, w_ref, o_ref):
        o_ref[...] = _dot(a_ref[...], w_ref[...])

    return pl.pallas_call(
        body, name="proj", grid=(NDEV, S // tm),
        in_specs=[pl.BlockSpec((tm, D), lambda p, m: (m, 0)),
                  pl.BlockSpec((None, D, N_IN), lambda p, m: (p, 0, 0))],
        out_specs=pl.BlockSpec((tm, N_IN), lambda p, m: (m, p)),
        out_shape=jax.ShapeDtypeStruct((S, NDEV * N_IN), F32),
        compiler_params=_cp(("parallel", "parallel")),
    )(h1, win)


def _out_proj_rms(x, ma, mr, wout, nw):
    tm = 256
    half = D // 2

    def body(x_ref, ma_ref, mr_ref, w_ref, nw_ref, x2_ref, h_ref, r_ref):
        acc = _dot(ma_ref[...], w_ref[0:half, :]) + _dot(mr_ref[...], w_ref[half:D, :])
        x2 = x_ref[...] + acc
        r = lax.rsqrt(jnp.mean(x2 * x2, axis=-1, keepdims=True) + EPS)
        x2_ref[...] = x2
        h_ref[...] = ((x2 * r) * nw_ref[...]).astype(BF16)
        r_ref[...] = r

    return pl.pallas_call(
        body, name="out_proj_rms", grid=(S // tm,),
        in_specs=[pl.BlockSpec((tm, D), lambda i: (i, 0)),
                  pl.BlockSpec((tm, half), lambda i: (i, 0)),
                  pl.BlockSpec((tm, half), lambda i: (i, 0)),
                  pl.BlockSpec((D, D), lambda i: (0, 0)),
                  pl.BlockSpec((1, D), lambda i: (0, 0))],
        out_specs=[pl.BlockSpec((tm, D), lambda i: (i, 0)), pl.BlockSpec((tm, D), lambda i: (i, 0)),
                   pl.BlockSpec((tm, 1), lambda i: (i, 0))],
        out_shape=[jax.ShapeDtypeStruct((S, D), F32), jax.ShapeDtypeStruct((S, D), BF16),
                   jax.ShapeDtypeStruct((S, 1), F32)],
        compiler_params=_cp(("parallel",)),
    )(x, ma, mr, wout, nw)


def _ffn_up(h2, wg, wu):
    tm = 512

    def body(h_ref, wg_ref, wu_ref, a_ref, dadg_ref, dadu_ref):
        h = h_ref[...]
        g = _dot_nt(h, wg_ref[...])
        u = _dot_nt(h, wu_ref[...])
        sg = _sigmoid(g)
        silu = g * sg
        a_ref[...] = (silu * u).astype(BF16)
        dadg_ref[...] = (u * (sg * (1.0 + g * (1.0 - sg)))).astype(BF16)
        dadu_ref[...] = silu.astype(BF16)

    blk = pl.BlockSpec((None, tm, N_FG), lambda p, m: (p, m, 0))
    wblk = pl.BlockSpec((None, N_FG, D), lambda p, m: (p, 0, 0))
    return pl.pallas_call(
        body, name="ffn_up", grid=(NFG, S // tm),
        in_specs=[pl.BlockSpec((tm, D), lambda p, m: (m, 0)), wblk, wblk],
        out_specs=[blk, blk, blk],
        out_shape=[jax.ShapeDtypeStruct((NFG, S, N_FG), BF16)] * 3,
        compiler_params=_cp(("parallel", "parallel")),
    )(h2, wg, wu)


def _ffn_down_loss(x2, a, wd, nw, tgt):
    tm = 512

    def body(x2_hbm, a_ref, w_ref, nw_ref, t_hbm, dx_ref, dxb_ref, st_ref, acc_ref, x2_buf, t_buf, sems):
        m, p = pl.program_id(0), pl.program_id(1)
        tail_in = _row_copies((x2_hbm, t_hbm), (x2_buf, t_buf), sems, m, tm)

        @pl.when(p == 0)
        def _():
            acc_ref[...] = jnp.zeros_like(acc_ref)
            for cp in tail_in:
                cp.start()

        @pl.when((p == 0) & (m == 0))
        def _():
            st_ref[...] = jnp.zeros_like(st_ref)

        acc_ref[...] += _dot(a_ref[...], w_ref[...])

        @pl.when(p == NFG - 1)
        def _():
            for cp in tail_in:
                cp.wait()
            x3 = x2_buf[...] + acc_ref[...]
            nwv = nw_ref[...]
            r = lax.rsqrt(jnp.mean(x3 * x3, axis=-1, keepdims=True) + EPS)
            y = (x3 * r) * nwv
            err = y - t_buf[...]
            loss = 0.5 * jnp.sum(jnp.mean(err * err, axis=-1, keepdims=True), axis=0, keepdims=True)
            dy = err * (1.0 / D)
            dx, dnw = _rms_bwd_tile(dy, x3, r, nwv)
            dx_ref[...] = dx
            dxb_ref[...] = dx.astype(BF16)
            st_ref[0:1, :] += dnw
            st_ref[1:2, :] += jnp.broadcast_to(loss, (1, D))

    return pl.pallas_call(
        body, name="ffn_down_loss", grid=(S // tm, NFG),
        in_specs=[pl.BlockSpec(memory_space=pl.ANY),
                  pl.BlockSpec((None, tm, N_FG), lambda m, p: (p, m, 0)),
                  pl.BlockSpec((None, N_FG, D), lambda m, p: (p, 0, 0)),
                  pl.BlockSpec((1, D), lambda m, p: (0, 0)),
                  pl.BlockSpec(memory_space=pl.ANY)],
        out_specs=[pl.BlockSpec((tm, D), lambda m, p: (m, 0)), pl.BlockSpec((tm, D), lambda m, p: (m, 0)),
                   pl.BlockSpec((8, D), lambda m, p: (0, 0))],
        out_shape=[jax.ShapeDtypeStruct((S, D), F32), jax.ShapeDtypeStruct((S, D), BF16),
                   jax.ShapeDtypeStruct((8, D), F32)],
        scratch_shapes=[pltpu.VMEM((tm, D), F32), pltpu.VMEM((tm, D), F32), pltpu.VMEM((tm, D), F32),
                        pltpu.SemaphoreType.DMA((2,))],
        compiler_params=_cp(("arbitrary", "arbitrary")),
    )(x2, a, wd, nw, tgt)


def _ffn_down_bwd(dx3b, wd, dadg, dadu, part, before=None):
    tm = 1024
    half = NFG // 2

    def body(dx_ref, w_ref, dadg_ref, dadu_ref, *rest):
        dg_ref, du_ref = rest[-2:]
        da = _dot_nt(dx_ref[...], w_ref[...])
        dg_ref[...] = (da * dadg_ref[...].astype(F32)).astype(BF16)
        du_ref[...] = (da * dadu_ref[...].astype(F32)).astype(BF16)

    blk = pl.BlockSpec((None, tm, N_FG), lambda p, m: (p + part * half, m, 0))
    before = list(before or [])
    return pl.pallas_call(
        body, name=f"ffn_down_bwd_{part}", grid=(half, S // tm),
        in_specs=[pl.BlockSpec((tm, D), lambda p, m: (m, 0)),
                  pl.BlockSpec((None, N_FG, D), lambda p, m: (p + part * half, 0, 0)), blk, blk]
        + [pl.BlockSpec(memory_space=pl.ANY)] * len(before),
        out_specs=[blk, blk],
        out_shape=[jax.ShapeDtypeStruct((NFG, S, N_FG), BF16)] * 2,
        input_output_aliases={4 + k: k for k in range(len(before))},
        compiler_params=_cp(("parallel", "parallel")),
    )(dx3b, wd, dadg, dadu, *before)


def _ffn_up_bwd(dg, du, wg, wu, dres, xs, r, nw):
    tm = 512

    def body(dg_ref, du_ref, wg_ref, wu_ref, dres_hbm, x_hbm, r_ref, nw_ref, dx_ref, dxb_ref, st_ref,
             dres_buf, x_buf, sems):
        m, p = pl.program_id(0), pl.program_id(1)
        tail_in = _row_copies((dres_hbm, x_hbm), (dres_buf, x_buf), sems, m, tm)

        @pl.when(p == 0)
        def _():
            dx_ref[...] = jnp.zeros_like(dx_ref)
            for cp in tail_in:
                cp.start()

        @pl.when((p == 0) & (m == 0))
        def _():
            st_ref[...] = jnp.zeros_like(st_ref)

        dx_ref[...] += _dot(dg_ref[...], wg_ref[...])
        dx_ref[...] += _dot(du_ref[...], wu_ref[...])

        @pl.when(p == NFG - 1)
        def _():
            for cp in tail_in:
                cp.wait()
            dx, dnw = _rms_bwd_tile(dx_ref[...], x_buf[...], r_ref[...], nw_ref[...])
            dx = dres_buf[...] + dx
            dx_ref[...] = dx
            dxb_ref[...] = dx.astype(BF16)
            st_ref[0:1, :] += dnw

    blk = pl.BlockSpec((None, tm, N_FG), lambda m, p: (p, m, 0))
    wblk = pl.BlockSpec((None, N_FG, D), lambda m, p: (p, 0, 0))
    row = pl.BlockSpec((tm, D), lambda m, p: (m, 0))
    hbm = pl.BlockSpec(memory_space=pl.ANY)
    return pl.pallas_call(
        body, name="ffn_up_bwd", grid=(S // tm, NFG),
        in_specs=[blk, blk, wblk, wblk, hbm, hbm, pl.BlockSpec((tm, 1), lambda m, p: (m, 0)),
                  pl.BlockSpec((1, D), lambda m, p: (0, 0))],
        out_specs=[row, row, pl.BlockSpec((8, D), lambda m, p: (0, 0))],
        out_shape=[jax.ShapeDtypeStruct((S, D), F32), jax.ShapeDtypeStruct((S, D), BF16),
                   jax.ShapeDtypeStruct((8, D), F32)],
        scratch_shapes=[pltpu.VMEM((tm, D), F32), pltpu.VMEM((tm, D), F32), pltpu.SemaphoreType.DMA((2,))],
        compiler_params=_cp(("arbitrary", "arbitrary")),
    )(dg, du, wg, wu, dres, xs, r, nw)


def _out_proj_bwd(dx2b, wout):
    tm = 256

    def body(dx_ref, w_ref, o_ref):
        o_ref[...] = _dot_nt(dx_ref[...], w_ref[...])

    return pl.pallas_call(
        body, name="out_proj_bwd", grid=(S // tm,),
        in_specs=[pl.BlockSpec((tm, D), lambda i: (i, 0)), pl.BlockSpec((D, D), lambda i: (0, 0))],
        out_specs=pl.BlockSpec((tm, D), lambda i: (i, 0)),
        out_shape=jax.ShapeDtypeStruct((S, D), F32),
        compiler_params=_cp(("parallel",)),
    )(dx2b, wout)


def _in_proj_bwd(dproj, win, dres, xs, r, nw):
    tm = 1024

    def body(dp_ref, w_ref, dres_hbm, x_hbm, r_ref, nw_ref, dx_ref, st_ref, dres_buf, x_buf, sems):
        m, p = pl.program_id(0), pl.program_id(1)
        tail_in = _row_copies((dres_hbm, x_hbm), (dres_buf, x_buf), sems, m, tm)

        @pl.when(p == 0)
        def _():
            dx_ref[...] = jnp.zeros_like(dx_ref)
            for cp in tail_in:
                cp.start()

        @pl.when((p == 0) & (m == 0))
        def _():
            st_ref[...] = jnp.zeros_like(st_ref)

        dx_ref[...] += _dot_nt(dp_ref[...], w_ref[...])

        @pl.when(p == NDEV - 1)
        def _():
            for cp in tail_in:
                cp.wait()
            dx, dnw = _rms_bwd_tile(dx_ref[...], x_buf[...], r_ref[...], nw_ref[...])
            dx_ref[...] = dres_buf[...] + dx
            st_ref[0:1, :] += dnw

    row = pl.BlockSpec((tm, D), lambda m, p: (m, 0))
    hbm = pl.BlockSpec(memory_space=pl.ANY)
    return pl.pallas_call(
        body, name="in_proj_bwd", grid=(S // tm, NDEV),
        in_specs=[pl.BlockSpec((tm, N_IN), lambda m, p: (m, p)),
                  pl.BlockSpec((None, D, N_IN), lambda m, p: (p, 0, 0)),
                  hbm, hbm, pl.BlockSpec((tm, 1), lambda m, p: (m, 0)),
                  pl.BlockSpec((1, D), lambda m, p: (0, 0))],
        out_specs=[row, pl.BlockSpec((8, D), lambda m, p: (0, 0))],
        out_shape=[jax.ShapeDtypeStruct((S, D), F32), jax.ShapeDtypeStruct((8, D), F32)],
        scratch_shapes=[pltpu.VMEM((tm, D), F32), pltpu.VMEM((tm, D), F32), pltpu.SemaphoreType.DMA((2,))],
        compiler_params=_cp(("arbitrary", "arbitrary")),
    )(dproj, win, dres, xs, r, nw)


W_IN_PARTS = 2


def _wgrad_in(h1, dproj, part):
    rows = D // W_IN_PARTS

    def body(a_ref, d_ref, o_ref):
        both = _dot_tn(a_ref[...], d_ref[...]).astype(BF16)
        o_ref[0] = both[:, 0:N_IN]
        o_ref[1] = both[:, N_IN:2 * N_IN]

    return pl.pallas_call(
        body, name=f"wgrad_in_{part}", grid=(NDEV // 2,),
        in_specs=[pl.BlockSpec((S, rows), lambda p: (0, part)), pl.BlockSpec((S, 2 * N_IN), lambda p: (0, p))],
        out_specs=pl.BlockSpec((2, rows, N_IN), lambda p: (p, 0, 0)),
        out_shape=jax.ShapeDtypeStruct((NDEV, rows, N_IN), BF16),
        compiler_params=_cp(("parallel",)),
    )(h1, dproj)


def _wgrad_rows(a3, dy, name):
    def body(a_ref, d_ref, o_ref):
        o_ref[...] = _dot_tn(a_ref[...], d_ref[...]).astype(BF16)

    return pl.pallas_call(
        body, name=name, grid=(NFG,),
        in_specs=[pl.BlockSpec((None, S, N_FG), lambda p: (p, 0, 0)), pl.BlockSpec((S, D), lambda p: (0, 0))],
        out_specs=pl.BlockSpec((None, N_FG, D), lambda p: (p, 0, 0)),
        out_shape=jax.ShapeDtypeStruct((NFG, N_FG, D), BF16),
        compiler_params=_cp(("parallel",)),
    )(a3, dy).reshape(NDEV, N_FF, D)


def _wgrad_out(ma, mr, dx2b):
    half = D // 2
    per = half // N_OUT

    def body(ma_ref, mr_ref, d_ref, o_ref):
        p = pl.program_id(0)

        @pl.when(p == 0)
        def _():
            o_ref[...] = _dot_tn(ma_ref[...], d_ref[...]).astype(BF16).reshape(per, N_OUT, D)

        @pl.when(p == 1)
        def _():
            o_ref[...] = _dot_tn(mr_ref[...], d_ref[...]).astype(BF16).reshape(per, N_OUT, D)

    whole = pl.BlockSpec((S, half), lambda p: (0, 0))
    return pl.pallas_call(
        body, name="wgrad_out", grid=(2,),
        in_specs=[whole, whole, pl.BlockSpec((S, D), lambda p: (0, 0))],
        out_specs=pl.BlockSpec((per, N_OUT, D), lambda p: (p, 0, 0)),
        out_shape=jax.ShapeDtypeStruct((NDEV, N_OUT, D), BF16),
        compiler_params=_cp(("parallel",)),
    )(ma, mr, dx2b)


def _attn_consts():
    c = np.zeros((AH, 8, AHD), np.float32)
    for h in range(AH):
        c[h, :, :] = 2.0 ** (-(h + 1))
    return jnp.asarray(c)


def _permute_in(dst, src, d, cast=None):
    v = src[...]
    if d > 1:
        v = pltpu.einshape("jrc->rjc", v.reshape(S // d, d, AHD)).reshape(S, AHD)
    dst[...] = v if cast is None else v.astype(cast)


def _natural_order(v, d):
    if d == 1:
        return v
    return pltpu.einshape("rjc->jrc", v.reshape(d, S // d, AHD)).reshape(S, AHD)


def _attn_masks():
    qi = lax.broadcasted_iota(jnp.int32, (CH, CH), 0)
    kj = lax.broadcasted_iota(jnp.int32, (CH, CH), 1)
    dist_c = (qi - kj).astype(F32)
    dist_p = (qi - kj + CH).astype(F32)
    return (qi >= kj)[None], (kj >= qi)[None], dist_c[None], dist_p[None]


GB = 16


def _bdot_nt(a, b):
    return lax.dot_general(a, b, (((2,), (2,)), ((0,), (0,))), preferred_element_type=F32)


def _bdot(a, b):
    return lax.dot_general(a, b, (((2,), (1,)), ((0,), (0,))), preferred_element_type=F32)


def _bdot_tn(a, b):
    return lax.dot_general(a, b, (((1,), (1,)), ((0,), (0,))), preferred_element_type=F32)


def _shift_block(dst, src):
    dst[0:CH, :] = jnp.zeros((CH, AHD), dst.dtype)
    dst[CH:S, :] = src[0:S - CH, :]


def _has_prev(g, nb):
    blk = lax.broadcasted_iota(jnp.int32, (GB, 1, 1), 0) + g * GB
    return (blk & (nb - 1)) != 0


def _blocks(ref, g):
    return ref[g * GB * CH:(g + 1) * GB * CH, :].reshape(GB, CH, AHD)


def _attn_fwd(proj):
    scale = 1.0 / math.sqrt(AHD)

    def body(c_ref, q_ref, k_ref, v_ref, o_ref, ob_ref, lse_ref, qkvp_ref, lsep_ref, qd, kd, vd, kps, vps, od, ld, *nat):
        onat, lnat = nat[0:3], nat[3:6]
        slope = c_ref[0:1, :]
        mask_c, mask_p, dist_c, dist_p = _attn_masks()
        for pi, (d, nb) in enumerate(PATTERNS):
            _permute_in(qd, q_ref, d, BF16)
            _permute_in(kd, k_ref, d, BF16)
            _permute_in(vd, v_ref, d, BF16)
            if d > 1:
                qkvp_ref[pi - 1, 0] = qd[...]
                qkvp_ref[pi - 1, 1] = kd[...]
                qkvp_ref[pi - 1, 2] = vd[...]
            if nb > 1:
                _shift_block(kps, kd)
                _shift_block(vps, vd)
            bias_c = -(slope * float(d)) * dist_c
            bias_p = -(slope * float(d)) * dist_p
            for g in range(NB // GB):
                q3, k3, v3 = _blocks(qd, g), _blocks(kd, g), _blocks(vd, g)
                s_c = jnp.where(mask_c, _bdot_nt(q3, k3) * scale + bias_c, NEG)
                mx = jnp.max(s_c, axis=-1, keepdims=True)
                if nb > 1:
                    kp3, vp3 = _blocks(kps, g), _blocks(vps, g)
                    s_p = jnp.where(jnp.logical_and(mask_p, _has_prev(g, nb)),
                                    _bdot_nt(q3, kp3) * scale + bias_p, NEG)
                    mx = jnp.maximum(mx, jnp.max(s_p, axis=-1, keepdims=True))
                    l = (jnp.sum(jnp.exp(s_c - mx), axis=-1, keepdims=True)
                         + jnp.sum(jnp.exp(s_p - mx), axis=-1, keepdims=True))
                    lse = mx + jnp.log(l)
                    o3 = _bdot(jnp.exp(s_c - lse).astype(BF16), v3) + _bdot(jnp.exp(s_p - lse).astype(BF16), vp3)
                else:
                    l = jnp.sum(jnp.exp(s_c - mx), axis=-1, keepdims=True)
                    lse = mx + jnp.log(l)
                    o3 = _bdot(jnp.exp(s_c - lse).astype(BF16), v3)
                rows = slice(g * GB * CH, (g + 1) * GB * CH)
                od[rows, :] = o3.reshape(GB * CH, AHD)
                ld[rows, :] = jnp.broadcast_to(lse, (GB, CH, AHD)).reshape(GB * CH, AHD)
            onat[pi][...] = _natural_order(od[...], d)
            lnat[pi][...] = _natural_order(ld[...], d)
        l0, l1, l2 = lnat[0][...], lnat[1][...], lnat[2][...]
        mx = jnp.maximum(jnp.maximum(l0, l1), l2)
        e0, e1, e2 = jnp.exp(l0 - mx), jnp.exp(l1 - mx), jnp.exp(l2 - mx)
        den = e0 + e1 + e2
        out = (e0 / den) * onat[0][...] + (e1 / den) * onat[1][...] + (e2 / den) * onat[2][...]
        o_ref[...] = out
        ob_ref[...] = out.astype(BF16)
        lse_ref[...] = mx + jnp.log(den)
        for pi, (d, _) in enumerate(PATTERNS[1:]):
            _permute_in(lsep_ref.at[pi], lse_ref, d)

    def col(off):
        return pl.BlockSpec((S, AHD), lambda h: (0, off + h))

    return pl.pallas_call(
        body, name="attn_fwd", grid=(AH,),
        in_specs=[pl.BlockSpec((None, 8, AHD), lambda h: (h, 0, 0)), col(0), col(AH), col(2 * AH)],
        out_specs=[col(0), col(0), col(0), pl.BlockSpec((2, 3, S, AHD), lambda h: (0, 0, 0, h)),
                   pl.BlockSpec((2, S, AHD), lambda h: (0, 0, h))],
        out_shape=[jax.ShapeDtypeStruct((S, AH * AHD), F32), jax.ShapeDtypeStruct((S, AH * AHD), BF16),
                   jax.ShapeDtypeStruct((S, AH * AHD), F32),
                   jax.ShapeDtypeStruct((2, 3, S, AH * AHD), BF16), jax.ShapeDtypeStruct((2, S, AH * AHD), F32)],
        scratch_shapes=[pltpu.VMEM((S, AHD), BF16) for _ in range(5)]
        + [pltpu.VMEM((S, AHD), F32) for _ in range(8)],
        compiler_params=_cp(("parallel",)),
    )(_attn_consts(), proj, proj, proj)


def _attn_bwd(proj, dmixed, o, lse, qkvp, lsep):
    scale = 1.0 / math.sqrt(AHD)

    def body(c_ref, q_ref, k_ref, v_ref, do_ref, o_ref, lse_ref, qkvp_ref, lsep_ref, dproj_hbm,
             qd, kd, vd, dod, kps, vps, dld, dqd, dkd, dvd, delta, aq, ak, av, sq, sk, sv, sems):
        h = pl.program_id(0)

        def out_copies(head):
            return [pltpu.make_async_copy(
                st, dproj_hbm.at[:, pl.ds(pl.multiple_of((k * AH + head) * AHD, AHD), AHD)], sems.at[k])
                for k, st in enumerate((sq, sk, sv))]

        slope = c_ref[0:1, :]
        mask_c, mask_p, dist_c, dist_p = _attn_masks()
        delta[...] = jnp.broadcast_to(jnp.sum(do_ref[...] * o_ref[...], axis=-1, keepdims=True), (S, AHD))
        for pi, (d, nb) in enumerate(PATTERNS):
            if d == 1:
                _permute_in(qd, q_ref, d, BF16)
                _permute_in(kd, k_ref, d, BF16)
                _permute_in(vd, v_ref, d, BF16)
                qs, ks, vs, lss = qd, kd, vd, lse_ref
            else:
                qs, ks, vs, lss = (qkvp_ref.at[pi - 1, 0], qkvp_ref.at[pi - 1, 1], qkvp_ref.at[pi - 1, 2],
                                   lsep_ref.at[pi - 1])
            _permute_in(dod, do_ref, d, BF16)
            _permute_in(dld, delta, d)
            if nb > 1:
                _shift_block(kps, ks)
                _shift_block(vps, vs)
            bias_c = -(slope * float(d)) * dist_c
            bias_p = -(slope * float(d)) * dist_p
            for g in range(NB // GB):
                q3, k3, v3, do3 = _blocks(qs, g), _blocks(ks, g), _blocks(vs, g), _blocks(dod, g)
                ls, dl = _blocks(lss, g), _blocks(dld, g)
                lo, hi = g * GB * CH, (g + 1) * GB * CH
                p_c = jnp.exp(jnp.where(mask_c, _bdot_nt(q3, k3) * scale + bias_c, NEG) - ls)
                ds_c = ((p_c * (_bdot_nt(do3, v3) - dl)) * scale).astype(BF16)
                dq3 = _bdot(ds_c, k3)
                dkd[lo:hi, :] = _bdot_tn(ds_c, q3).reshape(GB * CH, AHD)
                dvd[lo:hi, :] = _bdot_tn(p_c.astype(BF16), do3).reshape(GB * CH, AHD)
                if nb > 1:
                    kp3, vp3 = _blocks(kps, g), _blocks(vps, g)
                    p_p = jnp.exp(jnp.where(jnp.logical_and(mask_p, _has_prev(g, nb)),
                                            _bdot_nt(q3, kp3) * scale + bias_p, NEG) - ls)
                    ds_p = ((p_p * (_bdot_nt(do3, vp3) - dl)) * scale).astype(BF16)
                    dq3 = dq3 + _bdot(ds_p, kp3)
                    dkp = _bdot_tn(ds_p, q3).reshape(GB * CH, AHD)
                    dvp = _bdot_tn(p_p.astype(BF16), do3).reshape(GB * CH, AHD)
                    if g == 0:
                        dkd[0:hi - CH, :] += dkp[CH:, :]
                        dvd[0:hi - CH, :] += dvp[CH:, :]
                    else:
                        dkd[lo - CH:hi - CH, :] += dkp
                        dvd[lo - CH:hi - CH, :] += dvp
                dqd[lo:hi, :] = dq3.reshape(GB * CH, AHD)
            ln = S // d
            for acc, src in ((aq, dqd), (ak, dkd), (av, dvd)):
                if pi == 0:
                    acc[...] = src[...]
                else:
                    acc[...] += _natural_order(src[...], d)

        @pl.when(h > 0)
        def _():
            for cp in out_copies(h - 1):
                cp.wait()

        sq[...] = aq[...].astype(BF16)
        sk[...] = ak[...].astype(BF16)
        sv[...] = av[...].astype(BF16)
        for cp in out_copies(h):
            cp.start()

        @pl.when(h == AH - 1)
        def _():
            for cp in out_copies(h):
                cp.wait()

    def col(off):
        return pl.BlockSpec((S, AHD), lambda h: (0, off + h))

    return pl.pallas_call(
        body, name="attn_bwd", grid=(AH,),
        in_specs=[pl.BlockSpec((None, 8, AHD), lambda h: (h, 0, 0)), col(0), col(AH), col(2 * AH),
                  col(0), col(0), col(0), pl.BlockSpec((2, 3, S, AHD), lambda h: (0, 0, 0, h)),
                  pl.BlockSpec((2, S, AHD), lambda h: (0, 0, h))],
        out_specs=pl.BlockSpec(memory_space=pl.ANY),
        out_shape=jax.ShapeDtypeStruct((S, NDEV * N_IN), BF16),
        scratch_shapes=[pltpu.VMEM((S, AHD), BF16) for _ in range(6)]
        + [pltpu.VMEM((S, AHD), F32) for _ in range(8)]
        + [pltpu.VMEM((S, AHD), BF16) for _ in range(3)] + [pltpu.SemaphoreType.DMA((3,))],
        compiler_params=_cp(("arbitrary",)),
    )(_attn_consts(), proj, proj, proj, dmixed, o, lse, qkvp, lsep)


def _ret_consts():
    c = np.zeros((RH, 8, RHD), np.float32)
    for h in range(RH):
        c[h, :, :] = np.log(np.float32(1.0) - np.float32(2.0 ** (-5.0 - h)))
    return jnp.asarray(c)


def _ret_factors(lg):
    i = lax.broadcasted_iota(jnp.int32, (CH, CH), 0)
    j = lax.broadcasted_iota(jnp.int32, (CH, CH), 1)
    dif = (i - j).astype(F32)
    decay = jnp.where(dif >= 0, jnp.exp(lg[:, 0:CH] * jnp.maximum(dif, 0.0)), 0.0)
    row = lax.broadcasted_iota(jnp.int32, (CH, RHD), 0).astype(F32)
    zeta = jnp.exp(lg * (CH - 1.0 - row))
    xi = jnp.exp(lg * (row + 1.0))
    return decay, zeta, xi, jnp.exp(lg * float(CH))


CBK = 8
RSTEPS = NB // CBK


def _ret_specs(rev):
    off = 3 * AH * AHD // RHD
    rows = CBK * CH

    def ch(n):
        return (RSTEPS - 1 - n) if rev else n

    def col(k):
        return pl.BlockSpec((rows, RHD), lambda h, n: (ch(n), off + k * RH + h))

    own = pl.BlockSpec((rows, RHD), lambda h, n: (ch(n), h))
    state = pl.BlockSpec((None, CBK, RHD, RHD), lambda h, n: (h, ch(n), 0, 0))
    const = pl.BlockSpec((None, 8, RHD), lambda h, n: (h, 0, 0))
    dm = pl.BlockSpec((rows, RHD), lambda h, n: (ch(n), AH * AHD // RHD + h))
    return col, own, state, const, dm


def _chunks(x):
    return x.reshape(CBK, CH, RHD)


def _ret_fwd(proj):
    def body(c_ref, q_ref, k_ref, v_ref, g_ref, ret_ref, mr_ref, st_ref, r_acc):
        n = pl.program_id(1)

        @pl.when(n == 0)
        def _():
            r_acc[...] = jnp.zeros_like(r_acc)

        decay, zeta, xi, gch = _ret_factors(c_ref[0:1, :])
        q3 = _chunks(q_ref[...].astype(BF16))
        kc = _chunks(k_ref[...] * (1.0 / math.sqrt(RHD)))
        k3 = kc.astype(BF16)
        v3 = _chunks(v_ref[...].astype(BF16))
        kv3 = _bdot_tn((kc * zeta[None]).astype(BF16), v3)
        r = r_acc[...]
        for i in range(CBK):
            st_ref[i] = r.astype(BF16)
            r = r * gch + kv3[i]
        r_acc[...] = r
        scores = _bdot_nt(q3, k3) * decay[None]
        ret = (_bdot(scores.astype(BF16), v3) + _bdot(q3, st_ref[...]) * xi[None]).reshape(CBK * CH, RHD)
        ret_ref[...] = ret
        rr = lax.rsqrt(jnp.mean(ret * ret, axis=-1, keepdims=True) + EPS)
        gv = g_ref[...]
        mr_ref[...] = ((gv * _sigmoid(gv)) * (ret * rr)).astype(BF16)

    col, own, state, const, _ = _ret_specs(False)
    return pl.pallas_call(
        body, name="ret_fwd", grid=(RH, RSTEPS),
        in_specs=[const, col(0), col(1), col(2), col(3)],
        out_specs=[own, own, state],
        out_shape=[jax.ShapeDtypeStruct((S, RH * RHD), F32), jax.ShapeDtypeStruct((S, RH * RHD), BF16),
                   jax.ShapeDtypeStruct((RH, NB, RHD, RHD), BF16)],
        scratch_shapes=[pltpu.VMEM((RHD, RHD), F32)],
        compiler_params=_cp(("parallel", "arbitrary")),
    )(_ret_consts(), proj, proj, proj, proj)


def _ret_bwd(proj, ret, states, dmixed, dproj):
    rows = CBK * CH
    col0 = 3 * AH * AHD

    def body(c_ref, q_ref, k_ref, v_ref, g_ref, ret_ref, st_ref, dm_ref, dproj_in, dproj_hbm, g_acc, gs,
             sq, sk, sv, sg, sems):
        del dproj_in
        h, n = pl.program_id(0), pl.program_id(1)
        step = h * RSTEPS + n

        def out_copies(t):
            hh, nn = t // RSTEPS, t % RSTEPS
            r0 = pl.multiple_of((RSTEPS - 1 - nn) * rows, rows)
            return [pltpu.make_async_copy(
                st, dproj_hbm.at[pl.ds(r0, rows), pl.ds(pl.multiple_of(col0 + (k * RH + hh) * RHD, RHD), RHD)],
                sems.at[k]) for k, st in enumerate((sq, sk, sv, sg))]

        @pl.when(n == 0)
        def _():
            g_acc[...] = jnp.zeros_like(g_acc)

        decay, zeta, xi, gch = _ret_factors(c_ref[0:1, :])
        ret_v = ret_ref[...]
        rr = lax.rsqrt(jnp.mean(ret_v * ret_v, axis=-1, keepdims=True) + EPS)
        gv = g_ref[...]
        sgm = _sigmoid(gv)
        dmix = dm_ref[...]
        dgate = ((dmix * (ret_v * rr)) * (sgm * (1.0 + gv * (1.0 - sgm)))).astype(BF16)
        dretn = dmix * (gv * sgm)
        dret = _chunks(rr * dretn - ret_v * ((rr * rr * rr) * jnp.mean(dretn * ret_v, axis=-1, keepdims=True)))

        q3 = _chunks(q_ref[...].astype(BF16))
        kc = _chunks(k_ref[...] * (1.0 / math.sqrt(RHD)))
        k3 = kc.astype(BF16)
        v3 = _chunks(v_ref[...].astype(BF16))
        d3 = dret.astype(BF16)
        dxi = (dret * xi[None]).astype(BF16)
        kz = (kc * zeta[None]).astype(BF16)
        dr3 = _bdot_tn(q3, dxi)
        acc = g_acc[...]
        for i in reversed(range(CBK)):
            gs[i] = acc.astype(BF16)
            acc = dr3[i] + gch * acc
        g_acc[...] = acc
        g3 = gs[...]
        sc = (_bdot_nt(q3, k3) * decay[None]).astype(BF16)
        da = (_bdot_nt(d3, v3) * decay[None]).astype(BF16)
        dq = _bdot(da, k3) + _bdot_nt(dxi, st_ref[...])
        dkc = _bdot_tn(da, q3) + _bdot_nt(v3, g3) * zeta[None]
        dv = _bdot_tn(sc, d3) + _bdot(kz, g3)

        @pl.when(step > 0)
        def _():
            for cp in out_copies(step - 1):
                cp.wait()

        sq[...] = dq.reshape(rows, RHD).astype(BF16)
        sk[...] = (dkc * (1.0 / math.sqrt(RHD))).reshape(rows, RHD).astype(BF16)
        sv[...] = dv.reshape(rows, RHD).astype(BF16)
        sg[...] = dgate
        for cp in out_copies(step):
            cp.start()

        @pl.when(step == RH * RSTEPS - 1)
        def _():
            for cp in out_copies(step):
                cp.wait()

    col, own, state, const, dm = _ret_specs(True)
    hbm = pl.BlockSpec(memory_space=pl.ANY)
    return pl.pallas_call(
        body, name="ret_bwd", grid=(RH, RSTEPS),
        in_specs=[const, col(0), col(1), col(2), col(3), own, state, dm, hbm],
        out_specs=hbm,
        out_shape=jax.ShapeDtypeStruct(dproj.shape, dproj.dtype),
        input_output_aliases={8: 0},
        scratch_shapes=[pltpu.VMEM((RHD, RHD), F32), pltpu.VMEM((CBK, RHD, RHD), BF16)]
        + [pltpu.VMEM((rows, RHD), BF16) for _ in range(4)] + [pltpu.SemaphoreType.DMA((4,))],
        compiler_params=_cp(("arbitrary", "arbitrary")),
    )(_ret_consts(), proj, proj, proj, proj, ret, states, dmixed, dproj)


class _NoReduction:
    def start(self, group, grads):
        pass

    def local(self, name, first=()):
        return []

    def landed(self, name):
        return []

    def update(self, name):
        return []


def _local_step(x, tgt, nw1, nw2, nw3, win, wout, wg, wu, wd, red=None):
    red = red or _NoReduction()

    def after(values, first):
        return lax.optimization_barrier((tuple(values), tuple(first)))[0]

    wg, wu, wd = (w.reshape(NFG, N_FG, D) for w in (wg, wu, wd))
    h1, r1 = _rms_fwd(x, nw1)
    proj = _proj(h1, win)
    o, ma, lse, qkvp, lsep = _attn_fwd(proj)
    ret, mr, states = _ret_fwd(proj)
    x2, h2, r2 = _out_proj_rms(x, ma, mr, wout, nw2)
    a, dadg, dadu = _ffn_up(h2, wg, wu)
    dx3, dx3b, st3 = _ffn_down_loss(x2, a, wd, nw3, tgt)

    dwd = _wgrad_rows(a, dx3b, "wgrad_down")
    red.start(["w_down"], [dwd])
    (dx3b,) = after([dx3b], [dwd])
    part = _ffn_down_bwd(dx3b, wd, dadg, dadu, 0)
    (dx3b,) = after([dx3b], red.local("w_down", first=part))
    dg, du = _ffn_down_bwd(dx3b, wd, dadg, dadu, 1, part)
    dwg = _wgrad_rows(dg, h2, "wgrad_gate")
    red.start(["w_gate"], [dwg])
    (du,) = after([du], [dwg])
    dwu = _wgrad_rows(du, h2, "wgrad_up")
    red.start(["w_up"], [dwu])
    dg, du = after([dg, du], red.local("w_gate", first=[dwu] + red.landed("w_down")))
    dx2, dx2b, st2 = _ffn_up_bwd(dg, du, wg, wu, dx3, x2, r2, nw2)
    (dx2b,) = after([dx2b], red.local("w_up", first=[dx2b]))
    dwo = _wgrad_out(ma, mr, dx2b)
    red.start(["w_out"], [dwo])
    (dx2b,) = after([dx2b], [dwo])
    dmixed = _out_proj_bwd(dx2b, wout)
    dproj = _attn_bwd(proj, dmixed, o, lse, qkvp, lsep)
    (dmixed,) = after([dmixed], red.local("w_out", first=[dproj] + red.landed("w_gate")))
    dproj = _ret_bwd(proj, ret, states, dmixed, dproj)
    (dwi0,) = after([_wgrad_in(h1, dproj, 0)], red.landed("w_up"))
    red.start(["w_in_0"], [dwi0])
    (dproj,) = after([dproj], [dwi0])
    dwi1 = _wgrad_in(h1, dproj, 1)
    red.start(["w_in_1"], [dwi1])
    sums = red.local("w_in_0", first=[dwi1] + red.landed("w_out"))
    sums = red.local("w_in_1", first=sums + red.update("w_down"))
    (dproj,) = after([dproj], sums)
    gx, st1 = _in_proj_bwd(dproj, win, dx2, x, r1, nw1)
    dwi = jnp.concatenate([dwi0, dwi1], axis=1)
    stats = jnp.concatenate([st1[0:1], st2[0:1], st3[0:2], jnp.zeros((4, D), F32)], axis=0)
    return stats, gx, dwi, dwo, dwg, dwu, dwd


def _place():
    x, y, c = lax.axis_index("x"), lax.axis_index("y"), lax.axis_index("c")
    return x, y, c, [(1 - x, y), (x, 1 - y), (1 - x, 1 - y)]


def _handshake(peers):
    barrier = pltpu.get_barrier_semaphore()
    for peer in peers:
        pl.semaphore_signal(barrier, inc=1, device_id=peer, device_id_type=MESH)
    pl.semaphore_wait(barrier, len(peers))


def _all_gather(shards, name, collective_id):
    na = len(shards)
    SIB, XN0, XN1, YN1, YN0, VIA_X, VIA_Y = 0, 1, 2, 3, 4, 5, 6
    D2D = {XN0: 7, XN1: 8, YN1: 9, YN0: 10, VIA_X: 11, VIA_Y: 12}

    def body(*refs):
        ins, outs = refs[:na], refs[na:2 * na]
        send_sems, recv_sems, local_sems = refs[2 * na:]
        x, y, c, _ = _place()
        me, sib = (x, y, c), (x, y, 1 - c)
        xn, yn, dg = (1 - x, y, c), (x, 1 - y, c), (1 - x, 1 - y, c)
        _handshake([sib, xn, yn])

        def part(ref, h):
            rows = ref.shape[0] // 2
            return ref if h is None else ref.at[pl.ds(h * rows, rows)]

        def block(a, owner, h):
            return part(outs[a].at[4 * owner[0] + 2 * owner[1] + owner[2]], h)

        def copy(a, k, owner, h, to, own_src=False):
            return pltpu.make_async_remote_copy(
                src_ref=part(ins[a], h) if own_src else block(a, owner, h), dst_ref=block(a, owner, h),
                send_sem=send_sems.at[a, k], recv_sem=recv_sems.at[a, k], device_id=to, device_id_type=MESH)

        def other(p):
            return (p[0], p[1], 1 - c)

        mine = [pltpu.make_async_copy(ins[a], block(a, me, None), local_sems.at[a]) for a in range(na)]
        for cp in mine:
            cp.start()
        sent = []
        for a in range(na):
            sent += [copy(a, XN0, me, 0, xn, True), copy(a, YN1, me, 1, yn, True),
                     copy(a, XN1, me, 1, xn, True), copy(a, YN0, me, 0, yn, True)]
        sent += [copy(a, SIB, me, None, sib, True) for a in range(na)]
        for cp in sent:
            cp.start()

        def landed(a, k, owner, h, then):
            copy(a, k, owner, h, me).wait_recv()
            for k2, to in then + [(D2D[k], sib)]:
                cp = copy(a, k2, owner, h, to)
                cp.start()
                sent.append(cp)

        for a in range(na):
            landed(a, XN0, xn, 0, [(VIA_Y, yn)])
            landed(a, YN1, yn, 1, [(VIA_X, xn)])
            landed(a, XN1, xn, 1, [])
            landed(a, YN0, yn, 0, [])
        for a in range(na):
            landed(a, VIA_Y, dg, 0, [])
            landed(a, VIA_X, dg, 1, [])
        for a in range(na):
            copy(a, SIB, sib, None, me).wait_recv()
            for k, owner, h in ((XN0, xn, 0), (XN1, xn, 1), (YN1, yn, 1), (YN0, yn, 0), (VIA_Y, dg, 0), (VIA_X, dg, 1)):
                copy(a, D2D[k], other(owner), h, me).wait_recv()
        for cp in sent:
            cp.wait_send()
        for cp in mine:
            cp.wait()

    return _sequencer_call(
        body, name, collective_id,
        [jax.ShapeDtypeStruct((NDEV,) + s.shape, s.dtype) for s in shards],
        [pltpu.SemaphoreType.DMA((na, 13)), pltpu.SemaphoreType.DMA((na, 13)), pltpu.SemaphoreType.DMA((na,))])(*shards)


def _sequencer_call(body, name, collective_id, out_type, scratch_types):
    return pl.kernel(
        body, name=name, out_type=out_type,
        mesh=plsc.ScalarSubcoreMesh(axis_name="sequencer", num_cores=1),
        scratch_types=scratch_types,
        compiler_params=pltpu.CompilerParams(collective_id=collective_id))


def _exchange_sibling(grads, name, collective_id):
    na = len(grads)

    def body(*refs):
        ins, outs = refs[:na], refs[na:2 * na]
        send_sems, recv_sems = refs[2 * na:]
        x, y, c, _ = _place()
        _handshake([(x, y, 1 - c)])
        cps = []
        for a in range(na):
            for k in range(4):
                cps.append(pltpu.make_async_remote_copy(
                    src_ref=ins[a].at[2 * k + (1 - c)], dst_ref=outs[a].at[k],
                    send_sem=send_sems.at[a, k], recv_sem=recv_sems.at[a, k],
                    device_id=(x, y, 1 - c), device_id_type=MESH))
        for cp in cps:
            cp.start()
        for cp in cps:
            cp.wait()

    return _sequencer_call(
        body, name, collective_id,
        [jax.ShapeDtypeStruct((4,) + g.shape[1:], g.dtype) for g in grads],
        [pltpu.SemaphoreType.DMA((na, 4)), pltpu.SemaphoreType.DMA((na, 4))])(*grads)


def _row_tile(rows, cols):
    for t in (512, 256, 176, 128, 64, 32, 16):
        if rows % t == 0 and t * cols * 4 <= (2 << 20):
            return t
    raise ValueError((rows, cols))


def _chip_sum(place, g, got, name):
    _, r, c = g.shape
    tm = r

    def body(pos_ref, g_ref, got_ref, o_ref):
        o_ref[...] = (g_ref[...].astype(F32) + got_ref[...].astype(F32)).astype(BF16)

    def chip(j, pos):
        return 2 * (pos[0] ^ jnp.where(j == 1, 0, 1)) + (pos[1] ^ jnp.where(j == 0, 0, 1))

    return pl.pallas_call(
        body, name=name,
        grid_spec=pltpu.PrefetchScalarGridSpec(
            num_scalar_prefetch=1, grid=(3, r // tm),
            in_specs=[pl.BlockSpec((None, tm, c), lambda j, i, pos: (2 * chip(j, pos) + pos[2], i, 0)),
                      pl.BlockSpec((None, tm, c), lambda j, i, pos: (chip(j, pos), i, 0))],
            out_specs=pl.BlockSpec((None, tm, c), lambda j, i, pos: (j, i, 0))),
        out_shape=jax.ShapeDtypeStruct((3, r, c), BF16),
        compiler_params=_cp(("parallel", "parallel")),
    )(place, g, got)


def _exchange_chips(sums, name, collective_id):
    na = len(sums)

    def body(*refs):
        ins, outs = refs[:na], refs[na:2 * na]
        send_sems, recv_sems = refs[2 * na:]
        x, y, c, chips = _place()
        _handshake([(*chip, c) for chip in chips])
        cps = []
        for a in range(na):
            for j, chip in enumerate(chips):
                cps.append(pltpu.make_async_remote_copy(
                    src_ref=ins[a].at[j], dst_ref=outs[a].at[j],
                    send_sem=send_sems.at[a, j], recv_sem=recv_sems.at[a, j],
                    device_id=(*chip, c), device_id_type=MESH))
        for cp in cps:
            cp.start()
        for cp in cps:
            cp.wait()

    return _sequencer_call(
        body, name, collective_id,
        [jax.ShapeDtypeStruct((3,) + s.shape[1:], s.dtype) for s in sums],
        [pltpu.SemaphoreType.DMA((na, 3)), pltpu.SemaphoreType.DMA((na, 3))])(*sums)


def _exchange_stats(stats, collective_id):
    def body(st_in, st_out, st_send, st_recv, local_sem):
        x, y, c, _ = _place()
        me_idx = 4 * x + 2 * y + c
        peers = [(x ^ ((k >> 2) & 1), y ^ ((k >> 1) & 1), c ^ (k & 1)) for k in range(1, 8)]
        _handshake(peers)
        mine = pltpu.make_async_copy(st_in, st_out.at[me_idx], local_sem)
        mine.start()
        cps = [pltpu.make_async_remote_copy(
            src_ref=st_in, dst_ref=st_out.at[me_idx], send_sem=st_send.at[k], recv_sem=st_recv.at[k],
            device_id=peer, device_id_type=MESH) for k, peer in enumerate(peers)]
        for cp in cps:
            cp.start()
        for cp in cps:
            cp.wait()
        mine.wait()

    return _sequencer_call(
        body, "exchange_stats", collective_id,
        jax.ShapeDtypeStruct((NDEV,) + stats.shape, stats.dtype),
        [pltpu.SemaphoreType.DMA((7,)), pltpu.SemaphoreType.DMA((7,)), pltpu.SemaphoreType.DMA])(stats)


class _Reduction:
    def __init__(self, place, first_collective_id, state):
        self.place = place
        self.ids = iter(range(first_collective_id, 32))
        self.state = state
        self.groups = {}
        self.updates = {}

    def next_id(self):
        return next(self.ids)

    def start(self, group, grads):
        got = _exchange_sibling(grads, "sibling_exchange_" + group[0], self.next_id())
        self.groups[group[0]] = dict(names=group, grads=grads, got=got)

    def local(self, name, first=()):
        grp = self.groups[name]
        grads = lax.optimization_barrier((tuple(grp["grads"]), tuple(first)))[0]
        grp["sums"] = [_chip_sum(self.place, g, s, "chip_sum_" + n)
                       for g, s, n in zip(grads, grp["got"], grp["names"])]
        grp["chips"] = _exchange_chips(grp["sums"], "chip_exchange_" + name, self.next_id())
        return grp["sums"]

    def landed(self, name):
        return list(self.groups[name]["chips"])

    def update(self, name):
        if name not in self.updates:
            grp = next(g for g in self.groups.values() if name in g["names"])
            k = grp["names"].index(name)
            w, m, v, part, parts = self.state[name]
            before = self.update(f"{name[:-1]}{part - 1}") if part else None
            self.updates[name] = _shard_update(self.place, w, m, v, grp["grads"][k], grp["got"][k],
                                               grp["chips"][k], "update_" + name, part, parts, before)
        return list(self.updates[name])


def _adamw(w, g, m, v):
    m = ADAM_B1 * m + (1.0 - ADAM_B1) * g
    v = ADAM_B2 * v + (1.0 - ADAM_B2) * (g * g)
    m_hat = m / (1.0 - ADAM_B1 ** ADAM_STEP)
    v_hat = v / (1.0 - ADAM_B2 ** ADAM_STEP)
    delta = -ADAM_LR * (m_hat / (jnp.sqrt(v_hat) + ADAM_EPS) + ADAM_WD * w)
    return delta, m, v


def _shard_update(place, w, m, v, g, got_sib, got_chips, name, part=0, parts=1, before=None):
    r, c = w.shape
    rp = r // parts
    tm = _row_tile(rp, c)
    off = part * (rp // tm)

    def body(pos_ref, w_ref, m_ref, v_ref, g_ref, s_ref, c_ref, *rest):
        go_ref, d_ref, mo_ref, vo_ref = rest[-4:]
        grad = g_ref[...].astype(F32) + s_ref[...].astype(F32)
        for j in range(3):
            grad = grad + c_ref[j].astype(F32)
        delta, mn, vn = _adamw(w_ref[...], grad, m_ref[...], v_ref[...])
        go_ref[...] = grad
        d_ref[...] = delta
        mo_ref[...] = mn
        vo_ref[...] = vn

    row = pl.BlockSpec((tm, c), lambda i, pos: (i + off, 0))
    before = list(before or [])
    return pl.pallas_call(
        body, name=name,
        grid_spec=pltpu.PrefetchScalarGridSpec(
            num_scalar_prefetch=1, grid=(rp // tm,),
            in_specs=[row, row, row,
                      pl.BlockSpec((None, tm, c), lambda i, pos: (4 * pos[0] + 2 * pos[1] + pos[2], i, 0)),
                      pl.BlockSpec((None, tm, c), lambda i, pos: (2 * pos[0] + pos[1], i, 0)),
                      pl.BlockSpec((3, tm, c), lambda i, pos: (0, i, 0))]
            + [pl.BlockSpec(memory_space=pl.ANY)] * len(before),
            out_specs=[row, row, row, row]),
        out_shape=[jax.ShapeDtypeStruct((r, c), F32)] * 4,
        input_output_aliases={7 + k: k for k in range(len(before))},
        compiler_params=_cp(("parallel",)),
    )(place, w, m, v, g, got_sib, got_chips, *before)


def _small_update(stats_all, ws, ms, vs):
    def body(st_ref, w_ref, m_ref, v_ref, go_ref, d_ref, mo_ref, vo_ref):
        grad = st_ref[0]
        for k in range(1, NDEV):
            grad = grad + st_ref[k]
        delta, mn, vn = _adamw(w_ref[...], grad, m_ref[...], v_ref[...])
        go_ref[...] = grad
        d_ref[...] = delta
        mo_ref[...] = mn
        vo_ref[...] = vn

    return pl.pallas_call(
        body, name="small_update",
        out_shape=[jax.ShapeDtypeStruct((8, D), F32)] * 4,
        compiler_params=_cp(),
    )(stats_all, ws, ms, vs)


def kernel(x, norm_mix_w, w_in, w_out, norm_ffn_w, w_gate, w_up, w_down, norm_final_w, loss_target, m_norm_mix_w, m_w_in, m_w_out, m_norm_ffn_w, m_w_gate, m_w_up, m_w_down, m_norm_final_w, v_norm_mix_w, v_w_in, v_w_out, v_norm_ffn_w, v_w_gate, v_w_up, v_w_down, v_norm_final_w):
    tr = {"w_gate", "w_up"}
    names = ["w_in", "w_out", "w_gate", "w_up", "w_down"]

    def view(a, n):
        return a[0].T if n in tr else a[0]

    big_w = [view(a, n) for a, n in zip([w_in, w_out, w_gate, w_up, w_down], names)]
    big_m = [view(a, n) for a, n in zip([m_w_in, m_w_out, m_w_gate, m_w_up, m_w_down], names)]
    big_v = [view(a, n) for a, n in zip([v_w_in, v_w_out, v_w_gate, v_w_up, v_w_down], names)]

    shards = [_cast_bf16(w, "cast_" + n) for w, n in zip(big_w, names)]
    (win,) = _all_gather(shards[0:1], "all_gather_w_in", 1)
    (wout,) = _all_gather(shards[1:2], "all_gather_w_out", 2)
    wg, wu = _all_gather(shards[2:4], "all_gather_gate_up", 3)
    (wd,) = _all_gather(shards[4:5], "all_gather_w_down", 4)
    nw3 = norm_final_w.reshape(1, D)
    place = jnp.stack([lax.axis_index("x"), lax.axis_index("y"), lax.axis_index("c")]).astype(jnp.int32)
    state = {n: (w, m, v, 0, 1) for n, w, m, v in zip(names, big_w, big_m, big_v)}
    for part in range(W_IN_PARTS):
        state[f"w_in_{part}"] = state["w_in"][:3] + (part, W_IN_PARTS)
    red = _Reduction(place, 5, state)
    stats, gx, *_ = _local_step(
        x[0], loss_target[0], norm_mix_w, norm_ffn_w, nw3, win, wout.reshape(D, D), wg, wu, wd, red)
    stats_all = _exchange_stats(stats, red.next_id())
    upd = [red.update(f"w_in_{W_IN_PARTS - 1}" if n == "w_in" else n) for n in names]
    stats_all = lax.optimization_barrier((stats_all, tuple(upd[0])))[0]

    def rows(a, b, c):
        return jnp.concatenate([a.reshape(1, D), b.reshape(1, D), c.reshape(1, D), jnp.zeros((5, D), F32)], axis=0)

    sg, sd, sm, sv = _small_update(stats_all, rows(norm_mix_w, norm_ffn_w, norm_final_w),
                                   rows(m_norm_mix_w, m_norm_ffn_w, m_norm_final_w),
                                   rows(v_norm_mix_w, v_norm_ffn_w, v_norm_final_w))
    loss = sg[3, 0]

    def outs(k, small):
        big = [(u[k].T if n in tr else u[k])[None] for u, n in zip(upd, names)]
        return [small[0:1], big[0], big[1], small[1:2], big[2], big[3], big[4], small[2]]

    return (loss, gx[None], *outs(0, sg), *outs(1, sd), *outs(2, sm), *outs(3, sv))
```

```python
import functools
import math

import numpy as np
import jax
import jax.numpy as jnp
from jax import lax
from jax.experimental import pallas as pl
from jax.experimental.pallas import tpu as pltpu
from jax.experimental.pallas import tpu_sc as plsc

F32 = jnp.float32
BF16 = jnp.bfloat16

S = 2048
D = 2048
NDEV = 8
N_IN = 7168 // NDEV
N_FF = 5632 // NDEV
NFG, N_FG = NDEV // 2, 2 * N_FF
N_OUT = 2048 // NDEV
AH, AHD = 8, 128
RH, RHD = 4, 256
CH = 128
NB = S // CH
EPS = 1e-6
PATTERNS = ((1, 16), (4, 4), (16, 1))
NEG = -1e30
VMEM_LIMIT = 56 * 1024 * 1024

ADAM_LR, ADAM_B1, ADAM_B2, ADAM_EPS, ADAM_WD, ADAM_STEP = 0.001, 0.9, 0.999, 1e-08, 0.01, 10
MESH = pl.DeviceIdType.MESH


def _cp(sem=None):
    return pltpu.CompilerParams(dimension_semantics=sem, vmem_limit_bytes=VMEM_LIMIT)


def _dot(a, b):
    return jnp.dot(a, b, preferred_element_type=F32)


def _dot_nt(a, b):
    return lax.dot_general(a, b, (((1,), (1,)), ((), ())), preferred_element_type=F32)


def _dot_tn(a, b):
    return lax.dot_general(a, b, (((0,), (0,)), ((), ())), preferred_element_type=F32)


def _sigmoid(x):
    return 0.5 * jnp.tanh(0.5 * x) + 0.5


def _cast_bf16(w, name):
    r, c = w.shape
    tm = r if r <= 1024 else 512

    def body(w_ref, o_ref):
        o_ref[...] = w_ref[...].astype(BF16)

    return pl.pallas_call(
        body, name=name, grid=(r // tm,),
        in_specs=[pl.BlockSpec((tm, c), lambda i: (i, 0))],
        out_specs=pl.BlockSpec((tm, c), lambda i: (i, 0)),
        out_shape=jax.ShapeDtypeStruct((r, c), BF16),
        compiler_params=_cp(("parallel",)),
    )(w)


def _rms_fwd(x, nw):
    tm = 256

    def body(x_ref, w_ref, h_ref, r_ref):
        xs = x_ref[...]
        r = lax.rsqrt(jnp.mean(xs * xs, axis=-1, keepdims=True) + EPS)
        h_ref[...] = ((xs * r) * w_ref[...]).astype(BF16)
        r_ref[...] = r

    return pl.pallas_call(
        body, name="rms_fwd", grid=(S // tm,),
        in_specs=[pl.BlockSpec((tm, D), lambda i: (i, 0)), pl.BlockSpec((1, D), lambda i: (0, 0))],
        out_specs=[pl.BlockSpec((tm, D), lambda i: (i, 0)), pl.BlockSpec((tm, 1), lambda i: (i, 0))],
        out_shape=[jax.ShapeDtypeStruct((S, D), BF16), jax.ShapeDtypeStruct((S, 1), F32)],
        compiler_params=_cp(("parallel",)),
    )(x, nw)


def _row_copies(hbm_refs, bufs, sems, m, tm):
    rows = pl.ds(pl.multiple_of(m * tm, tm), tm)
    return [pltpu.make_async_copy(h.at[rows], b, sems.at[i]) for i, (h, b) in enumerate(zip(hbm_refs, bufs))]


def _rms_bwd_tile(dh, xs, r, nw):
    dnw = jnp.sum(dh * (xs * r), axis=0, keepdims=True)
    gy = dh * nw
    dx = r * gy - xs * ((r * r * r) * jnp.mean(gy * xs, axis=-1, keepdims=True))
    return dx, dnw


def _proj(h1, win):
    tm = 1024

    def body(a_ref, w_ref, o_ref):
        o_ref[...] = _dot(a_ref[...], w_ref[...])

    return pl.pallas_call(
        body, name="proj", grid=(NDEV, S // tm),
        in_specs=[pl.BlockSpec((tm, D), lambda p, m: (m, 0)),
                  pl.BlockSpec((None, D, N_IN), lambda p, m: (p, 0, 0))],
        out_specs=pl.BlockSpec((tm, N_IN), lambda p, m: (m, p)),
        out_shape=jax.ShapeDtypeStruct((S, NDEV * N_IN), F32),
        compiler_params=_cp(("parallel", "parallel")),
    )(h1, win)


def _out_proj_rms(x, ma, mr, wout, nw):
    tm = 256
    half = D // 2

    def body(x_ref, ma_ref, mr_ref, w_ref, nw_ref, x2_ref, h_ref, r_ref):
        acc = _dot(ma_ref[...], w_ref[0:half, :]) + _dot(mr_ref[...], w_ref[half:D, :])
        x2 = x_ref[...] + acc
        r = lax.rsqrt(jnp.mean(x2 * x2, axis=-1, keepdims=True) + EPS)
        x2_ref[...] = x2
        h_ref[...] = ((x2 * r) * nw_ref[...]).astype(BF16)
        r_ref[...] = r

    return pl.pallas_call(
        body, name="out_proj_rms", grid=(S // tm,),
        in_specs=[pl.BlockSpec((tm, D), lambda i: (i, 0)),
                  pl.BlockSpec((tm, half), lambda i: (i, 0)),
                  pl.BlockSpec((tm, half), lambda i: (i, 0)),
                  pl.BlockSpec((D, D), lambda i: (0, 0)),
                  pl.BlockSpec((1, D), lambda i: (0, 0))],
        out_specs=[pl.BlockSpec((tm, D), lambda i: (i, 0)), pl.BlockSpec((tm, D), lambda i: (i, 0)),
                   pl.BlockSpec((tm, 1), lambda i: (i, 0))],
        out_shape=[jax.ShapeDtypeStruct((S, D), F32), jax.ShapeDtypeStruct((S, D), BF16),
                   jax.ShapeDtypeStruct((S, 1), F32)],
        compiler_params=_cp(("parallel",)),
    )(x, ma, mr, wout, nw)


def _ffn_up(h2, wg, wu):
    tm = 512

    def body(h_ref, wg_ref, wu_ref, a_ref, dadg_ref, dadu_ref):
        h = h_ref[...]
        g = _dot_nt(h, wg_ref[...])
        u = _dot_nt(h, wu_ref[...])
        sg = _sigmoid(g)
        silu = g * sg
        a_ref[...] = (silu * u).astype(BF16)
        dadg_ref[...] = (u * (sg * (1.0 + g * (1.0 - sg)))).astype(BF16)
        dadu_ref[...] = silu.astype(BF16)

    blk = pl.BlockSpec((None, tm, N_FG), lambda p, m: (p, m, 0))
    wblk = pl.BlockSpec((None, N_FG, D), lambda p, m: (p, 0, 0))
    return pl.pallas_call(
        body, name="ffn_up", grid=(NFG, S // tm),
        in_specs=[pl.BlockSpec((tm, D), lambda p, m: (m, 0)), wblk, wblk],
        out_specs=[blk, blk, blk],
        out_shape=[jax.ShapeDtypeStruct((NFG, S, N_FG), BF16)] * 3,
        compiler_params=_cp(("parallel", "parallel")),
    )(h2, wg, wu)


def _ffn_down_loss(x2, a, wd, nw, tgt):
    tm = 512

    def body(x2_hbm, a_ref, w_ref, nw_ref, t_hbm, dx_ref, dxb_ref, st_ref, acc_ref, x2_buf, t_buf, sems):
        m, p = pl.program_id(0), pl.program_id(1)
        tail_in = _row_copies((x2_hbm, t_hbm), (x2_buf, t_buf), sems, m, tm)

        @pl.when(p == 0)
        def _():
            acc_ref[...] = jnp.zeros_like(acc_ref)
            for cp in tail_in:
                cp.start()

        @pl.when((p == 0) & (m == 0))
        def _():
            st_ref[...] = jnp.zeros_like(st_ref)

        acc_ref[...] += _dot(a_ref[...], w_ref[...])

        @pl.when(p == NFG - 1)
        def _():
            for cp in tail_in:
                cp.wait()
            x3 = x2_buf[...] + acc_ref[...]
            nwv = nw_ref[...]
            r = lax.rsqrt(jnp.mean(x3 * x3, axis=-1, keepdims=True) + EPS)
            y = (x3 * r) * nwv
            err = y - t_buf[...]
            loss = 0.5 * jnp.sum(jnp.mean(err * err, axis=-1, keepdims=True), axis=0, keepdims=True)
            dy = err * (1.0 / D)
            dx, dnw = _rms_bwd_tile(dy, x3, r, nwv)
            dx_ref[...] = dx
            dxb_ref[...] = dx.astype(BF16)
            st_ref[0:1, :] += dnw
            st_ref[1:2, :] += jnp.broadcast_to(loss, (1, D))

    return pl.pallas_call(
        body, name="ffn_down_loss", grid=(S // tm, NFG),
        in_specs=[pl.BlockSpec(memory_space=pl.ANY),
                  pl.BlockSpec((None, tm, N_FG), lambda m, p: (p, m, 0)),
                  pl.BlockSpec((None, N_FG, D), lambda m, p: (p, 0, 0)),
                  pl.BlockSpec((1, D), lambda m, p: (0, 0)),
                  pl.BlockSpec(memory_space=pl.ANY)],
        out_specs=[pl.BlockSpec((tm, D), lambda m, p: (m, 0)), pl.BlockSpec((tm, D), lambda m, p: (m, 0)),
                   pl.BlockSpec((8, D), lambda m, p: (0, 0))],
        out_shape=[jax.ShapeDtypeStruct((S, D), F32), jax.ShapeDtypeStruct((S, D), BF16),
                   jax.ShapeDtypeStruct((8, D), F32)],
        scratch_shapes=[pltpu.VMEM((tm, D), F32), pltpu.VMEM((tm, D), F32), pltpu.VMEM((tm, D), F32),
                        pltpu.SemaphoreType.DMA((2,))],
        compiler_params=_cp(("arbitrary", "arbitrary")),
    )(x2, a, wd, nw, tgt)


def _ffn_down_bwd(dx3b, wd, dadg, dadu, part, before=None):
    tm = 1024
    half = NFG // 2

    def body(dx_ref, w_ref, dadg_ref, dadu_ref, *rest):
        dg_ref, du_ref = rest[-2:]
        da = _dot_nt(dx_ref[...], w_ref[...])
        dg_ref[...] = (da * dadg_ref[...].astype(F32)).astype(BF16)
        du_ref[...] = (da * dadu_ref[...].astype(F32)).astype(BF16)

    blk = pl.BlockSpec((None, tm, N_FG), lambda p, m: (p + part * half, m, 0))
    before = list(before or [])
    return pl.pallas_call(
        body, name=f"ffn_down_bwd_{part}", grid=(half, S // tm),
        in_specs=[pl.BlockSpec((tm, D), lambda p, m: (m, 0)),
                  pl.BlockSpec((None, N_FG, D), lambda p, m: (p + part * half, 0, 0)), blk, blk]
        + [pl.BlockSpec(memory_space=pl.ANY)] * len(before),
        out_specs=[blk, blk],
        out_shape=[jax.ShapeDtypeStruct((NFG, S, N_FG), BF16)] * 2,
        input_output_aliases={4 + k: k for k in range(len(before))},
        compiler_params=_cp(("parallel", "parallel")),
    )(dx3b, wd, dadg, dadu, *before)


def _ffn_up_bwd(dg, du, wg, wu, dres, xs, r, nw):
    tm = 512

    def body(dg_ref, du_ref, wg_ref, wu_ref, dres_hbm, x_hbm, r_ref, nw_ref, dx_ref, dxb_ref, st_ref,
             dres_buf, x_buf, sems):
        m, p = pl.program_id(0), pl.program_id(1)
        tail_in = _row_copies((dres_hbm, x_hbm), (dres_buf, x_buf), sems, m, tm)

        @pl.when(p == 0)
        def _():
            dx_ref[...] = jnp.zeros_like(dx_ref)
            for cp in tail_in:
                cp.start()

        @pl.when((p == 0) & (m == 0))
        def _():
            st_ref[...] = jnp.zeros_like(st_ref)

        dx_ref[...] += _dot(dg_ref[...], wg_ref[...])
        dx_ref[...] += _dot(du_ref[...], wu_ref[...])

        @pl.when(p == NFG - 1)
        def _():
            for cp in tail_in:
                cp.wait()
            dx, dnw = _rms_bwd_tile(dx_ref[...], x_buf[...], r_ref[...], nw_ref[...])
            dx = dres_buf[...] + dx
            dx_ref[...] = dx
            dxb_ref[...] = dx.astype(BF16)
            st_ref[0:1, :] += dnw

    blk = pl.BlockSpec((None, tm, N_FG), lambda m, p: (p, m, 0))
    wblk = pl.BlockSpec((None, N_FG, D), lambda m, p: (p, 0, 0))
    row = pl.BlockSpec((tm, D), lambda m, p: (m, 0))
    hbm = pl.BlockSpec(memory_space=pl.ANY)
    return pl.pallas_call(
        body, name="ffn_up_bwd", grid=(S // tm, NFG),
        in_specs=[blk, blk, wblk, wblk, hbm, hbm, pl.BlockSpec((tm, 1), lambda m, p: (m, 0)),
                  pl.BlockSpec((1, D), lambda m, p: (0, 0))],
        out_specs=[row, row, pl.BlockSpec((8, D), lambda m, p: (0, 0))],
        out_shape=[jax.ShapeDtypeStruct((S, D), F32), jax.ShapeDtypeStruct((S, D), BF16),
                   jax.ShapeDtypeStruct((8, D), F32)],
        scratch_shapes=[pltpu.VMEM((tm, D), F32), pltpu.VMEM((tm, D), F32), pltpu.SemaphoreType.DMA((2,))],
        compiler_params=_cp(("arbitrary", "arbitrary")),
    )(dg, du, wg, wu, dres, xs, r, nw)


def _out_proj_bwd(dx2b, wout, place=None, rider=None):
    tm = 256

    if rider is None:
        def body(dx_ref, w_ref, o_ref):
            o_ref[...] = _dot_nt(dx_ref[...], w_ref[...])

        return pl.pallas_call(
            body, name="out_proj_bwd", grid=(S // tm,),
            in_specs=[pl.BlockSpec((tm, D), lambda i: (i, 0)), pl.BlockSpec((D, D), lambda i: (0, 0))],
            out_specs=pl.BlockSpec((tm, D), lambda i: (i, 0)),
            out_shape=jax.ShapeDtypeStruct((S, D), F32),
            compiler_params=_cp(("parallel",)),
        )(dx2b, wout), None

    w = rider[0]
    r, c = w.shape
    rt = _row_tile(r, c)
    nt = r // rt
    assert nt <= S // tm

    def body(pos_ref, dx_ref, w_ref, uw, um, uv, ug, us, uc, o_ref, go, dd, mo, vo):
        o_ref[...] = _dot_nt(dx_ref[...], w_ref[...])

        @pl.when(pl.program_id(0) < nt)
        def _():
            _update_tile(uw, um, uv, ug, us, uc, go, dd, mo, vo)

    def at(i):
        return jnp.minimum(i, nt - 1)

    tile = pl.BlockSpec((rt, c), lambda i, pos: (at(i), 0))
    outs = pl.pallas_call(
        body, name="out_proj_bwd",
        grid_spec=pltpu.PrefetchScalarGridSpec(
            num_scalar_prefetch=1, grid=(S // tm,),
            in_specs=[pl.BlockSpec((tm, D), lambda i, pos: (i, 0)), pl.BlockSpec((D, D), lambda i, pos: (0, 0)),
                      tile, tile, tile,
                      pl.BlockSpec((None, rt, c), lambda i, pos: (4 * pos[0] + 2 * pos[1] + pos[2], at(i), 0)),
                      pl.BlockSpec((None, rt, c), lambda i, pos: (2 * pos[0] + pos[1], at(i), 0)),
                      pl.BlockSpec((3, rt, c), lambda i, pos: (0, at(i), 0))],
            out_specs=[pl.BlockSpec((tm, D), lambda i, pos: (i, 0)), tile, tile, tile, tile]),
        out_shape=[jax.ShapeDtypeStruct((S, D), F32)] + [jax.ShapeDtypeStruct((r, c), F32)] * 4,
        compiler_params=_cp(("arbitrary",)),
    )(place, dx2b, wout, *rider)
    return outs[0], outs[1:]


def _in_proj_bwd(dproj, win, dres, xs, r, nw):
    tm = 1024

    def body(dp_ref, w_ref, dres_hbm, x_hbm, r_ref, nw_ref, dx_ref, st_ref, dres_buf, x_buf, sems):
        m, p = pl.program_id(0), pl.program_id(1)
        tail_in = _row_copies((dres_hbm, x_hbm), (dres_buf, x_buf), sems, m, tm)

        @pl.when(p == 0)
        def _():
            dx_ref[...] = jnp.zeros_like(dx_ref)
            for cp in tail_in:
                cp.start()

        @pl.when((p == 0) & (m == 0))
        def _():
            st_ref[...] = jnp.zeros_like(st_ref)

        dx_ref[...] += _dot_nt(dp_ref[...], w_ref[...])

        @pl.when(p == NDEV - 1)
        def _():
            for cp in tail_in:
                cp.wait()
            dx, dnw = _rms_bwd_tile(dx_ref[...], x_buf[...], r_ref[...], nw_ref[...])
            dx_ref[...] = dres_buf[...] + dx
            st_ref[0:1, :] += dnw

    row = pl.BlockSpec((tm, D), lambda m, p: (m, 0))
    hbm = pl.BlockSpec(memory_space=pl.ANY)
    return pl.pallas_call(
        body, name="in_proj_bwd", grid=(S // tm, NDEV),
        in_specs=[pl.BlockSpec((tm, N_IN), lambda m, p: (m, p)),
                  pl.BlockSpec((None, D, N_IN), lambda m, p: (p, 0, 0)),
                  hbm, hbm, pl.BlockSpec((tm, 1), lambda m, p: (m, 0)),
                  pl.BlockSpec((1, D), lambda m, p: (0, 0))],
        out_specs=[row, pl.BlockSpec((8, D), lambda m, p: (0, 0))],
        out_shape=[jax.ShapeDtypeStruct((S, D), F32), jax.ShapeDtypeStruct((8, D), F32)],
        scratch_shapes=[pltpu.VMEM((tm, D), F32), pltpu.VMEM((tm, D), F32), pltpu.SemaphoreType.DMA((2,))],
        compiler_params=_cp(("arbitrary", "arbitrary")),
    )(dproj, win, dres, xs, r, nw)


W_IN_PARTS = 2


def _wgrad_in(h1, dproj, part):
    rows = D // W_IN_PARTS

    def body(a_ref, d_ref, o_ref):
        both = _dot_tn(a_ref[...], d_ref[...]).astype(BF16)
        o_ref[0] = both[:, 0:N_IN]
        o_ref[1] = both[:, N_IN:2 * N_IN]

    return pl.pallas_call(
        body, name=f"wgrad_in_{part}", grid=(NDEV // 2,),
        in_specs=[pl.BlockSpec((S, rows), lambda p: (0, part)), pl.BlockSpec((S, 2 * N_IN), lambda p: (0, p))],
        out_specs=pl.BlockSpec((2, rows, N_IN), lambda p: (p, 0, 0)),
        out_shape=jax.ShapeDtypeStruct((NDEV, rows, N_IN), BF16),
        compiler_params=_cp(("parallel",)),
    )(h1, dproj)


def _wgrad_rows(a3, dy, name):
    def body(a_ref, d_ref, o_ref):
        o_ref[...] = _dot_tn(a_ref[...], d_ref[...]).astype(BF16)

    return pl.pallas_call(
        body, name=name, grid=(NFG,),
        in_specs=[pl.BlockSpec((None, S, N_FG), lambda p: (p, 0, 0)), pl.BlockSpec((S, D), lambda p: (0, 0))],
        out_specs=pl.BlockSpec((None, N_FG, D), lambda p: (p, 0, 0)),
        out_shape=jax.ShapeDtypeStruct((NFG, N_FG, D), BF16),
        compiler_params=_cp(("parallel",)),
    )(a3, dy).reshape(NDEV, N_FF, D)


def _wgrad_out(ma, mr, dx2b):
    half = D // 2
    per = half // N_OUT

    def body(ma_ref, mr_ref, d_ref, o_ref):
        p = pl.program_id(0)

        @pl.when(p == 0)
        def _():
            o_ref[...] = _dot_tn(ma_ref[...], d_ref[...]).astype(BF16).reshape(per, N_OUT, D)

        @pl.when(p == 1)
        def _():
            o_ref[...] = _dot_tn(mr_ref[...], d_ref[...]).astype(BF16).reshape(per, N_OUT, D)

    whole = pl.BlockSpec((S, half), lambda p: (0, 0))
    return pl.pallas_call(
        body, name="wgrad_out", grid=(2,),
        in_specs=[whole, whole, pl.BlockSpec((S, D), lambda p: (0, 0))],
        out_specs=pl.BlockSpec((per, N_OUT, D), lambda p: (p, 0, 0)),
        out_shape=jax.ShapeDtypeStruct((NDEV, N_OUT, D), BF16),
        compiler_params=_cp(("parallel",)),
    )(ma, mr, dx2b)


def _attn_consts():
    c = np.zeros((AH, 8, AHD), np.float32)
    for h in range(AH):
        c[h, :, :] = 2.0 ** (-(h + 1))
    return jnp.asarray(c)


def _permute_in(dst, src, d, cast=None):
    v = src[...]
    if d > 1:
        v = pltpu.einshape("jrc->rjc", v.reshape(S // d, d, AHD)).reshape(S, AHD)
    dst[...] = v if cast is None else v.astype(cast)


def _natural_order(v, d):
    if d == 1:
        return v
    return pltpu.einshape("rjc->jrc", v.reshape(d, S // d, AHD)).reshape(S, AHD)


def _attn_masks():
    qi = lax.broadcasted_iota(jnp.int32, (CH, CH), 0)
    kj = lax.broadcasted_iota(jnp.int32, (CH, CH), 1)
    dist_c = (qi - kj).astype(F32)
    dist_p = (qi - kj + CH).astype(F32)
    return (qi >= kj)[None], (kj >= qi)[None], dist_c[None], dist_p[None]


GB = 16


def _bdot_nt(a, b):
    return lax.dot_general(a, b, (((2,), (2,)), ((0,), (0,))), preferred_element_type=F32)


def _bdot(a, b):
    return lax.dot_general(a, b, (((2,), (1,)), ((0,), (0,))), preferred_element_type=F32)


def _bdot_tn(a, b):
    return lax.dot_general(a, b, (((1,), (1,)), ((0,), (0,))), preferred_element_type=F32)


def _shift_block(dst, src):
    dst[0:CH, :] = jnp.zeros((CH, AHD), dst.dtype)
    dst[CH:S, :] = src[0:S - CH, :]


def _has_prev(g, nb):
    blk = lax.broadcasted_iota(jnp.int32, (GB, 1, 1), 0) + g * GB
    return (blk & (nb - 1)) != 0


def _blocks(ref, g):
    return ref[g * GB * CH:(g + 1) * GB * CH, :].reshape(GB, CH, AHD)


def _attn_fwd(proj):
    scale = 1.0 / math.sqrt(AHD)

    def body(c_ref, q_ref, k_ref, v_ref, o_ref, ob_ref, lse_ref, qkvp_ref, lsep_ref, qd, kd, vd, kps, vps, od, ld, *nat):
        onat, lnat = nat[0:3], nat[3:6]
        slope = c_ref[0:1, :]
        mask_c, mask_p, dist_c, dist_p = _attn_masks()
        for pi, (d, nb) in enumerate(PATTERNS):
            _permute_in(qd, q_ref, d, BF16)
            _permute_in(kd, k_ref, d, BF16)
            _permute_in(vd, v_ref, d, BF16)
            if d > 1:
                qkvp_ref[pi - 1, 0] = qd[...]
                qkvp_ref[pi - 1, 1] = kd[...]
                qkvp_ref[pi - 1, 2] = vd[...]
            if nb > 1:
                _shift_block(kps, kd)
                _shift_block(vps, vd)
            bias_c = -(slope * float(d)) * dist_c
            bias_p = -(slope * float(d)) * dist_p
            for g in range(NB // GB):
                q3, k3, v3 = _blocks(qd, g), _blocks(kd, g), _blocks(vd, g)
                s_c = jnp.where(mask_c, _bdot_nt(q3, k3) * scale + bias_c, NEG)
                mx = jnp.max(s_c, axis=-1, keepdims=True)
                if nb > 1:
                    kp3, vp3 = _blocks(kps, g), _blocks(vps, g)
                    s_p = jnp.where(jnp.logical_and(mask_p, _has_prev(g, nb)),
                                    _bdot_nt(q3, kp3) * scale + bias_p, NEG)
                    mx = jnp.maximum(mx, jnp.max(s_p, axis=-1, keepdims=True))
                    l = (jnp.sum(jnp.exp(s_c - mx), axis=-1, keepdims=True)
                         + jnp.sum(jnp.exp(s_p - mx), axis=-1, keepdims=True))
                    lse = mx + jnp.log(l)
                    o3 = _bdot(jnp.exp(s_c - lse).astype(BF16), v3) + _bdot(jnp.exp(s_p - lse).astype(BF16), vp3)
                else:
                    l = jnp.sum(jnp.exp(s_c - mx), axis=-1, keepdims=True)
                    lse = mx + jnp.log(l)
                    o3 = _bdot(jnp.exp(s_c - lse).astype(BF16), v3)
                rows = slice(g * GB * CH, (g + 1) * GB * CH)
                od[rows, :] = o3.reshape(GB * CH, AHD)
                ld[rows, :] = jnp.broadcast_to(lse, (GB, CH, AHD)).reshape(GB * CH, AHD)
            onat[pi][...] = _natural_order(od[...], d)
            lnat[pi][...] = _natural_order(ld[...], d)
        l0, l1, l2 = lnat[0][...], lnat[1][...], lnat[2][...]
        mx = jnp.maximum(jnp.maximum(l0, l1), l2)
        e0, e1, e2 = jnp.exp(l0 - mx), jnp.exp(l1 - mx), jnp.exp(l2 - mx)
        den = e0 + e1 + e2
        out = (e0 / den) * onat[0][...] + (e1 / den) * onat[1][...] + (e2 / den) * onat[2][...]
        o_ref[...] = out
        ob_ref[...] = out.astype(BF16)
        lse_ref[...] = mx + jnp.log(den)
        for pi, (d, _) in enumerate(PATTERNS[1:]):
            _permute_in(lsep_ref.at[pi], lse_ref, d)

    def col(off):
        return pl.BlockSpec((S, AHD), lambda h: (0, off + h))

    return pl.pallas_call(
        body, name="attn_fwd", grid=(AH,),
        in_specs=[pl.BlockSpec((None, 8, AHD), lambda h: (h, 0, 0)), col(0), col(AH), col(2 * AH)],
        out_specs=[col(0), col(0), col(0), pl.BlockSpec((2, 3, S, AHD), lambda h: (0, 0, 0, h)),
                   pl.BlockSpec((2, S, AHD), lambda h: (0, 0, h))],
        out_shape=[jax.ShapeDtypeStruct((S, AH * AHD), F32), jax.ShapeDtypeStruct((S, AH * AHD), BF16),
                   jax.ShapeDtypeStruct((S, AH * AHD), F32),
                   jax.ShapeDtypeStruct((2, 3, S, AH * AHD), BF16), jax.ShapeDtypeStruct((2, S, AH * AHD), F32)],
        scratch_shapes=[pltpu.VMEM((S, AHD), BF16) for _ in range(5)]
        + [pltpu.VMEM((S, AHD), F32) for _ in range(8)],
        compiler_params=_cp(("parallel",)),
    )(_attn_consts(), proj, proj, proj)


def _attn_bwd(proj, dmixed, o, lse, qkvp, lsep):
    scale = 1.0 / math.sqrt(AHD)

    def body(c_ref, q_ref, k_ref, v_ref, do_ref, o_ref, lse_ref, qkvp_ref, lsep_ref, dproj_hbm,
             qd, kd, vd, dod, kps, vps, dld, dqd, dkd, dvd, delta, aq, ak, av, sq, sk, sv, sems):
        h = pl.program_id(0)

        def out_copies(head):
            return [pltpu.make_async_copy(
                st, dproj_hbm.at[:, pl.ds(pl.multiple_of((k * AH + head) * AHD, AHD), AHD)], sems.at[k])
                for k, st in enumerate((sq, sk, sv))]

        slope = c_ref[0:1, :]
        mask_c, mask_p, dist_c, dist_p = _attn_masks()
        delta[...] = jnp.broadcast_to(jnp.sum(do_ref[...] * o_ref[...], axis=-1, keepdims=True), (S, AHD))
        for pi, (d, nb) in enumerate(PATTERNS):
            if d == 1:
                _permute_in(qd, q_ref, d, BF16)
                _permute_in(kd, k_ref, d, BF16)
                _permute_in(vd, v_ref, d, BF16)
                qs, ks, vs, lss = qd, kd, vd, lse_ref
            else:
                qs, ks, vs, lss = (qkvp_ref.at[pi - 1, 0], qkvp_ref.at[pi - 1, 1], qkvp_ref.at[pi - 1, 2],
                                   lsep_ref.at[pi - 1])
            _permute_in(dod, do_ref, d, BF16)
            _permute_in(dld, delta, d)
            if nb > 1:
                _shift_block(kps, ks)
                _shift_block(vps, vs)
            bias_c = -(slope * float(d)) * dist_c
            bias_p = -(slope * float(d)) * dist_p
            for g in range(NB // GB):
                q3, k3, v3, do3 = _blocks(qs, g), _blocks(ks, g), _blocks(vs, g), _blocks(dod, g)
                ls, dl = _blocks(lss, g), _blocks(dld, g)
                lo, hi = g * GB * CH, (g + 1) * GB * CH
                p_c = jnp.exp(jnp.where(mask_c, _bdot_nt(q3, k3) * scale + bias_c, NEG) - ls)
                ds_c = ((p_c * (_bdot_nt(do3, v3) - dl)) * scale).astype(BF16)
                dq3 = _bdot(ds_c, k3)
                dkd[lo:hi, :] = _bdot_tn(ds_c, q3).reshape(GB * CH, AHD)
                dvd[lo:hi, :] = _bdot_tn(p_c.astype(BF16), do3).reshape(GB * CH, AHD)
                if nb > 1:
                    kp3, vp3 = _blocks(kps, g), _blocks(vps, g)
                    p_p = jnp.exp(jnp.where(jnp.logical_and(mask_p, _has_prev(g, nb)),
                                            _bdot_nt(q3, kp3) * scale + bias_p, NEG) - ls)
                    ds_p = ((p_p * (_bdot_nt(do3, vp3) - dl)) * scale).astype(BF16)
                    dq3 = dq3 + _bdot(ds_p, kp3)
                    dkp = _bdot_tn(ds_p, q3).reshape(GB * CH, AHD)
                    dvp = _bdot_tn(p_p.astype(BF16), do3).reshape(GB * CH, AHD)
                    if g == 0:
                        dkd[0:hi - CH, :] += dkp[CH:, :]
                        dvd[0:hi - CH, :] += dvp[CH:, :]
                    else:
                        dkd[lo - CH:hi - CH, :] += dkp
                        dvd[lo - CH:hi - CH, :] += dvp
                dqd[lo:hi, :] = dq3.reshape(GB * CH, AHD)
            ln = S // d
            for acc, src in ((aq, dqd), (ak, dkd), (av, dvd)):
                if pi == 0:
                    acc[...] = src[...]
                else:
                    acc[...] += _natural_order(src[...], d)

        @pl.when(h > 0)
        def _():
            for cp in out_copies(h - 1):
                cp.wait()

        sq[...] = aq[...].astype(BF16)
        sk[...] = ak[...].astype(BF16)
        sv[...] = av[...].astype(BF16)
        for cp in out_copies(h):
            cp.start()

        @pl.when(h == AH - 1)
        def _():
            for cp in out_copies(h):
                cp.wait()

    def col(off):
        return pl.BlockSpec((S, AHD), lambda h: (0, off + h))

    return pl.pallas_call(
        body, name="attn_bwd", grid=(AH,),
        in_specs=[pl.BlockSpec((None, 8, AHD), lambda h: (h, 0, 0)), col(0), col(AH), col(2 * AH),
                  col(0), col(0), col(0), pl.BlockSpec((2, 3, S, AHD), lambda h: (0, 0, 0, h)),
                  pl.BlockSpec((2, S, AHD), lambda h: (0, 0, h))],
        out_specs=pl.BlockSpec(memory_space=pl.ANY),
        out_shape=jax.ShapeDtypeStruct((S, NDEV * N_IN), BF16),
        scratch_shapes=[pltpu.VMEM((S, AHD), BF16) for _ in range(6)]
        + [pltpu.VMEM((S, AHD), F32) for _ in range(8)]
        + [pltpu.VMEM((S, AHD), BF16) for _ in range(3)] + [pltpu.SemaphoreType.DMA((3,))],
        compiler_params=_cp(("arbitrary",)),
    )(_attn_consts(), proj, proj, proj, dmixed, o, lse, qkvp, lsep)


def _ret_consts():
    c = np.zeros((RH, 8, RHD), np.float32)
    for h in range(RH):
        c[h, :, :] = np.log(np.float32(1.0) - np.float32(2.0 ** (-5.0 - h)))
    return jnp.asarray(c)


def _ret_factors(lg):
    i = lax.broadcasted_iota(jnp.int32, (CH, CH), 0)
    j = lax.broadcasted_iota(jnp.int32, (CH, CH), 1)
    dif = (i - j).astype(F32)
    decay = jnp.where(dif >= 0, jnp.exp(lg[:, 0:CH] * jnp.maximum(dif, 0.0)), 0.0)
    row = lax.broadcasted_iota(jnp.int32, (CH, RHD), 0).astype(F32)
    zeta = jnp.exp(lg * (CH - 1.0 - row))
    xi = jnp.exp(lg * (row + 1.0))
    return decay, zeta, xi, jnp.exp(lg * float(CH))


CBK = 8
RSTEPS = NB // CBK


def _ret_specs(rev):
    off = 3 * AH * AHD // RHD
    rows = CBK * CH

    def ch(n):
        return (RSTEPS - 1 - n) if rev else n

    def col(k):
        return pl.BlockSpec((rows, RHD), lambda h, n: (ch(n), off + k * RH + h))

    own = pl.BlockSpec((rows, RHD), lambda h, n: (ch(n), h))
    state = pl.BlockSpec((None, CBK, RHD, RHD), lambda h, n: (h, ch(n), 0, 0))
    const = pl.BlockSpec((None, 8, RHD), lambda h, n: (h, 0, 0))
    dm = pl.BlockSpec((rows, RHD), lambda h, n: (ch(n), AH * AHD // RHD + h))
    return col, own, state, const, dm


def _chunks(x):
    return x.reshape(CBK, CH, RHD)


def _ret_fwd(proj):
    def body(c_ref, q_ref, k_ref, v_ref, g_ref, ret_ref, mr_ref, st_ref, r_acc):
        n = pl.program_id(1)

        @pl.when(n == 0)
        def _():
            r_acc[...] = jnp.zeros_like(r_acc)

        decay, zeta, xi, gch = _ret_factors(c_ref[0:1, :])
        q3 = _chunks(q_ref[...].astype(BF16))
        kc = _chunks(k_ref[...] * (1.0 / math.sqrt(RHD)))
        k3 = kc.astype(BF16)
        v3 = _chunks(v_ref[...].astype(BF16))
        kv3 = _bdot_tn((kc * zeta[None]).astype(BF16), v3)
        r = r_acc[...]
        for i in range(CBK):
            st_ref[i] = r.astype(BF16)
            r = r * gch + kv3[i]
        r_acc[...] = r
        scores = _bdot_nt(q3, k3) * decay[None]
        ret = (_bdot(scores.astype(BF16), v3) + _bdot(q3, st_ref[...]) * xi[None]).reshape(CBK * CH, RHD)
        ret_ref[...] = ret
        rr = lax.rsqrt(jnp.mean(ret * ret, axis=-1, keepdims=True) + EPS)
        gv = g_ref[...]
        mr_ref[...] = ((gv * _sigmoid(gv)) * (ret * rr)).astype(BF16)

    col, own, state, const, _ = _ret_specs(False)
    return pl.pallas_call(
        body, name="ret_fwd", grid=(RH, RSTEPS),
        in_specs=[const, col(0), col(1), col(2), col(3)],
        out_specs=[own, own, state],
        out_shape=[jax.ShapeDtypeStruct((S, RH * RHD), F32), jax.ShapeDtypeStruct((S, RH * RHD), BF16),
                   jax.ShapeDtypeStruct((RH, NB, RHD, RHD), BF16)],
        scratch_shapes=[pltpu.VMEM((RHD, RHD), F32)],
        compiler_params=_cp(("parallel", "arbitrary")),
    )(_ret_consts(), proj, proj, proj, proj)


def _ret_bwd(proj, ret, states, dmixed, dproj):
    rows = CBK * CH
    col0 = 3 * AH * AHD

    def body(c_ref, q_ref, k_ref, v_ref, g_ref, ret_ref, st_ref, dm_ref, dproj_in, dproj_hbm, g_acc, gs,
             sq, sk, sv, sg, sems):
        del dproj_in
        h, n = pl.program_id(0), pl.program_id(1)
        step = h * RSTEPS + n

        def out_copies(t):
            hh, nn = t // RSTEPS, t % RSTEPS
            r0 = pl.multiple_of((RSTEPS - 1 - nn) * rows, rows)
            return [pltpu.make_async_copy(
                st, dproj_hbm.at[pl.ds(r0, rows), pl.ds(pl.multiple_of(col0 + (k * RH + hh) * RHD, RHD), RHD)],
                sems.at[k]) for k, st in enumerate((sq, sk, sv, sg))]

        @pl.when(n == 0)
        def _():
            g_acc[...] = jnp.zeros_like(g_acc)

        decay, zeta, xi, gch = _ret_factors(c_ref[0:1, :])
        ret_v = ret_ref[...]
        rr = lax.rsqrt(jnp.mean(ret_v * ret_v, axis=-1, keepdims=True) + EPS)
        gv = g_ref[...]
        sgm = _sigmoid(gv)
        dmix = dm_ref[...]
        dgate = ((dmix * (ret_v * rr)) * (sgm * (1.0 + gv * (1.0 - sgm)))).astype(BF16)
        dretn = dmix * (gv * sgm)
        dret = _chunks(rr * dretn - ret_v * ((rr * rr * rr) * jnp.mean(dretn * ret_v, axis=-1, keepdims=True)))

        q3 = _chunks(q_ref[...].astype(BF16))
        kc = _chunks(k_ref[...] * (1.0 / math.sqrt(RHD)))
        k3 = kc.astype(BF16)
        v3 = _chunks(v_ref[...].astype(BF16))
        d3 = dret.astype(BF16)
        dxi = (dret * xi[None]).astype(BF16)
        kz = (kc * zeta[None]).astype(BF16)
        dr3 = _bdot_tn(q3, dxi)
        acc = g_acc[...]
        for i in reversed(range(CBK)):
            gs[i] = acc.astype(BF16)
            acc = dr3[i] + gch * acc
        g_acc[...] = acc
        g3 = gs[...]
        sc = (_bdot_nt(q3, k3) * decay[None]).astype(BF16)
        da = (_bdot_nt(d3, v3) * decay[None]).astype(BF16)
        dq = _bdot(da, k3) + _bdot_nt(dxi, st_ref[...])
        dkc = _bdot_tn(da, q3) + _bdot_nt(v3, g3) * zeta[None]
        dv = _bdot_tn(sc, d3) + _bdot(kz, g3)

        @pl.when(step > 0)
        def _():
            for cp in out_copies(step - 1):
                cp.wait()

        sq[...] = dq.reshape(rows, RHD).astype(BF16)
        sk[...] = (dkc * (1.0 / math.sqrt(RHD))).reshape(rows, RHD).astype(BF16)
        sv[...] = dv.reshape(rows, RHD).astype(BF16)
        sg[...] = dgate
        for cp in out_copies(step):
            cp.start()

        @pl.when(step == RH * RSTEPS - 1)
        def _():
            for cp in out_copies(step):
                cp.wait()

    col, own, state, const, dm = _ret_specs(True)
    hbm = pl.BlockSpec(memory_space=pl.ANY)
    return pl.pallas_call(
        body, name="ret_bwd", grid=(RH, RSTEPS),
        in_specs=[const, col(0), col(1), col(2), col(3), own, state, dm, hbm],
        out_specs=hbm,
        out_shape=jax.ShapeDtypeStruct(dproj.shape, dproj.dtype),
        input_output_aliases={8: 0},
        scratch_shapes=[pltpu.VMEM((RHD, RHD), F32), pltpu.VMEM((CBK, RHD, RHD), BF16)]
        + [pltpu.VMEM((rows, RHD), BF16) for _ in range(4)] + [pltpu.SemaphoreType.DMA((4,))],
        compiler_params=_cp(("arbitrary", "arbitrary")),
    )(_ret_consts(), proj, proj, proj, proj, ret, states, dmixed, dproj)


class _NoReduction:
    def start(self, group, grads):
        pass

    def local(self, name, first=()):
        return []

    def landed(self, name):
        return []

    def update(self, name):
        return []

    place = None

    def rider(self, name):
        return None

    def set_update(self, name, outs):
        pass


def _local_step(x, tgt, nw1, nw2, nw3, win, wout, wg, wu, wd, red=None):
    red = red or _NoReduction()

    def after(values, first):
        return lax.optimization_barrier((tuple(values), tuple(first)))[0]

    wg, wu, wd = (w.reshape(NFG, N_FG, D) for w in (wg, wu, wd))
    h1, r1 = _rms_fwd(x, nw1)
    proj = _proj(h1, win)
    o, ma, lse, qkvp, lsep = _attn_fwd(proj)
    ret, mr, states = _ret_fwd(proj)
    x2, h2, r2 = _out_proj_rms(x, ma, mr, wout, nw2)
    a, dadg, dadu = _ffn_up(h2, wg, wu)
    dx3, dx3b, st3 = _ffn_down_loss(x2, a, wd, nw3, tgt)

    dwd = _wgrad_rows(a, dx3b, "wgrad_down")
    red.start(["w_down"], [dwd])
    (dx3b,) = after([dx3b], [dwd])
    part = _ffn_down_bwd(dx3b, wd, dadg, dadu, 0)
    (dx3b,) = after([dx3b], red.local("w_down", first=part))
    dg, du = _ffn_down_bwd(dx3b, wd, dadg, dadu, 1, part)
    dwg = _wgrad_rows(dg, h2, "wgrad_gate")
    red.start(["w_gate"], [dwg])
    (du,) = after([du], [dwg])
    dwu = _wgrad_rows(du, h2, "wgrad_up")
    red.start(["w_up"], [dwu])
    dg, du = after([dg, du], red.local("w_gate", first=[dwu] + red.landed("w_down")))
    dx2, dx2b, st2 = _ffn_up_bwd(dg, du, wg, wu, dx3, x2, r2, nw2)
    (dx2b,) = after([dx2b], red.local("w_up", first=[dx2b]))
    dwo = _wgrad_out(ma, mr, dx2b)
    red.start(["w_out"], [dwo])
    (dx2b,) = after([dx2b], [dwo])
    dmixed, done = _out_proj_bwd(dx2b, wout, red.place, red.rider("w_down"))
    red.set_update("w_down", done)
    dproj = _attn_bwd(proj, dmixed, o, lse, qkvp, lsep)
    (dmixed,) = after([dmixed], red.local("w_out", first=[dproj] + red.landed("w_gate")))
    dproj = _ret_bwd(proj, ret, states, dmixed, dproj)
    (dwi0,) = after([_wgrad_in(h1, dproj, 0)], red.landed("w_up"))
    red.start(["w_in_0"], [dwi0])
    (dproj,) = after([dproj], [dwi0])
    dwi1 = _wgrad_in(h1, dproj, 1)
    red.start(["w_in_1"], [dwi1])
    sums = red.local("w_in_0", first=[dwi1] + red.landed("w_out"))
    sums = red.local("w_in_1", first=sums + red.update("w_gate"))
    (dproj,) = after([dproj], sums)
    gx, st1 = _in_proj_bwd(dproj, win, dx2, x, r1, nw1)
    dwi = jnp.concatenate([dwi0, dwi1], axis=1)
    stats = jnp.concatenate([st1[0:1], st2[0:1], st3[0:2], jnp.zeros((4, D), F32)], axis=0)
    return stats, gx, dwi, dwo, dwg, dwu, dwd


def _place():
    x, y, c = lax.axis_index("x"), lax.axis_index("y"), lax.axis_index("c")
    return x, y, c, [(1 - x, y), (x, 1 - y), (1 - x, 1 - y)]


def _handshake(peers):
    barrier = pltpu.get_barrier_semaphore()
    for peer in peers:
        pl.semaphore_signal(barrier, inc=1, device_id=peer, device_id_type=MESH)
    pl.semaphore_wait(barrier, len(peers))


def _all_gather(shards, name, collective_id):
    na = len(shards)
    SIB, XN0, XN1, YN1, YN0, VIA_X, VIA_Y = 0, 1, 2, 3, 4, 5, 6
    D2D = {XN0: 7, XN1: 8, YN1: 9, YN0: 10, VIA_X: 11, VIA_Y: 12}

    def body(*refs):
        ins, outs = refs[:na], refs[na:2 * na]
        send_sems, recv_sems, local_sems = refs[2 * na:]
        x, y, c, _ = _place()
        me, sib = (x, y, c), (x, y, 1 - c)
        xn, yn, dg = (1 - x, y, c), (x, 1 - y, c), (1 - x, 1 - y, c)
        _handshake([sib, xn, yn])

        def part(ref, h):
            rows = ref.shape[0] // 2
            return ref if h is None else ref.at[pl.ds(h * rows, rows)]

        def block(a, owner, h):
            return part(outs[a].at[4 * owner[0] + 2 * owner[1] + owner[2]], h)

        def copy(a, k, owner, h, to, own_src=False):
            return pltpu.make_async_remote_copy(
                src_ref=part(ins[a], h) if own_src else block(a, owner, h), dst_ref=block(a, owner, h),
                send_sem=send_sems.at[a, k], recv_sem=recv_sems.at[a, k], device_id=to, device_id_type=MESH)

        def other(p):
            return (p[0], p[1], 1 - c)

        mine = [pltpu.make_async_copy(ins[a], block(a, me, None), local_sems.at[a]) for a in range(na)]
        for cp in mine:
            cp.start()
        sent = []
        for a in range(na):
            sent += [copy(a, XN0, me, 0, xn, True), copy(a, YN1, me, 1, yn, True),
                     copy(a, XN1, me, 1, xn, True), copy(a, YN0, me, 0, yn, True)]
        sent += [copy(a, SIB, me, None, sib, True) for a in range(na)]
        for cp in sent:
            cp.start()

        def landed(a, k, owner, h, then):
            copy(a, k, owner, h, me).wait_recv()
            for k2, to in then + [(D2D[k], sib)]:
                cp = copy(a, k2, owner, h, to)
                cp.start()
                sent.append(cp)

        for a in range(na):
            landed(a, XN0, xn, 0, [(VIA_Y, yn)])
            landed(a, YN1, yn, 1, [(VIA_X, xn)])
            landed(a, XN1, xn, 1, [])
            landed(a, YN0, yn, 0, [])
        for a in range(na):
            landed(a, VIA_Y, dg, 0, [])
            landed(a, VIA_X, dg, 1, [])
        for a in range(na):
            copy(a, SIB, sib, None, me).wait_recv()
            for k, owner, h in ((XN0, xn, 0), (XN1, xn, 1), (YN1, yn, 1), (YN0, yn, 0), (VIA_Y, dg, 0), (VIA_X, dg, 1)):
                copy(a, D2D[k], other(owner), h, me).wait_recv()
        for cp in sent:
            cp.wait_send()
        for cp in mine:
            cp.wait()

    return _sequencer_call(
        body, name, collective_id,
        [jax.ShapeDtypeStruct((NDEV,) + s.shape, s.dtype) for s in shards],
        [pltpu.SemaphoreType.DMA((na, 13)), pltpu.SemaphoreType.DMA((na, 13)), pltpu.SemaphoreType.DMA((na,))])(*shards)


def _sequencer_call(body, name, collective_id, out_type, scratch_types):
    return pl.kernel(
        body, name=name, out_type=out_type,
        mesh=plsc.ScalarSubcoreMesh(axis_name="sequencer", num_cores=1),
        scratch_types=scratch_types,
        compiler_params=pltpu.CompilerParams(collective_id=collective_id))


def _exchange_sibling(grads, name, collective_id):
    na = len(grads)

    def body(*refs):
        ins, outs = refs[:na], refs[na:2 * na]
        send_sems, recv_sems = refs[2 * na:]
        x, y, c, _ = _place()
        _handshake([(x, y, 1 - c)])
        cps = []
        for a in range(na):
            for k in range(4):
                cps.append(pltpu.make_async_remote_copy(
                    src_ref=ins[a].at[2 * k + (1 - c)], dst_ref=outs[a].at[k],
                    send_sem=send_sems.at[a, k], recv_sem=recv_sems.at[a, k],
                    device_id=(x, y, 1 - c), device_id_type=MESH))
        for cp in cps:
            cp.start()
        for cp in cps:
            cp.wait()

    return _sequencer_call(
        body, name, collective_id,
        [jax.ShapeDtypeStruct((4,) + g.shape[1:], g.dtype) for g in grads],
        [pltpu.SemaphoreType.DMA((na, 4)), pltpu.SemaphoreType.DMA((na, 4))])(*grads)


def _row_tile(rows, cols):
    for t in (512, 256, 176, 128, 64, 32, 16):
        if rows % t == 0 and t * cols * 4 <= (2 << 20):
            return t
    raise ValueError((rows, cols))


def _chip_sum(place, g, got, name):
    _, r, c = g.shape
    tm = r

    def body(pos_ref, g_ref, got_ref, o_ref):
        o_ref[...] = (g_ref[...].astype(F32) + got_ref[...].astype(F32)).astype(BF16)

    def chip(j, pos):
        return 2 * (pos[0] ^ jnp.where(j == 1, 0, 1)) + (pos[1] ^ jnp.where(j == 0, 0, 1))

    return pl.pallas_call(
        body, name=name,
        grid_spec=pltpu.PrefetchScalarGridSpec(
            num_scalar_prefetch=1, grid=(3, r // tm),
            in_specs=[pl.BlockSpec((None, tm, c), lambda j, i, pos: (2 * chip(j, pos) + pos[2], i, 0)),
                      pl.BlockSpec((None, tm, c), lambda j, i, pos: (chip(j, pos), i, 0))],
            out_specs=pl.BlockSpec((None, tm, c), lambda j, i, pos: (j, i, 0))),
        out_shape=jax.ShapeDtypeStruct((3, r, c), BF16),
        compiler_params=_cp(("parallel", "parallel")),
    )(place, g, got)


def _exchange_chips(sums, name, collective_id):
    na = len(sums)

    def body(*refs):
        ins, outs = refs[:na], refs[na:2 * na]
        send_sems, recv_sems = refs[2 * na:]
        x, y, c, chips = _place()
        _handshake([(*chip, c) for chip in chips])
        cps = []
        for a in range(na):
            for j, chip in enumerate(chips):
                cps.append(pltpu.make_async_remote_copy(
                    src_ref=ins[a].at[j], dst_ref=outs[a].at[j],
                    send_sem=send_sems.at[a, j], recv_sem=recv_sems.at[a, j],
                    device_id=(*chip, c), device_id_type=MESH))
        for cp in cps:
            cp.start()
        for cp in cps:
            cp.wait()

    return _sequencer_call(
        body, name, collective_id,
        [jax.ShapeDtypeStruct((3,) + s.shape[1:], s.dtype) for s in sums],
        [pltpu.SemaphoreType.DMA((na, 3)), pltpu.SemaphoreType.DMA((na, 3))])(*sums)


def _exchange_stats(stats, collective_id):
    def body(st_in, st_out, st_send, st_recv, local_sem):
        x, y, c, _ = _place()
        me_idx = 4 * x + 2 * y + c
        peers = [(x ^ ((k >> 2) & 1), y ^ ((k >> 1) & 1), c ^ (k & 1)) for k in range(1, 8)]
        _handshake(peers)
        mine = pltpu.make_async_copy(st_in, st_out.at[me_idx], local_sem)
        mine.start()
        cps = [pltpu.make_async_remote_copy(
            src_ref=st_in, dst_ref=st_out.at[me_idx], send_sem=st_send.at[k], recv_sem=st_recv.at[k],
            device_id=peer, device_id_type=MESH) for k, peer in enumerate(peers)]
        for cp in cps:
            cp.start()
        for cp in cps:
            cp.wait()
        mine.wait()

    return _sequencer_call(
        body, "exchange_stats", collective_id,
        jax.ShapeDtypeStruct((NDEV,) + stats.shape, stats.dtype),
        [pltpu.SemaphoreType.DMA((7,)), pltpu.SemaphoreType.DMA((7,)), pltpu.SemaphoreType.DMA])(stats)


class _Reduction:
    def __init__(self, place, first_collective_id, state):
        self.place = place
        self.ids = iter(range(first_collective_id, 32))
        self.state = state
        self.groups = {}
        self.updates = {}

    def next_id(self):
        return next(self.ids)

    def start(self, group, grads):
        got = _exchange_sibling(grads, "sibling_exchange_" + group[0], self.next_id())
        self.groups[group[0]] = dict(names=group, grads=grads, got=got)

    def local(self, name, first=()):
        grp = self.groups[name]
        grads = lax.optimization_barrier((tuple(grp["grads"]), tuple(first)))[0]
        grp["sums"] = [_chip_sum(self.place, g, s, "chip_sum_" + n)
                       for g, s, n in zip(grads, grp["got"], grp["names"])]
        grp["chips"] = _exchange_chips(grp["sums"], "chip_exchange_" + name, self.next_id())
        return grp["sums"]

    def landed(self, name):
        return list(self.groups[name]["chips"])

    def rider(self, name):
        grp = next(g for g in self.groups.values() if name in g["names"])
        k = grp["names"].index(name)
        return self.state[name][:3] + (grp["grads"][k], grp["got"][k], grp["chips"][k])

    def set_update(self, name, outs):
        self.updates[name] = list(outs)

    def update(self, name):
        if name not in self.updates:
            grp = next(g for g in self.groups.values() if name in g["names"])
            k = grp["names"].index(name)
            w, m, v, part, parts = self.state[name]
            before = self.update(f"{name[:-1]}{part - 1}") if part else None
            self.updates[name] = _shard_update(self.place, w, m, v, grp["grads"][k], grp["got"][k],
                                               grp["chips"][k], "update_" + name, part, parts, before)
        return list(self.updates[name])


def _adamw(w, g, m, v):
    m = ADAM_B1 * m + (1.0 - ADAM_B1) * g
    v = ADAM_B2 * v + (1.0 - ADAM_B2) * (g * g)
    m_hat = m / (1.0 - ADAM_B1 ** ADAM_STEP)
    v_hat = v / (1.0 - ADAM_B2 ** ADAM_STEP)
    delta = -ADAM_LR * (m_hat / (jnp.sqrt(v_hat) + ADAM_EPS) + ADAM_WD * w)
    return delta, m, v


def _update_tile(w_ref, m_ref, v_ref, g_ref, s_ref, c_ref, go_ref, d_ref, mo_ref, vo_ref):
    grad = g_ref[...].astype(F32) + s_ref[...].astype(F32)
    for j in range(3):
        grad = grad + c_ref[j].astype(F32)
    delta, mn, vn = _adamw(w_ref[...], grad, m_ref[...], v_ref[...])
    go_ref[...] = grad
    d_ref[...] = delta
    mo_ref[...] = mn
    vo_ref[...] = vn


def _shard_update(place, w, m, v, g, got_sib, got_chips, name, part=0, parts=1, before=None):
    r, c = w.shape
    rp = r // parts
    tm = _row_tile(rp, c)
    off = part * (rp // tm)

    def body(pos_ref, w_ref, m_ref, v_ref, g_ref, s_ref, c_ref, *rest):
        _update_tile(w_ref, m_ref, v_ref, g_ref, s_ref, c_ref, *rest[-4:])

    row = pl.BlockSpec((tm, c), lambda i, pos: (i + off, 0))
    before = list(before or [])
    return pl.pallas_call(
        body, name=name,
        grid_spec=pltpu.PrefetchScalarGridSpec(
            num_scalar_prefetch=1, grid=(rp // tm,),
            in_specs=[row, row, row,
                      pl.BlockSpec((None, tm, c), lambda i, pos: (4 * pos[0] + 2 * pos[1] + pos[2], i, 0)),
                      pl.BlockSpec((None, tm, c), lambda i, pos: (2 * pos[0] + pos[1], i, 0)),
                      pl.BlockSpec((3, tm, c), lambda i, pos: (0, i, 0))]
            + [pl.BlockSpec(memory_space=pl.ANY)] * len(before),
            out_specs=[row, row, row, row]),
        out_shape=[jax.ShapeDtypeStruct((r, c), F32)] * 4,
        input_output_aliases={7 + k: k for k in range(len(before))},
        compiler_params=_cp(("parallel",)),
    )(place, w, m, v, g, got_sib, got_chips, *before)


def _small_update(stats_all, ws, ms, vs):
    def body(st_ref, w_ref, m_ref, v_ref, go_ref, d_ref, mo_ref, vo_ref):
        grad = st_ref[0]
        for k in range(1, NDEV):
            grad = grad + st_ref[k]
        delta, mn, vn = _adamw(w_ref[...], grad, m_ref[...], v_ref[...])
        go_ref[...] = grad
        d_ref[...] = delta
        mo_ref[...] = mn
        vo_ref[...] = vn

    return pl.pallas_call(
        body, name="small_update",
        out_shape=[jax.ShapeDtypeStruct((8, D), F32)] * 4,
        compiler_params=_cp(),
    )(stats_all, ws, ms, vs)


def kernel(x, norm_mix_w, w_in, w_out, norm_ffn_w, w_gate, w_up, w_down, norm_final_w, loss_target, m_norm_mix_w, m_w_in, m_w_out, m_norm_ffn_w, m_w_gate, m_w_up, m_w_down, m_norm_final_w, v_norm_mix_w, v_w_in, v_w_out, v_norm_ffn_w, v_w_gate, v_w_up, v_w_down, v_norm_final_w):
    tr = {"w_gate", "w_up"}
    names = ["w_in", "w_out", "w_gate", "w_up", "w_down"]

    def view(a, n):
        return a[0].T if n in tr else a[0]

    big_w = [view(a, n) for a, n in zip([w_in, w_out, w_gate, w_up, w_down], names)]
    big_m = [view(a, n) for a, n in zip([m_w_in, m_w_out, m_w_gate, m_w_up, m_w_down], names)]
    big_v = [view(a, n) for a, n in zip([v_w_in, v_w_out, v_w_gate, v_w_up, v_w_down], names)]

    shards = [_cast_bf16(w, "cast_" + n) for w, n in zip(big_w, names)]
    (win,) = _all_gather(shards[0:1], "all_gather_w_in", 1)
    (wout,) = _all_gather(shards[1:2], "all_gather_w_out", 2)
    wg, wu = _all_gather(shards[2:4], "all_gather_gate_up", 3)
    (wd,) = _all_gather(shards[4:5], "all_gather_w_down", 4)
    nw3 = norm_final_w.reshape(1, D)
    place = jnp.stack([lax.axis_index("x"), lax.axis_index("y"), lax.axis_index("c")]).astype(jnp.int32)
    state = {n: (w, m, v, 0, 1) for n, w, m, v in zip(names, big_w, big_m, big_v)}
    for part in range(W_IN_PARTS):
        state[f"w_in_{part}"] = state["w_in"][:3] + (part, W_IN_PARTS)
    red = _Reduction(place, 5, state)
    stats, gx, *_ = _local_step(
        x[0], loss_target[0], norm_mix_w, norm_ffn_w, nw3, win, wout.reshape(D, D), wg, wu, wd, red)
    stats_all = _exchange_stats(stats, red.next_id())
    upd = [red.update(f"w_in_{W_IN_PARTS - 1}" if n == "w_in" else n) for n in names]
    stats_all = lax.optimization_barrier((stats_all, tuple(upd[0])))[0]

    def rows(a, b, c):
        return jnp.concatenate([a.reshape(1, D), b.reshape(1, D), c.reshape(1, D), jnp.zeros((5, D), F32)], axis=0)

    sg, sd, sm, sv = _small_update(stats_all, rows(norm_mix_w, norm_ffn_w, norm_final_w),
                                   rows(m_norm_mix_w, m_norm_ffn_w, m_norm_final_w),
                                   rows(v_norm_mix_w, v_norm_ffn_w, v_norm_final_w))
    loss = sg[3, 0]

    def outs(k, small):
        big = [(u[k].T if n in tr else u[k])[None] for u, n in zip(upd, names)]
        return [small[0:1], big[0], big[1], small[1:2], big[2], big[3], big[4], small[2]]

    return (loss, gx[None], *outs(0, sg), *outs(1, sd), *outs(2, sm), *outs(3, sv))
```

```python
import math

import numpy as np
import jax
import jax.numpy as jnp
from jax import lax
from jax.experimental import pallas as pl
from jax.experimental.pallas import tpu as pltpu
from jax.experimental.pallas import tpu_sc as plsc

F32 = jnp.float32
BF16 = jnp.bfloat16

S = 2048
D = 2048
NDEV = 8
N_IN = 7168 // NDEV
N_FF = 5632 // NDEV
NFG, N_FG = NDEV // 2, 2 * N_FF
N_OUT = 2048 // NDEV
AH, AHD = 8, 128
RH, RHD = 4, 256
CH = 128
NB = S // CH
EPS = 1e-6
PATTERNS = ((1, 16), (4, 4), (16, 1))
NEG = -1e30
VMEM_LIMIT = 56 * 1024 * 1024

ADAM_LR, ADAM_B1, ADAM_B2, ADAM_EPS, ADAM_WD, ADAM_STEP = 0.001, 0.9, 0.999, 1e-08, 0.01, 10
MESH = pl.DeviceIdType.MESH


def _cp(sem=None):
    return pltpu.CompilerParams(dimension_semantics=sem, vmem_limit_bytes=VMEM_LIMIT)


def _dot(a, b):
    return jnp.dot(a, b, preferred_element_type=F32)


def _dot_nt(a, b):
    return lax.dot_general(a, b, (((1,), (1,)), ((), ())), preferred_element_type=F32)


def _dot_tn(a, b):
    return lax.dot_general(a, b, (((0,), (0,)), ((), ())), preferred_element_type=F32)


def _sigmoid(x):
    return 0.5 * jnp.tanh(0.5 * x) + 0.5


def _cast_bf16(w, name):
    r, c = w.shape
    tm = r if r <= 1024 else 512

    def body(w_ref, o_ref):
        o_ref[...] = w_ref[...].astype(BF16)

    return pl.pallas_call(
        body, name=name, grid=(r // tm,),
        in_specs=[pl.BlockSpec((tm, c), lambda i: (i, 0))],
        out_specs=pl.BlockSpec((tm, c), lambda i: (i, 0)),
        out_shape=jax.ShapeDtypeStruct((r, c), BF16),
        compiler_params=_cp(("parallel",)),
    )(w)


def _rms_fwd(x, nw):
    tm = 256

    def body(x_ref, w_ref, h_ref, r_ref):
        xs = x_ref[...]
        r = lax.rsqrt(jnp.mean(xs * xs, axis=-1, keepdims=True) + EPS)
        h_ref[...] = ((xs * r) * w_ref[...]).astype(BF16)
        r_ref[...] = r

    return pl.pallas_call(
        body, name="rms_fwd", grid=(S // tm,),
        in_specs=[pl.BlockSpec((tm, D), lambda i: (i, 0)), pl.BlockSpec((1, D), lambda i: (0, 0))],
        out_specs=[pl.BlockSpec((tm, D), lambda i: (i, 0)), pl.BlockSpec((tm, 1), lambda i: (i, 0))],
        out_shape=[jax.ShapeDtypeStruct((S, D), BF16), jax.ShapeDtypeStruct((S, 1), F32)],
        compiler_params=_cp(("parallel",)),
    )(x, nw)


def _row_copies(hbm_refs, bufs, sems, m, tm):
    rows = pl.ds(pl.multiple_of(m * tm, tm), tm)
    return [pltpu.make_async_copy(h.at[rows], b, sems.at[i]) for i, (h, b) in enumerate(zip(hbm_refs, bufs))]


def _rms_bwd_tile(dh, xs, r, nw):
    dnw = jnp.sum(dh * (xs * r), axis=0, keepdims=True)
    gy = dh * nw
    dx = r * gy - xs * ((r * r * r) * jnp.mean(gy * xs, axis=-1, keepdims=True))
    return dx, dnw


def _proj(h1, win):
    tm = 1024

    def body(a_ref, w_ref, o_ref):
        o_ref[...] = _dot(a_ref[...], w_ref[...])

    return pl.pallas_call(
        body, name="proj", grid=(NDEV, S // tm),
        in_specs=[pl.BlockSpec((tm, D), lambda p, m: (m, 0)),
                  pl.BlockSpec((None, D, N_IN), lambda p, m: (p, 0, 0))],
        out_specs=pl.BlockSpec((tm, N_IN), lambda p, m: (m, p)),
        out_shape=jax.ShapeDtypeStruct((S, NDEV * N_IN), F32),
        compiler_params=_cp(("parallel", "parallel")),
    )(h1, win)


def _out_proj_rms(x, ma, mr, wout, nw):
    tm = 256
    half = D // 2

    def body(x_ref, ma_ref, mr_ref, w_ref, nw_ref, x2_ref, h_ref, r_ref):
        acc = _dot(ma_ref[...], w_ref[0:half, :]) + _dot(mr_ref[...], w_ref[half:D, :])
        x2 = x_ref[...] + acc
        r = lax.rsqrt(jnp.mean(x2 * x2, axis=-1, keepdims=True) + EPS)
        x2_ref[...] = x2
        h_ref[...] = ((x2 * r) * nw_ref[...]).astype(BF16)
        r_ref[...] = r

    return pl.pallas_call(
        body, name="out_proj_rms", grid=(S // tm,),
        in_specs=[pl.BlockSpec((tm, D), lambda i: (i, 0)),
                  pl.BlockSpec((tm, half), lambda i: (i, 0)),
                  pl.BlockSpec((tm, half), lambda i: (i, 0)),
                  pl.BlockSpec((D, D), lambda i: (0, 0)),
                  pl.BlockSpec((1, D), lambda i: (0, 0))],
        out_specs=[pl.BlockSpec((tm, D), lambda i: (i, 0)), pl.BlockSpec((tm, D), lambda i: (i, 0)),
                   pl.BlockSpec((tm, 1), lambda i: (i, 0))],
        out_shape=[jax.ShapeDtypeStruct((S, D), F32), jax.ShapeDtypeStruct((S, D), BF16),
                   jax.ShapeDtypeStruct((S, 1), F32)],
        compiler_params=_cp(("parallel",)),
    )(x, ma, mr, wout, nw)


def _ffn_gate(h2, wg):
    tm = 1024

    def body(h_ref, wg_ref, g_ref):
        g_ref[...] = _dot_nt(h_ref[...], wg_ref[...])

    return pl.pallas_call(
        body, name="ffn_gate", grid=(NFG, S // tm),
        in_specs=[pl.BlockSpec((tm, D), lambda p, m: (m, 0)),
                  pl.BlockSpec((None, N_FG, D), lambda p, m: (p, 0, 0))],
        out_specs=pl.BlockSpec((None, tm, N_FG), lambda p, m: (p, m, 0)),
        out_shape=jax.ShapeDtypeStruct((NFG, S, N_FG), F32),
        compiler_params=_cp(("parallel", "parallel")),
    )(h2, wg)


def _ffn_up(h2, g, wu):
    tm = 512

    def body(h_ref, g_ref, wu_ref, a_ref, dadg_ref, dadu_ref):
        g = g_ref[...]
        u = _dot_nt(h_ref[...], wu_ref[...])
        sg = _sigmoid(g)
        silu = g * sg
        a_ref[...] = (silu * u).astype(BF16)
        dadg_ref[...] = (u * (sg * (1.0 + g * (1.0 - sg)))).astype(BF16)
        dadu_ref[...] = silu.astype(BF16)

    blk = pl.BlockSpec((None, tm, N_FG), lambda p, m: (p, m, 0))
    wblk = pl.BlockSpec((None, N_FG, D), lambda p, m: (p, 0, 0))
    return pl.pallas_call(
        body, name="ffn_up", grid=(NFG, S // tm),
        in_specs=[pl.BlockSpec((tm, D), lambda p, m: (m, 0)), blk, wblk],
        out_specs=[blk, blk, blk],
        out_shape=[jax.ShapeDtypeStruct((NFG, S, N_FG), BF16)] * 3,
        compiler_params=_cp(("parallel", "parallel")),
    )(h2, g, wu)


def _ffn_down_loss(x2, a, wd, nw, tgt):
    tm = 512

    def body(x2_hbm, a_ref, w_ref, nw_ref, t_hbm, dx_ref, dxb_ref, st_ref, acc_ref, x2_buf, t_buf, sems):
        m, p = pl.program_id(0), pl.program_id(1)
        tail_in = _row_copies((x2_hbm, t_hbm), (x2_buf, t_buf), sems, m, tm)

        @pl.when(p == 0)
        def _():
            acc_ref[...] = jnp.zeros_like(acc_ref)
            for cp in tail_in:
                cp.start()

        @pl.when((p == 0) & (m == 0))
        def _():
            st_ref[...] = jnp.zeros_like(st_ref)

        acc_ref[...] += _dot(a_ref[...], w_ref[...])

        @pl.when(p == NFG - 1)
        def _():
            for cp in tail_in:
                cp.wait()
            x3 = x2_buf[...] + acc_ref[...]
            nwv = nw_ref[...]
            r = lax.rsqrt(jnp.mean(x3 * x3, axis=-1, keepdims=True) + EPS)
            y = (x3 * r) * nwv
            err = y - t_buf[...]
            loss = 0.5 * jnp.sum(jnp.mean(err * err, axis=-1, keepdims=True), axis=0, keepdims=True)
            dy = err * (1.0 / D)
            dx, dnw = _rms_bwd_tile(dy, x3, r, nwv)
            dx_ref[...] = dx
            dxb_ref[...] = dx.astype(BF16)
            st_ref[0:1, :] += dnw
            st_ref[1:2, :] += jnp.broadcast_to(loss, (1, D))

    return pl.pallas_call(
        body, name="ffn_down_loss", grid=(S // tm, NFG),
        in_specs=[pl.BlockSpec(memory_space=pl.ANY),
                  pl.BlockSpec((None, tm, N_FG), lambda m, p: (p, m, 0)),
                  pl.BlockSpec((None, N_FG, D), lambda m, p: (p, 0, 0)),
                  pl.BlockSpec((1, D), lambda m, p: (0, 0)),
                  pl.BlockSpec(memory_space=pl.ANY)],
        out_specs=[pl.BlockSpec((tm, D), lambda m, p: (m, 0)), pl.BlockSpec((tm, D), lambda m, p: (m, 0)),
                   pl.BlockSpec((8, D), lambda m, p: (0, 0))],
        out_shape=[jax.ShapeDtypeStruct((S, D), F32), jax.ShapeDtypeStruct((S, D), BF16),
                   jax.ShapeDtypeStruct((8, D), F32)],
        scratch_shapes=[pltpu.VMEM((tm, D), F32), pltpu.VMEM((tm, D), F32), pltpu.VMEM((tm, D), F32),
                        pltpu.SemaphoreType.DMA((2,))],
        compiler_params=_cp(("arbitrary", "arbitrary")),
    )(x2, a, wd, nw, tgt)


def _ffn_down_bwd(dx3b, wd, dadg, dadu, part, before=None):
    tm = 1024
    half = NFG // 2

    def body(dx_ref, w_ref, dadg_ref, dadu_ref, *rest):
        dg_ref, du_ref = rest[-2:]
        da = _dot_nt(dx_ref[...], w_ref[...])
        dg_ref[...] = (da * dadg_ref[...].astype(F32)).astype(BF16)
        du_ref[...] = (da * dadu_ref[...].astype(F32)).astype(BF16)

    blk = pl.BlockSpec((None, tm, N_FG), lambda p, m: (p + part * half, m, 0))
    before = list(before or [])
    return pl.pallas_call(
        body, name=f"ffn_down_bwd_{part}", grid=(half, S // tm),
        in_specs=[pl.BlockSpec((tm, D), lambda p, m: (m, 0)),
                  pl.BlockSpec((None, N_FG, D), lambda p, m: (p + part * half, 0, 0)), blk, blk]
        + [pl.BlockSpec(memory_space=pl.ANY)] * len(before),
        out_specs=[blk, blk],
        out_shape=[jax.ShapeDtypeStruct((NFG, S, N_FG), BF16)] * 2,
        input_output_aliases={4 + k: k for k in range(len(before))},
        compiler_params=_cp(("parallel", "parallel")),
    )(dx3b, wd, dadg, dadu, *before)


def _ffn_up_bwd(dg, du, wg, wu, dres, xs, r, nw):
    tm = 512

    def body(dg_ref, du_ref, wg_ref, wu_ref, dres_hbm, x_hbm, r_ref, nw_ref, dx_ref, dxb_ref, st_ref,
             dres_buf, x_buf, sems):
        m, p = pl.program_id(0), pl.program_id(1)
        tail_in = _row_copies((dres_hbm, x_hbm), (dres_buf, x_buf), sems, m, tm)

        @pl.when(p == 0)
        def _():
            dx_ref[...] = jnp.zeros_like(dx_ref)
            for cp in tail_in:
                cp.start()

        @pl.when((p == 0) & (m == 0))
        def _():
            st_ref[...] = jnp.zeros_like(st_ref)

        dx_ref[...] += _dot(dg_ref[...], wg_ref[...])
        dx_ref[...] += _dot(du_ref[...], wu_ref[...])

        @pl.when(p == NFG - 1)
        def _():
            for cp in tail_in:
                cp.wait()
            dx, dnw = _rms_bwd_tile(dx_ref[...], x_buf[...], r_ref[...], nw_ref[...])
            dx = dres_buf[...] + dx
            dx_ref[...] = dx
            dxb_ref[...] = dx.astype(BF16)
            st_ref[0:1, :] += dnw

    blk = pl.BlockSpec((None, tm, N_FG), lambda m, p: (p, m, 0))
    wblk = pl.BlockSpec((None, N_FG, D), lambda m, p: (p, 0, 0))
    row = pl.BlockSpec((tm, D), lambda m, p: (m, 0))
    hbm = pl.BlockSpec(memory_space=pl.ANY)
    return pl.pallas_call(
        body, name="ffn_up_bwd", grid=(S // tm, NFG),
        in_specs=[blk, blk, wblk, wblk, hbm, hbm, pl.BlockSpec((tm, 1), lambda m, p: (m, 0)),
                  pl.BlockSpec((1, D), lambda m, p: (0, 0))],
        out_specs=[row, row, pl.BlockSpec((8, D), lambda m, p: (0, 0))],
        out_shape=[jax.ShapeDtypeStruct((S, D), F32), jax.ShapeDtypeStruct((S, D), BF16),
                   jax.ShapeDtypeStruct((8, D), F32)],
        scratch_shapes=[pltpu.VMEM((tm, D), F32), pltpu.VMEM((tm, D), F32), pltpu.SemaphoreType.DMA((2,))],
        compiler_params=_cp(("arbitrary", "arbitrary")),
    )(dg, du, wg, wu, dres, xs, r, nw)


def _out_proj_bwd(dx2b, wout, place=None, rider=None):
    tm = 256

    if rider is None:
        def body(dx_ref, w_ref, o_ref):
            o_ref[...] = _dot_nt(dx_ref[...], w_ref[...])

        return pl.pallas_call(
            body, name="out_proj_bwd", grid=(S // tm,),
            in_specs=[pl.BlockSpec((tm, D), lambda i: (i, 0)), pl.BlockSpec((D, D), lambda i: (0, 0))],
            out_specs=pl.BlockSpec((tm, D), lambda i: (i, 0)),
            out_shape=jax.ShapeDtypeStruct((S, D), F32),
            compiler_params=_cp(("parallel",)),
        )(dx2b, wout), None

    w = rider[0]
    r, c = w.shape
    rt = _row_tile(r, c)
    nt = r // rt
    assert nt <= S // tm

    def body(pos_ref, dx_ref, w_ref, uw, um, uv, ug, us, uc, o_ref, go, dd, mo, vo):
        o_ref[...] = _dot_nt(dx_ref[...], w_ref[...])

        @pl.when(pl.program_id(0) < nt)
        def _():
            _update_tile(uw, um, uv, ug, us, uc, go, dd, mo, vo)

    def at(i):
        return jnp.minimum(i, nt - 1)

    tile = pl.BlockSpec((rt, c), lambda i, pos: (at(i), 0))
    outs = pl.pallas_call(
        body, name="out_proj_bwd",
        grid_spec=pltpu.PrefetchScalarGridSpec(
            num_scalar_prefetch=1, grid=(S // tm,),
            in_specs=[pl.BlockSpec((tm, D), lambda i, pos: (i, 0)), pl.BlockSpec((D, D), lambda i, pos: (0, 0)),
                      tile, tile, tile,
                      pl.BlockSpec((None, rt, c), lambda i, pos: (4 * pos[0] + 2 * pos[1] + pos[2], at(i), 0)),
                      pl.BlockSpec((None, rt, c), lambda i, pos: (2 * pos[0] + pos[1], at(i), 0)),
                      pl.BlockSpec((3, rt, c), lambda i, pos: (0, at(i), 0))],
            out_specs=[pl.BlockSpec((tm, D), lambda i, pos: (i, 0)), tile, tile, tile, tile]),
        out_shape=[jax.ShapeDtypeStruct((S, D), F32)] + [jax.ShapeDtypeStruct((r, c), F32)] * 4,
        compiler_params=_cp(("arbitrary",)),
    )(place, dx2b, wout, *rider)
    return outs[0], outs[1:]


def _in_proj_bwd(dproj, win, dres, xs, r, nw):
    tm = 1024

    def body(dp_ref, w_ref, dres_hbm, x_hbm, r_ref, nw_ref, dx_ref, st_ref, dres_buf, x_buf, sems):
        m, p = pl.program_id(0), pl.program_id(1)
        tail_in = _row_copies((dres_hbm, x_hbm), (dres_buf, x_buf), sems, m, tm)

        @pl.when(p == 0)
        def _():
            dx_ref[...] = jnp.zeros_like(dx_ref)
            for cp in tail_in:
                cp.start()

        @pl.when((p == 0) & (m == 0))
        def _():
            st_ref[...] = jnp.zeros_like(st_ref)

        dx_ref[...] += _dot_nt(dp_ref[...], w_ref[...])

        @pl.when(p == NDEV - 1)
        def _():
            for cp in tail_in:
                cp.wait()
            dx, dnw = _rms_bwd_tile(dx_ref[...], x_buf[...], r_ref[...], nw_ref[...])
            dx_ref[...] = dres_buf[...] + dx
            st_ref[0:1, :] += dnw

    row = pl.BlockSpec((tm, D), lambda m, p: (m, 0))
    hbm = pl.BlockSpec(memory_space=pl.ANY)
    return pl.pallas_call(
        body, name="in_proj_bwd", grid=(S // tm, NDEV),
        in_specs=[pl.BlockSpec((tm, N_IN), lambda m, p: (m, p)),
                  pl.BlockSpec((None, D, N_IN), lambda m, p: (p, 0, 0)),
                  hbm, hbm, pl.BlockSpec((tm, 1), lambda m, p: (m, 0)),
                  pl.BlockSpec((1, D), lambda m, p: (0, 0))],
        out_specs=[row, pl.BlockSpec((8, D), lambda m, p: (0, 0))],
        out_shape=[jax.ShapeDtypeStruct((S, D), F32), jax.ShapeDtypeStruct((8, D), F32)],
        scratch_shapes=[pltpu.VMEM((tm, D), F32), pltpu.VMEM((tm, D), F32), pltpu.SemaphoreType.DMA((2,))],
        compiler_params=_cp(("arbitrary", "arbitrary")),
    )(dproj, win, dres, xs, r, nw)


W_IN_PARTS = 2


def _wgrad_in(h1, dproj, part):
    rows = D // W_IN_PARTS

    def body(a_ref, d_ref, o_ref):
        both = _dot_tn(a_ref[...], d_ref[...]).astype(BF16)
        o_ref[0] = both[:, 0:N_IN]
        o_ref[1] = both[:, N_IN:2 * N_IN]

    return pl.pallas_call(
        body, name=f"wgrad_in_{part}", grid=(NDEV // 2,),
        in_specs=[pl.BlockSpec((S, rows), lambda p: (0, part)), pl.BlockSpec((S, 2 * N_IN), lambda p: (0, p))],
        out_specs=pl.BlockSpec((2, rows, N_IN), lambda p: (p, 0, 0)),
        out_shape=jax.ShapeDtypeStruct((NDEV, rows, N_IN), BF16),
        compiler_params=_cp(("parallel",)),
    )(h1, dproj)


def _wgrad_rows(a3, dy, name):
    def body(a_ref, d_ref, o_ref):
        o_ref[...] = _dot_tn(a_ref[...], d_ref[...]).astype(BF16)

    return pl.pallas_call(
        body, name=name, grid=(NFG,),
        in_specs=[pl.BlockSpec((None, S, N_FG), lambda p: (p, 0, 0)), pl.BlockSpec((S, D), lambda p: (0, 0))],
        out_specs=pl.BlockSpec((None, N_FG, D), lambda p: (p, 0, 0)),
        out_shape=jax.ShapeDtypeStruct((NFG, N_FG, D), BF16),
        compiler_params=_cp(("parallel",)),
    )(a3, dy).reshape(NDEV, N_FF, D)


def _wgrad_out(ma, mr, dx2b):
    half = D // 2
    per = half // N_OUT

    def body(ma_ref, mr_ref, d_ref, o_ref):
        p = pl.program_id(0)

        @pl.when(p == 0)
        def _():
            o_ref[...] = _dot_tn(ma_ref[...], d_ref[...]).astype(BF16).reshape(per, N_OUT, D)

        @pl.when(p == 1)
        def _():
            o_ref[...] = _dot_tn(mr_ref[...], d_ref[...]).astype(BF16).reshape(per, N_OUT, D)

    whole = pl.BlockSpec((S, half), lambda p: (0, 0))
    return pl.pallas_call(
        body, name="wgrad_out", grid=(2,),
        in_specs=[whole, whole, pl.BlockSpec((S, D), lambda p: (0, 0))],
        out_specs=pl.BlockSpec((per, N_OUT, D), lambda p: (p, 0, 0)),
        out_shape=jax.ShapeDtypeStruct((NDEV, N_OUT, D), BF16),
        compiler_params=_cp(("parallel",)),
    )(ma, mr, dx2b)


def _attn_consts():
    c = np.zeros((AH, 8, AHD), np.float32)
    for h in range(AH):
        c[h, :, :] = 2.0 ** (-(h + 1))
    return jnp.asarray(c)


def _permute_in(dst, src, d, cast=None):
    v = src[...]
    if d > 1:
        v = pltpu.einshape("jrc->rjc", v.reshape(S // d, d, AHD)).reshape(S, AHD)
    dst[...] = v if cast is None else v.astype(cast)


def _natural_order(v, d):
    if d == 1:
        return v
    return pltpu.einshape("rjc->jrc", v.reshape(d, S // d, AHD)).reshape(S, AHD)


def _attn_masks():
    qi = lax.broadcasted_iota(jnp.int32, (CH, CH), 0)
    kj = lax.broadcasted_iota(jnp.int32, (CH, CH), 1)
    dist_c = (qi - kj).astype(F32)
    dist_p = (qi - kj + CH).astype(F32)
    return (qi >= kj)[None], (kj >= qi)[None], dist_c[None], dist_p[None]


GB = 16


def _bdot_nt(a, b):
    return lax.dot_general(a, b, (((2,), (2,)), ((0,), (0,))), preferred_element_type=F32)


def _bdot(a, b):
    return lax.dot_general(a, b, (((2,), (1,)), ((0,), (0,))), preferred_element_type=F32)


def _bdot_tn(a, b):
    return lax.dot_general(a, b, (((1,), (1,)), ((0,), (0,))), preferred_element_type=F32)


def _shift_block(dst, src):
    dst[0:CH, :] = jnp.zeros((CH, AHD), dst.dtype)
    dst[CH:S, :] = src[0:S - CH, :]


def _has_prev(g, nb):
    blk = lax.broadcasted_iota(jnp.int32, (GB, 1, 1), 0) + g * GB
    return (blk & (nb - 1)) != 0


def _blocks(ref, g):
    return ref[g * GB * CH:(g + 1) * GB * CH, :].reshape(GB, CH, AHD)


def _attn_fwd(proj):
    scale = 1.0 / math.sqrt(AHD)

    def body(c_ref, q_ref, k_ref, v_ref, o_ref, ob_ref, lse_ref, qkvp_ref, lsep_ref, qd, kd, vd, kps, vps, od, ld, *nat):
        onat, lnat = nat[0:3], nat[3:6]
        slope = c_ref[0:1, :]
        mask_c, mask_p, dist_c, dist_p = _attn_masks()
        for pi, (d, nb) in enumerate(PATTERNS):
            _permute_in(qd, q_ref, d, BF16)
            _permute_in(kd, k_ref, d, BF16)
            _permute_in(vd, v_ref, d, BF16)
            if d > 1:
                qkvp_ref[pi - 1, 0] = qd[...]
                qkvp_ref[pi - 1, 1] = kd[...]
                qkvp_ref[pi - 1, 2] = vd[...]
            if nb > 1:
                _shift_block(kps, kd)
                _shift_block(vps, vd)
            bias_c = -(slope * float(d)) * dist_c
            bias_p = -(slope * float(d)) * dist_p
            for g in range(NB // GB):
                q3, k3, v3 = _blocks(qd, g), _blocks(kd, g), _blocks(vd, g)
                s_c = jnp.where(mask_c, _bdot_nt(q3, k3) * scale + bias_c, NEG)
                mx = jnp.max(s_c, axis=-1, keepdims=True)
                if nb > 1:
                    kp3, vp3 = _blocks(kps, g), _blocks(vps, g)
                    s_p = jnp.where(jnp.logical_and(mask_p, _has_prev(g, nb)),
                                    _bdot_nt(q3, kp3) * scale + bias_p, NEG)
                    mx = jnp.maximum(mx, jnp.max(s_p, axis=-1, keepdims=True))
                    l = (jnp.sum(jnp.exp(s_c - mx), axis=-1, keepdims=True)
                         + jnp.sum(jnp.exp(s_p - mx), axis=-1, keepdims=True))
                    lse = mx + jnp.log(l)
                    o3 = _bdot(jnp.exp(s_c - lse).astype(BF16), v3) + _bdot(jnp.exp(s_p - lse).astype(BF16), vp3)
                else:
                    l = jnp.sum(jnp.exp(s_c - mx), axis=-1, keepdims=True)
                    lse = mx + jnp.log(l)
                    o3 = _bdot(jnp.exp(s_c - lse).astype(BF16), v3)
                rows = slice(g * GB * CH, (g + 1) * GB * CH)
                od[rows, :] = o3.reshape(GB * CH, AHD)
                ld[rows, :] = jnp.broadcast_to(lse, (GB, CH, AHD)).reshape(GB * CH, AHD)
            onat[pi][...] = _natural_order(od[...], d)
            lnat[pi][...] = _natural_order(ld[...], d)
        l0, l1, l2 = lnat[0][...], lnat[1][...], lnat[2][...]
        mx = jnp.maximum(jnp.maximum(l0, l1), l2)
        e0, e1, e2 = jnp.exp(l0 - mx), jnp.exp(l1 - mx), jnp.exp(l2 - mx)
        den = e0 + e1 + e2
        out = (e0 / den) * onat[0][...] + (e1 / den) * onat[1][...] + (e2 / den) * onat[2][...]
        o_ref[...] = out
        ob_ref[...] = out.astype(BF16)
        lse_ref[...] = mx + jnp.log(den)
        for pi, (d, _) in enumerate(PATTERNS[1:]):
            _permute_in(lsep_ref.at[pi], lse_ref, d)

    def col(off):
        return pl.BlockSpec((S, AHD), lambda h: (0, off + h))

    return pl.pallas_call(
        body, name="attn_fwd", grid=(AH,),
        in_specs=[pl.BlockSpec((None, 8, AHD), lambda h: (h, 0, 0)), col(0), col(AH), col(2 * AH)],
        out_specs=[col(0), col(0), col(0), pl.BlockSpec((2, 3, S, AHD), lambda h: (0, 0, 0, h)),
                   pl.BlockSpec((2, S, AHD), lambda h: (0, 0, h))],
        out_shape=[jax.ShapeDtypeStruct((S, AH * AHD), F32), jax.ShapeDtypeStruct((S, AH * AHD), BF16),
                   jax.ShapeDtypeStruct((S, AH * AHD), F32),
                   jax.ShapeDtypeStruct((2, 3, S, AH * AHD), BF16), jax.ShapeDtypeStruct((2, S, AH * AHD), F32)],
        scratch_shapes=[pltpu.VMEM((S, AHD), BF16) for _ in range(5)]
        + [pltpu.VMEM((S, AHD), F32) for _ in range(8)],
        compiler_params=_cp(("parallel",)),
    )(_attn_consts(), proj, proj, proj)


def _attn_bwd(proj, dmixed, o, lse, qkvp, lsep):
    scale = 1.0 / math.sqrt(AHD)

    def body(c_ref, q_ref, k_ref, v_ref, do_ref, o_ref, lse_ref, qkvp_ref, lsep_ref, dproj_hbm,
             qd, kd, vd, dod, kps, vps, dld, dqd, dkd, dvd, delta, aq, ak, av, sq, sk, sv, sems):
        h = pl.program_id(0)

        def out_copies(head):
            return [pltpu.make_async_copy(
                st, dproj_hbm.at[:, pl.ds(pl.multiple_of((k * AH + head) * AHD, AHD), AHD)], sems.at[k])
                for k, st in enumerate((sq, sk, sv))]

        slope = c_ref[0:1, :]
        mask_c, mask_p, dist_c, dist_p = _attn_masks()
        delta[...] = jnp.broadcast_to(jnp.sum(do_ref[...] * o_ref[...], axis=-1, keepdims=True), (S, AHD))
        for pi, (d, nb) in enumerate(PATTERNS):
            if d == 1:
                _permute_in(qd, q_ref, d, BF16)
                _permute_in(kd, k_ref, d, BF16)
                _permute_in(vd, v_ref, d, BF16)
                qs, ks, vs, lss = qd, kd, vd, lse_ref
            else:
                qs, ks, vs, lss = (qkvp_ref.at[pi - 1, 0], qkvp_ref.at[pi - 1, 1], qkvp_ref.at[pi - 1, 2],
                                   lsep_ref.at[pi - 1])
            _permute_in(dod, do_ref, d, BF16)
            _permute_in(dld, delta, d)
            if nb > 1:
                _shift_block(kps, ks)
                _shift_block(vps, vs)
            bias_c = -(slope * float(d)) * dist_c
            bias_p = -(slope * float(d)) * dist_p
            for g in range(NB // GB):
                q3, k3, v3, do3 = _blocks(qs, g), _blocks(ks, g), _blocks(vs, g), _blocks(dod, g)
                ls, dl = _blocks(lss, g), _blocks(dld, g)
                lo, hi = g * GB * CH, (g + 1) * GB * CH
                p_c = jnp.exp(jnp.where(mask_c, _bdot_nt(q3, k3) * scale + bias_c, NEG) - ls)
                ds_c = ((p_c * (_bdot_nt(do3, v3) - dl)) * scale).astype(BF16)
                dq3 = _bdot(ds_c, k3)
                dkd[lo:hi, :] = _bdot_tn(ds_c, q3).reshape(GB * CH, AHD)
                dvd[lo:hi, :] = _bdot_tn(p_c.astype(BF16), do3).reshape(GB * CH, AHD)
                if nb > 1:
                    kp3, vp3 = _blocks(kps, g), _blocks(vps, g)
                    p_p = jnp.exp(jnp.where(jnp.logical_and(mask_p, _has_prev(g, nb)),
                                            _bdot_nt(q3, kp3) * scale + bias_p, NEG) - ls)
                    ds_p = ((p_p * (_bdot_nt(do3, vp3) - dl)) * scale).astype(BF16)
                    dq3 = dq3 + _bdot(ds_p, kp3)
                    dkp = _bdot_tn(ds_p, q3).reshape(GB * CH, AHD)
                    dvp = _bdot_tn(p_p.astype(BF16), do3).reshape(GB * CH, AHD)
                    if g == 0:
                        dkd[0:hi - CH, :] += dkp[CH:, :]
                        dvd[0:hi - CH, :] += dvp[CH:, :]
                    else:
                        dkd[lo - CH:hi - CH, :] += dkp
                        dvd[lo - CH:hi - CH, :] += dvp
                dqd[lo:hi, :] = dq3.reshape(GB * CH, AHD)
            ln = S // d
            for acc, src in ((aq, dqd), (ak, dkd), (av, dvd)):
                if pi == 0:
                    acc[...] = src[...]
                else:
                    acc[...] += _natural_order(src[...], d)

        @pl.when(h > 0)
        def _():
            for cp in out_copies(h - 1):
                cp.wait()

        sq[...] = aq[...].astype(BF16)
        sk[...] = ak[...].astype(BF16)
        sv[...] = av[...].astype(BF16)
        for cp in out_copies(h):
            cp.start()

        @pl.when(h == AH - 1)
        def _():
            for cp in out_copies(h):
                cp.wait()

    def col(off):
        return pl.BlockSpec((S, AHD), lambda h: (0, off + h))

    return pl.pallas_call(
        body, name="attn_bwd", grid=(AH,),
        in_specs=[pl.BlockSpec((None, 8, AHD), lambda h: (h, 0, 0)), col(0), col(AH), col(2 * AH),
                  col(0), col(0), col(0), pl.BlockSpec((2, 3, S, AHD), lambda h: (0, 0, 0, h)),
                  pl.BlockSpec((2, S, AHD), lambda h: (0, 0, h))],
        out_specs=pl.BlockSpec(memory_space=pl.ANY),
        out_shape=jax.ShapeDtypeStruct((S, NDEV * N_IN), BF16),
        scratch_shapes=[pltpu.VMEM((S, AHD), BF16) for _ in range(6)]
        + [pltpu.VMEM((S, AHD), F32) for _ in range(8)]
        + [pltpu.VMEM((S, AHD), BF16) for _ in range(3)] + [pltpu.SemaphoreType.DMA((3,))],
        compiler_params=_cp(("arbitrary",)),
    )(_attn_consts(), proj, proj, proj, dmixed, o, lse, qkvp, lsep)


def _ret_consts():
    c = np.zeros((RH, 8, RHD), np.float32)
    for h in range(RH):
        c[h, :, :] = np.log(np.float32(1.0) - np.float32(2.0 ** (-5.0 - h)))
    return jnp.asarray(c)


def _ret_factors(lg):
    i = lax.broadcasted_iota(jnp.int32, (CH, CH), 0)
    j = lax.broadcasted_iota(jnp.int32, (CH, CH), 1)
    dif = (i - j).astype(F32)
    decay = jnp.where(dif >= 0, jnp.exp(lg[:, 0:CH] * jnp.maximum(dif, 0.0)), 0.0)
    row = lax.broadcasted_iota(jnp.int32, (CH, RHD), 0).astype(F32)
    zeta = jnp.exp(lg * (CH - 1.0 - row))
    xi = jnp.exp(lg * (row + 1.0))
    return decay, zeta, xi, jnp.exp(lg * float(CH))


CBK = 8
RSTEPS = NB // CBK


def _ret_specs(rev):
    off = 3 * AH * AHD // RHD
    rows = CBK * CH

    def ch(n):
        return (RSTEPS - 1 - n) if rev else n

    def col(k):
        return pl.BlockSpec((rows, RHD), lambda h, n: (ch(n), off + k * RH + h))

    own = pl.BlockSpec((rows, RHD), lambda h, n: (ch(n), h))
    state = pl.BlockSpec((None, CBK, RHD, RHD), lambda h, n: (h, ch(n), 0, 0))
    const = pl.BlockSpec((None, 8, RHD), lambda h, n: (h, 0, 0))
    dm = pl.BlockSpec((rows, RHD), lambda h, n: (ch(n), AH * AHD // RHD + h))
    return col, own, state, const, dm


def _chunks(x):
    return x.reshape(CBK, CH, RHD)


def _ret_fwd(proj):
    def body(c_ref, q_ref, k_ref, v_ref, g_ref, ret_ref, mr_ref, st_ref, r_acc):
        n = pl.program_id(1)

        @pl.when(n == 0)
        def _():
            r_acc[...] = jnp.zeros_like(r_acc)

        decay, zeta, xi, gch = _ret_factors(c_ref[0:1, :])
        q3 = _chunks(q_ref[...].astype(BF16))
        kc = _chunks(k_ref[...] * (1.0 / math.sqrt(RHD)))
        k3 = kc.astype(BF16)
        v3 = _chunks(v_ref[...].astype(BF16))
        kv3 = _bdot_tn((kc * zeta[None]).astype(BF16), v3)
        r = r_acc[...]
        for i in range(CBK):
            st_ref[i] = r.astype(BF16)
            r = r * gch + kv3[i]
        r_acc[...] = r
        scores = _bdot_nt(q3, k3) * decay[None]
        ret = (_bdot(scores.astype(BF16), v3) + _bdot(q3, st_ref[...]) * xi[None]).reshape(CBK * CH, RHD)
        ret_ref[...] = ret
        rr = lax.rsqrt(jnp.mean(ret * ret, axis=-1, keepdims=True) + EPS)
        gv = g_ref[...]
        mr_ref[...] = ((gv * _sigmoid(gv)) * (ret * rr)).astype(BF16)

    col, own, state, const, _ = _ret_specs(False)
    return pl.pallas_call(
        body, name="ret_fwd", grid=(RH, RSTEPS),
        in_specs=[const, col(0), col(1), col(2), col(3)],
        out_specs=[own, own, state],
        out_shape=[jax.ShapeDtypeStruct((S, RH * RHD), F32), jax.ShapeDtypeStruct((S, RH * RHD), BF16),
                   jax.ShapeDtypeStruct((RH, NB, RHD, RHD), BF16)],
        scratch_shapes=[pltpu.VMEM((RHD, RHD), F32)],
        compiler_params=_cp(("parallel", "arbitrary")),
    )(_ret_consts(), proj, proj, proj, proj)


def _ret_bwd(proj, ret, states, dmixed, dproj):
    rows = CBK * CH
    col0 = 3 * AH * AHD

    def body(c_ref, q_ref, k_ref, v_ref, g_ref, ret_ref, st_ref, dm_ref, dproj_in, dproj_hbm, g_acc, gs,
             sq, sk, sv, sg, sems):
        del dproj_in
        h, n = pl.program_id(0), pl.program_id(1)
        step = h * RSTEPS + n

        def out_copies(t):
            hh, nn = t // RSTEPS, t % RSTEPS
            r0 = pl.multiple_of((RSTEPS - 1 - nn) * rows, rows)
            return [pltpu.make_async_copy(
                st, dproj_hbm.at[pl.ds(r0, rows), pl.ds(pl.multiple_of(col0 + (k * RH + hh) * RHD, RHD), RHD)],
                sems.at[k]) for k, st in enumerate((sq, sk, sv, sg))]

        @pl.when(n == 0)
        def _():
            g_acc[...] = jnp.zeros_like(g_acc)

        decay, zeta, xi, gch = _ret_factors(c_ref[0:1, :])
        ret_v = ret_ref[...]
        rr = lax.rsqrt(jnp.mean(ret_v * ret_v, axis=-1, keepdims=True) + EPS)
        gv = g_ref[...]
        sgm = _sigmoid(gv)
        dmix = dm_ref[...]
        dgate = ((dmix * (ret_v * rr)) * (sgm * (1.0 + gv * (1.0 - sgm)))).astype(BF16)
        dretn = dmix * (gv * sgm)
        dret = _chunks(rr * dretn - ret_v * ((rr * rr * rr) * jnp.mean(dretn * ret_v, axis=-1, keepdims=True)))

        q3 = _chunks(q_ref[...].astype(BF16))
        kc = _chunks(k_ref[...] * (1.0 / math.sqrt(RHD)))
        k3 = kc.astype(BF16)
        v3 = _chunks(v_ref[...].astype(BF16))
        d3 = dret.astype(BF16)
        dxi = (dret * xi[None]).astype(BF16)
        kz = (kc * zeta[None]).astype(BF16)
        dr3 = _bdot_tn(q3, dxi)
        acc = g_acc[...]
        for i in reversed(range(CBK)):
            gs[i] = acc.astype(BF16)
            acc = dr3[i] + gch * acc
        g_acc[...] = acc
        g3 = gs[...]
        sc = (_bdot_nt(q3, k3) * decay[None]).astype(BF16)
        da = (_bdot_nt(d3, v3) * decay[None]).astype(BF16)
        dq = _bdot(da, k3) + _bdot_nt(dxi, st_ref[...])
        dkc = _bdot_tn(da, q3) + _bdot_nt(v3, g3) * zeta[None]
        dv = _bdot_tn(sc, d3) + _bdot(kz, g3)

        @pl.when(step > 0)
        def _():
            for cp in out_copies(step - 1):
                cp.wait()

        sq[...] = dq.reshape(rows, RHD).astype(BF16)
        sk[...] = (dkc * (1.0 / math.sqrt(RHD))).reshape(rows, RHD).astype(BF16)
        sv[...] = dv.reshape(rows, RHD).astype(BF16)
        sg[...] = dgate
        for cp in out_copies(step):
            cp.start()

        @pl.when(step == RH * RSTEPS - 1)
        def _():
            for cp in out_copies(step):
                cp.wait()

    col, own, state, const, dm = _ret_specs(True)
    hbm = pl.BlockSpec(memory_space=pl.ANY)
    return pl.pallas_call(
        body, name="ret_bwd", grid=(RH, RSTEPS),
        in_specs=[const, col(0), col(1), col(2), col(3), own, state, dm, hbm],
        out_specs=hbm,
        out_shape=jax.ShapeDtypeStruct(dproj.shape, dproj.dtype),
        input_output_aliases={8: 0},
        scratch_shapes=[pltpu.VMEM((RHD, RHD), F32), pltpu.VMEM((CBK, RHD, RHD), BF16)]
        + [pltpu.VMEM((rows, RHD), BF16) for _ in range(4)] + [pltpu.SemaphoreType.DMA((4,))],
        compiler_params=_cp(("arbitrary", "arbitrary")),
    )(_ret_consts(), proj, proj, proj, proj, ret, states, dmixed, dproj)


class _NoReduction:
    def start(self, group, grads):
        pass

    def local(self, name, first=()):
        return []

    def landed(self, name):
        return []

    def update(self, name):
        return []

    place = None

    def rider(self, name):
        return None

    def set_update(self, name, outs):
        pass


def _local_step(x, tgt, nw1, nw2, nw3, win, wout, wg, wu, wd, red=None):
    red = red or _NoReduction()

    def after(values, first):
        return lax.optimization_barrier((tuple(values), tuple(first)))[0]

    wg, wu, wd = (w.reshape(NFG, N_FG, D) for w in (wg, wu, wd))
    h1, r1 = _rms_fwd(x, nw1)
    proj = _proj(h1, win)
    o, ma, lse, qkvp, lsep = _attn_fwd(proj)
    ret, mr, states = _ret_fwd(proj)
    x2, h2, r2 = _out_proj_rms(x, ma, mr, wout, nw2)
    a, dadg, dadu = _ffn_up(h2, _ffn_gate(h2, wg), wu)
    dx3, dx3b, st3 = _ffn_down_loss(x2, a, wd, nw3, tgt)

    dwd = _wgrad_rows(a, dx3b, "wgrad_down")
    red.start(["w_down"], [dwd])
    (dx3b,) = after([dx3b], [dwd])
    part = _ffn_down_bwd(dx3b, wd, dadg, dadu, 0)
    (dx3b,) = after([dx3b], red.local("w_down", first=part))
    dg, du = _ffn_down_bwd(dx3b, wd, dadg, dadu, 1, part)
    dwg = _wgrad_rows(dg, h2, "wgrad_gate")
    red.start(["w_gate"], [dwg])
    (du,) = after([du], [dwg])
    dwu = _wgrad_rows(du, h2, "wgrad_up")
    red.start(["w_up"], [dwu])
    dg, du = after([dg, du], red.local("w_gate", first=[dwu] + red.landed("w_down")))
    dx2, dx2b, st2 = _ffn_up_bwd(dg, du, wg, wu, dx3, x2, r2, nw2)
    (dx2b,) = after([dx2b], red.local("w_up", first=[dx2b]))
    dwo = _wgrad_out(ma, mr, dx2b)
    red.start(["w_out"], [dwo])
    (dx2b,) = after([dx2b], [dwo])
    dmixed, done = _out_proj_bwd(dx2b, wout, red.place, red.rider("w_down"))
    red.set_update("w_down", done)
    dproj = _attn_bwd(proj, dmixed, o, lse, qkvp, lsep)
    (dmixed,) = after([dmixed], red.local("w_out", first=[dproj] + red.landed("w_gate")))
    dproj = _ret_bwd(proj, ret, states, dmixed, dproj)
    (dwi0,) = after([_wgrad_in(h1, dproj, 0)], red.landed("w_up"))
    red.start(["w_in_0"], [dwi0])
    (dproj,) = after([dproj], [dwi0])
    dwi1 = _wgrad_in(h1, dproj, 1)
    red.start(["w_in_1"], [dwi1])
    sums = red.local("w_in_0", first=[dwi1] + red.landed("w_out"))
    sums = red.local("w_in_1", first=sums + red.update("w_gate"))
    (dproj,) = after([dproj], sums)
    gx, st1 = _in_proj_bwd(dproj, win, dx2, x, r1, nw1)
    dwi = jnp.concatenate([dwi0, dwi1], axis=1)
    stats = jnp.concatenate([st1[0:1], st2[0:1], st3[0:2], jnp.zeros((4, D), F32)], axis=0)
    return stats, gx, dwi, dwo, dwg, dwu, dwd


def _place():
    x, y, c = lax.axis_index("x"), lax.axis_index("y"), lax.axis_index("c")
    return x, y, c, [(1 - x, y), (x, 1 - y), (1 - x, 1 - y)]


def _handshake(peers):
    barrier = pltpu.get_barrier_semaphore()
    for peer in peers:
        pl.semaphore_signal(barrier, inc=1, device_id=peer, device_id_type=MESH)
    pl.semaphore_wait(barrier, len(peers))


def _all_gather(shards, name, collective_id):
    na = len(shards)
    SIB, XN0, XN1, YN1, YN0, VIA_X, VIA_Y = 0, 1, 2, 3, 4, 5, 6
    D2D = {XN0: 7, XN1: 8, YN1: 9, YN0: 10, VIA_X: 11, VIA_Y: 12}

    def body(*refs):
        ins, outs = refs[:na], refs[na:2 * na]
        send_sems, recv_sems, local_sems = refs[2 * na:]
        x, y, c, _ = _place()
        me, sib = (x, y, c), (x, y, 1 - c)
        xn, yn, dg = (1 - x, y, c), (x, 1 - y, c), (1 - x, 1 - y, c)
        _handshake([sib, xn, yn])

        def part(ref, h):
            rows = ref.shape[0] // 2
            return ref if h is None else ref.at[pl.ds(h * rows, rows)]

        def block(a, owner, h):
            return part(outs[a].at[4 * owner[0] + 2 * owner[1] + owner[2]], h)

        def copy(a, k, owner, h, to, own_src=False):
            return pltpu.make_async_remote_copy(
                src_ref=part(ins[a], h) if own_src else block(a, owner, h), dst_ref=block(a, owner, h),
                send_sem=send_sems.at[a, k], recv_sem=recv_sems.at[a, k], device_id=to, device_id_type=MESH)

        def other(p):
            return (p[0], p[1], 1 - c)

        mine = [pltpu.make_async_copy(ins[a], block(a, me, None), local_sems.at[a]) for a in range(na)]
        for cp in mine:
            cp.start()
        sent = []
        for a in range(na):
            sent += [copy(a, XN0, me, 0, xn, True), copy(a, YN1, me, 1, yn, True),
                     copy(a, XN1, me, 1, xn, True), copy(a, YN0, me, 0, yn, True)]
        sent += [copy(a, SIB, me, None, sib, True) for a in range(na)]
        for cp in sent:
            cp.start()

        def landed(a, k, owner, h, then):
            copy(a, k, owner, h, me).wait_recv()
            for k2, to in then + [(D2D[k], sib)]:
                cp = copy(a, k2, owner, h, to)
                cp.start()
                sent.append(cp)

        for a in range(na):
            landed(a, XN0, xn, 0, [(VIA_Y, yn)])
            landed(a, YN1, yn, 1, [(VIA_X, xn)])
            landed(a, XN1, xn, 1, [])
            landed(a, YN0, yn, 0, [])
        for a in range(na):
            landed(a, VIA_Y, dg, 0, [])
            landed(a, VIA_X, dg, 1, [])
        for a in range(na):
            copy(a, SIB, sib, None, me).wait_recv()
            for k, owner, h in ((XN0, xn, 0), (XN1, xn, 1), (YN1, yn, 1), (YN0, yn, 0), (VIA_Y, dg, 0), (VIA_X, dg, 1)):
                copy(a, D2D[k], other(owner), h, me).wait_recv()
        for cp in sent:
            cp.wait_send()
        for cp in mine:
            cp.wait()

    return _sequencer_call(
        body, name, collective_id,
        [jax.ShapeDtypeStruct((NDEV,) + s.shape, s.dtype) for s in shards],
        [pltpu.SemaphoreType.DMA((na, 13)), pltpu.SemaphoreType.DMA((na, 13)), pltpu.SemaphoreType.DMA((na,))])(*shards)


def _sequencer_call(body, name, collective_id, out_type, scratch_types):
    return pl.kernel(
        body, name=name, out_type=out_type,
        mesh=plsc.ScalarSubcoreMesh(axis_name="sequencer", num_cores=1),
        scratch_types=scratch_types,
        compiler_params=pltpu.CompilerParams(collective_id=collective_id))


def _exchange_sibling(grads, name, collective_id):
    na = len(grads)

    def body(*refs):
        ins, outs = refs[:na], refs[na:2 * na]
        send_sems, recv_sems = refs[2 * na:]
        x, y, c, _ = _place()
        _handshake([(x, y, 1 - c)])
        cps = []
        for a in range(na):
            for k in range(4):
                cps.append(pltpu.make_async_remote_copy(
                    src_ref=ins[a].at[2 * k + (1 - c)], dst_ref=outs[a].at[k],
                    send_sem=send_sems.at[a, k], recv_sem=recv_sems.at[a, k],
                    device_id=(x, y, 1 - c), device_id_type=MESH))
        for cp in cps:
            cp.start()
        for cp in cps:
            cp.wait()

    return _sequencer_call(
        body, name, collective_id,
        [jax.ShapeDtypeStruct((4,) + g.shape[1:], g.dtype) for g in grads],
        [pltpu.SemaphoreType.DMA((na, 4)), pltpu.SemaphoreType.DMA((na, 4))])(*grads)


def _row_tile(rows, cols):
    for t in (512, 256, 176, 128, 64, 32, 16):
        if rows % t == 0 and t * cols * 4 <= (2 << 20):
            return t
    raise ValueError((rows, cols))


def _chip_sum(place, g, got, name):
    _, r, c = g.shape
    tm = r

    def body(pos_ref, g_ref, got_ref, o_ref):
        o_ref[...] = (g_ref[...].astype(F32) + got_ref[...].astype(F32)).astype(BF16)

    def chip(j, pos):
        return 2 * (pos[0] ^ jnp.where(j == 1, 0, 1)) + (pos[1] ^ jnp.where(j == 0, 0, 1))

    return pl.pallas_call(
        body, name=name,
        grid_spec=pltpu.PrefetchScalarGridSpec(
            num_scalar_prefetch=1, grid=(3, r // tm),
            in_specs=[pl.BlockSpec((None, tm, c), lambda j, i, pos: (2 * chip(j, pos) + pos[2], i, 0)),
                      pl.BlockSpec((None, tm, c), lambda j, i, pos: (chip(j, pos), i, 0))],
            out_specs=pl.BlockSpec((None, tm, c), lambda j, i, pos: (j, i, 0))),
        out_shape=jax.ShapeDtypeStruct((3, r, c), BF16),
        compiler_params=_cp(("parallel", "parallel")),
    )(place, g, got)


def _exchange_chips(sums, name, collective_id):
    na = len(sums)

    def body(*refs):
        ins, outs = refs[:na], refs[na:2 * na]
        send_sems, recv_sems = refs[2 * na:]
        x, y, c, chips = _place()
        _handshake([(*chip, c) for chip in chips])
        cps = []
        for a in range(na):
            for j, chip in enumerate(chips):
                cps.append(pltpu.make_async_remote_copy(
                    src_ref=ins[a].at[j], dst_ref=outs[a].at[j],
                    send_sem=send_sems.at[a, j], recv_sem=recv_sems.at[a, j],
                    device_id=(*chip, c), device_id_type=MESH))
        for cp in cps:
            cp.start()
        for cp in cps:
            cp.wait()

    return _sequencer_call(
        body, name, collective_id,
        [jax.ShapeDtypeStruct((3,) + s.shape[1:], s.dtype) for s in sums],
        [pltpu.SemaphoreType.DMA((na, 3)), pltpu.SemaphoreType.DMA((na, 3))])(*sums)


def _exchange_stats(stats, collective_id):
    def body(st_in, st_out, st_send, st_recv, local_sem):
        x, y, c, _ = _place()
        me_idx = 4 * x + 2 * y + c
        peers = [(x ^ ((k >> 2) & 1), y ^ ((k >> 1) & 1), c ^ (k & 1)) for k in range(1, 8)]
        _handshake(peers)
        mine = pltpu.make_async_copy(st_in, st_out.at[me_idx], local_sem)
        mine.start()
        cps = [pltpu.make_async_remote_copy(
            src_ref=st_in, dst_ref=st_out.at[me_idx], send_sem=st_send.at[k], recv_sem=st_recv.at[k],
            device_id=peer, device_id_type=MESH) for k, peer in enumerate(peers)]
        for cp in cps:
            cp.start()
        for cp in cps:
            cp.wait()
        mine.wait()

    return _sequencer_call(
        body, "exchange_stats", collective_id,
        jax.ShapeDtypeStruct((NDEV,) + stats.shape, stats.dtype),
        [pltpu.SemaphoreType.DMA((7,)), pltpu.SemaphoreType.DMA((7,)), pltpu.SemaphoreType.DMA])(stats)


class _Reduction:
    def __init__(self, place, first_collective_id, state):
        self.place = place
        self.ids = iter(range(first_collective_id, 32))
        self.state = state
        self.groups = {}
        self.updates = {}

    def next_id(self):
        return next(self.ids)

    def start(self, group, grads):
        got = _exchange_sibling(grads, "sibling_exchange_" + group[0], self.next_id())
        self.groups[group[0]] = dict(names=group, grads=grads, got=got)

    def local(self, name, first=()):
        grp = self.groups[name]
        grads = lax.optimization_barrier((tuple(grp["grads"]), tuple(first)))[0]
        grp["sums"] = [_chip_sum(self.place, g, s, "chip_sum_" + n)
                       for g, s, n in zip(grads, grp["got"], grp["names"])]
        grp["chips"] = _exchange_chips(grp["sums"], "chip_exchange_" + name, self.next_id())
        return grp["sums"]

    def landed(self, name):
        return list(self.groups[name]["chips"])

    def rider(self, name):
        grp = next(g for g in self.groups.values() if name in g["names"])
        k = grp["names"].index(name)
        return self.state[name][:3] + (grp["grads"][k], grp["got"][k], grp["chips"][k])

    def set_update(self, name, outs):
        self.updates[name] = list(outs)

    def update(self, name):
        if name not in self.updates:
            grp = next(g for g in self.groups.values() if name in g["names"])
            k = grp["names"].index(name)
            w, m, v, part, parts = self.state[name]
            before = self.update(f"{name[:-1]}{part - 1}") if part else None
            self.updates[name] = _shard_update(self.place, w, m, v, grp["grads"][k], grp["got"][k],
                                               grp["chips"][k], "update_" + name, part, parts, before)
        return list(self.updates[name])


def _adamw(w, g, m, v):
    m = ADAM_B1 * m + (1.0 - ADAM_B1) * g
    v = ADAM_B2 * v + (1.0 - ADAM_B2) * (g * g)
    m_hat = m / (1.0 - ADAM_B1 ** ADAM_STEP)
    v_hat = v / (1.0 - ADAM_B2 ** ADAM_STEP)
    delta = -ADAM_LR * (m_hat / (jnp.sqrt(v_hat) + ADAM_EPS) + ADAM_WD * w)
    return delta, m, v


def _update_tile(w_ref, m_ref, v_ref, g_ref, s_ref, c_ref, go_ref, d_ref, mo_ref, vo_ref):
    grad = g_ref[...].astype(F32) + s_ref[...].astype(F32)
    for j in range(3):
        grad = grad + c_ref[j].astype(F32)
    delta, mn, vn = _adamw(w_ref[...], grad, m_ref[...], v_ref[...])
    go_ref[...] = grad
    d_ref[...] = delta
    mo_ref[...] = mn
    vo_ref[...] = vn


def _shard_update(place, w, m, v, g, got_sib, got_chips, name, part=0, parts=1, before=None):
    r, c = w.shape
    rp = r // parts
    tm = _row_tile(rp, c)
    off = part * (rp // tm)

    def body(pos_ref, w_ref, m_ref, v_ref, g_ref, s_ref, c_ref, *rest):
        _update_tile(w_ref, m_ref, v_ref, g_ref, s_ref, c_ref, *rest[-4:])

    row = pl.BlockSpec((tm, c), lambda i, pos: (i + off, 0))
    before = list(before or [])
    return pl.pallas_call(
        body, name=name,
        grid_spec=pltpu.PrefetchScalarGridSpec(
            num_scalar_prefetch=1, grid=(rp // tm,),
            in_specs=[row, row, row,
                      pl.BlockSpec((None, tm, c), lambda i, pos: (4 * pos[0] + 2 * pos[1] + pos[2], i, 0)),
                      pl.BlockSpec((None, tm, c), lambda i, pos: (2 * pos[0] + pos[1], i, 0)),
                      pl.BlockSpec((3, tm, c), lambda i, pos: (0, i, 0))]
            + [pl.BlockSpec(memory_space=pl.ANY)] * len(before),
            out_specs=[row, row, row, row]),
        out_shape=[jax.ShapeDtypeStruct((r, c), F32)] * 4,
        input_output_aliases={7 + k: k for k in range(len(before))},
        compiler_params=_cp(("parallel",)),
    )(place, w, m, v, g, got_sib, got_chips, *before)


def _small_update(stats_all, ws, ms, vs):
    def body(st_ref, w_ref, m_ref, v_ref, go_ref, d_ref, mo_ref, vo_ref):
        grad = st_ref[0]
        for k in range(1, NDEV):
            grad = grad + st_ref[k]
        delta, mn, vn = _adamw(w_ref[...], grad, m_ref[...], v_ref[...])
        go_ref[...] = grad
        d_ref[...] = delta
        mo_ref[...] = mn
        vo_ref[...] = vn

    return pl.pallas_call(
        body, name="small_update",
        out_shape=[jax.ShapeDtypeStruct((8, D), F32)] * 4,
        compiler_params=_cp(),
    )(stats_all, ws, ms, vs)


def kernel(x, norm_mix_w, w_in, w_out, norm_ffn_w, w_gate, w_up, w_down, norm_final_w, loss_target, m_norm_mix_w, m_w_in, m_w_out, m_norm_ffn_w, m_w_gate, m_w_up, m_w_down, m_norm_final_w, v_norm_mix_w, v_w_in, v_w_out, v_norm_ffn_w, v_w_gate, v_w_up, v_w_down, v_norm_final_w):
    tr = {"w_gate", "w_up"}
    names = ["w_in", "w_out", "w_gate", "w_up", "w_down"]

    def view(a, n):
        return a[0].T if n in tr else a[0]

    big_w = [view(a, n) for a, n in zip([w_in, w_out, w_gate, w_up, w_down], names)]
    big_m = [view(a, n) for a, n in zip([m_w_in, m_w_out, m_w_gate, m_w_up, m_w_down], names)]
    big_v = [view(a, n) for a, n in zip([v_w_in, v_w_out, v_w_gate, v_w_up, v_w_down], names)]

    shards = [_cast_bf16(w, "cast_" + n) for w, n in zip(big_w, names)]
    (win,) = _all_gather(shards[0:1], "all_gather_w_in", 1)
    (wout,) = _all_gather(shards[1:2], "all_gather_w_out", 2)
    (wg,) = _all_gather(shards[2:3], "all_gather_w_gate", 3)
    (wu,) = _all_gather(shards[3:4], "all_gather_w_up", 4)
    (wd,) = _all_gather(shards[4:5], "all_gather_w_down", 5)
    nw3 = norm_final_w.reshape(1, D)
    place = jnp.stack([lax.axis_index("x"), lax.axis_index("y"), lax.axis_index("c")]).astype(jnp.int32)
    state = {n: (w, m, v, 0, 1) for n, w, m, v in zip(names, big_w, big_m, big_v)}
    for part in range(W_IN_PARTS):
        state[f"w_in_{part}"] = state["w_in"][:3] + (part, W_IN_PARTS)
    red = _Reduction(place, 6, state)
    stats, gx, *_ = _local_step(
        x[0], loss_target[0], norm_mix_w, norm_ffn_w, nw3, win, wout.reshape(D, D), wg, wu, wd, red)
    stats_all = _exchange_stats(stats, red.next_id())
    upd = [red.update(f"w_in_{W_IN_PARTS - 1}" if n == "w_in" else n) for n in names]
    stats_all = lax.optimization_barrier((stats_all, tuple(upd[0])))[0]

    def rows(a, b, c):
        return jnp.concatenate([a.reshape(1, D), b.reshape(1, D), c.reshape(1, D), jnp.zeros((5, D), F32)], axis=0)

    sg, sd, sm, sv = _small_update(stats_all, rows(norm_mix_w, norm_ffn_w, norm_final_w),
                                   rows(m_norm_mix_w, m_norm_ffn_w, m_norm_final_w),
                                   rows(v_norm_mix_w, v_norm_ffn_w, v_norm_final_w))
    loss = sg[3, 0]

    def outs(k, small):
        big = [(u[k].T if n in tr else u[k])[None] for u, n in zip(upd, names)]
        return [small[0:1], big[0], big[1], small[1:2], big[2], big[3], big[4], small[2]]

    return (loss, gx[None], *outs(0, sg), *outs(1, sd), *outs(2, sm), *outs(3, sv))
```

```python
import math

import numpy as np
import jax
import jax.numpy as jnp
from jax import lax
from jax.experimental import pallas as pl
from jax.experimental.pallas import tpu as pltpu
from jax.experimental.pallas import tpu_sc as plsc

F32 = jnp.float32
BF16 = jnp.bfloat16

S = 2048
D = 2048
NDEV = 8
N_IN = 7168 // NDEV
N_FF = 5632 // NDEV
NFG, N_FG = NDEV // 2, 2 * N_FF
N_OUT = 2048 // NDEV
AH, AHD = 8, 128
RH, RHD = 4, 256
CH = 128
NB = S // CH
EPS = 1e-6
PATTERNS = ((1, 16), (4, 4), (16, 1))
NEG = -1e30
VMEM_LIMIT = 56 * 1024 * 1024

ADAM_LR, ADAM_B1, ADAM_B2, ADAM_EPS, ADAM_WD, ADAM_STEP = 0.001, 0.9, 0.999, 1e-08, 0.01, 10
MESH = pl.DeviceIdType.MESH


def _cp(sem=None):
    return pltpu.CompilerParams(dimension_semantics=sem, vmem_limit_bytes=VMEM_LIMIT)


def _dot(a, b):
    return jnp.dot(a, b, preferred_element_type=F32)


def _dot_nt(a, b):
    return lax.dot_general(a, b, (((1,), (1,)), ((), ())), preferred_element_type=F32)


def _dot_tn(a, b):
    return lax.dot_general(a, b, (((0,), (0,)), ((), ())), preferred_element_type=F32)


def _sigmoid(x):
    return 0.5 * jnp.tanh(0.5 * x) + 0.5


def _cast_bf16(w, name):
    r, c = w.shape
    tm = r if r <= 1024 else 512

    def body(w_ref, o_ref):
        o_ref[...] = w_ref[...].astype(BF16)

    return pl.pallas_call(
        body, name=name, grid=(r // tm,),
        in_specs=[pl.BlockSpec((tm, c), lambda i: (i, 0))],
        out_specs=pl.BlockSpec((tm, c), lambda i: (i, 0)),
        out_shape=jax.ShapeDtypeStruct((r, c), BF16),
        compiler_params=_cp(("parallel",)),
    )(w)


def _rms_fwd(x, nw):
    tm = 256

    def body(x_ref, w_ref, h_ref, r_ref):
        xs = x_ref[...]
        r = lax.rsqrt(jnp.mean(xs * xs, axis=-1, keepdims=True) + EPS)
        h_ref[...] = ((xs * r) * w_ref[...]).astype(BF16)
        r_ref[...] = r

    return pl.pallas_call(
        body, name="rms_fwd", grid=(S // tm,),
        in_specs=[pl.BlockSpec((tm, D), lambda i: (i, 0)), pl.BlockSpec((1, D), lambda i: (0, 0))],
        out_specs=[pl.BlockSpec((tm, D), lambda i: (i, 0)), pl.BlockSpec((tm, 1), lambda i: (i, 0))],
        out_shape=[jax.ShapeDtypeStruct((S, D), BF16), jax.ShapeDtypeStruct((S, 1), F32)],
        compiler_params=_cp(("parallel",)),
    )(x, nw)


def _row_copies(hbm_refs, bufs, sems, m, tm):
    rows = pl.ds(pl.multiple_of(m * tm, tm), tm)
    return [pltpu.make_async_copy(h.at[rows], b, sems.at[i]) for i, (h, b) in enumerate(zip(hbm_refs, bufs))]


def _rms_bwd_tile(dh, xs, r, nw):
    dnw = jnp.sum(dh * (xs * r), axis=0, keepdims=True)
    gy = dh * nw
    dx = r * gy - xs * ((r * r * r) * jnp.mean(gy * xs, axis=-1, keepdims=True))
    return dx, dnw


def _proj(h1, win):
    tm = 1024

    def body(a_ref, w_ref, o_ref):
        o_ref[...] = _dot(a_ref[...], w_ref[...])

    return pl.pallas_call(
        body, name="proj", grid=(NDEV, S // tm),
        in_specs=[pl.BlockSpec((tm, D), lambda p, m: (m, 0)),
                  pl.BlockSpec((None, D, N_IN), lambda p, m: (p, 0, 0))],
        out_specs=pl.BlockSpec((tm, N_IN), lambda p, m: (m, p)),
        out_shape=jax.ShapeDtypeStruct((S, NDEV * N_IN), F32),
        compiler_params=_cp(("parallel", "parallel")),
    )(h1, win)


def _out_proj_rms(x, ma, mr, wout, nw):
    tm = 256
    half = D // 2

    def body(x_ref, ma_ref, mr_ref, w_ref, nw_ref, x2_ref, h_ref, r_ref):
        acc = _dot(ma_ref[...], w_ref[0:half, :]) + _dot(mr_ref[...], w_ref[half:D, :])
        x2 = x_ref[...] + acc
        r = lax.rsqrt(jnp.mean(x2 * x2, axis=-1, keepdims=True) + EPS)
        x2_ref[...] = x2
        h_ref[...] = ((x2 * r) * nw_ref[...]).astype(BF16)
        r_ref[...] = r

    return pl.pallas_call(
        body, name="out_proj_rms", grid=(S // tm,),
        in_specs=[pl.BlockSpec((tm, D), lambda i: (i, 0)),
                  pl.BlockSpec((tm, half), lambda i: (i, 0)),
                  pl.BlockSpec((tm, half), lambda i: (i, 0)),
                  pl.BlockSpec((D, D), lambda i: (0, 0)),
                  pl.BlockSpec((1, D), lambda i: (0, 0))],
        out_specs=[pl.BlockSpec((tm, D), lambda i: (i, 0)), pl.BlockSpec((tm, D), lambda i: (i, 0)),
                   pl.BlockSpec((tm, 1), lambda i: (i, 0))],
        out_shape=[jax.ShapeDtypeStruct((S, D), F32), jax.ShapeDtypeStruct((S, D), BF16),
                   jax.ShapeDtypeStruct((S, 1), F32)],
        compiler_params=_cp(("parallel",)),
    )(x, ma, mr, wout, nw)


def _ffn_up(h2, wgu):
    tm = 512

    def body(h_ref, w_ref, a_ref, dadg_ref, dadu_ref):
        gu = _dot_nt(h_ref[...], w_ref[...])
        g, u = gu[:, 0:N_FG], gu[:, N_FG:2 * N_FG]
        sg = _sigmoid(g)
        silu = g * sg
        a_ref[...] = (silu * u).astype(BF16)
        dadg_ref[...] = (u * (sg * (1.0 + g * (1.0 - sg)))).astype(BF16)
        dadu_ref[...] = silu.astype(BF16)

    blk = pl.BlockSpec((None, tm, N_FG), lambda p, m: (p, m, 0))
    return pl.pallas_call(
        body, name="ffn_up", grid=(NFG, S // tm),
        in_specs=[pl.BlockSpec((tm, D), lambda p, m: (m, 0)),
                  pl.BlockSpec((None, 2 * N_FG, D), lambda p, m: (p, 0, 0))],
        out_specs=[blk, blk, blk],
        out_shape=[jax.ShapeDtypeStruct((NFG, S, N_FG), BF16)] * 3,
        compiler_params=_cp(("parallel", "parallel")),
    )(h2, wgu)


def _ffn_down_loss(x2, a, wd, nw, tgt):
    tm = 512

    def body(x2_hbm, a_ref, w_ref, nw_ref, t_hbm, dx_ref, dxb_ref, st_ref, acc_ref, x2_buf, t_buf, sems):
        m, p = pl.program_id(0), pl.program_id(1)
        tail_in = _row_copies((x2_hbm, t_hbm), (x2_buf, t_buf), sems, m, tm)

        @pl.when(p == 0)
        def _():
            acc_ref[...] = jnp.zeros_like(acc_ref)
            for cp in tail_in:
                cp.start()

        @pl.when((p == 0) & (m == 0))
        def _():
            st_ref[...] = jnp.zeros_like(st_ref)

        acc_ref[...] += _dot(a_ref[...], w_ref[...])

        @pl.when(p == NFG - 1)
        def _():
            for cp in tail_in:
                cp.wait()
            x3 = x2_buf[...] + acc_ref[...]
            nwv = nw_ref[...]
            r = lax.rsqrt(jnp.mean(x3 * x3, axis=-1, keepdims=True) + EPS)
            y = (x3 * r) * nwv
            err = y - t_buf[...]
            loss = 0.5 * jnp.sum(jnp.mean(err * err, axis=-1, keepdims=True), axis=0, keepdims=True)
            dy = err * (1.0 / D)
            dx, dnw = _rms_bwd_tile(dy, x3, r, nwv)
            dx_ref[...] = dx
            dxb_ref[...] = dx.astype(BF16)
            st_ref[0:1, :] += dnw
            st_ref[1:2, :] += jnp.broadcast_to(loss, (1, D))

    return pl.pallas_call(
        body, name="ffn_down_loss", grid=(S // tm, NFG),
        in_specs=[pl.BlockSpec(memory_space=pl.ANY),
                  pl.BlockSpec((None, tm, N_FG), lambda m, p: (p, m, 0)),
                  pl.BlockSpec((None, N_FG, D), lambda m, p: (p, 0, 0)),
                  pl.BlockSpec((1, D), lambda m, p: (0, 0)),
                  pl.BlockSpec(memory_space=pl.ANY)],
        out_specs=[pl.BlockSpec((tm, D), lambda m, p: (m, 0)), pl.BlockSpec((tm, D), lambda m, p: (m, 0)),
                   pl.BlockSpec((8, D), lambda m, p: (0, 0))],
        out_shape=[jax.ShapeDtypeStruct((S, D), F32), jax.ShapeDtypeStruct((S, D), BF16),
                   jax.ShapeDtypeStruct((8, D), F32)],
        scratch_shapes=[pltpu.VMEM((tm, D), F32), pltpu.VMEM((tm, D), F32), pltpu.VMEM((tm, D), F32),
                        pltpu.SemaphoreType.DMA((2,))],
        compiler_params=_cp(("arbitrary", "arbitrary")),
    )(x2, a, wd, nw, tgt)


def _ffn_down_bwd(dx3b, wd, dadg, dadu, part, before=None):
    tm = 1024
    half = NFG // 2

    def body(dx_ref, w_ref, dadg_ref, dadu_ref, *rest):
        dgu_ref = rest[-1]
        da = _dot_nt(dx_ref[...], w_ref[...])
        dgu_ref[:, 0:N_FG] = (da * dadg_ref[...].astype(F32)).astype(BF16)
        dgu_ref[:, N_FG:2 * N_FG] = (da * dadu_ref[...].astype(F32)).astype(BF16)

    blk = pl.BlockSpec((None, tm, N_FG), lambda p, m: (p + part * half, m, 0))
    before = list(before or [])
    return pl.pallas_call(
        body, name=f"ffn_down_bwd_{part}", grid=(half, S // tm),
        in_specs=[pl.BlockSpec((tm, D), lambda p, m: (m, 0)),
                  pl.BlockSpec((None, N_FG, D), lambda p, m: (p + part * half, 0, 0)), blk, blk]
        + [pl.BlockSpec(memory_space=pl.ANY)] * len(before),
        out_specs=pl.BlockSpec((None, tm, 2 * N_FG), lambda p, m: (p + part * half, m, 0)),
        out_shape=jax.ShapeDtypeStruct((NFG, S, 2 * N_FG), BF16),
        input_output_aliases={4 + k: k for k in range(len(before))},
        compiler_params=_cp(("parallel", "parallel")),
    )(dx3b, wd, dadg, dadu, *before)


def _ffn_up_bwd(dgu, wgu, dres, xs, r, nw):
    tm = 512

    def body(dgu_ref, w_ref, dres_hbm, x_hbm, r_ref, nw_ref, dx_ref, dxb_ref, st_ref, dres_buf, x_buf, sems):
        m, p = pl.program_id(0), pl.program_id(1)
        tail_in = _row_copies((dres_hbm, x_hbm), (dres_buf, x_buf), sems, m, tm)

        @pl.when(p == 0)
        def _():
            dx_ref[...] = jnp.zeros_like(dx_ref)
            for cp in tail_in:
                cp.start()

        @pl.when((p == 0) & (m == 0))
        def _():
            st_ref[...] = jnp.zeros_like(st_ref)

        dx_ref[...] += _dot(dgu_ref[...], w_ref[...])

        @pl.when(p == NFG - 1)
        def _():
            for cp in tail_in:
                cp.wait()
            dx, dnw = _rms_bwd_tile(dx_ref[...], x_buf[...], r_ref[...], nw_ref[...])
            dx = dres_buf[...] + dx
            dx_ref[...] = dx
            dxb_ref[...] = dx.astype(BF16)
            st_ref[0:1, :] += dnw

    blk = pl.BlockSpec((None, tm, 2 * N_FG), lambda m, p: (p, m, 0))
    wblk = pl.BlockSpec((None, 2 * N_FG, D), lambda m, p: (p, 0, 0))
    row = pl.BlockSpec((tm, D), lambda m, p: (m, 0))
    hbm = pl.BlockSpec(memory_space=pl.ANY)
    return pl.pallas_call(
        body, name="ffn_up_bwd", grid=(S // tm, NFG),
        in_specs=[blk, wblk, hbm, hbm, pl.BlockSpec((tm, 1), lambda m, p: (m, 0)),
                  pl.BlockSpec((1, D), lambda m, p: (0, 0))],
        out_specs=[row, row, pl.BlockSpec((8, D), lambda m, p: (0, 0))],
        out_shape=[jax.ShapeDtypeStruct((S, D), F32), jax.ShapeDtypeStruct((S, D), BF16),
                   jax.ShapeDtypeStruct((8, D), F32)],
        scratch_shapes=[pltpu.VMEM((tm, D), F32), pltpu.VMEM((tm, D), F32), pltpu.SemaphoreType.DMA((2,))],
        compiler_params=_cp(("arbitrary", "arbitrary")),
    )(dgu, wgu, dres, xs, r, nw)


def _out_proj_bwd(dx2b, wout, place=None, rider=None):
    tm = 256

    if rider is None:
        def body(dx_ref, w_ref, o_ref):
            o_ref[...] = _dot_nt(dx_ref[...], w_ref[...])

        return pl.pallas_call(
            body, name="out_proj_bwd", grid=(S // tm,),
            in_specs=[pl.BlockSpec((tm, D), lambda i: (i, 0)), pl.BlockSpec((D, D), lambda i: (0, 0))],
            out_specs=pl.BlockSpec((tm, D), lambda i: (i, 0)),
            out_shape=jax.ShapeDtypeStruct((S, D), F32),
            compiler_params=_cp(("parallel",)),
        )(dx2b, wout), None

    w = rider[0]
    r, c = w.shape
    rt = _row_tile(r, c)
    nt = r // rt
    assert nt <= S // tm

    def body(pos_ref, dx_ref, w_ref, uw, um, uv, ug, us, uc, o_ref, go, dd, mo, vo):
        o_ref[...] = _dot_nt(dx_ref[...], w_ref[...])

        @pl.when(pl.program_id(0) < nt)
        def _():
            _update_tile(uw, um, uv, ug, us, uc, go, dd, mo, vo)

    def at(i):
        return jnp.minimum(i, nt - 1)

    tile = pl.BlockSpec((rt, c), lambda i, pos: (at(i), 0))
    outs = pl.pallas_call(
        body, name="out_proj_bwd",
        grid_spec=pltpu.PrefetchScalarGridSpec(
            num_scalar_prefetch=1, grid=(S // tm,),
            in_specs=[pl.BlockSpec((tm, D), lambda i, pos: (i, 0)), pl.BlockSpec((D, D), lambda i, pos: (0, 0)),
                      tile, tile, tile,
                      pl.BlockSpec((None, rt, c), lambda i, pos: (4 * pos[0] + 2 * pos[1] + pos[2], at(i), 0)),
                      pl.BlockSpec((None, rt, c), lambda i, pos: (2 * pos[0] + pos[1], at(i), 0)),
                      pl.BlockSpec((3, rt, c), lambda i, pos: (0, at(i), 0))],
            out_specs=[pl.BlockSpec((tm, D), lambda i, pos: (i, 0)), tile, tile, tile, tile]),
        out_shape=[jax.ShapeDtypeStruct((S, D), F32)] + [jax.ShapeDtypeStruct((r, c), F32)] * 4,
        compiler_params=_cp(("arbitrary",)),
    )(place, dx2b, wout, *rider)
    return outs[0], outs[1:]


def _in_proj_bwd(dproj, win, dres, xs, r, nw):
    tm = 1024

    def body(dp_ref, w_ref, dres_hbm, x_hbm, r_ref, nw_ref, dx_ref, st_ref, dres_buf, x_buf, sems):
        m, p = pl.program_id(0), pl.program_id(1)
        tail_in = _row_copies((dres_hbm, x_hbm), (dres_buf, x_buf), sems, m, tm)

        @pl.when(p == 0)
        def _():
            dx_ref[...] = jnp.zeros_like(dx_ref)
            for cp in tail_in:
                cp.start()

        @pl.when((p == 0) & (m == 0))
        def _():
            st_ref[...] = jnp.zeros_like(st_ref)

        dx_ref[...] += _dot_nt(dp_ref[...], w_ref[...])

        @pl.when(p == NDEV - 1)
        def _():
            for cp in tail_in:
                cp.wait()
            dx, dnw = _rms_bwd_tile(dx_ref[...], x_buf[...], r_ref[...], nw_ref[...])
            dx_ref[...] = dres_buf[...] + dx
            st_ref[0:1, :] += dnw

    row = pl.BlockSpec((tm, D), lambda m, p: (m, 0))
    hbm = pl.BlockSpec(memory_space=pl.ANY)
    return pl.pallas_call(
        body, name="in_proj_bwd", grid=(S // tm, NDEV),
        in_specs=[pl.BlockSpec((tm, N_IN), lambda m, p: (m, p)),
                  pl.BlockSpec((None, D, N_IN), lambda m, p: (p, 0, 0)),
                  hbm, hbm, pl.BlockSpec((tm, 1), lambda m, p: (m, 0)),
                  pl.BlockSpec((1, D), lambda m, p: (0, 0))],
        out_specs=[row, pl.BlockSpec((8, D), lambda m, p: (0, 0))],
        out_shape=[jax.ShapeDtypeStruct((S, D), F32), jax.ShapeDtypeStruct((8, D), F32)],
        scratch_shapes=[pltpu.VMEM((tm, D), F32), pltpu.VMEM((tm, D), F32), pltpu.SemaphoreType.DMA((2,))],
        compiler_params=_cp(("arbitrary", "arbitrary")),
    )(dproj, win, dres, xs, r, nw)


W_IN_PARTS = 2


def _wgrad_in(h1, dproj, part):
    rows = D // W_IN_PARTS

    def body(a_ref, d_ref, o_ref):
        both = _dot_tn(a_ref[...], d_ref[...]).astype(BF16)
        o_ref[0] = both[:, 0:N_IN]
        o_ref[1] = both[:, N_IN:2 * N_IN]

    return pl.pallas_call(
        body, name=f"wgrad_in_{part}", grid=(NDEV // 2,),
        in_specs=[pl.BlockSpec((S, rows), lambda p: (0, part)), pl.BlockSpec((S, 2 * N_IN), lambda p: (0, p))],
        out_specs=pl.BlockSpec((2, rows, N_IN), lambda p: (p, 0, 0)),
        out_shape=jax.ShapeDtypeStruct((NDEV, rows, N_IN), BF16),
        compiler_params=_cp(("parallel",)),
    )(h1, dproj)


def _wgrad_rows(a3, dy, name, col=0):
    def body(a_ref, d_ref, o_ref):
        o_ref[...] = _dot_tn(a_ref[...], d_ref[...]).astype(BF16)

    return pl.pallas_call(
        body, name=name, grid=(NFG,),
        in_specs=[pl.BlockSpec((None, S, N_FG), lambda p: (p, 0, col)), pl.BlockSpec((S, D), lambda p: (0, 0))],
        out_specs=pl.BlockSpec((None, N_FG, D), lambda p: (p, 0, 0)),
        out_shape=jax.ShapeDtypeStruct((NFG, N_FG, D), BF16),
        compiler_params=_cp(("parallel",)),
    )(a3, dy).reshape(NDEV, N_FF, D)


def _wgrad_out(ma, mr, dx2b):
    half = D // 2
    per = half // N_OUT

    def body(ma_ref, mr_ref, d_ref, o_ref):
        p = pl.program_id(0)

        @pl.when(p == 0)
        def _():
            o_ref[...] = _dot_tn(ma_ref[...], d_ref[...]).astype(BF16).reshape(per, N_OUT, D)

        @pl.when(p == 1)
        def _():
            o_ref[...] = _dot_tn(mr_ref[...], d_ref[...]).astype(BF16).reshape(per, N_OUT, D)

    whole = pl.BlockSpec((S, half), lambda p: (0, 0))
    return pl.pallas_call(
        body, name="wgrad_out", grid=(2,),
        in_specs=[whole, whole, pl.BlockSpec((S, D), lambda p: (0, 0))],
        out_specs=pl.BlockSpec((per, N_OUT, D), lambda p: (p, 0, 0)),
        out_shape=jax.ShapeDtypeStruct((NDEV, N_OUT, D), BF16),
        compiler_params=_cp(("parallel",)),
    )(ma, mr, dx2b)


def _attn_consts():
    c = np.zeros((AH, 8, AHD), np.float32)
    for h in range(AH):
        c[h, :, :] = 2.0 ** (-(h + 1))
    return jnp.asarray(c)


def _permute_in(dst, src, d, cast=None):
    v = src[...]
    if d > 1:
        v = pltpu.einshape("jrc->rjc", v.reshape(S // d, d, AHD)).reshape(S, AHD)
    dst[...] = v if cast is None else v.astype(cast)


def _natural_order(v, d):
    if d == 1:
        return v
    return pltpu.einshape("rjc->jrc", v.reshape(d, S // d, AHD)).reshape(S, AHD)


def _attn_masks():
    qi = lax.broadcasted_iota(jnp.int32, (CH, CH), 0)
    kj = lax.broadcasted_iota(jnp.int32, (CH, CH), 1)
    dist_c = (qi - kj).astype(F32)
    dist_p = (qi - kj + CH).astype(F32)
    return (qi >= kj)[None], (kj >= qi)[None], dist_c[None], dist_p[None]


GB = 16


def _bdot_nt(a, b):
    return lax.dot_general(a, b, (((2,), (2,)), ((0,), (0,))), preferred_element_type=F32)


def _bdot(a, b):
    return lax.dot_general(a, b, (((2,), (1,)), ((0,), (0,))), preferred_element_type=F32)


def _bdot_tn(a, b):
    return lax.dot_general(a, b, (((1,), (1,)), ((0,), (0,))), preferred_element_type=F32)


def _shift_block(dst, src):
    dst[0:CH, :] = jnp.zeros((CH, AHD), dst.dtype)
    dst[CH:S, :] = src[0:S - CH, :]


def _has_prev(g, nb):
    blk = lax.broadcasted_iota(jnp.int32, (GB, 1, 1), 0) + g * GB
    return (blk & (nb - 1)) != 0


def _blocks(ref, g):
    return ref[g * GB * CH:(g + 1) * GB * CH, :].reshape(GB, CH, AHD)


def _attn_fwd(proj):
    scale = 1.0 / math.sqrt(AHD)

    def body(c_ref, q_ref, k_ref, v_ref, o_ref, ob_ref, lse_ref, qkvp_ref, lsep_ref, qd, kd, vd, kps, vps, od, ld, *nat):
        onat, lnat = nat[0:3], nat[3:6]
        slope = c_ref[0:1, :]
        mask_c, mask_p, dist_c, dist_p = _attn_masks()
        for pi, (d, nb) in enumerate(PATTERNS):
            _permute_in(qd, q_ref, d, BF16)
            _permute_in(kd, k_ref, d, BF16)
            _permute_in(vd, v_ref, d, BF16)
            if d > 1:
                qkvp_ref[pi - 1, 0] = qd[...]
                qkvp_ref[pi - 1, 1] = kd[...]
                qkvp_ref[pi - 1, 2] = vd[...]
            if nb > 1:
                _shift_block(kps, kd)
                _shift_block(vps, vd)
            bias_c = -(slope * float(d)) * dist_c
            bias_p = -(slope * float(d)) * dist_p
            for g in range(NB // GB):
                q3, k3, v3 = _blocks(qd, g), _blocks(kd, g), _blocks(vd, g)
                s_c = jnp.where(mask_c, _bdot_nt(q3, k3) * scale + bias_c, NEG)
                mx = jnp.max(s_c, axis=-1, keepdims=True)
                if nb > 1:
                    kp3, vp3 = _blocks(kps, g), _blocks(vps, g)
                    s_p = jnp.where(jnp.logical_and(mask_p, _has_prev(g, nb)),
                                    _bdot_nt(q3, kp3) * scale + bias_p, NEG)
                    mx = jnp.maximum(mx, jnp.max(s_p, axis=-1, keepdims=True))
                    l = (jnp.sum(jnp.exp(s_c - mx), axis=-1, keepdims=True)
                         + jnp.sum(jnp.exp(s_p - mx), axis=-1, keepdims=True))
                    lse = mx + jnp.log(l)
                    o3 = _bdot(jnp.exp(s_c - lse).astype(BF16), v3) + _bdot(jnp.exp(s_p - lse).astype(BF16), vp3)
                else:
                    l = jnp.sum(jnp.exp(s_c - mx), axis=-1, keepdims=True)
                    lse = mx + jnp.log(l)
                    o3 = _bdot(jnp.exp(s_c - lse).astype(BF16), v3)
                rows = slice(g * GB * CH, (g + 1) * GB * CH)
                od[rows, :] = o3.reshape(GB * CH, AHD)
                ld[rows, :] = jnp.broadcast_to(lse, (GB, CH, AHD)).reshape(GB * CH, AHD)
            onat[pi][...] = _natural_order(od[...], d)
            lnat[pi][...] = _natural_order(ld[...], d)
        l0, l1, l2 = lnat[0][...], lnat[1][...], lnat[2][...]
        mx = jnp.maximum(jnp.maximum(l0, l1), l2)
        e0, e1, e2 = jnp.exp(l0 - mx), jnp.exp(l1 - mx), jnp.exp(l2 - mx)
        den = e0 + e1 + e2
        out = (e0 / den) * onat[0][...] + (e1 / den) * onat[1][...] + (e2 / den) * onat[2][...]
        o_ref[...] = out
        ob_ref[...] = out.astype(BF16)
        lse_ref[...] = mx + jnp.log(den)
        for pi, (d, _) in enumerate(PATTERNS[1:]):
            _permute_in(lsep_ref.at[pi], lse_ref, d)

    def col(off):
        return pl.BlockSpec((S, AHD), lambda h: (0, off + h))

    return pl.pallas_call(
        body, name="attn_fwd", grid=(AH,),
        in_specs=[pl.BlockSpec((None, 8, AHD), lambda h: (h, 0, 0)), col(0), col(AH), col(2 * AH)],
        out_specs=[col(0), col(0), col(0), pl.BlockSpec((2, 3, S, AHD), lambda h: (0, 0, 0, h)),
                   pl.BlockSpec((2, S, AHD), lambda h: (0, 0, h))],
        out_shape=[jax.ShapeDtypeStruct((S, AH * AHD), F32), jax.ShapeDtypeStruct((S, AH * AHD), BF16),
                   jax.ShapeDtypeStruct((S, AH * AHD), F32),
                   jax.ShapeDtypeStruct((2, 3, S, AH * AHD), BF16), jax.ShapeDtypeStruct((2, S, AH * AHD), F32)],
        scratch_shapes=[pltpu.VMEM((S, AHD), BF16) for _ in range(5)]
        + [pltpu.VMEM((S, AHD), F32) for _ in range(8)],
        compiler_params=_cp(("parallel",)),
    )(_attn_consts(), proj, proj, proj)


def _attn_bwd(proj, dmixed, o, lse, qkvp, lsep):
    scale = 1.0 / math.sqrt(AHD)

    def body(c_ref, q_ref, k_ref, v_ref, do_ref, o_ref, lse_ref, qkvp_ref, lsep_ref, dproj_hbm,
             qd, kd, vd, dod, kps, vps, dld, dqd, dkd, dvd, delta, aq, ak, av, sq, sk, sv, sems):
        h = pl.program_id(0)

        def out_copies(head):
            return [pltpu.make_async_copy(
                st, dproj_hbm.at[:, pl.ds(pl.multiple_of((k * AH + head) * AHD, AHD), AHD)], sems.at[k])
                for k, st in enumerate((sq, sk, sv))]

        slope = c_ref[0:1, :]
        mask_c, mask_p, dist_c, dist_p = _attn_masks()
        delta[...] = jnp.broadcast_to(jnp.sum(do_ref[...] * o_ref[...], axis=-1, keepdims=True), (S, AHD))
        for pi, (d, nb) in enumerate(PATTERNS):
            if d == 1:
                _permute_in(qd, q_ref, d, BF16)
                _permute_in(kd, k_ref, d, BF16)
                _permute_in(vd, v_ref, d, BF16)
                qs, ks, vs, lss = qd, kd, vd, lse_ref
            else:
                qs, ks, vs, lss = (qkvp_ref.at[pi - 1, 0], qkvp_ref.at[pi - 1, 1], qkvp_ref.at[pi - 1, 2],
                                   lsep_ref.at[pi - 1])
            _permute_in(dod, do_ref, d, BF16)
            _permute_in(dld, delta, d)
            if nb > 1:
                _shift_block(kps, ks)
                _shift_block(vps, vs)
            bias_c = -(slope * float(d)) * dist_c
            bias_p = -(slope * float(d)) * dist_p
            for g in range(NB // GB):
                q3, k3, v3, do3 = _blocks(qs, g), _blocks(ks, g), _blocks(vs, g), _blocks(dod, g)
                ls, dl = _blocks(lss, g), _blocks(dld, g)
                lo, hi = g * GB * CH, (g + 1) * GB * CH
                p_c = jnp.exp(jnp.where(mask_c, _bdot_nt(q3, k3) * scale + bias_c, NEG) - ls)
                ds_c = ((p_c * (_bdot_nt(do3, v3) - dl)) * scale).astype(BF16)
                dq3 = _bdot(ds_c, k3)
                dkd[lo:hi, :] = _bdot_tn(ds_c, q3).reshape(GB * CH, AHD)
                dvd[lo:hi, :] = _bdot_tn(p_c.astype(BF16), do3).reshape(GB * CH, AHD)
                if nb > 1:
                    kp3, vp3 = _blocks(kps, g), _blocks(vps, g)
                    p_p = jnp.exp(jnp.where(jnp.logical_and(mask_p, _has_prev(g, nb)),
                                            _bdot_nt(q3, kp3) * scale + bias_p, NEG) - ls)
                    ds_p = ((p_p * (_bdot_nt(do3, vp3) - dl)) * scale).astype(BF16)
                    dq3 = dq3 + _bdot(ds_p, kp3)
                    dkp = _bdot_tn(ds_p, q3).reshape(GB * CH, AHD)
                    dvp = _bdot_tn(p_p.astype(BF16), do3).reshape(GB * CH, AHD)
                    if g == 0:
                        dkd[0:hi - CH, :] += dkp[CH:, :]
                        dvd[0:hi - CH, :] += dvp[CH:, :]
                    else:
                        dkd[lo - CH:hi - CH, :] += dkp
                        dvd[lo - CH:hi - CH, :] += dvp
                dqd[lo:hi, :] = dq3.reshape(GB * CH, AHD)
            ln = S // d
            for acc, src in ((aq, dqd), (ak, dkd), (av, dvd)):
                if pi == 0:
                    acc[...] = src[...]
                else:
                    acc[...] += _natural_order(src[...], d)

        @pl.when(h > 0)
        def _():
            for cp in out_copies(h - 1):
                cp.wait()

        sq[...] = aq[...].astype(BF16)
        sk[...] = ak[...].astype(BF16)
        sv[...] = av[...].astype(BF16)
        for cp in out_copies(h):
            cp.start()

        @pl.when(h == AH - 1)
        def _():
            for cp in out_copies(h):
                cp.wait()

    def col(off):
        return pl.BlockSpec((S, AHD), lambda h: (0, off + h))

    return pl.pallas_call(
        body, name="attn_bwd", grid=(AH,),
        in_specs=[pl.BlockSpec((None, 8, AHD), lambda h: (h, 0, 0)), col(0), col(AH), col(2 * AH),
                  col(0), col(0), col(0), pl.BlockSpec((2, 3, S, AHD), lambda h: (0, 0, 0, h)),
                  pl.BlockSpec((2, S, AHD), lambda h: (0, 0, h))],
        out_specs=pl.BlockSpec(memory_space=pl.ANY),
        out_shape=jax.ShapeDtypeStruct((S, NDEV * N_IN), BF16),
        scratch_shapes=[pltpu.VMEM((S, AHD), BF16) for _ in range(6)]
        + [pltpu.VMEM((S, AHD), F32) for _ in range(8)]
        + [pltpu.VMEM((S, AHD), BF16) for _ in range(3)] + [pltpu.SemaphoreType.DMA((3,))],
        compiler_params=_cp(("arbitrary",)),
    )(_attn_consts(), proj, proj, proj, dmixed, o, lse, qkvp, lsep)


def _ret_consts():
    c = np.zeros((RH, 8, RHD), np.float32)
    for h in range(RH):
        c[h, :, :] = np.log(np.float32(1.0) - np.float32(2.0 ** (-5.0 - h)))
    return jnp.asarray(c)


def _ret_factors(lg):
    i = lax.broadcasted_iota(jnp.int32, (CH, CH), 0)
    j = lax.broadcasted_iota(jnp.int32, (CH, CH), 1)
    dif = (i - j).astype(F32)
    decay = jnp.where(dif >= 0, jnp.exp(lg[:, 0:CH] * jnp.maximum(dif, 0.0)), 0.0)
    row = lax.broadcasted_iota(jnp.int32, (CH, RHD), 0).astype(F32)
    zeta = jnp.exp(lg * (CH - 1.0 - row))
    xi = jnp.exp(lg * (row + 1.0))
    return decay, zeta, xi, jnp.exp(lg * float(CH))


CBK = 8
RSTEPS = NB // CBK


def _ret_specs(rev):
    off = 3 * AH * AHD // RHD
    rows = CBK * CH

    def ch(n):
        return (RSTEPS - 1 - n) if rev else n

    def col(k):
        return pl.BlockSpec((rows, RHD), lambda h, n: (ch(n), off + k * RH + h))

    own = pl.BlockSpec((rows, RHD), lambda h, n: (ch(n), h))
    state = pl.BlockSpec((None, CBK, RHD, RHD), lambda h, n: (h, ch(n), 0, 0))
    const = pl.BlockSpec((None, 8, RHD), lambda h, n: (h, 0, 0))
    dm = pl.BlockSpec((rows, RHD), lambda h, n: (ch(n), AH * AHD // RHD + h))
    return col, own, state, const, dm


def _chunks(x):
    return x.reshape(CBK, CH, RHD)


def _ret_fwd(proj):
    def body(c_ref, q_ref, k_ref, v_ref, g_ref, ret_ref, mr_ref, st_ref, r_acc):
        n = pl.program_id(1)

        @pl.when(n == 0)
        def _():
            r_acc[...] = jnp.zeros_like(r_acc)

        decay, zeta, xi, gch = _ret_factors(c_ref[0:1, :])
        q3 = _chunks(q_ref[...].astype(BF16))
        kc = _chunks(k_ref[...] * (1.0 / math.sqrt(RHD)))
        k3 = kc.astype(BF16)
        v3 = _chunks(v_ref[...].astype(BF16))
        kv3 = _bdot_tn((kc * zeta[None]).astype(BF16), v3)
        r = r_acc[...]
        for i in range(CBK):
            st_ref[i] = r.astype(BF16)
            r = r * gch + kv3[i]
        r_acc[...] = r
        scores = _bdot_nt(q3, k3) * decay[None]
        ret = (_bdot(scores.astype(BF16), v3) + _bdot(q3, st_ref[...]) * xi[None]).reshape(CBK * CH, RHD)
        ret_ref[...] = ret
        rr = lax.rsqrt(jnp.mean(ret * ret, axis=-1, keepdims=True) + EPS)
        gv = g_ref[...]
        mr_ref[...] = ((gv * _sigmoid(gv)) * (ret * rr)).astype(BF16)

    col, own, state, const, _ = _ret_specs(False)
    return pl.pallas_call(
        body, name="ret_fwd", grid=(RH, RSTEPS),
        in_specs=[const, col(0), col(1), col(2), col(3)],
        out_specs=[own, own, state],
        out_shape=[jax.ShapeDtypeStruct((S, RH * RHD), F32), jax.ShapeDtypeStruct((S, RH * RHD), BF16),
                   jax.ShapeDtypeStruct((RH, NB, RHD, RHD), BF16)],
        scratch_shapes=[pltpu.VMEM((RHD, RHD), F32)],
        compiler_params=_cp(("parallel", "arbitrary")),
    )(_ret_consts(), proj, proj, proj, proj)


def _ret_bwd(proj, ret, states, dmixed, dproj):
    rows = CBK * CH
    col0 = 3 * AH * AHD

    def body(c_ref, q_ref, k_ref, v_ref, g_ref, ret_ref, st_ref, dm_ref, dproj_in, dproj_hbm, g_acc, gs,
             sq, sk, sv, sg, sems):
        del dproj_in
        h, n = pl.program_id(0), pl.program_id(1)
        step = h * RSTEPS + n

        def out_copies(t):
            hh, nn = t // RSTEPS, t % RSTEPS
            r0 = pl.multiple_of((RSTEPS - 1 - nn) * rows, rows)
            return [pltpu.make_async_copy(
                st, dproj_hbm.at[pl.ds(r0, rows), pl.ds(pl.multiple_of(col0 + (k * RH + hh) * RHD, RHD), RHD)],
                sems.at[k]) for k, st in enumerate((sq, sk, sv, sg))]

        @pl.when(n == 0)
        def _():
            g_acc[...] = jnp.zeros_like(g_acc)

        decay, zeta, xi, gch = _ret_factors(c_ref[0:1, :])
        ret_v = ret_ref[...]
        rr = lax.rsqrt(jnp.mean(ret_v * ret_v, axis=-1, keepdims=True) + EPS)
        gv = g_ref[...]
        sgm = _sigmoid(gv)
        dmix = dm_ref[...]
        dgate = ((dmix * (ret_v * rr)) * (sgm * (1.0 + gv * (1.0 - sgm)))).astype(BF16)
        dretn = dmix * (gv * sgm)
        dret = _chunks(rr * dretn - ret_v * ((rr * rr * rr) * jnp.mean(dretn * ret_v, axis=-1, keepdims=True)))

        q3 = _chunks(q_ref[...].astype(BF16))
        kc = _chunks(k_ref[...] * (1.0 / math.sqrt(RHD)))
        k3 = kc.astype(BF16)
        v3 = _chunks(v_ref[...].astype(BF16))
        d3 = dret.astype(BF16)
        dxi = (dret * xi[None]).astype(BF16)
        kz = (kc * zeta[None]).astype(BF16)
        dr3 = _bdot_tn(q3, dxi)
        acc = g_acc[...]
        for i in reversed(range(CBK)):
            gs[i] = acc.astype(BF16)
            acc = dr3[i] + gch * acc
        g_acc[...] = acc
        g3 = gs[...]
        sc = (_bdot_nt(q3, k3) * decay[None]).astype(BF16)
        da = (_bdot_nt(d3, v3) * decay[None]).astype(BF16)
        dq = _bdot(da, k3) + _bdot_nt(dxi, st_ref[...])
        dkc = _bdot_tn(da, q3) + _bdot_nt(v3, g3) * zeta[None]
        dv = _bdot_tn(sc, d3) + _bdot(kz, g3)

        @pl.when(step > 0)
        def _():
            for cp in out_copies(step - 1):
                cp.wait()

        sq[...] = dq.reshape(rows, RHD).astype(BF16)
        sk[...] = (dkc * (1.0 / math.sqrt(RHD))).reshape(rows, RHD).astype(BF16)
        sv[...] = dv.reshape(rows, RHD).astype(BF16)
        sg[...] = dgate
        for cp in out_copies(step):
            cp.start()

        @pl.when(step == RH * RSTEPS - 1)
        def _():
            for cp in out_copies(step):
                cp.wait()

    col, own, state, const, dm = _ret_specs(True)
    hbm = pl.BlockSpec(memory_space=pl.ANY)
    return pl.pallas_call(
        body, name="ret_bwd", grid=(RH, RSTEPS),
        in_specs=[const, col(0), col(1), col(2), col(3), own, state, dm, hbm],
        out_specs=hbm,
        out_shape=jax.ShapeDtypeStruct(dproj.shape, dproj.dtype),
        input_output_aliases={8: 0},
        scratch_shapes=[pltpu.VMEM((RHD, RHD), F32), pltpu.VMEM((CBK, RHD, RHD), BF16)]
        + [pltpu.VMEM((rows, RHD), BF16) for _ in range(4)] + [pltpu.SemaphoreType.DMA((4,))],
        compiler_params=_cp(("arbitrary", "arbitrary")),
    )(_ret_consts(), proj, proj, proj, proj, ret, states, dmixed, dproj)


class _NoReduction:
    def start(self, group, grads):
        pass

    def local(self, name, first=()):
        return []

    def landed(self, name):
        return []

    def update(self, name):
        return []

    place = None

    def rider(self, name):
        return None

    def set_update(self, name, outs):
        pass


def _local_step(x, tgt, nw1, nw2, nw3, win, wout, wgu, wd, red=None):
    red = red or _NoReduction()

    def after(values, first):
        return lax.optimization_barrier((tuple(values), tuple(first)))[0]

    wd = wd.reshape(NFG, N_FG, D)
    h1, r1 = _rms_fwd(x, nw1)
    proj = _proj(h1, win)
    o, ma, lse, qkvp, lsep = _attn_fwd(proj)
    ret, mr, states = _ret_fwd(proj)
    x2, h2, r2 = _out_proj_rms(x, ma, mr, wout, nw2)
    a, dadg, dadu = _ffn_up(h2, wgu)
    dx3, dx3b, st3 = _ffn_down_loss(x2, a, wd, nw3, tgt)

    dwd = _wgrad_rows(a, dx3b, "wgrad_down")
    red.start(["w_down"], [dwd])
    (dx3b,) = after([dx3b], [dwd])
    part = _ffn_down_bwd(dx3b, wd, dadg, dadu, 0)
    (dx3b,) = after([dx3b], red.local("w_down", first=[part]))
    dgu = _ffn_down_bwd(dx3b, wd, dadg, dadu, 1, [part])
    dwg = _wgrad_rows(dgu, h2, "wgrad_gate", 0)
    red.start(["w_gate"], [dwg])
    (dgu,) = after([dgu], [dwg])
    dwu = _wgrad_rows(dgu, h2, "wgrad_up", 1)
    red.start(["w_up"], [dwu])
    (dgu,) = after([dgu], red.local("w_gate", first=[dwu] + red.landed("w_down")))
    dx2, dx2b, st2 = _ffn_up_bwd(dgu, wgu, dx3, x2, r2, nw2)
    (dx2b,) = after([dx2b], red.local("w_up", first=[dx2b]))
    dwo = _wgrad_out(ma, mr, dx2b)
    red.start(["w_out"], [dwo])
    (dx2b,) = after([dx2b], [dwo])
    dmixed, done = _out_proj_bwd(dx2b, wout, red.place, red.rider("w_down"))
    red.set_update("w_down", done)
    dproj = _attn_bwd(proj, dmixed, o, lse, qkvp, lsep)
    (dmixed,) = after([dmixed], red.local("w_out", first=[dproj] + red.landed("w_gate")))
    dproj = _ret_bwd(proj, ret, states, dmixed, dproj)
    (dwi0,) = after([_wgrad_in(h1, dproj, 0)], red.landed("w_up"))
    red.start(["w_in_0"], [dwi0])
    (dproj,) = after([dproj], [dwi0])
    dwi1 = _wgrad_in(h1, dproj, 1)
    red.start(["w_in_1"], [dwi1])
    sums = red.local("w_in_0", first=[dwi1] + red.landed("w_out"))
    sums = red.local("w_in_1", first=sums + red.update("w_gate"))
    (dproj,) = after([dproj], sums)
    gx, st1 = _in_proj_bwd(dproj, win, dx2, x, r1, nw1)
    dwi = jnp.concatenate([dwi0, dwi1], axis=1)
    stats = jnp.concatenate([st1[0:1], st2[0:1], st3[0:2], jnp.zeros((4, D), F32)], axis=0)
    return stats, gx, dwi, dwo, dwg, dwu, dwd


def _place():
    x, y, c = lax.axis_index("x"), lax.axis_index("y"), lax.axis_index("c")
    return x, y, c, [(1 - x, y), (x, 1 - y), (1 - x, 1 - y)]


def _handshake(peers):
    barrier = pltpu.get_barrier_semaphore()
    for peer in peers:
        pl.semaphore_signal(barrier, inc=1, device_id=peer, device_id_type=MESH)
    pl.semaphore_wait(barrier, len(peers))


def _all_gather(shards, name, collective_id, stacked=False):
    na = len(shards)
    nout = 1 if stacked else na
    SIB, XN0, XN1, YN1, YN0, VIA_X, VIA_Y = 0, 1, 2, 3, 4, 5, 6
    D2D = {XN0: 7, XN1: 8, YN1: 9, YN0: 10, VIA_X: 11, VIA_Y: 12}

    def body(*refs):
        ins, outs = refs[:na], refs[na:na + nout]
        send_sems, recv_sems, local_sems = refs[na + nout:]
        x, y, c, _ = _place()
        me, sib = (x, y, c), (x, y, 1 - c)
        xn, yn, dg = (1 - x, y, c), (x, 1 - y, c), (1 - x, 1 - y, c)
        _handshake([sib, xn, yn])

        def part(ref, h):
            rows = ref.shape[0] // 2
            return ref if h is None else ref.at[pl.ds(h * rows, rows)]

        def block(a, owner, h):
            idx = 4 * owner[0] + 2 * owner[1] + owner[2]
            if not stacked:
                return part(outs[a].at[idx], h)
            rows = shards[a].shape[0]
            return part(outs[0].at[idx // 2, a, pl.ds(pl.multiple_of((idx % 2) * rows, rows), rows)], h)

        def copy(a, k, owner, h, to, own_src=False):
            return pltpu.make_async_remote_copy(
                src_ref=part(ins[a], h) if own_src else block(a, owner, h), dst_ref=block(a, owner, h),
                send_sem=send_sems.at[a, k], recv_sem=recv_sems.at[a, k], device_id=to, device_id_type=MESH)

        def other(p):
            return (p[0], p[1], 1 - c)

        mine = [pltpu.make_async_copy(ins[a], block(a, me, None), local_sems.at[a]) for a in range(na)]
        for cp in mine:
            cp.start()
        sent = []
        for a in range(na):
            sent += [copy(a, XN0, me, 0, xn, True), copy(a, YN1, me, 1, yn, True),
                     copy(a, XN1, me, 1, xn, True), copy(a, YN0, me, 0, yn, True)]
        sent += [copy(a, SIB, me, None, sib, True) for a in range(na)]
        for cp in sent:
            cp.start()

        def landed(a, k, owner, h, then):
            copy(a, k, owner, h, me).wait_recv()
            for k2, to in then + [(D2D[k], sib)]:
                cp = copy(a, k2, owner, h, to)
                cp.start()
                sent.append(cp)

        for a in range(na):
            landed(a, XN0, xn, 0, [(VIA_Y, yn)])
            landed(a, YN1, yn, 1, [(VIA_X, xn)])
            landed(a, XN1, xn, 1, [])
            landed(a, YN0, yn, 0, [])
        for a in range(na):
            landed(a, VIA_Y, dg, 0, [])
            landed(a, VIA_X, dg, 1, [])
        for a in range(na):
            copy(a, SIB, sib, None, me).wait_recv()
            for k, owner, h in ((XN0, xn, 0), (XN1, xn, 1), (YN1, yn, 1), (YN0, yn, 0), (VIA_Y, dg, 0), (VIA_X, dg, 1)):
                copy(a, D2D[k], other(owner), h, me).wait_recv()
        for cp in sent:
            cp.wait_send()
        for cp in mine:
            cp.wait()

    if stacked:
        r, c = shards[0].shape
        out_type = [jax.ShapeDtypeStruct((NDEV // 2, na, 2 * r, c), shards[0].dtype)]
    else:
        out_type = [jax.ShapeDtypeStruct((NDEV,) + s.shape, s.dtype) for s in shards]
    return _sequencer_call(
        body, name, collective_id, out_type,
        [pltpu.SemaphoreType.DMA((na, 13)), pltpu.SemaphoreType.DMA((na, 13)), pltpu.SemaphoreType.DMA((na,))])(*shards)


def _sequencer_call(body, name, collective_id, out_type, scratch_types):
    return pl.kernel(
        body, name=name, out_type=out_type,
        mesh=plsc.ScalarSubcoreMesh(axis_name="sequencer", num_cores=1),
        scratch_types=scratch_types,
        compiler_params=pltpu.CompilerParams(collective_id=collective_id))


def _exchange_sibling(grads, name, collective_id):
    na = len(grads)

    def body(*refs):
        ins, outs = refs[:na], refs[na:2 * na]
        send_sems, recv_sems = refs[2 * na:]
        x, y, c, _ = _place()
        _handshake([(x, y, 1 - c)])
        cps = []
        for a in range(na):
            for k in range(4):
                cps.append(pltpu.make_async_remote_copy(
                    src_ref=ins[a].at[2 * k + (1 - c)], dst_ref=outs[a].at[k],
                    send_sem=send_sems.at[a, k], recv_sem=recv_sems.at[a, k],
                    device_id=(x, y, 1 - c), device_id_type=MESH))
        for cp in cps:
            cp.start()
        for cp in cps:
            cp.wait()

    return _sequencer_call(
        body, name, collective_id,
        [jax.ShapeDtypeStruct((4,) + g.shape[1:], g.dtype) for g in grads],
        [pltpu.SemaphoreType.DMA((na, 4)), pltpu.SemaphoreType.DMA((na, 4))])(*grads)


def _row_tile(rows, cols):
    for t in (512, 256, 176, 128, 64, 32, 16):
        if rows % t == 0 and t * cols * 4 <= (2 << 20):
            return t
    raise ValueError((rows, cols))


def _chip_sum(place, g, got, name):
    _, r, c = g.shape
    tm = r

    def body(pos_ref, g_ref, got_ref, o_ref):
        o_ref[...] = (g_ref[...].astype(F32) + got_ref[...].astype(F32)).astype(BF16)

    def chip(j, pos):
        return 2 * (pos[0] ^ jnp.where(j == 1, 0, 1)) + (pos[1] ^ jnp.where(j == 0, 0, 1))

    return pl.pallas_call(
        body, name=name,
        grid_spec=pltpu.PrefetchScalarGridSpec(
            num_scalar_prefetch=1, grid=(3, r // tm),
            in_specs=[pl.BlockSpec((None, tm, c), lambda j, i, pos: (2 * chip(j, pos) + pos[2], i, 0)),
                      pl.BlockSpec((None, tm, c), lambda j, i, pos: (chip(j, pos), i, 0))],
            out_specs=pl.BlockSpec((None, tm, c), lambda j, i, pos: (j, i, 0))),
        out_shape=jax.ShapeDtypeStruct((3, r, c), BF16),
        compiler_params=_cp(("parallel", "parallel")),
    )(place, g, got)


def _exchange_chips(sums, name, collective_id):
    na = len(sums)

    def body(*refs):
        ins, outs = refs[:na], refs[na:2 * na]
        send_sems, recv_sems = refs[2 * na:]
        x, y, c, chips = _place()
        _handshake([(*chip, c) for chip in chips])
        cps = []
        for a in range(na):
            for j, chip in enumerate(chips):
                cps.append(pltpu.make_async_remote_copy(
                    src_ref=ins[a].at[j], dst_ref=outs[a].at[j],
                    send_sem=send_sems.at[a, j], recv_sem=recv_sems.at[a, j],
                    device_id=(*chip, c), device_id_type=MESH))
        for cp in cps:
            cp.start()
        for cp in cps:
            cp.wait()

    return _sequencer_call(
        body, name, collective_id,
        [jax.ShapeDtypeStruct((3,) + s.shape[1:], s.dtype) for s in sums],
        [pltpu.SemaphoreType.DMA((na, 3)), pltpu.SemaphoreType.DMA((na, 3))])(*sums)


def _exchange_stats(stats, collective_id):
    def body(st_in, st_out, st_send, st_recv, local_sem):
        x, y, c, _ = _place()
        me_idx = 4 * x + 2 * y + c
        peers = [(x ^ ((k >> 2) & 1), y ^ ((k >> 1) & 1), c ^ (k & 1)) for k in range(1, 8)]
        _handshake(peers)
        mine = pltpu.make_async_copy(st_in, st_out.at[me_idx], local_sem)
        mine.start()
        cps = [pltpu.make_async_remote_copy(
            src_ref=st_in, dst_ref=st_out.at[me_idx], send_sem=st_send.at[k], recv_sem=st_recv.at[k],
            device_id=peer, device_id_type=MESH) for k, peer in enumerate(peers)]
        for cp in cps:
            cp.start()
        for cp in cps:
            cp.wait()
        mine.wait()

    return _sequencer_call(
        body, "exchange_stats", collective_id,
        jax.ShapeDtypeStruct((NDEV,) + stats.shape, stats.dtype),
        [pltpu.SemaphoreType.DMA((7,)), pltpu.SemaphoreType.DMA((7,)), pltpu.SemaphoreType.DMA])(stats)


class _Reduction:
    def __init__(self, place, first_collective_id, state):
        self.place = place
        self.ids = iter(range(first_collective_id, 32))
        self.state = state
        self.groups = {}
        self.updates = {}

    def next_id(self):
        return next(self.ids)

    def start(self, group, grads):
        got = _exchange_sibling(grads, "sibling_exchange_" + group[0], self.next_id())
        self.groups[group[0]] = dict(names=group, grads=grads, got=got)

    def local(self, name, first=()):
        grp = self.groups[name]
        grads = lax.optimization_barrier((tuple(grp["grads"]), tuple(first)))[0]
        grp["sums"] = [_chip_sum(self.place, g, s, "chip_sum_" + n)
                       for g, s, n in zip(grads, grp["got"], grp["names"])]
        grp["chips"] = _exchange_chips(grp["sums"], "chip_exchange_" + name, self.next_id())
        return grp["sums"]

    def landed(self, name):
        return list(self.groups[name]["chips"])

    def rider(self, name):
        grp = next(g for g in self.groups.values() if name in g["names"])
        k = grp["names"].index(name)
        return self.state[name][:3] + (grp["grads"][k], grp["got"][k], grp["chips"][k])

    def set_update(self, name, outs):
        self.updates[name] = list(outs)

    def update(self, name):
        if name not in self.updates:
            grp = next(g for g in self.groups.values() if name in g["names"])
            k = grp["names"].index(name)
            w, m, v, part, parts = self.state[name]
            before = self.update(f"{name[:-1]}{part - 1}") if part else None
            self.updates[name] = _shard_update(self.place, w, m, v, grp["grads"][k], grp["got"][k],
                                               grp["chips"][k], "update_" + name, part, parts, before)
        return list(self.updates[name])


def _adamw(w, g, m, v):
    m = ADAM_B1 * m + (1.0 - ADAM_B1) * g
    v = ADAM_B2 * v + (1.0 - ADAM_B2) * (g * g)
    m_hat = m / (1.0 - ADAM_B1 ** ADAM_STEP)
    v_hat = v / (1.0 - ADAM_B2 ** ADAM_STEP)
    delta = -ADAM_LR * (m_hat / (jnp.sqrt(v_hat) + ADAM_EPS) + ADAM_WD * w)
    return delta, m, v


def _update_tile(w_ref, m_ref, v_ref, g_ref, s_ref, c_ref, go_ref, d_ref, mo_ref, vo_ref):
    grad = g_ref[...].astype(F32) + s_ref[...].astype(F32)
    for j in range(3):
        grad = grad + c_ref[j].astype(F32)
    delta, mn, vn = _adamw(w_ref[...], grad, m_ref[...], v_ref[...])
    go_ref[...] = grad
    d_ref[...] = delta
    mo_ref[...] = mn
    vo_ref[...] = vn


def _shard_update(place, w, m, v, g, got_sib, got_chips, name, part=0, parts=1, before=None):
    r, c = w.shape
    rp = r // parts
    tm = _row_tile(rp, c)
    off = part * (rp // tm)

    def body(pos_ref, w_ref, m_ref, v_ref, g_ref, s_ref, c_ref, *rest):
        _update_tile(w_ref, m_ref, v_ref, g_ref, s_ref, c_ref, *rest[-4:])

    row = pl.BlockSpec((tm, c), lambda i, pos: (i + off, 0))
    before = list(before or [])
    return pl.pallas_call(
        body, name=name,
        grid_spec=pltpu.PrefetchScalarGridSpec(
            num_scalar_prefetch=1, grid=(rp // tm,),
            in_specs=[row, row, row,
                      pl.BlockSpec((None, tm, c), lambda i, pos: (4 * pos[0] + 2 * pos[1] + pos[2], i, 0)),
                      pl.BlockSpec((None, tm, c), lambda i, pos: (2 * pos[0] + pos[1], i, 0)),
                      pl.BlockSpec((3, tm, c), lambda i, pos: (0, i, 0))]
            + [pl.BlockSpec(memory_space=pl.ANY)] * len(before),
            out_specs=[row, row, row, row]),
        out_shape=[jax.ShapeDtypeStruct((r, c), F32)] * 4,
        input_output_aliases={7 + k: k for k in range(len(before))},
        compiler_params=_cp(("parallel",)),
    )(place, w, m, v, g, got_sib, got_chips, *before)


def _small_update(stats_all, ws, ms, vs):
    def body(st_ref, w_ref, m_ref, v_ref, go_ref, d_ref, mo_ref, vo_ref):
        grad = st_ref[0]
        for k in range(1, NDEV):
            grad = grad + st_ref[k]
        delta, mn, vn = _adamw(w_ref[...], grad, m_ref[...], v_ref[...])
        go_ref[...] = grad
        d_ref[...] = delta
        mo_ref[...] = mn
        vo_ref[...] = vn

    return pl.pallas_call(
        body, name="small_update",
        out_shape=[jax.ShapeDtypeStruct((8, D), F32)] * 4,
        compiler_params=_cp(),
    )(stats_all, ws, ms, vs)


def kernel(x, norm_mix_w, w_in, w_out, norm_ffn_w, w_gate, w_up, w_down, norm_final_w, loss_target, m_norm_mix_w, m_w_in, m_w_out, m_norm_ffn_w, m_w_gate, m_w_up, m_w_down, m_norm_final_w, v_norm_mix_w, v_w_in, v_w_out, v_norm_ffn_w, v_w_gate, v_w_up, v_w_down, v_norm_final_w):
    tr = {"w_gate", "w_up"}
    names = ["w_in", "w_out", "w_gate", "w_up", "w_down"]

    def view(a, n):
        return a[0].T if n in tr else a[0]

    big_w = [view(a, n) for a, n in zip([w_in, w_out, w_gate, w_up, w_down], names)]
    big_m = [view(a, n) for a, n in zip([m_w_in, m_w_out, m_w_gate, m_w_up, m_w_down], names)]
    big_v = [view(a, n) for a, n in zip([v_w_in, v_w_out, v_w_gate, v_w_up, v_w_down], names)]

    shards = [_cast_bf16(w, "cast_" + n) for w, n in zip(big_w, names)]
    (win,) = _all_gather(shards[0:1], "all_gather_w_in", 1)
    (wout,) = _all_gather(shards[1:2], "all_gather_w_out", 2)
    (wgu,) = _all_gather(shards[2:4], "all_gather_gate_up", 3, stacked=True)
    (wd,) = _all_gather(shards[4:5], "all_gather_w_down", 4)
    nw3 = norm_final_w.reshape(1, D)
    place = jnp.stack([lax.axis_index("x"), lax.axis_index("y"), lax.axis_index("c")]).astype(jnp.int32)
    state = {n: (w, m, v, 0, 1) for n, w, m, v in zip(names, big_w, big_m, big_v)}
    for part in range(W_IN_PARTS):
        state[f"w_in_{part}"] = state["w_in"][:3] + (part, W_IN_PARTS)
    red = _Reduction(place, 5, state)
    stats, gx, *_ = _local_step(
        x[0], loss_target[0], norm_mix_w, norm_ffn_w, nw3, win, wout.reshape(D, D), wgu.reshape(NFG, 2 * N_FG, D), wd, red)
    stats_all = _exchange_stats(stats, red.next_id())
    upd = [red.update(f"w_in_{W_IN_PARTS - 1}" if n == "w_in" else n) for n in names]
    stats_all = lax.optimization_barrier((stats_all, tuple(upd[0])))[0]

    def rows(a, b, c):
        return jnp.concatenate([a.reshape(1, D), b.reshape(1, D), c.reshape(1, D), jnp.zeros((5, D), F32)], axis=0)

    sg, sd, sm, sv = _small_update(stats_all, rows(norm_mix_w, norm_ffn_w, norm_final_w),
                                   rows(m_norm_mix_w, m_norm_ffn_w, m_norm_final_w),
                                   rows(v_norm_mix_w, v_norm_ffn_w, v_norm_final_w))
    loss = sg[3, 0]

    def outs(k, small):
        big = [(u[k].T if n in tr else u[k])[None] for u, n in zip(upd, names)]
        return [small[0:1], big[0], big[1], small[1:2], big[2], big[3], big[4], small[2]]

    return (loss, gx[None], *outs(0, sg), *outs(1, sd), *outs(2, sm), *outs(3, sv))
```

```python
import math

import numpy as np
import jax
import jax.numpy as jnp
from jax import lax
from jax.experimental import pallas as pl
from jax.experimental.pallas import tpu as pltpu
from jax.experimental.pallas import tpu_sc as plsc

F32 = jnp.float32
BF16 = jnp.bfloat16

S = 2048
D = 2048
NDEV = 8
N_IN = 7168 // NDEV
N_FF = 5632 // NDEV
NFG, N_FG = NDEV // 2, 2 * N_FF
N_OUT = 2048 // NDEV
AH, AHD = 8, 128
RH, RHD = 4, 256
CH = 128
NB = S // CH
EPS = 1e-6
PATTERNS = ((1, 16), (4, 4), (16, 1))
NEG = -1e30
VMEM_LIMIT = 56 * 1024 * 1024

ADAM_LR, ADAM_B1, ADAM_B2, ADAM_EPS, ADAM_WD, ADAM_STEP = 0.001, 0.9, 0.999, 1e-08, 0.01, 10
MESH = pl.DeviceIdType.MESH


def _cp(sem=None):
    return pltpu.CompilerParams(dimension_semantics=sem, vmem_limit_bytes=VMEM_LIMIT)


def _dot(a, b):
    return jnp.dot(a, b, preferred_element_type=F32)


def _dot_nt(a, b):
    return lax.dot_general(a, b, (((1,), (1,)), ((), ())), preferred_element_type=F32)


def _dot_tn(a, b):
    return lax.dot_general(a, b, (((0,), (0,)), ((), ())), preferred_element_type=F32)


def _sigmoid(x):
    return 0.5 * jnp.tanh(0.5 * x) + 0.5


def _cast_bf16(w, name):
    r, c = w.shape
    tm = r if r <= 1024 else 512

    def body(w_ref, o_ref):
        o_ref[...] = w_ref[...].astype(BF16)

    return pl.pallas_call(
        body, name=name, grid=(r // tm,),
        in_specs=[pl.BlockSpec((tm, c), lambda i: (i, 0))],
        out_specs=pl.BlockSpec((tm, c), lambda i: (i, 0)),
        out_shape=jax.ShapeDtypeStruct((r, c), BF16),
        compiler_params=_cp(("parallel",)),
    )(w)


def _rms_fwd(x, nw):
    tm = 256

    def body(x_ref, w_ref, h_ref, r_ref):
        xs = x_ref[...]
        r = lax.rsqrt(jnp.mean(xs * xs, axis=-1, keepdims=True) + EPS)
        h_ref[...] = ((xs * r) * w_ref[...]).astype(BF16)
        r_ref[...] = r

    return pl.pallas_call(
        body, name="rms_fwd", grid=(S // tm,),
        in_specs=[pl.BlockSpec((tm, D), lambda i: (i, 0)), pl.BlockSpec((1, D), lambda i: (0, 0))],
        out_specs=[pl.BlockSpec((tm, D), lambda i: (i, 0)), pl.BlockSpec((tm, 1), lambda i: (i, 0))],
        out_shape=[jax.ShapeDtypeStruct((S, D), BF16), jax.ShapeDtypeStruct((S, 1), F32)],
        compiler_params=_cp(("parallel",)),
    )(x, nw)


def _row_copies(hbm_refs, bufs, sems, m, tm):
    rows = pl.ds(pl.multiple_of(m * tm, tm), tm)
    return [pltpu.make_async_copy(h.at[rows], b, sems.at[i]) for i, (h, b) in enumerate(zip(hbm_refs, bufs))]


def _rms_bwd_tile(dh, xs, r, nw):
    dnw = jnp.sum(dh * (xs * r), axis=0, keepdims=True)
    gy = dh * nw
    dx = r * gy - xs * ((r * r * r) * jnp.mean(gy * xs, axis=-1, keepdims=True))
    return dx, dnw


def _proj(h1, win):
    tm = 1024

    def body(a_ref, w_ref, o_ref):
        o_ref[...] = _dot(a_ref[...], w_ref[...])

    return pl.pallas_call(
        body, name="proj", grid=(NDEV, S // tm),
        in_specs=[pl.BlockSpec((tm, D), lambda p, m: (m, 0)),
                  pl.BlockSpec((None, D, N_IN), lambda p, m: (p, 0, 0))],
        out_specs=pl.BlockSpec((tm, N_IN), lambda p, m: (m, p)),
        out_shape=jax.ShapeDtypeStruct((S, NDEV * N_IN), F32),
        compiler_params=_cp(("parallel", "parallel")),
    )(h1, win)


def _out_proj_rms(x, ma, mr, wout, nw):
    tm = 256
    half = D // 2

    def body(x_ref, ma_ref, mr_ref, w_ref, nw_ref, x2_ref, h_ref, r_ref):
        acc = _dot(ma_ref[...], w_ref[0:half, :]) + _dot(mr_ref[...], w_ref[half:D, :])
        x2 = x_ref[...] + acc
        r = lax.rsqrt(jnp.mean(x2 * x2, axis=-1, keepdims=True) + EPS)
        x2_ref[...] = x2
        h_ref[...] = ((x2 * r) * nw_ref[...]).astype(BF16)
        r_ref[...] = r

    return pl.pallas_call(
        body, name="out_proj_rms", grid=(S // tm,),
        in_specs=[pl.BlockSpec((tm, D), lambda i: (i, 0)),
                  pl.BlockSpec((tm, half), lambda i: (i, 0)),
                  pl.BlockSpec((tm, half), lambda i: (i, 0)),
                  pl.BlockSpec((D, D), lambda i: (0, 0)),
                  pl.BlockSpec((1, D), lambda i: (0, 0))],
        out_specs=[pl.BlockSpec((tm, D), lambda i: (i, 0)), pl.BlockSpec((tm, D), lambda i: (i, 0)),
                   pl.BlockSpec((tm, 1), lambda i: (i, 0))],
        out_shape=[jax.ShapeDtypeStruct((S, D), F32), jax.ShapeDtypeStruct((S, D), BF16),
                   jax.ShapeDtypeStruct((S, 1), F32)],
        compiler_params=_cp(("parallel",)),
    )(x, ma, mr, wout, nw)


def _ffn_up(h2, wgu):
    tm = 512

    def body(h_ref, w_ref, a_ref, dadg_ref, dadu_ref):
        gu = _dot_nt(h_ref[...], w_ref[...])
        g, u = gu[:, 0:N_FG], gu[:, N_FG:2 * N_FG]
        sg = _sigmoid(g)
        silu = g * sg
        a_ref[...] = (silu * u).astype(BF16)
        dadg_ref[...] = (u * (sg * (1.0 + g * (1.0 - sg)))).astype(BF16)
        dadu_ref[...] = silu.astype(BF16)

    blk = pl.BlockSpec((None, tm, N_FG), lambda p, m: (p, m, 0))
    return pl.pallas_call(
        body, name="ffn_up", grid=(NFG, S // tm),
        in_specs=[pl.BlockSpec((tm, D), lambda p, m: (m, 0)),
                  pl.BlockSpec((None, 2 * N_FG, D), lambda p, m: (p, 0, 0))],
        out_specs=[blk, blk, blk],
        out_shape=[jax.ShapeDtypeStruct((NFG, S, N_FG), BF16)] * 3,
        compiler_params=_cp(("parallel", "parallel")),
    )(h2, wgu)


def _ffn_down_loss(x2, a, wd, nw, tgt):
    tm = 512

    def body(x2_hbm, a_ref, w_ref, nw_ref, t_hbm, dx_ref, dxb_ref, st_ref, acc_ref, x2_buf, t_buf, sems):
        m, p = pl.program_id(0), pl.program_id(1)
        tail_in = _row_copies((x2_hbm, t_hbm), (x2_buf, t_buf), sems, m, tm)

        @pl.when(p == 0)
        def _():
            acc_ref[...] = jnp.zeros_like(acc_ref)
            for cp in tail_in:
                cp.start()

        @pl.when((p == 0) & (m == 0))
        def _():
            st_ref[...] = jnp.zeros_like(st_ref)

        acc_ref[...] += _dot(a_ref[...], w_ref[...])

        @pl.when(p == NFG - 1)
        def _():
            for cp in tail_in:
                cp.wait()
            x3 = x2_buf[...] + acc_ref[...]
            nwv = nw_ref[...]
            r = lax.rsqrt(jnp.mean(x3 * x3, axis=-1, keepdims=True) + EPS)
            y = (x3 * r) * nwv
            err = y - t_buf[...]
            loss = 0.5 * jnp.sum(jnp.mean(err * err, axis=-1, keepdims=True), axis=0, keepdims=True)
            dy = err * (1.0 / D)
            dx, dnw = _rms_bwd_tile(dy, x3, r, nwv)
            dx_ref[...] = dx
            dxb_ref[...] = dx.astype(BF16)
            st_ref[0:1, :] += dnw
            st_ref[1:2, :] += jnp.broadcast_to(loss, (1, D))

    return pl.pallas_call(
        body, name="ffn_down_loss", grid=(S // tm, NFG),
        in_specs=[pl.BlockSpec(memory_space=pl.ANY),
                  pl.BlockSpec((None, tm, N_FG), lambda m, p: (p, m, 0)),
                  pl.BlockSpec((None, N_FG, D), lambda m, p: (p, 0, 0)),
                  pl.BlockSpec((1, D), lambda m, p: (0, 0)),
                  pl.BlockSpec(memory_space=pl.ANY)],
        out_specs=[pl.BlockSpec((tm, D), lambda m, p: (m, 0)), pl.BlockSpec((tm, D), lambda m, p: (m, 0)),
                   pl.BlockSpec((8, D), lambda m, p: (0, 0))],
        out_shape=[jax.ShapeDtypeStruct((S, D), F32), jax.ShapeDtypeStruct((S, D), BF16),
                   jax.ShapeDtypeStruct((8, D), F32)],
        scratch_shapes=[pltpu.VMEM((tm, D), F32), pltpu.VMEM((tm, D), F32), pltpu.VMEM((tm, D), F32),
                        pltpu.SemaphoreType.DMA((2,))],
        compiler_params=_cp(("arbitrary", "arbitrary")),
    )(x2, a, wd, nw, tgt)


def _ffn_down_bwd(dx3b, wd, dadg, dadu, part, before=None):
    tm = 1024
    half = NFG // 2

    def body(dx_ref, w_ref, dadg_ref, dadu_ref, *rest):
        dgu_ref = rest[-1]
        da = _dot_nt(dx_ref[...], w_ref[...])
        dgu_ref[:, 0:N_FG] = (da * dadg_ref[...].astype(F32)).astype(BF16)
        dgu_ref[:, N_FG:2 * N_FG] = (da * dadu_ref[...].astype(F32)).astype(BF16)

    blk = pl.BlockSpec((None, tm, N_FG), lambda p, m: (p + part * half, m, 0))
    before = list(before or [])
    return pl.pallas_call(
        body, name=f"ffn_down_bwd_{part}", grid=(half, S // tm),
        in_specs=[pl.BlockSpec((tm, D), lambda p, m: (m, 0)),
                  pl.BlockSpec((None, N_FG, D), lambda p, m: (p + part * half, 0, 0)), blk, blk]
        + [pl.BlockSpec(memory_space=pl.ANY)] * len(before),
        out_specs=pl.BlockSpec((None, tm, 2 * N_FG), lambda p, m: (p + part * half, m, 0)),
        out_shape=jax.ShapeDtypeStruct((NFG, S, 2 * N_FG), BF16),
        input_output_aliases={4 + k: k for k in range(len(before))},
        compiler_params=_cp(("parallel", "parallel")),
    )(dx3b, wd, dadg, dadu, *before)


def _ffn_up_bwd(dgu, wgu, dres, xs, r, nw):
    tm = 512

    def body(dgu_ref, w_ref, dres_hbm, x_hbm, r_ref, nw_ref, dx_ref, dxb_ref, st_ref, dres_buf, x_buf, sems):
        m, p = pl.program_id(0), pl.program_id(1)
        tail_in = _row_copies((dres_hbm, x_hbm), (dres_buf, x_buf), sems, m, tm)

        @pl.when(p == 0)
        def _():
            dx_ref[...] = jnp.zeros_like(dx_ref)
            for cp in tail_in:
                cp.start()

        @pl.when((p == 0) & (m == 0))
        def _():
            st_ref[...] = jnp.zeros_like(st_ref)

        dx_ref[...] += _dot(dgu_ref[...], w_ref[...])

        @pl.when(p == NFG - 1)
        def _():
            for cp in tail_in:
                cp.wait()
            dx, dnw = _rms_bwd_tile(dx_ref[...], x_buf[...], r_ref[...], nw_ref[...])
            dx = dres_buf[...] + dx
            dx_ref[...] = dx
            dxb_ref[...] = dx.astype(BF16)
            st_ref[0:1, :] += dnw

    blk = pl.BlockSpec((None, tm, 2 * N_FG), lambda m, p: (p, m, 0))
    wblk = pl.BlockSpec((None, 2 * N_FG, D), lambda m, p: (p, 0, 0))
    row = pl.BlockSpec((tm, D), lambda m, p: (m, 0))
    hbm = pl.BlockSpec(memory_space=pl.ANY)
    return pl.pallas_call(
        body, name="ffn_up_bwd", grid=(S // tm, NFG),
        in_specs=[blk, wblk, hbm, hbm, pl.BlockSpec((tm, 1), lambda m, p: (m, 0)),
                  pl.BlockSpec((1, D), lambda m, p: (0, 0))],
        out_specs=[row, row, pl.BlockSpec((8, D), lambda m, p: (0, 0))],
        out_shape=[jax.ShapeDtypeStruct((S, D), F32), jax.ShapeDtypeStruct((S, D), BF16),
                   jax.ShapeDtypeStruct((8, D), F32)],
        scratch_shapes=[pltpu.VMEM((tm, D), F32), pltpu.VMEM((tm, D), F32), pltpu.SemaphoreType.DMA((2,))],
        compiler_params=_cp(("arbitrary", "arbitrary")),
    )(dgu, wgu, dres, xs, r, nw)


def _out_proj_bwd(dx2b, wout, place=None, rider=None):
    tm = 256

    if rider is None:
        def body(dx_ref, w_ref, o_ref):
            o_ref[...] = _dot_nt(dx_ref[...], w_ref[...])

        return pl.pallas_call(
            body, name="out_proj_bwd", grid=(S // tm,),
            in_specs=[pl.BlockSpec((tm, D), lambda i: (i, 0)), pl.BlockSpec((D, D), lambda i: (0, 0))],
            out_specs=pl.BlockSpec((tm, D), lambda i: (i, 0)),
            out_shape=jax.ShapeDtypeStruct((S, D), F32),
            compiler_params=_cp(("parallel",)),
        )(dx2b, wout), None

    w = rider[0]
    r, c = w.shape
    rt = _row_tile(r, c)
    nt = r // rt
    assert nt <= S // tm

    def body(pos_ref, dx_ref, w_ref, uw, um, uv, ug, us, uc, o_ref, go, dd, mo, vo):
        o_ref[...] = _dot_nt(dx_ref[...], w_ref[...])

        @pl.when(pl.program_id(0) < nt)
        def _():
            _update_tile(uw, um, uv, ug, us, uc, go, dd, mo, vo)

    def at(i):
        return jnp.minimum(i, nt - 1)

    tile = pl.BlockSpec((rt, c), lambda i, pos: (at(i), 0))
    outs = pl.pallas_call(
        body, name="out_proj_bwd",
        grid_spec=pltpu.PrefetchScalarGridSpec(
            num_scalar_prefetch=1, grid=(S // tm,),
            in_specs=[pl.BlockSpec((tm, D), lambda i, pos: (i, 0)), pl.BlockSpec((D, D), lambda i, pos: (0, 0)),
                      tile, tile, tile,
                      pl.BlockSpec((None, rt, c), lambda i, pos: (4 * pos[0] + 2 * pos[1] + pos[2], at(i), 0)),
                      pl.BlockSpec((None, rt, c), lambda i, pos: (2 * pos[0] + pos[1], at(i), 0)),
                      pl.BlockSpec((3, rt, c), lambda i, pos: (0, at(i), 0))],
            out_specs=[pl.BlockSpec((tm, D), lambda i, pos: (i, 0)), tile, tile, tile, tile]),
        out_shape=[jax.ShapeDtypeStruct((S, D), F32)] + [jax.ShapeDtypeStruct((r, c), F32)] * 4,
        compiler_params=_cp(("arbitrary",)),
    )(place, dx2b, wout, *rider)
    return outs[0], outs[1:]


def _in_proj_bwd(dproj, win, dres, xs, r, nw):
    tm = 1024

    def body(dp_ref, w_ref, dres_hbm, x_hbm, r_ref, nw_ref, dx_ref, st_ref, dres_buf, x_buf, sems):
        m, p = pl.program_id(0), pl.program_id(1)
        tail_in = _row_copies((dres_hbm, x_hbm), (dres_buf, x_buf), sems, m, tm)

        @pl.when(p == 0)
        def _():
            dx_ref[...] = jnp.zeros_like(dx_ref)
            for cp in tail_in:
                cp.start()

        @pl.when((p == 0) & (m == 0))
        def _():
            st_ref[...] = jnp.zeros_like(st_ref)

        dx_ref[...] += _dot_nt(dp_ref[...], w_ref[...])

        @pl.when(p == NDEV - 1)
        def _():
            for cp in tail_in:
                cp.wait()
            dx, dnw = _rms_bwd_tile(dx_ref[...], x_buf[...], r_ref[...], nw_ref[...])
            dx_ref[...] = dres_buf[...] + dx
            st_ref[0:1, :] += dnw

    row = pl.BlockSpec((tm, D), lambda m, p: (m, 0))
    hbm = pl.BlockSpec(memory_space=pl.ANY)
    return pl.pallas_call(
        body, name="in_proj_bwd", grid=(S // tm, NDEV),
        in_specs=[pl.BlockSpec((tm, N_IN), lambda m, p: (m, p)),
                  pl.BlockSpec((None, D, N_IN), lambda m, p: (p, 0, 0)),
                  hbm, hbm, pl.BlockSpec((tm, 1), lambda m, p: (m, 0)),
                  pl.BlockSpec((1, D), lambda m, p: (0, 0))],
        out_specs=[row, pl.BlockSpec((8, D), lambda m, p: (0, 0))],
        out_shape=[jax.ShapeDtypeStruct((S, D), F32), jax.ShapeDtypeStruct((8, D), F32)],
        scratch_shapes=[pltpu.VMEM((tm, D), F32), pltpu.VMEM((tm, D), F32), pltpu.SemaphoreType.DMA((2,))],
        compiler_params=_cp(("arbitrary", "arbitrary")),
    )(dproj, win, dres, xs, r, nw)


W_IN_PARTS = 2


def _wgrad_in(h1, dproj, part):
    rows = D // W_IN_PARTS

    def body(a_ref, d_ref, o_ref):
        both = _dot_tn(a_ref[...], d_ref[...]).astype(BF16)
        o_ref[0] = both[:, 0:N_IN]
        o_ref[1] = both[:, N_IN:2 * N_IN]

    return pl.pallas_call(
        body, name=f"wgrad_in_{part}", grid=(NDEV // 2,),
        in_specs=[pl.BlockSpec((S, rows), lambda p: (0, part)), pl.BlockSpec((S, 2 * N_IN), lambda p: (0, p))],
        out_specs=pl.BlockSpec((2, rows, N_IN), lambda p: (p, 0, 0)),
        out_shape=jax.ShapeDtypeStruct((NDEV, rows, N_IN), BF16),
        compiler_params=_cp(("parallel",)),
    )(h1, dproj)


def _wgrad_rows(a3, dy, name, col=0):
    def body(a_ref, d_ref, o_ref):
        o_ref[...] = _dot_tn(a_ref[...], d_ref[...]).astype(BF16)

    return pl.pallas_call(
        body, name=name, grid=(NFG,),
        in_specs=[pl.BlockSpec((None, S, N_FG), lambda p: (p, 0, col)), pl.BlockSpec((S, D), lambda p: (0, 0))],
        out_specs=pl.BlockSpec((None, N_FG, D), lambda p: (p, 0, 0)),
        out_shape=jax.ShapeDtypeStruct((NFG, N_FG, D), BF16),
        compiler_params=_cp(("parallel",)),
    )(a3, dy).reshape(NDEV, N_FF, D)


def _wgrad_out(ma, mr, dx2b):
    half = D // 2
    per = half // N_OUT

    def body(ma_ref, mr_ref, d_ref, o_ref):
        p = pl.program_id(0)

        @pl.when(p == 0)
        def _():
            o_ref[...] = _dot_tn(ma_ref[...], d_ref[...]).astype(BF16).reshape(per, N_OUT, D)

        @pl.when(p == 1)
        def _():
            o_ref[...] = _dot_tn(mr_ref[...], d_ref[...]).astype(BF16).reshape(per, N_OUT, D)

    whole = pl.BlockSpec((S, half), lambda p: (0, 0))
    return pl.pallas_call(
        body, name="wgrad_out", grid=(2,),
        in_specs=[whole, whole, pl.BlockSpec((S, D), lambda p: (0, 0))],
        out_specs=pl.BlockSpec((per, N_OUT, D), lambda p: (p, 0, 0)),
        out_shape=jax.ShapeDtypeStruct((NDEV, N_OUT, D), BF16),
        compiler_params=_cp(("parallel",)),
    )(ma, mr, dx2b)


def _attn_consts():
    c = np.zeros((AH, 8, AHD), np.float32)
    for h in range(AH):
        c[h, :, :] = 2.0 ** (-(h + 1))
    return jnp.asarray(c)


def _permute_in(dst, src, d, cast=None):
    v = src[...]
    if d > 1:
        v = pltpu.einshape("jrc->rjc", v.reshape(S // d, d, AHD)).reshape(S, AHD)
    dst[...] = v if cast is None else v.astype(cast)


def _natural_order(v, d):
    if d == 1:
        return v
    return pltpu.einshape("rjc->jrc", v.reshape(d, S // d, AHD)).reshape(S, AHD)


def _attn_masks():
    qi = lax.broadcasted_iota(jnp.int32, (CH, CH), 0)
    kj = lax.broadcasted_iota(jnp.int32, (CH, CH), 1)
    dist_c = (qi - kj).astype(F32)
    dist_p = (qi - kj + CH).astype(F32)
    return (qi >= kj)[None], (kj >= qi)[None], dist_c[None], dist_p[None]


GB = 16


def _bdot_nt(a, b):
    return lax.dot_general(a, b, (((2,), (2,)), ((0,), (0,))), preferred_element_type=F32)


def _bdot(a, b):
    return lax.dot_general(a, b, (((2,), (1,)), ((0,), (0,))), preferred_element_type=F32)


def _bdot_tn(a, b):
    return lax.dot_general(a, b, (((1,), (1,)), ((0,), (0,))), preferred_element_type=F32)


def _shift_block(dst, src):
    dst[0:CH, :] = jnp.zeros((CH, AHD), dst.dtype)
    dst[CH:S, :] = src[0:S - CH, :]


def _has_prev(g, nb):
    blk = lax.broadcasted_iota(jnp.int32, (GB, 1, 1), 0) + g * GB
    return (blk & (nb - 1)) != 0


def _blocks(ref, g):
    return ref[g * GB * CH:(g + 1) * GB * CH, :].reshape(GB, CH, AHD)


def _attn_fwd(proj):
    scale = 1.0 / math.sqrt(AHD)

    def body(c_ref, q_ref, k_ref, v_ref, o_ref, ob_ref, lse_ref, qkvp_ref, lsep_ref, qd, kd, vd, kps, vps, od, ld, *nat):
        onat, lnat = nat[0:3], nat[3:6]
        slope = c_ref[0:1, :]
        mask_c, mask_p, dist_c, dist_p = _attn_masks()
        for pi, (d, nb) in enumerate(PATTERNS):
            _permute_in(qd, q_ref, d, BF16)
            _permute_in(kd, k_ref, d, BF16)
            _permute_in(vd, v_ref, d, BF16)
            if d > 1:
                qkvp_ref[pi - 1, 0] = qd[...]
                qkvp_ref[pi - 1, 1] = kd[...]
                qkvp_ref[pi - 1, 2] = vd[...]
            if nb > 1:
                _shift_block(kps, kd)
                _shift_block(vps, vd)
            bias_c = -(slope * float(d)) * dist_c
            bias_p = -(slope * float(d)) * dist_p
            for g in range(NB // GB):
                q3, k3, v3 = _blocks(qd, g), _blocks(kd, g), _blocks(vd, g)
                s_c = jnp.where(mask_c, _bdot_nt(q3, k3) * scale + bias_c, NEG)
                mx = jnp.max(s_c, axis=-1, keepdims=True)
                if nb > 1:
                    kp3, vp3 = _blocks(kps, g), _blocks(vps, g)
                    s_p = jnp.where(jnp.logical_and(mask_p, _has_prev(g, nb)),
                                    _bdot_nt(q3, kp3) * scale + bias_p, NEG)
                    mx = jnp.maximum(mx, jnp.max(s_p, axis=-1, keepdims=True))
                    l = (jnp.sum(jnp.exp(s_c - mx), axis=-1, keepdims=True)
                         + jnp.sum(jnp.exp(s_p - mx), axis=-1, keepdims=True))
                    lse = mx + jnp.log(l)
                    o3 = _bdot(jnp.exp(s_c - lse).astype(BF16), v3) + _bdot(jnp.exp(s_p - lse).astype(BF16), vp3)
                else:
                    l = jnp.sum(jnp.exp(s_c - mx), axis=-1, keepdims=True)
                    lse = mx + jnp.log(l)
                    o3 = _bdot(jnp.exp(s_c - lse).astype(BF16), v3)
                rows = slice(g * GB * CH, (g + 1) * GB * CH)
                od[rows, :] = o3.reshape(GB * CH, AHD)
                ld[rows, :] = jnp.broadcast_to(lse, (GB, CH, AHD)).reshape(GB * CH, AHD)
            onat[pi][...] = _natural_order(od[...], d)
            lnat[pi][...] = _natural_order(ld[...], d)
        l0, l1, l2 = lnat[0][...], lnat[1][...], lnat[2][...]
        mx = jnp.maximum(jnp.maximum(l0, l1), l2)
        e0, e1, e2 = jnp.exp(l0 - mx), jnp.exp(l1 - mx), jnp.exp(l2 - mx)
        den = e0 + e1 + e2
        out = (e0 / den) * onat[0][...] + (e1 / den) * onat[1][...] + (e2 / den) * onat[2][...]
        o_ref[...] = out
        ob_ref[...] = out.astype(BF16)
        lse_ref[...] = mx + jnp.log(den)
        for pi, (d, _) in enumerate(PATTERNS[1:]):
            _permute_in(lsep_ref.at[pi], lse_ref, d)

    def col(off):
        return pl.BlockSpec((S, AHD), lambda h: (0, off + h))

    return pl.pallas_call(
        body, name="attn_fwd", grid=(AH,),
        in_specs=[pl.BlockSpec((None, 8, AHD), lambda h: (h, 0, 0)), col(0), col(AH), col(2 * AH)],
        out_specs=[col(0), col(0), col(0), pl.BlockSpec((2, 3, S, AHD), lambda h: (0, 0, 0, h)),
                   pl.BlockSpec((2, S, AHD), lambda h: (0, 0, h))],
        out_shape=[jax.ShapeDtypeStruct((S, AH * AHD), F32), jax.ShapeDtypeStruct((S, AH * AHD), BF16),
                   jax.ShapeDtypeStruct((S, AH * AHD), F32),
                   jax.ShapeDtypeStruct((2, 3, S, AH * AHD), BF16), jax.ShapeDtypeStruct((2, S, AH * AHD), F32)],
        scratch_shapes=[pltpu.VMEM((S, AHD), BF16) for _ in range(5)]
        + [pltpu.VMEM((S, AHD), F32) for _ in range(8)],
        compiler_params=_cp(("parallel",)),
    )(_attn_consts(), proj, proj, proj)


def _attn_bwd(proj, dmixed, o, lse, qkvp, lsep):
    scale = 1.0 / math.sqrt(AHD)

    def body(c_ref, q_ref, k_ref, v_ref, do_ref, o_ref, lse_ref, qkvp_ref, lsep_ref, dproj_hbm,
             qd, kd, vd, dod, kps, vps, dld, dqd, dkd, dvd, delta, aq, ak, av, sq, sk, sv, sems):
        h = pl.program_id(0)

        def out_copies(head):
            return [pltpu.make_async_copy(
                st, dproj_hbm.at[:, pl.ds(pl.multiple_of((k * AH + head) * AHD, AHD), AHD)], sems.at[k])
                for k, st in enumerate((sq, sk, sv))]

        slope = c_ref[0:1, :]
        mask_c, mask_p, dist_c, dist_p = _attn_masks()
        delta[...] = jnp.broadcast_to(jnp.sum(do_ref[...] * o_ref[...], axis=-1, keepdims=True), (S, AHD))
        for pi, (d, nb) in enumerate(PATTERNS):
            if d == 1:
                _permute_in(qd, q_ref, d, BF16)
                _permute_in(kd, k_ref, d, BF16)
                _permute_in(vd, v_ref, d, BF16)
                qs, ks, vs, lss = qd, kd, vd, lse_ref
            else:
                qs, ks, vs, lss = (qkvp_ref.at[pi - 1, 0], qkvp_ref.at[pi - 1, 1], qkvp_ref.at[pi - 1, 2],
                                   lsep_ref.at[pi - 1])
            _permute_in(dod, do_ref, d, BF16)
            _permute_in(dld, delta, d)
            if nb > 1:
                _shift_block(kps, ks)
                _shift_block(vps, vs)
            bias_c = -(slope * float(d)) * dist_c
            bias_p = -(slope * float(d)) * dist_p
            for g in range(NB // GB):
                q3, k3, v3, do3 = _blocks(qs, g), _blocks(ks, g), _blocks(vs, g), _blocks(dod, g)
                ls, dl = _blocks(lss, g), _blocks(dld, g)
                lo, hi = g * GB * CH, (g + 1) * GB * CH
                p_c = jnp.exp(jnp.where(mask_c, _bdot_nt(q3, k3) * scale + bias_c, NEG) - ls)
                ds_c = ((p_c * (_bdot_nt(do3, v3) - dl)) * scale).astype(BF16)
                dq3 = _bdot(ds_c, k3)
                dkd[lo:hi, :] = _bdot_tn(ds_c, q3).reshape(GB * CH, AHD)
                dvd[lo:hi, :] = _bdot_tn(p_c.astype(BF16), do3).reshape(GB * CH, AHD)
                if nb > 1:
                    kp3, vp3 = _blocks(kps, g), _blocks(vps, g)
                    p_p = jnp.exp(jnp.where(jnp.logical_and(mask_p, _has_prev(g, nb)),
                                            _bdot_nt(q3, kp3) * scale + bias_p, NEG) - ls)
                    ds_p = ((p_p * (_bdot_nt(do3, vp3) - dl)) * scale).astype(BF16)
                    dq3 = dq3 + _bdot(ds_p, kp3)
                    dkp = _bdot_tn(ds_p, q3).reshape(GB * CH, AHD)
                    dvp = _bdot_tn(p_p.astype(BF16), do3).reshape(GB * CH, AHD)
                    if g == 0:
                        dkd[0:hi - CH, :] += dkp[CH:, :]
                        dvd[0:hi - CH, :] += dvp[CH:, :]
                    else:
                        dkd[lo - CH:hi - CH, :] += dkp
                        dvd[lo - CH:hi - CH, :] += dvp
                dqd[lo:hi, :] = dq3.reshape(GB * CH, AHD)
            ln = S // d
            for acc, src in ((aq, dqd), (ak, dkd), (av, dvd)):
                if pi == 0:
                    acc[...] = src[...]
                else:
                    acc[...] += _natural_order(src[...], d)

        @pl.when(h > 0)
        def _():
            for cp in out_copies(h - 1):
                cp.wait()

        sq[...] = aq[...].astype(BF16)
        sk[...] = ak[...].astype(BF16)
        sv[...] = av[...].astype(BF16)
        for cp in out_copies(h):
            cp.start()

        @pl.when(h == AH - 1)
        def _():
            for cp in out_copies(h):
                cp.wait()

    def col(off):
        return pl.BlockSpec((S, AHD), lambda h: (0, off + h))

    return pl.pallas_call(
        body, name="attn_bwd", grid=(AH,),
        in_specs=[pl.BlockSpec((None, 8, AHD), lambda h: (h, 0, 0)), col(0), col(AH), col(2 * AH),
                  col(0), col(0), col(0), pl.BlockSpec((2, 3, S, AHD), lambda h: (0, 0, 0, h)),
                  pl.BlockSpec((2, S, AHD), lambda h: (0, 0, h))],
        out_specs=pl.BlockSpec(memory_space=pl.ANY),
        out_shape=jax.ShapeDtypeStruct((S, NDEV * N_IN), BF16),
        scratch_shapes=[pltpu.VMEM((S, AHD), BF16) for _ in range(6)]
        + [pltpu.VMEM((S, AHD), F32) for _ in range(8)]
        + [pltpu.VMEM((S, AHD), BF16) for _ in range(3)] + [pltpu.SemaphoreType.DMA((3,))],
        compiler_params=_cp(("arbitrary",)),
    )(_attn_consts(), proj, proj, proj, dmixed, o, lse, qkvp, lsep)


def _ret_consts():
    c = np.zeros((RH, 8, RHD), np.float32)
    for h in range(RH):
        c[h, :, :] = np.log(np.float32(1.0) - np.float32(2.0 ** (-5.0 - h)))
    return jnp.asarray(c)


def _ret_factors(lg):
    i = lax.broadcasted_iota(jnp.int32, (CH, CH), 0)
    j = lax.broadcasted_iota(jnp.int32, (CH, CH), 1)
    dif = (i - j).astype(F32)
    decay = jnp.where(dif >= 0, jnp.exp(lg[:, 0:CH] * jnp.maximum(dif, 0.0)), 0.0)
    row = lax.broadcasted_iota(jnp.int32, (CH, RHD), 0).astype(F32)
    zeta = jnp.exp(lg * (CH - 1.0 - row))
    xi = jnp.exp(lg * (row + 1.0))
    return decay, zeta, xi, jnp.exp(lg * float(CH))


CBK = 8
RSTEPS = NB // CBK


def _ret_specs(rev):
    off = 3 * AH * AHD // RHD
    rows = CBK * CH

    def ch(n):
        return (RSTEPS - 1 - n) if rev else n

    def col(k):
        return pl.BlockSpec((rows, RHD), lambda h, n: (ch(n), off + k * RH + h))

    own = pl.BlockSpec((rows, RHD), lambda h, n: (ch(n), h))
    state = pl.BlockSpec((None, CBK, RHD, RHD), lambda h, n: (h, ch(n), 0, 0))
    const = pl.BlockSpec((None, 8, RHD), lambda h, n: (h, 0, 0))
    dm = pl.BlockSpec((rows, RHD), lambda h, n: (ch(n), AH * AHD // RHD + h))
    return col, own, state, const, dm


def _chunks(x):
    return x.reshape(CBK, CH, RHD)


def _ret_fwd(proj):
    def body(c_ref, q_ref, k_ref, v_ref, g_ref, ret_ref, mr_ref, st_ref, r_acc):
        n = pl.program_id(1)

        @pl.when(n == 0)
        def _():
            r_acc[...] = jnp.zeros_like(r_acc)

        decay, zeta, xi, gch = _ret_factors(c_ref[0:1, :])
        q3 = _chunks(q_ref[...].astype(BF16))
        kc = _chunks(k_ref[...] * (1.0 / math.sqrt(RHD)))
        k3 = kc.astype(BF16)
        v3 = _chunks(v_ref[...].astype(BF16))
        kv3 = _bdot_tn((kc * zeta[None]).astype(BF16), v3)
        r = r_acc[...]
        for i in range(CBK):
            st_ref[i] = r.astype(BF16)
            r = r * gch + kv3[i]
        r_acc[...] = r
        scores = _bdot_nt(q3, k3) * decay[None]
        ret = (_bdot(scores.astype(BF16), v3) + _bdot(q3, st_ref[...]) * xi[None]).reshape(CBK * CH, RHD)
        ret_ref[...] = ret
        rr = lax.rsqrt(jnp.mean(ret * ret, axis=-1, keepdims=True) + EPS)
        gv = g_ref[...]
        mr_ref[...] = ((gv * _sigmoid(gv)) * (ret * rr)).astype(BF16)

    col, own, state, const, _ = _ret_specs(False)
    return pl.pallas_call(
        body, name="ret_fwd", grid=(RH, RSTEPS),
        in_specs=[const, col(0), col(1), col(2), col(3)],
        out_specs=[own, own, state],
        out_shape=[jax.ShapeDtypeStruct((S, RH * RHD), F32), jax.ShapeDtypeStruct((S, RH * RHD), BF16),
                   jax.ShapeDtypeStruct((RH, NB, RHD, RHD), BF16)],
        scratch_shapes=[pltpu.VMEM((RHD, RHD), F32)],
        compiler_params=_cp(("parallel", "arbitrary")),
    )(_ret_consts(), proj, proj, proj, proj)


def _ret_bwd(proj, ret, states, dmixed, dproj):
    rows = CBK * CH
    col0 = 3 * AH * AHD

    def body(c_ref, q_ref, k_ref, v_ref, g_ref, ret_ref, st_ref, dm_ref, dproj_in, dproj_hbm, g_acc, gs,
             sq, sk, sv, sg, sems):
        del dproj_in
        h, n = pl.program_id(0), pl.program_id(1)
        step = h * RSTEPS + n

        def out_copies(t):
            hh, nn = t // RSTEPS, t % RSTEPS
            r0 = pl.multiple_of((RSTEPS - 1 - nn) * rows, rows)
            return [pltpu.make_async_copy(
                st, dproj_hbm.at[pl.ds(r0, rows), pl.ds(pl.multiple_of(col0 + (k * RH + hh) * RHD, RHD), RHD)],
                sems.at[k]) for k, st in enumerate((sq, sk, sv, sg))]

        @pl.when(n == 0)
        def _():
            g_acc[...] = jnp.zeros_like(g_acc)

        decay, zeta, xi, gch = _ret_factors(c_ref[0:1, :])
        ret_v = ret_ref[...]
        rr = lax.rsqrt(jnp.mean(ret_v * ret_v, axis=-1, keepdims=True) + EPS)
        gv = g_ref[...]
        sgm = _sigmoid(gv)
        dmix = dm_ref[...]
        dgate = ((dmix * (ret_v * rr)) * (sgm * (1.0 + gv * (1.0 - sgm)))).astype(BF16)
        dretn = dmix * (gv * sgm)
        dret = _chunks(rr * dretn - ret_v * ((rr * rr * rr) * jnp.mean(dretn * ret_v, axis=-1, keepdims=True)))

        q3 = _chunks(q_ref[...].astype(BF16))
        kc = _chunks(k_ref[...] * (1.0 / math.sqrt(RHD)))
        k3 = kc.astype(BF16)
        v3 = _chunks(v_ref[...].astype(BF16))
        d3 = dret.astype(BF16)
        dxi = (dret * xi[None]).astype(BF16)
        kz = (kc * zeta[None]).astype(BF16)
        dr3 = _bdot_tn(q3, dxi)
        acc = g_acc[...]
        for i in reversed(range(CBK)):
            gs[i] = acc.astype(BF16)
            acc = dr3[i] + gch * acc
        g_acc[...] = acc
        g3 = gs[...]
        sc = (_bdot_nt(q3, k3) * decay[None]).astype(BF16)
        da = (_bdot_nt(d3, v3) * decay[None]).astype(BF16)
        dq = _bdot(da, k3) + _bdot_nt(dxi, st_ref[...])
        dkc = _bdot_tn(da, q3) + _bdot_nt(v3, g3) * zeta[None]
        dv = _bdot_tn(sc, d3) + _bdot(kz, g3)

        @pl.when(step > 0)
        def _():
            for cp in out_copies(step - 1):
                cp.wait()

        sq[...] = dq.reshape(rows, RHD).astype(BF16)
        sk[...] = (dkc * (1.0 / math.sqrt(RHD))).reshape(rows, RHD).astype(BF16)
        sv[...] = dv.reshape(rows, RHD).astype(BF16)
        sg[...] = dgate
        for cp in out_copies(step):
            cp.start()

        @pl.when(step == RH * RSTEPS - 1)
        def _():
            for cp in out_copies(step):
                cp.wait()

    col, own, state, const, dm = _ret_specs(True)
    hbm = pl.BlockSpec(memory_space=pl.ANY)
    return pl.pallas_call(
        body, name="ret_bwd", grid=(RH, RSTEPS),
        in_specs=[const, col(0), col(1), col(2), col(3), own, state, dm, hbm],
        out_specs=hbm,
        out_shape=jax.ShapeDtypeStruct(dproj.shape, dproj.dtype),
        input_output_aliases={8: 0},
        scratch_shapes=[pltpu.VMEM((RHD, RHD), F32), pltpu.VMEM((CBK, RHD, RHD), BF16)]
        + [pltpu.VMEM((rows, RHD), BF16) for _ in range(4)] + [pltpu.SemaphoreType.DMA((4,))],
        compiler_params=_cp(("arbitrary", "arbitrary")),
    )(_ret_consts(), proj, proj, proj, proj, ret, states, dmixed, dproj)


class _NoReduction:
    def start(self, group, grads):
        pass

    def local(self, name, first=()):
        return []

    def landed(self, name):
        return []

    def update(self, name):
        return []

    place = None

    def rider(self, name):
        return None

    def set_update(self, name, outs):
        pass


def _local_step(x, tgt, nw1, nw2, nw3, win, wout, wgu, wd, red=None):
    red = red or _NoReduction()

    def after(values, first):
        return lax.optimization_barrier((tuple(values), tuple(first)))[0]

    wd = wd.reshape(NFG, N_FG, D)
    h1, r1 = _rms_fwd(x, nw1)
    proj = _proj(h1, win)
    o, ma, lse, qkvp, lsep = _attn_fwd(proj)
    ret, mr, states = _ret_fwd(proj)
    x2, h2, r2 = _out_proj_rms(x, ma, mr, wout, nw2)
    a, dadg, dadu = _ffn_up(h2, wgu)
    dx3, dx3b, st3 = _ffn_down_loss(x2, a, wd, nw3, tgt)

    dwd = _wgrad_rows(a, dx3b, "wgrad_down")
    red.start(["w_down"], [dwd])
    (dx3b,) = after([dx3b], [dwd])
    part = _ffn_down_bwd(dx3b, wd, dadg, dadu, 0)
    (dx3b,) = after([dx3b], red.local("w_down", first=[part]))
    dgu = _ffn_down_bwd(dx3b, wd, dadg, dadu, 1, [part])
    dwg = _wgrad_rows(dgu, h2, "wgrad_gate", 0)
    red.start(["w_gate"], [dwg])
    (dgu,) = after([dgu], [dwg])
    dwu = _wgrad_rows(dgu, h2, "wgrad_up", 1)
    red.start(["w_up"], [dwu])
    (dgu,) = after([dgu], red.local("w_gate", first=[dwu] + red.landed("w_down")))
    dx2, dx2b, st2 = _ffn_up_bwd(dgu, wgu, dx3, x2, r2, nw2)
    (dx2b,) = after([dx2b], red.local("w_up", first=[dx2b]))
    dwo = _wgrad_out(ma, mr, dx2b)
    red.start(["w_out"], [dwo])
    (dx2b,) = after([dx2b], [dwo])
    dmixed, done = _out_proj_bwd(dx2b, wout, red.place, red.rider("w_down"))
    red.set_update("w_down", done)
    dproj = _attn_bwd(proj, dmixed, o, lse, qkvp, lsep)
    (dmixed,) = after([dmixed], red.local("w_out", first=[dproj] + red.landed("w_gate")))
    dproj = _ret_bwd(proj, ret, states, dmixed, dproj)
    (dwi0,) = after([_wgrad_in(h1, dproj, 0)], red.landed("w_up"))
    red.start(["w_in_0"], [dwi0])
    (dproj,) = after([dproj], [dwi0])
    dwi1 = _wgrad_in(h1, dproj, 1)
    red.start(["w_in_1"], [dwi1])
    sums = red.local("w_in_0", first=[dwi1] + red.landed("w_out"))
    sums = red.local("w_in_1", first=sums + red.update("w_gate"))
    (dproj,) = after([dproj], sums)
    gx, st1 = _in_proj_bwd(dproj, win, dx2, x, r1, nw1)
    dwi = jnp.concatenate([dwi0, dwi1], axis=1)
    stats = jnp.concatenate([st1[0:1], st2[0:1], st3[0:2], jnp.zeros((4, D), F32)], axis=0)
    return stats, gx, dwi, dwo, dwg, dwu, dwd


def _place():
    x, y, c = lax.axis_index("x"), lax.axis_index("y"), lax.axis_index("c")
    return x, y, c, [(1 - x, y), (x, 1 - y), (1 - x, 1 - y)]


def _handshake(peers):
    barrier = pltpu.get_barrier_semaphore()
    for peer in peers:
        pl.semaphore_signal(barrier, inc=1, device_id=peer, device_id_type=MESH)
    pl.semaphore_wait(barrier, len(peers))


def _all_gather(shards, name, collective_id, stacked=False):
    na = len(shards)
    nout = 1 if stacked else na
    SIB, XN0, XN1, YN1, YN0, VIA_X, VIA_Y = 0, 1, 2, 3, 4, 5, 6
    D2D = {XN0: 7, XN1: 8, YN1: 9, YN0: 10, VIA_X: 11, VIA_Y: 12}

    def body(*refs):
        ins, outs = refs[:na], refs[na:na + nout]
        send_sems, recv_sems, local_sems = refs[na + nout:]
        x, y, c, _ = _place()
        me, sib = (x, y, c), (x, y, 1 - c)
        xn, yn, dg = (1 - x, y, c), (x, 1 - y, c), (1 - x, 1 - y, c)
        _handshake([sib, xn, yn])

        def part(ref, h):
            rows = ref.shape[0] // 2
            return ref if h is None else ref.at[pl.ds(h * rows, rows)]

        def block(a, owner, h):
            idx = 4 * owner[0] + 2 * owner[1] + owner[2]
            if not stacked:
                return part(outs[a].at[idx], h)
            rows = shards[a].shape[0]
            return part(outs[0].at[idx // 2, a, pl.ds(pl.multiple_of((idx % 2) * rows, rows), rows)], h)

        def copy(a, k, owner, h, to, own_src=False):
            return pltpu.make_async_remote_copy(
                src_ref=part(ins[a], h) if own_src else block(a, owner, h), dst_ref=block(a, owner, h),
                send_sem=send_sems.at[a, k], recv_sem=recv_sems.at[a, k], device_id=to, device_id_type=MESH)

        def other(p):
            return (p[0], p[1], 1 - c)

        mine = [pltpu.make_async_copy(ins[a], block(a, me, None), local_sems.at[a]) for a in range(na)]
        for cp in mine:
            cp.start()
        sent = []
        for a in range(na):
            sent += [copy(a, XN0, me, 0, xn, True), copy(a, YN1, me, 1, yn, True),
                     copy(a, XN1, me, 1, xn, True), copy(a, YN0, me, 0, yn, True)]
        sent += [copy(a, SIB, me, None, sib, True) for a in range(na)]
        for cp in sent:
            cp.start()

        def landed(a, k, owner, h, then):
            copy(a, k, owner, h, me).wait_recv()
            for k2, to in then + [(D2D[k], sib)]:
                cp = copy(a, k2, owner, h, to)
                cp.start()
                sent.append(cp)

        for a in range(na):
            landed(a, XN0, xn, 0, [(VIA_Y, yn)])
            landed(a, YN1, yn, 1, [(VIA_X, xn)])
            landed(a, XN1, xn, 1, [])
            landed(a, YN0, yn, 0, [])
        for a in range(na):
            landed(a, VIA_Y, dg, 0, [])
            landed(a, VIA_X, dg, 1, [])
        for a in range(na):
            copy(a, SIB, sib, None, me).wait_recv()
            for k, owner, h in ((XN0, xn, 0), (XN1, xn, 1), (YN1, yn, 1), (YN0, yn, 0), (VIA_Y, dg, 0), (VIA_X, dg, 1)):
                copy(a, D2D[k], other(owner), h, me).wait_recv()
        for cp in sent:
            cp.wait_send()
        for cp in mine:
            cp.wait()

    if stacked:
        r, c = shards[0].shape
        out_type = [jax.ShapeDtypeStruct((NDEV // 2, na, 2 * r, c), shards[0].dtype)]
    else:
        out_type = [jax.ShapeDtypeStruct((NDEV,) + s.shape, s.dtype) for s in shards]
    return _sequencer_call(
        body, name, collective_id, out_type,
        [pltpu.SemaphoreType.DMA((na, 13)), pltpu.SemaphoreType.DMA((na, 13)), pltpu.SemaphoreType.DMA((na,))])(*shards)


def _sequencer_call(body, name, collective_id, out_type, scratch_types):
    return pl.kernel(
        body, name=name, out_type=out_type,
        mesh=plsc.ScalarSubcoreMesh(axis_name="sequencer", num_cores=1),
        scratch_types=scratch_types,
        compiler_params=pltpu.CompilerParams(collective_id=collective_id))


def _exchange_sibling(grads, name, collective_id):
    na = len(grads)

    def body(*refs):
        ins, outs = refs[:na], refs[na:2 * na]
        send_sems, recv_sems = refs[2 * na:]
        x, y, c, _ = _place()
        _handshake([(x, y, 1 - c)])
        cps = []
        for a in range(na):
            for k in range(4):
                cps.append(pltpu.make_async_remote_copy(
                    src_ref=ins[a].at[2 * k + (1 - c)], dst_ref=outs[a].at[k],
                    send_sem=send_sems.at[a, k], recv_sem=recv_sems.at[a, k],
                    device_id=(x, y, 1 - c), device_id_type=MESH))
        for cp in cps:
            cp.start()
        for cp in cps:
            cp.wait()

    return _sequencer_call(
        body, name, collective_id,
        [jax.ShapeDtypeStruct((4,) + g.shape[1:], g.dtype) for g in grads],
        [pltpu.SemaphoreType.DMA((na, 4)), pltpu.SemaphoreType.DMA((na, 4))])(*grads)


def _row_tile(rows, cols):
    for t in (512, 256, 176, 128, 64, 32, 16):
        if rows % t == 0 and t * cols * 4 <= (2 << 20):
            return t
    raise ValueError((rows, cols))


STREAM_BUFS = 3


def _stream_tile(rows, steps):
    for t in (512, 256, 176, 128, 64, 32, 16):
        if rows % t == 0 and rows // t >= steps:
            return t
    raise ValueError((rows, steps))


def _stream(n, loads, stores, compute):
    for k in range(min(STREAM_BUFS, n)):
        for cp in loads(k):
            cp.start()
    for k in range(n):
        for cp in loads(k):
            cp.wait()
        if k >= 2:
            for cp in stores(k - 2):
                cp.wait()
        compute(k)
        for cp in stores(k):
            cp.start()
        if k + STREAM_BUFS < n:
            for cp in loads(k + STREAM_BUFS):
                cp.start()
    for k in range(max(n - 2, 0), n):
        for cp in stores(k):
            cp.wait()


def _chip_sum(place, g, got, name):
    _, r, c = g.shape
    tm = _stream_tile(r, 4)
    nt = r // tm

    def body(pos_ref, g_hbm, got_hbm, o_hbm, g_buf, s_buf, o_buf, sem_in, sem_out):
        def chip(j):
            return 2 * (pos_ref[0] ^ (0 if j == 1 else 1)) + (pos_ref[1] ^ (0 if j == 0 else 1))

        def loads(k):
            j, rows, slot = k // nt, pl.ds((k % nt) * tm, tm), k % STREAM_BUFS
            return [pltpu.make_async_copy(g_hbm.at[2 * chip(j) + pos_ref[2], rows], g_buf.at[slot], sem_in.at[slot, 0]),
                    pltpu.make_async_copy(got_hbm.at[chip(j), rows], s_buf.at[slot], sem_in.at[slot, 1])]

        def stores(k):
            return [pltpu.make_async_copy(o_buf.at[k % 2], o_hbm.at[k // nt, pl.ds((k % nt) * tm, tm)],
                                          sem_out.at[k % 2])]

        def compute(k):
            slot = k % STREAM_BUFS
            o_buf[k % 2] = (g_buf[slot].astype(F32) + s_buf[slot].astype(F32)).astype(BF16)

        _stream(3 * nt, loads, stores, compute)

    hbm = pl.BlockSpec(memory_space=pl.ANY)
    return pl.pallas_call(
        body, name=name,
        grid_spec=pltpu.PrefetchScalarGridSpec(
            num_scalar_prefetch=1, grid=(1,), in_specs=[hbm, hbm], out_specs=hbm,
            scratch_shapes=[pltpu.VMEM((STREAM_BUFS, tm, c), BF16), pltpu.VMEM((STREAM_BUFS, tm, c), BF16),
                            pltpu.VMEM((2, tm, c), BF16),
                            pltpu.SemaphoreType.DMA((STREAM_BUFS, 2)), pltpu.SemaphoreType.DMA((2,))]),
        out_shape=jax.ShapeDtypeStruct((3, r, c), BF16),
        compiler_params=_cp(("arbitrary",)),
    )(place, g, got)


def _exchange_chips(sums, name, collective_id):
    na = len(sums)

    def body(*refs):
        ins, outs = refs[:na], refs[na:2 * na]
        send_sems, recv_sems = refs[2 * na:]
        x, y, c, chips = _place()
        _handshake([(*chip, c) for chip in chips])
        cps = []
        for a in range(na):
            for j, chip in enumerate(chips):
                cps.append(pltpu.make_async_remote_copy(
                    src_ref=ins[a].at[j], dst_ref=outs[a].at[j],
                    send_sem=send_sems.at[a, j], recv_sem=recv_sems.at[a, j],
                    device_id=(*chip, c), device_id_type=MESH))
        for cp in cps:
            cp.start()
        for cp in cps:
            cp.wait()

    return _sequencer_call(
        body, name, collective_id,
        [jax.ShapeDtypeStruct((3,) + s.shape[1:], s.dtype) for s in sums],
        [pltpu.SemaphoreType.DMA((na, 3)), pltpu.SemaphoreType.DMA((na, 3))])(*sums)


def _exchange_stats(stats, collective_id):
    def body(st_in, st_out, st_send, st_recv, local_sem):
        x, y, c, _ = _place()
        me_idx = 4 * x + 2 * y + c
        peers = [(x ^ ((k >> 2) & 1), y ^ ((k >> 1) & 1), c ^ (k & 1)) for k in range(1, 8)]
        _handshake(peers)
        mine = pltpu.make_async_copy(st_in, st_out.at[me_idx], local_sem)
        mine.start()
        cps = [pltpu.make_async_remote_copy(
            src_ref=st_in, dst_ref=st_out.at[me_idx], send_sem=st_send.at[k], recv_sem=st_recv.at[k],
            device_id=peer, device_id_type=MESH) for k, peer in enumerate(peers)]
        for cp in cps:
            cp.start()
        for cp in cps:
            cp.wait()
        mine.wait()

    return _sequencer_call(
        body, "exchange_stats", collective_id,
        jax.ShapeDtypeStruct((NDEV,) + stats.shape, stats.dtype),
        [pltpu.SemaphoreType.DMA((7,)), pltpu.SemaphoreType.DMA((7,)), pltpu.SemaphoreType.DMA])(stats)


class _Reduction:
    def __init__(self, place, first_collective_id, state):
        self.place = place
        self.ids = iter(range(first_collective_id, 32))
        self.state = state
        self.groups = {}
        self.updates = {}

    def next_id(self):
        return next(self.ids)

    def start(self, group, grads):
        got = _exchange_sibling(grads, "sibling_exchange_" + group[0], self.next_id())
        self.groups[group[0]] = dict(names=group, grads=grads, got=got)

    def local(self, name, first=()):
        grp = self.groups[name]
        grads = lax.optimization_barrier((tuple(grp["grads"]), tuple(first)))[0]
        grp["sums"] = [_chip_sum(self.place, g, s, "chip_sum_" + n)
                       for g, s, n in zip(grads, grp["got"], grp["names"])]
        grp["chips"] = _exchange_chips(grp["sums"], "chip_exchange_" + name, self.next_id())
        return grp["sums"]

    def landed(self, name):
        return list(self.groups[name]["chips"])

    def rider(self, name):
        grp = next(g for g in self.groups.values() if name in g["names"])
        k = grp["names"].index(name)
        return self.state[name][:3] + (grp["grads"][k], grp["got"][k], grp["chips"][k])

    def set_update(self, name, outs):
        self.updates[name] = list(outs)

    def update(self, name):
        if name not in self.updates:
            grp = next(g for g in self.groups.values() if name in g["names"])
            k = grp["names"].index(name)
            w, m, v, part, parts = self.state[name]
            before = self.update(f"{name[:-1]}{part - 1}") if part else None
            self.updates[name] = _shard_update(self.place, w, m, v, grp["grads"][k], grp["got"][k],
                                               grp["chips"][k], "update_" + name, part, parts, before)
        return list(self.updates[name])


def _adamw(w, g, m, v):
    m = ADAM_B1 * m + (1.0 - ADAM_B1) * g
    v = ADAM_B2 * v + (1.0 - ADAM_B2) * (g * g)
    m_hat = m / (1.0 - ADAM_B1 ** ADAM_STEP)
    v_hat = v / (1.0 - ADAM_B2 ** ADAM_STEP)
    delta = -ADAM_LR * (m_hat / (jnp.sqrt(v_hat) + ADAM_EPS) + ADAM_WD * w)
    return delta, m, v


def _update_tile(w_ref, m_ref, v_ref, g_ref, s_ref, c_ref, go_ref, d_ref, mo_ref, vo_ref):
    grad = g_ref[...].astype(F32) + s_ref[...].astype(F32)
    for j in range(3):
        grad = grad + c_ref[j].astype(F32)
    delta, mn, vn = _adamw(w_ref[...], grad, m_ref[...], v_ref[...])
    go_ref[...] = grad
    d_ref[...] = delta
    mo_ref[...] = mn
    vo_ref[...] = vn


def _shard_update(place, w, m, v, g, got_sib, got_chips, name, part=0, parts=1, before=None):
    r, c = w.shape
    rp = r // parts
    tm = _stream_tile(rp, 8)
    nt = rp // tm
    before = list(before or [])

    def body(pos_ref, w_hbm, m_hbm, v_hbm, g_hbm, s_hbm, c_hbm, *rest):
        outs = rest[len(before):len(before) + 4]
        w_buf, m_buf, v_buf, g_buf, s_buf, c_buf, o_buf, sem_in, sem_out = rest[len(before) + 4:]
        own = 4 * pos_ref[0] + 2 * pos_ref[1] + pos_ref[2]
        chip = 2 * pos_ref[0] + pos_ref[1]

        def loads(k):
            slot, rows, mine = k % STREAM_BUFS, pl.ds(k * tm, tm), pl.ds(part * rp + k * tm, tm)
            pairs = [(w_hbm.at[mine], w_buf), (m_hbm.at[mine], m_buf), (v_hbm.at[mine], v_buf),
                     (g_hbm.at[own, rows], g_buf), (s_hbm.at[chip, rows], s_buf), (c_hbm.at[:, rows], c_buf)]
            return [pltpu.make_async_copy(src, buf.at[slot], sem_in.at[slot, n]) for n, (src, buf) in enumerate(pairs)]

        def stores(k):
            mine = pl.ds(part * rp + k * tm, tm)
            return [pltpu.make_async_copy(o_buf.at[k % 2, n], out.at[mine], sem_out.at[k % 2, n])
                    for n, out in enumerate(outs)]

        def compute(k):
            slot = k % STREAM_BUFS
            _update_tile(w_buf.at[slot], m_buf.at[slot], v_buf.at[slot], g_buf.at[slot], s_buf.at[slot],
                         c_buf.at[slot], *[o_buf.at[k % 2, n] for n in range(4)])

        _stream(nt, loads, stores, compute)

    hbm = pl.BlockSpec(memory_space=pl.ANY)
    return pl.pallas_call(
        body, name=name,
        grid_spec=pltpu.PrefetchScalarGridSpec(
            num_scalar_prefetch=1, grid=(1,), in_specs=[hbm] * (6 + len(before)), out_specs=[hbm] * 4,
            scratch_shapes=[pltpu.VMEM((STREAM_BUFS, tm, c), F32)] * 3 + [pltpu.VMEM((STREAM_BUFS, tm, c), BF16)] * 2
            + [pltpu.VMEM((STREAM_BUFS, 3, tm, c), BF16), pltpu.VMEM((2, 4, tm, c), F32),
               pltpu.SemaphoreType.DMA((STREAM_BUFS, 6)), pltpu.SemaphoreType.DMA((2, 4))]),
        out_shape=[jax.ShapeDtypeStruct((r, c), F32)] * 4,
        input_output_aliases={7 + k: k for k in range(len(before))},
        compiler_params=_cp(("arbitrary",)),
    )(place, w, m, v, g, got_sib, got_chips, *before)


def _small_update(stats_all, ws, ms, vs):
    def body(st_ref, w_ref, m_ref, v_ref, go_ref, d_ref, mo_ref, vo_ref):
        grad = st_ref[0]
        for k in range(1, NDEV):
            grad = grad + st_ref[k]
        delta, mn, vn = _adamw(w_ref[...], grad, m_ref[...], v_ref[...])
        go_ref[...] = grad
        d_ref[...] = delta
        mo_ref[...] = mn
        vo_ref[...] = vn

    return pl.pallas_call(
        body, name="small_update",
        out_shape=[jax.ShapeDtypeStruct((8, D), F32)] * 4,
        compiler_params=_cp(),
    )(stats_all, ws, ms, vs)


def kernel(x, norm_mix_w, w_in, w_out, norm_ffn_w, w_gate, w_up, w_down, norm_final_w, loss_target, m_norm_mix_w, m_w_in, m_w_out, m_norm_ffn_w, m_w_gate, m_w_up, m_w_down, m_norm_final_w, v_norm_mix_w, v_w_in, v_w_out, v_norm_ffn_w, v_w_gate, v_w_up, v_w_down, v_norm_final_w):
    tr = {"w_gate", "w_up"}
    names = ["w_in", "w_out", "w_gate", "w_up", "w_down"]

    def view(a, n):
        return a[0].T if n in tr else a[0]

    big_w = [view(a, n) for a, n in zip([w_in, w_out, w_gate, w_up, w_down], names)]
    big_m = [view(a, n) for a, n in zip([m_w_in, m_w_out, m_w_gate, m_w_up, m_w_down], names)]
    big_v = [view(a, n) for a, n in zip([v_w_in, v_w_out, v_w_gate, v_w_up, v_w_down], names)]

    shards = [_cast_bf16(w, "cast_" + n) for w, n in zip(big_w, names)]
    (win,) = _all_gather(shards[0:1], "all_gather_w_in", 1)
    (wout,) = _all_gather(shards[1:2], "all_gather_w_out", 2)
    (wgu,) = _all_gather(shards[2:4], "all_gather_gate_up", 3, stacked=True)
    (wd,) = _all_gather(shards[4:5], "all_gather_w_down", 4)
    nw3 = norm_final_w.reshape(1, D)
    place = jnp.stack([lax.axis_index("x"), lax.axis_index("y"), lax.axis_index("c")]).astype(jnp.int32)
    state = {n: (w, m, v, 0, 1) for n, w, m, v in zip(names, big_w, big_m, big_v)}
    for part in range(W_IN_PARTS):
        state[f"w_in_{part}"] = state["w_in"][:3] + (part, W_IN_PARTS)
    red = _Reduction(place, 5, state)
    stats, gx, *_ = _local_step(
        x[0], loss_target[0], norm_mix_w, norm_ffn_w, nw3, win, wout.reshape(D, D), wgu.reshape(NFG, 2 * N_FG, D), wd, red)
    stats_all = _exchange_stats(stats, red.next_id())
    upd = [red.update(f"w_in_{W_IN_PARTS - 1}" if n == "w_in" else n) for n in names]
    stats_all = lax.optimization_barrier((stats_all, tuple(upd[0])))[0]

    def rows(a, b, c):
        return jnp.concatenate([a.reshape(1, D), b.reshape(1, D), c.reshape(1, D), jnp.zeros((5, D), F32)], axis=0)

    sg, sd, sm, sv = _small_update(stats_all, rows(norm_mix_w, norm_ffn_w, norm_final_w),
                                   rows(m_norm_mix_w, m_norm_ffn_w, m_norm_final_w),
                                   rows(v_norm_mix_w, v_norm_ffn_w, v_norm_final_w))
    loss = sg[3, 0]

    def outs(k, small):
        big = [(u[k].T if n in tr else u[k])[None] for u, n in zip(upd, names)]
        return [small[0:1], big[0], big[1], small[1:2], big[2], big[3], big[4], small[2]]

    return (loss, gx[None], *outs(0, sg), *outs(1, sd), *outs(2, sm), *outs(3, sv))
```

```python
import math

import numpy as np
import jax
import jax.numpy as jnp
from jax import lax
from jax.experimental import pallas as pl
from jax.experimental.pallas import tpu as pltpu
from jax.experimental.pallas import tpu_sc as plsc

F32 = jnp.float32
BF16 = jnp.bfloat16

S = 2048
D = 2048
NDEV = 8
N_IN = 7168 // NDEV
N_FF = 5632 // NDEV
NFG, N_FG = NDEV // 2, 2 * N_FF
FF_PER, FF_ROWS = 4, N_FF // 2
N_OUT = 2048 // NDEV
AH, AHD = 8, 128
RH, RHD = 4, 256
CH = 128
NB = S // CH
EPS = 1e-6
PATTERNS = ((1, 16), (4, 4), (16, 1))
NEG = -1e30
VMEM_LIMIT = 56 * 1024 * 1024

ADAM_LR, ADAM_B1, ADAM_B2, ADAM_EPS, ADAM_WD, ADAM_STEP = 0.001, 0.9, 0.999, 1e-08, 0.01, 10
MESH = pl.DeviceIdType.MESH


def _cp(sem=None):
    return pltpu.CompilerParams(dimension_semantics=sem, vmem_limit_bytes=VMEM_LIMIT)


def _dot(a, b):
    return jnp.dot(a, b, preferred_element_type=F32)


def _dot_nt(a, b):
    return lax.dot_general(a, b, (((1,), (1,)), ((), ())), preferred_element_type=F32)


def _dot_tn(a, b):
    return lax.dot_general(a, b, (((0,), (0,)), ((), ())), preferred_element_type=F32)


def _sigmoid(x):
    return 0.5 * jnp.tanh(0.5 * x) + 0.5


def _cast_bf16(w, name):
    r, c = w.shape
    tm = r if r <= 1024 else 512

    def body(w_ref, o_ref):
        o_ref[...] = w_ref[...].astype(BF16)

    return pl.pallas_call(
        body, name=name, grid=(r // tm,),
        in_specs=[pl.BlockSpec((tm, c), lambda i: (i, 0))],
        out_specs=pl.BlockSpec((tm, c), lambda i: (i, 0)),
        out_shape=jax.ShapeDtypeStruct((r, c), BF16),
        compiler_params=_cp(("parallel",)),
    )(w)


def _rms_fwd(x, nw):
    tm = 256

    def body(x_ref, w_ref, h_ref, r_ref):
        xs = x_ref[...]
        r = lax.rsqrt(jnp.mean(xs * xs, axis=-1, keepdims=True) + EPS)
        h_ref[...] = ((xs * r) * w_ref[...]).astype(BF16)
        r_ref[...] = r

    return pl.pallas_call(
        body, name="rms_fwd", grid=(S // tm,),
        in_specs=[pl.BlockSpec((tm, D), lambda i: (i, 0)), pl.BlockSpec((1, D), lambda i: (0, 0))],
        out_specs=[pl.BlockSpec((tm, D), lambda i: (i, 0)), pl.BlockSpec((tm, 1), lambda i: (i, 0))],
        out_shape=[jax.ShapeDtypeStruct((S, D), BF16), jax.ShapeDtypeStruct((S, 1), F32)],
        compiler_params=_cp(("parallel",)),
    )(x, nw)


def _row_copies(hbm_refs, bufs, sems, m, tm):
    rows = pl.ds(pl.multiple_of(m * tm, tm), tm)
    return [pltpu.make_async_copy(h.at[rows], b, sems.at[i]) for i, (h, b) in enumerate(zip(hbm_refs, bufs))]


def _rms_bwd_tile(dh, xs, r, nw):
    dnw = jnp.sum(dh * (xs * r), axis=0, keepdims=True)
    gy = dh * nw
    dx = r * gy - xs * ((r * r * r) * jnp.mean(gy * xs, axis=-1, keepdims=True))
    return dx, dnw


def _proj(h1, win):
    tm = 1024

    def body(a_ref, w_ref, o_ref):
        o_ref[...] = _dot(a_ref[...], w_ref[...])

    return pl.pallas_call(
        body, name="proj", grid=(NDEV, S // tm),
        in_specs=[pl.BlockSpec((tm, D), lambda p, m: (m, 0)),
                  pl.BlockSpec((None, D, N_IN), lambda p, m: (p, 0, 0))],
        out_specs=pl.BlockSpec((tm, N_IN), lambda p, m: (m, p)),
        out_shape=jax.ShapeDtypeStruct((S, NDEV * N_IN), F32),
        compiler_params=_cp(("parallel", "parallel")),
    )(h1, win)


def _out_proj_rms(x, ma, mr, wout, nw):
    tm = 256
    half = D // 2

    def body(x_ref, ma_ref, mr_ref, w_ref, nw_ref, x2_ref, h_ref, r_ref):
        acc = _dot(ma_ref[...], w_ref[0:half, :]) + _dot(mr_ref[...], w_ref[half:D, :])
        x2 = x_ref[...] + acc
        r = lax.rsqrt(jnp.mean(x2 * x2, axis=-1, keepdims=True) + EPS)
        x2_ref[...] = x2
        h_ref[...] = ((x2 * r) * nw_ref[...]).astype(BF16)
        r_ref[...] = r

    return pl.pallas_call(
        body, name="out_proj_rms", grid=(S // tm,),
        in_specs=[pl.BlockSpec((tm, D), lambda i: (i, 0)),
                  pl.BlockSpec((tm, half), lambda i: (i, 0)),
                  pl.BlockSpec((tm, half), lambda i: (i, 0)),
                  pl.BlockSpec((D, D), lambda i: (0, 0)),
                  pl.BlockSpec((1, D), lambda i: (0, 0))],
        out_specs=[pl.BlockSpec((tm, D), lambda i: (i, 0)), pl.BlockSpec((tm, D), lambda i: (i, 0)),
                   pl.BlockSpec((tm, 1), lambda i: (i, 0))],
        out_shape=[jax.ShapeDtypeStruct((S, D), F32), jax.ShapeDtypeStruct((S, D), BF16),
                   jax.ShapeDtypeStruct((S, 1), F32)],
        compiler_params=_cp(("parallel",)),
    )(x, ma, mr, wout, nw)


def _ffn_up(h2, wgu, part, before=None):
    tm = 512

    def body(h_ref, w_ref, *rest):
        a_ref, dadg_ref, dadu_ref = rest[-3:]
        gu = _dot_nt(h_ref[...], w_ref[...])
        g, u = gu[:, 0:N_FG], gu[:, N_FG:2 * N_FG]
        sg = _sigmoid(g)
        silu = g * sg
        a_ref[...] = (silu * u).astype(BF16)
        dadg_ref[...] = (u * (sg * (1.0 + g * (1.0 - sg)))).astype(BF16)
        dadu_ref[...] = silu.astype(BF16)

    half = NFG // 2
    first = part * half
    wfirst = first if wgu.shape[0] == NFG else 0
    before = list(before or [])
    blk = pl.BlockSpec((None, tm, N_FG), lambda p, m: (p + first, m, 0))
    return pl.pallas_call(
        body, name=f"ffn_up_{part}", grid=(half, S // tm),
        in_specs=[pl.BlockSpec((tm, D), lambda p, m: (m, 0)),
                  pl.BlockSpec((None, 2 * N_FG, D), lambda p, m: (p + wfirst, 0, 0))]
        + [pl.BlockSpec(memory_space=pl.ANY)] * len(before),
        out_specs=[blk, blk, blk],
        out_shape=[jax.ShapeDtypeStruct((NFG, S, N_FG), BF16)] * 3,
        input_output_aliases={2 + k: k for k in range(len(before))},
        compiler_params=_cp(("parallel", "parallel")),
    )(h2, wgu, *before)


def _ffn_down_first(x2, a, wd):
    tm = 512
    n = wd.shape[0]

    def body(x_ref, a_ref, w_ref, o_ref):
        p = pl.program_id(1)

        @pl.when(p == 0)
        def _():
            o_ref[...] = x_ref[...] + _dot(a_ref[...], w_ref[...])

        @pl.when(p > 0)
        def _():
            o_ref[...] += _dot(a_ref[...], w_ref[...])

    return pl.pallas_call(
        body, name="ffn_down_first", grid=(S // tm, n),
        in_specs=[pl.BlockSpec((tm, D), lambda m, p: (m, 0)),
                  pl.BlockSpec((None, tm, N_FG), lambda m, p: (p, m, 0)),
                  pl.BlockSpec((None, N_FG, D), lambda m, p: (p, 0, 0))],
        out_specs=pl.BlockSpec((tm, D), lambda m, p: (m, 0)),
        out_shape=jax.ShapeDtypeStruct((S, D), F32),
        compiler_params=_cp(("parallel", "arbitrary")),
    )(x2, a, wd)


def _ffn_down_loss(x2, a, wd, nw, tgt, first=0):
    tm = 512

    def body(x2_hbm, a_ref, w_ref, nw_ref, t_hbm, dx_ref, dxb_ref, st_ref, acc_ref, x2_buf, t_buf, sems):
        m, p = pl.program_id(0), pl.program_id(1)
        tail_in = _row_copies((x2_hbm, t_hbm), (x2_buf, t_buf), sems, m, tm)

        @pl.when(p == 0)
        def _():
            acc_ref[...] = jnp.zeros_like(acc_ref)
            for cp in tail_in:
                cp.start()

        @pl.when((p == 0) & (m == 0))
        def _():
            st_ref[...] = jnp.zeros_like(st_ref)

        acc_ref[...] += _dot(a_ref[...], w_ref[...])

        @pl.when(p == NFG - first - 1)
        def _():
            for cp in tail_in:
                cp.wait()
            x3 = x2_buf[...] + acc_ref[...]
            nwv = nw_ref[...]
            r = lax.rsqrt(jnp.mean(x3 * x3, axis=-1, keepdims=True) + EPS)
            y = (x3 * r) * nwv
            err = y - t_buf[...]
            loss = 0.5 * jnp.sum(jnp.mean(err * err, axis=-1, keepdims=True), axis=0, keepdims=True)
            dy = err * (1.0 / D)
            dx, dnw = _rms_bwd_tile(dy, x3, r, nwv)
            dx_ref[...] = dx
            dxb_ref[...] = dx.astype(BF16)
            st_ref[0:1, :] += dnw
            st_ref[1:2, :] += jnp.broadcast_to(loss, (1, D))

    return pl.pallas_call(
        body, name="ffn_down_loss", grid=(S // tm, NFG - first),
        in_specs=[pl.BlockSpec(memory_space=pl.ANY),
                  pl.BlockSpec((None, tm, N_FG), lambda m, p: (p + first, m, 0)),
                  pl.BlockSpec((None, N_FG, D), lambda m, p: (p + first, 0, 0)),
                  pl.BlockSpec((1, D), lambda m, p: (0, 0)),
                  pl.BlockSpec(memory_space=pl.ANY)],
        out_specs=[pl.BlockSpec((tm, D), lambda m, p: (m, 0)), pl.BlockSpec((tm, D), lambda m, p: (m, 0)),
                   pl.BlockSpec((8, D), lambda m, p: (0, 0))],
        out_shape=[jax.ShapeDtypeStruct((S, D), F32), jax.ShapeDtypeStruct((S, D), BF16),
                   jax.ShapeDtypeStruct((8, D), F32)],
        scratch_shapes=[pltpu.VMEM((tm, D), F32), pltpu.VMEM((tm, D), F32), pltpu.VMEM((tm, D), F32),
                        pltpu.SemaphoreType.DMA((2,))],
        compiler_params=_cp(("arbitrary", "arbitrary")),
    )(x2, a, wd, nw, tgt)


def _ffn_down_bwd(dx3b, wd, dadg, dadu, part, before=None):
    tm = 1024
    half = NFG // 2

    def body(dx_ref, w_ref, dadg_ref, dadu_ref, *rest):
        dgu_ref = rest[-1]
        da = _dot_nt(dx_ref[...], w_ref[...])
        dgu_ref[:, 0:N_FG] = (da * dadg_ref[...].astype(F32)).astype(BF16)
        dgu_ref[:, N_FG:2 * N_FG] = (da * dadu_ref[...].astype(F32)).astype(BF16)

    blk = pl.BlockSpec((None, tm, N_FG), lambda p, m: (p + part * half, m, 0))
    before = list(before or [])
    return pl.pallas_call(
        body, name=f"ffn_down_bwd_{part}", grid=(half, S // tm),
        in_specs=[pl.BlockSpec((tm, D), lambda p, m: (m, 0)),
                  pl.BlockSpec((None, N_FG, D), lambda p, m: (p + part * half, 0, 0)), blk, blk]
        + [pl.BlockSpec(memory_space=pl.ANY)] * len(before),
        out_specs=pl.BlockSpec((None, tm, 2 * N_FG), lambda p, m: (p + part * half, m, 0)),
        out_shape=jax.ShapeDtypeStruct((NFG, S, 2 * N_FG), BF16),
        input_output_aliases={4 + k: k for k in range(len(before))},
        compiler_params=_cp(("parallel", "parallel")),
    )(dx3b, wd, dadg, dadu, *before)


def _ffn_up_bwd(dgu, wgu, dres, xs, r, nw):
    tm = 512

    def body(dgu_ref, w_ref, dres_hbm, x_hbm, r_ref, nw_ref, dx_ref, dxb_ref, st_ref, dres_buf, x_buf, sems):
        m, p = pl.program_id(0), pl.program_id(1)
        tail_in = _row_copies((dres_hbm, x_hbm), (dres_buf, x_buf), sems, m, tm)

        @pl.when(p == 0)
        def _():
            dx_ref[...] = jnp.zeros_like(dx_ref)
            for cp in tail_in:
                cp.start()

        @pl.when((p == 0) & (m == 0))
        def _():
            st_ref[...] = jnp.zeros_like(st_ref)

        dx_ref[...] += _dot(dgu_ref[...], w_ref[...])

        @pl.when(p == NFG - 1)
        def _():
            for cp in tail_in:
                cp.wait()
            dx, dnw = _rms_bwd_tile(dx_ref[...], x_buf[...], r_ref[...], nw_ref[...])
            dx = dres_buf[...] + dx
            dx_ref[...] = dx
            dxb_ref[...] = dx.astype(BF16)
            st_ref[0:1, :] += dnw

    blk = pl.BlockSpec((None, tm, 2 * N_FG), lambda m, p: (p, m, 0))
    wblk = pl.BlockSpec((None, 2 * N_FG, D), lambda m, p: (p, 0, 0))
    row = pl.BlockSpec((tm, D), lambda m, p: (m, 0))
    hbm = pl.BlockSpec(memory_space=pl.ANY)
    return pl.pallas_call(
        body, name="ffn_up_bwd", grid=(S // tm, NFG),
        in_specs=[blk, wblk, hbm, hbm, pl.BlockSpec((tm, 1), lambda m, p: (m, 0)),
                  pl.BlockSpec((1, D), lambda m, p: (0, 0))],
        out_specs=[row, row, pl.BlockSpec((8, D), lambda m, p: (0, 0))],
        out_shape=[jax.ShapeDtypeStruct((S, D), F32), jax.ShapeDtypeStruct((S, D), BF16),
                   jax.ShapeDtypeStruct((8, D), F32)],
        scratch_shapes=[pltpu.VMEM((tm, D), F32), pltpu.VMEM((tm, D), F32), pltpu.SemaphoreType.DMA((2,))],
        compiler_params=_cp(("arbitrary", "arbitrary")),
    )(dgu, wgu, dres, xs, r, nw)


def _out_proj_bwd(dx2b, wout, place=None, rider=None):
    tm = 256

    if rider is None:
        def body(dx_ref, w_ref, o_ref):
            o_ref[...] = _dot_nt(dx_ref[...], w_ref[...])

        return pl.pallas_call(
            body, name="out_proj_bwd", grid=(S // tm,),
            in_specs=[pl.BlockSpec((tm, D), lambda i: (i, 0)), pl.BlockSpec((D, D), lambda i: (0, 0))],
            out_specs=pl.BlockSpec((tm, D), lambda i: (i, 0)),
            out_shape=jax.ShapeDtypeStruct((S, D), F32),
            compiler_params=_cp(("parallel",)),
        )(dx2b, wout), None

    w = rider[0]
    r, c = w.shape
    rt = _row_tile(r, c)
    nt = r // rt
    assert nt <= S // tm

    def body(pos_ref, dx_ref, w_ref, uw, um, uv, ug, us, uc, o_ref, go, dd, mo, vo):
        o_ref[...] = _dot_nt(dx_ref[...], w_ref[...])

        @pl.when(pl.program_id(0) < nt)
        def _():
            _update_tile(uw, um, uv, ug, us, uc, go, dd, mo, vo)

    def at(i):
        return jnp.minimum(i, nt - 1)

    tile = pl.BlockSpec((rt, c), lambda i, pos: (at(i), 0))
    outs = pl.pallas_call(
        body, name="out_proj_bwd",
        grid_spec=pltpu.PrefetchScalarGridSpec(
            num_scalar_prefetch=1, grid=(S // tm,),
            in_specs=[pl.BlockSpec((tm, D), lambda i, pos: (i, 0)), pl.BlockSpec((D, D), lambda i, pos: (0, 0)),
                      tile, tile, tile,
                      pl.BlockSpec((None, rt, c), lambda i, pos: (4 * pos[0] + 2 * pos[1] + pos[2], at(i), 0)),
                      pl.BlockSpec((None, rt, c), lambda i, pos: (2 * pos[0] + pos[1], at(i), 0)),
                      pl.BlockSpec((3, rt, c), lambda i, pos: (0, at(i), 0))],
            out_specs=[pl.BlockSpec((tm, D), lambda i, pos: (i, 0)), tile, tile, tile, tile]),
        out_shape=[jax.ShapeDtypeStruct((S, D), F32)] + [jax.ShapeDtypeStruct((r, c), F32)] * 4,
        compiler_params=_cp(("arbitrary",)),
    )(place, dx2b, wout, *rider)
    return outs[0], outs[1:]


def _in_proj_bwd(dproj, win, dres, xs, r, nw):
    tm = 1024

    def body(dp_ref, w_ref, dres_hbm, x_hbm, r_ref, nw_ref, dx_ref, st_ref, dres_buf, x_buf, sems):
        m, p = pl.program_id(0), pl.program_id(1)
        tail_in = _row_copies((dres_hbm, x_hbm), (dres_buf, x_buf), sems, m, tm)

        @pl.when(p == 0)
        def _():
            dx_ref[...] = jnp.zeros_like(dx_ref)
            for cp in tail_in:
                cp.start()

        @pl.when((p == 0) & (m == 0))
        def _():
            st_ref[...] = jnp.zeros_like(st_ref)

        dx_ref[...] += _dot_nt(dp_ref[...], w_ref[...])

        @pl.when(p == NDEV - 1)
        def _():
            for cp in tail_in:
                cp.wait()
            dx, dnw = _rms_bwd_tile(dx_ref[...], x_buf[...], r_ref[...], nw_ref[...])
            dx_ref[...] = dres_buf[...] + dx
            st_ref[0:1, :] += dnw

    row = pl.BlockSpec((tm, D), lambda m, p: (m, 0))
    hbm = pl.BlockSpec(memory_space=pl.ANY)
    return pl.pallas_call(
        body, name="in_proj_bwd", grid=(S // tm, NDEV),
        in_specs=[pl.BlockSpec((tm, N_IN), lambda m, p: (m, p)),
                  pl.BlockSpec((None, D, N_IN), lambda m, p: (p, 0, 0)),
                  hbm, hbm, pl.BlockSpec((tm, 1), lambda m, p: (m, 0)),
                  pl.BlockSpec((1, D), lambda m, p: (0, 0))],
        out_specs=[row, pl.BlockSpec((8, D), lambda m, p: (0, 0))],
        out_shape=[jax.ShapeDtypeStruct((S, D), F32), jax.ShapeDtypeStruct((8, D), F32)],
        scratch_shapes=[pltpu.VMEM((tm, D), F32), pltpu.VMEM((tm, D), F32), pltpu.SemaphoreType.DMA((2,))],
        compiler_params=_cp(("arbitrary", "arbitrary")),
    )(dproj, win, dres, xs, r, nw)


W_IN_PARTS = 2


def _wgrad_in(h1, dproj, part):
    rows = D // W_IN_PARTS

    def body(a_ref, d_ref, o_ref):
        both = _dot_tn(a_ref[...], d_ref[...]).astype(BF16)
        o_ref[0] = both[:, 0:N_IN]
        o_ref[1] = both[:, N_IN:2 * N_IN]

    return pl.pallas_call(
        body, name=f"wgrad_in_{part}", grid=(NDEV // 2,),
        in_specs=[pl.BlockSpec((S, rows), lambda p: (0, part)), pl.BlockSpec((S, 2 * N_IN), lambda p: (0, p))],
        out_specs=pl.BlockSpec((2, rows, N_IN), lambda p: (p, 0, 0)),
        out_shape=jax.ShapeDtypeStruct((NDEV, rows, N_IN), BF16),
        compiler_params=_cp(("parallel",)),
    )(h1, dproj)


def _wgrad_rows(a3, dy, name, col=0):
    def body(a_ref, d_ref, o_ref):
        dw = _dot_tn(a_ref[...], d_ref[...]).astype(BF16)
        for j in range(FF_PER):
            o_ref[j] = dw[j * FF_ROWS:(j + 1) * FF_ROWS]

    return pl.pallas_call(
        body, name=name, grid=(NFG,),
        in_specs=[pl.BlockSpec((None, S, N_FG), lambda p: (p, 0, col)), pl.BlockSpec((S, D), lambda p: (0, 0))],
        out_specs=pl.BlockSpec((FF_PER, FF_ROWS, D), lambda p: (p % 2, p // 2, 0)),
        out_shape=jax.ShapeDtypeStruct((NDEV, N_FF, D), BF16),
        compiler_params=_cp(("parallel",)),
    )(a3, dy)


def _wgrad_out(ma, mr, dx2b):
    half = D // 2
    per = half // N_OUT

    def body(ma_ref, mr_ref, d_ref, o_ref):
        p = pl.program_id(0)

        @pl.when(p == 0)
        def _():
            o_ref[...] = _dot_tn(ma_ref[...], d_ref[...]).astype(BF16).reshape(per, N_OUT, D)

        @pl.when(p == 1)
        def _():
            o_ref[...] = _dot_tn(mr_ref[...], d_ref[...]).astype(BF16).reshape(per, N_OUT, D)

    whole = pl.BlockSpec((S, half), lambda p: (0, 0))
    return pl.pallas_call(
        body, name="wgrad_out", grid=(2,),
        in_specs=[whole, whole, pl.BlockSpec((S, D), lambda p: (0, 0))],
        out_specs=pl.BlockSpec((per, N_OUT, D), lambda p: (p, 0, 0)),
        out_shape=jax.ShapeDtypeStruct((NDEV, N_OUT, D), BF16),
        compiler_params=_cp(("parallel",)),
    )(ma, mr, dx2b)


def _attn_consts():
    c = np.zeros((AH, 8, AHD), np.float32)
    for h in range(AH):
        c[h, :, :] = 2.0 ** (-(h + 1))
    return jnp.asarray(c)


def _permute_in(dst, src, d, cast=None):
    v = src[...]
    if d > 1:
        v = pltpu.einshape("jrc->rjc", v.reshape(S // d, d, AHD)).reshape(S, AHD)
    dst[...] = v if cast is None else v.astype(cast)


def _natural_order(v, d):
    if d == 1:
        return v
    return pltpu.einshape("rjc->jrc", v.reshape(d, S // d, AHD)).reshape(S, AHD)


def _attn_masks():
    qi = lax.broadcasted_iota(jnp.int32, (CH, CH), 0)
    kj = lax.broadcasted_iota(jnp.int32, (CH, CH), 1)
    dist_c = (qi - kj).astype(F32)
    dist_p = (qi - kj + CH).astype(F32)
    return (qi >= kj)[None], (kj >= qi)[None], dist_c[None], dist_p[None]


GB = 16


def _bdot_nt(a, b):
    return lax.dot_general(a, b, (((2,), (2,)), ((0,), (0,))), preferred_element_type=F32)


def _bdot(a, b):
    return lax.dot_general(a, b, (((2,), (1,)), ((0,), (0,))), preferred_element_type=F32)


def _bdot_tn(a, b):
    return lax.dot_general(a, b, (((1,), (1,)), ((0,), (0,))), preferred_element_type=F32)


def _shift_block(dst, src):
    dst[0:CH, :] = jnp.zeros((CH, AHD), dst.dtype)
    dst[CH:S, :] = src[0:S - CH, :]


def _has_prev(g, nb):
    blk = lax.broadcasted_iota(jnp.int32, (GB, 1, 1), 0) + g * GB
    return (blk & (nb - 1)) != 0


def _blocks(ref, g):
    return ref[g * GB * CH:(g + 1) * GB * CH, :].reshape(GB, CH, AHD)


def _attn_fwd(proj):
    scale = 1.0 / math.sqrt(AHD)

    def body(c_ref, q_ref, k_ref, v_ref, o_ref, ob_ref, lse_ref, qkvp_ref, lsep_ref, qd, kd, vd, kps, vps, od, ld, *nat):
        onat, lnat = nat[0:3], nat[3:6]
        slope = c_ref[0:1, :]
        mask_c, mask_p, dist_c, dist_p = _attn_masks()
        for pi, (d, nb) in enumerate(PATTERNS):
            _permute_in(qd, q_ref, d, BF16)
            _permute_in(kd, k_ref, d, BF16)
            _permute_in(vd, v_ref, d, BF16)
            if d > 1:
                qkvp_ref[pi - 1, 0] = qd[...]
                qkvp_ref[pi - 1, 1] = kd[...]
                qkvp_ref[pi - 1, 2] = vd[...]
            if nb > 1:
                _shift_block(kps, kd)
                _shift_block(vps, vd)
            bias_c = -(slope * float(d)) * dist_c
            bias_p = -(slope * float(d)) * dist_p
            for g in range(NB // GB):
                q3, k3, v3 = _blocks(qd, g), _blocks(kd, g), _blocks(vd, g)
                s_c = jnp.where(mask_c, _bdot_nt(q3, k3) * scale + bias_c, NEG)
                mx = jnp.max(s_c, axis=-1, keepdims=True)
                if nb > 1:
                    kp3, vp3 = _blocks(kps, g), _blocks(vps, g)
                    s_p = jnp.where(jnp.logical_and(mask_p, _has_prev(g, nb)),
                                    _bdot_nt(q3, kp3) * scale + bias_p, NEG)
                    mx = jnp.maximum(mx, jnp.max(s_p, axis=-1, keepdims=True))
                    l = (jnp.sum(jnp.exp(s_c - mx), axis=-1, keepdims=True)
                         + jnp.sum(jnp.exp(s_p - mx), axis=-1, keepdims=True))
                    lse = mx + jnp.log(l)
                    o3 = _bdot(jnp.exp(s_c - lse).astype(BF16), v3) + _bdot(jnp.exp(s_p - lse).astype(BF16), vp3)
                else:
                    l = jnp.sum(jnp.exp(s_c - mx), axis=-1, keepdims=True)
                    lse = mx + jnp.log(l)
                    o3 = _bdot(jnp.exp(s_c - lse).astype(BF16), v3)
                rows = slice(g * GB * CH, (g + 1) * GB * CH)
                od[rows, :] = o3.reshape(GB * CH, AHD)
                ld[rows, :] = jnp.broadcast_to(lse, (GB, CH, AHD)).reshape(GB * CH, AHD)
            onat[pi][...] = _natural_order(od[...], d)
            lnat[pi][...] = _natural_order(ld[...], d)
        l0, l1, l2 = lnat[0][...], lnat[1][...], lnat[2][...]
        mx = jnp.maximum(jnp.maximum(l0, l1), l2)
        e0, e1, e2 = jnp.exp(l0 - mx), jnp.exp(l1 - mx), jnp.exp(l2 - mx)
        den = e0 + e1 + e2
        out = (e0 / den) * onat[0][...] + (e1 / den) * onat[1][...] + (e2 / den) * onat[2][...]
        o_ref[...] = out
        ob_ref[...] = out.astype(BF16)
        lse_ref[...] = mx + jnp.log(den)
        for pi, (d, _) in enumerate(PATTERNS[1:]):
            _permute_in(lsep_ref.at[pi], lse_ref, d)

    def col(off):
        return pl.BlockSpec((S, AHD), lambda h: (0, off + h))

    return pl.pallas_call(
        body, name="attn_fwd", grid=(AH,),
        in_specs=[pl.BlockSpec((None, 8, AHD), lambda h: (h, 0, 0)), col(0), col(AH), col(2 * AH)],
        out_specs=[col(0), col(0), col(0), pl.BlockSpec((2, 3, S, AHD), lambda h: (0, 0, 0, h)),
                   pl.BlockSpec((2, S, AHD), lambda h: (0, 0, h))],
        out_shape=[jax.ShapeDtypeStruct((S, AH * AHD), F32), jax.ShapeDtypeStruct((S, AH * AHD), BF16),
                   jax.ShapeDtypeStruct((S, AH * AHD), F32),
                   jax.ShapeDtypeStruct((2, 3, S, AH * AHD), BF16), jax.ShapeDtypeStruct((2, S, AH * AHD), F32)],
        scratch_shapes=[pltpu.VMEM((S, AHD), BF16) for _ in range(5)]
        + [pltpu.VMEM((S, AHD), F32) for _ in range(8)],
        compiler_params=_cp(("parallel",)),
    )(_attn_consts(), proj, proj, proj)


def _attn_bwd(proj, dmixed, o, lse, qkvp, lsep):
    scale = 1.0 / math.sqrt(AHD)

    def body(c_ref, q_ref, k_ref, v_ref, do_ref, o_ref, lse_ref, qkvp_ref, lsep_ref, dproj_hbm,
             qd, kd, vd, dod, kps, vps, dld, dqd, dkd, dvd, delta, aq, ak, av, sq, sk, sv, sems):
        h = pl.program_id(0)

        def out_copies(head):
            return [pltpu.make_async_copy(
                st, dproj_hbm.at[:, pl.ds(pl.multiple_of((k * AH + head) * AHD, AHD), AHD)], sems.at[k])
                for k, st in enumerate((sq, sk, sv))]

        slope = c_ref[0:1, :]
        mask_c, mask_p, dist_c, dist_p = _attn_masks()
        delta[...] = jnp.broadcast_to(jnp.sum(do_ref[...] * o_ref[...], axis=-1, keepdims=True), (S, AHD))
        for pi, (d, nb) in enumerate(PATTERNS):
            if d == 1:
                _permute_in(qd, q_ref, d, BF16)
                _permute_in(kd, k_ref, d, BF16)
                _permute_in(vd, v_ref, d, BF16)
                qs, ks, vs, lss = qd, kd, vd, lse_ref
            else:
                qs, ks, vs, lss = (qkvp_ref.at[pi - 1, 0], qkvp_ref.at[pi - 1, 1], qkvp_ref.at[pi - 1, 2],
                                   lsep_ref.at[pi - 1])
            _permute_in(dod, do_ref, d, BF16)
            _permute_in(dld, delta, d)
            if nb > 1:
                _shift_block(kps, ks)
                _shift_block(vps, vs)
            bias_c = -(slope * float(d)) * dist_c
            bias_p = -(slope * float(d)) * dist_p
            for g in range(NB // GB):
                q3, k3, v3, do3 = _blocks(qs, g), _blocks(ks, g), _blocks(vs, g), _blocks(dod, g)
                ls, dl = _blocks(lss, g), _blocks(dld, g)
                lo, hi = g * GB * CH, (g + 1) * GB * CH
                p_c = jnp.exp(jnp.where(mask_c, _bdot_nt(q3, k3) * scale + bias_c, NEG) - ls)
                ds_c = ((p_c * (_bdot_nt(do3, v3) - dl)) * scale).astype(BF16)
                dq3 = _bdot(ds_c, k3)
                dkd[lo:hi, :] = _bdot_tn(ds_c, q3).reshape(GB * CH, AHD)
                dvd[lo:hi, :] = _bdot_tn(p_c.astype(BF16), do3).reshape(GB * CH, AHD)
                if nb > 1:
                    kp3, vp3 = _blocks(kps, g), _blocks(vps, g)
                    p_p = jnp.exp(jnp.where(jnp.logical_and(mask_p, _has_prev(g, nb)),
                                            _bdot_nt(q3, kp3) * scale + bias_p, NEG) - ls)
                    ds_p = ((p_p * (_bdot_nt(do3, vp3) - dl)) * scale).astype(BF16)
                    dq3 = dq3 + _bdot(ds_p, kp3)
                    dkp = _bdot_tn(ds_p, q3).reshape(GB * CH, AHD)
                    dvp = _bdot_tn(p_p.astype(BF16), do3).reshape(GB * CH, AHD)
                    if g == 0:
                        dkd[0:hi - CH, :] += dkp[CH:, :]
                        dvd[0:hi - CH, :] += dvp[CH:, :]
                    else:
                        dkd[lo - CH:hi - CH, :] += dkp
                        dvd[lo - CH:hi - CH, :] += dvp
                dqd[lo:hi, :] = dq3.reshape(GB * CH, AHD)
            ln = S // d
            for acc, src in ((aq, dqd), (ak, dkd), (av, dvd)):
                if pi == 0:
                    acc[...] = src[...]
                else:
                    acc[...] += _natural_order(src[...], d)

        @pl.when(h > 0)
        def _():
            for cp in out_copies(h - 1):
                cp.wait()

        sq[...] = aq[...].astype(BF16)
        sk[...] = ak[...].astype(BF16)
        sv[...] = av[...].astype(BF16)
        for cp in out_copies(h):
            cp.start()

        @pl.when(h == AH - 1)
        def _():
            for cp in out_copies(h):
                cp.wait()

    def col(off):
        return pl.BlockSpec((S, AHD), lambda h: (0, off + h))

    return pl.pallas_call(
        body, name="attn_bwd", grid=(AH,),
        in_specs=[pl.BlockSpec((None, 8, AHD), lambda h: (h, 0, 0)), col(0), col(AH), col(2 * AH),
                  col(0), col(0), col(0), pl.BlockSpec((2, 3, S, AHD), lambda h: (0, 0, 0, h)),
                  pl.BlockSpec((2, S, AHD), lambda h: (0, 0, h))],
        out_specs=pl.BlockSpec(memory_space=pl.ANY),
        out_shape=jax.ShapeDtypeStruct((S, NDEV * N_IN), BF16),
        scratch_shapes=[pltpu.VMEM((S, AHD), BF16) for _ in range(6)]
        + [pltpu.VMEM((S, AHD), F32) for _ in range(8)]
        + [pltpu.VMEM((S, AHD), BF16) for _ in range(3)] + [pltpu.SemaphoreType.DMA((3,))],
        compiler_params=_cp(("arbitrary",)),
    )(_attn_consts(), proj, proj, proj, dmixed, o, lse, qkvp, lsep)


def _ret_consts():
    c = np.zeros((RH, 8, RHD), np.float32)
    for h in range(RH):
        c[h, :, :] = np.log(np.float32(1.0) - np.float32(2.0 ** (-5.0 - h)))
    return jnp.asarray(c)


def _ret_factors(lg):
    i = lax.broadcasted_iota(jnp.int32, (CH, CH), 0)
    j = lax.broadcasted_iota(jnp.int32, (CH, CH), 1)
    dif = (i - j).astype(F32)
    decay = jnp.where(dif >= 0, jnp.exp(lg[:, 0:CH] * jnp.maximum(dif, 0.0)), 0.0)
    row = lax.broadcasted_iota(jnp.int32, (CH, RHD), 0).astype(F32)
    zeta = jnp.exp(lg * (CH - 1.0 - row))
    xi = jnp.exp(lg * (row + 1.0))
    return decay, zeta, xi, jnp.exp(lg * float(CH))


CBK = 8
RSTEPS = NB // CBK


def _ret_specs(rev):
    off = 3 * AH * AHD // RHD
    rows = CBK * CH

    def ch(n):
        return (RSTEPS - 1 - n) if rev else n

    def col(k):
        return pl.BlockSpec((rows, RHD), lambda h, n: (ch(n), off + k * RH + h))

    own = pl.BlockSpec((rows, RHD), lambda h, n: (ch(n), h))
    state = pl.BlockSpec((None, CBK, RHD, RHD), lambda h, n: (h, ch(n), 0, 0))
    const = pl.BlockSpec((None, 8, RHD), lambda h, n: (h, 0, 0))
    dm = pl.BlockSpec((rows, RHD), lambda h, n: (ch(n), AH * AHD // RHD + h))
    return col, own, state, const, dm


def _chunks(x):
    return x.reshape(CBK, CH, RHD)


def _ret_fwd(proj):
    def body(c_ref, q_ref, k_ref, v_ref, g_ref, ret_ref, mr_ref, st_ref, r_acc):
        n = pl.program_id(1)

        @pl.when(n == 0)
        def _():
            r_acc[...] = jnp.zeros_like(r_acc)

        decay, zeta, xi, gch = _ret_factors(c_ref[0:1, :])
        q3 = _chunks(q_ref[...].astype(BF16))
        kc = _chunks(k_ref[...] * (1.0 / math.sqrt(RHD)))
        k3 = kc.astype(BF16)
        v3 = _chunks(v_ref[...].astype(BF16))
        kv3 = _bdot_tn((kc * zeta[None]).astype(BF16), v3)
        r = r_acc[...]
        for i in range(CBK):
            st_ref[i] = r.astype(BF16)
            r = r * gch + kv3[i]
        r_acc[...] = r
        scores = _bdot_nt(q3, k3) * decay[None]
        ret = (_bdot(scores.astype(BF16), v3) + _bdot(q3, st_ref[...]) * xi[None]).reshape(CBK * CH, RHD)
        ret_ref[...] = ret
        rr = lax.rsqrt(jnp.mean(ret * ret, axis=-1, keepdims=True) + EPS)
        gv = g_ref[...]
        mr_ref[...] = ((gv * _sigmoid(gv)) * (ret * rr)).astype(BF16)

    col, own, state, const, _ = _ret_specs(False)
    return pl.pallas_call(
        body, name="ret_fwd", grid=(RH, RSTEPS),
        in_specs=[const, col(0), col(1), col(2), col(3)],
        out_specs=[own, own, state],
        out_shape=[jax.ShapeDtypeStruct((S, RH * RHD), F32), jax.ShapeDtypeStruct((S, RH * RHD), BF16),
                   jax.ShapeDtypeStruct((RH, NB, RHD, RHD), BF16)],
        scratch_shapes=[pltpu.VMEM((RHD, RHD), F32)],
        compiler_params=_cp(("parallel", "arbitrary")),
    )(_ret_consts(), proj, proj, proj, proj)


def _ret_bwd(proj, ret, states, dmixed, dproj):
    rows = CBK * CH
    col0 = 3 * AH * AHD

    def body(c_ref, q_ref, k_ref, v_ref, g_ref, ret_ref, st_ref, dm_ref, dproj_in, dproj_hbm, g_acc, gs,
             sq, sk, sv, sg, sems):
        del dproj_in
        h, n = pl.program_id(0), pl.program_id(1)
        step = h * RSTEPS + n

        def out_copies(t):
            hh, nn = t // RSTEPS, t % RSTEPS
            r0 = pl.multiple_of((RSTEPS - 1 - nn) * rows, rows)
            return [pltpu.make_async_copy(
                st, dproj_hbm.at[pl.ds(r0, rows), pl.ds(pl.multiple_of(col0 + (k * RH + hh) * RHD, RHD), RHD)],
                sems.at[k]) for k, st in enumerate((sq, sk, sv, sg))]

        @pl.when(n == 0)
        def _():
            g_acc[...] = jnp.zeros_like(g_acc)

        decay, zeta, xi, gch = _ret_factors(c_ref[0:1, :])
        ret_v = ret_ref[...]
        rr = lax.rsqrt(jnp.mean(ret_v * ret_v, axis=-1, keepdims=True) + EPS)
        gv = g_ref[...]
        sgm = _sigmoid(gv)
        dmix = dm_ref[...]
        dgate = ((dmix * (ret_v * rr)) * (sgm * (1.0 + gv * (1.0 - sgm)))).astype(BF16)
        dretn = dmix * (gv * sgm)
        dret = _chunks(rr * dretn - ret_v * ((rr * rr * rr) * jnp.mean(dretn * ret_v, axis=-1, keepdims=True)))

        q3 = _chunks(q_ref[...].astype(BF16))
        kc = _chunks(k_ref[...] * (1.0 / math.sqrt(RHD)))
        k3 = kc.astype(BF16)
        v3 = _chunks(v_ref[...].astype(BF16))
        d3 = dret.astype(BF16)
        dxi = (dret * xi[None]).astype(BF16)
        kz = (kc * zeta[None]).astype(BF16)
        dr3 = _bdot_tn(q3, dxi)
        acc = g_acc[...]
        for i in reversed(range(CBK)):
            gs[i] = acc.astype(BF16)
            acc = dr3[i] + gch * acc
        g_acc[...] = acc
        g3 = gs[...]
        sc = (_bdot_nt(q3, k3) * decay[None]).astype(BF16)
        da = (_bdot_nt(d3, v3) * decay[None]).astype(BF16)
        dq = _bdot(da, k3) + _bdot_nt(dxi, st_ref[...])
        dkc = _bdot_tn(da, q3) + _bdot_nt(v3, g3) * zeta[None]
        dv = _bdot_tn(sc, d3) + _bdot(kz, g3)

        @pl.when(step > 0)
        def _():
            for cp in out_copies(step - 1):
                cp.wait()

        sq[...] = dq.reshape(rows, RHD).astype(BF16)
        sk[...] = (dkc * (1.0 / math.sqrt(RHD))).reshape(rows, RHD).astype(BF16)
        sv[...] = dv.reshape(rows, RHD).astype(BF16)
        sg[...] = dgate
        for cp in out_copies(step):
            cp.start()

        @pl.when(step == RH * RSTEPS - 1)
        def _():
            for cp in out_copies(step):
                cp.wait()

    col, own, state, const, dm = _ret_specs(True)
    hbm = pl.BlockSpec(memory_space=pl.ANY)
    return pl.pallas_call(
        body, name="ret_bwd", grid=(RH, RSTEPS),
        in_specs=[const, col(0), col(1), col(2), col(3), own, state, dm, hbm],
        out_specs=hbm,
        out_shape=jax.ShapeDtypeStruct(dproj.shape, dproj.dtype),
        input_output_aliases={8: 0},
        scratch_shapes=[pltpu.VMEM((RHD, RHD), F32), pltpu.VMEM((CBK, RHD, RHD), BF16)]
        + [pltpu.VMEM((rows, RHD), BF16) for _ in range(4)] + [pltpu.SemaphoreType.DMA((4,))],
        compiler_params=_cp(("arbitrary", "arbitrary")),
    )(_ret_consts(), proj, proj, proj, proj, ret, states, dmixed, dproj)


class _NoReduction:
    def start(self, group, grads):
        pass

    def local(self, name, first=()):
        return []

    def landed(self, name):
        return []

    def update(self, name):
        return []

    place = None

    def rider(self, name):
        return None

    def set_update(self, name, outs):
        pass


def _local_step(x, tgt, nw1, nw2, nw3, win, wout, wgu_a, wgu, wd_a, wd, red=None):
    red = red or _NoReduction()

    def after(values, first):
        return lax.optimization_barrier((tuple(values), tuple(first)))[0]

    h1, r1 = _rms_fwd(x, nw1)
    proj = _proj(h1, win)
    o, ma, lse, qkvp, lsep = _attn_fwd(proj)
    ret, mr, states = _ret_fwd(proj)
    x2, h2, r2 = _out_proj_rms(x, ma, mr, wout, nw2)
    a, dadg, dadu = _ffn_up(h2, wgu, 1, _ffn_up(h2, wgu_a, 0))
    dx3, dx3b, st3 = _ffn_down_loss(_ffn_down_first(x2, a, wd_a), a, wd, nw3, tgt, first=wd_a.shape[0])

    dwd = _wgrad_rows(a, dx3b, "wgrad_down")
    red.start(["w_down"], [dwd])
    (dx3b,) = after([dx3b], [dwd])
    part = _ffn_down_bwd(dx3b, wd, dadg, dadu, 0)
    (dx3b,) = after([dx3b], red.local("w_down", first=[part]))
    dgu = _ffn_down_bwd(dx3b, wd, dadg, dadu, 1, [part])
    dwg = _wgrad_rows(dgu, h2, "wgrad_gate", 0)
    red.start(["w_gate"], [dwg])
    (dgu,) = after([dgu], [dwg])
    dwu = _wgrad_rows(dgu, h2, "wgrad_up", 1)
    red.start(["w_up"], [dwu])
    (dgu,) = after([dgu], red.local("w_gate", first=[dwu] + red.landed("w_down")))
    dx2, dx2b, st2 = _ffn_up_bwd(dgu, wgu, dx3, x2, r2, nw2)
    (dx2b,) = after([dx2b], red.local("w_up", first=[dx2b]))
    dwo = _wgrad_out(ma, mr, dx2b)
    red.start(["w_out"], [dwo])
    (dx2b,) = after([dx2b], [dwo])
    dmixed, done = _out_proj_bwd(dx2b, wout, red.place, red.rider("w_down"))
    red.set_update("w_down", done)
    dproj = _attn_bwd(proj, dmixed, o, lse, qkvp, lsep)
    (dmixed,) = after([dmixed], red.local("w_out", first=[dproj] + red.landed("w_gate")))
    dproj = _ret_bwd(proj, ret, states, dmixed, dproj)
    (dwi0,) = after([_wgrad_in(h1, dproj, 0)], red.landed("w_up"))
    red.start(["w_in_0"], [dwi0])
    (dproj,) = after([dproj], [dwi0])
    dwi1 = _wgrad_in(h1, dproj, 1)
    red.start(["w_in_1"], [dwi1])
    sums = red.local("w_in_0", first=[dwi1] + red.landed("w_out"))
    sums = red.local("w_in_1", first=sums + red.update("w_gate"))
    (dproj,) = after([dproj], sums)
    gx, st1 = _in_proj_bwd(dproj, win, dx2, x, r1, nw1)
    dwi = jnp.concatenate([dwi0, dwi1], axis=1)
    stats = jnp.concatenate([st1[0:1], st2[0:1], st3[0:2], jnp.zeros((4, D), F32)], axis=0)
    return stats, gx, dwi, dwo, dwg, dwu, dwd


def _place():
    x, y, c = lax.axis_index("x"), lax.axis_index("y"), lax.axis_index("c")
    return x, y, c, [(1 - x, y), (x, 1 - y), (1 - x, 1 - y)]


def _handshake(peers):
    barrier = pltpu.get_barrier_semaphore()
    for peer in peers:
        pl.semaphore_signal(barrier, inc=1, device_id=peer, device_id_type=MESH)
    pl.semaphore_wait(barrier, len(peers))


def _all_gather(shards, name, collective_id, per=0, rows=None, carry=None):
    na = len(shards)
    nout = 1 if per else na
    lo, r = rows or (0, shards[0].shape[0])
    ngroups = NDEV // per if per else 0
    base = carry.shape[0] if carry is not None else 0
    SIB, XN0, XN1, YN1, YN0, VIA_X, VIA_Y = 0, 1, 2, 3, 4, 5, 6
    D2D = {XN0: 7, XN1: 8, YN1: 9, YN0: 10, VIA_X: 11, VIA_Y: 12}

    def body(*refs):
        nin = na + (carry is not None)
        ins, outs = [ref.at[pl.ds(lo, r)] for ref in refs[:na]], refs[nin:nin + nout]
        send_sems, recv_sems, local_sems = refs[nin + nout:]
        x, y, c, _ = _place()
        me, sib = (x, y, c), (x, y, 1 - c)
        xn, yn, dg = (1 - x, y, c), (x, 1 - y, c), (1 - x, 1 - y, c)
        _handshake([sib, xn, yn])

        def part(ref, h):
            rows = ref.shape[0] // 2
            return ref if h is None else ref.at[pl.ds(h * rows, rows)]

        def block(a, owner, h):
            idx = 4 * owner[0] + 2 * owner[1] + owner[2]
            if not per:
                return part(outs[a].at[idx], h)
            return part(outs[0].at[base + idx // per, a, pl.ds(pl.multiple_of((idx % per) * r, r), r)], h)

        def copy(a, k, owner, h, to, own_src=False):
            return pltpu.make_async_remote_copy(
                src_ref=part(ins[a], h) if own_src else block(a, owner, h), dst_ref=block(a, owner, h),
                send_sem=send_sems.at[a, k], recv_sem=recv_sems.at[a, k], device_id=to, device_id_type=MESH)

        def other(p):
            return (p[0], p[1], 1 - c)

        mine = [pltpu.make_async_copy(ins[a], block(a, me, None), local_sems.at[a]) for a in range(na)]
        if carry is not None:
            mine.append(pltpu.make_async_copy(refs[na], outs[0].at[pl.ds(0, base)], local_sems.at[na]))
        for cp in mine:
            cp.start()
        sent = []
        for a in range(na):
            sent += [copy(a, XN0, me, 0, xn, True), copy(a, YN1, me, 1, yn, True),
                     copy(a, XN1, me, 1, xn, True), copy(a, YN0, me, 0, yn, True)]
        sent += [copy(a, SIB, me, None, sib, True) for a in range(na)]
        for cp in sent:
            cp.start()

        def landed(a, k, owner, h, then):
            copy(a, k, owner, h, me).wait_recv()
            for k2, to in then + [(D2D[k], sib)]:
                cp = copy(a, k2, owner, h, to)
                cp.start()
                sent.append(cp)

        for a in range(na):
            landed(a, XN0, xn, 0, [(VIA_Y, yn)])
            landed(a, YN1, yn, 1, [(VIA_X, xn)])
            landed(a, XN1, xn, 1, [])
            landed(a, YN0, yn, 0, [])
        for a in range(na):
            landed(a, VIA_Y, dg, 0, [])
            landed(a, VIA_X, dg, 1, [])
        for a in range(na):
            copy(a, SIB, sib, None, me).wait_recv()
            for k, owner, h in ((XN0, xn, 0), (XN1, xn, 1), (YN1, yn, 1), (YN0, yn, 0), (VIA_Y, dg, 0), (VIA_X, dg, 1)):
                copy(a, D2D[k], other(owner), h, me).wait_recv()
        for cp in sent:
            cp.wait_send()
        for cp in mine:
            cp.wait()

    if per:
        out_type = [jax.ShapeDtypeStruct((base + ngroups, na, per * r, shards[0].shape[1]), shards[0].dtype)]
    else:
        out_type = [jax.ShapeDtypeStruct((NDEV,) + s.shape, s.dtype) for s in shards]
    return _sequencer_call(
        body, name, collective_id, out_type,
        [pltpu.SemaphoreType.DMA((na, 13)), pltpu.SemaphoreType.DMA((na, 13)), pltpu.SemaphoreType.DMA((na + 1,))],
    )(*shards, *([carry] if carry is not None else []))


def _sequencer_call(body, name, collective_id, out_type, scratch_types):
    return pl.kernel(
        body, name=name, out_type=out_type,
        mesh=plsc.ScalarSubcoreMesh(axis_name="sequencer", num_cores=1),
        scratch_types=scratch_types,
        compiler_params=pltpu.CompilerParams(collective_id=collective_id))


def _exchange_sibling(grads, name, collective_id):
    na = len(grads)

    def body(*refs):
        ins, outs = refs[:na], refs[na:2 * na]
        send_sems, recv_sems = refs[2 * na:]
        x, y, c, _ = _place()
        _handshake([(x, y, 1 - c)])
        cps = []
        for a in range(na):
            for k in range(4):
                cps.append(pltpu.make_async_remote_copy(
                    src_ref=ins[a].at[2 * k + (1 - c)], dst_ref=outs[a].at[k],
                    send_sem=send_sems.at[a, k], recv_sem=recv_sems.at[a, k],
                    device_id=(x, y, 1 - c), device_id_type=MESH))
        for cp in cps:
            cp.start()
        for cp in cps:
            cp.wait()

    return _sequencer_call(
        body, name, collective_id,
        [jax.ShapeDtypeStruct((4,) + g.shape[1:], g.dtype) for g in grads],
        [pltpu.SemaphoreType.DMA((na, 4)), pltpu.SemaphoreType.DMA((na, 4))])(*grads)


def _row_tile(rows, cols):
    for t in (512, 256, 176, 128, 64, 32, 16):
        if rows % t == 0 and t * cols * 4 <= (2 << 20):
            return t
    raise ValueError((rows, cols))


STREAM_BUFS = 3


def _stream_tile(rows, steps):
    for t in (512, 256, 176, 128, 64, 32, 16):
        if rows % t == 0 and rows // t >= steps:
            return t
    raise ValueError((rows, steps))


def _stream(n, loads, stores, compute):
    for k in range(min(STREAM_BUFS, n)):
        for cp in loads(k):
            cp.start()
    for k in range(n):
        for cp in loads(k):
            cp.wait()
        if k >= 2:
            for cp in stores(k - 2):
                cp.wait()
        compute(k)
        for cp in stores(k):
            cp.start()
        if k + STREAM_BUFS < n:
            for cp in loads(k + STREAM_BUFS):
                cp.start()
    for k in range(max(n - 2, 0), n):
        for cp in stores(k):
            cp.wait()


def _chip_sum(place, g, got, name):
    _, r, c = g.shape
    tm = _stream_tile(r, 4)
    nt = r // tm

    def body(pos_ref, g_hbm, got_hbm, o_hbm, g_buf, s_buf, o_buf, sem_in, sem_out):
        def chip(j):
            return 2 * (pos_ref[0] ^ (0 if j == 1 else 1)) + (pos_ref[1] ^ (0 if j == 0 else 1))

        def loads(k):
            j, rows, slot = k // nt, pl.ds((k % nt) * tm, tm), k % STREAM_BUFS
            return [pltpu.make_async_copy(g_hbm.at[2 * chip(j) + pos_ref[2], rows], g_buf.at[slot], sem_in.at[slot, 0]),
                    pltpu.make_async_copy(got_hbm.at[chip(j), rows], s_buf.at[slot], sem_in.at[slot, 1])]

        def stores(k):
            return [pltpu.make_async_copy(o_buf.at[k % 2], o_hbm.at[k // nt, pl.ds((k % nt) * tm, tm)],
                                          sem_out.at[k % 2])]

        def compute(k):
            slot = k % STREAM_BUFS
            o_buf[k % 2] = (g_buf[slot].astype(F32) + s_buf[slot].astype(F32)).astype(BF16)

        _stream(3 * nt, loads, stores, compute)

    hbm = pl.BlockSpec(memory_space=pl.ANY)
    return pl.pallas_call(
        body, name=name,
        grid_spec=pltpu.PrefetchScalarGridSpec(
            num_scalar_prefetch=1, grid=(1,), in_specs=[hbm, hbm], out_specs=hbm,
            scratch_shapes=[pltpu.VMEM((STREAM_BUFS, tm, c), BF16), pltpu.VMEM((STREAM_BUFS, tm, c), BF16),
                            pltpu.VMEM((2, tm, c), BF16),
                            pltpu.SemaphoreType.DMA((STREAM_BUFS, 2)), pltpu.SemaphoreType.DMA((2,))]),
        out_shape=jax.ShapeDtypeStruct((3, r, c), BF16),
        compiler_params=_cp(("arbitrary",)),
    )(place, g, got)


def _exchange_chips(sums, name, collective_id):
    na = len(sums)

    def body(*refs):
        ins, outs = refs[:na], refs[na:2 * na]
        send_sems, recv_sems = refs[2 * na:]
        x, y, c, chips = _place()
        _handshake([(*chip, c) for chip in chips])
        cps = []
        for a in range(na):
            for j, chip in enumerate(chips):
                cps.append(pltpu.make_async_remote_copy(
                    src_ref=ins[a].at[j], dst_ref=outs[a].at[j],
                    send_sem=send_sems.at[a, j], recv_sem=recv_sems.at[a, j],
                    device_id=(*chip, c), device_id_type=MESH))
        for cp in cps:
            cp.start()
        for cp in cps:
            cp.wait()

    return _sequencer_call(
        body, name, collective_id,
        [jax.ShapeDtypeStruct((3,) + s.shape[1:], s.dtype) for s in sums],
        [pltpu.SemaphoreType.DMA((na, 3)), pltpu.SemaphoreType.DMA((na, 3))])(*sums)


def _exchange_stats(stats, collective_id):
    def body(st_in, st_out, st_send, st_recv, local_sem):
        x, y, c, _ = _place()
        me_idx = 4 * x + 2 * y + c
        peers = [(x ^ ((k >> 2) & 1), y ^ ((k >> 1) & 1), c ^ (k & 1)) for k in range(1, 8)]
        _handshake(peers)
        mine = pltpu.make_async_copy(st_in, st_out.at[me_idx], local_sem)
        mine.start()
        cps = [pltpu.make_async_remote_copy(
            src_ref=st_in, dst_ref=st_out.at[me_idx], send_sem=st_send.at[k], recv_sem=st_recv.at[k],
            device_id=peer, device_id_type=MESH) for k, peer in enumerate(peers)]
        for cp in cps:
            cp.start()
        for cp in cps:
            cp.wait()
        mine.wait()

    return _sequencer_call(
        body, "exchange_stats", collective_id,
        jax.ShapeDtypeStruct((NDEV,) + stats.shape, stats.dtype),
        [pltpu.SemaphoreType.DMA((7,)), pltpu.SemaphoreType.DMA((7,)), pltpu.SemaphoreType.DMA])(stats)


class _Reduction:
    def __init__(self, place, first_collective_id, state):
        self.place = place
        self.ids = iter(range(first_collective_id, 32))
        self.state = state
        self.groups = {}
        self.updates = {}

    def next_id(self):
        return next(self.ids)

    def start(self, group, grads):
        got = _exchange_sibling(grads, "sibling_exchange_" + group[0], self.next_id())
        self.groups[group[0]] = dict(names=group, grads=grads, got=got)

    def local(self, name, first=()):
        grp = self.groups[name]
        grads = lax.optimization_barrier((tuple(grp["grads"]), tuple(first)))[0]
        grp["sums"] = [_chip_sum(self.place, g, s, "chip_sum_" + n)
                       for g, s, n in zip(grads, grp["got"], grp["names"])]
        grp["chips"] = _exchange_chips(grp["sums"], "chip_exchange_" + name, self.next_id())
        return grp["sums"]

    def landed(self, name):
        return list(self.groups[name]["chips"])

    def rider(self, name):
        grp = next(g for g in self.groups.values() if name in g["names"])
        k = grp["names"].index(name)
        return self.state[name][:3] + (grp["grads"][k], grp["got"][k], grp["chips"][k])

    def set_update(self, name, outs):
        self.updates[name] = list(outs)

    def update(self, name):
        if name not in self.updates:
            grp = next(g for g in self.groups.values() if name in g["names"])
            k = grp["names"].index(name)
            w, m, v, part, parts = self.state[name]
            before = self.update(f"{name[:-1]}{part - 1}") if part else None
            self.updates[name] = _shard_update(self.place, w, m, v, grp["grads"][k], grp["got"][k],
                                               grp["chips"][k], "update_" + name, part, parts, before)
        return list(self.updates[name])


def _adamw(w, g, m, v):
    m = ADAM_B1 * m + (1.0 - ADAM_B1) * g
    v = ADAM_B2 * v + (1.0 - ADAM_B2) * (g * g)
    m_hat = m / (1.0 - ADAM_B1 ** ADAM_STEP)
    v_hat = v / (1.0 - ADAM_B2 ** ADAM_STEP)
    delta = -ADAM_LR * (m_hat / (jnp.sqrt(v_hat) + ADAM_EPS) + ADAM_WD * w)
    return delta, m, v


def _update_tile(w_ref, m_ref, v_ref, g_ref, s_ref, c_ref, go_ref, d_ref, mo_ref, vo_ref):
    grad = g_ref[...].astype(F32) + s_ref[...].astype(F32)
    for j in range(3):
        grad = grad + c_ref[j].astype(F32)
    delta, mn, vn = _adamw(w_ref[...], grad, m_ref[...], v_ref[...])
    go_ref[...] = grad
    d_ref[...] = delta
    mo_ref[...] = mn
    vo_ref[...] = vn


def _shard_update(place, w, m, v, g, got_sib, got_chips, name, part=0, parts=1, before=None):
    r, c = w.shape
    rp = r // parts
    tm = _stream_tile(rp, 8)
    nt = rp // tm
    before = list(before or [])

    def body(pos_ref, w_hbm, m_hbm, v_hbm, g_hbm, s_hbm, c_hbm, *rest):
        outs = rest[len(before):len(before) + 4]
        w_buf, m_buf, v_buf, g_buf, s_buf, c_buf, o_buf, sem_in, sem_out = rest[len(before) + 4:]
        own = 4 * pos_ref[0] + 2 * pos_ref[1] + pos_ref[2]
        chip = 2 * pos_ref[0] + pos_ref[1]

        def loads(k):
            slot, rows, mine = k % STREAM_BUFS, pl.ds(k * tm, tm), pl.ds(part * rp + k * tm, tm)
            pairs = [(w_hbm.at[mine], w_buf), (m_hbm.at[mine], m_buf), (v_hbm.at[mine], v_buf),
                     (g_hbm.at[own, rows], g_buf), (s_hbm.at[chip, rows], s_buf), (c_hbm.at[:, rows], c_buf)]
            return [pltpu.make_async_copy(src, buf.at[slot], sem_in.at[slot, n]) for n, (src, buf) in enumerate(pairs)]

        def stores(k):
            mine = pl.ds(part * rp + k * tm, tm)
            return [pltpu.make_async_copy(o_buf.at[k % 2, n], out.at[mine], sem_out.at[k % 2, n])
                    for n, out in enumerate(outs)]

        def compute(k):
            slot = k % STREAM_BUFS
            _update_tile(w_buf.at[slot], m_buf.at[slot], v_buf.at[slot], g_buf.at[slot], s_buf.at[slot],
                         c_buf.at[slot], *[o_buf.at[k % 2, n] for n in range(4)])

        _stream(nt, loads, stores, compute)

    hbm = pl.BlockSpec(memory_space=pl.ANY)
    return pl.pallas_call(
        body, name=name,
        grid_spec=pltpu.PrefetchScalarGridSpec(
            num_scalar_prefetch=1, grid=(1,), in_specs=[hbm] * (6 + len(before)), out_specs=[hbm] * 4,
            scratch_shapes=[pltpu.VMEM((STREAM_BUFS, tm, c), F32)] * 3 + [pltpu.VMEM((STREAM_BUFS, tm, c), BF16)] * 2
            + [pltpu.VMEM((STREAM_BUFS, 3, tm, c), BF16), pltpu.VMEM((2, 4, tm, c), F32),
               pltpu.SemaphoreType.DMA((STREAM_BUFS, 6)), pltpu.SemaphoreType.DMA((2, 4))]),
        out_shape=[jax.ShapeDtypeStruct((r, c), F32)] * 4,
        input_output_aliases={7 + k: k for k in range(len(before))},
        compiler_params=_cp(("arbitrary",)),
    )(place, w, m, v, g, got_sib, got_chips, *before)


def _small_update(stats_all, ws, ms, vs):
    def body(st_ref, w_ref, m_ref, v_ref, go_ref, d_ref, mo_ref, vo_ref):
        grad = st_ref[0]
        for k in range(1, NDEV):
            grad = grad + st_ref[k]
        delta, mn, vn = _adamw(w_ref[...], grad, m_ref[...], v_ref[...])
        go_ref[...] = grad
        d_ref[...] = delta
        mo_ref[...] = mn
        vo_ref[...] = vn

    return pl.pallas_call(
        body, name="small_update",
        out_shape=[jax.ShapeDtypeStruct((8, D), F32)] * 4,
        compiler_params=_cp(),
    )(stats_all, ws, ms, vs)


def kernel(x, norm_mix_w, w_in, w_out, norm_ffn_w, w_gate, w_up, w_down, norm_final_w, loss_target, m_norm_mix_w, m_w_in, m_w_out, m_norm_ffn_w, m_w_gate, m_w_up, m_w_down, m_norm_final_w, v_norm_mix_w, v_w_in, v_w_out, v_norm_ffn_w, v_w_gate, v_w_up, v_w_down, v_norm_final_w):
    tr = {"w_gate", "w_up"}
    names = ["w_in", "w_out", "w_gate", "w_up", "w_down"]

    def view(a, n):
        return a[0].T if n in tr else a[0]

    big_w = [view(a, n) for a, n in zip([w_in, w_out, w_gate, w_up, w_down], names)]
    big_m = [view(a, n) for a, n in zip([m_w_in, m_w_out, m_w_gate, m_w_up, m_w_down], names)]
    big_v = [view(a, n) for a, n in zip([v_w_in, v_w_out, v_w_gate, v_w_up, v_w_down], names)]

    shards = [_cast_bf16(w, "cast_" + n) for w, n in zip(big_w, names)]
    (win,) = _all_gather(shards[0:1], "all_gather_w_in", 1)
    (wout,) = _all_gather(shards[1:2], "all_gather_w_out", 2)
    (wgu_a,) = _all_gather(shards[2:4], "all_gather_gate_up_0", 3, per=FF_PER, rows=(0, FF_ROWS))
    (wgu,) = _all_gather(shards[2:4], "all_gather_gate_up_1", 4, per=FF_PER, rows=(FF_ROWS, FF_ROWS), carry=wgu_a)
    (wd_a,) = _all_gather(shards[4:5], "all_gather_w_down_0", 5, per=FF_PER, rows=(0, FF_ROWS))
    (wd,) = _all_gather(shards[4:5], "all_gather_w_down_1", 6, per=FF_PER, rows=(FF_ROWS, FF_ROWS), carry=wd_a)
    nw3 = norm_final_w.reshape(1, D)
    place = jnp.stack([lax.axis_index("x"), lax.axis_index("y"), lax.axis_index("c")]).astype(jnp.int32)
    state = {n: (w, m, v, 0, 1) for n, w, m, v in zip(names, big_w, big_m, big_v)}
    for part in range(W_IN_PARTS):
        state[f"w_in_{part}"] = state["w_in"][:3] + (part, W_IN_PARTS)
    red = _Reduction(place, 7, state)
    stats, gx, *_ = _local_step(
        x[0], loss_target[0], norm_mix_w, norm_ffn_w, nw3, win, wout.reshape(D, D),
        wgu_a.reshape(NFG // 2, 2 * N_FG, D), wgu.reshape(NFG, 2 * N_FG, D),
        wd_a.reshape(NFG // 2, N_FG, D), wd.reshape(NFG, N_FG, D), red)
    stats_all = _exchange_stats(stats, red.next_id())
    upd = [red.update(f"w_in_{W_IN_PARTS - 1}" if n == "w_in" else n) for n in names]
    stats_all = lax.optimization_barrier((stats_all, tuple(upd[0])))[0]

    def rows(a, b, c):
        return jnp.concatenate([a.reshape(1, D), b.reshape(1, D), c.reshape(1, D), jnp.zeros((5, D), F32)], axis=0)

    sg, sd, sm, sv = _small_update(stats_all, rows(norm_mix_w, norm_ffn_w, norm_final_w),
                                   rows(m_norm_mix_w, m_norm_ffn_w, m_norm_final_w),
                                   rows(v_norm_mix_w, v_norm_ffn_w, v_norm_final_w))
    loss = sg[3, 0]

    def outs(k, small):
        big = [(u[k].T if n in tr else u[k])[None] for u, n in zip(upd, names)]
        return [small[0:1], big[0], big[1], small[1:2], big[2], big[3], big[4], small[2]]

    return (loss, gx[None], *outs(0, sg), *outs(1, sd), *outs(2, sm), *outs(3, sv))
```

```python
import math

import numpy as np
import jax
import jax.numpy as jnp
from jax import lax
from jax.experimental import pallas as pl
from jax.experimental.pallas import tpu as pltpu
from jax.experimental.pallas import tpu_sc as plsc

F32 = jnp.float32
BF16 = jnp.bfloat16

S = 2048
D = 2048
NDEV = 8
N_IN = 7168 // NDEV
N_FF = 5632 // NDEV
NFG, N_FG = NDEV // 2, 2 * N_FF
FF_PER, FF_ROWS = 4, N_FF // 2
N_OUT = 2048 // NDEV
AH, AHD = 8, 128
RH, RHD = 4, 256
CH = 128
NB = S // CH
EPS = 1e-6
PATTERNS = ((1, 16), (4, 4), (16, 1))
NEG = -1e30
VMEM_LIMIT = 56 * 1024 * 1024

ADAM_LR, ADAM_B1, ADAM_B2, ADAM_EPS, ADAM_WD, ADAM_STEP = 0.001, 0.9, 0.999, 1e-08, 0.01, 10
MESH = pl.DeviceIdType.MESH


def _cp(sem=None):
    return pltpu.CompilerParams(dimension_semantics=sem, vmem_limit_bytes=VMEM_LIMIT)


def _dot(a, b):
    return jnp.dot(a, b, preferred_element_type=F32)


def _dot_nt(a, b):
    return lax.dot_general(a, b, (((1,), (1,)), ((), ())), preferred_element_type=F32)


def _dot_tn(a, b):
    return lax.dot_general(a, b, (((0,), (0,)), ((), ())), preferred_element_type=F32)


def _sigmoid(x):
    return 0.5 * jnp.tanh(0.5 * x) + 0.5


def _cast_bf16(w, name):
    r, c = w.shape
    tm = r if r <= 1024 else 512

    def body(w_ref, o_ref):
        o_ref[...] = w_ref[...].astype(BF16)

    return pl.pallas_call(
        body, name=name, grid=(r // tm,),
        in_specs=[pl.BlockSpec((tm, c), lambda i: (i, 0))],
        out_specs=pl.BlockSpec((tm, c), lambda i: (i, 0)),
        out_shape=jax.ShapeDtypeStruct((r, c), BF16),
        compiler_params=_cp(("parallel",)),
    )(w)


def _rms_fwd(x, nw):
    tm = 256

    def body(x_ref, w_ref, h_ref, r_ref):
        xs = x_ref[...]
        r = lax.rsqrt(jnp.mean(xs * xs, axis=-1, keepdims=True) + EPS)
        h_ref[...] = ((xs * r) * w_ref[...]).astype(BF16)
        r_ref[...] = r

    return pl.pallas_call(
        body, name="rms_fwd", grid=(S // tm,),
        in_specs=[pl.BlockSpec((tm, D), lambda i: (i, 0)), pl.BlockSpec((1, D), lambda i: (0, 0))],
        out_specs=[pl.BlockSpec((tm, D), lambda i: (i, 0)), pl.BlockSpec((tm, 1), lambda i: (i, 0))],
        out_shape=[jax.ShapeDtypeStruct((S, D), BF16), jax.ShapeDtypeStruct((S, 1), F32)],
        compiler_params=_cp(("parallel",)),
    )(x, nw)


def _row_copies(hbm_refs, bufs, sems, m, tm):
    rows = pl.ds(pl.multiple_of(m * tm, tm), tm)
    return [pltpu.make_async_copy(h.at[rows], b, sems.at[i]) for i, (h, b) in enumerate(zip(hbm_refs, bufs))]


def _rms_bwd_tile(dh, xs, r, nw):
    dnw = jnp.sum(dh * (xs * r), axis=0, keepdims=True)
    gy = dh * nw
    dx = r * gy - xs * ((r * r * r) * jnp.mean(gy * xs, axis=-1, keepdims=True))
    return dx, dnw


def _proj(h1, win):
    tm = 1024

    def body(a_ref, w_ref, o_ref):
        o_ref[...] = _dot(a_ref[...], w_ref[...])

    return pl.pallas_call(
        body, name="proj", grid=(NDEV, S // tm),
        in_specs=[pl.BlockSpec((tm, D), lambda p, m: (m, 0)),
                  pl.BlockSpec((None, D, N_IN), lambda p, m: (p, 0, 0))],
        out_specs=pl.BlockSpec((tm, N_IN), lambda p, m: (m, p)),
        out_shape=jax.ShapeDtypeStruct((S, NDEV * N_IN), F32),
        compiler_params=_cp(("parallel", "parallel")),
    )(h1, win)


def _out_proj_rms(x, ma, mr, wout, nw):
    tm = 256
    half = D // 2

    def body(x_ref, ma_ref, mr_ref, w_ref, nw_ref, x2_ref, h_ref, r_ref):
        acc = _dot(ma_ref[...], w_ref[0:half, :]) + _dot(mr_ref[...], w_ref[half:D, :])
        x2 = x_ref[...] + acc
        r = lax.rsqrt(jnp.mean(x2 * x2, axis=-1, keepdims=True) + EPS)
        x2_ref[...] = x2
        h_ref[...] = ((x2 * r) * nw_ref[...]).astype(BF16)
        r_ref[...] = r

    return pl.pallas_call(
        body, name="out_proj_rms", grid=(S // tm,),
        in_specs=[pl.BlockSpec((tm, D), lambda i: (i, 0)),
                  pl.BlockSpec((tm, half), lambda i: (i, 0)),
                  pl.BlockSpec((tm, half), lambda i: (i, 0)),
                  pl.BlockSpec((D, D), lambda i: (0, 0)),
                  pl.BlockSpec((1, D), lambda i: (0, 0))],
        out_specs=[pl.BlockSpec((tm, D), lambda i: (i, 0)), pl.BlockSpec((tm, D), lambda i: (i, 0)),
                   pl.BlockSpec((tm, 1), lambda i: (i, 0))],
        out_shape=[jax.ShapeDtypeStruct((S, D), F32), jax.ShapeDtypeStruct((S, D), BF16),
                   jax.ShapeDtypeStruct((S, 1), F32)],
        compiler_params=_cp(("parallel",)),
    )(x, ma, mr, wout, nw)


def _ffn_up(h2, wgu, part, before=None):
    tm = 512

    def body(h_ref, w_ref, *rest):
        a_ref, dadg_ref, dadu_ref = rest[-3:]
        gu = _dot_nt(h_ref[...], w_ref[...])
        g, u = gu[:, 0:N_FG], gu[:, N_FG:2 * N_FG]
        sg = _sigmoid(g)
        silu = g * sg
        a_ref[...] = (silu * u).astype(BF16)
        dadg_ref[...] = (u * (sg * (1.0 + g * (1.0 - sg)))).astype(BF16)
        dadu_ref[...] = silu.astype(BF16)

    half = NFG // 2
    first = part * half
    before = list(before or [])
    blk = pl.BlockSpec((None, tm, N_FG), lambda p, m: (p + first, m, 0))
    return pl.pallas_call(
        body, name=f"ffn_up_{part}", grid=(half, S // tm),
        in_specs=[pl.BlockSpec((tm, D), lambda p, m: (m, 0)),
                  pl.BlockSpec((None, 2 * N_FG, D), lambda p, m: (p, 0, 0))]
        + [pl.BlockSpec(memory_space=pl.ANY)] * len(before),
        out_specs=[blk, blk, blk],
        out_shape=[jax.ShapeDtypeStruct((NFG, S, N_FG), BF16)] * 3,
        input_output_aliases={2 + k: k for k in range(len(before))},
        compiler_params=_cp(("parallel", "parallel")),
    )(h2, wgu, *before)


def _ffn_down_first(x2, a, wd):
    tm = 512
    n = wd.shape[0]

    def body(x_ref, a_ref, w_ref, o_ref):
        p = pl.program_id(1)

        @pl.when(p == 0)
        def _():
            o_ref[...] = x_ref[...] + _dot(a_ref[...], w_ref[...])

        @pl.when(p > 0)
        def _():
            o_ref[...] += _dot(a_ref[...], w_ref[...])

    return pl.pallas_call(
        body, name="ffn_down_first", grid=(S // tm, n),
        in_specs=[pl.BlockSpec((tm, D), lambda m, p: (m, 0)),
                  pl.BlockSpec((None, tm, N_FG), lambda m, p: (p, m, 0)),
                  pl.BlockSpec((None, N_FG, D), lambda m, p: (p, 0, 0))],
        out_specs=pl.BlockSpec((tm, D), lambda m, p: (m, 0)),
        out_shape=jax.ShapeDtypeStruct((S, D), F32),
        compiler_params=_cp(("parallel", "arbitrary")),
    )(x2, a, wd)


def _ffn_down_loss(x2, a, wd, nw, tgt):
    tm = 512
    first = NFG - wd.shape[0]

    def body(x2_hbm, a_ref, w_ref, nw_ref, t_hbm, dx_ref, dxb_ref, st_ref, acc_ref, x2_buf, t_buf, sems):
        m, p = pl.program_id(0), pl.program_id(1)
        tail_in = _row_copies((x2_hbm, t_hbm), (x2_buf, t_buf), sems, m, tm)

        @pl.when(p == 0)
        def _():
            acc_ref[...] = jnp.zeros_like(acc_ref)
            for cp in tail_in:
                cp.start()

        @pl.when((p == 0) & (m == 0))
        def _():
            st_ref[...] = jnp.zeros_like(st_ref)

        acc_ref[...] += _dot(a_ref[...], w_ref[...])

        @pl.when(p == NFG - first - 1)
        def _():
            for cp in tail_in:
                cp.wait()
            x3 = x2_buf[...] + acc_ref[...]
            nwv = nw_ref[...]
            r = lax.rsqrt(jnp.mean(x3 * x3, axis=-1, keepdims=True) + EPS)
            y = (x3 * r) * nwv
            err = y - t_buf[...]
            loss = 0.5 * jnp.sum(jnp.mean(err * err, axis=-1, keepdims=True), axis=0, keepdims=True)
            dy = err * (1.0 / D)
            dx, dnw = _rms_bwd_tile(dy, x3, r, nwv)
            dx_ref[...] = dx
            dxb_ref[...] = dx.astype(BF16)
            st_ref[0:1, :] += dnw
            st_ref[1:2, :] += jnp.broadcast_to(loss, (1, D))

    return pl.pallas_call(
        body, name="ffn_down_loss", grid=(S // tm, NFG - first),
        in_specs=[pl.BlockSpec(memory_space=pl.ANY),
                  pl.BlockSpec((None, tm, N_FG), lambda m, p: (p + first, m, 0)),
                  pl.BlockSpec((None, N_FG, D), lambda m, p: (p, 0, 0)),
                  pl.BlockSpec((1, D), lambda m, p: (0, 0)),
                  pl.BlockSpec(memory_space=pl.ANY)],
        out_specs=[pl.BlockSpec((tm, D), lambda m, p: (m, 0)), pl.BlockSpec((tm, D), lambda m, p: (m, 0)),
                   pl.BlockSpec((8, D), lambda m, p: (0, 0))],
        out_shape=[jax.ShapeDtypeStruct((S, D), F32), jax.ShapeDtypeStruct((S, D), BF16),
                   jax.ShapeDtypeStruct((8, D), F32)],
        scratch_shapes=[pltpu.VMEM((tm, D), F32), pltpu.VMEM((tm, D), F32), pltpu.VMEM((tm, D), F32),
                        pltpu.SemaphoreType.DMA((2,))],
        compiler_params=_cp(("arbitrary", "arbitrary")),
    )(x2, a, wd, nw, tgt)


def _ffn_down_bwd(dx3b, wd, dadg, dadu, part, before=None):
    tm = 1024
    half = NFG // 2

    def body(dx_ref, w_ref, dadg_ref, dadu_ref, *rest):
        dgu_ref = rest[-1]
        da = _dot_nt(dx_ref[...], w_ref[...])
        dgu_ref[:, 0:N_FG] = (da * dadg_ref[...].astype(F32)).astype(BF16)
        dgu_ref[:, N_FG:2 * N_FG] = (da * dadu_ref[...].astype(F32)).astype(BF16)

    blk = pl.BlockSpec((None, tm, N_FG), lambda p, m: (p + part * half, m, 0))
    before = list(before or [])
    return pl.pallas_call(
        body, name=f"ffn_down_bwd_{part}", grid=(half, S // tm),
        in_specs=[pl.BlockSpec((tm, D), lambda p, m: (m, 0)),
                  pl.BlockSpec((None, N_FG, D), lambda p, m: (p, 0, 0)), blk, blk]
        + [pl.BlockSpec(memory_space=pl.ANY)] * len(before),
        out_specs=pl.BlockSpec((None, tm, 2 * N_FG), lambda p, m: (p + part * half, m, 0)),
        out_shape=jax.ShapeDtypeStruct((NFG, S, 2 * N_FG), BF16),
        input_output_aliases={4 + k: k for k in range(len(before))},
        compiler_params=_cp(("parallel", "parallel")),
    )(dx3b, wd, dadg, dadu, *before)


def _ffn_up_bwd(dgu, wgu_a, wgu_b, dres, xs, r, nw):
    tm = 512
    nm = S // tm
    na = wgu_a.shape[0]

    def body(dgu_ref, wa_hbm, wb_hbm, dres_hbm, x_hbm, r_ref, nw_ref, dx_ref, dxb_ref, st_ref,
             w_buf, dres_buf, x_buf, sems, w_sems):
        m, p = pl.program_id(0), pl.program_id(1)
        tail_in = _row_copies((dres_hbm, x_hbm), (dres_buf, x_buf), sems, m, tm)

        def fetch(g, slot):
            for src, lo in ((wa_hbm, 0), (wb_hbm, na)):
                @pl.when((g >= lo) & (g < lo + na))
                def _():
                    pltpu.make_async_copy(src.at[g - lo], w_buf.at[slot], w_sems.at[slot]).start()

        @pl.when((p == 0) & (m == 0))
        def _():
            st_ref[...] = jnp.zeros_like(st_ref)
            fetch(p, 0)

        @pl.when((p < NFG - 1) | (m < nm - 1))
        def _():
            fetch((p + 1) % NFG, (p + 1) % 2)

        @pl.when(p == 0)
        def _():
            dx_ref[...] = jnp.zeros_like(dx_ref)
            for cp in tail_in:
                cp.start()

        slot = p % 2
        pltpu.make_async_copy(wa_hbm.at[0], w_buf.at[slot], w_sems.at[slot]).wait()
        dx_ref[...] += _dot(dgu_ref[...], w_buf[slot])

        @pl.when(p == NFG - 1)
        def _():
            for cp in tail_in:
                cp.wait()
            dx, dnw = _rms_bwd_tile(dx_ref[...], x_buf[...], r_ref[...], nw_ref[...])
            dx = dres_buf[...] + dx
            dx_ref[...] = dx
            dxb_ref[...] = dx.astype(BF16)
            st_ref[0:1, :] += dnw

    blk = pl.BlockSpec((None, tm, 2 * N_FG), lambda m, p: (p, m, 0))
    row = pl.BlockSpec((tm, D), lambda m, p: (m, 0))
    hbm = pl.BlockSpec(memory_space=pl.ANY)
    return pl.pallas_call(
        body, name="ffn_up_bwd", grid=(nm, NFG),
        in_specs=[blk, hbm, hbm, hbm, hbm, pl.BlockSpec((tm, 1), lambda m, p: (m, 0)),
                  pl.BlockSpec((1, D), lambda m, p: (0, 0))],
        out_specs=[row, row, pl.BlockSpec((8, D), lambda m, p: (0, 0))],
        out_shape=[jax.ShapeDtypeStruct((S, D), F32), jax.ShapeDtypeStruct((S, D), BF16),
                   jax.ShapeDtypeStruct((8, D), F32)],
        scratch_shapes=[pltpu.VMEM((2, 2 * N_FG, D), BF16), pltpu.VMEM((tm, D), F32), pltpu.VMEM((tm, D), F32),
                        pltpu.SemaphoreType.DMA((2,)), pltpu.SemaphoreType.DMA((2,))],
        compiler_params=_cp(("arbitrary", "arbitrary")),
    )(dgu, wgu_a, wgu_b, dres, xs, r, nw)


def _out_proj_bwd(dx2b, wout, place=None, rider=None):
    tm = 256

    if rider is None:
        def body(dx_ref, w_ref, o_ref):
            o_ref[...] = _dot_nt(dx_ref[...], w_ref[...])

        return pl.pallas_call(
            body, name="out_proj_bwd", grid=(S // tm,),
            in_specs=[pl.BlockSpec((tm, D), lambda i: (i, 0)), pl.BlockSpec((D, D), lambda i: (0, 0))],
            out_specs=pl.BlockSpec((tm, D), lambda i: (i, 0)),
            out_shape=jax.ShapeDtypeStruct((S, D), F32),
            compiler_params=_cp(("parallel",)),
        )(dx2b, wout), None

    w = rider[0]
    r, c = w.shape
    rt = _row_tile(r, c)
    nt = r // rt
    assert nt <= S // tm

    def body(pos_ref, dx_ref, w_ref, uw, um, uv, ug, us, uc, o_ref, go, dd, mo, vo):
        o_ref[...] = _dot_nt(dx_ref[...], w_ref[...])

        @pl.when(pl.program_id(0) < nt)
        def _():
            _update_tile(uw, um, uv, ug, us, uc, go, dd, mo, vo)

    def at(i):
        return jnp.minimum(i, nt - 1)

    tile = pl.BlockSpec((rt, c), lambda i, pos: (at(i), 0))
    outs = pl.pallas_call(
        body, name="out_proj_bwd",
        grid_spec=pltpu.PrefetchScalarGridSpec(
            num_scalar_prefetch=1, grid=(S // tm,),
            in_specs=[pl.BlockSpec((tm, D), lambda i, pos: (i, 0)), pl.BlockSpec((D, D), lambda i, pos: (0, 0)),
                      tile, tile, tile,
                      pl.BlockSpec((None, rt, c), lambda i, pos: (4 * pos[0] + 2 * pos[1] + pos[2], at(i), 0)),
                      pl.BlockSpec((None, rt, c), lambda i, pos: (2 * pos[0] + pos[1], at(i), 0)),
                      pl.BlockSpec((3, rt, c), lambda i, pos: (0, at(i), 0))],
            out_specs=[pl.BlockSpec((tm, D), lambda i, pos: (i, 0)), tile, tile, tile, tile]),
        out_shape=[jax.ShapeDtypeStruct((S, D), F32)] + [jax.ShapeDtypeStruct((r, c), F32)] * 4,
        compiler_params=_cp(("arbitrary",)),
    )(place, dx2b, wout, *rider)
    return outs[0], outs[1:]


def _in_proj_bwd(dproj, win, dres, xs, r, nw):
    tm = 1024

    def body(dp_ref, w_ref, dres_hbm, x_hbm, r_ref, nw_ref, dx_ref, st_ref, dres_buf, x_buf, sems):
        m, p = pl.program_id(0), pl.program_id(1)
        tail_in = _row_copies((dres_hbm, x_hbm), (dres_buf, x_buf), sems, m, tm)

        @pl.when(p == 0)
        def _():
            dx_ref[...] = jnp.zeros_like(dx_ref)
            for cp in tail_in:
                cp.start()

        @pl.when((p == 0) & (m == 0))
        def _():
            st_ref[...] = jnp.zeros_like(st_ref)

        dx_ref[...] += _dot_nt(dp_ref[...], w_ref[...])

        @pl.when(p == NDEV - 1)
        def _():
            for cp in tail_in:
                cp.wait()
            dx, dnw = _rms_bwd_tile(dx_ref[...], x_buf[...], r_ref[...], nw_ref[...])
            dx_ref[...] = dres_buf[...] + dx
            st_ref[0:1, :] += dnw

    row = pl.BlockSpec((tm, D), lambda m, p: (m, 0))
    hbm = pl.BlockSpec(memory_space=pl.ANY)
    return pl.pallas_call(
        body, name="in_proj_bwd", grid=(S // tm, NDEV),
        in_specs=[pl.BlockSpec((tm, N_IN), lambda m, p: (m, p)),
                  pl.BlockSpec((None, D, N_IN), lambda m, p: (p, 0, 0)),
                  hbm, hbm, pl.BlockSpec((tm, 1), lambda m, p: (m, 0)),
                  pl.BlockSpec((1, D), lambda m, p: (0, 0))],
        out_specs=[row, pl.BlockSpec((8, D), lambda m, p: (0, 0))],
        out_shape=[jax.ShapeDtypeStruct((S, D), F32), jax.ShapeDtypeStruct((8, D), F32)],
        scratch_shapes=[pltpu.VMEM((tm, D), F32), pltpu.VMEM((tm, D), F32), pltpu.SemaphoreType.DMA((2,))],
        compiler_params=_cp(("arbitrary", "arbitrary")),
    )(dproj, win, dres, xs, r, nw)


W_IN_PARTS = 2


def _wgrad_in(h1, dproj, part):
    rows = D // W_IN_PARTS

    def body(a_ref, d_ref, o_ref):
        both = _dot_tn(a_ref[...], d_ref[...]).astype(BF16)
        o_ref[0] = both[:, 0:N_IN]
        o_ref[1] = both[:, N_IN:2 * N_IN]

    return pl.pallas_call(
        body, name=f"wgrad_in_{part}", grid=(NDEV // 2,),
        in_specs=[pl.BlockSpec((S, rows), lambda p: (0, part)), pl.BlockSpec((S, 2 * N_IN), lambda p: (0, p))],
        out_specs=pl.BlockSpec((2, rows, N_IN), lambda p: (p, 0, 0)),
        out_shape=jax.ShapeDtypeStruct((NDEV, rows, N_IN), BF16),
        compiler_params=_cp(("parallel",)),
    )(h1, dproj)


def _wgrad_rows(a3, dy, name, col=0):
    def body(a_ref, d_ref, o_ref):
        dw = _dot_tn(a_ref[...], d_ref[...]).astype(BF16)
        for j in range(FF_PER):
            o_ref[j] = dw[j * FF_ROWS:(j + 1) * FF_ROWS]

    return pl.pallas_call(
        body, name=name, grid=(NFG,),
        in_specs=[pl.BlockSpec((None, S, N_FG), lambda p: (p, 0, col)), pl.BlockSpec((S, D), lambda p: (0, 0))],
        out_specs=pl.BlockSpec((FF_PER, FF_ROWS, D), lambda p: (p % 2, p // 2, 0)),
        out_shape=jax.ShapeDtypeStruct((NDEV, N_FF, D), BF16),
        compiler_params=_cp(("parallel",)),
    )(a3, dy)


def _wgrad_out(ma, mr, dx2b):
    half = D // 2
    per = half // N_OUT

    def body(ma_ref, mr_ref, d_ref, o_ref):
        p = pl.program_id(0)

        @pl.when(p == 0)
        def _():
            o_ref[...] = _dot_tn(ma_ref[...], d_ref[...]).astype(BF16).reshape(per, N_OUT, D)

        @pl.when(p == 1)
        def _():
            o_ref[...] = _dot_tn(mr_ref[...], d_ref[...]).astype(BF16).reshape(per, N_OUT, D)

    whole = pl.BlockSpec((S, half), lambda p: (0, 0))
    return pl.pallas_call(
        body, name="wgrad_out", grid=(2,),
        in_specs=[whole, whole, pl.BlockSpec((S, D), lambda p: (0, 0))],
        out_specs=pl.BlockSpec((per, N_OUT, D), lambda p: (p, 0, 0)),
        out_shape=jax.ShapeDtypeStruct((NDEV, N_OUT, D), BF16),
        compiler_params=_cp(("parallel",)),
    )(ma, mr, dx2b)


def _attn_consts():
    c = np.zeros((AH, 8, AHD), np.float32)
    for h in range(AH):
        c[h, :, :] = 2.0 ** (-(h + 1))
    return jnp.asarray(c)


def _permute_in(dst, src, d, cast=None):
    v = src[...]
    if d > 1:
        v = pltpu.einshape("jrc->rjc", v.reshape(S // d, d, AHD)).reshape(S, AHD)
    dst[...] = v if cast is None else v.astype(cast)


def _natural_order(v, d):
    if d == 1:
        return v
    return pltpu.einshape("rjc->jrc", v.reshape(d, S // d, AHD)).reshape(S, AHD)


def _attn_masks():
    qi = lax.broadcasted_iota(jnp.int32, (CH, CH), 0)
    kj = lax.broadcasted_iota(jnp.int32, (CH, CH), 1)
    dist_c = (qi - kj).astype(F32)
    dist_p = (qi - kj + CH).astype(F32)
    return (qi >= kj)[None], (kj >= qi)[None], dist_c[None], dist_p[None]


GB = 16


def _bdot_nt(a, b):
    return lax.dot_general(a, b, (((2,), (2,)), ((0,), (0,))), preferred_element_type=F32)


def _bdot(a, b):
    return lax.dot_general(a, b, (((2,), (1,)), ((0,), (0,))), preferred_element_type=F32)


def _bdot_tn(a, b):
    return lax.dot_general(a, b, (((1,), (1,)), ((0,), (0,))), preferred_element_type=F32)


def _shift_block(dst, src):
    dst[0:CH, :] = jnp.zeros((CH, AHD), dst.dtype)
    dst[CH:S, :] = src[0:S - CH, :]


def _has_prev(g, nb):
    blk = lax.broadcasted_iota(jnp.int32, (GB, 1, 1), 0) + g * GB
    return (blk & (nb - 1)) != 0


def _blocks(ref, g):
    return ref[g * GB * CH:(g + 1) * GB * CH, :].reshape(GB, CH, AHD)


def _attn_fwd(proj):
    scale = 1.0 / math.sqrt(AHD)

    def body(c_ref, q_ref, k_ref, v_ref, o_ref, ob_ref, lse_ref, qkvp_ref, lsep_ref, qd, kd, vd, kps, vps, od, ld, *nat):
        onat, lnat = nat[0:3], nat[3:6]
        slope = c_ref[0:1, :]
        mask_c, mask_p, dist_c, dist_p = _attn_masks()
        for pi, (d, nb) in enumerate(PATTERNS):
            _permute_in(qd, q_ref, d, BF16)
            _permute_in(kd, k_ref, d, BF16)
            _permute_in(vd, v_ref, d, BF16)
            if d > 1:
                qkvp_ref[pi - 1, 0] = qd[...]
                qkvp_ref[pi - 1, 1] = kd[...]
                qkvp_ref[pi - 1, 2] = vd[...]
            if nb > 1:
                _shift_block(kps, kd)
                _shift_block(vps, vd)
            bias_c = -(slope * float(d)) * dist_c
            bias_p = -(slope * float(d)) * dist_p
            for g in range(NB // GB):
                q3, k3, v3 = _blocks(qd, g), _blocks(kd, g), _blocks(vd, g)
                s_c = jnp.where(mask_c, _bdot_nt(q3, k3) * scale + bias_c, NEG)
                mx = jnp.max(s_c, axis=-1, keepdims=True)
                if nb > 1:
                    kp3, vp3 = _blocks(kps, g), _blocks(vps, g)
                    s_p = jnp.where(jnp.logical_and(mask_p, _has_prev(g, nb)),
                                    _bdot_nt(q3, kp3) * scale + bias_p, NEG)
                    mx = jnp.maximum(mx, jnp.max(s_p, axis=-1, keepdims=True))
                    l = (jnp.sum(jnp.exp(s_c - mx), axis=-1, keepdims=True)
                         + jnp.sum(jnp.exp(s_p - mx), axis=-1, keepdims=True))
                    lse = mx + jnp.log(l)
                    o3 = _bdot(jnp.exp(s_c - lse).astype(BF16), v3) + _bdot(jnp.exp(s_p - lse).astype(BF16), vp3)
                else:
                    l = jnp.sum(jnp.exp(s_c - mx), axis=-1, keepdims=True)
                    lse = mx + jnp.log(l)
                    o3 = _bdot(jnp.exp(s_c - lse).astype(BF16), v3)
                rows = slice(g * GB * CH, (g + 1) * GB * CH)
                od[rows, :] = o3.reshape(GB * CH, AHD)
                ld[rows, :] = jnp.broadcast_to(lse, (GB, CH, AHD)).reshape(GB * CH, AHD)
            onat[pi][...] = _natural_order(od[...], d)
            lnat[pi][...] = _natural_order(ld[...], d)
        l0, l1, l2 = lnat[0][...], lnat[1][...], lnat[2][...]
        mx = jnp.maximum(jnp.maximum(l0, l1), l2)
        e0, e1, e2 = jnp.exp(l0 - mx), jnp.exp(l1 - mx), jnp.exp(l2 - mx)
        den = e0 + e1 + e2
        out = (e0 / den) * onat[0][...] + (e1 / den) * onat[1][...] + (e2 / den) * onat[2][...]
        o_ref[...] = out
        ob_ref[...] = out.astype(BF16)
        lse_ref[...] = mx + jnp.log(den)
        for pi, (d, _) in enumerate(PATTERNS[1:]):
            _permute_in(lsep_ref.at[pi], lse_ref, d)

    def col(off):
        return pl.BlockSpec((S, AHD), lambda h: (0, off + h))

    return pl.pallas_call(
        body, name="attn_fwd", grid=(AH,),
        in_specs=[pl.BlockSpec((None, 8, AHD), lambda h: (h, 0, 0)), col(0), col(AH), col(2 * AH)],
        out_specs=[col(0), col(0), col(0), pl.BlockSpec((2, 3, S, AHD), lambda h: (0, 0, 0, h)),
                   pl.BlockSpec((2, S, AHD), lambda h: (0, 0, h))],
        out_shape=[jax.ShapeDtypeStruct((S, AH * AHD), F32), jax.ShapeDtypeStruct((S, AH * AHD), BF16),
                   jax.ShapeDtypeStruct((S, AH * AHD), F32),
                   jax.ShapeDtypeStruct((2, 3, S, AH * AHD), BF16), jax.ShapeDtypeStruct((2, S, AH * AHD), F32)],
        scratch_shapes=[pltpu.VMEM((S, AHD), BF16) for _ in range(5)]
        + [pltpu.VMEM((S, AHD), F32) for _ in range(8)],
        compiler_params=_cp(("parallel",)),
    )(_attn_consts(), proj, proj, proj)


def _attn_bwd(proj, dmixed, o, lse, qkvp, lsep):
    scale = 1.0 / math.sqrt(AHD)

    def body(c_ref, q_ref, k_ref, v_ref, do_ref, o_ref, lse_ref, qkvp_ref, lsep_ref, dproj_hbm,
             qd, kd, vd, dod, kps, vps, dld, dqd, dkd, dvd, delta, aq, ak, av, sq, sk, sv, sems):
        h = pl.program_id(0)

        def out_copies(head):
            return [pltpu.make_async_copy(
                st, dproj_hbm.at[:, pl.ds(pl.multiple_of((k * AH + head) * AHD, AHD), AHD)], sems.at[k])
                for k, st in enumerate((sq, sk, sv))]

        slope = c_ref[0:1, :]
        mask_c, mask_p, dist_c, dist_p = _attn_masks()
        delta[...] = jnp.broadcast_to(jnp.sum(do_ref[...] * o_ref[...], axis=-1, keepdims=True), (S, AHD))
        for pi, (d, nb) in enumerate(PATTERNS):
            if d == 1:
                _permute_in(qd, q_ref, d, BF16)
                _permute_in(kd, k_ref, d, BF16)
                _permute_in(vd, v_ref, d, BF16)
                qs, ks, vs, lss = qd, kd, vd, lse_ref
            else:
                qs, ks, vs, lss = (qkvp_ref.at[pi - 1, 0], qkvp_ref.at[pi - 1, 1], qkvp_ref.at[pi - 1, 2],
                                   lsep_ref.at[pi - 1])
            _permute_in(dod, do_ref, d, BF16)
            _permute_in(dld, delta, d)
            if nb > 1:
                _shift_block(kps, ks)
                _shift_block(vps, vs)
            bias_c = -(slope * float(d)) * dist_c
            bias_p = -(slope * float(d)) * dist_p
            for g in range(NB // GB):
                q3, k3, v3, do3 = _blocks(qs, g), _blocks(ks, g), _blocks(vs, g), _blocks(dod, g)
                ls, dl = _blocks(lss, g), _blocks(dld, g)
                lo, hi = g * GB * CH, (g + 1) * GB * CH
                p_c = jnp.exp(jnp.where(mask_c, _bdot_nt(q3, k3) * scale + bias_c, NEG) - ls)
                ds_c = ((p_c * (_bdot_nt(do3, v3) - dl)) * scale).astype(BF16)
                dq3 = _bdot(ds_c, k3)
                dkd[lo:hi, :] = _bdot_tn(ds_c, q3).reshape(GB * CH, AHD)
                dvd[lo:hi, :] = _bdot_tn(p_c.astype(BF16), do3).reshape(GB * CH, AHD)
                if nb > 1:
                    kp3, vp3 = _blocks(kps, g), _blocks(vps, g)
                    p_p = jnp.exp(jnp.where(jnp.logical_and(mask_p, _has_prev(g, nb)),
                                            _bdot_nt(q3, kp3) * scale + bias_p, NEG) - ls)
                    ds_p = ((p_p * (_bdot_nt(do3, vp3) - dl)) * scale).astype(BF16)
                    dq3 = dq3 + _bdot(ds_p, kp3)
                    dkp = _bdot_tn(ds_p, q3).reshape(GB * CH, AHD)
                    dvp = _bdot_tn(p_p.astype(BF16), do3).reshape(GB * CH, AHD)
                    if g == 0:
                        dkd[0:hi - CH, :] += dkp[CH:, :]
                        dvd[0:hi - CH, :] += dvp[CH:, :]
                    else:
                        dkd[lo - CH:hi - CH, :] += dkp
                        dvd[lo - CH:hi - CH, :] += dvp
                dqd[lo:hi, :] = dq3.reshape(GB * CH, AHD)
            ln = S // d
            for acc, src in ((aq, dqd), (ak, dkd), (av, dvd)):
                if pi == 0:
                    acc[...] = src[...]
                else:
                    acc[...] += _natural_order(src[...], d)

        @pl.when(h > 0)
        def _():
            for cp in out_copies(h - 1):
                cp.wait()

        sq[...] = aq[...].astype(BF16)
        sk[...] = ak[...].astype(BF16)
        sv[...] = av[...].astype(BF16)
        for cp in out_copies(h):
            cp.start()

        @pl.when(h == AH - 1)
        def _():
            for cp in out_copies(h):
                cp.wait()

    def col(off):
        return pl.BlockSpec((S, AHD), lambda h: (0, off + h))

    return pl.pallas_call(
        body, name="attn_bwd", grid=(AH,),
        in_specs=[pl.BlockSpec((None, 8, AHD), lambda h: (h, 0, 0)), col(0), col(AH), col(2 * AH),
                  col(0), col(0), col(0), pl.BlockSpec((2, 3, S, AHD), lambda h: (0, 0, 0, h)),
                  pl.BlockSpec((2, S, AHD), lambda h: (0, 0, h))],
        out_specs=pl.BlockSpec(memory_space=pl.ANY),
        out_shape=jax.ShapeDtypeStruct((S, NDEV * N_IN), BF16),
        scratch_shapes=[pltpu.VMEM((S, AHD), BF16) for _ in range(6)]
        + [pltpu.VMEM((S, AHD), F32) for _ in range(8)]
        + [pltpu.VMEM((S, AHD), BF16) for _ in range(3)] + [pltpu.SemaphoreType.DMA((3,))],
        compiler_params=_cp(("arbitrary",)),
    )(_attn_consts(), proj, proj, proj, dmixed, o, lse, qkvp, lsep)


def _ret_consts():
    c = np.zeros((RH, 8, RHD), np.float32)
    for h in range(RH):
        c[h, :, :] = np.log(np.float32(1.0) - np.float32(2.0 ** (-5.0 - h)))
    return jnp.asarray(c)


def _ret_factors(lg):
    i = lax.broadcasted_iota(jnp.int32, (CH, CH), 0)
    j = lax.broadcasted_iota(jnp.int32, (CH, CH), 1)
    dif = (i - j).astype(F32)
    decay = jnp.where(dif >= 0, jnp.exp(lg[:, 0:CH] * jnp.maximum(dif, 0.0)), 0.0)
    row = lax.broadcasted_iota(jnp.int32, (CH, RHD), 0).astype(F32)
    zeta = jnp.exp(lg * (CH - 1.0 - row))
    xi = jnp.exp(lg * (row + 1.0))
    return decay, zeta, xi, jnp.exp(lg * float(CH))


CBK = 8
RSTEPS = NB // CBK


def _ret_specs(rev):
    off = 3 * AH * AHD // RHD
    rows = CBK * CH

    def ch(n):
        return (RSTEPS - 1 - n) if rev else n

    def col(k):
        return pl.BlockSpec((rows, RHD), lambda h, n: (ch(n), off + k * RH + h))

    own = pl.BlockSpec((rows, RHD), lambda h, n: (ch(n), h))
    state = pl.BlockSpec((None, CBK, RHD, RHD), lambda h, n: (h, ch(n), 0, 0))
    const = pl.BlockSpec((None, 8, RHD), lambda h, n: (h, 0, 0))
    dm = pl.BlockSpec((rows, RHD), lambda h, n: (ch(n), AH * AHD // RHD + h))
    return col, own, state, const, dm


def _chunks(x):
    return x.reshape(CBK, CH, RHD)


def _ret_fwd(proj):
    def body(c_ref, q_ref, k_ref, v_ref, g_ref, ret_ref, mr_ref, st_ref, r_acc):
        n = pl.program_id(1)

        @pl.when(n == 0)
        def _():
            r_acc[...] = jnp.zeros_like(r_acc)

        decay, zeta, xi, gch = _ret_factors(c_ref[0:1, :])
        q3 = _chunks(q_ref[...].astype(BF16))
        kc = _chunks(k_ref[...] * (1.0 / math.sqrt(RHD)))
        k3 = kc.astype(BF16)
        v3 = _chunks(v_ref[...].astype(BF16))
        kv3 = _bdot_tn((kc * zeta[None]).astype(BF16), v3)
        r = r_acc[...]
        for i in range(CBK):
            st_ref[i] = r.astype(BF16)
            r = r * gch + kv3[i]
        r_acc[...] = r
        scores = _bdot_nt(q3, k3) * decay[None]
        ret = (_bdot(scores.astype(BF16), v3) + _bdot(q3, st_ref[...]) * xi[None]).reshape(CBK * CH, RHD)
        ret_ref[...] = ret
        rr = lax.rsqrt(jnp.mean(ret * ret, axis=-1, keepdims=True) + EPS)
        gv = g_ref[...]
        mr_ref[...] = ((gv * _sigmoid(gv)) * (ret * rr)).astype(BF16)

    col, own, state, const, _ = _ret_specs(False)
    return pl.pallas_call(
        body, name="ret_fwd", grid=(RH, RSTEPS),
        in_specs=[const, col(0), col(1), col(2), col(3)],
        out_specs=[own, own, state],
        out_shape=[jax.ShapeDtypeStruct((S, RH * RHD), F32), jax.ShapeDtypeStruct((S, RH * RHD), BF16),
                   jax.ShapeDtypeStruct((RH, NB, RHD, RHD), BF16)],
        scratch_shapes=[pltpu.VMEM((RHD, RHD), F32)],
        compiler_params=_cp(("parallel", "arbitrary")),
    )(_ret_consts(), proj, proj, proj, proj)


def _ret_bwd(proj, ret, states, dmixed, dproj):
    rows = CBK * CH
    col0 = 3 * AH * AHD

    def body(c_ref, q_ref, k_ref, v_ref, g_ref, ret_ref, st_ref, dm_ref, dproj_in, dproj_hbm, g_acc, gs,
             sq, sk, sv, sg, sems):
        del dproj_in
        h, n = pl.program_id(0), pl.program_id(1)
        step = h * RSTEPS + n

        def out_copies(t):
            hh, nn = t // RSTEPS, t % RSTEPS
            r0 = pl.multiple_of((RSTEPS - 1 - nn) * rows, rows)
            return [pltpu.make_async_copy(
                st, dproj_hbm.at[pl.ds(r0, rows), pl.ds(pl.multiple_of(col0 + (k * RH + hh) * RHD, RHD), RHD)],
                sems.at[k]) for k, st in enumerate((sq, sk, sv, sg))]

        @pl.when(n == 0)
        def _():
            g_acc[...] = jnp.zeros_like(g_acc)

        decay, zeta, xi, gch = _ret_factors(c_ref[0:1, :])
        ret_v = ret_ref[...]
        rr = lax.rsqrt(jnp.mean(ret_v * ret_v, axis=-1, keepdims=True) + EPS)
        gv = g_ref[...]
        sgm = _sigmoid(gv)
        dmix = dm_ref[...]
        dgate = ((dmix * (ret_v * rr)) * (sgm * (1.0 + gv * (1.0 - sgm)))).astype(BF16)
        dretn = dmix * (gv * sgm)
        dret = _chunks(rr * dretn - ret_v * ((rr * rr * rr) * jnp.mean(dretn * ret_v, axis=-1, keepdims=True)))

        q3 = _chunks(q_ref[...].astype(BF16))
        kc = _chunks(k_ref[...] * (1.0 / math.sqrt(RHD)))
        k3 = kc.astype(BF16)
        v3 = _chunks(v_ref[...].astype(BF16))
        d3 = dret.astype(BF16)
        dxi = (dret * xi[None]).astype(BF16)
        kz = (kc * zeta[None]).astype(BF16)
        dr3 = _bdot_tn(q3, dxi)
        acc = g_acc[...]
        for i in reversed(range(CBK)):
            gs[i] = acc.astype(BF16)
            acc = dr3[i] + gch * acc
        g_acc[...] = acc
        g3 = gs[...]
        sc = (_bdot_nt(q3, k3) * decay[None]).astype(BF16)
        da = (_bdot_nt(d3, v3) * decay[None]).astype(BF16)
        dq = _bdot(da, k3) + _bdot_nt(dxi, st_ref[...])
        dkc = _bdot_tn(da, q3) + _bdot_nt(v3, g3) * zeta[None]
        dv = _bdot_tn(sc, d3) + _bdot(kz, g3)

        @pl.when(step > 0)
        def _():
            for cp in out_copies(step - 1):
                cp.wait()

        sq[...] = dq.reshape(rows, RHD).astype(BF16)
        sk[...] = (dkc * (1.0 / math.sqrt(RHD))).reshape(rows, RHD).astype(BF16)
        sv[...] = dv.reshape(rows, RHD).astype(BF16)
        sg[...] = dgate
        for cp in out_copies(step):
            cp.start()

        @pl.when(step == RH * RSTEPS - 1)
        def _():
            for cp in out_copies(step):
                cp.wait()

    col, own, state, const, dm = _ret_specs(True)
    hbm = pl.BlockSpec(memory_space=pl.ANY)
    return pl.pallas_call(
        body, name="ret_bwd", grid=(RH, RSTEPS),
        in_specs=[const, col(0), col(1), col(2), col(3), own, state, dm, hbm],
        out_specs=hbm,
        out_shape=jax.ShapeDtypeStruct(dproj.shape, dproj.dtype),
        input_output_aliases={8: 0},
        scratch_shapes=[pltpu.VMEM((RHD, RHD), F32), pltpu.VMEM((CBK, RHD, RHD), BF16)]
        + [pltpu.VMEM((rows, RHD), BF16) for _ in range(4)] + [pltpu.SemaphoreType.DMA((4,))],
        compiler_params=_cp(("arbitrary", "arbitrary")),
    )(_ret_consts(), proj, proj, proj, proj, ret, states, dmixed, dproj)


class _NoReduction:
    def start(self, group, grads):
        pass

    def local(self, name, first=()):
        return []

    def landed(self, name):
        return []

    def update(self, name):
        return []

    place = None

    def rider(self, name):
        return None

    def set_update(self, name, outs):
        pass


def _local_step(x, tgt, nw1, nw2, nw3, win, wout, wgu_a, wgu_b, wd_a, wd_b, red=None):
    red = red or _NoReduction()

    def after(values, first):
        return lax.optimization_barrier((tuple(values), tuple(first)))[0]

    h1, r1 = _rms_fwd(x, nw1)
    proj = _proj(h1, win)
    o, ma, lse, qkvp, lsep = _attn_fwd(proj)
    ret, mr, states = _ret_fwd(proj)
    x2, h2, r2 = _out_proj_rms(x, ma, mr, wout, nw2)
    a, dadg, dadu = _ffn_up(h2, wgu_b, 1, _ffn_up(h2, wgu_a, 0))
    dx3, dx3b, st3 = _ffn_down_loss(_ffn_down_first(x2, a, wd_a), a, wd_b, nw3, tgt)

    dwd = _wgrad_rows(a, dx3b, "wgrad_down")
    red.start(["w_down"], [dwd])
    (dx3b,) = after([dx3b], [dwd])
    part = _ffn_down_bwd(dx3b, wd_a, dadg, dadu, 0)
    (dx3b,) = after([dx3b], red.local("w_down", first=[part]))
    dgu = _ffn_down_bwd(dx3b, wd_b, dadg, dadu, 1, [part])
    dwg = _wgrad_rows(dgu, h2, "wgrad_gate", 0)
    red.start(["w_gate"], [dwg])
    (dgu,) = after([dgu], [dwg])
    dwu = _wgrad_rows(dgu, h2, "wgrad_up", 1)
    red.start(["w_up"], [dwu])
    (dgu,) = after([dgu], red.local("w_gate", first=[dwu] + red.landed("w_down")))
    dx2, dx2b, st2 = _ffn_up_bwd(dgu, wgu_a, wgu_b, dx3, x2, r2, nw2)
    (dx2b,) = after([dx2b], red.local("w_up", first=[dx2b]))
    dwo = _wgrad_out(ma, mr, dx2b)
    red.start(["w_out"], [dwo])
    (dx2b,) = after([dx2b], [dwo])
    dmixed, done = _out_proj_bwd(dx2b, wout, red.place, red.rider("w_down"))
    red.set_update("w_down", done)
    dproj = _attn_bwd(proj, dmixed, o, lse, qkvp, lsep)
    (dmixed,) = after([dmixed], red.local("w_out", first=[dproj] + red.landed("w_gate")))
    dproj = _ret_bwd(proj, ret, states, dmixed, dproj)
    (dwi0,) = after([_wgrad_in(h1, dproj, 0)], red.landed("w_up"))
    red.start(["w_in_0"], [dwi0])
    (dproj,) = after([dproj], [dwi0])
    dwi1 = _wgrad_in(h1, dproj, 1)
    red.start(["w_in_1"], [dwi1])
    sums = red.local("w_in_0", first=[dwi1] + red.landed("w_out"))
    sums = red.local("w_in_1", first=sums + red.update("w_gate"))
    (dproj,) = after([dproj], sums)
    gx, st1 = _in_proj_bwd(dproj, win, dx2, x, r1, nw1)
    dwi = jnp.concatenate([dwi0, dwi1], axis=1)
    stats = jnp.concatenate([st1[0:1], st2[0:1], st3[0:2], jnp.zeros((4, D), F32)], axis=0)
    return stats, gx, dwi, dwo, dwg, dwu, dwd


def _place():
    x, y, c = lax.axis_index("x"), lax.axis_index("y"), lax.axis_index("c")
    return x, y, c, [(1 - x, y), (x, 1 - y), (1 - x, 1 - y)]


def _handshake(peers):
    barrier = pltpu.get_barrier_semaphore()
    for peer in peers:
        pl.semaphore_signal(barrier, inc=1, device_id=peer, device_id_type=MESH)
    pl.semaphore_wait(barrier, len(peers))


def _all_gather(shards, name, collective_id, per=0, rows=None):
    na = len(shards)
    nout = 1 if per else na
    lo, r = rows or (0, shards[0].shape[0])
    ngroups = NDEV // per if per else 0
    SIB, XN0, XN1, YN1, YN0, VIA_X, VIA_Y = 0, 1, 2, 3, 4, 5, 6
    D2D = {XN0: 7, XN1: 8, YN1: 9, YN0: 10, VIA_X: 11, VIA_Y: 12}

    def body(*refs):
        ins, outs = [ref.at[pl.ds(lo, r)] for ref in refs[:na]], refs[na:na + nout]
        send_sems, recv_sems, local_sems = refs[na + nout:]
        x, y, c, _ = _place()
        me, sib = (x, y, c), (x, y, 1 - c)
        xn, yn, dg = (1 - x, y, c), (x, 1 - y, c), (1 - x, 1 - y, c)
        _handshake([sib, xn, yn])

        def part(ref, h):
            rows = ref.shape[0] // 2
            return ref if h is None else ref.at[pl.ds(h * rows, rows)]

        def block(a, owner, h):
            idx = 4 * owner[0] + 2 * owner[1] + owner[2]
            if not per:
                return part(outs[a].at[idx], h)
            return part(outs[0].at[idx // per, a, pl.ds(pl.multiple_of((idx % per) * r, r), r)], h)

        def copy(a, k, owner, h, to, own_src=False):
            return pltpu.make_async_remote_copy(
                src_ref=part(ins[a], h) if own_src else block(a, owner, h), dst_ref=block(a, owner, h),
                send_sem=send_sems.at[a, k], recv_sem=recv_sems.at[a, k], device_id=to, device_id_type=MESH)

        def other(p):
            return (p[0], p[1], 1 - c)

        mine = [pltpu.make_async_copy(ins[a], block(a, me, None), local_sems.at[a]) for a in range(na)]
        for cp in mine:
            cp.start()
        sent = []
        for a in range(na):
            sent += [copy(a, XN0, me, 0, xn, True), copy(a, YN1, me, 1, yn, True),
                     copy(a, XN1, me, 1, xn, True), copy(a, YN0, me, 0, yn, True)]
        sent += [copy(a, SIB, me, None, sib, True) for a in range(na)]
        for cp in sent:
            cp.start()

        def landed(a, k, owner, h, then):
            copy(a, k, owner, h, me).wait_recv()
            for k2, to in then + [(D2D[k], sib)]:
                cp = copy(a, k2, owner, h, to)
                cp.start()
                sent.append(cp)

        for a in range(na):
            landed(a, XN0, xn, 0, [(VIA_Y, yn)])
            landed(a, YN1, yn, 1, [(VIA_X, xn)])
            landed(a, XN1, xn, 1, [])
            landed(a, YN0, yn, 0, [])
        for a in range(na):
            landed(a, VIA_Y, dg, 0, [])
            landed(a, VIA_X, dg, 1, [])
        for a in range(na):
            copy(a, SIB, sib, None, me).wait_recv()
            for k, owner, h in ((XN0, xn, 0), (XN1, xn, 1), (YN1, yn, 1), (YN0, yn, 0), (VIA_Y, dg, 0), (VIA_X, dg, 1)):
                copy(a, D2D[k], other(owner), h, me).wait_recv()
        for cp in sent:
            cp.wait_send()
        for cp in mine:
            cp.wait()

    if per:
        out_type = [jax.ShapeDtypeStruct((ngroups, na, per * r, shards[0].shape[1]), shards[0].dtype)]
    else:
        out_type = [jax.ShapeDtypeStruct((NDEV,) + s.shape, s.dtype) for s in shards]
    return _sequencer_call(
        body, name, collective_id, out_type,
        [pltpu.SemaphoreType.DMA((na, 13)), pltpu.SemaphoreType.DMA((na, 13)), pltpu.SemaphoreType.DMA((na,))])(*shards)


def _sequencer_call(body, name, collective_id, out_type, scratch_types):
    return pl.kernel(
        body, name=name, out_type=out_type,
        mesh=plsc.ScalarSubcoreMesh(axis_name="sequencer", num_cores=1),
        scratch_types=scratch_types,
        compiler_params=pltpu.CompilerParams(collective_id=collective_id))


def _exchange_sibling(grads, name, collective_id):
    na = len(grads)

    def body(*refs):
        ins, outs = refs[:na], refs[na:2 * na]
        send_sems, recv_sems = refs[2 * na:]
        x, y, c, _ = _place()
        _handshake([(x, y, 1 - c)])
        cps = []
        for a in range(na):
            for k in range(4):
                cps.append(pltpu.make_async_remote_copy(
                    src_ref=ins[a].at[2 * k + (1 - c)], dst_ref=outs[a].at[k],
                    send_sem=send_sems.at[a, k], recv_sem=recv_sems.at[a, k],
                    device_id=(x, y, 1 - c), device_id_type=MESH))
        for cp in cps:
            cp.start()
        for cp in cps:
            cp.wait()

    return _sequencer_call(
        body, name, collective_id,
        [jax.ShapeDtypeStruct((4,) + g.shape[1:], g.dtype) for g in grads],
        [pltpu.SemaphoreType.DMA((na, 4)), pltpu.SemaphoreType.DMA((na, 4))])(*grads)


def _row_tile(rows, cols):
    for t in (512, 256, 176, 128, 64, 32, 16):
        if rows % t == 0 and t * cols * 4 <= (2 << 20):
            return t
    raise ValueError((rows, cols))


STREAM_BUFS = 3


def _stream_tile(rows, steps):
    for t in (512, 256, 176, 128, 64, 32, 16):
        if rows % t == 0 and rows // t >= steps:
            return t
    raise ValueError((rows, steps))


def _stream(n, loads, stores, compute):
    for k in range(min(STREAM_BUFS, n)):
        for cp in loads(k):
            cp.start()
    for k in range(n):
        for cp in loads(k):
            cp.wait()
        if k >= 2:
            for cp in stores(k - 2):
                cp.wait()
        compute(k)
        for cp in stores(k):
            cp.start()
        if k + STREAM_BUFS < n:
            for cp in loads(k + STREAM_BUFS):
                cp.start()
    for k in range(max(n - 2, 0), n):
        for cp in stores(k):
            cp.wait()


def _chip_sum(place, g, got, name):
    _, r, c = g.shape
    tm = _stream_tile(r, 4)
    nt = r // tm

    def body(pos_ref, g_hbm, got_hbm, o_hbm, g_buf, s_buf, o_buf, sem_in, sem_out):
        def chip(j):
            return 2 * (pos_ref[0] ^ (0 if j == 1 else 1)) + (pos_ref[1] ^ (0 if j == 0 else 1))

        def loads(k):
            j, rows, slot = k // nt, pl.ds((k % nt) * tm, tm), k % STREAM_BUFS
            return [pltpu.make_async_copy(g_hbm.at[2 * chip(j) + pos_ref[2], rows], g_buf.at[slot], sem_in.at[slot, 0]),
                    pltpu.make_async_copy(got_hbm.at[chip(j), rows], s_buf.at[slot], sem_in.at[slot, 1])]

        def stores(k):
            return [pltpu.make_async_copy(o_buf.at[k % 2], o_hbm.at[k // nt, pl.ds((k % nt) * tm, tm)],
                                          sem_out.at[k % 2])]

        def compute(k):
            slot = k % STREAM_BUFS
            o_buf[k % 2] = (g_buf[slot].astype(F32) + s_buf[slot].astype(F32)).astype(BF16)

        _stream(3 * nt, loads, stores, compute)

    hbm = pl.BlockSpec(memory_space=pl.ANY)
    return pl.pallas_call(
        body, name=name,
        grid_spec=pltpu.PrefetchScalarGridSpec(
            num_scalar_prefetch=1, grid=(1,), in_specs=[hbm, hbm], out_specs=hbm,
            scratch_shapes=[pltpu.VMEM((STREAM_BUFS, tm, c), BF16), pltpu.VMEM((STREAM_BUFS, tm, c), BF16),
                            pltpu.VMEM((2, tm, c), BF16),
                            pltpu.SemaphoreType.DMA((STREAM_BUFS, 2)), pltpu.SemaphoreType.DMA((2,))]),
        out_shape=jax.ShapeDtypeStruct((3, r, c), BF16),
        compiler_params=_cp(("arbitrary",)),
    )(place, g, got)


def _exchange_chips(sums, name, collective_id):
    na = len(sums)

    def body(*refs):
        ins, outs = refs[:na], refs[na:2 * na]
        send_sems, recv_sems = refs[2 * na:]
        x, y, c, chips = _place()
        _handshake([(*chip, c) for chip in chips])
        cps = []
        for a in range(na):
            for j, chip in enumerate(chips):
                cps.append(pltpu.make_async_remote_copy(
                    src_ref=ins[a].at[j], dst_ref=outs[a].at[j],
                    send_sem=send_sems.at[a, j], recv_sem=recv_sems.at[a, j],
                    device_id=(*chip, c), device_id_type=MESH))
        for cp in cps:
            cp.start()
        for cp in cps:
            cp.wait()

    return _sequencer_call(
        body, name, collective_id,
        [jax.ShapeDtypeStruct((3,) + s.shape[1:], s.dtype) for s in sums],
        [pltpu.SemaphoreType.DMA((na, 3)), pltpu.SemaphoreType.DMA((na, 3))])(*sums)


def _exchange_stats(stats, collective_id):
    def body(st_in, st_out, st_send, st_recv, local_sem):
        x, y, c, _ = _place()
        me_idx = 4 * x + 2 * y + c
        peers = [(x ^ ((k >> 2) & 1), y ^ ((k >> 1) & 1), c ^ (k & 1)) for k in range(1, 8)]
        _handshake(peers)
        mine = pltpu.make_async_copy(st_in, st_out.at[me_idx], local_sem)
        mine.start()
        cps = [pltpu.make_async_remote_copy(
            src_ref=st_in, dst_ref=st_out.at[me_idx], send_sem=st_send.at[k], recv_sem=st_recv.at[k],
            device_id=peer, device_id_type=MESH) for k, peer in enumerate(peers)]
        for cp in cps:
            cp.start()
        for cp in cps:
            cp.wait()
        mine.wait()

    return _sequencer_call(
        body, "exchange_stats", collective_id,
        jax.ShapeDtypeStruct((NDEV,) + stats.shape, stats.dtype),
        [pltpu.SemaphoreType.DMA((7,)), pltpu.SemaphoreType.DMA((7,)), pltpu.SemaphoreType.DMA])(stats)


class _Reduction:
    def __init__(self, place, first_collective_id, state):
        self.place = place
        self.ids = iter(range(first_collective_id, 32))
        self.state = state
        self.groups = {}
        self.updates = {}

    def next_id(self):
        return next(self.ids)

    def start(self, group, grads):
        got = _exchange_sibling(grads, "sibling_exchange_" + group[0], self.next_id())
        self.groups[group[0]] = dict(names=group, grads=grads, got=got)

    def local(self, name, first=()):
        grp = self.groups[name]
        grads = lax.optimization_barrier((tuple(grp["grads"]), tuple(first)))[0]
        grp["sums"] = [_chip_sum(self.place, g, s, "chip_sum_" + n)
                       for g, s, n in zip(grads, grp["got"], grp["names"])]
        grp["chips"] = _exchange_chips(grp["sums"], "chip_exchange_" + name, self.next_id())
        return grp["sums"]

    def landed(self, name):
        return list(self.groups[name]["chips"])

    def rider(self, name):
        grp = next(g for g in self.groups.values() if name in g["names"])
        k = grp["names"].index(name)
        return self.state[name][:3] + (grp["grads"][k], grp["got"][k], grp["chips"][k])

    def set_update(self, name, outs):
        self.updates[name] = list(outs)

    def update(self, name):
        if name not in self.updates:
            grp = next(g for g in self.groups.values() if name in g["names"])
            k = grp["names"].index(name)
            w, m, v, part, parts = self.state[name]
            before = self.update(f"{name[:-1]}{part - 1}") if part else None
            self.updates[name] = _shard_update(self.place, w, m, v, grp["grads"][k], grp["got"][k],
                                               grp["chips"][k], "update_" + name, part, parts, before)
        return list(self.updates[name])


def _adamw(w, g, m, v):
    m = ADAM_B1 * m + (1.0 - ADAM_B1) * g
    v = ADAM_B2 * v + (1.0 - ADAM_B2) * (g * g)
    m_hat = m / (1.0 - ADAM_B1 ** ADAM_STEP)
    v_hat = v / (1.0 - ADAM_B2 ** ADAM_STEP)
    delta = -ADAM_LR * (m_hat / (jnp.sqrt(v_hat) + ADAM_EPS) + ADAM_WD * w)
    return delta, m, v


def _update_tile(w_ref, m_ref, v_ref, g_ref, s_ref, c_ref, go_ref, d_ref, mo_ref, vo_ref):
    grad = g_ref[...].astype(F32) + s_ref[...].astype(F32)
    for j in range(3):
        grad = grad + c_ref[j].astype(F32)
    delta, mn, vn = _adamw(w_ref[...], grad, m_ref[...], v_ref[...])
    go_ref[...] = grad
    d_ref[...] = delta
    mo_ref[...] = mn
    vo_ref[...] = vn


def _shard_update(place, w, m, v, g, got_sib, got_chips, name, part=0, parts=1, before=None):
    r, c = w.shape
    rp = r // parts
    tm = _stream_tile(rp, 8)
    nt = rp // tm
    before = list(before or [])

    def body(pos_ref, w_hbm, m_hbm, v_hbm, g_hbm, s_hbm, c_hbm, *rest):
        outs = rest[len(before):len(before) + 4]
        w_buf, m_buf, v_buf, g_buf, s_buf, c_buf, o_buf, sem_in, sem_out = rest[len(before) + 4:]
        own = 4 * pos_ref[0] + 2 * pos_ref[1] + pos_ref[2]
        chip = 2 * pos_ref[0] + pos_ref[1]

        def loads(k):
            slot, rows, mine = k % STREAM_BUFS, pl.ds(k * tm, tm), pl.ds(part * rp + k * tm, tm)
            pairs = [(w_hbm.at[mine], w_buf), (m_hbm.at[mine], m_buf), (v_hbm.at[mine], v_buf),
                     (g_hbm.at[own, rows], g_buf), (s_hbm.at[chip, rows], s_buf), (c_hbm.at[:, rows], c_buf)]
            return [pltpu.make_async_copy(src, buf.at[slot], sem_in.at[slot, n]) for n, (src, buf) in enumerate(pairs)]

        def stores(k):
            mine = pl.ds(part * rp + k * tm, tm)
            return [pltpu.make_async_copy(o_buf.at[k % 2, n], out.at[mine], sem_out.at[k % 2, n])
                    for n, out in enumerate(outs)]

        def compute(k):
            slot = k % STREAM_BUFS
            _update_tile(w_buf.at[slot], m_buf.at[slot], v_buf.at[slot], g_buf.at[slot], s_buf.at[slot],
                         c_buf.at[slot], *[o_buf.at[k % 2, n] for n in range(4)])

        _stream(nt, loads, stores, compute)

    hbm = pl.BlockSpec(memory_space=pl.ANY)
    return pl.pallas_call(
        body, name=name,
        grid_spec=pltpu.PrefetchScalarGridSpec(
            num_scalar_prefetch=1, grid=(1,), in_specs=[hbm] * (6 + len(before)), out_specs=[hbm] * 4,
            scratch_shapes=[pltpu.VMEM((STREAM_BUFS, tm, c), F32)] * 3 + [pltpu.VMEM((STREAM_BUFS, tm, c), BF16)] * 2
            + [pltpu.VMEM((STREAM_BUFS, 3, tm, c), BF16), pltpu.VMEM((2, 4, tm, c), F32),
               pltpu.SemaphoreType.DMA((STREAM_BUFS, 6)), pltpu.SemaphoreType.DMA((2, 4))]),
        out_shape=[jax.ShapeDtypeStruct((r, c), F32)] * 4,
        input_output_aliases={7 + k: k for k in range(len(before))},
        compiler_params=_cp(("arbitrary",)),
    )(place, w, m, v, g, got_sib, got_chips, *before)


def _small_update(stats_all, ws, ms, vs):
    def body(st_ref, w_ref, m_ref, v_ref, go_ref, d_ref, mo_ref, vo_ref):
        grad = st_ref[0]
        for k in range(1, NDEV):
            grad = grad + st_ref[k]
        delta, mn, vn = _adamw(w_ref[...], grad, m_ref[...], v_ref[...])
        go_ref[...] = grad
        d_ref[...] = delta
        mo_ref[...] = mn
        vo_ref[...] = vn

    return pl.pallas_call(
        body, name="small_update",
        out_shape=[jax.ShapeDtypeStruct((8, D), F32)] * 4,
        compiler_params=_cp(),
    )(stats_all, ws, ms, vs)


def kernel(x, norm_mix_w, w_in, w_out, norm_ffn_w, w_gate, w_up, w_down, norm_final_w, loss_target, m_norm_mix_w, m_w_in, m_w_out, m_norm_ffn_w, m_w_gate, m_w_up, m_w_down, m_norm_final_w, v_norm_mix_w, v_w_in, v_w_out, v_norm_ffn_w, v_w_gate, v_w_up, v_w_down, v_norm_final_w):
    tr = {"w_gate", "w_up"}
    names = ["w_in", "w_out", "w_gate", "w_up", "w_down"]

    def view(a, n):
        return a[0].T if n in tr else a[0]

    big_w = [view(a, n) for a, n in zip([w_in, w_out, w_gate, w_up, w_down], names)]
    big_m = [view(a, n) for a, n in zip([m_w_in, m_w_out, m_w_gate, m_w_up, m_w_down], names)]
    big_v = [view(a, n) for a, n in zip([v_w_in, v_w_out, v_w_gate, v_w_up, v_w_down], names)]

    shards = [_cast_bf16(w, "cast_" + n) for w, n in zip(big_w, names)]
    (win,) = _all_gather(shards[0:1], "all_gather_w_in", 1)
    (wout,) = _all_gather(shards[1:2], "all_gather_w_out", 2)
    (wgu_a,) = _all_gather(shards[2:4], "all_gather_gate_up_0", 3, per=FF_PER, rows=(0, FF_ROWS))
    (wgu_b,) = _all_gather(shards[2:4], "all_gather_gate_up_1", 4, per=FF_PER, rows=(FF_ROWS, FF_ROWS))
    (wd_a,) = _all_gather(shards[4:5], "all_gather_w_down_0", 5, per=FF_PER, rows=(0, FF_ROWS))
    (wd_b,) = _all_gather(shards[4:5], "all_gather_w_down_1", 6, per=FF_PER, rows=(FF_ROWS, FF_ROWS))
    nw3 = norm_final_w.reshape(1, D)
    place = jnp.stack([lax.axis_index("x"), lax.axis_index("y"), lax.axis_index("c")]).astype(jnp.int32)
    state = {n: (w, m, v, 0, 1) for n, w, m, v in zip(names, big_w, big_m, big_v)}
    for part in range(W_IN_PARTS):
        state[f"w_in_{part}"] = state["w_in"][:3] + (part, W_IN_PARTS)
    red = _Reduction(place, 7, state)
    stats, gx, *_ = _local_step(
        x[0], loss_target[0], norm_mix_w, norm_ffn_w, nw3, win, wout.reshape(D, D),
        wgu_a.reshape(NFG // 2, 2 * N_FG, D), wgu_b.reshape(NFG // 2, 2 * N_FG, D),
        wd_a.reshape(NFG // 2, N_FG, D), wd_b.reshape(NFG // 2, N_FG, D), red)
    stats_all = _exchange_stats(stats, red.next_id())
    upd = [red.update(f"w_in_{W_IN_PARTS - 1}" if n == "w_in" else n) for n in names]
    stats_all = lax.optimization_barrier((stats_all, tuple(upd[0])))[0]

    def rows(a, b, c):
        return jnp.concatenate([a.reshape(1, D), b.reshape(1, D), c.reshape(1, D), jnp.zeros((5, D), F32)], axis=0)

    sg, sd, sm, sv = _small_update(stats_all, rows(norm_mix_w, norm_ffn_w, norm_final_w),
                                   rows(m_norm_mix_w, m_norm_ffn_w, m_norm_final_w),
                                   rows(v_norm_mix_w, v_norm_ffn_w, v_norm_final_w))
    loss = sg[3, 0]

    def outs(k, small):
        big = [(u[k].T if n in tr else u[k])[None] for u, n in zip(upd, names)]
        return [small[0:1], big[0], big[1], small[1:2], big[2], big[3], big[4], small[2]]

    return (loss, gx[None], *outs(0, sg), *outs(1, sd), *outs(2, sm), *outs(3, sv))
```

```python
import math

import numpy as np
import jax
import jax.numpy as jnp
from jax import lax
from jax.experimental import pallas as pl
from jax.experimental.pallas import tpu as pltpu
from jax.experimental.pallas import tpu_sc as plsc

F32 = jnp.float32
BF16 = jnp.bfloat16

S = 2048
D = 2048
NDEV = 8
N_IN = 7168 // NDEV
N_FF = 5632 // NDEV
NFG, N_FG = NDEV // 2, 2 * N_FF
FF_PER, FF_ROWS = 4, N_FF // 2
IN_ROUNDS = ((0, 512), (512, N_IN - 512))
N_OUT = 2048 // NDEV
AH, AHD = 8, 128
RH, RHD = 4, 256
CH = 128
NB = S // CH
EPS = 1e-6
PATTERNS = ((1, 16), (4, 4), (16, 1))
NEG = -1e30
VMEM_LIMIT = 56 * 1024 * 1024

ADAM_LR, ADAM_B1, ADAM_B2, ADAM_EPS, ADAM_WD, ADAM_STEP = 0.001, 0.9, 0.999, 1e-08, 0.01, 10
MESH = pl.DeviceIdType.MESH


def _cp(sem=None):
    return pltpu.CompilerParams(dimension_semantics=sem, vmem_limit_bytes=VMEM_LIMIT)


def _dot(a, b):
    return jnp.dot(a, b, preferred_element_type=F32)


def _dot_nt(a, b):
    return lax.dot_general(a, b, (((1,), (1,)), ((), ())), preferred_element_type=F32)


def _dot_tn(a, b):
    return lax.dot_general(a, b, (((0,), (0,)), ((), ())), preferred_element_type=F32)


def _sigmoid(x):
    return 0.5 * jnp.tanh(0.5 * x) + 0.5


def _cast_bf16(w, name):
    r, c = w.shape
    tm = r if r <= 1024 else 512

    def body(w_ref, o_ref):
        o_ref[...] = w_ref[...].astype(BF16)

    return pl.pallas_call(
        body, name=name, grid=(r // tm,),
        in_specs=[pl.BlockSpec((tm, c), lambda i: (i, 0))],
        out_specs=pl.BlockSpec((tm, c), lambda i: (i, 0)),
        out_shape=jax.ShapeDtypeStruct((r, c), BF16),
        compiler_params=_cp(("parallel",)),
    )(w)


def _rms_fwd(x, nw):
    tm = 256

    def body(x_ref, w_ref, h_ref, r_ref):
        xs = x_ref[...]
        r = lax.rsqrt(jnp.mean(xs * xs, axis=-1, keepdims=True) + EPS)
        h_ref[...] = ((xs * r) * w_ref[...]).astype(BF16)
        r_ref[...] = r

    return pl.pallas_call(
        body, name="rms_fwd", grid=(S // tm,),
        in_specs=[pl.BlockSpec((tm, D), lambda i: (i, 0)), pl.BlockSpec((1, D), lambda i: (0, 0))],
        out_specs=[pl.BlockSpec((tm, D), lambda i: (i, 0)), pl.BlockSpec((tm, 1), lambda i: (i, 0))],
        out_shape=[jax.ShapeDtypeStruct((S, D), BF16), jax.ShapeDtypeStruct((S, 1), F32)],
        compiler_params=_cp(("parallel",)),
    )(x, nw)


def _row_copies(hbm_refs, bufs, sems, m, tm):
    rows = pl.ds(pl.multiple_of(m * tm, tm), tm)
    return [pltpu.make_async_copy(h.at[rows], b, sems.at[i]) for i, (h, b) in enumerate(zip(hbm_refs, bufs))]


def _rms_bwd_tile(dh, xs, r, nw):
    dnw = jnp.sum(dh * (xs * r), axis=0, keepdims=True)
    gy = dh * nw
    dx = r * gy - xs * ((r * r * r) * jnp.mean(gy * xs, axis=-1, keepdims=True))
    return dx, dnw


def _cast_cols(w, name):
    r, c = w.shape
    tm = 512

    def body(w_ref, *o_refs):
        for o_ref, (off, width) in zip(o_refs, IN_ROUNDS):
            o_ref[...] = w_ref[:, off:off + width].astype(BF16)

    return pl.pallas_call(
        body, name=name, grid=(r // tm,),
        in_specs=[pl.BlockSpec((tm, c), lambda i: (i, 0))],
        out_specs=[pl.BlockSpec((tm, width), lambda i: (i, 0)) for _, width in IN_ROUNDS],
        out_shape=[jax.ShapeDtypeStruct((r, width), BF16) for _, width in IN_ROUNDS],
        compiler_params=_cp(("parallel",)),
    )(w)


def _proj_round(h1, win, k, before):
    tm = 1024
    nm = S // tm
    off, width = IN_ROUNDS[k]
    before = [] if before is None else [before]

    def body(a_ref, w_ref, *rest):
        o_hbm, o_buf, sems = rest[-3:]
        p, m = pl.program_id(0), pl.program_id(1)
        t = p * nm + m

        def out_copy(pp, mm, slot):
            cols = pl.ds(pl.multiple_of(pp * N_IN + off, 128), width)
            return pltpu.make_async_copy(o_buf.at[slot], o_hbm.at[pl.ds(pl.multiple_of(mm * tm, tm), tm), cols],
                                         sems.at[slot])

        @pl.when(t >= 2)
        def _():
            out_copy(p, m, t % 2).wait()

        o_buf[t % 2] = _dot(a_ref[...], w_ref[...])
        out_copy(p, m, t % 2).start()

        @pl.when(t == NDEV * nm - 1)
        def _():
            out_copy(p, m, (t + 1) % 2).wait()
            out_copy(p, m, t % 2).wait()

    return pl.pallas_call(
        body, name=f"proj_{k}", grid=(NDEV, nm),
        in_specs=[pl.BlockSpec((tm, D), lambda p, m: (m, 0)),
                  pl.BlockSpec((None, D, width), lambda p, m: (p, 0, 0))]
        + [pl.BlockSpec(memory_space=pl.ANY)] * len(before),
        out_specs=pl.BlockSpec(memory_space=pl.ANY),
        out_shape=jax.ShapeDtypeStruct((S, NDEV * N_IN), F32),
        scratch_shapes=[pltpu.VMEM((2, tm, width), F32), pltpu.SemaphoreType.DMA((2,))],
        input_output_aliases={2: 0} if before else {},
        compiler_params=_cp(("arbitrary", "arbitrary")),
    )(h1, win, *before)


def _proj(h1, wins):
    out = None
    for k, win in enumerate(wins):
        out = _proj_round(h1, win, k, out)
    return out


def _out_proj_rms(x, ma, mr, wout, nw):
    tm = 256
    half = D // 2

    def body(x_ref, ma_ref, mr_ref, w_ref, nw_ref, x2_ref, h_ref, r_ref):
        acc = _dot(ma_ref[...], w_ref[0:half, :]) + _dot(mr_ref[...], w_ref[half:D, :])
        x2 = x_ref[...] + acc
        r = lax.rsqrt(jnp.mean(x2 * x2, axis=-1, keepdims=True) + EPS)
        x2_ref[...] = x2
        h_ref[...] = ((x2 * r) * nw_ref[...]).astype(BF16)
        r_ref[...] = r

    return pl.pallas_call(
        body, name="out_proj_rms", grid=(S // tm,),
        in_specs=[pl.BlockSpec((tm, D), lambda i: (i, 0)),
                  pl.BlockSpec((tm, half), lambda i: (i, 0)),
                  pl.BlockSpec((tm, half), lambda i: (i, 0)),
                  pl.BlockSpec((D, D), lambda i: (0, 0)),
                  pl.BlockSpec((1, D), lambda i: (0, 0))],
        out_specs=[pl.BlockSpec((tm, D), lambda i: (i, 0)), pl.BlockSpec((tm, D), lambda i: (i, 0)),
                   pl.BlockSpec((tm, 1), lambda i: (i, 0))],
        out_shape=[jax.ShapeDtypeStruct((S, D), F32), jax.ShapeDtypeStruct((S, D), BF16),
                   jax.ShapeDtypeStruct((S, 1), F32)],
        compiler_params=_cp(("parallel",)),
    )(x, ma, mr, wout, nw)


def _ffn_up(h2, wgu, part, before=None):
    tm = 512

    def body(h_ref, w_ref, *rest):
        a_ref, dadg_ref, dadu_ref = rest[-3:]
        gu = _dot_nt(h_ref[...], w_ref[...])
        g, u = gu[:, 0:N_FG], gu[:, N_FG:2 * N_FG]
        sg = _sigmoid(g)
        silu = g * sg
        a_ref[...] = (silu * u).astype(BF16)
        dadg_ref[...] = (u * (sg * (1.0 + g * (1.0 - sg)))).astype(BF16)
        dadu_ref[...] = silu.astype(BF16)

    half = NFG // 2
    first = part * half
    before = list(before or [])
    blk = pl.BlockSpec((None, tm, N_FG), lambda p, m: (p + first, m, 0))
    return pl.pallas_call(
        body, name=f"ffn_up_{part}", grid=(half, S // tm),
        in_specs=[pl.BlockSpec((tm, D), lambda p, m: (m, 0)),
                  pl.BlockSpec((None, 2 * N_FG, D), lambda p, m: (p, 0, 0))]
        + [pl.BlockSpec(memory_space=pl.ANY)] * len(before),
        out_specs=[blk, blk, blk],
        out_shape=[jax.ShapeDtypeStruct((NFG, S, N_FG), BF16)] * 3,
        input_output_aliases={2 + k: k for k in range(len(before))},
        compiler_params=_cp(("parallel", "parallel")),
    )(h2, wgu, *before)


def _ffn_down_first(x2, a, wd):
    tm = 512
    n = wd.shape[0]

    def body(x_ref, a_ref, w_ref, o_ref):
        p = pl.program_id(1)

        @pl.when(p == 0)
        def _():
            o_ref[...] = x_ref[...] + _dot(a_ref[...], w_ref[...])

        @pl.when(p > 0)
        def _():
            o_ref[...] += _dot(a_ref[...], w_ref[...])

    return pl.pallas_call(
        body, name="ffn_down_first", grid=(S // tm, n),
        in_specs=[pl.BlockSpec((tm, D), lambda m, p: (m, 0)),
                  pl.BlockSpec((None, tm, N_FG), lambda m, p: (p, m, 0)),
                  pl.BlockSpec((None, N_FG, D), lambda m, p: (p, 0, 0))],
        out_specs=pl.BlockSpec((tm, D), lambda m, p: (m, 0)),
        out_shape=jax.ShapeDtypeStruct((S, D), F32),
        compiler_params=_cp(("parallel", "arbitrary")),
    )(x2, a, wd)


def _ffn_down_loss(x2, a, wd, nw, tgt):
    tm = 512
    first = NFG - wd.shape[0]

    def body(x2_hbm, a_ref, w_ref, nw_ref, t_hbm, dx_ref, dxb_ref, st_ref, acc_ref, x2_buf, t_buf, sems):
        m, p = pl.program_id(0), pl.program_id(1)
        tail_in = _row_copies((x2_hbm, t_hbm), (x2_buf, t_buf), sems, m, tm)

        @pl.when(p == 0)
        def _():
            acc_ref[...] = jnp.zeros_like(acc_ref)
            for cp in tail_in:
                cp.start()

        @pl.when((p == 0) & (m == 0))
        def _():
            st_ref[...] = jnp.zeros_like(st_ref)

        acc_ref[...] += _dot(a_ref[...], w_ref[...])

        @pl.when(p == NFG - first - 1)
        def _():
            for cp in tail_in:
                cp.wait()
            x3 = x2_buf[...] + acc_ref[...]
            nwv = nw_ref[...]
            r = lax.rsqrt(jnp.mean(x3 * x3, axis=-1, keepdims=True) + EPS)
            y = (x3 * r) * nwv
            err = y - t_buf[...]
            loss = 0.5 * jnp.sum(jnp.mean(err * err, axis=-1, keepdims=True), axis=0, keepdims=True)
            dy = err * (1.0 / D)
            dx, dnw = _rms_bwd_tile(dy, x3, r, nwv)
            dx_ref[...] = dx
            dxb_ref[...] = dx.astype(BF16)
            st_ref[0:1, :] += dnw
            st_ref[1:2, :] += jnp.broadcast_to(loss, (1, D))

    return pl.pallas_call(
        body, name="ffn_down_loss", grid=(S // tm, NFG - first),
        in_specs=[pl.BlockSpec(memory_space=pl.ANY),
                  pl.BlockSpec((None, tm, N_FG), lambda m, p: (p + first, m, 0)),
                  pl.BlockSpec((None, N_FG, D), lambda m, p: (p, 0, 0)),
                  pl.BlockSpec((1, D), lambda m, p: (0, 0)),
                  pl.BlockSpec(memory_space=pl.ANY)],
        out_specs=[pl.BlockSpec((tm, D), lambda m, p: (m, 0)), pl.BlockSpec((tm, D), lambda m, p: (m, 0)),
                   pl.BlockSpec((8, D), lambda m, p: (0, 0))],
        out_shape=[jax.ShapeDtypeStruct((S, D), F32), jax.ShapeDtypeStruct((S, D), BF16),
                   jax.ShapeDtypeStruct((8, D), F32)],
        scratch_shapes=[pltpu.VMEM((tm, D), F32), pltpu.VMEM((tm, D), F32), pltpu.VMEM((tm, D), F32),
                        pltpu.SemaphoreType.DMA((2,))],
        compiler_params=_cp(("arbitrary", "arbitrary")),
    )(x2, a, wd, nw, tgt)


def _ffn_down_bwd(dx3b, wd, dadg, dadu, part, before=None):
    tm = 1024
    half = NFG // 2

    def body(dx_ref, w_ref, dadg_ref, dadu_ref, *rest):
        dgu_ref = rest[-1]
        da = _dot_nt(dx_ref[...], w_ref[...])
        dgu_ref[:, 0:N_FG] = (da * dadg_ref[...].astype(F32)).astype(BF16)
        dgu_ref[:, N_FG:2 * N_FG] = (da * dadu_ref[...].astype(F32)).astype(BF16)

    blk = pl.BlockSpec((None, tm, N_FG), lambda p, m: (p + part * half, m, 0))
    before = list(before or [])
    return pl.pallas_call(
        body, name=f"ffn_down_bwd_{part}", grid=(half, S // tm),
        in_specs=[pl.BlockSpec((tm, D), lambda p, m: (m, 0)),
                  pl.BlockSpec((None, N_FG, D), lambda p, m: (p, 0, 0)), blk, blk]
        + [pl.BlockSpec(memory_space=pl.ANY)] * len(before),
        out_specs=pl.BlockSpec((None, tm, 2 * N_FG), lambda p, m: (p + part * half, m, 0)),
        out_shape=jax.ShapeDtypeStruct((NFG, S, 2 * N_FG), BF16),
        input_output_aliases={4 + k: k for k in range(len(before))},
        compiler_params=_cp(("parallel", "parallel")),
    )(dx3b, wd, dadg, dadu, *before)


def _ffn_up_bwd(dgu, wgu_a, wgu_b, dres, xs, r, nw):
    tm = 512
    nm = S // tm
    na = wgu_a.shape[0]

    def body(dgu_ref, wa_hbm, wb_hbm, dres_hbm, x_hbm, r_ref, nw_ref, dx_ref, dxb_ref, st_ref,
             w_buf, dres_buf, x_buf, sems, w_sems):
        m, p = pl.program_id(0), pl.program_id(1)
        tail_in = _row_copies((dres_hbm, x_hbm), (dres_buf, x_buf), sems, m, tm)

        def fetch(g, slot):
            for src, lo in ((wa_hbm, 0), (wb_hbm, na)):
                @pl.when((g >= lo) & (g < lo + na))
                def _():
                    pltpu.make_async_copy(src.at[g - lo], w_buf.at[slot], w_sems.at[slot]).start()

        @pl.when((p == 0) & (m == 0))
        def _():
            st_ref[...] = jnp.zeros_like(st_ref)
            fetch(p, 0)

        @pl.when((p < NFG - 1) | (m < nm - 1))
        def _():
            fetch((p + 1) % NFG, (p + 1) % 2)

        @pl.when(p == 0)
        def _():
            dx_ref[...] = jnp.zeros_like(dx_ref)
            for cp in tail_in:
                cp.start()

        slot = p % 2
        pltpu.make_async_copy(wa_hbm.at[0], w_buf.at[slot], w_sems.at[slot]).wait()
        dx_ref[...] += _dot(dgu_ref[...], w_buf[slot])

        @pl.when(p == NFG - 1)
        def _():
            for cp in tail_in:
                cp.wait()
            dx, dnw = _rms_bwd_tile(dx_ref[...], x_buf[...], r_ref[...], nw_ref[...])
            dx = dres_buf[...] + dx
            dx_ref[...] = dx
            dxb_ref[...] = dx.astype(BF16)
            st_ref[0:1, :] += dnw

    blk = pl.BlockSpec((None, tm, 2 * N_FG), lambda m, p: (p, m, 0))
    row = pl.BlockSpec((tm, D), lambda m, p: (m, 0))
    hbm = pl.BlockSpec(memory_space=pl.ANY)
    return pl.pallas_call(
        body, name="ffn_up_bwd", grid=(nm, NFG),
        in_specs=[blk, hbm, hbm, hbm, hbm, pl.BlockSpec((tm, 1), lambda m, p: (m, 0)),
                  pl.BlockSpec((1, D), lambda m, p: (0, 0))],
        out_specs=[row, row, pl.BlockSpec((8, D), lambda m, p: (0, 0))],
        out_shape=[jax.ShapeDtypeStruct((S, D), F32), jax.ShapeDtypeStruct((S, D), BF16),
                   jax.ShapeDtypeStruct((8, D), F32)],
        scratch_shapes=[pltpu.VMEM((2, 2 * N_FG, D), BF16), pltpu.VMEM((tm, D), F32), pltpu.VMEM((tm, D), F32),
                        pltpu.SemaphoreType.DMA((2,)), pltpu.SemaphoreType.DMA((2,))],
        compiler_params=_cp(("arbitrary", "arbitrary")),
    )(dgu, wgu_a, wgu_b, dres, xs, r, nw)


def _out_proj_bwd(dx2b, wout, place=None, rider=None):
    tm = 256

    if rider is None:
        def body(dx_ref, w_ref, o_ref):
            o_ref[...] = _dot_nt(dx_ref[...], w_ref[...])

        return pl.pallas_call(
            body, name="out_proj_bwd", grid=(S // tm,),
            in_specs=[pl.BlockSpec((tm, D), lambda i: (i, 0)), pl.BlockSpec((D, D), lambda i: (0, 0))],
            out_specs=pl.BlockSpec((tm, D), lambda i: (i, 0)),
            out_shape=jax.ShapeDtypeStruct((S, D), F32),
            compiler_params=_cp(("parallel",)),
        )(dx2b, wout), None

    w = rider[0]
    r, c = w.shape
    rt = _row_tile(r, c)
    nt = r // rt
    assert nt <= S // tm

    def body(pos_ref, dx_ref, w_ref, uw, um, uv, ug, us, uc, o_ref, go, dd, mo, vo):
        o_ref[...] = _dot_nt(dx_ref[...], w_ref[...])

        @pl.when(pl.program_id(0) < nt)
        def _():
            _update_tile(uw, um, uv, ug, us, uc, go, dd, mo, vo)

    def at(i):
        return jnp.minimum(i, nt - 1)

    tile = pl.BlockSpec((rt, c), lambda i, pos: (at(i), 0))
    outs = pl.pallas_call(
        body, name="out_proj_bwd",
        grid_spec=pltpu.PrefetchScalarGridSpec(
            num_scalar_prefetch=1, grid=(S // tm,),
            in_specs=[pl.BlockSpec((tm, D), lambda i, pos: (i, 0)), pl.BlockSpec((D, D), lambda i, pos: (0, 0)),
                      tile, tile, tile,
                      pl.BlockSpec((None, rt, c), lambda i, pos: (4 * pos[0] + 2 * pos[1] + pos[2], at(i), 0)),
                      pl.BlockSpec((None, rt, c), lambda i, pos: (2 * pos[0] + pos[1], at(i), 0)),
                      pl.BlockSpec((3, rt, c), lambda i, pos: (0, at(i), 0))],
            out_specs=[pl.BlockSpec((tm, D), lambda i, pos: (i, 0)), tile, tile, tile, tile]),
        out_shape=[jax.ShapeDtypeStruct((S, D), F32)] + [jax.ShapeDtypeStruct((r, c), F32)] * 4,
        compiler_params=_cp(("arbitrary",)),
    )(place, dx2b, wout, *rider)
    return outs[0], outs[1:]


def _in_proj_bwd(dproj, wins, dres, xs, r, nw):
    tm = 1024

    nr = len(wins)
    nm = S // tm

    def body(dp_ref, *rest):
        w_hbms = rest[:nr]
        dres_hbm, x_hbm, r_ref, nw_ref, dx_ref, st_ref, w_buf, dres_buf, x_buf, sems, w_sems = rest[nr:]
        m, p = pl.program_id(0), pl.program_id(1)
        tail_in = _row_copies((dres_hbm, x_hbm), (dres_buf, x_buf), sems, m, tm)

        def w_copies(g, slot):
            return [pltpu.make_async_copy(w_hbm.at[g], w_buf.at[slot, pl.ds(0, D), pl.ds(off, width)],
                                          w_sems.at[slot, k])
                    for k, (w_hbm, (off, width)) in enumerate(zip(w_hbms, IN_ROUNDS))]

        @pl.when((p == 0) & (m == 0))
        def _():
            st_ref[...] = jnp.zeros_like(st_ref)
            for cp in w_copies(p, 0):
                cp.start()

        @pl.when((p < NDEV - 1) | (m < nm - 1))
        def _():
            for cp in w_copies((p + 1) % NDEV, (p + 1) % 2):
                cp.start()

        @pl.when(p == 0)
        def _():
            dx_ref[...] = jnp.zeros_like(dx_ref)
            for cp in tail_in:
                cp.start()

        for cp in w_copies(p, p % 2):
            cp.wait()
        dx_ref[...] += _dot_nt(dp_ref[...], w_buf[p % 2])

        @pl.when(p == NDEV - 1)
        def _():
            for cp in tail_in:
                cp.wait()
            dx, dnw = _rms_bwd_tile(dx_ref[...], x_buf[...], r_ref[...], nw_ref[...])
            dx_ref[...] = dres_buf[...] + dx
            st_ref[0:1, :] += dnw

    row = pl.BlockSpec((tm, D), lambda m, p: (m, 0))
    hbm = pl.BlockSpec(memory_space=pl.ANY)
    return pl.pallas_call(
        body, name="in_proj_bwd", grid=(S // tm, NDEV),
        in_specs=[pl.BlockSpec((tm, N_IN), lambda m, p: (m, p)),
                  *[hbm] * nr,
                  hbm, hbm, pl.BlockSpec((tm, 1), lambda m, p: (m, 0)),
                  pl.BlockSpec((1, D), lambda m, p: (0, 0))],
        out_specs=[row, pl.BlockSpec((8, D), lambda m, p: (0, 0))],
        out_shape=[jax.ShapeDtypeStruct((S, D), F32), jax.ShapeDtypeStruct((8, D), F32)],
        scratch_shapes=[pltpu.VMEM((2, D, N_IN), BF16), pltpu.VMEM((tm, D), F32), pltpu.VMEM((tm, D), F32),
                        pltpu.SemaphoreType.DMA((2,)), pltpu.SemaphoreType.DMA((2, nr))],
        compiler_params=_cp(("arbitrary", "arbitrary")),
    )(dproj, *wins, dres, xs, r, nw)


W_IN_PARTS = 2


def _wgrad_in(h1, dproj, part):
    rows = D // W_IN_PARTS

    def body(a_ref, d_ref, o_ref):
        both = _dot_tn(a_ref[...], d_ref[...]).astype(BF16)
        o_ref[0] = both[:, 0:N_IN]
        o_ref[1] = both[:, N_IN:2 * N_IN]

    return pl.pallas_call(
        body, name=f"wgrad_in_{part}", grid=(NDEV // 2,),
        in_specs=[pl.BlockSpec((S, rows), lambda p: (0, part)), pl.BlockSpec((S, 2 * N_IN), lambda p: (0, p))],
        out_specs=pl.BlockSpec((2, rows, N_IN), lambda p: (p, 0, 0)),
        out_shape=jax.ShapeDtypeStruct((NDEV, rows, N_IN), BF16),
        compiler_params=_cp(("parallel",)),
    )(h1, dproj)


def _wgrad_rows(a3, dy, name, col=0):
    def body(a_ref, d_ref, o_ref):
        dw = _dot_tn(a_ref[...], d_ref[...]).astype(BF16)
        for j in range(FF_PER):
            o_ref[j] = dw[j * FF_ROWS:(j + 1) * FF_ROWS]

    return pl.pallas_call(
        body, name=name, grid=(NFG,),
        in_specs=[pl.BlockSpec((None, S, N_FG), lambda p: (p, 0, col)), pl.BlockSpec((S, D), lambda p: (0, 0))],
        out_specs=pl.BlockSpec((FF_PER, FF_ROWS, D), lambda p: (p % 2, p // 2, 0)),
        out_shape=jax.ShapeDtypeStruct((NDEV, N_FF, D), BF16),
        compiler_params=_cp(("parallel",)),
    )(a3, dy)


def _wgrad_out(ma, mr, dx2b):
    half = D // 2
    per = half // N_OUT

    def body(ma_ref, mr_ref, d_ref, o_ref):
        p = pl.program_id(0)

        @pl.when(p == 0)
        def _():
            o_ref[...] = _dot_tn(ma_ref[...], d_ref[...]).astype(BF16).reshape(per, N_OUT, D)

        @pl.when(p == 1)
        def _():
            o_ref[...] = _dot_tn(mr_ref[...], d_ref[...]).astype(BF16).reshape(per, N_OUT, D)

    whole = pl.BlockSpec((S, half), lambda p: (0, 0))
    return pl.pallas_call(
        body, name="wgrad_out", grid=(2,),
        in_specs=[whole, whole, pl.BlockSpec((S, D), lambda p: (0, 0))],
        out_specs=pl.BlockSpec((per, N_OUT, D), lambda p: (p, 0, 0)),
        out_shape=jax.ShapeDtypeStruct((NDEV, N_OUT, D), BF16),
        compiler_params=_cp(("parallel",)),
    )(ma, mr, dx2b)


def _attn_consts():
    c = np.zeros((AH, 8, AHD), np.float32)
    for h in range(AH):
        c[h, :, :] = 2.0 ** (-(h + 1))
    return jnp.asarray(c)


def _permute_in(dst, src, d, cast=None):
    v = src[...]
    if d > 1:
        v = pltpu.einshape("jrc->rjc", v.reshape(S // d, d, AHD)).reshape(S, AHD)
    dst[...] = v if cast is None else v.astype(cast)


def _natural_order(v, d):
    if d == 1:
        return v
    return pltpu.einshape("rjc->jrc", v.reshape(d, S // d, AHD)).reshape(S, AHD)


def _attn_masks():
    qi = lax.broadcasted_iota(jnp.int32, (CH, CH), 0)
    kj = lax.broadcasted_iota(jnp.int32, (CH, CH), 1)
    dist_c = (qi - kj).astype(F32)
    dist_p = (qi - kj + CH).astype(F32)
    return (qi >= kj)[None], (kj >= qi)[None], dist_c[None], dist_p[None]


GB = 16


def _bdot_nt(a, b):
    return lax.dot_general(a, b, (((2,), (2,)), ((0,), (0,))), preferred_element_type=F32)


def _bdot(a, b):
    return lax.dot_general(a, b, (((2,), (1,)), ((0,), (0,))), preferred_element_type=F32)


def _bdot_tn(a, b):
    return lax.dot_general(a, b, (((1,), (1,)), ((0,), (0,))), preferred_element_type=F32)


def _shift_block(dst, src):
    dst[0:CH, :] = jnp.zeros((CH, AHD), dst.dtype)
    dst[CH:S, :] = src[0:S - CH, :]


def _has_prev(g, nb):
    blk = lax.broadcasted_iota(jnp.int32, (GB, 1, 1), 0) + g * GB
    return (blk & (nb - 1)) != 0


def _blocks(ref, g):
    return ref[g * GB * CH:(g + 1) * GB * CH, :].reshape(GB, CH, AHD)


def _attn_fwd(proj):
    scale = 1.0 / math.sqrt(AHD)

    def body(c_ref, q_ref, k_ref, v_ref, o_ref, ob_ref, lse_ref, qkvp_ref, lsep_ref, qd, kd, vd, kps, vps, od, ld, *nat):
        onat, lnat = nat[0:3], nat[3:6]
        slope = c_ref[0:1, :]
        mask_c, mask_p, dist_c, dist_p = _attn_masks()
        for pi, (d, nb) in enumerate(PATTERNS):
            _permute_in(qd, q_ref, d, BF16)
            _permute_in(kd, k_ref, d, BF16)
            _permute_in(vd, v_ref, d, BF16)
            if d > 1:
                qkvp_ref[pi - 1, 0] = qd[...]
                qkvp_ref[pi - 1, 1] = kd[...]
                qkvp_ref[pi - 1, 2] = vd[...]
            if nb > 1:
                _shift_block(kps, kd)
                _shift_block(vps, vd)
            bias_c = -(slope * float(d)) * dist_c
            bias_p = -(slope * float(d)) * dist_p
            for g in range(NB // GB):
                q3, k3, v3 = _blocks(qd, g), _blocks(kd, g), _blocks(vd, g)
                s_c = jnp.where(mask_c, _bdot_nt(q3, k3) * scale + bias_c, NEG)
                mx = jnp.max(s_c, axis=-1, keepdims=True)
                if nb > 1:
                    kp3, vp3 = _blocks(kps, g), _blocks(vps, g)
                    s_p = jnp.where(jnp.logical_and(mask_p, _has_prev(g, nb)),
                                    _bdot_nt(q3, kp3) * scale + bias_p, NEG)
                    mx = jnp.maximum(mx, jnp.max(s_p, axis=-1, keepdims=True))
                    l = (jnp.sum(jnp.exp(s_c - mx), axis=-1, keepdims=True)
                         + jnp.sum(jnp.exp(s_p - mx), axis=-1, keepdims=True))
                    lse = mx + jnp.log(l)
                    o3 = _bdot(jnp.exp(s_c - lse).astype(BF16), v3) + _bdot(jnp.exp(s_p - lse).astype(BF16), vp3)
                else:
                    l = jnp.sum(jnp.exp(s_c - mx), axis=-1, keepdims=True)
                    lse = mx + jnp.log(l)
                    o3 = _bdot(jnp.exp(s_c - lse).astype(BF16), v3)
                rows = slice(g * GB * CH, (g + 1) * GB * CH)
                od[rows, :] = o3.reshape(GB * CH, AHD)
                ld[rows, :] = jnp.broadcast_to(lse, (GB, CH, AHD)).reshape(GB * CH, AHD)
            onat[pi][...] = _natural_order(od[...], d)
            lnat[pi][...] = _natural_order(ld[...], d)
        l0, l1, l2 = lnat[0][...], lnat[1][...], lnat[2][...]
        mx = jnp.maximum(jnp.maximum(l0, l1), l2)
        e0, e1, e2 = jnp.exp(l0 - mx), jnp.exp(l1 - mx), jnp.exp(l2 - mx)
        den = e0 + e1 + e2
        out = (e0 / den) * onat[0][...] + (e1 / den) * onat[1][...] + (e2 / den) * onat[2][...]
        o_ref[...] = out
        ob_ref[...] = out.astype(BF16)
        lse_ref[...] = mx + jnp.log(den)
        for pi, (d, _) in enumerate(PATTERNS[1:]):
            _permute_in(lsep_ref.at[pi], lse_ref, d)

    def col(off):
        return pl.BlockSpec((S, AHD), lambda h: (0, off + h))

    return pl.pallas_call(
        body, name="attn_fwd", grid=(AH,),
        in_specs=[pl.BlockSpec((None, 8, AHD), lambda h: (h, 0, 0)), col(0), col(AH), col(2 * AH)],
        out_specs=[col(0), col(0), col(0), pl.BlockSpec((2, 3, S, AHD), lambda h: (0, 0, 0, h)),
                   pl.BlockSpec((2, S, AHD), lambda h: (0, 0, h))],
        out_shape=[jax.ShapeDtypeStruct((S, AH * AHD), F32), jax.ShapeDtypeStruct((S, AH * AHD), BF16),
                   jax.ShapeDtypeStruct((S, AH * AHD), F32),
                   jax.ShapeDtypeStruct((2, 3, S, AH * AHD), BF16), jax.ShapeDtypeStruct((2, S, AH * AHD), F32)],
        scratch_shapes=[pltpu.VMEM((S, AHD), BF16) for _ in range(5)]
        + [pltpu.VMEM((S, AHD), F32) for _ in range(8)],
        compiler_params=_cp(("parallel",)),
    )(_attn_consts(), proj, proj, proj)


def _attn_bwd(proj, dmixed, o, lse, qkvp, lsep):
    scale = 1.0 / math.sqrt(AHD)

    def body(c_ref, q_ref, k_ref, v_ref, do_ref, o_ref, lse_ref, qkvp_ref, lsep_ref, dproj_hbm,
             qd, kd, vd, dod, kps, vps, dld, dqd, dkd, dvd, delta, aq, ak, av, sq, sk, sv, sems):
        h = pl.program_id(0)

        def out_copies(head):
            return [pltpu.make_async_copy(
                st, dproj_hbm.at[:, pl.ds(pl.multiple_of((k * AH + head) * AHD, AHD), AHD)], sems.at[k])
                for k, st in enumerate((sq, sk, sv))]

        slope = c_ref[0:1, :]
        mask_c, mask_p, dist_c, dist_p = _attn_masks()
        delta[...] = jnp.broadcast_to(jnp.sum(do_ref[...] * o_ref[...], axis=-1, keepdims=True), (S, AHD))
        for pi, (d, nb) in enumerate(PATTERNS):
            if d == 1:
                _permute_in(qd, q_ref, d, BF16)
                _permute_in(kd, k_ref, d, BF16)
                _permute_in(vd, v_ref, d, BF16)
                qs, ks, vs, lss = qd, kd, vd, lse_ref
            else:
                qs, ks, vs, lss = (qkvp_ref.at[pi - 1, 0], qkvp_ref.at[pi - 1, 1], qkvp_ref.at[pi - 1, 2],
                                   lsep_ref.at[pi - 1])
            _permute_in(dod, do_ref, d, BF16)
            _permute_in(dld, delta, d)
            if nb > 1:
                _shift_block(kps, ks)
                _shift_block(vps, vs)
            bias_c = -(slope * float(d)) * dist_c
            bias_p = -(slope * float(d)) * dist_p
            for g in range(NB // GB):
                q3, k3, v3, do3 = _blocks(qs, g), _blocks(ks, g), _blocks(vs, g), _blocks(dod, g)
                ls, dl = _blocks(lss, g), _blocks(dld, g)
                lo, hi = g * GB * CH, (g + 1) * GB * CH
                p_c = jnp.exp(jnp.where(mask_c, _bdot_nt(q3, k3) * scale + bias_c, NEG) - ls)
                ds_c = ((p_c * (_bdot_nt(do3, v3) - dl)) * scale).astype(BF16)
                dq3 = _bdot(ds_c, k3)
                dkd[lo:hi, :] = _bdot_tn(ds_c, q3).reshape(GB * CH, AHD)
                dvd[lo:hi, :] = _bdot_tn(p_c.astype(BF16), do3).reshape(GB * CH, AHD)
                if nb > 1:
                    kp3, vp3 = _blocks(kps, g), _blocks(vps, g)
                    p_p = jnp.exp(jnp.where(jnp.logical_and(mask_p, _has_prev(g, nb)),
                                            _bdot_nt(q3, kp3) * scale + bias_p, NEG) - ls)
                    ds_p = ((p_p * (_bdot_nt(do3, vp3) - dl)) * scale).astype(BF16)
                    dq3 = dq3 + _bdot(ds_p, kp3)
                    dkp = _bdot_tn(ds_p, q3).reshape(GB * CH, AHD)
                    dvp = _bdot_tn(p_p.astype(BF16), do3).reshape(GB * CH, AHD)
                    if g == 0:
                        dkd[0:hi - CH, :] += dkp[CH:, :]
                        dvd[0:hi - CH, :] += dvp[CH:, :]
                    else:
                        dkd[lo - CH:hi - CH, :] += dkp
                        dvd[lo - CH:hi - CH, :] += dvp
                dqd[lo:hi, :] = dq3.reshape(GB * CH, AHD)
            ln = S // d
            for acc, src in ((aq, dqd), (ak, dkd), (av, dvd)):
                if pi == 0:
                    acc[...] = src[...]
                else:
                    acc[...] += _natural_order(src[...], d)

        @pl.when(h > 0)
        def _():
            for cp in out_copies(h - 1):
                cp.wait()

        sq[...] = aq[...].astype(BF16)
        sk[...] = ak[...].astype(BF16)
        sv[...] = av[...].astype(BF16)
        for cp in out_copies(h):
            cp.start()

        @pl.when(h == AH - 1)
        def _():
            for cp in out_copies(h):
                cp.wait()

    def col(off):
        return pl.BlockSpec((S, AHD), lambda h: (0, off + h))

    return pl.pallas_call(
        body, name="attn_bwd", grid=(AH,),
        in_specs=[pl.BlockSpec((None, 8, AHD), lambda h: (h, 0, 0)), col(0), col(AH), col(2 * AH),
                  col(0), col(0), col(0), pl.BlockSpec((2, 3, S, AHD), lambda h: (0, 0, 0, h)),
                  pl.BlockSpec((2, S, AHD), lambda h: (0, 0, h))],
        out_specs=pl.BlockSpec(memory_space=pl.ANY),
        out_shape=jax.ShapeDtypeStruct((S, NDEV * N_IN), BF16),
        scratch_shapes=[pltpu.VMEM((S, AHD), BF16) for _ in range(6)]
        + [pltpu.VMEM((S, AHD), F32) for _ in range(8)]
        + [pltpu.VMEM((S, AHD), BF16) for _ in range(3)] + [pltpu.SemaphoreType.DMA((3,))],
        compiler_params=_cp(("arbitrary",)),
    )(_attn_consts(), proj, proj, proj, dmixed, o, lse, qkvp, lsep)


def _ret_consts():
    c = np.zeros((RH, 8, RHD), np.float32)
    for h in range(RH):
        c[h, :, :] = np.log(np.float32(1.0) - np.float32(2.0 ** (-5.0 - h)))
    return jnp.asarray(c)


def _ret_factors(lg):
    i = lax.broadcasted_iota(jnp.int32, (CH, CH), 0)
    j = lax.broadcasted_iota(jnp.int32, (CH, CH), 1)
    dif = (i - j).astype(F32)
    decay = jnp.where(dif >= 0, jnp.exp(lg[:, 0:CH] * jnp.maximum(dif, 0.0)), 0.0)
    row = lax.broadcasted_iota(jnp.int32, (CH, RHD), 0).astype(F32)
    zeta = jnp.exp(lg * (CH - 1.0 - row))
    xi = jnp.exp(lg * (row + 1.0))
    return decay, zeta, xi, jnp.exp(lg * float(CH))


CBK = 8
RSTEPS = NB // CBK


def _ret_specs(rev):
    off = 3 * AH * AHD // RHD
    rows = CBK * CH

    def ch(n):
        return (RSTEPS - 1 - n) if rev else n

    def col(k):
        return pl.BlockSpec((rows, RHD), lambda h, n: (ch(n), off + k * RH + h))

    own = pl.BlockSpec((rows, RHD), lambda h, n: (ch(n), h))
    state = pl.BlockSpec((None, CBK, RHD, RHD), lambda h, n: (h, ch(n), 0, 0))
    const = pl.BlockSpec((None, 8, RHD), lambda h, n: (h, 0, 0))
    dm = pl.BlockSpec((rows, RHD), lambda h, n: (ch(n), AH * AHD // RHD + h))
    return col, own, state, const, dm


def _chunks(x):
    return x.reshape(CBK, CH, RHD)


def _ret_fwd(proj):
    def body(c_ref, q_ref, k_ref, v_ref, g_ref, ret_ref, mr_ref, st_ref, r_acc):
        n = pl.program_id(1)

        @pl.when(n == 0)
        def _():
            r_acc[...] = jnp.zeros_like(r_acc)

        decay, zeta, xi, gch = _ret_factors(c_ref[0:1, :])
        q3 = _chunks(q_ref[...].astype(BF16))
        kc = _chunks(k_ref[...] * (1.0 / math.sqrt(RHD)))
        k3 = kc.astype(BF16)
        v3 = _chunks(v_ref[...].astype(BF16))
        kv3 = _bdot_tn((kc * zeta[None]).astype(BF16), v3)
        r = r_acc[...]
        for i in range(CBK):
            st_ref[i] = r.astype(BF16)
            r = r * gch + kv3[i]
        r_acc[...] = r
        scores = _bdot_nt(q3, k3) * decay[None]
        ret = (_bdot(scores.astype(BF16), v3) + _bdot(q3, st_ref[...]) * xi[None]).reshape(CBK * CH, RHD)
        ret_ref[...] = ret
        rr = lax.rsqrt(jnp.mean(ret * ret, axis=-1, keepdims=True) + EPS)
        gv = g_ref[...]
        mr_ref[...] = ((gv * _sigmoid(gv)) * (ret * rr)).astype(BF16)

    col, own, state, const, _ = _ret_specs(False)
    return pl.pallas_call(
        body, name="ret_fwd", grid=(RH, RSTEPS),
        in_specs=[const, col(0), col(1), col(2), col(3)],
        out_specs=[own, own, state],
        out_shape=[jax.ShapeDtypeStruct((S, RH * RHD), F32), jax.ShapeDtypeStruct((S, RH * RHD), BF16),
                   jax.ShapeDtypeStruct((RH, NB, RHD, RHD), BF16)],
        scratch_shapes=[pltpu.VMEM((RHD, RHD), F32)],
        compiler_params=_cp(("parallel", "arbitrary")),
    )(_ret_consts(), proj, proj, proj, proj)


def _ret_bwd(proj, ret, states, dmixed, dproj):
    rows = CBK * CH
    col0 = 3 * AH * AHD

    def body(c_ref, q_ref, k_ref, v_ref, g_ref, ret_ref, st_ref, dm_ref, dproj_in, dproj_hbm, g_acc, gs,
             sq, sk, sv, sg, sems):
        del dproj_in
        h, n = pl.program_id(0), pl.program_id(1)
        step = h * RSTEPS + n

        def out_copies(t):
            hh, nn = t // RSTEPS, t % RSTEPS
            r0 = pl.multiple_of((RSTEPS - 1 - nn) * rows, rows)
            return [pltpu.make_async_copy(
                st, dproj_hbm.at[pl.ds(r0, rows), pl.ds(pl.multiple_of(col0 + (k * RH + hh) * RHD, RHD), RHD)],
                sems.at[k]) for k, st in enumerate((sq, sk, sv, sg))]

        @pl.when(n == 0)
        def _():
            g_acc[...] = jnp.zeros_like(g_acc)

        decay, zeta, xi, gch = _ret_factors(c_ref[0:1, :])
        ret_v = ret_ref[...]
        rr = lax.rsqrt(jnp.mean(ret_v * ret_v, axis=-1, keepdims=True) + EPS)
        gv = g_ref[...]
        sgm = _sigmoid(gv)
        dmix = dm_ref[...]
        dgate = ((dmix * (ret_v * rr)) * (sgm * (1.0 + gv * (1.0 - sgm)))).astype(BF16)
        dretn = dmix * (gv * sgm)
        dret = _chunks(rr * dretn - ret_v * ((rr * rr * rr) * jnp.mean(dretn * ret_v, axis=-1, keepdims=True)))

        q3 = _chunks(q_ref[...].astype(BF16))
        kc = _chunks(k_ref[...] * (1.0 / math.sqrt(RHD)))
        k3 = kc.astype(BF16)
        v3 = _chunks(v_ref[...].astype(BF16))
        d3 = dret.astype(BF16)
        dxi = (dret * xi[None]).astype(BF16)
        kz = (kc * zeta[None]).astype(BF16)
        dr3 = _bdot_tn(q3, dxi)
        acc = g_acc[...]
        for i in reversed(range(CBK)):
            gs[i] = acc.astype(BF16)
            acc = dr3[i] + gch * acc
        g_acc[...] = acc
        g3 = gs[...]
        sc = (_bdot_nt(q3, k3) * decay[None]).astype(BF16)
        da = (_bdot_nt(d3, v3) * decay[None]).astype(BF16)
        dq = _bdot(da, k3) + _bdot_nt(dxi, st_ref[...])
        dkc = _bdot_tn(da, q3) + _bdot_nt(v3, g3) * zeta[None]
        dv = _bdot_tn(sc, d3) + _bdot(kz, g3)

        @pl.when(step > 0)
        def _():
            for cp in out_copies(step - 1):
                cp.wait()

        sq[...] = dq.reshape(rows, RHD).astype(BF16)
        sk[...] = (dkc * (1.0 / math.sqrt(RHD))).reshape(rows, RHD).astype(BF16)
        sv[...] = dv.reshape(rows, RHD).astype(BF16)
        sg[...] = dgate
        for cp in out_copies(step):
            cp.start()

        @pl.when(step == RH * RSTEPS - 1)
        def _():
            for cp in out_copies(step):
                cp.wait()

    col, own, state, const, dm = _ret_specs(True)
    hbm = pl.BlockSpec(memory_space=pl.ANY)
    return pl.pallas_call(
        body, name="ret_bwd", grid=(RH, RSTEPS),
        in_specs=[const, col(0), col(1), col(2), col(3), own, state, dm, hbm],
        out_specs=hbm,
        out_shape=jax.ShapeDtypeStruct(dproj.shape, dproj.dtype),
        input_output_aliases={8: 0},
        scratch_shapes=[pltpu.VMEM((RHD, RHD), F32), pltpu.VMEM((CBK, RHD, RHD), BF16)]
        + [pltpu.VMEM((rows, RHD), BF16) for _ in range(4)] + [pltpu.SemaphoreType.DMA((4,))],
        compiler_params=_cp(("arbitrary", "arbitrary")),
    )(_ret_consts(), proj, proj, proj, proj, ret, states, dmixed, dproj)


class _NoReduction:
    def start(self, group, grads):
        pass

    def local(self, name, first=()):
        return []

    def landed(self, name):
        return []

    def update(self, name):
        return []

    place = None

    def rider(self, name):
        return None

    def set_update(self, name, outs):
        pass


def _local_step(x, tgt, nw1, nw2, nw3, win, wout, wgu_a, wgu_b, wd_a, wd_b, red=None):
    red = red or _NoReduction()

    def after(values, first):
        return lax.optimization_barrier((tuple(values), tuple(first)))[0]

    h1, r1 = _rms_fwd(x, nw1)
    proj = _proj(h1, win)
    o, ma, lse, qkvp, lsep = _attn_fwd(proj)
    ret, mr, states = _ret_fwd(proj)
    x2, h2, r2 = _out_proj_rms(x, ma, mr, wout, nw2)
    a, dadg, dadu = _ffn_up(h2, wgu_b, 1, _ffn_up(h2, wgu_a, 0))
    dx3, dx3b, st3 = _ffn_down_loss(_ffn_down_first(x2, a, wd_a), a, wd_b, nw3, tgt)

    dwd = _wgrad_rows(a, dx3b, "wgrad_down")
    red.start(["w_down"], [dwd])
    (dx3b,) = after([dx3b], [dwd])
    part = _ffn_down_bwd(dx3b, wd_a, dadg, dadu, 0)
    (dx3b,) = after([dx3b], red.local("w_down", first=[part]))
    dgu = _ffn_down_bwd(dx3b, wd_b, dadg, dadu, 1, [part])
    dwg = _wgrad_rows(dgu, h2, "wgrad_gate", 0)
    red.start(["w_gate"], [dwg])
    (dgu,) = after([dgu], [dwg])
    dwu = _wgrad_rows(dgu, h2, "wgrad_up", 1)
    red.start(["w_up"], [dwu])
    (dgu,) = after([dgu], red.local("w_gate", first=[dwu] + red.landed("w_down")))
    dx2, dx2b, st2 = _ffn_up_bwd(dgu, wgu_a, wgu_b, dx3, x2, r2, nw2)
    (dx2b,) = after([dx2b], red.local("w_up", first=[dx2b]))
    dwo = _wgrad_out(ma, mr, dx2b)
    red.start(["w_out"], [dwo])
    (dx2b,) = after([dx2b], [dwo])
    dmixed, done = _out_proj_bwd(dx2b, wout, red.place, red.rider("w_down"))
    red.set_update("w_down", done)
    dproj = _attn_bwd(proj, dmixed, o, lse, qkvp, lsep)
    (dmixed,) = after([dmixed], red.local("w_out", first=[dproj] + red.landed("w_gate")))
    dproj = _ret_bwd(proj, ret, states, dmixed, dproj)
    (dwi0,) = after([_wgrad_in(h1, dproj, 0)], red.landed("w_up"))
    red.start(["w_in_0"], [dwi0])
    (dproj,) = after([dproj], [dwi0])
    dwi1 = _wgrad_in(h1, dproj, 1)
    red.start(["w_in_1"], [dwi1])
    sums = red.local("w_in_0", first=[dwi1] + red.landed("w_out"))
    sums = red.local("w_in_1", first=sums + red.update("w_gate"))
    (dproj,) = after([dproj], sums)
    gx, st1 = _in_proj_bwd(dproj, win, dx2, x, r1, nw1)
    dwi = jnp.concatenate([dwi0, dwi1], axis=1)
    stats = jnp.concatenate([st1[0:1], st2[0:1], st3[0:2], jnp.zeros((4, D), F32)], axis=0)
    return stats, gx, dwi, dwo, dwg, dwu, dwd


def _place():
    x, y, c = lax.axis_index("x"), lax.axis_index("y"), lax.axis_index("c")
    return x, y, c, [(1 - x, y), (x, 1 - y), (1 - x, 1 - y)]


def _handshake(peers):
    barrier = pltpu.get_barrier_semaphore()
    for peer in peers:
        pl.semaphore_signal(barrier, inc=1, device_id=peer, device_id_type=MESH)
    pl.semaphore_wait(barrier, len(peers))


def _all_gather(shards, name, collective_id, per=0, rows=None):
    na = len(shards)
    nout = 1 if per else na
    lo, r = rows or (0, shards[0].shape[0])
    ngroups = NDEV // per if per else 0
    SIB, XN0, XN1, YN1, YN0, VIA_X, VIA_Y = 0, 1, 2, 3, 4, 5, 6
    D2D = {XN0: 7, XN1: 8, YN1: 9, YN0: 10, VIA_X: 11, VIA_Y: 12}

    def body(*refs):
        ins, outs = [ref.at[pl.ds(lo, r)] for ref in refs[:na]], refs[na:na + nout]
        send_sems, recv_sems, local_sems = refs[na + nout:]
        x, y, c, _ = _place()
        me, sib = (x, y, c), (x, y, 1 - c)
        xn, yn, dg = (1 - x, y, c), (x, 1 - y, c), (1 - x, 1 - y, c)
        _handshake([sib, xn, yn])

        def part(ref, h):
            rows = ref.shape[0] // 2
            return ref if h is None else ref.at[pl.ds(h * rows, rows)]

        def block(a, owner, h):
            idx = 4 * owner[0] + 2 * owner[1] + owner[2]
            if not per:
                return part(outs[a].at[idx], h)
            return part(outs[0].at[idx // per, a, pl.ds(pl.multiple_of((idx % per) * r, r), r)], h)

        def copy(a, k, owner, h, to, own_src=False):
            return pltpu.make_async_remote_copy(
                src_ref=part(ins[a], h) if own_src else block(a, owner, h), dst_ref=block(a, owner, h),
                send_sem=send_sems.at[a, k], recv_sem=recv_sems.at[a, k], device_id=to, device_id_type=MESH)

        def other(p):
            return (p[0], p[1], 1 - c)

        mine = [pltpu.make_async_copy(ins[a], block(a, me, None), local_sems.at[a]) for a in range(na)]
        for cp in mine:
            cp.start()
        sent = []
        for a in range(na):
            sent += [copy(a, XN0, me, 0, xn, True), copy(a, YN1, me, 1, yn, True),
                     copy(a, XN1, me, 1, xn, True), copy(a, YN0, me, 0, yn, True)]
        sent += [copy(a, SIB, me, None, sib, True) for a in range(na)]
        for cp in sent:
            cp.start()

        def landed(a, k, owner, h, then):
            copy(a, k, owner, h, me).wait_recv()
            for k2, to in then + [(D2D[k], sib)]:
                cp = copy(a, k2, owner, h, to)
                cp.start()
                sent.append(cp)

        for a in range(na):
            landed(a, XN0, xn, 0, [(VIA_Y, yn)])
            landed(a, YN1, yn, 1, [(VIA_X, xn)])
            landed(a, XN1, xn, 1, [])
            landed(a, YN0, yn, 0, [])
        for a in range(na):
            landed(a, VIA_Y, dg, 0, [])
            landed(a, VIA_X, dg, 1, [])
        for a in range(na):
            copy(a, SIB, sib, None, me).wait_recv()
            for k, owner, h in ((XN0, xn, 0), (XN1, xn, 1), (YN1, yn, 1), (YN0, yn, 0), (VIA_Y, dg, 0), (VIA_X, dg, 1)):
                copy(a, D2D[k], other(owner), h, me).wait_recv()
        for cp in sent:
            cp.wait_send()
        for cp in mine:
            cp.wait()

    if per:
        out_type = [jax.ShapeDtypeStruct((ngroups, na, per * r, shards[0].shape[1]), shards[0].dtype)]
    else:
        out_type = [jax.ShapeDtypeStruct((NDEV,) + s.shape, s.dtype) for s in shards]
    return _sequencer_call(
        body, name, collective_id, out_type,
        [pltpu.SemaphoreType.DMA((na, 13)), pltpu.SemaphoreType.DMA((na, 13)), pltpu.SemaphoreType.DMA((na,))])(*shards)


def _sequencer_call(body, name, collective_id, out_type, scratch_types):
    return pl.kernel(
        body, name=name, out_type=out_type,
        mesh=plsc.ScalarSubcoreMesh(axis_name="sequencer", num_cores=1),
        scratch_types=scratch_types,
        compiler_params=pltpu.CompilerParams(collective_id=collective_id))


def _exchange_sibling(grads, name, collective_id):
    na = len(grads)

    def body(*refs):
        ins, outs = refs[:na], refs[na:2 * na]
        send_sems, recv_sems = refs[2 * na:]
        x, y, c, _ = _place()
        _handshake([(x, y, 1 - c)])
        cps = []
        for a in range(na):
            for k in range(4):
                cps.append(pltpu.make_async_remote_copy(
                    src_ref=ins[a].at[2 * k + (1 - c)], dst_ref=outs[a].at[k],
                    send_sem=send_sems.at[a, k], recv_sem=recv_sems.at[a, k],
                    device_id=(x, y, 1 - c), device_id_type=MESH))
        for cp in cps:
            cp.start()
        for cp in cps:
            cp.wait()

    return _sequencer_call(
        body, name, collective_id,
        [jax.ShapeDtypeStruct((4,) + g.shape[1:], g.dtype) for g in grads],
        [pltpu.SemaphoreType.DMA((na, 4)), pltpu.SemaphoreType.DMA((na, 4))])(*grads)


def _row_tile(rows, cols):
    for t in (512, 256, 176, 128, 64, 32, 16):
        if rows % t == 0 and t * cols * 4 <= (2 << 20):
            return t
    raise ValueError((rows, cols))


STREAM_BUFS = 3


def _stream_tile(rows, steps):
    for t in (512, 256, 176, 128, 64, 32, 16):
        if rows % t == 0 and rows // t >= steps:
            return t
    raise ValueError((rows, steps))


def _stream(n, loads, stores, compute):
    for k in range(min(STREAM_BUFS, n)):
        for cp in loads(k):
            cp.start()
    for k in range(n):
        for cp in loads(k):
            cp.wait()
        if k >= 2:
            for cp in stores(k - 2):
                cp.wait()
        compute(k)
        for cp in stores(k):
            cp.start()
        if k + STREAM_BUFS < n:
            for cp in loads(k + STREAM_BUFS):
                cp.start()
    for k in range(max(n - 2, 0), n):
        for cp in stores(k):
            cp.wait()


def _chip_sum(place, g, got, name):
    _, r, c = g.shape
    tm = _stream_tile(r, 4)
    nt = r // tm

    def body(pos_ref, g_hbm, got_hbm, o_hbm, g_buf, s_buf, o_buf, sem_in, sem_out):
        def chip(j):
            return 2 * (pos_ref[0] ^ (0 if j == 1 else 1)) + (pos_ref[1] ^ (0 if j == 0 else 1))

        def loads(k):
            j, rows, slot = k // nt, pl.ds((k % nt) * tm, tm), k % STREAM_BUFS
            return [pltpu.make_async_copy(g_hbm.at[2 * chip(j) + pos_ref[2], rows], g_buf.at[slot], sem_in.at[slot, 0]),
                    pltpu.make_async_copy(got_hbm.at[chip(j), rows], s_buf.at[slot], sem_in.at[slot, 1])]

        def stores(k):
            return [pltpu.make_async_copy(o_buf.at[k % 2], o_hbm.at[k // nt, pl.ds((k % nt) * tm, tm)],
                                          sem_out.at[k % 2])]

        def compute(k):
            slot = k % STREAM_BUFS
            o_buf[k % 2] = (g_buf[slot].astype(F32) + s_buf[slot].astype(F32)).astype(BF16)

        _stream(3 * nt, loads, stores, compute)

    hbm = pl.BlockSpec(memory_space=pl.ANY)
    return pl.pallas_call(
        body, name=name,
        grid_spec=pltpu.PrefetchScalarGridSpec(
            num_scalar_prefetch=1, grid=(1,), in_specs=[hbm, hbm], out_specs=hbm,
            scratch_shapes=[pltpu.VMEM((STREAM_BUFS, tm, c), BF16), pltpu.VMEM((STREAM_BUFS, tm, c), BF16),
                            pltpu.VMEM((2, tm, c), BF16),
                            pltpu.SemaphoreType.DMA((STREAM_BUFS, 2)), pltpu.SemaphoreType.DMA((2,))]),
        out_shape=jax.ShapeDtypeStruct((3, r, c), BF16),
        compiler_params=_cp(("arbitrary",)),
    )(place, g, got)


def _exchange_chips(sums, name, collective_id):
    na = len(sums)

    def body(*refs):
        ins, outs = refs[:na], refs[na:2 * na]
        send_sems, recv_sems = refs[2 * na:]
        x, y, c, chips = _place()
        _handshake([(*chip, c) for chip in chips])
        cps = []
        for a in range(na):
            for j, chip in enumerate(chips):
                cps.append(pltpu.make_async_remote_copy(
                    src_ref=ins[a].at[j], dst_ref=outs[a].at[j],
                    send_sem=send_sems.at[a, j], recv_sem=recv_sems.at[a, j],
                    device_id=(*chip, c), device_id_type=MESH))
        for cp in cps:
            cp.start()
        for cp in cps:
            cp.wait()

    return _sequencer_call(
        body, name, collective_id,
        [jax.ShapeDtypeStruct((3,) + s.shape[1:], s.dtype) for s in sums],
        [pltpu.SemaphoreType.DMA((na, 3)), pltpu.SemaphoreType.DMA((na, 3))])(*sums)


def _exchange_stats(stats, collective_id):
    def body(st_in, st_out, st_send, st_recv, local_sem):
        x, y, c, _ = _place()
        me_idx = 4 * x + 2 * y + c
        peers = [(x ^ ((k >> 2) & 1), y ^ ((k >> 1) & 1), c ^ (k & 1)) for k in range(1, 8)]
        _handshake(peers)
        mine = pltpu.make_async_copy(st_in, st_out.at[me_idx], local_sem)
        mine.start()
        cps = [pltpu.make_async_remote_copy(
            src_ref=st_in, dst_ref=st_out.at[me_idx], send_sem=st_send.at[k], recv_sem=st_recv.at[k],
            device_id=peer, device_id_type=MESH) for k, peer in enumerate(peers)]
        for cp in cps:
            cp.start()
        for cp in cps:
            cp.wait()
        mine.wait()

    return _sequencer_call(
        body, "exchange_stats", collective_id,
        jax.ShapeDtypeStruct((NDEV,) + stats.shape, stats.dtype),
        [pltpu.SemaphoreType.DMA((7,)), pltpu.SemaphoreType.DMA((7,)), pltpu.SemaphoreType.DMA])(stats)


class _Reduction:
    def __init__(self, place, first_collective_id, state):
        self.place = place
        self.ids = iter(range(first_collective_id, 32))
        self.state = state
        self.groups = {}
        self.updates = {}

    def next_id(self):
        return next(self.ids)

    def start(self, group, grads):
        got = _exchange_sibling(grads, "sibling_exchange_" + group[0], self.next_id())
        self.groups[group[0]] = dict(names=group, grads=grads, got=got)

    def local(self, name, first=()):
        grp = self.groups[name]
        grads = lax.optimization_barrier((tuple(grp["grads"]), tuple(first)))[0]
        grp["sums"] = [_chip_sum(self.place, g, s, "chip_sum_" + n)
                       for g, s, n in zip(grads, grp["got"], grp["names"])]
        grp["chips"] = _exchange_chips(grp["sums"], "chip_exchange_" + name, self.next_id())
        return grp["sums"]

    def landed(self, name):
        return list(self.groups[name]["chips"])

    def rider(self, name):
        grp = next(g for g in self.groups.values() if name in g["names"])
        k = grp["names"].index(name)
        return self.state[name][:3] + (grp["grads"][k], grp["got"][k], grp["chips"][k])

    def set_update(self, name, outs):
        self.updates[name] = list(outs)

    def update(self, name):
        if name not in self.updates:
            grp = next(g for g in self.groups.values() if name in g["names"])
            k = grp["names"].index(name)
            w, m, v, part, parts = self.state[name]
            before = self.update(f"{name[:-1]}{part - 1}") if part else None
            self.updates[name] = _shard_update(self.place, w, m, v, grp["grads"][k], grp["got"][k],
                                               grp["chips"][k], "update_" + name, part, parts, before)
        return list(self.updates[name])


def _adamw(w, g, m, v):
    m = ADAM_B1 * m + (1.0 - ADAM_B1) * g
    v = ADAM_B2 * v + (1.0 - ADAM_B2) * (g * g)
    m_hat = m / (1.0 - ADAM_B1 ** ADAM_STEP)
    v_hat = v / (1.0 - ADAM_B2 ** ADAM_STEP)
    delta = -ADAM_LR * (m_hat / (jnp.sqrt(v_hat) + ADAM_EPS) + ADAM_WD * w)
    return delta, m, v


def _update_tile(w_ref, m_ref, v_ref, g_ref, s_ref, c_ref, go_ref, d_ref, mo_ref, vo_ref):
    grad = g_ref[...].astype(F32) + s_ref[...].astype(F32)
    for j in range(3):
        grad = grad + c_ref[j].astype(F32)
    delta, mn, vn = _adamw(w_ref[...], grad, m_ref[...], v_ref[...])
    go_ref[...] = grad
    d_ref[...] = delta
    mo_ref[...] = mn
    vo_ref[...] = vn


def _shard_update(place, w, m, v, g, got_sib, got_chips, name, part=0, parts=1, before=None):
    r, c = w.shape
    rp = r // parts
    tm = _stream_tile(rp, 8)
    nt = rp // tm
    before = list(before or [])

    def body(pos_ref, w_hbm, m_hbm, v_hbm, g_hbm, s_hbm, c_hbm, *rest):
        outs = rest[len(before):len(before) + 4]
        w_buf, m_buf, v_buf, g_buf, s_buf, c_buf, o_buf, sem_in, sem_out = rest[len(before) + 4:]
        own = 4 * pos_ref[0] + 2 * pos_ref[1] + pos_ref[2]
        chip = 2 * pos_ref[0] + pos_ref[1]

        def loads(k):
            slot, rows, mine = k % STREAM_BUFS, pl.ds(k * tm, tm), pl.ds(part * rp + k * tm, tm)
            pairs = [(w_hbm.at[mine], w_buf), (m_hbm.at[mine], m_buf), (v_hbm.at[mine], v_buf),
                     (g_hbm.at[own, rows], g_buf), (s_hbm.at[chip, rows], s_buf), (c_hbm.at[:, rows], c_buf)]
            return [pltpu.make_async_copy(src, buf.at[slot], sem_in.at[slot, n]) for n, (src, buf) in enumerate(pairs)]

        def stores(k):
            mine = pl.ds(part * rp + k * tm, tm)
            return [pltpu.make_async_copy(o_buf.at[k % 2, n], out.at[mine], sem_out.at[k % 2, n])
                    for n, out in enumerate(outs)]

        def compute(k):
            slot = k % STREAM_BUFS
            _update_tile(w_buf.at[slot], m_buf.at[slot], v_buf.at[slot], g_buf.at[slot], s_buf.at[slot],
                         c_buf.at[slot], *[o_buf.at[k % 2, n] for n in range(4)])

        _stream(nt, loads, stores, compute)

    hbm = pl.BlockSpec(memory_space=pl.ANY)
    return pl.pallas_call(
        body, name=name,
        grid_spec=pltpu.PrefetchScalarGridSpec(
            num_scalar_prefetch=1, grid=(1,), in_specs=[hbm] * (6 + len(before)), out_specs=[hbm] * 4,
            scratch_shapes=[pltpu.VMEM((STREAM_BUFS, tm, c), F32)] * 3 + [pltpu.VMEM((STREAM_BUFS, tm, c), BF16)] * 2
            + [pltpu.VMEM((STREAM_BUFS, 3, tm, c), BF16), pltpu.VMEM((2, 4, tm, c), F32),
               pltpu.SemaphoreType.DMA((STREAM_BUFS, 6)), pltpu.SemaphoreType.DMA((2, 4))]),
        out_shape=[jax.ShapeDtypeStruct((r, c), F32)] * 4,
        input_output_aliases={7 + k: k for k in range(len(before))},
        compiler_params=_cp(("arbitrary",)),
    )(place, w, m, v, g, got_sib, got_chips, *before)


def _small_update(stats_all, ws, ms, vs):
    def body(st_ref, w_ref, m_ref, v_ref, go_ref, d_ref, mo_ref, vo_ref):
        grad = st_ref[0]
        for k in range(1, NDEV):
            grad = grad + st_ref[k]
        delta, mn, vn = _adamw(w_ref[...], grad, m_ref[...], v_ref[...])
        go_ref[...] = grad
        d_ref[...] = delta
        mo_ref[...] = mn
        vo_ref[...] = vn

    return pl.pallas_call(
        body, name="small_update",
        out_shape=[jax.ShapeDtypeStruct((8, D), F32)] * 4,
        compiler_params=_cp(),
    )(stats_all, ws, ms, vs)


def kernel(x, norm_mix_w, w_in, w_out, norm_ffn_w, w_gate, w_up, w_down, norm_final_w, loss_target, m_norm_mix_w, m_w_in, m_w_out, m_norm_ffn_w, m_w_gate, m_w_up, m_w_down, m_norm_final_w, v_norm_mix_w, v_w_in, v_w_out, v_norm_ffn_w, v_w_gate, v_w_up, v_w_down, v_norm_final_w):
    tr = {"w_gate", "w_up"}
    names = ["w_in", "w_out", "w_gate", "w_up", "w_down"]

    def view(a, n):
        return a[0].T if n in tr else a[0]

    big_w = [view(a, n) for a, n in zip([w_in, w_out, w_gate, w_up, w_down], names)]
    big_m = [view(a, n) for a, n in zip([m_w_in, m_w_out, m_w_gate, m_w_up, m_w_down], names)]
    big_v = [view(a, n) for a, n in zip([v_w_in, v_w_out, v_w_gate, v_w_up, v_w_down], names)]

    shards = [None] + [_cast_bf16(w, "cast_" + n) for w, n in zip(big_w[1:], names[1:])]
    win = [_all_gather([cols], f"all_gather_w_in_{k}", 1 + k)[0]
           for k, cols in enumerate(_cast_cols(big_w[0], "cast_w_in"))]
    (wout,) = _all_gather(shards[1:2], "all_gather_w_out", 3)
    (wgu_a,) = _all_gather(shards[2:4], "all_gather_gate_up_0", 4, per=FF_PER, rows=(0, FF_ROWS))
    (wgu_b,) = _all_gather(shards[2:4], "all_gather_gate_up_1", 5, per=FF_PER, rows=(FF_ROWS, FF_ROWS))
    (wd_a,) = _all_gather(shards[4:5], "all_gather_w_down_0", 6, per=FF_PER, rows=(0, FF_ROWS))
    (wd_b,) = _all_gather(shards[4:5], "all_gather_w_down_1", 7, per=FF_PER, rows=(FF_ROWS, FF_ROWS))
    nw3 = norm_final_w.reshape(1, D)
    place = jnp.stack([lax.axis_index("x"), lax.axis_index("y"), lax.axis_index("c")]).astype(jnp.int32)
    state = {n: (w, m, v, 0, 1) for n, w, m, v in zip(names, big_w, big_m, big_v)}
    for part in range(W_IN_PARTS):
        state[f"w_in_{part}"] = state["w_in"][:3] + (part, W_IN_PARTS)
    red = _Reduction(place, 8, state)
    stats, gx, *_ = _local_step(
        x[0], loss_target[0], norm_mix_w, norm_ffn_w, nw3, win, wout.reshape(D, D),
        wgu_a.reshape(NFG // 2, 2 * N_FG, D), wgu_b.reshape(NFG // 2, 2 * N_FG, D),
        wd_a.reshape(NFG // 2, N_FG, D), wd_b.reshape(NFG // 2, N_FG, D), red)
    stats_all = _exchange_stats(stats, red.next_id())
    upd = [red.update(f"w_in_{W_IN_PARTS - 1}" if n == "w_in" else n) for n in names]
    stats_all = lax.optimization_barrier((stats_all, tuple(upd[0])))[0]

    def rows(a, b, c):
        return jnp.concatenate([a.reshape(1, D), b.reshape(1, D), c.reshape(1, D), jnp.zeros((5, D), F32)], axis=0)

    sg, sd, sm, sv = _small_update(stats_all, rows(norm_mix_w, norm_ffn_w, norm_final_w),
                                   rows(m_norm_mix_w, m_norm_ffn_w, m_norm_final_w),
                                   rows(v_norm_mix_w, v_norm_ffn_w, v_norm_final_w))
    loss = sg[3, 0]

    def outs(k, small):
        big = [(u[k].T if n in tr else u[k])[None] for u, n in zip(upd, names)]
        return [small[0:1], big[0], big[1], small[1:2], big[2], big[3], big[4], small[2]]

    return (loss, gx[None], *outs(0, sg), *outs(1, sd), *outs(2, sm), *outs(3, sv))
```

```python
import math

import numpy as np
import jax
import jax.numpy as jnp
from jax import lax
from jax.experimental import pallas as pl
from jax.experimental.pallas import tpu as pltpu
from jax.experimental.pallas import tpu_sc as plsc

F32 = jnp.float32
BF16 = jnp.bfloat16

S = 2048
D = 2048
NDEV = 8
N_IN = 7168 // NDEV
N_FF = 5632 // NDEV
NFG, N_FG = NDEV // 2, 2 * N_FF
FF_PER, FF_ROWS = 4, N_FF // 2
IN_ROUNDS = ((0, 512), (512, N_IN - 512))
N_OUT = 2048 // NDEV
AH, AHD = 8, 128
RH, RHD = 4, 256
CH = 128
NB = S // CH
EPS = 1e-6
PATTERNS = ((1, 16), (4, 4), (16, 1))
NEG = -1e30
VMEM_LIMIT = 56 * 1024 * 1024

ADAM_LR, ADAM_B1, ADAM_B2, ADAM_EPS, ADAM_WD, ADAM_STEP = 0.001, 0.9, 0.999, 1e-08, 0.01, 10
MESH = pl.DeviceIdType.MESH


def _cp(sem=None):
    return pltpu.CompilerParams(dimension_semantics=sem, vmem_limit_bytes=VMEM_LIMIT)


def _dot(a, b):
    return jnp.dot(a, b, preferred_element_type=F32)


def _dot_nt(a, b):
    return lax.dot_general(a, b, (((1,), (1,)), ((), ())), preferred_element_type=F32)


def _dot_tn(a, b):
    return lax.dot_general(a, b, (((0,), (0,)), ((), ())), preferred_element_type=F32)


def _sigmoid(x):
    return 0.5 * jnp.tanh(0.5 * x) + 0.5


def _cast_bf16(w, name):
    r, c = w.shape
    tm = r if r <= 1024 else 512

    def body(w_ref, o_ref):
        o_ref[...] = w_ref[...].astype(BF16)

    return pl.pallas_call(
        body, name=name, grid=(r // tm,),
        in_specs=[pl.BlockSpec((tm, c), lambda i: (i, 0))],
        out_specs=pl.BlockSpec((tm, c), lambda i: (i, 0)),
        out_shape=jax.ShapeDtypeStruct((r, c), BF16),
        compiler_params=_cp(("parallel",)),
    )(w)


def _rms_fwd(x, nw):
    tm = 256

    def body(x_ref, w_ref, h_ref, r_ref):
        xs = x_ref[...]
        r = lax.rsqrt(jnp.mean(xs * xs, axis=-1, keepdims=True) + EPS)
        h_ref[...] = ((xs * r) * w_ref[...]).astype(BF16)
        r_ref[...] = r

    return pl.pallas_call(
        body, name="rms_fwd", grid=(S // tm,),
        in_specs=[pl.BlockSpec((tm, D), lambda i: (i, 0)), pl.BlockSpec((1, D), lambda i: (0, 0))],
        out_specs=[pl.BlockSpec((tm, D), lambda i: (i, 0)), pl.BlockSpec((tm, 1), lambda i: (i, 0))],
        out_shape=[jax.ShapeDtypeStruct((S, D), BF16), jax.ShapeDtypeStruct((S, 1), F32)],
        compiler_params=_cp(("parallel",)),
    )(x, nw)


def _row_copies(hbm_refs, bufs, sems, m, tm):
    rows = pl.ds(pl.multiple_of(m * tm, tm), tm)
    return [pltpu.make_async_copy(h.at[rows], b, sems.at[i]) for i, (h, b) in enumerate(zip(hbm_refs, bufs))]


def _rms_bwd_tile(dh, xs, r, nw):
    dnw = jnp.sum(dh * (xs * r), axis=0, keepdims=True)
    gy = dh * nw
    dx = r * gy - xs * ((r * r * r) * jnp.mean(gy * xs, axis=-1, keepdims=True))
    return dx, dnw


def _cast_cols(w, name):
    r, c = w.shape
    tm = 512

    def body(w_ref, *o_refs):
        for o_ref, (off, width) in zip(o_refs, IN_ROUNDS):
            o_ref[...] = w_ref[:, off:off + width].astype(BF16)

    return pl.pallas_call(
        body, name=name, grid=(r // tm,),
        in_specs=[pl.BlockSpec((tm, c), lambda i: (i, 0))],
        out_specs=[pl.BlockSpec((tm, width), lambda i: (i, 0)) for _, width in IN_ROUNDS],
        out_shape=[jax.ShapeDtypeStruct((r, width), BF16) for _, width in IN_ROUNDS],
        compiler_params=_cp(("parallel",)),
    )(w)


def _proj_round(h1, win, k, before):
    tm = 1024
    nm = S // tm
    off, width = IN_ROUNDS[k]
    before = [] if before is None else [before]

    def body(a_ref, w_ref, *rest):
        o_hbm, o_buf, sems = rest[-3:]
        p, m = pl.program_id(0), pl.program_id(1)
        t = p * nm + m

        def out_copy(pp, mm, slot):
            cols = pl.ds(pl.multiple_of(pp * N_IN + off, 128), width)
            return pltpu.make_async_copy(o_buf.at[slot], o_hbm.at[pl.ds(pl.multiple_of(mm * tm, tm), tm), cols],
                                         sems.at[slot])

        @pl.when(t >= 2)
        def _():
            out_copy(p, m, t % 2).wait()

        o_buf[t % 2] = _dot(a_ref[...], w_ref[...])
        out_copy(p, m, t % 2).start()

        @pl.when(t == NDEV * nm - 1)
        def _():
            out_copy(p, m, (t + 1) % 2).wait()
            out_copy(p, m, t % 2).wait()

    return pl.pallas_call(
        body, name=f"proj_{k}", grid=(NDEV, nm),
        in_specs=[pl.BlockSpec((tm, D), lambda p, m: (m, 0)),
                  pl.BlockSpec((None, D, width), lambda p, m: (p, 0, 0))]
        + [pl.BlockSpec(memory_space=pl.ANY)] * len(before),
        out_specs=pl.BlockSpec(memory_space=pl.ANY),
        out_shape=jax.ShapeDtypeStruct((S, NDEV * N_IN), F32),
        scratch_shapes=[pltpu.VMEM((2, tm, width), F32), pltpu.SemaphoreType.DMA((2,))],
        input_output_aliases={2: 0} if before else {},
        compiler_params=_cp(("arbitrary", "arbitrary")),
    )(h1, win, *before)


def _proj(h1, wins):
    out = None
    for k, win in enumerate(wins):
        out = _proj_round(h1, win, k, out)
    return out


def _out_proj_rms(x, ma, mr, wout, nw):
    tm = 256
    half = D // 2

    def body(x_ref, ma_ref, mr_ref, w_ref, nw_ref, x2_ref, h_ref, r_ref):
        acc = _dot(ma_ref[...], w_ref[0:half, :]) + _dot(mr_ref[...], w_ref[half:D, :])
        x2 = x_ref[...] + acc
        r = lax.rsqrt(jnp.mean(x2 * x2, axis=-1, keepdims=True) + EPS)
        x2_ref[...] = x2
        h_ref[...] = ((x2 * r) * nw_ref[...]).astype(BF16)
        r_ref[...] = r

    return pl.pallas_call(
        body, name="out_proj_rms", grid=(S // tm,),
        in_specs=[pl.BlockSpec((tm, D), lambda i: (i, 0)),
                  pl.BlockSpec((tm, half), lambda i: (i, 0)),
                  pl.BlockSpec((tm, half), lambda i: (i, 0)),
                  pl.BlockSpec((D, D), lambda i: (0, 0)),
                  pl.BlockSpec((1, D), lambda i: (0, 0))],
        out_specs=[pl.BlockSpec((tm, D), lambda i: (i, 0)), pl.BlockSpec((tm, D), lambda i: (i, 0)),
                   pl.BlockSpec((tm, 1), lambda i: (i, 0))],
        out_shape=[jax.ShapeDtypeStruct((S, D), F32), jax.ShapeDtypeStruct((S, D), BF16),
                   jax.ShapeDtypeStruct((S, 1), F32)],
        compiler_params=_cp(("parallel",)),
    )(x, ma, mr, wout, nw)


def _ffn_up(h2, wgu, part, before=None):
    tm = 512

    def body(h_ref, w_ref, *rest):
        a_ref, dadg_ref, dadu_ref = rest[-3:]
        gu = _dot_nt(h_ref[...], w_ref[...])
        g, u = gu[:, 0:N_FG], gu[:, N_FG:2 * N_FG]
        sg = _sigmoid(g)
        silu = g * sg
        a_ref[...] = (silu * u).astype(BF16)
        dadg_ref[...] = (u * (sg * (1.0 + g * (1.0 - sg)))).astype(BF16)
        dadu_ref[...] = silu.astype(BF16)

    half = NFG // 2
    first = part * half
    before = list(before or [])
    blk = pl.BlockSpec((None, tm, N_FG), lambda p, m: (p + first, m, 0))
    return pl.pallas_call(
        body, name=f"ffn_up_{part}", grid=(half, S // tm),
        in_specs=[pl.BlockSpec((tm, D), lambda p, m: (m, 0)),
                  pl.BlockSpec((None, 2 * N_FG, D), lambda p, m: (p, 0, 0))]
        + [pl.BlockSpec(memory_space=pl.ANY)] * len(before),
        out_specs=[blk, blk, blk],
        out_shape=[jax.ShapeDtypeStruct((NFG, S, N_FG), BF16)] * 3,
        input_output_aliases={2 + k: k for k in range(len(before))},
        compiler_params=_cp(("parallel", "parallel")),
    )(h2, wgu, *before)


def _ffn_down_first(x2, a, wd):
    tm = 512
    n = wd.shape[0]

    def body(x_ref, a_ref, w_ref, o_ref):
        p = pl.program_id(1)

        @pl.when(p == 0)
        def _():
            o_ref[...] = x_ref[...] + _dot(a_ref[...], w_ref[...])

        @pl.when(p > 0)
        def _():
            o_ref[...] += _dot(a_ref[...], w_ref[...])

    return pl.pallas_call(
        body, name="ffn_down_first", grid=(S // tm, n),
        in_specs=[pl.BlockSpec((tm, D), lambda m, p: (m, 0)),
                  pl.BlockSpec((None, tm, N_FG), lambda m, p: (p, m, 0)),
                  pl.BlockSpec((None, N_FG, D), lambda m, p: (p, 0, 0))],
        out_specs=pl.BlockSpec((tm, D), lambda m, p: (m, 0)),
        out_shape=jax.ShapeDtypeStruct((S, D), F32),
        compiler_params=_cp(("parallel", "arbitrary")),
    )(x2, a, wd)


def _ffn_down_loss(x2, a, wd, nw, tgt):
    tm = 512
    first = NFG - wd.shape[0]

    def body(x2_hbm, a_ref, w_ref, nw_ref, t_hbm, dx_ref, dxb_ref, st_ref, acc_ref, x2_buf, t_buf, sems):
        m, p = pl.program_id(0), pl.program_id(1)
        tail_in = _row_copies((x2_hbm, t_hbm), (x2_buf, t_buf), sems, m, tm)

        @pl.when(p == 0)
        def _():
            acc_ref[...] = jnp.zeros_like(acc_ref)
            for cp in tail_in:
                cp.start()

        @pl.when((p == 0) & (m == 0))
        def _():
            st_ref[...] = jnp.zeros_like(st_ref)

        acc_ref[...] += _dot(a_ref[...], w_ref[...])

        @pl.when(p == NFG - first - 1)
        def _():
            for cp in tail_in:
                cp.wait()
            x3 = x2_buf[...] + acc_ref[...]
            nwv = nw_ref[...]
            r = lax.rsqrt(jnp.mean(x3 * x3, axis=-1, keepdims=True) + EPS)
            y = (x3 * r) * nwv
            err = y - t_buf[...]
            loss = 0.5 * jnp.sum(jnp.mean(err * err, axis=-1, keepdims=True), axis=0, keepdims=True)
            dy = err * (1.0 / D)
            dx, dnw = _rms_bwd_tile(dy, x3, r, nwv)
            dx_ref[...] = dx
            dxb_ref[...] = dx.astype(BF16)
            st_ref[0:1, :] += dnw
            st_ref[1:2, :] += jnp.broadcast_to(loss, (1, D))

    return pl.pallas_call(
        body, name="ffn_down_loss", grid=(S // tm, NFG - first),
        in_specs=[pl.BlockSpec(memory_space=pl.ANY),
                  pl.BlockSpec((None, tm, N_FG), lambda m, p: (p + first, m, 0)),
                  pl.BlockSpec((None, N_FG, D), lambda m, p: (p, 0, 0)),
                  pl.BlockSpec((1, D), lambda m, p: (0, 0)),
                  pl.BlockSpec(memory_space=pl.ANY)],
        out_specs=[pl.BlockSpec((tm, D), lambda m, p: (m, 0)), pl.BlockSpec((tm, D), lambda m, p: (m, 0)),
                   pl.BlockSpec((8, D), lambda m, p: (0, 0))],
        out_shape=[jax.ShapeDtypeStruct((S, D), F32), jax.ShapeDtypeStruct((S, D), BF16),
                   jax.ShapeDtypeStruct((8, D), F32)],
        scratch_shapes=[pltpu.VMEM((tm, D), F32), pltpu.VMEM((tm, D), F32), pltpu.VMEM((tm, D), F32),
                        pltpu.SemaphoreType.DMA((2,))],
        compiler_params=_cp(("arbitrary", "arbitrary")),
    )(x2, a, wd, nw, tgt)


def _ffn_down_bwd(dx3b, wd, dadg, dadu, part, before=None):
    tm = 1024
    half = NFG // 2

    def body(dx_ref, w_ref, dadg_ref, dadu_ref, *rest):
        dgu_ref = rest[-1]
        rows = pl.ds(pl.multiple_of(pl.program_id(1) * tm, tm), tm)
        da = _dot_nt(dx_ref[rows, :], w_ref[...])
        dgu_ref[:, 0:N_FG] = (da * dadg_ref[...].astype(F32)).astype(BF16)
        dgu_ref[:, N_FG:2 * N_FG] = (da * dadu_ref[...].astype(F32)).astype(BF16)

    blk = pl.BlockSpec((None, tm, N_FG), lambda p, m: (p + part * half, m, 0))
    before = list(before or [])
    return pl.pallas_call(
        body, name=f"ffn_down_bwd_{part}", grid=(half, S // tm),
        in_specs=[pl.BlockSpec((S, D), lambda p, m: (0, 0)),
                  pl.BlockSpec((None, N_FG, D), lambda p, m: (p, 0, 0)), blk, blk]
        + [pl.BlockSpec(memory_space=pl.ANY)] * len(before),
        out_specs=pl.BlockSpec((None, tm, 2 * N_FG), lambda p, m: (p + part * half, m, 0)),
        out_shape=jax.ShapeDtypeStruct((NFG, S, 2 * N_FG), BF16),
        input_output_aliases={4 + k: k for k in range(len(before))},
        compiler_params=_cp(("parallel", "parallel")),
    )(dx3b, wd, dadg, dadu, *before)


def _ffn_up_bwd(dgu, wgu_a, wgu_b, dres, xs, r, nw):
    tm = 512
    nm = S // tm
    na = wgu_a.shape[0]

    def body(dgu_ref, wa_hbm, wb_hbm, dres_hbm, x_hbm, r_ref, nw_ref, dx_ref, dxb_ref, st_ref,
             w_buf, dres_buf, x_buf, sems, w_sems):
        m, p = pl.program_id(0), pl.program_id(1)
        tail_in = _row_copies((dres_hbm, x_hbm), (dres_buf, x_buf), sems, m, tm)

        def fetch(g, slot):
            for src, lo in ((wa_hbm, 0), (wb_hbm, na)):
                @pl.when((g >= lo) & (g < lo + na))
                def _():
                    pltpu.make_async_copy(src.at[g - lo], w_buf.at[slot], w_sems.at[slot]).start()

        @pl.when((p == 0) & (m == 0))
        def _():
            st_ref[...] = jnp.zeros_like(st_ref)
            fetch(p, 0)

        @pl.when((p < NFG - 1) | (m < nm - 1))
        def _():
            fetch((p + 1) % NFG, (p + 1) % 2)

        @pl.when(p == 0)
        def _():
            dx_ref[...] = jnp.zeros_like(dx_ref)
            for cp in tail_in:
                cp.start()

        slot = p % 2
        pltpu.make_async_copy(wa_hbm.at[0], w_buf.at[slot], w_sems.at[slot]).wait()
        dx_ref[...] += _dot(dgu_ref[...], w_buf[slot])

        @pl.when(p == NFG - 1)
        def _():
            for cp in tail_in:
                cp.wait()
            dx, dnw = _rms_bwd_tile(dx_ref[...], x_buf[...], r_ref[...], nw_ref[...])
            dx = dres_buf[...] + dx
            dx_ref[...] = dx
            dxb_ref[...] = dx.astype(BF16)
            st_ref[0:1, :] += dnw

    blk = pl.BlockSpec((None, tm, 2 * N_FG), lambda m, p: (p, m, 0))
    row = pl.BlockSpec((tm, D), lambda m, p: (m, 0))
    hbm = pl.BlockSpec(memory_space=pl.ANY)
    return pl.pallas_call(
        body, name="ffn_up_bwd", grid=(nm, NFG),
        in_specs=[blk, hbm, hbm, hbm, hbm, pl.BlockSpec((tm, 1), lambda m, p: (m, 0)),
                  pl.BlockSpec((1, D), lambda m, p: (0, 0))],
        out_specs=[row, row, pl.BlockSpec((8, D), lambda m, p: (0, 0))],
        out_shape=[jax.ShapeDtypeStruct((S, D), F32), jax.ShapeDtypeStruct((S, D), BF16),
                   jax.ShapeDtypeStruct((8, D), F32)],
        scratch_shapes=[pltpu.VMEM((2, 2 * N_FG, D), BF16), pltpu.VMEM((tm, D), F32), pltpu.VMEM((tm, D), F32),
                        pltpu.SemaphoreType.DMA((2,)), pltpu.SemaphoreType.DMA((2,))],
        compiler_params=_cp(("arbitrary", "arbitrary")),
    )(dgu, wgu_a, wgu_b, dres, xs, r, nw)


def _out_proj_bwd(dx2b, wout, place=None, rider=None):
    tm = 256

    if rider is None:
        def body(dx_ref, w_ref, o_ref):
            o_ref[...] = _dot_nt(dx_ref[...], w_ref[...])

        return pl.pallas_call(
            body, name="out_proj_bwd", grid=(S // tm,),
            in_specs=[pl.BlockSpec((tm, D), lambda i: (i, 0)), pl.BlockSpec((D, D), lambda i: (0, 0))],
            out_specs=pl.BlockSpec((tm, D), lambda i: (i, 0)),
            out_shape=jax.ShapeDtypeStruct((S, D), F32),
            compiler_params=_cp(("parallel",)),
        )(dx2b, wout), None

    w = rider[0]
    r, c = w.shape
    rt = _row_tile(r, c)
    nt = r // rt
    assert nt <= S // tm

    def body(pos_ref, dx_ref, w_ref, uw, um, uv, ug, us, uc, o_ref, go, dd, mo, vo):
        o_ref[...] = _dot_nt(dx_ref[...], w_ref[...])

        @pl.when(pl.program_id(0) < nt)
        def _():
            _update_tile(uw, um, uv, ug, us, uc, go, dd, mo, vo)

    def at(i):
        return jnp.minimum(i, nt - 1)

    tile = pl.BlockSpec((rt, c), lambda i, pos: (at(i), 0))
    outs = pl.pallas_call(
        body, name="out_proj_bwd",
        grid_spec=pltpu.PrefetchScalarGridSpec(
            num_scalar_prefetch=1, grid=(S // tm,),
            in_specs=[pl.BlockSpec((tm, D), lambda i, pos: (i, 0)), pl.BlockSpec((D, D), lambda i, pos: (0, 0)),
                      tile, tile, tile,
                      pl.BlockSpec((None, rt, c), lambda i, pos: (4 * pos[0] + 2 * pos[1] + pos[2], at(i), 0)),
                      pl.BlockSpec((None, rt, c), lambda i, pos: (2 * pos[0] + pos[1], at(i), 0)),
                      pl.BlockSpec((3, rt, c), lambda i, pos: (0, at(i), 0))],
            out_specs=[pl.BlockSpec((tm, D), lambda i, pos: (i, 0)), tile, tile, tile, tile]),
        out_shape=[jax.ShapeDtypeStruct((S, D), F32)] + [jax.ShapeDtypeStruct((r, c), F32)] * 4,
        compiler_params=_cp(("arbitrary",)),
    )(place, dx2b, wout, *rider)
    return outs[0], outs[1:]


def _in_proj_bwd(dproj, wins, dres, xs, r, nw):
    tm = 1024

    nr = len(wins)
    nm = S // tm

    def body(dp_ref, *rest):
        w_hbms = rest[:nr]
        dres_hbm, x_hbm, r_ref, nw_ref, dx_ref, st_ref, w_buf, dres_buf, x_buf, sems, w_sems = rest[nr:]
        m, p = pl.program_id(0), pl.program_id(1)
        tail_in = _row_copies((dres_hbm, x_hbm), (dres_buf, x_buf), sems, m, tm)

        def w_copies(g, slot):
            return [pltpu.make_async_copy(w_hbm.at[g], w_buf.at[slot, pl.ds(0, D), pl.ds(off, width)],
                                          w_sems.at[slot, k])
                    for k, (w_hbm, (off, width)) in enumerate(zip(w_hbms, IN_ROUNDS))]

        @pl.when((p == 0) & (m == 0))
        def _():
            st_ref[...] = jnp.zeros_like(st_ref)
            for cp in w_copies(p, 0):
                cp.start()

        @pl.when((p < NDEV - 1) | (m < nm - 1))
        def _():
            for cp in w_copies((p + 1) % NDEV, (p + 1) % 2):
                cp.start()

        @pl.when(p == 0)
        def _():
            dx_ref[...] = jnp.zeros_like(dx_ref)
            for cp in tail_in:
                cp.start()

        for cp in w_copies(p, p % 2):
            cp.wait()
        dx_ref[...] += _dot_nt(dp_ref[...], w_buf[p % 2])

        @pl.when(p == NDEV - 1)
        def _():
            for cp in tail_in:
                cp.wait()
            dx, dnw = _rms_bwd_tile(dx_ref[...], x_buf[...], r_ref[...], nw_ref[...])
            dx_ref[...] = dres_buf[...] + dx
            st_ref[0:1, :] += dnw

    row = pl.BlockSpec((tm, D), lambda m, p: (m, 0))
    hbm = pl.BlockSpec(memory_space=pl.ANY)
    return pl.pallas_call(
        body, name="in_proj_bwd", grid=(S // tm, NDEV),
        in_specs=[pl.BlockSpec((tm, N_IN), lambda m, p: (m, p)),
                  *[hbm] * nr,
                  hbm, hbm, pl.BlockSpec((tm, 1), lambda m, p: (m, 0)),
                  pl.BlockSpec((1, D), lambda m, p: (0, 0))],
        out_specs=[row, pl.BlockSpec((8, D), lambda m, p: (0, 0))],
        out_shape=[jax.ShapeDtypeStruct((S, D), F32), jax.ShapeDtypeStruct((8, D), F32)],
        scratch_shapes=[pltpu.VMEM((2, D, N_IN), BF16), pltpu.VMEM((tm, D), F32), pltpu.VMEM((tm, D), F32),
                        pltpu.SemaphoreType.DMA((2,)), pltpu.SemaphoreType.DMA((2, nr))],
        compiler_params=_cp(("arbitrary", "arbitrary")),
    )(dproj, *wins, dres, xs, r, nw)


W_IN_PARTS = 2


def _wgrad_in(h1, dproj, part):
    rows = D // W_IN_PARTS

    def body(a_ref, d_ref, o_ref):
        both = _dot_tn(a_ref[...], d_ref[...]).astype(BF16)
        o_ref[0] = both[:, 0:N_IN]
        o_ref[1] = both[:, N_IN:2 * N_IN]

    return pl.pallas_call(
        body, name=f"wgrad_in_{part}", grid=(NDEV // 2,),
        in_specs=[pl.BlockSpec((S, rows), lambda p: (0, part)), pl.BlockSpec((S, 2 * N_IN), lambda p: (0, p))],
        out_specs=pl.BlockSpec((2, rows, N_IN), lambda p: (p, 0, 0)),
        out_shape=jax.ShapeDtypeStruct((NDEV, rows, N_IN), BF16),
        compiler_params=_cp(("parallel",)),
    )(h1, dproj)


def _wgrad_rows(a3, dy, name, col=0):
    def body(a_ref, d_ref, o_ref):
        dw = _dot_tn(a_ref[...], d_ref[...]).astype(BF16)
        for j in range(FF_PER):
            o_ref[j] = dw[j * FF_ROWS:(j + 1) * FF_ROWS]

    return pl.pallas_call(
        body, name=name, grid=(NFG,),
        in_specs=[pl.BlockSpec((None, S, N_FG), lambda p: (p, 0, col)), pl.BlockSpec((S, D), lambda p: (0, 0))],
        out_specs=pl.BlockSpec((FF_PER, FF_ROWS, D), lambda p: (p % 2, p // 2, 0)),
        out_shape=jax.ShapeDtypeStruct((NDEV, N_FF, D), BF16),
        compiler_params=_cp(("parallel",)),
    )(a3, dy)


def _wgrad_out(ma, mr, dx2b):
    half = D // 2
    per = half // N_OUT

    def body(ma_ref, mr_ref, d_ref, o_ref):
        p = pl.program_id(0)

        @pl.when(p == 0)
        def _():
            o_ref[...] = _dot_tn(ma_ref[...], d_ref[...]).astype(BF16).reshape(per, N_OUT, D)

        @pl.when(p == 1)
        def _():
            o_ref[...] = _dot_tn(mr_ref[...], d_ref[...]).astype(BF16).reshape(per, N_OUT, D)

    whole = pl.BlockSpec((S, half), lambda p: (0, 0))
    return pl.pallas_call(
        body, name="wgrad_out", grid=(2,),
        in_specs=[whole, whole, pl.BlockSpec((S, D), lambda p: (0, 0))],
        out_specs=pl.BlockSpec((per, N_OUT, D), lambda p: (p, 0, 0)),
        out_shape=jax.ShapeDtypeStruct((NDEV, N_OUT, D), BF16),
        compiler_params=_cp(("parallel",)),
    )(ma, mr, dx2b)


def _attn_consts():
    c = np.zeros((AH, 8, AHD), np.float32)
    for h in range(AH):
        c[h, :, :] = 2.0 ** (-(h + 1))
    return jnp.asarray(c)


def _permute_in(dst, src, d, cast=None):
    v = src[...]
    if d > 1:
        v = pltpu.einshape("jrc->rjc", v.reshape(S // d, d, AHD)).reshape(S, AHD)
    dst[...] = v if cast is None else v.astype(cast)


def _natural_order(v, d):
    if d == 1:
        return v
    return pltpu.einshape("rjc->jrc", v.reshape(d, S // d, AHD)).reshape(S, AHD)


def _attn_masks():
    qi = lax.broadcasted_iota(jnp.int32, (CH, CH), 0)
    kj = lax.broadcasted_iota(jnp.int32, (CH, CH), 1)
    dist_c = (qi - kj).astype(F32)
    dist_p = (qi - kj + CH).astype(F32)
    return (qi >= kj)[None], (kj >= qi)[None], dist_c[None], dist_p[None]


GB = 16


def _bdot_nt(a, b):
    return lax.dot_general(a, b, (((2,), (2,)), ((0,), (0,))), preferred_element_type=F32)


def _bdot(a, b):
    return lax.dot_general(a, b, (((2,), (1,)), ((0,), (0,))), preferred_element_type=F32)


def _bdot_tn(a, b):
    return lax.dot_general(a, b, (((1,), (1,)), ((0,), (0,))), preferred_element_type=F32)


def _shift_block(dst, src):
    dst[0:CH, :] = jnp.zeros((CH, AHD), dst.dtype)
    dst[CH:S, :] = src[0:S - CH, :]


def _has_prev(g, nb):
    blk = lax.broadcasted_iota(jnp.int32, (GB, 1, 1), 0) + g * GB
    return (blk & (nb - 1)) != 0


def _blocks(ref, g):
    return ref[g * GB * CH:(g + 1) * GB * CH, :].reshape(GB, CH, AHD)


def _attn_fwd(proj):
    scale = 1.0 / math.sqrt(AHD)

    def body(c_ref, q_ref, k_ref, v_ref, o_ref, ob_ref, lse_ref, qkvp_ref, lsep_ref, qd, kd, vd, kps, vps, od, ld, *nat):
        onat, lnat = nat[0:3], nat[3:6]
        slope = c_ref[0:1, :]
        mask_c, mask_p, dist_c, dist_p = _attn_masks()
        for pi, (d, nb) in enumerate(PATTERNS):
            _permute_in(qd, q_ref, d, BF16)
            _permute_in(kd, k_ref, d, BF16)
            _permute_in(vd, v_ref, d, BF16)
            if d > 1:
                qkvp_ref[pi - 1, 0] = qd[...]
                qkvp_ref[pi - 1, 1] = kd[...]
                qkvp_ref[pi - 1, 2] = vd[...]
            if nb > 1:
                _shift_block(kps, kd)
                _shift_block(vps, vd)
            bias_c = -(slope * float(d)) * dist_c
            bias_p = -(slope * float(d)) * dist_p
            for g in range(NB // GB):
                q3, k3, v3 = _blocks(qd, g), _blocks(kd, g), _blocks(vd, g)
                s_c = jnp.where(mask_c, _bdot_nt(q3, k3) * scale + bias_c, NEG)
                mx = jnp.max(s_c, axis=-1, keepdims=True)
                if nb > 1:
                    kp3, vp3 = _blocks(kps, g), _blocks(vps, g)
                    s_p = jnp.where(jnp.logical_and(mask_p, _has_prev(g, nb)),
                                    _bdot_nt(q3, kp3) * scale + bias_p, NEG)
                    mx = jnp.maximum(mx, jnp.max(s_p, axis=-1, keepdims=True))
                    l = (jnp.sum(jnp.exp(s_c - mx), axis=-1, keepdims=True)
                         + jnp.sum(jnp.exp(s_p - mx), axis=-1, keepdims=True))
                    lse = mx + jnp.log(l)
                    o3 = _bdot(jnp.exp(s_c - lse).astype(BF16), v3) + _bdot(jnp.exp(s_p - lse).astype(BF16), vp3)
                else:
                    l = jnp.sum(jnp.exp(s_c - mx), axis=-1, keepdims=True)
                    lse = mx + jnp.log(l)
                    o3 = _bdot(jnp.exp(s_c - lse).astype(BF16), v3)
                rows = slice(g * GB * CH, (g + 1) * GB * CH)
                od[rows, :] = o3.reshape(GB * CH, AHD)
                ld[rows, :] = jnp.broadcast_to(lse, (GB, CH, AHD)).reshape(GB * CH, AHD)
            onat[pi][...] = _natural_order(od[...], d)
            lnat[pi][...] = _natural_order(ld[...], d)
        l0, l1, l2 = lnat[0][...], lnat[1][...], lnat[2][...]
        mx = jnp.maximum(jnp.maximum(l0, l1), l2)
        e0, e1, e2 = jnp.exp(l0 - mx), jnp.exp(l1 - mx), jnp.exp(l2 - mx)
        den = e0 + e1 + e2
        out = (e0 / den) * onat[0][...] + (e1 / den) * onat[1][...] + (e2 / den) * onat[2][...]
        o_ref[...] = out
        ob_ref[...] = out.astype(BF16)
        lse_ref[...] = mx + jnp.log(den)
        for pi, (d, _) in enumerate(PATTERNS[1:]):
            _permute_in(lsep_ref.at[pi], lse_ref, d)

    def col(off):
        return pl.BlockSpec((S, AHD), lambda h: (0, off + h))

    return pl.pallas_call(
        body, name="attn_fwd", grid=(AH,),
        in_specs=[pl.BlockSpec((None, 8, AHD), lambda h: (h, 0, 0)), col(0), col(AH), col(2 * AH)],
        out_specs=[col(0), col(0), col(0), pl.BlockSpec((2, 3, S, AHD), lambda h: (0, 0, 0, h)),
                   pl.BlockSpec((2, S, AHD), lambda h: (0, 0, h))],
        out_shape=[jax.ShapeDtypeStruct((S, AH * AHD), F32), jax.ShapeDtypeStruct((S, AH * AHD), BF16),
                   jax.ShapeDtypeStruct((S, AH * AHD), F32),
                   jax.ShapeDtypeStruct((2, 3, S, AH * AHD), BF16), jax.ShapeDtypeStruct((2, S, AH * AHD), F32)],
        scratch_shapes=[pltpu.VMEM((S, AHD), BF16) for _ in range(5)]
        + [pltpu.VMEM((S, AHD), F32) for _ in range(8)],
        compiler_params=_cp(("parallel",)),
    )(_attn_consts(), proj, proj, proj)


def _attn_bwd(proj, dmixed, o, lse, qkvp, lsep):
    scale = 1.0 / math.sqrt(AHD)

    def body(c_ref, q_ref, k_ref, v_ref, do_ref, o_ref, lse_ref, qkvp_ref, lsep_ref, dproj_hbm,
             qd, kd, vd, dod, kps, vps, dld, dqd, dkd, dvd, delta, aq, ak, av, sq, sk, sv, sems):
        h = pl.program_id(0)

        def out_copies(head):
            return [pltpu.make_async_copy(
                st, dproj_hbm.at[:, pl.ds(pl.multiple_of((k * AH + head) * AHD, AHD), AHD)], sems.at[k])
                for k, st in enumerate((sq, sk, sv))]

        slope = c_ref[0:1, :]
        mask_c, mask_p, dist_c, dist_p = _attn_masks()
        delta[...] = jnp.broadcast_to(jnp.sum(do_ref[...] * o_ref[...], axis=-1, keepdims=True), (S, AHD))
        for pi, (d, nb) in enumerate(PATTERNS):
            if d == 1:
                _permute_in(qd, q_ref, d, BF16)
                _permute_in(kd, k_ref, d, BF16)
                _permute_in(vd, v_ref, d, BF16)
                qs, ks, vs, lss = qd, kd, vd, lse_ref
            else:
                qs, ks, vs, lss = (qkvp_ref.at[pi - 1, 0], qkvp_ref.at[pi - 1, 1], qkvp_ref.at[pi - 1, 2],
                                   lsep_ref.at[pi - 1])
            _permute_in(dod, do_ref, d, BF16)
            _permute_in(dld, delta, d)
            if nb > 1:
                _shift_block(kps, ks)
                _shift_block(vps, vs)
            bias_c = -(slope * float(d)) * dist_c
            bias_p = -(slope * float(d)) * dist_p
            for g in range(NB // GB):
                q3, k3, v3, do3 = _blocks(qs, g), _blocks(ks, g), _blocks(vs, g), _blocks(dod, g)
                ls, dl = _blocks(lss, g), _blocks(dld, g)
                lo, hi = g * GB * CH, (g + 1) * GB * CH
                p_c = jnp.exp(jnp.where(mask_c, _bdot_nt(q3, k3) * scale + bias_c, NEG) - ls)
                ds_c = ((p_c * (_bdot_nt(do3, v3) - dl)) * scale).astype(BF16)
                dq3 = _bdot(ds_c, k3)
                dkd[lo:hi, :] = _bdot_tn(ds_c, q3).reshape(GB * CH, AHD)
                dvd[lo:hi, :] = _bdot_tn(p_c.astype(BF16), do3).reshape(GB * CH, AHD)
                if nb > 1:
                    kp3, vp3 = _blocks(kps, g), _blocks(vps, g)
                    p_p = jnp.exp(jnp.where(jnp.logical_and(mask_p, _has_prev(g, nb)),
                                            _bdot_nt(q3, kp3) * scale + bias_p, NEG) - ls)
                    ds_p = ((p_p * (_bdot_nt(do3, vp3) - dl)) * scale).astype(BF16)
                    dq3 = dq3 + _bdot(ds_p, kp3)
                    dkp = _bdot_tn(ds_p, q3).reshape(GB * CH, AHD)
                    dvp = _bdot_tn(p_p.astype(BF16), do3).reshape(GB * CH, AHD)
                    if g == 0:
                        dkd[0:hi - CH, :] += dkp[CH:, :]
                        dvd[0:hi - CH, :] += dvp[CH:, :]
                    else:
                        dkd[lo - CH:hi - CH, :] += dkp
                        dvd[lo - CH:hi - CH, :] += dvp
                dqd[lo:hi, :] = dq3.reshape(GB * CH, AHD)
            ln = S // d
            for acc, src in ((aq, dqd), (ak, dkd), (av, dvd)):
                if pi == 0:
                    acc[...] = src[...]
                else:
                    acc[...] += _natural_order(src[...], d)

        @pl.when(h > 0)
        def _():
            for cp in out_copies(h - 1):
                cp.wait()

        sq[...] = aq[...].astype(BF16)
        sk[...] = ak[...].astype(BF16)
        sv[...] = av[...].astype(BF16)
        for cp in out_copies(h):
            cp.start()

        @pl.when(h == AH - 1)
        def _():
            for cp in out_copies(h):
                cp.wait()

    def col(off):
        return pl.BlockSpec((S, AHD), lambda h: (0, off + h))

    return pl.pallas_call(
        body, name="attn_bwd", grid=(AH,),
        in_specs=[pl.BlockSpec((None, 8, AHD), lambda h: (h, 0, 0)), col(0), col(AH), col(2 * AH),
                  col(0), col(0), col(0), pl.BlockSpec((2, 3, S, AHD), lambda h: (0, 0, 0, h)),
                  pl.BlockSpec((2, S, AHD), lambda h: (0, 0, h))],
        out_specs=pl.BlockSpec(memory_space=pl.ANY),
        out_shape=jax.ShapeDtypeStruct((S, NDEV * N_IN), BF16),
        scratch_shapes=[pltpu.VMEM((S, AHD), BF16) for _ in range(6)]
        + [pltpu.VMEM((S, AHD), F32) for _ in range(8)]
        + [pltpu.VMEM((S, AHD), BF16) for _ in range(3)] + [pltpu.SemaphoreType.DMA((3,))],
        compiler_params=_cp(("arbitrary",)),
    )(_attn_consts(), proj, proj, proj, dmixed, o, lse, qkvp, lsep)


def _ret_consts():
    c = np.zeros((RH, 8, RHD), np.float32)
    for h in range(RH):
        c[h, :, :] = np.log(np.float32(1.0) - np.float32(2.0 ** (-5.0 - h)))
    return jnp.asarray(c)


def _ret_factors(lg):
    i = lax.broadcasted_iota(jnp.int32, (CH, CH), 0)
    j = lax.broadcasted_iota(jnp.int32, (CH, CH), 1)
    dif = (i - j).astype(F32)
    decay = jnp.where(dif >= 0, jnp.exp(lg[:, 0:CH] * jnp.maximum(dif, 0.0)), 0.0)
    row = lax.broadcasted_iota(jnp.int32, (CH, RHD), 0).astype(F32)
    zeta = jnp.exp(lg * (CH - 1.0 - row))
    xi = jnp.exp(lg * (row + 1.0))
    return decay, zeta, xi, jnp.exp(lg * float(CH))


CBK = 8
RSTEPS = NB // CBK


def _ret_specs(rev):
    off = 3 * AH * AHD // RHD
    rows = CBK * CH

    def ch(n):
        return (RSTEPS - 1 - n) if rev else n

    def col(k):
        return pl.BlockSpec((rows, RHD), lambda h, n: (ch(n), off + k * RH + h))

    own = pl.BlockSpec((rows, RHD), lambda h, n: (ch(n), h))
    state = pl.BlockSpec((None, CBK, RHD, RHD), lambda h, n: (h, ch(n), 0, 0))
    const = pl.BlockSpec((None, 8, RHD), lambda h, n: (h, 0, 0))
    dm = pl.BlockSpec((rows, RHD), lambda h, n: (ch(n), AH * AHD // RHD + h))
    return col, own, state, const, dm


def _chunks(x):
    return x.reshape(CBK, CH, RHD)


def _ret_fwd(proj):
    def body(c_ref, q_ref, k_ref, v_ref, g_ref, ret_ref, mr_ref, st_ref, r_acc):
        n = pl.program_id(1)

        @pl.when(n == 0)
        def _():
            r_acc[...] = jnp.zeros_like(r_acc)

        decay, zeta, xi, gch = _ret_factors(c_ref[0:1, :])
        q3 = _chunks(q_ref[...].astype(BF16))
        kc = _chunks(k_ref[...] * (1.0 / math.sqrt(RHD)))
        k3 = kc.astype(BF16)
        v3 = _chunks(v_ref[...].astype(BF16))
        kv3 = _bdot_tn((kc * zeta[None]).astype(BF16), v3)
        r = r_acc[...]
        for i in range(CBK):
            st_ref[i] = r.astype(BF16)
            r = r * gch + kv3[i]
        r_acc[...] = r
        scores = _bdot_nt(q3, k3) * decay[None]
        ret = (_bdot(scores.astype(BF16), v3) + _bdot(q3, st_ref[...]) * xi[None]).reshape(CBK * CH, RHD)
        ret_ref[...] = ret
        rr = lax.rsqrt(jnp.mean(ret * ret, axis=-1, keepdims=True) + EPS)
        gv = g_ref[...]
        mr_ref[...] = ((gv * _sigmoid(gv)) * (ret * rr)).astype(BF16)

    col, own, state, const, _ = _ret_specs(False)
    return pl.pallas_call(
        body, name="ret_fwd", grid=(RH, RSTEPS),
        in_specs=[const, col(0), col(1), col(2), col(3)],
        out_specs=[own, own, state],
        out_shape=[jax.ShapeDtypeStruct((S, RH * RHD), F32), jax.ShapeDtypeStruct((S, RH * RHD), BF16),
                   jax.ShapeDtypeStruct((RH, NB, RHD, RHD), BF16)],
        scratch_shapes=[pltpu.VMEM((RHD, RHD), F32)],
        compiler_params=_cp(("parallel", "arbitrary")),
    )(_ret_consts(), proj, proj, proj, proj)


def _ret_bwd(proj, ret, states, dmixed, dproj):
    rows = CBK * CH
    col0 = 3 * AH * AHD

    def body(c_ref, q_ref, k_ref, v_ref, g_ref, ret_ref, st_ref, dm_ref, dproj_in, dproj_hbm, g_acc, gs,
             sq, sk, sv, sg, sems):
        del dproj_in
        h, n = pl.program_id(0), pl.program_id(1)
        step = h * RSTEPS + n

        def out_copies(t):
            hh, nn = t // RSTEPS, t % RSTEPS
            r0 = pl.multiple_of((RSTEPS - 1 - nn) * rows, rows)
            return [pltpu.make_async_copy(
                st, dproj_hbm.at[pl.ds(r0, rows), pl.ds(pl.multiple_of(col0 + (k * RH + hh) * RHD, RHD), RHD)],
                sems.at[k]) for k, st in enumerate((sq, sk, sv, sg))]

        @pl.when(n == 0)
        def _():
            g_acc[...] = jnp.zeros_like(g_acc)

        decay, zeta, xi, gch = _ret_factors(c_ref[0:1, :])
        ret_v = ret_ref[...]
        rr = lax.rsqrt(jnp.mean(ret_v * ret_v, axis=-1, keepdims=True) + EPS)
        gv = g_ref[...]
        sgm = _sigmoid(gv)
        dmix = dm_ref[...]
        dgate = ((dmix * (ret_v * rr)) * (sgm * (1.0 + gv * (1.0 - sgm)))).astype(BF16)
        dretn = dmix * (gv * sgm)
        dret = _chunks(rr * dretn - ret_v * ((rr * rr * rr) * jnp.mean(dretn * ret_v, axis=-1, keepdims=True)))

        q3 = _chunks(q_ref[...].astype(BF16))
        kc = _chunks(k_ref[...] * (1.0 / math.sqrt(RHD)))
        k3 = kc.astype(BF16)
        v3 = _chunks(v_ref[...].astype(BF16))
        d3 = dret.astype(BF16)
        dxi = (dret * xi[None]).astype(BF16)
        kz = (kc * zeta[None]).astype(BF16)
        dr3 = _bdot_tn(q3, dxi)
        acc = g_acc[...]
        for i in reversed(range(CBK)):
            gs[i] = acc.astype(BF16)
            acc = dr3[i] + gch * acc
        g_acc[...] = acc
        g3 = gs[...]
        sc = (_bdot_nt(q3, k3) * decay[None]).astype(BF16)
        da = (_bdot_nt(d3, v3) * decay[None]).astype(BF16)
        dq = _bdot(da, k3) + _bdot_nt(dxi, st_ref[...])
        dkc = _bdot_tn(da, q3) + _bdot_nt(v3, g3) * zeta[None]
        dv = _bdot_tn(sc, d3) + _bdot(kz, g3)

        @pl.when(step > 0)
        def _():
            for cp in out_copies(step - 1):
                cp.wait()

        sq[...] = dq.reshape(rows, RHD).astype(BF16)
        sk[...] = (dkc * (1.0 / math.sqrt(RHD))).reshape(rows, RHD).astype(BF16)
        sv[...] = dv.reshape(rows, RHD).astype(BF16)
        sg[...] = dgate
        for cp in out_copies(step):
            cp.start()

        @pl.when(step == RH * RSTEPS - 1)
        def _():
            for cp in out_copies(step):
                cp.wait()

    col, own, state, const, dm = _ret_specs(True)
    hbm = pl.BlockSpec(memory_space=pl.ANY)
    return pl.pallas_call(
        body, name="ret_bwd", grid=(RH, RSTEPS),
        in_specs=[const, col(0), col(1), col(2), col(3), own, state, dm, hbm],
        out_specs=hbm,
        out_shape=jax.ShapeDtypeStruct(dproj.shape, dproj.dtype),
        input_output_aliases={8: 0},
        scratch_shapes=[pltpu.VMEM((RHD, RHD), F32), pltpu.VMEM((CBK, RHD, RHD), BF16)]
        + [pltpu.VMEM((rows, RHD), BF16) for _ in range(4)] + [pltpu.SemaphoreType.DMA((4,))],
        compiler_params=_cp(("arbitrary", "arbitrary")),
    )(_ret_consts(), proj, proj, proj, proj, ret, states, dmixed, dproj)


class _NoReduction:
    def start(self, group, grads):
        pass

    def local(self, name, first=()):
        return []

    def landed(self, name):
        return []

    def update(self, name):
        return []

    place = None

    def rider(self, name):
        return None

    def set_update(self, name, outs):
        pass


def _local_step(x, tgt, nw1, nw2, nw3, win, wout, wgu_a, wgu_b, wd_a, wd_b, red=None):
    red = red or _NoReduction()

    def after(values, first):
        return lax.optimization_barrier((tuple(values), tuple(first)))[0]

    h1, r1 = _rms_fwd(x, nw1)
    proj = _proj(h1, win)
    o, ma, lse, qkvp, lsep = _attn_fwd(proj)
    ret, mr, states = _ret_fwd(proj)
    x2, h2, r2 = _out_proj_rms(x, ma, mr, wout, nw2)
    a, dadg, dadu = _ffn_up(h2, wgu_b, 1, _ffn_up(h2, wgu_a, 0))
    dx3, dx3b, st3 = _ffn_down_loss(_ffn_down_first(x2, a, wd_a), a, wd_b, nw3, tgt)

    dwd = _wgrad_rows(a, dx3b, "wgrad_down")
    red.start(["w_down"], [dwd])
    (dx3b,) = after([dx3b], [dwd])
    part = _ffn_down_bwd(dx3b, wd_a, dadg, dadu, 0)
    (dx3b,) = after([dx3b], red.local("w_down", first=[part]))
    dgu = _ffn_down_bwd(dx3b, wd_b, dadg, dadu, 1, [part])
    dwg = _wgrad_rows(dgu, h2, "wgrad_gate", 0)
    red.start(["w_gate"], [dwg])
    (dgu,) = after([dgu], [dwg])
    dwu = _wgrad_rows(dgu, h2, "wgrad_up", 1)
    red.start(["w_up"], [dwu])
    (dgu,) = after([dgu], red.local("w_gate", first=[dwu] + red.landed("w_down")))
    dx2, dx2b, st2 = _ffn_up_bwd(dgu, wgu_a, wgu_b, dx3, x2, r2, nw2)
    (dx2b,) = after([dx2b], red.local("w_up", first=[dx2b]))
    dwo = _wgrad_out(ma, mr, dx2b)
    red.start(["w_out"], [dwo])
    (dx2b,) = after([dx2b], [dwo])
    dmixed, done = _out_proj_bwd(dx2b, wout, red.place, red.rider("w_down"))
    red.set_update("w_down", done)
    dproj = _attn_bwd(proj, dmixed, o, lse, qkvp, lsep)
    (dmixed,) = after([dmixed], red.local("w_out", first=[dproj] + red.landed("w_gate")))
    dproj = _ret_bwd(proj, ret, states, dmixed, dproj)
    (dwi0,) = after([_wgrad_in(h1, dproj, 0)], red.landed("w_up"))
    red.start(["w_in_0"], [dwi0])
    (dproj,) = after([dproj], [dwi0])
    dwi1 = _wgrad_in(h1, dproj, 1)
    red.start(["w_in_1"], [dwi1])
    sums = red.local("w_in_0", first=[dwi1] + red.landed("w_out"))
    sums = red.local("w_in_1", first=sums + red.update("w_gate"))
    (dproj,) = after([dproj], sums)
    gx, st1 = _in_proj_bwd(dproj, win, dx2, x, r1, nw1)
    dwi = jnp.concatenate([dwi0, dwi1], axis=1)
    stats = jnp.concatenate([st1[0:1], st2[0:1], st3[0:2], jnp.zeros((4, D), F32)], axis=0)
    return stats, gx, dwi, dwo, dwg, dwu, dwd


def _place():
    x, y, c = lax.axis_index("x"), lax.axis_index("y"), lax.axis_index("c")
    return x, y, c, [(1 - x, y), (x, 1 - y), (1 - x, 1 - y)]


def _handshake(peers):
    barrier = pltpu.get_barrier_semaphore()
    for peer in peers:
        pl.semaphore_signal(barrier, inc=1, device_id=peer, device_id_type=MESH)
    pl.semaphore_wait(barrier, len(peers))


def _all_gather(shards, name, collective_id, per=0, rows=None):
    na = len(shards)
    nout = 1 if per else na
    lo, r = rows or (0, shards[0].shape[0])
    ngroups = NDEV // per if per else 0
    SIB, XN0, XN1, YN1, YN0, VIA_X, VIA_Y = 0, 1, 2, 3, 4, 5, 6
    D2D = {XN0: 7, XN1: 8, YN1: 9, YN0: 10, VIA_X: 11, VIA_Y: 12}

    def body(*refs):
        ins, outs = [ref.at[pl.ds(lo, r)] for ref in refs[:na]], refs[na:na + nout]
        send_sems, recv_sems, local_sems = refs[na + nout:]
        x, y, c, _ = _place()
        me, sib = (x, y, c), (x, y, 1 - c)
        xn, yn, dg = (1 - x, y, c), (x, 1 - y, c), (1 - x, 1 - y, c)
        _handshake([sib, xn, yn])

        def part(ref, h):
            rows = ref.shape[0] // 2
            return ref if h is None else ref.at[pl.ds(h * rows, rows)]

        def block(a, owner, h):
            idx = 4 * owner[0] + 2 * owner[1] + owner[2]
            if not per:
                return part(outs[a].at[idx], h)
            return part(outs[0].at[idx // per, a, pl.ds(pl.multiple_of((idx % per) * r, r), r)], h)

        def copy(a, k, owner, h, to, own_src=False):
            return pltpu.make_async_remote_copy(
                src_ref=part(ins[a], h) if own_src else block(a, owner, h), dst_ref=block(a, owner, h),
                send_sem=send_sems.at[a, k], recv_sem=recv_sems.at[a, k], device_id=to, device_id_type=MESH)

        def other(p):
            return (p[0], p[1], 1 - c)

        mine = [pltpu.make_async_copy(ins[a], block(a, me, None), local_sems.at[a]) for a in range(na)]
        for cp in mine:
            cp.start()
        sent = []
        for a in range(na):
            sent += [copy(a, XN0, me, 0, xn, True), copy(a, YN1, me, 1, yn, True),
                     copy(a, XN1, me, 1, xn, True), copy(a, YN0, me, 0, yn, True)]
        sent += [copy(a, SIB, me, None, sib, True) for a in range(na)]
        for cp in sent:
            cp.start()

        def landed(a, k, owner, h, then):
            copy(a, k, owner, h, me).wait_recv()
            for k2, to in then + [(D2D[k], sib)]:
                cp = copy(a, k2, owner, h, to)
                cp.start()
                sent.append(cp)

        for a in range(na):
            landed(a, XN0, xn, 0, [(VIA_Y, yn)])
            landed(a, YN1, yn, 1, [(VIA_X, xn)])
            landed(a, XN1, xn, 1, [])
            landed(a, YN0, yn, 0, [])
        for a in range(na):
            landed(a, VIA_Y, dg, 0, [])
            landed(a, VIA_X, dg, 1, [])
        for a in range(na):
            copy(a, SIB, sib, None, me).wait_recv()
            for k, owner, h in ((XN0, xn, 0), (XN1, xn, 1), (YN1, yn, 1), (YN0, yn, 0), (VIA_Y, dg, 0), (VIA_X, dg, 1)):
                copy(a, D2D[k], other(owner), h, me).wait_recv()
        for cp in sent:
            cp.wait_send()
        for cp in mine:
            cp.wait()

    if per:
        out_type = [jax.ShapeDtypeStruct((ngroups, na, per * r, shards[0].shape[1]), shards[0].dtype)]
    else:
        out_type = [jax.ShapeDtypeStruct((NDEV,) + s.shape, s.dtype) for s in shards]
    return _sequencer_call(
        body, name, collective_id, out_type,
        [pltpu.SemaphoreType.DMA((na, 13)), pltpu.SemaphoreType.DMA((na, 13)), pltpu.SemaphoreType.DMA((na,))])(*shards)


def _sequencer_call(body, name, collective_id, out_type, scratch_types):
    return pl.kernel(
        body, name=name, out_type=out_type,
        mesh=plsc.ScalarSubcoreMesh(axis_name="sequencer", num_cores=1),
        scratch_types=scratch_types,
        compiler_params=pltpu.CompilerParams(collective_id=collective_id))


def _exchange_sibling(grads, name, collective_id):
    na = len(grads)

    def body(*refs):
        ins, outs = refs[:na], refs[na:2 * na]
        send_sems, recv_sems = refs[2 * na:]
        x, y, c, _ = _place()
        _handshake([(x, y, 1 - c)])
        cps = []
        for a in range(na):
            for k in range(4):
                cps.append(pltpu.make_async_remote_copy(
                    src_ref=ins[a].at[2 * k + (1 - c)], dst_ref=outs[a].at[k],
                    send_sem=send_sems.at[a, k], recv_sem=recv_sems.at[a, k],
                    device_id=(x, y, 1 - c), device_id_type=MESH))
        for cp in cps:
            cp.start()
        for cp in cps:
            cp.wait()

    return _sequencer_call(
        body, name, collective_id,
        [jax.ShapeDtypeStruct((4,) + g.shape[1:], g.dtype) for g in grads],
        [pltpu.SemaphoreType.DMA((na, 4)), pltpu.SemaphoreType.DMA((na, 4))])(*grads)


def _row_tile(rows, cols):
    for t in (512, 256, 176, 128, 64, 32, 16):
        if rows % t == 0 and t * cols * 4 <= (2 << 20):
            return t
    raise ValueError((rows, cols))


STREAM_BUFS = 3


def _stream_tile(rows, steps):
    for t in (512, 256, 176, 128, 64, 32, 16):
        if rows % t == 0 and rows // t >= steps:
            return t
    raise ValueError((rows, steps))


def _stream(n, loads, stores, compute):
    for k in range(min(STREAM_BUFS, n)):
        for cp in loads(k):
            cp.start()
    for k in range(n):
        for cp in loads(k):
            cp.wait()
        if k >= 2:
            for cp in stores(k - 2):
                cp.wait()
        compute(k)
        for cp in stores(k):
            cp.start()
        if k + STREAM_BUFS < n:
            for cp in loads(k + STREAM_BUFS):
                cp.start()
    for k in range(max(n - 2, 0), n):
        for cp in stores(k):
            cp.wait()


def _chip_sum(place, g, got, name):
    _, r, c = g.shape
    tm = _stream_tile(r, 4)
    nt = r // tm

    def body(pos_ref, g_hbm, got_hbm, o_hbm, g_buf, s_buf, o_buf, sem_in, sem_out):
        def chip(j):
            return 2 * (pos_ref[0] ^ (0 if j == 1 else 1)) + (pos_ref[1] ^ (0 if j == 0 else 1))

        def loads(k):
            j, rows, slot = k // nt, pl.ds((k % nt) * tm, tm), k % STREAM_BUFS
            return [pltpu.make_async_copy(g_hbm.at[2 * chip(j) + pos_ref[2], rows], g_buf.at[slot], sem_in.at[slot, 0]),
                    pltpu.make_async_copy(got_hbm.at[chip(j), rows], s_buf.at[slot], sem_in.at[slot, 1])]

        def stores(k):
            return [pltpu.make_async_copy(o_buf.at[k % 2], o_hbm.at[k // nt, pl.ds((k % nt) * tm, tm)],
                                          sem_out.at[k % 2])]

        def compute(k):
            slot = k % STREAM_BUFS
            o_buf[k % 2] = (g_buf[slot].astype(F32) + s_buf[slot].astype(F32)).astype(BF16)

        _stream(3 * nt, loads, stores, compute)

    hbm = pl.BlockSpec(memory_space=pl.ANY)
    return pl.pallas_call(
        body, name=name,
        grid_spec=pltpu.PrefetchScalarGridSpec(
            num_scalar_prefetch=1, grid=(1,), in_specs=[hbm, hbm], out_specs=hbm,
            scratch_shapes=[pltpu.VMEM((STREAM_BUFS, tm, c), BF16), pltpu.VMEM((STREAM_BUFS, tm, c), BF16),
                            pltpu.VMEM((2, tm, c), BF16),
                            pltpu.SemaphoreType.DMA((STREAM_BUFS, 2)), pltpu.SemaphoreType.DMA((2,))]),
        out_shape=jax.ShapeDtypeStruct((3, r, c), BF16),
        compiler_params=_cp(("arbitrary",)),
    )(place, g, got)


def _exchange_chips(sums, name, collective_id):
    na = len(sums)

    def body(*refs):
        ins, outs = refs[:na], refs[na:2 * na]
        send_sems, recv_sems = refs[2 * na:]
        x, y, c, chips = _place()
        _handshake([(*chip, c) for chip in chips])
        cps = []
        for a in range(na):
            for j, chip in enumerate(chips):
                cps.append(pltpu.make_async_remote_copy(
                    src_ref=ins[a].at[j], dst_ref=outs[a].at[j],
                    send_sem=send_sems.at[a, j], recv_sem=recv_sems.at[a, j],
                    device_id=(*chip, c), device_id_type=MESH))
        for cp in cps:
            cp.start()
        for cp in cps:
            cp.wait()

    return _sequencer_call(
        body, name, collective_id,
        [jax.ShapeDtypeStruct((3,) + s.shape[1:], s.dtype) for s in sums],
        [pltpu.SemaphoreType.DMA((na, 3)), pltpu.SemaphoreType.DMA((na, 3))])(*sums)


def _exchange_stats(stats, collective_id):
    def body(st_in, st_out, st_send, st_recv, local_sem):
        x, y, c, _ = _place()
        me_idx = 4 * x + 2 * y + c
        peers = [(x ^ ((k >> 2) & 1), y ^ ((k >> 1) & 1), c ^ (k & 1)) for k in range(1, 8)]
        _handshake(peers)
        mine = pltpu.make_async_copy(st_in, st_out.at[me_idx], local_sem)
        mine.start()
        cps = [pltpu.make_async_remote_copy(
            src_ref=st_in, dst_ref=st_out.at[me_idx], send_sem=st_send.at[k], recv_sem=st_recv.at[k],
            device_id=peer, device_id_type=MESH) for k, peer in enumerate(peers)]
        for cp in cps:
            cp.start()
        for cp in cps:
            cp.wait()
        mine.wait()

    return _sequencer_call(
        body, "exchange_stats", collective_id,
        jax.ShapeDtypeStruct((NDEV,) + stats.shape, stats.dtype),
        [pltpu.SemaphoreType.DMA((7,)), pltpu.SemaphoreType.DMA((7,)), pltpu.SemaphoreType.DMA])(stats)


class _Reduction:
    def __init__(self, place, first_collective_id, state):
        self.place = place
        self.ids = iter(range(first_collective_id, 32))
        self.state = state
        self.groups = {}
        self.updates = {}

    def next_id(self):
        return next(self.ids)

    def start(self, group, grads):
        got = _exchange_sibling(grads, "sibling_exchange_" + group[0], self.next_id())
        self.groups[group[0]] = dict(names=group, grads=grads, got=got)

    def local(self, name, first=()):
        grp = self.groups[name]
        grads = lax.optimization_barrier((tuple(grp["grads"]), tuple(first)))[0]
        grp["sums"] = [_chip_sum(self.place, g, s, "chip_sum_" + n)
                       for g, s, n in zip(grads, grp["got"], grp["names"])]
        grp["chips"] = _exchange_chips(grp["sums"], "chip_exchange_" + name, self.next_id())
        return grp["sums"]

    def landed(self, name):
        return list(self.groups[name]["chips"])

    def rider(self, name):
        grp = next(g for g in self.groups.values() if name in g["names"])
        k = grp["names"].index(name)
        return self.state[name][:3] + (grp["grads"][k], grp["got"][k], grp["chips"][k])

    def set_update(self, name, outs):
        self.updates[name] = list(outs)

    def update(self, name):
        if name not in self.updates:
            grp = next(g for g in self.groups.values() if name in g["names"])
            k = grp["names"].index(name)
            w, m, v, part, parts = self.state[name]
            before = self.update(f"{name[:-1]}{part - 1}") if part else None
            self.updates[name] = _shard_update(self.place, w, m, v, grp["grads"][k], grp["got"][k],
                                               grp["chips"][k], "update_" + name, part, parts, before)
        return list(self.updates[name])


def _adamw(w, g, m, v):
    m = ADAM_B1 * m + (1.0 - ADAM_B1) * g
    v = ADAM_B2 * v + (1.0 - ADAM_B2) * (g * g)
    m_hat = m / (1.0 - ADAM_B1 ** ADAM_STEP)
    v_hat = v / (1.0 - ADAM_B2 ** ADAM_STEP)
    delta = -ADAM_LR * (m_hat / (jnp.sqrt(v_hat) + ADAM_EPS) + ADAM_WD * w)
    return delta, m, v


def _update_tile(w_ref, m_ref, v_ref, g_ref, s_ref, c_ref, go_ref, d_ref, mo_ref, vo_ref):
    grad = g_ref[...].astype(F32) + s_ref[...].astype(F32)
    for j in range(3):
        grad = grad + c_ref[j].astype(F32)
    delta, mn, vn = _adamw(w_ref[...], grad, m_ref[...], v_ref[...])
    go_ref[...] = grad
    d_ref[...] = delta
    mo_ref[...] = mn
    vo_ref[...] = vn


def _shard_update(place, w, m, v, g, got_sib, got_chips, name, part=0, parts=1, before=None):
    r, c = w.shape
    rp = r // parts
    tm = _stream_tile(rp, 8)
    nt = rp // tm
    before = list(before or [])

    def body(pos_ref, w_hbm, m_hbm, v_hbm, g_hbm, s_hbm, c_hbm, *rest):
        outs = rest[len(before):len(before) + 4]
        w_buf, m_buf, v_buf, g_buf, s_buf, c_buf, o_buf, sem_in, sem_out = rest[len(before) + 4:]
        own = 4 * pos_ref[0] + 2 * pos_ref[1] + pos_ref[2]
        chip = 2 * pos_ref[0] + pos_ref[1]

        def loads(k):
            slot, rows, mine = k % STREAM_BUFS, pl.ds(k * tm, tm), pl.ds(part * rp + k * tm, tm)
            pairs = [(w_hbm.at[mine], w_buf), (m_hbm.at[mine], m_buf), (v_hbm.at[mine], v_buf),
                     (g_hbm.at[own, rows], g_buf), (s_hbm.at[chip, rows], s_buf), (c_hbm.at[:, rows], c_buf)]
            return [pltpu.make_async_copy(src, buf.at[slot], sem_in.at[slot, n]) for n, (src, buf) in enumerate(pairs)]

        def stores(k):
            mine = pl.ds(part * rp + k * tm, tm)
            return [pltpu.make_async_copy(o_buf.at[k % 2, n], out.at[mine], sem_out.at[k % 2, n])
                    for n, out in enumerate(outs)]

        def compute(k):
            slot = k % STREAM_BUFS
            _update_tile(w_buf.at[slot], m_buf.at[slot], v_buf.at[slot], g_buf.at[slot], s_buf.at[slot],
                         c_buf.at[slot], *[o_buf.at[k % 2, n] for n in range(4)])

        _stream(nt, loads, stores, compute)

    hbm = pl.BlockSpec(memory_space=pl.ANY)
    return pl.pallas_call(
        body, name=name,
        grid_spec=pltpu.PrefetchScalarGridSpec(
            num_scalar_prefetch=1, grid=(1,), in_specs=[hbm] * (6 + len(before)), out_specs=[hbm] * 4,
            scratch_shapes=[pltpu.VMEM((STREAM_BUFS, tm, c), F32)] * 3 + [pltpu.VMEM((STREAM_BUFS, tm, c), BF16)] * 2
            + [pltpu.VMEM((STREAM_BUFS, 3, tm, c), BF16), pltpu.VMEM((2, 4, tm, c), F32),
               pltpu.SemaphoreType.DMA((STREAM_BUFS, 6)), pltpu.SemaphoreType.DMA((2, 4))]),
        out_shape=[jax.ShapeDtypeStruct((r, c), F32)] * 4,
        input_output_aliases={7 + k: k for k in range(len(before))},
        compiler_params=_cp(("arbitrary",)),
    )(place, w, m, v, g, got_sib, got_chips, *before)


def _small_update(stats_all, ws, ms, vs):
    def body(st_ref, w_ref, m_ref, v_ref, go_ref, d_ref, mo_ref, vo_ref):
        grad = st_ref[0]
        for k in range(1, NDEV):
            grad = grad + st_ref[k]
        delta, mn, vn = _adamw(w_ref[...], grad, m_ref[...], v_ref[...])
        go_ref[...] = grad
        d_ref[...] = delta
        mo_ref[...] = mn
        vo_ref[...] = vn

    return pl.pallas_call(
        body, name="small_update",
        out_shape=[jax.ShapeDtypeStruct((8, D), F32)] * 4,
        compiler_params=_cp(),
    )(stats_all, ws, ms, vs)


def kernel(x, norm_mix_w, w_in, w_out, norm_ffn_w, w_gate, w_up, w_down, norm_final_w, loss_target, m_norm_mix_w, m_w_in, m_w_out, m_norm_ffn_w, m_w_gate, m_w_up, m_w_down, m_norm_final_w, v_norm_mix_w, v_w_in, v_w_out, v_norm_ffn_w, v_w_gate, v_w_up, v_w_down, v_norm_final_w):
    tr = {"w_gate", "w_up"}
    names = ["w_in", "w_out", "w_gate", "w_up", "w_down"]

    def view(a, n):
        return a[0].T if n in tr else a[0]

    big_w = [view(a, n) for a, n in zip([w_in, w_out, w_gate, w_up, w_down], names)]
    big_m = [view(a, n) for a, n in zip([m_w_in, m_w_out, m_w_gate, m_w_up, m_w_down], names)]
    big_v = [view(a, n) for a, n in zip([v_w_in, v_w_out, v_w_gate, v_w_up, v_w_down], names)]

    shards = [None] + [_cast_bf16(w, "cast_" + n) for w, n in zip(big_w[1:], names[1:])]
    win = [_all_gather([cols], f"all_gather_w_in_{k}", 1 + k)[0]
           for k, cols in enumerate(_cast_cols(big_w[0], "cast_w_in"))]
    (wout,) = _all_gather(shards[1:2], "all_gather_w_out", 3)
    (wgu_a,) = _all_gather(shards[2:4], "all_gather_gate_up_0", 4, per=FF_PER, rows=(0, FF_ROWS))
    (wgu_b,) = _all_gather(shards[2:4], "all_gather_gate_up_1", 5, per=FF_PER, rows=(FF_ROWS, FF_ROWS))
    (wd_a,) = _all_gather(shards[4:5], "all_gather_w_down_0", 6, per=FF_PER, rows=(0, FF_ROWS))
    (wd_b,) = _all_gather(shards[4:5], "all_gather_w_down_1", 7, per=FF_PER, rows=(FF_ROWS, FF_ROWS))
    nw3 = norm_final_w.reshape(1, D)
    place = jnp.stack([lax.axis_index("x"), lax.axis_index("y"), lax.axis_index("c")]).astype(jnp.int32)
    state = {n: (w, m, v, 0, 1) for n, w, m, v in zip(names, big_w, big_m, big_v)}
    for part in range(W_IN_PARTS):
        state[f"w_in_{part}"] = state["w_in"][:3] + (part, W_IN_PARTS)
    red = _Reduction(place, 8, state)
    stats, gx, *_ = _local_step(
        x[0], loss_target[0], norm_mix_w, norm_ffn_w, nw3, win, wout.reshape(D, D),
        wgu_a.reshape(NFG // 2, 2 * N_FG, D), wgu_b.reshape(NFG // 2, 2 * N_FG, D),
        wd_a.reshape(NFG // 2, N_FG, D), wd_b.reshape(NFG // 2, N_FG, D), red)
    stats_all = _exchange_stats(stats, red.next_id())
    upd = [red.update(f"w_in_{W_IN_PARTS - 1}" if n == "w_in" else n) for n in names]
    stats_all = lax.optimization_barrier((stats_all, tuple(upd[0])))[0]

    def rows(a, b, c):
        return jnp.concatenate([a.reshape(1, D), b.reshape(1, D), c.reshape(1, D), jnp.zeros((5, D), F32)], axis=0)

    sg, sd, sm, sv = _small_update(stats_all, rows(norm_mix_w, norm_ffn_w, norm_final_w),
                                   rows(m_norm_mix_w, m_norm_ffn_w, m_norm_final_w),
                                   rows(v_norm_mix_w, v_norm_ffn_w, v_norm_final_w))
    loss = sg[3, 0]

    def outs(k, small):
        big = [(u[k].T if n in tr else u[k])[None] for u, n in zip(upd, names)]
        return [small[0:1], big[0], big[1], small[1:2], big[2], big[3], big[4], small[2]]

    return (loss, gx[None], *outs(0, sg), *outs(1, sd), *outs(2, sm), *outs(3, sv))
```

```python
import math

import numpy as np
import jax
import jax.numpy as jnp
from jax import lax
from jax.experimental import pallas as pl
from jax.experimental.pallas import tpu as pltpu
from jax.experimental.pallas import tpu_sc as plsc

F32 = jnp.float32
BF16 = jnp.bfloat16

S = 2048
D = 2048
NDEV = 8
N_IN = 7168 // NDEV
N_FF = 5632 // NDEV
NFG, N_FG = NDEV // 2, 2 * N_FF
FF_PER, FF_ROWS = 4, N_FF // 2
IN_ROUNDS = ((0, 512), (512, N_IN - 512))
N_OUT = 2048 // NDEV
AH, AHD = 8, 128
RH, RHD = 4, 256
CH = 128
NB = S // CH
EPS = 1e-6
PATTERNS = ((1, 16), (4, 4), (16, 1))
NEG = -1e30
VMEM_LIMIT = 56 * 1024 * 1024

ADAM_LR, ADAM_B1, ADAM_B2, ADAM_EPS, ADAM_WD, ADAM_STEP = 0.001, 0.9, 0.999, 1e-08, 0.01, 10
MESH = pl.DeviceIdType.MESH


def _cp(sem=None):
    return pltpu.CompilerParams(dimension_semantics=sem, vmem_limit_bytes=VMEM_LIMIT)


def _dot(a, b):
    return jnp.dot(a, b, preferred_element_type=F32)


def _dot_nt(a, b):
    return lax.dot_general(a, b, (((1,), (1,)), ((), ())), preferred_element_type=F32)


def _dot_tn(a, b):
    return lax.dot_general(a, b, (((0,), (0,)), ((), ())), preferred_element_type=F32)


def _sigmoid(x):
    return 0.5 * jnp.tanh(0.5 * x) + 0.5


def _cast_bf16(w, name):
    r, c = w.shape
    tm = r if r <= 1024 else 512

    def body(w_ref, o_ref):
        o_ref[...] = w_ref[...].astype(BF16)

    return pl.pallas_call(
        body, name=name, grid=(r // tm,),
        in_specs=[pl.BlockSpec((tm, c), lambda i: (i, 0))],
        out_specs=pl.BlockSpec((tm, c), lambda i: (i, 0)),
        out_shape=jax.ShapeDtypeStruct((r, c), BF16),
        compiler_params=_cp(("parallel",)),
    )(w)


def _rms_fwd(x, nw):
    tm = 256

    def body(x_ref, w_ref, h_ref, r_ref):
        xs = x_ref[...]
        r = lax.rsqrt(jnp.mean(xs * xs, axis=-1, keepdims=True) + EPS)
        h_ref[...] = ((xs * r) * w_ref[...]).astype(BF16)
        r_ref[...] = r

    return pl.pallas_call(
        body, name="rms_fwd", grid=(S // tm,),
        in_specs=[pl.BlockSpec((tm, D), lambda i: (i, 0)), pl.BlockSpec((1, D), lambda i: (0, 0))],
        out_specs=[pl.BlockSpec((tm, D), lambda i: (i, 0)), pl.BlockSpec((tm, 1), lambda i: (i, 0))],
        out_shape=[jax.ShapeDtypeStruct((S, D), BF16), jax.ShapeDtypeStruct((S, 1), F32)],
        compiler_params=_cp(("parallel",)),
    )(x, nw)


def _row_copies(hbm_refs, bufs, sems, m, tm):
    rows = pl.ds(pl.multiple_of(m * tm, tm), tm)
    return [pltpu.make_async_copy(h.at[rows], b, sems.at[i]) for i, (h, b) in enumerate(zip(hbm_refs, bufs))]


def _rms_bwd_tile(dh, xs, r, nw):
    dnw = jnp.sum(dh * (xs * r), axis=0, keepdims=True)
    gy = dh * nw
    dx = r * gy - xs * ((r * r * r) * jnp.mean(gy * xs, axis=-1, keepdims=True))
    return dx, dnw


def _cast_cols(w, name):
    r, c = w.shape
    tm = 512

    def body(w_ref, *o_refs):
        for o_ref, (off, width) in zip(o_refs, IN_ROUNDS):
            o_ref[...] = w_ref[:, off:off + width].astype(BF16)

    return pl.pallas_call(
        body, name=name, grid=(r // tm,),
        in_specs=[pl.BlockSpec((tm, c), lambda i: (i, 0))],
        out_specs=[pl.BlockSpec((tm, width), lambda i: (i, 0)) for _, width in IN_ROUNDS],
        out_shape=[jax.ShapeDtypeStruct((r, width), BF16) for _, width in IN_ROUNDS],
        compiler_params=_cp(("parallel",)),
    )(w)


def _proj_round(h1, win, k, before):
    tm = 1024
    nm = S // tm
    off, width = IN_ROUNDS[k]
    before = [] if before is None else [before]

    def body(a_ref, w_ref, *rest):
        o_hbm, o_buf, sems = rest[-3:]
        p, m = pl.program_id(0), pl.program_id(1)
        t = p * nm + m

        def out_copy(pp, mm, slot):
            cols = pl.ds(pl.multiple_of(pp * N_IN + off, 128), width)
            return pltpu.make_async_copy(o_buf.at[slot], o_hbm.at[pl.ds(pl.multiple_of(mm * tm, tm), tm), cols],
                                         sems.at[slot])

        @pl.when(t >= 2)
        def _():
            out_copy(p, m, t % 2).wait()

        o_buf[t % 2] = _dot(a_ref[...], w_ref[...])
        out_copy(p, m, t % 2).start()

        @pl.when(t == NDEV * nm - 1)
        def _():
            out_copy(p, m, (t + 1) % 2).wait()
            out_copy(p, m, t % 2).wait()

    return pl.pallas_call(
        body, name=f"proj_{k}", grid=(NDEV, nm),
        in_specs=[pl.BlockSpec((tm, D), lambda p, m: (m, 0)),
                  pl.BlockSpec((None, D, width), lambda p, m: (p, 0, 0))]
        + [pl.BlockSpec(memory_space=pl.ANY)] * len(before),
        out_specs=pl.BlockSpec(memory_space=pl.ANY),
        out_shape=jax.ShapeDtypeStruct((S, NDEV * N_IN), F32),
        scratch_shapes=[pltpu.VMEM((2, tm, width), F32), pltpu.SemaphoreType.DMA((2,))],
        input_output_aliases={2: 0} if before else {},
        compiler_params=_cp(("arbitrary", "arbitrary")),
    )(h1, win, *before)


def _proj(h1, wins):
    out = None
    for k, win in enumerate(wins):
        out = _proj_round(h1, win, k, out)
    return out


def _out_proj_rms(x, ma, mr, wout, nw):
    tm = 256
    half = D // 2

    def body(x_ref, ma_ref, mr_ref, w_ref, nw_ref, x2_ref, h_ref, r_ref):
        acc = _dot(ma_ref[...], w_ref[0:half, :]) + _dot(mr_ref[...], w_ref[half:D, :])
        x2 = x_ref[...] + acc
        r = lax.rsqrt(jnp.mean(x2 * x2, axis=-1, keepdims=True) + EPS)
        x2_ref[...] = x2
        h_ref[...] = ((x2 * r) * nw_ref[...]).astype(BF16)
        r_ref[...] = r

    return pl.pallas_call(
        body, name="out_proj_rms", grid=(S // tm,),
        in_specs=[pl.BlockSpec((tm, D), lambda i: (i, 0)),
                  pl.BlockSpec((tm, half), lambda i: (i, 0)),
                  pl.BlockSpec((tm, half), lambda i: (i, 0)),
                  pl.BlockSpec((D, D), lambda i: (0, 0)),
                  pl.BlockSpec((1, D), lambda i: (0, 0))],
        out_specs=[pl.BlockSpec((tm, D), lambda i: (i, 0)), pl.BlockSpec((tm, D), lambda i: (i, 0)),
                   pl.BlockSpec((tm, 1), lambda i: (i, 0))],
        out_shape=[jax.ShapeDtypeStruct((S, D), F32), jax.ShapeDtypeStruct((S, D), BF16),
                   jax.ShapeDtypeStruct((S, 1), F32)],
        compiler_params=_cp(("parallel",)),
    )(x, ma, mr, wout, nw)


def _ffn_up(h2, wgu, part, before=None):
    tm = 512

    def body(h_ref, w_ref, *rest):
        a_ref, dadg_ref, dadu_ref = rest[-3:]
        gu = _dot_nt(h_ref[...], w_ref[...])
        g, u = gu[:, 0:N_FG], gu[:, N_FG:2 * N_FG]
        sg = _sigmoid(g)
        silu = g * sg
        a_ref[...] = (silu * u).astype(BF16)
        dadg_ref[...] = (u * (sg * (1.0 + g * (1.0 - sg)))).astype(BF16)
        dadu_ref[...] = silu.astype(BF16)

    half = NFG // 2
    first = part * half
    before = list(before or [])
    blk = pl.BlockSpec((None, tm, N_FG), lambda p, m: (p + first, m, 0))
    return pl.pallas_call(
        body, name=f"ffn_up_{part}", grid=(half, S // tm),
        in_specs=[pl.BlockSpec((tm, D), lambda p, m: (m, 0)),
                  pl.BlockSpec((None, 2 * N_FG, D), lambda p, m: (p, 0, 0))]
        + [pl.BlockSpec(memory_space=pl.ANY)] * len(before),
        out_specs=[blk, blk, blk],
        out_shape=[jax.ShapeDtypeStruct((NFG, S, N_FG), BF16)] * 3,
        input_output_aliases={2 + k: k for k in range(len(before))},
        compiler_params=_cp(("parallel", "parallel")),
    )(h2, wgu, *before)


def _ffn_down_first(x2, a, wd):
    tm = 512
    n = wd.shape[0]

    def body(x_ref, a_ref, w_ref, o_ref):
        p = pl.program_id(1)

        @pl.when(p == 0)
        def _():
            o_ref[...] = x_ref[...] + _dot(a_ref[...], w_ref[0])

        @pl.when(p > 0)
        def _():
            o_ref[...] += _dot(a_ref[...], w_ref[p])

    return pl.pallas_call(
        body, name="ffn_down_first", grid=(S // tm, n),
        in_specs=[pl.BlockSpec((tm, D), lambda m, p: (m, 0)),
                  pl.BlockSpec((None, tm, N_FG), lambda m, p: (p, m, 0)),
                  pl.BlockSpec((n, N_FG, D), lambda m, p: (0, 0, 0))],
        out_specs=pl.BlockSpec((tm, D), lambda m, p: (m, 0)),
        out_shape=jax.ShapeDtypeStruct((S, D), F32),
        compiler_params=_cp(("parallel", "arbitrary")),
    )(x2, a, wd)


def _ffn_down_loss(x2, a, wd, nw, tgt):
    tm = 512
    first = NFG - wd.shape[0]

    def body(x2_hbm, a_ref, w_ref, nw_ref, t_hbm, dx_ref, dxb_ref, st_ref, acc_ref, x2_buf, t_buf, sems):
        m, p = pl.program_id(0), pl.program_id(1)
        tail_in = _row_copies((x2_hbm, t_hbm), (x2_buf, t_buf), sems, m, tm)

        @pl.when(p == 0)
        def _():
            acc_ref[...] = jnp.zeros_like(acc_ref)
            for cp in tail_in:
                cp.start()

        @pl.when((p == 0) & (m == 0))
        def _():
            st_ref[...] = jnp.zeros_like(st_ref)

        acc_ref[...] += _dot(a_ref[...], w_ref[p])

        @pl.when(p == NFG - first - 1)
        def _():
            for cp in tail_in:
                cp.wait()
            x3 = x2_buf[...] + acc_ref[...]
            nwv = nw_ref[...]
            r = lax.rsqrt(jnp.mean(x3 * x3, axis=-1, keepdims=True) + EPS)
            y = (x3 * r) * nwv
            err = y - t_buf[...]
            loss = 0.5 * jnp.sum(jnp.mean(err * err, axis=-1, keepdims=True), axis=0, keepdims=True)
            dy = err * (1.0 / D)
            dx, dnw = _rms_bwd_tile(dy, x3, r, nwv)
            dx_ref[...] = dx
            dxb_ref[...] = dx.astype(BF16)
            st_ref[0:1, :] += dnw
            st_ref[1:2, :] += jnp.broadcast_to(loss, (1, D))

    return pl.pallas_call(
        body, name="ffn_down_loss", grid=(S // tm, NFG - first),
        in_specs=[pl.BlockSpec(memory_space=pl.ANY),
                  pl.BlockSpec((None, tm, N_FG), lambda m, p: (p + first, m, 0)),
                  pl.BlockSpec((NFG - first, N_FG, D), lambda m, p: (0, 0, 0)),
                  pl.BlockSpec((1, D), lambda m, p: (0, 0)),
                  pl.BlockSpec(memory_space=pl.ANY)],
        out_specs=[pl.BlockSpec((tm, D), lambda m, p: (m, 0)), pl.BlockSpec((tm, D), lambda m, p: (m, 0)),
                   pl.BlockSpec((8, D), lambda m, p: (0, 0))],
        out_shape=[jax.ShapeDtypeStruct((S, D), F32), jax.ShapeDtypeStruct((S, D), BF16),
                   jax.ShapeDtypeStruct((8, D), F32)],
        scratch_shapes=[pltpu.VMEM((tm, D), F32), pltpu.VMEM((tm, D), F32), pltpu.VMEM((tm, D), F32),
                        pltpu.SemaphoreType.DMA((2,))],
        compiler_params=_cp(("arbitrary", "arbitrary")),
    )(x2, a, wd, nw, tgt)


def _ffn_down_bwd(dx3b, wd, dadg, dadu, part, before=None):
    tm = 1024
    half = NFG // 2

    def body(dx_ref, w_ref, dadg_ref, dadu_ref, *rest):
        dgu_ref = rest[-1]
        rows = pl.ds(pl.multiple_of(pl.program_id(1) * tm, tm), tm)
        da = _dot_nt(dx_ref[rows, :], w_ref[...])
        dgu_ref[:, 0:N_FG] = (da * dadg_ref[...].astype(F32)).astype(BF16)
        dgu_ref[:, N_FG:2 * N_FG] = (da * dadu_ref[...].astype(F32)).astype(BF16)

    blk = pl.BlockSpec((None, tm, N_FG), lambda p, m: (p + part * half, m, 0))
    before = list(before or [])
    return pl.pallas_call(
        body, name=f"ffn_down_bwd_{part}", grid=(half, S // tm),
        in_specs=[pl.BlockSpec((S, D), lambda p, m: (0, 0)),
                  pl.BlockSpec((None, N_FG, D), lambda p, m: (p, 0, 0)), blk, blk]
        + [pl.BlockSpec(memory_space=pl.ANY)] * len(before),
        out_specs=pl.BlockSpec((None, tm, 2 * N_FG), lambda p, m: (p + part * half, m, 0)),
        out_shape=jax.ShapeDtypeStruct((NFG, S, 2 * N_FG), BF16),
        input_output_aliases={4 + k: k for k in range(len(before))},
        compiler_params=_cp(("parallel", "parallel")),
    )(dx3b, wd, dadg, dadu, *before)


def _ffn_up_bwd(dgu, wgu_a, wgu_b, dres, xs, r, nw):
    tm = 512
    nm = S // tm
    na = wgu_a.shape[0]

    def body(dgu_ref, wa_hbm, wb_hbm, dres_hbm, x_hbm, r_ref, nw_ref, dx_ref, dxb_ref, st_ref,
             w_buf, dres_buf, x_buf, sems, w_sems):
        m, p = pl.program_id(0), pl.program_id(1)
        tail_in = _row_copies((dres_hbm, x_hbm), (dres_buf, x_buf), sems, m, tm)

        def fetch(g, slot):
            for src, lo in ((wa_hbm, 0), (wb_hbm, na)):
                @pl.when((g >= lo) & (g < lo + na))
                def _():
                    pltpu.make_async_copy(src.at[g - lo], w_buf.at[slot], w_sems.at[slot]).start()

        @pl.when((p == 0) & (m == 0))
        def _():
            st_ref[...] = jnp.zeros_like(st_ref)
            fetch(p, 0)

        @pl.when((p < NFG - 1) | (m < nm - 1))
        def _():
            fetch((p + 1) % NFG, (p + 1) % 2)

        @pl.when(p == 0)
        def _():
            dx_ref[...] = jnp.zeros_like(dx_ref)
            for cp in tail_in:
                cp.start()

        slot = p % 2
        pltpu.make_async_copy(wa_hbm.at[0], w_buf.at[slot], w_sems.at[slot]).wait()
        dx_ref[...] += _dot(dgu_ref[...], w_buf[slot])

        @pl.when(p == NFG - 1)
        def _():
            for cp in tail_in:
                cp.wait()
            dx, dnw = _rms_bwd_tile(dx_ref[...], x_buf[...], r_ref[...], nw_ref[...])
            dx = dres_buf[...] + dx
            dx_ref[...] = dx
            dxb_ref[...] = dx.astype(BF16)
            st_ref[0:1, :] += dnw

    blk = pl.BlockSpec((None, tm, 2 * N_FG), lambda m, p: (p, m, 0))
    row = pl.BlockSpec((tm, D), lambda m, p: (m, 0))
    hbm = pl.BlockSpec(memory_space=pl.ANY)
    return pl.pallas_call(
        body, name="ffn_up_bwd", grid=(nm, NFG),
        in_specs=[blk, hbm, hbm, hbm, hbm, pl.BlockSpec((tm, 1), lambda m, p: (m, 0)),
                  pl.BlockSpec((1, D), lambda m, p: (0, 0))],
        out_specs=[row, row, pl.BlockSpec((8, D), lambda m, p: (0, 0))],
        out_shape=[jax.ShapeDtypeStruct((S, D), F32), jax.ShapeDtypeStruct((S, D), BF16),
                   jax.ShapeDtypeStruct((8, D), F32)],
        scratch_shapes=[pltpu.VMEM((2, 2 * N_FG, D), BF16), pltpu.VMEM((tm, D), F32), pltpu.VMEM((tm, D), F32),
                        pltpu.SemaphoreType.DMA((2,)), pltpu.SemaphoreType.DMA((2,))],
        compiler_params=_cp(("arbitrary", "arbitrary")),
    )(dgu, wgu_a, wgu_b, dres, xs, r, nw)


def _out_proj_bwd(dx2b, wout, place=None, rider=None):
    tm = 256

    if rider is None:
        def body(dx_ref, w_ref, o_ref):
            o_ref[...] = _dot_nt(dx_ref[...], w_ref[...])

        return pl.pallas_call(
            body, name="out_proj_bwd", grid=(S // tm,),
            in_specs=[pl.BlockSpec((tm, D), lambda i: (i, 0)), pl.BlockSpec((D, D), lambda i: (0, 0))],
            out_specs=pl.BlockSpec((tm, D), lambda i: (i, 0)),
            out_shape=jax.ShapeDtypeStruct((S, D), F32),
            compiler_params=_cp(("parallel",)),
        )(dx2b, wout), None

    w = rider[0]
    r, c = w.shape
    rt = _row_tile(r, c)
    nt = r // rt
    assert nt <= S // tm

    def body(pos_ref, dx_ref, w_ref, uw, um, uv, ug, us, uc, o_ref, go, dd, mo, vo):
        o_ref[...] = _dot_nt(dx_ref[...], w_ref[...])

        @pl.when(pl.program_id(0) < nt)
        def _():
            _update_tile(uw, um, uv, ug, us, uc, go, dd, mo, vo)

    def at(i):
        return jnp.minimum(i, nt - 1)

    tile = pl.BlockSpec((rt, c), lambda i, pos: (at(i), 0))
    outs = pl.pallas_call(
        body, name="out_proj_bwd",
        grid_spec=pltpu.PrefetchScalarGridSpec(
            num_scalar_prefetch=1, grid=(S // tm,),
            in_specs=[pl.BlockSpec((tm, D), lambda i, pos: (i, 0)), pl.BlockSpec((D, D), lambda i, pos: (0, 0)),
                      tile, tile, tile,
                      pl.BlockSpec((None, rt, c), lambda i, pos: (4 * pos[0] + 2 * pos[1] + pos[2], at(i), 0)),
                      pl.BlockSpec((None, rt, c), lambda i, pos: (2 * pos[0] + pos[1], at(i), 0)),
                      pl.BlockSpec((3, rt, c), lambda i, pos: (0, at(i), 0))],
            out_specs=[pl.BlockSpec((tm, D), lambda i, pos: (i, 0)), tile, tile, tile, tile]),
        out_shape=[jax.ShapeDtypeStruct((S, D), F32)] + [jax.ShapeDtypeStruct((r, c), F32)] * 4,
        compiler_params=_cp(("arbitrary",)),
    )(place, dx2b, wout, *rider)
    return outs[0], outs[1:]


def _in_proj_bwd(dproj, wins, dres, xs, r, nw):
    tm = 1024

    nr = len(wins)
    nm = S // tm

    def body(dp_ref, *rest):
        w_hbms = rest[:nr]
        dres_hbm, x_hbm, r_ref, nw_ref, dx_ref, st_ref, w_buf, dres_buf, x_buf, sems, w_sems = rest[nr:]
        m, p = pl.program_id(0), pl.program_id(1)
        tail_in = _row_copies((dres_hbm, x_hbm), (dres_buf, x_buf), sems, m, tm)

        def w_copies(g, slot):
            return [pltpu.make_async_copy(w_hbm.at[g], w_buf.at[slot, pl.ds(0, D), pl.ds(off, width)],
                                          w_sems.at[slot, k])
                    for k, (w_hbm, (off, width)) in enumerate(zip(w_hbms, IN_ROUNDS))]

        @pl.when((p == 0) & (m == 0))
        def _():
            st_ref[...] = jnp.zeros_like(st_ref)
            for cp in w_copies(p, 0):
                cp.start()

        @pl.when((p < NDEV - 1) | (m < nm - 1))
        def _():
            for cp in w_copies((p + 1) % NDEV, (p + 1) % 2):
                cp.start()

        @pl.when(p == 0)
        def _():
            dx_ref[...] = jnp.zeros_like(dx_ref)
            for cp in tail_in:
                cp.start()

        for cp in w_copies(p, p % 2):
            cp.wait()
        dx_ref[...] += _dot_nt(dp_ref[...], w_buf[p % 2])

        @pl.when(p == NDEV - 1)
        def _():
            for cp in tail_in:
                cp.wait()
            dx, dnw = _rms_bwd_tile(dx_ref[...], x_buf[...], r_ref[...], nw_ref[...])
            dx_ref[...] = dres_buf[...] + dx
            st_ref[0:1, :] += dnw

    row = pl.BlockSpec((tm, D), lambda m, p: (m, 0))
    hbm = pl.BlockSpec(memory_space=pl.ANY)
    return pl.pallas_call(
        body, name="in_proj_bwd", grid=(S // tm, NDEV),
        in_specs=[pl.BlockSpec((tm, N_IN), lambda m, p: (m, p)),
                  *[hbm] * nr,
                  hbm, hbm, pl.BlockSpec((tm, 1), lambda m, p: (m, 0)),
                  pl.BlockSpec((1, D), lambda m, p: (0, 0))],
        out_specs=[row, pl.BlockSpec((8, D), lambda m, p: (0, 0))],
        out_shape=[jax.ShapeDtypeStruct((S, D), F32), jax.ShapeDtypeStruct((8, D), F32)],
        scratch_shapes=[pltpu.VMEM((2, D, N_IN), BF16), pltpu.VMEM((tm, D), F32), pltpu.VMEM((tm, D), F32),
                        pltpu.SemaphoreType.DMA((2,)), pltpu.SemaphoreType.DMA((2, nr))],
        compiler_params=_cp(("arbitrary", "arbitrary")),
    )(dproj, *wins, dres, xs, r, nw)


W_IN_PARTS = 2


def _wgrad_in(h1, dproj, part):
    rows = D // W_IN_PARTS

    def body(a_ref, d_ref, o_ref):
        both = _dot_tn(a_ref[...], d_ref[...]).astype(BF16)
        o_ref[0] = both[:, 0:N_IN]
        o_ref[1] = both[:, N_IN:2 * N_IN]

    return pl.pallas_call(
        body, name=f"wgrad_in_{part}", grid=(NDEV // 2,),
        in_specs=[pl.BlockSpec((S, rows), lambda p: (0, part)), pl.BlockSpec((S, 2 * N_IN), lambda p: (0, p))],
        out_specs=pl.BlockSpec((2, rows, N_IN), lambda p: (p, 0, 0)),
        out_shape=jax.ShapeDtypeStruct((NDEV, rows, N_IN), BF16),
        compiler_params=_cp(("parallel",)),
    )(h1, dproj)


def _wgrad_rows(a3, dy, name, col=0):
    def body(a_ref, d_ref, o_ref):
        dw = _dot_tn(a_ref[...], d_ref[...]).astype(BF16)
        for j in range(FF_PER):
            o_ref[j] = dw[j * FF_ROWS:(j + 1) * FF_ROWS]

    return pl.pallas_call(
        body, name=name, grid=(NFG,),
        in_specs=[pl.BlockSpec((None, S, N_FG), lambda p: (p, 0, col)), pl.BlockSpec((S, D), lambda p: (0, 0))],
        out_specs=pl.BlockSpec((FF_PER, FF_ROWS, D), lambda p: (p % 2, p // 2, 0)),
        out_shape=jax.ShapeDtypeStruct((NDEV, N_FF, D), BF16),
        compiler_params=_cp(("parallel",)),
    )(a3, dy)


def _wgrad_out(ma, mr, dx2b):
    half = D // 2
    per = half // N_OUT

    def body(ma_ref, mr_ref, d_ref, o_ref):
        p = pl.program_id(0)

        @pl.when(p == 0)
        def _():
            o_ref[...] = _dot_tn(ma_ref[...], d_ref[...]).astype(BF16).reshape(per, N_OUT, D)

        @pl.when(p == 1)
        def _():
            o_ref[...] = _dot_tn(mr_ref[...], d_ref[...]).astype(BF16).reshape(per, N_OUT, D)

    whole = pl.BlockSpec((S, half), lambda p: (0, 0))
    return pl.pallas_call(
        body, name="wgrad_out", grid=(2,),
        in_specs=[whole, whole, pl.BlockSpec((S, D), lambda p: (0, 0))],
        out_specs=pl.BlockSpec((per, N_OUT, D), lambda p: (p, 0, 0)),
        out_shape=jax.ShapeDtypeStruct((NDEV, N_OUT, D), BF16),
        compiler_params=_cp(("parallel",)),
    )(ma, mr, dx2b)


def _attn_consts():
    c = np.zeros((AH, 8, AHD), np.float32)
    for h in range(AH):
        c[h, :, :] = 2.0 ** (-(h + 1))
    return jnp.asarray(c)


def _permute_in(dst, src, d, cast=None):
    v = src[...]
    if d > 1:
        v = pltpu.einshape("jrc->rjc", v.reshape(S // d, d, AHD)).reshape(S, AHD)
    dst[...] = v if cast is None else v.astype(cast)


def _natural_order(v, d):
    if d == 1:
        return v
    return pltpu.einshape("rjc->jrc", v.reshape(d, S // d, AHD)).reshape(S, AHD)


def _attn_masks():
    qi = lax.broadcasted_iota(jnp.int32, (CH, CH), 0)
    kj = lax.broadcasted_iota(jnp.int32, (CH, CH), 1)
    dist_c = (qi - kj).astype(F32)
    dist_p = (qi - kj + CH).astype(F32)
    return (qi >= kj)[None], (kj >= qi)[None], dist_c[None], dist_p[None]


GB = 16


def _bdot_nt(a, b):
    return lax.dot_general(a, b, (((2,), (2,)), ((0,), (0,))), preferred_element_type=F32)


def _bdot(a, b):
    return lax.dot_general(a, b, (((2,), (1,)), ((0,), (0,))), preferred_element_type=F32)


def _bdot_tn(a, b):
    return lax.dot_general(a, b, (((1,), (1,)), ((0,), (0,))), preferred_element_type=F32)


def _shift_block(dst, src):
    dst[0:CH, :] = jnp.zeros((CH, AHD), dst.dtype)
    dst[CH:S, :] = src[0:S - CH, :]


def _has_prev(g, nb):
    blk = lax.broadcasted_iota(jnp.int32, (GB, 1, 1), 0) + g * GB
    return (blk & (nb - 1)) != 0


def _blocks(ref, g):
    return ref[g * GB * CH:(g + 1) * GB * CH, :].reshape(GB, CH, AHD)


def _attn_fwd(proj):
    scale = 1.0 / math.sqrt(AHD)

    def body(c_ref, q_ref, k_ref, v_ref, o_ref, ob_ref, lse_ref, qkvp_ref, lsep_ref, qd, kd, vd, kps, vps, od, ld, *nat):
        onat, lnat = nat[0:3], nat[3:6]
        slope = c_ref[0:1, :]
        mask_c, mask_p, dist_c, dist_p = _attn_masks()
        for pi, (d, nb) in enumerate(PATTERNS):
            _permute_in(qd, q_ref, d, BF16)
            _permute_in(kd, k_ref, d, BF16)
            _permute_in(vd, v_ref, d, BF16)
            if d > 1:
                qkvp_ref[pi - 1, 0] = qd[...]
                qkvp_ref[pi - 1, 1] = kd[...]
                qkvp_ref[pi - 1, 2] = vd[...]
            if nb > 1:
                _shift_block(kps, kd)
                _shift_block(vps, vd)
            bias_c = -(slope * float(d)) * dist_c
            bias_p = -(slope * float(d)) * dist_p
            for g in range(NB // GB):
                q3, k3, v3 = _blocks(qd, g), _blocks(kd, g), _blocks(vd, g)
                s_c = jnp.where(mask_c, _bdot_nt(q3, k3) * scale + bias_c, NEG)
                mx = jnp.max(s_c, axis=-1, keepdims=True)
                if nb > 1:
                    kp3, vp3 = _blocks(kps, g), _blocks(vps, g)
                    s_p = jnp.where(jnp.logical_and(mask_p, _has_prev(g, nb)),
                                    _bdot_nt(q3, kp3) * scale + bias_p, NEG)
                    mx = jnp.maximum(mx, jnp.max(s_p, axis=-1, keepdims=True))
                    l = (jnp.sum(jnp.exp(s_c - mx), axis=-1, keepdims=True)
                         + jnp.sum(jnp.exp(s_p - mx), axis=-1, keepdims=True))
                    lse = mx + jnp.log(l)
                    o3 = _bdot(jnp.exp(s_c - lse).astype(BF16), v3) + _bdot(jnp.exp(s_p - lse).astype(BF16), vp3)
                else:
                    l = jnp.sum(jnp.exp(s_c - mx), axis=-1, keepdims=True)
                    lse = mx + jnp.log(l)
                    o3 = _bdot(jnp.exp(s_c - lse).astype(BF16), v3)
                rows = slice(g * GB * CH, (g + 1) * GB * CH)
                od[rows, :] = o3.reshape(GB * CH, AHD)
                ld[rows, :] = jnp.broadcast_to(lse, (GB, CH, AHD)).reshape(GB * CH, AHD)
            onat[pi][...] = _natural_order(od[...], d)
            lnat[pi][...] = _natural_order(ld[...], d)
        l0, l1, l2 = lnat[0][...], lnat[1][...], lnat[2][...]
        mx = jnp.maximum(jnp.maximum(l0, l1), l2)
        e0, e1, e2 = jnp.exp(l0 - mx), jnp.exp(l1 - mx), jnp.exp(l2 - mx)
        den = e0 + e1 + e2
        out = (e0 / den) * onat[0][...] + (e1 / den) * onat[1][...] + (e2 / den) * onat[2][...]
        o_ref[...] = out
        ob_ref[...] = out.astype(BF16)
        lse_ref[...] = mx + jnp.log(den)
        for pi, (d, _) in enumerate(PATTERNS[1:]):
            _permute_in(lsep_ref.at[pi], lse_ref, d)

    def col(off):
        return pl.BlockSpec((S, AHD), lambda h: (0, off + h))

    return pl.pallas_call(
        body, name="attn_fwd", grid=(AH,),
        in_specs=[pl.BlockSpec((None, 8, AHD), lambda h: (h, 0, 0)), col(0), col(AH), col(2 * AH)],
        out_specs=[col(0), col(0), col(0), pl.BlockSpec((2, 3, S, AHD), lambda h: (0, 0, 0, h)),
                   pl.BlockSpec((2, S, AHD), lambda h: (0, 0, h))],
        out_shape=[jax.ShapeDtypeStruct((S, AH * AHD), F32), jax.ShapeDtypeStruct((S, AH * AHD), BF16),
                   jax.ShapeDtypeStruct((S, AH * AHD), F32),
                   jax.ShapeDtypeStruct((2, 3, S, AH * AHD), BF16), jax.ShapeDtypeStruct((2, S, AH * AHD), F32)],
        scratch_shapes=[pltpu.VMEM((S, AHD), BF16) for _ in range(5)]
        + [pltpu.VMEM((S, AHD), F32) for _ in range(8)],
        compiler_params=_cp(("parallel",)),
    )(_attn_consts(), proj, proj, proj)


def _attn_bwd(proj, dmixed, o, lse, qkvp, lsep):
    scale = 1.0 / math.sqrt(AHD)

    def body(c_ref, q_ref, k_ref, v_ref, do_ref, o_ref, lse_ref, qkvp_ref, lsep_ref, dproj_hbm,
             qd, kd, vd, dod, kps, vps, dld, dqd, dkd, dvd, delta, aq, ak, av, sq, sk, sv, sems):
        h = pl.program_id(0)

        def out_copies(head):
            return [pltpu.make_async_copy(
                st, dproj_hbm.at[:, pl.ds(pl.multiple_of((k * AH + head) * AHD, AHD), AHD)], sems.at[k])
                for k, st in enumerate((sq, sk, sv))]

        slope = c_ref[0:1, :]
        mask_c, mask_p, dist_c, dist_p = _attn_masks()
        delta[...] = jnp.broadcast_to(jnp.sum(do_ref[...] * o_ref[...], axis=-1, keepdims=True), (S, AHD))
        for pi, (d, nb) in enumerate(PATTERNS):
            if d == 1:
                _permute_in(qd, q_ref, d, BF16)
                _permute_in(kd, k_ref, d, BF16)
                _permute_in(vd, v_ref, d, BF16)
                qs, ks, vs, lss = qd, kd, vd, lse_ref
            else:
                qs, ks, vs, lss = (qkvp_ref.at[pi - 1, 0], qkvp_ref.at[pi - 1, 1], qkvp_ref.at[pi - 1, 2],
                                   lsep_ref.at[pi - 1])
            _permute_in(dod, do_ref, d, BF16)
            _permute_in(dld, delta, d)
            if nb > 1:
                _shift_block(kps, ks)
                _shift_block(vps, vs)
            bias_c = -(slope * float(d)) * dist_c
            bias_p = -(slope * float(d)) * dist_p
            for g in range(NB // GB):
                q3, k3, v3, do3 = _blocks(qs, g), _blocks(ks, g), _blocks(vs, g), _blocks(dod, g)
                ls, dl = _blocks(lss, g), _blocks(dld, g)
                lo, hi = g * GB * CH, (g + 1) * GB * CH
                p_c = jnp.exp(jnp.where(mask_c, _bdot_nt(q3, k3) * scale + bias_c, NEG) - ls)
                ds_c = ((p_c * (_bdot_nt(do3, v3) - dl)) * scale).astype(BF16)
                dq3 = _bdot(ds_c, k3)
                dkd[lo:hi, :] = _bdot_tn(ds_c, q3).reshape(GB * CH, AHD)
                dvd[lo:hi, :] = _bdot_tn(p_c.astype(BF16), do3).reshape(GB * CH, AHD)
                if nb > 1:
                    kp3, vp3 = _blocks(kps, g), _blocks(vps, g)
                    p_p = jnp.exp(jnp.where(jnp.logical_and(mask_p, _has_prev(g, nb)),
                                            _bdot_nt(q3, kp3) * scale + bias_p, NEG) - ls)
                    ds_p = ((p_p * (_bdot_nt(do3, vp3) - dl)) * scale).astype(BF16)
                    dq3 = dq3 + _bdot(ds_p, kp3)
                    dkp = _bdot_tn(ds_p, q3).reshape(GB * CH, AHD)
                    dvp = _bdot_tn(p_p.astype(BF16), do3).reshape(GB * CH, AHD)
                    if g == 0:
                        dkd[0:hi - CH, :] += dkp[CH:, :]
                        dvd[0:hi - CH, :] += dvp[CH:, :]
                    else:
                        dkd[lo - CH:hi - CH, :] += dkp
                        dvd[lo - CH:hi - CH, :] += dvp
                dqd[lo:hi, :] = dq3.reshape(GB * CH, AHD)
            ln = S // d
            for acc, src in ((aq, dqd), (ak, dkd), (av, dvd)):
                if pi == 0:
                    acc[...] = src[...]
                else:
                    acc[...] += _natural_order(src[...], d)

        @pl.when(h > 0)
        def _():
            for cp in out_copies(h - 1):
                cp.wait()

        sq[...] = aq[...].astype(BF16)
        sk[...] = ak[...].astype(BF16)
        sv[...] = av[...].astype(BF16)
        for cp in out_copies(h):
            cp.start()

        @pl.when(h == AH - 1)
        def _():
            for cp in out_copies(h):
                cp.wait()

    def col(off):
        return pl.BlockSpec((S, AHD), lambda h: (0, off + h))

    return pl.pallas_call(
        body, name="attn_bwd", grid=(AH,),
        in_specs=[pl.BlockSpec((None, 8, AHD), lambda h: (h, 0, 0)), col(0), col(AH), col(2 * AH),
                  col(0), col(0), col(0), pl.BlockSpec((2, 3, S, AHD), lambda h: (0, 0, 0, h)),
                  pl.BlockSpec((2, S, AHD), lambda h: (0, 0, h))],
        out_specs=pl.BlockSpec(memory_space=pl.ANY),
        out_shape=jax.ShapeDtypeStruct((S, NDEV * N_IN), BF16),
        scratch_shapes=[pltpu.VMEM((S, AHD), BF16) for _ in range(6)]
        + [pltpu.VMEM((S, AHD), F32) for _ in range(8)]
        + [pltpu.VMEM((S, AHD), BF16) for _ in range(3)] + [pltpu.SemaphoreType.DMA((3,))],
        compiler_params=_cp(("arbitrary",)),
    )(_attn_consts(), proj, proj, proj, dmixed, o, lse, qkvp, lsep)


def _ret_consts():
    c = np.zeros((RH, 8, RHD), np.float32)
    for h in range(RH):
        c[h, :, :] = np.log(np.float32(1.0) - np.float32(2.0 ** (-5.0 - h)))
    return jnp.asarray(c)


def _ret_factors(lg):
    i = lax.broadcasted_iota(jnp.int32, (CH, CH), 0)
    j = lax.broadcasted_iota(jnp.int32, (CH, CH), 1)
    dif = (i - j).astype(F32)
    decay = jnp.where(dif >= 0, jnp.exp(lg[:, 0:CH] * jnp.maximum(dif, 0.0)), 0.0)
    row = lax.broadcasted_iota(jnp.int32, (CH, RHD), 0).astype(F32)
    zeta = jnp.exp(lg * (CH - 1.0 - row))
    xi = jnp.exp(lg * (row + 1.0))
    return decay, zeta, xi, jnp.exp(lg * float(CH))


CBK = 8
RSTEPS = NB // CBK


def _ret_specs(rev):
    off = 3 * AH * AHD // RHD
    rows = CBK * CH

    def ch(n):
        return (RSTEPS - 1 - n) if rev else n

    def col(k):
        return pl.BlockSpec((rows, RHD), lambda h, n: (ch(n), off + k * RH + h))

    own = pl.BlockSpec((rows, RHD), lambda h, n: (ch(n), h))
    state = pl.BlockSpec((None, CBK, RHD, RHD), lambda h, n: (h, ch(n), 0, 0))
    const = pl.BlockSpec((None, 8, RHD), lambda h, n: (h, 0, 0))
    dm = pl.BlockSpec((rows, RHD), lambda h, n: (ch(n), AH * AHD // RHD + h))
    return col, own, state, const, dm


def _chunks(x):
    return x.reshape(CBK, CH, RHD)


def _ret_fwd(proj):
    def body(c_ref, q_ref, k_ref, v_ref, g_ref, ret_ref, mr_ref, st_ref, r_acc):
        n = pl.program_id(1)

        @pl.when(n == 0)
        def _():
            r_acc[...] = jnp.zeros_like(r_acc)

        decay, zeta, xi, gch = _ret_factors(c_ref[0:1, :])
        q3 = _chunks(q_ref[...].astype(BF16))
        kc = _chunks(k_ref[...] * (1.0 / math.sqrt(RHD)))
        k3 = kc.astype(BF16)
        v3 = _chunks(v_ref[...].astype(BF16))
        kv3 = _bdot_tn((kc * zeta[None]).astype(BF16), v3)
        r = r_acc[...]
        for i in range(CBK):
            st_ref[i] = r.astype(BF16)
            r = r * gch + kv3[i]
        r_acc[...] = r
        scores = _bdot_nt(q3, k3) * decay[None]
        ret = (_bdot(scores.astype(BF16), v3) + _bdot(q3, st_ref[...]) * xi[None]).reshape(CBK * CH, RHD)
        ret_ref[...] = ret
        rr = lax.rsqrt(jnp.mean(ret * ret, axis=-1, keepdims=True) + EPS)
        gv = g_ref[...]
        mr_ref[...] = ((gv * _sigmoid(gv)) * (ret * rr)).astype(BF16)

    col, own, state, const, _ = _ret_specs(False)
    return pl.pallas_call(
        body, name="ret_fwd", grid=(RH, RSTEPS),
        in_specs=[const, col(0), col(1), col(2), col(3)],
        out_specs=[own, own, state],
        out_shape=[jax.ShapeDtypeStruct((S, RH * RHD), F32), jax.ShapeDtypeStruct((S, RH * RHD), BF16),
                   jax.ShapeDtypeStruct((RH, NB, RHD, RHD), BF16)],
        scratch_shapes=[pltpu.VMEM((RHD, RHD), F32)],
        compiler_params=_cp(("parallel", "arbitrary")),
    )(_ret_consts(), proj, proj, proj, proj)


def _ret_bwd(proj, ret, states, dmixed, dproj):
    rows = CBK * CH
    col0 = 3 * AH * AHD

    def body(c_ref, q_ref, k_ref, v_ref, g_ref, ret_ref, st_ref, dm_ref, dproj_in, dproj_hbm, g_acc, gs,
             sq, sk, sv, sg, sems):
        del dproj_in
        h, n = pl.program_id(0), pl.program_id(1)
        step = h * RSTEPS + n

        def out_copies(t):
            hh, nn = t // RSTEPS, t % RSTEPS
            r0 = pl.multiple_of((RSTEPS - 1 - nn) * rows, rows)
            return [pltpu.make_async_copy(
                st, dproj_hbm.at[pl.ds(r0, rows), pl.ds(pl.multiple_of(col0 + (k * RH + hh) * RHD, RHD), RHD)],
                sems.at[k]) for k, st in enumerate((sq, sk, sv, sg))]

        @pl.when(n == 0)
        def _():
            g_acc[...] = jnp.zeros_like(g_acc)

        decay, zeta, xi, gch = _ret_factors(c_ref[0:1, :])
        ret_v = ret_ref[...]
        rr = lax.rsqrt(jnp.mean(ret_v * ret_v, axis=-1, keepdims=True) + EPS)
        gv = g_ref[...]
        sgm = _sigmoid(gv)
        dmix = dm_ref[...]
        dgate = ((dmix * (ret_v * rr)) * (sgm * (1.0 + gv * (1.0 - sgm)))).astype(BF16)
        dretn = dmix * (gv * sgm)
        dret = _chunks(rr * dretn - ret_v * ((rr * rr * rr) * jnp.mean(dretn * ret_v, axis=-1, keepdims=True)))

        q3 = _chunks(q_ref[...].astype(BF16))
        kc = _chunks(k_ref[...] * (1.0 / math.sqrt(RHD)))
        k3 = kc.astype(BF16)
        v3 = _chunks(v_ref[...].astype(BF16))
        d3 = dret.astype(BF16)
        dxi = (dret * xi[None]).astype(BF16)
        kz = (kc * zeta[None]).astype(BF16)
        dr3 = _bdot_tn(q3, dxi)
        acc = g_acc[...]
        for i in reversed(range(CBK)):
            gs[i] = acc.astype(BF16)
            acc = dr3[i] + gch * acc
        g_acc[...] = acc
        g3 = gs[...]
        sc = (_bdot_nt(q3, k3) * decay[None]).astype(BF16)
        da = (_bdot_nt(d3, v3) * decay[None]).astype(BF16)
        dq = _bdot(da, k3) + _bdot_nt(dxi, st_ref[...])
        dkc = _bdot_tn(da, q3) + _bdot_nt(v3, g3) * zeta[None]
        dv = _bdot_tn(sc, d3) + _bdot(kz, g3)

        @pl.when(step > 0)
        def _():
            for cp in out_copies(step - 1):
                cp.wait()

        sq[...] = dq.reshape(rows, RHD).astype(BF16)
        sk[...] = (dkc * (1.0 / math.sqrt(RHD))).reshape(rows, RHD).astype(BF16)
        sv[...] = dv.reshape(rows, RHD).astype(BF16)
        sg[...] = dgate
        for cp in out_copies(step):
            cp.start()

        @pl.when(step == RH * RSTEPS - 1)
        def _():
            for cp in out_copies(step):
                cp.wait()

    col, own, state, const, dm = _ret_specs(True)
    hbm = pl.BlockSpec(memory_space=pl.ANY)
    return pl.pallas_call(
        body, name="ret_bwd", grid=(RH, RSTEPS),
        in_specs=[const, col(0), col(1), col(2), col(3), own, state, dm, hbm],
        out_specs=hbm,
        out_shape=jax.ShapeDtypeStruct(dproj.shape, dproj.dtype),
        input_output_aliases={8: 0},
        scratch_shapes=[pltpu.VMEM((RHD, RHD), F32), pltpu.VMEM((CBK, RHD, RHD), BF16)]
        + [pltpu.VMEM((rows, RHD), BF16) for _ in range(4)] + [pltpu.SemaphoreType.DMA((4,))],
        compiler_params=_cp(("arbitrary", "arbitrary")),
    )(_ret_consts(), proj, proj, proj, proj, ret, states, dmixed, dproj)


class _NoReduction:
    def start(self, group, grads):
        pass

    def local(self, name, first=()):
        return []

    def landed(self, name):
        return []

    def update(self, name):
        return []

    place = None

    def rider(self, name):
        return None

    def set_update(self, name, outs):
        pass


def _local_step(x, tgt, nw1, nw2, nw3, win, wout, wgu_a, wgu_b, wd_a, wd_b, red=None):
    red = red or _NoReduction()

    def after(values, first):
        return lax.optimization_barrier((tuple(values), tuple(first)))[0]

    h1, r1 = _rms_fwd(x, nw1)
    proj = _proj(h1, win)
    o, ma, lse, qkvp, lsep = _attn_fwd(proj)
    ret, mr, states = _ret_fwd(proj)
    x2, h2, r2 = _out_proj_rms(x, ma, mr, wout, nw2)
    a, dadg, dadu = _ffn_up(h2, wgu_b, 1, _ffn_up(h2, wgu_a, 0))
    dx3, dx3b, st3 = _ffn_down_loss(_ffn_down_first(x2, a, wd_a), a, wd_b, nw3, tgt)

    dwd = _wgrad_rows(a, dx3b, "wgrad_down")
    red.start(["w_down"], [dwd])
    (dx3b,) = after([dx3b], [dwd])
    part = _ffn_down_bwd(dx3b, wd_a, dadg, dadu, 0)
    (dx3b,) = after([dx3b], red.local("w_down", first=[part]))
    dgu = _ffn_down_bwd(dx3b, wd_b, dadg, dadu, 1, [part])
    dwg = _wgrad_rows(dgu, h2, "wgrad_gate", 0)
    red.start(["w_gate"], [dwg])
    (dgu,) = after([dgu], [dwg])
    dwu = _wgrad_rows(dgu, h2, "wgrad_up", 1)
    red.start(["w_up"], [dwu])
    (dgu,) = after([dgu], red.local("w_gate", first=[dwu] + red.landed("w_down")))
    dx2, dx2b, st2 = _ffn_up_bwd(dgu, wgu_a, wgu_b, dx3, x2, r2, nw2)
    (dx2b,) = after([dx2b], red.local("w_up", first=[dx2b]))
    dwo = _wgrad_out(ma, mr, dx2b)
    red.start(["w_out"], [dwo])
    (dx2b,) = after([dx2b], [dwo])
    dmixed, done = _out_proj_bwd(dx2b, wout, red.place, red.rider("w_down"))
    red.set_update("w_down", done)
    dproj = _attn_bwd(proj, dmixed, o, lse, qkvp, lsep)
    (dmixed,) = after([dmixed], red.local("w_out", first=[dproj] + red.landed("w_gate")))
    dproj = _ret_bwd(proj, ret, states, dmixed, dproj)
    (dwi0,) = after([_wgrad_in(h1, dproj, 0)], red.landed("w_up"))
    red.start(["w_in_0"], [dwi0])
    (dproj,) = after([dproj], [dwi0])
    dwi1 = _wgrad_in(h1, dproj, 1)
    red.start(["w_in_1"], [dwi1])
    sums = red.local("w_in_0", first=[dwi1] + red.landed("w_out"))
    sums = red.local("w_in_1", first=sums + red.update("w_gate"))
    (dproj,) = after([dproj], sums)
    gx, st1 = _in_proj_bwd(dproj, win, dx2, x, r1, nw1)
    dwi = jnp.concatenate([dwi0, dwi1], axis=1)
    stats = jnp.concatenate([st1[0:1], st2[0:1], st3[0:2], jnp.zeros((4, D), F32)], axis=0)
    return stats, gx, dwi, dwo, dwg, dwu, dwd


def _place():
    x, y, c = lax.axis_index("x"), lax.axis_index("y"), lax.axis_index("c")
    return x, y, c, [(1 - x, y), (x, 1 - y), (1 - x, 1 - y)]


def _handshake(peers):
    barrier = pltpu.get_barrier_semaphore()
    for peer in peers:
        pl.semaphore_signal(barrier, inc=1, device_id=peer, device_id_type=MESH)
    pl.semaphore_wait(barrier, len(peers))


def _all_gather(shards, name, collective_id, per=0, rows=None):
    na = len(shards)
    nout = 1 if per else na
    lo, r = rows or (0, shards[0].shape[0])
    ngroups = NDEV // per if per else 0
    SIB, XN0, XN1, YN1, YN0, VIA_X, VIA_Y = 0, 1, 2, 3, 4, 5, 6
    D2D = {XN0: 7, XN1: 8, YN1: 9, YN0: 10, VIA_X: 11, VIA_Y: 12}

    def body(*refs):
        ins, outs = [ref.at[pl.ds(lo, r)] for ref in refs[:na]], refs[na:na + nout]
        send_sems, recv_sems, local_sems = refs[na + nout:]
        x, y, c, _ = _place()
        me, sib = (x, y, c), (x, y, 1 - c)
        xn, yn, dg = (1 - x, y, c), (x, 1 - y, c), (1 - x, 1 - y, c)
        _handshake([sib, xn, yn])

        def part(ref, h):
            rows = ref.shape[0] // 2
            return ref if h is None else ref.at[pl.ds(h * rows, rows)]

        def block(a, owner, h):
            idx = 4 * owner[0] + 2 * owner[1] + owner[2]
            if not per:
                return part(outs[a].at[idx], h)
            return part(outs[0].at[idx // per, a, pl.ds(pl.multiple_of((idx % per) * r, r), r)], h)

        def copy(a, k, owner, h, to, own_src=False):
            return pltpu.make_async_remote_copy(
                src_ref=part(ins[a], h) if own_src else block(a, owner, h), dst_ref=block(a, owner, h),
                send_sem=send_sems.at[a, k], recv_sem=recv_sems.at[a, k], device_id=to, device_id_type=MESH)

        def other(p):
            return (p[0], p[1], 1 - c)

        mine = [pltpu.make_async_copy(ins[a], block(a, me, None), local_sems.at[a]) for a in range(na)]
        for cp in mine:
            cp.start()
        sent = []
        for a in range(na):
            sent += [copy(a, XN0, me, 0, xn, True), copy(a, YN1, me, 1, yn, True),
                     copy(a, XN1, me, 1, xn, True), copy(a, YN0, me, 0, yn, True)]
        sent += [copy(a, SIB, me, None, sib, True) for a in range(na)]
        for cp in sent:
            cp.start()

        def landed(a, k, owner, h, then):
            copy(a, k, owner, h, me).wait_recv()
            for k2, to in then + [(D2D[k], sib)]:
                cp = copy(a, k2, owner, h, to)
                cp.start()
                sent.append(cp)

        for a in range(na):
            landed(a, XN0, xn, 0, [(VIA_Y, yn)])
            landed(a, YN1, yn, 1, [(VIA_X, xn)])
            landed(a, XN1, xn, 1, [])
            landed(a, YN0, yn, 0, [])
        for a in range(na):
            landed(a, VIA_Y, dg, 0, [])
            landed(a, VIA_X, dg, 1, [])
        for a in range(na):
            copy(a, SIB, sib, None, me).wait_recv()
            for k, owner, h in ((XN0, xn, 0), (XN1, xn, 1), (YN1, yn, 1), (YN0, yn, 0), (VIA_Y, dg, 0), (VIA_X, dg, 1)):
                copy(a, D2D[k], other(owner), h, me).wait_recv()
        for cp in sent:
            cp.wait_send()
        for cp in mine:
            cp.wait()

    if per:
        out_type = [jax.ShapeDtypeStruct((ngroups, na, per * r, shards[0].shape[1]), shards[0].dtype)]
    else:
        out_type = [jax.ShapeDtypeStruct((NDEV,) + s.shape, s.dtype) for s in shards]
    return _sequencer_call(
        body, name, collective_id, out_type,
        [pltpu.SemaphoreType.DMA((na, 13)), pltpu.SemaphoreType.DMA((na, 13)), pltpu.SemaphoreType.DMA((na,))])(*shards)


def _sequencer_call(body, name, collective_id, out_type, scratch_types):
    return pl.kernel(
        body, name=name, out_type=out_type,
        mesh=plsc.ScalarSubcoreMesh(axis_name="sequencer", num_cores=1),
        scratch_types=scratch_types,
        compiler_params=pltpu.CompilerParams(collective_id=collective_id))


def _exchange_sibling(grads, name, collective_id):
    na = len(grads)

    def body(*refs):
        ins, outs = refs[:na], refs[na:2 * na]
        send_sems, recv_sems = refs[2 * na:]
        x, y, c, _ = _place()
        _handshake([(x, y, 1 - c)])
        cps = []
        for a in range(na):
            for k in range(4):
                cps.append(pltpu.make_async_remote_copy(
                    src_ref=ins[a].at[2 * k + (1 - c)], dst_ref=outs[a].at[k],
                    send_sem=send_sems.at[a, k], recv_sem=recv_sems.at[a, k],
                    device_id=(x, y, 1 - c), device_id_type=MESH))
        for cp in cps:
            cp.start()
        for cp in cps:
            cp.wait()

    return _sequencer_call(
        body, name, collective_id,
        [jax.ShapeDtypeStruct((4,) + g.shape[1:], g.dtype) for g in grads],
        [pltpu.SemaphoreType.DMA((na, 4)), pltpu.SemaphoreType.DMA((na, 4))])(*grads)


def _row_tile(rows, cols):
    for t in (512, 256, 176, 128, 64, 32, 16):
        if rows % t == 0 and t * cols * 4 <= (2 << 20):
            return t
    raise ValueError((rows, cols))


STREAM_BUFS = 3


def _stream_tile(rows, steps):
    for t in (512, 256, 176, 128, 64, 32, 16):
        if rows % t == 0 and rows // t >= steps:
            return t
    raise ValueError((rows, steps))


def _stream(n, loads, stores, compute):
    for k in range(min(STREAM_BUFS, n)):
        for cp in loads(k):
            cp.start()
    for k in range(n):
        for cp in loads(k):
            cp.wait()
        if k >= 2:
            for cp in stores(k - 2):
                cp.wait()
        compute(k)
        for cp in stores(k):
            cp.start()
        if k + STREAM_BUFS < n:
            for cp in loads(k + STREAM_BUFS):
                cp.start()
    for k in range(max(n - 2, 0), n):
        for cp in stores(k):
            cp.wait()


def _chip_sum(place, g, got, name):
    _, r, c = g.shape
    tm = _stream_tile(r, 4)
    nt = r // tm

    def body(pos_ref, g_hbm, got_hbm, o_hbm, g_buf, s_buf, o_buf, sem_in, sem_out):
        def chip(j):
            return 2 * (pos_ref[0] ^ (0 if j == 1 else 1)) + (pos_ref[1] ^ (0 if j == 0 else 1))

        def loads(k):
            j, rows, slot = k // nt, pl.ds((k % nt) * tm, tm), k % STREAM_BUFS
            return [pltpu.make_async_copy(g_hbm.at[2 * chip(j) + pos_ref[2], rows], g_buf.at[slot], sem_in.at[slot, 0]),
                    pltpu.make_async_copy(got_hbm.at[chip(j), rows], s_buf.at[slot], sem_in.at[slot, 1])]

        def stores(k):
            return [pltpu.make_async_copy(o_buf.at[k % 2], o_hbm.at[k // nt, pl.ds((k % nt) * tm, tm)],
                                          sem_out.at[k % 2])]

        def compute(k):
            slot = k % STREAM_BUFS
            o_buf[k % 2] = (g_buf[slot].astype(F32) + s_buf[slot].astype(F32)).astype(BF16)

        _stream(3 * nt, loads, stores, compute)

    hbm = pl.BlockSpec(memory_space=pl.ANY)
    return pl.pallas_call(
        body, name=name,
        grid_spec=pltpu.PrefetchScalarGridSpec(
            num_scalar_prefetch=1, grid=(1,), in_specs=[hbm, hbm], out_specs=hbm,
            scratch_shapes=[pltpu.VMEM((STREAM_BUFS, tm, c), BF16), pltpu.VMEM((STREAM_BUFS, tm, c), BF16),
                            pltpu.VMEM((2, tm, c), BF16),
                            pltpu.SemaphoreType.DMA((STREAM_BUFS, 2)), pltpu.SemaphoreType.DMA((2,))]),
        out_shape=jax.ShapeDtypeStruct((3, r, c), BF16),
        compiler_params=_cp(("arbitrary",)),
    )(place, g, got)


def _exchange_chips(sums, name, collective_id):
    na = len(sums)

    def body(*refs):
        ins, outs = refs[:na], refs[na:2 * na]
        send_sems, recv_sems = refs[2 * na:]
        x, y, c, chips = _place()
        _handshake([(*chip, c) for chip in chips])
        cps = []
        for a in range(na):
            for j, chip in enumerate(chips):
                cps.append(pltpu.make_async_remote_copy(
                    src_ref=ins[a].at[j], dst_ref=outs[a].at[j],
                    send_sem=send_sems.at[a, j], recv_sem=recv_sems.at[a, j],
                    device_id=(*chip, c), device_id_type=MESH))
        for cp in cps:
            cp.start()
        for cp in cps:
            cp.wait()

    return _sequencer_call(
        body, name, collective_id,
        [jax.ShapeDtypeStruct((3,) + s.shape[1:], s.dtype) for s in sums],
        [pltpu.SemaphoreType.DMA((na, 3)), pltpu.SemaphoreType.DMA((na, 3))])(*sums)


def _exchange_stats(stats, collective_id):
    def body(st_in, st_out, st_send, st_recv, local_sem):
        x, y, c, _ = _place()
        me_idx = 4 * x + 2 * y + c
        peers = [(x ^ ((k >> 2) & 1), y ^ ((k >> 1) & 1), c ^ (k & 1)) for k in range(1, 8)]
        _handshake(peers)
        mine = pltpu.make_async_copy(st_in, st_out.at[me_idx], local_sem)
        mine.start()
        cps = [pltpu.make_async_remote_copy(
            src_ref=st_in, dst_ref=st_out.at[me_idx], send_sem=st_send.at[k], recv_sem=st_recv.at[k],
            device_id=peer, device_id_type=MESH) for k, peer in enumerate(peers)]
        for cp in cps:
            cp.start()
        for cp in cps:
            cp.wait()
        mine.wait()

    return _sequencer_call(
        body, "exchange_stats", collective_id,
        jax.ShapeDtypeStruct((NDEV,) + stats.shape, stats.dtype),
        [pltpu.SemaphoreType.DMA((7,)), pltpu.SemaphoreType.DMA((7,)), pltpu.SemaphoreType.DMA])(stats)


class _Reduction:
    def __init__(self, place, first_collective_id, state):
        self.place = place
        self.ids = iter(range(first_collective_id, 32))
        self.state = state
        self.groups = {}
        self.updates = {}

    def next_id(self):
        return next(self.ids)

    def start(self, group, grads):
        got = _exchange_sibling(grads, "sibling_exchange_" + group[0], self.next_id())
        self.groups[group[0]] = dict(names=group, grads=grads, got=got)

    def local(self, name, first=()):
        grp = self.groups[name]
        grads = lax.optimization_barrier((tuple(grp["grads"]), tuple(first)))[0]
        grp["sums"] = [_chip_sum(self.place, g, s, "chip_sum_" + n)
                       for g, s, n in zip(grads, grp["got"], grp["names"])]
        grp["chips"] = _exchange_chips(grp["sums"], "chip_exchange_" + name, self.next_id())
        return grp["sums"]

    def landed(self, name):
        return list(self.groups[name]["chips"])

    def rider(self, name):
        grp = next(g for g in self.groups.values() if name in g["names"])
        k = grp["names"].index(name)
        return self.state[name][:3] + (grp["grads"][k], grp["got"][k], grp["chips"][k])

    def set_update(self, name, outs):
        self.updates[name] = list(outs)

    def update(self, name):
        if name not in self.updates:
            grp = next(g for g in self.groups.values() if name in g["names"])
            k = grp["names"].index(name)
            w, m, v, part, parts = self.state[name]
            before = self.update(f"{name[:-1]}{part - 1}") if part else None
            self.updates[name] = _shard_update(self.place, w, m, v, grp["grads"][k], grp["got"][k],
                                               grp["chips"][k], "update_" + name, part, parts, before)
        return list(self.updates[name])


def _adamw(w, g, m, v):
    m = ADAM_B1 * m + (1.0 - ADAM_B1) * g
    v = ADAM_B2 * v + (1.0 - ADAM_B2) * (g * g)
    m_hat = m / (1.0 - ADAM_B1 ** ADAM_STEP)
    v_hat = v / (1.0 - ADAM_B2 ** ADAM_STEP)
    delta = -ADAM_LR * (m_hat / (jnp.sqrt(v_hat) + ADAM_EPS) + ADAM_WD * w)
    return delta, m, v


def _update_tile(w_ref, m_ref, v_ref, g_ref, s_ref, c_ref, go_ref, d_ref, mo_ref, vo_ref):
    grad = g_ref[...].astype(F32) + s_ref[...].astype(F32)
    for j in range(3):
        grad = grad + c_ref[j].astype(F32)
    delta, mn, vn = _adamw(w_ref[...], grad, m_ref[...], v_ref[...])
    go_ref[...] = grad
    d_ref[...] = delta
    mo_ref[...] = mn
    vo_ref[...] = vn


def _shard_update(place, w, m, v, g, got_sib, got_chips, name, part=0, parts=1, before=None):
    r, c = w.shape
    rp = r // parts
    tm = _stream_tile(rp, 8)
    nt = rp // tm
    before = list(before or [])

    def body(pos_ref, w_hbm, m_hbm, v_hbm, g_hbm, s_hbm, c_hbm, *rest):
        outs = rest[len(before):len(before) + 4]
        w_buf, m_buf, v_buf, g_buf, s_buf, c_buf, o_buf, sem_in, sem_out = rest[len(before) + 4:]
        own = 4 * pos_ref[0] + 2 * pos_ref[1] + pos_ref[2]
        chip = 2 * pos_ref[0] + pos_ref[1]

        def loads(k):
            slot, rows, mine = k % STREAM_BUFS, pl.ds(k * tm, tm), pl.ds(part * rp + k * tm, tm)
            pairs = [(w_hbm.at[mine], w_buf), (m_hbm.at[mine], m_buf), (v_hbm.at[mine], v_buf),
                     (g_hbm.at[own, rows], g_buf), (s_hbm.at[chip, rows], s_buf), (c_hbm.at[:, rows], c_buf)]
            return [pltpu.make_async_copy(src, buf.at[slot], sem_in.at[slot, n]) for n, (src, buf) in enumerate(pairs)]

        def stores(k):
            mine = pl.ds(part * rp + k * tm, tm)
            return [pltpu.make_async_copy(o_buf.at[k % 2, n], out.at[mine], sem_out.at[k % 2, n])
                    for n, out in enumerate(outs)]

        def compute(k):
            slot = k % STREAM_BUFS
            _update_tile(w_buf.at[slot], m_buf.at[slot], v_buf.at[slot], g_buf.at[slot], s_buf.at[slot],
                         c_buf.at[slot], *[o_buf.at[k % 2, n] for n in range(4)])

        _stream(nt, loads, stores, compute)

    hbm = pl.BlockSpec(memory_space=pl.ANY)
    return pl.pallas_call(
        body, name=name,
        grid_spec=pltpu.PrefetchScalarGridSpec(
            num_scalar_prefetch=1, grid=(1,), in_specs=[hbm] * (6 + len(before)), out_specs=[hbm] * 4,
            scratch_shapes=[pltpu.VMEM((STREAM_BUFS, tm, c), F32)] * 3 + [pltpu.VMEM((STREAM_BUFS, tm, c), BF16)] * 2
            + [pltpu.VMEM((STREAM_BUFS, 3, tm, c), BF16), pltpu.VMEM((2, 4, tm, c), F32),
               pltpu.SemaphoreType.DMA((STREAM_BUFS, 6)), pltpu.SemaphoreType.DMA((2, 4))]),
        out_shape=[jax.ShapeDtypeStruct((r, c), F32)] * 4,
        input_output_aliases={7 + k: k for k in range(len(before))},
        compiler_params=_cp(("arbitrary",)),
    )(place, w, m, v, g, got_sib, got_chips, *before)


def _small_update(stats_all, ws, ms, vs):
    def body(st_ref, w_ref, m_ref, v_ref, go_ref, d_ref, mo_ref, vo_ref):
        grad = st_ref[0]
        for k in range(1, NDEV):
            grad = grad + st_ref[k]
        delta, mn, vn = _adamw(w_ref[...], grad, m_ref[...], v_ref[...])
        go_ref[...] = grad
        d_ref[...] = delta
        mo_ref[...] = mn
        vo_ref[...] = vn

    return pl.pallas_call(
        body, name="small_update",
        out_shape=[jax.ShapeDtypeStruct((8, D), F32)] * 4,
        compiler_params=_cp(),
    )(stats_all, ws, ms, vs)


def kernel(x, norm_mix_w, w_in, w_out, norm_ffn_w, w_gate, w_up, w_down, norm_final_w, loss_target, m_norm_mix_w, m_w_in, m_w_out, m_norm_ffn_w, m_w_gate, m_w_up, m_w_down, m_norm_final_w, v_norm_mix_w, v_w_in, v_w_out, v_norm_ffn_w, v_w_gate, v_w_up, v_w_down, v_norm_final_w):
    tr = {"w_gate", "w_up"}
    names = ["w_in", "w_out", "w_gate", "w_up", "w_down"]

    def view(a, n):
        return a[0].T if n in tr else a[0]

    big_w = [view(a, n) for a, n in zip([w_in, w_out, w_gate, w_up, w_down], names)]
    big_m = [view(a, n) for a, n in zip([m_w_in, m_w_out, m_w_gate, m_w_up, m_w_down], names)]
    big_v = [view(a, n) for a, n in zip([v_w_in, v_w_out, v_w_gate, v_w_up, v_w_down], names)]

    shards = [None] + [_cast_bf16(w, "cast_" + n) for w, n in zip(big_w[1:], names[1:])]
    win = [_all_gather([cols], f"all_gather_w_in_{k}", 1 + k)[0]
           for k, cols in enumerate(_cast_cols(big_w[0], "cast_w_in"))]
    (wout,) = _all_gather(shards[1:2], "all_gather_w_out", 3)
    (wgu_a,) = _all_gather(shards[2:4], "all_gather_gate_up_0", 4, per=FF_PER, rows=(0, FF_ROWS))
    (wgu_b,) = _all_gather(shards[2:4], "all_gather_gate_up_1", 5, per=FF_PER, rows=(FF_ROWS, FF_ROWS))
    (wd_a,) = _all_gather(shards[4:5], "all_gather_w_down_0", 6, per=FF_PER, rows=(0, FF_ROWS))
    (wd_b,) = _all_gather(shards[4:5], "all_gather_w_down_1", 7, per=FF_PER, rows=(FF_ROWS, FF_ROWS))
    nw3 = norm_final_w.reshape(1, D)
    place = jnp.stack([lax.axis_index("x"), lax.axis_index("y"), lax.axis_index("c")]).astype(jnp.int32)
    state = {n: (w, m, v, 0, 1) for n, w, m, v in zip(names, big_w, big_m, big_v)}
    for part in range(W_IN_PARTS):
        state[f"w_in_{part}"] = state["w_in"][:3] + (part, W_IN_PARTS)
    red = _Reduction(place, 8, state)
    stats, gx, *_ = _local_step(
        x[0], loss_target[0], norm_mix_w, norm_ffn_w, nw3, win, wout.reshape(D, D),
        wgu_a.reshape(NFG // 2, 2 * N_FG, D), wgu_b.reshape(NFG // 2, 2 * N_FG, D),
        wd_a.reshape(NFG // 2, N_FG, D), wd_b.reshape(NFG // 2, N_FG, D), red)
    stats_all = _exchange_stats(stats, red.next_id())
    upd = [red.update(f"w_in_{W_IN_PARTS - 1}" if n == "w_in" else n) for n in names]
    stats_all = lax.optimization_barrier((stats_all, tuple(upd[0])))[0]

    def rows(a, b, c):
        return jnp.concatenate([a.reshape(1, D), b.reshape(1, D), c.reshape(1, D), jnp.zeros((5, D), F32)], axis=0)

    sg, sd, sm, sv = _small_update(stats_all, rows(norm_mix_w, norm_ffn_w, norm_final_w),
                                   rows(m_norm_mix_w, m_norm_ffn_w, m_norm_final_w),
                                   rows(v_norm_mix_w, v_norm_ffn_w, v_norm_final_w))
    loss = sg[3, 0]

    def outs(k, small):
        big = [(u[k].T if n in tr else u[k])[None] for u, n in zip(upd, names)]
        return [small[0:1], big[0], big[1], small[1:2], big[2], big[3], big[4], small[2]]

    return (loss, gx[None], *outs(0, sg), *outs(1, sd), *outs(2, sm), *outs(3, sv))
```

```python
import math

import numpy as np
import jax
import jax.numpy as jnp
from jax import lax
from jax.experimental import pallas as pl
from jax.experimental.pallas import tpu as pltpu
from jax.experimental.pallas import tpu_sc as plsc

F32 = jnp.float32
BF16 = jnp.bfloat16

S = 2048
D = 2048
NDEV = 8
N_IN = 7168 // NDEV
N_FF = 5632 // NDEV
NFG, N_FG = NDEV // 2, 2 * N_FF
FF_PER, FF_ROWS = 4, N_FF // 2
IN_ROUNDS = ((0, 512), (512, N_IN - 512))
N_OUT = 2048 // NDEV
AH, AHD = 8, 128
RH, RHD = 4, 256
CH = 128
NB = S // CH
EPS = 1e-6
PATTERNS = ((1, 16), (4, 4), (16, 1))
NEG = -1e30
VMEM_LIMIT = 56 * 1024 * 1024

ADAM_LR, ADAM_B1, ADAM_B2, ADAM_EPS, ADAM_WD, ADAM_STEP = 0.001, 0.9, 0.999, 1e-08, 0.01, 10
MESH = pl.DeviceIdType.MESH


def _cp(sem=None):
    return pltpu.CompilerParams(dimension_semantics=sem, vmem_limit_bytes=VMEM_LIMIT)


def _dot(a, b):
    return jnp.dot(a, b, preferred_element_type=F32)


def _dot_nt(a, b):
    return lax.dot_general(a, b, (((1,), (1,)), ((), ())), preferred_element_type=F32)


def _dot_tn(a, b):
    return lax.dot_general(a, b, (((0,), (0,)), ((), ())), preferred_element_type=F32)


def _sigmoid(x):
    return 0.5 * jnp.tanh(0.5 * x) + 0.5


def _cast_bf16(w, name):
    r, c = w.shape
    tm = r if r <= 1024 else 512

    def body(w_ref, o_ref):
        o_ref[...] = w_ref[...].astype(BF16)

    return pl.pallas_call(
        body, name=name, grid=(r // tm,),
        in_specs=[pl.BlockSpec((tm, c), lambda i: (i, 0))],
        out_specs=pl.BlockSpec((tm, c), lambda i: (i, 0)),
        out_shape=jax.ShapeDtypeStruct((r, c), BF16),
        compiler_params=_cp(("parallel",)),
    )(w)


def _rms_fwd(x, nw):
    tm = 256

    def body(x_ref, w_ref, h_ref, r_ref):
        xs = x_ref[...]
        r = lax.rsqrt(jnp.mean(xs * xs, axis=-1, keepdims=True) + EPS)
        h_ref[...] = ((xs * r) * w_ref[...]).astype(BF16)
        r_ref[...] = r

    return pl.pallas_call(
        body, name="rms_fwd", grid=(S // tm,),
        in_specs=[pl.BlockSpec((tm, D), lambda i: (i, 0)), pl.BlockSpec((1, D), lambda i: (0, 0))],
        out_specs=[pl.BlockSpec((tm, D), lambda i: (i, 0)), pl.BlockSpec((tm, 1), lambda i: (i, 0))],
        out_shape=[jax.ShapeDtypeStruct((S, D), BF16), jax.ShapeDtypeStruct((S, 1), F32)],
        compiler_params=_cp(("parallel",)),
    )(x, nw)


def _row_copies(hbm_refs, bufs, sems, m, tm):
    rows = pl.ds(pl.multiple_of(m * tm, tm), tm)
    return [pltpu.make_async_copy(h.at[rows], b, sems.at[i]) for i, (h, b) in enumerate(zip(hbm_refs, bufs))]


def _rms_bwd_tile(dh, xs, r, nw):
    dnw = jnp.sum(dh * (xs * r), axis=0, keepdims=True)
    gy = dh * nw
    dx = r * gy - xs * ((r * r * r) * jnp.mean(gy * xs, axis=-1, keepdims=True))
    return dx, dnw


def _cast_cols(w, name):
    r, c = w.shape
    tm = 512

    def window(k):
        off, width = IN_ROUNDS[k]
        whole = off % width != 0

        def body(w_ref, o_ref):
            o_ref[...] = (w_ref[:, off:off + width] if whole else w_ref[...]).astype(BF16)

        return pl.pallas_call(
            body, name=f"{name}_{k}", grid=(r // tm,),
            in_specs=[pl.BlockSpec((tm, c), lambda i: (i, 0)) if whole
                      else pl.BlockSpec((tm, width), lambda i: (i, off // width))],
            out_specs=pl.BlockSpec((tm, width), lambda i: (i, 0)),
            out_shape=jax.ShapeDtypeStruct((r, width), BF16),
            compiler_params=_cp(("parallel",)),
        )(w)

    return [window(k) for k in range(len(IN_ROUNDS))]


def _proj_round(h1, win, k, before):
    tm = 1024
    nm = S // tm
    off, width = IN_ROUNDS[k]
    before = [] if before is None else [before]

    def body(a_ref, w_ref, *rest):
        o_hbm, o_buf, sems = rest[-3:]
        p, m = pl.program_id(0), pl.program_id(1)
        t = p * nm + m

        def out_copy(pp, mm, slot):
            cols = pl.ds(pl.multiple_of(pp * N_IN + off, 128), width)
            return pltpu.make_async_copy(o_buf.at[slot], o_hbm.at[pl.ds(pl.multiple_of(mm * tm, tm), tm), cols],
                                         sems.at[slot])

        @pl.when(t >= 2)
        def _():
            out_copy(p, m, t % 2).wait()

        o_buf[t % 2] = _dot(a_ref[...], w_ref[...])
        out_copy(p, m, t % 2).start()

        @pl.when(t == NDEV * nm - 1)
        def _():
            out_copy(p, m, (t + 1) % 2).wait()
            out_copy(p, m, t % 2).wait()

    return pl.pallas_call(
        body, name=f"proj_{k}", grid=(NDEV, nm),
        in_specs=[pl.BlockSpec((tm, D), lambda p, m: (m, 0)),
                  pl.BlockSpec((None, D, width), lambda p, m: (p, 0, 0))]
        + [pl.BlockSpec(memory_space=pl.ANY)] * len(before),
        out_specs=pl.BlockSpec(memory_space=pl.ANY),
        out_shape=jax.ShapeDtypeStruct((S, NDEV * N_IN), F32),
        scratch_shapes=[pltpu.VMEM((2, tm, width), F32), pltpu.SemaphoreType.DMA((2,))],
        input_output_aliases={2: 0} if before else {},
        compiler_params=_cp(("arbitrary", "arbitrary")),
    )(h1, win, *before)


def _proj(h1, wins):
    out = None
    for k, win in enumerate(wins):
        out = _proj_round(h1, win, k, out)
    return out


def _out_proj_rms(x, ma, mr, wout, nw):
    tm = 256
    half = D // 2

    def body(x_ref, ma_ref, mr_ref, w_ref, nw_ref, x2_ref, h_ref, r_ref):
        acc = _dot(ma_ref[...], w_ref[0:half, :]) + _dot(mr_ref[...], w_ref[half:D, :])
        x2 = x_ref[...] + acc
        r = lax.rsqrt(jnp.mean(x2 * x2, axis=-1, keepdims=True) + EPS)
        x2_ref[...] = x2
        h_ref[...] = ((x2 * r) * nw_ref[...]).astype(BF16)
        r_ref[...] = r

    return pl.pallas_call(
        body, name="out_proj_rms", grid=(S // tm,),
        in_specs=[pl.BlockSpec((tm, D), lambda i: (i, 0)),
                  pl.BlockSpec((tm, half), lambda i: (i, 0)),
                  pl.BlockSpec((tm, half), lambda i: (i, 0)),
                  pl.BlockSpec((D, D), lambda i: (0, 0)),
                  pl.BlockSpec((1, D), lambda i: (0, 0))],
        out_specs=[pl.BlockSpec((tm, D), lambda i: (i, 0)), pl.BlockSpec((tm, D), lambda i: (i, 0)),
                   pl.BlockSpec((tm, 1), lambda i: (i, 0))],
        out_shape=[jax.ShapeDtypeStruct((S, D), F32), jax.ShapeDtypeStruct((S, D), BF16),
                   jax.ShapeDtypeStruct((S, 1), F32)],
        compiler_params=_cp(("parallel",)),
    )(x, ma, mr, wout, nw)


def _ffn_up(h2, wgu, part, before=None):
    tm = 512

    def body(h_ref, w_ref, *rest):
        a_ref, dadg_ref, dadu_ref = rest[-3:]
        gu = _dot_nt(h_ref[...], w_ref[...])
        g, u = gu[:, 0:N_FG], gu[:, N_FG:2 * N_FG]
        sg = _sigmoid(g)
        silu = g * sg
        a_ref[...] = (silu * u).astype(BF16)
        dadg_ref[...] = (u * (sg * (1.0 + g * (1.0 - sg)))).astype(BF16)
        dadu_ref[...] = silu.astype(BF16)

    half = NFG // 2
    first = part * half
    before = list(before or [])
    blk = pl.BlockSpec((None, tm, N_FG), lambda p, m: (p + first, m, 0))
    return pl.pallas_call(
        body, name=f"ffn_up_{part}", grid=(half, S // tm),
        in_specs=[pl.BlockSpec((tm, D), lambda p, m: (m, 0)),
                  pl.BlockSpec((None, 2 * N_FG, D), lambda p, m: (p, 0, 0))]
        + [pl.BlockSpec(memory_space=pl.ANY)] * len(before),
        out_specs=[blk, blk, blk],
        out_shape=[jax.ShapeDtypeStruct((NFG, S, N_FG), BF16)] * 3,
        input_output_aliases={2 + k: k for k in range(len(before))},
        compiler_params=_cp(("parallel", "parallel")),
    )(h2, wgu, *before)


def _ffn_down_first(x2, a, wd):
    tm = 512
    n = wd.shape[0]

    def body(x_ref, a_ref, w_ref, o_ref):
        p = pl.program_id(1)

        @pl.when(p == 0)
        def _():
            o_ref[...] = x_ref[...] + _dot(a_ref[...], w_ref[0])

        @pl.when(p > 0)
        def _():
            o_ref[...] += _dot(a_ref[...], w_ref[p])

    return pl.pallas_call(
        body, name="ffn_down_first", grid=(S // tm, n),
        in_specs=[pl.BlockSpec((tm, D), lambda m, p: (m, 0)),
                  pl.BlockSpec((None, tm, N_FG), lambda m, p: (p, m, 0)),
                  pl.BlockSpec((n, N_FG, D), lambda m, p: (0, 0, 0))],
        out_specs=pl.BlockSpec((tm, D), lambda m, p: (m, 0)),
        out_shape=jax.ShapeDtypeStruct((S, D), F32),
        compiler_params=_cp(("parallel", "arbitrary")),
    )(x2, a, wd)


def _ffn_down_loss(x2, a, wd, nw, tgt):
    tm = 512
    first = NFG - wd.shape[0]

    def body(x2_hbm, a_ref, w_ref, nw_ref, t_hbm, dx_ref, dxb_ref, st_ref, acc_ref, x2_buf, t_buf, sems):
        m, p = pl.program_id(0), pl.program_id(1)
        tail_in = _row_copies((x2_hbm, t_hbm), (x2_buf, t_buf), sems, m, tm)

        @pl.when(p == 0)
        def _():
            acc_ref[...] = jnp.zeros_like(acc_ref)
            for cp in tail_in:
                cp.start()

        @pl.when((p == 0) & (m == 0))
        def _():
            st_ref[...] = jnp.zeros_like(st_ref)

        acc_ref[...] += _dot(a_ref[...], w_ref[p])

        @pl.when(p == NFG - first - 1)
        def _():
            for cp in tail_in:
                cp.wait()
            x3 = x2_buf[...] + acc_ref[...]
            nwv = nw_ref[...]
            r = lax.rsqrt(jnp.mean(x3 * x3, axis=-1, keepdims=True) + EPS)
            y = (x3 * r) * nwv
            err = y - t_buf[...]
            loss = 0.5 * jnp.sum(jnp.mean(err * err, axis=-1, keepdims=True), axis=0, keepdims=True)
            dy = err * (1.0 / D)
            dx, dnw = _rms_bwd_tile(dy, x3, r, nwv)
            dx_ref[...] = dx
            dxb_ref[...] = dx.astype(BF16)
            st_ref[0:1, :] += dnw
            st_ref[1:2, :] += jnp.broadcast_to(loss, (1, D))

    return pl.pallas_call(
        body, name="ffn_down_loss", grid=(S // tm, NFG - first),
        in_specs=[pl.BlockSpec(memory_space=pl.ANY),
                  pl.BlockSpec((None, tm, N_FG), lambda m, p: (p + first, m, 0)),
                  pl.BlockSpec((NFG - first, N_FG, D), lambda m, p: (0, 0, 0)),
                  pl.BlockSpec((1, D), lambda m, p: (0, 0)),
                  pl.BlockSpec(memory_space=pl.ANY)],
        out_specs=[pl.BlockSpec((tm, D), lambda m, p: (m, 0)), pl.BlockSpec((tm, D), lambda m, p: (m, 0)),
                   pl.BlockSpec((8, D), lambda m, p: (0, 0))],
        out_shape=[jax.ShapeDtypeStruct((S, D), F32), jax.ShapeDtypeStruct((S, D), BF16),
                   jax.ShapeDtypeStruct((8, D), F32)],
        scratch_shapes=[pltpu.VMEM((tm, D), F32), pltpu.VMEM((tm, D), F32), pltpu.VMEM((tm, D), F32),
                        pltpu.SemaphoreType.DMA((2,))],
        compiler_params=_cp(("arbitrary", "arbitrary")),
    )(x2, a, wd, nw, tgt)


def _ffn_down_bwd(dx3b, wd, dadg, dadu, part, before=None):
    tm = 1024
    half = NFG // 2

    def body(dx_ref, w_ref, dadg_ref, dadu_ref, *rest):
        dgu_ref = rest[-1]
        rows = pl.ds(pl.multiple_of(pl.program_id(1) * tm, tm), tm)
        da = _dot_nt(dx_ref[rows, :], w_ref[...])
        dgu_ref[:, 0:N_FG] = (da * dadg_ref[...].astype(F32)).astype(BF16)
        dgu_ref[:, N_FG:2 * N_FG] = (da * dadu_ref[...].astype(F32)).astype(BF16)

    blk = pl.BlockSpec((None, tm, N_FG), lambda p, m: (p + part * half, m, 0))
    before = list(before or [])
    return pl.pallas_call(
        body, name=f"ffn_down_bwd_{part}", grid=(half, S // tm),
        in_specs=[pl.BlockSpec((S, D), lambda p, m: (0, 0)),
                  pl.BlockSpec((None, N_FG, D), lambda p, m: (p, 0, 0)), blk, blk]
        + [pl.BlockSpec(memory_space=pl.ANY)] * len(before),
        out_specs=pl.BlockSpec((None, tm, 2 * N_FG), lambda p, m: (p + part * half, m, 0)),
        out_shape=jax.ShapeDtypeStruct((NFG, S, 2 * N_FG), BF16),
        input_output_aliases={4 + k: k for k in range(len(before))},
        compiler_params=_cp(("parallel", "parallel")),
    )(dx3b, wd, dadg, dadu, *before)


def _ffn_up_bwd(dgu, wgu_a, wgu_b, dres, xs, r, nw):
    tm = 512
    nm = S // tm
    na = wgu_a.shape[0]

    def body(dgu_ref, wa_hbm, wb_hbm, dres_hbm, x_hbm, r_ref, nw_ref, dx_ref, dxb_ref, st_ref,
             w_buf, dres_buf, x_buf, sems, w_sems):
        m, p = pl.program_id(0), pl.program_id(1)
        tail_in = _row_copies((dres_hbm, x_hbm), (dres_buf, x_buf), sems, m, tm)

        def fetch(g, slot):
            for src, lo in ((wa_hbm, 0), (wb_hbm, na)):
                @pl.when((g >= lo) & (g < lo + na))
                def _():
                    pltpu.make_async_copy(src.at[g - lo], w_buf.at[slot], w_sems.at[slot]).start()

        @pl.when((p == 0) & (m == 0))
        def _():
            st_ref[...] = jnp.zeros_like(st_ref)
            fetch(p, 0)

        @pl.when((p < NFG - 1) | (m < nm - 1))
        def _():
            fetch((p + 1) % NFG, (p + 1) % 2)

        @pl.when(p == 0)
        def _():
            dx_ref[...] = jnp.zeros_like(dx_ref)
            for cp in tail_in:
                cp.start()

        slot = p % 2
        pltpu.make_async_copy(wa_hbm.at[0], w_buf.at[slot], w_sems.at[slot]).wait()
        dx_ref[...] += _dot(dgu_ref[...], w_buf[slot])

        @pl.when(p == NFG - 1)
        def _():
            for cp in tail_in:
                cp.wait()
            dx, dnw = _rms_bwd_tile(dx_ref[...], x_buf[...], r_ref[...], nw_ref[...])
            dx = dres_buf[...] + dx
            dx_ref[...] = dx
            dxb_ref[...] = dx.astype(BF16)
            st_ref[0:1, :] += dnw

    blk = pl.BlockSpec((None, tm, 2 * N_FG), lambda m, p: (p, m, 0))
    row = pl.BlockSpec((tm, D), lambda m, p: (m, 0))
    hbm = pl.BlockSpec(memory_space=pl.ANY)
    return pl.pallas_call(
        body, name="ffn_up_bwd", grid=(nm, NFG),
        in_specs=[blk, hbm, hbm, hbm, hbm, pl.BlockSpec((tm, 1), lambda m, p: (m, 0)),
                  pl.BlockSpec((1, D), lambda m, p: (0, 0))],
        out_specs=[row, row, pl.BlockSpec((8, D), lambda m, p: (0, 0))],
        out_shape=[jax.ShapeDtypeStruct((S, D), F32), jax.ShapeDtypeStruct((S, D), BF16),
                   jax.ShapeDtypeStruct((8, D), F32)],
        scratch_shapes=[pltpu.VMEM((2, 2 * N_FG, D), BF16), pltpu.VMEM((tm, D), F32), pltpu.VMEM((tm, D), F32),
                        pltpu.SemaphoreType.DMA((2,)), pltpu.SemaphoreType.DMA((2,))],
        compiler_params=_cp(("arbitrary", "arbitrary")),
    )(dgu, wgu_a, wgu_b, dres, xs, r, nw)


def _out_proj_bwd(dx2b, wout, place=None, rider=None):
    tm = 256

    if rider is None:
        def body(dx_ref, w_ref, o_ref):
            o_ref[...] = _dot_nt(dx_ref[...], w_ref[...])

        return pl.pallas_call(
            body, name="out_proj_bwd", grid=(S // tm,),
            in_specs=[pl.BlockSpec((tm, D), lambda i: (i, 0)), pl.BlockSpec((D, D), lambda i: (0, 0))],
            out_specs=pl.BlockSpec((tm, D), lambda i: (i, 0)),
            out_shape=jax.ShapeDtypeStruct((S, D), F32),
            compiler_params=_cp(("parallel",)),
        )(dx2b, wout), None

    w = rider[0]
    r, c = w.shape
    rt = _row_tile(r, c)
    nt = r // rt
    assert nt <= S // tm

    def body(pos_ref, dx_ref, w_ref, uw, um, uv, ug, us, uc, o_ref, go, dd, mo, vo):
        o_ref[...] = _dot_nt(dx_ref[...], w_ref[...])

        @pl.when(pl.program_id(0) < nt)
        def _():
            _update_tile(uw, um, uv, ug, us, uc, go, dd, mo, vo)

    def at(i):
        return jnp.minimum(i, nt - 1)

    tile = pl.BlockSpec((rt, c), lambda i, pos: (at(i), 0))
    outs = pl.pallas_call(
        body, name="out_proj_bwd",
        grid_spec=pltpu.PrefetchScalarGridSpec(
            num_scalar_prefetch=1, grid=(S // tm,),
            in_specs=[pl.BlockSpec((tm, D), lambda i, pos: (i, 0)), pl.BlockSpec((D, D), lambda i, pos: (0, 0)),
                      tile, tile, tile,
                      pl.BlockSpec((None, rt, c), lambda i, pos: (4 * pos[0] + 2 * pos[1] + pos[2], at(i), 0)),
                      pl.BlockSpec((None, rt, c), lambda i, pos: (2 * pos[0] + pos[1], at(i), 0)),
                      pl.BlockSpec((3, rt, c), lambda i, pos: (0, at(i), 0))],
            out_specs=[pl.BlockSpec((tm, D), lambda i, pos: (i, 0)), tile, tile, tile, tile]),
        out_shape=[jax.ShapeDtypeStruct((S, D), F32)] + [jax.ShapeDtypeStruct((r, c), F32)] * 4,
        compiler_params=_cp(("arbitrary",)),
    )(place, dx2b, wout, *rider)
    return outs[0], outs[1:]


def _in_proj_bwd(dproj, wins, dres, xs, r, nw):
    tm = 1024

    nr = len(wins)
    nm = S // tm

    def body(dp_ref, *rest):
        w_hbms = rest[:nr]
        dres_hbm, x_hbm, r_ref, nw_ref, dx_ref, st_ref, w_buf, dres_buf, x_buf, sems, w_sems = rest[nr:]
        m, p = pl.program_id(0), pl.program_id(1)
        tail_in = _row_copies((dres_hbm, x_hbm), (dres_buf, x_buf), sems, m, tm)

        def w_copies(g, slot):
            return [pltpu.make_async_copy(w_hbm.at[g], w_buf.at[slot, pl.ds(0, D), pl.ds(off, width)],
                                          w_sems.at[slot, k])
                    for k, (w_hbm, (off, width)) in enumerate(zip(w_hbms, IN_ROUNDS))]

        @pl.when((p == 0) & (m == 0))
        def _():
            st_ref[...] = jnp.zeros_like(st_ref)
            for cp in w_copies(p, 0):
                cp.start()

        @pl.when((p < NDEV - 1) | (m < nm - 1))
        def _():
            for cp in w_copies((p + 1) % NDEV, (p + 1) % 2):
                cp.start()

        @pl.when(p == 0)
        def _():
            dx_ref[...] = jnp.zeros_like(dx_ref)
            for cp in tail_in:
                cp.start()

        for cp in w_copies(p, p % 2):
            cp.wait()
        dx_ref[...] += _dot_nt(dp_ref[...], w_buf[p % 2])

        @pl.when(p == NDEV - 1)
        def _():
            for cp in tail_in:
                cp.wait()
            dx, dnw = _rms_bwd_tile(dx_ref[...], x_buf[...], r_ref[...], nw_ref[...])
            dx_ref[...] = dres_buf[...] + dx
            st_ref[0:1, :] += dnw

    row = pl.BlockSpec((tm, D), lambda m, p: (m, 0))
    hbm = pl.BlockSpec(memory_space=pl.ANY)
    return pl.pallas_call(
        body, name="in_proj_bwd", grid=(S // tm, NDEV),
        in_specs=[pl.BlockSpec((tm, N_IN), lambda m, p: (m, p)),
                  *[hbm] * nr,
                  hbm, hbm, pl.BlockSpec((tm, 1), lambda m, p: (m, 0)),
                  pl.BlockSpec((1, D), lambda m, p: (0, 0))],
        out_specs=[row, pl.BlockSpec((8, D), lambda m, p: (0, 0))],
        out_shape=[jax.ShapeDtypeStruct((S, D), F32), jax.ShapeDtypeStruct((8, D), F32)],
        scratch_shapes=[pltpu.VMEM((2, D, N_IN), BF16), pltpu.VMEM((tm, D), F32), pltpu.VMEM((tm, D), F32),
                        pltpu.SemaphoreType.DMA((2,)), pltpu.SemaphoreType.DMA((2, nr))],
        compiler_params=_cp(("arbitrary", "arbitrary")),
    )(dproj, *wins, dres, xs, r, nw)


W_IN_PARTS = 2


def _wgrad_in(h1, dproj, part):
    rows = D // W_IN_PARTS

    def body(a_ref, d_ref, o_ref):
        both = _dot_tn(a_ref[...], d_ref[...]).astype(BF16)
        o_ref[0] = both[:, 0:N_IN]
        o_ref[1] = both[:, N_IN:2 * N_IN]

    return pl.pallas_call(
        body, name=f"wgrad_in_{part}", grid=(NDEV // 2,),
        in_specs=[pl.BlockSpec((S, rows), lambda p: (0, part)), pl.BlockSpec((S, 2 * N_IN), lambda p: (0, p))],
        out_specs=pl.BlockSpec((2, rows, N_IN), lambda p: (p, 0, 0)),
        out_shape=jax.ShapeDtypeStruct((NDEV, rows, N_IN), BF16),
        compiler_params=_cp(("parallel",)),
    )(h1, dproj)


def _wgrad_rows(a3, dy, name, col=0):
    def body(a_ref, d_ref, o_ref):
        dw = _dot_tn(a_ref[...], d_ref[...]).astype(BF16)
        for j in range(FF_PER):
            o_ref[j] = dw[j * FF_ROWS:(j + 1) * FF_ROWS]

    return pl.pallas_call(
        body, name=name, grid=(NFG,),
        in_specs=[pl.BlockSpec((None, S, N_FG), lambda p: (p, 0, col)), pl.BlockSpec((S, D), lambda p: (0, 0))],
        out_specs=pl.BlockSpec((FF_PER, FF_ROWS, D), lambda p: (p % 2, p // 2, 0)),
        out_shape=jax.ShapeDtypeStruct((NDEV, N_FF, D), BF16),
        compiler_params=_cp(("parallel",)),
    )(a3, dy)


def _wgrad_out(ma, mr, dx2b):
    half = D // 2
    per = half // N_OUT

    def body(ma_ref, mr_ref, d_ref, o_ref):
        p = pl.program_id(0)

        @pl.when(p == 0)
        def _():
            o_ref[...] = _dot_tn(ma_ref[...], d_ref[...]).astype(BF16).reshape(per, N_OUT, D)

        @pl.when(p == 1)
        def _():
            o_ref[...] = _dot_tn(mr_ref[...], d_ref[...]).astype(BF16).reshape(per, N_OUT, D)

    whole = pl.BlockSpec((S, half), lambda p: (0, 0))
    return pl.pallas_call(
        body, name="wgrad_out", grid=(2,),
        in_specs=[whole, whole, pl.BlockSpec((S, D), lambda p: (0, 0))],
        out_specs=pl.BlockSpec((per, N_OUT, D), lambda p: (p, 0, 0)),
        out_shape=jax.ShapeDtypeStruct((NDEV, N_OUT, D), BF16),
        compiler_params=_cp(("parallel",)),
    )(ma, mr, dx2b)


def _attn_consts():
    c = np.zeros((AH, 8, AHD), np.float32)
    for h in range(AH):
        c[h, :, :] = 2.0 ** (-(h + 1))
    return jnp.asarray(c)


def _permute_in(dst, src, d, cast=None):
    v = src[...]
    if d > 1:
        v = pltpu.einshape("jrc->rjc", v.reshape(S // d, d, AHD)).reshape(S, AHD)
    dst[...] = v if cast is None else v.astype(cast)


def _natural_order(v, d):
    if d == 1:
        return v
    return pltpu.einshape("rjc->jrc", v.reshape(d, S // d, AHD)).reshape(S, AHD)


def _attn_masks():
    qi = lax.broadcasted_iota(jnp.int32, (CH, CH), 0)
    kj = lax.broadcasted_iota(jnp.int32, (CH, CH), 1)
    dist_c = (qi - kj).astype(F32)
    dist_p = (qi - kj + CH).astype(F32)
    return (qi >= kj)[None], (kj >= qi)[None], dist_c[None], dist_p[None]


GB = 16


def _bdot_nt(a, b):
    return lax.dot_general(a, b, (((2,), (2,)), ((0,), (0,))), preferred_element_type=F32)


def _bdot(a, b):
    return lax.dot_general(a, b, (((2,), (1,)), ((0,), (0,))), preferred_element_type=F32)


def _bdot_tn(a, b):
    return lax.dot_general(a, b, (((1,), (1,)), ((0,), (0,))), preferred_element_type=F32)


def _shift_block(dst, src):
    dst[0:CH, :] = jnp.zeros((CH, AHD), dst.dtype)
    dst[CH:S, :] = src[0:S - CH, :]


def _has_prev(g, nb):
    blk = lax.broadcasted_iota(jnp.int32, (GB, 1, 1), 0) + g * GB
    return (blk & (nb - 1)) != 0


def _blocks(ref, g):
    return ref[g * GB * CH:(g + 1) * GB * CH, :].reshape(GB, CH, AHD)


def _attn_fwd(proj):
    scale = 1.0 / math.sqrt(AHD)

    def body(c_ref, q_ref, k_ref, v_ref, o_ref, ob_ref, lse_ref, qkvp_ref, lsep_ref, qd, kd, vd, kps, vps, od, ld, *nat):
        onat, lnat = nat[0:3], nat[3:6]
        slope = c_ref[0:1, :]
        mask_c, mask_p, dist_c, dist_p = _attn_masks()
        for pi, (d, nb) in enumerate(PATTERNS):
            _permute_in(qd, q_ref, d, BF16)
            _permute_in(kd, k_ref, d, BF16)
            _permute_in(vd, v_ref, d, BF16)
            if d > 1:
                qkvp_ref[pi - 1, 0] = qd[...]
                qkvp_ref[pi - 1, 1] = kd[...]
                qkvp_ref[pi - 1, 2] = vd[...]
            if nb > 1:
                _shift_block(kps, kd)
                _shift_block(vps, vd)
            bias_c = -(slope * float(d)) * dist_c
            bias_p = -(slope * float(d)) * dist_p
            for g in range(NB // GB):
                q3, k3, v3 = _blocks(qd, g), _blocks(kd, g), _blocks(vd, g)
                s_c = jnp.where(mask_c, _bdot_nt(q3, k3) * scale + bias_c, NEG)
                mx = jnp.max(s_c, axis=-1, keepdims=True)
                if nb > 1:
                    kp3, vp3 = _blocks(kps, g), _blocks(vps, g)
                    s_p = jnp.where(jnp.logical_and(mask_p, _has_prev(g, nb)),
                                    _bdot_nt(q3, kp3) * scale + bias_p, NEG)
                    mx = jnp.maximum(mx, jnp.max(s_p, axis=-1, keepdims=True))
                    l = (jnp.sum(jnp.exp(s_c - mx), axis=-1, keepdims=True)
                         + jnp.sum(jnp.exp(s_p - mx), axis=-1, keepdims=True))
                    lse = mx + jnp.log(l)
                    o3 = _bdot(jnp.exp(s_c - lse).astype(BF16), v3) + _bdot(jnp.exp(s_p - lse).astype(BF16), vp3)
                else:
                    l = jnp.sum(jnp.exp(s_c - mx), axis=-1, keepdims=True)
                    lse = mx + jnp.log(l)
                    o3 = _bdot(jnp.exp(s_c - lse).astype(BF16), v3)
                rows = slice(g * GB * CH, (g + 1) * GB * CH)
                od[rows, :] = o3.reshape(GB * CH, AHD)
                ld[rows, :] = jnp.broadcast_to(lse, (GB, CH, AHD)).reshape(GB * CH, AHD)
            onat[pi][...] = _natural_order(od[...], d)
            lnat[pi][...] = _natural_order(ld[...], d)
        l0, l1, l2 = lnat[0][...], lnat[1][...], lnat[2][...]
        mx = jnp.maximum(jnp.maximum(l0, l1), l2)
        e0, e1, e2 = jnp.exp(l0 - mx), jnp.exp(l1 - mx), jnp.exp(l2 - mx)
        den = e0 + e1 + e2
        out = (e0 / den) * onat[0][...] + (e1 / den) * onat[1][...] + (e2 / den) * onat[2][...]
        o_ref[...] = out
        ob_ref[...] = out.astype(BF16)
        lse_ref[...] = mx + jnp.log(den)
        for pi, (d, _) in enumerate(PATTERNS[1:]):
            _permute_in(lsep_ref.at[pi], lse_ref, d)

    def col(off):
        return pl.BlockSpec((S, AHD), lambda h: (0, off + h))

    return pl.pallas_call(
        body, name="attn_fwd", grid=(AH,),
        in_specs=[pl.BlockSpec((None, 8, AHD), lambda h: (h, 0, 0)), col(0), col(AH), col(2 * AH)],
        out_specs=[col(0), col(0), col(0), pl.BlockSpec((2, 3, S, AHD), lambda h: (0, 0, 0, h)),
                   pl.BlockSpec((2, S, AHD), lambda h: (0, 0, h))],
        out_shape=[jax.ShapeDtypeStruct((S, AH * AHD), F32), jax.ShapeDtypeStruct((S, AH * AHD), BF16),
                   jax.ShapeDtypeStruct((S, AH * AHD), F32),
                   jax.ShapeDtypeStruct((2, 3, S, AH * AHD), BF16), jax.ShapeDtypeStruct((2, S, AH * AHD), F32)],
        scratch_shapes=[pltpu.VMEM((S, AHD), BF16) for _ in range(5)]
        + [pltpu.VMEM((S, AHD), F32) for _ in range(8)],
        compiler_params=_cp(("parallel",)),
    )(_attn_consts(), proj, proj, proj)


def _attn_bwd(proj, dmixed, o, lse, qkvp, lsep):
    scale = 1.0 / math.sqrt(AHD)

    def body(c_ref, q_ref, k_ref, v_ref, do_ref, o_ref, lse_ref, qkvp_ref, lsep_ref, dproj_hbm,
             qd, kd, vd, dod, kps, vps, dld, dqd, dkd, dvd, delta, aq, ak, av, sq, sk, sv, sems):
        h = pl.program_id(0)

        def out_copies(head):
            return [pltpu.make_async_copy(
                st, dproj_hbm.at[:, pl.ds(pl.multiple_of((k * AH + head) * AHD, AHD), AHD)], sems.at[k])
                for k, st in enumerate((sq, sk, sv))]

        slope = c_ref[0:1, :]
        mask_c, mask_p, dist_c, dist_p = _attn_masks()
        delta[...] = jnp.broadcast_to(jnp.sum(do_ref[...] * o_ref[...], axis=-1, keepdims=True), (S, AHD))
        for pi, (d, nb) in enumerate(PATTERNS):
            if d == 1:
                _permute_in(qd, q_ref, d, BF16)
                _permute_in(kd, k_ref, d, BF16)
                _permute_in(vd, v_ref, d, BF16)
                qs, ks, vs, lss = qd, kd, vd, lse_ref
            else:
                qs, ks, vs, lss = (qkvp_ref.at[pi - 1, 0], qkvp_ref.at[pi - 1, 1], qkvp_ref.at[pi - 1, 2],
                                   lsep_ref.at[pi - 1])
            _permute_in(dod, do_ref, d, BF16)
            _permute_in(dld, delta, d)
            if nb > 1:
                _shift_block(kps, ks)
                _shift_block(vps, vs)
            bias_c = -(slope * float(d)) * dist_c
            bias_p = -(slope * float(d)) * dist_p
            for g in range(NB // GB):
                q3, k3, v3, do3 = _blocks(qs, g), _blocks(ks, g), _blocks(vs, g), _blocks(dod, g)
                ls, dl = _blocks(lss, g), _blocks(dld, g)
                lo, hi = g * GB * CH, (g + 1) * GB * CH
                p_c = jnp.exp(jnp.where(mask_c, _bdot_nt(q3, k3) * scale + bias_c, NEG) - ls)
                ds_c = ((p_c * (_bdot_nt(do3, v3) - dl)) * scale).astype(BF16)
                dq3 = _bdot(ds_c, k3)
                dkd[lo:hi, :] = _bdot_tn(ds_c, q3).reshape(GB * CH, AHD)
                dvd[lo:hi, :] = _bdot_tn(p_c.astype(BF16), do3).reshape(GB * CH, AHD)
                if nb > 1:
                    kp3, vp3 = _blocks(kps, g), _blocks(vps, g)
                    p_p = jnp.exp(jnp.where(jnp.logical_and(mask_p, _has_prev(g, nb)),
                                            _bdot_nt(q3, kp3) * scale + bias_p, NEG) - ls)
                    ds_p = ((p_p * (_bdot_nt(do3, vp3) - dl)) * scale).astype(BF16)
                    dq3 = dq3 + _bdot(ds_p, kp3)
                    dkp = _bdot_tn(ds_p, q3).reshape(GB * CH, AHD)
                    dvp = _bdot_tn(p_p.astype(BF16), do3).reshape(GB * CH, AHD)
                    if g == 0:
                        dkd[0:hi - CH, :] += dkp[CH:, :]
                        dvd[0:hi - CH, :] += dvp[CH:, :]
                    else:
                        dkd[lo - CH:hi - CH, :] += dkp
                        dvd[lo - CH:hi - CH, :] += dvp
                dqd[lo:hi, :] = dq3.reshape(GB * CH, AHD)
            ln = S // d
            for acc, src in ((aq, dqd), (ak, dkd), (av, dvd)):
                if pi == 0:
                    acc[...] = src[...]
                else:
                    acc[...] += _natural_order(src[...], d)

        @pl.when(h > 0)
        def _():
            for cp in out_copies(h - 1):
                cp.wait()

        sq[...] = aq[...].astype(BF16)
        sk[...] = ak[...].astype(BF16)
        sv[...] = av[...].astype(BF16)
        for cp in out_copies(h):
            cp.start()

        @pl.when(h == AH - 1)
        def _():
            for cp in out_copies(h):
                cp.wait()

    def col(off):
        return pl.BlockSpec((S, AHD), lambda h: (0, off + h))

    return pl.pallas_call(
        body, name="attn_bwd", grid=(AH,),
        in_specs=[pl.BlockSpec((None, 8, AHD), lambda h: (h, 0, 0)), col(0), col(AH), col(2 * AH),
                  col(0), col(0), col(0), pl.BlockSpec((2, 3, S, AHD), lambda h: (0, 0, 0, h)),
                  pl.BlockSpec((2, S, AHD), lambda h: (0, 0, h))],
        out_specs=pl.BlockSpec(memory_space=pl.ANY),
        out_shape=jax.ShapeDtypeStruct((S, NDEV * N_IN), BF16),
        scratch_shapes=[pltpu.VMEM((S, AHD), BF16) for _ in range(6)]
        + [pltpu.VMEM((S, AHD), F32) for _ in range(8)]
        + [pltpu.VMEM((S, AHD), BF16) for _ in range(3)] + [pltpu.SemaphoreType.DMA((3,))],
        compiler_params=_cp(("arbitrary",)),
    )(_attn_consts(), proj, proj, proj, dmixed, o, lse, qkvp, lsep)


def _ret_consts():
    c = np.zeros((RH, 8, RHD), np.float32)
    for h in range(RH):
        c[h, :, :] = np.log(np.float32(1.0) - np.float32(2.0 ** (-5.0 - h)))
    return jnp.asarray(c)


def _ret_factors(lg):
    i = lax.broadcasted_iota(jnp.int32, (CH, CH), 0)
    j = lax.broadcasted_iota(jnp.int32, (CH, CH), 1)
    dif = (i - j).astype(F32)
    decay = jnp.where(dif >= 0, jnp.exp(lg[:, 0:CH] * jnp.maximum(dif, 0.0)), 0.0)
    row = lax.broadcasted_iota(jnp.int32, (CH, RHD), 0).astype(F32)
    zeta = jnp.exp(lg * (CH - 1.0 - row))
    xi = jnp.exp(lg * (row + 1.0))
    return decay, zeta, xi, jnp.exp(lg * float(CH))


CBK = 8
RSTEPS = NB // CBK


def _ret_specs(rev):
    off = 3 * AH * AHD // RHD
    rows = CBK * CH

    def ch(n):
        return (RSTEPS - 1 - n) if rev else n

    def col(k):
        return pl.BlockSpec((rows, RHD), lambda h, n: (ch(n), off + k * RH + h))

    own = pl.BlockSpec((rows, RHD), lambda h, n: (ch(n), h))
    state = pl.BlockSpec((None, CBK, RHD, RHD), lambda h, n: (h, ch(n), 0, 0))
    const = pl.BlockSpec((None, 8, RHD), lambda h, n: (h, 0, 0))
    dm = pl.BlockSpec((rows, RHD), lambda h, n: (ch(n), AH * AHD // RHD + h))
    return col, own, state, const, dm


def _chunks(x):
    return x.reshape(CBK, CH, RHD)


def _ret_fwd(proj):
    def body(c_ref, q_ref, k_ref, v_ref, g_ref, ret_ref, mr_ref, st_ref, r_acc):
        n = pl.program_id(1)

        @pl.when(n == 0)
        def _():
            r_acc[...] = jnp.zeros_like(r_acc)

        decay, zeta, xi, gch = _ret_factors(c_ref[0:1, :])
        q3 = _chunks(q_ref[...].astype(BF16))
        kc = _chunks(k_ref[...] * (1.0 / math.sqrt(RHD)))
        k3 = kc.astype(BF16)
        v3 = _chunks(v_ref[...].astype(BF16))
        kv3 = _bdot_tn((kc * zeta[None]).astype(BF16), v3)
        r = r_acc[...]
        for i in range(CBK):
            st_ref[i] = r.astype(BF16)
            r = r * gch + kv3[i]
        r_acc[...] = r
        scores = _bdot_nt(q3, k3) * decay[None]
        ret = (_bdot(scores.astype(BF16), v3) + _bdot(q3, st_ref[...]) * xi[None]).reshape(CBK * CH, RHD)
        ret_ref[...] = ret
        rr = lax.rsqrt(jnp.mean(ret * ret, axis=-1, keepdims=True) + EPS)
        gv = g_ref[...]
        mr_ref[...] = ((gv * _sigmoid(gv)) * (ret * rr)).astype(BF16)

    col, own, state, const, _ = _ret_specs(False)
    return pl.pallas_call(
        body, name="ret_fwd", grid=(RH, RSTEPS),
        in_specs=[const, col(0), col(1), col(2), col(3)],
        out_specs=[own, own, state],
        out_shape=[jax.ShapeDtypeStruct((S, RH * RHD), F32), jax.ShapeDtypeStruct((S, RH * RHD), BF16),
                   jax.ShapeDtypeStruct((RH, NB, RHD, RHD), BF16)],
        scratch_shapes=[pltpu.VMEM((RHD, RHD), F32)],
        compiler_params=_cp(("parallel", "arbitrary")),
    )(_ret_consts(), proj, proj, proj, proj)


def _ret_bwd(proj, ret, states, dmixed, dproj):
    rows = CBK * CH
    col0 = 3 * AH * AHD

    def body(c_ref, q_ref, k_ref, v_ref, g_ref, ret_ref, st_ref, dm_ref, dproj_in, dproj_hbm, g_acc, gs,
             sq, sk, sv, sg, sems):
        del dproj_in
        h, n = pl.program_id(0), pl.program_id(1)
        step = h * RSTEPS + n

        def out_copies(t):
            hh, nn = t // RSTEPS, t % RSTEPS
            r0 = pl.multiple_of((RSTEPS - 1 - nn) * rows, rows)
            return [pltpu.make_async_copy(
                st, dproj_hbm.at[pl.ds(r0, rows), pl.ds(pl.multiple_of(col0 + (k * RH + hh) * RHD, RHD), RHD)],
                sems.at[k]) for k, st in enumerate((sq, sk, sv, sg))]

        @pl.when(n == 0)
        def _():
            g_acc[...] = jnp.zeros_like(g_acc)

        decay, zeta, xi, gch = _ret_factors(c_ref[0:1, :])
        ret_v = ret_ref[...]
        rr = lax.rsqrt(jnp.mean(ret_v * ret_v, axis=-1, keepdims=True) + EPS)
        gv = g_ref[...]
        sgm = _sigmoid(gv)
        dmix = dm_ref[...]
        dgate = ((dmix * (ret_v * rr)) * (sgm * (1.0 + gv * (1.0 - sgm)))).astype(BF16)
        dretn = dmix * (gv * sgm)
        dret = _chunks(rr * dretn - ret_v * ((rr * rr * rr) * jnp.mean(dretn * ret_v, axis=-1, keepdims=True)))

        q3 = _chunks(q_ref[...].astype(BF16))
        kc = _chunks(k_ref[...] * (1.0 / math.sqrt(RHD)))
        k3 = kc.astype(BF16)
        v3 = _chunks(v_ref[...].astype(BF16))
        d3 = dret.astype(BF16)
        dxi = (dret * xi[None]).astype(BF16)
        kz = (kc * zeta[None]).astype(BF16)
        dr3 = _bdot_tn(q3, dxi)
        acc = g_acc[...]
        for i in reversed(range(CBK)):
            gs[i] = acc.astype(BF16)
            acc = dr3[i] + gch * acc
        g_acc[...] = acc
        g3 = gs[...]
        sc = (_bdot_nt(q3, k3) * decay[None]).astype(BF16)
        da = (_bdot_nt(d3, v3) * decay[None]).astype(BF16)
        dq = _bdot(da, k3) + _bdot_nt(dxi, st_ref[...])
        dkc = _bdot_tn(da, q3) + _bdot_nt(v3, g3) * zeta[None]
        dv = _bdot_tn(sc, d3) + _bdot(kz, g3)

        @pl.when(step > 0)
        def _():
            for cp in out_copies(step - 1):
                cp.wait()

        sq[...] = dq.reshape(rows, RHD).astype(BF16)
        sk[...] = (dkc * (1.0 / math.sqrt(RHD))).reshape(rows, RHD).astype(BF16)
        sv[...] = dv.reshape(rows, RHD).astype(BF16)
        sg[...] = dgate
        for cp in out_copies(step):
            cp.start()

        @pl.when(step == RH * RSTEPS - 1)
        def _():
            for cp in out_copies(step):
                cp.wait()

    col, own, state, const, dm = _ret_specs(True)
    hbm = pl.BlockSpec(memory_space=pl.ANY)
    return pl.pallas_call(
        body, name="ret_bwd", grid=(RH, RSTEPS),
        in_specs=[const, col(0), col(1), col(2), col(3), own, state, dm, hbm],
        out_specs=hbm,
        out_shape=jax.ShapeDtypeStruct(dproj.shape, dproj.dtype),
        input_output_aliases={8: 0},
        scratch_shapes=[pltpu.VMEM((RHD, RHD), F32), pltpu.VMEM((CBK, RHD, RHD), BF16)]
        + [pltpu.VMEM((rows, RHD), BF16) for _ in range(4)] + [pltpu.SemaphoreType.DMA((4,))],
        compiler_params=_cp(("arbitrary", "arbitrary")),
    )(_ret_consts(), proj, proj, proj, proj, ret, states, dmixed, dproj)


class _NoReduction:
    def start(self, group, grads):
        pass

    def local(self, name, first=()):
        return []

    def landed(self, name):
        return []

    def update(self, name):
        return []

    place = None

    def rider(self, name):
        return None

    def set_update(self, name, outs):
        pass


def _local_step(x, tgt, nw1, nw2, nw3, win, wout, wgu_a, wgu_b, wd_a, wd_b, red=None):
    red = red or _NoReduction()

    def after(values, first):
        return lax.optimization_barrier((tuple(values), tuple(first)))[0]

    h1, r1 = _rms_fwd(x, nw1)
    proj = _proj(h1, win)
    o, ma, lse, qkvp, lsep = _attn_fwd(proj)
    ret, mr, states = _ret_fwd(proj)
    x2, h2, r2 = _out_proj_rms(x, ma, mr, wout, nw2)
    a, dadg, dadu = _ffn_up(h2, wgu_b, 1, _ffn_up(h2, wgu_a, 0))
    dx3, dx3b, st3 = _ffn_down_loss(_ffn_down_first(x2, a, wd_a), a, wd_b, nw3, tgt)

    dwd = _wgrad_rows(a, dx3b, "wgrad_down")
    red.start(["w_down"], [dwd])
    (dx3b,) = after([dx3b], [dwd])
    part = _ffn_down_bwd(dx3b, wd_a, dadg, dadu, 0)
    (dx3b,) = after([dx3b], red.local("w_down", first=[part]))
    dgu = _ffn_down_bwd(dx3b, wd_b, dadg, dadu, 1, [part])
    dwg = _wgrad_rows(dgu, h2, "wgrad_gate", 0)
    red.start(["w_gate"], [dwg])
    (dgu,) = after([dgu], [dwg])
    dwu = _wgrad_rows(dgu, h2, "wgrad_up", 1)
    red.start(["w_up"], [dwu])
    (dgu,) = after([dgu], red.local("w_gate", first=[dwu] + red.landed("w_down")))
    dx2, dx2b, st2 = _ffn_up_bwd(dgu, wgu_a, wgu_b, dx3, x2, r2, nw2)
    (dx2b,) = after([dx2b], red.local("w_up", first=[dx2b]))
    dwo = _wgrad_out(ma, mr, dx2b)
    red.start(["w_out"], [dwo])
    (dx2b,) = after([dx2b], [dwo])
    dmixed, done = _out_proj_bwd(dx2b, wout, red.place, red.rider("w_down"))
    red.set_update("w_down", done)
    dproj = _attn_bwd(proj, dmixed, o, lse, qkvp, lsep)
    (dmixed,) = after([dmixed], red.local("w_out", first=[dproj] + red.landed("w_gate")))
    dproj = _ret_bwd(proj, ret, states, dmixed, dproj)
    (dwi0,) = after([_wgrad_in(h1, dproj, 0)], red.landed("w_up"))
    red.start(["w_in_0"], [dwi0])
    (dproj,) = after([dproj], [dwi0])
    dwi1 = _wgrad_in(h1, dproj, 1)
    red.start(["w_in_1"], [dwi1])
    sums = red.local("w_in_0", first=[dwi1] + red.landed("w_out"))
    sums = red.local("w_in_1", first=sums + red.update("w_gate"))
    (dproj,) = after([dproj], sums)
    gx, st1 = _in_proj_bwd(dproj, win, dx2, x, r1, nw1)
    dwi = jnp.concatenate([dwi0, dwi1], axis=1)
    stats = jnp.concatenate([st1[0:1], st2[0:1], st3[0:2], jnp.zeros((4, D), F32)], axis=0)
    return stats, gx, dwi, dwo, dwg, dwu, dwd


def _place():
    x, y, c = lax.axis_index("x"), lax.axis_index("y"), lax.axis_index("c")
    return x, y, c, [(1 - x, y), (x, 1 - y), (1 - x, 1 - y)]


def _handshake(peers):
    barrier = pltpu.get_barrier_semaphore()
    for peer in peers:
        pl.semaphore_signal(barrier, inc=1, device_id=peer, device_id_type=MESH)
    pl.semaphore_wait(barrier, len(peers))


def _all_gather(shards, name, collective_id, per=0, rows=None):
    na = len(shards)
    nout = 1 if per else na
    lo, r = rows or (0, shards[0].shape[0])
    ngroups = NDEV // per if per else 0
    SIB, XN0, XN1, YN1, YN0, VIA_X, VIA_Y = 0, 1, 2, 3, 4, 5, 6
    D2D = {XN0: 7, XN1: 8, YN1: 9, YN0: 10, VIA_X: 11, VIA_Y: 12}

    def body(*refs):
        ins, outs = [ref.at[pl.ds(lo, r)] for ref in refs[:na]], refs[na:na + nout]
        send_sems, recv_sems, local_sems = refs[na + nout:]
        x, y, c, _ = _place()
        me, sib = (x, y, c), (x, y, 1 - c)
        xn, yn, dg = (1 - x, y, c), (x, 1 - y, c), (1 - x, 1 - y, c)
        _handshake([sib, xn, yn])

        def part(ref, h):
            rows = ref.shape[0] // 2
            return ref if h is None else ref.at[pl.ds(h * rows, rows)]

        def block(a, owner, h):
            idx = 4 * owner[0] + 2 * owner[1] + owner[2]
            if not per:
                return part(outs[a].at[idx], h)
            return part(outs[0].at[idx // per, a, pl.ds(pl.multiple_of((idx % per) * r, r), r)], h)

        def copy(a, k, owner, h, to, own_src=False):
            return pltpu.make_async_remote_copy(
                src_ref=part(ins[a], h) if own_src else block(a, owner, h), dst_ref=block(a, owner, h),
                send_sem=send_sems.at[a, k], recv_sem=recv_sems.at[a, k], device_id=to, device_id_type=MESH)

        def other(p):
            return (p[0], p[1], 1 - c)

        mine = [pltpu.make_async_copy(ins[a], block(a, me, None), local_sems.at[a]) for a in range(na)]
        for cp in mine:
            cp.start()
        sent = []
        for a in range(na):
            sent += [copy(a, XN0, me, 0, xn, True), copy(a, YN1, me, 1, yn, True),
                     copy(a, XN1, me, 1, xn, True), copy(a, YN0, me, 0, yn, True)]
        sent += [copy(a, SIB, me, None, sib, True) for a in range(na)]
        for cp in sent:
            cp.start()

        def landed(a, k, owner, h, then):
            copy(a, k, owner, h, me).wait_recv()
            for k2, to in then + [(D2D[k], sib)]:
                cp = copy(a, k2, owner, h, to)
                cp.start()
                sent.append(cp)

        for a in range(na):
            landed(a, XN0, xn, 0, [(VIA_Y, yn)])
            landed(a, YN1, yn, 1, [(VIA_X, xn)])
            landed(a, XN1, xn, 1, [])
            landed(a, YN0, yn, 0, [])
        for a in range(na):
            landed(a, VIA_Y, dg, 0, [])
            landed(a, VIA_X, dg, 1, [])
        for a in range(na):
            copy(a, SIB, sib, None, me).wait_recv()
            for k, owner, h in ((XN0, xn, 0), (XN1, xn, 1), (YN1, yn, 1), (YN0, yn, 0), (VIA_Y, dg, 0), (VIA_X, dg, 1)):
                copy(a, D2D[k], other(owner), h, me).wait_recv()
        for cp in sent:
            cp.wait_send()
        for cp in mine:
            cp.wait()

    if per:
        out_type = [jax.ShapeDtypeStruct((ngroups, na, per * r, shards[0].shape[1]), shards[0].dtype)]
    else:
        out_type = [jax.ShapeDtypeStruct((NDEV,) + s.shape, s.dtype) for s in shards]
    return _sequencer_call(
        body, name, collective_id, out_type,
        [pltpu.SemaphoreType.DMA((na, 13)), pltpu.SemaphoreType.DMA((na, 13)), pltpu.SemaphoreType.DMA((na,))])(*shards)


def _sequencer_call(body, name, collective_id, out_type, scratch_types):
    return pl.kernel(
        body, name=name, out_type=out_type,
        mesh=plsc.ScalarSubcoreMesh(axis_name="sequencer", num_cores=1),
        scratch_types=scratch_types,
        compiler_params=pltpu.CompilerParams(collective_id=collective_id))


def _exchange_sibling(grads, name, collective_id):
    na = len(grads)

    def body(*refs):
        ins, outs = refs[:na], refs[na:2 * na]
        send_sems, recv_sems = refs[2 * na:]
        x, y, c, _ = _place()
        _handshake([(x, y, 1 - c)])
        cps = []
        for a in range(na):
            for k in range(4):
                cps.append(pltpu.make_async_remote_copy(
                    src_ref=ins[a].at[2 * k + (1 - c)], dst_ref=outs[a].at[k],
                    send_sem=send_sems.at[a, k], recv_sem=recv_sems.at[a, k],
                    device_id=(x, y, 1 - c), device_id_type=MESH))
        for cp in cps:
            cp.start()
        for cp in cps:
            cp.wait()

    return _sequencer_call(
        body, name, collective_id,
        [jax.ShapeDtypeStruct((4,) + g.shape[1:], g.dtype) for g in grads],
        [pltpu.SemaphoreType.DMA((na, 4)), pltpu.SemaphoreType.DMA((na, 4))])(*grads)


def _row_tile(rows, cols):
    for t in (512, 256, 176, 128, 64, 32, 16):
        if rows % t == 0 and t * cols * 4 <= (2 << 20):
            return t
    raise ValueError((rows, cols))


STREAM_BUFS = 3


def _stream_tile(rows, steps):
    for t in (512, 256, 176, 128, 64, 32, 16):
        if rows % t == 0 and rows // t >= steps:
            return t
    raise ValueError((rows, steps))


def _stream(n, loads, stores, compute):
    for k in range(min(STREAM_BUFS, n)):
        for cp in loads(k):
            cp.start()
    for k in range(n):
        for cp in loads(k):
            cp.wait()
        if k >= 2:
            for cp in stores(k - 2):
                cp.wait()
        compute(k)
        for cp in stores(k):
            cp.start()
        if k + STREAM_BUFS < n:
            for cp in loads(k + STREAM_BUFS):
                cp.start()
    for k in range(max(n - 2, 0), n):
        for cp in stores(k):
            cp.wait()


def _chip_sum(place, g, got, name):
    _, r, c = g.shape
    tm = _stream_tile(r, 4)
    nt = r // tm

    def body(pos_ref, g_hbm, got_hbm, o_hbm, g_buf, s_buf, o_buf, sem_in, sem_out):
        def chip(j):
            return 2 * (pos_ref[0] ^ (0 if j == 1 else 1)) + (pos_ref[1] ^ (0 if j == 0 else 1))

        def loads(k):
            j, rows, slot = k // nt, pl.ds((k % nt) * tm, tm), k % STREAM_BUFS
            return [pltpu.make_async_copy(g_hbm.at[2 * chip(j) + pos_ref[2], rows], g_buf.at[slot], sem_in.at[slot, 0]),
                    pltpu.make_async_copy(got_hbm.at[chip(j), rows], s_buf.at[slot], sem_in.at[slot, 1])]

        def stores(k):
            return [pltpu.make_async_copy(o_buf.at[k % 2], o_hbm.at[k // nt, pl.ds((k % nt) * tm, tm)],
                                          sem_out.at[k % 2])]

        def compute(k):
            slot = k % STREAM_BUFS
            o_buf[k % 2] = (g_buf[slot].astype(F32) + s_buf[slot].astype(F32)).astype(BF16)

        _stream(3 * nt, loads, stores, compute)

    hbm = pl.BlockSpec(memory_space=pl.ANY)
    return pl.pallas_call(
        body, name=name,
        grid_spec=pltpu.PrefetchScalarGridSpec(
            num_scalar_prefetch=1, grid=(1,), in_specs=[hbm, hbm], out_specs=hbm,
            scratch_shapes=[pltpu.VMEM((STREAM_BUFS, tm, c), BF16), pltpu.VMEM((STREAM_BUFS, tm, c), BF16),
                            pltpu.VMEM((2, tm, c), BF16),
                            pltpu.SemaphoreType.DMA((STREAM_BUFS, 2)), pltpu.SemaphoreType.DMA((2,))]),
        out_shape=jax.ShapeDtypeStruct((3, r, c), BF16),
        compiler_params=_cp(("arbitrary",)),
    )(place, g, got)


def _exchange_chips(sums, name, collective_id):
    na = len(sums)

    def body(*refs):
        ins, outs = refs[:na], refs[na:2 * na]
        send_sems, recv_sems = refs[2 * na:]
        x, y, c, chips = _place()
        _handshake([(*chip, c) for chip in chips])
        cps = []
        for a in range(na):
            for j, chip in enumerate(chips):
                cps.append(pltpu.make_async_remote_copy(
                    src_ref=ins[a].at[j], dst_ref=outs[a].at[j],
                    send_sem=send_sems.at[a, j], recv_sem=recv_sems.at[a, j],
                    device_id=(*chip, c), device_id_type=MESH))
        for cp in cps:
            cp.start()
        for cp in cps:
            cp.wait()

    return _sequencer_call(
        body, name, collective_id,
        [jax.ShapeDtypeStruct((3,) + s.shape[1:], s.dtype) for s in sums],
        [pltpu.SemaphoreType.DMA((na, 3)), pltpu.SemaphoreType.DMA((na, 3))])(*sums)


def _exchange_stats(stats, collective_id):
    def body(st_in, st_out, st_send, st_recv, local_sem):
        x, y, c, _ = _place()
        me_idx = 4 * x + 2 * y + c
        peers = [(x ^ ((k >> 2) & 1), y ^ ((k >> 1) & 1), c ^ (k & 1)) for k in range(1, 8)]
        _handshake(peers)
        mine = pltpu.make_async_copy(st_in, st_out.at[me_idx], local_sem)
        mine.start()
        cps = [pltpu.make_async_remote_copy(
            src_ref=st_in, dst_ref=st_out.at[me_idx], send_sem=st_send.at[k], recv_sem=st_recv.at[k],
            device_id=peer, device_id_type=MESH) for k, peer in enumerate(peers)]
        for cp in cps:
            cp.start()
        for cp in cps:
            cp.wait()
        mine.wait()

    return _sequencer_call(
        body, "exchange_stats", collective_id,
        jax.ShapeDtypeStruct((NDEV,) + stats.shape, stats.dtype),
        [pltpu.SemaphoreType.DMA((7,)), pltpu.SemaphoreType.DMA((7,)), pltpu.SemaphoreType.DMA])(stats)


class _Reduction:
    def __init__(self, place, first_collective_id, state):
        self.place = place
        self.ids = iter(range(first_collective_id, 32))
        self.state = state
        self.groups = {}
        self.updates = {}

    def next_id(self):
        return next(self.ids)

    def start(self, group, grads):
        got = _exchange_sibling(grads, "sibling_exchange_" + group[0], self.next_id())
        self.groups[group[0]] = dict(names=group, grads=grads, got=got)

    def local(self, name, first=()):
        grp = self.groups[name]
        grads = lax.optimization_barrier((tuple(grp["grads"]), tuple(first)))[0]
        grp["sums"] = [_chip_sum(self.place, g, s, "chip_sum_" + n)
                       for g, s, n in zip(grads, grp["got"], grp["names"])]
        grp["chips"] = _exchange_chips(grp["sums"], "chip_exchange_" + name, self.next_id())
        return grp["sums"]

    def landed(self, name):
        return list(self.groups[name]["chips"])

    def rider(self, name):
        grp = next(g for g in self.groups.values() if name in g["names"])
        k = grp["names"].index(name)
        return self.state[name][:3] + (grp["grads"][k], grp["got"][k], grp["chips"][k])

    def set_update(self, name, outs):
        self.updates[name] = list(outs)

    def update(self, name):
        if name not in self.updates:
            grp = next(g for g in self.groups.values() if name in g["names"])
            k = grp["names"].index(name)
            w, m, v, part, parts = self.state[name]
            before = self.update(f"{name[:-1]}{part - 1}") if part else None
            self.updates[name] = _shard_update(self.place, w, m, v, grp["grads"][k], grp["got"][k],
                                               grp["chips"][k], "update_" + name, part, parts, before)
        return list(self.updates[name])


def _adamw(w, g, m, v):
    m = ADAM_B1 * m + (1.0 - ADAM_B1) * g
    v = ADAM_B2 * v + (1.0 - ADAM_B2) * (g * g)
    m_hat = m / (1.0 - ADAM_B1 ** ADAM_STEP)
    v_hat = v / (1.0 - ADAM_B2 ** ADAM_STEP)
    delta = -ADAM_LR * (m_hat / (jnp.sqrt(v_hat) + ADAM_EPS) + ADAM_WD * w)
    return delta, m, v


def _update_tile(w_ref, m_ref, v_ref, g_ref, s_ref, c_ref, go_ref, d_ref, mo_ref, vo_ref):
    grad = g_ref[...].astype(F32) + s_ref[...].astype(F32)
    for j in range(3):
        grad = grad + c_ref[j].astype(F32)
    delta, mn, vn = _adamw(w_ref[...], grad, m_ref[...], v_ref[...])
    go_ref[...] = grad
    d_ref[...] = delta
    mo_ref[...] = mn
    vo_ref[...] = vn


def _shard_update(place, w, m, v, g, got_sib, got_chips, name, part=0, parts=1, before=None):
    r, c = w.shape
    rp = r // parts
    tm = _stream_tile(rp, 8)
    nt = rp // tm
    before = list(before or [])

    def body(pos_ref, w_hbm, m_hbm, v_hbm, g_hbm, s_hbm, c_hbm, *rest):
        outs = rest[len(before):len(before) + 4]
        w_buf, m_buf, v_buf, g_buf, s_buf, c_buf, o_buf, sem_in, sem_out = rest[len(before) + 4:]
        own = 4 * pos_ref[0] + 2 * pos_ref[1] + pos_ref[2]
        chip = 2 * pos_ref[0] + pos_ref[1]

        def loads(k):
            slot, rows, mine = k % STREAM_BUFS, pl.ds(k * tm, tm), pl.ds(part * rp + k * tm, tm)
            pairs = [(w_hbm.at[mine], w_buf), (m_hbm.at[mine], m_buf), (v_hbm.at[mine], v_buf),
                     (g_hbm.at[own, rows], g_buf), (s_hbm.at[chip, rows], s_buf), (c_hbm.at[:, rows], c_buf)]
            return [pltpu.make_async_copy(src, buf.at[slot], sem_in.at[slot, n]) for n, (src, buf) in enumerate(pairs)]

        def stores(k):
            mine = pl.ds(part * rp + k * tm, tm)
            return [pltpu.make_async_copy(o_buf.at[k % 2, n], out.at[mine], sem_out.at[k % 2, n])
                    for n, out in enumerate(outs)]

        def compute(k):
            slot = k % STREAM_BUFS
            _update_tile(w_buf.at[slot], m_buf.at[slot], v_buf.at[slot], g_buf.at[slot], s_buf.at[slot],
                         c_buf.at[slot], *[o_buf.at[k % 2, n] for n in range(4)])

        _stream(nt, loads, stores, compute)

    hbm = pl.BlockSpec(memory_space=pl.ANY)
    return pl.pallas_call(
        body, name=name,
        grid_spec=pltpu.PrefetchScalarGridSpec(
            num_scalar_prefetch=1, grid=(1,), in_specs=[hbm] * (6 + len(before)), out_specs=[hbm] * 4,
            scratch_shapes=[pltpu.VMEM((STREAM_BUFS, tm, c), F32)] * 3 + [pltpu.VMEM((STREAM_BUFS, tm, c), BF16)] * 2
            + [pltpu.VMEM((STREAM_BUFS, 3, tm, c), BF16), pltpu.VMEM((2, 4, tm, c), F32),
               pltpu.SemaphoreType.DMA((STREAM_BUFS, 6)), pltpu.SemaphoreType.DMA((2, 4))]),
        out_shape=[jax.ShapeDtypeStruct((r, c), F32)] * 4,
        input_output_aliases={7 + k: k for k in range(len(before))},
        compiler_params=_cp(("arbitrary",)),
    )(place, w, m, v, g, got_sib, got_chips, *before)


def _small_update(stats_all, ws, ms, vs):
    def body(st_ref, w_ref, m_ref, v_ref, go_ref, d_ref, mo_ref, vo_ref):
        grad = st_ref[0]
        for k in range(1, NDEV):
            grad = grad + st_ref[k]
        delta, mn, vn = _adamw(w_ref[...], grad, m_ref[...], v_ref[...])
        go_ref[...] = grad
        d_ref[...] = delta
        mo_ref[...] = mn
        vo_ref[...] = vn

    return pl.pallas_call(
        body, name="small_update",
        out_shape=[jax.ShapeDtypeStruct((8, D), F32)] * 4,
        compiler_params=_cp(),
    )(stats_all, ws, ms, vs)


def kernel(x, norm_mix_w, w_in, w_out, norm_ffn_w, w_gate, w_up, w_down, norm_final_w, loss_target, m_norm_mix_w, m_w_in, m_w_out, m_norm_ffn_w, m_w_gate, m_w_up, m_w_down, m_norm_final_w, v_norm_mix_w, v_w_in, v_w_out, v_norm_ffn_w, v_w_gate, v_w_up, v_w_down, v_norm_final_w):
    tr = {"w_gate", "w_up"}
    names = ["w_in", "w_out", "w_gate", "w_up", "w_down"]

    def view(a, n):
        return a[0].T if n in tr else a[0]

    big_w = [view(a, n) for a, n in zip([w_in, w_out, w_gate, w_up, w_down], names)]
    big_m = [view(a, n) for a, n in zip([m_w_in, m_w_out, m_w_gate, m_w_up, m_w_down], names)]
    big_v = [view(a, n) for a, n in zip([v_w_in, v_w_out, v_w_gate, v_w_up, v_w_down], names)]

    shards = [None] + [_cast_bf16(w, "cast_" + n) for w, n in zip(big_w[1:], names[1:])]
    win = [_all_gather([cols], f"all_gather_w_in_{k}", 1 + k)[0]
           for k, cols in enumerate(_cast_cols(big_w[0], "cast_w_in"))]
    (wout,) = _all_gather(shards[1:2], "all_gather_w_out", 3)
    (wgu_a,) = _all_gather(shards[2:4], "all_gather_gate_up_0", 4, per=FF_PER, rows=(0, FF_ROWS))
    (wgu_b,) = _all_gather(shards[2:4], "all_gather_gate_up_1", 5, per=FF_PER, rows=(FF_ROWS, FF_ROWS))
    (wd_a,) = _all_gather(shards[4:5], "all_gather_w_down_0", 6, per=FF_PER, rows=(0, FF_ROWS))
    (wd_b,) = _all_gather(shards[4:5], "all_gather_w_down_1", 7, per=FF_PER, rows=(FF_ROWS, FF_ROWS))
    nw3 = norm_final_w.reshape(1, D)
    place = jnp.stack([lax.axis_index("x"), lax.axis_index("y"), lax.axis_index("c")]).astype(jnp.int32)
    state = {n: (w, m, v, 0, 1) for n, w, m, v in zip(names, big_w, big_m, big_v)}
    for part in range(W_IN_PARTS):
        state[f"w_in_{part}"] = state["w_in"][:3] + (part, W_IN_PARTS)
    red = _Reduction(place, 8, state)
    stats, gx, *_ = _local_step(
        x[0], loss_target[0], norm_mix_w, norm_ffn_w, nw3, win, wout.reshape(D, D),
        wgu_a.reshape(NFG // 2, 2 * N_FG, D), wgu_b.reshape(NFG // 2, 2 * N_FG, D),
        wd_a.reshape(NFG // 2, N_FG, D), wd_b.reshape(NFG // 2, N_FG, D), red)
    stats_all = _exchange_stats(stats, red.next_id())
    upd = [red.update(f"w_in_{W_IN_PARTS - 1}" if n == "w_in" else n) for n in names]
    stats_all = lax.optimization_barrier((stats_all, tuple(upd[0])))[0]

    def rows(a, b, c):
        return jnp.concatenate([a.reshape(1, D), b.reshape(1, D), c.reshape(1, D), jnp.zeros((5, D), F32)], axis=0)

    sg, sd, sm, sv = _small_update(stats_all, rows(norm_mix_w, norm_ffn_w, norm_final_w),
                                   rows(m_norm_mix_w, m_norm_ffn_w, m_norm_final_w),
                                   rows(v_norm_mix_w, v_norm_ffn_w, v_norm_final_w))
    loss = sg[3, 0]

    def outs(k, small):
        big = [(u[k].T if n in tr else u[k])[None] for u, n in zip(upd, names)]
        return [small[0:1], big[0], big[1], small[1:2], big[2], big[3], big[4], small[2]]

    return (loss, gx[None], *outs(0, sg), *outs(1, sd), *outs(2, sm), *outs(3, sv))
```

```python
import math

import numpy as np
import jax
import jax.numpy as jnp
from jax import lax
from jax.experimental import pallas as pl
from jax.experimental.pallas import tpu as pltpu
from jax.experimental.pallas import tpu_sc as plsc

F32 = jnp.float32
BF16 = jnp.bfloat16

S = 2048
D = 2048
NDEV = 8
N_IN = 7168 // NDEV
N_FF = 5632 // NDEV
NFG, N_FG = NDEV // 2, 2 * N_FF
FF_PER, FF_ROWS = 4, N_FF // 2
FFN_CHUNK = 512
IN_ROUNDS = ((0, 512), (512, N_IN - 512))
N_OUT = 2048 // NDEV
AH, AHD = 8, 128
RH, RHD = 4, 256
CH = 128
NB = S // CH
EPS = 1e-6
PATTERNS = ((1, 16), (4, 4), (16, 1))
NEG = -1e30
VMEM_LIMIT = 56 * 1024 * 1024

ADAM_LR, ADAM_B1, ADAM_B2, ADAM_EPS, ADAM_WD, ADAM_STEP = 0.001, 0.9, 0.999, 1e-08, 0.01, 10
MESH = pl.DeviceIdType.MESH


def _cp(sem=None):
    return pltpu.CompilerParams(dimension_semantics=sem, vmem_limit_bytes=VMEM_LIMIT)


def _dot(a, b):
    return jnp.dot(a, b, preferred_element_type=F32)


def _dot_nt(a, b):
    return lax.dot_general(a, b, (((1,), (1,)), ((), ())), preferred_element_type=F32)


def _dot_tn(a, b):
    return lax.dot_general(a, b, (((0,), (0,)), ((), ())), preferred_element_type=F32)


def _sigmoid(x):
    return 0.5 * jnp.tanh(0.5 * x) + 0.5


def _cast_bf16(w, name):
    r, c = w.shape
    tm = r if r <= 1024 else 512

    def body(w_ref, o_ref):
        o_ref[...] = w_ref[...].astype(BF16)

    return pl.pallas_call(
        body, name=name, grid=(r // tm,),
        in_specs=[pl.BlockSpec((tm, c), lambda i: (i, 0))],
        out_specs=pl.BlockSpec((tm, c), lambda i: (i, 0)),
        out_shape=jax.ShapeDtypeStruct((r, c), BF16),
        compiler_params=_cp(("parallel",)),
    )(w)


def _rms_fwd(x, nw):
    tm = 256

    def body(x_ref, w_ref, h_ref, r_ref):
        xs = x_ref[...]
        r = lax.rsqrt(jnp.mean(xs * xs, axis=-1, keepdims=True) + EPS)
        h_ref[...] = ((xs * r) * w_ref[...]).astype(BF16)
        r_ref[...] = r

    return pl.pallas_call(
        body, name="rms_fwd", grid=(S // tm,),
        in_specs=[pl.BlockSpec((tm, D), lambda i: (i, 0)), pl.BlockSpec((1, D), lambda i: (0, 0))],
        out_specs=[pl.BlockSpec((tm, D), lambda i: (i, 0)), pl.BlockSpec((tm, 1), lambda i: (i, 0))],
        out_shape=[jax.ShapeDtypeStruct((S, D), BF16), jax.ShapeDtypeStruct((S, 1), F32)],
        compiler_params=_cp(("parallel",)),
    )(x, nw)


def _row_copies(hbm_refs, bufs, sems, m, tm):
    rows = pl.ds(pl.multiple_of(m * tm, tm), tm)
    return [pltpu.make_async_copy(h.at[rows], b, sems.at[i]) for i, (h, b) in enumerate(zip(hbm_refs, bufs))]


def _rms_bwd_tile(dh, xs, r, nw):
    dnw = jnp.sum(dh * (xs * r), axis=0, keepdims=True)
    gy = dh * nw
    dx = r * gy - xs * ((r * r * r) * jnp.mean(gy * xs, axis=-1, keepdims=True))
    return dx, dnw


def _cast_cols(w, name):
    r, c = w.shape
    tm = 512

    def body(w_ref, *o_refs):
        for o_ref, (off, width) in zip(o_refs, IN_ROUNDS):
            o_ref[...] = w_ref[:, off:off + width].astype(BF16)

    return pl.pallas_call(
        body, name=name, grid=(r // tm,),
        in_specs=[pl.BlockSpec((tm, c), lambda i: (i, 0))],
        out_specs=[pl.BlockSpec((tm, width), lambda i: (i, 0)) for _, width in IN_ROUNDS],
        out_shape=[jax.ShapeDtypeStruct((r, width), BF16) for _, width in IN_ROUNDS],
        compiler_params=_cp(("parallel",)),
    )(w)


def _proj_round(h1, win, k, before):
    tm = 1024
    nm = S // tm
    off, width = IN_ROUNDS[k]
    before = [] if before is None else [before]

    def body(a_ref, w_ref, *rest):
        o_hbm, o_buf, sems = rest[-3:]
        p, m = pl.program_id(0), pl.program_id(1)
        t = p * nm + m

        def out_copy(pp, mm, slot):
            cols = pl.ds(pl.multiple_of(pp * N_IN + off, 128), width)
            return pltpu.make_async_copy(o_buf.at[slot], o_hbm.at[pl.ds(pl.multiple_of(mm * tm, tm), tm), cols],
                                         sems.at[slot])

        @pl.when(t >= 2)
        def _():
            out_copy(p, m, t % 2).wait()

        o_buf[t % 2] = _dot(a_ref[...], w_ref[...])
        out_copy(p, m, t % 2).start()

        @pl.when(t == NDEV * nm - 1)
        def _():
            out_copy(p, m, (t + 1) % 2).wait()
            out_copy(p, m, t % 2).wait()

    return pl.pallas_call(
        body, name=f"proj_{k}", grid=(NDEV, nm),
        in_specs=[pl.BlockSpec((tm, D), lambda p, m: (m, 0)),
                  pl.BlockSpec((None, D, width), lambda p, m: (p, 0, 0))]
        + [pl.BlockSpec(memory_space=pl.ANY)] * len(before),
        out_specs=pl.BlockSpec(memory_space=pl.ANY),
        out_shape=jax.ShapeDtypeStruct((S, NDEV * N_IN), F32),
        scratch_shapes=[pltpu.VMEM((2, tm, width), F32), pltpu.SemaphoreType.DMA((2,))],
        input_output_aliases={2: 0} if before else {},
        compiler_params=_cp(("arbitrary", "arbitrary")),
    )(h1, win, *before)


def _proj(h1, wins):
    out = None
    for k, win in enumerate(wins):
        out = _proj_round(h1, win, k, out)
    return out


def _out_proj_rms(x, ma, mr, wout, nw):
    tm = 256
    half = D // 2

    def body(x_ref, ma_ref, mr_ref, w_ref, nw_ref, x2_ref, h_ref, r_ref):
        acc = _dot(ma_ref[...], w_ref[0:half, :]) + _dot(mr_ref[...], w_ref[half:D, :])
        x2 = x_ref[...] + acc
        r = lax.rsqrt(jnp.mean(x2 * x2, axis=-1, keepdims=True) + EPS)
        x2_ref[...] = x2
        h_ref[...] = ((x2 * r) * nw_ref[...]).astype(BF16)
        r_ref[...] = r

    return pl.pallas_call(
        body, name="out_proj_rms", grid=(S // tm,),
        in_specs=[pl.BlockSpec((tm, D), lambda i: (i, 0)),
                  pl.BlockSpec((tm, half), lambda i: (i, 0)),
                  pl.BlockSpec((tm, half), lambda i: (i, 0)),
                  pl.BlockSpec((D, D), lambda i: (0, 0)),
                  pl.BlockSpec((1, D), lambda i: (0, 0))],
        out_specs=[pl.BlockSpec((tm, D), lambda i: (i, 0)), pl.BlockSpec((tm, D), lambda i: (i, 0)),
                   pl.BlockSpec((tm, 1), lambda i: (i, 0))],
        out_shape=[jax.ShapeDtypeStruct((S, D), F32), jax.ShapeDtypeStruct((S, D), BF16),
                   jax.ShapeDtypeStruct((S, 1), F32)],
        compiler_params=_cp(("parallel",)),
    )(x, ma, mr, wout, nw)


def _ffn_up(h2, wgu, part, before=None):
    tm = 512

    def body(h_ref, w_ref, *rest):
        a_ref, dadg_ref, dadu_ref = rest[-3:]
        gu = _dot_nt(h_ref[...], w_ref[...])
        g, u = gu[:, 0:N_FG], gu[:, N_FG:2 * N_FG]
        sg = _sigmoid(g)
        silu = g * sg
        a_ref[...] = (silu * u).astype(BF16)
        dadg_ref[...] = (u * (sg * (1.0 + g * (1.0 - sg)))).astype(BF16)
        dadu_ref[...] = silu.astype(BF16)

    half = NFG // 2
    first = part * half
    before = list(before or [])
    blk = pl.BlockSpec((None, tm, N_FG), lambda p, m: (p + first, m, 0))
    return pl.pallas_call(
        body, name=f"ffn_up_{part}", grid=(half, S // tm),
        in_specs=[pl.BlockSpec((tm, D), lambda p, m: (m, 0)),
                  pl.BlockSpec((None, 2 * N_FG, D), lambda p, m: (p, 0, 0))]
        + [pl.BlockSpec(memory_space=pl.ANY)] * len(before),
        out_specs=[blk, blk, blk],
        out_shape=[jax.ShapeDtypeStruct((NFG, S, N_FG), BF16)] * 3,
        input_output_aliases={2 + k: k for k in range(len(before))},
        compiler_params=_cp(("parallel", "parallel")),
    )(h2, wgu, *before)


def _ffn_down_first(x2, a, wd):
    tm = 512
    n = wd.shape[0]

    def body(x_ref, a_ref, w_ref, o_ref):
        p = pl.program_id(1)

        @pl.when(p == 0)
        def _():
            o_ref[...] = x_ref[...] + _dot(a_ref[...], w_ref[0])

        @pl.when(p > 0)
        def _():
            o_ref[...] += _dot(a_ref[...], w_ref[p])

    return pl.pallas_call(
        body, name="ffn_down_first", grid=(S // tm, n),
        in_specs=[pl.BlockSpec((tm, D), lambda m, p: (m, 0)),
                  pl.BlockSpec((None, tm, N_FG), lambda m, p: (p, m, 0)),
                  pl.BlockSpec((n, N_FG, D), lambda m, p: (0, 0, 0))],
        out_specs=pl.BlockSpec((tm, D), lambda m, p: (m, 0)),
        out_shape=jax.ShapeDtypeStruct((S, D), F32),
        compiler_params=_cp(("parallel", "arbitrary")),
    )(x2, a, wd)


def _ffn_down_loss(x2, a, wd, nw, tgt):
    tm = 512
    first = NFG - wd.shape[0]

    def body(x2_hbm, a_ref, w_ref, nw_ref, t_hbm, dx_ref, dxb_ref, st_ref, acc_ref, x2_buf, t_buf, sems):
        m, p = pl.program_id(0), pl.program_id(1)
        tail_in = _row_copies((x2_hbm, t_hbm), (x2_buf, t_buf), sems, m, tm)

        @pl.when(p == 0)
        def _():
            acc_ref[...] = jnp.zeros_like(acc_ref)
            for cp in tail_in:
                cp.start()

        @pl.when((p == 0) & (m == 0))
        def _():
            st_ref[...] = jnp.zeros_like(st_ref)

        acc_ref[...] += _dot(a_ref[...], w_ref[p])

        @pl.when(p == NFG - first - 1)
        def _():
            for cp in tail_in:
                cp.wait()
            x3 = x2_buf[...] + acc_ref[...]
            nwv = nw_ref[...]
            r = lax.rsqrt(jnp.mean(x3 * x3, axis=-1, keepdims=True) + EPS)
            y = (x3 * r) * nwv
            err = y - t_buf[...]
            loss = 0.5 * jnp.sum(jnp.mean(err * err, axis=-1, keepdims=True), axis=0, keepdims=True)
            dy = err * (1.0 / D)
            dx, dnw = _rms_bwd_tile(dy, x3, r, nwv)
            dx_ref[...] = dx
            dxb_ref[...] = dx.astype(BF16)
            st_ref[0:1, :] += dnw
            st_ref[1:2, :] += jnp.broadcast_to(loss, (1, D))

    return pl.pallas_call(
        body, name="ffn_down_loss", grid=(S // tm, NFG - first),
        in_specs=[pl.BlockSpec(memory_space=pl.ANY),
                  pl.BlockSpec((None, tm, N_FG), lambda m, p: (p + first, m, 0)),
                  pl.BlockSpec((NFG - first, N_FG, D), lambda m, p: (0, 0, 0)),
                  pl.BlockSpec((1, D), lambda m, p: (0, 0)),
                  pl.BlockSpec(memory_space=pl.ANY)],
        out_specs=[pl.BlockSpec((tm, D), lambda m, p: (m, 0)), pl.BlockSpec((tm, D), lambda m, p: (m, 0)),
                   pl.BlockSpec((8, D), lambda m, p: (0, 0))],
        out_shape=[jax.ShapeDtypeStruct((S, D), F32), jax.ShapeDtypeStruct((S, D), BF16),
                   jax.ShapeDtypeStruct((8, D), F32)],
        scratch_shapes=[pltpu.VMEM((tm, D), F32), pltpu.VMEM((tm, D), F32), pltpu.VMEM((tm, D), F32),
                        pltpu.SemaphoreType.DMA((2,))],
        compiler_params=_cp(("arbitrary", "arbitrary")),
    )(x2, a, wd, nw, tgt)


def _ffn_down_bwd(dx3b, wd, dadg, dadu, part, before=None):
    tm = 1024
    half = NFG // 2

    def body(dx_ref, w_ref, dadg_ref, dadu_ref, *rest):
        dgu_ref = rest[-1]
        rows = pl.ds(pl.multiple_of(pl.program_id(1) * tm, tm), tm)
        dx = dx_ref[rows, :]
        for lo in range(0, N_FG, FFN_CHUNK):
            n = min(FFN_CHUNK, N_FG - lo)
            da = _dot_nt(dx, w_ref[lo:lo + n, :])
            dgu_ref[:, lo:lo + n] = (da * dadg_ref[:, lo:lo + n].astype(F32)).astype(BF16)
            dgu_ref[:, N_FG + lo:N_FG + lo + n] = (da * dadu_ref[:, lo:lo + n].astype(F32)).astype(BF16)

    blk = pl.BlockSpec((None, tm, N_FG), lambda p, m: (p + part * half, m, 0))
    before = list(before or [])
    return pl.pallas_call(
        body, name=f"ffn_down_bwd_{part}", grid=(half, S // tm),
        in_specs=[pl.BlockSpec((S, D), lambda p, m: (0, 0)),
                  pl.BlockSpec((None, N_FG, D), lambda p, m: (p, 0, 0)), blk, blk]
        + [pl.BlockSpec(memory_space=pl.ANY)] * len(before),
        out_specs=pl.BlockSpec((None, tm, 2 * N_FG), lambda p, m: (p + part * half, m, 0)),
        out_shape=jax.ShapeDtypeStruct((NFG, S, 2 * N_FG), BF16),
        input_output_aliases={4 + k: k for k in range(len(before))},
        compiler_params=_cp(("parallel", "parallel")),
    )(dx3b, wd, dadg, dadu, *before)


def _ffn_up_bwd(dgu, wgu_a, wgu_b, dres, xs, r, nw):
    tm = 512
    nm = S // tm
    na = wgu_a.shape[0]

    def body(dgu_ref, wa_hbm, wb_hbm, dres_hbm, x_hbm, r_ref, nw_ref, dx_ref, dxb_ref, st_ref,
             w_buf, dres_buf, x_buf, sems, w_sems):
        m, p = pl.program_id(0), pl.program_id(1)
        tail_in = _row_copies((dres_hbm, x_hbm), (dres_buf, x_buf), sems, m, tm)

        def fetch(g, slot):
            for src, lo in ((wa_hbm, 0), (wb_hbm, na)):
                @pl.when((g >= lo) & (g < lo + na))
                def _():
                    pltpu.make_async_copy(src.at[g - lo], w_buf.at[slot], w_sems.at[slot]).start()

        @pl.when((p == 0) & (m == 0))
        def _():
            st_ref[...] = jnp.zeros_like(st_ref)
            fetch(p, 0)

        @pl.when((p < NFG - 1) | (m < nm - 1))
        def _():
            fetch((p + 1) % NFG, (p + 1) % 2)

        @pl.when(p == 0)
        def _():
            dx_ref[...] = jnp.zeros_like(dx_ref)
            for cp in tail_in:
                cp.start()

        slot = p % 2
        pltpu.make_async_copy(wa_hbm.at[0], w_buf.at[slot], w_sems.at[slot]).wait()
        dx_ref[...] += _dot(dgu_ref[...], w_buf[slot])

        @pl.when(p == NFG - 1)
        def _():
            for cp in tail_in:
                cp.wait()
            dx, dnw = _rms_bwd_tile(dx_ref[...], x_buf[...], r_ref[...], nw_ref[...])
            dx = dres_buf[...] + dx
            dx_ref[...] = dx
            dxb_ref[...] = dx.astype(BF16)
            st_ref[0:1, :] += dnw

    blk = pl.BlockSpec((None, tm, 2 * N_FG), lambda m, p: (p, m, 0))
    row = pl.BlockSpec((tm, D), lambda m, p: (m, 0))
    hbm = pl.BlockSpec(memory_space=pl.ANY)
    return pl.pallas_call(
        body, name="ffn_up_bwd", grid=(nm, NFG),
        in_specs=[blk, hbm, hbm, hbm, hbm, pl.BlockSpec((tm, 1), lambda m, p: (m, 0)),
                  pl.BlockSpec((1, D), lambda m, p: (0, 0))],
        out_specs=[row, row, pl.BlockSpec((8, D), lambda m, p: (0, 0))],
        out_shape=[jax.ShapeDtypeStruct((S, D), F32), jax.ShapeDtypeStruct((S, D), BF16),
                   jax.ShapeDtypeStruct((8, D), F32)],
        scratch_shapes=[pltpu.VMEM((2, 2 * N_FG, D), BF16), pltpu.VMEM((tm, D), F32), pltpu.VMEM((tm, D), F32),
                        pltpu.SemaphoreType.DMA((2,)), pltpu.SemaphoreType.DMA((2,))],
        compiler_params=_cp(("arbitrary", "arbitrary")),
    )(dgu, wgu_a, wgu_b, dres, xs, r, nw)


def _out_proj_bwd(dx2b, wout, place=None, rider=None):
    tm = 256

    if rider is None:
        def body(dx_ref, w_ref, o_ref):
            o_ref[...] = _dot_nt(dx_ref[...], w_ref[...])

        return pl.pallas_call(
            body, name="out_proj_bwd", grid=(S // tm,),
            in_specs=[pl.BlockSpec((tm, D), lambda i: (i, 0)), pl.BlockSpec((D, D), lambda i: (0, 0))],
            out_specs=pl.BlockSpec((tm, D), lambda i: (i, 0)),
            out_shape=jax.ShapeDtypeStruct((S, D), F32),
            compiler_params=_cp(("parallel",)),
        )(dx2b, wout), None

    w = rider[0]
    r, c = w.shape
    rt = _row_tile(r, c)
    nt = r // rt
    assert nt <= S // tm

    def body(pos_ref, dx_ref, w_ref, uw, um, uv, ug, us, uc, o_ref, go, dd, mo, vo):
        o_ref[...] = _dot_nt(dx_ref[...], w_ref[...])

        @pl.when(pl.program_id(0) < nt)
        def _():
            _update_tile(uw, um, uv, ug, us, uc, go, dd, mo, vo)

    def at(i):
        return jnp.minimum(i, nt - 1)

    tile = pl.BlockSpec((rt, c), lambda i, pos: (at(i), 0))
    outs = pl.pallas_call(
        body, name="out_proj_bwd",
        grid_spec=pltpu.PrefetchScalarGridSpec(
            num_scalar_prefetch=1, grid=(S // tm,),
            in_specs=[pl.BlockSpec((tm, D), lambda i, pos: (i, 0)), pl.BlockSpec((D, D), lambda i, pos: (0, 0)),
                      tile, tile, tile,
                      pl.BlockSpec((None, rt, c), lambda i, pos: (4 * pos[0] + 2 * pos[1] + pos[2], at(i), 0)),
                      pl.BlockSpec((None, rt, c), lambda i, pos: (2 * pos[0] + pos[1], at(i), 0)),
                      pl.BlockSpec((3, rt, c), lambda i, pos: (0, at(i), 0))],
            out_specs=[pl.BlockSpec((tm, D), lambda i, pos: (i, 0)), tile, tile, tile, tile]),
        out_shape=[jax.ShapeDtypeStruct((S, D), F32)] + [jax.ShapeDtypeStruct((r, c), F32)] * 4,
        compiler_params=_cp(("arbitrary",)),
    )(place, dx2b, wout, *rider)
    return outs[0], outs[1:]


def _in_proj_bwd(dproj, wins, dres, xs, r, nw):
    tm = 1024

    nr = len(wins)
    nm = S // tm

    def body(dp_ref, *rest):
        w_hbms = rest[:nr]
        dres_hbm, x_hbm, r_ref, nw_ref, dx_ref, st_ref, w_buf, dres_buf, x_buf, sems, w_sems = rest[nr:]
        m, p = pl.program_id(0), pl.program_id(1)
        tail_in = _row_copies((dres_hbm, x_hbm), (dres_buf, x_buf), sems, m, tm)

        def w_copies(g, slot):
            return [pltpu.make_async_copy(w_hbm.at[g], w_buf.at[slot, pl.ds(0, D), pl.ds(off, width)],
                                          w_sems.at[slot, k])
                    for k, (w_hbm, (off, width)) in enumerate(zip(w_hbms, IN_ROUNDS))]

        @pl.when((p == 0) & (m == 0))
        def _():
            st_ref[...] = jnp.zeros_like(st_ref)
            for cp in w_copies(p, 0):
                cp.start()

        @pl.when((p < NDEV - 1) | (m < nm - 1))
        def _():
            for cp in w_copies((p + 1) % NDEV, (p + 1) % 2):
                cp.start()

        @pl.when(p == 0)
        def _():
            dx_ref[...] = jnp.zeros_like(dx_ref)
            for cp in tail_in:
                cp.start()

        for cp in w_copies(p, p % 2):
            cp.wait()
        dx_ref[...] += _dot_nt(dp_ref[...], w_buf[p % 2])

        @pl.when(p == NDEV - 1)
        def _():
            for cp in tail_in:
                cp.wait()
            dx, dnw = _rms_bwd_tile(dx_ref[...], x_buf[...], r_ref[...], nw_ref[...])
            dx_ref[...] = dres_buf[...] + dx
            st_ref[0:1, :] += dnw

    row = pl.BlockSpec((tm, D), lambda m, p: (m, 0))
    hbm = pl.BlockSpec(memory_space=pl.ANY)
    return pl.pallas_call(
        body, name="in_proj_bwd", grid=(S // tm, NDEV),
        in_specs=[pl.BlockSpec((tm, N_IN), lambda m, p: (m, p)),
                  *[hbm] * nr,
                  hbm, hbm, pl.BlockSpec((tm, 1), lambda m, p: (m, 0)),
                  pl.BlockSpec((1, D), lambda m, p: (0, 0))],
        out_specs=[row, pl.BlockSpec((8, D), lambda m, p: (0, 0))],
        out_shape=[jax.ShapeDtypeStruct((S, D), F32), jax.ShapeDtypeStruct((8, D), F32)],
        scratch_shapes=[pltpu.VMEM((2, D, N_IN), BF16), pltpu.VMEM((tm, D), F32), pltpu.VMEM((tm, D), F32),
                        pltpu.SemaphoreType.DMA((2,)), pltpu.SemaphoreType.DMA((2, nr))],
        compiler_params=_cp(("arbitrary", "arbitrary")),
    )(dproj, *wins, dres, xs, r, nw)


W_IN_PARTS = 2


def _wgrad_in(h1, dproj, part):
    rows = D // W_IN_PARTS

    def body(a_ref, d_ref, o_ref):
        both = _dot_tn(a_ref[...], d_ref[...]).astype(BF16)
        o_ref[0] = both[:, 0:N_IN]
        o_ref[1] = both[:, N_IN:2 * N_IN]

    return pl.pallas_call(
        body, name=f"wgrad_in_{part}", grid=(NDEV // 2,),
        in_specs=[pl.BlockSpec((S, rows), lambda p: (0, part)), pl.BlockSpec((S, 2 * N_IN), lambda p: (0, p))],
        out_specs=pl.BlockSpec((2, rows, N_IN), lambda p: (p, 0, 0)),
        out_shape=jax.ShapeDtypeStruct((NDEV, rows, N_IN), BF16),
        compiler_params=_cp(("parallel",)),
    )(h1, dproj)


def _wgrad_rows(a3, dy, name, col=0):
    def body(a_ref, d_ref, o_ref):
        dw = _dot_tn(a_ref[...], d_ref[...]).astype(BF16)
        for j in range(FF_PER):
            o_ref[j] = dw[j * FF_ROWS:(j + 1) * FF_ROWS]

    return pl.pallas_call(
        body, name=name, grid=(NFG,),
        in_specs=[pl.BlockSpec((None, S, N_FG), lambda p: (p, 0, col)), pl.BlockSpec((S, D), lambda p: (0, 0))],
        out_specs=pl.BlockSpec((FF_PER, FF_ROWS, D), lambda p: (p % 2, p // 2, 0)),
        out_shape=jax.ShapeDtypeStruct((NDEV, N_FF, D), BF16),
        compiler_params=_cp(("parallel",)),
    )(a3, dy)


def _wgrad_out(ma, mr, dx2b):
    half = D // 2
    per = half // N_OUT

    def body(ma_ref, mr_ref, d_ref, o_ref):
        p = pl.program_id(0)

        @pl.when(p == 0)
        def _():
            o_ref[...] = _dot_tn(ma_ref[...], d_ref[...]).astype(BF16).reshape(per, N_OUT, D)

        @pl.when(p == 1)
        def _():
            o_ref[...] = _dot_tn(mr_ref[...], d_ref[...]).astype(BF16).reshape(per, N_OUT, D)

    whole = pl.BlockSpec((S, half), lambda p: (0, 0))
    return pl.pallas_call(
        body, name="wgrad_out", grid=(2,),
        in_specs=[whole, whole, pl.BlockSpec((S, D), lambda p: (0, 0))],
        out_specs=pl.BlockSpec((per, N_OUT, D), lambda p: (p, 0, 0)),
        out_shape=jax.ShapeDtypeStruct((NDEV, N_OUT, D), BF16),
        compiler_params=_cp(("parallel",)),
    )(ma, mr, dx2b)


def _attn_consts():
    c = np.zeros((AH, 8, AHD), np.float32)
    for h in range(AH):
        c[h, :, :] = 2.0 ** (-(h + 1))
    return jnp.asarray(c)


def _permute_in(dst, src, d, cast=None):
    v = src[...]
    if d > 1:
        v = pltpu.einshape("jrc->rjc", v.reshape(S // d, d, AHD)).reshape(S, AHD)
    dst[...] = v if cast is None else v.astype(cast)


def _natural_order(v, d):
    if d == 1:
        return v
    return pltpu.einshape("rjc->jrc", v.reshape(d, S // d, AHD)).reshape(S, AHD)


def _attn_masks():
    qi = lax.broadcasted_iota(jnp.int32, (CH, CH), 0)
    kj = lax.broadcasted_iota(jnp.int32, (CH, CH), 1)
    dist_c = (qi - kj).astype(F32)
    dist_p = (qi - kj + CH).astype(F32)
    return (qi >= kj)[None], (kj >= qi)[None], dist_c[None], dist_p[None]


GB = 16


def _bdot_nt(a, b):
    return lax.dot_general(a, b, (((2,), (2,)), ((0,), (0,))), preferred_element_type=F32)


def _bdot(a, b):
    return lax.dot_general(a, b, (((2,), (1,)), ((0,), (0,))), preferred_element_type=F32)


def _bdot_tn(a, b):
    return lax.dot_general(a, b, (((1,), (1,)), ((0,), (0,))), preferred_element_type=F32)


def _shift_block(dst, src):
    dst[0:CH, :] = jnp.zeros((CH, AHD), dst.dtype)
    dst[CH:S, :] = src[0:S - CH, :]


def _has_prev(g, nb):
    blk = lax.broadcasted_iota(jnp.int32, (GB, 1, 1), 0) + g * GB
    return (blk & (nb - 1)) != 0


def _blocks(ref, g):
    return ref[g * GB * CH:(g + 1) * GB * CH, :].reshape(GB, CH, AHD)


def _attn_fwd(proj):
    scale = 1.0 / math.sqrt(AHD)

    def body(c_ref, q_ref, k_ref, v_ref, o_ref, ob_ref, lse_ref, qkvp_ref, lsep_ref, qd, kd, vd, kps, vps, od, ld, *nat):
        onat, lnat = nat[0:3], nat[3:6]
        slope = c_ref[0:1, :]
        mask_c, mask_p, dist_c, dist_p = _attn_masks()
        for pi, (d, nb) in enumerate(PATTERNS):
            _permute_in(qd, q_ref, d, BF16)
            _permute_in(kd, k_ref, d, BF16)
            _permute_in(vd, v_ref, d, BF16)
            if d > 1:
                qkvp_ref[pi - 1, 0] = qd[...]
                qkvp_ref[pi - 1, 1] = kd[...]
                qkvp_ref[pi - 1, 2] = vd[...]
            if nb > 1:
                _shift_block(kps, kd)
                _shift_block(vps, vd)
            bias_c = -(slope * float(d)) * dist_c
            bias_p = -(slope * float(d)) * dist_p
            for g in range(NB // GB):
                q3, k3, v3 = _blocks(qd, g), _blocks(kd, g), _blocks(vd, g)
                s_c = jnp.where(mask_c, _bdot_nt(q3, k3) * scale + bias_c, NEG)
                mx = jnp.max(s_c, axis=-1, keepdims=True)
                if nb > 1:
                    kp3, vp3 = _blocks(kps, g), _blocks(vps, g)
                    s_p = jnp.where(jnp.logical_and(mask_p, _has_prev(g, nb)),
                                    _bdot_nt(q3, kp3) * scale + bias_p, NEG)
                    mx = jnp.maximum(mx, jnp.max(s_p, axis=-1, keepdims=True))
                    l = (jnp.sum(jnp.exp(s_c - mx), axis=-1, keepdims=True)
                         + jnp.sum(jnp.exp(s_p - mx), axis=-1, keepdims=True))
                    lse = mx + jnp.log(l)
                    o3 = _bdot(jnp.exp(s_c - lse).astype(BF16), v3) + _bdot(jnp.exp(s_p - lse).astype(BF16), vp3)
                else:
                    l = jnp.sum(jnp.exp(s_c - mx), axis=-1, keepdims=True)
                    lse = mx + jnp.log(l)
                    o3 = _bdot(jnp.exp(s_c - lse).astype(BF16), v3)
                rows = slice(g * GB * CH, (g + 1) * GB * CH)
                od[rows, :] = o3.reshape(GB * CH, AHD)
                ld[rows, :] = jnp.broadcast_to(lse, (GB, CH, AHD)).reshape(GB * CH, AHD)
            onat[pi][...] = _natural_order(od[...], d)
            lnat[pi][...] = _natural_order(ld[...], d)
        l0, l1, l2 = lnat[0][...], lnat[1][...], lnat[2][...]
        mx = jnp.maximum(jnp.maximum(l0, l1), l2)
        e0, e1, e2 = jnp.exp(l0 - mx), jnp.exp(l1 - mx), jnp.exp(l2 - mx)
        den = e0 + e1 + e2
        out = (e0 / den) * onat[0][...] + (e1 / den) * onat[1][...] + (e2 / den) * onat[2][...]
        o_ref[...] = out
        ob_ref[...] = out.astype(BF16)
        lse_ref[...] = mx + jnp.log(den)
        for pi, (d, _) in enumerate(PATTERNS[1:]):
            _permute_in(lsep_ref.at[pi], lse_ref, d)

    def col(off):
        return pl.BlockSpec((S, AHD), lambda h: (0, off + h))

    return pl.pallas_call(
        body, name="attn_fwd", grid=(AH,),
        in_specs=[pl.BlockSpec((None, 8, AHD), lambda h: (h, 0, 0)), col(0), col(AH), col(2 * AH)],
        out_specs=[col(0), col(0), col(0), pl.BlockSpec((2, 3, S, AHD), lambda h: (0, 0, 0, h)),
                   pl.BlockSpec((2, S, AHD), lambda h: (0, 0, h))],
        out_shape=[jax.ShapeDtypeStruct((S, AH * AHD), F32), jax.ShapeDtypeStruct((S, AH * AHD), BF16),
                   jax.ShapeDtypeStruct((S, AH * AHD), F32),
                   jax.ShapeDtypeStruct((2, 3, S, AH * AHD), BF16), jax.ShapeDtypeStruct((2, S, AH * AHD), F32)],
        scratch_shapes=[pltpu.VMEM((S, AHD), BF16) for _ in range(5)]
        + [pltpu.VMEM((S, AHD), F32) for _ in range(8)],
        compiler_params=_cp(("parallel",)),
    )(_attn_consts(), proj, proj, proj)


def _attn_bwd(proj, dmixed, o, lse, qkvp, lsep):
    scale = 1.0 / math.sqrt(AHD)

    def body(c_ref, q_ref, k_ref, v_ref, do_ref, o_ref, lse_ref, qkvp_ref, lsep_ref, dproj_hbm,
             qd, kd, vd, dod, kps, vps, dld, dqd, dkd, dvd, delta, aq, ak, av, sq, sk, sv, sems):
        h = pl.program_id(0)

        def out_copies(head):
            return [pltpu.make_async_copy(
                st, dproj_hbm.at[:, pl.ds(pl.multiple_of((k * AH + head) * AHD, AHD), AHD)], sems.at[k])
                for k, st in enumerate((sq, sk, sv))]

        slope = c_ref[0:1, :]
        mask_c, mask_p, dist_c, dist_p = _attn_masks()
        delta[...] = jnp.broadcast_to(jnp.sum(do_ref[...] * o_ref[...], axis=-1, keepdims=True), (S, AHD))
        for pi, (d, nb) in enumerate(PATTERNS):
            if d == 1:
                _permute_in(qd, q_ref, d, BF16)
                _permute_in(kd, k_ref, d, BF16)
                _permute_in(vd, v_ref, d, BF16)
                qs, ks, vs, lss = qd, kd, vd, lse_ref
            else:
                qs, ks, vs, lss = (qkvp_ref.at[pi - 1, 0], qkvp_ref.at[pi - 1, 1], qkvp_ref.at[pi - 1, 2],
                                   lsep_ref.at[pi - 1])
            _permute_in(dod, do_ref, d, BF16)
            _permute_in(dld, delta, d)
            if nb > 1:
                _shift_block(kps, ks)
                _shift_block(vps, vs)
            bias_c = -(slope * float(d)) * dist_c
            bias_p = -(slope * float(d)) * dist_p
            for g in range(NB // GB):
                q3, k3, v3, do3 = _blocks(qs, g), _blocks(ks, g), _blocks(vs, g), _blocks(dod, g)
                ls, dl = _blocks(lss, g), _blocks(dld, g)
                lo, hi = g * GB * CH, (g + 1) * GB * CH
                p_c = jnp.exp(jnp.where(mask_c, _bdot_nt(q3, k3) * scale + bias_c, NEG) - ls)
                ds_c = ((p_c * (_bdot_nt(do3, v3) - dl)) * scale).astype(BF16)
                dq3 = _bdot(ds_c, k3)
                dkd[lo:hi, :] = _bdot_tn(ds_c, q3).reshape(GB * CH, AHD)
                dvd[lo:hi, :] = _bdot_tn(p_c.astype(BF16), do3).reshape(GB * CH, AHD)
                if nb > 1:
                    kp3, vp3 = _blocks(kps, g), _blocks(vps, g)
                    p_p = jnp.exp(jnp.where(jnp.logical_and(mask_p, _has_prev(g, nb)),
                                            _bdot_nt(q3, kp3) * scale + bias_p, NEG) - ls)
                    ds_p = ((p_p * (_bdot_nt(do3, vp3) - dl)) * scale).astype(BF16)
                    dq3 = dq3 + _bdot(ds_p, kp3)
                    dkp = _bdot_tn(ds_p, q3).reshape(GB * CH, AHD)
                    dvp = _bdot_tn(p_p.astype(BF16), do3).reshape(GB * CH, AHD)
                    if g == 0:
                        dkd[0:hi - CH, :] += dkp[CH:, :]
                        dvd[0:hi - CH, :] += dvp[CH:, :]
                    else:
                        dkd[lo - CH:hi - CH, :] += dkp
                        dvd[lo - CH:hi - CH, :] += dvp
                dqd[lo:hi, :] = dq3.reshape(GB * CH, AHD)
            ln = S // d
            for acc, src in ((aq, dqd), (ak, dkd), (av, dvd)):
                if pi == 0:
                    acc[...] = src[...]
                else:
                    acc[...] += _natural_order(src[...], d)

        @pl.when(h > 0)
        def _():
            for cp in out_copies(h - 1):
                cp.wait()

        sq[...] = aq[...].astype(BF16)
        sk[...] = ak[...].astype(BF16)
        sv[...] = av[...].astype(BF16)
        for cp in out_copies(h):
            cp.start()

        @pl.when(h == AH - 1)
        def _():
            for cp in out_copies(h):
                cp.wait()

    def col(off):
        return pl.BlockSpec((S, AHD), lambda h: (0, off + h))

    return pl.pallas_call(
        body, name="attn_bwd", grid=(AH,),
        in_specs=[pl.BlockSpec((None, 8, AHD), lambda h: (h, 0, 0)), col(0), col(AH), col(2 * AH),
                  col(0), col(0), col(0), pl.BlockSpec((2, 3, S, AHD), lambda h: (0, 0, 0, h)),
                  pl.BlockSpec((2, S, AHD), lambda h: (0, 0, h))],
        out_specs=pl.BlockSpec(memory_space=pl.ANY),
        out_shape=jax.ShapeDtypeStruct((S, NDEV * N_IN), BF16),
        scratch_shapes=[pltpu.VMEM((S, AHD), BF16) for _ in range(6)]
        + [pltpu.VMEM((S, AHD), F32) for _ in range(8)]
        + [pltpu.VMEM((S, AHD), BF16) for _ in range(3)] + [pltpu.SemaphoreType.DMA((3,))],
        compiler_params=_cp(("arbitrary",)),
    )(_attn_consts(), proj, proj, proj, dmixed, o, lse, qkvp, lsep)


def _ret_consts():
    c = np.zeros((RH, 8, RHD), np.float32)
    for h in range(RH):
        c[h, :, :] = np.log(np.float32(1.0) - np.float32(2.0 ** (-5.0 - h)))
    return jnp.asarray(c)


def _ret_factors(lg):
    i = lax.broadcasted_iota(jnp.int32, (CH, CH), 0)
    j = lax.broadcasted_iota(jnp.int32, (CH, CH), 1)
    dif = (i - j).astype(F32)
    decay = jnp.where(dif >= 0, jnp.exp(lg[:, 0:CH] * jnp.maximum(dif, 0.0)), 0.0)
    row = lax.broadcasted_iota(jnp.int32, (CH, RHD), 0).astype(F32)
    zeta = jnp.exp(lg * (CH - 1.0 - row))
    xi = jnp.exp(lg * (row + 1.0))
    return decay, zeta, xi, jnp.exp(lg * float(CH))


CBK = 8
RSTEPS = NB // CBK


def _ret_specs(rev):
    off = 3 * AH * AHD // RHD
    rows = CBK * CH

    def ch(n):
        return (RSTEPS - 1 - n) if rev else n

    def col(k):
        return pl.BlockSpec((rows, RHD), lambda h, n: (ch(n), off + k * RH + h))

    own = pl.BlockSpec((rows, RHD), lambda h, n: (ch(n), h))
    state = pl.BlockSpec((None, CBK, RHD, RHD), lambda h, n: (h, ch(n), 0, 0))
    const = pl.BlockSpec((None, 8, RHD), lambda h, n: (h, 0, 0))
    dm = pl.BlockSpec((rows, RHD), lambda h, n: (ch(n), AH * AHD // RHD + h))
    return col, own, state, const, dm


def _chunks(x):
    return x.reshape(CBK, CH, RHD)


def _ret_fwd(proj):
    def body(c_ref, q_ref, k_ref, v_ref, g_ref, ret_ref, mr_ref, st_ref, r_acc):
        n = pl.program_id(1)

        @pl.when(n == 0)
        def _():
            r_acc[...] = jnp.zeros_like(r_acc)

        decay, zeta, xi, gch = _ret_factors(c_ref[0:1, :])
        q3 = _chunks(q_ref[...].astype(BF16))
        kc = _chunks(k_ref[...] * (1.0 / math.sqrt(RHD)))
        k3 = kc.astype(BF16)
        v3 = _chunks(v_ref[...].astype(BF16))
        kv3 = _bdot_tn((kc * zeta[None]).astype(BF16), v3)
        r = r_acc[...]
        for i in range(CBK):
            st_ref[i] = r.astype(BF16)
            r = r * gch + kv3[i]
        r_acc[...] = r
        scores = _bdot_nt(q3, k3) * decay[None]
        ret = (_bdot(scores.astype(BF16), v3) + _bdot(q3, st_ref[...]) * xi[None]).reshape(CBK * CH, RHD)
        ret_ref[...] = ret
        rr = lax.rsqrt(jnp.mean(ret * ret, axis=-1, keepdims=True) + EPS)
        gv = g_ref[...]
        mr_ref[...] = ((gv * _sigmoid(gv)) * (ret * rr)).astype(BF16)

    col, own, state, const, _ = _ret_specs(False)
    return pl.pallas_call(
        body, name="ret_fwd", grid=(RH, RSTEPS),
        in_specs=[const, col(0), col(1), col(2), col(3)],
        out_specs=[own, own, state],
        out_shape=[jax.ShapeDtypeStruct((S, RH * RHD), F32), jax.ShapeDtypeStruct((S, RH * RHD), BF16),
                   jax.ShapeDtypeStruct((RH, NB, RHD, RHD), BF16)],
        scratch_shapes=[pltpu.VMEM((RHD, RHD), F32)],
        compiler_params=_cp(("parallel", "arbitrary")),
    )(_ret_consts(), proj, proj, proj, proj)


def _ret_bwd(proj, ret, states, dmixed, dproj):
    rows = CBK * CH
    col0 = 3 * AH * AHD

    def body(c_ref, q_ref, k_ref, v_ref, g_ref, ret_ref, st_ref, dm_ref, dproj_in, dproj_hbm, g_acc, gs,
             sq, sk, sv, sg, sems):
        del dproj_in
        h, n = pl.program_id(0), pl.program_id(1)
        step = h * RSTEPS + n

        def out_copies(t):
            hh, nn = t // RSTEPS, t % RSTEPS
            r0 = pl.multiple_of((RSTEPS - 1 - nn) * rows, rows)
            return [pltpu.make_async_copy(
                st, dproj_hbm.at[pl.ds(r0, rows), pl.ds(pl.multiple_of(col0 + (k * RH + hh) * RHD, RHD), RHD)],
                sems.at[k]) for k, st in enumerate((sq, sk, sv, sg))]

        @pl.when(n == 0)
        def _():
            g_acc[...] = jnp.zeros_like(g_acc)

        decay, zeta, xi, gch = _ret_factors(c_ref[0:1, :])
        ret_v = ret_ref[...]
        rr = lax.rsqrt(jnp.mean(ret_v * ret_v, axis=-1, keepdims=True) + EPS)
        gv = g_ref[...]
        sgm = _sigmoid(gv)
        dmix = dm_ref[...]
        dgate = ((dmix * (ret_v * rr)) * (sgm * (1.0 + gv * (1.0 - sgm)))).astype(BF16)
        dretn = dmix * (gv * sgm)
        dret = _chunks(rr * dretn - ret_v * ((rr * rr * rr) * jnp.mean(dretn * ret_v, axis=-1, keepdims=True)))

        q3 = _chunks(q_ref[...].astype(BF16))
        kc = _chunks(k_ref[...] * (1.0 / math.sqrt(RHD)))
        k3 = kc.astype(BF16)
        v3 = _chunks(v_ref[...].astype(BF16))
        d3 = dret.astype(BF16)
        dxi = (dret * xi[None]).astype(BF16)
        kz = (kc * zeta[None]).astype(BF16)
        dr3 = _bdot_tn(q3, dxi)
        acc = g_acc[...]
        for i in reversed(range(CBK)):
            gs[i] = acc.astype(BF16)
            acc = dr3[i] + gch * acc
        g_acc[...] = acc
        g3 = gs[...]
        sc = (_bdot_nt(q3, k3) * decay[None]).astype(BF16)
        da = (_bdot_nt(d3, v3) * decay[None]).astype(BF16)
        dq = _bdot(da, k3) + _bdot_nt(dxi, st_ref[...])
        dkc = _bdot_tn(da, q3) + _bdot_nt(v3, g3) * zeta[None]
        dv = _bdot_tn(sc, d3) + _bdot(kz, g3)

        @pl.when(step > 0)
        def _():
            for cp in out_copies(step - 1):
                cp.wait()

        sq[...] = dq.reshape(rows, RHD).astype(BF16)
        sk[...] = (dkc * (1.0 / math.sqrt(RHD))).reshape(rows, RHD).astype(BF16)
        sv[...] = dv.reshape(rows, RHD).astype(BF16)
        sg[...] = dgate
        for cp in out_copies(step):
            cp.start()

        @pl.when(step == RH * RSTEPS - 1)
        def _():
            for cp in out_copies(step):
                cp.wait()

    col, own, state, const, dm = _ret_specs(True)
    hbm = pl.BlockSpec(memory_space=pl.ANY)
    return pl.pallas_call(
        body, name="ret_bwd", grid=(RH, RSTEPS),
        in_specs=[const, col(0), col(1), col(2), col(3), own, state, dm, hbm],
        out_specs=hbm,
        out_shape=jax.ShapeDtypeStruct(dproj.shape, dproj.dtype),
        input_output_aliases={8: 0},
        scratch_shapes=[pltpu.VMEM((RHD, RHD), F32), pltpu.VMEM((CBK, RHD, RHD), BF16)]
        + [pltpu.VMEM((rows, RHD), BF16) for _ in range(4)] + [pltpu.SemaphoreType.DMA((4,))],
        compiler_params=_cp(("arbitrary", "arbitrary")),
    )(_ret_consts(), proj, proj, proj, proj, ret, states, dmixed, dproj)


class _NoReduction:
    def start(self, group, grads):
        pass

    def local(self, name, first=()):
        return []

    def landed(self, name):
        return []

    def update(self, name):
        return []

    place = None

    def rider(self, name):
        return None

    def set_update(self, name, outs):
        pass


def _local_step(x, tgt, nw1, nw2, nw3, win, wout, wgu_a, wgu_b, wd_a, wd_b, red=None):
    red = red or _NoReduction()

    def after(values, first):
        return lax.optimization_barrier((tuple(values), tuple(first)))[0]

    h1, r1 = _rms_fwd(x, nw1)
    proj = _proj(h1, win)
    o, ma, lse, qkvp, lsep = _attn_fwd(proj)
    ret, mr, states = _ret_fwd(proj)
    x2, h2, r2 = _out_proj_rms(x, ma, mr, wout, nw2)
    a, dadg, dadu = _ffn_up(h2, wgu_b, 1, _ffn_up(h2, wgu_a, 0))
    dx3, dx3b, st3 = _ffn_down_loss(_ffn_down_first(x2, a, wd_a), a, wd_b, nw3, tgt)

    dwd = _wgrad_rows(a, dx3b, "wgrad_down")
    red.start(["w_down"], [dwd])
    (dx3b,) = after([dx3b], [dwd])
    part = _ffn_down_bwd(dx3b, wd_a, dadg, dadu, 0)
    (dx3b,) = after([dx3b], red.local("w_down", first=[part]))
    dgu = _ffn_down_bwd(dx3b, wd_b, dadg, dadu, 1, [part])
    dwg = _wgrad_rows(dgu, h2, "wgrad_gate", 0)
    red.start(["w_gate"], [dwg])
    (dgu,) = after([dgu], [dwg])
    dwu = _wgrad_rows(dgu, h2, "wgrad_up", 1)
    red.start(["w_up"], [dwu])
    (dgu,) = after([dgu], red.local("w_gate", first=[dwu] + red.landed("w_down")))
    dx2, dx2b, st2 = _ffn_up_bwd(dgu, wgu_a, wgu_b, dx3, x2, r2, nw2)
    (dx2b,) = after([dx2b], red.local("w_up", first=[dx2b]))
    dwo = _wgrad_out(ma, mr, dx2b)
    red.start(["w_out"], [dwo])
    (dx2b,) = after([dx2b], [dwo])
    dmixed, done = _out_proj_bwd(dx2b, wout, red.place, red.rider("w_down"))
    red.set_update("w_down", done)
    dproj = _attn_bwd(proj, dmixed, o, lse, qkvp, lsep)
    (dmixed,) = after([dmixed], red.local("w_out", first=[dproj] + red.landed("w_gate")))
    dproj = _ret_bwd(proj, ret, states, dmixed, dproj)
    (dwi0,) = after([_wgrad_in(h1, dproj, 0)], red.landed("w_up"))
    red.start(["w_in_0"], [dwi0])
    (dproj,) = after([dproj], [dwi0])
    dwi1 = _wgrad_in(h1, dproj, 1)
    red.start(["w_in_1"], [dwi1])
    sums = red.local("w_in_0", first=[dwi1] + red.landed("w_out"))
    sums = red.local("w_in_1", first=sums + red.update("w_gate"))
    (dproj,) = after([dproj], sums)
    gx, st1 = _in_proj_bwd(dproj, win, dx2, x, r1, nw1)
    dwi = jnp.concatenate([dwi0, dwi1], axis=1)
    stats = jnp.concatenate([st1[0:1], st2[0:1], st3[0:2], jnp.zeros((4, D), F32)], axis=0)
    return stats, gx, dwi, dwo, dwg, dwu, dwd


def _place():
    x, y, c = lax.axis_index("x"), lax.axis_index("y"), lax.axis_index("c")
    return x, y, c, [(1 - x, y), (x, 1 - y), (1 - x, 1 - y)]


def _handshake(peers):
    barrier = pltpu.get_barrier_semaphore()
    for peer in peers:
        pl.semaphore_signal(barrier, inc=1, device_id=peer, device_id_type=MESH)
    pl.semaphore_wait(barrier, len(peers))


def _all_gather(shards, name, collective_id, per=0, rows=None):
    na = len(shards)
    nout = 1 if per else na
    lo, r = rows or (0, shards[0].shape[0])
    ngroups = NDEV // per if per else 0
    SIB, XN0, XN1, YN1, YN0, VIA_X, VIA_Y = 0, 1, 2, 3, 4, 5, 6
    D2D = {XN0: 7, XN1: 8, YN1: 9, YN0: 10, VIA_X: 11, VIA_Y: 12}

    def body(*refs):
        ins, outs = [ref.at[pl.ds(lo, r)] for ref in refs[:na]], refs[na:na + nout]
        send_sems, recv_sems, local_sems = refs[na + nout:]
        x, y, c, _ = _place()
        me, sib = (x, y, c), (x, y, 1 - c)
        xn, yn, dg = (1 - x, y, c), (x, 1 - y, c), (1 - x, 1 - y, c)
        _handshake([sib, xn, yn])

        def part(ref, h):
            rows = ref.shape[0] // 2
            return ref if h is None else ref.at[pl.ds(h * rows, rows)]

        def block(a, owner, h):
            idx = 4 * owner[0] + 2 * owner[1] + owner[2]
            if not per:
                return part(outs[a].at[idx], h)
            return part(outs[0].at[idx // per, a, pl.ds(pl.multiple_of((idx % per) * r, r), r)], h)

        def copy(a, k, owner, h, to, own_src=False):
            return pltpu.make_async_remote_copy(
                src_ref=part(ins[a], h) if own_src else block(a, owner, h), dst_ref=block(a, owner, h),
                send_sem=send_sems.at[a, k], recv_sem=recv_sems.at[a, k], device_id=to, device_id_type=MESH)

        def other(p):
            return (p[0], p[1], 1 - c)

        mine = [pltpu.make_async_copy(ins[a], block(a, me, None), local_sems.at[a]) for a in range(na)]
        for cp in mine:
            cp.start()
        sent = []
        for a in range(na):
            sent += [copy(a, XN0, me, 0, xn, True), copy(a, YN1, me, 1, yn, True),
                     copy(a, XN1, me, 1, xn, True), copy(a, YN0, me, 0, yn, True)]
        sent += [copy(a, SIB, me, None, sib, True) for a in range(na)]
        for cp in sent:
            cp.start()

        def landed(a, k, owner, h, then):
            copy(a, k, owner, h, me).wait_recv()
            for k2, to in then + [(D2D[k], sib)]:
                cp = copy(a, k2, owner, h, to)
                cp.start()
                sent.append(cp)

        for a in range(na):
            landed(a, XN0, xn, 0, [(VIA_Y, yn)])
            landed(a, YN1, yn, 1, [(VIA_X, xn)])
            landed(a, XN1, xn, 1, [])
            landed(a, YN0, yn, 0, [])
        for a in range(na):
            landed(a, VIA_Y, dg, 0, [])
            landed(a, VIA_X, dg, 1, [])
        for a in range(na):
            copy(a, SIB, sib, None, me).wait_recv()
            for k, owner, h in ((XN0, xn, 0), (XN1, xn, 1), (YN1, yn, 1), (YN0, yn, 0), (VIA_Y, dg, 0), (VIA_X, dg, 1)):
                copy(a, D2D[k], other(owner), h, me).wait_recv()
        for cp in sent:
            cp.wait_send()
        for cp in mine:
            cp.wait()

    if per:
        out_type = [jax.ShapeDtypeStruct((ngroups, na, per * r, shards[0].shape[1]), shards[0].dtype)]
    else:
        out_type = [jax.ShapeDtypeStruct((NDEV,) + s.shape, s.dtype) for s in shards]
    return _sequencer_call(
        body, name, collective_id, out_type,
        [pltpu.SemaphoreType.DMA((na, 13)), pltpu.SemaphoreType.DMA((na, 13)), pltpu.SemaphoreType.DMA((na,))])(*shards)


def _sequencer_call(body, name, collective_id, out_type, scratch_types):
    return pl.kernel(
        body, name=name, out_type=out_type,
        mesh=plsc.ScalarSubcoreMesh(axis_name="sequencer", num_cores=1),
        scratch_types=scratch_types,
        compiler_params=pltpu.CompilerParams(collective_id=collective_id))


def _exchange_sibling(grads, name, collective_id):
    na = len(grads)

    def body(*refs):
        ins, outs = refs[:na], refs[na:2 * na]
        send_sems, recv_sems = refs[2 * na:]
        x, y, c, _ = _place()
        _handshake([(x, y, 1 - c)])
        cps = []
        for a in range(na):
            for k in range(4):
                cps.append(pltpu.make_async_remote_copy(
                    src_ref=ins[a].at[2 * k + (1 - c)], dst_ref=outs[a].at[k],
                    send_sem=send_sems.at[a, k], recv_sem=recv_sems.at[a, k],
                    device_id=(x, y, 1 - c), device_id_type=MESH))
        for cp in cps:
            cp.start()
        for cp in cps:
            cp.wait()

    return _sequencer_call(
        body, name, collective_id,
        [jax.ShapeDtypeStruct((4,) + g.shape[1:], g.dtype) for g in grads],
        [pltpu.SemaphoreType.DMA((na, 4)), pltpu.SemaphoreType.DMA((na, 4))])(*grads)


def _row_tile(rows, cols):
    for t in (512, 256, 176, 128, 64, 32, 16):
        if rows % t == 0 and t * cols * 4 <= (2 << 20):
            return t
    raise ValueError((rows, cols))


STREAM_BUFS = 3


def _stream_tile(rows, steps):
    for t in (512, 256, 176, 128, 64, 32, 16):
        if rows % t == 0 and rows // t >= steps:
            return t
    raise ValueError((rows, steps))


def _stream(n, loads, stores, compute):
    for k in range(min(STREAM_BUFS, n)):
        for cp in loads(k):
            cp.start()
    for k in range(n):
        for cp in loads(k):
            cp.wait()
        if k >= 2:
            for cp in stores(k - 2):
                cp.wait()
        compute(k)
        for cp in stores(k):
            cp.start()
        if k + STREAM_BUFS < n:
            for cp in loads(k + STREAM_BUFS):
                cp.start()
    for k in range(max(n - 2, 0), n):
        for cp in stores(k):
            cp.wait()


def _chip_sum(place, g, got, name):
    _, r, c = g.shape
    tm = _stream_tile(r, 4)
    nt = r // tm

    def body(pos_ref, g_hbm, got_hbm, o_hbm, g_buf, s_buf, o_buf, sem_in, sem_out):
        def chip(j):
            return 2 * (pos_ref[0] ^ (0 if j == 1 else 1)) + (pos_ref[1] ^ (0 if j == 0 else 1))

        def loads(k):
            j, rows, slot = k // nt, pl.ds((k % nt) * tm, tm), k % STREAM_BUFS
            return [pltpu.make_async_copy(g_hbm.at[2 * chip(j) + pos_ref[2], rows], g_buf.at[slot], sem_in.at[slot, 0]),
                    pltpu.make_async_copy(got_hbm.at[chip(j), rows], s_buf.at[slot], sem_in.at[slot, 1])]

        def stores(k):
            return [pltpu.make_async_copy(o_buf.at[k % 2], o_hbm.at[k // nt, pl.ds((k % nt) * tm, tm)],
                                          sem_out.at[k % 2])]

        def compute(k):
            slot = k % STREAM_BUFS
            o_buf[k % 2] = (g_buf[slot].astype(F32) + s_buf[slot].astype(F32)).astype(BF16)

        _stream(3 * nt, loads, stores, compute)

    hbm = pl.BlockSpec(memory_space=pl.ANY)
    return pl.pallas_call(
        body, name=name,
        grid_spec=pltpu.PrefetchScalarGridSpec(
            num_scalar_prefetch=1, grid=(1,), in_specs=[hbm, hbm], out_specs=hbm,
            scratch_shapes=[pltpu.VMEM((STREAM_BUFS, tm, c), BF16), pltpu.VMEM((STREAM_BUFS, tm, c), BF16),
                            pltpu.VMEM((2, tm, c), BF16),
                            pltpu.SemaphoreType.DMA((STREAM_BUFS, 2)), pltpu.SemaphoreType.DMA((2,))]),
        out_shape=jax.ShapeDtypeStruct((3, r, c), BF16),
        compiler_params=_cp(("arbitrary",)),
    )(place, g, got)


def _exchange_chips(sums, name, collective_id):
    na = len(sums)

    def body(*refs):
        ins, outs = refs[:na], refs[na:2 * na]
        send_sems, recv_sems = refs[2 * na:]
        x, y, c, chips = _place()
        _handshake([(*chip, c) for chip in chips])
        cps = []
        for a in range(na):
            for j, chip in enumerate(chips):
                cps.append(pltpu.make_async_remote_copy(
                    src_ref=ins[a].at[j], dst_ref=outs[a].at[j],
                    send_sem=send_sems.at[a, j], recv_sem=recv_sems.at[a, j],
                    device_id=(*chip, c), device_id_type=MESH))
        for cp in cps:
            cp.start()
        for cp in cps:
            cp.wait()

    return _sequencer_call(
        body, name, collective_id,
        [jax.ShapeDtypeStruct((3,) + s.shape[1:], s.dtype) for s in sums],
        [pltpu.SemaphoreType.DMA((na, 3)), pltpu.SemaphoreType.DMA((na, 3))])(*sums)


def _exchange_stats(stats, collective_id):
    def body(st_in, st_out, st_send, st_recv, local_sem):
        x, y, c, _ = _place()
        me_idx = 4 * x + 2 * y + c
        peers = [(x ^ ((k >> 2) & 1), y ^ ((k >> 1) & 1), c ^ (k & 1)) for k in range(1, 8)]
        _handshake(peers)
        mine = pltpu.make_async_copy(st_in, st_out.at[me_idx], local_sem)
        mine.start()
        cps = [pltpu.make_async_remote_copy(
            src_ref=st_in, dst_ref=st_out.at[me_idx], send_sem=st_send.at[k], recv_sem=st_recv.at[k],
            device_id=peer, device_id_type=MESH) for k, peer in enumerate(peers)]
        for cp in cps:
            cp.start()
        for cp in cps:
            cp.wait()
        mine.wait()

    return _sequencer_call(
        body, "exchange_stats", collective_id,
        jax.ShapeDtypeStruct((NDEV,) + stats.shape, stats.dtype),
        [pltpu.SemaphoreType.DMA((7,)), pltpu.SemaphoreType.DMA((7,)), pltpu.SemaphoreType.DMA])(stats)


class _Reduction:
    def __init__(self, place, first_collective_id, state):
        self.place = place
        self.ids = iter(range(first_collective_id, 32))
        self.state = state
        self.groups = {}
        self.updates = {}

    def next_id(self):
        return next(self.ids)

    def start(self, group, grads):
        got = _exchange_sibling(grads, "sibling_exchange_" + group[0], self.next_id())
        self.groups[group[0]] = dict(names=group, grads=grads, got=got)

    def local(self, name, first=()):
        grp = self.groups[name]
        grads = lax.optimization_barrier((tuple(grp["grads"]), tuple(first)))[0]
        grp["sums"] = [_chip_sum(self.place, g, s, "chip_sum_" + n)
                       for g, s, n in zip(grads, grp["got"], grp["names"])]
        grp["chips"] = _exchange_chips(grp["sums"], "chip_exchange_" + name, self.next_id())
        return grp["sums"]

    def landed(self, name):
        return list(self.groups[name]["chips"])

    def rider(self, name):
        grp = next(g for g in self.groups.values() if name in g["names"])
        k = grp["names"].index(name)
        return self.state[name][:3] + (grp["grads"][k], grp["got"][k], grp["chips"][k])

    def set_update(self, name, outs):
        self.updates[name] = list(outs)

    def update(self, name):
        if name not in self.updates:
            grp = next(g for g in self.groups.values() if name in g["names"])
            k = grp["names"].index(name)
            w, m, v, part, parts = self.state[name]
            before = self.update(f"{name[:-1]}{part - 1}") if part else None
            self.updates[name] = _shard_update(self.place, w, m, v, grp["grads"][k], grp["got"][k],
                                               grp["chips"][k], "update_" + name, part, parts, before)
        return list(self.updates[name])


def _adamw(w, g, m, v):
    m = ADAM_B1 * m + (1.0 - ADAM_B1) * g
    v = ADAM_B2 * v + (1.0 - ADAM_B2) * (g * g)
    m_hat = m / (1.0 - ADAM_B1 ** ADAM_STEP)
    v_hat = v / (1.0 - ADAM_B2 ** ADAM_STEP)
    delta = -ADAM_LR * (m_hat / (jnp.sqrt(v_hat) + ADAM_EPS) + ADAM_WD * w)
    return delta, m, v


def _update_tile(w_ref, m_ref, v_ref, g_ref, s_ref, c_ref, go_ref, d_ref, mo_ref, vo_ref):
    grad = g_ref[...].astype(F32) + s_ref[...].astype(F32)
    for j in range(3):
        grad = grad + c_ref[j].astype(F32)
    delta, mn, vn = _adamw(w_ref[...], grad, m_ref[...], v_ref[...])
    go_ref[...] = grad
    d_ref[...] = delta
    mo_ref[...] = mn
    vo_ref[...] = vn


def _shard_update(place, w, m, v, g, got_sib, got_chips, name, part=0, parts=1, before=None):
    r, c = w.shape
    rp = r // parts
    tm = _stream_tile(rp, 8)
    nt = rp // tm
    before = list(before or [])

    def body(pos_ref, w_hbm, m_hbm, v_hbm, g_hbm, s_hbm, c_hbm, *rest):
        outs = rest[len(before):len(before) + 4]
        w_buf, m_buf, v_buf, g_buf, s_buf, c_buf, o_buf, sem_in, sem_out = rest[len(before) + 4:]
        own = 4 * pos_ref[0] + 2 * pos_ref[1] + pos_ref[2]
        chip = 2 * pos_ref[0] + pos_ref[1]

        def loads(k):
            slot, rows, mine = k % STREAM_BUFS, pl.ds(k * tm, tm), pl.ds(part * rp + k * tm, tm)
            pairs = [(w_hbm.at[mine], w_buf), (m_hbm.at[mine], m_buf), (v_hbm.at[mine], v_buf),
                     (g_hbm.at[own, rows], g_buf), (s_hbm.at[chip, rows], s_buf), (c_hbm.at[:, rows], c_buf)]
            return [pltpu.make_async_copy(src, buf.at[slot], sem_in.at[slot, n]) for n, (src, buf) in enumerate(pairs)]

        def stores(k):
            mine = pl.ds(part * rp + k * tm, tm)
            return [pltpu.make_async_copy(o_buf.at[k % 2, n], out.at[mine], sem_out.at[k % 2, n])
                    for n, out in enumerate(outs)]

        def compute(k):
            slot = k % STREAM_BUFS
            _update_tile(w_buf.at[slot], m_buf.at[slot], v_buf.at[slot], g_buf.at[slot], s_buf.at[slot],
                         c_buf.at[slot], *[o_buf.at[k % 2, n] for n in range(4)])

        _stream(nt, loads, stores, compute)

    hbm = pl.BlockSpec(memory_space=pl.ANY)
    return pl.pallas_call(
        body, name=name,
        grid_spec=pltpu.PrefetchScalarGridSpec(
            num_scalar_prefetch=1, grid=(1,), in_specs=[hbm] * (6 + len(before)), out_specs=[hbm] * 4,
            scratch_shapes=[pltpu.VMEM((STREAM_BUFS, tm, c), F32)] * 3 + [pltpu.VMEM((STREAM_BUFS, tm, c), BF16)] * 2
            + [pltpu.VMEM((STREAM_BUFS, 3, tm, c), BF16), pltpu.VMEM((2, 4, tm, c), F32),
               pltpu.SemaphoreType.DMA((STREAM_BUFS, 6)), pltpu.SemaphoreType.DMA((2, 4))]),
        out_shape=[jax.ShapeDtypeStruct((r, c), F32)] * 4,
        input_output_aliases={7 + k: k for k in range(len(before))},
        compiler_params=_cp(("arbitrary",)),
    )(place, w, m, v, g, got_sib, got_chips, *before)


def _small_update(stats_all, ws, ms, vs):
    def body(st_ref, w_ref, m_ref, v_ref, go_ref, d_ref, mo_ref, vo_ref):
        grad = st_ref[0]
        for k in range(1, NDEV):
            grad = grad + st_ref[k]
        delta, mn, vn = _adamw(w_ref[...], grad, m_ref[...], v_ref[...])
        go_ref[...] = grad
        d_ref[...] = delta
        mo_ref[...] = mn
        vo_ref[...] = vn

    return pl.pallas_call(
        body, name="small_update",
        out_shape=[jax.ShapeDtypeStruct((8, D), F32)] * 4,
        compiler_params=_cp(),
    )(stats_all, ws, ms, vs)


def kernel(x, norm_mix_w, w_in, w_out, norm_ffn_w, w_gate, w_up, w_down, norm_final_w, loss_target, m_norm_mix_w, m_w_in, m_w_out, m_norm_ffn_w, m_w_gate, m_w_up, m_w_down, m_norm_final_w, v_norm_mix_w, v_w_in, v_w_out, v_norm_ffn_w, v_w_gate, v_w_up, v_w_down, v_norm_final_w):
    tr = {"w_gate", "w_up"}
    names = ["w_in", "w_out", "w_gate", "w_up", "w_down"]

    def view(a, n):
        return a[0].T if n in tr else a[0]

    big_w = [view(a, n) for a, n in zip([w_in, w_out, w_gate, w_up, w_down], names)]
    big_m = [view(a, n) for a, n in zip([m_w_in, m_w_out, m_w_gate, m_w_up, m_w_down], names)]
    big_v = [view(a, n) for a, n in zip([v_w_in, v_w_out, v_w_gate, v_w_up, v_w_down], names)]

    shards = [None] + [_cast_bf16(w, "cast_" + n) for w, n in zip(big_w[1:], names[1:])]
    win = [_all_gather([cols], f"all_gather_w_in_{k}", 1 + k)[0]
           for k, cols in enumerate(_cast_cols(big_w[0], "cast_w_in"))]
    (wout,) = _all_gather(shards[1:2], "all_gather_w_out", 3)
    (wgu_a,) = _all_gather(shards[2:4], "all_gather_gate_up_0", 4, per=FF_PER, rows=(0, FF_ROWS))
    (wgu_b,) = _all_gather(shards[2:4], "all_gather_gate_up_1", 5, per=FF_PER, rows=(FF_ROWS, FF_ROWS))
    (wd_a,) = _all_gather(shards[4:5], "all_gather_w_down_0", 6, per=FF_PER, rows=(0, FF_ROWS))
    (wd_b,) = _all_gather(shards[4:5], "all_gather_w_down_1", 7, per=FF_PER, rows=(FF_ROWS, FF_ROWS))
    nw3 = norm_final_w.reshape(1, D)
    place = jnp.stack([lax.axis_index("x"), lax.axis_index("y"), lax.axis_index("c")]).astype(jnp.int32)
    state = {n: (w, m, v, 0, 1) for n, w, m, v in zip(names, big_w, big_m, big_v)}
    for part in range(W_IN_PARTS):
        state[f"w_in_{part}"] = state["w_in"][:3] + (part, W_IN_PARTS)
    red = _Reduction(place, 8, state)
    stats, gx, *_ = _local_step(
        x[0], loss_target[0], norm_mix_w, norm_ffn_w, nw3, win, wout.reshape(D, D),
        wgu_a.reshape(NFG // 2, 2 * N_FG, D), wgu_b.reshape(NFG // 2, 2 * N_FG, D),
        wd_a.reshape(NFG // 2, N_FG, D), wd_b.reshape(NFG // 2, N_FG, D), red)
    stats_all = _exchange_stats(stats, red.next_id())
    upd = [red.update(f"w_in_{W_IN_PARTS - 1}" if n == "w_in" else n) for n in names]
    stats_all = lax.optimization_barrier((stats_all, tuple(upd[0])))[0]

    def rows(a, b, c):
        return jnp.concatenate([a.reshape(1, D), b.reshape(1, D), c.reshape(1, D), jnp.zeros((5, D), F32)], axis=0)

    sg, sd, sm, sv = _small_update(stats_all, rows(norm_mix_w, norm_ffn_w, norm_final_w),
                                   rows(m_norm_mix_w, m_norm_ffn_w, m_norm_final_w),
                                   rows(v_norm_mix_w, v_norm_ffn_w, v_norm_final_w))
    loss = sg[3, 0]

    def outs(k, small):
        big = [(u[k].T if n in tr else u[k])[None] for u, n in zip(upd, names)]
        return [small[0:1], big[0], big[1], small[1:2], big[2], big[3], big[4], small[2]]

    return (loss, gx[None], *outs(0, sg), *outs(1, sd), *outs(2, sm), *outs(3, sv))
```

```python
import math

import numpy as np
import jax
import jax.numpy as jnp
from jax import lax
from jax.experimental import pallas as pl
from jax.experimental.pallas import tpu as pltpu
from jax.experimental.pallas import tpu_sc as plsc

F32 = jnp.float32
BF16 = jnp.bfloat16

S = 2048
D = 2048
NDEV = 8
N_IN = 7168 // NDEV
N_FF = 5632 // NDEV
NFG, N_FG = NDEV // 2, 2 * N_FF
FF_PER, FF_ROWS = 4, N_FF // 2
TAIL_ROWS = 256
IN_ROUNDS = ((0, 512), (512, N_IN - 512))
N_OUT = 2048 // NDEV
AH, AHD = 8, 128
RH, RHD = 4, 256
CH = 128
NB = S // CH
EPS = 1e-6
PATTERNS = ((1, 16), (4, 4), (16, 1))
NEG = -1e30
VMEM_LIMIT = 56 * 1024 * 1024

ADAM_LR, ADAM_B1, ADAM_B2, ADAM_EPS, ADAM_WD, ADAM_STEP = 0.001, 0.9, 0.999, 1e-08, 0.01, 10
MESH = pl.DeviceIdType.MESH


def _cp(sem=None):
    return pltpu.CompilerParams(dimension_semantics=sem, vmem_limit_bytes=VMEM_LIMIT)


def _dot(a, b):
    return jnp.dot(a, b, preferred_element_type=F32)


def _dot_nt(a, b):
    return lax.dot_general(a, b, (((1,), (1,)), ((), ())), preferred_element_type=F32)


def _dot_tn(a, b):
    return lax.dot_general(a, b, (((0,), (0,)), ((), ())), preferred_element_type=F32)


def _sigmoid(x):
    return 0.5 * jnp.tanh(0.5 * x) + 0.5


def _cast_bf16(w, name):
    r, c = w.shape
    tm = r if r <= 1024 else 512

    def body(w_ref, o_ref):
        o_ref[...] = w_ref[...].astype(BF16)

    return pl.pallas_call(
        body, name=name, grid=(r // tm,),
        in_specs=[pl.BlockSpec((tm, c), lambda i: (i, 0))],
        out_specs=pl.BlockSpec((tm, c), lambda i: (i, 0)),
        out_shape=jax.ShapeDtypeStruct((r, c), BF16),
        compiler_params=_cp(("parallel",)),
    )(w)


def _rms_fwd(x, nw):
    tm = 256

    def body(x_ref, w_ref, h_ref, r_ref):
        xs = x_ref[...]
        r = lax.rsqrt(jnp.mean(xs * xs, axis=-1, keepdims=True) + EPS)
        h_ref[...] = ((xs * r) * w_ref[...]).astype(BF16)
        r_ref[...] = r

    return pl.pallas_call(
        body, name="rms_fwd", grid=(S // tm,),
        in_specs=[pl.BlockSpec((tm, D), lambda i: (i, 0)), pl.BlockSpec((1, D), lambda i: (0, 0))],
        out_specs=[pl.BlockSpec((tm, D), lambda i: (i, 0)), pl.BlockSpec((tm, 1), lambda i: (i, 0))],
        out_shape=[jax.ShapeDtypeStruct((S, D), BF16), jax.ShapeDtypeStruct((S, 1), F32)],
        compiler_params=_cp(("parallel",)),
    )(x, nw)


def _row_copies(hbm_refs, bufs, sems, m, tm):
    rows = pl.ds(pl.multiple_of(m * tm, tm), tm)
    return [pltpu.make_async_copy(h.at[rows], b, sems.at[i]) for i, (h, b) in enumerate(zip(hbm_refs, bufs))]


def _rms_bwd_tile(dh, xs, r, nw):
    dnw = jnp.sum(dh * (xs * r), axis=0, keepdims=True)
    gy = dh * nw
    dx = r * gy - xs * ((r * r * r) * jnp.mean(gy * xs, axis=-1, keepdims=True))
    return dx, dnw


def _cast_cols(w, name):
    r, c = w.shape
    tm = 512

    def body(w_ref, *o_refs):
        for o_ref, (off, width) in zip(o_refs, IN_ROUNDS):
            o_ref[...] = w_ref[:, off:off + width].astype(BF16)

    return pl.pallas_call(
        body, name=name, grid=(r // tm,),
        in_specs=[pl.BlockSpec((tm, c), lambda i: (i, 0))],
        out_specs=[pl.BlockSpec((tm, width), lambda i: (i, 0)) for _, width in IN_ROUNDS],
        out_shape=[jax.ShapeDtypeStruct((r, width), BF16) for _, width in IN_ROUNDS],
        compiler_params=_cp(("parallel",)),
    )(w)


def _proj_round(h1, win, k, before):
    tm = 1024
    nm = S // tm
    off, width = IN_ROUNDS[k]
    before = [] if before is None else [before]

    def body(a_ref, w_ref, *rest):
        o_hbm, o_buf, sems = rest[-3:]
        p, m = pl.program_id(0), pl.program_id(1)
        t = p * nm + m

        def out_copy(pp, mm, slot):
            cols = pl.ds(pl.multiple_of(pp * N_IN + off, 128), width)
            return pltpu.make_async_copy(o_buf.at[slot], o_hbm.at[pl.ds(pl.multiple_of(mm * tm, tm), tm), cols],
                                         sems.at[slot])

        @pl.when(t >= 2)
        def _():
            out_copy(p, m, t % 2).wait()

        o_buf[t % 2] = _dot(a_ref[...], w_ref[...])
        out_copy(p, m, t % 2).start()

        @pl.when(t == NDEV * nm - 1)
        def _():
            out_copy(p, m, (t + 1) % 2).wait()
            out_copy(p, m, t % 2).wait()

    return pl.pallas_call(
        body, name=f"proj_{k}", grid=(NDEV, nm),
        in_specs=[pl.BlockSpec((tm, D), lambda p, m: (m, 0)),
                  pl.BlockSpec((None, D, width), lambda p, m: (p, 0, 0))]
        + [pl.BlockSpec(memory_space=pl.ANY)] * len(before),
        out_specs=pl.BlockSpec(memory_space=pl.ANY),
        out_shape=jax.ShapeDtypeStruct((S, NDEV * N_IN), F32),
        scratch_shapes=[pltpu.VMEM((2, tm, width), F32), pltpu.SemaphoreType.DMA((2,))],
        input_output_aliases={2: 0} if before else {},
        compiler_params=_cp(("arbitrary", "arbitrary")),
    )(h1, win, *before)


def _proj(h1, wins):
    out = None
    for k, win in enumerate(wins):
        out = _proj_round(h1, win, k, out)
    return out


def _out_proj_rms(x, ma, mr, wout, nw):
    tm = 256
    half = D // 2

    def body(x_ref, ma_ref, mr_ref, w_ref, nw_ref, x2_ref, h_ref, r_ref):
        acc = _dot(ma_ref[...], w_ref[0:half, :]) + _dot(mr_ref[...], w_ref[half:D, :])
        x2 = x_ref[...] + acc
        r = lax.rsqrt(jnp.mean(x2 * x2, axis=-1, keepdims=True) + EPS)
        x2_ref[...] = x2
        h_ref[...] = ((x2 * r) * nw_ref[...]).astype(BF16)
        r_ref[...] = r

    return pl.pallas_call(
        body, name="out_proj_rms", grid=(S // tm,),
        in_specs=[pl.BlockSpec((tm, D), lambda i: (i, 0)),
                  pl.BlockSpec((tm, half), lambda i: (i, 0)),
                  pl.BlockSpec((tm, half), lambda i: (i, 0)),
                  pl.BlockSpec((D, D), lambda i: (0, 0)),
                  pl.BlockSpec((1, D), lambda i: (0, 0))],
        out_specs=[pl.BlockSpec((tm, D), lambda i: (i, 0)), pl.BlockSpec((tm, D), lambda i: (i, 0)),
                   pl.BlockSpec((tm, 1), lambda i: (i, 0))],
        out_shape=[jax.ShapeDtypeStruct((S, D), F32), jax.ShapeDtypeStruct((S, D), BF16),
                   jax.ShapeDtypeStruct((S, 1), F32)],
        compiler_params=_cp(("parallel",)),
    )(x, ma, mr, wout, nw)


def _ffn_up(h2, wgu, part, before=None):
    tm = 512

    def body(h_ref, w_ref, *rest):
        a_ref, dadg_ref, dadu_ref = rest[-3:]
        gu = _dot_nt(h_ref[...], w_ref[...])
        g, u = gu[:, 0:N_FG], gu[:, N_FG:2 * N_FG]
        sg = _sigmoid(g)
        silu = g * sg
        a_ref[...] = (silu * u).astype(BF16)
        dadg_ref[...] = (u * (sg * (1.0 + g * (1.0 - sg)))).astype(BF16)
        dadu_ref[...] = silu.astype(BF16)

    half = NFG // 2
    first = part * half
    before = list(before or [])
    blk = pl.BlockSpec((None, tm, N_FG), lambda p, m: (p + first, m, 0))
    return pl.pallas_call(
        body, name=f"ffn_up_{part}", grid=(half, S // tm),
        in_specs=[pl.BlockSpec((tm, D), lambda p, m: (m, 0)),
                  pl.BlockSpec((None, 2 * N_FG, D), lambda p, m: (p, 0, 0))]
        + [pl.BlockSpec(memory_space=pl.ANY)] * len(before),
        out_specs=[blk, blk, blk],
        out_shape=[jax.ShapeDtypeStruct((NFG, S, N_FG), BF16)] * 3,
        input_output_aliases={2 + k: k for k in range(len(before))},
        compiler_params=_cp(("parallel", "parallel")),
    )(h2, wgu, *before)


def _ffn_down_first(x2, a, wd):
    tm = 512
    n = wd.shape[0]

    def body(x_ref, a_ref, w_ref, o_ref):
        p = pl.program_id(1)

        @pl.when(p == 0)
        def _():
            o_ref[...] = x_ref[...] + _dot(a_ref[...], w_ref[0])

        @pl.when(p > 0)
        def _():
            o_ref[...] += _dot(a_ref[...], w_ref[p])

    return pl.pallas_call(
        body, name="ffn_down_first", grid=(S // tm, n),
        in_specs=[pl.BlockSpec((tm, D), lambda m, p: (m, 0)),
                  pl.BlockSpec((None, tm, N_FG), lambda m, p: (p, m, 0)),
                  pl.BlockSpec((n, N_FG, D), lambda m, p: (0, 0, 0))],
        out_specs=pl.BlockSpec((tm, D), lambda m, p: (m, 0)),
        out_shape=jax.ShapeDtypeStruct((S, D), F32),
        compiler_params=_cp(("parallel", "arbitrary")),
    )(x2, a, wd)


def _ffn_down_loss(x2, a, wd, nw, tgt):
    tm = 512
    first = NFG - wd.shape[0]

    def body(x2_hbm, a_ref, w_ref, nw_ref, t_hbm, dx_ref, dxb_ref, st_ref, acc_ref, x2_buf, t_buf, sems):
        m, p = pl.program_id(0), pl.program_id(1)
        tail_in = _row_copies((x2_hbm, t_hbm), (x2_buf, t_buf), sems, m, tm)

        @pl.when(p == 0)
        def _():
            acc_ref[...] = jnp.zeros_like(acc_ref)
            for cp in tail_in:
                cp.start()

        @pl.when((p == 0) & (m == 0))
        def _():
            st_ref[...] = jnp.zeros_like(st_ref)

        acc_ref[...] += _dot(a_ref[...], w_ref[p])

        @pl.when(p == NFG - first - 1)
        def _():
            for cp in tail_in:
                cp.wait()
            x3 = x2_buf[...] + acc_ref[...]
            nwv = nw_ref[...]
            r = lax.rsqrt(jnp.mean(x3 * x3, axis=-1, keepdims=True) + EPS)
            y = (x3 * r) * nwv
            err = y - t_buf[...]
            loss = 0.5 * jnp.sum(jnp.mean(err * err, axis=-1, keepdims=True), axis=0, keepdims=True)
            dy = err * (1.0 / D)
            dx, dnw = _rms_bwd_tile(dy, x3, r, nwv)
            dx_ref[...] = dx
            dxb_ref[...] = dx.astype(BF16)
            st_ref[0:1, :] += dnw
            st_ref[1:2, :] += jnp.broadcast_to(loss, (1, D))

    return pl.pallas_call(
        body, name="ffn_down_loss", grid=(S // tm, NFG - first),
        in_specs=[pl.BlockSpec(memory_space=pl.ANY),
                  pl.BlockSpec((None, tm, N_FG), lambda m, p: (p + first, m, 0)),
                  pl.BlockSpec((NFG - first, N_FG, D), lambda m, p: (0, 0, 0)),
                  pl.BlockSpec((1, D), lambda m, p: (0, 0)),
                  pl.BlockSpec(memory_space=pl.ANY)],
        out_specs=[pl.BlockSpec((tm, D), lambda m, p: (m, 0)), pl.BlockSpec((tm, D), lambda m, p: (m, 0)),
                   pl.BlockSpec((8, D), lambda m, p: (0, 0))],
        out_shape=[jax.ShapeDtypeStruct((S, D), F32), jax.ShapeDtypeStruct((S, D), BF16),
                   jax.ShapeDtypeStruct((8, D), F32)],
        scratch_shapes=[pltpu.VMEM((tm, D), F32), pltpu.VMEM((tm, D), F32), pltpu.VMEM((tm, D), F32),
                        pltpu.SemaphoreType.DMA((2,))],
        compiler_params=_cp(("arbitrary", "arbitrary")),
    )(x2, a, wd, nw, tgt)


def _ffn_down_bwd(dx3b, wd, dadg, dadu, part, before=None):
    tm = 1024
    half = NFG // 2

    def body(dx_ref, w_ref, dadg_ref, dadu_ref, *rest):
        dgu_ref = rest[-1]
        rows = pl.ds(pl.multiple_of(pl.program_id(1) * tm, tm), tm)
        da = _dot_nt(dx_ref[rows, :], w_ref[...])
        dgu_ref[:, 0:N_FG] = (da * dadg_ref[...].astype(F32)).astype(BF16)
        dgu_ref[:, N_FG:2 * N_FG] = (da * dadu_ref[...].astype(F32)).astype(BF16)

    blk = pl.BlockSpec((None, tm, N_FG), lambda p, m: (p + part * half, m, 0))
    before = list(before or [])
    return pl.pallas_call(
        body, name=f"ffn_down_bwd_{part}", grid=(half, S // tm),
        in_specs=[pl.BlockSpec((S, D), lambda p, m: (0, 0)),
                  pl.BlockSpec((None, N_FG, D), lambda p, m: (p, 0, 0)), blk, blk]
        + [pl.BlockSpec(memory_space=pl.ANY)] * len(before),
        out_specs=pl.BlockSpec((None, tm, 2 * N_FG), lambda p, m: (p + part * half, m, 0)),
        out_shape=jax.ShapeDtypeStruct((NFG, S, 2 * N_FG), BF16),
        input_output_aliases={4 + k: k for k in range(len(before))},
        compiler_params=_cp(("parallel", "parallel")),
    )(dx3b, wd, dadg, dadu, *before)


def _ffn_up_bwd(dgu, wgu_a, wgu_b, dres, xs, r, nw):
    tm = 512
    nm = S // tm
    na = wgu_a.shape[0]

    def body(dgu_ref, wa_hbm, wb_hbm, dres_hbm, x_hbm, r_ref, nw_ref, dx_ref, dxb_ref, st_ref,
             w_buf, dres_buf, x_buf, sems, w_sems):
        m, p = pl.program_id(0), pl.program_id(1)
        tail_in = _row_copies((dres_hbm, x_hbm), (dres_buf, x_buf), sems, m, tm)

        def fetch(g, slot):
            for src, lo in ((wa_hbm, 0), (wb_hbm, na)):
                @pl.when((g >= lo) & (g < lo + na))
                def _():
                    pltpu.make_async_copy(src.at[g - lo], w_buf.at[slot], w_sems.at[slot]).start()

        @pl.when((p == 0) & (m == 0))
        def _():
            st_ref[...] = jnp.zeros_like(st_ref)
            fetch(p, 0)

        @pl.when((p < NFG - 1) | (m < nm - 1))
        def _():
            fetch((p + 1) % NFG, (p + 1) % 2)

        @pl.when(p == 0)
        def _():
            dx_ref[...] = jnp.zeros_like(dx_ref)
            for cp in tail_in:
                cp.start()

        slot = p % 2
        pltpu.make_async_copy(wa_hbm.at[0], w_buf.at[slot], w_sems.at[slot]).wait()
        dx_ref[...] += _dot(dgu_ref[...], w_buf[slot])

        @pl.when(p == NFG - 1)
        def _():
            for cp in tail_in:
                cp.wait()
            dx, dnw = _rms_bwd_tile(dx_ref[...], x_buf[...], r_ref[...], nw_ref[...])
            dx = dres_buf[...] + dx
            dx_ref[...] = dx
            dxb_ref[...] = dx.astype(BF16)
            st_ref[0:1, :] += dnw

    blk = pl.BlockSpec((None, tm, 2 * N_FG), lambda m, p: (p, m, 0))
    row = pl.BlockSpec((tm, D), lambda m, p: (m, 0))
    hbm = pl.BlockSpec(memory_space=pl.ANY)
    return pl.pallas_call(
        body, name="ffn_up_bwd", grid=(nm, NFG),
        in_specs=[blk, hbm, hbm, hbm, hbm, pl.BlockSpec((tm, 1), lambda m, p: (m, 0)),
                  pl.BlockSpec((1, D), lambda m, p: (0, 0))],
        out_specs=[row, row, pl.BlockSpec((8, D), lambda m, p: (0, 0))],
        out_shape=[jax.ShapeDtypeStruct((S, D), F32), jax.ShapeDtypeStruct((S, D), BF16),
                   jax.ShapeDtypeStruct((8, D), F32)],
        scratch_shapes=[pltpu.VMEM((2, 2 * N_FG, D), BF16), pltpu.VMEM((tm, D), F32), pltpu.VMEM((tm, D), F32),
                        pltpu.SemaphoreType.DMA((2,)), pltpu.SemaphoreType.DMA((2,))],
        compiler_params=_cp(("arbitrary", "arbitrary")),
    )(dgu, wgu_a, wgu_b, dres, xs, r, nw)


def _out_proj_bwd(dx2b, wout, place=None, rider=None):
    tm = 256

    if rider is None:
        def body(dx_ref, w_ref, o_ref):
            o_ref[...] = _dot_nt(dx_ref[...], w_ref[...])

        return pl.pallas_call(
            body, name="out_proj_bwd", grid=(S // tm,),
            in_specs=[pl.BlockSpec((tm, D), lambda i: (i, 0)), pl.BlockSpec((D, D), lambda i: (0, 0))],
            out_specs=pl.BlockSpec((tm, D), lambda i: (i, 0)),
            out_shape=jax.ShapeDtypeStruct((S, D), F32),
            compiler_params=_cp(("parallel",)),
        )(dx2b, wout), None

    w = rider[0]
    r, c = w.shape
    rt = _row_tile(r, c)
    nt = r // rt
    assert nt <= S // tm

    def body(pos_ref, dx_ref, w_ref, uw, um, uv, ug, us, uc, o_ref, go, dd, mo, vo):
        o_ref[...] = _dot_nt(dx_ref[...], w_ref[...])

        @pl.when(pl.program_id(0) < nt)
        def _():
            _update_tile(uw, um, uv, ug, us, uc, go, dd, mo, vo)

    def at(i):
        return jnp.minimum(i, nt - 1)

    tile = pl.BlockSpec((rt, c), lambda i, pos: (at(i), 0))
    outs = pl.pallas_call(
        body, name="out_proj_bwd",
        grid_spec=pltpu.PrefetchScalarGridSpec(
            num_scalar_prefetch=1, grid=(S // tm,),
            in_specs=[pl.BlockSpec((tm, D), lambda i, pos: (i, 0)), pl.BlockSpec((D, D), lambda i, pos: (0, 0)),
                      tile, tile, tile,
                      pl.BlockSpec((None, rt, c), lambda i, pos: (4 * pos[0] + 2 * pos[1] + pos[2], at(i), 0)),
                      pl.BlockSpec((None, rt, c), lambda i, pos: (2 * pos[0] + pos[1], at(i), 0)),
                      pl.BlockSpec((3, rt, c), lambda i, pos: (0, at(i), 0))],
            out_specs=[pl.BlockSpec((tm, D), lambda i, pos: (i, 0)), tile, tile, tile, tile]),
        out_shape=[jax.ShapeDtypeStruct((S, D), F32)] + [jax.ShapeDtypeStruct((r, c), F32)] * 4,
        compiler_params=_cp(("arbitrary",)),
    )(place, dx2b, wout, *rider)
    return outs[0], outs[1:]


def _in_proj_bwd(dproj, wins, dres, xs, r, nw):
    tm = 1024

    nr = len(wins)
    nm = S // tm

    def body(dp_ref, *rest):
        w_hbms = rest[:nr]
        dres_hbm, x_hbm, r_ref, nw_ref, dx_ref, st_ref, w_buf, dres_buf, x_buf, sems, w_sems = rest[nr:]
        m, p = pl.program_id(0), pl.program_id(1)
        tail_in = _row_copies((dres_hbm, x_hbm), (dres_buf, x_buf), sems, m, tm)

        def w_copies(g, slot):
            return [pltpu.make_async_copy(w_hbm.at[g], w_buf.at[slot, pl.ds(0, D), pl.ds(off, width)],
                                          w_sems.at[slot, k])
                    for k, (w_hbm, (off, width)) in enumerate(zip(w_hbms, IN_ROUNDS))]

        @pl.when((p == 0) & (m == 0))
        def _():
            st_ref[...] = jnp.zeros_like(st_ref)
            for cp in w_copies(p, 0):
                cp.start()

        @pl.when((p < NDEV - 1) | (m < nm - 1))
        def _():
            for cp in w_copies((p + 1) % NDEV, (p + 1) % 2):
                cp.start()

        @pl.when(p == 0)
        def _():
            dx_ref[...] = jnp.zeros_like(dx_ref)
            for cp in tail_in:
                cp.start()

        for cp in w_copies(p, p % 2):
            cp.wait()

        @pl.when(p < NDEV - 1)
        def _():
            dx_ref[...] += _dot_nt(dp_ref[...], w_buf[p % 2])

        @pl.when(p == NDEV - 1)
        def _():
            for cp in tail_in:
                cp.wait()
            dnw_sum = jnp.zeros((1, D), F32)
            for lo in range(0, tm, TAIL_ROWS):
                rows = slice(lo, lo + TAIL_ROWS)
                dh = dx_ref[rows, :] + _dot_nt(dp_ref[rows, :], w_buf[p % 2])
                dx, dnw = _rms_bwd_tile(dh, x_buf[rows, :], r_ref[rows, :], nw_ref[...])
                dx_ref[rows, :] = dres_buf[rows, :] + dx
                dnw_sum = dnw_sum + dnw
            st_ref[0:1, :] += dnw_sum

    row = pl.BlockSpec((tm, D), lambda m, p: (m, 0))
    hbm = pl.BlockSpec(memory_space=pl.ANY)
    return pl.pallas_call(
        body, name="in_proj_bwd", grid=(S // tm, NDEV),
        in_specs=[pl.BlockSpec((tm, N_IN), lambda m, p: (m, p)),
                  *[hbm] * nr,
                  hbm, hbm, pl.BlockSpec((tm, 1), lambda m, p: (m, 0)),
                  pl.BlockSpec((1, D), lambda m, p: (0, 0))],
        out_specs=[row, pl.BlockSpec((8, D), lambda m, p: (0, 0))],
        out_shape=[jax.ShapeDtypeStruct((S, D), F32), jax.ShapeDtypeStruct((8, D), F32)],
        scratch_shapes=[pltpu.VMEM((2, D, N_IN), BF16), pltpu.VMEM((tm, D), F32), pltpu.VMEM((tm, D), F32),
                        pltpu.SemaphoreType.DMA((2,)), pltpu.SemaphoreType.DMA((2, nr))],
        compiler_params=_cp(("arbitrary", "arbitrary")),
    )(dproj, *wins, dres, xs, r, nw)


W_IN_PARTS = 2


def _wgrad_in(h1, dproj, part):
    rows = D // W_IN_PARTS

    def body(a_ref, d_ref, o_ref):
        both = _dot_tn(a_ref[...], d_ref[...]).astype(BF16)
        o_ref[0] = both[:, 0:N_IN]
        o_ref[1] = both[:, N_IN:2 * N_IN]

    return pl.pallas_call(
        body, name=f"wgrad_in_{part}", grid=(NDEV // 2,),
        in_specs=[pl.BlockSpec((S, rows), lambda p: (0, part)), pl.BlockSpec((S, 2 * N_IN), lambda p: (0, p))],
        out_specs=pl.BlockSpec((2, rows, N_IN), lambda p: (p, 0, 0)),
        out_shape=jax.ShapeDtypeStruct((NDEV, rows, N_IN), BF16),
        compiler_params=_cp(("parallel",)),
    )(h1, dproj)


def _wgrad_rows(a3, dy, name, col=0):
    def body(a_ref, d_ref, o_ref):
        dw = _dot_tn(a_ref[...], d_ref[...]).astype(BF16)
        for j in range(FF_PER):
            o_ref[j] = dw[j * FF_ROWS:(j + 1) * FF_ROWS]

    return pl.pallas_call(
        body, name=name, grid=(NFG,),
        in_specs=[pl.BlockSpec((None, S, N_FG), lambda p: (p, 0, col)), pl.BlockSpec((S, D), lambda p: (0, 0))],
        out_specs=pl.BlockSpec((FF_PER, FF_ROWS, D), lambda p: (p % 2, p // 2, 0)),
        out_shape=jax.ShapeDtypeStruct((NDEV, N_FF, D), BF16),
        compiler_params=_cp(("parallel",)),
    )(a3, dy)


def _wgrad_out(ma, mr, dx2b):
    half = D // 2
    per = half // N_OUT

    def body(ma_ref, mr_ref, d_ref, o_ref):
        p = pl.program_id(0)

        @pl.when(p == 0)
        def _():
            o_ref[...] = _dot_tn(ma_ref[...], d_ref[...]).astype(BF16).reshape(per, N_OUT, D)

        @pl.when(p == 1)
        def _():
            o_ref[...] = _dot_tn(mr_ref[...], d_ref[...]).astype(BF16).reshape(per, N_OUT, D)

    whole = pl.BlockSpec((S, half), lambda p: (0, 0))
    return pl.pallas_call(
        body, name="wgrad_out", grid=(2,),
        in_specs=[whole, whole, pl.BlockSpec((S, D), lambda p: (0, 0))],
        out_specs=pl.BlockSpec((per, N_OUT, D), lambda p: (p, 0, 0)),
        out_shape=jax.ShapeDtypeStruct((NDEV, N_OUT, D), BF16),
        compiler_params=_cp(("parallel",)),
    )(ma, mr, dx2b)


def _attn_consts():
    c = np.zeros((AH, 8, AHD), np.float32)
    for h in range(AH):
        c[h, :, :] = 2.0 ** (-(h + 1))
    return jnp.asarray(c)


def _permute_in(dst, src, d, cast=None):
    v = src[...]
    if d > 1:
        v = pltpu.einshape("jrc->rjc", v.reshape(S // d, d, AHD)).reshape(S, AHD)
    dst[...] = v if cast is None else v.astype(cast)


def _natural_order(v, d):
    if d == 1:
        return v
    return pltpu.einshape("rjc->jrc", v.reshape(d, S // d, AHD)).reshape(S, AHD)


def _attn_masks():
    qi = lax.broadcasted_iota(jnp.int32, (CH, CH), 0)
    kj = lax.broadcasted_iota(jnp.int32, (CH, CH), 1)
    dist_c = (qi - kj).astype(F32)
    dist_p = (qi - kj + CH).astype(F32)
    return (qi >= kj)[None], (kj >= qi)[None], dist_c[None], dist_p[None]


GB = 16


def _bdot_nt(a, b):
    return lax.dot_general(a, b, (((2,), (2,)), ((0,), (0,))), preferred_element_type=F32)


def _bdot(a, b):
    return lax.dot_general(a, b, (((2,), (1,)), ((0,), (0,))), preferred_element_type=F32)


def _bdot_tn(a, b):
    return lax.dot_general(a, b, (((1,), (1,)), ((0,), (0,))), preferred_element_type=F32)


def _shift_block(dst, src):
    dst[0:CH, :] = jnp.zeros((CH, AHD), dst.dtype)
    dst[CH:S, :] = src[0:S - CH, :]


def _has_prev(g, nb):
    blk = lax.broadcasted_iota(jnp.int32, (GB, 1, 1), 0) + g * GB
    return (blk & (nb - 1)) != 0


def _blocks(ref, g):
    return ref[g * GB * CH:(g + 1) * GB * CH, :].reshape(GB, CH, AHD)


def _attn_fwd(proj):
    scale = 1.0 / math.sqrt(AHD)

    def body(c_ref, q_ref, k_ref, v_ref, o_ref, ob_ref, lse_ref, qkvp_ref, lsep_ref, qd, kd, vd, kps, vps, od, ld, *nat):
        onat, lnat = nat[0:3], nat[3:6]
        slope = c_ref[0:1, :]
        mask_c, mask_p, dist_c, dist_p = _attn_masks()
        for pi, (d, nb) in enumerate(PATTERNS):
            _permute_in(qd, q_ref, d, BF16)
            _permute_in(kd, k_ref, d, BF16)
            _permute_in(vd, v_ref, d, BF16)
            if d > 1:
                qkvp_ref[pi - 1, 0] = qd[...]
                qkvp_ref[pi - 1, 1] = kd[...]
                qkvp_ref[pi - 1, 2] = vd[...]
            if nb > 1:
                _shift_block(kps, kd)
                _shift_block(vps, vd)
            bias_c = -(slope * float(d)) * dist_c
            bias_p = -(slope * float(d)) * dist_p
            for g in range(NB // GB):
                q3, k3, v3 = _blocks(qd, g), _blocks(kd, g), _blocks(vd, g)
                s_c = jnp.where(mask_c, _bdot_nt(q3, k3) * scale + bias_c, NEG)
                mx = jnp.max(s_c, axis=-1, keepdims=True)
                if nb > 1:
                    kp3, vp3 = _blocks(kps, g), _blocks(vps, g)
                    s_p = jnp.where(jnp.logical_and(mask_p, _has_prev(g, nb)),
                                    _bdot_nt(q3, kp3) * scale + bias_p, NEG)
                    mx = jnp.maximum(mx, jnp.max(s_p, axis=-1, keepdims=True))
                    l = (jnp.sum(jnp.exp(s_c - mx), axis=-1, keepdims=True)
                         + jnp.sum(jnp.exp(s_p - mx), axis=-1, keepdims=True))
                    lse = mx + jnp.log(l)
                    o3 = _bdot(jnp.exp(s_c - lse).astype(BF16), v3) + _bdot(jnp.exp(s_p - lse).astype(BF16), vp3)
                else:
                    l = jnp.sum(jnp.exp(s_c - mx), axis=-1, keepdims=True)
                    lse = mx + jnp.log(l)
                    o3 = _bdot(jnp.exp(s_c - lse).astype(BF16), v3)
                rows = slice(g * GB * CH, (g + 1) * GB * CH)
                od[rows, :] = o3.reshape(GB * CH, AHD)
                ld[rows, :] = jnp.broadcast_to(lse, (GB, CH, AHD)).reshape(GB * CH, AHD)
            onat[pi][...] = _natural_order(od[...], d)
            lnat[pi][...] = _natural_order(ld[...], d)
        l0, l1, l2 = lnat[0][...], lnat[1][...], lnat[2][...]
        mx = jnp.maximum(jnp.maximum(l0, l1), l2)
        e0, e1, e2 = jnp.exp(l0 - mx), jnp.exp(l1 - mx), jnp.exp(l2 - mx)
        den = e0 + e1 + e2
        out = (e0 / den) * onat[0][...] + (e1 / den) * onat[1][...] + (e2 / den) * onat[2][...]
        o_ref[...] = out
        ob_ref[...] = out.astype(BF16)
        lse_ref[...] = mx + jnp.log(den)
        for pi, (d, _) in enumerate(PATTERNS[1:]):
            _permute_in(lsep_ref.at[pi], lse_ref, d)

    def col(off):
        return pl.BlockSpec((S, AHD), lambda h: (0, off + h))

    return pl.pallas_call(
        body, name="attn_fwd", grid=(AH,),
        in_specs=[pl.BlockSpec((None, 8, AHD), lambda h: (h, 0, 0)), col(0), col(AH), col(2 * AH)],
        out_specs=[col(0), col(0), col(0), pl.BlockSpec((2, 3, S, AHD), lambda h: (0, 0, 0, h)),
                   pl.BlockSpec((2, S, AHD), lambda h: (0, 0, h))],
        out_shape=[jax.ShapeDtypeStruct((S, AH * AHD), F32), jax.ShapeDtypeStruct((S, AH * AHD), BF16),
                   jax.ShapeDtypeStruct((S, AH * AHD), F32),
                   jax.ShapeDtypeStruct((2, 3, S, AH * AHD), BF16), jax.ShapeDtypeStruct((2, S, AH * AHD), F32)],
        scratch_shapes=[pltpu.VMEM((S, AHD), BF16) for _ in range(5)]
        + [pltpu.VMEM((S, AHD), F32) for _ in range(8)],
        compiler_params=_cp(("parallel",)),
    )(_attn_consts(), proj, proj, proj)


def _attn_bwd(proj, dmixed, o, lse, qkvp, lsep):
    scale = 1.0 / math.sqrt(AHD)

    def body(c_ref, q_ref, k_ref, v_ref, do_ref, o_ref, lse_ref, qkvp_ref, lsep_ref, dproj_hbm,
             qd, kd, vd, dod, kps, vps, dld, dqd, dkd, dvd, delta, aq, ak, av, sq, sk, sv, sems):
        h = pl.program_id(0)

        def out_copies(head):
            return [pltpu.make_async_copy(
                st, dproj_hbm.at[:, pl.ds(pl.multiple_of((k * AH + head) * AHD, AHD), AHD)], sems.at[k])
                for k, st in enumerate((sq, sk, sv))]

        slope = c_ref[0:1, :]
        mask_c, mask_p, dist_c, dist_p = _attn_masks()
        delta[...] = jnp.broadcast_to(jnp.sum(do_ref[...] * o_ref[...], axis=-1, keepdims=True), (S, AHD))
        for pi, (d, nb) in enumerate(PATTERNS):
            if d == 1:
                _permute_in(qd, q_ref, d, BF16)
                _permute_in(kd, k_ref, d, BF16)
                _permute_in(vd, v_ref, d, BF16)
                qs, ks, vs, lss = qd, kd, vd, lse_ref
            else:
                qs, ks, vs, lss = (qkvp_ref.at[pi - 1, 0], qkvp_ref.at[pi - 1, 1], qkvp_ref.at[pi - 1, 2],
                                   lsep_ref.at[pi - 1])
            _permute_in(dod, do_ref, d, BF16)
            _permute_in(dld, delta, d)
            if nb > 1:
                _shift_block(kps, ks)
                _shift_block(vps, vs)
            bias_c = -(slope * float(d)) * dist_c
            bias_p = -(slope * float(d)) * dist_p
            for g in range(NB // GB):
                q3, k3, v3, do3 = _blocks(qs, g), _blocks(ks, g), _blocks(vs, g), _blocks(dod, g)
                ls, dl = _blocks(lss, g), _blocks(dld, g)
                lo, hi = g * GB * CH, (g + 1) * GB * CH
                p_c = jnp.exp(jnp.where(mask_c, _bdot_nt(q3, k3) * scale + bias_c, NEG) - ls)
                ds_c = ((p_c * (_bdot_nt(do3, v3) - dl)) * scale).astype(BF16)
                dq3 = _bdot(ds_c, k3)
                dkd[lo:hi, :] = _bdot_tn(ds_c, q3).reshape(GB * CH, AHD)
                dvd[lo:hi, :] = _bdot_tn(p_c.astype(BF16), do3).reshape(GB * CH, AHD)
                if nb > 1:
                    kp3, vp3 = _blocks(kps, g), _blocks(vps, g)
                    p_p = jnp.exp(jnp.where(jnp.logical_and(mask_p, _has_prev(g, nb)),
                                            _bdot_nt(q3, kp3) * scale + bias_p, NEG) - ls)
                    ds_p = ((p_p * (_bdot_nt(do3, vp3) - dl)) * scale).astype(BF16)
                    dq3 = dq3 + _bdot(ds_p, kp3)
                    dkp = _bdot_tn(ds_p, q3).reshape(GB * CH, AHD)
                    dvp = _bdot_tn(p_p.astype(BF16), do3).reshape(GB * CH, AHD)
                    if g == 0:
                        dkd[0:hi - CH, :] += dkp[CH:, :]
                        dvd[0:hi - CH, :] += dvp[CH:, :]
                    else:
                        dkd[lo - CH:hi - CH, :] += dkp
                        dvd[lo - CH:hi - CH, :] += dvp
                dqd[lo:hi, :] = dq3.reshape(GB * CH, AHD)
            ln = S // d
            for acc, src in ((aq, dqd), (ak, dkd), (av, dvd)):
                if pi == 0:
                    acc[...] = src[...]
                else:
                    acc[...] += _natural_order(src[...], d)

        @pl.when(h > 0)
        def _():
            for cp in out_copies(h - 1):
                cp.wait()

        sq[...] = aq[...].astype(BF16)
        sk[...] = ak[...].astype(BF16)
        sv[...] = av[...].astype(BF16)
        for cp in out_copies(h):
            cp.start()

        @pl.when(h == AH - 1)
        def _():
            for cp in out_copies(h):
                cp.wait()

    def col(off):
        return pl.BlockSpec((S, AHD), lambda h: (0, off + h))

    return pl.pallas_call(
        body, name="attn_bwd", grid=(AH,),
        in_specs=[pl.BlockSpec((None, 8, AHD), lambda h: (h, 0, 0)), col(0), col(AH), col(2 * AH),
                  col(0), col(0), col(0), pl.BlockSpec((2, 3, S, AHD), lambda h: (0, 0, 0, h)),
                  pl.BlockSpec((2, S, AHD), lambda h: (0, 0, h))],
        out_specs=pl.BlockSpec(memory_space=pl.ANY),
        out_shape=jax.ShapeDtypeStruct((S, NDEV * N_IN), BF16),
        scratch_shapes=[pltpu.VMEM((S, AHD), BF16) for _ in range(6)]
        + [pltpu.VMEM((S, AHD), F32) for _ in range(8)]
        + [pltpu.VMEM((S, AHD), BF16) for _ in range(3)] + [pltpu.SemaphoreType.DMA((3,))],
        compiler_params=_cp(("arbitrary",)),
    )(_attn_consts(), proj, proj, proj, dmixed, o, lse, qkvp, lsep)


def _ret_consts():
    c = np.zeros((RH, 8, RHD), np.float32)
    for h in range(RH):
        c[h, :, :] = np.log(np.float32(1.0) - np.float32(2.0 ** (-5.0 - h)))
    return jnp.asarray(c)


def _ret_factors(lg):
    i = lax.broadcasted_iota(jnp.int32, (CH, CH), 0)
    j = lax.broadcasted_iota(jnp.int32, (CH, CH), 1)
    dif = (i - j).astype(F32)
    decay = jnp.where(dif >= 0, jnp.exp(lg[:, 0:CH] * jnp.maximum(dif, 0.0)), 0.0)
    row = lax.broadcasted_iota(jnp.int32, (CH, RHD), 0).astype(F32)
    zeta = jnp.exp(lg * (CH - 1.0 - row))
    xi = jnp.exp(lg * (row + 1.0))
    return decay, zeta, xi, jnp.exp(lg * float(CH))


CBK = 8
RSTEPS = NB // CBK


def _ret_specs(rev):
    off = 3 * AH * AHD // RHD
    rows = CBK * CH

    def ch(n):
        return (RSTEPS - 1 - n) if rev else n

    def col(k):
        return pl.BlockSpec((rows, RHD), lambda h, n: (ch(n), off + k * RH + h))

    own = pl.BlockSpec((rows, RHD), lambda h, n: (ch(n), h))
    state = pl.BlockSpec((None, CBK, RHD, RHD), lambda h, n: (h, ch(n), 0, 0))
    const = pl.BlockSpec((None, 8, RHD), lambda h, n: (h, 0, 0))
    dm = pl.BlockSpec((rows, RHD), lambda h, n: (ch(n), AH * AHD // RHD + h))
    return col, own, state, const, dm


def _chunks(x):
    return x.reshape(CBK, CH, RHD)


def _ret_fwd(proj):
    def body(c_ref, q_ref, k_ref, v_ref, g_ref, ret_ref, mr_ref, st_ref, r_acc):
        n = pl.program_id(1)

        @pl.when(n == 0)
        def _():
            r_acc[...] = jnp.zeros_like(r_acc)

        decay, zeta, xi, gch = _ret_factors(c_ref[0:1, :])
        q3 = _chunks(q_ref[...].astype(BF16))
        kc = _chunks(k_ref[...] * (1.0 / math.sqrt(RHD)))
        k3 = kc.astype(BF16)
        v3 = _chunks(v_ref[...].astype(BF16))
        kv3 = _bdot_tn((kc * zeta[None]).astype(BF16), v3)
        r = r_acc[...]
        for i in range(CBK):
            st_ref[i] = r.astype(BF16)
            r = r * gch + kv3[i]
        r_acc[...] = r
        scores = _bdot_nt(q3, k3) * decay[None]
        ret = (_bdot(scores.astype(BF16), v3) + _bdot(q3, st_ref[...]) * xi[None]).reshape(CBK * CH, RHD)
        ret_ref[...] = ret
        rr = lax.rsqrt(jnp.mean(ret * ret, axis=-1, keepdims=True) + EPS)
        gv = g_ref[...]
        mr_ref[...] = ((gv * _sigmoid(gv)) * (ret * rr)).astype(BF16)

    col, own, state, const, _ = _ret_specs(False)
    return pl.pallas_call(
        body, name="ret_fwd", grid=(RH, RSTEPS),
        in_specs=[const, col(0), col(1), col(2), col(3)],
        out_specs=[own, own, state],
        out_shape=[jax.ShapeDtypeStruct((S, RH * RHD), F32), jax.ShapeDtypeStruct((S, RH * RHD), BF16),
                   jax.ShapeDtypeStruct((RH, NB, RHD, RHD), BF16)],
        scratch_shapes=[pltpu.VMEM((RHD, RHD), F32)],
        compiler_params=_cp(("parallel", "arbitrary")),
    )(_ret_consts(), proj, proj, proj, proj)


def _ret_bwd(proj, ret, states, dmixed, dproj):
    rows = CBK * CH
    col0 = 3 * AH * AHD

    def body(c_ref, q_ref, k_ref, v_ref, g_ref, ret_ref, st_ref, dm_ref, dproj_in, dproj_hbm, g_acc, gs,
             sq, sk, sv, sg, sems):
        del dproj_in
        h, n = pl.program_id(0), pl.program_id(1)
        step = h * RSTEPS + n

        def out_copies(t):
            hh, nn = t // RSTEPS, t % RSTEPS
            r0 = pl.multiple_of((RSTEPS - 1 - nn) * rows, rows)
            return [pltpu.make_async_copy(
                st, dproj_hbm.at[pl.ds(r0, rows), pl.ds(pl.multiple_of(col0 + (k * RH + hh) * RHD, RHD), RHD)],
                sems.at[k]) for k, st in enumerate((sq, sk, sv, sg))]

        @pl.when(n == 0)
        def _():
            g_acc[...] = jnp.zeros_like(g_acc)

        decay, zeta, xi, gch = _ret_factors(c_ref[0:1, :])
        ret_v = ret_ref[...]
        rr = lax.rsqrt(jnp.mean(ret_v * ret_v, axis=-1, keepdims=True) + EPS)
        gv = g_ref[...]
        sgm = _sigmoid(gv)
        dmix = dm_ref[...]
        dgate = ((dmix * (ret_v * rr)) * (sgm * (1.0 + gv * (1.0 - sgm)))).astype(BF16)
        dretn = dmix * (gv * sgm)
        dret = _chunks(rr * dretn - ret_v * ((rr * rr * rr) * jnp.mean(dretn * ret_v, axis=-1, keepdims=True)))

        q3 = _chunks(q_ref[...].astype(BF16))
        kc = _chunks(k_ref[...] * (1.0 / math.sqrt(RHD)))
        k3 = kc.astype(BF16)
        v3 = _chunks(v_ref[...].astype(BF16))
        d3 = dret.astype(BF16)
        dxi = (dret * xi[None]).astype(BF16)
        kz = (kc * zeta[None]).astype(BF16)
        dr3 = _bdot_tn(q3, dxi)
        acc = g_acc[...]
        for i in reversed(range(CBK)):
            gs[i] = acc.astype(BF16)
            acc = dr3[i] + gch * acc
        g_acc[...] = acc
        g3 = gs[...]
        sc = (_bdot_nt(q3, k3) * decay[None]).astype(BF16)
        da = (_bdot_nt(d3, v3) * decay[None]).astype(BF16)
        dq = _bdot(da, k3) + _bdot_nt(dxi, st_ref[...])
        dkc = _bdot_tn(da, q3) + _bdot_nt(v3, g3) * zeta[None]
        dv = _bdot_tn(sc, d3) + _bdot(kz, g3)

        @pl.when(step > 0)
        def _():
            for cp in out_copies(step - 1):
                cp.wait()

        sq[...] = dq.reshape(rows, RHD).astype(BF16)
        sk[...] = (dkc * (1.0 / math.sqrt(RHD))).reshape(rows, RHD).astype(BF16)
        sv[...] = dv.reshape(rows, RHD).astype(BF16)
        sg[...] = dgate
        for cp in out_copies(step):
            cp.start()

        @pl.when(step == RH * RSTEPS - 1)
        def _():
            for cp in out_copies(step):
                cp.wait()

    col, own, state, const, dm = _ret_specs(True)
    hbm = pl.BlockSpec(memory_space=pl.ANY)
    return pl.pallas_call(
        body, name="ret_bwd", grid=(RH, RSTEPS),
        in_specs=[const, col(0), col(1), col(2), col(3), own, state, dm, hbm],
        out_specs=hbm,
        out_shape=jax.ShapeDtypeStruct(dproj.shape, dproj.dtype),
        input_output_aliases={8: 0},
        scratch_shapes=[pltpu.VMEM((RHD, RHD), F32), pltpu.VMEM((CBK, RHD, RHD), BF16)]
        + [pltpu.VMEM((rows, RHD), BF16) for _ in range(4)] + [pltpu.SemaphoreType.DMA((4,))],
        compiler_params=_cp(("arbitrary", "arbitrary")),
    )(_ret_consts(), proj, proj, proj, proj, ret, states, dmixed, dproj)


class _NoReduction:
    def start(self, group, grads):
        pass

    def local(self, name, first=()):
        return []

    def landed(self, name):
        return []

    def update(self, name):
        return []

    place = None

    def rider(self, name):
        return None

    def set_update(self, name, outs):
        pass


def _local_step(x, tgt, nw1, nw2, nw3, win, wout, wgu_a, wgu_b, wd_a, wd_b, red=None):
    red = red or _NoReduction()

    def after(values, first):
        return lax.optimization_barrier((tuple(values), tuple(first)))[0]

    h1, r1 = _rms_fwd(x, nw1)
    proj = _proj(h1, win)
    o, ma, lse, qkvp, lsep = _attn_fwd(proj)
    ret, mr, states = _ret_fwd(proj)
    x2, h2, r2 = _out_proj_rms(x, ma, mr, wout, nw2)
    a, dadg, dadu = _ffn_up(h2, wgu_b, 1, _ffn_up(h2, wgu_a, 0))
    dx3, dx3b, st3 = _ffn_down_loss(_ffn_down_first(x2, a, wd_a), a, wd_b, nw3, tgt)

    dwd = _wgrad_rows(a, dx3b, "wgrad_down")
    red.start(["w_down"], [dwd])
    (dx3b,) = after([dx3b], [dwd])
    part = _ffn_down_bwd(dx3b, wd_a, dadg, dadu, 0)
    (dx3b,) = after([dx3b], red.local("w_down", first=[part]))
    dgu = _ffn_down_bwd(dx3b, wd_b, dadg, dadu, 1, [part])
    dwg = _wgrad_rows(dgu, h2, "wgrad_gate", 0)
    red.start(["w_gate"], [dwg])
    (dgu,) = after([dgu], [dwg])
    dwu = _wgrad_rows(dgu, h2, "wgrad_up", 1)
    red.start(["w_up"], [dwu])
    (dgu,) = after([dgu], red.local("w_gate", first=[dwu] + red.landed("w_down")))
    dx2, dx2b, st2 = _ffn_up_bwd(dgu, wgu_a, wgu_b, dx3, x2, r2, nw2)
    (dx2b,) = after([dx2b], red.local("w_up", first=[dx2b]))
    dwo = _wgrad_out(ma, mr, dx2b)
    red.start(["w_out"], [dwo])
    (dx2b,) = after([dx2b], [dwo])
    dmixed, done = _out_proj_bwd(dx2b, wout, red.place, red.rider("w_down"))
    red.set_update("w_down", done)
    dproj = _attn_bwd(proj, dmixed, o, lse, qkvp, lsep)
    (dmixed,) = after([dmixed], red.local("w_out", first=[dproj] + red.landed("w_gate")))
    dproj = _ret_bwd(proj, ret, states, dmixed, dproj)
    (dwi0,) = after([_wgrad_in(h1, dproj, 0)], red.landed("w_up"))
    red.start(["w_in_0"], [dwi0])
    (dproj,) = after([dproj], [dwi0])
    dwi1 = _wgrad_in(h1, dproj, 1)
    red.start(["w_in_1"], [dwi1])
    sums = red.local("w_in_0", first=[dwi1] + red.landed("w_out"))
    sums = red.local("w_in_1", first=sums + red.update("w_gate"))
    (dproj,) = after([dproj], sums)
    gx, st1 = _in_proj_bwd(dproj, win, dx2, x, r1, nw1)
    dwi = jnp.concatenate([dwi0, dwi1], axis=1)
    stats = jnp.concatenate([st1[0:1], st2[0:1], st3[0:2], jnp.zeros((4, D), F32)], axis=0)
    return stats, gx, dwi, dwo, dwg, dwu, dwd


def _place():
    x, y, c = lax.axis_index("x"), lax.axis_index("y"), lax.axis_index("c")
    return x, y, c, [(1 - x, y), (x, 1 - y), (1 - x, 1 - y)]


def _handshake(peers):
    barrier = pltpu.get_barrier_semaphore()
    for peer in peers:
        pl.semaphore_signal(barrier, inc=1, device_id=peer, device_id_type=MESH)
    pl.semaphore_wait(barrier, len(peers))


def _all_gather(shards, name, collective_id, per=0, rows=None):
    na = len(shards)
    nout = 1 if per else na
    lo, r = rows or (0, shards[0].shape[0])
    ngroups = NDEV // per if per else 0
    SIB, XN0, XN1, YN1, YN0, VIA_X, VIA_Y = 0, 1, 2, 3, 4, 5, 6
    D2D = {XN0: 7, XN1: 8, YN1: 9, YN0: 10, VIA_X: 11, VIA_Y: 12}

    def body(*refs):
        ins, outs = [ref.at[pl.ds(lo, r)] for ref in refs[:na]], refs[na:na + nout]
        send_sems, recv_sems, local_sems = refs[na + nout:]
        x, y, c, _ = _place()
        me, sib = (x, y, c), (x, y, 1 - c)
        xn, yn, dg = (1 - x, y, c), (x, 1 - y, c), (1 - x, 1 - y, c)
        _handshake([sib, xn, yn])

        def part(ref, h):
            rows = ref.shape[0] // 2
            return ref if h is None else ref.at[pl.ds(h * rows, rows)]

        def block(a, owner, h):
            idx = 4 * owner[0] + 2 * owner[1] + owner[2]
            if not per:
                return part(outs[a].at[idx], h)
            return part(outs[0].at[idx // per, a, pl.ds(pl.multiple_of((idx % per) * r, r), r)], h)

        def copy(a, k, owner, h, to, own_src=False):
            return pltpu.make_async_remote_copy(
                src_ref=part(ins[a], h) if own_src else block(a, owner, h), dst_ref=block(a, owner, h),
                send_sem=send_sems.at[a, k], recv_sem=recv_sems.at[a, k], device_id=to, device_id_type=MESH)

        def other(p):
            return (p[0], p[1], 1 - c)

        mine = [pltpu.make_async_copy(ins[a], block(a, me, None), local_sems.at[a]) for a in range(na)]
        for cp in mine:
            cp.start()
        sent = []
        for a in range(na):
            sent += [copy(a, XN0, me, 0, xn, True), copy(a, YN1, me, 1, yn, True),
                     copy(a, XN1, me, 1, xn, True), copy(a, YN0, me, 0, yn, True)]
        sent += [copy(a, SIB, me, None, sib, True) for a in range(na)]
        for cp in sent:
            cp.start()

        def landed(a, k, owner, h, then):
            copy(a, k, owner, h, me).wait_recv()
            for k2, to in then + [(D2D[k], sib)]:
                cp = copy(a, k2, owner, h, to)
                cp.start()
                sent.append(cp)

        for a in range(na):
            landed(a, XN0, xn, 0, [(VIA_Y, yn)])
            landed(a, YN1, yn, 1, [(VIA_X, xn)])
            landed(a, XN1, xn, 1, [])
            landed(a, YN0, yn, 0, [])
        for a in range(na):
            landed(a, VIA_Y, dg, 0, [])
            landed(a, VIA_X, dg, 1, [])
        for a in range(na):
            copy(a, SIB, sib, None, me).wait_recv()
            for k, owner, h in ((XN0, xn, 0), (XN1, xn, 1), (YN1, yn, 1), (YN0, yn, 0), (VIA_Y, dg, 0), (VIA_X, dg, 1)):
                copy(a, D2D[k], other(owner), h, me).wait_recv()
        for cp in sent:
            cp.wait_send()
        for cp in mine:
            cp.wait()

    if per:
        out_type = [jax.ShapeDtypeStruct((ngroups, na, per * r, shards[0].shape[1]), shards[0].dtype)]
    else:
        out_type = [jax.ShapeDtypeStruct((NDEV,) + s.shape, s.dtype) for s in shards]
    return _sequencer_call(
        body, name, collective_id, out_type,
        [pltpu.SemaphoreType.DMA((na, 13)), pltpu.SemaphoreType.DMA((na, 13)), pltpu.SemaphoreType.DMA((na,))])(*shards)


def _sequencer_call(body, name, collective_id, out_type, scratch_types):
    return pl.kernel(
        body, name=name, out_type=out_type,
        mesh=plsc.ScalarSubcoreMesh(axis_name="sequencer", num_cores=1),
        scratch_types=scratch_types,
        compiler_params=pltpu.CompilerParams(collective_id=collective_id))


def _exchange_sibling(grads, name, collective_id):
    na = len(grads)

    def body(*refs):
        ins, outs = refs[:na], refs[na:2 * na]
        send_sems, recv_sems = refs[2 * na:]
        x, y, c, _ = _place()
        _handshake([(x, y, 1 - c)])
        cps = []
        for a in range(na):
            for k in range(4):
                cps.append(pltpu.make_async_remote_copy(
                    src_ref=ins[a].at[2 * k + (1 - c)], dst_ref=outs[a].at[k],
                    send_sem=send_sems.at[a, k], recv_sem=recv_sems.at[a, k],
                    device_id=(x, y, 1 - c), device_id_type=MESH))
        for cp in cps:
            cp.start()
        for cp in cps:
            cp.wait()

    return _sequencer_call(
        body, name, collective_id,
        [jax.ShapeDtypeStruct((4,) + g.shape[1:], g.dtype) for g in grads],
        [pltpu.SemaphoreType.DMA((na, 4)), pltpu.SemaphoreType.DMA((na, 4))])(*grads)


def _row_tile(rows, cols):
    for t in (512, 256, 176, 128, 64, 32, 16):
        if rows % t == 0 and t * cols * 4 <= (2 << 20):
            return t
    raise ValueError((rows, cols))


STREAM_BUFS = 3


def _stream_tile(rows, steps):
    for t in (512, 256, 176, 128, 64, 32, 16):
        if rows % t == 0 and rows // t >= steps:
            return t
    raise ValueError((rows, steps))


def _stream(n, loads, stores, compute):
    for k in range(min(STREAM_BUFS, n)):
        for cp in loads(k):
            cp.start()
    for k in range(n):
        for cp in loads(k):
            cp.wait()
        if k >= 2:
            for cp in stores(k - 2):
                cp.wait()
        compute(k)
        for cp in stores(k):
            cp.start()
        if k + STREAM_BUFS < n:
            for cp in loads(k + STREAM_BUFS):
                cp.start()
    for k in range(max(n - 2, 0), n):
        for cp in stores(k):
            cp.wait()


def _chip_sum(place, g, got, name):
    _, r, c = g.shape
    tm = _stream_tile(r, 4)
    nt = r // tm

    def body(pos_ref, g_hbm, got_hbm, o_hbm, g_buf, s_buf, o_buf, sem_in, sem_out):
        def chip(j):
            return 2 * (pos_ref[0] ^ (0 if j == 1 else 1)) + (pos_ref[1] ^ (0 if j == 0 else 1))

        def loads(k):
            j, rows, slot = k // nt, pl.ds((k % nt) * tm, tm), k % STREAM_BUFS
            return [pltpu.make_async_copy(g_hbm.at[2 * chip(j) + pos_ref[2], rows], g_buf.at[slot], sem_in.at[slot, 0]),
                    pltpu.make_async_copy(got_hbm.at[chip(j), rows], s_buf.at[slot], sem_in.at[slot, 1])]

        def stores(k):
            return [pltpu.make_async_copy(o_buf.at[k % 2], o_hbm.at[k // nt, pl.ds((k % nt) * tm, tm)],
                                          sem_out.at[k % 2])]

        def compute(k):
            slot = k % STREAM_BUFS
            o_buf[k % 2] = (g_buf[slot].astype(F32) + s_buf[slot].astype(F32)).astype(BF16)

        _stream(3 * nt, loads, stores, compute)

    hbm = pl.BlockSpec(memory_space=pl.ANY)
    return pl.pallas_call(
        body, name=name,
        grid_spec=pltpu.PrefetchScalarGridSpec(
            num_scalar_prefetch=1, grid=(1,), in_specs=[hbm, hbm], out_specs=hbm,
            scratch_shapes=[pltpu.VMEM((STREAM_BUFS, tm, c), BF16), pltpu.VMEM((STREAM_BUFS, tm, c), BF16),
                            pltpu.VMEM((2, tm, c), BF16),
                            pltpu.SemaphoreType.DMA((STREAM_BUFS, 2)), pltpu.SemaphoreType.DMA((2,))]),
        out_shape=jax.ShapeDtypeStruct((3, r, c), BF16),
        compiler_params=_cp(("arbitrary",)),
    )(place, g, got)


def _exchange_chips(sums, name, collective_id):
    na = len(sums)

    def body(*refs):
        ins, outs = refs[:na], refs[na:2 * na]
        send_sems, recv_sems = refs[2 * na:]
        x, y, c, chips = _place()
        _handshake([(*chip, c) for chip in chips])
        cps = []
        for a in range(na):
            for j, chip in enumerate(chips):
                cps.append(pltpu.make_async_remote_copy(
                    src_ref=ins[a].at[j], dst_ref=outs[a].at[j],
                    send_sem=send_sems.at[a, j], recv_sem=recv_sems.at[a, j],
                    device_id=(*chip, c), device_id_type=MESH))
        for cp in cps:
            cp.start()
        for cp in cps:
            cp.wait()

    return _sequencer_call(
        body, name, collective_id,
        [jax.ShapeDtypeStruct((3,) + s.shape[1:], s.dtype) for s in sums],
        [pltpu.SemaphoreType.DMA((na, 3)), pltpu.SemaphoreType.DMA((na, 3))])(*sums)


def _exchange_stats(stats, collective_id):
    def body(st_in, st_out, st_send, st_recv, local_sem):
        x, y, c, _ = _place()
        me_idx = 4 * x + 2 * y + c
        peers = [(x ^ ((k >> 2) & 1), y ^ ((k >> 1) & 1), c ^ (k & 1)) for k in range(1, 8)]
        _handshake(peers)
        mine = pltpu.make_async_copy(st_in, st_out.at[me_idx], local_sem)
        mine.start()
        cps = [pltpu.make_async_remote_copy(
            src_ref=st_in, dst_ref=st_out.at[me_idx], send_sem=st_send.at[k], recv_sem=st_recv.at[k],
            device_id=peer, device_id_type=MESH) for k, peer in enumerate(peers)]
        for cp in cps:
            cp.start()
        for cp in cps:
            cp.wait()
        mine.wait()

    return _sequencer_call(
        body, "exchange_stats", collective_id,
        jax.ShapeDtypeStruct((NDEV,) + stats.shape, stats.dtype),
        [pltpu.SemaphoreType.DMA((7,)), pltpu.SemaphoreType.DMA((7,)), pltpu.SemaphoreType.DMA])(stats)


class _Reduction:
    def __init__(self, place, first_collective_id, state):
        self.place = place
        self.ids = iter(range(first_collective_id, 32))
        self.state = state
        self.groups = {}
        self.updates = {}

    def next_id(self):
        return next(self.ids)

    def start(self, group, grads):
        got = _exchange_sibling(grads, "sibling_exchange_" + group[0], self.next_id())
        self.groups[group[0]] = dict(names=group, grads=grads, got=got)

    def local(self, name, first=()):
        grp = self.groups[name]
        grads = lax.optimization_barrier((tuple(grp["grads"]), tuple(first)))[0]
        grp["sums"] = [_chip_sum(self.place, g, s, "chip_sum_" + n)
                       for g, s, n in zip(grads, grp["got"], grp["names"])]
        grp["chips"] = _exchange_chips(grp["sums"], "chip_exchange_" + name, self.next_id())
        return grp["sums"]

    def landed(self, name):
        return list(self.groups[name]["chips"])

    def rider(self, name):
        grp = next(g for g in self.groups.values() if name in g["names"])
        k = grp["names"].index(name)
        return self.state[name][:3] + (grp["grads"][k], grp["got"][k], grp["chips"][k])

    def set_update(self, name, outs):
        self.updates[name] = list(outs)

    def update(self, name):
        if name not in self.updates:
            grp = next(g for g in self.groups.values() if name in g["names"])
            k = grp["names"].index(name)
            w, m, v, part, parts = self.state[name]
            before = self.update(f"{name[:-1]}{part - 1}") if part else None
            self.updates[name] = _shard_update(self.place, w, m, v, grp["grads"][k], grp["got"][k],
                                               grp["chips"][k], "update_" + name, part, parts, before)
        return list(self.updates[name])


def _adamw(w, g, m, v):
    m = ADAM_B1 * m + (1.0 - ADAM_B1) * g
    v = ADAM_B2 * v + (1.0 - ADAM_B2) * (g * g)
    m_hat = m / (1.0 - ADAM_B1 ** ADAM_STEP)
    v_hat = v / (1.0 - ADAM_B2 ** ADAM_STEP)
    delta = -ADAM_LR * (m_hat / (jnp.sqrt(v_hat) + ADAM_EPS) + ADAM_WD * w)
    return delta, m, v


def _update_tile(w_ref, m_ref, v_ref, g_ref, s_ref, c_ref, go_ref, d_ref, mo_ref, vo_ref):
    grad = g_ref[...].astype(F32) + s_ref[...].astype(F32)
    for j in range(3):
        grad = grad + c_ref[j].astype(F32)
    delta, mn, vn = _adamw(w_ref[...], grad, m_ref[...], v_ref[...])
    go_ref[...] = grad
    d_ref[...] = delta
    mo_ref[...] = mn
    vo_ref[...] = vn


def _shard_update(place, w, m, v, g, got_sib, got_chips, name, part=0, parts=1, before=None):
    r, c = w.shape
    rp = r // parts
    tm = _stream_tile(rp, 8)
    nt = rp // tm
    before = list(before or [])

    def body(pos_ref, w_hbm, m_hbm, v_hbm, g_hbm, s_hbm, c_hbm, *rest):
        outs = rest[len(before):len(before) + 4]
        w_buf, m_buf, v_buf, g_buf, s_buf, c_buf, o_buf, sem_in, sem_out = rest[len(before) + 4:]
        own = 4 * pos_ref[0] + 2 * pos_ref[1] + pos_ref[2]
        chip = 2 * pos_ref[0] + pos_ref[1]

        def loads(k):
            slot, rows, mine = k % STREAM_BUFS, pl.ds(k * tm, tm), pl.ds(part * rp + k * tm, tm)
            pairs = [(w_hbm.at[mine], w_buf), (m_hbm.at[mine], m_buf), (v_hbm.at[mine], v_buf),
                     (g_hbm.at[own, rows], g_buf), (s_hbm.at[chip, rows], s_buf), (c_hbm.at[:, rows], c_buf)]
            return [pltpu.make_async_copy(src, buf.at[slot], sem_in.at[slot, n]) for n, (src, buf) in enumerate(pairs)]

        def stores(k):
            mine = pl.ds(part * rp + k * tm, tm)
            return [pltpu.make_async_copy(o_buf.at[k % 2, n], out.at[mine], sem_out.at[k % 2, n])
                    for n, out in enumerate(outs)]

        def compute(k):
            slot = k % STREAM_BUFS
            _update_tile(w_buf.at[slot], m_buf.at[slot], v_buf.at[slot], g_buf.at[slot], s_buf.at[slot],
                         c_buf.at[slot], *[o_buf.at[k % 2, n] for n in range(4)])

        _stream(nt, loads, stores, compute)

    hbm = pl.BlockSpec(memory_space=pl.ANY)
    return pl.pallas_call(
        body, name=name,
        grid_spec=pltpu.PrefetchScalarGridSpec(
            num_scalar_prefetch=1, grid=(1,), in_specs=[hbm] * (6 + len(before)), out_specs=[hbm] * 4,
            scratch_shapes=[pltpu.VMEM((STREAM_BUFS, tm, c), F32)] * 3 + [pltpu.VMEM((STREAM_BUFS, tm, c), BF16)] * 2
            + [pltpu.VMEM((STREAM_BUFS, 3, tm, c), BF16), pltpu.VMEM((2, 4, tm, c), F32),
               pltpu.SemaphoreType.DMA((STREAM_BUFS, 6)), pltpu.SemaphoreType.DMA((2, 4))]),
        out_shape=[jax.ShapeDtypeStruct((r, c), F32)] * 4,
        input_output_aliases={7 + k: k for k in range(len(before))},
        compiler_params=_cp(("arbitrary",)),
    )(place, w, m, v, g, got_sib, got_chips, *before)


def _small_update(stats_all, ws, ms, vs):
    def body(st_ref, w_ref, m_ref, v_ref, go_ref, d_ref, mo_ref, vo_ref):
        grad = st_ref[0]
        for k in range(1, NDEV):
            grad = grad + st_ref[k]
        delta, mn, vn = _adamw(w_ref[...], grad, m_ref[...], v_ref[...])
        go_ref[...] = grad
        d_ref[...] = delta
        mo_ref[...] = mn
        vo_ref[...] = vn

    return pl.pallas_call(
        body, name="small_update",
        out_shape=[jax.ShapeDtypeStruct((8, D), F32)] * 4,
        compiler_params=_cp(),
    )(stats_all, ws, ms, vs)


def kernel(x, norm_mix_w, w_in, w_out, norm_ffn_w, w_gate, w_up, w_down, norm_final_w, loss_target, m_norm_mix_w, m_w_in, m_w_out, m_norm_ffn_w, m_w_gate, m_w_up, m_w_down, m_norm_final_w, v_norm_mix_w, v_w_in, v_w_out, v_norm_ffn_w, v_w_gate, v_w_up, v_w_down, v_norm_final_w):
    tr = {"w_gate", "w_up"}
    names = ["w_in", "w_out", "w_gate", "w_up", "w_down"]

    def view(a, n):
        return a[0].T if n in tr else a[0]

    big_w = [view(a, n) for a, n in zip([w_in, w_out, w_gate, w_up, w_down], names)]
    big_m = [view(a, n) for a, n in zip([m_w_in, m_w_out, m_w_gate, m_w_up, m_w_down], names)]
    big_v = [view(a, n) for a, n in zip([v_w_in, v_w_out, v_w_gate, v_w_up, v_w_down], names)]

    shards = [None] + [_cast_bf16(w, "cast_" + n) for w, n in zip(big_w[1:], names[1:])]
    win = [_all_gather([cols], f"all_gather_w_in_{k}", 1 + k)[0]
           for k, cols in enumerate(_cast_cols(big_w[0], "cast_w_in"))]
    (wout,) = _all_gather(shards[1:2], "all_gather_w_out", 3)
    (wgu_a,) = _all_gather(shards[2:4], "all_gather_gate_up_0", 4, per=FF_PER, rows=(0, FF_ROWS))
    (wgu_b,) = _all_gather(shards[2:4], "all_gather_gate_up_1", 5, per=FF_PER, rows=(FF_ROWS, FF_ROWS))
    (wd_a,) = _all_gather(shards[4:5], "all_gather_w_down_0", 6, per=FF_PER, rows=(0, FF_ROWS))
    (wd_b,) = _all_gather(shards[4:5], "all_gather_w_down_1", 7, per=FF_PER, rows=(FF_ROWS, FF_ROWS))
    nw3 = norm_final_w.reshape(1, D)
    place = jnp.stack([lax.axis_index("x"), lax.axis_index("y"), lax.axis_index("c")]).astype(jnp.int32)
    state = {n: (w, m, v, 0, 1) for n, w, m, v in zip(names, big_w, big_m, big_v)}
    for part in range(W_IN_PARTS):
        state[f"w_in_{part}"] = state["w_in"][:3] + (part, W_IN_PARTS)
    red = _Reduction(place, 8, state)
    stats, gx, *_ = _local_step(
        x[0], loss_target[0], norm_mix_w, norm_ffn_w, nw3, win, wout.reshape(D, D),
        wgu_a.reshape(NFG // 2, 2 * N_FG, D), wgu_b.reshape(NFG // 2, 2 * N_FG, D),
        wd_a.reshape(NFG // 2, N_FG, D), wd_b.reshape(NFG // 2, N_FG, D), red)
    stats_all = _exchange_stats(stats, red.next_id())
    upd = [red.update(f"w_in_{W_IN_PARTS - 1}" if n == "w_in" else n) for n in names]
    stats_all = lax.optimization_barrier((stats_all, tuple(upd[0])))[0]

    def rows(a, b, c):
        return jnp.concatenate([a.reshape(1, D), b.reshape(1, D), c.reshape(1, D), jnp.zeros((5, D), F32)], axis=0)

    sg, sd, sm, sv = _small_update(stats_all, rows(norm_mix_w, norm_ffn_w, norm_final_w),
                                   rows(m_norm_mix_w, m_norm_ffn_w, m_norm_final_w),
                                   rows(v_norm_mix_w, v_norm_ffn_w, v_norm_final_w))
    loss = sg[3, 0]

    def outs(k, small):
        big = [(u[k].T if n in tr else u[k])[None] for u, n in zip(upd, names)]
        return [small[0:1], big[0], big[1], small[1:2], big[2], big[3], big[4], small[2]]

    return (loss, gx[None], *outs(0, sg), *outs(1, sd), *outs(2, sm), *outs(3, sv))
```

```python
import math

import numpy as np
import jax
import jax.numpy as jnp
from jax import lax
from jax.experimental import pallas as pl
from jax.experimental.pallas import tpu as pltpu
from jax.experimental.pallas import tpu_sc as plsc

F32 = jnp.float32
BF16 = jnp.bfloat16

S = 2048
D = 2048
NDEV = 8
N_IN = 7168 // NDEV
N_FF = 5632 // NDEV
NFG, N_FG = NDEV // 2, 2 * N_FF
FF_PER, FF_ROWS = 4, N_FF // 2
TAIL_ROWS = 256
IN_ROUNDS = ((0, 512), (512, N_IN - 512))
N_OUT = 2048 // NDEV
AH, AHD = 8, 128
RH, RHD = 4, 256
CH = 128
NB = S // CH
EPS = 1e-6
PATTERNS = ((1, 16), (4, 4), (16, 1))
NEG = -1e30
VMEM_LIMIT = 56 * 1024 * 1024

ADAM_LR, ADAM_B1, ADAM_B2, ADAM_EPS, ADAM_WD, ADAM_STEP = 0.001, 0.9, 0.999, 1e-08, 0.01, 10
MESH = pl.DeviceIdType.MESH


def _cp(sem=None):
    return pltpu.CompilerParams(dimension_semantics=sem, vmem_limit_bytes=VMEM_LIMIT)


def _dot(a, b):
    return jnp.dot(a, b, preferred_element_type=F32)


def _dot_nt(a, b):
    return lax.dot_general(a, b, (((1,), (1,)), ((), ())), preferred_element_type=F32)


def _dot_tn(a, b):
    return lax.dot_general(a, b, (((0,), (0,)), ((), ())), preferred_element_type=F32)


def _sigmoid(x):
    return 0.5 * jnp.tanh(0.5 * x) + 0.5


def _cast_bf16(w, name):
    r, c = w.shape
    tm = r if r <= 1024 else 512

    def body(w_ref, o_ref):
        o_ref[...] = w_ref[...].astype(BF16)

    return pl.pallas_call(
        body, name=name, grid=(r // tm,),
        in_specs=[pl.BlockSpec((tm, c), lambda i: (i, 0))],
        out_specs=pl.BlockSpec((tm, c), lambda i: (i, 0)),
        out_shape=jax.ShapeDtypeStruct((r, c), BF16),
        compiler_params=_cp(("parallel",)),
    )(w)


def _rms_fwd(x, nw):
    tm = 256

    def body(x_ref, w_ref, h_ref, r_ref):
        xs = x_ref[...]
        r = lax.rsqrt(jnp.mean(xs * xs, axis=-1, keepdims=True) + EPS)
        h_ref[...] = ((xs * r) * w_ref[...]).astype(BF16)
        r_ref[...] = r

    return pl.pallas_call(
        body, name="rms_fwd", grid=(S // tm,),
        in_specs=[pl.BlockSpec((tm, D), lambda i: (i, 0)), pl.BlockSpec((1, D), lambda i: (0, 0))],
        out_specs=[pl.BlockSpec((tm, D), lambda i: (i, 0)), pl.BlockSpec((tm, 1), lambda i: (i, 0))],
        out_shape=[jax.ShapeDtypeStruct((S, D), BF16), jax.ShapeDtypeStruct((S, 1), F32)],
        compiler_params=_cp(("parallel",)),
    )(x, nw)


def _row_copies(hbm_refs, bufs, sems, m, tm):
    rows = pl.ds(pl.multiple_of(m * tm, tm), tm)
    return [pltpu.make_async_copy(h.at[rows], b, sems.at[i]) for i, (h, b) in enumerate(zip(hbm_refs, bufs))]


def _rms_bwd_tile(dh, xs, r, nw):
    dnw = jnp.sum(dh * (xs * r), axis=0, keepdims=True)
    gy = dh * nw
    dx = r * gy - xs * ((r * r * r) * jnp.mean(gy * xs, axis=-1, keepdims=True))
    return dx, dnw


def _cast_cols(w, name):
    r, c = w.shape
    tm = 512

    def body(w_ref, *o_refs):
        for o_ref, (off, width) in zip(o_refs, IN_ROUNDS):
            o_ref[...] = w_ref[:, off:off + width].astype(BF16)

    return pl.pallas_call(
        body, name=name, grid=(r // tm,),
        in_specs=[pl.BlockSpec((tm, c), lambda i: (i, 0))],
        out_specs=[pl.BlockSpec((tm, width), lambda i: (i, 0)) for _, width in IN_ROUNDS],
        out_shape=[jax.ShapeDtypeStruct((r, width), BF16) for _, width in IN_ROUNDS],
        compiler_params=_cp(("parallel",)),
    )(w)


def _proj_round(h1, win, k, before):
    tm = 1024
    nm = S // tm
    off, width = IN_ROUNDS[k]
    before = [] if before is None else [before]

    def body(a_ref, w_ref, *rest):
        o_hbm, o_buf, sems = rest[-3:]
        p, m = pl.program_id(0), pl.program_id(1)
        t = p * nm + m

        def out_copy(pp, mm, slot):
            cols = pl.ds(pl.multiple_of(pp * N_IN + off, 128), width)
            return pltpu.make_async_copy(o_buf.at[slot], o_hbm.at[pl.ds(pl.multiple_of(mm * tm, tm), tm), cols],
                                         sems.at[slot])

        @pl.when(t >= 2)
        def _():
            out_copy(p, m, t % 2).wait()

        o_buf[t % 2] = _dot(a_ref[...], w_ref[...])
        out_copy(p, m, t % 2).start()

        @pl.when(t == NDEV * nm - 1)
        def _():
            out_copy(p, m, (t + 1) % 2).wait()
            out_copy(p, m, t % 2).wait()

    return pl.pallas_call(
        body, name=f"proj_{k}", grid=(NDEV, nm),
        in_specs=[pl.BlockSpec((tm, D), lambda p, m: (m, 0)),
                  pl.BlockSpec((None, D, width), lambda p, m: (p, 0, 0))]
        + [pl.BlockSpec(memory_space=pl.ANY)] * len(before),
        out_specs=pl.BlockSpec(memory_space=pl.ANY),
        out_shape=jax.ShapeDtypeStruct((S, NDEV * N_IN), F32),
        scratch_shapes=[pltpu.VMEM((2, tm, width), F32), pltpu.SemaphoreType.DMA((2,))],
        input_output_aliases={2: 0} if before else {},
        compiler_params=_cp(("arbitrary", "arbitrary")),
    )(h1, win, *before)


def _proj(h1, wins):
    out = None
    for k, win in enumerate(wins):
        out = _proj_round(h1, win, k, out)
    return out


def _out_proj_rms(x, ma, mr, wout, nw):
    tm = 256
    half = D // 2

    def body(x_ref, ma_ref, mr_ref, w_ref, nw_ref, x2_ref, h_ref, r_ref):
        acc = _dot(ma_ref[...], w_ref[0:half, :]) + _dot(mr_ref[...], w_ref[half:D, :])
        x2 = x_ref[...] + acc
        r = lax.rsqrt(jnp.mean(x2 * x2, axis=-1, keepdims=True) + EPS)
        x2_ref[...] = x2
        h_ref[...] = ((x2 * r) * nw_ref[...]).astype(BF16)
        r_ref[...] = r

    return pl.pallas_call(
        body, name="out_proj_rms", grid=(S // tm,),
        in_specs=[pl.BlockSpec((tm, D), lambda i: (i, 0)),
                  pl.BlockSpec((tm, half), lambda i: (i, 0)),
                  pl.BlockSpec((tm, half), lambda i: (i, 0)),
                  pl.BlockSpec((D, D), lambda i: (0, 0)),
                  pl.BlockSpec((1, D), lambda i: (0, 0))],
        out_specs=[pl.BlockSpec((tm, D), lambda i: (i, 0)), pl.BlockSpec((tm, D), lambda i: (i, 0)),
                   pl.BlockSpec((tm, 1), lambda i: (i, 0))],
        out_shape=[jax.ShapeDtypeStruct((S, D), F32), jax.ShapeDtypeStruct((S, D), BF16),
                   jax.ShapeDtypeStruct((S, 1), F32)],
        compiler_params=_cp(("parallel",)),
    )(x, ma, mr, wout, nw)


def _ffn_up(h2, wgu, part, before=None):
    tm = 512

    def body(h_ref, w_ref, *rest):
        a_ref, dadg_ref, dadu_ref = rest[-3:]
        gu = _dot_nt(h_ref[...], w_ref[...])
        g, u = gu[:, 0:N_FG], gu[:, N_FG:2 * N_FG]
        sg = _sigmoid(g)
        silu = g * sg
        a_ref[...] = (silu * u).astype(BF16)
        dadg_ref[...] = (u * (sg * (1.0 + g * (1.0 - sg)))).astype(BF16)
        dadu_ref[...] = silu.astype(BF16)

    half = NFG // 2
    first = part * half
    before = list(before or [])
    blk = pl.BlockSpec((None, tm, N_FG), lambda p, m: (p + first, m, 0))
    return pl.pallas_call(
        body, name=f"ffn_up_{part}", grid=(half, S // tm),
        in_specs=[pl.BlockSpec((tm, D), lambda p, m: (m, 0)),
                  pl.BlockSpec((None, 2 * N_FG, D), lambda p, m: (p, 0, 0))]
        + [pl.BlockSpec(memory_space=pl.ANY)] * len(before),
        out_specs=[blk, blk, blk],
        out_shape=[jax.ShapeDtypeStruct((NFG, S, N_FG), BF16)] * 3,
        input_output_aliases={2 + k: k for k in range(len(before))},
        compiler_params=_cp(("parallel", "parallel")),
    )(h2, wgu, *before)


def _ffn_down_first(x2, a, wd):
    tm = 512
    n = wd.shape[0]

    def body(x_ref, a_ref, w_ref, o_ref):
        p = pl.program_id(1)

        @pl.when(p == 0)
        def _():
            o_ref[...] = x_ref[...] + _dot(a_ref[...], w_ref[0])

        @pl.when(p > 0)
        def _():
            o_ref[...] += _dot(a_ref[...], w_ref[p])

    return pl.pallas_call(
        body, name="ffn_down_first", grid=(S // tm, n),
        in_specs=[pl.BlockSpec((tm, D), lambda m, p: (m, 0)),
                  pl.BlockSpec((None, tm, N_FG), lambda m, p: (p, m, 0)),
                  pl.BlockSpec((n, N_FG, D), lambda m, p: (0, 0, 0))],
        out_specs=pl.BlockSpec((tm, D), lambda m, p: (m, 0)),
        out_shape=jax.ShapeDtypeStruct((S, D), F32),
        compiler_params=_cp(("parallel", "arbitrary")),
    )(x2, a, wd)


def _ffn_down_loss(x2, a, wd, nw, tgt):
    tm = 512
    first = NFG - wd.shape[0]

    def body(x2_hbm, a_ref, w_ref, nw_ref, t_hbm, dx_ref, dxb_ref, st_ref, acc_ref, x2_buf, t_buf, sems):
        m, p = pl.program_id(0), pl.program_id(1)
        tail_in = _row_copies((x2_hbm, t_hbm), (x2_buf, t_buf), sems, m, tm)

        @pl.when(p == 0)
        def _():
            acc_ref[...] = jnp.zeros_like(acc_ref)
            for cp in tail_in:
                cp.start()

        @pl.when((p == 0) & (m == 0))
        def _():
            st_ref[...] = jnp.zeros_like(st_ref)

        @pl.when(p < NFG - first - 1)
        def _():
            acc_ref[...] += _dot(a_ref[...], w_ref[p])

        @pl.when(p == NFG - first - 1)
        def _():
            for cp in tail_in:
                cp.wait()
            nwv = nw_ref[...]
            dnw_sum = jnp.zeros((1, D), F32)
            loss_sum = jnp.zeros((1, 1), F32)
            for lo in range(0, tm, TAIL_ROWS):
                rows = slice(lo, lo + TAIL_ROWS)
                x3 = x2_buf[rows, :] + (acc_ref[rows, :] + _dot(a_ref[rows, :], w_ref[p]))
                r = lax.rsqrt(jnp.mean(x3 * x3, axis=-1, keepdims=True) + EPS)
                y = (x3 * r) * nwv
                err = y - t_buf[rows, :]
                loss_sum = loss_sum + 0.5 * jnp.sum(jnp.mean(err * err, axis=-1, keepdims=True), axis=0, keepdims=True)
                dy = err * (1.0 / D)
                dx, dnw = _rms_bwd_tile(dy, x3, r, nwv)
                dx_ref[rows, :] = dx
                dxb_ref[rows, :] = dx.astype(BF16)
                dnw_sum = dnw_sum + dnw
            st_ref[0:1, :] += dnw_sum
            st_ref[1:2, :] += jnp.broadcast_to(loss_sum, (1, D))

    return pl.pallas_call(
        body, name="ffn_down_loss", grid=(S // tm, NFG - first),
        in_specs=[pl.BlockSpec(memory_space=pl.ANY),
                  pl.BlockSpec((None, tm, N_FG), lambda m, p: (p + first, m, 0)),
                  pl.BlockSpec((NFG - first, N_FG, D), lambda m, p: (0, 0, 0)),
                  pl.BlockSpec((1, D), lambda m, p: (0, 0)),
                  pl.BlockSpec(memory_space=pl.ANY)],
        out_specs=[pl.BlockSpec((tm, D), lambda m, p: (m, 0)), pl.BlockSpec((tm, D), lambda m, p: (m, 0)),
                   pl.BlockSpec((8, D), lambda m, p: (0, 0))],
        out_shape=[jax.ShapeDtypeStruct((S, D), F32), jax.ShapeDtypeStruct((S, D), BF16),
                   jax.ShapeDtypeStruct((8, D), F32)],
        scratch_shapes=[pltpu.VMEM((tm, D), F32), pltpu.VMEM((tm, D), F32), pltpu.VMEM((tm, D), F32),
                        pltpu.SemaphoreType.DMA((2,))],
        compiler_params=_cp(("arbitrary", "arbitrary")),
    )(x2, a, wd, nw, tgt)


def _ffn_down_bwd(dx3b, wd, dadg, dadu, part, before=None):
    tm = 1024
    half = NFG // 2

    def body(dx_ref, w_ref, dadg_ref, dadu_ref, *rest):
        dgu_ref = rest[-1]
        rows = pl.ds(pl.multiple_of(pl.program_id(1) * tm, tm), tm)
        da = _dot_nt(dx_ref[rows, :], w_ref[...])
        dgu_ref[:, 0:N_FG] = (da * dadg_ref[...].astype(F32)).astype(BF16)
        dgu_ref[:, N_FG:2 * N_FG] = (da * dadu_ref[...].astype(F32)).astype(BF16)

    blk = pl.BlockSpec((None, tm, N_FG), lambda p, m: (p + part * half, m, 0))
    before = list(before or [])
    return pl.pallas_call(
        body, name=f"ffn_down_bwd_{part}", grid=(half, S // tm),
        in_specs=[pl.BlockSpec((S, D), lambda p, m: (0, 0)),
                  pl.BlockSpec((None, N_FG, D), lambda p, m: (p, 0, 0)), blk, blk]
        + [pl.BlockSpec(memory_space=pl.ANY)] * len(before),
        out_specs=pl.BlockSpec((None, tm, 2 * N_FG), lambda p, m: (p + part * half, m, 0)),
        out_shape=jax.ShapeDtypeStruct((NFG, S, 2 * N_FG), BF16),
        input_output_aliases={4 + k: k for k in range(len(before))},
        compiler_params=_cp(("parallel", "parallel")),
    )(dx3b, wd, dadg, dadu, *before)


def _ffn_up_bwd(dgu, wgu_a, wgu_b, dres, xs, r, nw):
    tm = 512
    nm = S // tm
    na = wgu_a.shape[0]

    def body(dgu_ref, wa_hbm, wb_hbm, dres_hbm, x_hbm, r_ref, nw_ref, dx_ref, dxb_ref, st_ref,
             w_buf, dres_buf, x_buf, sems, w_sems):
        m, p = pl.program_id(0), pl.program_id(1)
        tail_in = _row_copies((dres_hbm, x_hbm), (dres_buf, x_buf), sems, m, tm)

        def fetch(g, slot):
            for src, lo in ((wa_hbm, 0), (wb_hbm, na)):
                @pl.when((g >= lo) & (g < lo + na))
                def _():
                    pltpu.make_async_copy(src.at[g - lo], w_buf.at[slot], w_sems.at[slot]).start()

        @pl.when((p == 0) & (m == 0))
        def _():
            st_ref[...] = jnp.zeros_like(st_ref)
            fetch(p, 0)

        @pl.when((p < NFG - 1) | (m < nm - 1))
        def _():
            fetch((p + 1) % NFG, (p + 1) % 2)

        @pl.when(p == 0)
        def _():
            dx_ref[...] = jnp.zeros_like(dx_ref)
            for cp in tail_in:
                cp.start()

        slot = p % 2
        pltpu.make_async_copy(wa_hbm.at[0], w_buf.at[slot], w_sems.at[slot]).wait()
        @pl.when(p < NFG - 1)
        def _():
            dx_ref[...] += _dot(dgu_ref[...], w_buf[slot])

        @pl.when(p == NFG - 1)
        def _():
            for cp in tail_in:
                cp.wait()
            dnw_sum = jnp.zeros((1, D), F32)
            for lo in range(0, tm, TAIL_ROWS):
                rows = slice(lo, lo + TAIL_ROWS)
                dh = dx_ref[rows, :] + _dot(dgu_ref[rows, :], w_buf[slot])
                dx, dnw = _rms_bwd_tile(dh, x_buf[rows, :], r_ref[rows, :], nw_ref[...])
                dx = dres_buf[rows, :] + dx
                dx_ref[rows, :] = dx
                dxb_ref[rows, :] = dx.astype(BF16)
                dnw_sum = dnw_sum + dnw
            st_ref[0:1, :] += dnw_sum

    blk = pl.BlockSpec((None, tm, 2 * N_FG), lambda m, p: (p, m, 0))
    row = pl.BlockSpec((tm, D), lambda m, p: (m, 0))
    hbm = pl.BlockSpec(memory_space=pl.ANY)
    return pl.pallas_call(
        body, name="ffn_up_bwd", grid=(nm, NFG),
        in_specs=[blk, hbm, hbm, hbm, hbm, pl.BlockSpec((tm, 1), lambda m, p: (m, 0)),
                  pl.BlockSpec((1, D), lambda m, p: (0, 0))],
        out_specs=[row, row, pl.BlockSpec((8, D), lambda m, p: (0, 0))],
        out_shape=[jax.ShapeDtypeStruct((S, D), F32), jax.ShapeDtypeStruct((S, D), BF16),
                   jax.ShapeDtypeStruct((8, D), F32)],
        scratch_shapes=[pltpu.VMEM((2, 2 * N_FG, D), BF16), pltpu.VMEM((tm, D), F32), pltpu.VMEM((tm, D), F32),
                        pltpu.SemaphoreType.DMA((2,)), pltpu.SemaphoreType.DMA((2,))],
        compiler_params=_cp(("arbitrary", "arbitrary")),
    )(dgu, wgu_a, wgu_b, dres, xs, r, nw)


def _out_proj_bwd(dx2b, wout, place=None, rider=None):
    tm = 256

    if rider is None:
        def body(dx_ref, w_ref, o_ref):
            o_ref[...] = _dot_nt(dx_ref[...], w_ref[...])

        return pl.pallas_call(
            body, name="out_proj_bwd", grid=(S // tm,),
            in_specs=[pl.BlockSpec((tm, D), lambda i: (i, 0)), pl.BlockSpec((D, D), lambda i: (0, 0))],
            out_specs=pl.BlockSpec((tm, D), lambda i: (i, 0)),
            out_shape=jax.ShapeDtypeStruct((S, D), F32),
            compiler_params=_cp(("parallel",)),
        )(dx2b, wout), None

    w = rider[0]
    r, c = w.shape
    rt = _row_tile(r, c)
    nt = r // rt
    assert nt <= S // tm

    def body(pos_ref, dx_ref, w_ref, uw, um, uv, ug, us, uc, o_ref, go, dd, mo, vo):
        o_ref[...] = _dot_nt(dx_ref[...], w_ref[...])

        @pl.when(pl.program_id(0) < nt)
        def _():
            _update_tile(uw, um, uv, ug, us, uc, go, dd, mo, vo)

    def at(i):
        return jnp.minimum(i, nt - 1)

    tile = pl.BlockSpec((rt, c), lambda i, pos: (at(i), 0))
    outs = pl.pallas_call(
        body, name="out_proj_bwd",
        grid_spec=pltpu.PrefetchScalarGridSpec(
            num_scalar_prefetch=1, grid=(S // tm,),
            in_specs=[pl.BlockSpec((tm, D), lambda i, pos: (i, 0)), pl.BlockSpec((D, D), lambda i, pos: (0, 0)),
                      tile, tile, tile,
                      pl.BlockSpec((None, rt, c), lambda i, pos: (4 * pos[0] + 2 * pos[1] + pos[2], at(i), 0)),
                      pl.BlockSpec((None, rt, c), lambda i, pos: (2 * pos[0] + pos[1], at(i), 0)),
                      pl.BlockSpec((3, rt, c), lambda i, pos: (0, at(i), 0))],
            out_specs=[pl.BlockSpec((tm, D), lambda i, pos: (i, 0)), tile, tile, tile, tile]),
        out_shape=[jax.ShapeDtypeStruct((S, D), F32)] + [jax.ShapeDtypeStruct((r, c), F32)] * 4,
        compiler_params=_cp(("arbitrary",)),
    )(place, dx2b, wout, *rider)
    return outs[0], outs[1:]


def _in_proj_bwd(dproj, wins, dres, xs, r, nw):
    tm = 1024

    nr = len(wins)
    nm = S // tm

    def body(dp_ref, *rest):
        w_hbms = rest[:nr]
        dres_hbm, x_hbm, r_ref, nw_ref, dx_ref, st_ref, w_buf, dres_buf, x_buf, sems, w_sems = rest[nr:]
        m, p = pl.program_id(0), pl.program_id(1)
        tail_in = _row_copies((dres_hbm, x_hbm), (dres_buf, x_buf), sems, m, tm)

        def w_copies(g, slot):
            return [pltpu.make_async_copy(w_hbm.at[g], w_buf.at[slot, pl.ds(0, D), pl.ds(off, width)],
                                          w_sems.at[slot, k])
                    for k, (w_hbm, (off, width)) in enumerate(zip(w_hbms, IN_ROUNDS))]

        @pl.when((p == 0) & (m == 0))
        def _():
            st_ref[...] = jnp.zeros_like(st_ref)
            for cp in w_copies(p, 0):
                cp.start()

        @pl.when((p < NDEV - 1) | (m < nm - 1))
        def _():
            for cp in w_copies((p + 1) % NDEV, (p + 1) % 2):
                cp.start()

        @pl.when(p == 0)
        def _():
            dx_ref[...] = jnp.zeros_like(dx_ref)
            for cp in tail_in:
                cp.start()

        for cp in w_copies(p, p % 2):
            cp.wait()

        @pl.when(p < NDEV - 1)
        def _():
            dx_ref[...] += _dot_nt(dp_ref[...], w_buf[p % 2])

        @pl.when(p == NDEV - 1)
        def _():
            for cp in tail_in:
                cp.wait()
            dnw_sum = jnp.zeros((1, D), F32)
            for lo in range(0, tm, TAIL_ROWS):
                rows = slice(lo, lo + TAIL_ROWS)
                dh = dx_ref[rows, :] + _dot_nt(dp_ref[rows, :], w_buf[p % 2])
                dx, dnw = _rms_bwd_tile(dh, x_buf[rows, :], r_ref[rows, :], nw_ref[...])
                dx_ref[rows, :] = dres_buf[rows, :] + dx
                dnw_sum = dnw_sum + dnw
            st_ref[0:1, :] += dnw_sum

    row = pl.BlockSpec((tm, D), lambda m, p: (m, 0))
    hbm = pl.BlockSpec(memory_space=pl.ANY)
    return pl.pallas_call(
        body, name="in_proj_bwd", grid=(S // tm, NDEV),
        in_specs=[pl.BlockSpec((tm, N_IN), lambda m, p: (m, p)),
                  *[hbm] * nr,
                  hbm, hbm, pl.BlockSpec((tm, 1), lambda m, p: (m, 0)),
                  pl.BlockSpec((1, D), lambda m, p: (0, 0))],
        out_specs=[row, pl.BlockSpec((8, D), lambda m, p: (0, 0))],
        out_shape=[jax.ShapeDtypeStruct((S, D), F32), jax.ShapeDtypeStruct((8, D), F32)],
        scratch_shapes=[pltpu.VMEM((2, D, N_IN), BF16), pltpu.VMEM((tm, D), F32), pltpu.VMEM((tm, D), F32),
                        pltpu.SemaphoreType.DMA((2,)), pltpu.SemaphoreType.DMA((2, nr))],
        compiler_params=_cp(("arbitrary", "arbitrary")),
    )(dproj, *wins, dres, xs, r, nw)


W_IN_PARTS = 2


def _wgrad_in(h1, dproj, part):
    rows = D // W_IN_PARTS

    def body(a_ref, d_ref, o_ref):
        both = _dot_tn(a_ref[...], d_ref[...]).astype(BF16)
        o_ref[0] = both[:, 0:N_IN]
        o_ref[1] = both[:, N_IN:2 * N_IN]

    return pl.pallas_call(
        body, name=f"wgrad_in_{part}", grid=(NDEV // 2,),
        in_specs=[pl.BlockSpec((S, rows), lambda p: (0, part)), pl.BlockSpec((S, 2 * N_IN), lambda p: (0, p))],
        out_specs=pl.BlockSpec((2, rows, N_IN), lambda p: (p, 0, 0)),
        out_shape=jax.ShapeDtypeStruct((NDEV, rows, N_IN), BF16),
        compiler_params=_cp(("parallel",)),
    )(h1, dproj)


def _wgrad_rows(a3, dy, name, col=0):
    def body(a_ref, d_ref, o_ref):
        dw = _dot_tn(a_ref[...], d_ref[...]).astype(BF16)
        for j in range(FF_PER):
            o_ref[j] = dw[j * FF_ROWS:(j + 1) * FF_ROWS]

    return pl.pallas_call(
        body, name=name, grid=(NFG,),
        in_specs=[pl.BlockSpec((None, S, N_FG), lambda p: (p, 0, col)), pl.BlockSpec((S, D), lambda p: (0, 0))],
        out_specs=pl.BlockSpec((FF_PER, FF_ROWS, D), lambda p: (p % 2, p // 2, 0)),
        out_shape=jax.ShapeDtypeStruct((NDEV, N_FF, D), BF16),
        compiler_params=_cp(("parallel",)),
    )(a3, dy)


def _wgrad_out(ma, mr, dx2b):
    half = D // 2
    per = half // N_OUT

    def body(ma_ref, mr_ref, d_ref, o_ref):
        p = pl.program_id(0)

        @pl.when(p == 0)
        def _():
            o_ref[...] = _dot_tn(ma_ref[...], d_ref[...]).astype(BF16).reshape(per, N_OUT, D)

        @pl.when(p == 1)
        def _():
            o_ref[...] = _dot_tn(mr_ref[...], d_ref[...]).astype(BF16).reshape(per, N_OUT, D)

    whole = pl.BlockSpec((S, half), lambda p: (0, 0))
    return pl.pallas_call(
        body, name="wgrad_out", grid=(2,),
        in_specs=[whole, whole, pl.BlockSpec((S, D), lambda p: (0, 0))],
        out_specs=pl.BlockSpec((per, N_OUT, D), lambda p: (p, 0, 0)),
        out_shape=jax.ShapeDtypeStruct((NDEV, N_OUT, D), BF16),
        compiler_params=_cp(("parallel",)),
    )(ma, mr, dx2b)


def _attn_consts():
    c = np.zeros((AH, 8, AHD), np.float32)
    for h in range(AH):
        c[h, :, :] = 2.0 ** (-(h + 1))
    return jnp.asarray(c)


def _permute_in(dst, src, d, cast=None):
    v = src[...]
    if d > 1:
        v = pltpu.einshape("jrc->rjc", v.reshape(S // d, d, AHD)).reshape(S, AHD)
    dst[...] = v if cast is None else v.astype(cast)


def _natural_order(v, d):
    if d == 1:
        return v
    return pltpu.einshape("rjc->jrc", v.reshape(d, S // d, AHD)).reshape(S, AHD)


def _attn_masks():
    qi = lax.broadcasted_iota(jnp.int32, (CH, CH), 0)
    kj = lax.broadcasted_iota(jnp.int32, (CH, CH), 1)
    dist_c = (qi - kj).astype(F32)
    dist_p = (qi - kj + CH).astype(F32)
    return (qi >= kj)[None], (kj >= qi)[None], dist_c[None], dist_p[None]


GB = 16


def _bdot_nt(a, b):
    return lax.dot_general(a, b, (((2,), (2,)), ((0,), (0,))), preferred_element_type=F32)


def _bdot(a, b):
    return lax.dot_general(a, b, (((2,), (1,)), ((0,), (0,))), preferred_element_type=F32)


def _bdot_tn(a, b):
    return lax.dot_general(a, b, (((1,), (1,)), ((0,), (0,))), preferred_element_type=F32)


def _shift_block(dst, src):
    dst[0:CH, :] = jnp.zeros((CH, AHD), dst.dtype)
    dst[CH:S, :] = src[0:S - CH, :]


def _has_prev(g, nb):
    blk = lax.broadcasted_iota(jnp.int32, (GB, 1, 1), 0) + g * GB
    return (blk & (nb - 1)) != 0


def _blocks(ref, g):
    return ref[g * GB * CH:(g + 1) * GB * CH, :].reshape(GB, CH, AHD)


def _attn_fwd(proj):
    scale = 1.0 / math.sqrt(AHD)

    def body(c_ref, q_ref, k_ref, v_ref, o_ref, ob_ref, lse_ref, qkvp_ref, lsep_ref, qd, kd, vd, kps, vps, od, ld, *nat):
        onat, lnat = nat[0:3], nat[3:6]
        slope = c_ref[0:1, :]
        mask_c, mask_p, dist_c, dist_p = _attn_masks()
        for pi, (d, nb) in enumerate(PATTERNS):
            _permute_in(qd, q_ref, d, BF16)
            _permute_in(kd, k_ref, d, BF16)
            _permute_in(vd, v_ref, d, BF16)
            if d > 1:
                qkvp_ref[pi - 1, 0] = qd[...]
                qkvp_ref[pi - 1, 1] = kd[...]
                qkvp_ref[pi - 1, 2] = vd[...]
            if nb > 1:
                _shift_block(kps, kd)
                _shift_block(vps, vd)
            bias_c = -(slope * float(d)) * dist_c
            bias_p = -(slope * float(d)) * dist_p
            for g in range(NB // GB):
                q3, k3, v3 = _blocks(qd, g), _blocks(kd, g), _blocks(vd, g)
                s_c = jnp.where(mask_c, _bdot_nt(q3, k3) * scale + bias_c, NEG)
                mx = jnp.max(s_c, axis=-1, keepdims=True)
                if nb > 1:
                    kp3, vp3 = _blocks(kps, g), _blocks(vps, g)
                    s_p = jnp.where(jnp.logical_and(mask_p, _has_prev(g, nb)),
                                    _bdot_nt(q3, kp3) * scale + bias_p, NEG)
                    mx = jnp.maximum(mx, jnp.max(s_p, axis=-1, keepdims=True))
                    l = (jnp.sum(jnp.exp(s_c - mx), axis=-1, keepdims=True)
                         + jnp.sum(jnp.exp(s_p - mx), axis=-1, keepdims=True))
                    lse = mx + jnp.log(l)
                    o3 = _bdot(jnp.exp(s_c - lse).astype(BF16), v3) + _bdot(jnp.exp(s_p - lse).astype(BF16), vp3)
                else:
                    l = jnp.sum(jnp.exp(s_c - mx), axis=-1, keepdims=True)
                    lse = mx + jnp.log(l)
                    o3 = _bdot(jnp.exp(s_c - lse).astype(BF16), v3)
                rows = slice(g * GB * CH, (g + 1) * GB * CH)
                od[rows, :] = o3.reshape(GB * CH, AHD)
                ld[rows, :] = jnp.broadcast_to(lse, (GB, CH, AHD)).reshape(GB * CH, AHD)
            onat[pi][...] = _natural_order(od[...], d)
            lnat[pi][...] = _natural_order(ld[...], d)
        l0, l1, l2 = lnat[0][...], lnat[1][...], lnat[2][...]
        mx = jnp.maximum(jnp.maximum(l0, l1), l2)
        e0, e1, e2 = jnp.exp(l0 - mx), jnp.exp(l1 - mx), jnp.exp(l2 - mx)
        den = e0 + e1 + e2
        out = (e0 / den) * onat[0][...] + (e1 / den) * onat[1][...] + (e2 / den) * onat[2][...]
        o_ref[...] = out
        ob_ref[...] = out.astype(BF16)
        lse_ref[...] = mx + jnp.log(den)
        for pi, (d, _) in enumerate(PATTERNS[1:]):
            _permute_in(lsep_ref.at[pi], lse_ref, d)

    def col(off):
        return pl.BlockSpec((S, AHD), lambda h: (0, off + h))

    return pl.pallas_call(
        body, name="attn_fwd", grid=(AH,),
        in_specs=[pl.BlockSpec((None, 8, AHD), lambda h: (h, 0, 0)), col(0), col(AH), col(2 * AH)],
        out_specs=[col(0), col(0), col(0), pl.BlockSpec((2, 3, S, AHD), lambda h: (0, 0, 0, h)),
                   pl.BlockSpec((2, S, AHD), lambda h: (0, 0, h))],
        out_shape=[jax.ShapeDtypeStruct((S, AH * AHD), F32), jax.ShapeDtypeStruct((S, AH * AHD), BF16),
                   jax.ShapeDtypeStruct((S, AH * AHD), F32),
                   jax.ShapeDtypeStruct((2, 3, S, AH * AHD), BF16), jax.ShapeDtypeStruct((2, S, AH * AHD), F32)],
        scratch_shapes=[pltpu.VMEM((S, AHD), BF16) for _ in range(5)]
        + [pltpu.VMEM((S, AHD), F32) for _ in range(8)],
        compiler_params=_cp(("parallel",)),
    )(_attn_consts(), proj, proj, proj)


def _attn_bwd(proj, dmixed, o, lse, qkvp, lsep):
    scale = 1.0 / math.sqrt(AHD)

    def body(c_ref, q_ref, k_ref, v_ref, do_ref, o_ref, lse_ref, qkvp_ref, lsep_ref, dproj_hbm,
             qd, kd, vd, dod, kps, vps, dld, dqd, dkd, dvd, delta, aq, ak, av, sq, sk, sv, sems):
        h = pl.program_id(0)

        def out_copies(head):
            return [pltpu.make_async_copy(
                st, dproj_hbm.at[:, pl.ds(pl.multiple_of((k * AH + head) * AHD, AHD), AHD)], sems.at[k])
                for k, st in enumerate((sq, sk, sv))]

        slope = c_ref[0:1, :]
        mask_c, mask_p, dist_c, dist_p = _attn_masks()
        delta[...] = jnp.broadcast_to(jnp.sum(do_ref[...] * o_ref[...], axis=-1, keepdims=True), (S, AHD))
        for pi, (d, nb) in enumerate(PATTERNS):
            if d == 1:
                _permute_in(qd, q_ref, d, BF16)
                _permute_in(kd, k_ref, d, BF16)
                _permute_in(vd, v_ref, d, BF16)
                qs, ks, vs, lss = qd, kd, vd, lse_ref
            else:
                qs, ks, vs, lss = (qkvp_ref.at[pi - 1, 0], qkvp_ref.at[pi - 1, 1], qkvp_ref.at[pi - 1, 2],
                                   lsep_ref.at[pi - 1])
            _permute_in(dod, do_ref, d, BF16)
            _permute_in(dld, delta, d)
            if nb > 1:
                _shift_block(kps, ks)
                _shift_block(vps, vs)
            bias_c = -(slope * float(d)) * dist_c
            bias_p = -(slope * float(d)) * dist_p
            for g in range(NB // GB):
                q3, k3, v3, do3 = _blocks(qs, g), _blocks(ks, g), _blocks(vs, g), _blocks(dod, g)
                ls, dl = _blocks(lss, g), _blocks(dld, g)
                lo, hi = g * GB * CH, (g + 1) * GB * CH
                p_c = jnp.exp(jnp.where(mask_c, _bdot_nt(q3, k3) * scale + bias_c, NEG) - ls)
                ds_c = ((p_c * (_bdot_nt(do3, v3) - dl)) * scale).astype(BF16)
                dq3 = _bdot(ds_c, k3)
                dkd[lo:hi, :] = _bdot_tn(ds_c, q3).reshape(GB * CH, AHD)
                dvd[lo:hi, :] = _bdot_tn(p_c.astype(BF16), do3).reshape(GB * CH, AHD)
                if nb > 1:
                    kp3, vp3 = _blocks(kps, g), _blocks(vps, g)
                    p_p = jnp.exp(jnp.where(jnp.logical_and(mask_p, _has_prev(g, nb)),
                                            _bdot_nt(q3, kp3) * scale + bias_p, NEG) - ls)
                    ds_p = ((p_p * (_bdot_nt(do3, vp3) - dl)) * scale).astype(BF16)
                    dq3 = dq3 + _bdot(ds_p, kp3)
                    dkp = _bdot_tn(ds_p, q3).reshape(GB * CH, AHD)
                    dvp = _bdot_tn(p_p.astype(BF16), do3).reshape(GB * CH, AHD)
                    if g == 0:
                        dkd[0:hi - CH, :] += dkp[CH:, :]
                        dvd[0:hi - CH, :] += dvp[CH:, :]
                    else:
                        dkd[lo - CH:hi - CH, :] += dkp
                        dvd[lo - CH:hi - CH, :] += dvp
                dqd[lo:hi, :] = dq3.reshape(GB * CH, AHD)
            ln = S // d
            for acc, src in ((aq, dqd), (ak, dkd), (av, dvd)):
                if pi == 0:
                    acc[...] = src[...]
                else:
                    acc[...] += _natural_order(src[...], d)

        @pl.when(h > 0)
        def _():
            for cp in out_copies(h - 1):
                cp.wait()

        sq[...] = aq[...].astype(BF16)
        sk[...] = ak[...].astype(BF16)
        sv[...] = av[...].astype(BF16)
        for cp in out_copies(h):
            cp.start()

        @pl.when(h == AH - 1)
        def _():
            for cp in out_copies(h):
                cp.wait()

    def col(off):
        return pl.BlockSpec((S, AHD), lambda h: (0, off + h))

    return pl.pallas_call(
        body, name="attn_bwd", grid=(AH,),
        in_specs=[pl.BlockSpec((None, 8, AHD), lambda h: (h, 0, 0)), col(0), col(AH), col(2 * AH),
                  col(0), col(0), col(0), pl.BlockSpec((2, 3, S, AHD), lambda h: (0, 0, 0, h)),
                  pl.BlockSpec((2, S, AHD), lambda h: (0, 0, h))],
        out_specs=pl.BlockSpec(memory_space=pl.ANY),
        out_shape=jax.ShapeDtypeStruct((S, NDEV * N_IN), BF16),
        scratch_shapes=[pltpu.VMEM((S, AHD), BF16) for _ in range(6)]
        + [pltpu.VMEM((S, AHD), F32) for _ in range(8)]
        + [pltpu.VMEM((S, AHD), BF16) for _ in range(3)] + [pltpu.SemaphoreType.DMA((3,))],
        compiler_params=_cp(("arbitrary",)),
    )(_attn_consts(), proj, proj, proj, dmixed, o, lse, qkvp, lsep)


def _ret_consts():
    c = np.zeros((RH, 8, RHD), np.float32)
    for h in range(RH):
        c[h, :, :] = np.log(np.float32(1.0) - np.float32(2.0 ** (-5.0 - h)))
    return jnp.asarray(c)


def _ret_factors(lg):
    i = lax.broadcasted_iota(jnp.int32, (CH, CH), 0)
    j = lax.broadcasted_iota(jnp.int32, (CH, CH), 1)
    dif = (i - j).astype(F32)
    decay = jnp.where(dif >= 0, jnp.exp(lg[:, 0:CH] * jnp.maximum(dif, 0.0)), 0.0)
    row = lax.broadcasted_iota(jnp.int32, (CH, RHD), 0).astype(F32)
    zeta = jnp.exp(lg * (CH - 1.0 - row))
    xi = jnp.exp(lg * (row + 1.0))
    return decay, zeta, xi, jnp.exp(lg * float(CH))


CBK = 8
RSTEPS = NB // CBK


def _ret_specs(rev):
    off = 3 * AH * AHD // RHD
    rows = CBK * CH

    def ch(n):
        return (RSTEPS - 1 - n) if rev else n

    def col(k):
        return pl.BlockSpec((rows, RHD), lambda h, n: (ch(n), off + k * RH + h))

    own = pl.BlockSpec((rows, RHD), lambda h, n: (ch(n), h))
    state = pl.BlockSpec((None, CBK, RHD, RHD), lambda h, n: (h, ch(n), 0, 0))
    const = pl.BlockSpec((None, 8, RHD), lambda h, n: (h, 0, 0))
    dm = pl.BlockSpec((rows, RHD), lambda h, n: (ch(n), AH * AHD // RHD + h))
    return col, own, state, const, dm


def _chunks(x):
    return x.reshape(CBK, CH, RHD)


def _ret_fwd(proj):
    def body(c_ref, q_ref, k_ref, v_ref, g_ref, ret_ref, mr_ref, st_ref, r_acc):
        n = pl.program_id(1)

        @pl.when(n == 0)
        def _():
            r_acc[...] = jnp.zeros_like(r_acc)

        decay, zeta, xi, gch = _ret_factors(c_ref[0:1, :])
        q3 = _chunks(q_ref[...].astype(BF16))
        kc = _chunks(k_ref[...] * (1.0 / math.sqrt(RHD)))
        k3 = kc.astype(BF16)
        v3 = _chunks(v_ref[...].astype(BF16))
        kv3 = _bdot_tn((kc * zeta[None]).astype(BF16), v3)
        r = r_acc[...]
        for i in range(CBK):
            st_ref[i] = r.astype(BF16)
            r = r * gch + kv3[i]
        r_acc[...] = r
        scores = _bdot_nt(q3, k3) * decay[None]
        ret = (_bdot(scores.astype(BF16), v3) + _bdot(q3, st_ref[...]) * xi[None]).reshape(CBK * CH, RHD)
        ret_ref[...] = ret
        rr = lax.rsqrt(jnp.mean(ret * ret, axis=-1, keepdims=True) + EPS)
        gv = g_ref[...]
        mr_ref[...] = ((gv * _sigmoid(gv)) * (ret * rr)).astype(BF16)

    col, own, state, const, _ = _ret_specs(False)
    return pl.pallas_call(
        body, name="ret_fwd", grid=(RH, RSTEPS),
        in_specs=[const, col(0), col(1), col(2), col(3)],
        out_specs=[own, own, state],
        out_shape=[jax.ShapeDtypeStruct((S, RH * RHD), F32), jax.ShapeDtypeStruct((S, RH * RHD), BF16),
                   jax.ShapeDtypeStruct((RH, NB, RHD, RHD), BF16)],
        scratch_shapes=[pltpu.VMEM((RHD, RHD), F32)],
        compiler_params=_cp(("parallel", "arbitrary")),
    )(_ret_consts(), proj, proj, proj, proj)


def _ret_bwd(proj, ret, states, dmixed, dproj):
    rows = CBK * CH
    col0 = 3 * AH * AHD

    def body(c_ref, q_ref, k_ref, v_ref, g_ref, ret_ref, st_ref, dm_ref, dproj_in, dproj_hbm, g_acc, gs,
             sq, sk, sv, sg, sems):
        del dproj_in
        h, n = pl.program_id(0), pl.program_id(1)
        step = h * RSTEPS + n

        def out_copies(t):
            hh, nn = t // RSTEPS, t % RSTEPS
            r0 = pl.multiple_of((RSTEPS - 1 - nn) * rows, rows)
            return [pltpu.make_async_copy(
                st, dproj_hbm.at[pl.ds(r0, rows), pl.ds(pl.multiple_of(col0 + (k * RH + hh) * RHD, RHD), RHD)],
                sems.at[k]) for k, st in enumerate((sq, sk, sv, sg))]

        @pl.when(n == 0)
        def _():
            g_acc[...] = jnp.zeros_like(g_acc)

        decay, zeta, xi, gch = _ret_factors(c_ref[0:1, :])
        ret_v = ret_ref[...]
        rr = lax.rsqrt(jnp.mean(ret_v * ret_v, axis=-1, keepdims=True) + EPS)
        gv = g_ref[...]
        sgm = _sigmoid(gv)
        dmix = dm_ref[...]
        dgate = ((dmix * (ret_v * rr)) * (sgm * (1.0 + gv * (1.0 - sgm)))).astype(BF16)
        dretn = dmix * (gv * sgm)
        dret = _chunks(rr * dretn - ret_v * ((rr * rr * rr) * jnp.mean(dretn * ret_v, axis=-1, keepdims=True)))

        q3 = _chunks(q_ref[...].astype(BF16))
        kc = _chunks(k_ref[...] * (1.0 / math.sqrt(RHD)))
        k3 = kc.astype(BF16)
        v3 = _chunks(v_ref[...].astype(BF16))
        d3 = dret.astype(BF16)
        dxi = (dret * xi[None]).astype(BF16)
        kz = (kc * zeta[None]).astype(BF16)
        dr3 = _bdot_tn(q3, dxi)
        acc = g_acc[...]
        for i in reversed(range(CBK)):
            gs[i] = acc.astype(BF16)
            acc = dr3[i] + gch * acc
        g_acc[...] = acc
        g3 = gs[...]
        sc = (_bdot_nt(q3, k3) * decay[None]).astype(BF16)
        da = (_bdot_nt(d3, v3) * decay[None]).astype(BF16)
        dq = _bdot(da, k3) + _bdot_nt(dxi, st_ref[...])
        dkc = _bdot_tn(da, q3) + _bdot_nt(v3, g3) * zeta[None]
        dv = _bdot_tn(sc, d3) + _bdot(kz, g3)

        @pl.when(step > 0)
        def _():
            for cp in out_copies(step - 1):
                cp.wait()

        sq[...] = dq.reshape(rows, RHD).astype(BF16)
        sk[...] = (dkc * (1.0 / math.sqrt(RHD))).reshape(rows, RHD).astype(BF16)
        sv[...] = dv.reshape(rows, RHD).astype(BF16)
        sg[...] = dgate
        for cp in out_copies(step):
            cp.start()

        @pl.when(step == RH * RSTEPS - 1)
        def _():
            for cp in out_copies(step):
                cp.wait()

    col, own, state, const, dm = _ret_specs(True)
    hbm = pl.BlockSpec(memory_space=pl.ANY)
    return pl.pallas_call(
        body, name="ret_bwd", grid=(RH, RSTEPS),
        in_specs=[const, col(0), col(1), col(2), col(3), own, state, dm, hbm],
        out_specs=hbm,
        out_shape=jax.ShapeDtypeStruct(dproj.shape, dproj.dtype),
        input_output_aliases={8: 0},
        scratch_shapes=[pltpu.VMEM((RHD, RHD), F32), pltpu.VMEM((CBK, RHD, RHD), BF16)]
        + [pltpu.VMEM((rows, RHD), BF16) for _ in range(4)] + [pltpu.SemaphoreType.DMA((4,))],
        compiler_params=_cp(("arbitrary", "arbitrary")),
    )(_ret_consts(), proj, proj, proj, proj, ret, states, dmixed, dproj)


class _NoReduction:
    def start(self, group, grads):
        pass

    def local(self, name, first=()):
        return []

    def landed(self, name):
        return []

    def update(self, name):
        return []

    place = None

    def rider(self, name):
        return None

    def set_update(self, name, outs):
        pass


def _local_step(x, tgt, nw1, nw2, nw3, win, wout, wgu_a, wgu_b, wd_a, wd_b, red=None):
    red = red or _NoReduction()

    def after(values, first):
        return lax.optimization_barrier((tuple(values), tuple(first)))[0]

    h1, r1 = _rms_fwd(x, nw1)
    proj = _proj(h1, win)
    o, ma, lse, qkvp, lsep = _attn_fwd(proj)
    ret, mr, states = _ret_fwd(proj)
    x2, h2, r2 = _out_proj_rms(x, ma, mr, wout, nw2)
    a, dadg, dadu = _ffn_up(h2, wgu_b, 1, _ffn_up(h2, wgu_a, 0))
    dx3, dx3b, st3 = _ffn_down_loss(_ffn_down_first(x2, a, wd_a), a, wd_b, nw3, tgt)

    dwd = _wgrad_rows(a, dx3b, "wgrad_down")
    red.start(["w_down"], [dwd])
    (dx3b,) = after([dx3b], [dwd])
    part = _ffn_down_bwd(dx3b, wd_a, dadg, dadu, 0)
    (dx3b,) = after([dx3b], red.local("w_down", first=[part]))
    dgu = _ffn_down_bwd(dx3b, wd_b, dadg, dadu, 1, [part])
    dwg = _wgrad_rows(dgu, h2, "wgrad_gate", 0)
    red.start(["w_gate"], [dwg])
    (dgu,) = after([dgu], [dwg])
    dwu = _wgrad_rows(dgu, h2, "wgrad_up", 1)
    red.start(["w_up"], [dwu])
    (dgu,) = after([dgu], red.local("w_gate", first=[dwu] + red.landed("w_down")))
    dx2, dx2b, st2 = _ffn_up_bwd(dgu, wgu_a, wgu_b, dx3, x2, r2, nw2)
    (dx2b,) = after([dx2b], red.local("w_up", first=[dx2b]))
    dwo = _wgrad_out(ma, mr, dx2b)
    red.start(["w_out"], [dwo])
    (dx2b,) = after([dx2b], [dwo])
    dmixed, done = _out_proj_bwd(dx2b, wout, red.place, red.rider("w_down"))
    red.set_update("w_down", done)
    dproj = _attn_bwd(proj, dmixed, o, lse, qkvp, lsep)
    (dmixed,) = after([dmixed], red.local("w_out", first=[dproj] + red.landed("w_gate")))
    dproj = _ret_bwd(proj, ret, states, dmixed, dproj)
    (dwi0,) = after([_wgrad_in(h1, dproj, 0)], red.landed("w_up"))
    red.start(["w_in_0"], [dwi0])
    (dproj,) = after([dproj], [dwi0])
    dwi1 = _wgrad_in(h1, dproj, 1)
    red.start(["w_in_1"], [dwi1])
    sums = red.local("w_in_0", first=[dwi1] + red.landed("w_out"))
    sums = red.local("w_in_1", first=sums + red.update("w_gate"))
    (dproj,) = after([dproj], sums)
    gx, st1 = _in_proj_bwd(dproj, win, dx2, x, r1, nw1)
    dwi = jnp.concatenate([dwi0, dwi1], axis=1)
    stats = jnp.concatenate([st1[0:1], st2[0:1], st3[0:2], jnp.zeros((4, D), F32)], axis=0)
    return stats, gx, dwi, dwo, dwg, dwu, dwd


def _place():
    x, y, c = lax.axis_index("x"), lax.axis_index("y"), lax.axis_index("c")
    return x, y, c, [(1 - x, y), (x, 1 - y), (1 - x, 1 - y)]


def _handshake(peers):
    barrier = pltpu.get_barrier_semaphore()
    for peer in peers:
        pl.semaphore_signal(barrier, inc=1, device_id=peer, device_id_type=MESH)
    pl.semaphore_wait(barrier, len(peers))


def _all_gather(shards, name, collective_id, per=0, rows=None):
    na = len(shards)
    nout = 1 if per else na
    lo, r = rows or (0, shards[0].shape[0])
    ngroups = NDEV // per if per else 0
    SIB, XN0, XN1, YN1, YN0, VIA_X, VIA_Y = 0, 1, 2, 3, 4, 5, 6
    D2D = {XN0: 7, XN1: 8, YN1: 9, YN0: 10, VIA_X: 11, VIA_Y: 12}

    def body(*refs):
        ins, outs = [ref.at[pl.ds(lo, r)] for ref in refs[:na]], refs[na:na + nout]
        send_sems, recv_sems, local_sems = refs[na + nout:]
        x, y, c, _ = _place()
        me, sib = (x, y, c), (x, y, 1 - c)
        xn, yn, dg = (1 - x, y, c), (x, 1 - y, c), (1 - x, 1 - y, c)
        _handshake([sib, xn, yn])

        def part(ref, h):
            rows = ref.shape[0] // 2
            return ref if h is None else ref.at[pl.ds(h * rows, rows)]

        def block(a, owner, h):
            idx = 4 * owner[0] + 2 * owner[1] + owner[2]
            if not per:
                return part(outs[a].at[idx], h)
            return part(outs[0].at[idx // per, a, pl.ds(pl.multiple_of((idx % per) * r, r), r)], h)

        def copy(a, k, owner, h, to, own_src=False):
            return pltpu.make_async_remote_copy(
                src_ref=part(ins[a], h) if own_src else block(a, owner, h), dst_ref=block(a, owner, h),
                send_sem=send_sems.at[a, k], recv_sem=recv_sems.at[a, k], device_id=to, device_id_type=MESH)

        def other(p):
            return (p[0], p[1], 1 - c)

        mine = [pltpu.make_async_copy(ins[a], block(a, me, None), local_sems.at[a]) for a in range(na)]
        for cp in mine:
            cp.start()
        sent = []
        for a in range(na):
            sent += [copy(a, XN0, me, 0, xn, True), copy(a, YN1, me, 1, yn, True),
                     copy(a, XN1, me, 1, xn, True), copy(a, YN0, me, 0, yn, True)]
        sent += [copy(a, SIB, me, None, sib, True) for a in range(na)]
        for cp in sent:
            cp.start()

        def landed(a, k, owner, h, then):
            copy(a, k, owner, h, me).wait_recv()
            for k2, to in then + [(D2D[k], sib)]:
                cp = copy(a, k2, owner, h, to)
                cp.start()
                sent.append(cp)

        for a in range(na):
            landed(a, XN0, xn, 0, [(VIA_Y, yn)])
            landed(a, YN1, yn, 1, [(VIA_X, xn)])
            landed(a, XN1, xn, 1, [])
            landed(a, YN0, yn, 0, [])
        for a in range(na):
            landed(a, VIA_Y, dg, 0, [])
            landed(a, VIA_X, dg, 1, [])
        for a in range(na):
            copy(a, SIB, sib, None, me).wait_recv()
            for k, owner, h in ((XN0, xn, 0), (XN1, xn, 1), (YN1, yn, 1), (YN0, yn, 0), (VIA_Y, dg, 0), (VIA_X, dg, 1)):
                copy(a, D2D[k], other(owner), h, me).wait_recv()
        for cp in sent:
            cp.wait_send()
        for cp in mine:
            cp.wait()

    if per:
        out_type = [jax.ShapeDtypeStruct((ngroups, na, per * r, shards[0].shape[1]), shards[0].dtype)]
    else:
        out_type = [jax.ShapeDtypeStruct((NDEV,) + s.shape, s.dtype) for s in shards]
    return _sequencer_call(
        body, name, collective_id, out_type,
        [pltpu.SemaphoreType.DMA((na, 13)), pltpu.SemaphoreType.DMA((na, 13)), pltpu.SemaphoreType.DMA((na,))])(*shards)


def _sequencer_call(body, name, collective_id, out_type, scratch_types):
    return pl.kernel(
        body, name=name, out_type=out_type,
        mesh=plsc.ScalarSubcoreMesh(axis_name="sequencer", num_cores=1),
        scratch_types=scratch_types,
        compiler_params=pltpu.CompilerParams(collective_id=collective_id))


def _exchange_sibling(grads, name, collective_id):
    na = len(grads)

    def body(*refs):
        ins, outs = refs[:na], refs[na:2 * na]
        send_sems, recv_sems = refs[2 * na:]
        x, y, c, _ = _place()
        _handshake([(x, y, 1 - c)])
        cps = []
        for a in range(na):
            for k in range(4):
                cps.append(pltpu.make_async_remote_copy(
                    src_ref=ins[a].at[2 * k + (1 - c)], dst_ref=outs[a].at[k],
                    send_sem=send_sems.at[a, k], recv_sem=recv_sems.at[a, k],
                    device_id=(x, y, 1 - c), device_id_type=MESH))
        for cp in cps:
            cp.start()
        for cp in cps:
            cp.wait()

    return _sequencer_call(
        body, name, collective_id,
        [jax.ShapeDtypeStruct((4,) + g.shape[1:], g.dtype) for g in grads],
        [pltpu.SemaphoreType.DMA((na, 4)), pltpu.SemaphoreType.DMA((na, 4))])(*grads)


def _row_tile(rows, cols):
    for t in (512, 256, 176, 128, 64, 32, 16):
        if rows % t == 0 and t * cols * 4 <= (2 << 20):
            return t
    raise ValueError((rows, cols))


STREAM_BUFS = 3


def _stream_tile(rows, steps):
    for t in (512, 256, 176, 128, 64, 32, 16):
        if rows % t == 0 and rows // t >= steps:
            return t
    raise ValueError((rows, steps))


def _stream(n, loads, stores, compute):
    for k in range(min(STREAM_BUFS, n)):
        for cp in loads(k):
            cp.start()
    for k in range(n):
        for cp in loads(k):
            cp.wait()
        if k >= 2:
            for cp in stores(k - 2):
                cp.wait()
        compute(k)
        for cp in stores(k):
            cp.start()
        if k + STREAM_BUFS < n:
            for cp in loads(k + STREAM_BUFS):
                cp.start()
    for k in range(max(n - 2, 0), n):
        for cp in stores(k):
            cp.wait()


def _chip_sum(place, g, got, name):
    _, r, c = g.shape
    tm = _stream_tile(r, 4)
    nt = r // tm

    def body(pos_ref, g_hbm, got_hbm, o_hbm, g_buf, s_buf, o_buf, sem_in, sem_out):
        def chip(j):
            return 2 * (pos_ref[0] ^ (0 if j == 1 else 1)) + (pos_ref[1] ^ (0 if j == 0 else 1))

        def loads(k):
            j, rows, slot = k // nt, pl.ds((k % nt) * tm, tm), k % STREAM_BUFS
            return [pltpu.make_async_copy(g_hbm.at[2 * chip(j) + pos_ref[2], rows], g_buf.at[slot], sem_in.at[slot, 0]),
                    pltpu.make_async_copy(got_hbm.at[chip(j), rows], s_buf.at[slot], sem_in.at[slot, 1])]

        def stores(k):
            return [pltpu.make_async_copy(o_buf.at[k % 2], o_hbm.at[k // nt, pl.ds((k % nt) * tm, tm)],
                                          sem_out.at[k % 2])]

        def compute(k):
            slot = k % STREAM_BUFS
            o_buf[k % 2] = (g_buf[slot].astype(F32) + s_buf[slot].astype(F32)).astype(BF16)

        _stream(3 * nt, loads, stores, compute)

    hbm = pl.BlockSpec(memory_space=pl.ANY)
    return pl.pallas_call(
        body, name=name,
        grid_spec=pltpu.PrefetchScalarGridSpec(
            num_scalar_prefetch=1, grid=(1,), in_specs=[hbm, hbm], out_specs=hbm,
            scratch_shapes=[pltpu.VMEM((STREAM_BUFS, tm, c), BF16), pltpu.VMEM((STREAM_BUFS, tm, c), BF16),
                            pltpu.VMEM((2, tm, c), BF16),
                            pltpu.SemaphoreType.DMA((STREAM_BUFS, 2)), pltpu.SemaphoreType.DMA((2,))]),
        out_shape=jax.ShapeDtypeStruct((3, r, c), BF16),
        compiler_params=_cp(("arbitrary",)),
    )(place, g, got)


def _exchange_chips(sums, name, collective_id):
    na = len(sums)

    def body(*refs):
        ins, outs = refs[:na], refs[na:2 * na]
        send_sems, recv_sems = refs[2 * na:]
        x, y, c, chips = _place()
        _handshake([(*chip, c) for chip in chips])
        cps = []
        for a in range(na):
            for j, chip in enumerate(chips):
                cps.append(pltpu.make_async_remote_copy(
                    src_ref=ins[a].at[j], dst_ref=outs[a].at[j],
                    send_sem=send_sems.at[a, j], recv_sem=recv_sems.at[a, j],
                    device_id=(*chip, c), device_id_type=MESH))
        for cp in cps:
            cp.start()
        for cp in cps:
            cp.wait()

    return _sequencer_call(
        body, name, collective_id,
        [jax.ShapeDtypeStruct((3,) + s.shape[1:], s.dtype) for s in sums],
        [pltpu.SemaphoreType.DMA((na, 3)), pltpu.SemaphoreType.DMA((na, 3))])(*sums)


def _exchange_stats(stats, collective_id):
    def body(st_in, st_out, st_send, st_recv, local_sem):
        x, y, c, _ = _place()
        me_idx = 4 * x + 2 * y + c
        peers = [(x ^ ((k >> 2) & 1), y ^ ((k >> 1) & 1), c ^ (k & 1)) for k in range(1, 8)]
        _handshake(peers)
        mine = pltpu.make_async_copy(st_in, st_out.at[me_idx], local_sem)
        mine.start()
        cps = [pltpu.make_async_remote_copy(
            src_ref=st_in, dst_ref=st_out.at[me_idx], send_sem=st_send.at[k], recv_sem=st_recv.at[k],
            device_id=peer, device_id_type=MESH) for k, peer in enumerate(peers)]
        for cp in cps:
            cp.start()
        for cp in cps:
            cp.wait()
        mine.wait()

    return _sequencer_call(
        body, "exchange_stats", collective_id,
        jax.ShapeDtypeStruct((NDEV,) + stats.shape, stats.dtype),
        [pltpu.SemaphoreType.DMA((7,)), pltpu.SemaphoreType.DMA((7,)), pltpu.SemaphoreType.DMA])(stats)


class _Reduction:
    def __init__(self, place, first_collective_id, state):
        self.place = place
        self.ids = iter(range(first_collective_id, 32))
        self.state = state
        self.groups = {}
        self.updates = {}

    def next_id(self):
        return next(self.ids)

    def start(self, group, grads):
        got = _exchange_sibling(grads, "sibling_exchange_" + group[0], self.next_id())
        self.groups[group[0]] = dict(names=group, grads=grads, got=got)

    def local(self, name, first=()):
        grp = self.groups[name]
        grads = lax.optimization_barrier((tuple(grp["grads"]), tuple(first)))[0]
        grp["sums"] = [_chip_sum(self.place, g, s, "chip_sum_" + n)
                       for g, s, n in zip(grads, grp["got"], grp["names"])]
        grp["chips"] = _exchange_chips(grp["sums"], "chip_exchange_" + name, self.next_id())
        return grp["sums"]

    def landed(self, name):
        return list(self.groups[name]["chips"])

    def rider(self, name):
        grp = next(g for g in self.groups.values() if name in g["names"])
        k = grp["names"].index(name)
        return self.state[name][:3] + (grp["grads"][k], grp["got"][k], grp["chips"][k])

    def set_update(self, name, outs):
        self.updates[name] = list(outs)

    def update(self, name):
        if name not in self.updates:
            grp = next(g for g in self.groups.values() if name in g["names"])
            k = grp["names"].index(name)
            w, m, v, part, parts = self.state[name]
            before = self.update(f"{name[:-1]}{part - 1}") if part else None
            self.updates[name] = _shard_update(self.place, w, m, v, grp["grads"][k], grp["got"][k],
                                               grp["chips"][k], "update_" + name, part, parts, before)
        return list(self.updates[name])


def _adamw(w, g, m, v):
    m = ADAM_B1 * m + (1.0 - ADAM_B1) * g
    v = ADAM_B2 * v + (1.0 - ADAM_B2) * (g * g)
    m_hat = m / (1.0 - ADAM_B1 ** ADAM_STEP)
    v_hat = v / (1.0 - ADAM_B2 ** ADAM_STEP)
    delta = -ADAM_LR * (m_hat / (jnp.sqrt(v_hat) + ADAM_EPS) + ADAM_WD * w)
    return delta, m, v


def _update_tile(w_ref, m_ref, v_ref, g_ref, s_ref, c_ref, go_ref, d_ref, mo_ref, vo_ref):
    grad = g_ref[...].astype(F32) + s_ref[...].astype(F32)
    for j in range(3):
        grad = grad + c_ref[j].astype(F32)
    delta, mn, vn = _adamw(w_ref[...], grad, m_ref[...], v_ref[...])
    go_ref[...] = grad
    d_ref[...] = delta
    mo_ref[...] = mn
    vo_ref[...] = vn


def _shard_update(place, w, m, v, g, got_sib, got_chips, name, part=0, parts=1, before=None):
    r, c = w.shape
    rp = r // parts
    tm = _stream_tile(rp, 8)
    nt = rp // tm
    before = list(before or [])

    def body(pos_ref, w_hbm, m_hbm, v_hbm, g_hbm, s_hbm, c_hbm, *rest):
        outs = rest[len(before):len(before) + 4]
        w_buf, m_buf, v_buf, g_buf, s_buf, c_buf, o_buf, sem_in, sem_out = rest[len(before) + 4:]
        own = 4 * pos_ref[0] + 2 * pos_ref[1] + pos_ref[2]
        chip = 2 * pos_ref[0] + pos_ref[1]

        def loads(k):
            slot, rows, mine = k % STREAM_BUFS, pl.ds(k * tm, tm), pl.ds(part * rp + k * tm, tm)
            pairs = [(w_hbm.at[mine], w_buf), (m_hbm.at[mine], m_buf), (v_hbm.at[mine], v_buf),
                     (g_hbm.at[own, rows], g_buf), (s_hbm.at[chip, rows], s_buf), (c_hbm.at[:, rows], c_buf)]
            return [pltpu.make_async_copy(src, buf.at[slot], sem_in.at[slot, n]) for n, (src, buf) in enumerate(pairs)]

        def stores(k):
            mine = pl.ds(part * rp + k * tm, tm)
            return [pltpu.make_async_copy(o_buf.at[k % 2, n], out.at[mine], sem_out.at[k % 2, n])
                    for n, out in enumerate(outs)]

        def compute(k):
            slot = k % STREAM_BUFS
            _update_tile(w_buf.at[slot], m_buf.at[slot], v_buf.at[slot], g_buf.at[slot], s_buf.at[slot],
                         c_buf.at[slot], *[o_buf.at[k % 2, n] for n in range(4)])

        _stream(nt, loads, stores, compute)

    hbm = pl.BlockSpec(memory_space=pl.ANY)
    return pl.pallas_call(
        body, name=name,
        grid_spec=pltpu.PrefetchScalarGridSpec(
            num_scalar_prefetch=1, grid=(1,), in_specs=[hbm] * (6 + len(before)), out_specs=[hbm] * 4,
            scratch_shapes=[pltpu.VMEM((STREAM_BUFS, tm, c), F32)] * 3 + [pltpu.VMEM((STREAM_BUFS, tm, c), BF16)] * 2
            + [pltpu.VMEM((STREAM_BUFS, 3, tm, c), BF16), pltpu.VMEM((2, 4, tm, c), F32),
               pltpu.SemaphoreType.DMA((STREAM_BUFS, 6)), pltpu.SemaphoreType.DMA((2, 4))]),
        out_shape=[jax.ShapeDtypeStruct((r, c), F32)] * 4,
        input_output_aliases={7 + k: k for k in range(len(before))},
        compiler_params=_cp(("arbitrary",)),
    )(place, w, m, v, g, got_sib, got_chips, *before)


def _small_update(stats_all, ws, ms, vs):
    def body(st_ref, w_ref, m_ref, v_ref, go_ref, d_ref, mo_ref, vo_ref):
        grad = st_ref[0]
        for k in range(1, NDEV):
            grad = grad + st_ref[k]
        delta, mn, vn = _adamw(w_ref[...], grad, m_ref[...], v_ref[...])
        go_ref[...] = grad
        d_ref[...] = delta
        mo_ref[...] = mn
        vo_ref[...] = vn

    return pl.pallas_call(
        body, name="small_update",
        out_shape=[jax.ShapeDtypeStruct((8, D), F32)] * 4,
        compiler_params=_cp(),
    )(stats_all, ws, ms, vs)


def kernel(x, norm_mix_w, w_in, w_out, norm_ffn_w, w_gate, w_up, w_down, norm_final_w, loss_target, m_norm_mix_w, m_w_in, m_w_out, m_norm_ffn_w, m_w_gate, m_w_up, m_w_down, m_norm_final_w, v_norm_mix_w, v_w_in, v_w_out, v_norm_ffn_w, v_w_gate, v_w_up, v_w_down, v_norm_final_w):
    tr = {"w_gate", "w_up"}
    names = ["w_in", "w_out", "w_gate", "w_up", "w_down"]

    def view(a, n):
        return a[0].T if n in tr else a[0]

    big_w = [view(a, n) for a, n in zip([w_in, w_out, w_gate, w_up, w_down], names)]
    big_m = [view(a, n) for a, n in zip([m_w_in, m_w_out, m_w_gate, m_w_up, m_w_down], names)]
    big_v = [view(a, n) for a, n in zip([v_w_in, v_w_out, v_w_gate, v_w_up, v_w_down], names)]

    shards = [None] + [_cast_bf16(w, "cast_" + n) for w, n in zip(big_w[1:], names[1:])]
    win = [_all_gather([cols], f"all_gather_w_in_{k}", 1 + k)[0]
           for k, cols in enumerate(_cast_cols(big_w[0], "cast_w_in"))]
    (wout,) = _all_gather(shards[1:2], "all_gather_w_out", 3)
    (wgu_a,) = _all_gather(shards[2:4], "all_gather_gate_up_0", 4, per=FF_PER, rows=(0, FF_ROWS))
    (wgu_b,) = _all_gather(shards[2:4], "all_gather_gate_up_1", 5, per=FF_PER, rows=(FF_ROWS, FF_ROWS))
    (wd_a,) = _all_gather(shards[4:5], "all_gather_w_down_0", 6, per=FF_PER, rows=(0, FF_ROWS))
    (wd_b,) = _all_gather(shards[4:5], "all_gather_w_down_1", 7, per=FF_PER, rows=(FF_ROWS, FF_ROWS))
    nw3 = norm_final_w.reshape(1, D)
    place = jnp.stack([lax.axis_index("x"), lax.axis_index("y"), lax.axis_index("c")]).astype(jnp.int32)
    state = {n: (w, m, v, 0, 1) for n, w, m, v in zip(names, big_w, big_m, big_v)}
    for part in range(W_IN_PARTS):
        state[f"w_in_{part}"] = state["w_in"][:3] + (part, W_IN_PARTS)
    red = _Reduction(place, 8, state)
    stats, gx, *_ = _local_step(
        x[0], loss_target[0], norm_mix_w, norm_ffn_w, nw3, win, wout.reshape(D, D),
        wgu_a.reshape(NFG // 2, 2 * N_FG, D), wgu_b.reshape(NFG // 2, 2 * N_FG, D),
        wd_a.reshape(NFG // 2, N_FG, D), wd_b.reshape(NFG // 2, N_FG, D), red)
    stats_all = _exchange_stats(stats, red.next_id())
    upd = [red.update(f"w_in_{W_IN_PARTS - 1}" if n == "w_in" else n) for n in names]
    stats_all = lax.optimization_barrier((stats_all, tuple(upd[0])))[0]

    def rows(a, b, c):
        return jnp.concatenate([a.reshape(1, D), b.reshape(1, D), c.reshape(1, D), jnp.zeros((5, D), F32)], axis=0)

    sg, sd, sm, sv = _small_update(stats_all, rows(norm_mix_w, norm_ffn_w, norm_final_w),
                                   rows(m_norm_mix_w, m_norm_ffn_w, m_norm_final_w),
                                   rows(v_norm_mix_w, v_norm_ffn_w, v_norm_final_w))
    loss = sg[3, 0]

    def outs(k, small):
        big = [(u[k].T if n in tr else u[k])[None] for u, n in zip(upd, names)]
        return [small[0:1], big[0], big[1], small[1:2], big[2], big[3], big[4], small[2]]

    return (loss, gx[None], *outs(0, sg), *outs(1, sd), *outs(2, sm), *outs(3, sv))
```

```python
import math

import numpy as np
import jax
import jax.numpy as jnp
from jax import lax
from jax.experimental import pallas as pl
from jax.experimental.pallas import tpu as pltpu
from jax.experimental.pallas import tpu_sc as plsc

F32 = jnp.float32
BF16 = jnp.bfloat16

S = 2048
D = 2048
NDEV = 8
N_IN = 7168 // NDEV
N_FF = 5632 // NDEV
NFG, N_FG = NDEV // 2, 2 * N_FF
FF_PER, FF_ROWS = 4, N_FF // 2
TAIL_ROWS = 256
IN_ROUNDS = ((0, 512), (512, N_IN - 512))
N_OUT = 2048 // NDEV
AH, AHD = 8, 128
RH, RHD = 4, 256
CH = 128
NB = S // CH
EPS = 1e-6
PATTERNS = ((1, 16), (4, 4), (16, 1))
NEG = -1e30
VMEM_LIMIT = 56 * 1024 * 1024

ADAM_LR, ADAM_B1, ADAM_B2, ADAM_EPS, ADAM_WD, ADAM_STEP = 0.001, 0.9, 0.999, 1e-08, 0.01, 10
MESH = pl.DeviceIdType.MESH


def _cp(sem=None):
    return pltpu.CompilerParams(dimension_semantics=sem, vmem_limit_bytes=VMEM_LIMIT)


def _dot(a, b):
    return jnp.dot(a, b, preferred_element_type=F32)


def _dot_nt(a, b):
    return lax.dot_general(a, b, (((1,), (1,)), ((), ())), preferred_element_type=F32)


def _dot_tn(a, b):
    return lax.dot_general(a, b, (((0,), (0,)), ((), ())), preferred_element_type=F32)


def _sigmoid(x):
    return 0.5 * jnp.tanh(0.5 * x) + 0.5


def _cast_bf16(w, name):
    r, c = w.shape
    tm = r if r <= 1024 else 512

    def body(w_ref, o_ref):
        o_ref[...] = w_ref[...].astype(BF16)

    return pl.pallas_call(
        body, name=name, grid=(r // tm,),
        in_specs=[pl.BlockSpec((tm, c), lambda i: (i, 0))],
        out_specs=pl.BlockSpec((tm, c), lambda i: (i, 0)),
        out_shape=jax.ShapeDtypeStruct((r, c), BF16),
        compiler_params=_cp(("parallel",)),
    )(w)


def _rms_fwd(x, nw):
    tm = 256

    def body(x_ref, w_ref, h_ref, r_ref):
        xs = x_ref[...]
        r = lax.rsqrt(jnp.mean(xs * xs, axis=-1, keepdims=True) + EPS)
        h_ref[...] = ((xs * r) * w_ref[...]).astype(BF16)
        r_ref[...] = r

    return pl.pallas_call(
        body, name="rms_fwd", grid=(S // tm,),
        in_specs=[pl.BlockSpec((tm, D), lambda i: (i, 0)), pl.BlockSpec((1, D), lambda i: (0, 0))],
        out_specs=[pl.BlockSpec((tm, D), lambda i: (i, 0)), pl.BlockSpec((tm, 1), lambda i: (i, 0))],
        out_shape=[jax.ShapeDtypeStruct((S, D), BF16), jax.ShapeDtypeStruct((S, 1), F32)],
        compiler_params=_cp(("parallel",)),
    )(x, nw)


def _row_copies(hbm_refs, bufs, sems, m, tm):
    rows = pl.ds(pl.multiple_of(m * tm, tm), tm)
    return [pltpu.make_async_copy(h.at[rows], b, sems.at[i]) for i, (h, b) in enumerate(zip(hbm_refs, bufs))]


def _rms_bwd_tile(dh, xs, r, nw):
    dnw = jnp.sum(dh * (xs * r), axis=0, keepdims=True)
    gy = dh * nw
    dx = r * gy - xs * ((r * r * r) * jnp.mean(gy * xs, axis=-1, keepdims=True))
    return dx, dnw


def _cast_cols(w, name):
    r, c = w.shape
    tm = 512

    def body(w_ref, *o_refs):
        for o_ref, (off, width) in zip(o_refs, IN_ROUNDS):
            o_ref[...] = w_ref[:, off:off + width].astype(BF16)

    return pl.pallas_call(
        body, name=name, grid=(r // tm,),
        in_specs=[pl.BlockSpec((tm, c), lambda i: (i, 0))],
        out_specs=[pl.BlockSpec((tm, width), lambda i: (i, 0)) for _, width in IN_ROUNDS],
        out_shape=[jax.ShapeDtypeStruct((r, width), BF16) for _, width in IN_ROUNDS],
        compiler_params=_cp(("parallel",)),
    )(w)


def _proj_round(h1, win, k, before):
    tm = 1024
    nm = S // tm
    off, width = IN_ROUNDS[k]
    before = [] if before is None else [before]

    def body(a_ref, w_ref, *rest):
        o_hbm, o_buf, sems = rest[-3:]
        p, m = pl.program_id(0), pl.program_id(1)
        t = p * nm + m

        def out_copy(pp, mm, slot):
            cols = pl.ds(pl.multiple_of(pp * N_IN + off, 128), width)
            return pltpu.make_async_copy(o_buf.at[slot], o_hbm.at[pl.ds(pl.multiple_of(mm * tm, tm), tm), cols],
                                         sems.at[slot])

        @pl.when(t >= 2)
        def _():
            out_copy(p, m, t % 2).wait()

        o_buf[t % 2] = _dot(a_ref[...], w_ref[...])
        out_copy(p, m, t % 2).start()

        @pl.when(t == NDEV * nm - 1)
        def _():
            out_copy(p, m, (t + 1) % 2).wait()
            out_copy(p, m, t % 2).wait()

    return pl.pallas_call(
        body, name=f"proj_{k}", grid=(NDEV, nm),
        in_specs=[pl.BlockSpec((tm, D), lambda p, m: (m, 0)),
                  pl.BlockSpec((None, D, width), lambda p, m: (p, 0, 0))]
        + [pl.BlockSpec(memory_space=pl.ANY)] * len(before),
        out_specs=pl.BlockSpec(memory_space=pl.ANY),
        out_shape=jax.ShapeDtypeStruct((S, NDEV * N_IN), F32),
        scratch_shapes=[pltpu.VMEM((2, tm, width), F32), pltpu.SemaphoreType.DMA((2,))],
        input_output_aliases={2: 0} if before else {},
        compiler_params=_cp(("arbitrary", "arbitrary")),
    )(h1, win, *before)


def _proj(h1, wins):
    out = None
    for k, win in enumerate(wins):
        out = _proj_round(h1, win, k, out)
    return out


def _out_proj_rms(x, ma, mr, wout, nw):
    tm = 256
    half = D // 2

    def body(x_ref, ma_ref, mr_ref, w_ref, nw_ref, x2_ref, h_ref, r_ref):
        acc = _dot(ma_ref[...], w_ref[0:half, :]) + _dot(mr_ref[...], w_ref[half:D, :])
        x2 = x_ref[...] + acc
        r = lax.rsqrt(jnp.mean(x2 * x2, axis=-1, keepdims=True) + EPS)
        x2_ref[...] = x2
        h_ref[...] = ((x2 * r) * nw_ref[...]).astype(BF16)
        r_ref[...] = r

    return pl.pallas_call(
        body, name="out_proj_rms", grid=(S // tm,),
        in_specs=[pl.BlockSpec((tm, D), lambda i: (i, 0)),
                  pl.BlockSpec((tm, half), lambda i: (i, 0)),
                  pl.BlockSpec((tm, half), lambda i: (i, 0)),
                  pl.BlockSpec((D, D), lambda i: (0, 0)),
                  pl.BlockSpec((1, D), lambda i: (0, 0))],
        out_specs=[pl.BlockSpec((tm, D), lambda i: (i, 0)), pl.BlockSpec((tm, D), lambda i: (i, 0)),
                   pl.BlockSpec((tm, 1), lambda i: (i, 0))],
        out_shape=[jax.ShapeDtypeStruct((S, D), F32), jax.ShapeDtypeStruct((S, D), BF16),
                   jax.ShapeDtypeStruct((S, 1), F32)],
        compiler_params=_cp(("parallel",)),
    )(x, ma, mr, wout, nw)


def _ffn_up(h2, wgu, part, before=None):
    tm = 512

    def body(h_ref, w_ref, *rest):
        a_ref, dadg_ref, dadu_ref = rest[-3:]
        gu = _dot_nt(h_ref[...], w_ref[...])
        g, u = gu[:, 0:N_FG], gu[:, N_FG:2 * N_FG]
        sg = _sigmoid(g)
        silu = g * sg
        a_ref[...] = (silu * u).astype(BF16)
        dadg_ref[...] = (u * (sg * (1.0 + g * (1.0 - sg)))).astype(BF16)
        dadu_ref[...] = silu.astype(BF16)

    half = NFG // 2
    first = part * half
    before = list(before or [])
    blk = pl.BlockSpec((None, tm, N_FG), lambda p, m: (p + first, m, 0))
    return pl.pallas_call(
        body, name=f"ffn_up_{part}", grid=(half, S // tm),
        in_specs=[pl.BlockSpec((tm, D), lambda p, m: (m, 0)),
                  pl.BlockSpec((None, 2 * N_FG, D), lambda p, m: (p, 0, 0))]
        + [pl.BlockSpec(memory_space=pl.ANY)] * len(before),
        out_specs=[blk, blk, blk],
        out_shape=[jax.ShapeDtypeStruct((NFG, S, N_FG), BF16)] * 3,
        input_output_aliases={2 + k: k for k in range(len(before))},
        compiler_params=_cp(("parallel", "parallel")),
    )(h2, wgu, *before)


def _ffn_down_first(x2, a, wd):
    tm = 512
    n = wd.shape[0]

    def body(x_ref, a_ref, w_ref, o_ref):
        p = pl.program_id(1)

        @pl.when(p == 0)
        def _():
            o_ref[...] = x_ref[...] + _dot(a_ref[...], w_ref[0])

        @pl.when(p > 0)
        def _():
            o_ref[...] += _dot(a_ref[...], w_ref[p])

    return pl.pallas_call(
        body, name="ffn_down_first", grid=(S // tm, n),
        in_specs=[pl.BlockSpec((tm, D), lambda m, p: (m, 0)),
                  pl.BlockSpec((None, tm, N_FG), lambda m, p: (p, m, 0)),
                  pl.BlockSpec((n, N_FG, D), lambda m, p: (0, 0, 0))],
        out_specs=pl.BlockSpec((tm, D), lambda m, p: (m, 0)),
        out_shape=jax.ShapeDtypeStruct((S, D), F32),
        compiler_params=_cp(("parallel", "arbitrary")),
    )(x2, a, wd)


def _ffn_down_loss(x2, a, wd, nw, tgt):
    tm = 512
    first = NFG - wd.shape[0]

    def body(x2_hbm, a_ref, w_ref, nw_ref, t_hbm, dx_ref, dxb_ref, st_ref, acc_ref, x2_buf, t_buf, sems):
        m, p = pl.program_id(0), pl.program_id(1)
        tail_in = _row_copies((x2_hbm, t_hbm), (x2_buf, t_buf), sems, m, tm)

        @pl.when(p == 0)
        def _():
            acc_ref[...] = jnp.zeros_like(acc_ref)
            for cp in tail_in:
                cp.start()

        @pl.when((p == 0) & (m == 0))
        def _():
            st_ref[...] = jnp.zeros_like(st_ref)

        acc_ref[...] += _dot(a_ref[...], w_ref[p])

        @pl.when(p == NFG - first - 1)
        def _():
            for cp in tail_in:
                cp.wait()
            x3 = x2_buf[...] + acc_ref[...]
            nwv = nw_ref[...]
            r = lax.rsqrt(jnp.mean(x3 * x3, axis=-1, keepdims=True) + EPS)
            y = (x3 * r) * nwv
            err = y - t_buf[...]
            loss = 0.5 * jnp.sum(jnp.mean(err * err, axis=-1, keepdims=True), axis=0, keepdims=True)
            dy = err * (1.0 / D)
            dx, dnw = _rms_bwd_tile(dy, x3, r, nwv)
            dx_ref[...] = dx
            dxb_ref[...] = dx.astype(BF16)
            st_ref[0:1, :] += dnw
            st_ref[1:2, :] += jnp.broadcast_to(loss, (1, D))

    return pl.pallas_call(
        body, name="ffn_down_loss", grid=(S // tm, NFG - first),
        in_specs=[pl.BlockSpec(memory_space=pl.ANY),
                  pl.BlockSpec((None, tm, N_FG), lambda m, p: (p + first, m, 0)),
                  pl.BlockSpec((NFG - first, N_FG, D), lambda m, p: (0, 0, 0)),
                  pl.BlockSpec((1, D), lambda m, p: (0, 0)),
                  pl.BlockSpec(memory_space=pl.ANY)],
        out_specs=[pl.BlockSpec((tm, D), lambda m, p: (m, 0)), pl.BlockSpec((tm, D), lambda m, p: (m, 0)),
                   pl.BlockSpec((8, D), lambda m, p: (0, 0))],
        out_shape=[jax.ShapeDtypeStruct((S, D), F32), jax.ShapeDtypeStruct((S, D), BF16),
                   jax.ShapeDtypeStruct((8, D), F32)],
        scratch_shapes=[pltpu.VMEM((tm, D), F32), pltpu.VMEM((tm, D), F32), pltpu.VMEM((tm, D), F32),
                        pltpu.SemaphoreType.DMA((2,))],
        compiler_params=_cp(("arbitrary", "arbitrary")),
    )(x2, a, wd, nw, tgt)


def _ffn_down_bwd(dx3b, wd, dadg, dadu, part, before=None):
    tm = 1024
    half = NFG // 2

    def body(dx_ref, w_ref, dadg_ref, dadu_ref, *rest):
        dgu_ref = rest[-1]
        rows = pl.ds(pl.multiple_of(pl.program_id(1) * tm, tm), tm)
        da = _dot_nt(dx_ref[rows, :], w_ref[...])
        dgu_ref[:, 0:N_FG] = (da * dadg_ref[...].astype(F32)).astype(BF16)
        dgu_ref[:, N_FG:2 * N_FG] = (da * dadu_ref[...].astype(F32)).astype(BF16)

    blk = pl.BlockSpec((None, tm, N_FG), lambda p, m: (p + part * half, m, 0))
    before = list(before or [])
    return pl.pallas_call(
        body, name=f"ffn_down_bwd_{part}", grid=(half, S // tm),
        in_specs=[pl.BlockSpec((S, D), lambda p, m: (0, 0)),
                  pl.BlockSpec((None, N_FG, D), lambda p, m: (p, 0, 0)), blk, blk]
        + [pl.BlockSpec(memory_space=pl.ANY)] * len(before),
        out_specs=pl.BlockSpec((None, tm, 2 * N_FG), lambda p, m: (p + part * half, m, 0)),
        out_shape=jax.ShapeDtypeStruct((NFG, S, 2 * N_FG), BF16),
        input_output_aliases={4 + k: k for k in range(len(before))},
        compiler_params=_cp(("parallel", "parallel")),
    )(dx3b, wd, dadg, dadu, *before)


def _ffn_up_bwd(dgu, wgu_a, wgu_b, dres, xs, r, nw):
    tm = 512
    nm = S // tm
    na = wgu_a.shape[0]

    def body(dgu_ref, wa_hbm, wb_hbm, dres_hbm, x_hbm, r_ref, nw_ref, dx_ref, dxb_ref, st_ref,
             w_buf, dres_buf, x_buf, sems, w_sems):
        m, p = pl.program_id(0), pl.program_id(1)
        tail_in = _row_copies((dres_hbm, x_hbm), (dres_buf, x_buf), sems, m, tm)

        def fetch(g, slot):
            for src, lo in ((wa_hbm, 0), (wb_hbm, na)):
                @pl.when((g >= lo) & (g < lo + na))
                def _():
                    pltpu.make_async_copy(src.at[g - lo], w_buf.at[slot], w_sems.at[slot]).start()

        @pl.when((p == 0) & (m == 0))
        def _():
            st_ref[...] = jnp.zeros_like(st_ref)
            fetch(p, 0)

        @pl.when((p < NFG - 1) | (m < nm - 1))
        def _():
            fetch((p + 1) % NFG, (p + 1) % 2)

        @pl.when(p == 0)
        def _():
            dx_ref[...] = jnp.zeros_like(dx_ref)
            for cp in tail_in:
                cp.start()

        slot = p % 2
        pltpu.make_async_copy(wa_hbm.at[0], w_buf.at[slot], w_sems.at[slot]).wait()
        dx_ref[...] += _dot(dgu_ref[...], w_buf[slot])

        @pl.when(p == NFG - 1)
        def _():
            for cp in tail_in:
                cp.wait()
            dx, dnw = _rms_bwd_tile(dx_ref[...], x_buf[...], r_ref[...], nw_ref[...])
            dx = dres_buf[...] + dx
            dx_ref[...] = dx
            dxb_ref[...] = dx.astype(BF16)
            st_ref[0:1, :] += dnw

    blk = pl.BlockSpec((None, tm, 2 * N_FG), lambda m, p: (p, m, 0))
    row = pl.BlockSpec((tm, D), lambda m, p: (m, 0))
    hbm = pl.BlockSpec(memory_space=pl.ANY)
    return pl.pallas_call(
        body, name="ffn_up_bwd", grid=(nm, NFG),
        in_specs=[blk, hbm, hbm, hbm, hbm, pl.BlockSpec((tm, 1), lambda m, p: (m, 0)),
                  pl.BlockSpec((1, D), lambda m, p: (0, 0))],
        out_specs=[row, row, pl.BlockSpec((8, D), lambda m, p: (0, 0))],
        out_shape=[jax.ShapeDtypeStruct((S, D), F32), jax.ShapeDtypeStruct((S, D), BF16),
                   jax.ShapeDtypeStruct((8, D), F32)],
        scratch_shapes=[pltpu.VMEM((2, 2 * N_FG, D), BF16), pltpu.VMEM((tm, D), F32), pltpu.VMEM((tm, D), F32),
                        pltpu.SemaphoreType.DMA((2,)), pltpu.SemaphoreType.DMA((2,))],
        compiler_params=_cp(("arbitrary", "arbitrary")),
    )(dgu, wgu_a, wgu_b, dres, xs, r, nw)


def _out_proj_bwd(dx2b, wout, place=None, rider=None):
    tm = 256

    if rider is None:
        def body(dx_ref, w_ref, o_ref):
            o_ref[...] = _dot_nt(dx_ref[...], w_ref[...])

        return pl.pallas_call(
            body, name="out_proj_bwd", grid=(S // tm,),
            in_specs=[pl.BlockSpec((tm, D), lambda i: (i, 0)), pl.BlockSpec((D, D), lambda i: (0, 0))],
            out_specs=pl.BlockSpec((tm, D), lambda i: (i, 0)),
            out_shape=jax.ShapeDtypeStruct((S, D), F32),
            compiler_params=_cp(("parallel",)),
        )(dx2b, wout), None

    w = rider[0]
    r, c = w.shape
    rt = _row_tile(r, c)
    nt = r // rt
    tm = S // nt

    def body(pos_ref, dx_ref, w_ref, uw, um, uv, ug, us, uc, o_ref, go, dd, mo, vo):
        o_ref[...] = _dot_nt(dx_ref[...], w_ref[...])

        _update_tile(uw, um, uv, ug, us, uc, go, dd, mo, vo)

    def at(i):
        return jnp.minimum(i, nt - 1)

    tile = pl.BlockSpec((rt, c), lambda i, pos: (at(i), 0))
    outs = pl.pallas_call(
        body, name="out_proj_bwd",
        grid_spec=pltpu.PrefetchScalarGridSpec(
            num_scalar_prefetch=1, grid=(S // tm,),
            in_specs=[pl.BlockSpec((tm, D), lambda i, pos: (i, 0)),
                      pl.BlockSpec((D, D), lambda i, pos: (0, 0), pipeline_mode=pl.Buffered(1)),
                      tile, tile, tile,
                      pl.BlockSpec((None, rt, c), lambda i, pos: (4 * pos[0] + 2 * pos[1] + pos[2], at(i), 0)),
                      pl.BlockSpec((None, rt, c), lambda i, pos: (2 * pos[0] + pos[1], at(i), 0)),
                      pl.BlockSpec((3, rt, c), lambda i, pos: (0, at(i), 0))],
            out_specs=[pl.BlockSpec((tm, D), lambda i, pos: (i, 0)), tile, tile, tile, tile]),
        out_shape=[jax.ShapeDtypeStruct((S, D), F32)] + [jax.ShapeDtypeStruct((r, c), F32)] * 4,
        compiler_params=_cp(("arbitrary",)),
    )(place, dx2b, wout, *rider)
    return outs[0], outs[1:]


def _in_proj_bwd(dproj, wins, dres, xs, r, nw):
    tm = 1024

    nr = len(wins)
    nm = S // tm

    def body(dp_ref, *rest):
        w_hbms = rest[:nr]
        dres_hbm, x_hbm, r_ref, nw_ref, dx_ref, st_ref, w_buf, dres_buf, x_buf, sems, w_sems = rest[nr:]
        m, p = pl.program_id(0), pl.program_id(1)
        tail_in = _row_copies((dres_hbm, x_hbm), (dres_buf, x_buf), sems, m, tm)

        def w_copies(g, slot):
            return [pltpu.make_async_copy(w_hbm.at[g], w_buf.at[slot, pl.ds(0, D), pl.ds(off, width)],
                                          w_sems.at[slot, k])
                    for k, (w_hbm, (off, width)) in enumerate(zip(w_hbms, IN_ROUNDS))]

        @pl.when((p == 0) & (m == 0))
        def _():
            st_ref[...] = jnp.zeros_like(st_ref)
            for cp in w_copies(p, 0):
                cp.start()

        @pl.when((p < NDEV - 1) | (m < nm - 1))
        def _():
            for cp in w_copies((p + 1) % NDEV, (p + 1) % 2):
                cp.start()

        @pl.when(p == 0)
        def _():
            dx_ref[...] = jnp.zeros_like(dx_ref)
            for cp in tail_in:
                cp.start()

        for cp in w_copies(p, p % 2):
            cp.wait()

        @pl.when(p < NDEV - 1)
        def _():
            dx_ref[...] += _dot_nt(dp_ref[...], w_buf[p % 2])

        @pl.when(p == NDEV - 1)
        def _():
            for cp in tail_in:
                cp.wait()
            dnw_sum = jnp.zeros((1, D), F32)
            for lo in range(0, tm, TAIL_ROWS):
                rows = slice(lo, lo + TAIL_ROWS)
                dh = dx_ref[rows, :] + _dot_nt(dp_ref[rows, :], w_buf[p % 2])
                dx, dnw = _rms_bwd_tile(dh, x_buf[rows, :], r_ref[rows, :], nw_ref[...])
                dx_ref[rows, :] = dres_buf[rows, :] + dx
                dnw_sum = dnw_sum + dnw
            st_ref[0:1, :] += dnw_sum

    row = pl.BlockSpec((tm, D), lambda m, p: (m, 0))
    hbm = pl.BlockSpec(memory_space=pl.ANY)
    return pl.pallas_call(
        body, name="in_proj_bwd", grid=(S // tm, NDEV),
        in_specs=[pl.BlockSpec((tm, N_IN), lambda m, p: (m, p)),
                  *[hbm] * nr,
                  hbm, hbm, pl.BlockSpec((tm, 1), lambda m, p: (m, 0)),
                  pl.BlockSpec((1, D), lambda m, p: (0, 0))],
        out_specs=[row, pl.BlockSpec((8, D), lambda m, p: (0, 0))],
        out_shape=[jax.ShapeDtypeStruct((S, D), F32), jax.ShapeDtypeStruct((8, D), F32)],
        scratch_shapes=[pltpu.VMEM((2, D, N_IN), BF16), pltpu.VMEM((tm, D), F32), pltpu.VMEM((tm, D), F32),
                        pltpu.SemaphoreType.DMA((2,)), pltpu.SemaphoreType.DMA((2, nr))],
        compiler_params=_cp(("arbitrary", "arbitrary")),
    )(dproj, *wins, dres, xs, r, nw)


W_IN_PARTS = 2


def _wgrad_in(h1, dproj, part):
    rows = D // W_IN_PARTS

    def body(a_ref, d_ref, o_ref):
        both = _dot_tn(a_ref[...], d_ref[...]).astype(BF16)
        o_ref[0] = both[:, 0:N_IN]
        o_ref[1] = both[:, N_IN:2 * N_IN]

    return pl.pallas_call(
        body, name=f"wgrad_in_{part}", grid=(NDEV // 2,),
        in_specs=[pl.BlockSpec((S, rows), lambda p: (0, part)), pl.BlockSpec((S, 2 * N_IN), lambda p: (0, p))],
        out_specs=pl.BlockSpec((2, rows, N_IN), lambda p: (p, 0, 0)),
        out_shape=jax.ShapeDtypeStruct((NDEV, rows, N_IN), BF16),
        compiler_params=_cp(("parallel",)),
    )(h1, dproj)


def _wgrad_rows(a3, dy, name, col=0):
    def body(a_ref, d_ref, o_ref):
        dw = _dot_tn(a_ref[...], d_ref[...]).astype(BF16)
        for j in range(FF_PER):
            o_ref[j] = dw[j * FF_ROWS:(j + 1) * FF_ROWS]

    return pl.pallas_call(
        body, name=name, grid=(NFG,),
        in_specs=[pl.BlockSpec((None, S, N_FG), lambda p: (p, 0, col)), pl.BlockSpec((S, D), lambda p: (0, 0))],
        out_specs=pl.BlockSpec((FF_PER, FF_ROWS, D), lambda p: (p % 2, p // 2, 0)),
        out_shape=jax.ShapeDtypeStruct((NDEV, N_FF, D), BF16),
        compiler_params=_cp(("parallel",)),
    )(a3, dy)


def _wgrad_out(ma, mr, dx2b):
    half = D // 2
    per = half // N_OUT

    def body(ma_ref, mr_ref, d_ref, o_ref):
        p = pl.program_id(0)

        @pl.when(p == 0)
        def _():
            o_ref[...] = _dot_tn(ma_ref[...], d_ref[...]).astype(BF16).reshape(per, N_OUT, D)

        @pl.when(p == 1)
        def _():
            o_ref[...] = _dot_tn(mr_ref[...], d_ref[...]).astype(BF16).reshape(per, N_OUT, D)

    whole = pl.BlockSpec((S, half), lambda p: (0, 0))
    return pl.pallas_call(
        body, name="wgrad_out", grid=(2,),
        in_specs=[whole, whole, pl.BlockSpec((S, D), lambda p: (0, 0))],
        out_specs=pl.BlockSpec((per, N_OUT, D), lambda p: (p, 0, 0)),
        out_shape=jax.ShapeDtypeStruct((NDEV, N_OUT, D), BF16),
        compiler_params=_cp(("parallel",)),
    )(ma, mr, dx2b)


def _attn_consts():
    c = np.zeros((AH, 8, AHD), np.float32)
    for h in range(AH):
        c[h, :, :] = 2.0 ** (-(h + 1))
    return jnp.asarray(c)


def _permute_in(dst, src, d, cast=None):
    v = src[...]
    if d > 1:
        v = pltpu.einshape("jrc->rjc", v.reshape(S // d, d, AHD)).reshape(S, AHD)
    dst[...] = v if cast is None else v.astype(cast)


def _natural_order(v, d):
    if d == 1:
        return v
    return pltpu.einshape("rjc->jrc", v.reshape(d, S // d, AHD)).reshape(S, AHD)


def _attn_masks():
    qi = lax.broadcasted_iota(jnp.int32, (CH, CH), 0)
    kj = lax.broadcasted_iota(jnp.int32, (CH, CH), 1)
    dist_c = (qi - kj).astype(F32)
    dist_p = (qi - kj + CH).astype(F32)
    return (qi >= kj)[None], (kj >= qi)[None], dist_c[None], dist_p[None]


GB = 16


def _bdot_nt(a, b):
    return lax.dot_general(a, b, (((2,), (2,)), ((0,), (0,))), preferred_element_type=F32)


def _bdot(a, b):
    return lax.dot_general(a, b, (((2,), (1,)), ((0,), (0,))), preferred_element_type=F32)


def _bdot_tn(a, b):
    return lax.dot_general(a, b, (((1,), (1,)), ((0,), (0,))), preferred_element_type=F32)


def _shift_block(dst, src):
    dst[0:CH, :] = jnp.zeros((CH, AHD), dst.dtype)
    dst[CH:S, :] = src[0:S - CH, :]


def _has_prev(g, nb):
    blk = lax.broadcasted_iota(jnp.int32, (GB, 1, 1), 0) + g * GB
    return (blk & (nb - 1)) != 0


def _blocks(ref, g):
    return ref[g * GB * CH:(g + 1) * GB * CH, :].reshape(GB, CH, AHD)


def _attn_fwd(proj):
    scale = 1.0 / math.sqrt(AHD)

    def body(c_ref, q_ref, k_ref, v_ref, o_ref, ob_ref, lse_ref, qkvp_ref, lsep_ref, qd, kd, vd, kps, vps, od, ld, *nat):
        onat, lnat = nat[0:3], nat[3:6]
        slope = c_ref[0:1, :]
        mask_c, mask_p, dist_c, dist_p = _attn_masks()
        for pi, (d, nb) in enumerate(PATTERNS):
            _permute_in(qd, q_ref, d, BF16)
            _permute_in(kd, k_ref, d, BF16)
            _permute_in(vd, v_ref, d, BF16)
            if d > 1:
                qkvp_ref[pi - 1, 0] = qd[...]
                qkvp_ref[pi - 1, 1] = kd[...]
                qkvp_ref[pi - 1, 2] = vd[...]
            if nb > 1:
                _shift_block(kps, kd)
                _shift_block(vps, vd)
            bias_c = -(slope * float(d)) * dist_c
            bias_p = -(slope * float(d)) * dist_p
            for g in range(NB // GB):
                q3, k3, v3 = _blocks(qd, g), _blocks(kd, g), _blocks(vd, g)
                s_c = jnp.where(mask_c, _bdot_nt(q3, k3) * scale + bias_c, NEG)
                mx = jnp.max(s_c, axis=-1, keepdims=True)
                if nb > 1:
                    kp3, vp3 = _blocks(kps, g), _blocks(vps, g)
                    s_p = jnp.where(jnp.logical_and(mask_p, _has_prev(g, nb)),
                                    _bdot_nt(q3, kp3) * scale + bias_p, NEG)
                    mx = jnp.maximum(mx, jnp.max(s_p, axis=-1, keepdims=True))
                    l = (jnp.sum(jnp.exp(s_c - mx), axis=-1, keepdims=True)
                         + jnp.sum(jnp.exp(s_p - mx), axis=-1, keepdims=True))
                    lse = mx + jnp.log(l)
                    o3 = _bdot(jnp.exp(s_c - lse).astype(BF16), v3) + _bdot(jnp.exp(s_p - lse).astype(BF16), vp3)
                else:
                    l = jnp.sum(jnp.exp(s_c - mx), axis=-1, keepdims=True)
                    lse = mx + jnp.log(l)
                    o3 = _bdot(jnp.exp(s_c - lse).astype(BF16), v3)
                rows = slice(g * GB * CH, (g + 1) * GB * CH)
                od[rows, :] = o3.reshape(GB * CH, AHD)
                ld[rows, :] = jnp.broadcast_to(lse, (GB, CH, AHD)).reshape(GB * CH, AHD)
            onat[pi][...] = _natural_order(od[...], d)
            lnat[pi][...] = _natural_order(ld[...], d)
        l0, l1, l2 = lnat[0][...], lnat[1][...], lnat[2][...]
        mx = jnp.maximum(jnp.maximum(l0, l1), l2)
        e0, e1, e2 = jnp.exp(l0 - mx), jnp.exp(l1 - mx), jnp.exp(l2 - mx)
        den = e0 + e1 + e2
        out = (e0 / den) * onat[0][...] + (e1 / den) * onat[1][...] + (e2 / den) * onat[2][...]
        o_ref[...] = out
        ob_ref[...] = out.astype(BF16)
        lse_ref[...] = mx + jnp.log(den)
        for pi, (d, _) in enumerate(PATTERNS[1:]):
            _permute_in(lsep_ref.at[pi], lse_ref, d)

    def col(off):
        return pl.BlockSpec((S, AHD), lambda h: (0, off + h))

    return pl.pallas_call(
        body, name="attn_fwd", grid=(AH,),
        in_specs=[pl.BlockSpec((None, 8, AHD), lambda h: (h, 0, 0)), col(0), col(AH), col(2 * AH)],
        out_specs=[col(0), col(0), col(0), pl.BlockSpec((2, 3, S, AHD), lambda h: (0, 0, 0, h)),
                   pl.BlockSpec((2, S, AHD), lambda h: (0, 0, h))],
        out_shape=[jax.ShapeDtypeStruct((S, AH * AHD), F32), jax.ShapeDtypeStruct((S, AH * AHD), BF16),
                   jax.ShapeDtypeStruct((S, AH * AHD), F32),
                   jax.ShapeDtypeStruct((2, 3, S, AH * AHD), BF16), jax.ShapeDtypeStruct((2, S, AH * AHD), F32)],
        scratch_shapes=[pltpu.VMEM((S, AHD), BF16) for _ in range(5)]
        + [pltpu.VMEM((S, AHD), F32) for _ in range(8)],
        compiler_params=_cp(("parallel",)),
    )(_attn_consts(), proj, proj, proj)


def _attn_bwd(proj, dmixed, o, lse, qkvp, lsep):
    scale = 1.0 / math.sqrt(AHD)

    def body(c_ref, q_ref, k_ref, v_ref, do_ref, o_ref, lse_ref, qkvp_ref, lsep_ref, dproj_hbm,
             qd, kd, vd, dod, kps, vps, dld, dqd, dkd, dvd, delta, aq, ak, av, sq, sk, sv, sems):
        h = pl.program_id(0)

        def out_copies(head):
            return [pltpu.make_async_copy(
                st, dproj_hbm.at[:, pl.ds(pl.multiple_of((k * AH + head) * AHD, AHD), AHD)], sems.at[k])
                for k, st in enumerate((sq, sk, sv))]

        slope = c_ref[0:1, :]
        mask_c, mask_p, dist_c, dist_p = _attn_masks()
        delta[...] = jnp.broadcast_to(jnp.sum(do_ref[...] * o_ref[...], axis=-1, keepdims=True), (S, AHD))
        for pi, (d, nb) in enumerate(PATTERNS):
            if d == 1:
                _permute_in(qd, q_ref, d, BF16)
                _permute_in(kd, k_ref, d, BF16)
                _permute_in(vd, v_ref, d, BF16)
                qs, ks, vs, lss = qd, kd, vd, lse_ref
            else:
                qs, ks, vs, lss = (qkvp_ref.at[pi - 1, 0], qkvp_ref.at[pi - 1, 1], qkvp_ref.at[pi - 1, 2],
                                   lsep_ref.at[pi - 1])
            _permute_in(dod, do_ref, d, BF16)
            _permute_in(dld, delta, d)
            if nb > 1:
                _shift_block(kps, ks)
                _shift_block(vps, vs)
            bias_c = -(slope * float(d)) * dist_c
            bias_p = -(slope * float(d)) * dist_p
            for g in range(NB // GB):
                q3, k3, v3, do3 = _blocks(qs, g), _blocks(ks, g), _blocks(vs, g), _blocks(dod, g)
                ls, dl = _blocks(lss, g), _blocks(dld, g)
                lo, hi = g * GB * CH, (g + 1) * GB * CH
                p_c = jnp.exp(jnp.where(mask_c, _bdot_nt(q3, k3) * scale + bias_c, NEG) - ls)
                ds_c = ((p_c * (_bdot_nt(do3, v3) - dl)) * scale).astype(BF16)
                dq3 = _bdot(ds_c, k3)
                dkd[lo:hi, :] = _bdot_tn(ds_c, q3).reshape(GB * CH, AHD)
                dvd[lo:hi, :] = _bdot_tn(p_c.astype(BF16), do3).reshape(GB * CH, AHD)
                if nb > 1:
                    kp3, vp3 = _blocks(kps, g), _blocks(vps, g)
                    p_p = jnp.exp(jnp.where(jnp.logical_and(mask_p, _has_prev(g, nb)),
                                            _bdot_nt(q3, kp3) * scale + bias_p, NEG) - ls)
                    ds_p = ((p_p * (_bdot_nt(do3, vp3) - dl)) * scale).astype(BF16)
                    dq3 = dq3 + _bdot(ds_p, kp3)
                    dkp = _bdot_tn(ds_p, q3).reshape(GB * CH, AHD)
                    dvp = _bdot_tn(p_p.astype(BF16), do3).reshape(GB * CH, AHD)
                    if g == 0:
                        dkd[0:hi - CH, :] += dkp[CH:, :]
                        dvd[0:hi - CH, :] += dvp[CH:, :]
                    else:
                        dkd[lo - CH:hi - CH, :] += dkp
                        dvd[lo - CH:hi - CH, :] += dvp
                dqd[lo:hi, :] = dq3.reshape(GB * CH, AHD)
            ln = S // d
            for acc, src in ((aq, dqd), (ak, dkd), (av, dvd)):
                if pi == 0:
                    acc[...] = src[...]
                else:
                    acc[...] += _natural_order(src[...], d)

        @pl.when(h > 0)
        def _():
            for cp in out_copies(h - 1):
                cp.wait()

        sq[...] = aq[...].astype(BF16)
        sk[...] = ak[...].astype(BF16)
        sv[...] = av[...].astype(BF16)
        for cp in out_copies(h):
            cp.start()

        @pl.when(h == AH - 1)
        def _():
            for cp in out_copies(h):
                cp.wait()

    def col(off):
        return pl.BlockSpec((S, AHD), lambda h: (0, off + h))

    return pl.pallas_call(
        body, name="attn_bwd", grid=(AH,),
        in_specs=[pl.BlockSpec((None, 8, AHD), lambda h: (h, 0, 0)), col(0), col(AH), col(2 * AH),
                  col(0), col(0), col(0), pl.BlockSpec((2, 3, S, AHD), lambda h: (0, 0, 0, h)),
                  pl.BlockSpec((2, S, AHD), lambda h: (0, 0, h))],
        out_specs=pl.BlockSpec(memory_space=pl.ANY),
        out_shape=jax.ShapeDtypeStruct((S, NDEV * N_IN), BF16),
        scratch_shapes=[pltpu.VMEM((S, AHD), BF16) for _ in range(6)]
        + [pltpu.VMEM((S, AHD), F32) for _ in range(8)]
        + [pltpu.VMEM((S, AHD), BF16) for _ in range(3)] + [pltpu.SemaphoreType.DMA((3,))],
        compiler_params=_cp(("arbitrary",)),
    )(_attn_consts(), proj, proj, proj, dmixed, o, lse, qkvp, lsep)


def _ret_consts():
    c = np.zeros((RH, 8, RHD), np.float32)
    for h in range(RH):
        c[h, :, :] = np.log(np.float32(1.0) - np.float32(2.0 ** (-5.0 - h)))
    return jnp.asarray(c)


def _ret_factors(lg):
    i = lax.broadcasted_iota(jnp.int32, (CH, CH), 0)
    j = lax.broadcasted_iota(jnp.int32, (CH, CH), 1)
    dif = (i - j).astype(F32)
    decay = jnp.where(dif >= 0, jnp.exp(lg[:, 0:CH] * jnp.maximum(dif, 0.0)), 0.0)
    row = lax.broadcasted_iota(jnp.int32, (CH, RHD), 0).astype(F32)
    zeta = jnp.exp(lg * (CH - 1.0 - row))
    xi = jnp.exp(lg * (row + 1.0))
    return decay, zeta, xi, jnp.exp(lg * float(CH))


CBK = 8
RSTEPS = NB // CBK


def _ret_specs(rev):
    off = 3 * AH * AHD // RHD
    rows = CBK * CH

    def ch(n):
        return (RSTEPS - 1 - n) if rev else n

    def col(k):
        return pl.BlockSpec((rows, RHD), lambda h, n: (ch(n), off + k * RH + h))

    own = pl.BlockSpec((rows, RHD), lambda h, n: (ch(n), h))
    state = pl.BlockSpec((None, CBK, RHD, RHD), lambda h, n: (h, ch(n), 0, 0))
    const = pl.BlockSpec((None, 8, RHD), lambda h, n: (h, 0, 0))
    dm = pl.BlockSpec((rows, RHD), lambda h, n: (ch(n), AH * AHD // RHD + h))
    return col, own, state, const, dm


def _chunks(x):
    return x.reshape(CBK, CH, RHD)


def _ret_fwd(proj):
    def body(c_ref, q_ref, k_ref, v_ref, g_ref, ret_ref, mr_ref, st_ref, r_acc):
        n = pl.program_id(1)

        @pl.when(n == 0)
        def _():
            r_acc[...] = jnp.zeros_like(r_acc)

        decay, zeta, xi, gch = _ret_factors(c_ref[0:1, :])
        q3 = _chunks(q_ref[...].astype(BF16))
        kc = _chunks(k_ref[...] * (1.0 / math.sqrt(RHD)))
        k3 = kc.astype(BF16)
        v3 = _chunks(v_ref[...].astype(BF16))
        kv3 = _bdot_tn((kc * zeta[None]).astype(BF16), v3)
        r = r_acc[...]
        for i in range(CBK):
            st_ref[i] = r.astype(BF16)
            r = r * gch + kv3[i]
        r_acc[...] = r
        scores = _bdot_nt(q3, k3) * decay[None]
        ret = (_bdot(scores.astype(BF16), v3) + _bdot(q3, st_ref[...]) * xi[None]).reshape(CBK * CH, RHD)
        ret_ref[...] = ret
        rr = lax.rsqrt(jnp.mean(ret * ret, axis=-1, keepdims=True) + EPS)
        gv = g_ref[...]
        mr_ref[...] = ((gv * _sigmoid(gv)) * (ret * rr)).astype(BF16)

    col, own, state, const, _ = _ret_specs(False)
    return pl.pallas_call(
        body, name="ret_fwd", grid=(RH, RSTEPS),
        in_specs=[const, col(0), col(1), col(2), col(3)],
        out_specs=[own, own, state],
        out_shape=[jax.ShapeDtypeStruct((S, RH * RHD), F32), jax.ShapeDtypeStruct((S, RH * RHD), BF16),
                   jax.ShapeDtypeStruct((RH, NB, RHD, RHD), BF16)],
        scratch_shapes=[pltpu.VMEM((RHD, RHD), F32)],
        compiler_params=_cp(("parallel", "arbitrary")),
    )(_ret_consts(), proj, proj, proj, proj)


def _ret_bwd(proj, ret, states, dmixed, dproj):
    rows = CBK * CH
    col0 = 3 * AH * AHD

    def body(c_ref, q_ref, k_ref, v_ref, g_ref, ret_ref, st_ref, dm_ref, dproj_in, dproj_hbm, g_acc, gs,
             sq, sk, sv, sg, sems):
        del dproj_in
        h, n = pl.program_id(0), pl.program_id(1)
        step = h * RSTEPS + n

        def out_copies(t):
            hh, nn = t // RSTEPS, t % RSTEPS
            r0 = pl.multiple_of((RSTEPS - 1 - nn) * rows, rows)
            return [pltpu.make_async_copy(
                st, dproj_hbm.at[pl.ds(r0, rows), pl.ds(pl.multiple_of(col0 + (k * RH + hh) * RHD, RHD), RHD)],
                sems.at[k]) for k, st in enumerate((sq, sk, sv, sg))]

        @pl.when(n == 0)
        def _():
            g_acc[...] = jnp.zeros_like(g_acc)

        decay, zeta, xi, gch = _ret_factors(c_ref[0:1, :])
        ret_v = ret_ref[...]
        rr = lax.rsqrt(jnp.mean(ret_v * ret_v, axis=-1, keepdims=True) + EPS)
        gv = g_ref[...]
        sgm = _sigmoid(gv)
        dmix = dm_ref[...]
        dgate = ((dmix * (ret_v * rr)) * (sgm * (1.0 + gv * (1.0 - sgm)))).astype(BF16)
        dretn = dmix * (gv * sgm)
        dret = _chunks(rr * dretn - ret_v * ((rr * rr * rr) * jnp.mean(dretn * ret_v, axis=-1, keepdims=True)))

        q3 = _chunks(q_ref[...].astype(BF16))
        kc = _chunks(k_ref[...] * (1.0 / math.sqrt(RHD)))
        k3 = kc.astype(BF16)
        v3 = _chunks(v_ref[...].astype(BF16))
        d3 = dret.astype(BF16)
        dxi = (dret * xi[None]).astype(BF16)
        kz = (kc * zeta[None]).astype(BF16)
        dr3 = _bdot_tn(q3, dxi)
        acc = g_acc[...]
        for i in reversed(range(CBK)):
            gs[i] = acc.astype(BF16)
            acc = dr3[i] + gch * acc
        g_acc[...] = acc
        g3 = gs[...]
        sc = (_bdot_nt(q3, k3) * decay[None]).astype(BF16)
        da = (_bdot_nt(d3, v3) * decay[None]).astype(BF16)
        dq = _bdot(da, k3) + _bdot_nt(dxi, st_ref[...])
        dkc = _bdot_tn(da, q3) + _bdot_nt(v3, g3) * zeta[None]
        dv = _bdot_tn(sc, d3) + _bdot(kz, g3)

        @pl.when(step > 0)
        def _():
            for cp in out_copies(step - 1):
                cp.wait()

        sq[...] = dq.reshape(rows, RHD).astype(BF16)
        sk[...] = (dkc * (1.0 / math.sqrt(RHD))).reshape(rows, RHD).astype(BF16)
        sv[...] = dv.reshape(rows, RHD).astype(BF16)
        sg[...] = dgate
        for cp in out_copies(step):
            cp.start()

        @pl.when(step == RH * RSTEPS - 1)
        def _():
            for cp in out_copies(step):
                cp.wait()

    col, own, state, const, dm = _ret_specs(True)
    hbm = pl.BlockSpec(memory_space=pl.ANY)
    return pl.pallas_call(
        body, name="ret_bwd", grid=(RH, RSTEPS),
        in_specs=[const, col(0), col(1), col(2), col(3), own, state, dm, hbm],
        out_specs=hbm,
        out_shape=jax.ShapeDtypeStruct(dproj.shape, dproj.dtype),
        input_output_aliases={8: 0},
        scratch_shapes=[pltpu.VMEM((RHD, RHD), F32), pltpu.VMEM((CBK, RHD, RHD), BF16)]
        + [pltpu.VMEM((rows, RHD), BF16) for _ in range(4)] + [pltpu.SemaphoreType.DMA((4,))],
        compiler_params=_cp(("arbitrary", "arbitrary")),
    )(_ret_consts(), proj, proj, proj, proj, ret, states, dmixed, dproj)


class _NoReduction:
    def start(self, group, grads):
        pass

    def local(self, name, first=()):
        return []

    def landed(self, name):
        return []

    def update(self, name):
        return []

    place = None

    def rider(self, name):
        return None

    def set_update(self, name, outs):
        pass


def _local_step(x, tgt, nw1, nw2, nw3, win, wout, wgu_a, wgu_b, wd_a, wd_b, red=None):
    red = red or _NoReduction()

    def after(values, first):
        return lax.optimization_barrier((tuple(values), tuple(first)))[0]

    h1, r1 = _rms_fwd(x, nw1)
    proj = _proj(h1, win)
    o, ma, lse, qkvp, lsep = _attn_fwd(proj)
    ret, mr, states = _ret_fwd(proj)
    x2, h2, r2 = _out_proj_rms(x, ma, mr, wout, nw2)
    a, dadg, dadu = _ffn_up(h2, wgu_b, 1, _ffn_up(h2, wgu_a, 0))
    dx3, dx3b, st3 = _ffn_down_loss(_ffn_down_first(x2, a, wd_a), a, wd_b, nw3, tgt)

    dwd = _wgrad_rows(a, dx3b, "wgrad_down")
    red.start(["w_down"], [dwd])
    (dx3b,) = after([dx3b], [dwd])
    part = _ffn_down_bwd(dx3b, wd_a, dadg, dadu, 0)
    (dx3b,) = after([dx3b], red.local("w_down", first=[part]))
    dgu = _ffn_down_bwd(dx3b, wd_b, dadg, dadu, 1, [part])
    dwg = _wgrad_rows(dgu, h2, "wgrad_gate", 0)
    red.start(["w_gate"], [dwg])
    (dgu,) = after([dgu], [dwg])
    dwu = _wgrad_rows(dgu, h2, "wgrad_up", 1)
    red.start(["w_up"], [dwu])
    (dgu,) = after([dgu], red.local("w_gate", first=[dwu] + red.landed("w_down")))
    dx2, dx2b, st2 = _ffn_up_bwd(dgu, wgu_a, wgu_b, dx3, x2, r2, nw2)
    (dx2b,) = after([dx2b], red.local("w_up", first=[dx2b]))
    dwo = _wgrad_out(ma, mr, dx2b)
    red.start(["w_out"], [dwo])
    (dx2b,) = after([dx2b], [dwo])
    dmixed, done = _out_proj_bwd(dx2b, wout, red.place, red.rider("w_down"))
    red.set_update("w_down", done)
    dproj = _attn_bwd(proj, dmixed, o, lse, qkvp, lsep)
    (dmixed,) = after([dmixed], red.local("w_out", first=[dproj] + red.landed("w_gate")))
    dproj = _ret_bwd(proj, ret, states, dmixed, dproj)
    (dwi0,) = after([_wgrad_in(h1, dproj, 0)], red.landed("w_up"))
    red.start(["w_in_0"], [dwi0])
    (dproj,) = after([dproj], [dwi0])
    dwi1 = _wgrad_in(h1, dproj, 1)
    red.start(["w_in_1"], [dwi1])
    sums = red.local("w_in_0", first=[dwi1] + red.landed("w_out"))
    sums = red.local("w_in_1", first=sums + red.update("w_gate"))
    (dproj,) = after([dproj], sums)
    gx, st1 = _in_proj_bwd(dproj, win, dx2, x, r1, nw1)
    dwi = jnp.concatenate([dwi0, dwi1], axis=1)
    stats = jnp.concatenate([st1[0:1], st2[0:1], st3[0:2], jnp.zeros((4, D), F32)], axis=0)
    return stats, gx, dwi, dwo, dwg, dwu, dwd


def _place():
    x, y, c = lax.axis_index("x"), lax.axis_index("y"), lax.axis_index("c")
    return x, y, c, [(1 - x, y), (x, 1 - y), (1 - x, 1 - y)]


def _handshake(peers):
    barrier = pltpu.get_barrier_semaphore()
    for peer in peers:
        pl.semaphore_signal(barrier, inc=1, device_id=peer, device_id_type=MESH)
    pl.semaphore_wait(barrier, len(peers))


def _all_gather(shards, name, collective_id, per=0, rows=None):
    na = len(shards)
    nout = 1 if per else na
    lo, r = rows or (0, shards[0].shape[0])
    ngroups = NDEV // per if per else 0
    SIB, XN0, XN1, YN1, YN0, VIA_X, VIA_Y = 0, 1, 2, 3, 4, 5, 6
    D2D = {XN0: 7, XN1: 8, YN1: 9, YN0: 10, VIA_X: 11, VIA_Y: 12}

    def body(*refs):
        ins, outs = [ref.at[pl.ds(lo, r)] for ref in refs[:na]], refs[na:na + nout]
        send_sems, recv_sems, local_sems = refs[na + nout:]
        x, y, c, _ = _place()
        me, sib = (x, y, c), (x, y, 1 - c)
        xn, yn, dg = (1 - x, y, c), (x, 1 - y, c), (1 - x, 1 - y, c)
        _handshake([sib, xn, yn])

        def part(ref, h):
            rows = ref.shape[0] // 2
            return ref if h is None else ref.at[pl.ds(h * rows, rows)]

        def block(a, owner, h):
            idx = 4 * owner[0] + 2 * owner[1] + owner[2]
            if not per:
                return part(outs[a].at[idx], h)
            return part(outs[0].at[idx // per, a, pl.ds(pl.multiple_of((idx % per) * r, r), r)], h)

        def copy(a, k, owner, h, to, own_src=False):
            return pltpu.make_async_remote_copy(
                src_ref=part(ins[a], h) if own_src else block(a, owner, h), dst_ref=block(a, owner, h),
                send_sem=send_sems.at[a, k], recv_sem=recv_sems.at[a, k], device_id=to, device_id_type=MESH)

        def other(p):
            return (p[0], p[1], 1 - c)

        mine = [pltpu.make_async_copy(ins[a], block(a, me, None), local_sems.at[a]) for a in range(na)]
        for cp in mine:
            cp.start()
        sent = []
        for a in range(na):
            sent += [copy(a, XN0, me, 0, xn, True), copy(a, YN1, me, 1, yn, True),
                     copy(a, XN1, me, 1, xn, True), copy(a, YN0, me, 0, yn, True)]
        sent += [copy(a, SIB, me, None, sib, True) for a in range(na)]
        for cp in sent:
            cp.start()

        def landed(a, k, owner, h, then):
            copy(a, k, owner, h, me).wait_recv()
            for k2, to in then + [(D2D[k], sib)]:
                cp = copy(a, k2, owner, h, to)
                cp.start()
                sent.append(cp)

        for a in range(na):
            landed(a, XN0, xn, 0, [(VIA_Y, yn)])
            landed(a, YN1, yn, 1, [(VIA_X, xn)])
            landed(a, XN1, xn, 1, [])
            landed(a, YN0, yn, 0, [])
        for a in range(na):
            landed(a, VIA_Y, dg, 0, [])
            landed(a, VIA_X, dg, 1, [])
        for a in range(na):
            copy(a, SIB, sib, None, me).wait_recv()
            for k, owner, h in ((XN0, xn, 0), (XN1, xn, 1), (YN1, yn, 1), (YN0, yn, 0), (VIA_Y, dg, 0), (VIA_X, dg, 1)):
                copy(a, D2D[k], other(owner), h, me).wait_recv()
        for cp in sent:
            cp.wait_send()
        for cp in mine:
            cp.wait()

    if per:
        out_type = [jax.ShapeDtypeStruct((ngroups, na, per * r, shards[0].shape[1]), shards[0].dtype)]
    else:
        out_type = [jax.ShapeDtypeStruct((NDEV,) + s.shape, s.dtype) for s in shards]
    return _sequencer_call(
        body, name, collective_id, out_type,
        [pltpu.SemaphoreType.DMA((na, 13)), pltpu.SemaphoreType.DMA((na, 13)), pltpu.SemaphoreType.DMA((na,))])(*shards)


def _sequencer_call(body, name, collective_id, out_type, scratch_types):
    return pl.kernel(
        body, name=name, out_type=out_type,
        mesh=plsc.ScalarSubcoreMesh(axis_name="sequencer", num_cores=1),
        scratch_types=scratch_types,
        compiler_params=pltpu.CompilerParams(collective_id=collective_id))


def _exchange_sibling(grads, name, collective_id):
    na = len(grads)

    def body(*refs):
        ins, outs = refs[:na], refs[na:2 * na]
        send_sems, recv_sems = refs[2 * na:]
        x, y, c, _ = _place()
        _handshake([(x, y, 1 - c)])
        cps = []
        for a in range(na):
            for k in range(4):
                cps.append(pltpu.make_async_remote_copy(
                    src_ref=ins[a].at[2 * k + (1 - c)], dst_ref=outs[a].at[k],
                    send_sem=send_sems.at[a, k], recv_sem=recv_sems.at[a, k],
                    device_id=(x, y, 1 - c), device_id_type=MESH))
        for cp in cps:
            cp.start()
        for cp in cps:
            cp.wait()

    return _sequencer_call(
        body, name, collective_id,
        [jax.ShapeDtypeStruct((4,) + g.shape[1:], g.dtype) for g in grads],
        [pltpu.SemaphoreType.DMA((na, 4)), pltpu.SemaphoreType.DMA((na, 4))])(*grads)


def _row_tile(rows, cols):
    for t in (512, 256, 176, 128, 64, 32, 16):
        if rows % t == 0 and t * cols * 4 <= (2 << 20):
            return t
    raise ValueError((rows, cols))


STREAM_BUFS = 3


def _stream_tile(rows, steps):
    for t in (512, 256, 176, 128, 64, 32, 16):
        if rows % t == 0 and rows // t >= steps:
            return t
    raise ValueError((rows, steps))


def _stream(n, loads, stores, compute):
    for k in range(min(STREAM_BUFS, n)):
        for cp in loads(k):
            cp.start()
    for k in range(n):
        for cp in loads(k):
            cp.wait()
        if k >= 2:
            for cp in stores(k - 2):
                cp.wait()
        compute(k)
        for cp in stores(k):
            cp.start()
        if k + STREAM_BUFS < n:
            for cp in loads(k + STREAM_BUFS):
                cp.start()
    for k in range(max(n - 2, 0), n):
        for cp in stores(k):
            cp.wait()


def _chip_sum(place, g, got, name):
    _, r, c = g.shape
    tm = _stream_tile(r, 4)
    nt = r // tm

    def body(pos_ref, g_hbm, got_hbm, o_hbm, g_buf, s_buf, o_buf, sem_in, sem_out):
        def chip(j):
            return 2 * (pos_ref[0] ^ (0 if j == 1 else 1)) + (pos_ref[1] ^ (0 if j == 0 else 1))

        def loads(k):
            j, rows, slot = k // nt, pl.ds((k % nt) * tm, tm), k % STREAM_BUFS
            return [pltpu.make_async_copy(g_hbm.at[2 * chip(j) + pos_ref[2], rows], g_buf.at[slot], sem_in.at[slot, 0]),
                    pltpu.make_async_copy(got_hbm.at[chip(j), rows], s_buf.at[slot], sem_in.at[slot, 1])]

        def stores(k):
            return [pltpu.make_async_copy(o_buf.at[k % 2], o_hbm.at[k // nt, pl.ds((k % nt) * tm, tm)],
                                          sem_out.at[k % 2])]

        def compute(k):
            slot = k % STREAM_BUFS
            o_buf[k % 2] = (g_buf[slot].astype(F32) + s_buf[slot].astype(F32)).astype(BF16)

        _stream(3 * nt, loads, stores, compute)

    hbm = pl.BlockSpec(memory_space=pl.ANY)
    return pl.pallas_call(
        body, name=name,
        grid_spec=pltpu.PrefetchScalarGridSpec(
            num_scalar_prefetch=1, grid=(1,), in_specs=[hbm, hbm], out_specs=hbm,
            scratch_shapes=[pltpu.VMEM((STREAM_BUFS, tm, c), BF16), pltpu.VMEM((STREAM_BUFS, tm, c), BF16),
                            pltpu.VMEM((2, tm, c), BF16),
                            pltpu.SemaphoreType.DMA((STREAM_BUFS, 2)), pltpu.SemaphoreType.DMA((2,))]),
        out_shape=jax.ShapeDtypeStruct((3, r, c), BF16),
        compiler_params=_cp(("arbitrary",)),
    )(place, g, got)


def _exchange_chips(sums, name, collective_id):
    na = len(sums)

    def body(*refs):
        ins, outs = refs[:na], refs[na:2 * na]
        send_sems, recv_sems = refs[2 * na:]
        x, y, c, chips = _place()
        _handshake([(*chip, c) for chip in chips])
        cps = []
        for a in range(na):
            for j, chip in enumerate(chips):
                cps.append(pltpu.make_async_remote_copy(
                    src_ref=ins[a].at[j], dst_ref=outs[a].at[j],
                    send_sem=send_sems.at[a, j], recv_sem=recv_sems.at[a, j],
                    device_id=(*chip, c), device_id_type=MESH))
        for cp in cps:
            cp.start()
        for cp in cps:
            cp.wait()

    return _sequencer_call(
        body, name, collective_id,
        [jax.ShapeDtypeStruct((3,) + s.shape[1:], s.dtype) for s in sums],
        [pltpu.SemaphoreType.DMA((na, 3)), pltpu.SemaphoreType.DMA((na, 3))])(*sums)


def _exchange_stats(stats, collective_id):
    def body(st_in, st_out, st_send, st_recv, local_sem):
        x, y, c, _ = _place()
        me_idx = 4 * x + 2 * y + c
        peers = [(x ^ ((k >> 2) & 1), y ^ ((k >> 1) & 1), c ^ (k & 1)) for k in range(1, 8)]
        _handshake(peers)
        mine = pltpu.make_async_copy(st_in, st_out.at[me_idx], local_sem)
        mine.start()
        cps = [pltpu.make_async_remote_copy(
            src_ref=st_in, dst_ref=st_out.at[me_idx], send_sem=st_send.at[k], recv_sem=st_recv.at[k],
            device_id=peer, device_id_type=MESH) for k, peer in enumerate(peers)]
        for cp in cps:
            cp.start()
        for cp in cps:
            cp.wait()
        mine.wait()

    return _sequencer_call(
        body, "exchange_stats", collective_id,
        jax.ShapeDtypeStruct((NDEV,) + stats.shape, stats.dtype),
        [pltpu.SemaphoreType.DMA((7,)), pltpu.SemaphoreType.DMA((7,)), pltpu.SemaphoreType.DMA])(stats)


class _Reduction:
    def __init__(self, place, first_collective_id, state):
        self.place = place
        self.ids = iter(range(first_collective_id, 32))
        self.state = state
        self.groups = {}
        self.updates = {}

    def next_id(self):
        return next(self.ids)

    def start(self, group, grads):
        got = _exchange_sibling(grads, "sibling_exchange_" + group[0], self.next_id())
        self.groups[group[0]] = dict(names=group, grads=grads, got=got)

    def local(self, name, first=()):
        grp = self.groups[name]
        grads = lax.optimization_barrier((tuple(grp["grads"]), tuple(first)))[0]
        grp["sums"] = [_chip_sum(self.place, g, s, "chip_sum_" + n)
                       for g, s, n in zip(grads, grp["got"], grp["names"])]
        grp["chips"] = _exchange_chips(grp["sums"], "chip_exchange_" + name, self.next_id())
        return grp["sums"]

    def landed(self, name):
        return list(self.groups[name]["chips"])

    def rider(self, name):
        grp = next(g for g in self.groups.values() if name in g["names"])
        k = grp["names"].index(name)
        return self.state[name][:3] + (grp["grads"][k], grp["got"][k], grp["chips"][k])

    def set_update(self, name, outs):
        self.updates[name] = list(outs)

    def update(self, name):
        if name not in self.updates:
            grp = next(g for g in self.groups.values() if name in g["names"])
            k = grp["names"].index(name)
            w, m, v, part, parts = self.state[name]
            before = self.update(f"{name[:-1]}{part - 1}") if part else None
            self.updates[name] = _shard_update(self.place, w, m, v, grp["grads"][k], grp["got"][k],
                                               grp["chips"][k], "update_" + name, part, parts, before)
        return list(self.updates[name])


def _adamw(w, g, m, v):
    m = ADAM_B1 * m + (1.0 - ADAM_B1) * g
    v = ADAM_B2 * v + (1.0 - ADAM_B2) * (g * g)
    m_hat = m / (1.0 - ADAM_B1 ** ADAM_STEP)
    v_hat = v / (1.0 - ADAM_B2 ** ADAM_STEP)
    delta = -ADAM_LR * (m_hat / (jnp.sqrt(v_hat) + ADAM_EPS) + ADAM_WD * w)
    return delta, m, v


def _update_tile(w_ref, m_ref, v_ref, g_ref, s_ref, c_ref, go_ref, d_ref, mo_ref, vo_ref):
    grad = g_ref[...].astype(F32) + s_ref[...].astype(F32)
    for j in range(3):
        grad = grad + c_ref[j].astype(F32)
    delta, mn, vn = _adamw(w_ref[...], grad, m_ref[...], v_ref[...])
    go_ref[...] = grad
    d_ref[...] = delta
    mo_ref[...] = mn
    vo_ref[...] = vn


def _shard_update(place, w, m, v, g, got_sib, got_chips, name, part=0, parts=1, before=None):
    r, c = w.shape
    rp = r // parts
    tm = _stream_tile(rp, 8)
    nt = rp // tm
    before = list(before or [])

    def body(pos_ref, w_hbm, m_hbm, v_hbm, g_hbm, s_hbm, c_hbm, *rest):
        outs = rest[len(before):len(before) + 4]
        w_buf, m_buf, v_buf, g_buf, s_buf, c_buf, o_buf, sem_in, sem_out = rest[len(before) + 4:]
        own = 4 * pos_ref[0] + 2 * pos_ref[1] + pos_ref[2]
        chip = 2 * pos_ref[0] + pos_ref[1]

        def loads(k):
            slot, rows, mine = k % STREAM_BUFS, pl.ds(k * tm, tm), pl.ds(part * rp + k * tm, tm)
            pairs = [(w_hbm.at[mine], w_buf), (m_hbm.at[mine], m_buf), (v_hbm.at[mine], v_buf),
                     (g_hbm.at[own, rows], g_buf), (s_hbm.at[chip, rows], s_buf), (c_hbm.at[:, rows], c_buf)]
            return [pltpu.make_async_copy(src, buf.at[slot], sem_in.at[slot, n]) for n, (src, buf) in enumerate(pairs)]

        def stores(k):
            mine = pl.ds(part * rp + k * tm, tm)
            return [pltpu.make_async_copy(o_buf.at[k % 2, n], out.at[mine], sem_out.at[k % 2, n])
                    for n, out in enumerate(outs)]

        def compute(k):
            slot = k % STREAM_BUFS
            _update_tile(w_buf.at[slot], m_buf.at[slot], v_buf.at[slot], g_buf.at[slot], s_buf.at[slot],
                         c_buf.at[slot], *[o_buf.at[k % 2, n] for n in range(4)])

        _stream(nt, loads, stores, compute)

    hbm = pl.BlockSpec(memory_space=pl.ANY)
    return pl.pallas_call(
        body, name=name,
        grid_spec=pltpu.PrefetchScalarGridSpec(
            num_scalar_prefetch=1, grid=(1,), in_specs=[hbm] * (6 + len(before)), out_specs=[hbm] * 4,
            scratch_shapes=[pltpu.VMEM((STREAM_BUFS, tm, c), F32)] * 3 + [pltpu.VMEM((STREAM_BUFS, tm, c), BF16)] * 2
            + [pltpu.VMEM((STREAM_BUFS, 3, tm, c), BF16), pltpu.VMEM((2, 4, tm, c), F32),
               pltpu.SemaphoreType.DMA((STREAM_BUFS, 6)), pltpu.SemaphoreType.DMA((2, 4))]),
        out_shape=[jax.ShapeDtypeStruct((r, c), F32)] * 4,
        input_output_aliases={7 + k: k for k in range(len(before))},
        compiler_params=_cp(("arbitrary",)),
    )(place, w, m, v, g, got_sib, got_chips, *before)


def _small_update(stats_all, ws, ms, vs):
    def body(st_ref, w_ref, m_ref, v_ref, go_ref, d_ref, mo_ref, vo_ref):
        grad = st_ref[0]
        for k in range(1, NDEV):
            grad = grad + st_ref[k]
        delta, mn, vn = _adamw(w_ref[...], grad, m_ref[...], v_ref[...])
        go_ref[...] = grad
        d_ref[...] = delta
        mo_ref[...] = mn
        vo_ref[...] = vn

    return pl.pallas_call(
        body, name="small_update",
        out_shape=[jax.ShapeDtypeStruct((8, D), F32)] * 4,
        compiler_params=_cp(),
    )(stats_all, ws, ms, vs)


def kernel(x, norm_mix_w, w_in, w_out, norm_ffn_w, w_gate, w_up, w_down, norm_final_w, loss_target, m_norm_mix_w, m_w_in, m_w_out, m_norm_ffn_w, m_w_gate, m_w_up, m_w_down, m_norm_final_w, v_norm_mix_w, v_w_in, v_w_out, v_norm_ffn_w, v_w_gate, v_w_up, v_w_down, v_norm_final_w):
    tr = {"w_gate", "w_up"}
    names = ["w_in", "w_out", "w_gate", "w_up", "w_down"]

    def view(a, n):
        return a[0].T if n in tr else a[0]

    big_w = [view(a, n) for a, n in zip([w_in, w_out, w_gate, w_up, w_down], names)]
    big_m = [view(a, n) for a, n in zip([m_w_in, m_w_out, m_w_gate, m_w_up, m_w_down], names)]
    big_v = [view(a, n) for a, n in zip([v_w_in, v_w_out, v_w_gate, v_w_up, v_w_down], names)]

    shards = [None] + [_cast_bf16(w, "cast_" + n) for w, n in zip(big_w[1:], names[1:])]
    win = [_all_gather([cols], f"all_gather_w_in_{k}", 1 + k)[0]
           for k, cols in enumerate(_cast_cols(big_w[0], "cast_w_in"))]
    (wout,) = _all_gather(shards[1:2], "all_gather_w_out", 3)
    (wgu_a,) = _all_gather(shards[2:4], "all_gather_gate_up_0", 4, per=FF_PER, rows=(0, FF_ROWS))
    (wgu_b,) = _all_gather(shards[2:4], "all_gather_gate_up_1", 5, per=FF_PER, rows=(FF_ROWS, FF_ROWS))
    (wd_a,) = _all_gather(shards[4:5], "all_gather_w_down_0", 6, per=FF_PER, rows=(0, FF_ROWS))
    (wd_b,) = _all_gather(shards[4:5], "all_gather_w_down_1", 7, per=FF_PER, rows=(FF_ROWS, FF_ROWS))
    nw3 = norm_final_w.reshape(1, D)
    place = jnp.stack([lax.axis_index("x"), lax.axis_index("y"), lax.axis_index("c")]).astype(jnp.int32)
    state = {n: (w, m, v, 0, 1) for n, w, m, v in zip(names, big_w, big_m, big_v)}
    for part in range(W_IN_PARTS):
        state[f"w_in_{part}"] = state["w_in"][:3] + (part, W_IN_PARTS)
    red = _Reduction(place, 8, state)
    stats, gx, *_ = _local_step(
        x[0], loss_target[0], norm_mix_w, norm_ffn_w, nw3, win, wout.reshape(D, D),
        wgu_a.reshape(NFG // 2, 2 * N_FG, D), wgu_b.reshape(NFG // 2, 2 * N_FG, D),
        wd_a.reshape(NFG // 2, N_FG, D), wd_b.reshape(NFG // 2, N_FG, D), red)
    stats_all = _exchange_stats(stats, red.next_id())
    upd = [red.update(f"w_in_{W_IN_PARTS - 1}" if n == "w_in" else n) for n in names]
    stats_all = lax.optimization_barrier((stats_all, tuple(upd[0])))[0]

    def rows(a, b, c):
        return jnp.concatenate([a.reshape(1, D), b.reshape(1, D), c.reshape(1, D), jnp.zeros((5, D), F32)], axis=0)

    sg, sd, sm, sv = _small_update(stats_all, rows(norm_mix_w, norm_ffn_w, norm_final_w),
                                   rows(m_norm_mix_w, m_norm_ffn_w, m_norm_final_w),
                                   rows(v_norm_mix_w, v_norm_ffn_w, v_norm_final_w))
    loss = sg[3, 0]

    def outs(k, small):
        big = [(u[k].T if n in tr else u[k])[None] for u, n in zip(upd, names)]
        return [small[0:1], big[0], big[1], small[1:2], big[2], big[3], big[4], small[2]]

    return (loss, gx[None], *outs(0, sg), *outs(1, sd), *outs(2, sm), *outs(3, sv))
```
